```python
import math
import jax, jax.numpy as jnp
from jax import lax
import numpy as np

D_MODEL = 1024
BATCH = 8
SEQ = 8192
DEPTH = 2

N_MIXERS = 2
N_META = 16
GDN_HEADS = 8
GDN_HEAD_DIM = 128
GDN_WIDTH = GDN_HEADS * GDN_HEAD_DIM
GDN_CONV = 4
CHUNK = 64
META_PAD = (-N_META) % CHUNK
SC_WIDTH = D_MODEL
SC_CONV = 3
D_FF = 2816
FFN_CONV = 3
N_LAYERS_A = (DEPTH + 1) // 2
N_LAYERS_B = DEPTH // 2
ALPHA = (2.0 * DEPTH) ** 0.25
BETA_INIT = (8.0 * DEPTH) ** -0.25
LN_EPS = 1e-5
RMS_EPS = 1e-6
L2_EPS = 1e-6

kernel_name = "hybrid_gdn_shortconv_convffn_deepnorm"


def causal_dwconv(x, w):
    width, ch = w.shape
    return lax.conv_general_dilated(
        x, w[:, None, :].astype(x.dtype), window_strides=(1,), padding=[(width - 1, 0)],
        dimension_numbers=("NWC", "WIO", "NWC"), feature_group_count=ch)


def layer_norm(x, g, b):
    xf = x.astype(jnp.float32)
    mu = jnp.mean(xf, axis=-1, keepdims=True)
    var = jnp.mean(jnp.square(xf - mu), axis=-1, keepdims=True)
    y = (xf - mu) * lax.rsqrt(var + LN_EPS) * g.astype(jnp.float32) + b.astype(jnp.float32)
    return y.astype(x.dtype)


def l2norm(x):
    return x * lax.rsqrt(jnp.sum(x * x, axis=-1, keepdims=True) + L2_EPS)


def chunk_gated_delta_rule(q, k, v, g, beta):
    bsz, t_len, h, dk = q.shape
    n = t_len // CHUNK

    def to_chunks(t):
        return jnp.transpose(t.reshape(bsz, n, CHUNK, h, -1), (0, 3, 1, 2, 4))

    q, k, v = to_chunks(q), to_chunks(k), to_chunks(v)
    g = jnp.transpose(g.reshape(bsz, n, CHUNK, h), (0, 3, 1, 2))
    beta = jnp.transpose(beta.reshape(bsz, n, CHUNK, h), (0, 3, 1, 2))
    g = jnp.cumsum(g, axis=-1)

    causal = jnp.tril(jnp.ones((CHUNK, CHUNK), dtype=bool))
    strict = jnp.tril(jnp.ones((CHUNK, CHUNK), dtype=bool), -1)
    decay = jnp.exp(jnp.where(causal, g[..., :, None] - g[..., None, :], -jnp.inf))

    k_beta = k * beta[..., None]
    v_beta = v * beta[..., None]
    m = jnp.where(strict, jnp.einsum("bhnid,bhnjd->bhnij", k_beta, k) * decay, 0.0)
    a_mat = jnp.eye(CHUNK, dtype=q.dtype) + m
    u = lax.linalg.triangular_solve(a_mat, v_beta, left_side=True, lower=True)
    w = lax.linalg.triangular_solve(a_mat, k_beta * jnp.exp(g)[..., None], left_side=True, lower=True)

    qk = jnp.einsum("bhnid,bhnjd->bhnij", q, k) * decay
    q_g = q * jnp.exp(g)[..., None]
    g_last = g[..., -1]
    k_dec = k * jnp.exp(g_last[..., None] - g)[..., None]

    def step(state, inp):
        qg_i, kd_i, u_i, w_i, qk_i, gl_i = inp
        v_new = u_i - jnp.einsum("bhcd,bhde->bhce", w_i, state)
        o_i = jnp.einsum("bhcd,bhde->bhce", qg_i, state) + jnp.einsum("bhij,bhje->bhie", qk_i, v_new)
        state = state * jnp.exp(gl_i)[..., None, None] + jnp.einsum("bhcd,bhce->bhde", kd_i, v_new)
        return state, o_i

    xs = tuple(jnp.moveaxis(t, 2, 0) for t in (q_g, k_dec, u, w, qk, g_last))
    state0 = jnp.zeros((bsz, h, dk, v.shape[-1]), dtype=q.dtype)
    _, o = lax.scan(step, state0, xs)
    return jnp.transpose(o, (1, 0, 3, 2, 4)).reshape(bsz, t_len, h, -1)


def gated_deltanet(h, w_in, conv_w, a_log, dt_bias, norm_w, w_out):
    bsz, seq_len, _ = h.shape
    proj = h @ w_in
    qkv, z, b_raw, a_raw = jnp.split(
        proj, [3 * GDN_WIDTH, 4 * GDN_WIDTH, 4 * GDN_WIDTH + GDN_HEADS], axis=-1)
    qkv = jax.nn.silu(causal_dwconv(qkv, conv_w))
    q, k, v = [t.reshape(bsz, seq_len, GDN_HEADS, GDN_HEAD_DIM).astype(jnp.float32)
               for t in jnp.split(qkv, 3, axis=-1)]
    q = l2norm(q) * (GDN_HEAD_DIM ** -0.5)
    k = l2norm(k)
    beta = jax.nn.sigmoid(b_raw.astype(jnp.float32))
    g = -jnp.exp(a_log.astype(jnp.float32)) * jax.nn.softplus(
        a_raw.astype(jnp.float32) + dt_bias.astype(jnp.float32))
    pad4 = ((0, 0), (META_PAD, 0), (0, 0), (0, 0))
    pad3 = ((0, 0), (META_PAD, 0), (0, 0))
    o = chunk_gated_delta_rule(jnp.pad(q, pad4), jnp.pad(k, pad4), jnp.pad(v, pad4),
                               jnp.pad(g, pad3), jnp.pad(beta, pad3))[:, META_PAD:]
    o = o * lax.rsqrt(jnp.mean(o * o, axis=-1, keepdims=True) + RMS_EPS) * norm_w.astype(jnp.float32)
    o = o * jax.nn.silu(z.reshape(bsz, seq_len, GDN_HEADS, GDN_HEAD_DIM).astype(jnp.float32))
    return o.reshape(bsz, seq_len, GDN_WIDTH).astype(h.dtype) @ w_out


def short_conv_mixer(h, w_in, conv_w, w_out):
    b_gate, c_gate, xv = jnp.split(h @ w_in, 3, axis=-1)
    u = causal_dwconv(c_gate * xv, conv_w)
    return (b_gate * u) @ w_out


def conv_ffn(h, w_up, conv_w, w_down):
    u, gate = jnp.split(h @ w_up, 2, axis=-1)
    u = causal_dwconv(u, conv_w)
    return (jax.nn.silu(u) * gate) @ w_down


def _fwd_setup_inputs(seed: int = 0) -> dict:
    key = jax.random.key(seed)
    ks = iter(jax.random.split(key, 32))
    f32 = jnp.float32

    def nrm(shape, scale):
        return jax.random.normal(next(ks), shape, f32) * scale

    a_in_cols = 4 * GDN_WIDTH + 2 * GDN_HEADS
    dt = jnp.exp(jax.random.uniform(next(ks), (N_LAYERS_A, GDN_HEADS), f32,
                                    math.log(1e-3), math.log(1e-1)))
    return {
        "x": nrm((BATCH, SEQ, D_MODEL), 1.0),
        "meta": nrm((N_META, D_MODEL), 1.0),
        "a_w_in": nrm((N_LAYERS_A, D_MODEL, a_in_cols), D_MODEL ** -0.5),
        "a_conv": nrm((N_LAYERS_A, GDN_CONV, 3 * GDN_WIDTH), GDN_CONV ** -0.5),
        "a_log": jnp.log(jax.random.uniform(next(ks), (N_LAYERS_A, GDN_HEADS), f32, 1.0, 16.0)),
        "a_dt_bias": dt + jnp.log(-jnp.expm1(-dt)),
        "a_norm": 1.0 + nrm((N_LAYERS_A, GDN_HEAD_DIM), 0.02),
        "a_w_out": nrm((N_LAYERS_A, GDN_WIDTH, D_MODEL), BETA_INIT * GDN_WIDTH ** -0.5),
        "b_w_in": nrm((N_LAYERS_B, D_MODEL, 3 * SC_WIDTH), D_MODEL ** -0.5),
        "b_conv": nrm((N_LAYERS_B, SC_CONV, SC_WIDTH), SC_CONV ** -0.5),
        "b_w_out": nrm((N_LAYERS_B, SC_WIDTH, D_MODEL), BETA_INIT * SC_WIDTH ** -0.5),
        "ln_mix_g": 1.0 + nrm((DEPTH, D_MODEL), 0.02),
        "ln_mix_b": nrm((DEPTH, D_MODEL), 0.02),
        "ffn_w_up": nrm((DEPTH, D_MODEL, 2 * D_FF), D_MODEL ** -0.5),
        "ffn_conv": nrm((DEPTH, FFN_CONV, D_FF), FFN_CONV ** -0.5),
        "ffn_w_down": nrm((DEPTH, D_FF, D_MODEL), BETA_INIT * D_FF ** -0.5),
        "ln_ffn_g": 1.0 + nrm((DEPTH, D_MODEL), 0.02),
        "ln_ffn_b": nrm((DEPTH, D_MODEL), 0.02),
    }


def _fwd_reference(x, meta, a_w_in, a_conv, a_log, a_dt_bias, a_norm, a_w_out,
              b_w_in, b_conv, b_w_out, ln_mix_g, ln_mix_b,
              ffn_w_up, ffn_conv, ffn_w_down, ln_ffn_g, ln_ffn_b):
    bsz = x.shape[0]
    h = jnp.concatenate(
        [jnp.broadcast_to(meta.astype(x.dtype)[None], (bsz, N_META, D_MODEL)), x], axis=1)
    for i in range(DEPTH):
        j = i // N_MIXERS
        if i % N_MIXERS == 0:
            mix = gated_deltanet(h, a_w_in[j], a_conv[j], a_log[j], a_dt_bias[j], a_norm[j], a_w_out[j])
        else:
            mix = short_conv_mixer(h, b_w_in[j], b_conv[j], b_w_out[j])
        h = layer_norm(ALPHA * h + mix, ln_mix_g[i], ln_mix_b[i])
        h = layer_norm(ALPHA * h + conv_ffn(h, ffn_w_up[i], ffn_conv[i], ffn_w_down[i]),
                       ln_ffn_g[i], ln_ffn_b[i])
    return h[:, N_META:]


import jax as _jax
import jax.numpy as _jnp

TWIN_FORMAT = 'train_step'
FWD_PARAMS = ['x', 'meta', 'a_w_in', 'a_conv', 'a_log', 'a_dt_bias', 'a_norm', 'a_w_out', 'b_w_in', 'b_conv', 'b_w_out', 'ln_mix_g', 'ln_mix_b', 'ffn_w_up', 'ffn_conv', 'ffn_w_down', 'ln_ffn_g', 'ln_ffn_b']
TWIN_WEIGHTS = ['meta', 'a_w_in', 'a_conv', 'a_log', 'a_dt_bias', 'a_norm', 'a_w_out', 'b_w_in', 'b_conv', 'b_w_out', 'ln_mix_g', 'ln_mix_b', 'ffn_w_up', 'ffn_conv', 'ffn_w_down', 'ln_ffn_g', 'ln_ffn_b']
TWIN_DIFF_INPUT = 'x'
TWIN_INPUTS = ['x', 'meta', 'a_w_in', 'a_conv', 'a_log', 'a_dt_bias', 'a_norm', 'a_w_out', 'b_w_in', 'b_conv', 'b_w_out', 'ln_mix_g', 'ln_mix_b', 'ffn_w_up', 'ffn_conv', 'ffn_w_down', 'ln_ffn_g', 'ln_ffn_b', 'loss_target', 'm_meta', 'm_a_w_in', 'm_a_conv', 'm_a_log', 'm_a_dt_bias', 'm_a_norm', 'm_a_w_out', 'm_b_w_in', 'm_b_conv', 'm_b_w_out', 'm_ln_mix_g', 'm_ln_mix_b', 'm_ffn_w_up', 'm_ffn_conv', 'm_ffn_w_down', 'm_ln_ffn_g', 'm_ln_ffn_b', 'v_meta', 'v_a_w_in', 'v_a_conv', 'v_a_log', 'v_a_dt_bias', 'v_a_norm', 'v_a_w_out', 'v_b_w_in', 'v_b_conv', 'v_b_w_out', 'v_ln_mix_g', 'v_ln_mix_b', 'v_ffn_w_up', 'v_ffn_conv', 'v_ffn_w_down', 'v_ln_ffn_g', 'v_ln_ffn_b']
TWIN_OUTPUTS = ['loss', 'grad_x', 'grad_meta', 'grad_a_w_in', 'grad_a_conv', 'grad_a_log', 'grad_a_dt_bias', 'grad_a_norm', 'grad_a_w_out', 'grad_b_w_in', 'grad_b_conv', 'grad_b_w_out', 'grad_ln_mix_g', 'grad_ln_mix_b', 'grad_ffn_w_up', 'grad_ffn_conv', 'grad_ffn_w_down', 'grad_ln_ffn_g', 'grad_ln_ffn_b', 'delta_meta', 'delta_a_w_in', 'delta_a_conv', 'delta_a_log', 'delta_a_dt_bias', 'delta_a_norm', 'delta_a_w_out', 'delta_b_w_in', 'delta_b_conv', 'delta_b_w_out', 'delta_ln_mix_g', 'delta_ln_mix_b', 'delta_ffn_w_up', 'delta_ffn_conv', 'delta_ffn_w_down', 'delta_ln_ffn_g', 'delta_ln_ffn_b', 'new_m_meta', 'new_m_a_w_in', 'new_m_a_conv', 'new_m_a_log', 'new_m_a_dt_bias', 'new_m_a_norm', 'new_m_a_w_out', 'new_m_b_w_in', 'new_m_b_conv', 'new_m_b_w_out', 'new_m_ln_mix_g', 'new_m_ln_mix_b', 'new_m_ffn_w_up', 'new_m_ffn_conv', 'new_m_ffn_w_down', 'new_m_ln_ffn_g', 'new_m_ln_ffn_b', 'new_v_meta', 'new_v_a_w_in', 'new_v_a_conv', 'new_v_a_log', 'new_v_a_dt_bias', 'new_v_a_norm', 'new_v_a_w_out', 'new_v_b_w_in', 'new_v_b_conv', 'new_v_b_w_out', 'new_v_ln_mix_g', 'new_v_ln_mix_b', 'new_v_ffn_w_up', 'new_v_ffn_conv', 'new_v_ffn_w_down', 'new_v_ln_ffn_g', 'new_v_ln_ffn_b']
TWIN_LEAF_KINDS = {'loss': 'loss', 'grad_x': 'grad_x', 'grad_meta': 'grad_w', 'grad_a_w_in': 'grad_w', 'grad_a_conv': 'grad_w', 'grad_a_log': 'grad_w', 'grad_a_dt_bias': 'grad_w', 'grad_a_norm': 'grad_w', 'grad_a_w_out': 'grad_w', 'grad_b_w_in': 'grad_w', 'grad_b_conv': 'grad_w', 'grad_b_w_out': 'grad_w', 'grad_ln_mix_g': 'grad_w', 'grad_ln_mix_b': 'grad_w', 'grad_ffn_w_up': 'grad_w', 'grad_ffn_conv': 'grad_w', 'grad_ffn_w_down': 'grad_w', 'grad_ln_ffn_g': 'grad_w', 'grad_ln_ffn_b': 'grad_w', 'delta_meta': 'delta_w', 'delta_a_w_in': 'delta_w', 'delta_a_conv': 'delta_w', 'delta_a_log': 'delta_w', 'delta_a_dt_bias': 'delta_w', 'delta_a_norm': 'delta_w', 'delta_a_w_out': 'delta_w', 'delta_b_w_in': 'delta_w', 'delta_b_conv': 'delta_w', 'delta_b_w_out': 'delta_w', 'delta_ln_mix_g': 'delta_w', 'delta_ln_mix_b': 'delta_w', 'delta_ffn_w_up': 'delta_w', 'delta_ffn_conv': 'delta_w', 'delta_ffn_w_down': 'delta_w', 'delta_ln_ffn_g': 'delta_w', 'delta_ln_ffn_b': 'delta_w', 'new_m_meta': 'new_m', 'new_m_a_w_in': 'new_m', 'new_m_a_conv': 'new_m', 'new_m_a_log': 'new_m', 'new_m_a_dt_bias': 'new_m', 'new_m_a_norm': 'new_m', 'new_m_a_w_out': 'new_m', 'new_m_b_w_in': 'new_m', 'new_m_b_conv': 'new_m', 'new_m_b_w_out': 'new_m', 'new_m_ln_mix_g': 'new_m', 'new_m_ln_mix_b': 'new_m', 'new_m_ffn_w_up': 'new_m', 'new_m_ffn_conv': 'new_m', 'new_m_ffn_w_down': 'new_m', 'new_m_ln_ffn_g': 'new_m', 'new_m_ln_ffn_b': 'new_m', 'new_v_meta': 'new_v', 'new_v_a_w_in': 'new_v', 'new_v_a_conv': 'new_v', 'new_v_a_log': 'new_v', 'new_v_a_dt_bias': 'new_v', 'new_v_a_norm': 'new_v', 'new_v_a_w_out': 'new_v', 'new_v_b_w_in': 'new_v', 'new_v_b_conv': 'new_v', 'new_v_b_w_out': 'new_v', 'new_v_ln_mix_g': 'new_v', 'new_v_ln_mix_b': 'new_v', 'new_v_ffn_w_up': 'new_v', 'new_v_ffn_conv': 'new_v', 'new_v_ffn_w_down': 'new_v', 'new_v_ln_ffn_g': 'new_v', 'new_v_ln_ffn_b': 'new_v'}


def _forward(args):
    return _fwd_reference(*[args[k] for k in FWD_PARAMS])


def _output_shape():
    def fwd():
        inp = _fwd_setup_inputs(0)
        return _fwd_reference(*[inp[k] for k in FWD_PARAMS])
    out = _jax.eval_shape(fwd)
    return out.shape, out.dtype

N_MICROBATCH = 1
ADAM_LR = 0.001
ADAM_B1 = 0.9
ADAM_B2 = 0.999
ADAM_EPS = 1e-08
ADAM_WD = 0.01
ADAM_STEP = 10
PER_EXAMPLE_BATCH_AXIS = {'x': 0, 'loss_target': 0}
SHARED_INPUTS = []
_WEIGHT_DTYPES = {'meta': _jnp.float32, 'a_w_in': _jnp.float32, 'a_conv': _jnp.float32, 'a_log': _jnp.float32, 'a_dt_bias': _jnp.float32, 'a_norm': _jnp.float32, 'a_w_out': _jnp.float32, 'b_w_in': _jnp.float32, 'b_conv': _jnp.float32, 'b_w_out': _jnp.float32, 'ln_mix_g': _jnp.float32, 'ln_mix_b': _jnp.float32, 'ffn_w_up': _jnp.float32, 'ffn_conv': _jnp.float32, 'ffn_w_down': _jnp.float32, 'ln_ffn_g': _jnp.float32, 'ln_ffn_b': _jnp.float32}
MOMENT_SCALE = {'meta': 4.085288e-03, 'a_w_in': 4.978877e-02, 'a_conv': 4.536354e-02, 'a_log': 1.606919e-01, 'a_dt_bias': 1.581458e-01, 'a_norm': 1.871097e-01, 'a_w_out': 1.297309e-01, 'b_w_in': 8.627457e-02, 'b_conv': 8.714873e-02, 'b_w_out': 1.717547e-01, 'ln_mix_g': 1.958784e+00, 'ln_mix_b': 9.433148e-01, 'ffn_w_up': 3.468808e-02, 'ffn_conv': 3.652358e-02, 'ffn_w_down': 1.135428e-01, 'ln_ffn_g': 4.536549e+01, 'ln_ffn_b': 2.362494e+00}


def _to_microbatches(a, axis):
    t = _jnp.moveaxis(a, axis, 0)
    t = t.reshape((N_MICROBATCH, t.shape[0] // N_MICROBATCH) + t.shape[1:])
    return _jnp.moveaxis(t, 1, axis + 1)


def setup_inputs(seed: int = 0) -> dict:
    inp = _fwd_setup_inputs(seed)
    key = _jax.random.fold_in(_jax.random.key(seed), 7919)
    shape, _ = _output_shape()
    out = dict(inp)
    out["loss_target"] = _jax.random.normal(_jax.random.fold_in(key, 0), shape, _jnp.float32)
    for i, name in enumerate(TWIN_WEIGHTS):
        w = inp[name].astype(_jnp.float32)
        if MOMENT_SCALE is None:
            s = _jnp.sqrt(_jnp.mean(_jnp.square(w)) + 1e-30)
        else:
            s = MOMENT_SCALE[name]
        km, kv = _jax.random.split(_jax.random.fold_in(key, i + 1))
        out[name] = w
        out["m_" + name] = s * _jax.random.normal(km, w.shape, _jnp.float32)
        out["v_" + name] = (s * s) * _jax.random.uniform(kv, w.shape, _jnp.float32, 0.5, 1.5)
    if N_MICROBATCH > 1:
        for name, axis in PER_EXAMPLE_BATCH_AXIS.items():
            out[name] = _to_microbatches(out[name], axis)
    return {'x': out['x'], 'meta': out['meta'], 'a_w_in': out['a_w_in'], 'a_conv': out['a_conv'], 'a_log': out['a_log'], 'a_dt_bias': out['a_dt_bias'], 'a_norm': out['a_norm'], 'a_w_out': out['a_w_out'], 'b_w_in': out['b_w_in'], 'b_conv': out['b_conv'], 'b_w_out': out['b_w_out'], 'ln_mix_g': out['ln_mix_g'], 'ln_mix_b': out['ln_mix_b'], 'ffn_w_up': out['ffn_w_up'], 'ffn_conv': out['ffn_conv'], 'ffn_w_down': out['ffn_w_down'], 'ln_ffn_g': out['ln_ffn_g'], 'ln_ffn_b': out['ln_ffn_b'], 'loss_target': out['loss_target'], 'm_meta': out['m_meta'], 'm_a_w_in': out['m_a_w_in'], 'm_a_conv': out['m_a_conv'], 'm_a_log': out['m_a_log'], 'm_a_dt_bias': out['m_a_dt_bias'], 'm_a_norm': out['m_a_norm'], 'm_a_w_out': out['m_a_w_out'], 'm_b_w_in': out['m_b_w_in'], 'm_b_conv': out['m_b_conv'], 'm_b_w_out': out['m_b_w_out'], 'm_ln_mix_g': out['m_ln_mix_g'], 'm_ln_mix_b': out['m_ln_mix_b'], 'm_ffn_w_up': out['m_ffn_w_up'], 'm_ffn_conv': out['m_ffn_conv'], 'm_ffn_w_down': out['m_ffn_w_down'], 'm_ln_ffn_g': out['m_ln_ffn_g'], 'm_ln_ffn_b': out['m_ln_ffn_b'], 'v_meta': out['v_meta'], 'v_a_w_in': out['v_a_w_in'], 'v_a_conv': out['v_a_conv'], 'v_a_log': out['v_a_log'], 'v_a_dt_bias': out['v_a_dt_bias'], 'v_a_norm': out['v_a_norm'], 'v_a_w_out': out['v_a_w_out'], 'v_b_w_in': out['v_b_w_in'], 'v_b_conv': out['v_b_conv'], 'v_b_w_out': out['v_b_w_out'], 'v_ln_mix_g': out['v_ln_mix_g'], 'v_ln_mix_b': out['v_ln_mix_b'], 'v_ffn_w_up': out['v_ffn_w_up'], 'v_ffn_conv': out['v_ffn_conv'], 'v_ffn_w_down': out['v_ffn_w_down'], 'v_ln_ffn_g': out['v_ln_ffn_g'], 'v_ln_ffn_b': out['v_ln_ffn_b']}


def _loss(weights, diff, rest, loss_target):
    with _jax.named_scope("forward"):
        args = {**rest, TWIN_DIFF_INPUT: diff, **{k: w.astype(_WEIGHT_DTYPES[k]) for k, w in weights.items()}}
        y = _forward(args)
    with _jax.named_scope("loss_head"):
        err = _jnp.square(y.astype(_jnp.float32) - loss_target)
        return 0.5 * _jnp.sum(_jnp.mean(err, axis=-1)) if err.ndim else 0.5 * err


def _adamw(w, g, m, v):
    m = ADAM_B1 * m + (1.0 - ADAM_B1) * g
    v = ADAM_B2 * v + (1.0 - ADAM_B2) * _jnp.square(g)
    m_hat = m / (1.0 - ADAM_B1 ** ADAM_STEP)
    v_hat = v / (1.0 - ADAM_B2 ** ADAM_STEP)
    delta = -ADAM_LR * (m_hat / (_jnp.sqrt(v_hat) + ADAM_EPS) + ADAM_WD * w)
    return delta, m, v


def reference(x, meta, a_w_in, a_conv, a_log, a_dt_bias, a_norm, a_w_out, b_w_in, b_conv, b_w_out, ln_mix_g, ln_mix_b, ffn_w_up, ffn_conv, ffn_w_down, ln_ffn_g, ln_ffn_b, loss_target, m_meta, m_a_w_in, m_a_conv, m_a_log, m_a_dt_bias, m_a_norm, m_a_w_out, m_b_w_in, m_b_conv, m_b_w_out, m_ln_mix_g, m_ln_mix_b, m_ffn_w_up, m_ffn_conv, m_ffn_w_down, m_ln_ffn_g, m_ln_ffn_b, v_meta, v_a_w_in, v_a_conv, v_a_log, v_a_dt_bias, v_a_norm, v_a_w_out, v_b_w_in, v_b_conv, v_b_w_out, v_ln_mix_g, v_ln_mix_b, v_ffn_w_up, v_ffn_conv, v_ffn_w_down, v_ln_ffn_g, v_ln_ffn_b):
    given = dict(x=x, meta=meta, a_w_in=a_w_in, a_conv=a_conv, a_log=a_log, a_dt_bias=a_dt_bias, a_norm=a_norm, a_w_out=a_w_out, b_w_in=b_w_in, b_conv=b_conv, b_w_out=b_w_out, ln_mix_g=ln_mix_g, ln_mix_b=ln_mix_b, ffn_w_up=ffn_w_up, ffn_conv=ffn_conv, ffn_w_down=ffn_w_down, ln_ffn_g=ln_ffn_g, ln_ffn_b=ln_ffn_b, loss_target=loss_target, m_meta=m_meta, m_a_w_in=m_a_w_in, m_a_conv=m_a_conv, m_a_log=m_a_log, m_a_dt_bias=m_a_dt_bias, m_a_norm=m_a_norm, m_a_w_out=m_a_w_out, m_b_w_in=m_b_w_in, m_b_conv=m_b_conv, m_b_w_out=m_b_w_out, m_ln_mix_g=m_ln_mix_g, m_ln_mix_b=m_ln_mix_b, m_ffn_w_up=m_ffn_w_up, m_ffn_conv=m_ffn_conv, m_ffn_w_down=m_ffn_w_down, m_ln_ffn_g=m_ln_ffn_g, m_ln_ffn_b=m_ln_ffn_b, v_meta=v_meta, v_a_w_in=v_a_w_in, v_a_conv=v_a_conv, v_a_log=v_a_log, v_a_dt_bias=v_a_dt_bias, v_a_norm=v_a_norm, v_a_w_out=v_a_w_out, v_b_w_in=v_b_w_in, v_b_conv=v_b_conv, v_b_w_out=v_b_w_out, v_ln_mix_g=v_ln_mix_g, v_ln_mix_b=v_ln_mix_b, v_ffn_w_up=v_ffn_w_up, v_ffn_conv=v_ffn_conv, v_ffn_w_down=v_ffn_w_down, v_ln_ffn_g=v_ln_ffn_g, v_ln_ffn_b=v_ln_ffn_b)
    weights = {n: given[n] for n in TWIN_WEIGHTS}
    shared = {n: given[n] for n in SHARED_INPUTS}
    per_example = {n: given[n] for n in ['x']}
    grad_fn = _jax.value_and_grad(_loss, argnums=(0, 1))

    def one_microbatch(ex, loss_target):
        ex = dict(ex)
        diff = ex.pop(TWIN_DIFF_INPUT)
        return grad_fn(weights, diff, {**shared, **ex}, loss_target)

    if N_MICROBATCH == 1:
        loss, (grad_w, grad_x) = one_microbatch(per_example, given["loss_target"])
    else:
        def body(carry, xs):
            loss_sum, grad_sum = carry
            l_k, (gw_k, gx_k) = one_microbatch(xs[0], xs[1])
            with _jax.named_scope("update"):
                return (loss_sum + l_k, _jax.tree.map(_jnp.add, grad_sum, gw_k)), gx_k

        init = (_jnp.zeros((), _jnp.float32), _jax.tree.map(_jnp.zeros_like, weights))
        (loss, grad_w), grad_x = _jax.lax.scan(body, init, (per_example, given["loss_target"]))
    with _jax.named_scope("update"):
        delta_w, new_m, new_v = {}, {}, {}
        for n in TWIN_WEIGHTS:
            delta_w[n], new_m[n], new_v[n] = _adamw(weights[n], grad_w[n], given["m_" + n], given["v_" + n])
    return (loss, grad_x, *[grad_w[n] for n in TWIN_WEIGHTS], *[delta_w[n] for n in TWIN_WEIGHTS],
            *[new_m[n] for n in TWIN_WEIGHTS], *[new_v[n] for n in TWIN_WEIGHTS])
```

```python
import functools

import jax
import jax.numpy as jnp
from jax import lax
from jax.experimental import pallas as pl
from jax.experimental.pallas import tpu as pltpu

F32 = jnp.float32
BF16 = jnp.bfloat16
HI = lax.Precision.HIGHEST

N_META = 16
HEADS = 8
HEAD_DIM = 128
CHUNK = 64
GDN_CONV = 4
SC_CONV = 3
FFN_CONV = 3
ALPHA = 4.0 ** 0.25
LN_EPS = 1e-5
RMS_EPS = 1e-6
L2_EPS = 1e-6
Q_SCALE = HEAD_DIM ** -0.5

ADAM_LR = 0.001
ADAM_B1 = 0.9
ADAM_B2 = 0.999
ADAM_EPS = 1e-08
ADAM_WD = 0.01
ADAM_STEP = 10

HALO = 8
VMEM_LIMIT = 48 * 1024 * 1024


def _params(sem=None):
    return pltpu.CompilerParams(dimension_semantics=sem, vmem_limit_bytes=VMEM_LIMIT)


def _dot(a, b, prec=None):
    return jnp.dot(a, b, preferred_element_type=F32, precision=prec)


def _dot_nt(a, b, prec=None):
    return lax.dot_general(a, b, (((1,), (1,)), ((), ())), preferred_element_type=F32, precision=prec)


def _dot_tn(a, b, prec=None):
    return lax.dot_general(a, b, (((0,), (0,)), ((), ())), preferred_element_type=F32, precision=prec)


def _sigmoid(x):
    return 1.0 / (1.0 + jnp.exp(-x))


def _tri_masks():
    r = lax.broadcasted_iota(jnp.int32, (CHUNK, CHUNK), 0)
    c = lax.broadcasted_iota(jnp.int32, (CHUNK, CHUNK), 1)
    return r >= c, r > c, r == c


def _gdn_decay(g64):
    causal, _, _ = _tri_masks()
    a = _dot(causal.astype(F32), g64, HI)
    return jnp.exp(jnp.where(causal, a - a.T, -1e30))


def _gdn_m(k, g64, bb):
    _, strict, _ = _tri_masks()
    return jnp.where(strict, _dot_nt(k * bb, k) * _gdn_decay(g64), 0.0)


def _gdn_inverse(m):
    _, _, diag = _tri_masks()
    t = diag.astype(F32) - m
    p = m
    for _ in range(5):
        p = _dot(p, p, HI)
        t = t + _dot(t, p, HI)
    return t


def _gdn_apply(q, k, v, gb, g64, bb, s, t):
    causal, _, _ = _tri_masks()
    gc = _dot(causal.astype(F32), gb, HI)
    decay = _gdn_decay(g64)
    eg = jnp.exp(gc)
    kb = k * bb
    u = _dot(t, v * bb)
    w = _dot(t, kb * eg)
    qk = _dot_nt(q, k) * decay
    gl = jnp.sum(gb, axis=0, keepdims=True)
    kd = k * jnp.exp(gl - gc)
    v_new = u - _dot(w, s)
    o = _dot(q * eg, s) + _dot(qk, v_new)
    s2 = s * jnp.exp(gl) + _dot_tn(kd, v_new)
    return o, s2


def _gdn_head_refs(x_ref, h):
    sl = slice(h * HEAD_DIM, (h + 1) * HEAD_DIM)
    sl64 = slice(h * HEAD_DIM, h * HEAD_DIM + CHUNK)
    return (x_ref[0, :, sl], x_ref[1, :, sl], x_ref[2, :, sl], x_ref[4, :, sl], x_ref[4, :, sl64], x_ref[3, :, sl]), sl, sl64


def gdn_chunk_fwd(qkvbg):
    _, lp, width = qkvbg.shape
    n_chunks = lp // CHUNK

    def body(x_ref, o_ref, s_ref, t_ref, state):
        @pl.when(pl.program_id(0) == 0)
        def _():
            state[...] = jnp.zeros_like(state)

        for h in range(HEADS):
            (q, k, v, gb, g64, bb), sl, _ = _gdn_head_refs(x_ref, h)
            s = state[h]
            t = _gdn_inverse(_gdn_m(k, g64, bb))
            o, s2 = _gdn_apply(q, k, v, gb, g64, bb, s, t)
            s_ref[0, h] = s
            t_ref[0, h] = t
            o_ref[:, sl] = o
            state[h] = s2

    return pl.pallas_call(
        body,
        name="gdn_chunk_fwd",
        grid=(n_chunks,),
        in_specs=[pl.BlockSpec((5, CHUNK, width), lambda c: (0, c, 0))],
        out_specs=[
            pl.BlockSpec((CHUNK, width), lambda c: (c, 0)),
            pl.BlockSpec((1, HEADS, HEAD_DIM, HEAD_DIM), lambda c: (c, 0, 0, 0)),
            pl.BlockSpec((1, HEADS, CHUNK, CHUNK), lambda c: (c, 0, 0, 0)),
        ],
        out_shape=[
            jax.ShapeDtypeStruct((lp, width), F32),
            jax.ShapeDtypeStruct((n_chunks, HEADS, HEAD_DIM, HEAD_DIM), F32),
            jax.ShapeDtypeStruct((n_chunks, HEADS, CHUNK, CHUNK), F32),
        ],
        scratch_shapes=[pltpu.VMEM((HEADS, HEAD_DIM, HEAD_DIM), F32)],
        compiler_params=_params(("arbitrary",)),
    )(qkvbg)


def gdn_chunk_bwd(qkvbg, states, tinv, d_o):
    _, lp, width = qkvbg.shape
    n_chunks = lp // CHUNK
    last = n_chunks - 1

    def body(x_ref, s_ref, t_ref, do_ref, dx_ref, dstate):
        @pl.when(pl.program_id(0) == 0)
        def _():
            dstate[...] = jnp.zeros_like(dstate)

        for h in range(HEADS):
            (q, k, v, gb, g64, bb), sl, sl64 = _gdn_head_refs(x_ref, h)
            s = s_ref[0, h]
            t = t_ref[0, h]
            _, vjp_apply = jax.vjp(_gdn_apply, q, k, v, gb, g64, bb, s, t)
            dq, dk, dv, dgb, dg64, dbb, ds, dt = vjp_apply((do_ref[:, sl], dstate[h]))
            tt = t.T
            dm = -_dot(_dot(tt, dt), tt)
            _, vjp_m = jax.vjp(_gdn_m, k, g64, bb)
            dk2, dg64m, dbb2 = vjp_m(dm)
            dx_ref[0, :, sl] = dq
            dx_ref[1, :, sl] = dk + dk2
            dx_ref[2, :, sl] = dv
            dx_ref[3, :, sl] = dbb + dbb2
            dx_ref[4, :, sl] = dgb
            dx_ref[4, :, sl64] += dg64 + dg64m
            dstate[h] = ds

    return pl.pallas_call(
        body,
        name="gdn_chunk_bwd",
        grid=(n_chunks,),
        in_specs=[
            pl.BlockSpec((5, CHUNK, width), lambda c: (0, last - c, 0)),
            pl.BlockSpec((1, HEADS, HEAD_DIM, HEAD_DIM), lambda c: (last - c, 0, 0, 0)),
            pl.BlockSpec((1, HEADS, CHUNK, CHUNK), lambda c: (last - c, 0, 0, 0)),
            pl.BlockSpec((CHUNK, width), lambda c: (last - c, 0)),
        ],
        out_specs=pl.BlockSpec((5, CHUNK, width), lambda c: (0, last - c, 0)),
        out_shape=jax.ShapeDtypeStruct(qkvbg.shape, F32),
        scratch_shapes=[pltpu.VMEM((HEADS, HEAD_DIM, HEAD_DIM), F32)],
        compiler_params=_params(("arbitrary",)),
    )(qkvbg, states, tinv, d_o)


def mm_nn(a, b, *, tm, name):
    ks, m, tk = a.shape
    _, ns, _, tn = b.shape

    def body(a_ref, b_ref, o_ref):
        p = _dot(a_ref[...].astype(BF16), b_ref[...])

        @pl.when(pl.program_id(2) == 0)
        def _():
            o_ref[...] = p

        @pl.when(pl.program_id(2) > 0)
        def _():
            o_ref[...] += p

    return pl.pallas_call(
        body,
        name=name,
        grid=(ns, m // tm, ks),
        in_specs=[
            pl.BlockSpec((None, tm, tk), lambda n, i, k: (k, i, 0)),
            pl.BlockSpec((None, None, tk, tn), lambda n, i, k: (k, n, 0, 0)),
        ],
        out_specs=pl.BlockSpec((None, tm, tn), lambda n, i, k: (n, i, 0)),
        out_shape=jax.ShapeDtypeStruct((ns, m, tn), F32),
        compiler_params=_params(("arbitrary", "arbitrary", "arbitrary")),
    )(a, b)


def mm_nt(dy, w, *, tm, name, res=None, res_scale=1.0):
    ns, m, tn = dy.shape
    ks, _, tk, _ = w.shape

    def body(*refs):
        if res is None:
            dy_ref, w_ref, o_ref = refs
        else:
            dy_ref, w_ref, r_ref, o_ref = refs
        p = _dot_nt(dy_ref[...].astype(BF16), w_ref[...])

        @pl.when(pl.program_id(2) == 0)
        def _():
            o_ref[...] = p if res is None else p + res_scale * r_ref[...]

        @pl.when(pl.program_id(2) > 0)
        def _():
            o_ref[...] += p

    in_specs = [
        pl.BlockSpec((None, tm, tn), lambda k, i, n: (n, i, 0)),
        pl.BlockSpec((None, None, tk, tn), lambda k, i, n: (k, n, 0, 0)),
    ]
    args = [dy, w]
    if res is not None:
        in_specs.append(pl.BlockSpec((None, tm, tk), lambda k, i, n: (k, i, 0)))
        args.append(res)
    return pl.pallas_call(
        body,
        name=name,
        grid=(ks, m // tm, ns),
        in_specs=in_specs,
        out_specs=pl.BlockSpec((None, tm, tk), lambda k, i, n: (k, i, 0)),
        out_shape=jax.ShapeDtypeStruct((ks, m, tk), F32),
        compiler_params=_params(("arbitrary", "arbitrary", "arbitrary")),
    )(*args)


def mm_tn(x, dy, *, tm, name):
    ks, m, tk = x.shape
    ns, _, tn = dy.shape

    def body(x_ref, dy_ref, o_ref):
        p = _dot_tn(x_ref[...].astype(BF16), dy_ref[...].astype(BF16))

        @pl.when(pl.program_id(2) == 0)
        def _():
            o_ref[...] = p

        @pl.when(pl.program_id(2) > 0)
        def _():
            o_ref[...] += p

    return pl.pallas_call(
        body,
        name=name,
        grid=(ks, ns, m // tm),
        in_specs=[
            pl.BlockSpec((None, tm, tk), lambda k, n, i: (k, i, 0)),
            pl.BlockSpec((None, tm, tn), lambda k, n, i: (n, i, 0)),
        ],
        out_specs=pl.BlockSpec((None, None, tk, tn), lambda k, n, i: (k, n, 0, 0)),
        out_shape=jax.ShapeDtypeStruct((ks, ns, tk, tn), F32),
        compiler_params=_params(("arbitrary", "arbitrary", "arbitrary")),
    )(x, dy)


def _row_partial(x):
    rows, c = x.shape
    return jnp.sum(x.reshape(rows // 8, 8, c), axis=0)


def ln_fwd(h_prev, mix, g, b, *, tm, name):
    _, lp, d = h_prev.shape

    def body(h_ref, m_ref, g_ref, b_ref, r_ref, o_ref):
        r = ALPHA * h_ref[...] + m_ref[...]
        mu = jnp.mean(r, axis=-1, keepdims=True)
        xc = r - mu
        var = jnp.mean(xc * xc, axis=-1, keepdims=True)
        r_ref[...] = r
        o_ref[...] = xc * lax.rsqrt(var + LN_EPS) * g_ref[...] + b_ref[...]

    row = pl.BlockSpec((None, tm, d), lambda i: (0, i, 0))
    vec = pl.BlockSpec((1, d), lambda i: (0, 0))
    return pl.pallas_call(
        body,
        name=name,
        grid=(lp // tm,),
        in_specs=[row, row, vec, vec],
        out_specs=[row, row],
        out_shape=[jax.ShapeDtypeStruct((1, lp, d), F32)] * 2,
        compiler_params=_params(("arbitrary",)),
    )(h_prev, mix, g, b)


def ln_bwd(r, dh, g, *, tm, name):
    _, lp, d = r.shape

    def body(r_ref, dh_ref, g_ref, dr_ref, dgb_ref):
        x = r_ref[...]
        dh_v = dh_ref[...]
        mu = jnp.mean(x, axis=-1, keepdims=True)
        xc = x - mu
        rstd = lax.rsqrt(jnp.mean(xc * xc, axis=-1, keepdims=True) + LN_EPS)
        xh = xc * rstd
        dxh = dh_v * g_ref[...]
        m1 = jnp.mean(dxh, axis=-1, keepdims=True)
        m2 = jnp.mean(dxh * xh, axis=-1, keepdims=True)
        dr_ref[...] = rstd * (dxh - m1 - xh * m2)

        @pl.when(pl.program_id(0) == 0)
        def _():
            dgb_ref[...] = jnp.zeros_like(dgb_ref)

        dgb_ref[0] += _row_partial(dh_v * xh)
        dgb_ref[1] += _row_partial(dh_v)

    row = pl.BlockSpec((None, tm, d), lambda i: (0, i, 0))
    return pl.pallas_call(
        body,
        name=name,
        grid=(lp // tm,),
        in_specs=[row, row, pl.BlockSpec((1, d), lambda i: (0, 0))],
        out_specs=[row, pl.BlockSpec((2, 8, d), lambda i: (0, 0, 0))],
        out_shape=[jax.ShapeDtypeStruct((1, lp, d), F32), jax.ShapeDtypeStruct((2, 8, d), F32)],
        compiler_params=_params(("arbitrary",)),
    )(r, dh, g)


def loss_grad(h, target, *, first, count, tm):
    _, lp, d = h.shape

    def body(h_ref, t_ref, dh_ref, l_ref):
        row = pl.program_id(0) * tm + lax.broadcasted_iota(jnp.int32, (tm, d), 0)
        valid = (row >= first) & (row < first + count)
        err = jnp.where(valid, h_ref[...] - t_ref[...], 0.0)
        dh_ref[...] = err * (1.0 / d)

        @pl.when(pl.program_id(0) == 0)
        def _():
            l_ref[...] = jnp.zeros_like(l_ref)

        l_ref[...] += _row_partial(err * err) * (0.5 / d)

    return pl.pallas_call(
        body,
        name="loss_grad",
        grid=(lp // tm,),
        in_specs=[pl.BlockSpec((None, tm, d), lambda i: (0, i, 0)), pl.BlockSpec((tm, d), lambda i: (i, 0))],
        out_specs=[pl.BlockSpec((None, tm, d), lambda i: (0, i, 0)), pl.BlockSpec((8, d), lambda i: (0, 0))],
        out_shape=[jax.ShapeDtypeStruct((1, lp, d), F32), jax.ShapeDtypeStruct((8, d), F32)],
        compiler_params=_params(("arbitrary",)),
    )(h, target)


def _halo_index(tile, tm):
    return jnp.maximum(tile * (tm // HALO) - 1, 0)


def _conv_fwd(xs_ref, w, taps, tm):
    acc = w(0) * xs_ref[pl.ds(HALO - taps + 1, tm), :]
    for j in range(1, taps):
        acc += w(j) * xs_ref[pl.ds(HALO - taps + 1 + j, tm), :]
    return acc


def _conv_bwd_x(dcs_ref, w, taps, tm):
    acc = w(0) * dcs_ref[pl.ds(taps - 1, tm), :]
    for j in range(1, taps):
        acc += w(j) * dcs_ref[pl.ds(taps - 1 - j, tm), :]
    return acc


def _conv_bwd_w(dc, xs_ref, taps, tm):
    return [jnp.sum(dc * xs_ref[pl.ds(HALO - taps + 1 + j, tm), :], axis=0, keepdims=True) for j in range(taps)]


def _silu_parts(c):
    sg = _sigmoid(c)
    return c * sg, sg * (1.0 + c * (1.0 - sg))


def _head_sum(x):
    rows, c = x.shape
    parts = []
    for h in range(c // HEAD_DIM):
        s = jnp.sum(x[:, h * HEAD_DIM:(h + 1) * HEAD_DIM], axis=-1, keepdims=True)
        parts.append(jnp.broadcast_to(s, (rows, HEAD_DIM)))
    return parts[0] if len(parts) == 1 else jnp.concatenate(parts, axis=-1)


def _log1p(y):
    u = 1.0 + y
    d = u - 1.0
    return jnp.where(d == 0.0, y, jnp.log(u) * (y / jnp.where(d == 0.0, 1.0, d)))


def _softplus(x):
    return jnp.maximum(x, 0.0) + _log1p(jnp.exp(-jnp.abs(x)))


def gdn_pre_fwd(p5, conv_w, alog_b, dtb_b, *, tm, cb):
    _, lp, width = p5.shape
    taps = conv_w.shape[1]

    def body(x_ref, halo_ref, w_ref, al_ref, dt_ref, o_ref, xs):
        i = pl.program_id(1)
        for s in range(3):
            xs[s, 0:HALO, :] = jnp.where(i > 0, halo_ref[s], 0.0)
            xs[s, HALO:, :] = x_ref[s]
            c = _conv_fwd(xs.at[s], lambda j, s=s: w_ref[s, j:j + 1, :], taps, tm)
            y, _ = _silu_parts(c)
            if s < 2:
                y = y * lax.rsqrt(_head_sum(y * y) + L2_EPS)
                if s == 0:
                    y = y * Q_SCALE
            o_ref[s] = y
        o_ref[3] = _sigmoid(x_ref[3])
        o_ref[4] = -jnp.exp(al_ref[...]) * _softplus(x_ref[4] + dt_ref[...])

    return pl.pallas_call(
        body,
        name="gdn_pre_fwd",
        grid=(width // cb, lp // tm),
        in_specs=[
            pl.BlockSpec((5, tm, cb), lambda j, i: (0, i, j)),
            pl.BlockSpec((3, HALO, cb), lambda j, i: (0, _halo_index(i, tm), j)),
            pl.BlockSpec((3, taps, cb), lambda j, i: (0, 0, j)),
            pl.BlockSpec((1, cb), lambda j, i: (0, j)),
            pl.BlockSpec((1, cb), lambda j, i: (0, j)),
        ],
        out_specs=pl.BlockSpec((5, tm, cb), lambda j, i: (0, i, j)),
        out_shape=jax.ShapeDtypeStruct((5, lp, width), F32),
        scratch_shapes=[pltpu.VMEM((3, tm + HALO, cb), F32)],
        compiler_params=_params(("arbitrary", "arbitrary")),
    )(p5, p5, conv_w, alog_b, dtb_b)


def gdn_pre_bwd(p5, dqkvbg, conv_w, alog_b, dtb_b, *, tm, cb):
    _, lp, width = p5.shape
    taps = conv_w.shape[1]
    last = lp // tm - 1

    def body(x_ref, halo_ref, d_ref, w_ref, al_ref, dt_ref, dx_ref, dw_ref, dsc_ref, xs, dcs, carry):
        step = pl.program_id(1)
        tile = last - step

        @pl.when(step == 0)
        def _():
            carry[...] = jnp.zeros_like(carry)
            dw_ref[...] = jnp.zeros_like(dw_ref)
            dsc_ref[...] = jnp.zeros_like(dsc_ref)

        for s in range(3):
            w = lambda j, s=s: w_ref[s, j:j + 1, :]
            xs[s, 0:HALO, :] = jnp.where(tile > 0, halo_ref[s], 0.0)
            xs[s, HALO:, :] = x_ref[s]
            c = _conv_fwd(xs.at[s], w, taps, tm)
            y, dsilu = _silu_parts(c)
            dy = d_ref[s]
            if s < 2:
                rn = lax.rsqrt(_head_sum(y * y) + L2_EPS)
                yn = y * rn
                if s == 0:
                    dy = dy * Q_SCALE
                dy = rn * (dy - yn * _head_sum(dy * yn))
            dc = dy * dsilu
            dcs[s, 0:tm, :] = dc
            dcs[s, tm:, :] = carry[s]
            dx_ref[s] = _conv_bwd_x(dcs.at[s], w, taps, tm)
            carry[s] = dc[0:HALO, :]
            for j, row in enumerate(_conv_bwd_w(dc, xs.at[s], taps, tm)):
                dw_ref[s, j:j + 1, :] += row
        beta = _sigmoid(x_ref[3])
        dx_ref[3] = d_ref[3] * beta * (1.0 - beta)
        z = x_ref[4] + dt_ref[...]
        neg_ea = -jnp.exp(al_ref[...])
        da = d_ref[4] * neg_ea * _sigmoid(z)
        dx_ref[4] = da
        dsc_ref[0] += _row_partial(d_ref[4] * neg_ea * _softplus(z))
        dsc_ref[1] += _row_partial(da)

    tile_spec = pl.BlockSpec((5, tm, cb), lambda j, i: (0, last - i, j))
    return pl.pallas_call(
        body,
        name="gdn_pre_bwd",
        grid=(width // cb, lp // tm),
        in_specs=[
            tile_spec,
            pl.BlockSpec((3, HALO, cb), lambda j, i: (0, _halo_index(last - i, tm), j)),
            tile_spec,
            pl.BlockSpec((3, taps, cb), lambda j, i: (0, 0, j)),
            pl.BlockSpec((1, cb), lambda j, i: (0, j)),
            pl.BlockSpec((1, cb), lambda j, i: (0, j)),
        ],
        out_specs=[
            tile_spec,
            pl.BlockSpec((3, taps, cb), lambda j, i: (0, 0, j)),
            pl.BlockSpec((2, 8, cb), lambda j, i: (0, 0, j)),
        ],
        out_shape=[
            jax.ShapeDtypeStruct((5, lp, width), F32),
            jax.ShapeDtypeStruct((3, taps, width), F32),
            jax.ShapeDtypeStruct((2, 8, width), F32),
        ],
        scratch_shapes=[
            pltpu.VMEM((3, tm + HALO, cb), F32),
            pltpu.VMEM((3, tm + HALO, cb), F32),
            pltpu.VMEM((3, HALO, cb), F32),
        ],
        compiler_params=_params(("arbitrary", "arbitrary")),
    )(p5, p5, dqkvbg, conv_w, alog_b, dtb_b)


def gdn_post_fwd(o, z, nw_b, *, tm):
    _, lp, width = o.shape

    def body(o_ref, z_ref, nw_ref, y_ref):
        ov = o_ref[...]
        rn = lax.rsqrt(_head_sum(ov * ov) * (1.0 / HEAD_DIM) + RMS_EPS)
        gate, _ = _silu_parts(z_ref[...])
        y_ref[...] = ov * rn * nw_ref[...] * gate

    row = pl.BlockSpec((None, tm, width), lambda i: (0, i, 0))
    return pl.pallas_call(
        body,
        name="gdn_post_fwd",
        grid=(lp // tm,),
        in_specs=[row, row, pl.BlockSpec((1, width), lambda i: (0, 0))],
        out_specs=row,
        out_shape=jax.ShapeDtypeStruct((1, lp, width), F32),
        compiler_params=_params(("arbitrary",)),
    )(o, z, nw_b)


def gdn_post_bwd(o, z, dy, nw_b, *, tm):
    _, lp, width = o.shape

    def body(o_ref, z_ref, dy_ref, nw_ref, do_ref, dz_ref, dnw_ref):
        ov = o_ref[...]
        rn = lax.rsqrt(_head_sum(ov * ov) * (1.0 / HEAD_DIM) + RMS_EPS)
        yn = ov * rn
        gate, dgate = _silu_parts(z_ref[...])
        d_on = dy_ref[...] * gate
        dz_ref[...] = dy_ref[...] * yn * nw_ref[...] * dgate
        a = d_on * nw_ref[...]
        do_ref[...] = rn * (a - yn * (_head_sum(a * yn) * (1.0 / HEAD_DIM)))

        @pl.when(pl.program_id(0) == 0)
        def _():
            dnw_ref[...] = jnp.zeros_like(dnw_ref)

        dnw_ref[...] += _row_partial(d_on * yn)

    row = pl.BlockSpec((None, tm, width), lambda i: (0, i, 0))
    return pl.pallas_call(
        body,
        name="gdn_post_bwd",
        grid=(lp // tm,),
        in_specs=[row, row, row, pl.BlockSpec((1, width), lambda i: (0, 0))],
        out_specs=[row, row, pl.BlockSpec((8, width), lambda i: (0, 0))],
        out_shape=[jax.ShapeDtypeStruct((1, lp, width), F32)] * 2 + [jax.ShapeDtypeStruct((8, width), F32)],
        compiler_params=_params(("arbitrary",)),
    )(o, z, dy, nw_b)


def head_lane_sum(x):
    s_n, rows, width = x.shape

    def body(x_ref, o_ref):
        lane = lax.broadcasted_iota(jnp.int32, (rows, HEAD_DIM), 1)
        acc = jnp.zeros((rows, HEAD_DIM), F32)
        for h in range(width // HEAD_DIM):
            s = jnp.sum(x_ref[:, h * HEAD_DIM:(h + 1) * HEAD_DIM], axis=-1, keepdims=True)
            acc = jnp.where(lane == h, s, acc)
        o_ref[...] = acc

    return pl.pallas_call(
        body,
        name="head_lane_sum",
        grid=(s_n,),
        in_specs=[pl.BlockSpec((None, rows, width), lambda s: (s, 0, 0))],
        out_specs=pl.BlockSpec((None, rows, HEAD_DIM), lambda s: (s, 0, 0)),
        out_shape=jax.ShapeDtypeStruct((s_n, rows, HEAD_DIM), F32),
        compiler_params=_params(("arbitrary",)),
    )(x)


def ffn_act_fwd(up, conv_w, *, tm, name):
    _, lp, c_w = up.shape
    taps = conv_w.shape[1]

    def body(u_ref, halo_ref, g_ref, w_ref, o_ref, xs):
        i = pl.program_id(1)
        xs[0:HALO, :] = jnp.where(i > 0, halo_ref[...], 0.0)
        xs[HALO:, :] = u_ref[...]
        y, _ = _silu_parts(_conv_fwd(xs, lambda j: w_ref[j:j + 1, :], taps, tm))
        o_ref[...] = y * g_ref[...]

    return pl.pallas_call(
        body,
        name=name,
        grid=(2, lp // tm),
        in_specs=[
            pl.BlockSpec((None, tm, c_w), lambda s, i: (s, i, 0)),
            pl.BlockSpec((None, HALO, c_w), lambda s, i: (s, _halo_index(i, tm), 0)),
            pl.BlockSpec((None, tm, c_w), lambda s, i: (2 + s, i, 0)),
            pl.BlockSpec((None, taps, c_w), lambda s, i: (s, 0, 0)),
        ],
        out_specs=pl.BlockSpec((None, tm, c_w), lambda s, i: (s, i, 0)),
        out_shape=jax.ShapeDtypeStruct((2, lp, c_w), F32),
        scratch_shapes=[pltpu.VMEM((tm + HALO, c_w), F32)],
        compiler_params=_params(("arbitrary", "arbitrary")),
    )(up, up, up, conv_w)


def ffn_act_bwd(up, dact, conv_w, *, tm, name):
    _, lp, c_w = up.shape
    taps = conv_w.shape[1]
    last = lp // tm - 1

    def body(u_ref, halo_ref, g_ref, d_ref, w_ref, dup_ref, dw_ref, xs, dcs, carry):
        step = pl.program_id(1)
        tile = last - step
        w = lambda j: w_ref[j:j + 1, :]

        @pl.when(step == 0)
        def _():
            carry[...] = jnp.zeros_like(carry)
            dw_ref[...] = jnp.zeros_like(dw_ref)

        xs[0:HALO, :] = jnp.where(tile > 0, halo_ref[...], 0.0)
        xs[HALO:, :] = u_ref[...]
        y, dsilu = _silu_parts(_conv_fwd(xs, w, taps, tm))
        dup_ref[1] = d_ref[...] * y
        dc = d_ref[...] * g_ref[...] * dsilu
        dcs[0:tm, :] = dc
        dcs[tm:, :] = carry[...]
        dup_ref[0] = _conv_bwd_x(dcs, w, taps, tm)
        carry[...] = dc[0:HALO, :]
        for j, row in enumerate(_conv_bwd_w(dc, xs, taps, tm)):
            dw_ref[j:j + 1, :] += row

    return pl.pallas_call(
        body,
        name=name,
        grid=(2, lp // tm),
        in_specs=[
            pl.BlockSpec((None, tm, c_w), lambda s, i: (s, last - i, 0)),
            pl.BlockSpec((None, HALO, c_w), lambda s, i: (s, _halo_index(last - i, tm), 0)),
            pl.BlockSpec((None, tm, c_w), lambda s, i: (2 + s, last - i, 0)),
            pl.BlockSpec((None, tm, c_w), lambda s, i: (s, last - i, 0)),
            pl.BlockSpec((None, taps, c_w), lambda s, i: (s, 0, 0)),
        ],
        out_specs=[
            pl.BlockSpec((2, None, tm, c_w), lambda s, i: (0, s, last - i, 0)),
            pl.BlockSpec((None, taps, c_w), lambda s, i: (s, 0, 0)),
        ],
        out_shape=[jax.ShapeDtypeStruct((2, 2, lp, c_w), F32), jax.ShapeDtypeStruct((2, taps, c_w), F32)],
        scratch_shapes=[
            pltpu.VMEM((tm + HALO, c_w), F32),
            pltpu.VMEM((tm + HALO, c_w), F32),
            pltpu.VMEM((HALO, c_w), F32),
        ],
        compiler_params=_params(("arbitrary", "arbitrary")),
    )(up, up, up, dact, conv_w)


def sc_fwd(pb, conv_w, *, tm, cb):
    _, lp, width = pb.shape
    taps = conv_w.shape[0]

    def body(x_ref, halo_ref, w_ref, o_ref, xs):
        i = pl.program_id(1)
        xs[0:HALO, :] = jnp.where(i > 0, halo_ref[1] * halo_ref[2], 0.0)
        xs[HALO:, :] = x_ref[1] * x_ref[2]
        o_ref[...] = x_ref[0] * _conv_fwd(xs, lambda j: w_ref[j:j + 1, :], taps, tm)

    return pl.pallas_call(
        body,
        name="sc_fwd",
        grid=(width // cb, lp // tm),
        in_specs=[
            pl.BlockSpec((3, tm, cb), lambda j, i: (0, i, j)),
            pl.BlockSpec((3, HALO, cb), lambda j, i: (0, _halo_index(i, tm), j)),
            pl.BlockSpec((taps, cb), lambda j, i: (0, j)),
        ],
        out_specs=pl.BlockSpec((None, tm, cb), lambda j, i: (0, i, j)),
        out_shape=jax.ShapeDtypeStruct((1, lp, width), F32),
        scratch_shapes=[pltpu.VMEM((tm + HALO, cb), F32)],
        compiler_params=_params(("arbitrary", "arbitrary")),
    )(pb, pb, conv_w)


def sc_bwd(pb, ds, conv_w, *, tm, cb):
    _, lp, width = pb.shape
    taps = conv_w.shape[0]
    last = lp // tm - 1

    def body(x_ref, halo_ref, d_ref, w_ref, dx_ref, dw_ref, xs, dcs, carry):
        step = pl.program_id(1)
        tile = last - step
        w = lambda j: w_ref[j:j + 1, :]

        @pl.when(step == 0)
        def _():
            carry[...] = jnp.zeros_like(carry)
            dw_ref[...] = jnp.zeros_like(dw_ref)

        xs[0:HALO, :] = jnp.where(tile > 0, halo_ref[1] * halo_ref[2], 0.0)
        xs[HALO:, :] = x_ref[1] * x_ref[2]
        dx_ref[0] = d_ref[...] * _conv_fwd(xs, w, taps, tm)
        dc = d_ref[...] * x_ref[0]
        dcs[0:tm, :] = dc
        dcs[tm:, :] = carry[...]
        dp = _conv_bwd_x(dcs, w, taps, tm)
        dx_ref[1] = dp * x_ref[2]
        dx_ref[2] = dp * x_ref[1]
        carry[...] = dc[0:HALO, :]
        for j, row in enumerate(_conv_bwd_w(dc, xs, taps, tm)):
            dw_ref[j:j + 1, :] += row

    tile_spec = pl.BlockSpec((3, tm, cb), lambda j, i: (0, last - i, j))
    return pl.pallas_call(
        body,
        name="sc_bwd",
        grid=(width // cb, lp // tm),
        in_specs=[
            tile_spec,
            pl.BlockSpec((3, HALO, cb), lambda j, i: (0, _halo_index(last - i, tm), j)),
            pl.BlockSpec((None, tm, cb), lambda j, i: (0, last - i, j)),
            pl.BlockSpec((taps, cb), lambda j, i: (0, j)),
        ],
        out_specs=[tile_spec, pl.BlockSpec((taps, cb), lambda j, i: (0, j))],
        out_shape=[jax.ShapeDtypeStruct((3, lp, width), F32), jax.ShapeDtypeStruct((taps, width), F32)],
        scratch_shapes=[
            pltpu.VMEM((tm + HALO, cb), F32),
            pltpu.VMEM((tm + HALO, cb), F32),
            pltpu.VMEM((HALO, cb), F32),
        ],
        compiler_params=_params(("arbitrary", "arbitrary")),
    )(pb, pb, ds, conv_w)


TILE_BYTES = 1536 * 1024


def _rows_tile(rows, cols):
    if rows * cols * 4 <= TILE_BYTES or rows % 8:
        return rows
    best = 8
    for t in range(8, rows + 1, 8):
        if rows % t == 0 and t * cols * 4 <= TILE_BYTES:
            best = t
    return best


def add_n(xs, name):
    rows, cols = xs[0].shape
    tr = _rows_tile(rows, cols)

    def body(*refs):
        acc = refs[0][...] + refs[1][...]
        for r in refs[2:-1]:
            acc = acc + r[...]
        refs[-1][...] = acc

    spec = pl.BlockSpec((tr, cols), lambda i: (i, 0))
    return pl.pallas_call(
        body,
        name=name,
        grid=(rows // tr,),
        in_specs=[spec] * len(xs),
        out_specs=spec,
        out_shape=jax.ShapeDtypeStruct((rows, cols), F32),
        compiler_params=_params(("arbitrary",)),
    )(*xs)


def adamw(w, g, m, v, name):
    shape = w.shape
    cols = shape[-1]
    rows = w.size // cols
    tr = _rows_tile(rows, cols)

    def body(w_ref, g_ref, m_ref, v_ref, d_ref, m2_ref, v2_ref):
        gv = g_ref[...]
        m2 = ADAM_B1 * m_ref[...] + (1.0 - ADAM_B1) * gv
        v2 = ADAM_B2 * v_ref[...] + (1.0 - ADAM_B2) * (gv * gv)
        m_hat = m2 / (1.0 - ADAM_B1 ** ADAM_STEP)
        v_hat = v2 / (1.0 - ADAM_B2 ** ADAM_STEP)
        d_ref[...] = -ADAM_LR * (m_hat / (jnp.sqrt(v_hat) + ADAM_EPS) + ADAM_WD * w_ref[...])
        m2_ref[...] = m2
        v2_ref[...] = v2

    spec = pl.BlockSpec((tr, cols), lambda i: (i, 0))
    outs = pl.pallas_call(
        body,
        name=name,
        grid=(rows // tr,),
        in_specs=[spec] * 4,
        out_specs=[spec] * 3,
        out_shape=[jax.ShapeDtypeStruct((rows, cols), F32)] * 3,
        compiler_params=_params(("arbitrary",)),
    )(*[t.reshape(rows, cols) for t in (w, g, m, v)])
    return tuple(o.reshape(shape) for o in outs)


MESH_ID = pl.DeviceIdType.MESH
ANY = pl.BlockSpec(memory_space=pl.ANY)


def _place():
    x, y, c = lax.axis_index("x"), lax.axis_index("y"), lax.axis_index("c")
    other_chips = [(1 - x, y), (x, 1 - y), (1 - x, 1 - y)]
    return x, y, c, other_chips


def all_gather_shards(buf, name):
    rows, cols = buf.shape
    half = rows // 2

    def body(x_ref, o_ref, send_sems, recv_sems, local_sem):
        x, y, c, chips = _place()
        me = 2 * x + y
        sibling = (x, y, 1 - c)

        def part(slot, hf):
            return o_ref.at[slot, pl.ds(hf * half, half), :]

        def copy(k, src, dst, to):
            return pltpu.make_async_remote_copy(src_ref=src, dst_ref=dst, send_sem=send_sems.at[k],
                                                recv_sem=recv_sems.at[k], device_id=to, device_id_type=MESH_ID)

        mine = x_ref.at[pl.ds(c * half, half), :]
        local = pltpu.make_async_copy(x_ref, o_ref.at[me], local_sem)
        local.start()
        sent = [copy(j, mine, part(me, c), (px, py, c)) for j, (px, py) in enumerate(chips)]
        for cp in sent:
            cp.start()
        for j, (px, py) in enumerate(chips):
            landed = part(2 * px + py, c)
            copy(j, mine, landed, (px, py, c)).wait_recv()
            passed = copy(3 + j, landed, landed, sibling)
            passed.start()
            sent.append(passed)
        for j, (px, py) in enumerate(chips):
            theirs = part(2 * px + py, 1 - c)
            copy(3 + j, theirs, theirs, sibling).wait_recv()
        for cp in sent:
            cp.wait_send()
        local.wait()

    return pl.pallas_call(
        body,
        name=name,
        in_specs=[ANY],
        out_specs=ANY,
        out_shape=jax.ShapeDtypeStruct((4, rows, cols), buf.dtype),
        scratch_shapes=[pltpu.SemaphoreType.DMA((6,)), pltpu.SemaphoreType.DMA((6,)), pltpu.SemaphoreType.DMA],
    )(buf)


def swap_with_sibling(buf, name):
    def body(x_ref, o_ref, send_sem, recv_sem):
        x, y, c, _ = _place()
        cp = pltpu.make_async_remote_copy(src_ref=x_ref, dst_ref=o_ref, send_sem=send_sem, recv_sem=recv_sem,
                                          device_id=(x, y, 1 - c), device_id_type=MESH_ID)
        cp.start()
        cp.wait()

    return pl.pallas_call(
        body,
        name=name,
        in_specs=[ANY],
        out_specs=ANY,
        out_shape=jax.ShapeDtypeStruct(buf.shape, buf.dtype),
        scratch_shapes=[pltpu.SemaphoreType.DMA, pltpu.SemaphoreType.DMA],
    )(buf)


def scatter_to_chips(buf, name):
    def body(x_ref, o_ref, send_sems, recv_sems, local_sem):
        x, y, c, chips = _place()
        me = 2 * x + y
        local = pltpu.make_async_copy(x_ref.at[me], o_ref.at[me], local_sem)
        local.start()

        def copy(j, px, py):
            return pltpu.make_async_remote_copy(src_ref=x_ref.at[2 * px + py], dst_ref=o_ref.at[me],
                                                send_sem=send_sems.at[j], recv_sem=recv_sems.at[j],
                                                device_id=(px, py, c), device_id_type=MESH_ID)

        sent = [copy(j, px, py) for j, (px, py) in enumerate(chips)]
        for cp in sent:
            cp.start()
        for j, (px, py) in enumerate(chips):
            slot = o_ref.at[2 * px + py]
            pltpu.make_async_remote_copy(src_ref=slot, dst_ref=slot, send_sem=send_sems.at[j], recv_sem=recv_sems.at[j],
                                         device_id=(px, py, c), device_id_type=MESH_ID).wait_recv()
        for cp in sent:
            cp.wait_send()
        local.wait()

    return pl.pallas_call(
        body,
        name=name,
        in_specs=[ANY],
        out_specs=ANY,
        out_shape=jax.ShapeDtypeStruct(buf.shape, buf.dtype),
        scratch_shapes=[pltpu.SemaphoreType.DMA((3,)), pltpu.SemaphoreType.DMA((3,)), pltpu.SemaphoreType.DMA],
    )(buf)


def share_halves(buf, name):
    half, cols = buf.shape

    def body(x_ref, o_ref, send_sem, recv_sem, local_sem):
        x, y, c, _ = _place()
        mine = o_ref.at[pl.ds(c * half, half), :]
        theirs = o_ref.at[pl.ds((1 - c) * half, half), :]
        local = pltpu.make_async_copy(x_ref, mine, local_sem)
        local.start()
        cp = pltpu.make_async_remote_copy(src_ref=x_ref, dst_ref=mine, send_sem=send_sem, recv_sem=recv_sem,
                                          device_id=(x, y, 1 - c), device_id_type=MESH_ID)
        cp.start()
        pltpu.make_async_remote_copy(src_ref=x_ref, dst_ref=theirs, send_sem=send_sem, recv_sem=recv_sem,
                                     device_id=(x, y, 1 - c), device_id_type=MESH_ID).wait_recv()
        cp.wait_send()
        local.wait()

    return pl.pallas_call(
        body,
        name=name,
        in_specs=[ANY],
        out_specs=ANY,
        out_shape=jax.ShapeDtypeStruct((2 * half, cols), buf.dtype),
        scratch_shapes=[pltpu.SemaphoreType.DMA, pltpu.SemaphoreType.DMA, pltpu.SemaphoreType.DMA],
    )(buf)


def reduce_scatter_grads(g):
    _, rows, cols = g.shape
    half = rows // 2
    c = lax.axis_index("c")
    mine = lax.dynamic_slice_in_dim(g, c * half, half, axis=1)
    theirs = lax.dynamic_slice_in_dim(g, (1 - c) * half, half, axis=1)
    landed = swap_with_sibling(theirs, "rs_pair")
    chip_sum = add_n([mine.reshape(4 * half, cols), landed.reshape(4 * half, cols)], "rs_pair_sum")
    from_chips = scatter_to_chips(chip_sum.reshape(4, half, cols), "rs_chips")
    total_half = add_n([from_chips[k] for k in range(4)], "rs_chip_sum")
    return share_halves(total_half, "rs_share")


def _row_tiles(length):
    return (640, 320) if length > 2048 else (128, 64)


def _local_step(x, target, wt):
    seq, d = x.shape
    length = N_META + seq
    tm, tm_ffn = _row_tiles(length)
    lp = -(-length // tm) * tm
    tail = jnp.zeros((lp - length, d), F32)
    h0 = jnp.concatenate([wt["meta"], x, tail], axis=0)[None]
    tgt = jnp.concatenate([jnp.zeros((N_META, d), F32), target, tail], axis=0)
    nn = functools.partial(mm_nn, tm=tm)
    nt = functools.partial(mm_nt, tm=tm)
    tn = functools.partial(mm_tn, tm=tm)
    ln_g = [wt["ln_mix_g"][0:1], wt["ln_ffn_g"][0:1], wt["ln_mix_g"][1:2], wt["ln_ffn_g"][1:2]]
    ln_b = [wt["ln_mix_b"][0:1], wt["ln_ffn_b"][0:1], wt["ln_mix_b"][1:2], wt["ln_ffn_b"][1:2]]

    p5 = nn(h0, wt["a5"], name="a_in5")
    pz = nn(h0, wt["az"], name="a_inz")
    qkvbg = gdn_pre_fwd(p5, wt["a_conv3"], wt["alog_b"], wt["dtb_b"], tm=tm, cb=2 * HEAD_DIM)
    o, states, tinv = gdn_chunk_fwd(qkvbg)
    onz = gdn_post_fwd(o[None], pz, wt["anorm_b"], tm=tm)
    r1, h1 = ln_fwd(h0, nn(onz, wt["a_out"], name="a_out"), ln_g[0], ln_b[0], tm=tm, name="ln1")
    up0 = nn(h1, wt["up"][0], name="up0")
    act0 = ffn_act_fwd(up0, wt["fconv"][0], tm=tm_ffn, name="ffn_act0")
    r2, h2 = ln_fwd(h1, nn(act0, wt["down"][0], name="down0"), ln_g[1], ln_b[1], tm=tm, name="ln2")
    pb = nn(h2, wt["b_in"], name="b_in")
    sc = sc_fwd(pb, wt["b_conv"], tm=tm, cb=4 * HEAD_DIM)
    r3, h3 = ln_fwd(h2, nn(sc, wt["b_out"], name="b_out"), ln_g[2], ln_b[2], tm=tm, name="ln3")
    up1 = nn(h3, wt["up"][1], name="up1")
    act1 = ffn_act_fwd(up1, wt["fconv"][1], tm=tm_ffn, name="ffn_act1")
    r4, h4 = ln_fwd(h3, nn(act1, wt["down"][1], name="down1"), ln_g[3], ln_b[3], tm=tm, name="ln4")

    dh4, loss_part = loss_grad(h4, tgt, first=N_META, count=seq, tm=tm)

    grads = {}
    dr4, dgb4 = ln_bwd(r4, dh4, ln_g[3], tm=tm, name="ln4_bwd")
    d_down1 = tn(act1, dr4, name="d_down1")
    dact1 = nt(dr4, wt["down"][1], name="d_act1")
    dup1, dfconv1 = ffn_act_bwd(up1, dact1, wt["fconv"][1], tm=tm_ffn, name="ffn_act1_bwd")
    dup1 = dup1.reshape(up1.shape)
    d_up1 = tn(h3, dup1, name="d_up1")
    dh3 = nt(dup1, wt["up"][1], res=dr4, res_scale=ALPHA, name="d_h3")

    dr3, dgb3 = ln_bwd(r3, dh3, ln_g[2], tm=tm, name="ln3_bwd")
    d_bout = tn(sc, dr3, name="d_b_out")
    dsc = nt(dr3, wt["b_out"], name="d_sc")
    dpb, dbconv = sc_bwd(pb, dsc, wt["b_conv"], tm=tm, cb=4 * HEAD_DIM)
    d_bin = tn(h2, dpb, name="d_b_in")
    dh2 = nt(dpb, wt["b_in"], res=dr3, res_scale=ALPHA, name="d_h2")

    dr2, dgb2 = ln_bwd(r2, dh2, ln_g[1], tm=tm, name="ln2_bwd")
    d_down0 = tn(act0, dr2, name="d_down0")
    dact0 = nt(dr2, wt["down"][0], name="d_act0")
    dup0, dfconv0 = ffn_act_bwd(up0, dact0, wt["fconv"][0], tm=tm_ffn, name="ffn_act0_bwd")
    dup0 = dup0.reshape(up0.shape)
    d_up0 = tn(h1, dup0, name="d_up0")
    dh1 = nt(dup0, wt["up"][0], res=dr2, res_scale=ALPHA, name="d_h1")

    dr1, dgb1 = ln_bwd(r1, dh1, ln_g[0], tm=tm, name="ln1_bwd")
    d_aout = tn(onz, dr1, name="d_a_out")
    donz = nt(dr1, wt["a_out"], name="d_onz")
    d_o, dz, dnw = gdn_post_bwd(o[None], pz, donz, wt["anorm_b"], tm=tm)
    dqkvbg = gdn_chunk_bwd(qkvbg, states, tinv, d_o[0])
    dp5, daconv, dscal = gdn_pre_bwd(p5, dqkvbg, wt["a_conv3"], wt["alog_b"], wt["dtb_b"], tm=tm, cb=2 * HEAD_DIM)
    d_a5 = tn(h0, dp5, name="d_a_in5")
    d_az = tn(h0, dz, name="d_a_inz")
    dh0 = nt(dp5, wt["a5"], res=dr1, res_scale=ALPHA, name="d_h0a")
    dh0 = nt(dz, wt["az"], res=dh0, res_scale=1.0, name="d_h0")

    width = HEADS * HEAD_DIM
    d_ba = head_lane_sum(d_a5[0, 3:5])[:, :, :HEADS]
    grads["a_w_in"] = jnp.concatenate([d_a5[0, 0], d_a5[0, 1], d_a5[0, 2], d_az[0, 0], d_ba[0], d_ba[1]], axis=1)[None]
    grads["a_conv"] = daconv.transpose(1, 0, 2).reshape(1, GDN_CONV, 3 * width)
    per_head = dscal.reshape(2, 8, HEADS, HEAD_DIM).sum(axis=(1, 3))
    grads["a_log"] = per_head[0][None]
    grads["a_dt_bias"] = per_head[1][None]
    grads["a_norm"] = dnw.reshape(8, HEADS, HEAD_DIM).sum(axis=(0, 1))[None]
    grads["a_w_out"] = d_aout.reshape(1, width, d)
    grads["b_w_in"] = d_bin[0].transpose(1, 0, 2).reshape(1, d, 3 * d)
    grads["b_conv"] = dbconv[None]
    grads["b_w_out"] = d_bout.reshape(1, d, d)
    lns = [dgb1, dgb2, dgb3, dgb4]
    grads["ln_mix_g"] = jnp.stack([lns[0][0].sum(0), lns[2][0].sum(0)])
    grads["ln_mix_b"] = jnp.stack([lns[0][1].sum(0), lns[2][1].sum(0)])
    grads["ln_ffn_g"] = jnp.stack([lns[1][0].sum(0), lns[3][0].sum(0)])
    grads["ln_ffn_b"] = jnp.stack([lns[1][1].sum(0), lns[3][1].sum(0)])
    grads["ffn_w_up"] = jnp.stack([t[0].transpose(1, 0, 2).reshape(d, -1) for t in (d_up0, d_up1)])
    grads["ffn_conv"] = jnp.stack([t.transpose(1, 0, 2).reshape(FFN_CONV, -1) for t in (dfconv0, dfconv1)])
    grads["ffn_w_down"] = jnp.stack([t.reshape(-1, d) for t in (d_down0, d_down1)])
    grads["meta"] = dh0[0, :N_META]
    return loss_part, dh0, grads


WEIGHTS = ["meta", "a_w_in", "a_conv", "a_log", "a_dt_bias", "a_norm", "a_w_out", "b_w_in", "b_conv", "b_w_out",
           "ln_mix_g", "ln_mix_b", "ffn_w_up", "ffn_conv", "ffn_w_down", "ln_ffn_g", "ln_ffn_b"]
MATMUL_WEIGHTS = ["a_w_in", "a_w_out", "b_w_in", "b_w_out", "ffn_w_up", "ffn_w_down"]
SMALL_SHARDED = ["a_conv", "b_conv", "ffn_conv", "meta"]
REPLICATED = ["a_log", "a_dt_bias", "a_norm", "ln_mix_g", "ln_mix_b", "ln_ffn_g", "ln_ffn_b"]
SHARD_AXIS = {"meta": 1, "a_w_in": 2, "a_conv": 2, "a_w_out": 1, "b_w_in": 2, "b_conv": 2, "b_w_out": 1,
              "ffn_w_up": 2, "ffn_conv": 2, "ffn_w_down": 1}
PACK_COLS = 1024
PACK_ROWS_MULTIPLE = 32


def _pack(pieces, lead=()):
    flat = jnp.concatenate([p.reshape(lead + (-1,)) for p in pieces], axis=-1)
    n = flat.shape[-1]
    rows = -(-n // (PACK_COLS * PACK_ROWS_MULTIPLE)) * PACK_ROWS_MULTIPLE
    flat = jnp.pad(flat, [(0, 0)] * len(lead) + [(0, rows * PACK_COLS - n)])
    return flat.reshape(lead + (rows, PACK_COLS))


def _unpack(buf, shapes, lead=()):
    flat = buf.reshape(lead + (-1,))
    out, off = [], 0
    for shp in shapes:
        n = 1
        for s in shp:
            n *= s
        out.append(flat[..., off:off + n].reshape(lead + tuple(shp)))
        off += n
    return out


def _join_shards(stacked, axis):
    return jnp.concatenate([stacked[k] for k in range(4)], axis=axis)


def _split_shards(full, axis):
    return jnp.stack(jnp.split(full, 4, axis=axis))


def _gather_weights(w):
    big = all_gather_shards(_pack([w[n].astype(BF16) for n in MATMUL_WEIGHTS]), "gather_matmul_weights")
    small = all_gather_shards(_pack([w[n] for n in SMALL_SHARDED]), "gather_small_weights")
    full = {}
    for names, buf in ((MATMUL_WEIGHTS, big), (SMALL_SHARDED, small)):
        for n, t in zip(names, _unpack(buf, [w[n].shape for n in names], lead=(4,))):
            full[n] = _join_shards(t, SHARD_AXIS[n])
    return _layout_weights(full, w)


def _layout_weights(full, w):
    d = full["a_w_in"].shape[1]
    width = HEADS * HEAD_DIM
    wt = {n: w[n] for n in ("ln_mix_g", "ln_mix_b", "ln_ffn_g", "ln_ffn_b")}
    w_in = full["a_w_in"][0]
    blocks = [w_in[:, s * width:(s + 1) * width] for s in range(4)]
    b_exp = jnp.repeat(w_in[:, 4 * width:4 * width + HEADS], HEAD_DIM, axis=1)
    a_exp = jnp.repeat(w_in[:, 4 * width + HEADS:], HEAD_DIM, axis=1)
    wt["a5"] = jnp.stack([blocks[0], blocks[1], blocks[2], b_exp, a_exp])[None]
    wt["az"] = blocks[3][None, None]
    wt["a_out"] = full["a_w_out"][0][None, None]
    wt["b_in"] = full["b_w_in"][0].reshape(d, 3, d).transpose(1, 0, 2)[None]
    wt["b_out"] = full["b_w_out"][0][None, None]
    n_ff = full["ffn_w_up"].shape[2] // 4
    wt["up"] = [full["ffn_w_up"][l].reshape(d, 4, n_ff).transpose(1, 0, 2)[None] for l in range(2)]
    wt["down"] = [full["ffn_w_down"][l].reshape(2, n_ff, d)[:, None] for l in range(2)]
    wt["a_conv3"] = full["a_conv"][0].reshape(GDN_CONV, 3, width).transpose(1, 0, 2)
    wt["b_conv"] = full["b_conv"][0]
    wt["fconv"] = [full["ffn_conv"][l].reshape(FFN_CONV, 2, n_ff).transpose(1, 0, 2) for l in range(2)]
    wt["meta"] = full["meta"]
    wt["alog_b"] = jnp.repeat(w["a_log"][0], HEAD_DIM)[None]
    wt["dtb_b"] = jnp.repeat(w["a_dt_bias"][0], HEAD_DIM)[None]
    wt["anorm_b"] = jnp.tile(w["a_norm"][0], HEADS)[None]
    return wt


def _reduce_grads(grads, loss_part, w):
    sharded = MATMUL_WEIGHTS + SMALL_SHARDED
    pieces = [_split_shards(grads[n], SHARD_AXIS[n]) for n in sharded]
    same = jnp.concatenate([grads[n].reshape(-1) for n in REPLICATED] + [jnp.sum(loss_part).reshape(1)])
    pieces.append(jnp.broadcast_to(same, (4,) + same.shape))
    total = reduce_scatter_grads(_pack(pieces, lead=(4,)))
    shapes = [w[n].shape for n in sharded] + [w[n].shape for n in REPLICATED] + [()]
    out = _unpack(total, shapes)
    return dict(zip(sharded + REPLICATED, out[:-1])), out[-1]


def kernel(x, meta, a_w_in, a_conv, a_log, a_dt_bias, a_norm, a_w_out, b_w_in, b_conv, b_w_out, ln_mix_g, ln_mix_b, ffn_w_up, ffn_conv, ffn_w_down, ln_ffn_g, ln_ffn_b, loss_target, m_meta, m_a_w_in, m_a_conv, m_a_log, m_a_dt_bias, m_a_norm, m_a_w_out, m_b_w_in, m_b_conv, m_b_w_out, m_ln_mix_g, m_ln_mix_b, m_ffn_w_up, m_ffn_conv, m_ffn_w_down, m_ln_ffn_g, m_ln_ffn_b, v_meta, v_a_w_in, v_a_conv, v_a_log, v_a_dt_bias, v_a_norm, v_a_w_out, v_b_w_in, v_b_conv, v_b_w_out, v_ln_mix_g, v_ln_mix_b, v_ffn_w_up, v_ffn_conv, v_ffn_w_down, v_ln_ffn_g, v_ln_ffn_b):
    w = dict(meta=meta, a_w_in=a_w_in, a_conv=a_conv, a_log=a_log, a_dt_bias=a_dt_bias, a_norm=a_norm, a_w_out=a_w_out,
             b_w_in=b_w_in, b_conv=b_conv, b_w_out=b_w_out, ln_mix_g=ln_mix_g, ln_mix_b=ln_mix_b, ffn_w_up=ffn_w_up,
             ffn_conv=ffn_conv, ffn_w_down=ffn_w_down, ln_ffn_g=ln_ffn_g, ln_ffn_b=ln_ffn_b)
    m = dict(meta=m_meta, a_w_in=m_a_w_in, a_conv=m_a_conv, a_log=m_a_log, a_dt_bias=m_a_dt_bias, a_norm=m_a_norm,
             a_w_out=m_a_w_out, b_w_in=m_b_w_in, b_conv=m_b_conv, b_w_out=m_b_w_out, ln_mix_g=m_ln_mix_g,
             ln_mix_b=m_ln_mix_b, ffn_w_up=m_ffn_w_up, ffn_conv=m_ffn_conv, ffn_w_down=m_ffn_w_down,
             ln_ffn_g=m_ln_ffn_g, ln_ffn_b=m_ln_ffn_b)
    v = dict(meta=v_meta, a_w_in=v_a_w_in, a_conv=v_a_conv, a_log=v_a_log, a_dt_bias=v_a_dt_bias, a_norm=v_a_norm,
             a_w_out=v_a_w_out, b_w_in=v_b_w_in, b_conv=v_b_conv, b_w_out=v_b_w_out, ln_mix_g=v_ln_mix_g,
             ln_mix_b=v_ln_mix_b, ffn_w_up=v_ffn_w_up, ffn_conv=v_ffn_conv, ffn_w_down=v_ffn_w_down,
             ln_ffn_g=v_ln_ffn_g, ln_ffn_b=v_ln_ffn_b)
    seq = x.shape[1]
    wt = _gather_weights(w)
    loss_part, dh0, grads = _local_step(x[0], loss_target[0], wt)
    grad_w, loss = _reduce_grads(grads, loss_part, w)
    grad_x = dh0[:, N_META:N_META + seq]
    steps = [adamw(w[n], grad_w[n], m[n], v[n], "adamw_" + n) for n in WEIGHTS]
    return (loss, grad_x, *[grad_w[n] for n in WEIGHTS], *[s[0] for s in steps], *[s[1] for s in steps],
            *[s[2] for s in steps])
```

```python
import functools

import jax
import jax.numpy as jnp
from jax import lax
from jax.experimental import pallas as pl
from jax.experimental.pallas import tpu as pltpu

F32 = jnp.float32
BF16 = jnp.bfloat16
HI = lax.Precision.HIGHEST

N_META = 16
HEADS = 8
HEAD_DIM = 128
CHUNK = 64
GDN_CONV = 4
SC_CONV = 3
FFN_CONV = 3
ALPHA = 4.0 ** 0.25
LN_EPS = 1e-5
RMS_EPS = 1e-6
L2_EPS = 1e-6
Q_SCALE = HEAD_DIM ** -0.5

ADAM_LR = 0.001
ADAM_B1 = 0.9
ADAM_B2 = 0.999
ADAM_EPS = 1e-08
ADAM_WD = 0.01
ADAM_STEP = 10

HALO = 8
VMEM_LIMIT = 48 * 1024 * 1024


def _params(sem=None):
    return pltpu.CompilerParams(dimension_semantics=sem, vmem_limit_bytes=VMEM_LIMIT)


def _dot(a, b, prec=None):
    return jnp.dot(a, b, preferred_element_type=F32, precision=prec)


def _dot_nt(a, b, prec=None):
    return lax.dot_general(a, b, (((1,), (1,)), ((), ())), preferred_element_type=F32, precision=prec)


def _dot_tn(a, b, prec=None):
    return lax.dot_general(a, b, (((0,), (0,)), ((), ())), preferred_element_type=F32, precision=prec)


def _sigmoid(x):
    return 1.0 / (1.0 + jnp.exp(-x))


def _tri_masks():
    r = lax.broadcasted_iota(jnp.int32, (CHUNK, CHUNK), 0)
    c = lax.broadcasted_iota(jnp.int32, (CHUNK, CHUNK), 1)
    return r >= c, r > c, r == c


def _split_hi_lo(x):
    hi = x.astype(BF16)
    return hi, (x - hi.astype(F32)).astype(BF16)


def _mask_dot(mask, x):
    hi, lo = _split_hi_lo(x)
    return _dot(mask, hi) + _dot(mask, lo)


@jax.custom_vjp
def _cumsum_rows(g):
    causal, _, _ = _tri_masks()
    return _mask_dot(causal.astype(BF16), g)


def _cumsum_rows_fwd(g):
    return _cumsum_rows(g), None


def _cumsum_rows_bwd(_, dy):
    _, strict, _ = _tri_masks()
    return (_mask_dot((~strict).astype(BF16), dy),)


_cumsum_rows.defvjp(_cumsum_rows_fwd, _cumsum_rows_bwd)


def _dot_split3(a, b):
    a_hi, a_lo = _split_hi_lo(a)
    b_hi, b_lo = _split_hi_lo(b)
    return _dot(a_hi, b_hi) + (_dot(a_hi, b_lo) + _dot(a_lo, b_hi))


def _gdn_m(ks, g64s, bbs):
    causal, strict, _ = _tri_masks()
    a = [_cumsum_rows(g) for g in g64s]
    decay = [jnp.exp(jnp.where(causal, x - x.T, -1e30)) for x in a]
    kk = [_dot_nt(k * b, k) for k, b in zip(ks, bbs)]
    return [jnp.where(strict, x * d, 0.0) for x, d in zip(kk, decay)]


def _gdn_inverse(ms):
    r = lax.broadcasted_iota(jnp.int32, (CHUNK, CHUNK), 0)
    c = lax.broadcasted_iota(jnp.int32, (CHUNK, CHUNK), 1)
    eye = (r == c).astype(F32)
    same = [jnp.right_shift(r, s) == jnp.right_shift(c, s) for s in (3, 4, 5)]
    d = [jnp.where(same[0], m, 0.0) for m in ms]
    p = [_dot(x, x) for x in d]
    t = [eye - x for x in d]
    t = [x + _dot(x, y) for x, y in zip(t, p)]
    p = [_dot(x, x) for x in p]
    t = [x + _dot(x, y) for x, y in zip(t, p)]
    for inner, outer in ((same[0], same[1]), (same[1], same[2]), (same[2], None)):
        joins = ~inner if outer is None else (outer & ~inner)
        o = [_dot(x, jnp.where(joins, m, 0.0)) for x, m in zip(t, ms)]
        t = [x - _dot(y, x) for x, y in zip(t, o)]
    res = [eye - x - _dot_split3(m, x) for m, x in zip(ms, t)]
    return [x + _dot(x, y) for x, y in zip(t, res)]


def _gdn_apply(qs, ks, vs, gbs, g64s, bbs, ss, ts):
    causal, _, _ = _tri_masks()
    n = range(len(qs))
    gc = [_cumsum_rows(g) for g in gbs]
    a = [_cumsum_rows(g) for g in g64s]
    decay = [jnp.exp(jnp.where(causal, x - x.T, -1e30)) for x in a]
    eg = [jnp.exp(x) for x in gc]
    u = [_dot(ts[h], vs[h] * bbs[h]) for h in n]
    w = [_dot(ts[h], ks[h] * bbs[h] * eg[h]) for h in n]
    qk = [_dot_nt(qs[h], ks[h]) * decay[h] for h in n]
    gl = [jnp.sum(g, axis=0, keepdims=True) for g in gbs]
    kd = [ks[h] * jnp.exp(gl[h] - gc[h]) for h in n]
    v_new = [u[h] - _dot(w[h], ss[h]) for h in n]
    o = [_dot(qs[h] * eg[h], ss[h]) + _dot(qk[h], v_new[h]) for h in n]
    s2 = [ss[h] * jnp.exp(gl[h]) + _dot_tn(kd[h], v_new[h]) for h in n]
    return o, s2


def _head_slices(h):
    return slice(h * HEAD_DIM, (h + 1) * HEAD_DIM), slice(h * HEAD_DIM, h * HEAD_DIM + CHUNK)


def _gdn_head_values(x_ref):
    out = [[], [], [], [], [], []]
    for h in range(HEADS):
        sl, sl64 = _head_slices(h)
        for lst, val in zip(out, (x_ref[0, :, sl], x_ref[1, :, sl], x_ref[2, :, sl], x_ref[4, :, sl],
                                  x_ref[4, :, sl64], x_ref[3, :, sl])):
            lst.append(val)
    return out


def gdn_chunk_fwd(qkvbg):
    _, lp, width = qkvbg.shape
    n_chunks = lp // CHUNK

    def body(x_ref, o_ref, s_ref, t_ref, state):
        @pl.when(pl.program_id(0) == 0)
        def _():
            state[...] = jnp.zeros_like(state)

        qs, ks, vs, gbs, g64s, bbs = _gdn_head_values(x_ref)
        ss = [state[h] for h in range(HEADS)]
        ts = _gdn_inverse(_gdn_m(ks, g64s, bbs))
        os_, s2 = _gdn_apply(qs, ks, vs, gbs, g64s, bbs, ss, ts)
        for h in range(HEADS):
            s_ref[0, h] = ss[h]
            t_ref[0, h] = ts[h]
            o_ref[:, _head_slices(h)[0]] = os_[h]
            state[h] = s2[h]

    return pl.pallas_call(
        body,
        name="gdn_chunk_fwd",
        grid=(n_chunks,),
        in_specs=[pl.BlockSpec((5, CHUNK, width), lambda c: (0, c, 0))],
        out_specs=[
            pl.BlockSpec((CHUNK, width), lambda c: (c, 0)),
            pl.BlockSpec((1, HEADS, HEAD_DIM, HEAD_DIM), lambda c: (c, 0, 0, 0)),
            pl.BlockSpec((1, HEADS, CHUNK, CHUNK), lambda c: (c, 0, 0, 0)),
        ],
        out_shape=[
            jax.ShapeDtypeStruct((lp, width), F32),
            jax.ShapeDtypeStruct((n_chunks, HEADS, HEAD_DIM, HEAD_DIM), F32),
            jax.ShapeDtypeStruct((n_chunks, HEADS, CHUNK, CHUNK), F32),
        ],
        scratch_shapes=[pltpu.VMEM((HEADS, HEAD_DIM, HEAD_DIM), F32)],
        compiler_params=_params(("arbitrary",)),
    )(qkvbg)


def gdn_chunk_bwd(qkvbg, states, tinv, d_o):
    _, lp, width = qkvbg.shape
    n_chunks = lp // CHUNK
    last = n_chunks - 1

    def body(x_ref, s_ref, t_ref, do_ref, dx_ref, dstate):
        @pl.when(pl.program_id(0) == 0)
        def _():
            dstate[...] = jnp.zeros_like(dstate)

        heads = range(HEADS)
        qs, ks, vs, gbs, g64s, bbs = _gdn_head_values(x_ref)
        ss = [s_ref[0, h] for h in heads]
        ts = [t_ref[0, h] for h in heads]
        d_out = ([do_ref[:, _head_slices(h)[0]] for h in heads], [dstate[h] for h in heads])
        _, vjp_apply = jax.vjp(_gdn_apply, qs, ks, vs, gbs, g64s, bbs, ss, ts)
        dq, dk, dv, dgb, dg64, dbb, ds, dt = vjp_apply(d_out)
        tts = [t.T for t in ts]
        dm = [_dot(tts[h], dt[h]) for h in heads]
        dm = [-_dot(dm[h], tts[h]) for h in heads]
        _, vjp_m = jax.vjp(_gdn_m, ks, g64s, bbs)
        dk2, dg64m, dbb2 = vjp_m(dm)
        for h in heads:
            sl, sl64 = _head_slices(h)
            dx_ref[0, :, sl] = dq[h]
            dx_ref[1, :, sl] = dk[h] + dk2[h]
            dx_ref[2, :, sl] = dv[h]
            dx_ref[3, :, sl] = dbb[h] + dbb2[h]
            dx_ref[4, :, sl] = dgb[h]
            dx_ref[4, :, sl64] += dg64[h] + dg64m[h]
            dstate[h] = ds[h]

    return pl.pallas_call(
        body,
        name="gdn_chunk_bwd",
        grid=(n_chunks,),
        in_specs=[
            pl.BlockSpec((5, CHUNK, width), lambda c: (0, last - c, 0)),
            pl.BlockSpec((1, HEADS, HEAD_DIM, HEAD_DIM), lambda c: (last - c, 0, 0, 0)),
            pl.BlockSpec((1, HEADS, CHUNK, CHUNK), lambda c: (last - c, 0, 0, 0)),
            pl.BlockSpec((CHUNK, width), lambda c: (last - c, 0)),
        ],
        out_specs=pl.BlockSpec((5, CHUNK, width), lambda c: (0, last - c, 0)),
        out_shape=jax.ShapeDtypeStruct(qkvbg.shape, F32),
        scratch_shapes=[pltpu.VMEM((HEADS, HEAD_DIM, HEAD_DIM), F32)],
        compiler_params=_params(("arbitrary",)),
    )(qkvbg, states, tinv, d_o)


def mm_nn(a, b, *, tm, name):
    ks, m, tk = a.shape
    _, ns, _, tn = b.shape

    def body(a_ref, b_ref, o_ref):
        p = _dot(a_ref[...].astype(BF16), b_ref[...])

        @pl.when(pl.program_id(2) == 0)
        def _():
            o_ref[...] = p

        @pl.when(pl.program_id(2) > 0)
        def _():
            o_ref[...] += p

    return pl.pallas_call(
        body,
        name=name,
        grid=(ns, m // tm, ks),
        in_specs=[
            pl.BlockSpec((None, tm, tk), lambda n, i, k: (k, i, 0)),
            pl.BlockSpec((None, None, tk, tn), lambda n, i, k: (k, n, 0, 0)),
        ],
        out_specs=pl.BlockSpec((None, tm, tn), lambda n, i, k: (n, i, 0)),
        out_shape=jax.ShapeDtypeStruct((ns, m, tn), F32),
        compiler_params=_params(("arbitrary", "arbitrary", "arbitrary")),
    )(a, b)


def mm_nt(dy, w, *, tm, name, res=None, res_scale=1.0):
    ns, m, tn = dy.shape
    ks, _, tk, _ = w.shape

    def body(*refs):
        if res is None:
            dy_ref, w_ref, o_ref = refs
        else:
            dy_ref, w_ref, r_ref, o_ref = refs
        p = _dot_nt(dy_ref[...].astype(BF16), w_ref[...])

        @pl.when(pl.program_id(2) == 0)
        def _():
            o_ref[...] = p if res is None else p + res_scale * r_ref[...]

        @pl.when(pl.program_id(2) > 0)
        def _():
            o_ref[...] += p

    in_specs = [
        pl.BlockSpec((None, tm, tn), lambda k, i, n: (n, i, 0)),
        pl.BlockSpec((None, None, tk, tn), lambda k, i, n: (k, n, 0, 0)),
    ]
    args = [dy, w]
    if res is not None:
        in_specs.append(pl.BlockSpec((None, tm, tk), lambda k, i, n: (k, i, 0)))
        args.append(res)
    return pl.pallas_call(
        body,
        name=name,
        grid=(ks, m // tm, ns),
        in_specs=in_specs,
        out_specs=pl.BlockSpec((None, tm, tk), lambda k, i, n: (k, i, 0)),
        out_shape=jax.ShapeDtypeStruct((ks, m, tk), F32),
        compiler_params=_params(("arbitrary", "arbitrary", "arbitrary")),
    )(*args)


def mm_tn(x, dy, *, tm, name):
    ks, m, tk = x.shape
    ns, _, tn = dy.shape

    def body(x_ref, dy_ref, o_ref):
        p = _dot_tn(x_ref[...].astype(BF16), dy_ref[...].astype(BF16))

        @pl.when(pl.program_id(2) == 0)
        def _():
            o_ref[...] = p

        @pl.when(pl.program_id(2) > 0)
        def _():
            o_ref[...] += p

    return pl.pallas_call(
        body,
        name=name,
        grid=(ks, ns, m // tm),
        in_specs=[
            pl.BlockSpec((None, tm, tk), lambda k, n, i: (k, i, 0)),
            pl.BlockSpec((None, tm, tn), lambda k, n, i: (n, i, 0)),
        ],
        out_specs=pl.BlockSpec((None, None, tk, tn), lambda k, n, i: (k, n, 0, 0)),
        out_shape=jax.ShapeDtypeStruct((ks, ns, tk, tn), F32),
        compiler_params=_params(("arbitrary", "arbitrary", "arbitrary")),
    )(x, dy)


def _row_partial(x):
    rows, c = x.shape
    return jnp.sum(x.reshape(rows // 8, 8, c), axis=0)


def ln_fwd(h_prev, mix, g, b, *, tm, name):
    _, lp, d = h_prev.shape

    def body(h_ref, m_ref, g_ref, b_ref, r_ref, o_ref):
        r = ALPHA * h_ref[...] + m_ref[...]
        mu = jnp.mean(r, axis=-1, keepdims=True)
        xc = r - mu
        var = jnp.mean(xc * xc, axis=-1, keepdims=True)
        r_ref[...] = r
        o_ref[...] = xc * lax.rsqrt(var + LN_EPS) * g_ref[...] + b_ref[...]

    row = pl.BlockSpec((None, tm, d), lambda i: (0, i, 0))
    vec = pl.BlockSpec((1, d), lambda i: (0, 0))
    return pl.pallas_call(
        body,
        name=name,
        grid=(lp // tm,),
        in_specs=[row, row, vec, vec],
        out_specs=[row, row],
        out_shape=[jax.ShapeDtypeStruct((1, lp, d), F32)] * 2,
        compiler_params=_params(("arbitrary",)),
    )(h_prev, mix, g, b)


def ln_bwd(r, dh, g, *, tm, name):
    _, lp, d = r.shape

    def body(r_ref, dh_ref, g_ref, dr_ref, dgb_ref):
        x = r_ref[...]
        dh_v = dh_ref[...]
        mu = jnp.mean(x, axis=-1, keepdims=True)
        xc = x - mu
        rstd = lax.rsqrt(jnp.mean(xc * xc, axis=-1, keepdims=True) + LN_EPS)
        xh = xc * rstd
        dxh = dh_v * g_ref[...]
        m1 = jnp.mean(dxh, axis=-1, keepdims=True)
        m2 = jnp.mean(dxh * xh, axis=-1, keepdims=True)
        dr_ref[...] = rstd * (dxh - m1 - xh * m2)

        @pl.when(pl.program_id(0) == 0)
        def _():
            dgb_ref[...] = jnp.zeros_like(dgb_ref)

        dgb_ref[0] += _row_partial(dh_v * xh)
        dgb_ref[1] += _row_partial(dh_v)

    row = pl.BlockSpec((None, tm, d), lambda i: (0, i, 0))
    return pl.pallas_call(
        body,
        name=name,
        grid=(lp // tm,),
        in_specs=[row, row, pl.BlockSpec((1, d), lambda i: (0, 0))],
        out_specs=[row, pl.BlockSpec((2, 8, d), lambda i: (0, 0, 0))],
        out_shape=[jax.ShapeDtypeStruct((1, lp, d), F32), jax.ShapeDtypeStruct((2, 8, d), F32)],
        compiler_params=_params(("arbitrary",)),
    )(r, dh, g)


def loss_grad(h, target, *, first, count, tm):
    _, lp, d = h.shape

    def body(h_ref, t_ref, dh_ref, l_ref):
        row = pl.program_id(0) * tm + lax.broadcasted_iota(jnp.int32, (tm, d), 0)
        valid = (row >= first) & (row < first + count)
        err = jnp.where(valid, h_ref[...] - t_ref[...], 0.0)
        dh_ref[...] = err * (1.0 / d)

        @pl.when(pl.program_id(0) == 0)
        def _():
            l_ref[...] = jnp.zeros_like(l_ref)

        l_ref[...] += _row_partial(err * err) * (0.5 / d)

    return pl.pallas_call(
        body,
        name="loss_grad",
        grid=(lp // tm,),
        in_specs=[pl.BlockSpec((None, tm, d), lambda i: (0, i, 0)), pl.BlockSpec((tm, d), lambda i: (i, 0))],
        out_specs=[pl.BlockSpec((None, tm, d), lambda i: (0, i, 0)), pl.BlockSpec((8, d), lambda i: (0, 0))],
        out_shape=[jax.ShapeDtypeStruct((1, lp, d), F32), jax.ShapeDtypeStruct((8, d), F32)],
        compiler_params=_params(("arbitrary",)),
    )(h, target)


def _halo_index(tile, tm):
    return jnp.maximum(tile * (tm // HALO) - 1, 0)


def _conv_fwd(xs_ref, w, taps, tm):
    acc = w(0) * xs_ref[pl.ds(HALO - taps + 1, tm), :]
    for j in range(1, taps):
        acc += w(j) * xs_ref[pl.ds(HALO - taps + 1 + j, tm), :]
    return acc


def _conv_bwd_x(dcs_ref, w, taps, tm):
    acc = w(0) * dcs_ref[pl.ds(taps - 1, tm), :]
    for j in range(1, taps):
        acc += w(j) * dcs_ref[pl.ds(taps - 1 - j, tm), :]
    return acc


def _conv_bwd_w(dc, xs_ref, taps, tm):
    return [jnp.sum(dc * xs_ref[pl.ds(HALO - taps + 1 + j, tm), :], axis=0, keepdims=True) for j in range(taps)]


def _silu_parts(c):
    sg = _sigmoid(c)
    return c * sg, sg * (1.0 + c * (1.0 - sg))


def _head_sum(x):
    rows, c = x.shape
    parts = []
    for h in range(c // HEAD_DIM):
        s = jnp.sum(x[:, h * HEAD_DIM:(h + 1) * HEAD_DIM], axis=-1, keepdims=True)
        parts.append(jnp.broadcast_to(s, (rows, HEAD_DIM)))
    return parts[0] if len(parts) == 1 else jnp.concatenate(parts, axis=-1)


def _log1p(y):
    u = 1.0 + y
    d = u - 1.0
    return jnp.where(d == 0.0, y, jnp.log(u) * (y / jnp.where(d == 0.0, 1.0, d)))


def _softplus(x):
    return jnp.maximum(x, 0.0) + _log1p(jnp.exp(-jnp.abs(x)))


def gdn_pre_fwd(p5, conv_w, alog_b, dtb_b, *, tm, cb):
    _, lp, width = p5.shape
    taps = conv_w.shape[1]

    def body(x_ref, halo_ref, w_ref, al_ref, dt_ref, o_ref, xs):
        i = pl.program_id(1)
        for s in range(3):
            xs[s, 0:HALO, :] = jnp.where(i > 0, halo_ref[s], 0.0)
            xs[s, HALO:, :] = x_ref[s]
            c = _conv_fwd(xs.at[s], lambda j, s=s: w_ref[s, j:j + 1, :], taps, tm)
            y, _ = _silu_parts(c)
            if s < 2:
                y = y * lax.rsqrt(_head_sum(y * y) + L2_EPS)
                if s == 0:
                    y = y * Q_SCALE
            o_ref[s] = y
        o_ref[3] = _sigmoid(x_ref[3])
        o_ref[4] = -jnp.exp(al_ref[...]) * _softplus(x_ref[4] + dt_ref[...])

    return pl.pallas_call(
        body,
        name="gdn_pre_fwd",
        grid=(width // cb, lp // tm),
        in_specs=[
            pl.BlockSpec((5, tm, cb), lambda j, i: (0, i, j)),
            pl.BlockSpec((3, HALO, cb), lambda j, i: (0, _halo_index(i, tm), j)),
            pl.BlockSpec((3, taps, cb), lambda j, i: (0, 0, j)),
            pl.BlockSpec((1, cb), lambda j, i: (0, j)),
            pl.BlockSpec((1, cb), lambda j, i: (0, j)),
        ],
        out_specs=pl.BlockSpec((5, tm, cb), lambda j, i: (0, i, j)),
        out_shape=jax.ShapeDtypeStruct((5, lp, width), F32),
        scratch_shapes=[pltpu.VMEM((3, tm + HALO, cb), F32)],
        compiler_params=_params(("arbitrary", "arbitrary")),
    )(p5, p5, conv_w, alog_b, dtb_b)


def gdn_pre_bwd(p5, dqkvbg, conv_w, alog_b, dtb_b, *, tm, cb):
    _, lp, width = p5.shape
    taps = conv_w.shape[1]
    last = lp // tm - 1

    def body(x_ref, halo_ref, d_ref, w_ref, al_ref, dt_ref, dx_ref, dw_ref, dsc_ref, xs, dcs, carry):
        step = pl.program_id(1)
        tile = last - step

        @pl.when(step == 0)
        def _():
            carry[...] = jnp.zeros_like(carry)
            dw_ref[...] = jnp.zeros_like(dw_ref)
            dsc_ref[...] = jnp.zeros_like(dsc_ref)

        for s in range(3):
            w = lambda j, s=s: w_ref[s, j:j + 1, :]
            xs[s, 0:HALO, :] = jnp.where(tile > 0, halo_ref[s], 0.0)
            xs[s, HALO:, :] = x_ref[s]
            c = _conv_fwd(xs.at[s], w, taps, tm)
            y, dsilu = _silu_parts(c)
            dy = d_ref[s]
            if s < 2:
                rn = lax.rsqrt(_head_sum(y * y) + L2_EPS)
                yn = y * rn
                if s == 0:
                    dy = dy * Q_SCALE
                dy = rn * (dy - yn * _head_sum(dy * yn))
            dc = dy * dsilu
            dcs[s, 0:tm, :] = dc
            dcs[s, tm:, :] = carry[s]
            dx_ref[s] = _conv_bwd_x(dcs.at[s], w, taps, tm)
            carry[s] = dc[0:HALO, :]
            for j, row in enumerate(_conv_bwd_w(dc, xs.at[s], taps, tm)):
                dw_ref[s, j:j + 1, :] += row
        beta = _sigmoid(x_ref[3])
        dx_ref[3] = d_ref[3] * beta * (1.0 - beta)
        z = x_ref[4] + dt_ref[...]
        neg_ea = -jnp.exp(al_ref[...])
        da = d_ref[4] * neg_ea * _sigmoid(z)
        dx_ref[4] = da
        dsc_ref[0] += _row_partial(d_ref[4] * neg_ea * _softplus(z))
        dsc_ref[1] += _row_partial(da)

    tile_spec = pl.BlockSpec((5, tm, cb), lambda j, i: (0, last - i, j))
    return pl.pallas_call(
        body,
        name="gdn_pre_bwd",
        grid=(width // cb, lp // tm),
        in_specs=[
            tile_spec,
            pl.BlockSpec((3, HALO, cb), lambda j, i: (0, _halo_index(last - i, tm), j)),
            tile_spec,
            pl.BlockSpec((3, taps, cb), lambda j, i: (0, 0, j)),
            pl.BlockSpec((1, cb), lambda j, i: (0, j)),
            pl.BlockSpec((1, cb), lambda j, i: (0, j)),
        ],
        out_specs=[
            tile_spec,
            pl.BlockSpec((3, taps, cb), lambda j, i: (0, 0, j)),
            pl.BlockSpec((2, 8, cb), lambda j, i: (0, 0, j)),
        ],
        out_shape=[
            jax.ShapeDtypeStruct((5, lp, width), F32),
            jax.ShapeDtypeStruct((3, taps, width), F32),
            jax.ShapeDtypeStruct((2, 8, width), F32),
        ],
        scratch_shapes=[
            pltpu.VMEM((3, tm + HALO, cb), F32),
            pltpu.VMEM((3, tm + HALO, cb), F32),
            pltpu.VMEM((3, HALO, cb), F32),
        ],
        compiler_params=_params(("arbitrary", "arbitrary")),
    )(p5, p5, dqkvbg, conv_w, alog_b, dtb_b)


def gdn_post_fwd(o, z, nw_b, *, tm):
    _, lp, width = o.shape

    def body(o_ref, z_ref, nw_ref, y_ref):
        ov = o_ref[...]
        rn = lax.rsqrt(_head_sum(ov * ov) * (1.0 / HEAD_DIM) + RMS_EPS)
        gate, _ = _silu_parts(z_ref[...])
        y_ref[...] = ov * rn * nw_ref[...] * gate

    row = pl.BlockSpec((None, tm, width), lambda i: (0, i, 0))
    return pl.pallas_call(
        body,
        name="gdn_post_fwd",
        grid=(lp // tm,),
        in_specs=[row, row, pl.BlockSpec((1, width), lambda i: (0, 0))],
        out_specs=row,
        out_shape=jax.ShapeDtypeStruct((1, lp, width), F32),
        compiler_params=_params(("arbitrary",)),
    )(o, z, nw_b)


def gdn_post_bwd(o, z, dy, nw_b, *, tm):
    _, lp, width = o.shape

    def body(o_ref, z_ref, dy_ref, nw_ref, do_ref, dz_ref, dnw_ref):
        ov = o_ref[...]
        rn = lax.rsqrt(_head_sum(ov * ov) * (1.0 / HEAD_DIM) + RMS_EPS)
        yn = ov * rn
        gate, dgate = _silu_parts(z_ref[...])
        d_on = dy_ref[...] * gate
        dz_ref[...] = dy_ref[...] * yn * nw_ref[...] * dgate
        a = d_on * nw_ref[...]
        do_ref[...] = rn * (a - yn * (_head_sum(a * yn) * (1.0 / HEAD_DIM)))

        @pl.when(pl.program_id(0) == 0)
        def _():
            dnw_ref[...] = jnp.zeros_like(dnw_ref)

        dnw_ref[...] += _row_partial(d_on * yn)

    row = pl.BlockSpec((None, tm, width), lambda i: (0, i, 0))
    return pl.pallas_call(
        body,
        name="gdn_post_bwd",
        grid=(lp // tm,),
        in_specs=[row, row, row, pl.BlockSpec((1, width), lambda i: (0, 0))],
        out_specs=[row, row, pl.BlockSpec((8, width), lambda i: (0, 0))],
        out_shape=[jax.ShapeDtypeStruct((1, lp, width), F32)] * 2 + [jax.ShapeDtypeStruct((8, width), F32)],
        compiler_params=_params(("arbitrary",)),
    )(o, z, dy, nw_b)


def head_lane_sum(x):
    s_n, rows, width = x.shape

    def body(x_ref, o_ref):
        lane = lax.broadcasted_iota(jnp.int32, (rows, HEAD_DIM), 1)
        acc = jnp.zeros((rows, HEAD_DIM), F32)
        for h in range(width // HEAD_DIM):
            s = jnp.sum(x_ref[:, h * HEAD_DIM:(h + 1) * HEAD_DIM], axis=-1, keepdims=True)
            acc = jnp.where(lane == h, s, acc)
        o_ref[...] = acc

    return pl.pallas_call(
        body,
        name="head_lane_sum",
        grid=(s_n,),
        in_specs=[pl.BlockSpec((None, rows, width), lambda s: (s, 0, 0))],
        out_specs=pl.BlockSpec((None, rows, HEAD_DIM), lambda s: (s, 0, 0)),
        out_shape=jax.ShapeDtypeStruct((s_n, rows, HEAD_DIM), F32),
        compiler_params=_params(("arbitrary",)),
    )(x)


def ffn_act_fwd(up, conv_w, *, tm, name):
    _, lp, c_w = up.shape
    taps = conv_w.shape[1]

    def body(u_ref, halo_ref, g_ref, w_ref, o_ref, xs):
        i = pl.program_id(1)
        xs[0:HALO, :] = jnp.where(i > 0, halo_ref[...], 0.0)
        xs[HALO:, :] = u_ref[...]
        y, _ = _silu_parts(_conv_fwd(xs, lambda j: w_ref[j:j + 1, :], taps, tm))
        o_ref[...] = y * g_ref[...]

    return pl.pallas_call(
        body,
        name=name,
        grid=(2, lp // tm),
        in_specs=[
            pl.BlockSpec((None, tm, c_w), lambda s, i: (s, i, 0)),
            pl.BlockSpec((None, HALO, c_w), lambda s, i: (s, _halo_index(i, tm), 0)),
            pl.BlockSpec((None, tm, c_w), lambda s, i: (2 + s, i, 0)),
            pl.BlockSpec((None, taps, c_w), lambda s, i: (s, 0, 0)),
        ],
        out_specs=pl.BlockSpec((None, tm, c_w), lambda s, i: (s, i, 0)),
        out_shape=jax.ShapeDtypeStruct((2, lp, c_w), F32),
        scratch_shapes=[pltpu.VMEM((tm + HALO, c_w), F32)],
        compiler_params=_params(("arbitrary", "arbitrary")),
    )(up, up, up, conv_w)


def ffn_act_bwd(up, dact, conv_w, *, tm, name):
    _, lp, c_w = up.shape
    taps = conv_w.shape[1]
    last = lp // tm - 1

    def body(u_ref, halo_ref, g_ref, d_ref, w_ref, dup_ref, dw_ref, xs, dcs, carry):
        step = pl.program_id(1)
        tile = last - step
        w = lambda j: w_ref[j:j + 1, :]

        @pl.when(step == 0)
        def _():
            carry[...] = jnp.zeros_like(carry)
            dw_ref[...] = jnp.zeros_like(dw_ref)

        xs[0:HALO, :] = jnp.where(tile > 0, halo_ref[...], 0.0)
        xs[HALO:, :] = u_ref[...]
        y, dsilu = _silu_parts(_conv_fwd(xs, w, taps, tm))
        dup_ref[1] = d_ref[...] * y
        dc = d_ref[...] * g_ref[...] * dsilu
        dcs[0:tm, :] = dc
        dcs[tm:, :] = carry[...]
        dup_ref[0] = _conv_bwd_x(dcs, w, taps, tm)
        carry[...] = dc[0:HALO, :]
        for j, row in enumerate(_conv_bwd_w(dc, xs, taps, tm)):
            dw_ref[j:j + 1, :] += row

    return pl.pallas_call(
        body,
        name=name,
        grid=(2, lp // tm),
        in_specs=[
            pl.BlockSpec((None, tm, c_w), lambda s, i: (s, last - i, 0)),
            pl.BlockSpec((None, HALO, c_w), lambda s, i: (s, _halo_index(last - i, tm), 0)),
            pl.BlockSpec((None, tm, c_w), lambda s, i: (2 + s, last - i, 0)),
            pl.BlockSpec((None, tm, c_w), lambda s, i: (s, last - i, 0)),
            pl.BlockSpec((None, taps, c_w), lambda s, i: (s, 0, 0)),
        ],
        out_specs=[
            pl.BlockSpec((2, None, tm, c_w), lambda s, i: (0, s, last - i, 0)),
            pl.BlockSpec((None, taps, c_w), lambda s, i: (s, 0, 0)),
        ],
        out_shape=[jax.ShapeDtypeStruct((2, 2, lp, c_w), F32), jax.ShapeDtypeStruct((2, taps, c_w), F32)],
        scratch_shapes=[
            pltpu.VMEM((tm + HALO, c_w), F32),
            pltpu.VMEM((tm + HALO, c_w), F32),
            pltpu.VMEM((HALO, c_w), F32),
        ],
        compiler_params=_params(("arbitrary", "arbitrary")),
    )(up, up, up, dact, conv_w)


def sc_fwd(pb, conv_w, *, tm, cb):
    _, lp, width = pb.shape
    taps = conv_w.shape[0]

    def body(x_ref, halo_ref, w_ref, o_ref, xs):
        i = pl.program_id(1)
        xs[0:HALO, :] = jnp.where(i > 0, halo_ref[1] * halo_ref[2], 0.0)
        xs[HALO:, :] = x_ref[1] * x_ref[2]
        o_ref[...] = x_ref[0] * _conv_fwd(xs, lambda j: w_ref[j:j + 1, :], taps, tm)

    return pl.pallas_call(
        body,
        name="sc_fwd",
        grid=(width // cb, lp // tm),
        in_specs=[
            pl.BlockSpec((3, tm, cb), lambda j, i: (0, i, j)),
            pl.BlockSpec((3, HALO, cb), lambda j, i: (0, _halo_index(i, tm), j)),
            pl.BlockSpec((taps, cb), lambda j, i: (0, j)),
        ],
        out_specs=pl.BlockSpec((None, tm, cb), lambda j, i: (0, i, j)),
        out_shape=jax.ShapeDtypeStruct((1, lp, width), F32),
        scratch_shapes=[pltpu.VMEM((tm + HALO, cb), F32)],
        compiler_params=_params(("arbitrary", "arbitrary")),
    )(pb, pb, conv_w)


def sc_bwd(pb, ds, conv_w, *, tm, cb):
    _, lp, width = pb.shape
    taps = conv_w.shape[0]
    last = lp // tm - 1

    def body(x_ref, halo_ref, d_ref, w_ref, dx_ref, dw_ref, xs, dcs, carry):
        step = pl.program_id(1)
        tile = last - step
        w = lambda j: w_ref[j:j + 1, :]

        @pl.when(step == 0)
        def _():
            carry[...] = jnp.zeros_like(carry)
            dw_ref[...] = jnp.zeros_like(dw_ref)

        xs[0:HALO, :] = jnp.where(tile > 0, halo_ref[1] * halo_ref[2], 0.0)
        xs[HALO:, :] = x_ref[1] * x_ref[2]
        dx_ref[0] = d_ref[...] * _conv_fwd(xs, w, taps, tm)
        dc = d_ref[...] * x_ref[0]
        dcs[0:tm, :] = dc
        dcs[tm:, :] = carry[...]
        dp = _conv_bwd_x(dcs, w, taps, tm)
        dx_ref[1] = dp * x_ref[2]
        dx_ref[2] = dp * x_ref[1]
        carry[...] = dc[0:HALO, :]
        for j, row in enumerate(_conv_bwd_w(dc, xs, taps, tm)):
            dw_ref[j:j + 1, :] += row

    tile_spec = pl.BlockSpec((3, tm, cb), lambda j, i: (0, last - i, j))
    return pl.pallas_call(
        body,
        name="sc_bwd",
        grid=(width // cb, lp // tm),
        in_specs=[
            tile_spec,
            pl.BlockSpec((3, HALO, cb), lambda j, i: (0, _halo_index(last - i, tm), j)),
            pl.BlockSpec((None, tm, cb), lambda j, i: (0, last - i, j)),
            pl.BlockSpec((taps, cb), lambda j, i: (0, j)),
        ],
        out_specs=[tile_spec, pl.BlockSpec((taps, cb), lambda j, i: (0, j))],
        out_shape=[jax.ShapeDtypeStruct((3, lp, width), F32), jax.ShapeDtypeStruct((taps, width), F32)],
        scratch_shapes=[
            pltpu.VMEM((tm + HALO, cb), F32),
            pltpu.VMEM((tm + HALO, cb), F32),
            pltpu.VMEM((HALO, cb), F32),
        ],
        compiler_params=_params(("arbitrary", "arbitrary")),
    )(pb, pb, ds, conv_w)


TILE_BYTES = 1536 * 1024


def _rows_tile(rows, cols, multiple=8):
    if rows * cols * 4 <= TILE_BYTES or rows % multiple:
        return rows
    best = multiple
    for t in range(multiple, rows + 1, multiple):
        if rows % t == 0 and t * cols * 4 <= TILE_BYTES:
            best = t
    return best


def pair_sum(g, landed, core, out_dtype, name):
    _, rows, cols = g.shape
    half = rows // 2
    tr = _rows_tile(half, cols, 16)
    nb = half // tr

    def body(c_ref, g_ref, l_ref, o_ref):
        o_ref[...] = (g_ref[...] + l_ref[...]).astype(out_dtype)

    return pl.pallas_call(
        body,
        name=name,
        grid_spec=pltpu.PrefetchScalarGridSpec(
            num_scalar_prefetch=1,
            grid=(4, nb),
            in_specs=[
                pl.BlockSpec((None, tr, cols), lambda s, i, c: (s, c[0] * nb + i, 0)),
                pl.BlockSpec((None, tr, cols), lambda s, i, c: (s, i, 0)),
            ],
            out_specs=pl.BlockSpec((None, tr, cols), lambda s, i, c: (s, i, 0)),
        ),
        out_shape=jax.ShapeDtypeStruct((4, half, cols), out_dtype),
        compiler_params=_params(("arbitrary", "arbitrary")),
    )(core, g, landed)


def chip_sum(x, name):
    _, rows, cols = x.shape
    tr = _rows_tile(rows, cols, 16)

    def body(x0, x1, x2, x3, o_ref):
        acc = x0[...].astype(F32) + x1[...].astype(F32)
        o_ref[...] = (acc + x2[...].astype(F32)) + x3[...].astype(F32)

    return pl.pallas_call(
        body,
        name=name,
        grid=(rows // tr,),
        in_specs=[pl.BlockSpec((None, tr, cols), lambda i, k=k: (k, i, 0)) for k in range(4)],
        out_specs=pl.BlockSpec((tr, cols), lambda i: (i, 0)),
        out_shape=jax.ShapeDtypeStruct((rows, cols), F32),
        compiler_params=_params(("arbitrary",)),
    )(x, x, x, x)


def adamw(w, g, m, v, name):
    shape = w.shape
    cols = shape[-1]
    rows = w.size // cols
    tr = _rows_tile(rows, cols)

    def body(w_ref, g_ref, m_ref, v_ref, d_ref, m2_ref, v2_ref):
        gv = g_ref[...]
        m2 = ADAM_B1 * m_ref[...] + (1.0 - ADAM_B1) * gv
        v2 = ADAM_B2 * v_ref[...] + (1.0 - ADAM_B2) * (gv * gv)
        m_hat = m2 / (1.0 - ADAM_B1 ** ADAM_STEP)
        v_hat = v2 / (1.0 - ADAM_B2 ** ADAM_STEP)
        d_ref[...] = -ADAM_LR * (m_hat / (jnp.sqrt(v_hat) + ADAM_EPS) + ADAM_WD * w_ref[...])
        m2_ref[...] = m2
        v2_ref[...] = v2

    spec = pl.BlockSpec((tr, cols), lambda i: (i, 0))
    outs = pl.pallas_call(
        body,
        name=name,
        grid=(rows // tr,),
        in_specs=[spec] * 4,
        out_specs=[spec] * 3,
        out_shape=[jax.ShapeDtypeStruct((rows, cols), F32)] * 3,
        compiler_params=_params(("arbitrary",)),
    )(*[t.reshape(rows, cols) for t in (w, g, m, v)])
    return tuple(o.reshape(shape) for o in outs)


MESH_ID = pl.DeviceIdType.MESH
ANY = pl.BlockSpec(memory_space=pl.ANY)


def _place():
    x, y, c = lax.axis_index("x"), lax.axis_index("y"), lax.axis_index("c")
    other_chips = [(1 - x, y), (x, 1 - y), (1 - x, 1 - y)]
    return x, y, c, other_chips


def all_gather_shards(bufs, name):
    n = len(bufs)

    def body(*refs):
        x_refs, o_refs = refs[:n], refs[n:2 * n]
        send_sems, recv_sems, local_sems = refs[2 * n:]
        x, y, c, chips = _place()
        me = 2 * x + y
        sibling = (x, y, 1 - c)

        def part(a, slot, hf):
            half = bufs[a].shape[0] // 2
            return o_refs[a].at[slot, pl.ds(hf * half, half), :]

        def mine(a):
            half = bufs[a].shape[0] // 2
            return x_refs[a].at[pl.ds(c * half, half), :]

        def copy(k, src, dst, to):
            return pltpu.make_async_remote_copy(src_ref=src, dst_ref=dst, send_sem=send_sems.at[k],
                                                recv_sem=recv_sems.at[k], device_id=to, device_id_type=MESH_ID)

        local = [pltpu.make_async_copy(x_refs[a], o_refs[a].at[me], local_sems.at[a]) for a in range(n)]
        for cp in local:
            cp.start()
        sent = [copy(6 * a + j, mine(a), part(a, me, c), (px, py, c)) for a in range(n) for j, (px, py) in enumerate(chips)]
        for cp in sent:
            cp.start()
        for a in range(n):
            for j, (px, py) in enumerate(chips):
                landed = part(a, 2 * px + py, c)
                copy(6 * a + j, mine(a), landed, (px, py, c)).wait_recv()
                passed = copy(6 * a + 3 + j, landed, landed, sibling)
                passed.start()
                sent.append(passed)
        for a in range(n):
            for j, (px, py) in enumerate(chips):
                theirs = part(a, 2 * px + py, 1 - c)
                copy(6 * a + 3 + j, theirs, theirs, sibling).wait_recv()
        for cp in sent:
            cp.wait_send()
        for cp in local:
            cp.wait()

    return pl.pallas_call(
        body,
        name=name,
        in_specs=[ANY] * n,
        out_specs=[ANY] * n,
        out_shape=[jax.ShapeDtypeStruct((4,) + b.shape, b.dtype) for b in bufs],
        scratch_shapes=[pltpu.SemaphoreType.DMA((6 * n,)), pltpu.SemaphoreType.DMA((6 * n,)), pltpu.SemaphoreType.DMA((n,))],
    )(*bufs)


def swap_halves(bufs, name):
    n = len(bufs)

    def body(*refs):
        x_refs, o_refs = refs[:n], refs[n:2 * n]
        send_sems, recv_sems = refs[2 * n:]
        x, y, c, _ = _place()
        copies = []
        for a in range(n):
            half = bufs[a].shape[1] // 2
            cp = pltpu.make_async_remote_copy(src_ref=x_refs[a].at[:, pl.ds((1 - c) * half, half), :], dst_ref=o_refs[a],
                                              send_sem=send_sems.at[a], recv_sem=recv_sems.at[a],
                                              device_id=(x, y, 1 - c), device_id_type=MESH_ID)
            cp.start()
            copies.append(cp)
        for cp in copies:
            cp.wait()

    return pl.pallas_call(
        body,
        name=name,
        in_specs=[ANY] * n,
        out_specs=[ANY] * n,
        out_shape=[jax.ShapeDtypeStruct((4, b.shape[1] // 2, b.shape[2]), b.dtype) for b in bufs],
        scratch_shapes=[pltpu.SemaphoreType.DMA((n,)), pltpu.SemaphoreType.DMA((n,))],
    )(*bufs)


def scatter_to_chips(bufs, name):
    n = len(bufs)

    def body(*refs):
        x_refs, o_refs = refs[:n], refs[n:2 * n]
        send_sems, recv_sems, local_sems = refs[2 * n:]
        x, y, c, chips = _place()
        me = 2 * x + y
        local = [pltpu.make_async_copy(x_refs[a].at[me], o_refs[a].at[me], local_sems.at[a]) for a in range(n)]
        for cp in local:
            cp.start()

        def copy(a, j, src_slot, dst_slot, px, py):
            return pltpu.make_async_remote_copy(src_ref=x_refs[a].at[src_slot], dst_ref=o_refs[a].at[dst_slot],
                                                send_sem=send_sems.at[3 * a + j], recv_sem=recv_sems.at[3 * a + j],
                                                device_id=(px, py, c), device_id_type=MESH_ID)

        sent = [copy(a, j, 2 * px + py, me, px, py) for a in range(n) for j, (px, py) in enumerate(chips)]
        for cp in sent:
            cp.start()
        for a in range(n):
            for j, (px, py) in enumerate(chips):
                copy(a, j, me, 2 * px + py, px, py).wait_recv()
        for cp in sent:
            cp.wait_send()
        for cp in local:
            cp.wait()

    return pl.pallas_call(
        body,
        name=name,
        in_specs=[ANY] * n,
        out_specs=[ANY] * n,
        out_shape=[jax.ShapeDtypeStruct(b.shape, b.dtype) for b in bufs],
        scratch_shapes=[pltpu.SemaphoreType.DMA((3 * n,)), pltpu.SemaphoreType.DMA((3 * n,)), pltpu.SemaphoreType.DMA((n,))],
    )(*bufs)


def share_halves(groups, name):
    bufs = [b for grp in groups for b in grp]
    where = [(gi, li) for gi, grp in enumerate(groups) for li in range(len(grp))]
    n = len(bufs)

    def body(*refs):
        x_refs, o_refs = refs[:n], refs[n:n + len(groups)]
        send_sems, recv_sems, local_sems = refs[n + len(groups):]
        x, y, c, _ = _place()
        sent, local, arrive = [], [], []
        for a, (gi, li) in enumerate(where):
            half = bufs[a].shape[0]
            mine = o_refs[gi].at[li, pl.ds(c * half, half), :]
            theirs = o_refs[gi].at[li, pl.ds((1 - c) * half, half), :]
            local.append(pltpu.make_async_copy(x_refs[a], mine, local_sems.at[a]))

            def copy(dst, a=a):
                return pltpu.make_async_remote_copy(src_ref=x_refs[a], dst_ref=dst, send_sem=send_sems.at[a],
                                                    recv_sem=recv_sems.at[a], device_id=(x, y, 1 - c),
                                                    device_id_type=MESH_ID)

            sent.append(copy(mine))
            arrive.append(copy(theirs))
        for cp in local + sent:
            cp.start()
        for cp in arrive:
            cp.wait_recv()
        for cp in sent:
            cp.wait_send()
        for cp in local:
            cp.wait()

    return pl.pallas_call(
        body,
        name=name,
        in_specs=[ANY] * n,
        out_specs=[ANY] * len(groups),
        out_shape=[jax.ShapeDtypeStruct((len(grp), 2 * grp[0].shape[0], grp[0].shape[1]), grp[0].dtype) for grp in groups],
        scratch_shapes=[pltpu.SemaphoreType.DMA((n,)), pltpu.SemaphoreType.DMA((n,)), pltpu.SemaphoreType.DMA((n,))],
    )(*bufs)


def reduce_scatter_grads(groups, travel_dtypes):
    bufs = [b for grp in groups for b in grp]
    dtypes = [dt for grp, dt in zip(groups, travel_dtypes) for _ in grp]
    core = lax.axis_index("c").astype(jnp.int32).reshape(1)
    landed = swap_halves(bufs, "rs_pair")
    sums = [pair_sum(b, l, core, dt, "rs_pair_sum%d" % i) for i, (b, l, dt) in enumerate(zip(bufs, landed, dtypes))]
    from_chips = scatter_to_chips(sums, "rs_chips")
    totals = [chip_sum(t, "rs_chip_sum%d" % i) for i, t in enumerate(from_chips)]
    it = iter(totals)
    return share_halves([[next(it) for _ in grp] for grp in groups], "rs_share")


def _row_tiles(length):
    return (640, 320) if length > 2048 else (128, 64)


def _local_step(x, target, wt):
    seq, d = x.shape
    length = N_META + seq
    tm, tm_ffn = _row_tiles(length)
    lp = -(-length // tm) * tm
    tail = jnp.zeros((lp - length, d), F32)
    h0 = jnp.concatenate([wt["meta"], x, tail], axis=0)[None]
    tgt = jnp.concatenate([jnp.zeros((N_META, d), F32), target, tail], axis=0)
    nn = functools.partial(mm_nn, tm=tm)
    nt = functools.partial(mm_nt, tm=tm)
    tn = functools.partial(mm_tn, tm=tm)
    ln_g = [wt["ln_mix_g"][0:1], wt["ln_ffn_g"][0:1], wt["ln_mix_g"][1:2], wt["ln_ffn_g"][1:2]]
    ln_b = [wt["ln_mix_b"][0:1], wt["ln_ffn_b"][0:1], wt["ln_mix_b"][1:2], wt["ln_ffn_b"][1:2]]

    p5 = nn(h0, wt["a5"], name="a_in5")
    pz = nn(h0, wt["az"], name="a_inz")
    qkvbg = gdn_pre_fwd(p5, wt["a_conv3"], wt["alog_b"], wt["dtb_b"], tm=tm, cb=2 * HEAD_DIM)
    o, states, tinv = gdn_chunk_fwd(qkvbg)
    onz = gdn_post_fwd(o[None], pz, wt["anorm_b"], tm=tm)
    r1, h1 = ln_fwd(h0, nn(onz, wt["a_out"], name="a_out"), ln_g[0], ln_b[0], tm=tm, name="ln1")
    up0 = nn(h1, wt["up"][0], name="up0")
    act0 = ffn_act_fwd(up0, wt["fconv"][0], tm=tm_ffn, name="ffn_act0")
    r2, h2 = ln_fwd(h1, nn(act0, wt["down"][0], name="down0"), ln_g[1], ln_b[1], tm=tm, name="ln2")
    pb = nn(h2, wt["b_in"], name="b_in")
    sc = sc_fwd(pb, wt["b_conv"], tm=tm, cb=4 * HEAD_DIM)
    r3, h3 = ln_fwd(h2, nn(sc, wt["b_out"], name="b_out"), ln_g[2], ln_b[2], tm=tm, name="ln3")
    up1 = nn(h3, wt["up"][1], name="up1")
    act1 = ffn_act_fwd(up1, wt["fconv"][1], tm=tm_ffn, name="ffn_act1")
    r4, h4 = ln_fwd(h3, nn(act1, wt["down"][1], name="down1"), ln_g[3], ln_b[3], tm=tm, name="ln4")

    dh4, loss_part = loss_grad(h4, tgt, first=N_META, count=seq, tm=tm)

    grads = {}
    dr4, dgb4 = ln_bwd(r4, dh4, ln_g[3], tm=tm, name="ln4_bwd")
    d_down1 = tn(act1, dr4, name="d_down1")
    dact1 = nt(dr4, wt["down"][1], name="d_act1")
    dup1, dfconv1 = ffn_act_bwd(up1, dact1, wt["fconv"][1], tm=tm_ffn, name="ffn_act1_bwd")
    dup1 = dup1.reshape(up1.shape)
    d_up1 = tn(h3, dup1, name="d_up1")
    dh3 = nt(dup1, wt["up"][1], res=dr4, res_scale=ALPHA, name="d_h3")

    dr3, dgb3 = ln_bwd(r3, dh3, ln_g[2], tm=tm, name="ln3_bwd")
    d_bout = tn(sc, dr3, name="d_b_out")
    dsc = nt(dr3, wt["b_out"], name="d_sc")
    dpb, dbconv = sc_bwd(pb, dsc, wt["b_conv"], tm=tm, cb=4 * HEAD_DIM)
    d_bin = tn(h2, dpb, name="d_b_in")
    dh2 = nt(dpb, wt["b_in"], res=dr3, res_scale=ALPHA, name="d_h2")

    dr2, dgb2 = ln_bwd(r2, dh2, ln_g[1], tm=tm, name="ln2_bwd")
    d_down0 = tn(act0, dr2, name="d_down0")
    dact0 = nt(dr2, wt["down"][0], name="d_act0")
    dup0, dfconv0 = ffn_act_bwd(up0, dact0, wt["fconv"][0], tm=tm_ffn, name="ffn_act0_bwd")
    dup0 = dup0.reshape(up0.shape)
    d_up0 = tn(h1, dup0, name="d_up0")
    dh1 = nt(dup0, wt["up"][0], res=dr2, res_scale=ALPHA, name="d_h1")

    dr1, dgb1 = ln_bwd(r1, dh1, ln_g[0], tm=tm, name="ln1_bwd")
    d_aout = tn(onz, dr1, name="d_a_out")
    donz = nt(dr1, wt["a_out"], name="d_onz")
    d_o, dz, dnw = gdn_post_bwd(o[None], pz, donz, wt["anorm_b"], tm=tm)
    dqkvbg = gdn_chunk_bwd(qkvbg, states, tinv, d_o[0])
    dp5, daconv, dscal = gdn_pre_bwd(p5, dqkvbg, wt["a_conv3"], wt["alog_b"], wt["dtb_b"], tm=tm, cb=2 * HEAD_DIM)
    d_a5 = tn(h0, dp5, name="d_a_in5")
    d_az = tn(h0, dz, name="d_a_inz")
    dh0 = nt(dp5, wt["a5"], res=dr1, res_scale=ALPHA, name="d_h0a")
    dh0 = nt(dz, wt["az"], res=dh0, res_scale=1.0, name="d_h0")

    width = HEADS * HEAD_DIM
    d_ba = head_lane_sum(d_a5[0, 3:5])[:, :, :HEADS]
    d_a_in = jnp.concatenate([d_a5[0, 0], d_a5[0, 1], d_a5[0, 2], d_az[0, 0], d_ba[0], d_ba[1]], axis=1)
    n_in = d_a_in.shape[1] // 4
    grads["a_w_in"] = [d_a_in.reshape(d, 4, n_in).transpose(1, 0, 2)]
    grads["a_w_out"] = [d_aout.reshape(4, width // 4, d)]
    grads["b_w_in"] = [d_bin[0].transpose(1, 0, 2).reshape(d, 4, 3 * d // 4).transpose(1, 0, 2)]
    grads["b_w_out"] = [d_bout.reshape(4, d // 4, d)]
    grads["ffn_w_up"] = [d_up0[0], d_up1[0]]
    grads["ffn_w_down"] = [t.reshape(4, -1, d) for t in (d_down0, d_down1)]
    grads["a_conv"] = daconv.transpose(1, 0, 2).reshape(1, GDN_CONV, 3 * width)
    per_head = dscal.reshape(2, 8, HEADS, HEAD_DIM).sum(axis=(1, 3))
    grads["a_log"] = per_head[0][None]
    grads["a_dt_bias"] = per_head[1][None]
    grads["a_norm"] = dnw.reshape(8, HEADS, HEAD_DIM).sum(axis=(0, 1))[None]
    grads["b_conv"] = dbconv[None]
    lns = [dgb1, dgb2, dgb3, dgb4]
    grads["ln_mix_g"] = jnp.stack([lns[0][0].sum(0), lns[2][0].sum(0)])
    grads["ln_mix_b"] = jnp.stack([lns[0][1].sum(0), lns[2][1].sum(0)])
    grads["ln_ffn_g"] = jnp.stack([lns[1][0].sum(0), lns[3][0].sum(0)])
    grads["ln_ffn_b"] = jnp.stack([lns[1][1].sum(0), lns[3][1].sum(0)])
    grads["ffn_conv"] = jnp.stack([t.transpose(1, 0, 2).reshape(FFN_CONV, -1) for t in (dfconv0, dfconv1)])
    grads["meta"] = dh0[0, :N_META]
    return loss_part, dh0, grads


WEIGHTS = ["meta", "a_w_in", "a_conv", "a_log", "a_dt_bias", "a_norm", "a_w_out", "b_w_in", "b_conv", "b_w_out",
           "ln_mix_g", "ln_mix_b", "ffn_w_up", "ffn_conv", "ffn_w_down", "ln_ffn_g", "ln_ffn_b"]
MATMUL_WEIGHTS = ["a_w_in", "a_w_out", "b_w_in", "b_w_out", "ffn_w_up", "ffn_w_down"]
SMALL_SHARDED = ["a_conv", "b_conv", "ffn_conv", "meta"]
REPLICATED = ["a_log", "a_dt_bias", "a_norm", "ln_mix_g", "ln_mix_b", "ln_ffn_g", "ln_ffn_b"]
SHARD_AXIS = {"meta": 1, "a_w_in": 2, "a_conv": 2, "a_w_out": 1, "b_w_in": 2, "b_conv": 2, "b_w_out": 1,
              "ffn_w_up": 2, "ffn_conv": 2, "ffn_w_down": 1}
PACK_COLS = 1024
PACK_ROWS_MULTIPLE = 32


def _pack(pieces, lead=()):
    flat = jnp.concatenate([p.reshape(lead + (-1,)) for p in pieces], axis=-1)
    n = flat.shape[-1]
    rows = -(-n // (PACK_COLS * PACK_ROWS_MULTIPLE)) * PACK_ROWS_MULTIPLE
    flat = jnp.pad(flat, [(0, 0)] * len(lead) + [(0, rows * PACK_COLS - n)])
    return flat.reshape(lead + (rows, PACK_COLS))


def _unpack(buf, shapes, lead=()):
    flat = buf.reshape(lead + (-1,))
    out, off = [], 0
    for shp in shapes:
        n = 1
        for s in shp:
            n *= s
        out.append(flat[..., off:off + n].reshape(lead + tuple(shp)))
        off += n
    return out


def _join_shards(stacked, axis):
    return jnp.concatenate([stacked[k] for k in range(4)], axis=axis)


def _split_shards(full, axis):
    return jnp.stack(jnp.split(full, 4, axis=axis))


def _gather_weights(w):
    layers = [w[n][l].astype(BF16) for n in MATMUL_WEIGHTS for l in range(w[n].shape[0])]
    *stacked, small = all_gather_shards(layers + [_pack([w[n] for n in SMALL_SHARDED])], "gather_weights")
    full, it = {}, iter(stacked)
    for n in MATMUL_WEIGHTS:
        full[n] = [next(it) for _ in range(w[n].shape[0])]
    for n, t in zip(SMALL_SHARDED, _unpack(small, [w[n].shape for n in SMALL_SHARDED], lead=(4,))):
        full[n] = _join_shards(t, SHARD_AXIS[n])
    return _layout_weights(full, w)


def _layout_weights(full, w):
    width = HEADS * HEAD_DIM
    wt = {n: w[n] for n in ("ln_mix_g", "ln_mix_b", "ln_ffn_g", "ln_ffn_b")}
    w_in = _join_shards(full["a_w_in"][0], 1)
    d = w_in.shape[0]
    blocks = [w_in[:, s * width:(s + 1) * width] for s in range(4)]
    b_exp = jnp.repeat(w_in[:, 4 * width:4 * width + HEADS], HEAD_DIM, axis=1)
    a_exp = jnp.repeat(w_in[:, 4 * width + HEADS:], HEAD_DIM, axis=1)
    wt["a5"] = jnp.stack([blocks[0], blocks[1], blocks[2], b_exp, a_exp])[None]
    wt["az"] = blocks[3][None, None]
    wt["a_out"] = full["a_w_out"][0].reshape(1, 1, width, d)
    wt["b_in"] = _join_shards(full["b_w_in"][0], 1).reshape(d, 3, d).transpose(1, 0, 2)[None]
    wt["b_out"] = full["b_w_out"][0].reshape(1, 1, d, d)
    n_ff = full["ffn_w_up"][0].shape[2]
    wt["up"] = [t[None] for t in full["ffn_w_up"]]
    wt["down"] = [t.reshape(2, 1, n_ff, d) for t in full["ffn_w_down"]]
    wt["a_conv3"] = full["a_conv"][0].reshape(GDN_CONV, 3, width).transpose(1, 0, 2)
    wt["b_conv"] = full["b_conv"][0]
    wt["fconv"] = [full["ffn_conv"][l].reshape(FFN_CONV, 2, n_ff).transpose(1, 0, 2) for l in range(2)]
    wt["meta"] = full["meta"]
    wt["alog_b"] = jnp.repeat(w["a_log"][0], HEAD_DIM)[None]
    wt["dtb_b"] = jnp.repeat(w["a_dt_bias"][0], HEAD_DIM)[None]
    wt["anorm_b"] = jnp.tile(w["a_norm"][0], HEADS)[None]
    return wt


def _reduce_grads(grads, loss_part, w):
    pieces = [_split_shards(grads[n], SHARD_AXIS[n]) for n in SMALL_SHARDED]
    same = jnp.concatenate([grads[n].reshape(-1) for n in REPLICATED] + [jnp.sum(loss_part).reshape(1)])
    pieces.append(jnp.broadcast_to(same, (4,) + same.shape))
    groups = [grads[n] for n in MATMUL_WEIGHTS] + [[_pack(pieces, lead=(4,))]]
    *totals, small = reduce_scatter_grads(groups, [BF16] * len(MATMUL_WEIGHTS) + [F32])
    out = {n: t.reshape(w[n].shape) for n, t in zip(MATMUL_WEIGHTS, totals)}
    rest = SMALL_SHARDED + REPLICATED
    unpacked = _unpack(small[0], [w[n].shape for n in rest] + [()])
    out.update(zip(rest, unpacked[:-1]))
    return out, unpacked[-1]


def kernel(x, meta, a_w_in, a_conv, a_log, a_dt_bias, a_norm, a_w_out, b_w_in, b_conv, b_w_out, ln_mix_g, ln_mix_b, ffn_w_up, ffn_conv, ffn_w_down, ln_ffn_g, ln_ffn_b, loss_target, m_meta, m_a_w_in, m_a_conv, m_a_log, m_a_dt_bias, m_a_norm, m_a_w_out, m_b_w_in, m_b_conv, m_b_w_out, m_ln_mix_g, m_ln_mix_b, m_ffn_w_up, m_ffn_conv, m_ffn_w_down, m_ln_ffn_g, m_ln_ffn_b, v_meta, v_a_w_in, v_a_conv, v_a_log, v_a_dt_bias, v_a_norm, v_a_w_out, v_b_w_in, v_b_conv, v_b_w_out, v_ln_mix_g, v_ln_mix_b, v_ffn_w_up, v_ffn_conv, v_ffn_w_down, v_ln_ffn_g, v_ln_ffn_b):
    w = dict(meta=meta, a_w_in=a_w_in, a_conv=a_conv, a_log=a_log, a_dt_bias=a_dt_bias, a_norm=a_norm, a_w_out=a_w_out,
             b_w_in=b_w_in, b_conv=b_conv, b_w_out=b_w_out, ln_mix_g=ln_mix_g, ln_mix_b=ln_mix_b, ffn_w_up=ffn_w_up,
             ffn_conv=ffn_conv, ffn_w_down=ffn_w_down, ln_ffn_g=ln_ffn_g, ln_ffn_b=ln_ffn_b)
    m = dict(meta=m_meta, a_w_in=m_a_w_in, a_conv=m_a_conv, a_log=m_a_log, a_dt_bias=m_a_dt_bias, a_norm=m_a_norm,
             a_w_out=m_a_w_out, b_w_in=m_b_w_in, b_conv=m_b_conv, b_w_out=m_b_w_out, ln_mix_g=m_ln_mix_g,
             ln_mix_b=m_ln_mix_b, ffn_w_up=m_ffn_w_up, ffn_conv=m_ffn_conv, ffn_w_down=m_ffn_w_down,
             ln_ffn_g=m_ln_ffn_g, ln_ffn_b=m_ln_ffn_b)
    v = dict(meta=v_meta, a_w_in=v_a_w_in, a_conv=v_a_conv, a_log=v_a_log, a_dt_bias=v_a_dt_bias, a_norm=v_a_norm,
             a_w_out=v_a_w_out, b_w_in=v_b_w_in, b_conv=v_b_conv, b_w_out=v_b_w_out, ln_mix_g=v_ln_mix_g,
             ln_mix_b=v_ln_mix_b, ffn_w_up=v_ffn_w_up, ffn_conv=v_ffn_conv, ffn_w_down=v_ffn_w_down,
             ln_ffn_g=v_ln_ffn_g, ln_ffn_b=v_ln_ffn_b)
    seq = x.shape[1]
    wt = _gather_weights(w)
    loss_part, dh0, grads = _local_step(x[0], loss_target[0], wt)
    grad_w, loss = _reduce_grads(grads, loss_part, w)
    grad_x = dh0[:, N_META:N_META + seq]
    steps = [adamw(w[n], grad_w[n], m[n], v[n], "adamw_" + n) for n in WEIGHTS]
    return (loss, grad_x, *[grad_w[n] for n in WEIGHTS], *[s[0] for s in steps], *[s[1] for s in steps],
            *[s[2] for s in steps])
```

```python
import functools

import jax
import jax.numpy as jnp
from jax import lax
from jax.experimental import pallas as pl
from jax.experimental.pallas import tpu as pltpu

F32 = jnp.float32
BF16 = jnp.bfloat16
HI = lax.Precision.HIGHEST

N_META = 16
HEADS = 8
HEAD_DIM = 128
CHUNK = 64
GDN_CONV = 4
SC_CONV = 3
FFN_CONV = 3
ALPHA = 4.0 ** 0.25
LN_EPS = 1e-5
RMS_EPS = 1e-6
L2_EPS = 1e-6
Q_SCALE = HEAD_DIM ** -0.5

ADAM_LR = 0.001
ADAM_B1 = 0.9
ADAM_B2 = 0.999
ADAM_EPS = 1e-08
ADAM_WD = 0.01
ADAM_STEP = 10

HALO = 8
VMEM_LIMIT = 48 * 1024 * 1024


def _params(sem=None):
    return pltpu.CompilerParams(dimension_semantics=sem, vmem_limit_bytes=VMEM_LIMIT)


def _dot(a, b, prec=None):
    return jnp.dot(a, b, preferred_element_type=F32, precision=prec)


def _dot_nt(a, b, prec=None):
    return lax.dot_general(a, b, (((1,), (1,)), ((), ())), preferred_element_type=F32, precision=prec)


def _dot_tn(a, b, prec=None):
    return lax.dot_general(a, b, (((0,), (0,)), ((), ())), preferred_element_type=F32, precision=prec)


def _sigmoid(x):
    return 1.0 / (1.0 + jnp.exp(-x))


def _tri_masks():
    r = lax.broadcasted_iota(jnp.int32, (CHUNK, CHUNK), 0)
    c = lax.broadcasted_iota(jnp.int32, (CHUNK, CHUNK), 1)
    return r >= c, r > c, r == c


def _split_hi_lo(x):
    hi = x.astype(BF16)
    return hi, (x - hi.astype(F32)).astype(BF16)


def _mask_dot(mask, x):
    hi, lo = _split_hi_lo(x)
    return _dot(mask, hi) + _dot(mask, lo)


@jax.custom_vjp
def _cumsum_rows(g):
    causal, _, _ = _tri_masks()
    return _mask_dot(causal.astype(BF16), g)


def _cumsum_rows_fwd(g):
    return _cumsum_rows(g), None


def _cumsum_rows_bwd(_, dy):
    _, strict, _ = _tri_masks()
    return (_mask_dot((~strict).astype(BF16), dy),)


_cumsum_rows.defvjp(_cumsum_rows_fwd, _cumsum_rows_bwd)


def _dot_split3(a, b):
    a_hi, a_lo = _split_hi_lo(a)
    b_hi, b_lo = _split_hi_lo(b)
    return _dot(a_hi, b_hi) + (_dot(a_hi, b_lo) + _dot(a_lo, b_hi))


def _gdn_m(ks, g64s, bbs):
    causal, strict, _ = _tri_masks()
    a = [_cumsum_rows(g) for g in g64s]
    decay = [jnp.exp(jnp.where(causal, x - x.T, -1e30)) for x in a]
    kk = [_dot_nt(k * b, k) for k, b in zip(ks, bbs)]
    return [jnp.where(strict, x * d, 0.0) for x, d in zip(kk, decay)]


def _gdn_inverse(ms):
    r = lax.broadcasted_iota(jnp.int32, (CHUNK, CHUNK), 0)
    c = lax.broadcasted_iota(jnp.int32, (CHUNK, CHUNK), 1)
    eye = (r == c).astype(F32)
    same = [jnp.right_shift(r, s) == jnp.right_shift(c, s) for s in (3, 4, 5)]
    d = [jnp.where(same[0], m, 0.0) for m in ms]
    p = [_dot(x, x) for x in d]
    t = [eye - x for x in d]
    t = [x + _dot(x, y) for x, y in zip(t, p)]
    p = [_dot(x, x) for x in p]
    t = [x + _dot(x, y) for x, y in zip(t, p)]
    for inner, outer in ((same[0], same[1]), (same[1], same[2]), (same[2], None)):
        joins = ~inner if outer is None else (outer & ~inner)
        o = [_dot(x, jnp.where(joins, m, 0.0)) for x, m in zip(t, ms)]
        t = [x - _dot(y, x) for x, y in zip(t, o)]
    res = [eye - x - _dot_split3(m, x) for m, x in zip(ms, t)]
    return [x + _dot(x, y) for x, y in zip(t, res)]


def _gdn_apply(qs, ks, vs, gbs, g64s, bbs, ss, ts):
    causal, _, _ = _tri_masks()
    n = range(len(qs))
    gc = [_cumsum_rows(g) for g in gbs]
    a = [_cumsum_rows(g) for g in g64s]
    decay = [jnp.exp(jnp.where(causal, x - x.T, -1e30)) for x in a]
    eg = [jnp.exp(x) for x in gc]
    u = [_dot(ts[h], vs[h] * bbs[h]) for h in n]
    w = [_dot(ts[h], ks[h] * bbs[h] * eg[h]) for h in n]
    qk = [_dot_nt(qs[h], ks[h]) * decay[h] for h in n]
    gl = [jnp.sum(g, axis=0, keepdims=True) for g in gbs]
    kd = [ks[h] * jnp.exp(gl[h] - gc[h]) for h in n]
    v_new = [u[h] - _dot(w[h], ss[h]) for h in n]
    o = [_dot(qs[h] * eg[h], ss[h]) + _dot(qk[h], v_new[h]) for h in n]
    s2 = [ss[h] * jnp.exp(gl[h]) + _dot_tn(kd[h], v_new[h]) for h in n]
    return o, s2


def _head_slices(h):
    return slice(h * HEAD_DIM, (h + 1) * HEAD_DIM), slice(h * HEAD_DIM, h * HEAD_DIM + CHUNK)


def _gdn_head_values(x_ref):
    out = [[], [], [], [], [], []]
    for h in range(HEADS):
        sl, sl64 = _head_slices(h)
        for lst, val in zip(out, (x_ref[0, :, sl], x_ref[1, :, sl], x_ref[2, :, sl], x_ref[4, :, sl],
                                  x_ref[4, :, sl64], x_ref[3, :, sl])):
            lst.append(val)
    return out


def gdn_chunk_fwd(qkvbg):
    _, lp, width = qkvbg.shape
    n_chunks = lp // CHUNK

    def body(x_ref, o_ref, s_ref, t_ref, state):
        @pl.when(pl.program_id(0) == 0)
        def _():
            state[...] = jnp.zeros_like(state)

        qs, ks, vs, gbs, g64s, bbs = _gdn_head_values(x_ref)
        ss = [state[h] for h in range(HEADS)]
        ts = _gdn_inverse(_gdn_m(ks, g64s, bbs))
        os_, s2 = _gdn_apply(qs, ks, vs, gbs, g64s, bbs, ss, ts)
        for h in range(HEADS):
            s_ref[0, h] = ss[h]
            t_ref[0, h] = ts[h]
            o_ref[:, _head_slices(h)[0]] = os_[h]
            state[h] = s2[h]

    return pl.pallas_call(
        body,
        name="gdn_chunk_fwd",
        grid=(n_chunks,),
        in_specs=[pl.BlockSpec((5, CHUNK, width), lambda c: (0, c, 0))],
        out_specs=[
            pl.BlockSpec((CHUNK, width), lambda c: (c, 0)),
            pl.BlockSpec((1, HEADS, HEAD_DIM, HEAD_DIM), lambda c: (c, 0, 0, 0)),
            pl.BlockSpec((1, HEADS, CHUNK, CHUNK), lambda c: (c, 0, 0, 0)),
        ],
        out_shape=[
            jax.ShapeDtypeStruct((lp, width), F32),
            jax.ShapeDtypeStruct((n_chunks, HEADS, HEAD_DIM, HEAD_DIM), F32),
            jax.ShapeDtypeStruct((n_chunks, HEADS, CHUNK, CHUNK), F32),
        ],
        scratch_shapes=[pltpu.VMEM((HEADS, HEAD_DIM, HEAD_DIM), F32)],
        compiler_params=_params(("arbitrary",)),
    )(qkvbg)


def gdn_chunk_bwd(qkvbg, states, tinv, d_o):
    _, lp, width = qkvbg.shape
    n_chunks = lp // CHUNK
    last = n_chunks - 1

    def body(x_ref, s_ref, t_ref, do_ref, dx_ref, dstate):
        @pl.when(pl.program_id(0) == 0)
        def _():
            dstate[...] = jnp.zeros_like(dstate)

        heads = range(HEADS)
        qs, ks, vs, gbs, g64s, bbs = _gdn_head_values(x_ref)
        ss = [s_ref[0, h] for h in heads]
        ts = [t_ref[0, h] for h in heads]
        d_out = ([do_ref[:, _head_slices(h)[0]] for h in heads], [dstate[h] for h in heads])
        _, vjp_apply = jax.vjp(_gdn_apply, qs, ks, vs, gbs, g64s, bbs, ss, ts)
        dq, dk, dv, dgb, dg64, dbb, ds, dt = vjp_apply(d_out)
        tts = [t.T for t in ts]
        dm = [_dot(tts[h], dt[h]) for h in heads]
        dm = [-_dot(dm[h], tts[h]) for h in heads]
        _, vjp_m = jax.vjp(_gdn_m, ks, g64s, bbs)
        dk2, dg64m, dbb2 = vjp_m(dm)
        for h in heads:
            sl, sl64 = _head_slices(h)
            dx_ref[0, :, sl] = dq[h]
            dx_ref[1, :, sl] = dk[h] + dk2[h]
            dx_ref[2, :, sl] = dv[h]
            dx_ref[3, :, sl] = dbb[h] + dbb2[h]
            dx_ref[4, :, sl] = dgb[h]
            dx_ref[4, :, sl64] += dg64[h] + dg64m[h]
            dstate[h] = ds[h]

    return pl.pallas_call(
        body,
        name="gdn_chunk_bwd",
        grid=(n_chunks,),
        in_specs=[
            pl.BlockSpec((5, CHUNK, width), lambda c: (0, last - c, 0)),
            pl.BlockSpec((1, HEADS, HEAD_DIM, HEAD_DIM), lambda c: (last - c, 0, 0, 0)),
            pl.BlockSpec((1, HEADS, CHUNK, CHUNK), lambda c: (last - c, 0, 0, 0)),
            pl.BlockSpec((CHUNK, width), lambda c: (last - c, 0)),
        ],
        out_specs=pl.BlockSpec((5, CHUNK, width), lambda c: (0, last - c, 0)),
        out_shape=jax.ShapeDtypeStruct(qkvbg.shape, F32),
        scratch_shapes=[pltpu.VMEM((HEADS, HEAD_DIM, HEAD_DIM), F32)],
        compiler_params=_params(("arbitrary",)),
    )(qkvbg, states, tinv, d_o)


def mm_nn(a, b, *, tm, name):
    ks, m, tk = a.shape
    _, ns, _, tn = b.shape

    def body(a_ref, b_ref, o_ref):
        p = _dot(a_ref[...].astype(BF16), b_ref[...])

        @pl.when(pl.program_id(2) == 0)
        def _():
            o_ref[...] = p

        @pl.when(pl.program_id(2) > 0)
        def _():
            o_ref[...] += p

    return pl.pallas_call(
        body,
        name=name,
        grid=(ns, m // tm, ks),
        in_specs=[
            pl.BlockSpec((None, tm, tk), lambda n, i, k: (k, i, 0)),
            pl.BlockSpec((None, None, tk, tn), lambda n, i, k: (k, n, 0, 0)),
        ],
        out_specs=pl.BlockSpec((None, tm, tn), lambda n, i, k: (n, i, 0)),
        out_shape=jax.ShapeDtypeStruct((ns, m, tn), F32),
        compiler_params=_params(("arbitrary", "arbitrary", "arbitrary")),
    )(a, b)


def mm_nt(dy, w, *, tm, name, res=None, res_scale=1.0):
    ns, m, tn = dy.shape
    ks, _, tk, _ = w.shape

    def body(*refs):
        if res is None:
            dy_ref, w_ref, o_ref = refs
        else:
            dy_ref, w_ref, r_ref, o_ref = refs
        p = _dot_nt(dy_ref[...].astype(BF16), w_ref[...])

        @pl.when(pl.program_id(2) == 0)
        def _():
            o_ref[...] = p if res is None else p + res_scale * r_ref[...]

        @pl.when(pl.program_id(2) > 0)
        def _():
            o_ref[...] += p

    in_specs = [
        pl.BlockSpec((None, tm, tn), lambda k, i, n: (n, i, 0)),
        pl.BlockSpec((None, None, tk, tn), lambda k, i, n: (k, n, 0, 0)),
    ]
    args = [dy, w]
    if res is not None:
        in_specs.append(pl.BlockSpec((None, tm, tk), lambda k, i, n: (k, i, 0)))
        args.append(res)
    return pl.pallas_call(
        body,
        name=name,
        grid=(ks, m // tm, ns),
        in_specs=in_specs,
        out_specs=pl.BlockSpec((None, tm, tk), lambda k, i, n: (k, i, 0)),
        out_shape=jax.ShapeDtypeStruct((ks, m, tk), F32),
        compiler_params=_params(("arbitrary", "arbitrary", "arbitrary")),
    )(*args)


def mm_tn(x, dy, *, tm, name):
    ks, m, tk = x.shape
    ns, _, tn = dy.shape

    def body(x_ref, dy_ref, o_ref):
        p = _dot_tn(x_ref[...].astype(BF16), dy_ref[...].astype(BF16))

        @pl.when(pl.program_id(2) == 0)
        def _():
            o_ref[...] = p

        @pl.when(pl.program_id(2) > 0)
        def _():
            o_ref[...] += p

    return pl.pallas_call(
        body,
        name=name,
        grid=(ks, ns, m // tm),
        in_specs=[
            pl.BlockSpec((None, tm, tk), lambda k, n, i: (k, i, 0)),
            pl.BlockSpec((None, tm, tn), lambda k, n, i: (n, i, 0)),
        ],
        out_specs=pl.BlockSpec((None, None, tk, tn), lambda k, n, i: (k, n, 0, 0)),
        out_shape=jax.ShapeDtypeStruct((ks, ns, tk, tn), F32),
        compiler_params=_params(("arbitrary", "arbitrary", "arbitrary")),
    )(x, dy)


def _row_partial(x):
    rows, c = x.shape
    return jnp.sum(x.reshape(rows // 8, 8, c), axis=0)


def ln_fwd(h_prev, mix, g, b, *, tm, name):
    _, lp, d = h_prev.shape

    def body(h_ref, m_ref, g_ref, b_ref, r_ref, o_ref):
        r = ALPHA * h_ref[...] + m_ref[...]
        mu = jnp.mean(r, axis=-1, keepdims=True)
        xc = r - mu
        var = jnp.mean(xc * xc, axis=-1, keepdims=True)
        r_ref[...] = r
        o_ref[...] = xc * lax.rsqrt(var + LN_EPS) * g_ref[...] + b_ref[...]

    row = pl.BlockSpec((None, tm, d), lambda i: (0, i, 0))
    vec = pl.BlockSpec((1, d), lambda i: (0, 0))
    return pl.pallas_call(
        body,
        name=name,
        grid=(lp // tm,),
        in_specs=[row, row, vec, vec],
        out_specs=[row, row],
        out_shape=[jax.ShapeDtypeStruct((1, lp, d), F32)] * 2,
        compiler_params=_params(("arbitrary",)),
    )(h_prev, mix, g, b)


def ln_bwd(r, dh, g, *, tm, name):
    _, lp, d = r.shape

    def body(r_ref, dh_ref, g_ref, dr_ref, dgb_ref):
        x = r_ref[...]
        dh_v = dh_ref[...]
        mu = jnp.mean(x, axis=-1, keepdims=True)
        xc = x - mu
        rstd = lax.rsqrt(jnp.mean(xc * xc, axis=-1, keepdims=True) + LN_EPS)
        xh = xc * rstd
        dxh = dh_v * g_ref[...]
        m1 = jnp.mean(dxh, axis=-1, keepdims=True)
        m2 = jnp.mean(dxh * xh, axis=-1, keepdims=True)
        dr_ref[...] = rstd * (dxh - m1 - xh * m2)

        @pl.when(pl.program_id(0) == 0)
        def _():
            dgb_ref[...] = jnp.zeros_like(dgb_ref)

        dgb_ref[0] += _row_partial(dh_v * xh)
        dgb_ref[1] += _row_partial(dh_v)

    row = pl.BlockSpec((None, tm, d), lambda i: (0, i, 0))
    return pl.pallas_call(
        body,
        name=name,
        grid=(lp // tm,),
        in_specs=[row, row, pl.BlockSpec((1, d), lambda i: (0, 0))],
        out_specs=[row, pl.BlockSpec((2, 8, d), lambda i: (0, 0, 0))],
        out_shape=[jax.ShapeDtypeStruct((1, lp, d), F32), jax.ShapeDtypeStruct((2, 8, d), F32)],
        compiler_params=_params(("arbitrary",)),
    )(r, dh, g)


def loss_grad(h, target, *, first, count, tm):
    _, lp, d = h.shape

    def body(h_ref, t_ref, dh_ref, l_ref):
        row = pl.program_id(0) * tm + lax.broadcasted_iota(jnp.int32, (tm, d), 0)
        valid = (row >= first) & (row < first + count)
        err = jnp.where(valid, h_ref[...] - t_ref[...], 0.0)
        dh_ref[...] = err * (1.0 / d)

        @pl.when(pl.program_id(0) == 0)
        def _():
            l_ref[...] = jnp.zeros_like(l_ref)

        l_ref[...] += _row_partial(err * err) * (0.5 / d)

    return pl.pallas_call(
        body,
        name="loss_grad",
        grid=(lp // tm,),
        in_specs=[pl.BlockSpec((None, tm, d), lambda i: (0, i, 0)), pl.BlockSpec((tm, d), lambda i: (i, 0))],
        out_specs=[pl.BlockSpec((None, tm, d), lambda i: (0, i, 0)), pl.BlockSpec((8, d), lambda i: (0, 0))],
        out_shape=[jax.ShapeDtypeStruct((1, lp, d), F32), jax.ShapeDtypeStruct((8, d), F32)],
        compiler_params=_params(("arbitrary",)),
    )(h, target)


def _halo_index(tile, tm):
    return jnp.maximum(tile * (tm // HALO) - 1, 0)


SUB = 8
LANES = 128


def _shift_down(cur, prev, s):
    if s == 0:
        return cur
    row = lax.broadcasted_iota(jnp.int32, cur.shape, 0)
    return jnp.where(row < s, pltpu.roll(prev, s, axis=0), pltpu.roll(cur, s, axis=0))


def _shift_up(cur, nxt, s):
    if s == 0:
        return cur
    row = lax.broadcasted_iota(jnp.int32, cur.shape, 0)
    return jnp.where(row < SUB - s, pltpu.roll(cur, SUB - s, axis=0), pltpu.roll(nxt, SUB - s, axis=0))


def _silu_parts(c):
    sg = _sigmoid(c)
    return c * sg, sg * (1.0 + c * (1.0 - sg))


def _head_sum(x):
    rows, c = x.shape
    parts = []
    for h in range(c // HEAD_DIM):
        s = jnp.sum(x[:, h * HEAD_DIM:(h + 1) * HEAD_DIM], axis=-1, keepdims=True)
        parts.append(jnp.broadcast_to(s, (rows, HEAD_DIM)))
    return parts[0] if len(parts) == 1 else jnp.concatenate(parts, axis=-1)


def _log1p(y):
    u = 1.0 + y
    d = u - 1.0
    return jnp.where(d == 0.0, y, jnp.log(u) * (y / jnp.where(d == 0.0, 1.0, d)))


def _softplus(x):
    return jnp.maximum(x, 0.0) + _log1p(jnp.exp(-jnp.abs(x)))


def gdn_pre_fwd(p5, conv_w, alog_b, dtb_b, *, tm, cb):
    _, lp, width = p5.shape
    taps = conv_w.shape[1]

    def body(x_ref, halo_ref, w_ref, al_ref, dt_ref, o_ref):
        first_tile = pl.program_id(1) == 0

        def strip(r0, prev_of):
            rows = pl.ds(r0, SUB)
            for c0 in range(0, cb, HEAD_DIM):
                cs = slice(c0, c0 + HEAD_DIM)
                for s in range(3):
                    cur = x_ref[s, rows, cs]
                    prev = prev_of(s, cs)
                    conv = w_ref[s, taps - 1:taps, cs] * cur
                    for j in range(taps - 1):
                        conv += w_ref[s, j:j + 1, cs] * _shift_down(cur, prev, taps - 1 - j)
                    y, _ = _silu_parts(conv)
                    if s < 2:
                        y = y * lax.rsqrt(jnp.sum(y * y, axis=-1, keepdims=True) + L2_EPS)
                        if s == 0:
                            y = y * Q_SCALE
                    o_ref[s, rows, cs] = y
                o_ref[3, rows, cs] = _sigmoid(x_ref[3, rows, cs])
                o_ref[4, rows, cs] = -jnp.exp(al_ref[:, cs]) * _softplus(x_ref[4, rows, cs] + dt_ref[:, cs])

        strip(0, lambda s, cs: jnp.where(first_tile, 0.0, halo_ref[s, :, cs]))

        def loop_body(k, carry):
            r0 = pl.multiple_of(k * SUB, SUB)
            strip(r0, lambda s, cs: x_ref[s, pl.ds(pl.multiple_of(r0 - SUB, SUB), SUB), cs])
            return carry

        lax.fori_loop(1, tm // SUB, loop_body, 0)

    return pl.pallas_call(
        body,
        name="gdn_pre_fwd",
        grid=(width // cb, lp // tm),
        in_specs=[
            pl.BlockSpec((5, tm, cb), lambda j, i: (0, i, j)),
            pl.BlockSpec((3, HALO, cb), lambda j, i: (0, _halo_index(i, tm), j)),
            pl.BlockSpec((3, taps, cb), lambda j, i: (0, 0, j)),
            pl.BlockSpec((1, cb), lambda j, i: (0, j)),
            pl.BlockSpec((1, cb), lambda j, i: (0, j)),
        ],
        out_specs=pl.BlockSpec((5, tm, cb), lambda j, i: (0, i, j)),
        out_shape=jax.ShapeDtypeStruct((5, lp, width), F32),
        compiler_params=_params(("arbitrary", "arbitrary")),
    )(p5, p5, conv_w, alog_b, dtb_b)


def gdn_pre_bwd(p5, dqkvbg, conv_w, alog_b, dtb_b, *, tm, cb):
    _, lp, width = p5.shape
    taps = conv_w.shape[1]
    last = lp // tm - 1
    n_strips = tm // SUB

    def body(x_ref, halo_ref, d_ref, w_ref, al_ref, dt_ref, dx_ref, dw_ref, dsc_ref, below):
        step = pl.program_id(1)
        first_tile = step == last

        @pl.when(step == 0)
        def _():
            below[...] = jnp.zeros_like(below)
            dw_ref[...] = jnp.zeros_like(dw_ref)
            dsc_ref[...] = jnp.zeros_like(dsc_ref)

        def strip(r0, prev_of):
            rows = pl.ds(r0, SUB)
            for c0 in range(0, cb, HEAD_DIM):
                cs = slice(c0, c0 + HEAD_DIM)
                for s in range(3):
                    cur = x_ref[s, rows, cs]
                    prev = prev_of(s, cs)
                    shifted = [_shift_down(cur, prev, taps - 1 - j) for j in range(taps)]
                    conv = w_ref[s, 0:1, cs] * shifted[0]
                    for j in range(1, taps):
                        conv += w_ref[s, j:j + 1, cs] * shifted[j]
                    y, dsilu = _silu_parts(conv)
                    dy = d_ref[s, rows, cs]
                    if s < 2:
                        rn = lax.rsqrt(jnp.sum(y * y, axis=-1, keepdims=True) + L2_EPS)
                        yn = y * rn
                        if s == 0:
                            dy = dy * Q_SCALE
                        dy = rn * (dy - yn * jnp.sum(dy * yn, axis=-1, keepdims=True))
                    dc = dy * dsilu
                    nxt = below[s, :, cs]
                    dx = w_ref[s, taps - 1:taps, cs] * dc
                    for j in range(taps - 1):
                        dx += w_ref[s, j:j + 1, cs] * _shift_up(dc, nxt, taps - 1 - j)
                    dx_ref[s, rows, cs] = dx
                    below[s, :, cs] = dc
                    for j in range(taps):
                        dw_ref[s, j, :, cs] += dc * shifted[j]
                beta = _sigmoid(x_ref[3, rows, cs])
                dx_ref[3, rows, cs] = d_ref[3, rows, cs] * beta * (1.0 - beta)
                z = x_ref[4, rows, cs] + dt_ref[:, cs]
                dg = d_ref[4, rows, cs] * -jnp.exp(al_ref[:, cs])
                da = dg * _sigmoid(z)
                dx_ref[4, rows, cs] = da
                dsc_ref[0, :, cs] += dg * _softplus(z)
                dsc_ref[1, :, cs] += da

        def loop_body(it, carry):
            r0 = pl.multiple_of((n_strips - 1 - it) * SUB, SUB)
            strip(r0, lambda s, cs: x_ref[s, pl.ds(pl.multiple_of(r0 - SUB, SUB), SUB), cs])
            return carry

        lax.fori_loop(0, n_strips - 1, loop_body, 0)
        strip(0, lambda s, cs: jnp.where(first_tile, 0.0, halo_ref[s, :, cs]))

    tile_spec = pl.BlockSpec((5, tm, cb), lambda j, i: (0, last - i, j))
    return pl.pallas_call(
        body,
        name="gdn_pre_bwd",
        grid=(width // cb, lp // tm),
        in_specs=[
            tile_spec,
            pl.BlockSpec((3, HALO, cb), lambda j, i: (0, _halo_index(last - i, tm), j)),
            tile_spec,
            pl.BlockSpec((3, taps, cb), lambda j, i: (0, 0, j)),
            pl.BlockSpec((1, cb), lambda j, i: (0, j)),
            pl.BlockSpec((1, cb), lambda j, i: (0, j)),
        ],
        out_specs=[
            tile_spec,
            pl.BlockSpec((3, taps, SUB, cb), lambda j, i: (0, 0, 0, j)),
            pl.BlockSpec((2, SUB, cb), lambda j, i: (0, 0, j)),
        ],
        out_shape=[
            jax.ShapeDtypeStruct((5, lp, width), F32),
            jax.ShapeDtypeStruct((3, taps, SUB, width), F32),
            jax.ShapeDtypeStruct((2, SUB, width), F32),
        ],
        scratch_shapes=[pltpu.VMEM((3, SUB, cb), F32)],
        compiler_params=_params(("arbitrary", "arbitrary")),
    )(p5, p5, dqkvbg, conv_w, alog_b, dtb_b)


def gdn_post_fwd(o, z, nw_b, *, tm):
    _, lp, width = o.shape

    def body(o_ref, z_ref, nw_ref, y_ref):
        ov = o_ref[...]
        rn = lax.rsqrt(_head_sum(ov * ov) * (1.0 / HEAD_DIM) + RMS_EPS)
        gate, _ = _silu_parts(z_ref[...])
        y_ref[...] = ov * rn * nw_ref[...] * gate

    row = pl.BlockSpec((None, tm, width), lambda i: (0, i, 0))
    return pl.pallas_call(
        body,
        name="gdn_post_fwd",
        grid=(lp // tm,),
        in_specs=[row, row, pl.BlockSpec((1, width), lambda i: (0, 0))],
        out_specs=row,
        out_shape=jax.ShapeDtypeStruct((1, lp, width), F32),
        compiler_params=_params(("arbitrary",)),
    )(o, z, nw_b)


def gdn_post_bwd(o, z, dy, nw_b, *, tm):
    _, lp, width = o.shape

    def body(o_ref, z_ref, dy_ref, nw_ref, do_ref, dz_ref, dnw_ref):
        ov = o_ref[...]
        rn = lax.rsqrt(_head_sum(ov * ov) * (1.0 / HEAD_DIM) + RMS_EPS)
        yn = ov * rn
        gate, dgate = _silu_parts(z_ref[...])
        d_on = dy_ref[...] * gate
        dz_ref[...] = dy_ref[...] * yn * nw_ref[...] * dgate
        a = d_on * nw_ref[...]
        do_ref[...] = rn * (a - yn * (_head_sum(a * yn) * (1.0 / HEAD_DIM)))

        @pl.when(pl.program_id(0) == 0)
        def _():
            dnw_ref[...] = jnp.zeros_like(dnw_ref)

        dnw_ref[...] += _row_partial(d_on * yn)

    row = pl.BlockSpec((None, tm, width), lambda i: (0, i, 0))
    return pl.pallas_call(
        body,
        name="gdn_post_bwd",
        grid=(lp // tm,),
        in_specs=[row, row, row, pl.BlockSpec((1, width), lambda i: (0, 0))],
        out_specs=[row, row, pl.BlockSpec((8, width), lambda i: (0, 0))],
        out_shape=[jax.ShapeDtypeStruct((1, lp, width), F32)] * 2 + [jax.ShapeDtypeStruct((8, width), F32)],
        compiler_params=_params(("arbitrary",)),
    )(o, z, dy, nw_b)


def head_lane_sum(x):
    s_n, rows, width = x.shape

    def body(x_ref, o_ref):
        lane = lax.broadcasted_iota(jnp.int32, (rows, HEAD_DIM), 1)
        acc = jnp.zeros((rows, HEAD_DIM), F32)
        for h in range(width // HEAD_DIM):
            s = jnp.sum(x_ref[:, h * HEAD_DIM:(h + 1) * HEAD_DIM], axis=-1, keepdims=True)
            acc = jnp.where(lane == h, s, acc)
        o_ref[...] = acc

    return pl.pallas_call(
        body,
        name="head_lane_sum",
        grid=(s_n,),
        in_specs=[pl.BlockSpec((None, rows, width), lambda s: (s, 0, 0))],
        out_specs=pl.BlockSpec((None, rows, HEAD_DIM), lambda s: (s, 0, 0)),
        out_shape=jax.ShapeDtypeStruct((s_n, rows, HEAD_DIM), F32),
        compiler_params=_params(("arbitrary",)),
    )(x)


def ffn_act_fwd(up, conv_w, *, tm, name):
    _, lp, c_w = up.shape
    taps = conv_w.shape[1]

    def body(u_ref, halo_ref, g_ref, w_ref, o_ref):
        first_tile = pl.program_id(1) == 0

        def strip(r0, prev_of):
            rows = pl.ds(r0, SUB)
            for c0 in range(0, c_w, LANES):
                cs = slice(c0, c0 + LANES)
                cur = u_ref[rows, cs]
                prev = prev_of(cs)
                conv = w_ref[taps - 1:taps, cs] * cur
                for j in range(taps - 1):
                    conv += w_ref[j:j + 1, cs] * _shift_down(cur, prev, taps - 1 - j)
                y, _ = _silu_parts(conv)
                o_ref[rows, cs] = y * g_ref[rows, cs]

        strip(0, lambda cs: jnp.where(first_tile, 0.0, halo_ref[:, cs]))

        def loop_body(s, carry):
            r0 = pl.multiple_of(s * SUB, SUB)
            strip(r0, lambda cs: u_ref[pl.ds(pl.multiple_of(r0 - SUB, SUB), SUB), cs])
            return carry

        lax.fori_loop(1, tm // SUB, loop_body, 0)

    return pl.pallas_call(
        body,
        name=name,
        grid=(2, lp // tm),
        in_specs=[
            pl.BlockSpec((None, tm, c_w), lambda s, i: (s, i, 0)),
            pl.BlockSpec((None, HALO, c_w), lambda s, i: (s, _halo_index(i, tm), 0)),
            pl.BlockSpec((None, tm, c_w), lambda s, i: (2 + s, i, 0)),
            pl.BlockSpec((None, taps, c_w), lambda s, i: (s, 0, 0)),
        ],
        out_specs=pl.BlockSpec((None, tm, c_w), lambda s, i: (s, i, 0)),
        out_shape=jax.ShapeDtypeStruct((2, lp, c_w), F32),
        compiler_params=_params(("arbitrary", "arbitrary")),
    )(up, up, up, conv_w)


def ffn_act_bwd(up, dact, conv_w, *, tm, name):
    _, lp, c_w = up.shape
    taps = conv_w.shape[1]
    last = lp // tm - 1
    n_strips = tm // SUB

    def body(u_ref, halo_ref, g_ref, d_ref, w_ref, dup_ref, dw_ref, below):
        step = pl.program_id(1)
        first_tile = step == last

        @pl.when(step == 0)
        def _():
            below[...] = jnp.zeros_like(below)
            dw_ref[...] = jnp.zeros_like(dw_ref)

        def strip(r0, prev_of):
            rows = pl.ds(r0, SUB)
            for c0 in range(0, c_w, LANES):
                cs = slice(c0, c0 + LANES)
                cur = u_ref[rows, cs]
                prev = prev_of(cs)
                shifted = [_shift_down(cur, prev, taps - 1 - j) for j in range(taps)]
                conv = w_ref[0:1, cs] * shifted[0]
                for j in range(1, taps):
                    conv += w_ref[j:j + 1, cs] * shifted[j]
                y, dsilu = _silu_parts(conv)
                d = d_ref[rows, cs]
                dup_ref[1, rows, cs] = d * y
                dc = d * g_ref[rows, cs] * dsilu
                nxt = below[:, cs]
                dx = w_ref[taps - 1:taps, cs] * dc
                for j in range(taps - 1):
                    dx += w_ref[j:j + 1, cs] * _shift_up(dc, nxt, taps - 1 - j)
                dup_ref[0, rows, cs] = dx
                below[:, cs] = dc
                for j in range(taps):
                    dw_ref[j, :, cs] += dc * shifted[j]

        def loop_body(it, carry):
            r0 = pl.multiple_of((n_strips - 1 - it) * SUB, SUB)
            strip(r0, lambda cs: u_ref[pl.ds(pl.multiple_of(r0 - SUB, SUB), SUB), cs])
            return carry

        lax.fori_loop(0, n_strips - 1, loop_body, 0)
        strip(0, lambda cs: jnp.where(first_tile, 0.0, halo_ref[:, cs]))

    return pl.pallas_call(
        body,
        name=name,
        grid=(2, lp // tm),
        in_specs=[
            pl.BlockSpec((None, tm, c_w), lambda s, i: (s, last - i, 0)),
            pl.BlockSpec((None, HALO, c_w), lambda s, i: (s, _halo_index(last - i, tm), 0)),
            pl.BlockSpec((None, tm, c_w), lambda s, i: (2 + s, last - i, 0)),
            pl.BlockSpec((None, tm, c_w), lambda s, i: (s, last - i, 0)),
            pl.BlockSpec((None, taps, c_w), lambda s, i: (s, 0, 0)),
        ],
        out_specs=[
            pl.BlockSpec((2, None, tm, c_w), lambda s, i: (0, s, last - i, 0)),
            pl.BlockSpec((None, taps, SUB, c_w), lambda s, i: (s, 0, 0, 0)),
        ],
        out_shape=[jax.ShapeDtypeStruct((2, 2, lp, c_w), F32), jax.ShapeDtypeStruct((2, taps, SUB, c_w), F32)],
        scratch_shapes=[pltpu.VMEM((SUB, c_w), F32)],
        compiler_params=_params(("arbitrary", "arbitrary")),
    )(up, up, up, dact, conv_w)


def sc_fwd(pb, conv_w, *, tm, cb):
    _, lp, width = pb.shape
    taps = conv_w.shape[0]

    def body(x_ref, halo_ref, w_ref, o_ref):
        first_tile = pl.program_id(1) == 0

        def strip(r0, prev_of):
            rows = pl.ds(r0, SUB)
            for c0 in range(0, cb, LANES):
                cs = slice(c0, c0 + LANES)
                cur = x_ref[1, rows, cs] * x_ref[2, rows, cs]
                prev = prev_of(cs)
                conv = w_ref[taps - 1:taps, cs] * cur
                for j in range(taps - 1):
                    conv += w_ref[j:j + 1, cs] * _shift_down(cur, prev, taps - 1 - j)
                o_ref[rows, cs] = x_ref[0, rows, cs] * conv

        strip(0, lambda cs: jnp.where(first_tile, 0.0, halo_ref[1, :, cs] * halo_ref[2, :, cs]))

        def loop_body(k, carry):
            r0 = pl.multiple_of(k * SUB, SUB)
            before = pl.ds(pl.multiple_of(r0 - SUB, SUB), SUB)
            strip(r0, lambda cs: x_ref[1, before, cs] * x_ref[2, before, cs])
            return carry

        lax.fori_loop(1, tm // SUB, loop_body, 0)

    return pl.pallas_call(
        body,
        name="sc_fwd",
        grid=(width // cb, lp // tm),
        in_specs=[
            pl.BlockSpec((3, tm, cb), lambda j, i: (0, i, j)),
            pl.BlockSpec((3, HALO, cb), lambda j, i: (0, _halo_index(i, tm), j)),
            pl.BlockSpec((taps, cb), lambda j, i: (0, j)),
        ],
        out_specs=pl.BlockSpec((None, tm, cb), lambda j, i: (0, i, j)),
        out_shape=jax.ShapeDtypeStruct((1, lp, width), F32),
        compiler_params=_params(("arbitrary", "arbitrary")),
    )(pb, pb, conv_w)


def sc_bwd(pb, ds, conv_w, *, tm, cb):
    _, lp, width = pb.shape
    taps = conv_w.shape[0]
    last = lp // tm - 1
    n_strips = tm // SUB

    def body(x_ref, halo_ref, d_ref, w_ref, dx_ref, dw_ref, below):
        step = pl.program_id(1)
        first_tile = step == last

        @pl.when(step == 0)
        def _():
            below[...] = jnp.zeros_like(below)
            dw_ref[...] = jnp.zeros_like(dw_ref)

        def strip(r0, prev_of):
            rows = pl.ds(r0, SUB)
            for c0 in range(0, cb, LANES):
                cs = slice(c0, c0 + LANES)
                gate, left, right = x_ref[0, rows, cs], x_ref[1, rows, cs], x_ref[2, rows, cs]
                cur = left * right
                prev = prev_of(cs)
                shifted = [_shift_down(cur, prev, taps - 1 - j) for j in range(taps)]
                conv = w_ref[0:1, cs] * shifted[0]
                for j in range(1, taps):
                    conv += w_ref[j:j + 1, cs] * shifted[j]
                d = d_ref[rows, cs]
                dx_ref[0, rows, cs] = d * conv
                dc = d * gate
                nxt = below[:, cs]
                dp = w_ref[taps - 1:taps, cs] * dc
                for j in range(taps - 1):
                    dp += w_ref[j:j + 1, cs] * _shift_up(dc, nxt, taps - 1 - j)
                dx_ref[1, rows, cs] = dp * right
                dx_ref[2, rows, cs] = dp * left
                below[:, cs] = dc
                for j in range(taps):
                    dw_ref[j, :, cs] += dc * shifted[j]

        def loop_body(it, carry):
            r0 = pl.multiple_of((n_strips - 1 - it) * SUB, SUB)
            before = pl.ds(pl.multiple_of(r0 - SUB, SUB), SUB)
            strip(r0, lambda cs: x_ref[1, before, cs] * x_ref[2, before, cs])
            return carry

        lax.fori_loop(0, n_strips - 1, loop_body, 0)
        strip(0, lambda cs: jnp.where(first_tile, 0.0, halo_ref[1, :, cs] * halo_ref[2, :, cs]))

    tile_spec = pl.BlockSpec((3, tm, cb), lambda j, i: (0, last - i, j))
    return pl.pallas_call(
        body,
        name="sc_bwd",
        grid=(width // cb, lp // tm),
        in_specs=[
            tile_spec,
            pl.BlockSpec((3, HALO, cb), lambda j, i: (0, _halo_index(last - i, tm), j)),
            pl.BlockSpec((None, tm, cb), lambda j, i: (0, last - i, j)),
            pl.BlockSpec((taps, cb), lambda j, i: (0, j)),
        ],
        out_specs=[tile_spec, pl.BlockSpec((taps, SUB, cb), lambda j, i: (0, 0, j))],
        out_shape=[jax.ShapeDtypeStruct((3, lp, width), F32), jax.ShapeDtypeStruct((taps, SUB, width), F32)],
        scratch_shapes=[pltpu.VMEM((SUB, cb), F32)],
        compiler_params=_params(("arbitrary", "arbitrary")),
    )(pb, pb, ds, conv_w)


TILE_BYTES = 1536 * 1024


def _rows_tile(rows, cols, multiple=8):
    if rows * cols * 4 <= TILE_BYTES or rows % multiple:
        return rows
    best = multiple
    for t in range(multiple, rows + 1, multiple):
        if rows % t == 0 and t * cols * 4 <= TILE_BYTES:
            best = t
    return best


def pair_sum(g, landed, core, out_dtype, name):
    _, rows, cols = g.shape
    half = rows // 2
    tr = _rows_tile(half, cols, 16)
    nb = half // tr

    def body(c_ref, g_ref, l_ref, o_ref):
        o_ref[...] = (g_ref[...] + l_ref[...]).astype(out_dtype)

    return pl.pallas_call(
        body,
        name=name,
        grid_spec=pltpu.PrefetchScalarGridSpec(
            num_scalar_prefetch=1,
            grid=(4, nb),
            in_specs=[
                pl.BlockSpec((None, tr, cols), lambda s, i, c: (s, c[0] * nb + i, 0)),
                pl.BlockSpec((None, tr, cols), lambda s, i, c: (s, i, 0)),
            ],
            out_specs=pl.BlockSpec((None, tr, cols), lambda s, i, c: (s, i, 0)),
        ),
        out_shape=jax.ShapeDtypeStruct((4, half, cols), out_dtype),
        compiler_params=_params(("arbitrary", "arbitrary")),
    )(core, g, landed)


def chip_sum(x, name):
    _, rows, cols = x.shape
    tr = _rows_tile(rows, cols, 16)

    def body(x0, x1, x2, x3, o_ref):
        acc = x0[...].astype(F32) + x1[...].astype(F32)
        o_ref[...] = (acc + x2[...].astype(F32)) + x3[...].astype(F32)

    return pl.pallas_call(
        body,
        name=name,
        grid=(rows // tr,),
        in_specs=[pl.BlockSpec((None, tr, cols), lambda i, k=k: (k, i, 0)) for k in range(4)],
        out_specs=pl.BlockSpec((tr, cols), lambda i: (i, 0)),
        out_shape=jax.ShapeDtypeStruct((rows, cols), F32),
        compiler_params=_params(("arbitrary",)),
    )(x, x, x, x)


def adamw(w, g, m, v, name):
    shape = w.shape
    cols = shape[-1]
    rows = w.size // cols
    tr = _rows_tile(rows, cols)

    def body(w_ref, g_ref, m_ref, v_ref, d_ref, m2_ref, v2_ref):
        gv = g_ref[...]
        m2 = ADAM_B1 * m_ref[...] + (1.0 - ADAM_B1) * gv
        v2 = ADAM_B2 * v_ref[...] + (1.0 - ADAM_B2) * (gv * gv)
        m_hat = m2 / (1.0 - ADAM_B1 ** ADAM_STEP)
        v_hat = v2 / (1.0 - ADAM_B2 ** ADAM_STEP)
        d_ref[...] = -ADAM_LR * (m_hat / (jnp.sqrt(v_hat) + ADAM_EPS) + ADAM_WD * w_ref[...])
        m2_ref[...] = m2
        v2_ref[...] = v2

    spec = pl.BlockSpec((tr, cols), lambda i: (i, 0))
    outs = pl.pallas_call(
        body,
        name=name,
        grid=(rows // tr,),
        in_specs=[spec] * 4,
        out_specs=[spec] * 3,
        out_shape=[jax.ShapeDtypeStruct((rows, cols), F32)] * 3,
        compiler_params=_params(("arbitrary",)),
    )(*[t.reshape(rows, cols) for t in (w, g, m, v)])
    return tuple(o.reshape(shape) for o in outs)


MESH_ID = pl.DeviceIdType.MESH
ANY = pl.BlockSpec(memory_space=pl.ANY)


def _place():
    x, y, c = lax.axis_index("x"), lax.axis_index("y"), lax.axis_index("c")
    other_chips = [(1 - x, y), (x, 1 - y), (1 - x, 1 - y)]
    return x, y, c, other_chips


def all_gather_shards(bufs, name):
    n = len(bufs)

    def body(*refs):
        x_refs, o_refs = refs[:n], refs[n:2 * n]
        send_sems, recv_sems = refs[2 * n:]
        x, y, c, chips = _place()
        me = 2 * x + y
        sibling = (x, y, 1 - c)

        def part(a, slot, hf):
            half = bufs[a].shape[0] // 2
            return o_refs[a].at[slot, pl.ds(hf * half, half), :]

        def mine(a):
            half = bufs[a].shape[0] // 2
            return x_refs[a].at[pl.ds(c * half, half), :]

        def copy(k, src, dst, to):
            return pltpu.make_async_remote_copy(src_ref=src, dst_ref=dst, send_sem=send_sems.at[k],
                                                recv_sem=recv_sems.at[k], device_id=to, device_id_type=MESH_ID)

        sent = [copy(6 * a + j, mine(a), part(a, me, c), (px, py, c)) for a in range(n) for j, (px, py) in enumerate(chips)]
        for cp in sent:
            cp.start()
        for a in range(n):
            for j, (px, py) in enumerate(chips):
                landed = part(a, 2 * px + py, c)
                copy(6 * a + j, mine(a), landed, (px, py, c)).wait_recv()
                passed = copy(6 * a + 3 + j, landed, landed, sibling)
                passed.start()
                sent.append(passed)
        for a in range(n):
            for j, (px, py) in enumerate(chips):
                theirs = part(a, 2 * px + py, 1 - c)
                copy(6 * a + 3 + j, theirs, theirs, sibling).wait_recv()
        for cp in sent:
            cp.wait_send()

    outs = pl.pallas_call(
        body,
        name=name,
        in_specs=[ANY] * n,
        out_specs=[ANY] * n,
        out_shape=[jax.ShapeDtypeStruct((4,) + b.shape, b.dtype) for b in bufs],
        scratch_shapes=[pltpu.SemaphoreType.DMA((6 * n,)), pltpu.SemaphoreType.DMA((6 * n,))],
    )(*bufs)
    me = 2 * lax.axis_index("x") + lax.axis_index("y")
    return [lax.dynamic_update_index_in_dim(o, b, me, 0) for o, b in zip(outs, bufs)]


def swap_halves(bufs, name):
    n = len(bufs)

    def body(*refs):
        x_refs, o_refs = refs[:n], refs[n:2 * n]
        send_sems, recv_sems = refs[2 * n:]
        x, y, c, _ = _place()
        copies = []
        for a in range(n):
            half = bufs[a].shape[1] // 2
            cp = pltpu.make_async_remote_copy(src_ref=x_refs[a].at[:, pl.ds((1 - c) * half, half), :], dst_ref=o_refs[a],
                                              send_sem=send_sems.at[a], recv_sem=recv_sems.at[a],
                                              device_id=(x, y, 1 - c), device_id_type=MESH_ID)
            cp.start()
            copies.append(cp)
        for cp in copies:
            cp.wait()

    return pl.pallas_call(
        body,
        name=name,
        in_specs=[ANY] * n,
        out_specs=[ANY] * n,
        out_shape=[jax.ShapeDtypeStruct((4, b.shape[1] // 2, b.shape[2]), b.dtype) for b in bufs],
        scratch_shapes=[pltpu.SemaphoreType.DMA((n,)), pltpu.SemaphoreType.DMA((n,))],
    )(*bufs)


def scatter_to_chips(bufs, name):
    n = len(bufs)

    def body(*refs):
        x_refs, o_refs = refs[:n], refs[n:2 * n]
        send_sems, recv_sems = refs[2 * n:]
        x, y, c, chips = _place()
        me = 2 * x + y

        def copy(a, j, src_slot, dst_slot, px, py):
            return pltpu.make_async_remote_copy(src_ref=x_refs[a].at[src_slot], dst_ref=o_refs[a].at[dst_slot],
                                                send_sem=send_sems.at[3 * a + j], recv_sem=recv_sems.at[3 * a + j],
                                                device_id=(px, py, c), device_id_type=MESH_ID)

        sent = [copy(a, j, 2 * px + py, me, px, py) for a in range(n) for j, (px, py) in enumerate(chips)]
        for cp in sent:
            cp.start()
        for a in range(n):
            for j, (px, py) in enumerate(chips):
                copy(a, j, me, 2 * px + py, px, py).wait_recv()
        for cp in sent:
            cp.wait_send()

    outs = pl.pallas_call(
        body,
        name=name,
        in_specs=[ANY] * n,
        out_specs=[ANY] * n,
        out_shape=[jax.ShapeDtypeStruct(b.shape, b.dtype) for b in bufs],
        scratch_shapes=[pltpu.SemaphoreType.DMA((3 * n,)), pltpu.SemaphoreType.DMA((3 * n,))],
    )(*bufs)
    me = 2 * lax.axis_index("x") + lax.axis_index("y")
    return [lax.dynamic_update_index_in_dim(o, lax.dynamic_index_in_dim(b, me, 0, keepdims=False), me, 0)
            for o, b in zip(outs, bufs)]


def share_halves(groups, name):
    bufs = [b for grp in groups for b in grp]
    where = [(gi, li) for gi, grp in enumerate(groups) for li in range(len(grp))]
    n = len(bufs)

    def body(*refs):
        x_refs, o_refs = refs[:n], refs[n:n + len(groups)]
        send_sems, recv_sems = refs[n + len(groups):]
        x, y, c, _ = _place()
        sent, arrive = [], []
        for a, (gi, li) in enumerate(where):

            def copy(hf, a=a, gi=gi, li=li):
                return pltpu.make_async_remote_copy(src_ref=x_refs[a], dst_ref=o_refs[gi].at[li, hf],
                                                    send_sem=send_sems.at[a], recv_sem=recv_sems.at[a],
                                                    device_id=(x, y, 1 - c), device_id_type=MESH_ID)

            sent.append(copy(c))
            arrive.append(copy(1 - c))
        for cp in sent:
            cp.start()
        for cp in arrive:
            cp.wait_recv()
        for cp in sent:
            cp.wait_send()

    outs = pl.pallas_call(
        body,
        name=name,
        in_specs=[ANY] * n,
        out_specs=[ANY] * len(groups),
        out_shape=[jax.ShapeDtypeStruct((len(grp), 2) + grp[0].shape, grp[0].dtype) for grp in groups],
        scratch_shapes=[pltpu.SemaphoreType.DMA((n,)), pltpu.SemaphoreType.DMA((n,))],
    )(*bufs)
    c = lax.axis_index("c")
    full = [lax.dynamic_update_index_in_dim(o, jnp.stack(grp), c, 1) for o, grp in zip(outs, groups)]
    return [t.reshape(t.shape[0], 2 * t.shape[2], t.shape[3]) for t in full]


def reduce_scatter_grads(groups, travel_dtypes):
    bufs = [b for grp in groups for b in grp]
    dtypes = [dt for grp, dt in zip(groups, travel_dtypes) for _ in grp]
    core = lax.axis_index("c").astype(jnp.int32).reshape(1)
    landed = swap_halves(bufs, "rs_pair")
    sums = [pair_sum(b, l, core, dt, "rs_pair_sum%d" % i) for i, (b, l, dt) in enumerate(zip(bufs, landed, dtypes))]
    from_chips = scatter_to_chips(sums, "rs_chips")
    totals = [chip_sum(t, "rs_chip_sum%d" % i) for i, t in enumerate(from_chips)]
    it = iter(totals)
    return share_halves([[next(it) for _ in grp] for grp in groups], "rs_share")


def _row_tiles(length):
    return (640, 320) if length > 2048 else (128, 64)


def _local_step(x, target, wt):
    seq, d = x.shape
    length = N_META + seq
    tm, tm_ffn = _row_tiles(length)
    lp = -(-length // tm) * tm
    tail = jnp.zeros((lp - length, d), F32)
    h0 = jnp.concatenate([wt["meta"], x, tail], axis=0)[None]
    tgt = jnp.concatenate([jnp.zeros((N_META, d), F32), target, tail], axis=0)
    nn = functools.partial(mm_nn, tm=tm)
    nt = functools.partial(mm_nt, tm=tm)
    tn = functools.partial(mm_tn, tm=tm)
    ln_g = [wt["ln_mix_g"][0:1], wt["ln_ffn_g"][0:1], wt["ln_mix_g"][1:2], wt["ln_ffn_g"][1:2]]
    ln_b = [wt["ln_mix_b"][0:1], wt["ln_ffn_b"][0:1], wt["ln_mix_b"][1:2], wt["ln_ffn_b"][1:2]]

    p5 = nn(h0, wt["a5"], name="a_in5")
    pz = nn(h0, wt["az"], name="a_inz")
    qkvbg = gdn_pre_fwd(p5, wt["a_conv3"], wt["alog_b"], wt["dtb_b"], tm=tm, cb=2 * HEAD_DIM)
    o, states, tinv = gdn_chunk_fwd(qkvbg)
    onz = gdn_post_fwd(o[None], pz, wt["anorm_b"], tm=tm)
    r1, h1 = ln_fwd(h0, nn(onz, wt["a_out"], name="a_out"), ln_g[0], ln_b[0], tm=tm, name="ln1")
    up0 = nn(h1, wt["up"][0], name="up0")
    act0 = ffn_act_fwd(up0, wt["fconv"][0], tm=tm_ffn, name="ffn_act0")
    r2, h2 = ln_fwd(h1, nn(act0, wt["down"][0], name="down0"), ln_g[1], ln_b[1], tm=tm, name="ln2")
    pb = nn(h2, wt["b_in"], name="b_in")
    sc = sc_fwd(pb, wt["b_conv"], tm=tm, cb=4 * HEAD_DIM)
    r3, h3 = ln_fwd(h2, nn(sc, wt["b_out"], name="b_out"), ln_g[2], ln_b[2], tm=tm, name="ln3")
    up1 = nn(h3, wt["up"][1], name="up1")
    act1 = ffn_act_fwd(up1, wt["fconv"][1], tm=tm_ffn, name="ffn_act1")
    r4, h4 = ln_fwd(h3, nn(act1, wt["down"][1], name="down1"), ln_g[3], ln_b[3], tm=tm, name="ln4")

    dh4, loss_part = loss_grad(h4, tgt, first=N_META, count=seq, tm=tm)

    grads = {}
    dr4, dgb4 = ln_bwd(r4, dh4, ln_g[3], tm=tm, name="ln4_bwd")
    d_down1 = tn(act1, dr4, name="d_down1")
    dact1 = nt(dr4, wt["down"][1], name="d_act1")
    dup1, dfconv1 = ffn_act_bwd(up1, dact1, wt["fconv"][1], tm=tm_ffn, name="ffn_act1_bwd")
    dup1 = dup1.reshape(up1.shape)
    d_up1 = tn(h3, dup1, name="d_up1")
    dh3 = nt(dup1, wt["up"][1], res=dr4, res_scale=ALPHA, name="d_h3")

    dr3, dgb3 = ln_bwd(r3, dh3, ln_g[2], tm=tm, name="ln3_bwd")
    d_bout = tn(sc, dr3, name="d_b_out")
    dsc = nt(dr3, wt["b_out"], name="d_sc")
    dpb, dbconv = sc_bwd(pb, dsc, wt["b_conv"], tm=tm, cb=4 * HEAD_DIM)
    d_bin = tn(h2, dpb, name="d_b_in")
    dh2 = nt(dpb, wt["b_in"], res=dr3, res_scale=ALPHA, name="d_h2")

    dr2, dgb2 = ln_bwd(r2, dh2, ln_g[1], tm=tm, name="ln2_bwd")
    d_down0 = tn(act0, dr2, name="d_down0")
    dact0 = nt(dr2, wt["down"][0], name="d_act0")
    dup0, dfconv0 = ffn_act_bwd(up0, dact0, wt["fconv"][0], tm=tm_ffn, name="ffn_act0_bwd")
    dup0 = dup0.reshape(up0.shape)
    d_up0 = tn(h1, dup0, name="d_up0")
    dh1 = nt(dup0, wt["up"][0], res=dr2, res_scale=ALPHA, name="d_h1")

    dr1, dgb1 = ln_bwd(r1, dh1, ln_g[0], tm=tm, name="ln1_bwd")
    d_aout = tn(onz, dr1, name="d_a_out")
    donz = nt(dr1, wt["a_out"], name="d_onz")
    d_o, dz, dnw = gdn_post_bwd(o[None], pz, donz, wt["anorm_b"], tm=tm)
    dqkvbg = gdn_chunk_bwd(qkvbg, states, tinv, d_o[0])
    dp5, daconv, dscal = gdn_pre_bwd(p5, dqkvbg, wt["a_conv3"], wt["alog_b"], wt["dtb_b"], tm=tm, cb=2 * HEAD_DIM)
    d_a5 = tn(h0, dp5, name="d_a_in5")
    d_az = tn(h0, dz, name="d_a_inz")
    dh0 = nt(dp5, wt["a5"], res=dr1, res_scale=ALPHA, name="d_h0a")
    dh0 = nt(dz, wt["az"], res=dh0, res_scale=1.0, name="d_h0")

    width = HEADS * HEAD_DIM
    d_ba = head_lane_sum(d_a5[0, 3:5])[:, :, :HEADS]
    d_a_in = jnp.concatenate([d_a5[0, 0], d_a5[0, 1], d_a5[0, 2], d_az[0, 0], d_ba[0], d_ba[1]], axis=1)
    n_in = d_a_in.shape[1] // 4
    grads["a_w_in"] = [d_a_in.reshape(d, 4, n_in).transpose(1, 0, 2)]
    grads["a_w_out"] = [d_aout.reshape(4, width // 4, d)]
    grads["b_w_in"] = [d_bin[0].transpose(1, 0, 2).reshape(d, 4, 3 * d // 4).transpose(1, 0, 2)]
    grads["b_w_out"] = [d_bout.reshape(4, d // 4, d)]
    grads["ffn_w_up"] = [d_up0[0], d_up1[0]]
    grads["ffn_w_down"] = [t.reshape(4, -1, d) for t in (d_down0, d_down1)]
    grads["a_conv"] = daconv.sum(axis=2).transpose(1, 0, 2).reshape(1, GDN_CONV, 3 * width)
    per_head = dscal.reshape(2, 8, HEADS, HEAD_DIM).sum(axis=(1, 3))
    grads["a_log"] = per_head[0][None]
    grads["a_dt_bias"] = per_head[1][None]
    grads["a_norm"] = dnw.reshape(8, HEADS, HEAD_DIM).sum(axis=(0, 1))[None]
    grads["b_conv"] = dbconv.sum(axis=1)[None]
    lns = [dgb1, dgb2, dgb3, dgb4]
    grads["ln_mix_g"] = jnp.stack([lns[0][0].sum(0), lns[2][0].sum(0)])
    grads["ln_mix_b"] = jnp.stack([lns[0][1].sum(0), lns[2][1].sum(0)])
    grads["ln_ffn_g"] = jnp.stack([lns[1][0].sum(0), lns[3][0].sum(0)])
    grads["ln_ffn_b"] = jnp.stack([lns[1][1].sum(0), lns[3][1].sum(0)])
    grads["ffn_conv"] = jnp.stack([t.sum(axis=2).transpose(1, 0, 2).reshape(FFN_CONV, -1) for t in (dfconv0, dfconv1)])
    grads["meta"] = dh0[0, :N_META]
    return loss_part, dh0, grads


WEIGHTS = ["meta", "a_w_in", "a_conv", "a_log", "a_dt_bias", "a_norm", "a_w_out", "b_w_in", "b_conv", "b_w_out",
           "ln_mix_g", "ln_mix_b", "ffn_w_up", "ffn_conv", "ffn_w_down", "ln_ffn_g", "ln_ffn_b"]
MATMUL_WEIGHTS = ["a_w_in", "a_w_out", "b_w_in", "b_w_out", "ffn_w_up", "ffn_w_down"]
SMALL_SHARDED = ["a_conv", "b_conv", "ffn_conv", "meta"]
REPLICATED = ["a_log", "a_dt_bias", "a_norm", "ln_mix_g", "ln_mix_b", "ln_ffn_g", "ln_ffn_b"]
SHARD_AXIS = {"meta": 1, "a_w_in": 2, "a_conv": 2, "a_w_out": 1, "b_w_in": 2, "b_conv": 2, "b_w_out": 1,
              "ffn_w_up": 2, "ffn_conv": 2, "ffn_w_down": 1}
PACK_COLS = 1024
PACK_ROWS_MULTIPLE = 32


def _pack(pieces, lead=()):
    flat = jnp.concatenate([p.reshape(lead + (-1,)) for p in pieces], axis=-1)
    n = flat.shape[-1]
    rows = -(-n // (PACK_COLS * PACK_ROWS_MULTIPLE)) * PACK_ROWS_MULTIPLE
    flat = jnp.pad(flat, [(0, 0)] * len(lead) + [(0, rows * PACK_COLS - n)])
    return flat.reshape(lead + (rows, PACK_COLS))


def _unpack(buf, shapes, lead=()):
    flat = buf.reshape(lead + (-1,))
    out, off = [], 0
    for shp in shapes:
        n = 1
        for s in shp:
            n *= s
        out.append(flat[..., off:off + n].reshape(lead + tuple(shp)))
        off += n
    return out


def _join_shards(stacked, axis):
    return jnp.concatenate([stacked[k] for k in range(4)], axis=axis)


def _split_shards(full, axis):
    return jnp.stack(jnp.split(full, 4, axis=axis))


def _gather_weights(w):
    layers = [w[n][l].astype(BF16) for n in MATMUL_WEIGHTS for l in range(w[n].shape[0])]
    *stacked, small = all_gather_shards(layers + [_pack([w[n] for n in SMALL_SHARDED])], "gather_weights")
    full, it = {}, iter(stacked)
    for n in MATMUL_WEIGHTS:
        full[n] = [next(it) for _ in range(w[n].shape[0])]
    for n, t in zip(SMALL_SHARDED, _unpack(small, [w[n].shape for n in SMALL_SHARDED], lead=(4,))):
        full[n] = _join_shards(t, SHARD_AXIS[n])
    return _layout_weights(full, w)


def _layout_weights(full, w):
    width = HEADS * HEAD_DIM
    wt = {n: w[n] for n in ("ln_mix_g", "ln_mix_b", "ln_ffn_g", "ln_ffn_b")}
    w_in = _join_shards(full["a_w_in"][0], 1)
    d = w_in.shape[0]
    blocks = [w_in[:, s * width:(s + 1) * width] for s in range(4)]
    b_exp = jnp.repeat(w_in[:, 4 * width:4 * width + HEADS], HEAD_DIM, axis=1)
    a_exp = jnp.repeat(w_in[:, 4 * width + HEADS:], HEAD_DIM, axis=1)
    wt["a5"] = jnp.stack([blocks[0], blocks[1], blocks[2], b_exp, a_exp])[None]
    wt["az"] = blocks[3][None, None]
    wt["a_out"] = full["a_w_out"][0].reshape(1, 1, width, d)
    wt["b_in"] = _join_shards(full["b_w_in"][0], 1).reshape(d, 3, d).transpose(1, 0, 2)[None]
    wt["b_out"] = full["b_w_out"][0].reshape(1, 1, d, d)
    n_ff = full["ffn_w_up"][0].shape[2]
    wt["up"] = [t[None] for t in full["ffn_w_up"]]
    wt["down"] = [t.reshape(2, 1, n_ff, d) for t in full["ffn_w_down"]]
    wt["a_conv3"] = full["a_conv"][0].reshape(GDN_CONV, 3, width).transpose(1, 0, 2)
    wt["b_conv"] = full["b_conv"][0]
    wt["fconv"] = [full["ffn_conv"][l].reshape(FFN_CONV, 2, n_ff).transpose(1, 0, 2) for l in range(2)]
    wt["meta"] = full["meta"]
    wt["alog_b"] = jnp.repeat(w["a_log"][0], HEAD_DIM)[None]
    wt["dtb_b"] = jnp.repeat(w["a_dt_bias"][0], HEAD_DIM)[None]
    wt["anorm_b"] = jnp.tile(w["a_norm"][0], HEADS)[None]
    return wt


def _reduce_grads(grads, loss_part, w):
    pieces = [_split_shards(grads[n], SHARD_AXIS[n]) for n in SMALL_SHARDED]
    same = jnp.concatenate([grads[n].reshape(-1) for n in REPLICATED] + [jnp.sum(loss_part).reshape(1)])
    pieces.append(jnp.broadcast_to(same, (4,) + same.shape))
    groups = [grads[n] for n in MATMUL_WEIGHTS] + [[_pack(pieces, lead=(4,))]]
    *totals, small = reduce_scatter_grads(groups, [BF16] * len(MATMUL_WEIGHTS) + [F32])
    out = {n: t.reshape(w[n].shape) for n, t in zip(MATMUL_WEIGHTS, totals)}
    rest = SMALL_SHARDED + REPLICATED
    unpacked = _unpack(small[0], [w[n].shape for n in rest] + [()])
    out.update(zip(rest, unpacked[:-1]))
    return out, unpacked[-1]


def kernel(x, meta, a_w_in, a_conv, a_log, a_dt_bias, a_norm, a_w_out, b_w_in, b_conv, b_w_out, ln_mix_g, ln_mix_b, ffn_w_up, ffn_conv, ffn_w_down, ln_ffn_g, ln_ffn_b, loss_target, m_meta, m_a_w_in, m_a_conv, m_a_log, m_a_dt_bias, m_a_norm, m_a_w_out, m_b_w_in, m_b_conv, m_b_w_out, m_ln_mix_g, m_ln_mix_b, m_ffn_w_up, m_ffn_conv, m_ffn_w_down, m_ln_ffn_g, m_ln_ffn_b, v_meta, v_a_w_in, v_a_conv, v_a_log, v_a_dt_bias, v_a_norm, v_a_w_out, v_b_w_in, v_b_conv, v_b_w_out, v_ln_mix_g, v_ln_mix_b, v_ffn_w_up, v_ffn_conv, v_ffn_w_down, v_ln_ffn_g, v_ln_ffn_b):
    w = dict(meta=meta, a_w_in=a_w_in, a_conv=a_conv, a_log=a_log, a_dt_bias=a_dt_bias, a_norm=a_norm, a_w_out=a_w_out,
             b_w_in=b_w_in, b_conv=b_conv, b_w_out=b_w_out, ln_mix_g=ln_mix_g, ln_mix_b=ln_mix_b, ffn_w_up=ffn_w_up,
             ffn_conv=ffn_conv, ffn_w_down=ffn_w_down, ln_ffn_g=ln_ffn_g, ln_ffn_b=ln_ffn_b)
    m = dict(meta=m_meta, a_w_in=m_a_w_in, a_conv=m_a_conv, a_log=m_a_log, a_dt_bias=m_a_dt_bias, a_norm=m_a_norm,
             a_w_out=m_a_w_out, b_w_in=m_b_w_in, b_conv=m_b_conv, b_w_out=m_b_w_out, ln_mix_g=m_ln_mix_g,
             ln_mix_b=m_ln_mix_b, ffn_w_up=m_ffn_w_up, ffn_conv=m_ffn_conv, ffn_w_down=m_ffn_w_down,
             ln_ffn_g=m_ln_ffn_g, ln_ffn_b=m_ln_ffn_b)
    v = dict(meta=v_meta, a_w_in=v_a_w_in, a_conv=v_a_conv, a_log=v_a_log, a_dt_bias=v_a_dt_bias, a_norm=v_a_norm,
             a_w_out=v_a_w_out, b_w_in=v_b_w_in, b_conv=v_b_conv, b_w_out=v_b_w_out, ln_mix_g=v_ln_mix_g,
             ln_mix_b=v_ln_mix_b, ffn_w_up=v_ffn_w_up, ffn_conv=v_ffn_conv, ffn_w_down=v_ffn_w_down,
             ln_ffn_g=v_ln_ffn_g, ln_ffn_b=v_ln_ffn_b)
    seq = x.shape[1]
    wt = _gather_weights(w)
    loss_part, dh0, grads = _local_step(x[0], loss_target[0], wt)
    grad_w, loss = _reduce_grads(grads, loss_part, w)
    grad_x = dh0[:, N_META:N_META + seq]
    steps = [adamw(w[n], grad_w[n], m[n], v[n], "adamw_" + n) for n in WEIGHTS]
    return (loss, grad_x, *[grad_w[n] for n in WEIGHTS], *[s[0] for s in steps], *[s[1] for s in steps],
            *[s[2] for s in steps])
```

```python
import functools

import jax
import jax.numpy as jnp
from jax import lax
from jax.experimental import pallas as pl
from jax.experimental.pallas import tpu as pltpu

F32 = jnp.float32
BF16 = jnp.bfloat16
HI = lax.Precision.HIGHEST

N_META = 16
HEADS = 8
HEAD_DIM = 128
CHUNK = 64
GDN_CONV = 4
SC_CONV = 3
FFN_CONV = 3
ALPHA = 4.0 ** 0.25
LN_EPS = 1e-5
RMS_EPS = 1e-6
L2_EPS = 1e-6
Q_SCALE = HEAD_DIM ** -0.5

ADAM_LR = 0.001
ADAM_B1 = 0.9
ADAM_B2 = 0.999
ADAM_EPS = 1e-08
ADAM_WD = 0.01
ADAM_STEP = 10

HALO = 8
VMEM_LIMIT = 48 * 1024 * 1024


def _params(sem=None):
    return pltpu.CompilerParams(dimension_semantics=sem, vmem_limit_bytes=VMEM_LIMIT)


def _dot(a, b, prec=None):
    return jnp.dot(a, b, preferred_element_type=F32, precision=prec)


def _dot_nt(a, b, prec=None):
    return lax.dot_general(a, b, (((1,), (1,)), ((), ())), preferred_element_type=F32, precision=prec)


def _dot_tn(a, b, prec=None):
    return lax.dot_general(a, b, (((0,), (0,)), ((), ())), preferred_element_type=F32, precision=prec)


def _sigmoid(x):
    return 1.0 / (1.0 + jnp.exp(-x))


def _tri_masks():
    r = lax.broadcasted_iota(jnp.int32, (CHUNK, CHUNK), 0)
    c = lax.broadcasted_iota(jnp.int32, (CHUNK, CHUNK), 1)
    return r >= c, r > c, r == c


def _split_hi_lo(x):
    hi = x.astype(BF16)
    return hi, (x - hi.astype(F32)).astype(BF16)


def _mask_dot(mask, x):
    hi, lo = _split_hi_lo(x)
    return _dot(mask, hi) + _dot(mask, lo)


@jax.custom_vjp
def _cumsum_rows(g):
    causal, _, _ = _tri_masks()
    return _mask_dot(causal.astype(BF16), g)


def _cumsum_rows_fwd(g):
    return _cumsum_rows(g), None


def _cumsum_rows_bwd(_, dy):
    _, strict, _ = _tri_masks()
    return (_mask_dot((~strict).astype(BF16), dy),)


_cumsum_rows.defvjp(_cumsum_rows_fwd, _cumsum_rows_bwd)


def _dot_split3(a, b):
    a_hi, a_lo = _split_hi_lo(a)
    b_hi, b_lo = _split_hi_lo(b)
    return _dot(a_hi, b_hi) + (_dot(a_hi, b_lo) + _dot(a_lo, b_hi))


def _gdn_m(ks, g64s, bbs):
    causal, strict, _ = _tri_masks()
    a = [_cumsum_rows(g) for g in g64s]
    decay = [jnp.exp(jnp.where(causal, x - x.T, -1e30)) for x in a]
    kk = [_dot_nt(k * b, k) for k, b in zip(ks, bbs)]
    return [jnp.where(strict, x * d, 0.0) for x, d in zip(kk, decay)]


def _gdn_inverse(ms):
    r = lax.broadcasted_iota(jnp.int32, (CHUNK, CHUNK), 0)
    c = lax.broadcasted_iota(jnp.int32, (CHUNK, CHUNK), 1)
    eye = (r == c).astype(F32)
    same = [jnp.right_shift(r, s) == jnp.right_shift(c, s) for s in (3, 4, 5)]
    d = [jnp.where(same[0], m, 0.0) for m in ms]
    p = [_dot(x, x) for x in d]
    t = [eye - x for x in d]
    t = [x + _dot(x, y) for x, y in zip(t, p)]
    p = [_dot(x, x) for x in p]
    t = [x + _dot(x, y) for x, y in zip(t, p)]
    for inner, outer in ((same[0], same[1]), (same[1], same[2]), (same[2], None)):
        joins = ~inner if outer is None else (outer & ~inner)
        o = [_dot(x, jnp.where(joins, m, 0.0)) for x, m in zip(t, ms)]
        t = [x - _dot(y, x) for x, y in zip(t, o)]
    res = [eye - x - _dot_split3(m, x) for m, x in zip(ms, t)]
    return [x + _dot(x, y) for x, y in zip(t, res)]


def _gdn_apply(qs, ks, vs, gbs, g64s, bbs, ss, ts):
    causal, _, _ = _tri_masks()
    n = range(len(qs))
    gc = [_cumsum_rows(g) for g in gbs]
    a = [_cumsum_rows(g) for g in g64s]
    decay = [jnp.exp(jnp.where(causal, x - x.T, -1e30)) for x in a]
    eg = [jnp.exp(x) for x in gc]
    u = [_dot(ts[h], vs[h] * bbs[h]) for h in n]
    w = [_dot(ts[h], ks[h] * bbs[h] * eg[h]) for h in n]
    qk = [_dot_nt(qs[h], ks[h]) * decay[h] for h in n]
    gl = [jnp.sum(g, axis=0, keepdims=True) for g in gbs]
    kd = [ks[h] * jnp.exp(gl[h] - gc[h]) for h in n]
    v_new = [u[h] - _dot(w[h], ss[h]) for h in n]
    o = [_dot(qs[h] * eg[h], ss[h]) + _dot(qk[h], v_new[h]) for h in n]
    s2 = [ss[h] * jnp.exp(gl[h]) + _dot_tn(kd[h], v_new[h]) for h in n]
    return o, s2


def _head_slices(h):
    return slice(h * HEAD_DIM, (h + 1) * HEAD_DIM), slice(h * HEAD_DIM, h * HEAD_DIM + CHUNK)


def _gdn_head_values(x_ref):
    out = [[], [], [], [], [], []]
    for h in range(HEADS):
        sl, sl64 = _head_slices(h)
        for lst, val in zip(out, (x_ref[0, :, sl], x_ref[1, :, sl], x_ref[2, :, sl], x_ref[4, :, sl],
                                  x_ref[4, :, sl64], x_ref[3, :, sl])):
            lst.append(val)
    return out


def gdn_chunk_fwd(qkvbg):
    _, lp, width = qkvbg.shape
    n_chunks = lp // CHUNK

    def body(x_ref, o_ref, s_ref, t_ref, state):
        @pl.when(pl.program_id(0) == 0)
        def _():
            state[...] = jnp.zeros_like(state)

        qs, ks, vs, gbs, g64s, bbs = _gdn_head_values(x_ref)
        ss = [state[h] for h in range(HEADS)]
        ts = _gdn_inverse(_gdn_m(ks, g64s, bbs))
        os_, s2 = _gdn_apply(qs, ks, vs, gbs, g64s, bbs, ss, ts)
        for h in range(HEADS):
            s_ref[0, h] = ss[h]
            t_ref[0, h] = ts[h]
            o_ref[:, _head_slices(h)[0]] = os_[h]
            state[h] = s2[h]

    return pl.pallas_call(
        body,
        name="gdn_chunk_fwd",
        grid=(n_chunks,),
        in_specs=[pl.BlockSpec((5, CHUNK, width), lambda c: (0, c, 0))],
        out_specs=[
            pl.BlockSpec((CHUNK, width), lambda c: (c, 0)),
            pl.BlockSpec((1, HEADS, HEAD_DIM, HEAD_DIM), lambda c: (c, 0, 0, 0)),
            pl.BlockSpec((1, HEADS, CHUNK, CHUNK), lambda c: (c, 0, 0, 0)),
        ],
        out_shape=[
            jax.ShapeDtypeStruct((lp, width), F32),
            jax.ShapeDtypeStruct((n_chunks, HEADS, HEAD_DIM, HEAD_DIM), F32),
            jax.ShapeDtypeStruct((n_chunks, HEADS, CHUNK, CHUNK), F32),
        ],
        scratch_shapes=[pltpu.VMEM((HEADS, HEAD_DIM, HEAD_DIM), F32)],
        compiler_params=_params(("arbitrary",)),
    )(qkvbg)


def gdn_chunk_bwd(qkvbg, states, tinv, d_o):
    _, lp, width = qkvbg.shape
    n_chunks = lp // CHUNK
    last = n_chunks - 1

    def body(x_ref, s_ref, t_ref, do_ref, dx_ref, dstate):
        @pl.when(pl.program_id(0) == 0)
        def _():
            dstate[...] = jnp.zeros_like(dstate)

        heads = range(HEADS)
        qs, ks, vs, gbs, g64s, bbs = _gdn_head_values(x_ref)
        ss = [s_ref[0, h] for h in heads]
        ts = [t_ref[0, h] for h in heads]
        d_out = ([do_ref[:, _head_slices(h)[0]] for h in heads], [dstate[h] for h in heads])
        _, vjp_apply = jax.vjp(_gdn_apply, qs, ks, vs, gbs, g64s, bbs, ss, ts)
        dq, dk, dv, dgb, dg64, dbb, ds, dt = vjp_apply(d_out)
        tts = [t.T for t in ts]
        dm = [_dot(tts[h], dt[h]) for h in heads]
        dm = [-_dot(dm[h], tts[h]) for h in heads]
        _, vjp_m = jax.vjp(_gdn_m, ks, g64s, bbs)
        dk2, dg64m, dbb2 = vjp_m(dm)
        for h in heads:
            sl, sl64 = _head_slices(h)
            dx_ref[0, :, sl] = dq[h]
            dx_ref[1, :, sl] = dk[h] + dk2[h]
            dx_ref[2, :, sl] = dv[h]
            dx_ref[3, :, sl] = dbb[h] + dbb2[h]
            dx_ref[4, :, sl] = dgb[h]
            dx_ref[4, :, sl64] += dg64[h] + dg64m[h]
            dstate[h] = ds[h]

    return pl.pallas_call(
        body,
        name="gdn_chunk_bwd",
        grid=(n_chunks,),
        in_specs=[
            pl.BlockSpec((5, CHUNK, width), lambda c: (0, last - c, 0)),
            pl.BlockSpec((1, HEADS, HEAD_DIM, HEAD_DIM), lambda c: (last - c, 0, 0, 0)),
            pl.BlockSpec((1, HEADS, CHUNK, CHUNK), lambda c: (last - c, 0, 0, 0)),
            pl.BlockSpec((CHUNK, width), lambda c: (last - c, 0)),
        ],
        out_specs=pl.BlockSpec((5, CHUNK, width), lambda c: (0, last - c, 0)),
        out_shape=jax.ShapeDtypeStruct(qkvbg.shape, F32),
        scratch_shapes=[pltpu.VMEM((HEADS, HEAD_DIM, HEAD_DIM), F32)],
        compiler_params=_params(("arbitrary",)),
    )(qkvbg, states, tinv, d_o)


def mm_nn(a, b, *, tm, name):
    ks, m, tk = a.shape
    _, ns, _, tn = b.shape

    def body(a_ref, b_ref, o_ref):
        p = _dot(a_ref[...].astype(BF16), b_ref[...])

        @pl.when(pl.program_id(2) == 0)
        def _():
            o_ref[...] = p

        @pl.when(pl.program_id(2) > 0)
        def _():
            o_ref[...] += p

    return pl.pallas_call(
        body,
        name=name,
        grid=(ns, m // tm, ks),
        in_specs=[
            pl.BlockSpec((None, tm, tk), lambda n, i, k: (k, i, 0)),
            pl.BlockSpec((None, None, tk, tn), lambda n, i, k: (k, n, 0, 0)),
        ],
        out_specs=pl.BlockSpec((None, tm, tn), lambda n, i, k: (n, i, 0)),
        out_shape=jax.ShapeDtypeStruct((ns, m, tn), F32),
        compiler_params=_params(("arbitrary", "arbitrary", "arbitrary")),
    )(a, b)


def mm_nt(dy, w, *, tm, name, res=None, res_scale=1.0):
    ns, m, tn = dy.shape
    ks, _, tk, _ = w.shape

    def body(*refs):
        if res is None:
            dy_ref, w_ref, o_ref = refs
        else:
            dy_ref, w_ref, r_ref, o_ref = refs
        p = _dot_nt(dy_ref[...].astype(BF16), w_ref[...])

        @pl.when(pl.program_id(2) == 0)
        def _():
            o_ref[...] = p if res is None else p + res_scale * r_ref[...]

        @pl.when(pl.program_id(2) > 0)
        def _():
            o_ref[...] += p

    in_specs = [
        pl.BlockSpec((None, tm, tn), lambda k, i, n: (n, i, 0)),
        pl.BlockSpec((None, None, tk, tn), lambda k, i, n: (k, n, 0, 0)),
    ]
    args = [dy, w]
    if res is not None:
        in_specs.append(pl.BlockSpec((None, tm, tk), lambda k, i, n: (k, i, 0)))
        args.append(res)
    return pl.pallas_call(
        body,
        name=name,
        grid=(ks, m // tm, ns),
        in_specs=in_specs,
        out_specs=pl.BlockSpec((None, tm, tk), lambda k, i, n: (k, i, 0)),
        out_shape=jax.ShapeDtypeStruct((ks, m, tk), F32),
        compiler_params=_params(("arbitrary", "arbitrary", "arbitrary")),
    )(*args)


def mm_tn(x, dy, *, tm, name):
    ks, m, tk = x.shape
    ns, _, tn = dy.shape

    def body(x_ref, dy_ref, o_ref):
        p = _dot_tn(x_ref[...].astype(BF16), dy_ref[...].astype(BF16))

        @pl.when(pl.program_id(2) == 0)
        def _():
            o_ref[...] = p

        @pl.when(pl.program_id(2) > 0)
        def _():
            o_ref[...] += p

    return pl.pallas_call(
        body,
        name=name,
        grid=(ks, ns, m // tm),
        in_specs=[
            pl.BlockSpec((None, tm, tk), lambda k, n, i: (k, i, 0)),
            pl.BlockSpec((None, tm, tn), lambda k, n, i: (n, i, 0)),
        ],
        out_specs=pl.BlockSpec((None, None, tk, tn), lambda k, n, i: (k, n, 0, 0)),
        out_shape=jax.ShapeDtypeStruct((ks, ns, tk, tn), F32),
        compiler_params=_params(("arbitrary", "arbitrary", "arbitrary")),
    )(x, dy)


def _row_partial(x):
    rows, c = x.shape
    return jnp.sum(x.reshape(rows // 8, 8, c), axis=0)


def ln_fwd(h_prev, mix, g, b, *, tm, name):
    _, lp, d = h_prev.shape

    def body(h_ref, m_ref, g_ref, b_ref, r_ref, o_ref):
        r = ALPHA * h_ref[...] + m_ref[...]
        mu = jnp.mean(r, axis=-1, keepdims=True)
        xc = r - mu
        var = jnp.mean(xc * xc, axis=-1, keepdims=True)
        r_ref[...] = r
        o_ref[...] = xc * lax.rsqrt(var + LN_EPS) * g_ref[...] + b_ref[...]

    row = pl.BlockSpec((None, tm, d), lambda i: (0, i, 0))
    vec = pl.BlockSpec((1, d), lambda i: (0, 0))
    return pl.pallas_call(
        body,
        name=name,
        grid=(lp // tm,),
        in_specs=[row, row, vec, vec],
        out_specs=[row, row],
        out_shape=[jax.ShapeDtypeStruct((1, lp, d), F32)] * 2,
        compiler_params=_params(("arbitrary",)),
    )(h_prev, mix, g, b)


def ln_bwd(r, dh, g, *, tm, name):
    _, lp, d = r.shape

    def body(r_ref, dh_ref, g_ref, dr_ref, dgb_ref):
        x = r_ref[...]
        dh_v = dh_ref[...]
        mu = jnp.mean(x, axis=-1, keepdims=True)
        xc = x - mu
        rstd = lax.rsqrt(jnp.mean(xc * xc, axis=-1, keepdims=True) + LN_EPS)
        xh = xc * rstd
        dxh = dh_v * g_ref[...]
        m1 = jnp.mean(dxh, axis=-1, keepdims=True)
        m2 = jnp.mean(dxh * xh, axis=-1, keepdims=True)
        dr_ref[...] = rstd * (dxh - m1 - xh * m2)

        @pl.when(pl.program_id(0) == 0)
        def _():
            dgb_ref[...] = jnp.zeros_like(dgb_ref)

        dgb_ref[0] += _row_partial(dh_v * xh)
        dgb_ref[1] += _row_partial(dh_v)

    row = pl.BlockSpec((None, tm, d), lambda i: (0, i, 0))
    return pl.pallas_call(
        body,
        name=name,
        grid=(lp // tm,),
        in_specs=[row, row, pl.BlockSpec((1, d), lambda i: (0, 0))],
        out_specs=[row, pl.BlockSpec((2, 8, d), lambda i: (0, 0, 0))],
        out_shape=[jax.ShapeDtypeStruct((1, lp, d), F32), jax.ShapeDtypeStruct((2, 8, d), F32)],
        compiler_params=_params(("arbitrary",)),
    )(r, dh, g)


def loss_grad(h, target, *, first, count, tm):
    _, lp, d = h.shape

    def body(h_ref, t_ref, dh_ref, l_ref):
        row = pl.program_id(0) * tm + lax.broadcasted_iota(jnp.int32, (tm, d), 0)
        valid = (row >= first) & (row < first + count)
        err = jnp.where(valid, h_ref[...] - t_ref[...], 0.0)
        dh_ref[...] = err * (1.0 / d)

        @pl.when(pl.program_id(0) == 0)
        def _():
            l_ref[...] = jnp.zeros_like(l_ref)

        l_ref[...] += _row_partial(err * err) * (0.5 / d)

    return pl.pallas_call(
        body,
        name="loss_grad",
        grid=(lp // tm,),
        in_specs=[pl.BlockSpec((None, tm, d), lambda i: (0, i, 0)), pl.BlockSpec((tm, d), lambda i: (i, 0))],
        out_specs=[pl.BlockSpec((None, tm, d), lambda i: (0, i, 0)), pl.BlockSpec((8, d), lambda i: (0, 0))],
        out_shape=[jax.ShapeDtypeStruct((1, lp, d), F32), jax.ShapeDtypeStruct((8, d), F32)],
        compiler_params=_params(("arbitrary",)),
    )(h, target)


def _halo_index(tile, tm):
    return jnp.maximum(tile * (tm // HALO) - 1, 0)


def _conv_fwd(xs_ref, w, taps, tm):
    acc = w(0) * xs_ref[pl.ds(HALO - taps + 1, tm), :]
    for j in range(1, taps):
        acc += w(j) * xs_ref[pl.ds(HALO - taps + 1 + j, tm), :]
    return acc


def _conv_bwd_x(dcs_ref, w, taps, tm):
    acc = w(0) * dcs_ref[pl.ds(taps - 1, tm), :]
    for j in range(1, taps):
        acc += w(j) * dcs_ref[pl.ds(taps - 1 - j, tm), :]
    return acc


SUB = 8
LANES = 128


def _shift_down(cur, prev, s):
    if s == 0:
        return cur
    row = lax.broadcasted_iota(jnp.int32, cur.shape, 0)
    return jnp.where(row < s, pltpu.roll(prev, s, axis=0), pltpu.roll(cur, s, axis=0))


def _shift_up(cur, nxt, s):
    if s == 0:
        return cur
    row = lax.broadcasted_iota(jnp.int32, cur.shape, 0)
    return jnp.where(row < SUB - s, pltpu.roll(cur, SUB - s, axis=0), pltpu.roll(nxt, SUB - s, axis=0))


def _silu_parts(c):
    sg = _sigmoid(c)
    return c * sg, sg * (1.0 + c * (1.0 - sg))


def _head_sum(x):
    rows, c = x.shape
    parts = []
    for h in range(c // HEAD_DIM):
        s = jnp.sum(x[:, h * HEAD_DIM:(h + 1) * HEAD_DIM], axis=-1, keepdims=True)
        parts.append(jnp.broadcast_to(s, (rows, HEAD_DIM)))
    return parts[0] if len(parts) == 1 else jnp.concatenate(parts, axis=-1)


def _log1p(y):
    u = 1.0 + y
    d = u - 1.0
    return jnp.where(d == 0.0, y, jnp.log(u) * (y / jnp.where(d == 0.0, 1.0, d)))


def _softplus(x):
    return jnp.maximum(x, 0.0) + _log1p(jnp.exp(-jnp.abs(x)))


def gdn_pre_fwd(p5, conv_w, alog_b, dtb_b, *, tm, cb):
    _, lp, width = p5.shape
    taps = conv_w.shape[1]

    def body(x_ref, halo_ref, w_ref, al_ref, dt_ref, o_ref, xs):
        i = pl.program_id(1)
        for s in range(3):
            xs[s, 0:HALO, :] = jnp.where(i > 0, halo_ref[s], 0.0)
            xs[s, HALO:, :] = x_ref[s]
            c = _conv_fwd(xs.at[s], lambda j, s=s: w_ref[s, j:j + 1, :], taps, tm)
            y, _ = _silu_parts(c)
            if s < 2:
                y = y * lax.rsqrt(_head_sum(y * y) + L2_EPS)
                if s == 0:
                    y = y * Q_SCALE
            o_ref[s] = y
        o_ref[3] = _sigmoid(x_ref[3])
        o_ref[4] = -jnp.exp(al_ref[...]) * _softplus(x_ref[4] + dt_ref[...])

    return pl.pallas_call(
        body,
        name="gdn_pre_fwd",
        grid=(width // cb, lp // tm),
        in_specs=[
            pl.BlockSpec((5, tm, cb), lambda j, i: (0, i, j)),
            pl.BlockSpec((3, HALO, cb), lambda j, i: (0, _halo_index(i, tm), j)),
            pl.BlockSpec((3, taps, cb), lambda j, i: (0, 0, j)),
            pl.BlockSpec((1, cb), lambda j, i: (0, j)),
            pl.BlockSpec((1, cb), lambda j, i: (0, j)),
        ],
        out_specs=pl.BlockSpec((5, tm, cb), lambda j, i: (0, i, j)),
        out_shape=jax.ShapeDtypeStruct((5, lp, width), F32),
        scratch_shapes=[pltpu.VMEM((3, tm + HALO, cb), F32)],
        compiler_params=_params(("arbitrary", "arbitrary")),
    )(p5, p5, conv_w, alog_b, dtb_b)


def gdn_pre_bwd(p5, dqkvbg, conv_w, alog_b, dtb_b, *, tm, cb):
    _, lp, width = p5.shape
    taps = conv_w.shape[1]
    last = lp // tm - 1

    def body(x_ref, halo_ref, d_ref, w_ref, al_ref, dt_ref, dx_ref, dw_ref, dsc_ref, xs, dcs, carry):
        step = pl.program_id(1)
        tile = last - step

        @pl.when(step == 0)
        def _():
            carry[...] = jnp.zeros_like(carry)
            dw_ref[...] = jnp.zeros_like(dw_ref)
            dsc_ref[...] = jnp.zeros_like(dsc_ref)

        for s in range(3):
            w = lambda j, s=s: w_ref[s, j:j + 1, :]
            xs[s, 0:HALO, :] = jnp.where(tile > 0, halo_ref[s], 0.0)
            xs[s, HALO:, :] = x_ref[s]
            c = _conv_fwd(xs.at[s], w, taps, tm)
            y, dsilu = _silu_parts(c)
            dy = d_ref[s]
            if s < 2:
                rn = lax.rsqrt(_head_sum(y * y) + L2_EPS)
                yn = y * rn
                if s == 0:
                    dy = dy * Q_SCALE
                dy = rn * (dy - yn * _head_sum(dy * yn))
            dc = dy * dsilu
            dcs[s, 0:tm, :] = dc
            dcs[s, tm:, :] = carry[s]
            dx_ref[s] = _conv_bwd_x(dcs.at[s], w, taps, tm)
            carry[s] = dc[0:HALO, :]
            for j in range(taps):
                dw_ref[s, j] += _row_partial(dc * xs[s, pl.ds(HALO - taps + 1 + j, tm), :])
        beta = _sigmoid(x_ref[3])
        dx_ref[3] = d_ref[3] * beta * (1.0 - beta)
        z = x_ref[4] + dt_ref[...]
        dg = d_ref[4] * -jnp.exp(al_ref[...])
        da = dg * _sigmoid(z)
        dx_ref[4] = da
        dsc_ref[0] += _row_partial(dg * _softplus(z))
        dsc_ref[1] += _row_partial(da)

    tile_spec = pl.BlockSpec((5, tm, cb), lambda j, i: (0, last - i, j))
    return pl.pallas_call(
        body,
        name="gdn_pre_bwd",
        grid=(width // cb, lp // tm),
        in_specs=[
            tile_spec,
            pl.BlockSpec((3, HALO, cb), lambda j, i: (0, _halo_index(last - i, tm), j)),
            tile_spec,
            pl.BlockSpec((3, taps, cb), lambda j, i: (0, 0, j)),
            pl.BlockSpec((1, cb), lambda j, i: (0, j)),
            pl.BlockSpec((1, cb), lambda j, i: (0, j)),
        ],
        out_specs=[
            tile_spec,
            pl.BlockSpec((3, taps, SUB, cb), lambda j, i: (0, 0, 0, j)),
            pl.BlockSpec((2, SUB, cb), lambda j, i: (0, 0, j)),
        ],
        out_shape=[
            jax.ShapeDtypeStruct((5, lp, width), F32),
            jax.ShapeDtypeStruct((3, taps, SUB, width), F32),
            jax.ShapeDtypeStruct((2, SUB, width), F32),
        ],
        scratch_shapes=[
            pltpu.VMEM((3, tm + HALO, cb), F32),
            pltpu.VMEM((3, tm + HALO, cb), F32),
            pltpu.VMEM((3, HALO, cb), F32),
        ],
        compiler_params=_params(("arbitrary", "arbitrary")),
    )(p5, p5, dqkvbg, conv_w, alog_b, dtb_b)


def gdn_post_fwd(o, z, nw_b, *, tm):
    _, lp, width = o.shape

    def body(o_ref, z_ref, nw_ref, y_ref):
        ov = o_ref[...]
        rn = lax.rsqrt(_head_sum(ov * ov) * (1.0 / HEAD_DIM) + RMS_EPS)
        gate, _ = _silu_parts(z_ref[...])
        y_ref[...] = ov * rn * nw_ref[...] * gate

    row = pl.BlockSpec((None, tm, width), lambda i: (0, i, 0))
    return pl.pallas_call(
        body,
        name="gdn_post_fwd",
        grid=(lp // tm,),
        in_specs=[row, row, pl.BlockSpec((1, width), lambda i: (0, 0))],
        out_specs=row,
        out_shape=jax.ShapeDtypeStruct((1, lp, width), F32),
        compiler_params=_params(("arbitrary",)),
    )(o, z, nw_b)


def gdn_post_bwd(o, z, dy, nw_b, *, tm):
    _, lp, width = o.shape

    def body(o_ref, z_ref, dy_ref, nw_ref, do_ref, dz_ref, dnw_ref):
        ov = o_ref[...]
        rn = lax.rsqrt(_head_sum(ov * ov) * (1.0 / HEAD_DIM) + RMS_EPS)
        yn = ov * rn
        gate, dgate = _silu_parts(z_ref[...])
        d_on = dy_ref[...] * gate
        dz_ref[...] = dy_ref[...] * yn * nw_ref[...] * dgate
        a = d_on * nw_ref[...]
        do_ref[...] = rn * (a - yn * (_head_sum(a * yn) * (1.0 / HEAD_DIM)))

        @pl.when(pl.program_id(0) == 0)
        def _():
            dnw_ref[...] = jnp.zeros_like(dnw_ref)

        dnw_ref[...] += _row_partial(d_on * yn)

    row = pl.BlockSpec((None, tm, width), lambda i: (0, i, 0))
    return pl.pallas_call(
        body,
        name="gdn_post_bwd",
        grid=(lp // tm,),
        in_specs=[row, row, row, pl.BlockSpec((1, width), lambda i: (0, 0))],
        out_specs=[row, row, pl.BlockSpec((8, width), lambda i: (0, 0))],
        out_shape=[jax.ShapeDtypeStruct((1, lp, width), F32)] * 2 + [jax.ShapeDtypeStruct((8, width), F32)],
        compiler_params=_params(("arbitrary",)),
    )(o, z, dy, nw_b)


def head_lane_sum(x):
    s_n, rows, width = x.shape

    def body(x_ref, o_ref):
        lane = lax.broadcasted_iota(jnp.int32, (rows, HEAD_DIM), 1)
        acc = jnp.zeros((rows, HEAD_DIM), F32)
        for h in range(width // HEAD_DIM):
            s = jnp.sum(x_ref[:, h * HEAD_DIM:(h + 1) * HEAD_DIM], axis=-1, keepdims=True)
            acc = jnp.where(lane == h, s, acc)
        o_ref[...] = acc

    return pl.pallas_call(
        body,
        name="head_lane_sum",
        grid=(s_n,),
        in_specs=[pl.BlockSpec((None, rows, width), lambda s: (s, 0, 0))],
        out_specs=pl.BlockSpec((None, rows, HEAD_DIM), lambda s: (s, 0, 0)),
        out_shape=jax.ShapeDtypeStruct((s_n, rows, HEAD_DIM), F32),
        compiler_params=_params(("arbitrary",)),
    )(x)


def ffn_act_fwd(up, conv_w, *, tm, name):
    _, lp, c_w = up.shape
    taps = conv_w.shape[1]

    def body(u_ref, halo_ref, g_ref, w_ref, o_ref):
        first_tile = pl.program_id(1) == 0

        def strip(r0, prev_of):
            rows = pl.ds(r0, SUB)
            for c0 in range(0, c_w, LANES):
                cs = slice(c0, c0 + LANES)
                cur = u_ref[rows, cs]
                prev = prev_of(cs)
                conv = w_ref[taps - 1:taps, cs] * cur
                for j in range(taps - 1):
                    conv += w_ref[j:j + 1, cs] * _shift_down(cur, prev, taps - 1 - j)
                y, _ = _silu_parts(conv)
                o_ref[rows, cs] = y * g_ref[rows, cs]

        strip(0, lambda cs: jnp.where(first_tile, 0.0, halo_ref[:, cs]))

        def loop_body(s, carry):
            r0 = pl.multiple_of(s * SUB, SUB)
            strip(r0, lambda cs: u_ref[pl.ds(pl.multiple_of(r0 - SUB, SUB), SUB), cs])
            return carry

        lax.fori_loop(1, tm // SUB, loop_body, 0)

    return pl.pallas_call(
        body,
        name=name,
        grid=(2, lp // tm),
        in_specs=[
            pl.BlockSpec((None, tm, c_w), lambda s, i: (s, i, 0)),
            pl.BlockSpec((None, HALO, c_w), lambda s, i: (s, _halo_index(i, tm), 0)),
            pl.BlockSpec((None, tm, c_w), lambda s, i: (2 + s, i, 0)),
            pl.BlockSpec((None, taps, c_w), lambda s, i: (s, 0, 0)),
        ],
        out_specs=pl.BlockSpec((None, tm, c_w), lambda s, i: (s, i, 0)),
        out_shape=jax.ShapeDtypeStruct((2, lp, c_w), F32),
        compiler_params=_params(("arbitrary", "arbitrary")),
    )(up, up, up, conv_w)


def ffn_act_bwd(up, dact, conv_w, *, tm, name):
    _, lp, c_w = up.shape
    taps = conv_w.shape[1]
    last = lp // tm - 1
    n_strips = tm // SUB

    def body(u_ref, halo_ref, g_ref, d_ref, w_ref, dup_ref, dw_ref, below):
        step = pl.program_id(1)
        first_tile = step == last

        @pl.when(step == 0)
        def _():
            below[...] = jnp.zeros_like(below)
            dw_ref[...] = jnp.zeros_like(dw_ref)

        def strip(r0, prev_of):
            rows = pl.ds(r0, SUB)
            for c0 in range(0, c_w, LANES):
                cs = slice(c0, c0 + LANES)
                cur = u_ref[rows, cs]
                prev = prev_of(cs)
                shifted = [_shift_down(cur, prev, taps - 1 - j) for j in range(taps)]
                conv = w_ref[0:1, cs] * shifted[0]
                for j in range(1, taps):
                    conv += w_ref[j:j + 1, cs] * shifted[j]
                y, dsilu = _silu_parts(conv)
                d = d_ref[rows, cs]
                dup_ref[1, rows, cs] = d * y
                dc = d * g_ref[rows, cs] * dsilu
                nxt = below[:, cs]
                dx = w_ref[taps - 1:taps, cs] * dc
                for j in range(taps - 1):
                    dx += w_ref[j:j + 1, cs] * _shift_up(dc, nxt, taps - 1 - j)
                dup_ref[0, rows, cs] = dx
                below[:, cs] = dc
                for j in range(taps):
                    dw_ref[j, :, cs] += dc * shifted[j]

        def loop_body(it, carry):
            r0 = pl.multiple_of((n_strips - 1 - it) * SUB, SUB)
            strip(r0, lambda cs: u_ref[pl.ds(pl.multiple_of(r0 - SUB, SUB), SUB), cs])
            return carry

        lax.fori_loop(0, n_strips - 1, loop_body, 0)
        strip(0, lambda cs: jnp.where(first_tile, 0.0, halo_ref[:, cs]))

    return pl.pallas_call(
        body,
        name=name,
        grid=(2, lp // tm),
        in_specs=[
            pl.BlockSpec((None, tm, c_w), lambda s, i: (s, last - i, 0)),
            pl.BlockSpec((None, HALO, c_w), lambda s, i: (s, _halo_index(last - i, tm), 0)),
            pl.BlockSpec((None, tm, c_w), lambda s, i: (2 + s, last - i, 0)),
            pl.BlockSpec((None, tm, c_w), lambda s, i: (s, last - i, 0)),
            pl.BlockSpec((None, taps, c_w), lambda s, i: (s, 0, 0)),
        ],
        out_specs=[
            pl.BlockSpec((2, None, tm, c_w), lambda s, i: (0, s, last - i, 0)),
            pl.BlockSpec((None, taps, SUB, c_w), lambda s, i: (s, 0, 0, 0)),
        ],
        out_shape=[jax.ShapeDtypeStruct((2, 2, lp, c_w), F32), jax.ShapeDtypeStruct((2, taps, SUB, c_w), F32)],
        scratch_shapes=[pltpu.VMEM((SUB, c_w), F32)],
        compiler_params=_params(("arbitrary", "arbitrary")),
    )(up, up, up, dact, conv_w)


def sc_fwd(pb, conv_w, *, tm, cb):
    _, lp, width = pb.shape
    taps = conv_w.shape[0]

    def body(x_ref, halo_ref, w_ref, o_ref):
        first_tile = pl.program_id(1) == 0

        def strip(r0, prev_of):
            rows = pl.ds(r0, SUB)
            for c0 in range(0, cb, LANES):
                cs = slice(c0, c0 + LANES)
                cur = x_ref[1, rows, cs] * x_ref[2, rows, cs]
                prev = prev_of(cs)
                conv = w_ref[taps - 1:taps, cs] * cur
                for j in range(taps - 1):
                    conv += w_ref[j:j + 1, cs] * _shift_down(cur, prev, taps - 1 - j)
                o_ref[rows, cs] = x_ref[0, rows, cs] * conv

        strip(0, lambda cs: jnp.where(first_tile, 0.0, halo_ref[1, :, cs] * halo_ref[2, :, cs]))

        def loop_body(k, carry):
            r0 = pl.multiple_of(k * SUB, SUB)
            before = pl.ds(pl.multiple_of(r0 - SUB, SUB), SUB)
            strip(r0, lambda cs: x_ref[1, before, cs] * x_ref[2, before, cs])
            return carry

        lax.fori_loop(1, tm // SUB, loop_body, 0)

    return pl.pallas_call(
        body,
        name="sc_fwd",
        grid=(width // cb, lp // tm),
        in_specs=[
            pl.BlockSpec((3, tm, cb), lambda j, i: (0, i, j)),
            pl.BlockSpec((3, HALO, cb), lambda j, i: (0, _halo_index(i, tm), j)),
            pl.BlockSpec((taps, cb), lambda j, i: (0, j)),
        ],
        out_specs=pl.BlockSpec((None, tm, cb), lambda j, i: (0, i, j)),
        out_shape=jax.ShapeDtypeStruct((1, lp, width), F32),
        compiler_params=_params(("arbitrary", "arbitrary")),
    )(pb, pb, conv_w)


def sc_bwd(pb, ds, conv_w, *, tm, cb):
    _, lp, width = pb.shape
    taps = conv_w.shape[0]
    last = lp // tm - 1
    n_strips = tm // SUB

    def body(x_ref, halo_ref, d_ref, w_ref, dx_ref, dw_ref, below):
        step = pl.program_id(1)
        first_tile = step == last

        @pl.when(step == 0)
        def _():
            below[...] = jnp.zeros_like(below)
            dw_ref[...] = jnp.zeros_like(dw_ref)

        def strip(r0, prev_of):
            rows = pl.ds(r0, SUB)
            for c0 in range(0, cb, LANES):
                cs = slice(c0, c0 + LANES)
                gate, left, right = x_ref[0, rows, cs], x_ref[1, rows, cs], x_ref[2, rows, cs]
                cur = left * right
                prev = prev_of(cs)
                shifted = [_shift_down(cur, prev, taps - 1 - j) for j in range(taps)]
                conv = w_ref[0:1, cs] * shifted[0]
                for j in range(1, taps):
                    conv += w_ref[j:j + 1, cs] * shifted[j]
                d = d_ref[rows, cs]
                dx_ref[0, rows, cs] = d * conv
                dc = d * gate
                nxt = below[:, cs]
                dp = w_ref[taps - 1:taps, cs] * dc
                for j in range(taps - 1):
                    dp += w_ref[j:j + 1, cs] * _shift_up(dc, nxt, taps - 1 - j)
                dx_ref[1, rows, cs] = dp * right
                dx_ref[2, rows, cs] = dp * left
                below[:, cs] = dc
                for j in range(taps):
                    dw_ref[j, :, cs] += dc * shifted[j]

        def loop_body(it, carry):
            r0 = pl.multiple_of((n_strips - 1 - it) * SUB, SUB)
            before = pl.ds(pl.multiple_of(r0 - SUB, SUB), SUB)
            strip(r0, lambda cs: x_ref[1, before, cs] * x_ref[2, before, cs])
            return carry

        lax.fori_loop(0, n_strips - 1, loop_body, 0)
        strip(0, lambda cs: jnp.where(first_tile, 0.0, halo_ref[1, :, cs] * halo_ref[2, :, cs]))

    tile_spec = pl.BlockSpec((3, tm, cb), lambda j, i: (0, last - i, j))
    return pl.pallas_call(
        body,
        name="sc_bwd",
        grid=(width // cb, lp // tm),
        in_specs=[
            tile_spec,
            pl.BlockSpec((3, HALO, cb), lambda j, i: (0, _halo_index(last - i, tm), j)),
            pl.BlockSpec((None, tm, cb), lambda j, i: (0, last - i, j)),
            pl.BlockSpec((taps, cb), lambda j, i: (0, j)),
        ],
        out_specs=[tile_spec, pl.BlockSpec((taps, SUB, cb), lambda j, i: (0, 0, j))],
        out_shape=[jax.ShapeDtypeStruct((3, lp, width), F32), jax.ShapeDtypeStruct((taps, SUB, width), F32)],
        scratch_shapes=[pltpu.VMEM((SUB, cb), F32)],
        compiler_params=_params(("arbitrary", "arbitrary")),
    )(pb, pb, ds, conv_w)


TILE_BYTES = 1536 * 1024


def _rows_tile(rows, cols, multiple=8):
    if rows * cols * 4 <= TILE_BYTES or rows % multiple:
        return rows
    best = multiple
    for t in range(multiple, rows + 1, multiple):
        if rows % t == 0 and t * cols * 4 <= TILE_BYTES:
            best = t
    return best


def pair_sum(g, landed, core, out_dtype, name):
    _, rows, cols = g.shape
    half = rows // 2
    tr = _rows_tile(half, cols, 16)
    nb = half // tr

    def body(c_ref, g_ref, l_ref, o_ref):
        o_ref[...] = (g_ref[...] + l_ref[...]).astype(out_dtype)

    return pl.pallas_call(
        body,
        name=name,
        grid_spec=pltpu.PrefetchScalarGridSpec(
            num_scalar_prefetch=1,
            grid=(4, nb),
            in_specs=[
                pl.BlockSpec((None, tr, cols), lambda s, i, c: (s, c[0] * nb + i, 0)),
                pl.BlockSpec((None, tr, cols), lambda s, i, c: (s, i, 0)),
            ],
            out_specs=pl.BlockSpec((None, tr, cols), lambda s, i, c: (s, i, 0)),
        ),
        out_shape=jax.ShapeDtypeStruct((4, half, cols), out_dtype),
        compiler_params=_params(("arbitrary", "arbitrary")),
    )(core, g, landed)


def chip_sum(x, name):
    _, rows, cols = x.shape
    tr = _rows_tile(rows, cols, 16)

    def body(x0, x1, x2, x3, o_ref):
        acc = x0[...].astype(F32) + x1[...].astype(F32)
        o_ref[...] = (acc + x2[...].astype(F32)) + x3[...].astype(F32)

    return pl.pallas_call(
        body,
        name=name,
        grid=(rows // tr,),
        in_specs=[pl.BlockSpec((None, tr, cols), lambda i, k=k: (k, i, 0)) for k in range(4)],
        out_specs=pl.BlockSpec((tr, cols), lambda i: (i, 0)),
        out_shape=jax.ShapeDtypeStruct((rows, cols), F32),
        compiler_params=_params(("arbitrary",)),
    )(x, x, x, x)


def adamw(w, g, m, v, name):
    shape = w.shape
    cols = shape[-1]
    rows = w.size // cols
    tr = _rows_tile(rows, cols)

    def body(w_ref, g_ref, m_ref, v_ref, d_ref, m2_ref, v2_ref):
        gv = g_ref[...]
        m2 = ADAM_B1 * m_ref[...] + (1.0 - ADAM_B1) * gv
        v2 = ADAM_B2 * v_ref[...] + (1.0 - ADAM_B2) * (gv * gv)
        m_hat = m2 / (1.0 - ADAM_B1 ** ADAM_STEP)
        v_hat = v2 / (1.0 - ADAM_B2 ** ADAM_STEP)
        d_ref[...] = -ADAM_LR * (m_hat / (jnp.sqrt(v_hat) + ADAM_EPS) + ADAM_WD * w_ref[...])
        m2_ref[...] = m2
        v2_ref[...] = v2

    spec = pl.BlockSpec((tr, cols), lambda i: (i, 0))
    outs = pl.pallas_call(
        body,
        name=name,
        grid=(rows // tr,),
        in_specs=[spec] * 4,
        out_specs=[spec] * 3,
        out_shape=[jax.ShapeDtypeStruct((rows, cols), F32)] * 3,
        compiler_params=_params(("arbitrary",)),
    )(*[t.reshape(rows, cols) for t in (w, g, m, v)])
    return tuple(o.reshape(shape) for o in outs)


MESH_ID = pl.DeviceIdType.MESH
ANY = pl.BlockSpec(memory_space=pl.ANY)


def _place():
    x, y, c = lax.axis_index("x"), lax.axis_index("y"), lax.axis_index("c")
    other_chips = [(1 - x, y), (x, 1 - y), (1 - x, 1 - y)]
    return x, y, c, other_chips


def all_gather_shards(bufs, name):
    n = len(bufs)

    def body(*refs):
        x_refs, o_refs = refs[:n], refs[n:2 * n]
        send_sems, recv_sems = refs[2 * n:]
        x, y, c, chips = _place()
        me = 2 * x + y
        sibling = (x, y, 1 - c)

        def part(a, slot, hf):
            half = bufs[a].shape[0] // 2
            return o_refs[a].at[slot, pl.ds(hf * half, half), :]

        def mine(a):
            half = bufs[a].shape[0] // 2
            return x_refs[a].at[pl.ds(c * half, half), :]

        def copy(k, src, dst, to):
            return pltpu.make_async_remote_copy(src_ref=src, dst_ref=dst, send_sem=send_sems.at[k],
                                                recv_sem=recv_sems.at[k], device_id=to, device_id_type=MESH_ID)

        sent = [copy(6 * a + j, mine(a), part(a, me, c), (px, py, c)) for a in range(n) for j, (px, py) in enumerate(chips)]
        for cp in sent:
            cp.start()
        for a in range(n):
            for j, (px, py) in enumerate(chips):
                landed = part(a, 2 * px + py, c)
                copy(6 * a + j, mine(a), landed, (px, py, c)).wait_recv()
                passed = copy(6 * a + 3 + j, landed, landed, sibling)
                passed.start()
                sent.append(passed)
        for a in range(n):
            for j, (px, py) in enumerate(chips):
                theirs = part(a, 2 * px + py, 1 - c)
                copy(6 * a + 3 + j, theirs, theirs, sibling).wait_recv()
        for cp in sent:
            cp.wait_send()

    outs = pl.pallas_call(
        body,
        name=name,
        in_specs=[ANY] * n,
        out_specs=[ANY] * n,
        out_shape=[jax.ShapeDtypeStruct((4,) + b.shape, b.dtype) for b in bufs],
        scratch_shapes=[pltpu.SemaphoreType.DMA((6 * n,)), pltpu.SemaphoreType.DMA((6 * n,))],
    )(*bufs)
    me = 2 * lax.axis_index("x") + lax.axis_index("y")
    return [lax.dynamic_update_index_in_dim(o, b, me, 0) for o, b in zip(outs, bufs)]


def swap_halves(bufs, name):
    n = len(bufs)

    def body(*refs):
        x_refs, o_refs = refs[:n], refs[n:2 * n]
        send_sems, recv_sems = refs[2 * n:]
        x, y, c, _ = _place()
        copies = []
        for a in range(n):
            half = bufs[a].shape[1] // 2
            cp = pltpu.make_async_remote_copy(src_ref=x_refs[a].at[:, pl.ds((1 - c) * half, half), :], dst_ref=o_refs[a],
                                              send_sem=send_sems.at[a], recv_sem=recv_sems.at[a],
                                              device_id=(x, y, 1 - c), device_id_type=MESH_ID)
            cp.start()
            copies.append(cp)
        for cp in copies:
            cp.wait()

    return pl.pallas_call(
        body,
        name=name,
        in_specs=[ANY] * n,
        out_specs=[ANY] * n,
        out_shape=[jax.ShapeDtypeStruct((4, b.shape[1] // 2, b.shape[2]), b.dtype) for b in bufs],
        scratch_shapes=[pltpu.SemaphoreType.DMA((n,)), pltpu.SemaphoreType.DMA((n,))],
    )(*bufs)


def scatter_to_chips(bufs, name):
    n = len(bufs)

    def body(*refs):
        x_refs, o_refs = refs[:n], refs[n:2 * n]
        send_sems, recv_sems = refs[2 * n:]
        x, y, c, chips = _place()
        me = 2 * x + y

        def copy(a, j, src_slot, dst_slot, px, py):
            return pltpu.make_async_remote_copy(src_ref=x_refs[a].at[src_slot], dst_ref=o_refs[a].at[dst_slot],
                                                send_sem=send_sems.at[3 * a + j], recv_sem=recv_sems.at[3 * a + j],
                                                device_id=(px, py, c), device_id_type=MESH_ID)

        sent = [copy(a, j, 2 * px + py, me, px, py) for a in range(n) for j, (px, py) in enumerate(chips)]
        for cp in sent:
            cp.start()
        for a in range(n):
            for j, (px, py) in enumerate(chips):
                copy(a, j, me, 2 * px + py, px, py).wait_recv()
        for cp in sent:
            cp.wait_send()

    outs = pl.pallas_call(
        body,
        name=name,
        in_specs=[ANY] * n,
        out_specs=[ANY] * n,
        out_shape=[jax.ShapeDtypeStruct(b.shape, b.dtype) for b in bufs],
        scratch_shapes=[pltpu.SemaphoreType.DMA((3 * n,)), pltpu.SemaphoreType.DMA((3 * n,))],
    )(*bufs)
    me = 2 * lax.axis_index("x") + lax.axis_index("y")
    return [lax.dynamic_update_index_in_dim(o, lax.dynamic_index_in_dim(b, me, 0, keepdims=False), me, 0)
            for o, b in zip(outs, bufs)]


def share_halves(groups, name):
    bufs = [b for grp in groups for b in grp]
    where = [(gi, li) for gi, grp in enumerate(groups) for li in range(len(grp))]
    n = len(bufs)

    def body(*refs):
        x_refs, o_refs = refs[:n], refs[n:n + len(groups)]
        send_sems, recv_sems = refs[n + len(groups):]
        x, y, c, _ = _place()
        sent, arrive = [], []
        for a, (gi, li) in enumerate(where):

            def copy(hf, a=a, gi=gi, li=li):
                return pltpu.make_async_remote_copy(src_ref=x_refs[a], dst_ref=o_refs[gi].at[li, hf],
                                                    send_sem=send_sems.at[a], recv_sem=recv_sems.at[a],
                                                    device_id=(x, y, 1 - c), device_id_type=MESH_ID)

            sent.append(copy(c))
            arrive.append(copy(1 - c))
        for cp in sent:
            cp.start()
        for cp in arrive:
            cp.wait_recv()
        for cp in sent:
            cp.wait_send()

    outs = pl.pallas_call(
        body,
        name=name,
        in_specs=[ANY] * n,
        out_specs=[ANY] * len(groups),
        out_shape=[jax.ShapeDtypeStruct((len(grp), 2) + grp[0].shape, grp[0].dtype) for grp in groups],
        scratch_shapes=[pltpu.SemaphoreType.DMA((n,)), pltpu.SemaphoreType.DMA((n,))],
    )(*bufs)
    c = lax.axis_index("c")
    full = [lax.dynamic_update_index_in_dim(o, jnp.stack(grp), c, 1) for o, grp in zip(outs, groups)]
    return [t.reshape(t.shape[0], 2 * t.shape[2], t.shape[3]) for t in full]


def reduce_scatter_grads(groups, travel_dtypes):
    bufs = [b for grp in groups for b in grp]
    dtypes = [dt for grp, dt in zip(groups, travel_dtypes) for _ in grp]
    core = lax.axis_index("c").astype(jnp.int32).reshape(1)
    landed = swap_halves(bufs, "rs_pair")
    sums = [pair_sum(b, l, core, dt, "rs_pair_sum%d" % i) for i, (b, l, dt) in enumerate(zip(bufs, landed, dtypes))]
    from_chips = scatter_to_chips(sums, "rs_chips")
    totals = [chip_sum(t, "rs_chip_sum%d" % i) for i, t in enumerate(from_chips)]
    it = iter(totals)
    return share_halves([[next(it) for _ in grp] for grp in groups], "rs_share")


def _row_tiles(length):
    return (640, 320) if length > 2048 else (128, 64)


def _local_step(x, target, wt):
    seq, d = x.shape
    length = N_META + seq
    tm, tm_ffn = _row_tiles(length)
    lp = -(-length // tm) * tm
    tail = jnp.zeros((lp - length, d), F32)
    h0 = jnp.concatenate([wt["meta"], x, tail], axis=0)[None]
    tgt = jnp.concatenate([jnp.zeros((N_META, d), F32), target, tail], axis=0)
    nn = functools.partial(mm_nn, tm=tm)
    nt = functools.partial(mm_nt, tm=tm)
    tn = functools.partial(mm_tn, tm=tm)
    ln_g = [wt["ln_mix_g"][0:1], wt["ln_ffn_g"][0:1], wt["ln_mix_g"][1:2], wt["ln_ffn_g"][1:2]]
    ln_b = [wt["ln_mix_b"][0:1], wt["ln_ffn_b"][0:1], wt["ln_mix_b"][1:2], wt["ln_ffn_b"][1:2]]

    p5 = nn(h0, wt["a5"], name="a_in5")
    pz = nn(h0, wt["az"], name="a_inz")
    qkvbg = gdn_pre_fwd(p5, wt["a_conv3"], wt["alog_b"], wt["dtb_b"], tm=tm, cb=2 * HEAD_DIM)
    o, states, tinv = gdn_chunk_fwd(qkvbg)
    onz = gdn_post_fwd(o[None], pz, wt["anorm_b"], tm=tm)
    r1, h1 = ln_fwd(h0, nn(onz, wt["a_out"], name="a_out"), ln_g[0], ln_b[0], tm=tm, name="ln1")
    up0 = nn(h1, wt["up"][0], name="up0")
    act0 = ffn_act_fwd(up0, wt["fconv"][0], tm=tm_ffn, name="ffn_act0")
    r2, h2 = ln_fwd(h1, nn(act0, wt["down"][0], name="down0"), ln_g[1], ln_b[1], tm=tm, name="ln2")
    pb = nn(h2, wt["b_in"], name="b_in")
    sc = sc_fwd(pb, wt["b_conv"], tm=tm_ffn, cb=d)
    r3, h3 = ln_fwd(h2, nn(sc, wt["b_out"], name="b_out"), ln_g[2], ln_b[2], tm=tm, name="ln3")
    up1 = nn(h3, wt["up"][1], name="up1")
    act1 = ffn_act_fwd(up1, wt["fconv"][1], tm=tm_ffn, name="ffn_act1")
    r4, h4 = ln_fwd(h3, nn(act1, wt["down"][1], name="down1"), ln_g[3], ln_b[3], tm=tm, name="ln4")

    dh4, loss_part = loss_grad(h4, tgt, first=N_META, count=seq, tm=tm)

    grads = {}
    dr4, dgb4 = ln_bwd(r4, dh4, ln_g[3], tm=tm, name="ln4_bwd")
    d_down1 = tn(act1, dr4, name="d_down1")
    dact1 = nt(dr4, wt["down"][1], name="d_act1")
    dup1, dfconv1 = ffn_act_bwd(up1, dact1, wt["fconv"][1], tm=tm_ffn, name="ffn_act1_bwd")
    dup1 = dup1.reshape(up1.shape)
    d_up1 = tn(h3, dup1, name="d_up1")
    dh3 = nt(dup1, wt["up"][1], res=dr4, res_scale=ALPHA, name="d_h3")

    dr3, dgb3 = ln_bwd(r3, dh3, ln_g[2], tm=tm, name="ln3_bwd")
    d_bout = tn(sc, dr3, name="d_b_out")
    dsc = nt(dr3, wt["b_out"], name="d_sc")
    dpb, dbconv = sc_bwd(pb, dsc, wt["b_conv"], tm=tm_ffn, cb=d)
    d_bin = tn(h2, dpb, name="d_b_in")
    dh2 = nt(dpb, wt["b_in"], res=dr3, res_scale=ALPHA, name="d_h2")

    dr2, dgb2 = ln_bwd(r2, dh2, ln_g[1], tm=tm, name="ln2_bwd")
    d_down0 = tn(act0, dr2, name="d_down0")
    dact0 = nt(dr2, wt["down"][0], name="d_act0")
    dup0, dfconv0 = ffn_act_bwd(up0, dact0, wt["fconv"][0], tm=tm_ffn, name="ffn_act0_bwd")
    dup0 = dup0.reshape(up0.shape)
    d_up0 = tn(h1, dup0, name="d_up0")
    dh1 = nt(dup0, wt["up"][0], res=dr2, res_scale=ALPHA, name="d_h1")

    dr1, dgb1 = ln_bwd(r1, dh1, ln_g[0], tm=tm, name="ln1_bwd")
    d_aout = tn(onz, dr1, name="d_a_out")
    donz = nt(dr1, wt["a_out"], name="d_onz")
    d_o, dz, dnw = gdn_post_bwd(o[None], pz, donz, wt["anorm_b"], tm=tm)
    dqkvbg = gdn_chunk_bwd(qkvbg, states, tinv, d_o[0])
    dp5, daconv, dscal = gdn_pre_bwd(p5, dqkvbg, wt["a_conv3"], wt["alog_b"], wt["dtb_b"], tm=tm, cb=2 * HEAD_DIM)
    d_a5 = tn(h0, dp5, name="d_a_in5")
    d_az = tn(h0, dz, name="d_a_inz")
    dh0 = nt(dp5, wt["a5"], res=dr1, res_scale=ALPHA, name="d_h0a")
    dh0 = nt(dz, wt["az"], res=dh0, res_scale=1.0, name="d_h0")

    width = HEADS * HEAD_DIM
    d_ba = head_lane_sum(d_a5[0, 3:5])[:, :, :HEADS]
    d_a_in = jnp.concatenate([d_a5[0, 0], d_a5[0, 1], d_a5[0, 2], d_az[0, 0], d_ba[0], d_ba[1]], axis=1)
    n_in = d_a_in.shape[1] // 4
    grads["a_w_in"] = [d_a_in.reshape(d, 4, n_in).transpose(1, 0, 2)]
    grads["a_w_out"] = [d_aout.reshape(4, width // 4, d)]
    grads["b_w_in"] = [d_bin[0].transpose(1, 0, 2).reshape(d, 4, 3 * d // 4).transpose(1, 0, 2)]
    grads["b_w_out"] = [d_bout.reshape(4, d // 4, d)]
    grads["ffn_w_up"] = [d_up0[0], d_up1[0]]
    grads["ffn_w_down"] = [t.reshape(4, -1, d) for t in (d_down0, d_down1)]
    grads["a_conv"] = daconv.sum(axis=2).transpose(1, 0, 2).reshape(1, GDN_CONV, 3 * width)
    per_head = dscal.reshape(2, 8, HEADS, HEAD_DIM).sum(axis=(1, 3))
    grads["a_log"] = per_head[0][None]
    grads["a_dt_bias"] = per_head[1][None]
    grads["a_norm"] = dnw.reshape(8, HEADS, HEAD_DIM).sum(axis=(0, 1))[None]
    grads["b_conv"] = dbconv.sum(axis=1)[None]
    lns = [dgb1, dgb2, dgb3, dgb4]
    grads["ln_mix_g"] = jnp.stack([lns[0][0].sum(0), lns[2][0].sum(0)])
    grads["ln_mix_b"] = jnp.stack([lns[0][1].sum(0), lns[2][1].sum(0)])
    grads["ln_ffn_g"] = jnp.stack([lns[1][0].sum(0), lns[3][0].sum(0)])
    grads["ln_ffn_b"] = jnp.stack([lns[1][1].sum(0), lns[3][1].sum(0)])
    grads["ffn_conv"] = jnp.stack([t.sum(axis=2).transpose(1, 0, 2).reshape(FFN_CONV, -1) for t in (dfconv0, dfconv1)])
    grads["meta"] = dh0[0, :N_META]
    return loss_part, dh0, grads


WEIGHTS = ["meta", "a_w_in", "a_conv", "a_log", "a_dt_bias", "a_norm", "a_w_out", "b_w_in", "b_conv", "b_w_out",
           "ln_mix_g", "ln_mix_b", "ffn_w_up", "ffn_conv", "ffn_w_down", "ln_ffn_g", "ln_ffn_b"]
MATMUL_WEIGHTS = ["a_w_in", "a_w_out", "b_w_in", "b_w_out", "ffn_w_up", "ffn_w_down"]
SMALL_SHARDED = ["a_conv", "b_conv", "ffn_conv", "meta"]
REPLICATED = ["a_log", "a_dt_bias", "a_norm", "ln_mix_g", "ln_mix_b", "ln_ffn_g", "ln_ffn_b"]
SHARD_AXIS = {"meta": 1, "a_w_in": 2, "a_conv": 2, "a_w_out": 1, "b_w_in": 2, "b_conv": 2, "b_w_out": 1,
              "ffn_w_up": 2, "ffn_conv": 2, "ffn_w_down": 1}
PACK_COLS = 1024
PACK_ROWS_MULTIPLE = 32


def _pack(pieces, lead=()):
    flat = jnp.concatenate([p.reshape(lead + (-1,)) for p in pieces], axis=-1)
    n = flat.shape[-1]
    rows = -(-n // (PACK_COLS * PACK_ROWS_MULTIPLE)) * PACK_ROWS_MULTIPLE
    flat = jnp.pad(flat, [(0, 0)] * len(lead) + [(0, rows * PACK_COLS - n)])
    return flat.reshape(lead + (rows, PACK_COLS))


def _unpack(buf, shapes, lead=()):
    flat = buf.reshape(lead + (-1,))
    out, off = [], 0
    for shp in shapes:
        n = 1
        for s in shp:
            n *= s
        out.append(flat[..., off:off + n].reshape(lead + tuple(shp)))
        off += n
    return out


def _join_shards(stacked, axis):
    return jnp.concatenate([stacked[k] for k in range(4)], axis=axis)


def _split_shards(full, axis):
    return jnp.stack(jnp.split(full, 4, axis=axis))


def _gather_weights(w):
    layers = [w[n][l].astype(BF16) for n in MATMUL_WEIGHTS for l in range(w[n].shape[0])]
    *stacked, small = all_gather_shards(layers + [_pack([w[n] for n in SMALL_SHARDED])], "gather_weights")
    full, it = {}, iter(stacked)
    for n in MATMUL_WEIGHTS:
        full[n] = [next(it) for _ in range(w[n].shape[0])]
    for n, t in zip(SMALL_SHARDED, _unpack(small, [w[n].shape for n in SMALL_SHARDED], lead=(4,))):
        full[n] = _join_shards(t, SHARD_AXIS[n])
    return _layout_weights(full, w)


def _layout_weights(full, w):
    width = HEADS * HEAD_DIM
    wt = {n: w[n] for n in ("ln_mix_g", "ln_mix_b", "ln_ffn_g", "ln_ffn_b")}
    w_in = _join_shards(full["a_w_in"][0], 1)
    d = w_in.shape[0]
    blocks = [w_in[:, s * width:(s + 1) * width] for s in range(4)]
    b_exp = jnp.repeat(w_in[:, 4 * width:4 * width + HEADS], HEAD_DIM, axis=1)
    a_exp = jnp.repeat(w_in[:, 4 * width + HEADS:], HEAD_DIM, axis=1)
    wt["a5"] = jnp.stack([blocks[0], blocks[1], blocks[2], b_exp, a_exp])[None]
    wt["az"] = blocks[3][None, None]
    wt["a_out"] = full["a_w_out"][0].reshape(1, 1, width, d)
    wt["b_in"] = _join_shards(full["b_w_in"][0], 1).reshape(d, 3, d).transpose(1, 0, 2)[None]
    wt["b_out"] = full["b_w_out"][0].reshape(1, 1, d, d)
    n_ff = full["ffn_w_up"][0].shape[2]
    wt["up"] = [t[None] for t in full["ffn_w_up"]]
    wt["down"] = [t.reshape(2, 1, n_ff, d) for t in full["ffn_w_down"]]
    wt["a_conv3"] = full["a_conv"][0].reshape(GDN_CONV, 3, width).transpose(1, 0, 2)
    wt["b_conv"] = full["b_conv"][0]
    wt["fconv"] = [full["ffn_conv"][l].reshape(FFN_CONV, 2, n_ff).transpose(1, 0, 2) for l in range(2)]
    wt["meta"] = full["meta"]
    wt["alog_b"] = jnp.repeat(w["a_log"][0], HEAD_DIM)[None]
    wt["dtb_b"] = jnp.repeat(w["a_dt_bias"][0], HEAD_DIM)[None]
    wt["anorm_b"] = jnp.tile(w["a_norm"][0], HEADS)[None]
    return wt


def _reduce_grads(grads, loss_part, w):
    pieces = [_split_shards(grads[n], SHARD_AXIS[n]) for n in SMALL_SHARDED]
    same = jnp.concatenate([grads[n].reshape(-1) for n in REPLICATED] + [jnp.sum(loss_part).reshape(1)])
    pieces.append(jnp.broadcast_to(same, (4,) + same.shape))
    groups = [grads[n] for n in MATMUL_WEIGHTS] + [[_pack(pieces, lead=(4,))]]
    *totals, small = reduce_scatter_grads(groups, [BF16] * len(MATMUL_WEIGHTS) + [F32])
    out = {n: t.reshape(w[n].shape) for n, t in zip(MATMUL_WEIGHTS, totals)}
    rest = SMALL_SHARDED + REPLICATED
    unpacked = _unpack(small[0], [w[n].shape for n in rest] + [()])
    out.update(zip(rest, unpacked[:-1]))
    return out, unpacked[-1]


def kernel(x, meta, a_w_in, a_conv, a_log, a_dt_bias, a_norm, a_w_out, b_w_in, b_conv, b_w_out, ln_mix_g, ln_mix_b, ffn_w_up, ffn_conv, ffn_w_down, ln_ffn_g, ln_ffn_b, loss_target, m_meta, m_a_w_in, m_a_conv, m_a_log, m_a_dt_bias, m_a_norm, m_a_w_out, m_b_w_in, m_b_conv, m_b_w_out, m_ln_mix_g, m_ln_mix_b, m_ffn_w_up, m_ffn_conv, m_ffn_w_down, m_ln_ffn_g, m_ln_ffn_b, v_meta, v_a_w_in, v_a_conv, v_a_log, v_a_dt_bias, v_a_norm, v_a_w_out, v_b_w_in, v_b_conv, v_b_w_out, v_ln_mix_g, v_ln_mix_b, v_ffn_w_up, v_ffn_conv, v_ffn_w_down, v_ln_ffn_g, v_ln_ffn_b):
    w = dict(meta=meta, a_w_in=a_w_in, a_conv=a_conv, a_log=a_log, a_dt_bias=a_dt_bias, a_norm=a_norm, a_w_out=a_w_out,
             b_w_in=b_w_in, b_conv=b_conv, b_w_out=b_w_out, ln_mix_g=ln_mix_g, ln_mix_b=ln_mix_b, ffn_w_up=ffn_w_up,
             ffn_conv=ffn_conv, ffn_w_down=ffn_w_down, ln_ffn_g=ln_ffn_g, ln_ffn_b=ln_ffn_b)
    m = dict(meta=m_meta, a_w_in=m_a_w_in, a_conv=m_a_conv, a_log=m_a_log, a_dt_bias=m_a_dt_bias, a_norm=m_a_norm,
             a_w_out=m_a_w_out, b_w_in=m_b_w_in, b_conv=m_b_conv, b_w_out=m_b_w_out, ln_mix_g=m_ln_mix_g,
             ln_mix_b=m_ln_mix_b, ffn_w_up=m_ffn_w_up, ffn_conv=m_ffn_conv, ffn_w_down=m_ffn_w_down,
             ln_ffn_g=m_ln_ffn_g, ln_ffn_b=m_ln_ffn_b)
    v = dict(meta=v_meta, a_w_in=v_a_w_in, a_conv=v_a_conv, a_log=v_a_log, a_dt_bias=v_a_dt_bias, a_norm=v_a_norm,
             a_w_out=v_a_w_out, b_w_in=v_b_w_in, b_conv=v_b_conv, b_w_out=v_b_w_out, ln_mix_g=v_ln_mix_g,
             ln_mix_b=v_ln_mix_b, ffn_w_up=v_ffn_w_up, ffn_conv=v_ffn_conv, ffn_w_down=v_ffn_w_down,
             ln_ffn_g=v_ln_ffn_g, ln_ffn_b=v_ln_ffn_b)
    seq = x.shape[1]
    wt = _gather_weights(w)
    loss_part, dh0, grads = _local_step(x[0], loss_target[0], wt)
    grad_w, loss = _reduce_grads(grads, loss_part, w)
    grad_x = dh0[:, N_META:N_META + seq]
    steps = [adamw(w[n], grad_w[n], m[n], v[n], "adamw_" + n) for n in WEIGHTS]
    return (loss, grad_x, *[grad_w[n] for n in WEIGHTS], *[s[0] for s in steps], *[s[1] for s in steps],
            *[s[2] for s in steps])
```

```python
import functools

import jax
import jax.numpy as jnp
from jax import lax
from jax.experimental import pallas as pl
from jax.experimental.pallas import tpu as pltpu

F32 = jnp.float32
BF16 = jnp.bfloat16
HI = lax.Precision.HIGHEST

N_META = 16
HEADS = 8
HEAD_DIM = 128
CHUNK = 64
GDN_CONV = 4
SC_CONV = 3
FFN_CONV = 3
ALPHA = 4.0 ** 0.25
LN_EPS = 1e-5
RMS_EPS = 1e-6
L2_EPS = 1e-6
Q_SCALE = HEAD_DIM ** -0.5

ADAM_LR = 0.001
ADAM_B1 = 0.9
ADAM_B2 = 0.999
ADAM_EPS = 1e-08
ADAM_WD = 0.01
ADAM_STEP = 10

HALO = 8
VMEM_LIMIT = 48 * 1024 * 1024


def _params(sem=None):
    return pltpu.CompilerParams(dimension_semantics=sem, vmem_limit_bytes=VMEM_LIMIT)


def _dot(a, b, prec=None):
    return jnp.dot(a, b, preferred_element_type=F32, precision=prec)


def _dot_nt(a, b, prec=None):
    return lax.dot_general(a, b, (((1,), (1,)), ((), ())), preferred_element_type=F32, precision=prec)


def _dot_tn(a, b, prec=None):
    return lax.dot_general(a, b, (((0,), (0,)), ((), ())), preferred_element_type=F32, precision=prec)


def _sigmoid(x):
    return 1.0 / (1.0 + jnp.exp(-x))


def _tri_masks():
    r = lax.broadcasted_iota(jnp.int32, (CHUNK, CHUNK), 0)
    c = lax.broadcasted_iota(jnp.int32, (CHUNK, CHUNK), 1)
    return r >= c, r > c, r == c


def _split_hi_lo(x):
    hi = x.astype(BF16)
    return hi, (x - hi.astype(F32)).astype(BF16)


def _mask_dot(mask, x):
    hi, lo = _split_hi_lo(x)
    return _dot(mask, hi) + _dot(mask, lo)


@jax.custom_vjp
def _cumsum_rows(g):
    causal, _, _ = _tri_masks()
    return _mask_dot(causal.astype(BF16), g)


def _cumsum_rows_fwd(g):
    return _cumsum_rows(g), None


def _cumsum_rows_bwd(_, dy):
    _, strict, _ = _tri_masks()
    return (_mask_dot((~strict).astype(BF16), dy),)


_cumsum_rows.defvjp(_cumsum_rows_fwd, _cumsum_rows_bwd)


def _dot_split3(a, b):
    a_hi, a_lo = _split_hi_lo(a)
    b_hi, b_lo = _split_hi_lo(b)
    return _dot(a_hi, b_hi) + (_dot(a_hi, b_lo) + _dot(a_lo, b_hi))


def _gdn_m(ks, g64s, bbs):
    causal, strict, _ = _tri_masks()
    a = [_cumsum_rows(g) for g in g64s]
    decay = [jnp.exp(jnp.where(causal, x - x.T, -1e30)) for x in a]
    kk = [_dot_nt(k * b, k) for k, b in zip(ks, bbs)]
    return [jnp.where(strict, x * d, 0.0) for x, d in zip(kk, decay)]


def _gdn_inverse(ms):
    r = lax.broadcasted_iota(jnp.int32, (CHUNK, CHUNK), 0)
    c = lax.broadcasted_iota(jnp.int32, (CHUNK, CHUNK), 1)
    eye = (r == c).astype(F32)
    same = [jnp.right_shift(r, s) == jnp.right_shift(c, s) for s in (3, 4, 5)]
    d = [jnp.where(same[0], m, 0.0) for m in ms]
    p = [_dot(x, x) for x in d]
    t = [eye - x for x in d]
    t = [x + _dot(x, y) for x, y in zip(t, p)]
    p = [_dot(x, x) for x in p]
    t = [x + _dot(x, y) for x, y in zip(t, p)]
    for inner, outer in ((same[0], same[1]), (same[1], same[2]), (same[2], None)):
        joins = ~inner if outer is None else (outer & ~inner)
        o = [_dot(x, jnp.where(joins, m, 0.0)) for x, m in zip(t, ms)]
        t = [x - _dot(y, x) for x, y in zip(t, o)]
    res = [eye - x - _dot_split3(m, x) for m, x in zip(ms, t)]
    return [x + _dot(x, y) for x, y in zip(t, res)]


def _gdn_apply(qs, ks, vs, gbs, g64s, bbs, ss, ts):
    causal, _, _ = _tri_masks()
    n = range(len(qs))
    gc = [_cumsum_rows(g) for g in gbs]
    a = [_cumsum_rows(g) for g in g64s]
    decay = [jnp.exp(jnp.where(causal, x - x.T, -1e30)) for x in a]
    eg = [jnp.exp(x) for x in gc]
    u = [_dot(ts[h], vs[h] * bbs[h]) for h in n]
    w = [_dot(ts[h], ks[h] * bbs[h] * eg[h]) for h in n]
    qk = [_dot_nt(qs[h], ks[h]) * decay[h] for h in n]
    gl = [jnp.sum(g, axis=0, keepdims=True) for g in gbs]
    kd = [ks[h] * jnp.exp(gl[h] - gc[h]) for h in n]
    v_new = [u[h] - _dot(w[h], ss[h]) for h in n]
    o = [_dot(qs[h] * eg[h], ss[h]) + _dot(qk[h], v_new[h]) for h in n]
    s2 = [ss[h] * jnp.exp(gl[h]) + _dot_tn(kd[h], v_new[h]) for h in n]
    return o, s2


def _head_slices(h):
    return slice(h * HEAD_DIM, (h + 1) * HEAD_DIM), slice(h * HEAD_DIM, h * HEAD_DIM + CHUNK)


def _gdn_head_values(x_ref):
    out = [[], [], [], [], [], []]
    for h in range(HEADS):
        sl, sl64 = _head_slices(h)
        for lst, val in zip(out, (x_ref[0, :, sl], x_ref[1, :, sl], x_ref[2, :, sl], x_ref[4, :, sl],
                                  x_ref[4, :, sl64], x_ref[3, :, sl])):
            lst.append(val)
    return out


def gdn_chunk_fwd(qkvbg, gather=()):
    _, lp, width = qkvbg.shape
    n_chunks = lp // CHUNK
    n = len(gather)

    def body(x_ref, *refs):
        shard_refs, (o_ref, s_ref, t_ref), refs = refs[:n], refs[n:n + 3], refs[n + 3:]
        stack_refs, state, sems = refs[:n], refs[n], refs[n + 1:]
        copies = _gather_copies(shard_refs, stack_refs, *sems) if n else None

        @pl.when(pl.program_id(0) == 0)
        def _():
            state[...] = jnp.zeros_like(state)
            if n:
                _gather_start(copies)

        qs, ks, vs, gbs, g64s, bbs = _gdn_head_values(x_ref)
        ss = [state[h] for h in range(HEADS)]
        ts = _gdn_inverse(_gdn_m(ks, g64s, bbs))
        os_, s2 = _gdn_apply(qs, ks, vs, gbs, g64s, bbs, ss, ts)
        for h in range(HEADS):
            s_ref[0, h] = ss[h]
            t_ref[0, h] = ts[h]
            o_ref[:, _head_slices(h)[0]] = os_[h]
            state[h] = s2[h]

        if n:
            @pl.when(pl.program_id(0) == n_chunks - 1)
            def _():
                _gather_finish(copies)

    o, states, tinv, *stacks = pl.pallas_call(
        body,
        name="gdn_chunk_fwd",
        grid=(n_chunks,),
        in_specs=[pl.BlockSpec((5, CHUNK, width), lambda c: (0, c, 0))] + [ANY] * n,
        out_specs=[
            pl.BlockSpec((CHUNK, width), lambda c: (c, 0)),
            pl.BlockSpec((1, HEADS, HEAD_DIM, HEAD_DIM), lambda c: (c, 0, 0, 0)),
            pl.BlockSpec((1, HEADS, CHUNK, CHUNK), lambda c: (c, 0, 0, 0)),
        ] + [ANY] * n,
        out_shape=[
            jax.ShapeDtypeStruct((lp, width), F32),
            jax.ShapeDtypeStruct((n_chunks, HEADS, HEAD_DIM, HEAD_DIM), F32),
            jax.ShapeDtypeStruct((n_chunks, HEADS, CHUNK, CHUNK), F32),
        ] + _gather_out_shapes(gather),
        scratch_shapes=[pltpu.VMEM((HEADS, HEAD_DIM, HEAD_DIM), F32)] + (_gather_sems(n) if n else []),
        compiler_params=_params(("arbitrary",)),
    )(qkvbg, *gather)
    return o, states, tinv, _set_own_slots(stacks, gather)


def gdn_chunk_bwd(qkvbg, states, tinv, d_o, scatter=()):
    _, lp, width = qkvbg.shape
    n_chunks = lp // CHUNK
    last = n_chunks - 1
    n = len(scatter)

    def body(x_ref, s_ref, t_ref, do_ref, *refs):
        leaving_refs, dx_ref, refs = refs[:n], refs[n], refs[n + 1:]
        landing_refs, dstate, sems = refs[:n], refs[n], refs[n + 1:]
        copies = _scatter_copies(leaving_refs, landing_refs, *sems) if n else None

        @pl.when(pl.program_id(0) == 0)
        def _():
            dstate[...] = jnp.zeros_like(dstate)
            if n:
                _scatter_start(copies)

        heads = range(HEADS)
        qs, ks, vs, gbs, g64s, bbs = _gdn_head_values(x_ref)
        ss = [s_ref[0, h] for h in heads]
        ts = [t_ref[0, h] for h in heads]
        d_out = ([do_ref[:, _head_slices(h)[0]] for h in heads], [dstate[h] for h in heads])
        _, vjp_apply = jax.vjp(_gdn_apply, qs, ks, vs, gbs, g64s, bbs, ss, ts)
        dq, dk, dv, dgb, dg64, dbb, ds, dt = vjp_apply(d_out)
        tts = [t.T for t in ts]
        dm = [_dot(tts[h], dt[h]) for h in heads]
        dm = [-_dot(dm[h], tts[h]) for h in heads]
        _, vjp_m = jax.vjp(_gdn_m, ks, g64s, bbs)
        dk2, dg64m, dbb2 = vjp_m(dm)
        for h in heads:
            sl, sl64 = _head_slices(h)
            dx_ref[0, :, sl] = dq[h]
            dx_ref[1, :, sl] = dk[h] + dk2[h]
            dx_ref[2, :, sl] = dv[h]
            dx_ref[3, :, sl] = dbb[h] + dbb2[h]
            dx_ref[4, :, sl] = dgb[h]
            dx_ref[4, :, sl64] += dg64[h] + dg64m[h]
            dstate[h] = ds[h]

        if n:
            @pl.when(pl.program_id(0) == n_chunks - 1)
            def _():
                _scatter_finish(copies)

    dqkvbg, *landed = pl.pallas_call(
        body,
        name="gdn_chunk_bwd",
        grid=(n_chunks,),
        in_specs=[
            pl.BlockSpec((5, CHUNK, width), lambda c: (0, last - c, 0)),
            pl.BlockSpec((1, HEADS, HEAD_DIM, HEAD_DIM), lambda c: (last - c, 0, 0, 0)),
            pl.BlockSpec((1, HEADS, CHUNK, CHUNK), lambda c: (last - c, 0, 0, 0)),
            pl.BlockSpec((CHUNK, width), lambda c: (last - c, 0)),
        ] + [ANY] * n,
        out_specs=[pl.BlockSpec((5, CHUNK, width), lambda c: (0, last - c, 0))] + [ANY] * n,
        out_shape=[jax.ShapeDtypeStruct(qkvbg.shape, F32)] + [jax.ShapeDtypeStruct(b.shape, b.dtype) for b in scatter],
        scratch_shapes=[pltpu.VMEM((HEADS, HEAD_DIM, HEAD_DIM), F32)] + (_scatter_sems(n) if n else []),
        compiler_params=_params(("arbitrary",)),
    )(qkvbg, states, tinv, d_o, *scatter)
    return dqkvbg, _keep_own_slots(landed, scatter)


def mm_nn(a, b, *, tm, name):
    ks, m, tk = a.shape
    _, ns, _, tn = b.shape

    def body(a_ref, b_ref, o_ref):
        p = _dot(a_ref[...].astype(BF16), b_ref[...])

        @pl.when(pl.program_id(2) == 0)
        def _():
            o_ref[...] = p

        @pl.when(pl.program_id(2) > 0)
        def _():
            o_ref[...] += p

    return pl.pallas_call(
        body,
        name=name,
        grid=(ns, m // tm, ks),
        in_specs=[
            pl.BlockSpec((None, tm, tk), lambda n, i, k: (k, i, 0)),
            pl.BlockSpec((None, None, tk, tn), lambda n, i, k: (k, n, 0, 0)),
        ],
        out_specs=pl.BlockSpec((None, tm, tn), lambda n, i, k: (n, i, 0)),
        out_shape=jax.ShapeDtypeStruct((ns, m, tn), F32),
        compiler_params=_params(("arbitrary", "arbitrary", "arbitrary")),
    )(a, b)


def mm_nt(dy, w, *, tm, name, res=None, res_scale=1.0):
    ns, m, tn = dy.shape
    ks, _, tk, _ = w.shape

    def body(*refs):
        if res is None:
            dy_ref, w_ref, o_ref = refs
        else:
            dy_ref, w_ref, r_ref, o_ref = refs
        p = _dot_nt(dy_ref[...].astype(BF16), w_ref[...])

        @pl.when(pl.program_id(2) == 0)
        def _():
            o_ref[...] = p if res is None else p + res_scale * r_ref[...]

        @pl.when(pl.program_id(2) > 0)
        def _():
            o_ref[...] += p

    in_specs = [
        pl.BlockSpec((None, tm, tn), lambda k, i, n: (n, i, 0)),
        pl.BlockSpec((None, None, tk, tn), lambda k, i, n: (k, n, 0, 0)),
    ]
    args = [dy, w]
    if res is not None:
        in_specs.append(pl.BlockSpec((None, tm, tk), lambda k, i, n: (k, i, 0)))
        args.append(res)
    return pl.pallas_call(
        body,
        name=name,
        grid=(ks, m // tm, ns),
        in_specs=in_specs,
        out_specs=pl.BlockSpec((None, tm, tk), lambda k, i, n: (k, i, 0)),
        out_shape=jax.ShapeDtypeStruct((ks, m, tk), F32),
        compiler_params=_params(("arbitrary", "arbitrary", "arbitrary")),
    )(*args)


def mm_tn(x, dy, *, tm, name):
    ks, m, tk = x.shape
    ns, _, tn = dy.shape

    def body(x_ref, dy_ref, o_ref):
        p = _dot_tn(x_ref[...].astype(BF16), dy_ref[...].astype(BF16))

        @pl.when(pl.program_id(2) == 0)
        def _():
            o_ref[...] = p

        @pl.when(pl.program_id(2) > 0)
        def _():
            o_ref[...] += p

    return pl.pallas_call(
        body,
        name=name,
        grid=(ks, ns, m // tm),
        in_specs=[
            pl.BlockSpec((None, tm, tk), lambda k, n, i: (k, i, 0)),
            pl.BlockSpec((None, tm, tn), lambda k, n, i: (n, i, 0)),
        ],
        out_specs=pl.BlockSpec((None, None, tk, tn), lambda k, n, i: (k, n, 0, 0)),
        out_shape=jax.ShapeDtypeStruct((ks, ns, tk, tn), F32),
        compiler_params=_params(("arbitrary", "arbitrary", "arbitrary")),
    )(x, dy)


def _row_partial(x):
    rows, c = x.shape
    return jnp.sum(x.reshape(rows // 8, 8, c), axis=0)


def ln_fwd(h_prev, mix, g, b, *, tm, name):
    _, lp, d = h_prev.shape

    def body(h_ref, m_ref, g_ref, b_ref, r_ref, o_ref):
        r = ALPHA * h_ref[...] + m_ref[...]
        mu = jnp.mean(r, axis=-1, keepdims=True)
        xc = r - mu
        var = jnp.mean(xc * xc, axis=-1, keepdims=True)
        r_ref[...] = r
        o_ref[...] = xc * lax.rsqrt(var + LN_EPS) * g_ref[...] + b_ref[...]

    row = pl.BlockSpec((None, tm, d), lambda i: (0, i, 0))
    vec = pl.BlockSpec((1, d), lambda i: (0, 0))
    return pl.pallas_call(
        body,
        name=name,
        grid=(lp // tm,),
        in_specs=[row, row, vec, vec],
        out_specs=[row, row],
        out_shape=[jax.ShapeDtypeStruct((1, lp, d), F32)] * 2,
        compiler_params=_params(("arbitrary",)),
    )(h_prev, mix, g, b)


def ln_bwd(r, dh, g, *, tm, name):
    _, lp, d = r.shape

    def body(r_ref, dh_ref, g_ref, dr_ref, dgb_ref):
        x = r_ref[...]
        dh_v = dh_ref[...]
        mu = jnp.mean(x, axis=-1, keepdims=True)
        xc = x - mu
        rstd = lax.rsqrt(jnp.mean(xc * xc, axis=-1, keepdims=True) + LN_EPS)
        xh = xc * rstd
        dxh = dh_v * g_ref[...]
        m1 = jnp.mean(dxh, axis=-1, keepdims=True)
        m2 = jnp.mean(dxh * xh, axis=-1, keepdims=True)
        dr_ref[...] = rstd * (dxh - m1 - xh * m2)

        @pl.when(pl.program_id(0) == 0)
        def _():
            dgb_ref[...] = jnp.zeros_like(dgb_ref)

        dgb_ref[0] += _row_partial(dh_v * xh)
        dgb_ref[1] += _row_partial(dh_v)

    row = pl.BlockSpec((None, tm, d), lambda i: (0, i, 0))
    return pl.pallas_call(
        body,
        name=name,
        grid=(lp // tm,),
        in_specs=[row, row, pl.BlockSpec((1, d), lambda i: (0, 0))],
        out_specs=[row, pl.BlockSpec((2, 8, d), lambda i: (0, 0, 0))],
        out_shape=[jax.ShapeDtypeStruct((1, lp, d), F32), jax.ShapeDtypeStruct((2, 8, d), F32)],
        compiler_params=_params(("arbitrary",)),
    )(r, dh, g)


def loss_grad(h, target, *, first, count, tm):
    _, lp, d = h.shape

    def body(h_ref, t_ref, dh_ref, l_ref):
        row = pl.program_id(0) * tm + lax.broadcasted_iota(jnp.int32, (tm, d), 0)
        valid = (row >= first) & (row < first + count)
        err = jnp.where(valid, h_ref[...] - t_ref[...], 0.0)
        dh_ref[...] = err * (1.0 / d)

        @pl.when(pl.program_id(0) == 0)
        def _():
            l_ref[...] = jnp.zeros_like(l_ref)

        l_ref[...] += _row_partial(err * err) * (0.5 / d)

    return pl.pallas_call(
        body,
        name="loss_grad",
        grid=(lp // tm,),
        in_specs=[pl.BlockSpec((None, tm, d), lambda i: (0, i, 0)), pl.BlockSpec((tm, d), lambda i: (i, 0))],
        out_specs=[pl.BlockSpec((None, tm, d), lambda i: (0, i, 0)), pl.BlockSpec((8, d), lambda i: (0, 0))],
        out_shape=[jax.ShapeDtypeStruct((1, lp, d), F32), jax.ShapeDtypeStruct((8, d), F32)],
        compiler_params=_params(("arbitrary",)),
    )(h, target)


def _halo_index(tile, tm):
    return jnp.maximum(tile * (tm // HALO) - 1, 0)


def _conv_fwd(xs_ref, w, taps, tm):
    acc = w(0) * xs_ref[pl.ds(HALO - taps + 1, tm), :]
    for j in range(1, taps):
        acc += w(j) * xs_ref[pl.ds(HALO - taps + 1 + j, tm), :]
    return acc


def _conv_bwd_x(dcs_ref, w, taps, tm):
    acc = w(0) * dcs_ref[pl.ds(taps - 1, tm), :]
    for j in range(1, taps):
        acc += w(j) * dcs_ref[pl.ds(taps - 1 - j, tm), :]
    return acc


SUB = 8
LANES = 128


def _shift_down(cur, prev, s):
    if s == 0:
        return cur
    row = lax.broadcasted_iota(jnp.int32, cur.shape, 0)
    return jnp.where(row < s, pltpu.roll(prev, s, axis=0), pltpu.roll(cur, s, axis=0))


def _shift_up(cur, nxt, s):
    if s == 0:
        return cur
    row = lax.broadcasted_iota(jnp.int32, cur.shape, 0)
    return jnp.where(row < SUB - s, pltpu.roll(cur, SUB - s, axis=0), pltpu.roll(nxt, SUB - s, axis=0))


def _silu_parts(c):
    sg = _sigmoid(c)
    return c * sg, sg * (1.0 + c * (1.0 - sg))


def _head_sum(x):
    rows, c = x.shape
    parts = []
    for h in range(c // HEAD_DIM):
        s = jnp.sum(x[:, h * HEAD_DIM:(h + 1) * HEAD_DIM], axis=-1, keepdims=True)
        parts.append(jnp.broadcast_to(s, (rows, HEAD_DIM)))
    return parts[0] if len(parts) == 1 else jnp.concatenate(parts, axis=-1)


def _log1p(y):
    u = 1.0 + y
    d = u - 1.0
    return jnp.where(d == 0.0, y, jnp.log(u) * (y / jnp.where(d == 0.0, 1.0, d)))


def _softplus(x):
    return jnp.maximum(x, 0.0) + _log1p(jnp.exp(-jnp.abs(x)))


def gdn_pre_fwd(p5, conv_w, alog_b, dtb_b, *, tm, cb):
    _, lp, width = p5.shape
    taps = conv_w.shape[1]

    def body(x_ref, halo_ref, w_ref, al_ref, dt_ref, o_ref, xs):
        i = pl.program_id(1)
        for s in range(3):
            xs[s, 0:HALO, :] = jnp.where(i > 0, halo_ref[s], 0.0)
            xs[s, HALO:, :] = x_ref[s]
            c = _conv_fwd(xs.at[s], lambda j, s=s: w_ref[s, j:j + 1, :], taps, tm)
            y, _ = _silu_parts(c)
            if s < 2:
                y = y * lax.rsqrt(_head_sum(y * y) + L2_EPS)
                if s == 0:
                    y = y * Q_SCALE
            o_ref[s] = y
        o_ref[3] = _sigmoid(x_ref[3])
        o_ref[4] = -jnp.exp(al_ref[...]) * _softplus(x_ref[4] + dt_ref[...])

    return pl.pallas_call(
        body,
        name="gdn_pre_fwd",
        grid=(width // cb, lp // tm),
        in_specs=[
            pl.BlockSpec((5, tm, cb), lambda j, i: (0, i, j)),
            pl.BlockSpec((3, HALO, cb), lambda j, i: (0, _halo_index(i, tm), j)),
            pl.BlockSpec((3, taps, cb), lambda j, i: (0, 0, j)),
            pl.BlockSpec((1, cb), lambda j, i: (0, j)),
            pl.BlockSpec((1, cb), lambda j, i: (0, j)),
        ],
        out_specs=pl.BlockSpec((5, tm, cb), lambda j, i: (0, i, j)),
        out_shape=jax.ShapeDtypeStruct((5, lp, width), F32),
        scratch_shapes=[pltpu.VMEM((3, tm + HALO, cb), F32)],
        compiler_params=_params(("arbitrary", "arbitrary")),
    )(p5, p5, conv_w, alog_b, dtb_b)


def gdn_pre_bwd(p5, dqkvbg, conv_w, alog_b, dtb_b, *, tm, cb):
    _, lp, width = p5.shape
    taps = conv_w.shape[1]
    last = lp // tm - 1

    def body(x_ref, halo_ref, d_ref, w_ref, al_ref, dt_ref, dx_ref, dw_ref, dsc_ref, xs, dcs, carry):
        step = pl.program_id(1)
        tile = last - step

        @pl.when(step == 0)
        def _():
            carry[...] = jnp.zeros_like(carry)
            dw_ref[...] = jnp.zeros_like(dw_ref)
            dsc_ref[...] = jnp.zeros_like(dsc_ref)

        for s in range(3):
            w = lambda j, s=s: w_ref[s, j:j + 1, :]
            xs[s, 0:HALO, :] = jnp.where(tile > 0, halo_ref[s], 0.0)
            xs[s, HALO:, :] = x_ref[s]
            c = _conv_fwd(xs.at[s], w, taps, tm)
            y, dsilu = _silu_parts(c)
            dy = d_ref[s]
            if s < 2:
                rn = lax.rsqrt(_head_sum(y * y) + L2_EPS)
                yn = y * rn
                if s == 0:
                    dy = dy * Q_SCALE
                dy = rn * (dy - yn * _head_sum(dy * yn))
            dc = dy * dsilu
            dcs[s, 0:tm, :] = dc
            dcs[s, tm:, :] = carry[s]
            dx_ref[s] = _conv_bwd_x(dcs.at[s], w, taps, tm)
            carry[s] = dc[0:HALO, :]
            for j in range(taps):
                dw_ref[s, j] += _row_partial(dc * xs[s, pl.ds(HALO - taps + 1 + j, tm), :])
        beta = _sigmoid(x_ref[3])
        dx_ref[3] = d_ref[3] * beta * (1.0 - beta)
        z = x_ref[4] + dt_ref[...]
        dg = d_ref[4] * -jnp.exp(al_ref[...])
        da = dg * _sigmoid(z)
        dx_ref[4] = da
        dsc_ref[0] += _row_partial(dg * _softplus(z))
        dsc_ref[1] += _row_partial(da)

    tile_spec = pl.BlockSpec((5, tm, cb), lambda j, i: (0, last - i, j))
    return pl.pallas_call(
        body,
        name="gdn_pre_bwd",
        grid=(width // cb, lp // tm),
        in_specs=[
            tile_spec,
            pl.BlockSpec((3, HALO, cb), lambda j, i: (0, _halo_index(last - i, tm), j)),
            tile_spec,
            pl.BlockSpec((3, taps, cb), lambda j, i: (0, 0, j)),
            pl.BlockSpec((1, cb), lambda j, i: (0, j)),
            pl.BlockSpec((1, cb), lambda j, i: (0, j)),
        ],
        out_specs=[
            tile_spec,
            pl.BlockSpec((3, taps, SUB, cb), lambda j, i: (0, 0, 0, j)),
            pl.BlockSpec((2, SUB, cb), lambda j, i: (0, 0, j)),
        ],
        out_shape=[
            jax.ShapeDtypeStruct((5, lp, width), F32),
            jax.ShapeDtypeStruct((3, taps, SUB, width), F32),
            jax.ShapeDtypeStruct((2, SUB, width), F32),
        ],
        scratch_shapes=[
            pltpu.VMEM((3, tm + HALO, cb), F32),
            pltpu.VMEM((3, tm + HALO, cb), F32),
            pltpu.VMEM((3, HALO, cb), F32),
        ],
        compiler_params=_params(("arbitrary", "arbitrary")),
    )(p5, p5, dqkvbg, conv_w, alog_b, dtb_b)


def gdn_post_fwd(o, z, nw_b, *, tm):
    _, lp, width = o.shape

    def body(o_ref, z_ref, nw_ref, y_ref):
        ov = o_ref[...]
        rn = lax.rsqrt(_head_sum(ov * ov) * (1.0 / HEAD_DIM) + RMS_EPS)
        gate, _ = _silu_parts(z_ref[...])
        y_ref[...] = ov * rn * nw_ref[...] * gate

    row = pl.BlockSpec((None, tm, width), lambda i: (0, i, 0))
    return pl.pallas_call(
        body,
        name="gdn_post_fwd",
        grid=(lp // tm,),
        in_specs=[row, row, pl.BlockSpec((1, width), lambda i: (0, 0))],
        out_specs=row,
        out_shape=jax.ShapeDtypeStruct((1, lp, width), F32),
        compiler_params=_params(("arbitrary",)),
    )(o, z, nw_b)


def gdn_post_bwd(o, z, dy, nw_b, *, tm):
    _, lp, width = o.shape

    def body(o_ref, z_ref, dy_ref, nw_ref, do_ref, dz_ref, dnw_ref):
        ov = o_ref[...]
        rn = lax.rsqrt(_head_sum(ov * ov) * (1.0 / HEAD_DIM) + RMS_EPS)
        yn = ov * rn
        gate, dgate = _silu_parts(z_ref[...])
        d_on = dy_ref[...] * gate
        dz_ref[...] = dy_ref[...] * yn * nw_ref[...] * dgate
        a = d_on * nw_ref[...]
        do_ref[...] = rn * (a - yn * (_head_sum(a * yn) * (1.0 / HEAD_DIM)))

        @pl.when(pl.program_id(0) == 0)
        def _():
            dnw_ref[...] = jnp.zeros_like(dnw_ref)

        dnw_ref[...] += _row_partial(d_on * yn)

    row = pl.BlockSpec((None, tm, width), lambda i: (0, i, 0))
    return pl.pallas_call(
        body,
        name="gdn_post_bwd",
        grid=(lp // tm,),
        in_specs=[row, row, row, pl.BlockSpec((1, width), lambda i: (0, 0))],
        out_specs=[row, row, pl.BlockSpec((8, width), lambda i: (0, 0))],
        out_shape=[jax.ShapeDtypeStruct((1, lp, width), F32)] * 2 + [jax.ShapeDtypeStruct((8, width), F32)],
        compiler_params=_params(("arbitrary",)),
    )(o, z, dy, nw_b)


def head_lane_sum(x):
    s_n, rows, width = x.shape

    def body(x_ref, o_ref):
        lane = lax.broadcasted_iota(jnp.int32, (rows, HEAD_DIM), 1)
        acc = jnp.zeros((rows, HEAD_DIM), F32)
        for h in range(width // HEAD_DIM):
            s = jnp.sum(x_ref[:, h * HEAD_DIM:(h + 1) * HEAD_DIM], axis=-1, keepdims=True)
            acc = jnp.where(lane == h, s, acc)
        o_ref[...] = acc

    return pl.pallas_call(
        body,
        name="head_lane_sum",
        grid=(s_n,),
        in_specs=[pl.BlockSpec((None, rows, width), lambda s: (s, 0, 0))],
        out_specs=pl.BlockSpec((None, rows, HEAD_DIM), lambda s: (s, 0, 0)),
        out_shape=jax.ShapeDtypeStruct((s_n, rows, HEAD_DIM), F32),
        compiler_params=_params(("arbitrary",)),
    )(x)


def ffn_act_fwd(up, conv_w, *, tm, name):
    _, lp, c_w = up.shape
    taps = conv_w.shape[1]

    def body(u_ref, halo_ref, g_ref, w_ref, o_ref):
        first_tile = pl.program_id(1) == 0

        def strip(r0, prev_of):
            rows = pl.ds(r0, SUB)
            for c0 in range(0, c_w, LANES):
                cs = slice(c0, c0 + LANES)
                cur = u_ref[rows, cs]
                prev = prev_of(cs)
                conv = w_ref[taps - 1:taps, cs] * cur
                for j in range(taps - 1):
                    conv += w_ref[j:j + 1, cs] * _shift_down(cur, prev, taps - 1 - j)
                y, _ = _silu_parts(conv)
                o_ref[rows, cs] = y * g_ref[rows, cs]

        strip(0, lambda cs: jnp.where(first_tile, 0.0, halo_ref[:, cs]))

        def loop_body(s, carry):
            r0 = pl.multiple_of(s * SUB, SUB)
            strip(r0, lambda cs: u_ref[pl.ds(pl.multiple_of(r0 - SUB, SUB), SUB), cs])
            return carry

        lax.fori_loop(1, tm // SUB, loop_body, 0)

    return pl.pallas_call(
        body,
        name=name,
        grid=(2, lp // tm),
        in_specs=[
            pl.BlockSpec((None, tm, c_w), lambda s, i: (s, i, 0)),
            pl.BlockSpec((None, HALO, c_w), lambda s, i: (s, _halo_index(i, tm), 0)),
            pl.BlockSpec((None, tm, c_w), lambda s, i: (2 + s, i, 0)),
            pl.BlockSpec((None, taps, c_w), lambda s, i: (s, 0, 0)),
        ],
        out_specs=pl.BlockSpec((None, tm, c_w), lambda s, i: (s, i, 0)),
        out_shape=jax.ShapeDtypeStruct((2, lp, c_w), F32),
        compiler_params=_params(("arbitrary", "arbitrary")),
    )(up, up, up, conv_w)


def ffn_act_bwd(up, dact, conv_w, *, tm, name):
    _, lp, c_w = up.shape
    taps = conv_w.shape[1]
    last = lp // tm - 1
    n_strips = tm // SUB

    def body(u_ref, halo_ref, g_ref, d_ref, w_ref, dup_ref, dw_ref, below):
        step = pl.program_id(1)
        first_tile = step == last

        @pl.when(step == 0)
        def _():
            below[...] = jnp.zeros_like(below)
            dw_ref[...] = jnp.zeros_like(dw_ref)

        def strip(r0, prev_of):
            rows = pl.ds(r0, SUB)
            for c0 in range(0, c_w, LANES):
                cs = slice(c0, c0 + LANES)
                cur = u_ref[rows, cs]
                prev = prev_of(cs)
                shifted = [_shift_down(cur, prev, taps - 1 - j) for j in range(taps)]
                conv = w_ref[0:1, cs] * shifted[0]
                for j in range(1, taps):
                    conv += w_ref[j:j + 1, cs] * shifted[j]
                y, dsilu = _silu_parts(conv)
                d = d_ref[rows, cs]
                dup_ref[1, rows, cs] = d * y
                dc = d * g_ref[rows, cs] * dsilu
                nxt = below[:, cs]
                dx = w_ref[taps - 1:taps, cs] * dc
                for j in range(taps - 1):
                    dx += w_ref[j:j + 1, cs] * _shift_up(dc, nxt, taps - 1 - j)
                dup_ref[0, rows, cs] = dx
                below[:, cs] = dc
                for j in range(taps):
                    dw_ref[j, :, cs] += dc * shifted[j]

        def loop_body(it, carry):
            r0 = pl.multiple_of((n_strips - 1 - it) * SUB, SUB)
            strip(r0, lambda cs: u_ref[pl.ds(pl.multiple_of(r0 - SUB, SUB), SUB), cs])
            return carry

        lax.fori_loop(0, n_strips - 1, loop_body, 0)
        strip(0, lambda cs: jnp.where(first_tile, 0.0, halo_ref[:, cs]))

    return pl.pallas_call(
        body,
        name=name,
        grid=(2, lp // tm),
        in_specs=[
            pl.BlockSpec((None, tm, c_w), lambda s, i: (s, last - i, 0)),
            pl.BlockSpec((None, HALO, c_w), lambda s, i: (s, _halo_index(last - i, tm), 0)),
            pl.BlockSpec((None, tm, c_w), lambda s, i: (2 + s, last - i, 0)),
            pl.BlockSpec((None, tm, c_w), lambda s, i: (s, last - i, 0)),
            pl.BlockSpec((None, taps, c_w), lambda s, i: (s, 0, 0)),
        ],
        out_specs=[
            pl.BlockSpec((2, None, tm, c_w), lambda s, i: (0, s, last - i, 0)),
            pl.BlockSpec((None, taps, SUB, c_w), lambda s, i: (s, 0, 0, 0)),
        ],
        out_shape=[jax.ShapeDtypeStruct((2, 2, lp, c_w), F32), jax.ShapeDtypeStruct((2, taps, SUB, c_w), F32)],
        scratch_shapes=[pltpu.VMEM((SUB, c_w), F32)],
        compiler_params=_params(("arbitrary", "arbitrary")),
    )(up, up, up, dact, conv_w)


def sc_fwd(pb, conv_w, *, tm, cb):
    _, lp, width = pb.shape
    taps = conv_w.shape[0]

    def body(x_ref, halo_ref, w_ref, o_ref):
        first_tile = pl.program_id(1) == 0

        def strip(r0, prev_of):
            rows = pl.ds(r0, SUB)
            for c0 in range(0, cb, LANES):
                cs = slice(c0, c0 + LANES)
                cur = x_ref[1, rows, cs] * x_ref[2, rows, cs]
                prev = prev_of(cs)
                conv = w_ref[taps - 1:taps, cs] * cur
                for j in range(taps - 1):
                    conv += w_ref[j:j + 1, cs] * _shift_down(cur, prev, taps - 1 - j)
                o_ref[rows, cs] = x_ref[0, rows, cs] * conv

        strip(0, lambda cs: jnp.where(first_tile, 0.0, halo_ref[1, :, cs] * halo_ref[2, :, cs]))

        def loop_body(k, carry):
            r0 = pl.multiple_of(k * SUB, SUB)
            before = pl.ds(pl.multiple_of(r0 - SUB, SUB), SUB)
            strip(r0, lambda cs: x_ref[1, before, cs] * x_ref[2, before, cs])
            return carry

        lax.fori_loop(1, tm // SUB, loop_body, 0)

    return pl.pallas_call(
        body,
        name="sc_fwd",
        grid=(width // cb, lp // tm),
        in_specs=[
            pl.BlockSpec((3, tm, cb), lambda j, i: (0, i, j)),
            pl.BlockSpec((3, HALO, cb), lambda j, i: (0, _halo_index(i, tm), j)),
            pl.BlockSpec((taps, cb), lambda j, i: (0, j)),
        ],
        out_specs=pl.BlockSpec((None, tm, cb), lambda j, i: (0, i, j)),
        out_shape=jax.ShapeDtypeStruct((1, lp, width), F32),
        compiler_params=_params(("arbitrary", "arbitrary")),
    )(pb, pb, conv_w)


def sc_bwd(pb, ds, conv_w, *, tm, cb):
    _, lp, width = pb.shape
    taps = conv_w.shape[0]
    last = lp // tm - 1
    n_strips = tm // SUB

    def body(x_ref, halo_ref, d_ref, w_ref, dx_ref, dw_ref, below):
        step = pl.program_id(1)
        first_tile = step == last

        @pl.when(step == 0)
        def _():
            below[...] = jnp.zeros_like(below)
            dw_ref[...] = jnp.zeros_like(dw_ref)

        def strip(r0, prev_of):
            rows = pl.ds(r0, SUB)
            for c0 in range(0, cb, LANES):
                cs = slice(c0, c0 + LANES)
                gate, left, right = x_ref[0, rows, cs], x_ref[1, rows, cs], x_ref[2, rows, cs]
                cur = left * right
                prev = prev_of(cs)
                shifted = [_shift_down(cur, prev, taps - 1 - j) for j in range(taps)]
                conv = w_ref[0:1, cs] * shifted[0]
                for j in range(1, taps):
                    conv += w_ref[j:j + 1, cs] * shifted[j]
                d = d_ref[rows, cs]
                dx_ref[0, rows, cs] = d * conv
                dc = d * gate
                nxt = below[:, cs]
                dp = w_ref[taps - 1:taps, cs] * dc
                for j in range(taps - 1):
                    dp += w_ref[j:j + 1, cs] * _shift_up(dc, nxt, taps - 1 - j)
                dx_ref[1, rows, cs] = dp * right
                dx_ref[2, rows, cs] = dp * left
                below[:, cs] = dc
                for j in range(taps):
                    dw_ref[j, :, cs] += dc * shifted[j]

        def loop_body(it, carry):
            r0 = pl.multiple_of((n_strips - 1 - it) * SUB, SUB)
            before = pl.ds(pl.multiple_of(r0 - SUB, SUB), SUB)
            strip(r0, lambda cs: x_ref[1, before, cs] * x_ref[2, before, cs])
            return carry

        lax.fori_loop(0, n_strips - 1, loop_body, 0)
        strip(0, lambda cs: jnp.where(first_tile, 0.0, halo_ref[1, :, cs] * halo_ref[2, :, cs]))

    tile_spec = pl.BlockSpec((3, tm, cb), lambda j, i: (0, last - i, j))
    return pl.pallas_call(
        body,
        name="sc_bwd",
        grid=(width // cb, lp // tm),
        in_specs=[
            tile_spec,
            pl.BlockSpec((3, HALO, cb), lambda j, i: (0, _halo_index(last - i, tm), j)),
            pl.BlockSpec((None, tm, cb), lambda j, i: (0, last - i, j)),
            pl.BlockSpec((taps, cb), lambda j, i: (0, j)),
        ],
        out_specs=[tile_spec, pl.BlockSpec((taps, SUB, cb), lambda j, i: (0, 0, j))],
        out_shape=[jax.ShapeDtypeStruct((3, lp, width), F32), jax.ShapeDtypeStruct((taps, SUB, width), F32)],
        scratch_shapes=[pltpu.VMEM((SUB, cb), F32)],
        compiler_params=_params(("arbitrary", "arbitrary")),
    )(pb, pb, ds, conv_w)


TILE_BYTES = 1536 * 1024


def _rows_tile(rows, cols, multiple=8):
    if rows * cols * 4 <= TILE_BYTES or rows % multiple:
        return rows
    best = multiple
    for t in range(multiple, rows + 1, multiple):
        if rows % t == 0 and t * cols * 4 <= TILE_BYTES:
            best = t
    return best


def pair_sum(g, landed, core, out_dtype, name):
    _, rows, cols = g.shape
    half = rows // 2
    tr = _rows_tile(half, cols, 16)
    nb = half // tr

    def body(c_ref, g_ref, l_ref, o_ref):
        o_ref[...] = (g_ref[...] + l_ref[...]).astype(out_dtype)

    return pl.pallas_call(
        body,
        name=name,
        grid_spec=pltpu.PrefetchScalarGridSpec(
            num_scalar_prefetch=1,
            grid=(4, nb),
            in_specs=[
                pl.BlockSpec((None, tr, cols), lambda s, i, c: (s, c[0] * nb + i, 0)),
                pl.BlockSpec((None, tr, cols), lambda s, i, c: (s, i, 0)),
            ],
            out_specs=pl.BlockSpec((None, tr, cols), lambda s, i, c: (s, i, 0)),
        ),
        out_shape=jax.ShapeDtypeStruct((4, half, cols), out_dtype),
        compiler_params=_params(("arbitrary", "arbitrary")),
    )(core, g, landed)


def chip_sum(x, name):
    _, rows, cols = x.shape
    tr = _rows_tile(rows, cols, 16)

    def body(x0, x1, x2, x3, o_ref):
        acc = x0[...].astype(F32) + x1[...].astype(F32)
        o_ref[...] = (acc + x2[...].astype(F32)) + x3[...].astype(F32)

    return pl.pallas_call(
        body,
        name=name,
        grid=(rows // tr,),
        in_specs=[pl.BlockSpec((None, tr, cols), lambda i, k=k: (k, i, 0)) for k in range(4)],
        out_specs=pl.BlockSpec((tr, cols), lambda i: (i, 0)),
        out_shape=jax.ShapeDtypeStruct((rows, cols), F32),
        compiler_params=_params(("arbitrary",)),
    )(x, x, x, x)


def adamw(w, g, m, v, name):
    shape = w.shape
    cols = shape[-1]
    rows = w.size // cols
    tr = _rows_tile(rows, cols)

    def body(w_ref, g_ref, m_ref, v_ref, d_ref, m2_ref, v2_ref):
        gv = g_ref[...]
        m2 = ADAM_B1 * m_ref[...] + (1.0 - ADAM_B1) * gv
        v2 = ADAM_B2 * v_ref[...] + (1.0 - ADAM_B2) * (gv * gv)
        m_hat = m2 / (1.0 - ADAM_B1 ** ADAM_STEP)
        v_hat = v2 / (1.0 - ADAM_B2 ** ADAM_STEP)
        d_ref[...] = -ADAM_LR * (m_hat / (jnp.sqrt(v_hat) + ADAM_EPS) + ADAM_WD * w_ref[...])
        m2_ref[...] = m2
        v2_ref[...] = v2

    spec = pl.BlockSpec((tr, cols), lambda i: (i, 0))
    outs = pl.pallas_call(
        body,
        name=name,
        grid=(rows // tr,),
        in_specs=[spec] * 4,
        out_specs=[spec] * 3,
        out_shape=[jax.ShapeDtypeStruct((rows, cols), F32)] * 3,
        compiler_params=_params(("arbitrary",)),
    )(*[t.reshape(rows, cols) for t in (w, g, m, v)])
    return tuple(o.reshape(shape) for o in outs)


MESH_ID = pl.DeviceIdType.MESH
ANY = pl.BlockSpec(memory_space=pl.ANY)


def _place():
    x, y, c = lax.axis_index("x"), lax.axis_index("y"), lax.axis_index("c")
    other_chips = [(1 - x, y), (x, 1 - y), (1 - x, 1 - y)]
    return x, y, c, other_chips


def all_gather_shards(bufs, name):
    n = len(bufs)

    def body(*refs):
        x_refs, o_refs = refs[:n], refs[n:2 * n]
        copies = _gather_copies(x_refs, o_refs, *refs[2 * n:])
        _gather_start(copies)
        _gather_finish(copies)

    outs = pl.pallas_call(
        body,
        name=name,
        in_specs=[ANY] * n,
        out_specs=[ANY] * n,
        out_shape=_gather_out_shapes(bufs),
        scratch_shapes=_gather_sems(n),
    )(*bufs)
    return _set_own_slots(outs, bufs)


def _gather_out_shapes(bufs):
    return [jax.ShapeDtypeStruct((4,) + b.shape, b.dtype) for b in bufs]


def _gather_sems(n):
    return [pltpu.SemaphoreType.DMA((6 * n,)), pltpu.SemaphoreType.DMA((6 * n,))]


def _set_own_slots(outs, bufs):
    if not outs:
        return []
    me = 2 * lax.axis_index("x") + lax.axis_index("y")
    return [lax.dynamic_update_index_in_dim(o, b, me, 0) for o, b in zip(outs, bufs)]


def _gather_copies(x_refs, o_refs, send_sems, recv_sems):
    x, y, c, chips = _place()
    me = 2 * x + y
    sibling = (x, y, 1 - c)

    def part(a, slot, hf):
        half = x_refs[a].shape[0] // 2
        return o_refs[a].at[slot, pl.ds(hf * half, half), :]

    def mine(a):
        half = x_refs[a].shape[0] // 2
        return x_refs[a].at[pl.ds(c * half, half), :]

    def copy(k, src, dst, to):
        return pltpu.make_async_remote_copy(src_ref=src, dst_ref=dst, send_sem=send_sems.at[k],
                                            recv_sem=recv_sems.at[k], device_id=to, device_id_type=MESH_ID)

    sends, arrivals, passes, passed = [], [], [], []
    for a in range(len(x_refs)):
        for j, (px, py) in enumerate(chips):
            landed, theirs = part(a, 2 * px + py, c), part(a, 2 * px + py, 1 - c)
            sends.append(copy(6 * a + j, mine(a), part(a, me, c), (px, py, c)))
            arrivals.append(copy(6 * a + j, mine(a), landed, (px, py, c)))
            passes.append(copy(6 * a + 3 + j, landed, landed, sibling))
            passed.append(copy(6 * a + 3 + j, theirs, theirs, sibling))
    return sends, arrivals, passes, passed


def _gather_start(copies):
    for cp in copies[0]:
        cp.start()


def _gather_finish(copies):
    sends, arrivals, passes, passed = copies
    for arrival, cp in zip(arrivals, passes):
        arrival.wait_recv()
        cp.start()
    for cp in passed:
        cp.wait_recv()
    for cp in sends + passes:
        cp.wait_send()


def swap_halves(bufs, name):
    n = len(bufs)

    def body(*refs):
        x_refs, o_refs = refs[:n], refs[n:2 * n]
        send_sems, recv_sems = refs[2 * n:]
        x, y, c, _ = _place()
        copies = []
        for a in range(n):
            half = bufs[a].shape[1] // 2
            cp = pltpu.make_async_remote_copy(src_ref=x_refs[a].at[:, pl.ds((1 - c) * half, half), :], dst_ref=o_refs[a],
                                              send_sem=send_sems.at[a], recv_sem=recv_sems.at[a],
                                              device_id=(x, y, 1 - c), device_id_type=MESH_ID)
            cp.start()
            copies.append(cp)
        for cp in copies:
            cp.wait()

    return pl.pallas_call(
        body,
        name=name,
        in_specs=[ANY] * n,
        out_specs=[ANY] * n,
        out_shape=[jax.ShapeDtypeStruct((4, b.shape[1] // 2, b.shape[2]), b.dtype) for b in bufs],
        scratch_shapes=[pltpu.SemaphoreType.DMA((n,)), pltpu.SemaphoreType.DMA((n,))],
    )(*bufs)


def scatter_to_chips(bufs, name):
    n = len(bufs)

    def body(*refs):
        x_refs, o_refs = refs[:n], refs[n:2 * n]
        copies = _scatter_copies(x_refs, o_refs, *refs[2 * n:])
        _scatter_start(copies)
        _scatter_finish(copies)

    outs = pl.pallas_call(
        body,
        name=name,
        in_specs=[ANY] * n,
        out_specs=[ANY] * n,
        out_shape=[jax.ShapeDtypeStruct(b.shape, b.dtype) for b in bufs],
        scratch_shapes=_scatter_sems(n),
    )(*bufs)
    return _keep_own_slots(outs, bufs)


def _scatter_sems(n):
    return [pltpu.SemaphoreType.DMA((3 * n,)), pltpu.SemaphoreType.DMA((3 * n,))]


def _keep_own_slots(outs, bufs):
    if not outs:
        return []
    me = 2 * lax.axis_index("x") + lax.axis_index("y")
    return [lax.dynamic_update_index_in_dim(o, lax.dynamic_index_in_dim(b, me, 0, keepdims=False), me, 0)
            for o, b in zip(outs, bufs)]


def _scatter_copies(x_refs, o_refs, send_sems, recv_sems):
    x, y, c, chips = _place()
    me = 2 * x + y

    def copy(a, j, src_slot, dst_slot, px, py):
        return pltpu.make_async_remote_copy(src_ref=x_refs[a].at[src_slot], dst_ref=o_refs[a].at[dst_slot],
                                            send_sem=send_sems.at[3 * a + j], recv_sem=recv_sems.at[3 * a + j],
                                            device_id=(px, py, c), device_id_type=MESH_ID)

    sends = [copy(a, j, 2 * px + py, me, px, py) for a in range(len(x_refs)) for j, (px, py) in enumerate(chips)]
    arrivals = [copy(a, j, me, 2 * px + py, px, py) for a in range(len(x_refs)) for j, (px, py) in enumerate(chips)]
    return sends, arrivals


def _scatter_start(copies):
    for cp in copies[0]:
        cp.start()


def _scatter_finish(copies):
    for cp in copies[1]:
        cp.wait_recv()
    for cp in copies[0]:
        cp.wait_send()


def share_halves(groups, name):
    bufs = [b for grp in groups for b in grp]
    where = [(gi, li) for gi, grp in enumerate(groups) for li in range(len(grp))]
    n = len(bufs)

    def body(*refs):
        x_refs, o_refs = refs[:n], refs[n:n + len(groups)]
        send_sems, recv_sems = refs[n + len(groups):]
        x, y, c, _ = _place()
        sent, arrive = [], []
        for a, (gi, li) in enumerate(where):

            def copy(hf, a=a, gi=gi, li=li):
                return pltpu.make_async_remote_copy(src_ref=x_refs[a], dst_ref=o_refs[gi].at[li, hf],
                                                    send_sem=send_sems.at[a], recv_sem=recv_sems.at[a],
                                                    device_id=(x, y, 1 - c), device_id_type=MESH_ID)

            sent.append(copy(c))
            arrive.append(copy(1 - c))
        for cp in sent:
            cp.start()
        for cp in arrive:
            cp.wait_recv()
        for cp in sent:
            cp.wait_send()

    outs = pl.pallas_call(
        body,
        name=name,
        in_specs=[ANY] * n,
        out_specs=[ANY] * len(groups),
        out_shape=[jax.ShapeDtypeStruct((len(grp), 2) + grp[0].shape, grp[0].dtype) for grp in groups],
        scratch_shapes=[pltpu.SemaphoreType.DMA((n,)), pltpu.SemaphoreType.DMA((n,))],
    )(*bufs)
    c = lax.axis_index("c")
    full = [lax.dynamic_update_index_in_dim(o, jnp.stack(grp), c, 1) for o, grp in zip(outs, groups)]
    return [t.reshape(t.shape[0], 2 * t.shape[2], t.shape[3]) for t in full]


def pair_sums(bufs, dtypes, tag):
    core = lax.axis_index("c").astype(jnp.int32).reshape(1)
    landed = swap_halves(bufs, "rs_pair_" + tag)
    return [pair_sum(b, l, core, dt, "rs_pair_sum_%s%d" % (tag, i)) for i, (b, l, dt) in enumerate(zip(bufs, landed, dtypes))]


def _row_tiles(length):
    return (640, 320) if length > 2048 else (128, 64)


def _local_step(x, target, wt, late_shards, layout_late, reduce_early):
    seq, d = x.shape
    length = N_META + seq
    tm, tm_ffn = _row_tiles(length)
    lp = -(-length // tm) * tm
    tail = jnp.zeros((lp - length, d), F32)
    h0 = jnp.concatenate([wt["meta"], x, tail], axis=0)[None]
    tgt = jnp.concatenate([jnp.zeros((N_META, d), F32), target, tail], axis=0)
    nn = functools.partial(mm_nn, tm=tm)
    nt = functools.partial(mm_nt, tm=tm)
    tn = functools.partial(mm_tn, tm=tm)
    ln_g = [wt["ln_mix_g"][0:1], wt["ln_ffn_g"][0:1], wt["ln_mix_g"][1:2], wt["ln_ffn_g"][1:2]]
    ln_b = [wt["ln_mix_b"][0:1], wt["ln_ffn_b"][0:1], wt["ln_mix_b"][1:2], wt["ln_ffn_b"][1:2]]

    p5 = nn(h0, wt["a5"], name="a_in5")
    pz = nn(h0, wt["az"], name="a_inz")
    qkvbg = gdn_pre_fwd(p5, wt["a_conv3"], wt["alog_b"], wt["dtb_b"], tm=tm, cb=2 * HEAD_DIM)
    o, states, tinv, late_stacks = gdn_chunk_fwd(qkvbg, late_shards)
    wt = {**wt, **layout_late(late_stacks)}
    onz = gdn_post_fwd(o[None], pz, wt["anorm_b"], tm=tm)
    r1, h1 = ln_fwd(h0, nn(onz, wt["a_out"], name="a_out"), ln_g[0], ln_b[0], tm=tm, name="ln1")
    up0 = nn(h1, wt["up"][0], name="up0")
    act0 = ffn_act_fwd(up0, wt["fconv"][0], tm=tm_ffn, name="ffn_act0")
    r2, h2 = ln_fwd(h1, nn(act0, wt["down"][0], name="down0"), ln_g[1], ln_b[1], tm=tm, name="ln2")
    pb = nn(h2, wt["b_in"], name="b_in")
    sc = sc_fwd(pb, wt["b_conv"], tm=tm_ffn, cb=d)
    r3, h3 = ln_fwd(h2, nn(sc, wt["b_out"], name="b_out"), ln_g[2], ln_b[2], tm=tm, name="ln3")
    up1 = nn(h3, wt["up"][1], name="up1")
    act1 = ffn_act_fwd(up1, wt["fconv"][1], tm=tm_ffn, name="ffn_act1")
    r4, h4 = ln_fwd(h3, nn(act1, wt["down"][1], name="down1"), ln_g[3], ln_b[3], tm=tm, name="ln4")

    dh4, loss_part = loss_grad(h4, tgt, first=N_META, count=seq, tm=tm)

    grads = {}
    dr4, dgb4 = ln_bwd(r4, dh4, ln_g[3], tm=tm, name="ln4_bwd")
    d_down1 = tn(act1, dr4, name="d_down1")
    dact1 = nt(dr4, wt["down"][1], name="d_act1")
    dup1, dfconv1 = ffn_act_bwd(up1, dact1, wt["fconv"][1], tm=tm_ffn, name="ffn_act1_bwd")
    dup1 = dup1.reshape(up1.shape)
    d_up1 = tn(h3, dup1, name="d_up1")
    dh3 = nt(dup1, wt["up"][1], res=dr4, res_scale=ALPHA, name="d_h3")

    dr3, dgb3 = ln_bwd(r3, dh3, ln_g[2], tm=tm, name="ln3_bwd")
    d_bout = tn(sc, dr3, name="d_b_out")
    dsc = nt(dr3, wt["b_out"], name="d_sc")
    dpb, dbconv = sc_bwd(pb, dsc, wt["b_conv"], tm=tm_ffn, cb=d)
    d_bin = tn(h2, dpb, name="d_b_in")
    dh2 = nt(dpb, wt["b_in"], res=dr3, res_scale=ALPHA, name="d_h2")

    dr2, dgb2 = ln_bwd(r2, dh2, ln_g[1], tm=tm, name="ln2_bwd")
    d_down0 = tn(act0, dr2, name="d_down0")
    dact0 = nt(dr2, wt["down"][0], name="d_act0")
    dup0, dfconv0 = ffn_act_bwd(up0, dact0, wt["fconv"][0], tm=tm_ffn, name="ffn_act0_bwd")
    dup0 = dup0.reshape(up0.shape)
    d_up0 = tn(h1, dup0, name="d_up0")
    dh1 = nt(dup0, wt["up"][0], res=dr2, res_scale=ALPHA, name="d_h1")
    grads["b_w_in"] = [d_bin[0].transpose(1, 0, 2).reshape(d, 4, 3 * d // 4).transpose(1, 0, 2)]
    grads["b_w_out"] = [d_bout.reshape(4, d // 4, d)]
    grads["ffn_w_up"] = [d_up0[0], d_up1[0]]
    grads["ffn_w_down"] = [t.reshape(4, -1, d) for t in (d_down0, d_down1)]
    leaving = reduce_early(grads)

    dr1, dgb1 = ln_bwd(r1, dh1, ln_g[0], tm=tm, name="ln1_bwd")
    d_aout = tn(onz, dr1, name="d_a_out")
    donz = nt(dr1, wt["a_out"], name="d_onz")
    d_o, dz, dnw = gdn_post_bwd(o[None], pz, donz, wt["anorm_b"], tm=tm)
    dqkvbg, landed = gdn_chunk_bwd(qkvbg, states, tinv, d_o[0], leaving)
    dp5, daconv, dscal = gdn_pre_bwd(p5, dqkvbg, wt["a_conv3"], wt["alog_b"], wt["dtb_b"], tm=tm, cb=2 * HEAD_DIM)
    d_a5 = tn(h0, dp5, name="d_a_in5")
    d_az = tn(h0, dz, name="d_a_inz")
    dh0 = nt(dp5, wt["a5"], res=dr1, res_scale=ALPHA, name="d_h0a")
    dh0 = nt(dz, wt["az"], res=dh0, res_scale=1.0, name="d_h0")

    width = HEADS * HEAD_DIM
    d_ba = head_lane_sum(d_a5[0, 3:5])[:, :, :HEADS]
    d_a_in = jnp.concatenate([d_a5[0, 0], d_a5[0, 1], d_a5[0, 2], d_az[0, 0], d_ba[0], d_ba[1]], axis=1)
    n_in = d_a_in.shape[1] // 4
    grads["a_w_in"] = [d_a_in.reshape(d, 4, n_in).transpose(1, 0, 2)]
    grads["a_w_out"] = [d_aout.reshape(4, width // 4, d)]
    grads["a_conv"] = daconv.sum(axis=2).transpose(1, 0, 2).reshape(1, GDN_CONV, 3 * width)
    per_head = dscal.reshape(2, 8, HEADS, HEAD_DIM).sum(axis=(1, 3))
    grads["a_log"] = per_head[0][None]
    grads["a_dt_bias"] = per_head[1][None]
    grads["a_norm"] = dnw.reshape(8, HEADS, HEAD_DIM).sum(axis=(0, 1))[None]
    grads["b_conv"] = dbconv.sum(axis=1)[None]
    lns = [dgb1, dgb2, dgb3, dgb4]
    grads["ln_mix_g"] = jnp.stack([lns[0][0].sum(0), lns[2][0].sum(0)])
    grads["ln_mix_b"] = jnp.stack([lns[0][1].sum(0), lns[2][1].sum(0)])
    grads["ln_ffn_g"] = jnp.stack([lns[1][0].sum(0), lns[3][0].sum(0)])
    grads["ln_ffn_b"] = jnp.stack([lns[1][1].sum(0), lns[3][1].sum(0)])
    grads["ffn_conv"] = jnp.stack([t.sum(axis=2).transpose(1, 0, 2).reshape(FFN_CONV, -1) for t in (dfconv0, dfconv1)])
    grads["meta"] = dh0[0, :N_META]
    return loss_part, dh0, grads, landed


WEIGHTS = ["meta", "a_w_in", "a_conv", "a_log", "a_dt_bias", "a_norm", "a_w_out", "b_w_in", "b_conv", "b_w_out",
           "ln_mix_g", "ln_mix_b", "ffn_w_up", "ffn_conv", "ffn_w_down", "ln_ffn_g", "ln_ffn_b"]
EARLY_WEIGHTS = ["a_w_in", "a_w_out"]
LATE_WEIGHTS = ["b_w_in", "b_w_out", "ffn_w_up", "ffn_w_down"]
MATMUL_WEIGHTS = EARLY_WEIGHTS + LATE_WEIGHTS
SMALL_SHARDED = ["a_conv", "b_conv", "ffn_conv", "meta"]
REPLICATED = ["a_log", "a_dt_bias", "a_norm", "ln_mix_g", "ln_mix_b", "ln_ffn_g", "ln_ffn_b"]
SHARD_AXIS = {"meta": 1, "a_w_in": 2, "a_conv": 2, "a_w_out": 1, "b_w_in": 2, "b_conv": 2, "b_w_out": 1,
              "ffn_w_up": 2, "ffn_conv": 2, "ffn_w_down": 1}
PACK_COLS = 1024
PACK_ROWS_MULTIPLE = 32


def _pack(pieces, lead=()):
    flat = jnp.concatenate([p.reshape(lead + (-1,)) for p in pieces], axis=-1)
    n = flat.shape[-1]
    rows = -(-n // (PACK_COLS * PACK_ROWS_MULTIPLE)) * PACK_ROWS_MULTIPLE
    flat = jnp.pad(flat, [(0, 0)] * len(lead) + [(0, rows * PACK_COLS - n)])
    return flat.reshape(lead + (rows, PACK_COLS))


def _unpack(buf, shapes, lead=()):
    flat = buf.reshape(lead + (-1,))
    out, off = [], 0
    for shp in shapes:
        n = 1
        for s in shp:
            n *= s
        out.append(flat[..., off:off + n].reshape(lead + tuple(shp)))
        off += n
    return out


def _join_shards(stacked, axis):
    return jnp.concatenate([stacked[k] for k in range(4)], axis=axis)


def _split_shards(full, axis):
    return jnp.stack(jnp.split(full, 4, axis=axis))


def _weight_layers(w, names):
    return [w[n][l].astype(BF16) for n in names for l in range(w[n].shape[0])]


def _per_weight(arrays, w, names):
    it = iter(arrays)
    return {n: [next(it) for _ in range(w[n].shape[0])] for n in names}


def _layout_early(full, w):
    width = HEADS * HEAD_DIM
    wt = {n: w[n] for n in ("ln_mix_g", "ln_mix_b", "ln_ffn_g", "ln_ffn_b")}
    w_in = _join_shards(full["a_w_in"][0], 1)
    d = w_in.shape[0]
    n_ff = full["ffn_conv"].shape[2] // 2
    blocks = [w_in[:, s * width:(s + 1) * width] for s in range(4)]
    b_exp = jnp.repeat(w_in[:, 4 * width:4 * width + HEADS], HEAD_DIM, axis=1)
    a_exp = jnp.repeat(w_in[:, 4 * width + HEADS:], HEAD_DIM, axis=1)
    wt["a5"] = jnp.stack([blocks[0], blocks[1], blocks[2], b_exp, a_exp])[None]
    wt["az"] = blocks[3][None, None]
    wt["a_out"] = full["a_w_out"][0].reshape(1, 1, width, d)
    wt["a_conv3"] = full["a_conv"][0].reshape(GDN_CONV, 3, width).transpose(1, 0, 2)
    wt["b_conv"] = full["b_conv"][0]
    wt["fconv"] = [full["ffn_conv"][l].reshape(FFN_CONV, 2, n_ff).transpose(1, 0, 2) for l in range(2)]
    wt["meta"] = full["meta"]
    wt["alog_b"] = jnp.repeat(w["a_log"][0], HEAD_DIM)[None]
    wt["dtb_b"] = jnp.repeat(w["a_dt_bias"][0], HEAD_DIM)[None]
    wt["anorm_b"] = jnp.tile(w["a_norm"][0], HEADS)[None]
    return wt


def _layout_late(full):
    d = full["b_w_in"][0].shape[1]
    n_ff = full["ffn_w_up"][0].shape[2]
    return {
        "b_in": _join_shards(full["b_w_in"][0], 1).reshape(d, 3, d).transpose(1, 0, 2)[None],
        "b_out": full["b_w_out"][0].reshape(1, 1, d, d),
        "up": [t[None] for t in full["ffn_w_up"]],
        "down": [t.reshape(2, 1, n_ff, d) for t in full["ffn_w_down"]],
    }


def kernel(x, meta, a_w_in, a_conv, a_log, a_dt_bias, a_norm, a_w_out, b_w_in, b_conv, b_w_out, ln_mix_g, ln_mix_b, ffn_w_up, ffn_conv, ffn_w_down, ln_ffn_g, ln_ffn_b, loss_target, m_meta, m_a_w_in, m_a_conv, m_a_log, m_a_dt_bias, m_a_norm, m_a_w_out, m_b_w_in, m_b_conv, m_b_w_out, m_ln_mix_g, m_ln_mix_b, m_ffn_w_up, m_ffn_conv, m_ffn_w_down, m_ln_ffn_g, m_ln_ffn_b, v_meta, v_a_w_in, v_a_conv, v_a_log, v_a_dt_bias, v_a_norm, v_a_w_out, v_b_w_in, v_b_conv, v_b_w_out, v_ln_mix_g, v_ln_mix_b, v_ffn_w_up, v_ffn_conv, v_ffn_w_down, v_ln_ffn_g, v_ln_ffn_b):
    w = dict(meta=meta, a_w_in=a_w_in, a_conv=a_conv, a_log=a_log, a_dt_bias=a_dt_bias, a_norm=a_norm, a_w_out=a_w_out,
             b_w_in=b_w_in, b_conv=b_conv, b_w_out=b_w_out, ln_mix_g=ln_mix_g, ln_mix_b=ln_mix_b, ffn_w_up=ffn_w_up,
             ffn_conv=ffn_conv, ffn_w_down=ffn_w_down, ln_ffn_g=ln_ffn_g, ln_ffn_b=ln_ffn_b)
    m = dict(meta=m_meta, a_w_in=m_a_w_in, a_conv=m_a_conv, a_log=m_a_log, a_dt_bias=m_a_dt_bias, a_norm=m_a_norm,
             a_w_out=m_a_w_out, b_w_in=m_b_w_in, b_conv=m_b_conv, b_w_out=m_b_w_out, ln_mix_g=m_ln_mix_g,
             ln_mix_b=m_ln_mix_b, ffn_w_up=m_ffn_w_up, ffn_conv=m_ffn_conv, ffn_w_down=m_ffn_w_down,
             ln_ffn_g=m_ln_ffn_g, ln_ffn_b=m_ln_ffn_b)
    v = dict(meta=v_meta, a_w_in=v_a_w_in, a_conv=v_a_conv, a_log=v_a_log, a_dt_bias=v_a_dt_bias, a_norm=v_a_norm,
             a_w_out=v_a_w_out, b_w_in=v_b_w_in, b_conv=v_b_conv, b_w_out=v_b_w_out, ln_mix_g=v_ln_mix_g,
             ln_mix_b=v_ln_mix_b, ffn_w_up=v_ffn_w_up, ffn_conv=v_ffn_conv, ffn_w_down=v_ffn_w_down,
             ln_ffn_g=v_ln_ffn_g, ln_ffn_b=v_ln_ffn_b)
    seq = x.shape[1]
    *stacks, small = all_gather_shards(_weight_layers(w, EARLY_WEIGHTS) + [_pack([w[n] for n in SMALL_SHARDED])],
                                       "gather_early")
    full = _per_weight(stacks, w, EARLY_WEIGHTS)
    for n, t in zip(SMALL_SHARDED, _unpack(small, [w[n].shape for n in SMALL_SHARDED], lead=(4,))):
        full[n] = _join_shards(t, SHARD_AXIS[n])

    def layout_late(late_stacks):
        return _layout_late(_per_weight(late_stacks, w, LATE_WEIGHTS))

    def reduce_early(grads):
        bufs = [g for n in LATE_WEIGHTS for g in grads[n]]
        return pair_sums(bufs, [BF16] * len(bufs), "late")

    loss_part, dh0, grads, landed_late = _local_step(x[0], loss_target[0], _layout_early(full, w),
                                                     _weight_layers(w, LATE_WEIGHTS), layout_late, reduce_early)
    pieces = [_split_shards(grads[n], SHARD_AXIS[n]) for n in SMALL_SHARDED]
    same = jnp.concatenate([grads[n].reshape(-1) for n in REPLICATED] + [jnp.sum(loss_part).reshape(1)])
    pieces.append(jnp.broadcast_to(same, (4,) + same.shape))
    bufs = [g for n in EARLY_WEIGHTS for g in grads[n]] + [_pack(pieces, lead=(4,))]
    landed = scatter_to_chips(pair_sums(bufs, [BF16] * (len(bufs) - 1) + [F32], "early"), "rs_chips_early")
    totals = [chip_sum(t, "rs_chip_sum%d" % i) for i, t in enumerate(landed + landed_late)]
    by_weight = _per_weight(totals[:len(bufs) - 1] + totals[len(bufs):], w, MATMUL_WEIGHTS)
    *shared, small_total = share_halves([by_weight[n] for n in MATMUL_WEIGHTS] + [[totals[len(bufs) - 1]]], "rs_share")
    grad_w = {n: t.reshape(w[n].shape) for n, t in zip(MATMUL_WEIGHTS, shared)}
    rest = SMALL_SHARDED + REPLICATED
    unpacked = _unpack(small_total[0], [w[n].shape for n in rest] + [()])
    grad_w.update(zip(rest, unpacked[:-1]))
    loss = unpacked[-1]
    grad_x = dh0[:, N_META:N_META + seq]
    steps = [adamw(w[n], grad_w[n], m[n], v[n], "adamw_" + n) for n in WEIGHTS]
    return (loss, grad_x, *[grad_w[n] for n in WEIGHTS], *[s[0] for s in steps], *[s[1] for s in steps],
            *[s[2] for s in steps])
```

```python
import functools

import jax
import jax.numpy as jnp
from jax import lax
from jax.experimental import pallas as pl
from jax.experimental.pallas import tpu as pltpu

F32 = jnp.float32
BF16 = jnp.bfloat16
HI = lax.Precision.HIGHEST

N_META = 16
HEADS = 8
HEAD_DIM = 128
CHUNK = 64
GDN_CONV = 4
SC_CONV = 3
FFN_CONV = 3
ALPHA = 4.0 ** 0.25
LN_EPS = 1e-5
RMS_EPS = 1e-6
L2_EPS = 1e-6
Q_SCALE = HEAD_DIM ** -0.5

ADAM_LR = 0.001
ADAM_B1 = 0.9
ADAM_B2 = 0.999
ADAM_EPS = 1e-08
ADAM_WD = 0.01
ADAM_STEP = 10

HALO = 8
VMEM_LIMIT = 48 * 1024 * 1024


def _params(sem=None):
    return pltpu.CompilerParams(dimension_semantics=sem, vmem_limit_bytes=VMEM_LIMIT)


def _dot(a, b, prec=None):
    return jnp.dot(a, b, preferred_element_type=F32, precision=prec)


def _dot_nt(a, b, prec=None):
    return lax.dot_general(a, b, (((1,), (1,)), ((), ())), preferred_element_type=F32, precision=prec)


def _dot_tn(a, b, prec=None):
    return lax.dot_general(a, b, (((0,), (0,)), ((), ())), preferred_element_type=F32, precision=prec)


def _sigmoid(x):
    return 1.0 / (1.0 + jnp.exp(-x))


def _tri_masks():
    r = lax.broadcasted_iota(jnp.int32, (CHUNK, CHUNK), 0)
    c = lax.broadcasted_iota(jnp.int32, (CHUNK, CHUNK), 1)
    return r >= c, r > c, r == c


def _split_hi_lo(x):
    hi = x.astype(BF16)
    return hi, (x - hi.astype(F32)).astype(BF16)


def _mask_dot(mask, x):
    hi, lo = _split_hi_lo(x)
    return _dot(mask, hi) + _dot(mask, lo)


@jax.custom_vjp
def _cumsum_rows(g):
    causal, _, _ = _tri_masks()
    return _mask_dot(causal.astype(BF16), g)


def _cumsum_rows_fwd(g):
    return _cumsum_rows(g), None


def _cumsum_rows_bwd(_, dy):
    _, strict, _ = _tri_masks()
    return (_mask_dot((~strict).astype(BF16), dy),)


_cumsum_rows.defvjp(_cumsum_rows_fwd, _cumsum_rows_bwd)


def _dot_split3(a, b):
    a_hi, a_lo = _split_hi_lo(a)
    b_hi, b_lo = _split_hi_lo(b)
    return _dot(a_hi, b_hi) + (_dot(a_hi, b_lo) + _dot(a_lo, b_hi))


def _gdn_m(ks, g64s, bbs):
    causal, strict, _ = _tri_masks()
    a = [_cumsum_rows(g) for g in g64s]
    decay = [jnp.exp(jnp.where(causal, x - x.T, -1e30)) for x in a]
    kk = [_dot_nt(k * b, k) for k, b in zip(ks, bbs)]
    return [jnp.where(strict, x * d, 0.0) for x, d in zip(kk, decay)]


def _gdn_inverse(ms):
    r = lax.broadcasted_iota(jnp.int32, (CHUNK, CHUNK), 0)
    c = lax.broadcasted_iota(jnp.int32, (CHUNK, CHUNK), 1)
    eye = (r == c).astype(F32)
    same = [jnp.right_shift(r, s) == jnp.right_shift(c, s) for s in (3, 4, 5)]
    d = [jnp.where(same[0], m, 0.0) for m in ms]
    p = [_dot(x, x) for x in d]
    t = [eye - x for x in d]
    t = [x + _dot(x, y) for x, y in zip(t, p)]
    p = [_dot(x, x) for x in p]
    t = [x + _dot(x, y) for x, y in zip(t, p)]
    for inner, outer in ((same[0], same[1]), (same[1], same[2]), (same[2], None)):
        joins = ~inner if outer is None else (outer & ~inner)
        o = [_dot(x, jnp.where(joins, m, 0.0)) for x, m in zip(t, ms)]
        t = [x - _dot(y, x) for x, y in zip(t, o)]
    res = [eye - x - _dot_split3(m, x) for m, x in zip(ms, t)]
    return [x + _dot(x, y) for x, y in zip(t, res)]


def _gdn_apply(qs, ks, vs, gbs, g64s, bbs, ss, ts):
    causal, _, _ = _tri_masks()
    n = range(len(qs))
    gc = [_cumsum_rows(g) for g in gbs]
    a = [_cumsum_rows(g) for g in g64s]
    decay = [jnp.exp(jnp.where(causal, x - x.T, -1e30)) for x in a]
    eg = [jnp.exp(x) for x in gc]
    u = [_dot(ts[h], vs[h] * bbs[h]) for h in n]
    w = [_dot(ts[h], ks[h] * bbs[h] * eg[h]) for h in n]
    qk = [_dot_nt(qs[h], ks[h]) * decay[h] for h in n]
    gl = [jnp.sum(g, axis=0, keepdims=True) for g in gbs]
    kd = [ks[h] * jnp.exp(gl[h] - gc[h]) for h in n]
    v_new = [u[h] - _dot(w[h], ss[h]) for h in n]
    o = [_dot(qs[h] * eg[h], ss[h]) + _dot(qk[h], v_new[h]) for h in n]
    s2 = [ss[h] * jnp.exp(gl[h]) + _dot_tn(kd[h], v_new[h]) for h in n]
    return o, s2


def _head_slices(h):
    return slice(h * HEAD_DIM, (h + 1) * HEAD_DIM), slice(h * HEAD_DIM, h * HEAD_DIM + CHUNK)


def _gdn_head_values(x_ref):
    out = [[], [], [], [], [], []]
    for h in range(HEADS):
        sl, sl64 = _head_slices(h)
        for lst, val in zip(out, (x_ref[0, :, sl], x_ref[1, :, sl], x_ref[2, :, sl], x_ref[4, :, sl],
                                  x_ref[4, :, sl64], x_ref[3, :, sl])):
            lst.append(val)
    return out


def gdn_chunk_fwd(qkvbg, gather=()):
    _, lp, width = qkvbg.shape
    n_chunks = lp // CHUNK
    n = len(gather)

    def body(x_ref, *refs):
        shard_refs, (o_ref, s_ref, t_ref), refs = refs[:n], refs[n:n + 3], refs[n + 3:]
        stack_refs, state, sems = refs[:n], refs[n], refs[n + 1:]
        copies = _gather_copies(shard_refs, stack_refs, *sems) if n else None

        @pl.when(pl.program_id(0) == 0)
        def _():
            state[...] = jnp.zeros_like(state)
            if n:
                _gather_start(copies)

        qs, ks, vs, gbs, g64s, bbs = _gdn_head_values(x_ref)
        ss = [state[h] for h in range(HEADS)]
        ts = _gdn_inverse(_gdn_m(ks, g64s, bbs))
        os_, s2 = _gdn_apply(qs, ks, vs, gbs, g64s, bbs, ss, ts)
        for h in range(HEADS):
            s_ref[0, h] = ss[h]
            t_ref[0, h] = ts[h]
            o_ref[:, _head_slices(h)[0]] = os_[h]
            state[h] = s2[h]

        if n:
            @pl.when(pl.program_id(0) == n_chunks - 1)
            def _():
                _gather_finish(copies)

    o, states, tinv, *stacks = pl.pallas_call(
        body,
        name="gdn_chunk_fwd",
        grid=(n_chunks,),
        in_specs=[pl.BlockSpec((5, CHUNK, width), lambda c: (0, c, 0))] + [ANY] * n,
        out_specs=[
            pl.BlockSpec((CHUNK, width), lambda c: (c, 0)),
            pl.BlockSpec((1, HEADS, HEAD_DIM, HEAD_DIM), lambda c: (c, 0, 0, 0)),
            pl.BlockSpec((1, HEADS, CHUNK, CHUNK), lambda c: (c, 0, 0, 0)),
        ] + [ANY] * n,
        out_shape=[
            jax.ShapeDtypeStruct((lp, width), F32),
            jax.ShapeDtypeStruct((n_chunks, HEADS, HEAD_DIM, HEAD_DIM), F32),
            jax.ShapeDtypeStruct((n_chunks, HEADS, CHUNK, CHUNK), F32),
        ] + _gather_out_shapes(gather),
        scratch_shapes=[pltpu.VMEM((HEADS, HEAD_DIM, HEAD_DIM), F32)] + (_gather_sems(n) if n else []),
        compiler_params=_params(("arbitrary",)),
    )(qkvbg, *gather)
    return o, states, tinv, _set_own_slots(stacks, gather)


def gdn_chunk_bwd(qkvbg, states, tinv, d_o, scatter=()):
    _, lp, width = qkvbg.shape
    n_chunks = lp // CHUNK
    last = n_chunks - 1
    n = len(scatter)

    def body(x_ref, s_ref, t_ref, do_ref, *refs):
        leaving_refs, dx_ref, refs = refs[:n], refs[n], refs[n + 1:]
        landing_refs, dstate, sems = refs[:n], refs[n], refs[n + 1:]
        copies = _scatter_copies(leaving_refs, landing_refs, *sems) if n else None

        @pl.when(pl.program_id(0) == 0)
        def _():
            dstate[...] = jnp.zeros_like(dstate)
            if n:
                _scatter_start(copies)

        heads = range(HEADS)
        qs, ks, vs, gbs, g64s, bbs = _gdn_head_values(x_ref)
        ss = [s_ref[0, h] for h in heads]
        ts = [t_ref[0, h] for h in heads]
        d_out = ([do_ref[:, _head_slices(h)[0]] for h in heads], [dstate[h] for h in heads])
        _, vjp_apply = jax.vjp(_gdn_apply, qs, ks, vs, gbs, g64s, bbs, ss, ts)
        dq, dk, dv, dgb, dg64, dbb, ds, dt = vjp_apply(d_out)
        tts = [t.T for t in ts]
        dm = [_dot(tts[h], dt[h]) for h in heads]
        dm = [-_dot(dm[h], tts[h]) for h in heads]
        _, vjp_m = jax.vjp(_gdn_m, ks, g64s, bbs)
        dk2, dg64m, dbb2 = vjp_m(dm)
        for h in heads:
            sl, sl64 = _head_slices(h)
            dx_ref[0, :, sl] = dq[h]
            dx_ref[1, :, sl] = dk[h] + dk2[h]
            dx_ref[2, :, sl] = dv[h]
            dx_ref[3, :, sl] = dbb[h] + dbb2[h]
            dx_ref[4, :, sl] = dgb[h]
            dx_ref[4, :, sl64] += dg64[h] + dg64m[h]
            dstate[h] = ds[h]

        if n:
            @pl.when(pl.program_id(0) == n_chunks - 1)
            def _():
                _scatter_finish(copies)

    dqkvbg, *landed = pl.pallas_call(
        body,
        name="gdn_chunk_bwd",
        grid=(n_chunks,),
        in_specs=[
            pl.BlockSpec((5, CHUNK, width), lambda c: (0, last - c, 0)),
            pl.BlockSpec((1, HEADS, HEAD_DIM, HEAD_DIM), lambda c: (last - c, 0, 0, 0)),
            pl.BlockSpec((1, HEADS, CHUNK, CHUNK), lambda c: (last - c, 0, 0, 0)),
            pl.BlockSpec((CHUNK, width), lambda c: (last - c, 0)),
        ] + [ANY] * n,
        out_specs=[pl.BlockSpec((5, CHUNK, width), lambda c: (0, last - c, 0))] + [ANY] * n,
        out_shape=[jax.ShapeDtypeStruct(qkvbg.shape, F32)] + [jax.ShapeDtypeStruct(b.shape, b.dtype) for b in scatter],
        scratch_shapes=[pltpu.VMEM((HEADS, HEAD_DIM, HEAD_DIM), F32)] + (_scatter_sems(n) if n else []),
        compiler_params=_params(("arbitrary",)),
    )(qkvbg, states, tinv, d_o, *scatter)
    return dqkvbg, _keep_own_slots(landed, scatter)


def mm_nn(a, b, *, tm, name):
    ks, m, tk = a.shape
    _, ns, _, tn = b.shape

    def body(a_ref, b_ref, o_ref):
        p = _dot(a_ref[...].astype(BF16), b_ref[...])

        @pl.when(pl.program_id(2) == 0)
        def _():
            o_ref[...] = p

        @pl.when(pl.program_id(2) > 0)
        def _():
            o_ref[...] += p

    return pl.pallas_call(
        body,
        name=name,
        grid=(ns, m // tm, ks),
        in_specs=[
            pl.BlockSpec((None, tm, tk), lambda n, i, k: (k, i, 0)),
            pl.BlockSpec((None, None, tk, tn), lambda n, i, k: (k, n, 0, 0)),
        ],
        out_specs=pl.BlockSpec((None, tm, tn), lambda n, i, k: (n, i, 0)),
        out_shape=jax.ShapeDtypeStruct((ns, m, tn), F32),
        compiler_params=_params(("arbitrary", "arbitrary", "arbitrary")),
    )(a, b)


def mm_nt(dy, w, *, tm, name, res=None, res_scale=1.0):
    ns, m, tn = dy.shape
    ks, _, tk, _ = w.shape

    def body(*refs):
        if res is None:
            dy_ref, w_ref, o_ref = refs
        else:
            dy_ref, w_ref, r_ref, o_ref = refs
        p = _dot_nt(dy_ref[...].astype(BF16), w_ref[...])

        @pl.when(pl.program_id(2) == 0)
        def _():
            o_ref[...] = p if res is None else p + res_scale * r_ref[...]

        @pl.when(pl.program_id(2) > 0)
        def _():
            o_ref[...] += p

    in_specs = [
        pl.BlockSpec((None, tm, tn), lambda k, i, n: (n, i, 0)),
        pl.BlockSpec((None, None, tk, tn), lambda k, i, n: (k, n, 0, 0)),
    ]
    args = [dy, w]
    if res is not None:
        in_specs.append(pl.BlockSpec((None, tm, tk), lambda k, i, n: (k, i, 0)))
        args.append(res)
    return pl.pallas_call(
        body,
        name=name,
        grid=(ks, m // tm, ns),
        in_specs=in_specs,
        out_specs=pl.BlockSpec((None, tm, tk), lambda k, i, n: (k, i, 0)),
        out_shape=jax.ShapeDtypeStruct((ks, m, tk), F32),
        compiler_params=_params(("arbitrary", "arbitrary", "arbitrary")),
    )(*args)


def mm_tn(x, dy, *, tm, name):
    ks, m, tk = x.shape
    ns, _, tn = dy.shape

    def body(x_ref, dy_ref, o_ref):
        p = _dot_tn(x_ref[...].astype(BF16), dy_ref[...].astype(BF16))

        @pl.when(pl.program_id(2) == 0)
        def _():
            o_ref[...] = p

        @pl.when(pl.program_id(2) > 0)
        def _():
            o_ref[...] += p

    return pl.pallas_call(
        body,
        name=name,
        grid=(ks, ns, m // tm),
        in_specs=[
            pl.BlockSpec((None, tm, tk), lambda k, n, i: (k, i, 0)),
            pl.BlockSpec((None, tm, tn), lambda k, n, i: (n, i, 0)),
        ],
        out_specs=pl.BlockSpec((None, None, tk, tn), lambda k, n, i: (k, n, 0, 0)),
        out_shape=jax.ShapeDtypeStruct((ks, ns, tk, tn), F32),
        compiler_params=_params(("arbitrary", "arbitrary", "arbitrary")),
    )(x, dy)


def _row_partial(x):
    rows, c = x.shape
    return jnp.sum(x.reshape(rows // 8, 8, c), axis=0)


def ln_fwd(h_prev, mix, g, b, *, tm, name):
    _, lp, d = h_prev.shape

    def body(h_ref, m_ref, g_ref, b_ref, r_ref, o_ref):
        r = ALPHA * h_ref[...] + m_ref[...]
        mu = jnp.mean(r, axis=-1, keepdims=True)
        xc = r - mu
        var = jnp.mean(xc * xc, axis=-1, keepdims=True)
        r_ref[...] = r
        o_ref[...] = xc * lax.rsqrt(var + LN_EPS) * g_ref[...] + b_ref[...]

    row = pl.BlockSpec((None, tm, d), lambda i: (0, i, 0))
    vec = pl.BlockSpec((1, d), lambda i: (0, 0))
    return pl.pallas_call(
        body,
        name=name,
        grid=(lp // tm,),
        in_specs=[row, row, vec, vec],
        out_specs=[row, row],
        out_shape=[jax.ShapeDtypeStruct((1, lp, d), F32)] * 2,
        compiler_params=_params(("arbitrary",)),
    )(h_prev, mix, g, b)


def ln_bwd(r, dh, g, *, tm, name):
    _, lp, d = r.shape

    def body(r_ref, dh_ref, g_ref, dr_ref, dgb_ref):
        x = r_ref[...]
        dh_v = dh_ref[...]
        mu = jnp.mean(x, axis=-1, keepdims=True)
        xc = x - mu
        rstd = lax.rsqrt(jnp.mean(xc * xc, axis=-1, keepdims=True) + LN_EPS)
        xh = xc * rstd
        dxh = dh_v * g_ref[...]
        m1 = jnp.mean(dxh, axis=-1, keepdims=True)
        m2 = jnp.mean(dxh * xh, axis=-1, keepdims=True)
        dr_ref[...] = rstd * (dxh - m1 - xh * m2)

        @pl.when(pl.program_id(0) == 0)
        def _():
            dgb_ref[...] = jnp.zeros_like(dgb_ref)

        dgb_ref[0] += _row_partial(dh_v * xh)
        dgb_ref[1] += _row_partial(dh_v)

    row = pl.BlockSpec((None, tm, d), lambda i: (0, i, 0))
    return pl.pallas_call(
        body,
        name=name,
        grid=(lp // tm,),
        in_specs=[row, row, pl.BlockSpec((1, d), lambda i: (0, 0))],
        out_specs=[row, pl.BlockSpec((2, 8, d), lambda i: (0, 0, 0))],
        out_shape=[jax.ShapeDtypeStruct((1, lp, d), F32), jax.ShapeDtypeStruct((2, 8, d), F32)],
        compiler_params=_params(("arbitrary",)),
    )(r, dh, g)


def loss_grad(h, target, *, first, count, tm):
    _, lp, d = h.shape

    def body(h_ref, t_ref, dh_ref, l_ref):
        row = pl.program_id(0) * tm + lax.broadcasted_iota(jnp.int32, (tm, d), 0)
        valid = (row >= first) & (row < first + count)
        err = jnp.where(valid, h_ref[...] - t_ref[...], 0.0)
        dh_ref[...] = err * (1.0 / d)

        @pl.when(pl.program_id(0) == 0)
        def _():
            l_ref[...] = jnp.zeros_like(l_ref)

        l_ref[...] += _row_partial(err * err) * (0.5 / d)

    return pl.pallas_call(
        body,
        name="loss_grad",
        grid=(lp // tm,),
        in_specs=[pl.BlockSpec((None, tm, d), lambda i: (0, i, 0)), pl.BlockSpec((tm, d), lambda i: (i, 0))],
        out_specs=[pl.BlockSpec((None, tm, d), lambda i: (0, i, 0)), pl.BlockSpec((8, d), lambda i: (0, 0))],
        out_shape=[jax.ShapeDtypeStruct((1, lp, d), F32), jax.ShapeDtypeStruct((8, d), F32)],
        compiler_params=_params(("arbitrary",)),
    )(h, target)


def _halo_index(tile, tm):
    return jnp.maximum(tile * (tm // HALO) - 1, 0)


def _conv_fwd(xs_ref, w, taps, tm):
    acc = w(0) * xs_ref[pl.ds(HALO - taps + 1, tm), :]
    for j in range(1, taps):
        acc += w(j) * xs_ref[pl.ds(HALO - taps + 1 + j, tm), :]
    return acc


def _conv_bwd_x(dcs_ref, w, taps, tm):
    acc = w(0) * dcs_ref[pl.ds(taps - 1, tm), :]
    for j in range(1, taps):
        acc += w(j) * dcs_ref[pl.ds(taps - 1 - j, tm), :]
    return acc


SUB = 8
LANES = 128
STRIP_UNROLL = 3


def _shift_down(cur, prev, s):
    if s == 0:
        return cur
    row = lax.broadcasted_iota(jnp.int32, cur.shape, 0)
    return jnp.where(row < s, pltpu.roll(prev, s, axis=0), pltpu.roll(cur, s, axis=0))


def _shift_up(cur, nxt, s):
    if s == 0:
        return cur
    row = lax.broadcasted_iota(jnp.int32, cur.shape, 0)
    return jnp.where(row < SUB - s, pltpu.roll(cur, SUB - s, axis=0), pltpu.roll(nxt, SUB - s, axis=0))


def _silu_parts(c):
    sg = _sigmoid(c)
    return c * sg, sg * (1.0 + c * (1.0 - sg))


def _head_sum(x):
    rows, c = x.shape
    parts = []
    for h in range(c // HEAD_DIM):
        s = jnp.sum(x[:, h * HEAD_DIM:(h + 1) * HEAD_DIM], axis=-1, keepdims=True)
        parts.append(jnp.broadcast_to(s, (rows, HEAD_DIM)))
    return parts[0] if len(parts) == 1 else jnp.concatenate(parts, axis=-1)


def _log1p(y):
    u = 1.0 + y
    d = u - 1.0
    return jnp.where(d == 0.0, y, jnp.log(u) * (y / jnp.where(d == 0.0, 1.0, d)))


def _softplus(x):
    return jnp.maximum(x, 0.0) + _log1p(jnp.exp(-jnp.abs(x)))


def gdn_pre_fwd(p5, conv_w, alog_b, dtb_b, *, tm, cb):
    _, lp, width = p5.shape
    taps = conv_w.shape[1]

    def body(x_ref, halo_ref, w_ref, al_ref, dt_ref, o_ref, xs):
        i = pl.program_id(1)
        for s in range(3):
            xs[s, 0:HALO, :] = jnp.where(i > 0, halo_ref[s], 0.0)
            xs[s, HALO:, :] = x_ref[s]
            c = _conv_fwd(xs.at[s], lambda j, s=s: w_ref[s, j:j + 1, :], taps, tm)
            y, _ = _silu_parts(c)
            if s < 2:
                y = y * lax.rsqrt(_head_sum(y * y) + L2_EPS)
                if s == 0:
                    y = y * Q_SCALE
            o_ref[s] = y
        o_ref[3] = _sigmoid(x_ref[3])
        o_ref[4] = -jnp.exp(al_ref[...]) * _softplus(x_ref[4] + dt_ref[...])

    return pl.pallas_call(
        body,
        name="gdn_pre_fwd",
        grid=(width // cb, lp // tm),
        in_specs=[
            pl.BlockSpec((5, tm, cb), lambda j, i: (0, i, j)),
            pl.BlockSpec((3, HALO, cb), lambda j, i: (0, _halo_index(i, tm), j)),
            pl.BlockSpec((3, taps, cb), lambda j, i: (0, 0, j)),
            pl.BlockSpec((1, cb), lambda j, i: (0, j)),
            pl.BlockSpec((1, cb), lambda j, i: (0, j)),
        ],
        out_specs=pl.BlockSpec((5, tm, cb), lambda j, i: (0, i, j)),
        out_shape=jax.ShapeDtypeStruct((5, lp, width), F32),
        scratch_shapes=[pltpu.VMEM((3, tm + HALO, cb), F32)],
        compiler_params=_params(("arbitrary", "arbitrary")),
    )(p5, p5, conv_w, alog_b, dtb_b)


def gdn_pre_bwd(p5, dqkvbg, conv_w, alog_b, dtb_b, *, tm, cb):
    _, lp, width = p5.shape
    taps = conv_w.shape[1]
    last = lp // tm - 1

    def body(x_ref, halo_ref, d_ref, w_ref, al_ref, dt_ref, dx_ref, dw_ref, dsc_ref, xs, dcs, carry):
        step = pl.program_id(1)
        tile = last - step

        @pl.when(step == 0)
        def _():
            carry[...] = jnp.zeros_like(carry)
            dw_ref[...] = jnp.zeros_like(dw_ref)
            dsc_ref[...] = jnp.zeros_like(dsc_ref)

        for s in range(3):
            w = lambda j, s=s: w_ref[s, j:j + 1, :]
            xs[s, 0:HALO, :] = jnp.where(tile > 0, halo_ref[s], 0.0)
            xs[s, HALO:, :] = x_ref[s]
            c = _conv_fwd(xs.at[s], w, taps, tm)
            y, dsilu = _silu_parts(c)
            dy = d_ref[s]
            if s < 2:
                rn = lax.rsqrt(_head_sum(y * y) + L2_EPS)
                yn = y * rn
                if s == 0:
                    dy = dy * Q_SCALE
                dy = rn * (dy - yn * _head_sum(dy * yn))
            dc = dy * dsilu
            dcs[s, 0:tm, :] = dc
            dcs[s, tm:, :] = carry[s]
            dx_ref[s] = _conv_bwd_x(dcs.at[s], w, taps, tm)
            carry[s] = dc[0:HALO, :]
            for j in range(taps):
                dw_ref[s, j] += _row_partial(dc * xs[s, pl.ds(HALO - taps + 1 + j, tm), :])
        beta = _sigmoid(x_ref[3])
        dx_ref[3] = d_ref[3] * beta * (1.0 - beta)
        z = x_ref[4] + dt_ref[...]
        dg = d_ref[4] * -jnp.exp(al_ref[...])
        da = dg * _sigmoid(z)
        dx_ref[4] = da
        dsc_ref[0] += _row_partial(dg * _softplus(z))
        dsc_ref[1] += _row_partial(da)

    tile_spec = pl.BlockSpec((5, tm, cb), lambda j, i: (0, last - i, j))
    return pl.pallas_call(
        body,
        name="gdn_pre_bwd",
        grid=(width // cb, lp // tm),
        in_specs=[
            tile_spec,
            pl.BlockSpec((3, HALO, cb), lambda j, i: (0, _halo_index(last - i, tm), j)),
            tile_spec,
            pl.BlockSpec((3, taps, cb), lambda j, i: (0, 0, j)),
            pl.BlockSpec((1, cb), lambda j, i: (0, j)),
            pl.BlockSpec((1, cb), lambda j, i: (0, j)),
        ],
        out_specs=[
            tile_spec,
            pl.BlockSpec((3, taps, SUB, cb), lambda j, i: (0, 0, 0, j)),
            pl.BlockSpec((2, SUB, cb), lambda j, i: (0, 0, j)),
        ],
        out_shape=[
            jax.ShapeDtypeStruct((5, lp, width), F32),
            jax.ShapeDtypeStruct((3, taps, SUB, width), F32),
            jax.ShapeDtypeStruct((2, SUB, width), F32),
        ],
        scratch_shapes=[
            pltpu.VMEM((3, tm + HALO, cb), F32),
            pltpu.VMEM((3, tm + HALO, cb), F32),
            pltpu.VMEM((3, HALO, cb), F32),
        ],
        compiler_params=_params(("arbitrary", "arbitrary")),
    )(p5, p5, dqkvbg, conv_w, alog_b, dtb_b)


def gdn_post_fwd(o, z, nw_b, *, tm):
    _, lp, width = o.shape

    def body(o_ref, z_ref, nw_ref, y_ref):
        ov = o_ref[...]
        rn = lax.rsqrt(_head_sum(ov * ov) * (1.0 / HEAD_DIM) + RMS_EPS)
        gate, _ = _silu_parts(z_ref[...])
        y_ref[...] = ov * rn * nw_ref[...] * gate

    row = pl.BlockSpec((None, tm, width), lambda i: (0, i, 0))
    return pl.pallas_call(
        body,
        name="gdn_post_fwd",
        grid=(lp // tm,),
        in_specs=[row, row, pl.BlockSpec((1, width), lambda i: (0, 0))],
        out_specs=row,
        out_shape=jax.ShapeDtypeStruct((1, lp, width), F32),
        compiler_params=_params(("arbitrary",)),
    )(o, z, nw_b)


def gdn_post_bwd(o, z, dy, nw_b, *, tm):
    _, lp, width = o.shape

    def body(o_ref, z_ref, dy_ref, nw_ref, do_ref, dz_ref, dnw_ref):
        ov = o_ref[...]
        rn = lax.rsqrt(_head_sum(ov * ov) * (1.0 / HEAD_DIM) + RMS_EPS)
        yn = ov * rn
        gate, dgate = _silu_parts(z_ref[...])
        d_on = dy_ref[...] * gate
        dz_ref[...] = dy_ref[...] * yn * nw_ref[...] * dgate
        a = d_on * nw_ref[...]
        do_ref[...] = rn * (a - yn * (_head_sum(a * yn) * (1.0 / HEAD_DIM)))

        @pl.when(pl.program_id(0) == 0)
        def _():
            dnw_ref[...] = jnp.zeros_like(dnw_ref)

        dnw_ref[...] += _row_partial(d_on * yn)

    row = pl.BlockSpec((None, tm, width), lambda i: (0, i, 0))
    return pl.pallas_call(
        body,
        name="gdn_post_bwd",
        grid=(lp // tm,),
        in_specs=[row, row, row, pl.BlockSpec((1, width), lambda i: (0, 0))],
        out_specs=[row, row, pl.BlockSpec((8, width), lambda i: (0, 0))],
        out_shape=[jax.ShapeDtypeStruct((1, lp, width), F32)] * 2 + [jax.ShapeDtypeStruct((8, width), F32)],
        compiler_params=_params(("arbitrary",)),
    )(o, z, dy, nw_b)


def head_lane_sum(x):
    s_n, rows, width = x.shape

    def body(x_ref, o_ref):
        lane = lax.broadcasted_iota(jnp.int32, (rows, HEAD_DIM), 1)
        acc = jnp.zeros((rows, HEAD_DIM), F32)
        for h in range(width // HEAD_DIM):
            s = jnp.sum(x_ref[:, h * HEAD_DIM:(h + 1) * HEAD_DIM], axis=-1, keepdims=True)
            acc = jnp.where(lane == h, s, acc)
        o_ref[...] = acc

    return pl.pallas_call(
        body,
        name="head_lane_sum",
        grid=(s_n,),
        in_specs=[pl.BlockSpec((None, rows, width), lambda s: (s, 0, 0))],
        out_specs=pl.BlockSpec((None, rows, HEAD_DIM), lambda s: (s, 0, 0)),
        out_shape=jax.ShapeDtypeStruct((s_n, rows, HEAD_DIM), F32),
        compiler_params=_params(("arbitrary",)),
    )(x)


def ffn_act_fwd(up, conv_w, *, tm, name):
    _, lp, c_w = up.shape
    taps = conv_w.shape[1]

    def body(u_ref, halo_ref, g_ref, w_ref, o_ref):
        first_tile = pl.program_id(1) == 0

        def strip(r0, prev_of):
            rows = pl.ds(r0, SUB)
            for c0 in range(0, c_w, LANES):
                cs = slice(c0, c0 + LANES)
                cur = u_ref[rows, cs]
                prev = prev_of(cs)
                conv = w_ref[taps - 1:taps, cs] * cur
                for j in range(taps - 1):
                    conv += w_ref[j:j + 1, cs] * _shift_down(cur, prev, taps - 1 - j)
                y, _ = _silu_parts(conv)
                o_ref[rows, cs] = y * g_ref[rows, cs]

        strip(0, lambda cs: jnp.where(first_tile, 0.0, halo_ref[:, cs]))

        def loop_body(s, carry):
            r0 = pl.multiple_of(s * SUB, SUB)
            strip(r0, lambda cs: u_ref[pl.ds(pl.multiple_of(r0 - SUB, SUB), SUB), cs])
            return carry

        lax.fori_loop(1, tm // SUB, loop_body, 0, unroll=STRIP_UNROLL)

    return pl.pallas_call(
        body,
        name=name,
        grid=(2, lp // tm),
        in_specs=[
            pl.BlockSpec((None, tm, c_w), lambda s, i: (s, i, 0)),
            pl.BlockSpec((None, HALO, c_w), lambda s, i: (s, _halo_index(i, tm), 0)),
            pl.BlockSpec((None, tm, c_w), lambda s, i: (2 + s, i, 0)),
            pl.BlockSpec((None, taps, c_w), lambda s, i: (s, 0, 0)),
        ],
        out_specs=pl.BlockSpec((None, tm, c_w), lambda s, i: (s, i, 0)),
        out_shape=jax.ShapeDtypeStruct((2, lp, c_w), F32),
        compiler_params=_params(("arbitrary", "arbitrary")),
    )(up, up, up, conv_w)


def ffn_act_bwd(up, dact, conv_w, *, tm, name):
    _, lp, c_w = up.shape
    taps = conv_w.shape[1]
    last = lp // tm - 1
    n_strips = tm // SUB

    def body(u_ref, halo_ref, g_ref, d_ref, w_ref, dup_ref, dw_ref, below):
        step = pl.program_id(1)
        first_tile = step == last

        @pl.when(step == 0)
        def _():
            below[...] = jnp.zeros_like(below)
            dw_ref[...] = jnp.zeros_like(dw_ref)

        def strip(r0, prev_of):
            rows = pl.ds(r0, SUB)
            for c0 in range(0, c_w, LANES):
                cs = slice(c0, c0 + LANES)
                cur = u_ref[rows, cs]
                prev = prev_of(cs)
                shifted = [_shift_down(cur, prev, taps - 1 - j) for j in range(taps)]
                conv = w_ref[0:1, cs] * shifted[0]
                for j in range(1, taps):
                    conv += w_ref[j:j + 1, cs] * shifted[j]
                y, dsilu = _silu_parts(conv)
                d = d_ref[rows, cs]
                dup_ref[1, rows, cs] = d * y
                dc = d * g_ref[rows, cs] * dsilu
                nxt = below[:, cs]
                dx = w_ref[taps - 1:taps, cs] * dc
                for j in range(taps - 1):
                    dx += w_ref[j:j + 1, cs] * _shift_up(dc, nxt, taps - 1 - j)
                dup_ref[0, rows, cs] = dx
                below[:, cs] = dc
                for j in range(taps):
                    dw_ref[j, :, cs] += dc * shifted[j]

        def loop_body(it, carry):
            r0 = pl.multiple_of((n_strips - 1 - it) * SUB, SUB)
            strip(r0, lambda cs: u_ref[pl.ds(pl.multiple_of(r0 - SUB, SUB), SUB), cs])
            return carry

        lax.fori_loop(0, n_strips - 1, loop_body, 0, unroll=STRIP_UNROLL)
        strip(0, lambda cs: jnp.where(first_tile, 0.0, halo_ref[:, cs]))

    return pl.pallas_call(
        body,
        name=name,
        grid=(2, lp // tm),
        in_specs=[
            pl.BlockSpec((None, tm, c_w), lambda s, i: (s, last - i, 0)),
            pl.BlockSpec((None, HALO, c_w), lambda s, i: (s, _halo_index(last - i, tm), 0)),
            pl.BlockSpec((None, tm, c_w), lambda s, i: (2 + s, last - i, 0)),
            pl.BlockSpec((None, tm, c_w), lambda s, i: (s, last - i, 0)),
            pl.BlockSpec((None, taps, c_w), lambda s, i: (s, 0, 0)),
        ],
        out_specs=[
            pl.BlockSpec((2, None, tm, c_w), lambda s, i: (0, s, last - i, 0)),
            pl.BlockSpec((None, taps, SUB, c_w), lambda s, i: (s, 0, 0, 0)),
        ],
        out_shape=[jax.ShapeDtypeStruct((2, 2, lp, c_w), F32), jax.ShapeDtypeStruct((2, taps, SUB, c_w), F32)],
        scratch_shapes=[pltpu.VMEM((SUB, c_w), F32)],
        compiler_params=_params(("arbitrary", "arbitrary")),
    )(up, up, up, dact, conv_w)


def sc_fwd(pb, conv_w, *, tm, cb):
    _, lp, width = pb.shape
    taps = conv_w.shape[0]

    def body(x_ref, halo_ref, w_ref, o_ref):
        first_tile = pl.program_id(1) == 0

        def strip(r0, prev_of):
            rows = pl.ds(r0, SUB)
            for c0 in range(0, cb, LANES):
                cs = slice(c0, c0 + LANES)
                cur = x_ref[1, rows, cs] * x_ref[2, rows, cs]
                prev = prev_of(cs)
                conv = w_ref[taps - 1:taps, cs] * cur
                for j in range(taps - 1):
                    conv += w_ref[j:j + 1, cs] * _shift_down(cur, prev, taps - 1 - j)
                o_ref[rows, cs] = x_ref[0, rows, cs] * conv

        strip(0, lambda cs: jnp.where(first_tile, 0.0, halo_ref[1, :, cs] * halo_ref[2, :, cs]))

        def loop_body(k, carry):
            r0 = pl.multiple_of(k * SUB, SUB)
            before = pl.ds(pl.multiple_of(r0 - SUB, SUB), SUB)
            strip(r0, lambda cs: x_ref[1, before, cs] * x_ref[2, before, cs])
            return carry

        lax.fori_loop(1, tm // SUB, loop_body, 0, unroll=STRIP_UNROLL)

    return pl.pallas_call(
        body,
        name="sc_fwd",
        grid=(width // cb, lp // tm),
        in_specs=[
            pl.BlockSpec((3, tm, cb), lambda j, i: (0, i, j)),
            pl.BlockSpec((3, HALO, cb), lambda j, i: (0, _halo_index(i, tm), j)),
            pl.BlockSpec((taps, cb), lambda j, i: (0, j)),
        ],
        out_specs=pl.BlockSpec((None, tm, cb), lambda j, i: (0, i, j)),
        out_shape=jax.ShapeDtypeStruct((1, lp, width), F32),
        compiler_params=_params(("arbitrary", "arbitrary")),
    )(pb, pb, conv_w)


def sc_bwd(pb, ds, conv_w, *, tm, cb):
    _, lp, width = pb.shape
    taps = conv_w.shape[0]
    last = lp // tm - 1
    n_strips = tm // SUB

    def body(x_ref, halo_ref, d_ref, w_ref, dx_ref, dw_ref, below):
        step = pl.program_id(1)
        first_tile = step == last

        @pl.when(step == 0)
        def _():
            below[...] = jnp.zeros_like(below)
            dw_ref[...] = jnp.zeros_like(dw_ref)

        def strip(r0, prev_of):
            rows = pl.ds(r0, SUB)
            for c0 in range(0, cb, LANES):
                cs = slice(c0, c0 + LANES)
                gate, left, right = x_ref[0, rows, cs], x_ref[1, rows, cs], x_ref[2, rows, cs]
                cur = left * right
                prev = prev_of(cs)
                shifted = [_shift_down(cur, prev, taps - 1 - j) for j in range(taps)]
                conv = w_ref[0:1, cs] * shifted[0]
                for j in range(1, taps):
                    conv += w_ref[j:j + 1, cs] * shifted[j]
                d = d_ref[rows, cs]
                dx_ref[0, rows, cs] = d * conv
                dc = d * gate
                nxt = below[:, cs]
                dp = w_ref[taps - 1:taps, cs] * dc
                for j in range(taps - 1):
                    dp += w_ref[j:j + 1, cs] * _shift_up(dc, nxt, taps - 1 - j)
                dx_ref[1, rows, cs] = dp * right
                dx_ref[2, rows, cs] = dp * left
                below[:, cs] = dc
                for j in range(taps):
                    dw_ref[j, :, cs] += dc * shifted[j]

        def loop_body(it, carry):
            r0 = pl.multiple_of((n_strips - 1 - it) * SUB, SUB)
            before = pl.ds(pl.multiple_of(r0 - SUB, SUB), SUB)
            strip(r0, lambda cs: x_ref[1, before, cs] * x_ref[2, before, cs])
            return carry

        lax.fori_loop(0, n_strips - 1, loop_body, 0, unroll=STRIP_UNROLL)
        strip(0, lambda cs: jnp.where(first_tile, 0.0, halo_ref[1, :, cs] * halo_ref[2, :, cs]))

    tile_spec = pl.BlockSpec((3, tm, cb), lambda j, i: (0, last - i, j))
    return pl.pallas_call(
        body,
        name="sc_bwd",
        grid=(width // cb, lp // tm),
        in_specs=[
            tile_spec,
            pl.BlockSpec((3, HALO, cb), lambda j, i: (0, _halo_index(last - i, tm), j)),
            pl.BlockSpec((None, tm, cb), lambda j, i: (0, last - i, j)),
            pl.BlockSpec((taps, cb), lambda j, i: (0, j)),
        ],
        out_specs=[tile_spec, pl.BlockSpec((taps, SUB, cb), lambda j, i: (0, 0, j))],
        out_shape=[jax.ShapeDtypeStruct((3, lp, width), F32), jax.ShapeDtypeStruct((taps, SUB, width), F32)],
        scratch_shapes=[pltpu.VMEM((SUB, cb), F32)],
        compiler_params=_params(("arbitrary", "arbitrary")),
    )(pb, pb, ds, conv_w)


TILE_BYTES = 1536 * 1024


def _rows_tile(rows, cols, multiple=8):
    if rows * cols * 4 <= TILE_BYTES or rows % multiple:
        return rows
    best = multiple
    for t in range(multiple, rows + 1, multiple):
        if rows % t == 0 and t * cols * 4 <= TILE_BYTES:
            best = t
    return best


def pair_sum(g, landed, core, out_dtype, name):
    _, rows, cols = g.shape
    half = rows // 2
    tr = _rows_tile(half, cols, 16)
    nb = half // tr

    def body(c_ref, g_ref, l_ref, o_ref):
        o_ref[...] = (g_ref[...] + l_ref[...]).astype(out_dtype)

    return pl.pallas_call(
        body,
        name=name,
        grid_spec=pltpu.PrefetchScalarGridSpec(
            num_scalar_prefetch=1,
            grid=(4, nb),
            in_specs=[
                pl.BlockSpec((None, tr, cols), lambda s, i, c: (s, c[0] * nb + i, 0)),
                pl.BlockSpec((None, tr, cols), lambda s, i, c: (s, i, 0)),
            ],
            out_specs=pl.BlockSpec((None, tr, cols), lambda s, i, c: (s, i, 0)),
        ),
        out_shape=jax.ShapeDtypeStruct((4, half, cols), out_dtype),
        compiler_params=_params(("arbitrary", "arbitrary")),
    )(core, g, landed)


def chip_sum(x, name):
    _, rows, cols = x.shape
    tr = _rows_tile(rows, cols, 16)

    def body(x0, x1, x2, x3, o_ref):
        acc = x0[...].astype(F32) + x1[...].astype(F32)
        o_ref[...] = (acc + x2[...].astype(F32)) + x3[...].astype(F32)

    return pl.pallas_call(
        body,
        name=name,
        grid=(rows // tr,),
        in_specs=[pl.BlockSpec((None, tr, cols), lambda i, k=k: (k, i, 0)) for k in range(4)],
        out_specs=pl.BlockSpec((tr, cols), lambda i: (i, 0)),
        out_shape=jax.ShapeDtypeStruct((rows, cols), F32),
        compiler_params=_params(("arbitrary",)),
    )(x, x, x, x)


def adamw(w, g, m, v, name):
    shape = w.shape
    cols = shape[-1]
    rows = w.size // cols
    tr = _rows_tile(rows, cols)

    def body(w_ref, g_ref, m_ref, v_ref, d_ref, m2_ref, v2_ref):
        gv = g_ref[...]
        m2 = ADAM_B1 * m_ref[...] + (1.0 - ADAM_B1) * gv
        v2 = ADAM_B2 * v_ref[...] + (1.0 - ADAM_B2) * (gv * gv)
        m_hat = m2 / (1.0 - ADAM_B1 ** ADAM_STEP)
        v_hat = v2 / (1.0 - ADAM_B2 ** ADAM_STEP)
        d_ref[...] = -ADAM_LR * (m_hat / (jnp.sqrt(v_hat) + ADAM_EPS) + ADAM_WD * w_ref[...])
        m2_ref[...] = m2
        v2_ref[...] = v2

    spec = pl.BlockSpec((tr, cols), lambda i: (i, 0))
    outs = pl.pallas_call(
        body,
        name=name,
        grid=(rows // tr,),
        in_specs=[spec] * 4,
        out_specs=[spec] * 3,
        out_shape=[jax.ShapeDtypeStruct((rows, cols), F32)] * 3,
        compiler_params=_params(("arbitrary",)),
    )(*[t.reshape(rows, cols) for t in (w, g, m, v)])
    return tuple(o.reshape(shape) for o in outs)


MESH_ID = pl.DeviceIdType.MESH
ANY = pl.BlockSpec(memory_space=pl.ANY)


def _place():
    x, y, c = lax.axis_index("x"), lax.axis_index("y"), lax.axis_index("c")
    other_chips = [(1 - x, y), (x, 1 - y), (1 - x, 1 - y)]
    return x, y, c, other_chips


def all_gather_shards(bufs, name):
    n = len(bufs)

    def body(*refs):
        x_refs, o_refs = refs[:n], refs[n:2 * n]
        copies = _gather_copies(x_refs, o_refs, *refs[2 * n:])
        _gather_start(copies)
        _gather_finish(copies)

    outs = pl.pallas_call(
        body,
        name=name,
        in_specs=[ANY] * n,
        out_specs=[ANY] * n,
        out_shape=_gather_out_shapes(bufs),
        scratch_shapes=_gather_sems(n),
    )(*bufs)
    return _set_own_slots(outs, bufs)


def _gather_out_shapes(bufs):
    return [jax.ShapeDtypeStruct((4,) + b.shape, b.dtype) for b in bufs]


def _gather_sems(n):
    return [pltpu.SemaphoreType.DMA((6 * n,)), pltpu.SemaphoreType.DMA((6 * n,))]


def _set_own_slots(outs, bufs):
    if not outs:
        return []
    me = 2 * lax.axis_index("x") + lax.axis_index("y")
    return [lax.dynamic_update_index_in_dim(o, b, me, 0) for o, b in zip(outs, bufs)]


def _gather_copies(x_refs, o_refs, send_sems, recv_sems):
    x, y, c, chips = _place()
    me = 2 * x + y
    sibling = (x, y, 1 - c)

    def part(a, slot, hf):
        half = x_refs[a].shape[0] // 2
        return o_refs[a].at[slot, pl.ds(hf * half, half), :]

    def mine(a):
        half = x_refs[a].shape[0] // 2
        return x_refs[a].at[pl.ds(c * half, half), :]

    def copy(k, src, dst, to):
        return pltpu.make_async_remote_copy(src_ref=src, dst_ref=dst, send_sem=send_sems.at[k],
                                            recv_sem=recv_sems.at[k], device_id=to, device_id_type=MESH_ID)

    sends, arrivals, passes, passed = [], [], [], []
    for a in range(len(x_refs)):
        for j, (px, py) in enumerate(chips):
            landed, theirs = part(a, 2 * px + py, c), part(a, 2 * px + py, 1 - c)
            sends.append(copy(6 * a + j, mine(a), part(a, me, c), (px, py, c)))
            arrivals.append(copy(6 * a + j, mine(a), landed, (px, py, c)))
            passes.append(copy(6 * a + 3 + j, landed, landed, sibling))
            passed.append(copy(6 * a + 3 + j, theirs, theirs, sibling))
    return sends, arrivals, passes, passed


def _gather_start(copies):
    for cp in copies[0]:
        cp.start()


def _gather_finish(copies):
    sends, arrivals, passes, passed = copies
    for arrival, cp in zip(arrivals, passes):
        arrival.wait_recv()
        cp.start()
    for cp in passed:
        cp.wait_recv()
    for cp in sends + passes:
        cp.wait_send()


def swap_halves(bufs, name):
    n = len(bufs)

    def body(*refs):
        x_refs, o_refs = refs[:n], refs[n:2 * n]
        send_sems, recv_sems = refs[2 * n:]
        x, y, c, _ = _place()
        copies = []
        for a in range(n):
            half = bufs[a].shape[1] // 2
            cp = pltpu.make_async_remote_copy(src_ref=x_refs[a].at[:, pl.ds((1 - c) * half, half), :], dst_ref=o_refs[a],
                                              send_sem=send_sems.at[a], recv_sem=recv_sems.at[a],
                                              device_id=(x, y, 1 - c), device_id_type=MESH_ID)
            cp.start()
            copies.append(cp)
        for cp in copies:
            cp.wait()

    return pl.pallas_call(
        body,
        name=name,
        in_specs=[ANY] * n,
        out_specs=[ANY] * n,
        out_shape=[jax.ShapeDtypeStruct((4, b.shape[1] // 2, b.shape[2]), b.dtype) for b in bufs],
        scratch_shapes=[pltpu.SemaphoreType.DMA((n,)), pltpu.SemaphoreType.DMA((n,))],
    )(*bufs)


def scatter_to_chips(bufs, name):
    n = len(bufs)

    def body(*refs):
        x_refs, o_refs = refs[:n], refs[n:2 * n]
        copies = _scatter_copies(x_refs, o_refs, *refs[2 * n:])
        _scatter_start(copies)
        _scatter_finish(copies)

    outs = pl.pallas_call(
        body,
        name=name,
        in_specs=[ANY] * n,
        out_specs=[ANY] * n,
        out_shape=[jax.ShapeDtypeStruct(b.shape, b.dtype) for b in bufs],
        scratch_shapes=_scatter_sems(n),
    )(*bufs)
    return _keep_own_slots(outs, bufs)


def _scatter_sems(n):
    return [pltpu.SemaphoreType.DMA((3 * n,)), pltpu.SemaphoreType.DMA((3 * n,))]


def _keep_own_slots(outs, bufs):
    if not outs:
        return []
    me = 2 * lax.axis_index("x") + lax.axis_index("y")
    return [lax.dynamic_update_index_in_dim(o, lax.dynamic_index_in_dim(b, me, 0, keepdims=False), me, 0)
            for o, b in zip(outs, bufs)]


def _scatter_copies(x_refs, o_refs, send_sems, recv_sems):
    x, y, c, chips = _place()
    me = 2 * x + y

    def copy(a, j, src_slot, dst_slot, px, py):
        return pltpu.make_async_remote_copy(src_ref=x_refs[a].at[src_slot], dst_ref=o_refs[a].at[dst_slot],
                                            send_sem=send_sems.at[3 * a + j], recv_sem=recv_sems.at[3 * a + j],
                                            device_id=(px, py, c), device_id_type=MESH_ID)

    sends = [copy(a, j, 2 * px + py, me, px, py) for a in range(len(x_refs)) for j, (px, py) in enumerate(chips)]
    arrivals = [copy(a, j, me, 2 * px + py, px, py) for a in range(len(x_refs)) for j, (px, py) in enumerate(chips)]
    return sends, arrivals


def _scatter_start(copies):
    for cp in copies[0]:
        cp.start()


def _scatter_finish(copies):
    for cp in copies[1]:
        cp.wait_recv()
    for cp in copies[0]:
        cp.wait_send()


def share_halves(groups, name):
    bufs = [b for grp in groups for b in grp]
    where = [(gi, li) for gi, grp in enumerate(groups) for li in range(len(grp))]
    n = len(bufs)

    def body(*refs):
        x_refs, o_refs = refs[:n], refs[n:n + len(groups)]
        send_sems, recv_sems = refs[n + len(groups):]
        x, y, c, _ = _place()
        sent, arrive = [], []
        for a, (gi, li) in enumerate(where):

            def copy(hf, a=a, gi=gi, li=li):
                return pltpu.make_async_remote_copy(src_ref=x_refs[a], dst_ref=o_refs[gi].at[li, hf],
                                                    send_sem=send_sems.at[a], recv_sem=recv_sems.at[a],
                                                    device_id=(x, y, 1 - c), device_id_type=MESH_ID)

            sent.append(copy(c))
            arrive.append(copy(1 - c))
        for cp in sent:
            cp.start()
        for cp in arrive:
            cp.wait_recv()
        for cp in sent:
            cp.wait_send()

    outs = pl.pallas_call(
        body,
        name=name,
        in_specs=[ANY] * n,
        out_specs=[ANY] * len(groups),
        out_shape=[jax.ShapeDtypeStruct((len(grp), 2) + grp[0].shape, grp[0].dtype) for grp in groups],
        scratch_shapes=[pltpu.SemaphoreType.DMA((n,)), pltpu.SemaphoreType.DMA((n,))],
    )(*bufs)
    c = lax.axis_index("c")
    full = [lax.dynamic_update_index_in_dim(o, jnp.stack(grp), c, 1) for o, grp in zip(outs, groups)]
    return [t.reshape(t.shape[0], 2 * t.shape[2], t.shape[3]) for t in full]


def pair_sums(bufs, dtypes, tag):
    core = lax.axis_index("c").astype(jnp.int32).reshape(1)
    landed = swap_halves(bufs, "rs_pair_" + tag)
    return [pair_sum(b, l, core, dt, "rs_pair_sum_%s%d" % (tag, i)) for i, (b, l, dt) in enumerate(zip(bufs, landed, dtypes))]


def _row_tiles(length):
    return (640, 320) if length > 2048 else (128, 64)


def _divisor_tile(rows, target):
    return max(t for t in range(8, min(rows, target) + 1, 8) if rows % t == 0)


def _local_step(x, target, wt, late_shards, layout_late, reduce_early):
    seq, d = x.shape
    length = N_META + seq
    tm, tm_ffn = _row_tiles(length)
    lp = -(-length // tm) * tm
    tail = jnp.zeros((lp - length, d), F32)
    h0 = jnp.concatenate([wt["meta"], x, tail], axis=0)[None]
    tgt = jnp.concatenate([jnp.zeros((N_META, d), F32), target, tail], axis=0)
    nn = functools.partial(mm_nn, tm=_divisor_tile(lp, 1664))
    nt = functools.partial(mm_nt, tm=_divisor_tile(lp, 1040))
    tn = functools.partial(mm_tn, tm=_divisor_tile(lp, 832))
    ln_g = [wt["ln_mix_g"][0:1], wt["ln_ffn_g"][0:1], wt["ln_mix_g"][1:2], wt["ln_ffn_g"][1:2]]
    ln_b = [wt["ln_mix_b"][0:1], wt["ln_ffn_b"][0:1], wt["ln_mix_b"][1:2], wt["ln_ffn_b"][1:2]]

    p5 = nn(h0, wt["a5"], name="a_in5")
    pz = nn(h0, wt["az"], name="a_inz")
    qkvbg = gdn_pre_fwd(p5, wt["a_conv3"], wt["alog_b"], wt["dtb_b"], tm=tm, cb=2 * HEAD_DIM)
    o, states, tinv, late_stacks = gdn_chunk_fwd(qkvbg, late_shards)
    wt = {**wt, **layout_late(late_stacks)}
    onz = gdn_post_fwd(o[None], pz, wt["anorm_b"], tm=tm)
    r1, h1 = ln_fwd(h0, nn(onz, wt["a_out"], name="a_out"), ln_g[0], ln_b[0], tm=tm, name="ln1")
    up0 = nn(h1, wt["up"][0], name="up0")
    act0 = ffn_act_fwd(up0, wt["fconv"][0], tm=tm_ffn, name="ffn_act0")
    r2, h2 = ln_fwd(h1, nn(act0, wt["down"][0], name="down0"), ln_g[1], ln_b[1], tm=tm, name="ln2")
    pb = nn(h2, wt["b_in"], name="b_in")
    sc = sc_fwd(pb, wt["b_conv"], tm=tm_ffn, cb=d)
    r3, h3 = ln_fwd(h2, nn(sc, wt["b_out"], name="b_out"), ln_g[2], ln_b[2], tm=tm, name="ln3")
    up1 = nn(h3, wt["up"][1], name="up1")
    act1 = ffn_act_fwd(up1, wt["fconv"][1], tm=tm_ffn, name="ffn_act1")
    r4, h4 = ln_fwd(h3, nn(act1, wt["down"][1], name="down1"), ln_g[3], ln_b[3], tm=tm, name="ln4")

    dh4, loss_part = loss_grad(h4, tgt, first=N_META, count=seq, tm=tm)

    grads = {}
    dr4, dgb4 = ln_bwd(r4, dh4, ln_g[3], tm=tm, name="ln4_bwd")
    d_down1 = tn(act1, dr4, name="d_down1")
    dact1 = nt(dr4, wt["down"][1], name="d_act1")
    dup1, dfconv1 = ffn_act_bwd(up1, dact1, wt["fconv"][1], tm=tm_ffn, name="ffn_act1_bwd")
    dup1 = dup1.reshape(up1.shape)
    d_up1 = tn(h3, dup1, name="d_up1")
    dh3 = nt(dup1, wt["up"][1], res=dr4, res_scale=ALPHA, name="d_h3")

    dr3, dgb3 = ln_bwd(r3, dh3, ln_g[2], tm=tm, name="ln3_bwd")
    d_bout = tn(sc, dr3, name="d_b_out")
    dsc = nt(dr3, wt["b_out"], name="d_sc")
    dpb, dbconv = sc_bwd(pb, dsc, wt["b_conv"], tm=tm_ffn, cb=d)
    d_bin = tn(h2, dpb, name="d_b_in")
    dh2 = nt(dpb, wt["b_in"], res=dr3, res_scale=ALPHA, name="d_h2")

    dr2, dgb2 = ln_bwd(r2, dh2, ln_g[1], tm=tm, name="ln2_bwd")
    d_down0 = tn(act0, dr2, name="d_down0")
    dact0 = nt(dr2, wt["down"][0], name="d_act0")
    dup0, dfconv0 = ffn_act_bwd(up0, dact0, wt["fconv"][0], tm=tm_ffn, name="ffn_act0_bwd")
    dup0 = dup0.reshape(up0.shape)
    d_up0 = tn(h1, dup0, name="d_up0")
    dh1 = nt(dup0, wt["up"][0], res=dr2, res_scale=ALPHA, name="d_h1")
    grads["b_w_in"] = [d_bin[0].transpose(1, 0, 2).reshape(d, 4, 3 * d // 4).transpose(1, 0, 2)]
    grads["b_w_out"] = [d_bout.reshape(4, d // 4, d)]
    grads["ffn_w_up"] = [d_up0[0], d_up1[0]]
    grads["ffn_w_down"] = [t.reshape(4, -1, d) for t in (d_down0, d_down1)]
    leaving = reduce_early(grads)

    dr1, dgb1 = ln_bwd(r1, dh1, ln_g[0], tm=tm, name="ln1_bwd")
    d_aout = tn(onz, dr1, name="d_a_out")
    donz = nt(dr1, wt["a_out"], name="d_onz")
    d_o, dz, dnw = gdn_post_bwd(o[None], pz, donz, wt["anorm_b"], tm=tm)
    dqkvbg, landed = gdn_chunk_bwd(qkvbg, states, tinv, d_o[0], leaving)
    dp5, daconv, dscal = gdn_pre_bwd(p5, dqkvbg, wt["a_conv3"], wt["alog_b"], wt["dtb_b"], tm=tm, cb=2 * HEAD_DIM)
    d_a5 = tn(h0, dp5, name="d_a_in5")
    d_az = tn(h0, dz, name="d_a_inz")
    dh0 = nt(dp5, wt["a5"], res=dr1, res_scale=ALPHA, name="d_h0a")
    dh0 = nt(dz, wt["az"], res=dh0, res_scale=1.0, name="d_h0")

    width = HEADS * HEAD_DIM
    d_ba = head_lane_sum(d_a5[0, 3:5])[:, :, :HEADS]
    d_a_in = jnp.concatenate([d_a5[0, 0], d_a5[0, 1], d_a5[0, 2], d_az[0, 0], d_ba[0], d_ba[1]], axis=1)
    n_in = d_a_in.shape[1] // 4
    grads["a_w_in"] = [d_a_in.reshape(d, 4, n_in).transpose(1, 0, 2)]
    grads["a_w_out"] = [d_aout.reshape(4, width // 4, d)]
    grads["a_conv"] = daconv.sum(axis=2).transpose(1, 0, 2).reshape(1, GDN_CONV, 3 * width)
    per_head = dscal.reshape(2, 8, HEADS, HEAD_DIM).sum(axis=(1, 3))
    grads["a_log"] = per_head[0][None]
    grads["a_dt_bias"] = per_head[1][None]
    grads["a_norm"] = dnw.reshape(8, HEADS, HEAD_DIM).sum(axis=(0, 1))[None]
    grads["b_conv"] = dbconv.sum(axis=1)[None]
    lns = [dgb1, dgb2, dgb3, dgb4]
    grads["ln_mix_g"] = jnp.stack([lns[0][0].sum(0), lns[2][0].sum(0)])
    grads["ln_mix_b"] = jnp.stack([lns[0][1].sum(0), lns[2][1].sum(0)])
    grads["ln_ffn_g"] = jnp.stack([lns[1][0].sum(0), lns[3][0].sum(0)])
    grads["ln_ffn_b"] = jnp.stack([lns[1][1].sum(0), lns[3][1].sum(0)])
    grads["ffn_conv"] = jnp.stack([t.sum(axis=2).transpose(1, 0, 2).reshape(FFN_CONV, -1) for t in (dfconv0, dfconv1)])
    grads["meta"] = dh0[0, :N_META]
    return loss_part, dh0, grads, landed


WEIGHTS = ["meta", "a_w_in", "a_conv", "a_log", "a_dt_bias", "a_norm", "a_w_out", "b_w_in", "b_conv", "b_w_out",
           "ln_mix_g", "ln_mix_b", "ffn_w_up", "ffn_conv", "ffn_w_down", "ln_ffn_g", "ln_ffn_b"]
EARLY_WEIGHTS = ["a_w_in", "a_w_out"]
LATE_WEIGHTS = ["b_w_in", "b_w_out", "ffn_w_up", "ffn_w_down"]
MATMUL_WEIGHTS = EARLY_WEIGHTS + LATE_WEIGHTS
SMALL_SHARDED = ["a_conv", "b_conv", "ffn_conv", "meta"]
REPLICATED = ["a_log", "a_dt_bias", "a_norm", "ln_mix_g", "ln_mix_b", "ln_ffn_g", "ln_ffn_b"]
SHARD_AXIS = {"meta": 1, "a_w_in": 2, "a_conv": 2, "a_w_out": 1, "b_w_in": 2, "b_conv": 2, "b_w_out": 1,
              "ffn_w_up": 2, "ffn_conv": 2, "ffn_w_down": 1}
PACK_COLS = 1024
PACK_ROWS_MULTIPLE = 32


def _pack(pieces, lead=()):
    flat = jnp.concatenate([p.reshape(lead + (-1,)) for p in pieces], axis=-1)
    n = flat.shape[-1]
    rows = -(-n // (PACK_COLS * PACK_ROWS_MULTIPLE)) * PACK_ROWS_MULTIPLE
    flat = jnp.pad(flat, [(0, 0)] * len(lead) + [(0, rows * PACK_COLS - n)])
    return flat.reshape(lead + (rows, PACK_COLS))


def _unpack(buf, shapes, lead=()):
    flat = buf.reshape(lead + (-1,))
    out, off = [], 0
    for shp in shapes:
        n = 1
        for s in shp:
            n *= s
        out.append(flat[..., off:off + n].reshape(lead + tuple(shp)))
        off += n
    return out


def _join_shards(stacked, axis):
    return jnp.concatenate([stacked[k] for k in range(4)], axis=axis)


def _split_shards(full, axis):
    return jnp.stack(jnp.split(full, 4, axis=axis))


def _weight_layers(w, names):
    return [w[n][l].astype(BF16) for n in names for l in range(w[n].shape[0])]


def _per_weight(arrays, w, names):
    it = iter(arrays)
    return {n: [next(it) for _ in range(w[n].shape[0])] for n in names}


def _layout_early(full, w):
    width = HEADS * HEAD_DIM
    wt = {n: w[n] for n in ("ln_mix_g", "ln_mix_b", "ln_ffn_g", "ln_ffn_b")}
    w_in = _join_shards(full["a_w_in"][0], 1)
    d = w_in.shape[0]
    n_ff = full["ffn_conv"].shape[2] // 2
    blocks = [w_in[:, s * width:(s + 1) * width] for s in range(4)]
    b_exp = jnp.repeat(w_in[:, 4 * width:4 * width + HEADS], HEAD_DIM, axis=1)
    a_exp = jnp.repeat(w_in[:, 4 * width + HEADS:], HEAD_DIM, axis=1)
    wt["a5"] = jnp.stack([blocks[0], blocks[1], blocks[2], b_exp, a_exp])[None]
    wt["az"] = blocks[3][None, None]
    wt["a_out"] = full["a_w_out"][0].reshape(1, 1, width, d)
    wt["a_conv3"] = full["a_conv"][0].reshape(GDN_CONV, 3, width).transpose(1, 0, 2)
    wt["b_conv"] = full["b_conv"][0]
    wt["fconv"] = [full["ffn_conv"][l].reshape(FFN_CONV, 2, n_ff).transpose(1, 0, 2) for l in range(2)]
    wt["meta"] = full["meta"]
    wt["alog_b"] = jnp.repeat(w["a_log"][0], HEAD_DIM)[None]
    wt["dtb_b"] = jnp.repeat(w["a_dt_bias"][0], HEAD_DIM)[None]
    wt["anorm_b"] = jnp.tile(w["a_norm"][0], HEADS)[None]
    return wt


def _layout_late(full):
    d = full["b_w_in"][0].shape[1]
    n_ff = full["ffn_w_up"][0].shape[2]
    return {
        "b_in": _join_shards(full["b_w_in"][0], 1).reshape(d, 3, d).transpose(1, 0, 2)[None],
        "b_out": full["b_w_out"][0].reshape(1, 1, d, d),
        "up": [t[None] for t in full["ffn_w_up"]],
        "down": [t.reshape(2, 1, n_ff, d) for t in full["ffn_w_down"]],
    }


def kernel(x, meta, a_w_in, a_conv, a_log, a_dt_bias, a_norm, a_w_out, b_w_in, b_conv, b_w_out, ln_mix_g, ln_mix_b, ffn_w_up, ffn_conv, ffn_w_down, ln_ffn_g, ln_ffn_b, loss_target, m_meta, m_a_w_in, m_a_conv, m_a_log, m_a_dt_bias, m_a_norm, m_a_w_out, m_b_w_in, m_b_conv, m_b_w_out, m_ln_mix_g, m_ln_mix_b, m_ffn_w_up, m_ffn_conv, m_ffn_w_down, m_ln_ffn_g, m_ln_ffn_b, v_meta, v_a_w_in, v_a_conv, v_a_log, v_a_dt_bias, v_a_norm, v_a_w_out, v_b_w_in, v_b_conv, v_b_w_out, v_ln_mix_g, v_ln_mix_b, v_ffn_w_up, v_ffn_conv, v_ffn_w_down, v_ln_ffn_g, v_ln_ffn_b):
    w = dict(meta=meta, a_w_in=a_w_in, a_conv=a_conv, a_log=a_log, a_dt_bias=a_dt_bias, a_norm=a_norm, a_w_out=a_w_out,
             b_w_in=b_w_in, b_conv=b_conv, b_w_out=b_w_out, ln_mix_g=ln_mix_g, ln_mix_b=ln_mix_b, ffn_w_up=ffn_w_up,
             ffn_conv=ffn_conv, ffn_w_down=ffn_w_down, ln_ffn_g=ln_ffn_g, ln_ffn_b=ln_ffn_b)
    m = dict(meta=m_meta, a_w_in=m_a_w_in, a_conv=m_a_conv, a_log=m_a_log, a_dt_bias=m_a_dt_bias, a_norm=m_a_norm,
             a_w_out=m_a_w_out, b_w_in=m_b_w_in, b_conv=m_b_conv, b_w_out=m_b_w_out, ln_mix_g=m_ln_mix_g,
             ln_mix_b=m_ln_mix_b, ffn_w_up=m_ffn_w_up, ffn_conv=m_ffn_conv, ffn_w_down=m_ffn_w_down,
             ln_ffn_g=m_ln_ffn_g, ln_ffn_b=m_ln_ffn_b)
    v = dict(meta=v_meta, a_w_in=v_a_w_in, a_conv=v_a_conv, a_log=v_a_log, a_dt_bias=v_a_dt_bias, a_norm=v_a_norm,
             a_w_out=v_a_w_out, b_w_in=v_b_w_in, b_conv=v_b_conv, b_w_out=v_b_w_out, ln_mix_g=v_ln_mix_g,
             ln_mix_b=v_ln_mix_b, ffn_w_up=v_ffn_w_up, ffn_conv=v_ffn_conv, ffn_w_down=v_ffn_w_down,
             ln_ffn_g=v_ln_ffn_g, ln_ffn_b=v_ln_ffn_b)
    seq = x.shape[1]
    *stacks, small = all_gather_shards(_weight_layers(w, EARLY_WEIGHTS) + [_pack([w[n] for n in SMALL_SHARDED])],
                                       "gather_early")
    full = _per_weight(stacks, w, EARLY_WEIGHTS)
    for n, t in zip(SMALL_SHARDED, _unpack(small, [w[n].shape for n in SMALL_SHARDED], lead=(4,))):
        full[n] = _join_shards(t, SHARD_AXIS[n])

    def layout_late(late_stacks):
        return _layout_late(_per_weight(late_stacks, w, LATE_WEIGHTS))

    def reduce_early(grads):
        bufs = [g for n in LATE_WEIGHTS for g in grads[n]]
        return pair_sums(bufs, [BF16] * len(bufs), "late")

    loss_part, dh0, grads, landed_late = _local_step(x[0], loss_target[0], _layout_early(full, w),
                                                     _weight_layers(w, LATE_WEIGHTS), layout_late, reduce_early)
    pieces = [_split_shards(grads[n], SHARD_AXIS[n]) for n in SMALL_SHARDED]
    same = jnp.concatenate([grads[n].reshape(-1) for n in REPLICATED] + [jnp.sum(loss_part).reshape(1)])
    pieces.append(jnp.broadcast_to(same, (4,) + same.shape))
    bufs = [g for n in EARLY_WEIGHTS for g in grads[n]] + [_pack(pieces, lead=(4,))]
    landed = scatter_to_chips(pair_sums(bufs, [BF16] * (len(bufs) - 1) + [F32], "early"), "rs_chips_early")
    totals = [chip_sum(t, "rs_chip_sum%d" % i) for i, t in enumerate(landed + landed_late)]
    by_weight = _per_weight(totals[:len(bufs) - 1] + totals[len(bufs):], w, MATMUL_WEIGHTS)
    *shared, small_total = share_halves([by_weight[n] for n in MATMUL_WEIGHTS] + [[totals[len(bufs) - 1]]], "rs_share")
    grad_w = {n: t.reshape(w[n].shape) for n, t in zip(MATMUL_WEIGHTS, shared)}
    rest = SMALL_SHARDED + REPLICATED
    unpacked = _unpack(small_total[0], [w[n].shape for n in rest] + [()])
    grad_w.update(zip(rest, unpacked[:-1]))
    loss = unpacked[-1]
    grad_x = dh0[:, N_META:N_META + seq]
    steps = [adamw(w[n], grad_w[n], m[n], v[n], "adamw_" + n) for n in WEIGHTS]
    return (loss, grad_x, *[grad_w[n] for n in WEIGHTS], *[s[0] for s in steps], *[s[1] for s in steps],
            *[s[2] for s in steps])
```

```python
import functools

import jax
import jax.numpy as jnp
from jax import lax
from jax.experimental import pallas as pl
from jax.experimental.pallas import tpu as pltpu

F32 = jnp.float32
BF16 = jnp.bfloat16
HI = lax.Precision.HIGHEST

N_META = 16
HEADS = 8
HEAD_DIM = 128
CHUNK = 64
GDN_CONV = 4
SC_CONV = 3
FFN_CONV = 3
ALPHA = 4.0 ** 0.25
LN_EPS = 1e-5
RMS_EPS = 1e-6
L2_EPS = 1e-6
Q_SCALE = HEAD_DIM ** -0.5

ADAM_LR = 0.001
ADAM_B1 = 0.9
ADAM_B2 = 0.999
ADAM_EPS = 1e-08
ADAM_WD = 0.01
ADAM_STEP = 10

HALO = 8
VMEM_LIMIT = 48 * 1024 * 1024


def _params(sem=None):
    return pltpu.CompilerParams(dimension_semantics=sem, vmem_limit_bytes=VMEM_LIMIT)


def _dot(a, b, prec=None):
    return jnp.dot(a, b, preferred_element_type=F32, precision=prec)


def _dot_nt(a, b, prec=None):
    return lax.dot_general(a, b, (((1,), (1,)), ((), ())), preferred_element_type=F32, precision=prec)


def _dot_tn(a, b, prec=None):
    return lax.dot_general(a, b, (((0,), (0,)), ((), ())), preferred_element_type=F32, precision=prec)


def _sigmoid(x):
    return 1.0 / (1.0 + jnp.exp(-x))


def _tri_masks():
    r = lax.broadcasted_iota(jnp.int32, (CHUNK, CHUNK), 0)
    c = lax.broadcasted_iota(jnp.int32, (CHUNK, CHUNK), 1)
    return r >= c, r > c, r == c


def _split_hi_lo(x):
    hi = x.astype(BF16)
    return hi, (x - hi.astype(F32)).astype(BF16)


def _mask_dot(mask, x):
    hi, lo = _split_hi_lo(x)
    return _dot(mask, hi) + _dot(mask, lo)


@jax.custom_vjp
def _cumsum_rows(g):
    causal, _, _ = _tri_masks()
    return _mask_dot(causal.astype(BF16), g)


def _cumsum_rows_fwd(g):
    return _cumsum_rows(g), None


def _cumsum_rows_bwd(_, dy):
    _, strict, _ = _tri_masks()
    return (_mask_dot((~strict).astype(BF16), dy),)


_cumsum_rows.defvjp(_cumsum_rows_fwd, _cumsum_rows_bwd)


def _dot_split3(a, b):
    a_hi, a_lo = _split_hi_lo(a)
    b_hi, b_lo = _split_hi_lo(b)
    return _dot(a_hi, b_hi) + (_dot(a_hi, b_lo) + _dot(a_lo, b_hi))


def _gdn_m(ks, g64s, bbs):
    causal, strict, _ = _tri_masks()
    a = [_cumsum_rows(g) for g in g64s]
    decay = [jnp.exp(jnp.where(causal, x - x.T, -1e30)) for x in a]
    kk = [_dot_nt(k * b, k) for k, b in zip(ks, bbs)]
    return [jnp.where(strict, x * d, 0.0) for x, d in zip(kk, decay)]


def _gdn_inverse(ms):
    r = lax.broadcasted_iota(jnp.int32, (CHUNK, CHUNK), 0)
    c = lax.broadcasted_iota(jnp.int32, (CHUNK, CHUNK), 1)
    eye = (r == c).astype(F32)
    same = [jnp.right_shift(r, s) == jnp.right_shift(c, s) for s in (3, 4, 5)]
    d = [jnp.where(same[0], m, 0.0) for m in ms]
    p = [_dot(x, x) for x in d]
    t = [eye - x for x in d]
    t = [x + _dot(x, y) for x, y in zip(t, p)]
    p = [_dot(x, x) for x in p]
    t = [x + _dot(x, y) for x, y in zip(t, p)]
    for inner, outer in ((same[0], same[1]), (same[1], same[2]), (same[2], None)):
        joins = ~inner if outer is None else (outer & ~inner)
        o = [_dot(x, jnp.where(joins, m, 0.0)) for x, m in zip(t, ms)]
        t = [x - _dot(y, x) for x, y in zip(t, o)]
    res = [eye - x - _dot_split3(m, x) for m, x in zip(ms, t)]
    return [x + _dot(x, y) for x, y in zip(t, res)]


def _gdn_apply(qs, ks, vs, gbs, g64s, bbs, ss, ts):
    causal, _, _ = _tri_masks()
    n = range(len(qs))
    gc = [_cumsum_rows(g) for g in gbs]
    a = [_cumsum_rows(g) for g in g64s]
    decay = [jnp.exp(jnp.where(causal, x - x.T, -1e30)) for x in a]
    eg = [jnp.exp(x) for x in gc]
    u = [_dot(ts[h], vs[h] * bbs[h]) for h in n]
    w = [_dot(ts[h], ks[h] * bbs[h] * eg[h]) for h in n]
    qk = [_dot_nt(qs[h], ks[h]) * decay[h] for h in n]
    gl = [jnp.sum(g, axis=0, keepdims=True) for g in gbs]
    kd = [ks[h] * jnp.exp(gl[h] - gc[h]) for h in n]
    v_new = [u[h] - _dot(w[h], ss[h]) for h in n]
    o = [_dot(qs[h] * eg[h], ss[h]) + _dot(qk[h], v_new[h]) for h in n]
    s2 = [ss[h] * jnp.exp(gl[h]) + _dot_tn(kd[h], v_new[h]) for h in n]
    return o, s2


def _head_slices(h):
    return slice(h * HEAD_DIM, (h + 1) * HEAD_DIM), slice(h * HEAD_DIM, h * HEAD_DIM + CHUNK)


def _gdn_head_values(x_ref):
    out = [[], [], [], [], [], []]
    for h in range(HEADS):
        sl, sl64 = _head_slices(h)
        for lst, val in zip(out, (x_ref[0, :, sl], x_ref[1, :, sl], x_ref[2, :, sl], x_ref[4, :, sl],
                                  x_ref[4, :, sl64], x_ref[3, :, sl])):
            lst.append(val)
    return out


def gdn_chunk_fwd(qkvbg, gather=()):
    _, lp, width = qkvbg.shape
    n_chunks = lp // CHUNK
    n = len(gather)

    def body(x_ref, *refs):
        shard_refs, (o_ref, s_ref, t_ref), refs = refs[:n], refs[n:n + 3], refs[n + 3:]
        stack_refs, state, sems = refs[:n], refs[n], refs[n + 1:]
        copies = _gather_copies(shard_refs, stack_refs, *sems) if n else None

        @pl.when(pl.program_id(0) == 0)
        def _():
            state[...] = jnp.zeros_like(state)
            if n:
                _gather_start(copies)

        qs, ks, vs, gbs, g64s, bbs = _gdn_head_values(x_ref)
        ss = [state[h] for h in range(HEADS)]
        ts = _gdn_inverse(_gdn_m(ks, g64s, bbs))
        os_, s2 = _gdn_apply(qs, ks, vs, gbs, g64s, bbs, ss, ts)
        for h in range(HEADS):
            s_ref[0, h] = ss[h]
            t_ref[0, h] = ts[h]
            o_ref[:, _head_slices(h)[0]] = os_[h]
            state[h] = s2[h]

        if n:
            @pl.when(pl.program_id(0) == n_chunks - 1)
            def _():
                _gather_finish(copies)

    o, states, tinv, *stacks = pl.pallas_call(
        body,
        name="gdn_chunk_fwd",
        grid=(n_chunks,),
        in_specs=[pl.BlockSpec((5, CHUNK, width), lambda c: (0, c, 0))] + [ANY] * n,
        out_specs=[
            pl.BlockSpec((CHUNK, width), lambda c: (c, 0)),
            pl.BlockSpec((1, HEADS, HEAD_DIM, HEAD_DIM), lambda c: (c, 0, 0, 0)),
            pl.BlockSpec((1, HEADS, CHUNK, CHUNK), lambda c: (c, 0, 0, 0)),
        ] + [ANY] * n,
        out_shape=[
            jax.ShapeDtypeStruct((lp, width), F32),
            jax.ShapeDtypeStruct((n_chunks, HEADS, HEAD_DIM, HEAD_DIM), F32),
            jax.ShapeDtypeStruct((n_chunks, HEADS, CHUNK, CHUNK), F32),
        ] + _gather_out_shapes(gather),
        scratch_shapes=[pltpu.VMEM((HEADS, HEAD_DIM, HEAD_DIM), F32)] + (_gather_sems(n) if n else []),
        compiler_params=_params(("arbitrary",)),
    )(qkvbg, *gather)
    return o, states, tinv, _set_own_slots(stacks, gather)


def gdn_chunk_bwd(qkvbg, states, tinv, d_o, scatter=()):
    _, lp, width = qkvbg.shape
    n_chunks = lp // CHUNK
    last = n_chunks - 1
    n = len(scatter)

    def body(x_ref, s_ref, t_ref, do_ref, *refs):
        leaving_refs, dx_ref, refs = refs[:n], refs[n], refs[n + 1:]
        landing_refs, dstate, sems = refs[:n], refs[n], refs[n + 1:]
        copies = _scatter_copies(leaving_refs, landing_refs, *sems) if n else None

        @pl.when(pl.program_id(0) == 0)
        def _():
            dstate[...] = jnp.zeros_like(dstate)
            if n:
                _scatter_start(copies)

        heads = range(HEADS)
        qs, ks, vs, gbs, g64s, bbs = _gdn_head_values(x_ref)
        ss = [s_ref[0, h] for h in heads]
        ts = [t_ref[0, h] for h in heads]
        d_out = ([do_ref[:, _head_slices(h)[0]] for h in heads], [dstate[h] for h in heads])
        _, vjp_apply = jax.vjp(_gdn_apply, qs, ks, vs, gbs, g64s, bbs, ss, ts)
        dq, dk, dv, dgb, dg64, dbb, ds, dt = vjp_apply(d_out)
        tts = [t.T for t in ts]
        dm = [_dot(tts[h], dt[h]) for h in heads]
        dm = [-_dot(dm[h], tts[h]) for h in heads]
        _, vjp_m = jax.vjp(_gdn_m, ks, g64s, bbs)
        dk2, dg64m, dbb2 = vjp_m(dm)
        for h in heads:
            sl, sl64 = _head_slices(h)
            dx_ref[0, :, sl] = dq[h]
            dx_ref[1, :, sl] = dk[h] + dk2[h]
            dx_ref[2, :, sl] = dv[h]
            dx_ref[3, :, sl] = dbb[h] + dbb2[h]
            dx_ref[4, :, sl] = dgb[h]
            dx_ref[4, :, sl64] += dg64[h] + dg64m[h]
            dstate[h] = ds[h]

        if n:
            @pl.when(pl.program_id(0) == n_chunks - 1)
            def _():
                _scatter_finish(copies)

    dqkvbg, *landed = pl.pallas_call(
        body,
        name="gdn_chunk_bwd",
        grid=(n_chunks,),
        in_specs=[
            pl.BlockSpec((5, CHUNK, width), lambda c: (0, last - c, 0)),
            pl.BlockSpec((1, HEADS, HEAD_DIM, HEAD_DIM), lambda c: (last - c, 0, 0, 0)),
            pl.BlockSpec((1, HEADS, CHUNK, CHUNK), lambda c: (last - c, 0, 0, 0)),
            pl.BlockSpec((CHUNK, width), lambda c: (last - c, 0)),
        ] + [ANY] * n,
        out_specs=[pl.BlockSpec((5, CHUNK, width), lambda c: (0, last - c, 0))] + [ANY] * n,
        out_shape=[jax.ShapeDtypeStruct(qkvbg.shape, F32)] + [jax.ShapeDtypeStruct(b.shape, b.dtype) for b in scatter],
        scratch_shapes=[pltpu.VMEM((HEADS, HEAD_DIM, HEAD_DIM), F32)] + (_scatter_sems(n) if n else []),
        compiler_params=_params(("arbitrary",)),
    )(qkvbg, states, tinv, d_o, *scatter)
    return dqkvbg, _keep_own_slots(landed, scatter)


def mm_nn(a, b, *, tm, name):
    ks, m, tk = a.shape
    _, ns, _, tn = b.shape

    def body(a_ref, b_ref, o_ref):
        p = _dot(a_ref[...].astype(BF16), b_ref[...])

        @pl.when(pl.program_id(2) == 0)
        def _():
            o_ref[...] = p

        @pl.when(pl.program_id(2) > 0)
        def _():
            o_ref[...] += p

    return pl.pallas_call(
        body,
        name=name,
        grid=(ns, m // tm, ks),
        in_specs=[
            pl.BlockSpec((None, tm, tk), lambda n, i, k: (k, i, 0)),
            pl.BlockSpec((None, None, tk, tn), lambda n, i, k: (k, n, 0, 0)),
        ],
        out_specs=pl.BlockSpec((None, tm, tn), lambda n, i, k: (n, i, 0)),
        out_shape=jax.ShapeDtypeStruct((ns, m, tn), F32),
        compiler_params=_params(("arbitrary", "arbitrary", "arbitrary")),
    )(a, b)


def mm_nt(dy, w, *, tm, name, res=None, res_scale=1.0):
    ns, m, tn = dy.shape
    ks, _, tk, _ = w.shape

    def body(*refs):
        if res is None:
            dy_ref, w_ref, o_ref = refs
        else:
            dy_ref, w_ref, r_ref, o_ref = refs
        p = _dot_nt(dy_ref[...].astype(BF16), w_ref[...])

        @pl.when(pl.program_id(2) == 0)
        def _():
            o_ref[...] = p if res is None else p + res_scale * r_ref[...]

        @pl.when(pl.program_id(2) > 0)
        def _():
            o_ref[...] += p

    in_specs = [
        pl.BlockSpec((None, tm, tn), lambda k, i, n: (n, i, 0)),
        pl.BlockSpec((None, None, tk, tn), lambda k, i, n: (k, n, 0, 0)),
    ]
    args = [dy, w]
    if res is not None:
        in_specs.append(pl.BlockSpec((None, tm, tk), lambda k, i, n: (k, i, 0)))
        args.append(res)
    return pl.pallas_call(
        body,
        name=name,
        grid=(ks, m // tm, ns),
        in_specs=in_specs,
        out_specs=pl.BlockSpec((None, tm, tk), lambda k, i, n: (k, i, 0)),
        out_shape=jax.ShapeDtypeStruct((ks, m, tk), F32),
        compiler_params=_params(("arbitrary", "arbitrary", "arbitrary")),
    )(*args)


def mm_tn(x, dy, *, tm, name):
    ks, m, tk = x.shape
    ns, _, tn = dy.shape

    def body(x_ref, dy_ref, o_ref):
        p = _dot_tn(x_ref[...].astype(BF16), dy_ref[...].astype(BF16))

        @pl.when(pl.program_id(2) == 0)
        def _():
            o_ref[...] = p

        @pl.when(pl.program_id(2) > 0)
        def _():
            o_ref[...] += p

    return pl.pallas_call(
        body,
        name=name,
        grid=(ks, ns, m // tm),
        in_specs=[
            pl.BlockSpec((None, tm, tk), lambda k, n, i: (k, i, 0)),
            pl.BlockSpec((None, tm, tn), lambda k, n, i: (n, i, 0)),
        ],
        out_specs=pl.BlockSpec((None, None, tk, tn), lambda k, n, i: (k, n, 0, 0)),
        out_shape=jax.ShapeDtypeStruct((ks, ns, tk, tn), F32),
        compiler_params=_params(("arbitrary", "arbitrary", "arbitrary")),
    )(x, dy)


def _row_partial(x):
    rows, c = x.shape
    return jnp.sum(x.reshape(rows // 8, 8, c), axis=0)


def ln_fwd(h_prev, mix, g, b, *, tm, name):
    _, lp, d = h_prev.shape

    def body(h_ref, m_ref, g_ref, b_ref, r_ref, o_ref):
        r = ALPHA * h_ref[...] + m_ref[...]
        mu = jnp.mean(r, axis=-1, keepdims=True)
        xc = r - mu
        var = jnp.mean(xc * xc, axis=-1, keepdims=True)
        r_ref[...] = r
        o_ref[...] = xc * lax.rsqrt(var + LN_EPS) * g_ref[...] + b_ref[...]

    row = pl.BlockSpec((None, tm, d), lambda i: (0, i, 0))
    vec = pl.BlockSpec((1, d), lambda i: (0, 0))
    return pl.pallas_call(
        body,
        name=name,
        grid=(lp // tm,),
        in_specs=[row, row, vec, vec],
        out_specs=[row, row],
        out_shape=[jax.ShapeDtypeStruct((1, lp, d), F32)] * 2,
        compiler_params=_params(("arbitrary",)),
    )(h_prev, mix, g, b)


def ln_bwd(r, dh, g, *, tm, name):
    _, lp, d = r.shape

    def body(r_ref, dh_ref, g_ref, dr_ref, dgb_ref):
        x = r_ref[...]
        dh_v = dh_ref[...]
        mu = jnp.mean(x, axis=-1, keepdims=True)
        xc = x - mu
        rstd = lax.rsqrt(jnp.mean(xc * xc, axis=-1, keepdims=True) + LN_EPS)
        xh = xc * rstd
        dxh = dh_v * g_ref[...]
        m1 = jnp.mean(dxh, axis=-1, keepdims=True)
        m2 = jnp.mean(dxh * xh, axis=-1, keepdims=True)
        dr_ref[...] = rstd * (dxh - m1 - xh * m2)

        @pl.when(pl.program_id(0) == 0)
        def _():
            dgb_ref[...] = jnp.zeros_like(dgb_ref)

        dgb_ref[0] += _row_partial(dh_v * xh)
        dgb_ref[1] += _row_partial(dh_v)

    row = pl.BlockSpec((None, tm, d), lambda i: (0, i, 0))
    return pl.pallas_call(
        body,
        name=name,
        grid=(lp // tm,),
        in_specs=[row, row, pl.BlockSpec((1, d), lambda i: (0, 0))],
        out_specs=[row, pl.BlockSpec((2, 8, d), lambda i: (0, 0, 0))],
        out_shape=[jax.ShapeDtypeStruct((1, lp, d), F32), jax.ShapeDtypeStruct((2, 8, d), F32)],
        compiler_params=_params(("arbitrary",)),
    )(r, dh, g)


def loss_grad(h, target, *, first, count, tm):
    _, lp, d = h.shape

    def body(h_ref, t_ref, dh_ref, l_ref):
        row = pl.program_id(0) * tm + lax.broadcasted_iota(jnp.int32, (tm, d), 0)
        valid = (row >= first) & (row < first + count)
        err = jnp.where(valid, h_ref[...] - t_ref[...], 0.0)
        dh_ref[...] = err * (1.0 / d)

        @pl.when(pl.program_id(0) == 0)
        def _():
            l_ref[...] = jnp.zeros_like(l_ref)

        l_ref[...] += _row_partial(err * err) * (0.5 / d)

    return pl.pallas_call(
        body,
        name="loss_grad",
        grid=(lp // tm,),
        in_specs=[pl.BlockSpec((None, tm, d), lambda i: (0, i, 0)), pl.BlockSpec((tm, d), lambda i: (i, 0))],
        out_specs=[pl.BlockSpec((None, tm, d), lambda i: (0, i, 0)), pl.BlockSpec((8, d), lambda i: (0, 0))],
        out_shape=[jax.ShapeDtypeStruct((1, lp, d), F32), jax.ShapeDtypeStruct((8, d), F32)],
        compiler_params=_params(("arbitrary",)),
    )(h, target)


def _halo_index(tile, tm):
    return jnp.maximum(tile * (tm // HALO) - 1, 0)


def _conv_fwd(xs_ref, w, taps, tm):
    acc = w(0) * xs_ref[pl.ds(HALO - taps + 1, tm), :]
    for j in range(1, taps):
        acc += w(j) * xs_ref[pl.ds(HALO - taps + 1 + j, tm), :]
    return acc


def _conv_bwd_x(dcs_ref, w, taps, tm):
    acc = w(0) * dcs_ref[pl.ds(taps - 1, tm), :]
    for j in range(1, taps):
        acc += w(j) * dcs_ref[pl.ds(taps - 1 - j, tm), :]
    return acc


SUB = 8
LANES = 128
PAIR = 2 * SUB
STRIP_UNROLL = 2


def _pair_rows(r0):
    return pl.ds(r0, SUB), pl.ds(r0 + SUB if isinstance(r0, int) else pl.multiple_of(r0 + SUB, SUB), SUB)


def _shift_down(cur, prev, s):
    if s == 0:
        return cur
    row = lax.broadcasted_iota(jnp.int32, cur.shape, 0)
    return jnp.where(row < s, pltpu.roll(prev, s, axis=0), pltpu.roll(cur, s, axis=0))


def _shift_up(cur, nxt, s):
    if s == 0:
        return cur
    row = lax.broadcasted_iota(jnp.int32, cur.shape, 0)
    return jnp.where(row < SUB - s, pltpu.roll(cur, SUB - s, axis=0), pltpu.roll(nxt, SUB - s, axis=0))


def _silu_parts(c):
    sg = _sigmoid(c)
    return c * sg, sg * (1.0 + c * (1.0 - sg))


def _head_sum(x):
    rows, c = x.shape
    parts = []
    for h in range(c // HEAD_DIM):
        s = jnp.sum(x[:, h * HEAD_DIM:(h + 1) * HEAD_DIM], axis=-1, keepdims=True)
        parts.append(jnp.broadcast_to(s, (rows, HEAD_DIM)))
    return parts[0] if len(parts) == 1 else jnp.concatenate(parts, axis=-1)


def _log1p(y):
    u = 1.0 + y
    d = u - 1.0
    return jnp.where(d == 0.0, y, jnp.log(u) * (y / jnp.where(d == 0.0, 1.0, d)))


def _softplus(x):
    return jnp.maximum(x, 0.0) + _log1p(jnp.exp(-jnp.abs(x)))


def gdn_pre_fwd(p5, conv_w, alog_b, dtb_b, *, tm, cb):
    _, lp, width = p5.shape
    taps = conv_w.shape[1]

    def body(x_ref, halo_ref, w_ref, al_ref, dt_ref, o_ref, xs):
        i = pl.program_id(1)
        for s in range(3):
            xs[s, 0:HALO, :] = jnp.where(i > 0, halo_ref[s], 0.0)
            xs[s, HALO:, :] = x_ref[s]
            c = _conv_fwd(xs.at[s], lambda j, s=s: w_ref[s, j:j + 1, :], taps, tm)
            y, _ = _silu_parts(c)
            if s < 2:
                y = y * lax.rsqrt(_head_sum(y * y) + L2_EPS)
                if s == 0:
                    y = y * Q_SCALE
            o_ref[s] = y
        o_ref[3] = _sigmoid(x_ref[3])
        o_ref[4] = -jnp.exp(al_ref[...]) * _softplus(x_ref[4] + dt_ref[...])

    return pl.pallas_call(
        body,
        name="gdn_pre_fwd",
        grid=(width // cb, lp // tm),
        in_specs=[
            pl.BlockSpec((5, tm, cb), lambda j, i: (0, i, j)),
            pl.BlockSpec((3, HALO, cb), lambda j, i: (0, _halo_index(i, tm), j)),
            pl.BlockSpec((3, taps, cb), lambda j, i: (0, 0, j)),
            pl.BlockSpec((1, cb), lambda j, i: (0, j)),
            pl.BlockSpec((1, cb), lambda j, i: (0, j)),
        ],
        out_specs=pl.BlockSpec((5, tm, cb), lambda j, i: (0, i, j)),
        out_shape=jax.ShapeDtypeStruct((5, lp, width), F32),
        scratch_shapes=[pltpu.VMEM((3, tm + HALO, cb), F32)],
        compiler_params=_params(("arbitrary", "arbitrary")),
    )(p5, p5, conv_w, alog_b, dtb_b)


def gdn_pre_bwd(p5, dqkvbg, conv_w, alog_b, dtb_b, *, tm, cb):
    _, lp, width = p5.shape
    taps = conv_w.shape[1]
    last = lp // tm - 1

    def body(x_ref, halo_ref, d_ref, w_ref, al_ref, dt_ref, dx_ref, dw_ref, dsc_ref, xs, dcs, carry):
        step = pl.program_id(1)
        tile = last - step

        @pl.when(step == 0)
        def _():
            carry[...] = jnp.zeros_like(carry)
            dw_ref[...] = jnp.zeros_like(dw_ref)
            dsc_ref[...] = jnp.zeros_like(dsc_ref)

        for s in range(3):
            w = lambda j, s=s: w_ref[s, j:j + 1, :]
            xs[s, 0:HALO, :] = jnp.where(tile > 0, halo_ref[s], 0.0)
            xs[s, HALO:, :] = x_ref[s]
            c = _conv_fwd(xs.at[s], w, taps, tm)
            y, dsilu = _silu_parts(c)
            dy = d_ref[s]
            if s < 2:
                rn = lax.rsqrt(_head_sum(y * y) + L2_EPS)
                yn = y * rn
                if s == 0:
                    dy = dy * Q_SCALE
                dy = rn * (dy - yn * _head_sum(dy * yn))
            dc = dy * dsilu
            dcs[s, 0:tm, :] = dc
            dcs[s, tm:, :] = carry[s]
            dx_ref[s] = _conv_bwd_x(dcs.at[s], w, taps, tm).astype(dx_ref.dtype)
            carry[s] = dc[0:HALO, :]
            for j in range(taps):
                dw_ref[s, j] += _row_partial(dc * xs[s, pl.ds(HALO - taps + 1 + j, tm), :])
        beta = _sigmoid(x_ref[3])
        dx_ref[3] = (d_ref[3] * beta * (1.0 - beta)).astype(dx_ref.dtype)
        z = x_ref[4] + dt_ref[...]
        dg = d_ref[4] * -jnp.exp(al_ref[...])
        da = dg * _sigmoid(z)
        dx_ref[4] = da.astype(dx_ref.dtype)
        dsc_ref[0] += _row_partial(dg * _softplus(z))
        dsc_ref[1] += _row_partial(da)

    tile_spec = pl.BlockSpec((5, tm, cb), lambda j, i: (0, last - i, j))
    return pl.pallas_call(
        body,
        name="gdn_pre_bwd",
        grid=(width // cb, lp // tm),
        in_specs=[
            tile_spec,
            pl.BlockSpec((3, HALO, cb), lambda j, i: (0, _halo_index(last - i, tm), j)),
            tile_spec,
            pl.BlockSpec((3, taps, cb), lambda j, i: (0, 0, j)),
            pl.BlockSpec((1, cb), lambda j, i: (0, j)),
            pl.BlockSpec((1, cb), lambda j, i: (0, j)),
        ],
        out_specs=[
            tile_spec,
            pl.BlockSpec((3, taps, SUB, cb), lambda j, i: (0, 0, 0, j)),
            pl.BlockSpec((2, SUB, cb), lambda j, i: (0, 0, j)),
        ],
        out_shape=[
            jax.ShapeDtypeStruct((5, lp, width), BF16),
            jax.ShapeDtypeStruct((3, taps, SUB, width), F32),
            jax.ShapeDtypeStruct((2, SUB, width), F32),
        ],
        scratch_shapes=[
            pltpu.VMEM((3, tm + HALO, cb), F32),
            pltpu.VMEM((3, tm + HALO, cb), F32),
            pltpu.VMEM((3, HALO, cb), F32),
        ],
        compiler_params=_params(("arbitrary", "arbitrary")),
    )(p5, p5, dqkvbg, conv_w, alog_b, dtb_b)


def gdn_post_fwd(o, z, nw_b, *, tm):
    _, lp, width = o.shape

    def body(o_ref, z_ref, nw_ref, y_ref):
        ov = o_ref[...]
        rn = lax.rsqrt(_head_sum(ov * ov) * (1.0 / HEAD_DIM) + RMS_EPS)
        gate, _ = _silu_parts(z_ref[...])
        y_ref[...] = (ov * rn * nw_ref[...] * gate).astype(y_ref.dtype)

    row = pl.BlockSpec((None, tm, width), lambda i: (0, i, 0))
    return pl.pallas_call(
        body,
        name="gdn_post_fwd",
        grid=(lp // tm,),
        in_specs=[row, row, pl.BlockSpec((1, width), lambda i: (0, 0))],
        out_specs=row,
        out_shape=jax.ShapeDtypeStruct((1, lp, width), BF16),
        compiler_params=_params(("arbitrary",)),
    )(o, z, nw_b)


def gdn_post_bwd(o, z, dy, nw_b, *, tm):
    _, lp, width = o.shape

    def body(o_ref, z_ref, dy_ref, nw_ref, do_ref, dz_ref, dnw_ref):
        ov = o_ref[...]
        rn = lax.rsqrt(_head_sum(ov * ov) * (1.0 / HEAD_DIM) + RMS_EPS)
        yn = ov * rn
        gate, dgate = _silu_parts(z_ref[...])
        d_on = dy_ref[...] * gate
        dz_ref[...] = (dy_ref[...] * yn * nw_ref[...] * dgate).astype(dz_ref.dtype)
        a = d_on * nw_ref[...]
        do_ref[...] = rn * (a - yn * (_head_sum(a * yn) * (1.0 / HEAD_DIM)))

        @pl.when(pl.program_id(0) == 0)
        def _():
            dnw_ref[...] = jnp.zeros_like(dnw_ref)

        dnw_ref[...] += _row_partial(d_on * yn)

    row = pl.BlockSpec((None, tm, width), lambda i: (0, i, 0))
    return pl.pallas_call(
        body,
        name="gdn_post_bwd",
        grid=(lp // tm,),
        in_specs=[row, row, row, pl.BlockSpec((1, width), lambda i: (0, 0))],
        out_specs=[row, row, pl.BlockSpec((8, width), lambda i: (0, 0))],
        out_shape=[jax.ShapeDtypeStruct((1, lp, width), F32), jax.ShapeDtypeStruct((1, lp, width), BF16),
                   jax.ShapeDtypeStruct((8, width), F32)],
        compiler_params=_params(("arbitrary",)),
    )(o, z, dy, nw_b)


def head_lane_sum(x):
    s_n, rows, width = x.shape

    def body(x_ref, o_ref):
        lane = lax.broadcasted_iota(jnp.int32, (rows, HEAD_DIM), 1)
        acc = jnp.zeros((rows, HEAD_DIM), F32)
        for h in range(width // HEAD_DIM):
            s = jnp.sum(x_ref[:, h * HEAD_DIM:(h + 1) * HEAD_DIM], axis=-1, keepdims=True)
            acc = jnp.where(lane == h, s, acc)
        o_ref[...] = acc

    return pl.pallas_call(
        body,
        name="head_lane_sum",
        grid=(s_n,),
        in_specs=[pl.BlockSpec((None, rows, width), lambda s: (s, 0, 0))],
        out_specs=pl.BlockSpec((None, rows, HEAD_DIM), lambda s: (s, 0, 0)),
        out_shape=jax.ShapeDtypeStruct((s_n, rows, HEAD_DIM), F32),
        compiler_params=_params(("arbitrary",)),
    )(x)


def ffn_act_fwd(up, conv_w, *, tm, name):
    _, lp, c_w = up.shape
    taps = conv_w.shape[1]

    def body(u_ref, halo_ref, g_ref, w_ref, o_ref):
        first_tile = pl.program_id(1) == 0

        def strip(cur, prev, rows, cs):
            conv = w_ref[taps - 1:taps, cs] * cur
            for j in range(taps - 1):
                conv += w_ref[j:j + 1, cs] * _shift_down(cur, prev, taps - 1 - j)
            y, _ = _silu_parts(conv)
            return y * g_ref[rows, cs]

        def pair(r0, above_of):
            top, bot = _pair_rows(r0)
            for c0 in range(0, c_w, LANES):
                cs = slice(c0, c0 + LANES)
                cur_t, cur_b = u_ref[top, cs], u_ref[bot, cs]
                out = [strip(cur_t, above_of(cs), top, cs), strip(cur_b, cur_t, bot, cs)]
                o_ref[pl.ds(r0, PAIR), cs] = jnp.concatenate(out, axis=0).astype(o_ref.dtype)

        pair(0, lambda cs: jnp.where(first_tile, 0.0, halo_ref[:, cs]))

        def loop_body(s, carry):
            r0 = pl.multiple_of(s * PAIR, PAIR)
            pair(r0, lambda cs: u_ref[pl.ds(pl.multiple_of(r0 - SUB, SUB), SUB), cs])
            return carry

        lax.fori_loop(1, tm // PAIR, loop_body, 0, unroll=STRIP_UNROLL)

    return pl.pallas_call(
        body,
        name=name,
        grid=(2, lp // tm),
        in_specs=[
            pl.BlockSpec((None, tm, c_w), lambda s, i: (s, i, 0)),
            pl.BlockSpec((None, HALO, c_w), lambda s, i: (s, _halo_index(i, tm), 0)),
            pl.BlockSpec((None, tm, c_w), lambda s, i: (2 + s, i, 0)),
            pl.BlockSpec((None, taps, c_w), lambda s, i: (s, 0, 0)),
        ],
        out_specs=pl.BlockSpec((None, tm, c_w), lambda s, i: (s, i, 0)),
        out_shape=jax.ShapeDtypeStruct((2, lp, c_w), BF16),
        compiler_params=_params(("arbitrary", "arbitrary")),
    )(up, up, up, conv_w)


def ffn_act_bwd(up, dact, conv_w, *, tm, name):
    _, lp, c_w = up.shape
    taps = conv_w.shape[1]
    last = lp // tm - 1
    n_pairs = tm // PAIR

    def body(u_ref, halo_ref, g_ref, d_ref, w_ref, dup_ref, dw_ref, below):
        step = pl.program_id(1)
        first_tile = step == last

        @pl.when(step == 0)
        def _():
            below[...] = jnp.zeros_like(below)
            dw_ref[...] = jnp.zeros_like(dw_ref)

        def strip(cur, prev, rows, cs, nxt):
            shifted = [_shift_down(cur, prev, taps - 1 - j) for j in range(taps)]
            conv = w_ref[0:1, cs] * shifted[0]
            for j in range(1, taps):
                conv += w_ref[j:j + 1, cs] * shifted[j]
            y, dsilu = _silu_parts(conv)
            d = d_ref[rows, cs]
            dc = d * g_ref[rows, cs] * dsilu
            dx = w_ref[taps - 1:taps, cs] * dc
            for j in range(taps - 1):
                dx += w_ref[j:j + 1, cs] * _shift_up(dc, nxt, taps - 1 - j)
            return dx, d * y, dc, [dc * s for s in shifted]

        def pair(r0, above_of):
            top, bot = _pair_rows(r0)
            both = pl.ds(r0, PAIR)
            for c0 in range(0, c_w, LANES):
                cs = slice(c0, c0 + LANES)
                cur_t, cur_b = u_ref[top, cs], u_ref[bot, cs]
                dx_b, dg_b, dc_b, dw_b = strip(cur_b, cur_t, bot, cs, below[:, cs])
                dx_t, dg_t, dc_t, dw_t = strip(cur_t, above_of(cs), top, cs, dc_b)
                below[:, cs] = dc_t
                dup_ref[0, both, cs] = jnp.concatenate([dx_t, dx_b], axis=0).astype(dup_ref.dtype)
                dup_ref[1, both, cs] = jnp.concatenate([dg_t, dg_b], axis=0).astype(dup_ref.dtype)
                for j in range(taps):
                    dw_ref[j, :, cs] += dw_t[j] + dw_b[j]

        def loop_body(it, carry):
            r0 = pl.multiple_of((n_pairs - 1 - it) * PAIR, PAIR)
            pair(r0, lambda cs: u_ref[pl.ds(pl.multiple_of(r0 - SUB, SUB), SUB), cs])
            return carry

        lax.fori_loop(0, n_pairs - 1, loop_body, 0, unroll=STRIP_UNROLL)
        pair(0, lambda cs: jnp.where(first_tile, 0.0, halo_ref[:, cs]))

    return pl.pallas_call(
        body,
        name=name,
        grid=(2, lp // tm),
        in_specs=[
            pl.BlockSpec((None, tm, c_w), lambda s, i: (s, last - i, 0)),
            pl.BlockSpec((None, HALO, c_w), lambda s, i: (s, _halo_index(last - i, tm), 0)),
            pl.BlockSpec((None, tm, c_w), lambda s, i: (2 + s, last - i, 0)),
            pl.BlockSpec((None, tm, c_w), lambda s, i: (s, last - i, 0)),
            pl.BlockSpec((None, taps, c_w), lambda s, i: (s, 0, 0)),
        ],
        out_specs=[
            pl.BlockSpec((2, None, tm, c_w), lambda s, i: (0, s, last - i, 0)),
            pl.BlockSpec((None, taps, SUB, c_w), lambda s, i: (s, 0, 0, 0)),
        ],
        out_shape=[jax.ShapeDtypeStruct((2, 2, lp, c_w), BF16), jax.ShapeDtypeStruct((2, taps, SUB, c_w), F32)],
        scratch_shapes=[pltpu.VMEM((SUB, c_w), F32)],
        compiler_params=_params(("arbitrary", "arbitrary")),
    )(up, up, up, dact, conv_w)


def sc_fwd(pb, conv_w, *, tm, cb):
    _, lp, width = pb.shape
    taps = conv_w.shape[0]

    def body(x_ref, halo_ref, w_ref, o_ref):
        first_tile = pl.program_id(1) == 0

        def strip(cur, prev, rows, cs):
            conv = w_ref[taps - 1:taps, cs] * cur
            for j in range(taps - 1):
                conv += w_ref[j:j + 1, cs] * _shift_down(cur, prev, taps - 1 - j)
            return x_ref[0, rows, cs] * conv

        def pair(r0, above_of):
            top, bot = _pair_rows(r0)
            for c0 in range(0, cb, LANES):
                cs = slice(c0, c0 + LANES)
                cur_t = x_ref[1, top, cs] * x_ref[2, top, cs]
                cur_b = x_ref[1, bot, cs] * x_ref[2, bot, cs]
                out = [strip(cur_t, above_of(cs), top, cs), strip(cur_b, cur_t, bot, cs)]
                o_ref[pl.ds(r0, PAIR), cs] = jnp.concatenate(out, axis=0).astype(o_ref.dtype)

        pair(0, lambda cs: jnp.where(first_tile, 0.0, halo_ref[1, :, cs] * halo_ref[2, :, cs]))

        def loop_body(k, carry):
            r0 = pl.multiple_of(k * PAIR, PAIR)
            before = pl.ds(pl.multiple_of(r0 - SUB, SUB), SUB)
            pair(r0, lambda cs: x_ref[1, before, cs] * x_ref[2, before, cs])
            return carry

        lax.fori_loop(1, tm // PAIR, loop_body, 0, unroll=STRIP_UNROLL)

    return pl.pallas_call(
        body,
        name="sc_fwd",
        grid=(width // cb, lp // tm),
        in_specs=[
            pl.BlockSpec((3, tm, cb), lambda j, i: (0, i, j)),
            pl.BlockSpec((3, HALO, cb), lambda j, i: (0, _halo_index(i, tm), j)),
            pl.BlockSpec((taps, cb), lambda j, i: (0, j)),
        ],
        out_specs=pl.BlockSpec((None, tm, cb), lambda j, i: (0, i, j)),
        out_shape=jax.ShapeDtypeStruct((1, lp, width), BF16),
        compiler_params=_params(("arbitrary", "arbitrary")),
    )(pb, pb, conv_w)


def sc_bwd(pb, ds, conv_w, *, tm, cb):
    _, lp, width = pb.shape
    taps = conv_w.shape[0]
    last = lp // tm - 1
    n_pairs = tm // PAIR

    def body(x_ref, halo_ref, d_ref, w_ref, dx_ref, dw_ref, below):
        step = pl.program_id(1)
        first_tile = step == last

        @pl.when(step == 0)
        def _():
            below[...] = jnp.zeros_like(below)
            dw_ref[...] = jnp.zeros_like(dw_ref)

        def strip(cur, prev, rows, cs, nxt):
            gate, left, right = x_ref[0, rows, cs], x_ref[1, rows, cs], x_ref[2, rows, cs]
            shifted = [_shift_down(cur, prev, taps - 1 - j) for j in range(taps)]
            conv = w_ref[0:1, cs] * shifted[0]
            for j in range(1, taps):
                conv += w_ref[j:j + 1, cs] * shifted[j]
            d = d_ref[rows, cs]
            dc = d * gate
            dp = w_ref[taps - 1:taps, cs] * dc
            for j in range(taps - 1):
                dp += w_ref[j:j + 1, cs] * _shift_up(dc, nxt, taps - 1 - j)
            return d * conv, dp * right, dp * left, dc, [dc * s for s in shifted]

        def pair(r0, above_of):
            top, bot = _pair_rows(r0)
            both = pl.ds(r0, PAIR)
            for c0 in range(0, cb, LANES):
                cs = slice(c0, c0 + LANES)
                cur_t = x_ref[1, top, cs] * x_ref[2, top, cs]
                cur_b = x_ref[1, bot, cs] * x_ref[2, bot, cs]
                *dx_b, dc_b, dw_b = strip(cur_b, cur_t, bot, cs, below[:, cs])
                *dx_t, dc_t, dw_t = strip(cur_t, above_of(cs), top, cs, dc_b)
                below[:, cs] = dc_t
                for s in range(3):
                    dx_ref[s, both, cs] = jnp.concatenate([dx_t[s], dx_b[s]], axis=0).astype(dx_ref.dtype)
                for j in range(taps):
                    dw_ref[j, :, cs] += dw_t[j] + dw_b[j]

        def loop_body(it, carry):
            r0 = pl.multiple_of((n_pairs - 1 - it) * PAIR, PAIR)
            before = pl.ds(pl.multiple_of(r0 - SUB, SUB), SUB)
            pair(r0, lambda cs: x_ref[1, before, cs] * x_ref[2, before, cs])
            return carry

        lax.fori_loop(0, n_pairs - 1, loop_body, 0, unroll=STRIP_UNROLL)
        pair(0, lambda cs: jnp.where(first_tile, 0.0, halo_ref[1, :, cs] * halo_ref[2, :, cs]))

    tile_spec = pl.BlockSpec((3, tm, cb), lambda j, i: (0, last - i, j))
    return pl.pallas_call(
        body,
        name="sc_bwd",
        grid=(width // cb, lp // tm),
        in_specs=[
            tile_spec,
            pl.BlockSpec((3, HALO, cb), lambda j, i: (0, _halo_index(last - i, tm), j)),
            pl.BlockSpec((None, tm, cb), lambda j, i: (0, last - i, j)),
            pl.BlockSpec((taps, cb), lambda j, i: (0, j)),
        ],
        out_specs=[tile_spec, pl.BlockSpec((taps, SUB, cb), lambda j, i: (0, 0, j))],
        out_shape=[jax.ShapeDtypeStruct((3, lp, width), BF16), jax.ShapeDtypeStruct((taps, SUB, width), F32)],
        scratch_shapes=[pltpu.VMEM((SUB, cb), F32)],
        compiler_params=_params(("arbitrary", "arbitrary")),
    )(pb, pb, ds, conv_w)


TILE_BYTES = 1536 * 1024


def _rows_tile(rows, cols, multiple=8):
    if rows * cols * 4 <= TILE_BYTES or rows % multiple:
        return rows
    best = multiple
    for t in range(multiple, rows + 1, multiple):
        if rows % t == 0 and t * cols * 4 <= TILE_BYTES:
            best = t
    return best


def pair_sum(g, landed, core, out_dtype, name):
    _, rows, cols = g.shape
    half = rows // 2
    tr = _rows_tile(half, cols, 16)
    nb = half // tr

    def body(c_ref, g_ref, l_ref, o_ref):
        o_ref[...] = (g_ref[...] + l_ref[...]).astype(out_dtype)

    return pl.pallas_call(
        body,
        name=name,
        grid_spec=pltpu.PrefetchScalarGridSpec(
            num_scalar_prefetch=1,
            grid=(4, nb),
            in_specs=[
                pl.BlockSpec((None, tr, cols), lambda s, i, c: (s, c[0] * nb + i, 0)),
                pl.BlockSpec((None, tr, cols), lambda s, i, c: (s, i, 0)),
            ],
            out_specs=pl.BlockSpec((None, tr, cols), lambda s, i, c: (s, i, 0)),
        ),
        out_shape=jax.ShapeDtypeStruct((4, half, cols), out_dtype),
        compiler_params=_params(("arbitrary", "arbitrary")),
    )(core, g, landed)


def chip_sum(x, name):
    _, rows, cols = x.shape
    tr = _rows_tile(rows, cols, 16)

    def body(x0, x1, x2, x3, o_ref):
        acc = x0[...].astype(F32) + x1[...].astype(F32)
        o_ref[...] = (acc + x2[...].astype(F32)) + x3[...].astype(F32)

    return pl.pallas_call(
        body,
        name=name,
        grid=(rows // tr,),
        in_specs=[pl.BlockSpec((None, tr, cols), lambda i, k=k: (k, i, 0)) for k in range(4)],
        out_specs=pl.BlockSpec((tr, cols), lambda i: (i, 0)),
        out_shape=jax.ShapeDtypeStruct((rows, cols), F32),
        compiler_params=_params(("arbitrary",)),
    )(x, x, x, x)


def adamw(w, g, m, v, name):
    shape = w.shape
    cols = shape[-1]
    rows = w.size // cols
    tr = _rows_tile(rows, cols)

    def body(w_ref, g_ref, m_ref, v_ref, d_ref, m2_ref, v2_ref):
        gv = g_ref[...]
        m2 = ADAM_B1 * m_ref[...] + (1.0 - ADAM_B1) * gv
        v2 = ADAM_B2 * v_ref[...] + (1.0 - ADAM_B2) * (gv * gv)
        m_hat = m2 / (1.0 - ADAM_B1 ** ADAM_STEP)
        v_hat = v2 / (1.0 - ADAM_B2 ** ADAM_STEP)
        d_ref[...] = -ADAM_LR * (m_hat / (jnp.sqrt(v_hat) + ADAM_EPS) + ADAM_WD * w_ref[...])
        m2_ref[...] = m2
        v2_ref[...] = v2

    spec = pl.BlockSpec((tr, cols), lambda i: (i, 0))
    outs = pl.pallas_call(
        body,
        name=name,
        grid=(rows // tr,),
        in_specs=[spec] * 4,
        out_specs=[spec] * 3,
        out_shape=[jax.ShapeDtypeStruct((rows, cols), F32)] * 3,
        compiler_params=_params(("arbitrary",)),
    )(*[t.reshape(rows, cols) for t in (w, g, m, v)])
    return tuple(o.reshape(shape) for o in outs)


MESH_ID = pl.DeviceIdType.MESH
ANY = pl.BlockSpec(memory_space=pl.ANY)


def _place():
    x, y, c = lax.axis_index("x"), lax.axis_index("y"), lax.axis_index("c")
    other_chips = [(1 - x, y), (x, 1 - y), (1 - x, 1 - y)]
    return x, y, c, other_chips


def all_gather_shards(bufs, name):
    n = len(bufs)

    def body(*refs):
        x_refs, o_refs = refs[:n], refs[n:2 * n]
        copies = _gather_copies(x_refs, o_refs, *refs[2 * n:])
        _gather_start(copies)
        _gather_finish(copies)

    outs = pl.pallas_call(
        body,
        name=name,
        in_specs=[ANY] * n,
        out_specs=[ANY] * n,
        out_shape=_gather_out_shapes(bufs),
        scratch_shapes=_gather_sems(n),
    )(*bufs)
    return _set_own_slots(outs, bufs)


def _gather_out_shapes(bufs):
    return [jax.ShapeDtypeStruct((4,) + b.shape, b.dtype) for b in bufs]


def _gather_sems(n):
    return [pltpu.SemaphoreType.DMA((6 * n,)), pltpu.SemaphoreType.DMA((6 * n,))]


def _set_own_slots(outs, bufs):
    if not outs:
        return []
    me = 2 * lax.axis_index("x") + lax.axis_index("y")
    return [lax.dynamic_update_index_in_dim(o, b, me, 0) for o, b in zip(outs, bufs)]


def _gather_copies(x_refs, o_refs, send_sems, recv_sems):
    x, y, c, chips = _place()
    me = 2 * x + y
    sibling = (x, y, 1 - c)

    def part(a, slot, hf):
        half = x_refs[a].shape[0] // 2
        return o_refs[a].at[slot, pl.ds(hf * half, half), :]

    def mine(a):
        half = x_refs[a].shape[0] // 2
        return x_refs[a].at[pl.ds(c * half, half), :]

    def copy(k, src, dst, to):
        return pltpu.make_async_remote_copy(src_ref=src, dst_ref=dst, send_sem=send_sems.at[k],
                                            recv_sem=recv_sems.at[k], device_id=to, device_id_type=MESH_ID)

    sends, arrivals, passes, passed = [], [], [], []
    for a in range(len(x_refs)):
        for j, (px, py) in enumerate(chips):
            landed, theirs = part(a, 2 * px + py, c), part(a, 2 * px + py, 1 - c)
            sends.append(copy(6 * a + j, mine(a), part(a, me, c), (px, py, c)))
            arrivals.append(copy(6 * a + j, mine(a), landed, (px, py, c)))
            passes.append(copy(6 * a + 3 + j, landed, landed, sibling))
            passed.append(copy(6 * a + 3 + j, theirs, theirs, sibling))
    return sends, arrivals, passes, passed


def _gather_start(copies):
    for cp in copies[0]:
        cp.start()


def _gather_finish(copies):
    sends, arrivals, passes, passed = copies
    for arrival, cp in zip(arrivals, passes):
        arrival.wait_recv()
        cp.start()
    for cp in passed:
        cp.wait_recv()
    for cp in sends + passes:
        cp.wait_send()


def swap_halves(bufs, name):
    n = len(bufs)

    def body(*refs):
        x_refs, o_refs = refs[:n], refs[n:2 * n]
        send_sems, recv_sems = refs[2 * n:]
        x, y, c, _ = _place()
        copies = []
        for a in range(n):
            half = bufs[a].shape[1] // 2
            cp = pltpu.make_async_remote_copy(src_ref=x_refs[a].at[:, pl.ds((1 - c) * half, half), :], dst_ref=o_refs[a],
                                              send_sem=send_sems.at[a], recv_sem=recv_sems.at[a],
                                              device_id=(x, y, 1 - c), device_id_type=MESH_ID)
            cp.start()
            copies.append(cp)
        for cp in copies:
            cp.wait()

    return pl.pallas_call(
        body,
        name=name,
        in_specs=[ANY] * n,
        out_specs=[ANY] * n,
        out_shape=[jax.ShapeDtypeStruct((4, b.shape[1] // 2, b.shape[2]), b.dtype) for b in bufs],
        scratch_shapes=[pltpu.SemaphoreType.DMA((n,)), pltpu.SemaphoreType.DMA((n,))],
    )(*bufs)


def scatter_to_chips(bufs, name):
    n = len(bufs)

    def body(*refs):
        x_refs, o_refs = refs[:n], refs[n:2 * n]
        copies = _scatter_copies(x_refs, o_refs, *refs[2 * n:])
        _scatter_start(copies)
        _scatter_finish(copies)

    outs = pl.pallas_call(
        body,
        name=name,
        in_specs=[ANY] * n,
        out_specs=[ANY] * n,
        out_shape=[jax.ShapeDtypeStruct(b.shape, b.dtype) for b in bufs],
        scratch_shapes=_scatter_sems(n),
    )(*bufs)
    return _keep_own_slots(outs, bufs)


def _scatter_sems(n):
    return [pltpu.SemaphoreType.DMA((3 * n,)), pltpu.SemaphoreType.DMA((3 * n,))]


def _keep_own_slots(outs, bufs):
    if not outs:
        return []
    me = 2 * lax.axis_index("x") + lax.axis_index("y")
    return [lax.dynamic_update_index_in_dim(o, lax.dynamic_index_in_dim(b, me, 0, keepdims=False), me, 0)
            for o, b in zip(outs, bufs)]


def _scatter_copies(x_refs, o_refs, send_sems, recv_sems):
    x, y, c, chips = _place()
    me = 2 * x + y

    def copy(a, j, src_slot, dst_slot, px, py):
        return pltpu.make_async_remote_copy(src_ref=x_refs[a].at[src_slot], dst_ref=o_refs[a].at[dst_slot],
                                            send_sem=send_sems.at[3 * a + j], recv_sem=recv_sems.at[3 * a + j],
                                            device_id=(px, py, c), device_id_type=MESH_ID)

    sends = [copy(a, j, 2 * px + py, me, px, py) for a in range(len(x_refs)) for j, (px, py) in enumerate(chips)]
    arrivals = [copy(a, j, me, 2 * px + py, px, py) for a in range(len(x_refs)) for j, (px, py) in enumerate(chips)]
    return sends, arrivals


def _scatter_start(copies):
    for cp in copies[0]:
        cp.start()


def _scatter_finish(copies):
    for cp in copies[1]:
        cp.wait_recv()
    for cp in copies[0]:
        cp.wait_send()


def share_halves(groups, name):
    bufs = [b for grp in groups for b in grp]
    where = [(gi, li) for gi, grp in enumerate(groups) for li in range(len(grp))]
    n = len(bufs)

    def body(*refs):
        x_refs, o_refs = refs[:n], refs[n:n + len(groups)]
        send_sems, recv_sems = refs[n + len(groups):]
        x, y, c, _ = _place()
        sent, arrive = [], []
        for a, (gi, li) in enumerate(where):

            def copy(hf, a=a, gi=gi, li=li):
                return pltpu.make_async_remote_copy(src_ref=x_refs[a], dst_ref=o_refs[gi].at[li, hf],
                                                    send_sem=send_sems.at[a], recv_sem=recv_sems.at[a],
                                                    device_id=(x, y, 1 - c), device_id_type=MESH_ID)

            sent.append(copy(c))
            arrive.append(copy(1 - c))
        for cp in sent:
            cp.start()
        for cp in arrive:
            cp.wait_recv()
        for cp in sent:
            cp.wait_send()

    outs = pl.pallas_call(
        body,
        name=name,
        in_specs=[ANY] * n,
        out_specs=[ANY] * len(groups),
        out_shape=[jax.ShapeDtypeStruct((len(grp), 2) + grp[0].shape, grp[0].dtype) for grp in groups],
        scratch_shapes=[pltpu.SemaphoreType.DMA((n,)), pltpu.SemaphoreType.DMA((n,))],
    )(*bufs)
    c = lax.axis_index("c")
    full = [lax.dynamic_update_index_in_dim(o, jnp.stack(grp), c, 1) for o, grp in zip(outs, groups)]
    return [t.reshape(t.shape[0], 2 * t.shape[2], t.shape[3]) for t in full]


def pair_sums(bufs, dtypes, tag):
    core = lax.axis_index("c").astype(jnp.int32).reshape(1)
    landed = swap_halves(bufs, "rs_pair_" + tag)
    return [pair_sum(b, l, core, dt, "rs_pair_sum_%s%d" % (tag, i)) for i, (b, l, dt) in enumerate(zip(bufs, landed, dtypes))]


def _row_tiles(length):
    return (640, 320) if length > 2048 else (128, 64)


def _divisor_tile(rows, target):
    return max(t for t in range(8, min(rows, target) + 1, 8) if rows % t == 0)


def _local_step(x, target, wt, late_shards, layout_late, reduce_early):
    seq, d = x.shape
    length = N_META + seq
    tm, tm_ffn = _row_tiles(length)
    lp = -(-length // tm) * tm
    tail = jnp.zeros((lp - length, d), F32)
    h0 = jnp.concatenate([wt["meta"], x, tail], axis=0)[None]
    tgt = jnp.concatenate([jnp.zeros((N_META, d), F32), target, tail], axis=0)
    nn = functools.partial(mm_nn, tm=_divisor_tile(lp, 1664))
    nt = functools.partial(mm_nt, tm=_divisor_tile(lp, 1040))
    tn = functools.partial(mm_tn, tm=_divisor_tile(lp, 832))
    ln_g = [wt["ln_mix_g"][0:1], wt["ln_ffn_g"][0:1], wt["ln_mix_g"][1:2], wt["ln_ffn_g"][1:2]]
    ln_b = [wt["ln_mix_b"][0:1], wt["ln_ffn_b"][0:1], wt["ln_mix_b"][1:2], wt["ln_ffn_b"][1:2]]

    p5 = nn(h0, wt["a5"], name="a_in5")
    pz = nn(h0, wt["az"], name="a_inz")
    qkvbg = gdn_pre_fwd(p5, wt["a_conv3"], wt["alog_b"], wt["dtb_b"], tm=tm, cb=2 * HEAD_DIM)
    o, states, tinv, late_stacks = gdn_chunk_fwd(qkvbg, late_shards)
    wt = {**wt, **layout_late(late_stacks)}
    onz = gdn_post_fwd(o[None], pz, wt["anorm_b"], tm=tm)
    r1, h1 = ln_fwd(h0, nn(onz, wt["a_out"], name="a_out"), ln_g[0], ln_b[0], tm=tm, name="ln1")
    up0 = nn(h1, wt["up"][0], name="up0")
    act0 = ffn_act_fwd(up0, wt["fconv"][0], tm=tm_ffn, name="ffn_act0")
    r2, h2 = ln_fwd(h1, nn(act0, wt["down"][0], name="down0"), ln_g[1], ln_b[1], tm=tm, name="ln2")
    pb = nn(h2, wt["b_in"], name="b_in")
    sc = sc_fwd(pb, wt["b_conv"], tm=tm_ffn, cb=d)
    r3, h3 = ln_fwd(h2, nn(sc, wt["b_out"], name="b_out"), ln_g[2], ln_b[2], tm=tm, name="ln3")
    up1 = nn(h3, wt["up"][1], name="up1")
    act1 = ffn_act_fwd(up1, wt["fconv"][1], tm=tm_ffn, name="ffn_act1")
    r4, h4 = ln_fwd(h3, nn(act1, wt["down"][1], name="down1"), ln_g[3], ln_b[3], tm=tm, name="ln4")

    dh4, loss_part = loss_grad(h4, tgt, first=N_META, count=seq, tm=tm)

    grads = {}
    dr4, dgb4 = ln_bwd(r4, dh4, ln_g[3], tm=tm, name="ln4_bwd")
    d_down1 = tn(act1, dr4, name="d_down1")
    dact1 = nt(dr4, wt["down"][1], name="d_act1")
    dup1, dfconv1 = ffn_act_bwd(up1, dact1, wt["fconv"][1], tm=tm_ffn, name="ffn_act1_bwd")
    dup1 = dup1.reshape(up1.shape)
    d_up1 = tn(h3, dup1, name="d_up1")
    dh3 = nt(dup1, wt["up"][1], res=dr4, res_scale=ALPHA, name="d_h3")

    dr3, dgb3 = ln_bwd(r3, dh3, ln_g[2], tm=tm, name="ln3_bwd")
    d_bout = tn(sc, dr3, name="d_b_out")
    dsc = nt(dr3, wt["b_out"], name="d_sc")
    dpb, dbconv = sc_bwd(pb, dsc, wt["b_conv"], tm=tm_ffn, cb=d)
    d_bin = tn(h2, dpb, name="d_b_in")
    dh2 = nt(dpb, wt["b_in"], res=dr3, res_scale=ALPHA, name="d_h2")

    dr2, dgb2 = ln_bwd(r2, dh2, ln_g[1], tm=tm, name="ln2_bwd")
    d_down0 = tn(act0, dr2, name="d_down0")
    dact0 = nt(dr2, wt["down"][0], name="d_act0")
    dup0, dfconv0 = ffn_act_bwd(up0, dact0, wt["fconv"][0], tm=tm_ffn, name="ffn_act0_bwd")
    dup0 = dup0.reshape(up0.shape)
    d_up0 = tn(h1, dup0, name="d_up0")
    dh1 = nt(dup0, wt["up"][0], res=dr2, res_scale=ALPHA, name="d_h1")
    grads["b_w_in"] = [d_bin[0].transpose(1, 0, 2).reshape(d, 4, 3 * d // 4).transpose(1, 0, 2)]
    grads["b_w_out"] = [d_bout.reshape(4, d // 4, d)]
    grads["ffn_w_up"] = [d_up0[0], d_up1[0]]
    grads["ffn_w_down"] = [t.reshape(4, -1, d) for t in (d_down0, d_down1)]
    leaving = reduce_early(grads)

    dr1, dgb1 = ln_bwd(r1, dh1, ln_g[0], tm=tm, name="ln1_bwd")
    d_aout = tn(onz, dr1, name="d_a_out")
    donz = nt(dr1, wt["a_out"], name="d_onz")
    d_o, dz, dnw = gdn_post_bwd(o[None], pz, donz, wt["anorm_b"], tm=tm)
    dqkvbg, landed = gdn_chunk_bwd(qkvbg, states, tinv, d_o[0], leaving)
    dp5, daconv, dscal = gdn_pre_bwd(p5, dqkvbg, wt["a_conv3"], wt["alog_b"], wt["dtb_b"], tm=tm, cb=2 * HEAD_DIM)
    d_a5 = tn(h0, dp5, name="d_a_in5")
    d_az = tn(h0, dz, name="d_a_inz")
    dh0 = nt(dp5, wt["a5"], res=dr1, res_scale=ALPHA, name="d_h0a")
    dh0 = nt(dz, wt["az"], res=dh0, res_scale=1.0, name="d_h0")

    width = HEADS * HEAD_DIM
    d_ba = head_lane_sum(d_a5[0, 3:5])[:, :, :HEADS]
    d_a_in = jnp.concatenate([d_a5[0, 0], d_a5[0, 1], d_a5[0, 2], d_az[0, 0], d_ba[0], d_ba[1]], axis=1)
    n_in = d_a_in.shape[1] // 4
    grads["a_w_in"] = [d_a_in.reshape(d, 4, n_in).transpose(1, 0, 2)]
    grads["a_w_out"] = [d_aout.reshape(4, width // 4, d)]
    grads["a_conv"] = daconv.sum(axis=2).transpose(1, 0, 2).reshape(1, GDN_CONV, 3 * width)
    per_head = dscal.reshape(2, 8, HEADS, HEAD_DIM).sum(axis=(1, 3))
    grads["a_log"] = per_head[0][None]
    grads["a_dt_bias"] = per_head[1][None]
    grads["a_norm"] = dnw.reshape(8, HEADS, HEAD_DIM).sum(axis=(0, 1))[None]
    grads["b_conv"] = dbconv.sum(axis=1)[None]
    lns = [dgb1, dgb2, dgb3, dgb4]
    grads["ln_mix_g"] = jnp.stack([lns[0][0].sum(0), lns[2][0].sum(0)])
    grads["ln_mix_b"] = jnp.stack([lns[0][1].sum(0), lns[2][1].sum(0)])
    grads["ln_ffn_g"] = jnp.stack([lns[1][0].sum(0), lns[3][0].sum(0)])
    grads["ln_ffn_b"] = jnp.stack([lns[1][1].sum(0), lns[3][1].sum(0)])
    grads["ffn_conv"] = jnp.stack([t.sum(axis=2).transpose(1, 0, 2).reshape(FFN_CONV, -1) for t in (dfconv0, dfconv1)])
    grads["meta"] = dh0[0, :N_META]
    return loss_part, dh0, grads, landed


WEIGHTS = ["meta", "a_w_in", "a_conv", "a_log", "a_dt_bias", "a_norm", "a_w_out", "b_w_in", "b_conv", "b_w_out",
           "ln_mix_g", "ln_mix_b", "ffn_w_up", "ffn_conv", "ffn_w_down", "ln_ffn_g", "ln_ffn_b"]
EARLY_WEIGHTS = ["a_w_in", "a_w_out"]
LATE_WEIGHTS = ["b_w_in", "b_w_out", "ffn_w_up", "ffn_w_down"]
MATMUL_WEIGHTS = EARLY_WEIGHTS + LATE_WEIGHTS
SMALL_SHARDED = ["a_conv", "b_conv", "ffn_conv", "meta"]
REPLICATED = ["a_log", "a_dt_bias", "a_norm", "ln_mix_g", "ln_mix_b", "ln_ffn_g", "ln_ffn_b"]
SHARD_AXIS = {"meta": 1, "a_w_in": 2, "a_conv": 2, "a_w_out": 1, "b_w_in": 2, "b_conv": 2, "b_w_out": 1,
              "ffn_w_up": 2, "ffn_conv": 2, "ffn_w_down": 1}
PACK_COLS = 1024
PACK_ROWS_MULTIPLE = 32


def _pack(pieces, lead=()):
    flat = jnp.concatenate([p.reshape(lead + (-1,)) for p in pieces], axis=-1)
    n = flat.shape[-1]
    rows = -(-n // (PACK_COLS * PACK_ROWS_MULTIPLE)) * PACK_ROWS_MULTIPLE
    flat = jnp.pad(flat, [(0, 0)] * len(lead) + [(0, rows * PACK_COLS - n)])
    return flat.reshape(lead + (rows, PACK_COLS))


def _unpack(buf, shapes, lead=()):
    flat = buf.reshape(lead + (-1,))
    out, off = [], 0
    for shp in shapes:
        n = 1
        for s in shp:
            n *= s
        out.append(flat[..., off:off + n].reshape(lead + tuple(shp)))
        off += n
    return out


def _join_shards(stacked, axis):
    return jnp.concatenate([stacked[k] for k in range(4)], axis=axis)


def _split_shards(full, axis):
    return jnp.stack(jnp.split(full, 4, axis=axis))


def _weight_layers(w, names):
    return [w[n][l].astype(BF16) for n in names for l in range(w[n].shape[0])]


def _per_weight(arrays, w, names):
    it = iter(arrays)
    return {n: [next(it) for _ in range(w[n].shape[0])] for n in names}


def _layout_early(full, w):
    width = HEADS * HEAD_DIM
    wt = {n: w[n] for n in ("ln_mix_g", "ln_mix_b", "ln_ffn_g", "ln_ffn_b")}
    w_in = _join_shards(full["a_w_in"][0], 1)
    d = w_in.shape[0]
    n_ff = full["ffn_conv"].shape[2] // 2
    blocks = [w_in[:, s * width:(s + 1) * width] for s in range(4)]
    b_exp = jnp.repeat(w_in[:, 4 * width:4 * width + HEADS], HEAD_DIM, axis=1)
    a_exp = jnp.repeat(w_in[:, 4 * width + HEADS:], HEAD_DIM, axis=1)
    wt["a5"] = jnp.stack([blocks[0], blocks[1], blocks[2], b_exp, a_exp])[None]
    wt["az"] = blocks[3][None, None]
    wt["a_out"] = full["a_w_out"][0].reshape(1, 1, width, d)
    wt["a_conv3"] = full["a_conv"][0].reshape(GDN_CONV, 3, width).transpose(1, 0, 2)
    wt["b_conv"] = full["b_conv"][0]
    wt["fconv"] = [full["ffn_conv"][l].reshape(FFN_CONV, 2, n_ff).transpose(1, 0, 2) for l in range(2)]
    wt["meta"] = full["meta"]
    wt["alog_b"] = jnp.repeat(w["a_log"][0], HEAD_DIM)[None]
    wt["dtb_b"] = jnp.repeat(w["a_dt_bias"][0], HEAD_DIM)[None]
    wt["anorm_b"] = jnp.tile(w["a_norm"][0], HEADS)[None]
    return wt


def _layout_late(full):
    d = full["b_w_in"][0].shape[1]
    n_ff = full["ffn_w_up"][0].shape[2]
    return {
        "b_in": _join_shards(full["b_w_in"][0], 1).reshape(d, 3, d).transpose(1, 0, 2)[None],
        "b_out": full["b_w_out"][0].reshape(1, 1, d, d),
        "up": [t[None] for t in full["ffn_w_up"]],
        "down": [t.reshape(2, 1, n_ff, d) for t in full["ffn_w_down"]],
    }


def kernel(x, meta, a_w_in, a_conv, a_log, a_dt_bias, a_norm, a_w_out, b_w_in, b_conv, b_w_out, ln_mix_g, ln_mix_b, ffn_w_up, ffn_conv, ffn_w_down, ln_ffn_g, ln_ffn_b, loss_target, m_meta, m_a_w_in, m_a_conv, m_a_log, m_a_dt_bias, m_a_norm, m_a_w_out, m_b_w_in, m_b_conv, m_b_w_out, m_ln_mix_g, m_ln_mix_b, m_ffn_w_up, m_ffn_conv, m_ffn_w_down, m_ln_ffn_g, m_ln_ffn_b, v_meta, v_a_w_in, v_a_conv, v_a_log, v_a_dt_bias, v_a_norm, v_a_w_out, v_b_w_in, v_b_conv, v_b_w_out, v_ln_mix_g, v_ln_mix_b, v_ffn_w_up, v_ffn_conv, v_ffn_w_down, v_ln_ffn_g, v_ln_ffn_b):
    w = dict(meta=meta, a_w_in=a_w_in, a_conv=a_conv, a_log=a_log, a_dt_bias=a_dt_bias, a_norm=a_norm, a_w_out=a_w_out,
             b_w_in=b_w_in, b_conv=b_conv, b_w_out=b_w_out, ln_mix_g=ln_mix_g, ln_mix_b=ln_mix_b, ffn_w_up=ffn_w_up,
             ffn_conv=ffn_conv, ffn_w_down=ffn_w_down, ln_ffn_g=ln_ffn_g, ln_ffn_b=ln_ffn_b)
    m = dict(meta=m_meta, a_w_in=m_a_w_in, a_conv=m_a_conv, a_log=m_a_log, a_dt_bias=m_a_dt_bias, a_norm=m_a_norm,
             a_w_out=m_a_w_out, b_w_in=m_b_w_in, b_conv=m_b_conv, b_w_out=m_b_w_out, ln_mix_g=m_ln_mix_g,
             ln_mix_b=m_ln_mix_b, ffn_w_up=m_ffn_w_up, ffn_conv=m_ffn_conv, ffn_w_down=m_ffn_w_down,
             ln_ffn_g=m_ln_ffn_g, ln_ffn_b=m_ln_ffn_b)
    v = dict(meta=v_meta, a_w_in=v_a_w_in, a_conv=v_a_conv, a_log=v_a_log, a_dt_bias=v_a_dt_bias, a_norm=v_a_norm,
             a_w_out=v_a_w_out, b_w_in=v_b_w_in, b_conv=v_b_conv, b_w_out=v_b_w_out, ln_mix_g=v_ln_mix_g,
             ln_mix_b=v_ln_mix_b, ffn_w_up=v_ffn_w_up, ffn_conv=v_ffn_conv, ffn_w_down=v_ffn_w_down,
             ln_ffn_g=v_ln_ffn_g, ln_ffn_b=v_ln_ffn_b)
    seq = x.shape[1]
    *stacks, small = all_gather_shards(_weight_layers(w, EARLY_WEIGHTS) + [_pack([w[n] for n in SMALL_SHARDED])],
                                       "gather_early")
    full = _per_weight(stacks, w, EARLY_WEIGHTS)
    for n, t in zip(SMALL_SHARDED, _unpack(small, [w[n].shape for n in SMALL_SHARDED], lead=(4,))):
        full[n] = _join_shards(t, SHARD_AXIS[n])

    def layout_late(late_stacks):
        return _layout_late(_per_weight(late_stacks, w, LATE_WEIGHTS))

    def reduce_early(grads):
        bufs = [g for n in LATE_WEIGHTS for g in grads[n]]
        return pair_sums(bufs, [BF16] * len(bufs), "late")

    loss_part, dh0, grads, landed_late = _local_step(x[0], loss_target[0], _layout_early(full, w),
                                                     _weight_layers(w, LATE_WEIGHTS), layout_late, reduce_early)
    pieces = [_split_shards(grads[n], SHARD_AXIS[n]) for n in SMALL_SHARDED]
    same = jnp.concatenate([grads[n].reshape(-1) for n in REPLICATED] + [jnp.sum(loss_part).reshape(1)])
    pieces.append(jnp.broadcast_to(same, (4,) + same.shape))
    bufs = [g for n in EARLY_WEIGHTS for g in grads[n]] + [_pack(pieces, lead=(4,))]
    landed = scatter_to_chips(pair_sums(bufs, [BF16] * (len(bufs) - 1) + [F32], "early"), "rs_chips_early")
    totals = [chip_sum(t, "rs_chip_sum%d" % i) for i, t in enumerate(landed + landed_late)]
    by_weight = _per_weight(totals[:len(bufs) - 1] + totals[len(bufs):], w, MATMUL_WEIGHTS)
    *shared, small_total = share_halves([by_weight[n] for n in MATMUL_WEIGHTS] + [[totals[len(bufs) - 1]]], "rs_share")
    grad_w = {n: t.reshape(w[n].shape) for n, t in zip(MATMUL_WEIGHTS, shared)}
    rest = SMALL_SHARDED + REPLICATED
    unpacked = _unpack(small_total[0], [w[n].shape for n in rest] + [()])
    grad_w.update(zip(rest, unpacked[:-1]))
    loss = unpacked[-1]
    grad_x = dh0[:, N_META:N_META + seq]
    steps = [adamw(w[n], grad_w[n], m[n], v[n], "adamw_" + n) for n in WEIGHTS]
    return (loss, grad_x, *[grad_w[n] for n in WEIGHTS], *[s[0] for s in steps], *[s[1] for s in steps],
            *[s[2] for s in steps])
```

```python
import functools

import jax
import jax.numpy as jnp
from jax import lax
from jax.experimental import pallas as pl
from jax.experimental.pallas import tpu as pltpu

F32 = jnp.float32
BF16 = jnp.bfloat16
HI = lax.Precision.HIGHEST

N_META = 16
HEADS = 8
HEAD_DIM = 128
CHUNK = 64
GDN_CONV = 4
SC_CONV = 3
FFN_CONV = 3
ALPHA = 4.0 ** 0.25
LN_EPS = 1e-5
RMS_EPS = 1e-6
L2_EPS = 1e-6
Q_SCALE = HEAD_DIM ** -0.5

ADAM_LR = 0.001
ADAM_B1 = 0.9
ADAM_B2 = 0.999
ADAM_EPS = 1e-08
ADAM_WD = 0.01
ADAM_STEP = 10

HALO = 8
VMEM_LIMIT = 48 * 1024 * 1024


def _params(sem=None):
    return pltpu.CompilerParams(dimension_semantics=sem, vmem_limit_bytes=VMEM_LIMIT)


def _dot(a, b, prec=None):
    return jnp.dot(a, b, preferred_element_type=F32, precision=prec)


def _dot_nt(a, b, prec=None):
    return lax.dot_general(a, b, (((1,), (1,)), ((), ())), preferred_element_type=F32, precision=prec)


def _dot_tn(a, b, prec=None):
    return lax.dot_general(a, b, (((0,), (0,)), ((), ())), preferred_element_type=F32, precision=prec)


def _sigmoid(x):
    return 1.0 / (1.0 + jnp.exp(-x))


def _tri_masks():
    r = lax.broadcasted_iota(jnp.int32, (CHUNK, CHUNK), 0)
    c = lax.broadcasted_iota(jnp.int32, (CHUNK, CHUNK), 1)
    return r >= c, r > c, r == c


def _split_hi_lo(x):
    hi = x.astype(BF16)
    return hi, (x - hi.astype(F32)).astype(BF16)


def _mask_dot(mask, x):
    hi, lo = _split_hi_lo(x)
    return _dot(mask, hi) + _dot(mask, lo)


@jax.custom_vjp
def _cumsum_rows(g):
    causal, _, _ = _tri_masks()
    return _mask_dot(causal.astype(BF16), g)


def _cumsum_rows_fwd(g):
    return _cumsum_rows(g), None


def _cumsum_rows_bwd(_, dy):
    _, strict, _ = _tri_masks()
    return (_mask_dot((~strict).astype(BF16), dy),)


_cumsum_rows.defvjp(_cumsum_rows_fwd, _cumsum_rows_bwd)


def _dot_split3(a, b):
    a_hi, a_lo = _split_hi_lo(a)
    b_hi, b_lo = _split_hi_lo(b)
    return _dot(a_hi, b_hi) + (_dot(a_hi, b_lo) + _dot(a_lo, b_hi))


def _gdn_m(ks, g64s, bbs):
    causal, strict, _ = _tri_masks()
    a = [_cumsum_rows(g) for g in g64s]
    decay = [jnp.exp(jnp.where(causal, x - x.T, -1e30)) for x in a]
    kk = [_dot_nt(k * b, k) for k, b in zip(ks, bbs)]
    return [jnp.where(strict, x * d, 0.0) for x, d in zip(kk, decay)]


def _gdn_inverse(ms):
    r = lax.broadcasted_iota(jnp.int32, (CHUNK, CHUNK), 0)
    c = lax.broadcasted_iota(jnp.int32, (CHUNK, CHUNK), 1)
    eye = (r == c).astype(F32)
    same = [jnp.right_shift(r, s) == jnp.right_shift(c, s) for s in (3, 4, 5)]
    d = [jnp.where(same[0], m, 0.0) for m in ms]
    p = [_dot(x, x) for x in d]
    t = [eye - x for x in d]
    t = [x + _dot(x, y) for x, y in zip(t, p)]
    p = [_dot(x, x) for x in p]
    t = [x + _dot(x, y) for x, y in zip(t, p)]
    for inner, outer in ((same[0], same[1]), (same[1], same[2]), (same[2], None)):
        joins = ~inner if outer is None else (outer & ~inner)
        o = [_dot(x, jnp.where(joins, m, 0.0)) for x, m in zip(t, ms)]
        t = [x - _dot(y, x) for x, y in zip(t, o)]
    res = [eye - x - _dot_split3(m, x) for m, x in zip(ms, t)]
    return [x + _dot(x, y) for x, y in zip(t, res)]


def _gdn_apply(qs, ks, vs, gbs, g64s, bbs, ss, ts):
    causal, _, _ = _tri_masks()
    n = range(len(qs))
    gc = [_cumsum_rows(g) for g in gbs]
    a = [_cumsum_rows(g) for g in g64s]
    decay = [jnp.exp(jnp.where(causal, x - x.T, -1e30)) for x in a]
    eg = [jnp.exp(x) for x in gc]
    u = [_dot(ts[h], vs[h] * bbs[h]) for h in n]
    w = [_dot(ts[h], ks[h] * bbs[h] * eg[h]) for h in n]
    qk = [_dot_nt(qs[h], ks[h]) * decay[h] for h in n]
    gl = [jnp.sum(g, axis=0, keepdims=True) for g in gbs]
    kd = [ks[h] * jnp.exp(gl[h] - gc[h]) for h in n]
    v_new = [u[h] - _dot(w[h], ss[h]) for h in n]
    o = [_dot(qs[h] * eg[h], ss[h]) + _dot(qk[h], v_new[h]) for h in n]
    s2 = [ss[h] * jnp.exp(gl[h]) + _dot_tn(kd[h], v_new[h]) for h in n]
    return o, s2


def _head_slices(h):
    return slice(h * HEAD_DIM, (h + 1) * HEAD_DIM), slice(h * HEAD_DIM, h * HEAD_DIM + CHUNK)


def _gdn_head_values(x_ref):
    out = [[], [], [], [], [], []]
    for h in range(HEADS):
        sl, sl64 = _head_slices(h)
        for lst, val in zip(out, (x_ref[0, :, sl], x_ref[1, :, sl], x_ref[2, :, sl], x_ref[4, :, sl],
                                  x_ref[4, :, sl64], x_ref[3, :, sl])):
            lst.append(val)
    return out


def gdn_chunk_fwd(qkvbg, gather=()):
    _, lp, width = qkvbg.shape
    n_chunks = lp // CHUNK
    n = len(gather)

    def body(x_ref, *refs):
        shard_refs, (o_ref, s_ref, t_ref), refs = refs[:n], refs[n:n + 3], refs[n + 3:]
        stack_refs, state, sems = refs[:n], refs[n], refs[n + 1:]
        copies = _gather_copies(shard_refs, stack_refs, *sems) if n else None

        @pl.when(pl.program_id(0) == 0)
        def _():
            state[...] = jnp.zeros_like(state)
            if n:
                _gather_start(copies)

        qs, ks, vs, gbs, g64s, bbs = _gdn_head_values(x_ref)
        ss = [state[h] for h in range(HEADS)]
        ts = _gdn_inverse(_gdn_m(ks, g64s, bbs))
        os_, s2 = _gdn_apply(qs, ks, vs, gbs, g64s, bbs, ss, ts)
        for h in range(HEADS):
            s_ref[0, h] = ss[h]
            t_ref[0, h] = ts[h]
            o_ref[:, _head_slices(h)[0]] = os_[h]
            state[h] = s2[h]

        if n:
            @pl.when(pl.program_id(0) == n_chunks - 1)
            def _():
                _gather_finish(copies)

    o, states, tinv, *stacks = pl.pallas_call(
        body,
        name="gdn_chunk_fwd",
        grid=(n_chunks,),
        in_specs=[pl.BlockSpec((5, CHUNK, width), lambda c: (0, c, 0))] + [ANY] * n,
        out_specs=[
            pl.BlockSpec((CHUNK, width), lambda c: (c, 0)),
            pl.BlockSpec((1, HEADS, HEAD_DIM, HEAD_DIM), lambda c: (c, 0, 0, 0)),
            pl.BlockSpec((1, HEADS, CHUNK, CHUNK), lambda c: (c, 0, 0, 0)),
        ] + [ANY] * n,
        out_shape=[
            jax.ShapeDtypeStruct((lp, width), F32),
            jax.ShapeDtypeStruct((n_chunks, HEADS, HEAD_DIM, HEAD_DIM), F32),
            jax.ShapeDtypeStruct((n_chunks, HEADS, CHUNK, CHUNK), F32),
        ] + _gather_out_shapes(gather),
        scratch_shapes=[pltpu.VMEM((HEADS, HEAD_DIM, HEAD_DIM), F32)] + (_gather_sems(n) if n else []),
        compiler_params=_params(("arbitrary",)),
    )(qkvbg, *gather)
    return o, states, tinv, _set_own_slots(stacks, gather)


def gdn_chunk_bwd(qkvbg, states, tinv, d_o, scatter=()):
    _, lp, width = qkvbg.shape
    n_chunks = lp // CHUNK
    last = n_chunks - 1
    n = len(scatter)

    def body(x_ref, s_ref, t_ref, do_ref, *refs):
        leaving_refs, dx_ref, refs = refs[:n], refs[n], refs[n + 1:]
        landing_refs, dstate, sems = refs[:n], refs[n], refs[n + 1:]
        copies = _scatter_copies(leaving_refs, landing_refs, *sems) if n else None

        @pl.when(pl.program_id(0) == 0)
        def _():
            dstate[...] = jnp.zeros_like(dstate)
            if n:
                _scatter_start(copies)

        heads = range(HEADS)
        qs, ks, vs, gbs, g64s, bbs = _gdn_head_values(x_ref)
        ss = [s_ref[0, h] for h in heads]
        ts = [t_ref[0, h] for h in heads]
        d_out = ([do_ref[:, _head_slices(h)[0]] for h in heads], [dstate[h] for h in heads])
        _, vjp_apply = jax.vjp(_gdn_apply, qs, ks, vs, gbs, g64s, bbs, ss, ts)
        dq, dk, dv, dgb, dg64, dbb, ds, dt = vjp_apply(d_out)
        tts = [t.T for t in ts]
        dm = [_dot(tts[h], dt[h]) for h in heads]
        dm = [-_dot(dm[h], tts[h]) for h in heads]
        _, vjp_m = jax.vjp(_gdn_m, ks, g64s, bbs)
        dk2, dg64m, dbb2 = vjp_m(dm)
        for h in heads:
            sl, sl64 = _head_slices(h)
            dx_ref[0, :, sl] = dq[h]
            dx_ref[1, :, sl] = dk[h] + dk2[h]
            dx_ref[2, :, sl] = dv[h]
            dx_ref[3, :, sl] = dbb[h] + dbb2[h]
            dx_ref[4, :, sl] = dgb[h]
            dx_ref[4, :, sl64] += dg64[h] + dg64m[h]
            dstate[h] = ds[h]

        if n:
            @pl.when(pl.program_id(0) == n_chunks - 1)
            def _():
                _scatter_finish(copies)

    dqkvbg, *landed = pl.pallas_call(
        body,
        name="gdn_chunk_bwd",
        grid=(n_chunks,),
        in_specs=[
            pl.BlockSpec((5, CHUNK, width), lambda c: (0, last - c, 0)),
            pl.BlockSpec((1, HEADS, HEAD_DIM, HEAD_DIM), lambda c: (last - c, 0, 0, 0)),
            pl.BlockSpec((1, HEADS, CHUNK, CHUNK), lambda c: (last - c, 0, 0, 0)),
            pl.BlockSpec((CHUNK, width), lambda c: (last - c, 0)),
        ] + [ANY] * n,
        out_specs=[pl.BlockSpec((5, CHUNK, width), lambda c: (0, last - c, 0))] + [ANY] * n,
        out_shape=[jax.ShapeDtypeStruct(qkvbg.shape, F32)] + [jax.ShapeDtypeStruct(b.shape, b.dtype) for b in scatter],
        scratch_shapes=[pltpu.VMEM((HEADS, HEAD_DIM, HEAD_DIM), F32)] + (_scatter_sems(n) if n else []),
        compiler_params=_params(("arbitrary",)),
    )(qkvbg, states, tinv, d_o, *scatter)
    return dqkvbg, _keep_own_slots(landed, scatter)


def mm_nn(a, b, *, tm, name):
    ks, m, tk = a.shape
    _, ns, _, tn = b.shape

    def body(a_ref, b_ref, o_ref):
        p = _dot(a_ref[...].astype(BF16), b_ref[...])

        @pl.when(pl.program_id(2) == 0)
        def _():
            o_ref[...] = p

        @pl.when(pl.program_id(2) > 0)
        def _():
            o_ref[...] += p

    return pl.pallas_call(
        body,
        name=name,
        grid=(ns, m // tm, ks),
        in_specs=[
            pl.BlockSpec((None, tm, tk), lambda n, i, k: (k, i, 0)),
            pl.BlockSpec((None, None, tk, tn), lambda n, i, k: (k, n, 0, 0)),
        ],
        out_specs=pl.BlockSpec((None, tm, tn), lambda n, i, k: (n, i, 0)),
        out_shape=jax.ShapeDtypeStruct((ns, m, tn), F32),
        compiler_params=_params(("arbitrary", "arbitrary", "arbitrary")),
    )(a, b)


def mm_nt(dy, w, *, tm, name, res=None, res_scale=1.0):
    ns, m, tn = dy.shape
    ks, _, tk, _ = w.shape

    def body(*refs):
        if res is None:
            dy_ref, w_ref, o_ref = refs
        else:
            dy_ref, w_ref, r_ref, o_ref = refs
        p = _dot_nt(dy_ref[...].astype(BF16), w_ref[...])

        @pl.when(pl.program_id(2) == 0)
        def _():
            o_ref[...] = p if res is None else p + res_scale * r_ref[...]

        @pl.when(pl.program_id(2) > 0)
        def _():
            o_ref[...] += p

    in_specs = [
        pl.BlockSpec((None, tm, tn), lambda k, i, n: (n, i, 0)),
        pl.BlockSpec((None, None, tk, tn), lambda k, i, n: (k, n, 0, 0)),
    ]
    args = [dy, w]
    if res is not None:
        in_specs.append(pl.BlockSpec((None, tm, tk), lambda k, i, n: (k, i, 0)))
        args.append(res)
    return pl.pallas_call(
        body,
        name=name,
        grid=(ks, m // tm, ns),
        in_specs=in_specs,
        out_specs=pl.BlockSpec((None, tm, tk), lambda k, i, n: (k, i, 0)),
        out_shape=jax.ShapeDtypeStruct((ks, m, tk), F32),
        compiler_params=_params(("arbitrary", "arbitrary", "arbitrary")),
    )(*args)


def mm_tn(x, dy, *, tm, name, rb=None):
    ks, m, tk = x.shape
    ns, _, tn = dy.shape
    rb = tk if rb is None else rb

    def body(x_ref, dy_ref, o_ref):
        @pl.when(pl.program_id(2) == 0)
        def _():
            o_ref[...] = jnp.zeros_like(o_ref)

        dyb = dy_ref[...].astype(BF16)
        for r in range(0, tk, rb):
            o_ref[r:r + rb, :] += _dot_tn(x_ref[:, r:r + rb].astype(BF16), dyb)

    return pl.pallas_call(
        body,
        name=name,
        grid=(ks, ns, m // tm),
        in_specs=[
            pl.BlockSpec((None, tm, tk), lambda k, n, i: (k, i, 0)),
            pl.BlockSpec((None, tm, tn), lambda k, n, i: (n, i, 0)),
        ],
        out_specs=pl.BlockSpec((None, None, tk, tn), lambda k, n, i: (k, n, 0, 0)),
        out_shape=jax.ShapeDtypeStruct((ks, ns, tk, tn), F32),
        compiler_params=_params(("arbitrary", "arbitrary", "arbitrary")),
    )(x, dy)


def _row_partial(x):
    rows, c = x.shape
    return jnp.sum(x.reshape(rows // 8, 8, c), axis=0)


def ln_fwd(h_prev, mix, g, b, *, tm, name):
    _, lp, d = h_prev.shape

    def body(h_ref, m_ref, g_ref, b_ref, r_ref, o_ref):
        r = ALPHA * h_ref[...] + m_ref[...]
        mu = jnp.mean(r, axis=-1, keepdims=True)
        xc = r - mu
        var = jnp.mean(xc * xc, axis=-1, keepdims=True)
        r_ref[...] = r
        o_ref[...] = xc * lax.rsqrt(var + LN_EPS) * g_ref[...] + b_ref[...]

    row = pl.BlockSpec((None, tm, d), lambda i: (0, i, 0))
    vec = pl.BlockSpec((1, d), lambda i: (0, 0))
    return pl.pallas_call(
        body,
        name=name,
        grid=(lp // tm,),
        in_specs=[row, row, vec, vec],
        out_specs=[row, row],
        out_shape=[jax.ShapeDtypeStruct((1, lp, d), F32)] * 2,
        compiler_params=_params(("arbitrary",)),
    )(h_prev, mix, g, b)


def ln_bwd(r, dh, g, *, tm, name):
    _, lp, d = r.shape

    def body(r_ref, dh_ref, g_ref, dr_ref, dgb_ref):
        x = r_ref[...]
        dh_v = dh_ref[...]
        mu = jnp.mean(x, axis=-1, keepdims=True)
        xc = x - mu
        rstd = lax.rsqrt(jnp.mean(xc * xc, axis=-1, keepdims=True) + LN_EPS)
        xh = xc * rstd
        dxh = dh_v * g_ref[...]
        m1 = jnp.mean(dxh, axis=-1, keepdims=True)
        m2 = jnp.mean(dxh * xh, axis=-1, keepdims=True)
        dr_ref[...] = rstd * (dxh - m1 - xh * m2)

        @pl.when(pl.program_id(0) == 0)
        def _():
            dgb_ref[...] = jnp.zeros_like(dgb_ref)

        dgb_ref[0] += _row_partial(dh_v * xh)
        dgb_ref[1] += _row_partial(dh_v)

    row = pl.BlockSpec((None, tm, d), lambda i: (0, i, 0))
    return pl.pallas_call(
        body,
        name=name,
        grid=(lp // tm,),
        in_specs=[row, row, pl.BlockSpec((1, d), lambda i: (0, 0))],
        out_specs=[row, pl.BlockSpec((2, 8, d), lambda i: (0, 0, 0))],
        out_shape=[jax.ShapeDtypeStruct((1, lp, d), F32), jax.ShapeDtypeStruct((2, 8, d), F32)],
        compiler_params=_params(("arbitrary",)),
    )(r, dh, g)


def loss_grad(h, target, *, first, count, tm):
    _, lp, d = h.shape

    def body(h_ref, t_ref, dh_ref, l_ref):
        row = pl.program_id(0) * tm + lax.broadcasted_iota(jnp.int32, (tm, d), 0)
        valid = (row >= first) & (row < first + count)
        err = jnp.where(valid, h_ref[...] - t_ref[...], 0.0)
        dh_ref[...] = err * (1.0 / d)

        @pl.when(pl.program_id(0) == 0)
        def _():
            l_ref[...] = jnp.zeros_like(l_ref)

        l_ref[...] += _row_partial(err * err) * (0.5 / d)

    return pl.pallas_call(
        body,
        name="loss_grad",
        grid=(lp // tm,),
        in_specs=[pl.BlockSpec((None, tm, d), lambda i: (0, i, 0)), pl.BlockSpec((tm, d), lambda i: (i, 0))],
        out_specs=[pl.BlockSpec((None, tm, d), lambda i: (0, i, 0)), pl.BlockSpec((8, d), lambda i: (0, 0))],
        out_shape=[jax.ShapeDtypeStruct((1, lp, d), F32), jax.ShapeDtypeStruct((8, d), F32)],
        compiler_params=_params(("arbitrary",)),
    )(h, target)


def _halo_index(tile, tm):
    return jnp.maximum(tile * (tm // HALO) - 1, 0)


def _conv_fwd(xs_ref, w, taps, tm):
    acc = w(0) * xs_ref[pl.ds(HALO - taps + 1, tm), :]
    for j in range(1, taps):
        acc += w(j) * xs_ref[pl.ds(HALO - taps + 1 + j, tm), :]
    return acc


def _conv_bwd_x(dcs_ref, w, taps, tm):
    acc = w(0) * dcs_ref[pl.ds(taps - 1, tm), :]
    for j in range(1, taps):
        acc += w(j) * dcs_ref[pl.ds(taps - 1 - j, tm), :]
    return acc


SUB = 8
LANES = 128
PAIR = 2 * SUB
STRIP_UNROLL = 2


def _pair_rows(r0):
    return pl.ds(r0, SUB), pl.ds(r0 + SUB if isinstance(r0, int) else pl.multiple_of(r0 + SUB, SUB), SUB)


def _shift_down(cur, prev, s):
    if s == 0:
        return cur
    row = lax.broadcasted_iota(jnp.int32, cur.shape, 0)
    return jnp.where(row < s, pltpu.roll(prev, s, axis=0), pltpu.roll(cur, s, axis=0))


def _shift_up(cur, nxt, s):
    if s == 0:
        return cur
    row = lax.broadcasted_iota(jnp.int32, cur.shape, 0)
    return jnp.where(row < SUB - s, pltpu.roll(cur, SUB - s, axis=0), pltpu.roll(nxt, SUB - s, axis=0))


def _silu_parts(c):
    sg = _sigmoid(c)
    return c * sg, sg * (1.0 + c * (1.0 - sg))


def _head_sum(x):
    rows, c = x.shape
    parts = []
    for h in range(c // HEAD_DIM):
        s = jnp.sum(x[:, h * HEAD_DIM:(h + 1) * HEAD_DIM], axis=-1, keepdims=True)
        parts.append(jnp.broadcast_to(s, (rows, HEAD_DIM)))
    return parts[0] if len(parts) == 1 else jnp.concatenate(parts, axis=-1)


def _log1p(y):
    u = 1.0 + y
    d = u - 1.0
    return jnp.where(d == 0.0, y, jnp.log(u) * (y / jnp.where(d == 0.0, 1.0, d)))


def _softplus(x):
    return jnp.maximum(x, 0.0) + _log1p(jnp.exp(-jnp.abs(x)))


def gdn_pre_fwd(p5, conv_w, alog_b, dtb_b, *, tm, cb):
    _, lp, width = p5.shape
    taps = conv_w.shape[1]

    def body(x_ref, halo_ref, w_ref, al_ref, dt_ref, o_ref, xs):
        i = pl.program_id(1)
        for s in range(3):
            xs[s, 0:HALO, :] = jnp.where(i > 0, halo_ref[s], 0.0)
            xs[s, HALO:, :] = x_ref[s]
            c = _conv_fwd(xs.at[s], lambda j, s=s: w_ref[s, j:j + 1, :], taps, tm)
            y, _ = _silu_parts(c)
            if s < 2:
                y = y * lax.rsqrt(_head_sum(y * y) + L2_EPS)
                if s == 0:
                    y = y * Q_SCALE
            o_ref[s] = y
        o_ref[3] = _sigmoid(x_ref[3])
        o_ref[4] = -jnp.exp(al_ref[...]) * _softplus(x_ref[4] + dt_ref[...])

    return pl.pallas_call(
        body,
        name="gdn_pre_fwd",
        grid=(width // cb, lp // tm),
        in_specs=[
            pl.BlockSpec((5, tm, cb), lambda j, i: (0, i, j)),
            pl.BlockSpec((3, HALO, cb), lambda j, i: (0, _halo_index(i, tm), j)),
            pl.BlockSpec((3, taps, cb), lambda j, i: (0, 0, j)),
            pl.BlockSpec((1, cb), lambda j, i: (0, j)),
            pl.BlockSpec((1, cb), lambda j, i: (0, j)),
        ],
        out_specs=pl.BlockSpec((5, tm, cb), lambda j, i: (0, i, j)),
        out_shape=jax.ShapeDtypeStruct((5, lp, width), F32),
        scratch_shapes=[pltpu.VMEM((3, tm + HALO, cb), F32)],
        compiler_params=_params(("arbitrary", "arbitrary")),
    )(p5, p5, conv_w, alog_b, dtb_b)


def gdn_pre_bwd(p5, dqkvbg, conv_w, alog_b, dtb_b, *, tm, cb):
    _, lp, width = p5.shape
    taps = conv_w.shape[1]
    last = lp // tm - 1

    def body(x_ref, halo_ref, d_ref, w_ref, al_ref, dt_ref, dx_ref, dw_ref, dsc_ref, xs, dcs, carry):
        step = pl.program_id(1)
        tile = last - step

        @pl.when(step == 0)
        def _():
            carry[...] = jnp.zeros_like(carry)
            dw_ref[...] = jnp.zeros_like(dw_ref)
            dsc_ref[...] = jnp.zeros_like(dsc_ref)

        for s in range(3):
            w = lambda j, s=s: w_ref[s, j:j + 1, :]
            xs[s, 0:HALO, :] = jnp.where(tile > 0, halo_ref[s], 0.0)
            xs[s, HALO:, :] = x_ref[s]
            c = _conv_fwd(xs.at[s], w, taps, tm)
            y, dsilu = _silu_parts(c)
            dy = d_ref[s]
            if s < 2:
                rn = lax.rsqrt(_head_sum(y * y) + L2_EPS)
                yn = y * rn
                if s == 0:
                    dy = dy * Q_SCALE
                dy = rn * (dy - yn * _head_sum(dy * yn))
            dc = dy * dsilu
            dcs[s, 0:tm, :] = dc
            dcs[s, tm:, :] = carry[s]
            dx_ref[s] = _conv_bwd_x(dcs.at[s], w, taps, tm).astype(dx_ref.dtype)
            carry[s] = dc[0:HALO, :]
            for j in range(taps):
                dw_ref[s, j] += _row_partial(dc * xs[s, pl.ds(HALO - taps + 1 + j, tm), :])
        beta = _sigmoid(x_ref[3])
        dx_ref[3] = (d_ref[3] * beta * (1.0 - beta)).astype(dx_ref.dtype)
        z = x_ref[4] + dt_ref[...]
        dg = d_ref[4] * -jnp.exp(al_ref[...])
        da = dg * _sigmoid(z)
        dx_ref[4] = da.astype(dx_ref.dtype)
        dsc_ref[0] += _row_partial(dg * _softplus(z))
        dsc_ref[1] += _row_partial(da)

    tile_spec = pl.BlockSpec((5, tm, cb), lambda j, i: (0, last - i, j))
    return pl.pallas_call(
        body,
        name="gdn_pre_bwd",
        grid=(width // cb, lp // tm),
        in_specs=[
            tile_spec,
            pl.BlockSpec((3, HALO, cb), lambda j, i: (0, _halo_index(last - i, tm), j)),
            tile_spec,
            pl.BlockSpec((3, taps, cb), lambda j, i: (0, 0, j)),
            pl.BlockSpec((1, cb), lambda j, i: (0, j)),
            pl.BlockSpec((1, cb), lambda j, i: (0, j)),
        ],
        out_specs=[
            tile_spec,
            pl.BlockSpec((3, taps, SUB, cb), lambda j, i: (0, 0, 0, j)),
            pl.BlockSpec((2, SUB, cb), lambda j, i: (0, 0, j)),
        ],
        out_shape=[
            jax.ShapeDtypeStruct((5, lp, width), BF16),
            jax.ShapeDtypeStruct((3, taps, SUB, width), F32),
            jax.ShapeDtypeStruct((2, SUB, width), F32),
        ],
        scratch_shapes=[
            pltpu.VMEM((3, tm + HALO, cb), F32),
            pltpu.VMEM((3, tm + HALO, cb), F32),
            pltpu.VMEM((3, HALO, cb), F32),
        ],
        compiler_params=_params(("arbitrary", "arbitrary")),
    )(p5, p5, dqkvbg, conv_w, alog_b, dtb_b)


def gdn_post_fwd(o, z, nw_b, *, tm):
    _, lp, width = o.shape

    def body(o_ref, z_ref, nw_ref, y_ref):
        ov = o_ref[...]
        rn = lax.rsqrt(_head_sum(ov * ov) * (1.0 / HEAD_DIM) + RMS_EPS)
        gate, _ = _silu_parts(z_ref[...])
        y_ref[...] = (ov * rn * nw_ref[...] * gate).astype(y_ref.dtype)

    row = pl.BlockSpec((None, tm, width), lambda i: (0, i, 0))
    return pl.pallas_call(
        body,
        name="gdn_post_fwd",
        grid=(lp // tm,),
        in_specs=[row, row, pl.BlockSpec((1, width), lambda i: (0, 0))],
        out_specs=row,
        out_shape=jax.ShapeDtypeStruct((1, lp, width), BF16),
        compiler_params=_params(("arbitrary",)),
    )(o, z, nw_b)


def gdn_post_bwd(o, z, dy, nw_b, *, tm):
    _, lp, width = o.shape

    def body(o_ref, z_ref, dy_ref, nw_ref, do_ref, dz_ref, dnw_ref):
        ov = o_ref[...]
        rn = lax.rsqrt(_head_sum(ov * ov) * (1.0 / HEAD_DIM) + RMS_EPS)
        yn = ov * rn
        gate, dgate = _silu_parts(z_ref[...])
        d_on = dy_ref[...] * gate
        dz_ref[...] = (dy_ref[...] * yn * nw_ref[...] * dgate).astype(dz_ref.dtype)
        a = d_on * nw_ref[...]
        do_ref[...] = rn * (a - yn * (_head_sum(a * yn) * (1.0 / HEAD_DIM)))

        @pl.when(pl.program_id(0) == 0)
        def _():
            dnw_ref[...] = jnp.zeros_like(dnw_ref)

        dnw_ref[...] += _row_partial(d_on * yn)

    row = pl.BlockSpec((None, tm, width), lambda i: (0, i, 0))
    return pl.pallas_call(
        body,
        name="gdn_post_bwd",
        grid=(lp // tm,),
        in_specs=[row, row, row, pl.BlockSpec((1, width), lambda i: (0, 0))],
        out_specs=[row, row, pl.BlockSpec((8, width), lambda i: (0, 0))],
        out_shape=[jax.ShapeDtypeStruct((1, lp, width), F32), jax.ShapeDtypeStruct((1, lp, width), BF16),
                   jax.ShapeDtypeStruct((8, width), F32)],
        compiler_params=_params(("arbitrary",)),
    )(o, z, dy, nw_b)


def head_lane_sum(x):
    s_n, rows, width = x.shape

    def body(x_ref, o_ref):
        lane = lax.broadcasted_iota(jnp.int32, (rows, HEAD_DIM), 1)
        acc = jnp.zeros((rows, HEAD_DIM), F32)
        for h in range(width // HEAD_DIM):
            s = jnp.sum(x_ref[:, h * HEAD_DIM:(h + 1) * HEAD_DIM], axis=-1, keepdims=True)
            acc = jnp.where(lane == h, s, acc)
        o_ref[...] = acc

    return pl.pallas_call(
        body,
        name="head_lane_sum",
        grid=(s_n,),
        in_specs=[pl.BlockSpec((None, rows, width), lambda s: (s, 0, 0))],
        out_specs=pl.BlockSpec((None, rows, HEAD_DIM), lambda s: (s, 0, 0)),
        out_shape=jax.ShapeDtypeStruct((s_n, rows, HEAD_DIM), F32),
        compiler_params=_params(("arbitrary",)),
    )(x)


def ffn_act_fwd(up, conv_w, *, tm, name):
    _, lp, c_w = up.shape
    taps = conv_w.shape[1]

    def body(u_ref, halo_ref, g_ref, w_ref, o_ref):
        first_tile = pl.program_id(1) == 0

        def strip(cur, prev, rows, cs):
            conv = w_ref[taps - 1:taps, cs] * cur
            for j in range(taps - 1):
                conv += w_ref[j:j + 1, cs] * _shift_down(cur, prev, taps - 1 - j)
            y, _ = _silu_parts(conv)
            return y * g_ref[rows, cs]

        def pair(r0, above_of):
            top, bot = _pair_rows(r0)
            for c0 in range(0, c_w, LANES):
                cs = slice(c0, c0 + LANES)
                cur_t, cur_b = u_ref[top, cs], u_ref[bot, cs]
                out = [strip(cur_t, above_of(cs), top, cs), strip(cur_b, cur_t, bot, cs)]
                o_ref[pl.ds(r0, PAIR), cs] = jnp.concatenate(out, axis=0).astype(o_ref.dtype)

        pair(0, lambda cs: jnp.where(first_tile, 0.0, halo_ref[:, cs]))

        def loop_body(s, carry):
            r0 = pl.multiple_of(s * PAIR, PAIR)
            pair(r0, lambda cs: u_ref[pl.ds(pl.multiple_of(r0 - SUB, SUB), SUB), cs])
            return carry

        lax.fori_loop(1, tm // PAIR, loop_body, 0, unroll=STRIP_UNROLL)

    return pl.pallas_call(
        body,
        name=name,
        grid=(2, lp // tm),
        in_specs=[
            pl.BlockSpec((None, tm, c_w), lambda s, i: (s, i, 0)),
            pl.BlockSpec((None, HALO, c_w), lambda s, i: (s, _halo_index(i, tm), 0)),
            pl.BlockSpec((None, tm, c_w), lambda s, i: (2 + s, i, 0)),
            pl.BlockSpec((None, taps, c_w), lambda s, i: (s, 0, 0)),
        ],
        out_specs=pl.BlockSpec((None, tm, c_w), lambda s, i: (s, i, 0)),
        out_shape=jax.ShapeDtypeStruct((2, lp, c_w), BF16),
        compiler_params=_params(("arbitrary", "arbitrary")),
    )(up, up, up, conv_w)


def ffn_act_bwd(up, dact, conv_w, *, tm, name):
    _, lp, c_w = up.shape
    taps = conv_w.shape[1]
    last = lp // tm - 1
    n_pairs = tm // PAIR

    def body(u_ref, halo_ref, g_ref, d_ref, w_ref, dup_ref, dw_ref, below):
        step = pl.program_id(1)
        first_tile = step == last

        @pl.when(step == 0)
        def _():
            below[...] = jnp.zeros_like(below)
            dw_ref[...] = jnp.zeros_like(dw_ref)

        def strip(cur, prev, rows, cs, nxt):
            shifted = [_shift_down(cur, prev, taps - 1 - j) for j in range(taps)]
            conv = w_ref[0:1, cs] * shifted[0]
            for j in range(1, taps):
                conv += w_ref[j:j + 1, cs] * shifted[j]
            y, dsilu = _silu_parts(conv)
            d = d_ref[rows, cs]
            dc = d * g_ref[rows, cs] * dsilu
            dx = w_ref[taps - 1:taps, cs] * dc
            for j in range(taps - 1):
                dx += w_ref[j:j + 1, cs] * _shift_up(dc, nxt, taps - 1 - j)
            return dx, d * y, dc, [dc * s for s in shifted]

        def pair(r0, above_of):
            top, bot = _pair_rows(r0)
            both = pl.ds(r0, PAIR)
            for c0 in range(0, c_w, LANES):
                cs = slice(c0, c0 + LANES)
                cur_t, cur_b = u_ref[top, cs], u_ref[bot, cs]
                dx_b, dg_b, dc_b, dw_b = strip(cur_b, cur_t, bot, cs, below[:, cs])
                dx_t, dg_t, dc_t, dw_t = strip(cur_t, above_of(cs), top, cs, dc_b)
                below[:, cs] = dc_t
                dup_ref[0, both, cs] = jnp.concatenate([dx_t, dx_b], axis=0).astype(dup_ref.dtype)
                dup_ref[1, both, cs] = jnp.concatenate([dg_t, dg_b], axis=0).astype(dup_ref.dtype)
                for j in range(taps):
                    dw_ref[j, :, cs] += dw_t[j] + dw_b[j]

        def loop_body(it, carry):
            r0 = pl.multiple_of((n_pairs - 1 - it) * PAIR, PAIR)
            pair(r0, lambda cs: u_ref[pl.ds(pl.multiple_of(r0 - SUB, SUB), SUB), cs])
            return carry

        lax.fori_loop(0, n_pairs - 1, loop_body, 0, unroll=STRIP_UNROLL)
        pair(0, lambda cs: jnp.where(first_tile, 0.0, halo_ref[:, cs]))

    return pl.pallas_call(
        body,
        name=name,
        grid=(2, lp // tm),
        in_specs=[
            pl.BlockSpec((None, tm, c_w), lambda s, i: (s, last - i, 0)),
            pl.BlockSpec((None, HALO, c_w), lambda s, i: (s, _halo_index(last - i, tm), 0)),
            pl.BlockSpec((None, tm, c_w), lambda s, i: (2 + s, last - i, 0)),
            pl.BlockSpec((None, tm, c_w), lambda s, i: (s, last - i, 0)),
            pl.BlockSpec((None, taps, c_w), lambda s, i: (s, 0, 0)),
        ],
        out_specs=[
            pl.BlockSpec((2, None, tm, c_w), lambda s, i: (0, s, last - i, 0)),
            pl.BlockSpec((None, taps, SUB, c_w), lambda s, i: (s, 0, 0, 0)),
        ],
        out_shape=[jax.ShapeDtypeStruct((2, 2, lp, c_w), BF16), jax.ShapeDtypeStruct((2, taps, SUB, c_w), F32)],
        scratch_shapes=[pltpu.VMEM((SUB, c_w), F32)],
        compiler_params=_params(("arbitrary", "arbitrary")),
    )(up, up, up, dact, conv_w)


def sc_fwd(pb, conv_w, *, tm, cb):
    _, lp, width = pb.shape
    taps = conv_w.shape[0]

    def body(x_ref, halo_ref, w_ref, o_ref):
        first_tile = pl.program_id(1) == 0

        def strip(cur, prev, rows, cs):
            conv = w_ref[taps - 1:taps, cs] * cur
            for j in range(taps - 1):
                conv += w_ref[j:j + 1, cs] * _shift_down(cur, prev, taps - 1 - j)
            return x_ref[0, rows, cs] * conv

        def pair(r0, above_of):
            top, bot = _pair_rows(r0)
            for c0 in range(0, cb, LANES):
                cs = slice(c0, c0 + LANES)
                cur_t = x_ref[1, top, cs] * x_ref[2, top, cs]
                cur_b = x_ref[1, bot, cs] * x_ref[2, bot, cs]
                out = [strip(cur_t, above_of(cs), top, cs), strip(cur_b, cur_t, bot, cs)]
                o_ref[pl.ds(r0, PAIR), cs] = jnp.concatenate(out, axis=0).astype(o_ref.dtype)

        pair(0, lambda cs: jnp.where(first_tile, 0.0, halo_ref[1, :, cs] * halo_ref[2, :, cs]))

        def loop_body(k, carry):
            r0 = pl.multiple_of(k * PAIR, PAIR)
            before = pl.ds(pl.multiple_of(r0 - SUB, SUB), SUB)
            pair(r0, lambda cs: x_ref[1, before, cs] * x_ref[2, before, cs])
            return carry

        lax.fori_loop(1, tm // PAIR, loop_body, 0, unroll=STRIP_UNROLL)

    return pl.pallas_call(
        body,
        name="sc_fwd",
        grid=(width // cb, lp // tm),
        in_specs=[
            pl.BlockSpec((3, tm, cb), lambda j, i: (0, i, j)),
            pl.BlockSpec((3, HALO, cb), lambda j, i: (0, _halo_index(i, tm), j)),
            pl.BlockSpec((taps, cb), lambda j, i: (0, j)),
        ],
        out_specs=pl.BlockSpec((None, tm, cb), lambda j, i: (0, i, j)),
        out_shape=jax.ShapeDtypeStruct((1, lp, width), BF16),
        compiler_params=_params(("arbitrary", "arbitrary")),
    )(pb, pb, conv_w)


def sc_bwd(pb, ds, conv_w, *, tm, cb):
    _, lp, width = pb.shape
    taps = conv_w.shape[0]
    last = lp // tm - 1
    n_pairs = tm // PAIR

    def body(x_ref, halo_ref, d_ref, w_ref, dx_ref, dw_ref, below):
        step = pl.program_id(1)
        first_tile = step == last

        @pl.when(step == 0)
        def _():
            below[...] = jnp.zeros_like(below)
            dw_ref[...] = jnp.zeros_like(dw_ref)

        def strip(cur, prev, rows, cs, nxt):
            gate, left, right = x_ref[0, rows, cs], x_ref[1, rows, cs], x_ref[2, rows, cs]
            shifted = [_shift_down(cur, prev, taps - 1 - j) for j in range(taps)]
            conv = w_ref[0:1, cs] * shifted[0]
            for j in range(1, taps):
                conv += w_ref[j:j + 1, cs] * shifted[j]
            d = d_ref[rows, cs]
            dc = d * gate
            dp = w_ref[taps - 1:taps, cs] * dc
            for j in range(taps - 1):
                dp += w_ref[j:j + 1, cs] * _shift_up(dc, nxt, taps - 1 - j)
            return d * conv, dp * right, dp * left, dc, [dc * s for s in shifted]

        def pair(r0, above_of):
            top, bot = _pair_rows(r0)
            both = pl.ds(r0, PAIR)
            for c0 in range(0, cb, LANES):
                cs = slice(c0, c0 + LANES)
                cur_t = x_ref[1, top, cs] * x_ref[2, top, cs]
                cur_b = x_ref[1, bot, cs] * x_ref[2, bot, cs]
                *dx_b, dc_b, dw_b = strip(cur_b, cur_t, bot, cs, below[:, cs])
                *dx_t, dc_t, dw_t = strip(cur_t, above_of(cs), top, cs, dc_b)
                below[:, cs] = dc_t
                for s in range(3):
                    dx_ref[s, both, cs] = jnp.concatenate([dx_t[s], dx_b[s]], axis=0).astype(dx_ref.dtype)
                for j in range(taps):
                    dw_ref[j, :, cs] += dw_t[j] + dw_b[j]

        def loop_body(it, carry):
            r0 = pl.multiple_of((n_pairs - 1 - it) * PAIR, PAIR)
            before = pl.ds(pl.multiple_of(r0 - SUB, SUB), SUB)
            pair(r0, lambda cs: x_ref[1, before, cs] * x_ref[2, before, cs])
            return carry

        lax.fori_loop(0, n_pairs - 1, loop_body, 0, unroll=STRIP_UNROLL)
        pair(0, lambda cs: jnp.where(first_tile, 0.0, halo_ref[1, :, cs] * halo_ref[2, :, cs]))

    tile_spec = pl.BlockSpec((3, tm, cb), lambda j, i: (0, last - i, j))
    return pl.pallas_call(
        body,
        name="sc_bwd",
        grid=(width // cb, lp // tm),
        in_specs=[
            tile_spec,
            pl.BlockSpec((3, HALO, cb), lambda j, i: (0, _halo_index(last - i, tm), j)),
            pl.BlockSpec((None, tm, cb), lambda j, i: (0, last - i, j)),
            pl.BlockSpec((taps, cb), lambda j, i: (0, j)),
        ],
        out_specs=[tile_spec, pl.BlockSpec((taps, SUB, cb), lambda j, i: (0, 0, j))],
        out_shape=[jax.ShapeDtypeStruct((3, lp, width), BF16), jax.ShapeDtypeStruct((taps, SUB, width), F32)],
        scratch_shapes=[pltpu.VMEM((SUB, cb), F32)],
        compiler_params=_params(("arbitrary", "arbitrary")),
    )(pb, pb, ds, conv_w)


TILE_BYTES = 1536 * 1024


def _rows_tile(rows, cols, multiple=8):
    if rows * cols * 4 <= TILE_BYTES or rows % multiple:
        return rows
    best = multiple
    for t in range(multiple, rows + 1, multiple):
        if rows % t == 0 and t * cols * 4 <= TILE_BYTES:
            best = t
    return best


def pair_sum(g, landed, core, out_dtype, name):
    _, rows, cols = g.shape
    half = rows // 2
    tr = _rows_tile(half, cols, 16)
    nb = half // tr

    def body(c_ref, g_ref, l_ref, o_ref):
        o_ref[...] = (g_ref[...] + l_ref[...]).astype(out_dtype)

    return pl.pallas_call(
        body,
        name=name,
        grid_spec=pltpu.PrefetchScalarGridSpec(
            num_scalar_prefetch=1,
            grid=(4, nb),
            in_specs=[
                pl.BlockSpec((None, tr, cols), lambda s, i, c: (s, c[0] * nb + i, 0)),
                pl.BlockSpec((None, tr, cols), lambda s, i, c: (s, i, 0)),
            ],
            out_specs=pl.BlockSpec((None, tr, cols), lambda s, i, c: (s, i, 0)),
        ),
        out_shape=jax.ShapeDtypeStruct((4, half, cols), out_dtype),
        compiler_params=_params(("arbitrary", "arbitrary")),
    )(core, g, landed)


def chip_sum(x, name):
    _, rows, cols = x.shape
    tr = _rows_tile(rows, cols, 16)

    def body(x0, x1, x2, x3, o_ref):
        acc = x0[...].astype(F32) + x1[...].astype(F32)
        o_ref[...] = (acc + x2[...].astype(F32)) + x3[...].astype(F32)

    return pl.pallas_call(
        body,
        name=name,
        grid=(rows // tr,),
        in_specs=[pl.BlockSpec((None, tr, cols), lambda i, k=k: (k, i, 0)) for k in range(4)],
        out_specs=pl.BlockSpec((tr, cols), lambda i: (i, 0)),
        out_shape=jax.ShapeDtypeStruct((rows, cols), F32),
        compiler_params=_params(("arbitrary",)),
    )(x, x, x, x)


def adamw(w, g, m, v, name):
    shape = w.shape
    cols = shape[-1]
    rows = w.size // cols
    tr = _rows_tile(rows, cols)

    def body(w_ref, g_ref, m_ref, v_ref, d_ref, m2_ref, v2_ref):
        gv = g_ref[...]
        m2 = ADAM_B1 * m_ref[...] + (1.0 - ADAM_B1) * gv
        v2 = ADAM_B2 * v_ref[...] + (1.0 - ADAM_B2) * (gv * gv)
        m_hat = m2 / (1.0 - ADAM_B1 ** ADAM_STEP)
        v_hat = v2 / (1.0 - ADAM_B2 ** ADAM_STEP)
        d_ref[...] = -ADAM_LR * (m_hat / (jnp.sqrt(v_hat) + ADAM_EPS) + ADAM_WD * w_ref[...])
        m2_ref[...] = m2
        v2_ref[...] = v2

    spec = pl.BlockSpec((tr, cols), lambda i: (i, 0))
    outs = pl.pallas_call(
        body,
        name=name,
        grid=(rows // tr,),
        in_specs=[spec] * 4,
        out_specs=[spec] * 3,
        out_shape=[jax.ShapeDtypeStruct((rows, cols), F32)] * 3,
        compiler_params=_params(("arbitrary",)),
    )(*[t.reshape(rows, cols) for t in (w, g, m, v)])
    return tuple(o.reshape(shape) for o in outs)


MESH_ID = pl.DeviceIdType.MESH
ANY = pl.BlockSpec(memory_space=pl.ANY)


def _place():
    x, y, c = lax.axis_index("x"), lax.axis_index("y"), lax.axis_index("c")
    other_chips = [(1 - x, y), (x, 1 - y), (1 - x, 1 - y)]
    return x, y, c, other_chips


def all_gather_shards(bufs, name):
    n = len(bufs)

    def body(*refs):
        x_refs, o_refs = refs[:n], refs[n:2 * n]
        copies = _gather_copies(x_refs, o_refs, *refs[2 * n:])
        _gather_start(copies)
        _gather_finish(copies)

    outs = pl.pallas_call(
        body,
        name=name,
        in_specs=[ANY] * n,
        out_specs=[ANY] * n,
        out_shape=_gather_out_shapes(bufs),
        scratch_shapes=_gather_sems(n),
    )(*bufs)
    return _set_own_slots(outs, bufs)


def _gather_out_shapes(bufs):
    return [jax.ShapeDtypeStruct((4,) + b.shape, b.dtype) for b in bufs]


def _gather_sems(n):
    return [pltpu.SemaphoreType.DMA((6 * n,)), pltpu.SemaphoreType.DMA((6 * n,))]


def _set_own_slots(outs, bufs):
    if not outs:
        return []
    me = 2 * lax.axis_index("x") + lax.axis_index("y")
    return [lax.dynamic_update_index_in_dim(o, b, me, 0) for o, b in zip(outs, bufs)]


def _gather_copies(x_refs, o_refs, send_sems, recv_sems):
    x, y, c, chips = _place()
    me = 2 * x + y
    sibling = (x, y, 1 - c)

    def part(a, slot, hf):
        half = x_refs[a].shape[0] // 2
        return o_refs[a].at[slot, pl.ds(hf * half, half), :]

    def mine(a):
        half = x_refs[a].shape[0] // 2
        return x_refs[a].at[pl.ds(c * half, half), :]

    def copy(k, src, dst, to):
        return pltpu.make_async_remote_copy(src_ref=src, dst_ref=dst, send_sem=send_sems.at[k],
                                            recv_sem=recv_sems.at[k], device_id=to, device_id_type=MESH_ID)

    sends, arrivals, passes, passed = [], [], [], []
    for a in range(len(x_refs)):
        for j, (px, py) in enumerate(chips):
            landed, theirs = part(a, 2 * px + py, c), part(a, 2 * px + py, 1 - c)
            sends.append(copy(6 * a + j, mine(a), part(a, me, c), (px, py, c)))
            arrivals.append(copy(6 * a + j, mine(a), landed, (px, py, c)))
            passes.append(copy(6 * a + 3 + j, landed, landed, sibling))
            passed.append(copy(6 * a + 3 + j, theirs, theirs, sibling))
    return sends, arrivals, passes, passed


def _gather_start(copies):
    for cp in copies[0]:
        cp.start()


def _gather_finish(copies):
    sends, arrivals, passes, passed = copies
    for arrival, cp in zip(arrivals, passes):
        arrival.wait_recv()
        cp.start()
    for cp in passed:
        cp.wait_recv()
    for cp in sends + passes:
        cp.wait_send()


def swap_halves(bufs, name):
    n = len(bufs)

    def body(*refs):
        x_refs, o_refs = refs[:n], refs[n:2 * n]
        send_sems, recv_sems = refs[2 * n:]
        x, y, c, _ = _place()
        copies = []
        for a in range(n):
            half = bufs[a].shape[1] // 2
            cp = pltpu.make_async_remote_copy(src_ref=x_refs[a].at[:, pl.ds((1 - c) * half, half), :], dst_ref=o_refs[a],
                                              send_sem=send_sems.at[a], recv_sem=recv_sems.at[a],
                                              device_id=(x, y, 1 - c), device_id_type=MESH_ID)
            cp.start()
            copies.append(cp)
        for cp in copies:
            cp.wait()

    return pl.pallas_call(
        body,
        name=name,
        in_specs=[ANY] * n,
        out_specs=[ANY] * n,
        out_shape=[jax.ShapeDtypeStruct((4, b.shape[1] // 2, b.shape[2]), b.dtype) for b in bufs],
        scratch_shapes=[pltpu.SemaphoreType.DMA((n,)), pltpu.SemaphoreType.DMA((n,))],
    )(*bufs)


def scatter_to_chips(bufs, name):
    n = len(bufs)

    def body(*refs):
        x_refs, o_refs = refs[:n], refs[n:2 * n]
        copies = _scatter_copies(x_refs, o_refs, *refs[2 * n:])
        _scatter_start(copies)
        _scatter_finish(copies)

    outs = pl.pallas_call(
        body,
        name=name,
        in_specs=[ANY] * n,
        out_specs=[ANY] * n,
        out_shape=[jax.ShapeDtypeStruct(b.shape, b.dtype) for b in bufs],
        scratch_shapes=_scatter_sems(n),
    )(*bufs)
    return _keep_own_slots(outs, bufs)


def _scatter_sems(n):
    return [pltpu.SemaphoreType.DMA((3 * n,)), pltpu.SemaphoreType.DMA((3 * n,))]


def _keep_own_slots(outs, bufs):
    if not outs:
        return []
    me = 2 * lax.axis_index("x") + lax.axis_index("y")
    return [lax.dynamic_update_index_in_dim(o, lax.dynamic_index_in_dim(b, me, 0, keepdims=False), me, 0)
            for o, b in zip(outs, bufs)]


def _scatter_copies(x_refs, o_refs, send_sems, recv_sems):
    x, y, c, chips = _place()
    me = 2 * x + y

    def copy(a, j, src_slot, dst_slot, px, py):
        return pltpu.make_async_remote_copy(src_ref=x_refs[a].at[src_slot], dst_ref=o_refs[a].at[dst_slot],
                                            send_sem=send_sems.at[3 * a + j], recv_sem=recv_sems.at[3 * a + j],
                                            device_id=(px, py, c), device_id_type=MESH_ID)

    sends = [copy(a, j, 2 * px + py, me, px, py) for a in range(len(x_refs)) for j, (px, py) in enumerate(chips)]
    arrivals = [copy(a, j, me, 2 * px + py, px, py) for a in range(len(x_refs)) for j, (px, py) in enumerate(chips)]
    return sends, arrivals


def _scatter_start(copies):
    for cp in copies[0]:
        cp.start()


def _scatter_finish(copies):
    for cp in copies[1]:
        cp.wait_recv()
    for cp in copies[0]:
        cp.wait_send()


def share_halves(groups, name):
    bufs = [b for grp in groups for b in grp]
    where = [(gi, li) for gi, grp in enumerate(groups) for li in range(len(grp))]
    n = len(bufs)

    def body(*refs):
        x_refs, o_refs = refs[:n], refs[n:n + len(groups)]
        send_sems, recv_sems = refs[n + len(groups):]
        x, y, c, _ = _place()
        sent, arrive = [], []
        for a, (gi, li) in enumerate(where):

            def copy(hf, a=a, gi=gi, li=li):
                return pltpu.make_async_remote_copy(src_ref=x_refs[a], dst_ref=o_refs[gi].at[li, hf],
                                                    send_sem=send_sems.at[a], recv_sem=recv_sems.at[a],
                                                    device_id=(x, y, 1 - c), device_id_type=MESH_ID)

            sent.append(copy(c))
            arrive.append(copy(1 - c))
        for cp in sent:
            cp.start()
        for cp in arrive:
            cp.wait_recv()
        for cp in sent:
            cp.wait_send()

    outs = pl.pallas_call(
        body,
        name=name,
        in_specs=[ANY] * n,
        out_specs=[ANY] * len(groups),
        out_shape=[jax.ShapeDtypeStruct((len(grp), 2) + grp[0].shape, grp[0].dtype) for grp in groups],
        scratch_shapes=[pltpu.SemaphoreType.DMA((n,)), pltpu.SemaphoreType.DMA((n,))],
    )(*bufs)
    c = lax.axis_index("c")
    full = [lax.dynamic_update_index_in_dim(o, jnp.stack(grp), c, 1) for o, grp in zip(outs, groups)]
    return [t.reshape(t.shape[0], 2 * t.shape[2], t.shape[3]) for t in full]


def pair_sums(bufs, dtypes, tag):
    core = lax.axis_index("c").astype(jnp.int32).reshape(1)
    landed = swap_halves(bufs, "rs_pair_" + tag)
    return [pair_sum(b, l, core, dt, "rs_pair_sum_%s%d" % (tag, i)) for i, (b, l, dt) in enumerate(zip(bufs, landed, dtypes))]


def _row_tiles(length):
    return (640, 320) if length > 2048 else (128, 64)


def _divisor_tile(rows, target):
    return max(t for t in range(8, min(rows, target) + 1, 8) if rows % t == 0)


def _local_step(x, target, wt, late_shards, layout_late, reduce_early):
    seq, d = x.shape
    length = N_META + seq
    tm, tm_ffn = _row_tiles(length)
    lp = -(-length // tm) * tm
    tail = jnp.zeros((lp - length, d), F32)
    h0 = jnp.concatenate([wt["meta"], x, tail], axis=0)[None]
    tgt = jnp.concatenate([jnp.zeros((N_META, d), F32), target, tail], axis=0)
    nn = functools.partial(mm_nn, tm=_divisor_tile(lp, 1664))
    nt = functools.partial(mm_nt, tm=_divisor_tile(lp, 1040))
    tn = functools.partial(mm_tn, tm=_divisor_tile(lp, 1664), rb=256)
    ln_g = [wt["ln_mix_g"][0:1], wt["ln_ffn_g"][0:1], wt["ln_mix_g"][1:2], wt["ln_ffn_g"][1:2]]
    ln_b = [wt["ln_mix_b"][0:1], wt["ln_ffn_b"][0:1], wt["ln_mix_b"][1:2], wt["ln_ffn_b"][1:2]]

    p5 = nn(h0, wt["a5"], name="a_in5")
    pz = nn(h0, wt["az"], name="a_inz")
    qkvbg = gdn_pre_fwd(p5, wt["a_conv3"], wt["alog_b"], wt["dtb_b"], tm=tm, cb=2 * HEAD_DIM)
    o, states, tinv, late_stacks = gdn_chunk_fwd(qkvbg, late_shards)
    wt = {**wt, **layout_late(late_stacks)}
    onz = gdn_post_fwd(o[None], pz, wt["anorm_b"], tm=tm)
    r1, h1 = ln_fwd(h0, nn(onz, wt["a_out"], name="a_out"), ln_g[0], ln_b[0], tm=tm, name="ln1")
    up0 = nn(h1, wt["up"][0], name="up0")
    act0 = ffn_act_fwd(up0, wt["fconv"][0], tm=tm_ffn, name="ffn_act0")
    r2, h2 = ln_fwd(h1, nn(act0, wt["down"][0], name="down0"), ln_g[1], ln_b[1], tm=tm, name="ln2")
    pb = nn(h2, wt["b_in"], name="b_in")
    sc = sc_fwd(pb, wt["b_conv"], tm=tm_ffn, cb=d)
    r3, h3 = ln_fwd(h2, nn(sc, wt["b_out"], name="b_out"), ln_g[2], ln_b[2], tm=tm, name="ln3")
    up1 = nn(h3, wt["up"][1], name="up1")
    act1 = ffn_act_fwd(up1, wt["fconv"][1], tm=tm_ffn, name="ffn_act1")
    r4, h4 = ln_fwd(h3, nn(act1, wt["down"][1], name="down1"), ln_g[3], ln_b[3], tm=tm, name="ln4")

    dh4, loss_part = loss_grad(h4, tgt, first=N_META, count=seq, tm=tm)

    grads = {}
    dr4, dgb4 = ln_bwd(r4, dh4, ln_g[3], tm=tm, name="ln4_bwd")
    d_down1 = tn(act1, dr4, name="d_down1")
    dact1 = nt(dr4, wt["down"][1], name="d_act1")
    dup1, dfconv1 = ffn_act_bwd(up1, dact1, wt["fconv"][1], tm=tm_ffn, name="ffn_act1_bwd")
    dup1 = dup1.reshape(up1.shape)
    d_up1 = tn(h3, dup1, name="d_up1")
    dh3 = nt(dup1, wt["up"][1], res=dr4, res_scale=ALPHA, name="d_h3")

    dr3, dgb3 = ln_bwd(r3, dh3, ln_g[2], tm=tm, name="ln3_bwd")
    d_bout = tn(sc, dr3, name="d_b_out")
    dsc = nt(dr3, wt["b_out"], name="d_sc")
    dpb, dbconv = sc_bwd(pb, dsc, wt["b_conv"], tm=tm_ffn, cb=d)
    d_bin = tn(h2, dpb, name="d_b_in")
    dh2 = nt(dpb, wt["b_in"], res=dr3, res_scale=ALPHA, name="d_h2")

    dr2, dgb2 = ln_bwd(r2, dh2, ln_g[1], tm=tm, name="ln2_bwd")
    d_down0 = tn(act0, dr2, name="d_down0")
    dact0 = nt(dr2, wt["down"][0], name="d_act0")
    dup0, dfconv0 = ffn_act_bwd(up0, dact0, wt["fconv"][0], tm=tm_ffn, name="ffn_act0_bwd")
    dup0 = dup0.reshape(up0.shape)
    d_up0 = tn(h1, dup0, name="d_up0")
    dh1 = nt(dup0, wt["up"][0], res=dr2, res_scale=ALPHA, name="d_h1")
    grads["b_w_in"] = [d_bin[0].transpose(1, 0, 2).reshape(d, 4, 3 * d // 4).transpose(1, 0, 2)]
    grads["b_w_out"] = [d_bout.reshape(4, d // 4, d)]
    grads["ffn_w_up"] = [d_up0[0], d_up1[0]]
    grads["ffn_w_down"] = [t.reshape(4, -1, d) for t in (d_down0, d_down1)]
    leaving = reduce_early(grads)

    dr1, dgb1 = ln_bwd(r1, dh1, ln_g[0], tm=tm, name="ln1_bwd")
    d_aout = tn(onz, dr1, name="d_a_out")
    donz = nt(dr1, wt["a_out"], name="d_onz")
    d_o, dz, dnw = gdn_post_bwd(o[None], pz, donz, wt["anorm_b"], tm=tm)
    dqkvbg, landed = gdn_chunk_bwd(qkvbg, states, tinv, d_o[0], leaving)
    dp5, daconv, dscal = gdn_pre_bwd(p5, dqkvbg, wt["a_conv3"], wt["alog_b"], wt["dtb_b"], tm=tm, cb=2 * HEAD_DIM)
    d_a5 = tn(h0, dp5, name="d_a_in5")
    d_az = tn(h0, dz, name="d_a_inz")
    dh0 = nt(dp5, wt["a5"], res=dr1, res_scale=ALPHA, name="d_h0a")
    dh0 = nt(dz, wt["az"], res=dh0, res_scale=1.0, name="d_h0")

    width = HEADS * HEAD_DIM
    d_ba = head_lane_sum(d_a5[0, 3:5])[:, :, :HEADS]
    d_a_in = jnp.concatenate([d_a5[0, 0], d_a5[0, 1], d_a5[0, 2], d_az[0, 0], d_ba[0], d_ba[1]], axis=1)
    n_in = d_a_in.shape[1] // 4
    grads["a_w_in"] = [d_a_in.reshape(d, 4, n_in).transpose(1, 0, 2)]
    grads["a_w_out"] = [d_aout.reshape(4, width // 4, d)]
    grads["a_conv"] = daconv.sum(axis=2).transpose(1, 0, 2).reshape(1, GDN_CONV, 3 * width)
    per_head = dscal.reshape(2, 8, HEADS, HEAD_DIM).sum(axis=(1, 3))
    grads["a_log"] = per_head[0][None]
    grads["a_dt_bias"] = per_head[1][None]
    grads["a_norm"] = dnw.reshape(8, HEADS, HEAD_DIM).sum(axis=(0, 1))[None]
    grads["b_conv"] = dbconv.sum(axis=1)[None]
    lns = [dgb1, dgb2, dgb3, dgb4]
    grads["ln_mix_g"] = jnp.stack([lns[0][0].sum(0), lns[2][0].sum(0)])
    grads["ln_mix_b"] = jnp.stack([lns[0][1].sum(0), lns[2][1].sum(0)])
    grads["ln_ffn_g"] = jnp.stack([lns[1][0].sum(0), lns[3][0].sum(0)])
    grads["ln_ffn_b"] = jnp.stack([lns[1][1].sum(0), lns[3][1].sum(0)])
    grads["ffn_conv"] = jnp.stack([t.sum(axis=2).transpose(1, 0, 2).reshape(FFN_CONV, -1) for t in (dfconv0, dfconv1)])
    grads["meta"] = dh0[0, :N_META]
    return loss_part, dh0, grads, landed


WEIGHTS = ["meta", "a_w_in", "a_conv", "a_log", "a_dt_bias", "a_norm", "a_w_out", "b_w_in", "b_conv", "b_w_out",
           "ln_mix_g", "ln_mix_b", "ffn_w_up", "ffn_conv", "ffn_w_down", "ln_ffn_g", "ln_ffn_b"]
EARLY_WEIGHTS = ["a_w_in", "a_w_out"]
LATE_WEIGHTS = ["b_w_in", "b_w_out", "ffn_w_up", "ffn_w_down"]
MATMUL_WEIGHTS = EARLY_WEIGHTS + LATE_WEIGHTS
SMALL_SHARDED = ["a_conv", "b_conv", "ffn_conv", "meta"]
REPLICATED = ["a_log", "a_dt_bias", "a_norm", "ln_mix_g", "ln_mix_b", "ln_ffn_g", "ln_ffn_b"]
SHARD_AXIS = {"meta": 1, "a_w_in": 2, "a_conv": 2, "a_w_out": 1, "b_w_in": 2, "b_conv": 2, "b_w_out": 1,
              "ffn_w_up": 2, "ffn_conv": 2, "ffn_w_down": 1}
PACK_COLS = 1024
PACK_ROWS_MULTIPLE = 32


def _pack(pieces, lead=()):
    flat = jnp.concatenate([p.reshape(lead + (-1,)) for p in pieces], axis=-1)
    n = flat.shape[-1]
    rows = -(-n // (PACK_COLS * PACK_ROWS_MULTIPLE)) * PACK_ROWS_MULTIPLE
    flat = jnp.pad(flat, [(0, 0)] * len(lead) + [(0, rows * PACK_COLS - n)])
    return flat.reshape(lead + (rows, PACK_COLS))


def _unpack(buf, shapes, lead=()):
    flat = buf.reshape(lead + (-1,))
    out, off = [], 0
    for shp in shapes:
        n = 1
        for s in shp:
            n *= s
        out.append(flat[..., off:off + n].reshape(lead + tuple(shp)))
        off += n
    return out


def _join_shards(stacked, axis):
    return jnp.concatenate([stacked[k] for k in range(4)], axis=axis)


def _split_shards(full, axis):
    return jnp.stack(jnp.split(full, 4, axis=axis))


def _weight_layers(w, names):
    return [w[n][l].astype(BF16) for n in names for l in range(w[n].shape[0])]


def _per_weight(arrays, w, names):
    it = iter(arrays)
    return {n: [next(it) for _ in range(w[n].shape[0])] for n in names}


def _layout_early(full, w):
    width = HEADS * HEAD_DIM
    wt = {n: w[n] for n in ("ln_mix_g", "ln_mix_b", "ln_ffn_g", "ln_ffn_b")}
    w_in = _join_shards(full["a_w_in"][0], 1)
    d = w_in.shape[0]
    n_ff = full["ffn_conv"].shape[2] // 2
    blocks = [w_in[:, s * width:(s + 1) * width] for s in range(4)]
    b_exp = jnp.repeat(w_in[:, 4 * width:4 * width + HEADS], HEAD_DIM, axis=1)
    a_exp = jnp.repeat(w_in[:, 4 * width + HEADS:], HEAD_DIM, axis=1)
    wt["a5"] = jnp.stack([blocks[0], blocks[1], blocks[2], b_exp, a_exp])[None]
    wt["az"] = blocks[3][None, None]
    wt["a_out"] = full["a_w_out"][0].reshape(1, 1, width, d)
    wt["a_conv3"] = full["a_conv"][0].reshape(GDN_CONV, 3, width).transpose(1, 0, 2)
    wt["b_conv"] = full["b_conv"][0]
    wt["fconv"] = [full["ffn_conv"][l].reshape(FFN_CONV, 2, n_ff).transpose(1, 0, 2) for l in range(2)]
    wt["meta"] = full["meta"]
    wt["alog_b"] = jnp.repeat(w["a_log"][0], HEAD_DIM)[None]
    wt["dtb_b"] = jnp.repeat(w["a_dt_bias"][0], HEAD_DIM)[None]
    wt["anorm_b"] = jnp.tile(w["a_norm"][0], HEADS)[None]
    return wt


def _layout_late(full):
    d = full["b_w_in"][0].shape[1]
    n_ff = full["ffn_w_up"][0].shape[2]
    return {
        "b_in": _join_shards(full["b_w_in"][0], 1).reshape(d, 3, d).transpose(1, 0, 2)[None],
        "b_out": full["b_w_out"][0].reshape(1, 1, d, d),
        "up": [t[None] for t in full["ffn_w_up"]],
        "down": [t.reshape(2, 1, n_ff, d) for t in full["ffn_w_down"]],
    }


def kernel(x, meta, a_w_in, a_conv, a_log, a_dt_bias, a_norm, a_w_out, b_w_in, b_conv, b_w_out, ln_mix_g, ln_mix_b, ffn_w_up, ffn_conv, ffn_w_down, ln_ffn_g, ln_ffn_b, loss_target, m_meta, m_a_w_in, m_a_conv, m_a_log, m_a_dt_bias, m_a_norm, m_a_w_out, m_b_w_in, m_b_conv, m_b_w_out, m_ln_mix_g, m_ln_mix_b, m_ffn_w_up, m_ffn_conv, m_ffn_w_down, m_ln_ffn_g, m_ln_ffn_b, v_meta, v_a_w_in, v_a_conv, v_a_log, v_a_dt_bias, v_a_norm, v_a_w_out, v_b_w_in, v_b_conv, v_b_w_out, v_ln_mix_g, v_ln_mix_b, v_ffn_w_up, v_ffn_conv, v_ffn_w_down, v_ln_ffn_g, v_ln_ffn_b):
    w = dict(meta=meta, a_w_in=a_w_in, a_conv=a_conv, a_log=a_log, a_dt_bias=a_dt_bias, a_norm=a_norm, a_w_out=a_w_out,
             b_w_in=b_w_in, b_conv=b_conv, b_w_out=b_w_out, ln_mix_g=ln_mix_g, ln_mix_b=ln_mix_b, ffn_w_up=ffn_w_up,
             ffn_conv=ffn_conv, ffn_w_down=ffn_w_down, ln_ffn_g=ln_ffn_g, ln_ffn_b=ln_ffn_b)
    m = dict(meta=m_meta, a_w_in=m_a_w_in, a_conv=m_a_conv, a_log=m_a_log, a_dt_bias=m_a_dt_bias, a_norm=m_a_norm,
             a_w_out=m_a_w_out, b_w_in=m_b_w_in, b_conv=m_b_conv, b_w_out=m_b_w_out, ln_mix_g=m_ln_mix_g,
             ln_mix_b=m_ln_mix_b, ffn_w_up=m_ffn_w_up, ffn_conv=m_ffn_conv, ffn_w_down=m_ffn_w_down,
             ln_ffn_g=m_ln_ffn_g, ln_ffn_b=m_ln_ffn_b)
    v = dict(meta=v_meta, a_w_in=v_a_w_in, a_conv=v_a_conv, a_log=v_a_log, a_dt_bias=v_a_dt_bias, a_norm=v_a_norm,
             a_w_out=v_a_w_out, b_w_in=v_b_w_in, b_conv=v_b_conv, b_w_out=v_b_w_out, ln_mix_g=v_ln_mix_g,
             ln_mix_b=v_ln_mix_b, ffn_w_up=v_ffn_w_up, ffn_conv=v_ffn_conv, ffn_w_down=v_ffn_w_down,
             ln_ffn_g=v_ln_ffn_g, ln_ffn_b=v_ln_ffn_b)
    seq = x.shape[1]
    *stacks, small = all_gather_shards(_weight_layers(w, EARLY_WEIGHTS) + [_pack([w[n] for n in SMALL_SHARDED])],
                                       "gather_early")
    full = _per_weight(stacks, w, EARLY_WEIGHTS)
    for n, t in zip(SMALL_SHARDED, _unpack(small, [w[n].shape for n in SMALL_SHARDED], lead=(4,))):
        full[n] = _join_shards(t, SHARD_AXIS[n])

    def layout_late(late_stacks):
        return _layout_late(_per_weight(late_stacks, w, LATE_WEIGHTS))

    def reduce_early(grads):
        bufs = [g for n in LATE_WEIGHTS for g in grads[n]]
        return pair_sums(bufs, [BF16] * len(bufs), "late")

    loss_part, dh0, grads, landed_late = _local_step(x[0], loss_target[0], _layout_early(full, w),
                                                     _weight_layers(w, LATE_WEIGHTS), layout_late, reduce_early)
    pieces = [_split_shards(grads[n], SHARD_AXIS[n]) for n in SMALL_SHARDED]
    same = jnp.concatenate([grads[n].reshape(-1) for n in REPLICATED] + [jnp.sum(loss_part).reshape(1)])
    pieces.append(jnp.broadcast_to(same, (4,) + same.shape))
    bufs = [g for n in EARLY_WEIGHTS for g in grads[n]] + [_pack(pieces, lead=(4,))]
    landed = scatter_to_chips(pair_sums(bufs, [BF16] * (len(bufs) - 1) + [F32], "early"), "rs_chips_early")
    totals = [chip_sum(t, "rs_chip_sum%d" % i) for i, t in enumerate(landed + landed_late)]
    by_weight = _per_weight(totals[:len(bufs) - 1] + totals[len(bufs):], w, MATMUL_WEIGHTS)
    *shared, small_total = share_halves([by_weight[n] for n in MATMUL_WEIGHTS] + [[totals[len(bufs) - 1]]], "rs_share")
    grad_w = {n: t.reshape(w[n].shape) for n, t in zip(MATMUL_WEIGHTS, shared)}
    rest = SMALL_SHARDED + REPLICATED
    unpacked = _unpack(small_total[0], [w[n].shape for n in rest] + [()])
    grad_w.update(zip(rest, unpacked[:-1]))
    loss = unpacked[-1]
    grad_x = dh0[:, N_META:N_META + seq]
    steps = [adamw(w[n], grad_w[n], m[n], v[n], "adamw_" + n) for n in WEIGHTS]
    return (loss, grad_x, *[grad_w[n] for n in WEIGHTS], *[s[0] for s in steps], *[s[1] for s in steps],
            *[s[2] for s in steps])
```

```python
import functools

import jax
import jax.numpy as jnp
from jax import lax
from jax.experimental import pallas as pl
from jax.experimental.pallas import tpu as pltpu

F32 = jnp.float32
BF16 = jnp.bfloat16
HI = lax.Precision.HIGHEST

N_META = 16
HEADS = 8
HEAD_DIM = 128
CHUNK = 64
GDN_CONV = 4
SC_CONV = 3
FFN_CONV = 3
ALPHA = 4.0 ** 0.25
LN_EPS = 1e-5
RMS_EPS = 1e-6
L2_EPS = 1e-6
Q_SCALE = HEAD_DIM ** -0.5

ADAM_LR = 0.001
ADAM_B1 = 0.9
ADAM_B2 = 0.999
ADAM_EPS = 1e-08
ADAM_WD = 0.01
ADAM_STEP = 10

HALO = 8
VMEM_LIMIT = 48 * 1024 * 1024


def _params(sem=None):
    return pltpu.CompilerParams(dimension_semantics=sem, vmem_limit_bytes=VMEM_LIMIT)


def _dot(a, b, prec=None):
    return jnp.dot(a, b, preferred_element_type=F32, precision=prec)


def _dot_nt(a, b, prec=None):
    return lax.dot_general(a, b, (((1,), (1,)), ((), ())), preferred_element_type=F32, precision=prec)


def _dot_tn(a, b, prec=None):
    return lax.dot_general(a, b, (((0,), (0,)), ((), ())), preferred_element_type=F32, precision=prec)


def _sigmoid(x):
    return 1.0 / (1.0 + jnp.exp(-x))


def _tri_masks():
    r = lax.broadcasted_iota(jnp.int32, (CHUNK, CHUNK), 0)
    c = lax.broadcasted_iota(jnp.int32, (CHUNK, CHUNK), 1)
    return r >= c, r > c, r == c


def _split_hi_lo(x):
    hi = x.astype(BF16)
    return hi, (x - hi.astype(F32)).astype(BF16)


def _mask_dot(mask, x):
    hi, lo = _split_hi_lo(x)
    return _dot(mask, hi) + _dot(mask, lo)


@jax.custom_vjp
def _cumsum_rows(g):
    causal, _, _ = _tri_masks()
    return _mask_dot(causal.astype(BF16), g)


def _cumsum_rows_fwd(g):
    return _cumsum_rows(g), None


def _cumsum_rows_bwd(_, dy):
    _, strict, _ = _tri_masks()
    return (_mask_dot((~strict).astype(BF16), dy),)


_cumsum_rows.defvjp(_cumsum_rows_fwd, _cumsum_rows_bwd)


def _dot_split3(a, b):
    a_hi, a_lo = _split_hi_lo(a)
    b_hi, b_lo = _split_hi_lo(b)
    return _dot(a_hi, b_hi) + (_dot(a_hi, b_lo) + _dot(a_lo, b_hi))


@jax.custom_vjp
def _dot_precise(a, b):
    return _dot_split3(a, b)


def _dot_precise_fwd(a, b):
    return _dot_split3(a, b), (a, b)


def _dot_precise_bwd(operands, ct):
    a, b = operands
    return _dot_split3(ct, b.T), _dot_split3(a.T, ct)


_dot_precise.defvjp(_dot_precise_fwd, _dot_precise_bwd)


def _gdn_m(ks, g64s, bbs):
    causal, strict, _ = _tri_masks()
    a = [_cumsum_rows(g) for g in g64s]
    decay = [jnp.exp(jnp.where(causal, x - x.T, -1e30)) for x in a]
    kk = [_dot_nt(k * b, k) for k, b in zip(ks, bbs)]
    return [jnp.where(strict, x * d, 0.0) for x, d in zip(kk, decay)]


def _gdn_inverse_stages(ks, g64s, bbs):
    ms = _gdn_m(ks, g64s, bbs)
    yield
    r = lax.broadcasted_iota(jnp.int32, (CHUNK, CHUNK), 0)
    c = lax.broadcasted_iota(jnp.int32, (CHUNK, CHUNK), 1)
    eye = (r == c).astype(F32)
    same = [jnp.right_shift(r, s) == jnp.right_shift(c, s) for s in (3, 4, 5)]
    d = [jnp.where(same[0], m, 0.0) for m in ms]
    p = [_dot(x, x) for x in d]
    yield
    t = [eye - x for x in d]
    t = [x + _dot(x, y) for x, y in zip(t, p)]
    p = [_dot(x, x) for x in p]
    yield
    t = [x + _dot(x, y) for x, y in zip(t, p)]
    yield
    for inner, outer in ((same[0], same[1]), (same[1], same[2]), (same[2], None)):
        joins = ~inner if outer is None else (outer & ~inner)
        o = [_dot(x, jnp.where(joins, m, 0.0)) for x, m in zip(t, ms)]
        yield
        t = [x - _dot(y, x) for x, y in zip(t, o)]
        yield
    res = [eye - x - _dot_split3(m, x) for m, x in zip(ms, t)]
    yield
    return [x + _dot(x, y) for x, y in zip(t, res)]


def _gdn_apply_stages(qs, ks, vs, gbs, g64s, bbs, ss, ts):
    causal, _, _ = _tri_masks()
    n = range(len(qs))
    gc = [_cumsum_rows(g) for g in gbs]
    a = [_cumsum_rows(g) for g in g64s]
    qk = [_dot_nt(qs[h], ks[h]) for h in n]
    yield
    decay = [jnp.exp(jnp.where(causal, x - x.T, -1e30)) for x in a]
    eg = [jnp.exp(x) for x in gc]
    u = [_dot_precise(ts[h], vs[h] * bbs[h]) for h in n]
    w = [_dot_precise(ts[h], ks[h] * bbs[h] * eg[h]) for h in n]
    qk = [qk[h] * decay[h] for h in n]
    gl = [jnp.sum(g, axis=0, keepdims=True) for g in gbs]
    kd = [ks[h] * jnp.exp(gl[h] - gc[h]) for h in n]
    yield
    v_new = [u[h] - _dot(w[h], ss[h]) for h in n]
    q_s = [_dot(qs[h] * eg[h], ss[h]) for h in n]
    yield
    o = [q_s[h] + _dot(qk[h], v_new[h]) for h in n]
    s2 = [ss[h] * jnp.exp(gl[h]) + _dot_tn(kd[h], v_new[h]) for h in n]
    return o, s2


def _run_stages(*generators):
    results = [None] * len(generators)
    live = dict(enumerate(generators))
    while live:
        for i, gen in list(live.items()):
            try:
                next(gen)
            except StopIteration as stop:
                results[i] = stop.value
                del live[i]
    return results


def _gdn_apply(qs, ks, vs, gbs, g64s, bbs, ss, ts):
    return _run_stages(_gdn_apply_stages(qs, ks, vs, gbs, g64s, bbs, ss, ts))[0]


def _head_slices(h):
    return slice(h * HEAD_DIM, (h + 1) * HEAD_DIM), slice(h * HEAD_DIM, h * HEAD_DIM + CHUNK)


def _gdn_head_values(x_ref):
    out = [[], [], [], [], [], []]
    for h in range(HEADS):
        sl, sl64 = _head_slices(h)
        for lst, val in zip(out, (x_ref[0, :, sl], x_ref[1, :, sl], x_ref[2, :, sl], x_ref[4, :, sl],
                                  x_ref[4, :, sl64], x_ref[3, :, sl])):
            lst.append(val)
    return out


def gdn_chunk_fwd(qkvbg, gather=()):
    _, lp, width = qkvbg.shape
    n_chunks = lp // CHUNK
    n = len(gather)

    def body(x_ref, next_ref, *refs):
        shard_refs, (o_ref, s_ref, t_ref), refs = refs[:n], refs[n:n + 3], refs[n + 3:]
        stack_refs, state, t_next, sems = refs[:n], refs[n], refs[n + 1], refs[n + 2:]
        copies = _gather_copies(shard_refs, stack_refs, *sems) if n else None

        def inverse_stages(ref):
            _, ks, _, _, g64s, bbs = _gdn_head_values(ref)
            return _gdn_inverse_stages(ks, g64s, bbs)

        @pl.when(pl.program_id(0) == 0)
        def _():
            state[...] = jnp.zeros_like(state)
            for h, t in enumerate(_run_stages(inverse_stages(x_ref))[0]):
                t_next[h] = t
            if n:
                _gather_start(copies)

        qs, ks, vs, gbs, g64s, bbs = _gdn_head_values(x_ref)
        ss = [state[h] for h in range(HEADS)]
        ts = [t_next[h] for h in range(HEADS)]
        ts_next, (os_, s2) = _run_stages(inverse_stages(next_ref),
                                         _gdn_apply_stages(qs, ks, vs, gbs, g64s, bbs, ss, ts))
        for h in range(HEADS):
            s_ref[0, h] = ss[h]
            t_ref[0, h] = ts[h]
            t_next[h] = ts_next[h]
            o_ref[:, _head_slices(h)[0]] = os_[h]
            state[h] = s2[h]

        if n:
            @pl.when(pl.program_id(0) == n_chunks - 1)
            def _():
                _gather_finish(copies)

    o, states, tinv, *stacks = pl.pallas_call(
        body,
        name="gdn_chunk_fwd",
        grid=(n_chunks,),
        in_specs=[pl.BlockSpec((5, CHUNK, width), lambda c: (0, c, 0)),
                  pl.BlockSpec((5, CHUNK, width), lambda c: (0, jnp.minimum(c + 1, n_chunks - 1), 0))] + [ANY] * n,
        out_specs=[
            pl.BlockSpec((CHUNK, width), lambda c: (c, 0)),
            pl.BlockSpec((1, HEADS, HEAD_DIM, HEAD_DIM), lambda c: (c, 0, 0, 0)),
            pl.BlockSpec((1, HEADS, CHUNK, CHUNK), lambda c: (c, 0, 0, 0)),
        ] + [ANY] * n,
        out_shape=[
            jax.ShapeDtypeStruct((lp, width), F32),
            jax.ShapeDtypeStruct((n_chunks, HEADS, HEAD_DIM, HEAD_DIM), F32),
            jax.ShapeDtypeStruct((n_chunks, HEADS, CHUNK, CHUNK), F32),
        ] + _gather_out_shapes(gather),
        scratch_shapes=[pltpu.VMEM((HEADS, HEAD_DIM, HEAD_DIM), F32), pltpu.VMEM((HEADS, CHUNK, CHUNK), F32)]
        + (_gather_sems(n) if n else []),
        compiler_params=_params(("arbitrary",)),
    )(qkvbg, qkvbg, *gather)
    return o, states, tinv, _set_own_slots(stacks, gather)


def gdn_chunk_bwd(qkvbg, states, tinv, d_o, scatter=()):
    _, lp, width = qkvbg.shape
    n_chunks = lp // CHUNK
    last = n_chunks - 1
    n = len(scatter)

    def body(x_ref, s_ref, t_ref, do_ref, *refs):
        leaving_refs, dx_ref, refs = refs[:n], refs[n], refs[n + 1:]
        landing_refs, dstate, sems = refs[:n], refs[n], refs[n + 1:]
        copies = _scatter_copies(leaving_refs, landing_refs, *sems) if n else None

        @pl.when(pl.program_id(0) == 0)
        def _():
            dstate[...] = jnp.zeros_like(dstate)
            if n:
                _scatter_start(copies)

        heads = range(HEADS)
        qs, ks, vs, gbs, g64s, bbs = _gdn_head_values(x_ref)
        ss = [s_ref[0, h] for h in heads]
        ts = [t_ref[0, h] for h in heads]
        d_out = ([do_ref[:, _head_slices(h)[0]] for h in heads], [dstate[h] for h in heads])
        _, vjp_apply = jax.vjp(_gdn_apply, qs, ks, vs, gbs, g64s, bbs, ss, ts)
        dq, dk, dv, dgb, dg64, dbb, ds, dt = vjp_apply(d_out)
        tts = [t.T for t in ts]
        dm = [_dot(tts[h], dt[h]) for h in heads]
        dm = [-_dot(dm[h], tts[h]) for h in heads]
        _, vjp_m = jax.vjp(_gdn_m, ks, g64s, bbs)
        dk2, dg64m, dbb2 = vjp_m(dm)
        for h in heads:
            sl, sl64 = _head_slices(h)
            dx_ref[0, :, sl] = dq[h]
            dx_ref[1, :, sl] = dk[h] + dk2[h]
            dx_ref[2, :, sl] = dv[h]
            dx_ref[3, :, sl] = dbb[h] + dbb2[h]
            dx_ref[4, :, sl] = dgb[h]
            dx_ref[4, :, sl64] += dg64[h] + dg64m[h]
            dstate[h] = ds[h]

        if n:
            @pl.when(pl.program_id(0) == n_chunks - 1)
            def _():
                _scatter_finish(copies)

    dqkvbg, *landed = pl.pallas_call(
        body,
        name="gdn_chunk_bwd",
        grid=(n_chunks,),
        in_specs=[
            pl.BlockSpec((5, CHUNK, width), lambda c: (0, last - c, 0)),
            pl.BlockSpec((1, HEADS, HEAD_DIM, HEAD_DIM), lambda c: (last - c, 0, 0, 0)),
            pl.BlockSpec((1, HEADS, CHUNK, CHUNK), lambda c: (last - c, 0, 0, 0)),
            pl.BlockSpec((CHUNK, width), lambda c: (last - c, 0)),
        ] + [ANY] * n,
        out_specs=[pl.BlockSpec((5, CHUNK, width), lambda c: (0, last - c, 0))] + [ANY] * n,
        out_shape=[jax.ShapeDtypeStruct(qkvbg.shape, F32)] + [jax.ShapeDtypeStruct(b.shape, b.dtype) for b in scatter],
        scratch_shapes=[pltpu.VMEM((HEADS, HEAD_DIM, HEAD_DIM), F32)] + (_scatter_sems(n) if n else []),
        compiler_params=_params(("arbitrary",)),
    )(qkvbg, states, tinv, d_o, *scatter)
    return dqkvbg, _keep_own_slots(landed, scatter)


def mm_nn(a, b, *, tm, name):
    ks, m, tk = a.shape
    _, ns, _, tn = b.shape

    def body(a_ref, b_ref, o_ref):
        p = _dot(a_ref[...].astype(BF16), b_ref[...])

        @pl.when(pl.program_id(2) == 0)
        def _():
            o_ref[...] = p

        @pl.when(pl.program_id(2) > 0)
        def _():
            o_ref[...] += p

    return pl.pallas_call(
        body,
        name=name,
        grid=(ns, m // tm, ks),
        in_specs=[
            pl.BlockSpec((None, tm, tk), lambda n, i, k: (k, i, 0)),
            pl.BlockSpec((None, None, tk, tn), lambda n, i, k: (k, n, 0, 0)),
        ],
        out_specs=pl.BlockSpec((None, tm, tn), lambda n, i, k: (n, i, 0)),
        out_shape=jax.ShapeDtypeStruct((ns, m, tn), F32),
        compiler_params=_params(("arbitrary", "arbitrary", "arbitrary")),
    )(a, b)


def mm_nt(dy, w, *, tm, name, res=None, res_scale=1.0):
    ns, m, tn = dy.shape
    ks, _, tk, _ = w.shape

    def body(*refs):
        if res is None:
            dy_ref, w_ref, o_ref = refs
        else:
            dy_ref, w_ref, r_ref, o_ref = refs
        p = _dot_nt(dy_ref[...].astype(BF16), w_ref[...])

        @pl.when(pl.program_id(2) == 0)
        def _():
            o_ref[...] = p if res is None else p + res_scale * r_ref[...]

        @pl.when(pl.program_id(2) > 0)
        def _():
            o_ref[...] += p

    in_specs = [
        pl.BlockSpec((None, tm, tn), lambda k, i, n: (n, i, 0)),
        pl.BlockSpec((None, None, tk, tn), lambda k, i, n: (k, n, 0, 0)),
    ]
    args = [dy, w]
    if res is not None:
        in_specs.append(pl.BlockSpec((None, tm, tk), lambda k, i, n: (k, i, 0)))
        args.append(res)
    return pl.pallas_call(
        body,
        name=name,
        grid=(ks, m // tm, ns),
        in_specs=in_specs,
        out_specs=pl.BlockSpec((None, tm, tk), lambda k, i, n: (k, i, 0)),
        out_shape=jax.ShapeDtypeStruct((ks, m, tk), F32),
        compiler_params=_params(("arbitrary", "arbitrary", "arbitrary")),
    )(*args)


def mm_tn(x, dy, *, tm, name, rb=None):
    ks, m, tk = x.shape
    ns, _, tn = dy.shape
    rb = tk if rb is None else rb

    def body(x_ref, dy_ref, o_ref):
        @pl.when(pl.program_id(2) == 0)
        def _():
            o_ref[...] = jnp.zeros_like(o_ref)

        dyb = dy_ref[...].astype(BF16)
        for r in range(0, tk, rb):
            o_ref[r:r + rb, :] += _dot_tn(x_ref[:, r:r + rb].astype(BF16), dyb)

    return pl.pallas_call(
        body,
        name=name,
        grid=(ks, ns, m // tm),
        in_specs=[
            pl.BlockSpec((None, tm, tk), lambda k, n, i: (k, i, 0)),
            pl.BlockSpec((None, tm, tn), lambda k, n, i: (n, i, 0)),
        ],
        out_specs=pl.BlockSpec((None, None, tk, tn), lambda k, n, i: (k, n, 0, 0)),
        out_shape=jax.ShapeDtypeStruct((ks, ns, tk, tn), F32),
        compiler_params=_params(("arbitrary", "arbitrary", "arbitrary")),
    )(x, dy)


def _row_partial(x):
    rows, c = x.shape
    return jnp.sum(x.reshape(rows // 8, 8, c), axis=0)


def ln_fwd(h_prev, mix, g, b, *, tm, name):
    _, lp, d = h_prev.shape

    def body(h_ref, m_ref, g_ref, b_ref, r_ref, o_ref):
        r = ALPHA * h_ref[...] + m_ref[...]
        mu = jnp.mean(r, axis=-1, keepdims=True)
        xc = r - mu
        var = jnp.mean(xc * xc, axis=-1, keepdims=True)
        r_ref[...] = r
        o_ref[...] = xc * lax.rsqrt(var + LN_EPS) * g_ref[...] + b_ref[...]

    row = pl.BlockSpec((None, tm, d), lambda i: (0, i, 0))
    vec = pl.BlockSpec((1, d), lambda i: (0, 0))
    return pl.pallas_call(
        body,
        name=name,
        grid=(lp // tm,),
        in_specs=[row, row, vec, vec],
        out_specs=[row, row],
        out_shape=[jax.ShapeDtypeStruct((1, lp, d), F32)] * 2,
        compiler_params=_params(("arbitrary",)),
    )(h_prev, mix, g, b)


def ln_bwd(r, dh, g, *, tm, name):
    _, lp, d = r.shape

    def body(r_ref, dh_ref, g_ref, dr_ref, dgb_ref):
        x = r_ref[...]
        dh_v = dh_ref[...]
        mu = jnp.mean(x, axis=-1, keepdims=True)
        xc = x - mu
        rstd = lax.rsqrt(jnp.mean(xc * xc, axis=-1, keepdims=True) + LN_EPS)
        xh = xc * rstd
        dxh = dh_v * g_ref[...]
        m1 = jnp.mean(dxh, axis=-1, keepdims=True)
        m2 = jnp.mean(dxh * xh, axis=-1, keepdims=True)
        dr_ref[...] = rstd * (dxh - m1 - xh * m2)

        @pl.when(pl.program_id(0) == 0)
        def _():
            dgb_ref[...] = jnp.zeros_like(dgb_ref)

        dgb_ref[0] += _row_partial(dh_v * xh)
        dgb_ref[1] += _row_partial(dh_v)

    row = pl.BlockSpec((None, tm, d), lambda i: (0, i, 0))
    return pl.pallas_call(
        body,
        name=name,
        grid=(lp // tm,),
        in_specs=[row, row, pl.BlockSpec((1, d), lambda i: (0, 0))],
        out_specs=[row, pl.BlockSpec((2, 8, d), lambda i: (0, 0, 0))],
        out_shape=[jax.ShapeDtypeStruct((1, lp, d), F32), jax.ShapeDtypeStruct((2, 8, d), F32)],
        compiler_params=_params(("arbitrary",)),
    )(r, dh, g)


def loss_grad(h, target, *, first, count, tm):
    _, lp, d = h.shape

    def body(h_ref, t_ref, dh_ref, l_ref):
        row = pl.program_id(0) * tm + lax.broadcasted_iota(jnp.int32, (tm, d), 0)
        valid = (row >= first) & (row < first + count)
        err = jnp.where(valid, h_ref[...] - t_ref[...], 0.0)
        dh_ref[...] = err * (1.0 / d)

        @pl.when(pl.program_id(0) == 0)
        def _():
            l_ref[...] = jnp.zeros_like(l_ref)

        l_ref[...] += _row_partial(err * err) * (0.5 / d)

    return pl.pallas_call(
        body,
        name="loss_grad",
        grid=(lp // tm,),
        in_specs=[pl.BlockSpec((None, tm, d), lambda i: (0, i, 0)), pl.BlockSpec((tm, d), lambda i: (i, 0))],
        out_specs=[pl.BlockSpec((None, tm, d), lambda i: (0, i, 0)), pl.BlockSpec((8, d), lambda i: (0, 0))],
        out_shape=[jax.ShapeDtypeStruct((1, lp, d), F32), jax.ShapeDtypeStruct((8, d), F32)],
        compiler_params=_params(("arbitrary",)),
    )(h, target)


def _halo_index(tile, tm):
    return jnp.maximum(tile * (tm // HALO) - 1, 0)


def _conv_fwd(xs_ref, w, taps, tm):
    acc = w(0) * xs_ref[pl.ds(HALO - taps + 1, tm), :]
    for j in range(1, taps):
        acc += w(j) * xs_ref[pl.ds(HALO - taps + 1 + j, tm), :]
    return acc


def _conv_bwd_x(dcs_ref, w, taps, tm):
    acc = w(0) * dcs_ref[pl.ds(taps - 1, tm), :]
    for j in range(1, taps):
        acc += w(j) * dcs_ref[pl.ds(taps - 1 - j, tm), :]
    return acc


SUB = 8
LANES = 128
PAIR = 2 * SUB
STRIP_UNROLL = 2


def _pair_rows(r0):
    return pl.ds(r0, SUB), pl.ds(r0 + SUB if isinstance(r0, int) else pl.multiple_of(r0 + SUB, SUB), SUB)


def _shift_down(cur, prev, s):
    if s == 0:
        return cur
    row = lax.broadcasted_iota(jnp.int32, cur.shape, 0)
    return jnp.where(row < s, pltpu.roll(prev, s, axis=0), pltpu.roll(cur, s, axis=0))


def _shift_up(cur, nxt, s):
    if s == 0:
        return cur
    row = lax.broadcasted_iota(jnp.int32, cur.shape, 0)
    return jnp.where(row < SUB - s, pltpu.roll(cur, SUB - s, axis=0), pltpu.roll(nxt, SUB - s, axis=0))


def _silu_parts(c):
    sg = _sigmoid(c)
    return c * sg, sg * (1.0 + c * (1.0 - sg))


def _head_sum(x):
    rows, c = x.shape
    parts = []
    for h in range(c // HEAD_DIM):
        s = jnp.sum(x[:, h * HEAD_DIM:(h + 1) * HEAD_DIM], axis=-1, keepdims=True)
        parts.append(jnp.broadcast_to(s, (rows, HEAD_DIM)))
    return parts[0] if len(parts) == 1 else jnp.concatenate(parts, axis=-1)


def _log1p(y):
    u = 1.0 + y
    d = u - 1.0
    return jnp.where(d == 0.0, y, jnp.log(u) * (y / jnp.where(d == 0.0, 1.0, d)))


def _softplus(x):
    return jnp.maximum(x, 0.0) + _log1p(jnp.exp(-jnp.abs(x)))


def gdn_pre_fwd(p5, conv_w, alog_b, dtb_b, *, tm, cb):
    _, lp, width = p5.shape
    taps = conv_w.shape[1]

    def body(x_ref, halo_ref, w_ref, al_ref, dt_ref, o_ref, xs):
        i = pl.program_id(1)
        for s in range(3):
            xs[s, 0:HALO, :] = jnp.where(i > 0, halo_ref[s], 0.0)
            xs[s, HALO:, :] = x_ref[s]
            c = _conv_fwd(xs.at[s], lambda j, s=s: w_ref[s, j:j + 1, :], taps, tm)
            y, _ = _silu_parts(c)
            if s < 2:
                y = y * lax.rsqrt(_head_sum(y * y) + L2_EPS)
                if s == 0:
                    y = y * Q_SCALE
            o_ref[s] = y
        o_ref[3] = _sigmoid(x_ref[3])
        o_ref[4] = -jnp.exp(al_ref[...]) * _softplus(x_ref[4] + dt_ref[...])

    return pl.pallas_call(
        body,
        name="gdn_pre_fwd",
        grid=(width // cb, lp // tm),
        in_specs=[
            pl.BlockSpec((5, tm, cb), lambda j, i: (0, i, j)),
            pl.BlockSpec((3, HALO, cb), lambda j, i: (0, _halo_index(i, tm), j)),
            pl.BlockSpec((3, taps, cb), lambda j, i: (0, 0, j)),
            pl.BlockSpec((1, cb), lambda j, i: (0, j)),
            pl.BlockSpec((1, cb), lambda j, i: (0, j)),
        ],
        out_specs=pl.BlockSpec((5, tm, cb), lambda j, i: (0, i, j)),
        out_shape=jax.ShapeDtypeStruct((5, lp, width), F32),
        scratch_shapes=[pltpu.VMEM((3, tm + HALO, cb), F32)],
        compiler_params=_params(("arbitrary", "arbitrary")),
    )(p5, p5, conv_w, alog_b, dtb_b)


def gdn_pre_bwd(p5, dqkvbg, conv_w, alog_b, dtb_b, *, tm, cb):
    _, lp, width = p5.shape
    taps = conv_w.shape[1]
    last = lp // tm - 1

    def body(x_ref, halo_ref, d_ref, w_ref, al_ref, dt_ref, dx_ref, dw_ref, dsc_ref, xs, dcs, carry):
        step = pl.program_id(1)
        tile = last - step

        @pl.when(step == 0)
        def _():
            carry[...] = jnp.zeros_like(carry)
            dw_ref[...] = jnp.zeros_like(dw_ref)
            dsc_ref[...] = jnp.zeros_like(dsc_ref)

        for s in range(3):
            w = lambda j, s=s: w_ref[s, j:j + 1, :]
            xs[s, 0:HALO, :] = jnp.where(tile > 0, halo_ref[s], 0.0)
            xs[s, HALO:, :] = x_ref[s]
            c = _conv_fwd(xs.at[s], w, taps, tm)
            y, dsilu = _silu_parts(c)
            dy = d_ref[s]
            if s < 2:
                rn = lax.rsqrt(_head_sum(y * y) + L2_EPS)
                yn = y * rn
                if s == 0:
                    dy = dy * Q_SCALE
                dy = rn * (dy - yn * _head_sum(dy * yn))
            dc = dy * dsilu
            dcs[s, 0:tm, :] = dc
            dcs[s, tm:, :] = carry[s]
            dx_ref[s] = _conv_bwd_x(dcs.at[s], w, taps, tm).astype(dx_ref.dtype)
            carry[s] = dc[0:HALO, :]
            for j in range(taps):
                dw_ref[s, j] += _row_partial(dc * xs[s, pl.ds(HALO - taps + 1 + j, tm), :])
        beta = _sigmoid(x_ref[3])
        dx_ref[3] = (d_ref[3] * beta * (1.0 - beta)).astype(dx_ref.dtype)
        z = x_ref[4] + dt_ref[...]
        dg = d_ref[4] * -jnp.exp(al_ref[...])
        da = dg * _sigmoid(z)
        dx_ref[4] = da.astype(dx_ref.dtype)
        dsc_ref[0] += _row_partial(dg * _softplus(z))
        dsc_ref[1] += _row_partial(da)

    tile_spec = pl.BlockSpec((5, tm, cb), lambda j, i: (0, last - i, j))
    return pl.pallas_call(
        body,
        name="gdn_pre_bwd",
        grid=(width // cb, lp // tm),
        in_specs=[
            tile_spec,
            pl.BlockSpec((3, HALO, cb), lambda j, i: (0, _halo_index(last - i, tm), j)),
            tile_spec,
            pl.BlockSpec((3, taps, cb), lambda j, i: (0, 0, j)),
            pl.BlockSpec((1, cb), lambda j, i: (0, j)),
            pl.BlockSpec((1, cb), lambda j, i: (0, j)),
        ],
        out_specs=[
            tile_spec,
            pl.BlockSpec((3, taps, SUB, cb), lambda j, i: (0, 0, 0, j)),
            pl.BlockSpec((2, SUB, cb), lambda j, i: (0, 0, j)),
        ],
        out_shape=[
            jax.ShapeDtypeStruct((5, lp, width), BF16),
            jax.ShapeDtypeStruct((3, taps, SUB, width), F32),
            jax.ShapeDtypeStruct((2, SUB, width), F32),
        ],
        scratch_shapes=[
            pltpu.VMEM((3, tm + HALO, cb), F32),
            pltpu.VMEM((3, tm + HALO, cb), F32),
            pltpu.VMEM((3, HALO, cb), F32),
        ],
        compiler_params=_params(("arbitrary", "arbitrary")),
    )(p5, p5, dqkvbg, conv_w, alog_b, dtb_b)


def gdn_post_fwd(o, z, nw_b, *, tm):
    _, lp, width = o.shape

    def body(o_ref, z_ref, nw_ref, y_ref):
        ov = o_ref[...]
        rn = lax.rsqrt(_head_sum(ov * ov) * (1.0 / HEAD_DIM) + RMS_EPS)
        gate, _ = _silu_parts(z_ref[...])
        y_ref[...] = (ov * rn * nw_ref[...] * gate).astype(y_ref.dtype)

    row = pl.BlockSpec((None, tm, width), lambda i: (0, i, 0))
    return pl.pallas_call(
        body,
        name="gdn_post_fwd",
        grid=(lp // tm,),
        in_specs=[row, row, pl.BlockSpec((1, width), lambda i: (0, 0))],
        out_specs=row,
        out_shape=jax.ShapeDtypeStruct((1, lp, width), BF16),
        compiler_params=_params(("arbitrary",)),
    )(o, z, nw_b)


def gdn_post_bwd(o, z, dy, nw_b, *, tm):
    _, lp, width = o.shape

    def body(o_ref, z_ref, dy_ref, nw_ref, do_ref, dz_ref, dnw_ref):
        ov = o_ref[...]
        rn = lax.rsqrt(_head_sum(ov * ov) * (1.0 / HEAD_DIM) + RMS_EPS)
        yn = ov * rn
        gate, dgate = _silu_parts(z_ref[...])
        d_on = dy_ref[...] * gate
        dz_ref[...] = (dy_ref[...] * yn * nw_ref[...] * dgate).astype(dz_ref.dtype)
        a = d_on * nw_ref[...]
        do_ref[...] = rn * (a - yn * (_head_sum(a * yn) * (1.0 / HEAD_DIM)))

        @pl.when(pl.program_id(0) == 0)
        def _():
            dnw_ref[...] = jnp.zeros_like(dnw_ref)

        dnw_ref[...] += _row_partial(d_on * yn)

    row = pl.BlockSpec((None, tm, width), lambda i: (0, i, 0))
    return pl.pallas_call(
        body,
        name="gdn_post_bwd",
        grid=(lp // tm,),
        in_specs=[row, row, row, pl.BlockSpec((1, width), lambda i: (0, 0))],
        out_specs=[row, row, pl.BlockSpec((8, width), lambda i: (0, 0))],
        out_shape=[jax.ShapeDtypeStruct((1, lp, width), F32), jax.ShapeDtypeStruct((1, lp, width), BF16),
                   jax.ShapeDtypeStruct((8, width), F32)],
        compiler_params=_params(("arbitrary",)),
    )(o, z, dy, nw_b)


def head_lane_sum(x):
    s_n, rows, width = x.shape

    def body(x_ref, o_ref):
        lane = lax.broadcasted_iota(jnp.int32, (rows, HEAD_DIM), 1)
        acc = jnp.zeros((rows, HEAD_DIM), F32)
        for h in range(width // HEAD_DIM):
            s = jnp.sum(x_ref[:, h * HEAD_DIM:(h + 1) * HEAD_DIM], axis=-1, keepdims=True)
            acc = jnp.where(lane == h, s, acc)
        o_ref[...] = acc

    return pl.pallas_call(
        body,
        name="head_lane_sum",
        grid=(s_n,),
        in_specs=[pl.BlockSpec((None, rows, width), lambda s: (s, 0, 0))],
        out_specs=pl.BlockSpec((None, rows, HEAD_DIM), lambda s: (s, 0, 0)),
        out_shape=jax.ShapeDtypeStruct((s_n, rows, HEAD_DIM), F32),
        compiler_params=_params(("arbitrary",)),
    )(x)


def ffn_act_fwd(up, conv_w, *, tm, name):
    _, lp, c_w = up.shape
    taps = conv_w.shape[1]

    def body(u_ref, halo_ref, g_ref, w_ref, o_ref):
        first_tile = pl.program_id(1) == 0

        def strip(cur, prev, rows, cs):
            conv = w_ref[taps - 1:taps, cs] * cur
            for j in range(taps - 1):
                conv += w_ref[j:j + 1, cs] * _shift_down(cur, prev, taps - 1 - j)
            y, _ = _silu_parts(conv)
            return y * g_ref[rows, cs]

        def pair(r0, above_of):
            top, bot = _pair_rows(r0)
            for c0 in range(0, c_w, LANES):
                cs = slice(c0, c0 + LANES)
                cur_t, cur_b = u_ref[top, cs], u_ref[bot, cs]
                out = [strip(cur_t, above_of(cs), top, cs), strip(cur_b, cur_t, bot, cs)]
                o_ref[pl.ds(r0, PAIR), cs] = jnp.concatenate(out, axis=0).astype(o_ref.dtype)

        pair(0, lambda cs: jnp.where(first_tile, 0.0, halo_ref[:, cs]))

        def loop_body(s, carry):
            r0 = pl.multiple_of(s * PAIR, PAIR)
            pair(r0, lambda cs: u_ref[pl.ds(pl.multiple_of(r0 - SUB, SUB), SUB), cs])
            return carry

        lax.fori_loop(1, tm // PAIR, loop_body, 0, unroll=STRIP_UNROLL)

    return pl.pallas_call(
        body,
        name=name,
        grid=(2, lp // tm),
        in_specs=[
            pl.BlockSpec((None, tm, c_w), lambda s, i: (s, i, 0)),
            pl.BlockSpec((None, HALO, c_w), lambda s, i: (s, _halo_index(i, tm), 0)),
            pl.BlockSpec((None, tm, c_w), lambda s, i: (2 + s, i, 0)),
            pl.BlockSpec((None, taps, c_w), lambda s, i: (s, 0, 0)),
        ],
        out_specs=pl.BlockSpec((None, tm, c_w), lambda s, i: (s, i, 0)),
        out_shape=jax.ShapeDtypeStruct((2, lp, c_w), BF16),
        compiler_params=_params(("arbitrary", "arbitrary")),
    )(up, up, up, conv_w)


def ffn_act_bwd(up, dact, conv_w, *, tm, name):
    _, lp, c_w = up.shape
    taps = conv_w.shape[1]
    last = lp // tm - 1
    n_pairs = tm // PAIR

    def body(u_ref, halo_ref, g_ref, d_ref, w_ref, dup_ref, dw_ref, below):
        step = pl.program_id(1)
        first_tile = step == last

        @pl.when(step == 0)
        def _():
            below[...] = jnp.zeros_like(below)
            dw_ref[...] = jnp.zeros_like(dw_ref)

        def strip(cur, prev, rows, cs, nxt):
            shifted = [_shift_down(cur, prev, taps - 1 - j) for j in range(taps)]
            conv = w_ref[0:1, cs] * shifted[0]
            for j in range(1, taps):
                conv += w_ref[j:j + 1, cs] * shifted[j]
            y, dsilu = _silu_parts(conv)
            d = d_ref[rows, cs]
            dc = d * g_ref[rows, cs] * dsilu
            dx = w_ref[taps - 1:taps, cs] * dc
            for j in range(taps - 1):
                dx += w_ref[j:j + 1, cs] * _shift_up(dc, nxt, taps - 1 - j)
            return dx, d * y, dc, [dc * s for s in shifted]

        def pair(r0, above_of):
            top, bot = _pair_rows(r0)
            both = pl.ds(r0, PAIR)
            for c0 in range(0, c_w, LANES):
                cs = slice(c0, c0 + LANES)
                cur_t, cur_b = u_ref[top, cs], u_ref[bot, cs]
                dx_b, dg_b, dc_b, dw_b = strip(cur_b, cur_t, bot, cs, below[:, cs])
                dx_t, dg_t, dc_t, dw_t = strip(cur_t, above_of(cs), top, cs, dc_b)
                below[:, cs] = dc_t
                dup_ref[0, both, cs] = jnp.concatenate([dx_t, dx_b], axis=0).astype(dup_ref.dtype)
                dup_ref[1, both, cs] = jnp.concatenate([dg_t, dg_b], axis=0).astype(dup_ref.dtype)
                for j in range(taps):
                    dw_ref[j, :, cs] += dw_t[j] + dw_b[j]

        def loop_body(it, carry):
            r0 = pl.multiple_of((n_pairs - 1 - it) * PAIR, PAIR)
            pair(r0, lambda cs: u_ref[pl.ds(pl.multiple_of(r0 - SUB, SUB), SUB), cs])
            return carry

        lax.fori_loop(0, n_pairs - 1, loop_body, 0, unroll=STRIP_UNROLL)
        pair(0, lambda cs: jnp.where(first_tile, 0.0, halo_ref[:, cs]))

    return pl.pallas_call(
        body,
        name=name,
        grid=(2, lp // tm),
        in_specs=[
            pl.BlockSpec((None, tm, c_w), lambda s, i: (s, last - i, 0)),
            pl.BlockSpec((None, HALO, c_w), lambda s, i: (s, _halo_index(last - i, tm), 0)),
            pl.BlockSpec((None, tm, c_w), lambda s, i: (2 + s, last - i, 0)),
            pl.BlockSpec((None, tm, c_w), lambda s, i: (s, last - i, 0)),
            pl.BlockSpec((None, taps, c_w), lambda s, i: (s, 0, 0)),
        ],
        out_specs=[
            pl.BlockSpec((2, None, tm, c_w), lambda s, i: (0, s, last - i, 0)),
            pl.BlockSpec((None, taps, SUB, c_w), lambda s, i: (s, 0, 0, 0)),
        ],
        out_shape=[jax.ShapeDtypeStruct((2, 2, lp, c_w), BF16), jax.ShapeDtypeStruct((2, taps, SUB, c_w), F32)],
        scratch_shapes=[pltpu.VMEM((SUB, c_w), F32)],
        compiler_params=_params(("arbitrary", "arbitrary")),
    )(up, up, up, dact, conv_w)


def sc_fwd(pb, conv_w, *, tm, cb):
    _, lp, width = pb.shape
    taps = conv_w.shape[0]

    def body(x_ref, halo_ref, w_ref, o_ref):
        first_tile = pl.program_id(1) == 0

        def strip(cur, prev, rows, cs):
            conv = w_ref[taps - 1:taps, cs] * cur
            for j in range(taps - 1):
                conv += w_ref[j:j + 1, cs] * _shift_down(cur, prev, taps - 1 - j)
            return x_ref[0, rows, cs] * conv

        def pair(r0, above_of):
            top, bot = _pair_rows(r0)
            for c0 in range(0, cb, LANES):
                cs = slice(c0, c0 + LANES)
                cur_t = x_ref[1, top, cs] * x_ref[2, top, cs]
                cur_b = x_ref[1, bot, cs] * x_ref[2, bot, cs]
                out = [strip(cur_t, above_of(cs), top, cs), strip(cur_b, cur_t, bot, cs)]
                o_ref[pl.ds(r0, PAIR), cs] = jnp.concatenate(out, axis=0).astype(o_ref.dtype)

        pair(0, lambda cs: jnp.where(first_tile, 0.0, halo_ref[1, :, cs] * halo_ref[2, :, cs]))

        def loop_body(k, carry):
            r0 = pl.multiple_of(k * PAIR, PAIR)
            before = pl.ds(pl.multiple_of(r0 - SUB, SUB), SUB)
            pair(r0, lambda cs: x_ref[1, before, cs] * x_ref[2, before, cs])
            return carry

        lax.fori_loop(1, tm // PAIR, loop_body, 0, unroll=STRIP_UNROLL)

    return pl.pallas_call(
        body,
        name="sc_fwd",
        grid=(width // cb, lp // tm),
        in_specs=[
            pl.BlockSpec((3, tm, cb), lambda j, i: (0, i, j)),
            pl.BlockSpec((3, HALO, cb), lambda j, i: (0, _halo_index(i, tm), j)),
            pl.BlockSpec((taps, cb), lambda j, i: (0, j)),
        ],
        out_specs=pl.BlockSpec((None, tm, cb), lambda j, i: (0, i, j)),
        out_shape=jax.ShapeDtypeStruct((1, lp, width), BF16),
        compiler_params=_params(("arbitrary", "arbitrary")),
    )(pb, pb, conv_w)


def sc_bwd(pb, ds, conv_w, *, tm, cb):
    _, lp, width = pb.shape
    taps = conv_w.shape[0]
    last = lp // tm - 1
    n_pairs = tm // PAIR

    def body(x_ref, halo_ref, d_ref, w_ref, dx_ref, dw_ref, below):
        step = pl.program_id(1)
        first_tile = step == last

        @pl.when(step == 0)
        def _():
            below[...] = jnp.zeros_like(below)
            dw_ref[...] = jnp.zeros_like(dw_ref)

        def strip(cur, prev, rows, cs, nxt):
            gate, left, right = x_ref[0, rows, cs], x_ref[1, rows, cs], x_ref[2, rows, cs]
            shifted = [_shift_down(cur, prev, taps - 1 - j) for j in range(taps)]
            conv = w_ref[0:1, cs] * shifted[0]
            for j in range(1, taps):
                conv += w_ref[j:j + 1, cs] * shifted[j]
            d = d_ref[rows, cs]
            dc = d * gate
            dp = w_ref[taps - 1:taps, cs] * dc
            for j in range(taps - 1):
                dp += w_ref[j:j + 1, cs] * _shift_up(dc, nxt, taps - 1 - j)
            return d * conv, dp * right, dp * left, dc, [dc * s for s in shifted]

        def pair(r0, above_of):
            top, bot = _pair_rows(r0)
            both = pl.ds(r0, PAIR)
            for c0 in range(0, cb, LANES):
                cs = slice(c0, c0 + LANES)
                cur_t = x_ref[1, top, cs] * x_ref[2, top, cs]
                cur_b = x_ref[1, bot, cs] * x_ref[2, bot, cs]
                *dx_b, dc_b, dw_b = strip(cur_b, cur_t, bot, cs, below[:, cs])
                *dx_t, dc_t, dw_t = strip(cur_t, above_of(cs), top, cs, dc_b)
                below[:, cs] = dc_t
                for s in range(3):
                    dx_ref[s, both, cs] = jnp.concatenate([dx_t[s], dx_b[s]], axis=0).astype(dx_ref.dtype)
                for j in range(taps):
                    dw_ref[j, :, cs] += dw_t[j] + dw_b[j]

        def loop_body(it, carry):
            r0 = pl.multiple_of((n_pairs - 1 - it) * PAIR, PAIR)
            before = pl.ds(pl.multiple_of(r0 - SUB, SUB), SUB)
            pair(r0, lambda cs: x_ref[1, before, cs] * x_ref[2, before, cs])
            return carry

        lax.fori_loop(0, n_pairs - 1, loop_body, 0, unroll=STRIP_UNROLL)
        pair(0, lambda cs: jnp.where(first_tile, 0.0, halo_ref[1, :, cs] * halo_ref[2, :, cs]))

    tile_spec = pl.BlockSpec((3, tm, cb), lambda j, i: (0, last - i, j))
    return pl.pallas_call(
        body,
        name="sc_bwd",
        grid=(width // cb, lp // tm),
        in_specs=[
            tile_spec,
            pl.BlockSpec((3, HALO, cb), lambda j, i: (0, _halo_index(last - i, tm), j)),
            pl.BlockSpec((None, tm, cb), lambda j, i: (0, last - i, j)),
            pl.BlockSpec((taps, cb), lambda j, i: (0, j)),
        ],
        out_specs=[tile_spec, pl.BlockSpec((taps, SUB, cb), lambda j, i: (0, 0, j))],
        out_shape=[jax.ShapeDtypeStruct((3, lp, width), BF16), jax.ShapeDtypeStruct((taps, SUB, width), F32)],
        scratch_shapes=[pltpu.VMEM((SUB, cb), F32)],
        compiler_params=_params(("arbitrary", "arbitrary")),
    )(pb, pb, ds, conv_w)


TILE_BYTES = 1536 * 1024


def _rows_tile(rows, cols, multiple=8):
    if rows * cols * 4 <= TILE_BYTES or rows % multiple:
        return rows
    best = multiple
    for t in range(multiple, rows + 1, multiple):
        if rows % t == 0 and t * cols * 4 <= TILE_BYTES:
            best = t
    return best


def pair_sum(g, landed, core, out_dtype, name):
    _, rows, cols = g.shape
    half = rows // 2
    tr = _rows_tile(half, cols, 16)
    nb = half // tr

    def body(c_ref, g_ref, l_ref, o_ref):
        o_ref[...] = (g_ref[...] + l_ref[...]).astype(out_dtype)

    return pl.pallas_call(
        body,
        name=name,
        grid_spec=pltpu.PrefetchScalarGridSpec(
            num_scalar_prefetch=1,
            grid=(4, nb),
            in_specs=[
                pl.BlockSpec((None, tr, cols), lambda s, i, c: (s, c[0] * nb + i, 0)),
                pl.BlockSpec((None, tr, cols), lambda s, i, c: (s, i, 0)),
            ],
            out_specs=pl.BlockSpec((None, tr, cols), lambda s, i, c: (s, i, 0)),
        ),
        out_shape=jax.ShapeDtypeStruct((4, half, cols), out_dtype),
        compiler_params=_params(("arbitrary", "arbitrary")),
    )(core, g, landed)


def chip_sum(x, name):
    _, rows, cols = x.shape
    tr = _rows_tile(rows, cols, 16)

    def body(x0, x1, x2, x3, o_ref):
        acc = x0[...].astype(F32) + x1[...].astype(F32)
        o_ref[...] = (acc + x2[...].astype(F32)) + x3[...].astype(F32)

    return pl.pallas_call(
        body,
        name=name,
        grid=(rows // tr,),
        in_specs=[pl.BlockSpec((None, tr, cols), lambda i, k=k: (k, i, 0)) for k in range(4)],
        out_specs=pl.BlockSpec((tr, cols), lambda i: (i, 0)),
        out_shape=jax.ShapeDtypeStruct((rows, cols), F32),
        compiler_params=_params(("arbitrary",)),
    )(x, x, x, x)


def adamw(w, g, m, v, name):
    shape = w.shape
    cols = shape[-1]
    rows = w.size // cols
    tr = _rows_tile(rows, cols)

    def body(w_ref, g_ref, m_ref, v_ref, d_ref, m2_ref, v2_ref):
        gv = g_ref[...]
        m2 = ADAM_B1 * m_ref[...] + (1.0 - ADAM_B1) * gv
        v2 = ADAM_B2 * v_ref[...] + (1.0 - ADAM_B2) * (gv * gv)
        m_hat = m2 / (1.0 - ADAM_B1 ** ADAM_STEP)
        v_hat = v2 / (1.0 - ADAM_B2 ** ADAM_STEP)
        d_ref[...] = -ADAM_LR * (m_hat / (jnp.sqrt(v_hat) + ADAM_EPS) + ADAM_WD * w_ref[...])
        m2_ref[...] = m2
        v2_ref[...] = v2

    spec = pl.BlockSpec((tr, cols), lambda i: (i, 0))
    outs = pl.pallas_call(
        body,
        name=name,
        grid=(rows // tr,),
        in_specs=[spec] * 4,
        out_specs=[spec] * 3,
        out_shape=[jax.ShapeDtypeStruct((rows, cols), F32)] * 3,
        compiler_params=_params(("arbitrary",)),
    )(*[t.reshape(rows, cols) for t in (w, g, m, v)])
    return tuple(o.reshape(shape) for o in outs)


MESH_ID = pl.DeviceIdType.MESH
ANY = pl.BlockSpec(memory_space=pl.ANY)


def _place():
    x, y, c = lax.axis_index("x"), lax.axis_index("y"), lax.axis_index("c")
    other_chips = [(1 - x, y), (x, 1 - y), (1 - x, 1 - y)]
    return x, y, c, other_chips


def all_gather_shards(bufs, name):
    n = len(bufs)

    def body(*refs):
        x_refs, o_refs = refs[:n], refs[n:2 * n]
        copies = _gather_copies(x_refs, o_refs, *refs[2 * n:])
        _gather_start(copies)
        _gather_finish(copies)

    outs = pl.pallas_call(
        body,
        name=name,
        in_specs=[ANY] * n,
        out_specs=[ANY] * n,
        out_shape=_gather_out_shapes(bufs),
        scratch_shapes=_gather_sems(n),
    )(*bufs)
    return _set_own_slots(outs, bufs)


def _gather_out_shapes(bufs):
    return [jax.ShapeDtypeStruct((4,) + b.shape, b.dtype) for b in bufs]


def _gather_sems(n):
    return [pltpu.SemaphoreType.DMA((6 * n,)), pltpu.SemaphoreType.DMA((6 * n,))]


def _set_own_slots(outs, bufs):
    if not outs:
        return []
    me = 2 * lax.axis_index("x") + lax.axis_index("y")
    return [lax.dynamic_update_index_in_dim(o, b, me, 0) for o, b in zip(outs, bufs)]


def _gather_copies(x_refs, o_refs, send_sems, recv_sems):
    x, y, c, chips = _place()
    me = 2 * x + y
    sibling = (x, y, 1 - c)

    def part(a, slot, hf):
        half = x_refs[a].shape[0] // 2
        return o_refs[a].at[slot, pl.ds(hf * half, half), :]

    def mine(a):
        half = x_refs[a].shape[0] // 2
        return x_refs[a].at[pl.ds(c * half, half), :]

    def copy(k, src, dst, to):
        return pltpu.make_async_remote_copy(src_ref=src, dst_ref=dst, send_sem=send_sems.at[k],
                                            recv_sem=recv_sems.at[k], device_id=to, device_id_type=MESH_ID)

    sends, arrivals, passes, passed = [], [], [], []
    for a in range(len(x_refs)):
        for j, (px, py) in enumerate(chips):
            landed, theirs = part(a, 2 * px + py, c), part(a, 2 * px + py, 1 - c)
            sends.append(copy(6 * a + j, mine(a), part(a, me, c), (px, py, c)))
            arrivals.append(copy(6 * a + j, mine(a), landed, (px, py, c)))
            passes.append(copy(6 * a + 3 + j, landed, landed, sibling))
            passed.append(copy(6 * a + 3 + j, theirs, theirs, sibling))
    return sends, arrivals, passes, passed


def _gather_start(copies):
    for cp in copies[0]:
        cp.start()


def _gather_finish(copies):
    sends, arrivals, passes, passed = copies
    for arrival, cp in zip(arrivals, passes):
        arrival.wait_recv()
        cp.start()
    for cp in passed:
        cp.wait_recv()
    for cp in sends + passes:
        cp.wait_send()


def swap_halves(bufs, name):
    n = len(bufs)

    def body(*refs):
        x_refs, o_refs = refs[:n], refs[n:2 * n]
        send_sems, recv_sems = refs[2 * n:]
        x, y, c, _ = _place()
        copies = []
        for a in range(n):
            half = bufs[a].shape[1] // 2
            cp = pltpu.make_async_remote_copy(src_ref=x_refs[a].at[:, pl.ds((1 - c) * half, half), :], dst_ref=o_refs[a],
                                              send_sem=send_sems.at[a], recv_sem=recv_sems.at[a],
                                              device_id=(x, y, 1 - c), device_id_type=MESH_ID)
            cp.start()
            copies.append(cp)
        for cp in copies:
            cp.wait()

    return pl.pallas_call(
        body,
        name=name,
        in_specs=[ANY] * n,
        out_specs=[ANY] * n,
        out_shape=[jax.ShapeDtypeStruct((4, b.shape[1] // 2, b.shape[2]), b.dtype) for b in bufs],
        scratch_shapes=[pltpu.SemaphoreType.DMA((n,)), pltpu.SemaphoreType.DMA((n,))],
    )(*bufs)


def scatter_to_chips(bufs, name):
    n = len(bufs)

    def body(*refs):
        x_refs, o_refs = refs[:n], refs[n:2 * n]
        copies = _scatter_copies(x_refs, o_refs, *refs[2 * n:])
        _scatter_start(copies)
        _scatter_finish(copies)

    outs = pl.pallas_call(
        body,
        name=name,
        in_specs=[ANY] * n,
        out_specs=[ANY] * n,
        out_shape=[jax.ShapeDtypeStruct(b.shape, b.dtype) for b in bufs],
        scratch_shapes=_scatter_sems(n),
    )(*bufs)
    return _keep_own_slots(outs, bufs)


def _scatter_sems(n):
    return [pltpu.SemaphoreType.DMA((3 * n,)), pltpu.SemaphoreType.DMA((3 * n,))]


def _keep_own_slots(outs, bufs):
    if not outs:
        return []
    me = 2 * lax.axis_index("x") + lax.axis_index("y")
    return [lax.dynamic_update_index_in_dim(o, lax.dynamic_index_in_dim(b, me, 0, keepdims=False), me, 0)
            for o, b in zip(outs, bufs)]


def _scatter_copies(x_refs, o_refs, send_sems, recv_sems):
    x, y, c, chips = _place()
    me = 2 * x + y

    def copy(a, j, src_slot, dst_slot, px, py):
        return pltpu.make_async_remote_copy(src_ref=x_refs[a].at[src_slot], dst_ref=o_refs[a].at[dst_slot],
                                            send_sem=send_sems.at[3 * a + j], recv_sem=recv_sems.at[3 * a + j],
                                            device_id=(px, py, c), device_id_type=MESH_ID)

    sends = [copy(a, j, 2 * px + py, me, px, py) for a in range(len(x_refs)) for j, (px, py) in enumerate(chips)]
    arrivals = [copy(a, j, me, 2 * px + py, px, py) for a in range(len(x_refs)) for j, (px, py) in enumerate(chips)]
    return sends, arrivals


def _scatter_start(copies):
    for cp in copies[0]:
        cp.start()


def _scatter_finish(copies):
    for cp in copies[1]:
        cp.wait_recv()
    for cp in copies[0]:
        cp.wait_send()


def share_halves(groups, name):
    bufs = [b for grp in groups for b in grp]
    where = [(gi, li) for gi, grp in enumerate(groups) for li in range(len(grp))]
    n = len(bufs)

    def body(*refs):
        x_refs, o_refs = refs[:n], refs[n:n + len(groups)]
        send_sems, recv_sems = refs[n + len(groups):]
        x, y, c, _ = _place()
        sent, arrive = [], []
        for a, (gi, li) in enumerate(where):

            def copy(hf, a=a, gi=gi, li=li):
                return pltpu.make_async_remote_copy(src_ref=x_refs[a], dst_ref=o_refs[gi].at[li, hf],
                                                    send_sem=send_sems.at[a], recv_sem=recv_sems.at[a],
                                                    device_id=(x, y, 1 - c), device_id_type=MESH_ID)

            sent.append(copy(c))
            arrive.append(copy(1 - c))
        for cp in sent:
            cp.start()
        for cp in arrive:
            cp.wait_recv()
        for cp in sent:
            cp.wait_send()

    outs = pl.pallas_call(
        body,
        name=name,
        in_specs=[ANY] * n,
        out_specs=[ANY] * len(groups),
        out_shape=[jax.ShapeDtypeStruct((len(grp), 2) + grp[0].shape, grp[0].dtype) for grp in groups],
        scratch_shapes=[pltpu.SemaphoreType.DMA((n,)), pltpu.SemaphoreType.DMA((n,))],
    )(*bufs)
    c = lax.axis_index("c")
    full = [lax.dynamic_update_index_in_dim(o, jnp.stack(grp), c, 1) for o, grp in zip(outs, groups)]
    return [t.reshape(t.shape[0], 2 * t.shape[2], t.shape[3]) for t in full]


def pair_sums(bufs, dtypes, tag):
    core = lax.axis_index("c").astype(jnp.int32).reshape(1)
    landed = swap_halves(bufs, "rs_pair_" + tag)
    return [pair_sum(b, l, core, dt, "rs_pair_sum_%s%d" % (tag, i)) for i, (b, l, dt) in enumerate(zip(bufs, landed, dtypes))]


def _row_tiles(length):
    return (640, 320) if length > 2048 else (128, 64)


def _divisor_tile(rows, target):
    return max(t for t in range(8, min(rows, target) + 1, 8) if rows % t == 0)


def _local_step(x, target, wt, late_shards, layout_late, reduce_early):
    seq, d = x.shape
    length = N_META + seq
    tm, tm_ffn = _row_tiles(length)
    lp = -(-length // tm) * tm
    tail = jnp.zeros((lp - length, d), F32)
    h0 = jnp.concatenate([wt["meta"], x, tail], axis=0)[None]
    tgt = jnp.concatenate([jnp.zeros((N_META, d), F32), target, tail], axis=0)
    nn = functools.partial(mm_nn, tm=_divisor_tile(lp, 1664))
    nt = functools.partial(mm_nt, tm=_divisor_tile(lp, 1040))
    tn = functools.partial(mm_tn, tm=_divisor_tile(lp, 1664), rb=256)
    ln_g = [wt["ln_mix_g"][0:1], wt["ln_ffn_g"][0:1], wt["ln_mix_g"][1:2], wt["ln_ffn_g"][1:2]]
    ln_b = [wt["ln_mix_b"][0:1], wt["ln_ffn_b"][0:1], wt["ln_mix_b"][1:2], wt["ln_ffn_b"][1:2]]

    p5 = nn(h0, wt["a5"], name="a_in5")
    pz = nn(h0, wt["az"], name="a_inz")
    qkvbg = gdn_pre_fwd(p5, wt["a_conv3"], wt["alog_b"], wt["dtb_b"], tm=tm, cb=2 * HEAD_DIM)
    o, states, tinv, late_stacks = gdn_chunk_fwd(qkvbg, late_shards)
    wt = {**wt, **layout_late(late_stacks)}
    onz = gdn_post_fwd(o[None], pz, wt["anorm_b"], tm=tm)
    r1, h1 = ln_fwd(h0, nn(onz, wt["a_out"], name="a_out"), ln_g[0], ln_b[0], tm=tm, name="ln1")
    up0 = nn(h1, wt["up"][0], name="up0")
    act0 = ffn_act_fwd(up0, wt["fconv"][0], tm=tm_ffn, name="ffn_act0")
    r2, h2 = ln_fwd(h1, nn(act0, wt["down"][0], name="down0"), ln_g[1], ln_b[1], tm=tm, name="ln2")
    pb = nn(h2, wt["b_in"], name="b_in")
    sc = sc_fwd(pb, wt["b_conv"], tm=tm_ffn, cb=d)
    r3, h3 = ln_fwd(h2, nn(sc, wt["b_out"], name="b_out"), ln_g[2], ln_b[2], tm=tm, name="ln3")
    up1 = nn(h3, wt["up"][1], name="up1")
    act1 = ffn_act_fwd(up1, wt["fconv"][1], tm=tm_ffn, name="ffn_act1")
    r4, h4 = ln_fwd(h3, nn(act1, wt["down"][1], name="down1"), ln_g[3], ln_b[3], tm=tm, name="ln4")

    dh4, loss_part = loss_grad(h4, tgt, first=N_META, count=seq, tm=tm)

    grads = {}
    dr4, dgb4 = ln_bwd(r4, dh4, ln_g[3], tm=tm, name="ln4_bwd")
    d_down1 = tn(act1, dr4, name="d_down1")
    dact1 = nt(dr4, wt["down"][1], name="d_act1")
    dup1, dfconv1 = ffn_act_bwd(up1, dact1, wt["fconv"][1], tm=tm_ffn, name="ffn_act1_bwd")
    dup1 = dup1.reshape(up1.shape)
    d_up1 = tn(h3, dup1, name="d_up1")
    dh3 = nt(dup1, wt["up"][1], res=dr4, res_scale=ALPHA, name="d_h3")

    dr3, dgb3 = ln_bwd(r3, dh3, ln_g[2], tm=tm, name="ln3_bwd")
    d_bout = tn(sc, dr3, name="d_b_out")
    dsc = nt(dr3, wt["b_out"], name="d_sc")
    dpb, dbconv = sc_bwd(pb, dsc, wt["b_conv"], tm=tm_ffn, cb=d)
    d_bin = tn(h2, dpb, name="d_b_in")
    dh2 = nt(dpb, wt["b_in"], res=dr3, res_scale=ALPHA, name="d_h2")

    dr2, dgb2 = ln_bwd(r2, dh2, ln_g[1], tm=tm, name="ln2_bwd")
    d_down0 = tn(act0, dr2, name="d_down0")
    dact0 = nt(dr2, wt["down"][0], name="d_act0")
    dup0, dfconv0 = ffn_act_bwd(up0, dact0, wt["fconv"][0], tm=tm_ffn, name="ffn_act0_bwd")
    dup0 = dup0.reshape(up0.shape)
    d_up0 = tn(h1, dup0, name="d_up0")
    dh1 = nt(dup0, wt["up"][0], res=dr2, res_scale=ALPHA, name="d_h1")
    grads["b_w_in"] = [d_bin[0].transpose(1, 0, 2).reshape(d, 4, 3 * d // 4).transpose(1, 0, 2)]
    grads["b_w_out"] = [d_bout.reshape(4, d // 4, d)]
    grads["ffn_w_up"] = [d_up0[0], d_up1[0]]
    grads["ffn_w_down"] = [t.reshape(4, -1, d) for t in (d_down0, d_down1)]
    leaving = reduce_early(grads)

    dr1, dgb1 = ln_bwd(r1, dh1, ln_g[0], tm=tm, name="ln1_bwd")
    d_aout = tn(onz, dr1, name="d_a_out")
    donz = nt(dr1, wt["a_out"], name="d_onz")
    d_o, dz, dnw = gdn_post_bwd(o[None], pz, donz, wt["anorm_b"], tm=tm)
    dqkvbg, landed = gdn_chunk_bwd(qkvbg, states, tinv, d_o[0], leaving)
    dp5, daconv, dscal = gdn_pre_bwd(p5, dqkvbg, wt["a_conv3"], wt["alog_b"], wt["dtb_b"], tm=tm, cb=2 * HEAD_DIM)
    d_a5 = tn(h0, dp5, name="d_a_in5")
    d_az = tn(h0, dz, name="d_a_inz")
    dh0 = nt(dp5, wt["a5"], res=dr1, res_scale=ALPHA, name="d_h0a")
    dh0 = nt(dz, wt["az"], res=dh0, res_scale=1.0, name="d_h0")

    width = HEADS * HEAD_DIM
    d_ba = head_lane_sum(d_a5[0, 3:5])[:, :, :HEADS]
    d_a_in = jnp.concatenate([d_a5[0, 0], d_a5[0, 1], d_a5[0, 2], d_az[0, 0], d_ba[0], d_ba[1]], axis=1)
    n_in = d_a_in.shape[1] // 4
    grads["a_w_in"] = [d_a_in.reshape(d, 4, n_in).transpose(1, 0, 2)]
    grads["a_w_out"] = [d_aout.reshape(4, width // 4, d)]
    grads["a_conv"] = daconv.sum(axis=2).transpose(1, 0, 2).reshape(1, GDN_CONV, 3 * width)
    per_head = dscal.reshape(2, 8, HEADS, HEAD_DIM).sum(axis=(1, 3))
    grads["a_log"] = per_head[0][None]
    grads["a_dt_bias"] = per_head[1][None]
    grads["a_norm"] = dnw.reshape(8, HEADS, HEAD_DIM).sum(axis=(0, 1))[None]
    grads["b_conv"] = dbconv.sum(axis=1)[None]
    lns = [dgb1, dgb2, dgb3, dgb4]
    grads["ln_mix_g"] = jnp.stack([lns[0][0].sum(0), lns[2][0].sum(0)])
    grads["ln_mix_b"] = jnp.stack([lns[0][1].sum(0), lns[2][1].sum(0)])
    grads["ln_ffn_g"] = jnp.stack([lns[1][0].sum(0), lns[3][0].sum(0)])
    grads["ln_ffn_b"] = jnp.stack([lns[1][1].sum(0), lns[3][1].sum(0)])
    grads["ffn_conv"] = jnp.stack([t.sum(axis=2).transpose(1, 0, 2).reshape(FFN_CONV, -1) for t in (dfconv0, dfconv1)])
    grads["meta"] = dh0[0, :N_META]
    return loss_part, dh0, grads, landed


WEIGHTS = ["meta", "a_w_in", "a_conv", "a_log", "a_dt_bias", "a_norm", "a_w_out", "b_w_in", "b_conv", "b_w_out",
           "ln_mix_g", "ln_mix_b", "ffn_w_up", "ffn_conv", "ffn_w_down", "ln_ffn_g", "ln_ffn_b"]
EARLY_WEIGHTS = ["a_w_in", "a_w_out"]
LATE_WEIGHTS = ["b_w_in", "b_w_out", "ffn_w_up", "ffn_w_down"]
MATMUL_WEIGHTS = EARLY_WEIGHTS + LATE_WEIGHTS
SMALL_SHARDED = ["a_conv", "b_conv", "ffn_conv", "meta"]
REPLICATED = ["a_log", "a_dt_bias", "a_norm", "ln_mix_g", "ln_mix_b", "ln_ffn_g", "ln_ffn_b"]
SHARD_AXIS = {"meta": 1, "a_w_in": 2, "a_conv": 2, "a_w_out": 1, "b_w_in": 2, "b_conv": 2, "b_w_out": 1,
              "ffn_w_up": 2, "ffn_conv": 2, "ffn_w_down": 1}
PACK_COLS = 1024
PACK_ROWS_MULTIPLE = 32


def _pack(pieces, lead=()):
    flat = jnp.concatenate([p.reshape(lead + (-1,)) for p in pieces], axis=-1)
    n = flat.shape[-1]
    rows = -(-n // (PACK_COLS * PACK_ROWS_MULTIPLE)) * PACK_ROWS_MULTIPLE
    flat = jnp.pad(flat, [(0, 0)] * len(lead) + [(0, rows * PACK_COLS - n)])
    return flat.reshape(lead + (rows, PACK_COLS))


def _unpack(buf, shapes, lead=()):
    flat = buf.reshape(lead + (-1,))
    out, off = [], 0
    for shp in shapes:
        n = 1
        for s in shp:
            n *= s
        out.append(flat[..., off:off + n].reshape(lead + tuple(shp)))
        off += n
    return out


def _join_shards(stacked, axis):
    return jnp.concatenate([stacked[k] for k in range(4)], axis=axis)


def _split_shards(full, axis):
    return jnp.stack(jnp.split(full, 4, axis=axis))


def _weight_layers(w, names):
    return [w[n][l].astype(BF16) for n in names for l in range(w[n].shape[0])]


def _per_weight(arrays, w, names):
    it = iter(arrays)
    return {n: [next(it) for _ in range(w[n].shape[0])] for n in names}


def _layout_early(full, w):
    width = HEADS * HEAD_DIM
    wt = {n: w[n] for n in ("ln_mix_g", "ln_mix_b", "ln_ffn_g", "ln_ffn_b")}
    w_in = _join_shards(full["a_w_in"][0], 1)
    d = w_in.shape[0]
    n_ff = full["ffn_conv"].shape[2] // 2
    blocks = [w_in[:, s * width:(s + 1) * width] for s in range(4)]
    b_exp = jnp.repeat(w_in[:, 4 * width:4 * width + HEADS], HEAD_DIM, axis=1)
    a_exp = jnp.repeat(w_in[:, 4 * width + HEADS:], HEAD_DIM, axis=1)
    wt["a5"] = jnp.stack([blocks[0], blocks[1], blocks[2], b_exp, a_exp])[None]
    wt["az"] = blocks[3][None, None]
    wt["a_out"] = full["a_w_out"][0].reshape(1, 1, width, d)
    wt["a_conv3"] = full["a_conv"][0].reshape(GDN_CONV, 3, width).transpose(1, 0, 2)
    wt["b_conv"] = full["b_conv"][0]
    wt["fconv"] = [full["ffn_conv"][l].reshape(FFN_CONV, 2, n_ff).transpose(1, 0, 2) for l in range(2)]
    wt["meta"] = full["meta"]
    wt["alog_b"] = jnp.repeat(w["a_log"][0], HEAD_DIM)[None]
    wt["dtb_b"] = jnp.repeat(w["a_dt_bias"][0], HEAD_DIM)[None]
    wt["anorm_b"] = jnp.tile(w["a_norm"][0], HEADS)[None]
    return wt


def _layout_late(full):
    d = full["b_w_in"][0].shape[1]
    n_ff = full["ffn_w_up"][0].shape[2]
    return {
        "b_in": _join_shards(full["b_w_in"][0], 1).reshape(d, 3, d).transpose(1, 0, 2)[None],
        "b_out": full["b_w_out"][0].reshape(1, 1, d, d),
        "up": [t[None] for t in full["ffn_w_up"]],
        "down": [t.reshape(2, 1, n_ff, d) for t in full["ffn_w_down"]],
    }


def kernel(x, meta, a_w_in, a_conv, a_log, a_dt_bias, a_norm, a_w_out, b_w_in, b_conv, b_w_out, ln_mix_g, ln_mix_b, ffn_w_up, ffn_conv, ffn_w_down, ln_ffn_g, ln_ffn_b, loss_target, m_meta, m_a_w_in, m_a_conv, m_a_log, m_a_dt_bias, m_a_norm, m_a_w_out, m_b_w_in, m_b_conv, m_b_w_out, m_ln_mix_g, m_ln_mix_b, m_ffn_w_up, m_ffn_conv, m_ffn_w_down, m_ln_ffn_g, m_ln_ffn_b, v_meta, v_a_w_in, v_a_conv, v_a_log, v_a_dt_bias, v_a_norm, v_a_w_out, v_b_w_in, v_b_conv, v_b_w_out, v_ln_mix_g, v_ln_mix_b, v_ffn_w_up, v_ffn_conv, v_ffn_w_down, v_ln_ffn_g, v_ln_ffn_b):
    w = dict(meta=meta, a_w_in=a_w_in, a_conv=a_conv, a_log=a_log, a_dt_bias=a_dt_bias, a_norm=a_norm, a_w_out=a_w_out,
             b_w_in=b_w_in, b_conv=b_conv, b_w_out=b_w_out, ln_mix_g=ln_mix_g, ln_mix_b=ln_mix_b, ffn_w_up=ffn_w_up,
             ffn_conv=ffn_conv, ffn_w_down=ffn_w_down, ln_ffn_g=ln_ffn_g, ln_ffn_b=ln_ffn_b)
    m = dict(meta=m_meta, a_w_in=m_a_w_in, a_conv=m_a_conv, a_log=m_a_log, a_dt_bias=m_a_dt_bias, a_norm=m_a_norm,
             a_w_out=m_a_w_out, b_w_in=m_b_w_in, b_conv=m_b_conv, b_w_out=m_b_w_out, ln_mix_g=m_ln_mix_g,
             ln_mix_b=m_ln_mix_b, ffn_w_up=m_ffn_w_up, ffn_conv=m_ffn_conv, ffn_w_down=m_ffn_w_down,
             ln_ffn_g=m_ln_ffn_g, ln_ffn_b=m_ln_ffn_b)
    v = dict(meta=v_meta, a_w_in=v_a_w_in, a_conv=v_a_conv, a_log=v_a_log, a_dt_bias=v_a_dt_bias, a_norm=v_a_norm,
             a_w_out=v_a_w_out, b_w_in=v_b_w_in, b_conv=v_b_conv, b_w_out=v_b_w_out, ln_mix_g=v_ln_mix_g,
             ln_mix_b=v_ln_mix_b, ffn_w_up=v_ffn_w_up, ffn_conv=v_ffn_conv, ffn_w_down=v_ffn_w_down,
             ln_ffn_g=v_ln_ffn_g, ln_ffn_b=v_ln_ffn_b)
    seq = x.shape[1]
    *stacks, small = all_gather_shards(_weight_layers(w, EARLY_WEIGHTS) + [_pack([w[n] for n in SMALL_SHARDED])],
                                       "gather_early")
    full = _per_weight(stacks, w, EARLY_WEIGHTS)
    for n, t in zip(SMALL_SHARDED, _unpack(small, [w[n].shape for n in SMALL_SHARDED], lead=(4,))):
        full[n] = _join_shards(t, SHARD_AXIS[n])

    def layout_late(late_stacks):
        return _layout_late(_per_weight(late_stacks, w, LATE_WEIGHTS))

    def reduce_early(grads):
        bufs = [g for n in LATE_WEIGHTS for g in grads[n]]
        return pair_sums(bufs, [BF16] * len(bufs), "late")

    loss_part, dh0, grads, landed_late = _local_step(x[0], loss_target[0], _layout_early(full, w),
                                                     _weight_layers(w, LATE_WEIGHTS), layout_late, reduce_early)
    pieces = [_split_shards(grads[n], SHARD_AXIS[n]) for n in SMALL_SHARDED]
    same = jnp.concatenate([grads[n].reshape(-1) for n in REPLICATED] + [jnp.sum(loss_part).reshape(1)])
    pieces.append(jnp.broadcast_to(same, (4,) + same.shape))
    bufs = [g for n in EARLY_WEIGHTS for g in grads[n]] + [_pack(pieces, lead=(4,))]
    landed = scatter_to_chips(pair_sums(bufs, [BF16] * (len(bufs) - 1) + [F32], "early"), "rs_chips_early")
    totals = [chip_sum(t, "rs_chip_sum%d" % i) for i, t in enumerate(landed + landed_late)]
    by_weight = _per_weight(totals[:len(bufs) - 1] + totals[len(bufs):], w, MATMUL_WEIGHTS)
    *shared, small_total = share_halves([by_weight[n] for n in MATMUL_WEIGHTS] + [[totals[len(bufs) - 1]]], "rs_share")
    grad_w = {n: t.reshape(w[n].shape) for n, t in zip(MATMUL_WEIGHTS, shared)}
    rest = SMALL_SHARDED + REPLICATED
    unpacked = _unpack(small_total[0], [w[n].shape for n in rest] + [()])
    grad_w.update(zip(rest, unpacked[:-1]))
    loss = unpacked[-1]
    grad_x = dh0[:, N_META:N_META + seq]
    steps = [adamw(w[n], grad_w[n], m[n], v[n], "adamw_" + n) for n in WEIGHTS]
    return (loss, grad_x, *[grad_w[n] for n in WEIGHTS], *[s[0] for s in steps], *[s[1] for s in steps],
            *[s[2] for s in steps])
```

```python
import functools

import jax
import jax.numpy as jnp
from jax import lax
from jax.experimental import pallas as pl
from jax.experimental.pallas import tpu as pltpu

F32 = jnp.float32
BF16 = jnp.bfloat16
HI = lax.Precision.HIGHEST

N_META = 16
HEADS = 8
HEAD_DIM = 128
CHUNK = 64
GDN_CONV = 4
SC_CONV = 3
FFN_CONV = 3
ALPHA = 4.0 ** 0.25
LN_EPS = 1e-5
RMS_EPS = 1e-6
L2_EPS = 1e-6
Q_SCALE = HEAD_DIM ** -0.5

ADAM_LR = 0.001
ADAM_B1 = 0.9
ADAM_B2 = 0.999
ADAM_EPS = 1e-08
ADAM_WD = 0.01
ADAM_STEP = 10

HALO = 8
VMEM_LIMIT = 48 * 1024 * 1024


def _params(sem=None):
    return pltpu.CompilerParams(dimension_semantics=sem, vmem_limit_bytes=VMEM_LIMIT)


def _dot(a, b, prec=None):
    return jnp.dot(a, b, preferred_element_type=F32, precision=prec)


def _dot_nt(a, b, prec=None):
    return lax.dot_general(a, b, (((1,), (1,)), ((), ())), preferred_element_type=F32, precision=prec)


def _dot_tn(a, b, prec=None):
    return lax.dot_general(a, b, (((0,), (0,)), ((), ())), preferred_element_type=F32, precision=prec)


def _sigmoid(x):
    return 1.0 / (1.0 + jnp.exp(-x))


def _tri_masks():
    r = lax.broadcasted_iota(jnp.int32, (CHUNK, CHUNK), 0)
    c = lax.broadcasted_iota(jnp.int32, (CHUNK, CHUNK), 1)
    return r >= c, r > c, r == c


def _split_hi_lo(x):
    hi = x.astype(BF16)
    return hi, (x - hi.astype(F32)).astype(BF16)


def _mask_dot(mask, x):
    hi, lo = _split_hi_lo(x)
    return _dot(mask, hi) + _dot(mask, lo)


@jax.custom_vjp
def _cumsum_rows(g):
    causal, _, _ = _tri_masks()
    return _mask_dot(causal.astype(BF16), g)


def _cumsum_rows_fwd(g):
    return _cumsum_rows(g), None


def _cumsum_rows_bwd(_, dy):
    _, strict, _ = _tri_masks()
    return (_mask_dot((~strict).astype(BF16), dy),)


_cumsum_rows.defvjp(_cumsum_rows_fwd, _cumsum_rows_bwd)


def _dot_split3(a, b):
    a_hi, a_lo = _split_hi_lo(a)
    b_hi, b_lo = _split_hi_lo(b)
    return _dot(a_hi, b_hi) + (_dot(a_hi, b_lo) + _dot(a_lo, b_hi))


@jax.custom_vjp
def _dot_precise(a, b):
    return _dot_split3(a, b)


def _dot_precise_fwd(a, b):
    return _dot_split3(a, b), (a, b)


def _dot_precise_bwd(operands, ct):
    a, b = operands
    return _dot_split3(ct, b.T), _dot_split3(a.T, ct)


_dot_precise.defvjp(_dot_precise_fwd, _dot_precise_bwd)


def _gdn_m(ks, g64s, bbs):
    causal, strict, _ = _tri_masks()
    a = [_cumsum_rows(g) for g in g64s]
    decay = [jnp.exp(jnp.where(causal, x - x.T, -1e30)) for x in a]
    kk = [_dot_nt(k * b, k) for k, b in zip(ks, bbs)]
    return [jnp.where(strict, x * d, 0.0) for x, d in zip(kk, decay)]


def _gdn_inverse_stages(ks, g64s, bbs):
    ms = _gdn_m(ks, g64s, bbs)
    yield
    r = lax.broadcasted_iota(jnp.int32, (CHUNK, CHUNK), 0)
    c = lax.broadcasted_iota(jnp.int32, (CHUNK, CHUNK), 1)
    eye = (r == c).astype(F32)
    same = [jnp.right_shift(r, s) == jnp.right_shift(c, s) for s in (3, 4, 5)]
    d = [jnp.where(same[0], m, 0.0) for m in ms]
    p = [_dot(x, x) for x in d]
    yield
    t = [eye - x for x in d]
    t = [x + _dot(x, y) for x, y in zip(t, p)]
    p = [_dot(x, x) for x in p]
    yield
    t = [x + _dot(x, y) for x, y in zip(t, p)]
    yield
    for inner, outer in ((same[0], same[1]), (same[1], same[2]), (same[2], None)):
        joins = ~inner if outer is None else (outer & ~inner)
        o = [_dot(x, jnp.where(joins, m, 0.0)) for x, m in zip(t, ms)]
        yield
        t = [x - _dot(y, x) for x, y in zip(t, o)]
        yield
    res = [eye - x - _dot_split3(m, x) for m, x in zip(ms, t)]
    yield
    return [x + _dot(x, y) for x, y in zip(t, res)]


def _gdn_apply_stages(qs, ks, vs, gbs, g64s, bbs, ss, ts):
    causal, _, _ = _tri_masks()
    n = range(len(qs))
    gc = [_cumsum_rows(g) for g in gbs]
    a = [_cumsum_rows(g) for g in g64s]
    qk = [_dot_nt(qs[h], ks[h]) for h in n]
    yield
    decay = [jnp.exp(jnp.where(causal, x - x.T, -1e30)) for x in a]
    eg = [jnp.exp(x) for x in gc]
    u = [_dot_precise(ts[h], vs[h] * bbs[h]) for h in n]
    w = [_dot_precise(ts[h], ks[h] * bbs[h] * eg[h]) for h in n]
    qk = [qk[h] * decay[h] for h in n]
    gl = [jnp.sum(g, axis=0, keepdims=True) for g in gbs]
    kd = [ks[h] * jnp.exp(gl[h] - gc[h]) for h in n]
    yield
    v_new = [u[h] - _dot(w[h], ss[h]) for h in n]
    q_s = [_dot(qs[h] * eg[h], ss[h]) for h in n]
    yield
    o = [q_s[h] + _dot(qk[h], v_new[h]) for h in n]
    s2 = [ss[h] * jnp.exp(gl[h]) + _dot_tn(kd[h], v_new[h]) for h in n]
    return o, s2


def _run_stages(*generators):
    results = [None] * len(generators)
    live = dict(enumerate(generators))
    while live:
        for i, gen in list(live.items()):
            try:
                next(gen)
            except StopIteration as stop:
                results[i] = stop.value
                del live[i]
    return results


def _gdn_apply(qs, ks, vs, gbs, g64s, bbs, ss, ts):
    return _run_stages(_gdn_apply_stages(qs, ks, vs, gbs, g64s, bbs, ss, ts))[0]


def _head_slices(h):
    return slice(h * HEAD_DIM, (h + 1) * HEAD_DIM), slice(h * HEAD_DIM, h * HEAD_DIM + CHUNK)


def _gdn_head_values(x_ref):
    out = [[], [], [], [], [], []]
    for h in range(HEADS):
        sl, sl64 = _head_slices(h)
        for lst, val in zip(out, (x_ref[0, :, sl], x_ref[1, :, sl], x_ref[2, :, sl], x_ref[4, :, sl],
                                  x_ref[4, :, sl64], x_ref[3, :, sl])):
            lst.append(val)
    return out


def gdn_chunk_fwd(qkvbg, gather=()):
    _, lp, width = qkvbg.shape
    n_chunks = lp // CHUNK
    n = len(gather)

    def body(x_ref, next_ref, *refs):
        shard_refs, (o_ref, s_ref, t_ref), refs = refs[:n], refs[n:n + 3], refs[n + 3:]
        stack_refs, state, t_next, sems = refs[:n], refs[n], refs[n + 1], refs[n + 2:]
        copies = _gather_copies(shard_refs, stack_refs, *sems) if n else None

        def inverse_stages(ref):
            _, ks, _, _, g64s, bbs = _gdn_head_values(ref)
            return _gdn_inverse_stages(ks, g64s, bbs)

        @pl.when(pl.program_id(0) == 0)
        def _():
            state[...] = jnp.zeros_like(state)
            for h, t in enumerate(_run_stages(inverse_stages(x_ref))[0]):
                t_next[h] = t
            if n:
                _gather_start(copies)

        qs, ks, vs, gbs, g64s, bbs = _gdn_head_values(x_ref)
        ss = [state[h] for h in range(HEADS)]
        ts = [t_next[h] for h in range(HEADS)]
        ts_next, (os_, s2) = _run_stages(inverse_stages(next_ref),
                                         _gdn_apply_stages(qs, ks, vs, gbs, g64s, bbs, ss, ts))
        for h in range(HEADS):
            s_ref[0, h] = ss[h]
            t_ref[0, h] = ts[h]
            t_next[h] = ts_next[h]
            o_ref[:, _head_slices(h)[0]] = os_[h]
            state[h] = s2[h]

        if n:
            @pl.when(pl.program_id(0) == n_chunks - 1)
            def _():
                _gather_finish(copies)

    o, states, tinv, *stacks = pl.pallas_call(
        body,
        name="gdn_chunk_fwd",
        grid=(n_chunks,),
        in_specs=[pl.BlockSpec((5, CHUNK, width), lambda c: (0, c, 0)),
                  pl.BlockSpec((5, CHUNK, width), lambda c: (0, jnp.minimum(c + 1, n_chunks - 1), 0))] + [ANY] * n,
        out_specs=[
            pl.BlockSpec((CHUNK, width), lambda c: (c, 0)),
            pl.BlockSpec((1, HEADS, HEAD_DIM, HEAD_DIM), lambda c: (c, 0, 0, 0)),
            pl.BlockSpec((1, HEADS, CHUNK, CHUNK), lambda c: (c, 0, 0, 0)),
        ] + [ANY] * n,
        out_shape=[
            jax.ShapeDtypeStruct((lp, width), F32),
            jax.ShapeDtypeStruct((n_chunks, HEADS, HEAD_DIM, HEAD_DIM), F32),
            jax.ShapeDtypeStruct((n_chunks, HEADS, CHUNK, CHUNK), F32),
        ] + _gather_out_shapes(gather),
        scratch_shapes=[pltpu.VMEM((HEADS, HEAD_DIM, HEAD_DIM), F32), pltpu.VMEM((HEADS, CHUNK, CHUNK), F32)]
        + (_gather_sems(n) if n else []),
        compiler_params=_params(("arbitrary",)),
    )(qkvbg, qkvbg, *gather)
    return o, states, tinv, _set_own_slots(stacks, gather)


def gdn_chunk_bwd(qkvbg, states, tinv, d_o, scatter=()):
    _, lp, width = qkvbg.shape
    n_chunks = lp // CHUNK
    last = n_chunks - 1
    n = len(scatter)

    def body(x_ref, s_ref, t_ref, do_ref, *refs):
        leaving_refs, dx_ref, refs = refs[:n], refs[n], refs[n + 1:]
        landing_refs, dstate, sems = refs[:n], refs[n], refs[n + 1:]
        copies = _scatter_copies(leaving_refs, landing_refs, *sems) if n else None

        @pl.when(pl.program_id(0) == 0)
        def _():
            dstate[...] = jnp.zeros_like(dstate)
            if n:
                _scatter_start(copies)

        heads = range(HEADS)
        qs, ks, vs, gbs, g64s, bbs = _gdn_head_values(x_ref)
        ss = [s_ref[0, h] for h in heads]
        ts = [t_ref[0, h] for h in heads]
        d_out = ([do_ref[:, _head_slices(h)[0]] for h in heads], [dstate[h] for h in heads])
        _, vjp_apply = jax.vjp(_gdn_apply, qs, ks, vs, gbs, g64s, bbs, ss, ts)
        dq, dk, dv, dgb, dg64, dbb, ds, dt = vjp_apply(d_out)
        tts = [t.T for t in ts]
        dm = [_dot(tts[h], dt[h]) for h in heads]
        dm = [-_dot(dm[h], tts[h]) for h in heads]
        _, vjp_m = jax.vjp(_gdn_m, ks, g64s, bbs)
        dk2, dg64m, dbb2 = vjp_m(dm)
        for h in heads:
            sl, sl64 = _head_slices(h)
            dx_ref[0, :, sl] = dq[h]
            dx_ref[1, :, sl] = dk[h] + dk2[h]
            dx_ref[2, :, sl] = dv[h]
            dx_ref[3, :, sl] = dbb[h] + dbb2[h]
            dx_ref[4, :, sl] = dgb[h]
            dx_ref[4, :, sl64] += dg64[h] + dg64m[h]
            dstate[h] = ds[h]

        if n:
            @pl.when(pl.program_id(0) == n_chunks - 1)
            def _():
                _scatter_finish(copies)

    dqkvbg, *landed = pl.pallas_call(
        body,
        name="gdn_chunk_bwd",
        grid=(n_chunks,),
        in_specs=[
            pl.BlockSpec((5, CHUNK, width), lambda c: (0, last - c, 0)),
            pl.BlockSpec((1, HEADS, HEAD_DIM, HEAD_DIM), lambda c: (last - c, 0, 0, 0)),
            pl.BlockSpec((1, HEADS, CHUNK, CHUNK), lambda c: (last - c, 0, 0, 0)),
            pl.BlockSpec((CHUNK, width), lambda c: (last - c, 0)),
        ] + [ANY] * n,
        out_specs=[pl.BlockSpec((5, CHUNK, width), lambda c: (0, last - c, 0))] + [ANY] * n,
        out_shape=[jax.ShapeDtypeStruct(qkvbg.shape, F32)] + [jax.ShapeDtypeStruct(b.shape, b.dtype) for b in scatter],
        scratch_shapes=[pltpu.VMEM((HEADS, HEAD_DIM, HEAD_DIM), F32)] + (_scatter_sems(n) if n else []),
        compiler_params=_params(("arbitrary",)),
    )(qkvbg, states, tinv, d_o, *scatter)
    return dqkvbg, _keep_own_slots(landed, scatter)


def mm_nn(a, b, *, tm, name):
    ks, m, tk = a.shape
    _, ns, _, tn = b.shape

    def body(a_ref, b_ref, o_ref):
        p = _dot(a_ref[...].astype(BF16), b_ref[...])

        @pl.when(pl.program_id(2) == 0)
        def _():
            o_ref[...] = p

        @pl.when(pl.program_id(2) > 0)
        def _():
            o_ref[...] += p

    return pl.pallas_call(
        body,
        name=name,
        grid=(ns, m // tm, ks),
        in_specs=[
            pl.BlockSpec((None, tm, tk), lambda n, i, k: (k, i, 0)),
            pl.BlockSpec((None, None, tk, tn), lambda n, i, k: (k, n, 0, 0)),
        ],
        out_specs=pl.BlockSpec((None, tm, tn), lambda n, i, k: (n, i, 0)),
        out_shape=jax.ShapeDtypeStruct((ns, m, tn), F32),
        compiler_params=_params(("arbitrary", "arbitrary", "arbitrary")),
    )(a, b)


def mm_nt(dy, w, *, tm, name, res=None, res_scale=1.0):
    ns, m, tn = dy.shape
    ks, _, tk, _ = w.shape

    def body(*refs):
        if res is None:
            dy_ref, w_ref, o_ref = refs
        else:
            dy_ref, w_ref, r_ref, o_ref = refs
        p = _dot_nt(dy_ref[...].astype(BF16), w_ref[...])

        @pl.when(pl.program_id(2) == 0)
        def _():
            o_ref[...] = p if res is None else p + res_scale * r_ref[...]

        @pl.when(pl.program_id(2) > 0)
        def _():
            o_ref[...] += p

    in_specs = [
        pl.BlockSpec((None, tm, tn), lambda k, i, n: (n, i, 0)),
        pl.BlockSpec((None, None, tk, tn), lambda k, i, n: (k, n, 0, 0)),
    ]
    args = [dy, w]
    if res is not None:
        in_specs.append(pl.BlockSpec((None, tm, tk), lambda k, i, n: (k, i, 0)))
        args.append(res)
    return pl.pallas_call(
        body,
        name=name,
        grid=(ks, m // tm, ns),
        in_specs=in_specs,
        out_specs=pl.BlockSpec((None, tm, tk), lambda k, i, n: (k, i, 0)),
        out_shape=jax.ShapeDtypeStruct((ks, m, tk), F32),
        compiler_params=_params(("arbitrary", "arbitrary", "arbitrary")),
    )(*args)


def mm_tn(x, dy, *, tm, name, rb=None):
    ks, m, tk = x.shape
    ns, _, tn = dy.shape
    rb = tk if rb is None else rb

    def body(x_ref, dy_ref, o_ref):
        @pl.when(pl.program_id(2) == 0)
        def _():
            o_ref[...] = jnp.zeros_like(o_ref)

        dyb = dy_ref[...].astype(BF16)
        for r in range(0, tk, rb):
            o_ref[r:r + rb, :] += _dot_tn(x_ref[:, r:r + rb].astype(BF16), dyb)

    return pl.pallas_call(
        body,
        name=name,
        grid=(ks, ns, m // tm),
        in_specs=[
            pl.BlockSpec((None, tm, tk), lambda k, n, i: (k, i, 0)),
            pl.BlockSpec((None, tm, tn), lambda k, n, i: (n, i, 0)),
        ],
        out_specs=pl.BlockSpec((None, None, tk, tn), lambda k, n, i: (k, n, 0, 0)),
        out_shape=jax.ShapeDtypeStruct((ks, ns, tk, tn), F32),
        compiler_params=_params(("arbitrary", "arbitrary", "arbitrary")),
    )(x, dy)


def _row_partial(x):
    rows, c = x.shape
    return jnp.sum(x.reshape(rows // 8, 8, c), axis=0)


def _layer_norm(r, g, b):
    mu = jnp.mean(r, axis=-1, keepdims=True)
    xc = r - mu
    var = jnp.mean(xc * xc, axis=-1, keepdims=True)
    return xc * lax.rsqrt(var + LN_EPS) * g + b


def _layer_norm_bwd(x, dh, g):
    mu = jnp.mean(x, axis=-1, keepdims=True)
    xc = x - mu
    rstd = lax.rsqrt(jnp.mean(xc * xc, axis=-1, keepdims=True) + LN_EPS)
    xh = xc * rstd
    dxh = dh * g
    m1 = jnp.mean(dxh, axis=-1, keepdims=True)
    m2 = jnp.mean(dxh * xh, axis=-1, keepdims=True)
    return rstd * (dxh - m1 - xh * m2), _row_partial(dh * xh), _row_partial(dh)


def mm_nn_ln(a, b, h_prev, g, beta, *, tm, name):
    ks, m, tk = a.shape
    d = b.shape[3]

    def body(a_ref, b_ref, hp_ref, g_ref, be_ref, r_ref, h_ref):
        p = _dot(a_ref[...].astype(BF16), b_ref[...])

        @pl.when(pl.program_id(1) == 0)
        def _():
            r_ref[...] = p

        @pl.when(pl.program_id(1) > 0)
        def _():
            r_ref[...] += p

        @pl.when(pl.program_id(1) == ks - 1)
        def _():
            r = ALPHA * hp_ref[...] + r_ref[...]
            r_ref[...] = r
            h_ref[...] = _layer_norm(r, g_ref[...], be_ref[...])

    row = pl.BlockSpec((None, tm, d), lambda i, k: (0, i, 0))
    vec = pl.BlockSpec((1, d), lambda i, k: (0, 0))
    return pl.pallas_call(
        body,
        name=name,
        grid=(m // tm, ks),
        in_specs=[
            pl.BlockSpec((None, tm, tk), lambda i, k: (k, i, 0)),
            pl.BlockSpec((None, None, tk, d), lambda i, k: (k, 0, 0, 0)),
            row, vec, vec,
        ],
        out_specs=[row, row],
        out_shape=[jax.ShapeDtypeStruct((1, m, d), F32)] * 2,
        compiler_params=_params(("arbitrary", "arbitrary")),
    )(a, b, h_prev, g, beta)


def mm_nt_ln_bwd(dy, w, res, r, g, *, tm, name):
    ns, m, tn = dy.shape
    d = w.shape[2]

    def body(dy_ref, w_ref, res_ref, r_ref, g_ref, dr_ref, dgb_ref):
        p = _dot_nt(dy_ref[...].astype(BF16), w_ref[...])

        @pl.when((pl.program_id(0) == 0) & (pl.program_id(1) == 0))
        def _():
            dgb_ref[...] = jnp.zeros_like(dgb_ref)

        @pl.when(pl.program_id(1) == 0)
        def _():
            dr_ref[...] = p + ALPHA * res_ref[...]

        @pl.when(pl.program_id(1) > 0)
        def _():
            dr_ref[...] += p

        @pl.when(pl.program_id(1) == ns - 1)
        def _():
            dr, dgamma, dbeta = _layer_norm_bwd(r_ref[...], dr_ref[...], g_ref[...])
            dr_ref[...] = dr
            dgb_ref[0] += dgamma
            dgb_ref[1] += dbeta

    row = pl.BlockSpec((None, tm, d), lambda i, n: (0, i, 0))
    return pl.pallas_call(
        body,
        name=name,
        grid=(m // tm, ns),
        in_specs=[
            pl.BlockSpec((None, tm, tn), lambda i, n: (n, i, 0)),
            pl.BlockSpec((None, None, d, tn), lambda i, n: (0, n, 0, 0)),
            row, row,
            pl.BlockSpec((1, d), lambda i, n: (0, 0)),
        ],
        out_specs=[row, pl.BlockSpec((2, 8, d), lambda i, n: (0, 0, 0))],
        out_shape=[jax.ShapeDtypeStruct((1, m, d), F32), jax.ShapeDtypeStruct((2, 8, d), F32)],
        compiler_params=_params(("arbitrary", "arbitrary")),
    )(dy, w, res, r, g)


def ln_bwd(r, dh, g, *, tm, name):
    _, lp, d = r.shape

    def body(r_ref, dh_ref, g_ref, dr_ref, dgb_ref):
        dr, dgamma, dbeta = _layer_norm_bwd(r_ref[...], dh_ref[...], g_ref[...])
        dr_ref[...] = dr

        @pl.when(pl.program_id(0) == 0)
        def _():
            dgb_ref[...] = jnp.zeros_like(dgb_ref)

        dgb_ref[0] += dgamma
        dgb_ref[1] += dbeta

    row = pl.BlockSpec((None, tm, d), lambda i: (0, i, 0))
    return pl.pallas_call(
        body,
        name=name,
        grid=(lp // tm,),
        in_specs=[row, row, pl.BlockSpec((1, d), lambda i: (0, 0))],
        out_specs=[row, pl.BlockSpec((2, 8, d), lambda i: (0, 0, 0))],
        out_shape=[jax.ShapeDtypeStruct((1, lp, d), F32), jax.ShapeDtypeStruct((2, 8, d), F32)],
        compiler_params=_params(("arbitrary",)),
    )(r, dh, g)


def loss_grad(h, target, *, first, count, tm):
    _, lp, d = h.shape

    def body(h_ref, t_ref, dh_ref, l_ref):
        row = pl.program_id(0) * tm + lax.broadcasted_iota(jnp.int32, (tm, d), 0)
        valid = (row >= first) & (row < first + count)
        err = jnp.where(valid, h_ref[...] - t_ref[...], 0.0)
        dh_ref[...] = err * (1.0 / d)

        @pl.when(pl.program_id(0) == 0)
        def _():
            l_ref[...] = jnp.zeros_like(l_ref)

        l_ref[...] += _row_partial(err * err) * (0.5 / d)

    return pl.pallas_call(
        body,
        name="loss_grad",
        grid=(lp // tm,),
        in_specs=[pl.BlockSpec((None, tm, d), lambda i: (0, i, 0)), pl.BlockSpec((tm, d), lambda i: (i, 0))],
        out_specs=[pl.BlockSpec((None, tm, d), lambda i: (0, i, 0)), pl.BlockSpec((8, d), lambda i: (0, 0))],
        out_shape=[jax.ShapeDtypeStruct((1, lp, d), F32), jax.ShapeDtypeStruct((8, d), F32)],
        compiler_params=_params(("arbitrary",)),
    )(h, target)


def _halo_index(tile, tm):
    return jnp.maximum(tile * (tm // HALO) - 1, 0)


def _conv_fwd(xs_ref, w, taps, tm):
    acc = w(0) * xs_ref[pl.ds(HALO - taps + 1, tm), :]
    for j in range(1, taps):
        acc += w(j) * xs_ref[pl.ds(HALO - taps + 1 + j, tm), :]
    return acc


def _conv_bwd_x(dcs_ref, w, taps, tm):
    acc = w(0) * dcs_ref[pl.ds(taps - 1, tm), :]
    for j in range(1, taps):
        acc += w(j) * dcs_ref[pl.ds(taps - 1 - j, tm), :]
    return acc


SUB = 8
LANES = 128
PAIR = 2 * SUB
STRIP_UNROLL = 2


def _pair_rows(r0):
    return pl.ds(r0, SUB), pl.ds(r0 + SUB if isinstance(r0, int) else pl.multiple_of(r0 + SUB, SUB), SUB)


def _shift_down(cur, prev, s):
    if s == 0:
        return cur
    row = lax.broadcasted_iota(jnp.int32, cur.shape, 0)
    return jnp.where(row < s, pltpu.roll(prev, s, axis=0), pltpu.roll(cur, s, axis=0))


def _shift_up(cur, nxt, s):
    if s == 0:
        return cur
    row = lax.broadcasted_iota(jnp.int32, cur.shape, 0)
    return jnp.where(row < SUB - s, pltpu.roll(cur, SUB - s, axis=0), pltpu.roll(nxt, SUB - s, axis=0))


def _silu_parts(c):
    sg = _sigmoid(c)
    return c * sg, sg * (1.0 + c * (1.0 - sg))


def _head_sum(x):
    rows, c = x.shape
    parts = []
    for h in range(c // HEAD_DIM):
        s = jnp.sum(x[:, h * HEAD_DIM:(h + 1) * HEAD_DIM], axis=-1, keepdims=True)
        parts.append(jnp.broadcast_to(s, (rows, HEAD_DIM)))
    return parts[0] if len(parts) == 1 else jnp.concatenate(parts, axis=-1)


def _log1p(y):
    u = 1.0 + y
    d = u - 1.0
    return jnp.where(d == 0.0, y, jnp.log(u) * (y / jnp.where(d == 0.0, 1.0, d)))


def _softplus(x):
    return jnp.maximum(x, 0.0) + _log1p(jnp.exp(-jnp.abs(x)))


def gdn_pre_fwd(p5, conv_w, alog_b, dtb_b, *, tm, cb):
    _, lp, width = p5.shape
    taps = conv_w.shape[1]

    def body(x_ref, halo_ref, w_ref, al_ref, dt_ref, o_ref, xs):
        i = pl.program_id(1)
        for s in range(3):
            xs[s, 0:HALO, :] = jnp.where(i > 0, halo_ref[s], 0.0)
            xs[s, HALO:, :] = x_ref[s]
            c = _conv_fwd(xs.at[s], lambda j, s=s: w_ref[s, j:j + 1, :], taps, tm)
            y, _ = _silu_parts(c)
            if s < 2:
                y = y * lax.rsqrt(_head_sum(y * y) + L2_EPS)
                if s == 0:
                    y = y * Q_SCALE
            o_ref[s] = y
        o_ref[3] = _sigmoid(x_ref[3])
        o_ref[4] = -jnp.exp(al_ref[...]) * _softplus(x_ref[4] + dt_ref[...])

    return pl.pallas_call(
        body,
        name="gdn_pre_fwd",
        grid=(width // cb, lp // tm),
        in_specs=[
            pl.BlockSpec((5, tm, cb), lambda j, i: (0, i, j)),
            pl.BlockSpec((3, HALO, cb), lambda j, i: (0, _halo_index(i, tm), j)),
            pl.BlockSpec((3, taps, cb), lambda j, i: (0, 0, j)),
            pl.BlockSpec((1, cb), lambda j, i: (0, j)),
            pl.BlockSpec((1, cb), lambda j, i: (0, j)),
        ],
        out_specs=pl.BlockSpec((5, tm, cb), lambda j, i: (0, i, j)),
        out_shape=jax.ShapeDtypeStruct((5, lp, width), F32),
        scratch_shapes=[pltpu.VMEM((3, tm + HALO, cb), F32)],
        compiler_params=_params(("arbitrary", "arbitrary")),
    )(p5, p5, conv_w, alog_b, dtb_b)


def gdn_pre_bwd(p5, dqkvbg, conv_w, alog_b, dtb_b, *, tm, cb):
    _, lp, width = p5.shape
    taps = conv_w.shape[1]
    last = lp // tm - 1

    def body(x_ref, halo_ref, d_ref, w_ref, al_ref, dt_ref, dx_ref, dw_ref, dsc_ref, xs, dcs, carry):
        step = pl.program_id(1)
        tile = last - step

        @pl.when(step == 0)
        def _():
            carry[...] = jnp.zeros_like(carry)
            dw_ref[...] = jnp.zeros_like(dw_ref)
            dsc_ref[...] = jnp.zeros_like(dsc_ref)

        for s in range(3):
            w = lambda j, s=s: w_ref[s, j:j + 1, :]
            xs[s, 0:HALO, :] = jnp.where(tile > 0, halo_ref[s], 0.0)
            xs[s, HALO:, :] = x_ref[s]
            c = _conv_fwd(xs.at[s], w, taps, tm)
            y, dsilu = _silu_parts(c)
            dy = d_ref[s]
            if s < 2:
                rn = lax.rsqrt(_head_sum(y * y) + L2_EPS)
                yn = y * rn
                if s == 0:
                    dy = dy * Q_SCALE
                dy = rn * (dy - yn * _head_sum(dy * yn))
            dc = dy * dsilu
            dcs[s, 0:tm, :] = dc
            dcs[s, tm:, :] = carry[s]
            dx_ref[s] = _conv_bwd_x(dcs.at[s], w, taps, tm).astype(dx_ref.dtype)
            carry[s] = dc[0:HALO, :]
            for j in range(taps):
                dw_ref[s, j] += _row_partial(dc * xs[s, pl.ds(HALO - taps + 1 + j, tm), :])
        beta = _sigmoid(x_ref[3])
        dx_ref[3] = (d_ref[3] * beta * (1.0 - beta)).astype(dx_ref.dtype)
        z = x_ref[4] + dt_ref[...]
        dg = d_ref[4] * -jnp.exp(al_ref[...])
        da = dg * _sigmoid(z)
        dx_ref[4] = da.astype(dx_ref.dtype)
        dsc_ref[0] += _row_partial(dg * _softplus(z))
        dsc_ref[1] += _row_partial(da)

    tile_spec = pl.BlockSpec((5, tm, cb), lambda j, i: (0, last - i, j))
    return pl.pallas_call(
        body,
        name="gdn_pre_bwd",
        grid=(width // cb, lp // tm),
        in_specs=[
            tile_spec,
            pl.BlockSpec((3, HALO, cb), lambda j, i: (0, _halo_index(last - i, tm), j)),
            tile_spec,
            pl.BlockSpec((3, taps, cb), lambda j, i: (0, 0, j)),
            pl.BlockSpec((1, cb), lambda j, i: (0, j)),
            pl.BlockSpec((1, cb), lambda j, i: (0, j)),
        ],
        out_specs=[
            tile_spec,
            pl.BlockSpec((3, taps, SUB, cb), lambda j, i: (0, 0, 0, j)),
            pl.BlockSpec((2, SUB, cb), lambda j, i: (0, 0, j)),
        ],
        out_shape=[
            jax.ShapeDtypeStruct((5, lp, width), BF16),
            jax.ShapeDtypeStruct((3, taps, SUB, width), F32),
            jax.ShapeDtypeStruct((2, SUB, width), F32),
        ],
        scratch_shapes=[
            pltpu.VMEM((3, tm + HALO, cb), F32),
            pltpu.VMEM((3, tm + HALO, cb), F32),
            pltpu.VMEM((3, HALO, cb), F32),
        ],
        compiler_params=_params(("arbitrary", "arbitrary")),
    )(p5, p5, dqkvbg, conv_w, alog_b, dtb_b)


def gdn_post_fwd(o, z, nw_b, *, tm):
    _, lp, width = o.shape

    def body(o_ref, z_ref, nw_ref, y_ref):
        ov = o_ref[...]
        rn = lax.rsqrt(_head_sum(ov * ov) * (1.0 / HEAD_DIM) + RMS_EPS)
        gate, _ = _silu_parts(z_ref[...])
        y_ref[...] = (ov * rn * nw_ref[...] * gate).astype(y_ref.dtype)

    row = pl.BlockSpec((None, tm, width), lambda i: (0, i, 0))
    return pl.pallas_call(
        body,
        name="gdn_post_fwd",
        grid=(lp // tm,),
        in_specs=[row, row, pl.BlockSpec((1, width), lambda i: (0, 0))],
        out_specs=row,
        out_shape=jax.ShapeDtypeStruct((1, lp, width), BF16),
        compiler_params=_params(("arbitrary",)),
    )(o, z, nw_b)


def gdn_post_bwd(o, z, dy, nw_b, *, tm):
    _, lp, width = o.shape

    def body(o_ref, z_ref, dy_ref, nw_ref, do_ref, dz_ref, dnw_ref):
        ov = o_ref[...]
        rn = lax.rsqrt(_head_sum(ov * ov) * (1.0 / HEAD_DIM) + RMS_EPS)
        yn = ov * rn
        gate, dgate = _silu_parts(z_ref[...])
        d_on = dy_ref[...] * gate
        dz_ref[...] = (dy_ref[...] * yn * nw_ref[...] * dgate).astype(dz_ref.dtype)
        a = d_on * nw_ref[...]
        do_ref[...] = rn * (a - yn * (_head_sum(a * yn) * (1.0 / HEAD_DIM)))

        @pl.when(pl.program_id(0) == 0)
        def _():
            dnw_ref[...] = jnp.zeros_like(dnw_ref)

        dnw_ref[...] += _row_partial(d_on * yn)

    row = pl.BlockSpec((None, tm, width), lambda i: (0, i, 0))
    return pl.pallas_call(
        body,
        name="gdn_post_bwd",
        grid=(lp // tm,),
        in_specs=[row, row, row, pl.BlockSpec((1, width), lambda i: (0, 0))],
        out_specs=[row, row, pl.BlockSpec((8, width), lambda i: (0, 0))],
        out_shape=[jax.ShapeDtypeStruct((1, lp, width), F32), jax.ShapeDtypeStruct((1, lp, width), BF16),
                   jax.ShapeDtypeStruct((8, width), F32)],
        compiler_params=_params(("arbitrary",)),
    )(o, z, dy, nw_b)


def head_lane_sum(x):
    s_n, rows, width = x.shape

    def body(x_ref, o_ref):
        lane = lax.broadcasted_iota(jnp.int32, (rows, HEAD_DIM), 1)
        acc = jnp.zeros((rows, HEAD_DIM), F32)
        for h in range(width // HEAD_DIM):
            s = jnp.sum(x_ref[:, h * HEAD_DIM:(h + 1) * HEAD_DIM], axis=-1, keepdims=True)
            acc = jnp.where(lane == h, s, acc)
        o_ref[...] = acc

    return pl.pallas_call(
        body,
        name="head_lane_sum",
        grid=(s_n,),
        in_specs=[pl.BlockSpec((None, rows, width), lambda s: (s, 0, 0))],
        out_specs=pl.BlockSpec((None, rows, HEAD_DIM), lambda s: (s, 0, 0)),
        out_shape=jax.ShapeDtypeStruct((s_n, rows, HEAD_DIM), F32),
        compiler_params=_params(("arbitrary",)),
    )(x)


def ffn_act_fwd(up, conv_w, *, tm, name):
    _, lp, c_w = up.shape
    taps = conv_w.shape[1]

    def body(u_ref, halo_ref, g_ref, w_ref, o_ref):
        first_tile = pl.program_id(1) == 0

        def strip(cur, prev, rows, cs):
            conv = w_ref[taps - 1:taps, cs] * cur
            for j in range(taps - 1):
                conv += w_ref[j:j + 1, cs] * _shift_down(cur, prev, taps - 1 - j)
            y, _ = _silu_parts(conv)
            return y * g_ref[rows, cs]

        def pair(r0, above_of):
            top, bot = _pair_rows(r0)
            for c0 in range(0, c_w, LANES):
                cs = slice(c0, c0 + LANES)
                cur_t, cur_b = u_ref[top, cs], u_ref[bot, cs]
                out = [strip(cur_t, above_of(cs), top, cs), strip(cur_b, cur_t, bot, cs)]
                o_ref[pl.ds(r0, PAIR), cs] = jnp.concatenate(out, axis=0).astype(o_ref.dtype)

        pair(0, lambda cs: jnp.where(first_tile, 0.0, halo_ref[:, cs]))

        def loop_body(s, carry):
            r0 = pl.multiple_of(s * PAIR, PAIR)
            pair(r0, lambda cs: u_ref[pl.ds(pl.multiple_of(r0 - SUB, SUB), SUB), cs])
            return carry

        lax.fori_loop(1, tm // PAIR, loop_body, 0, unroll=STRIP_UNROLL)

    return pl.pallas_call(
        body,
        name=name,
        grid=(2, lp // tm),
        in_specs=[
            pl.BlockSpec((None, tm, c_w), lambda s, i: (s, i, 0)),
            pl.BlockSpec((None, HALO, c_w), lambda s, i: (s, _halo_index(i, tm), 0)),
            pl.BlockSpec((None, tm, c_w), lambda s, i: (2 + s, i, 0)),
            pl.BlockSpec((None, taps, c_w), lambda s, i: (s, 0, 0)),
        ],
        out_specs=pl.BlockSpec((None, tm, c_w), lambda s, i: (s, i, 0)),
        out_shape=jax.ShapeDtypeStruct((2, lp, c_w), BF16),
        compiler_params=_params(("arbitrary", "arbitrary")),
    )(up, up, up, conv_w)


def ffn_act_bwd(up, dact, conv_w, *, tm, name):
    _, lp, c_w = up.shape
    taps = conv_w.shape[1]
    last = lp // tm - 1
    n_pairs = tm // PAIR

    def body(u_ref, halo_ref, g_ref, d_ref, w_ref, dup_ref, dw_ref, below):
        step = pl.program_id(1)
        first_tile = step == last

        @pl.when(step == 0)
        def _():
            below[...] = jnp.zeros_like(below)
            dw_ref[...] = jnp.zeros_like(dw_ref)

        def strip(cur, prev, rows, cs, nxt):
            shifted = [_shift_down(cur, prev, taps - 1 - j) for j in range(taps)]
            conv = w_ref[0:1, cs] * shifted[0]
            for j in range(1, taps):
                conv += w_ref[j:j + 1, cs] * shifted[j]
            y, dsilu = _silu_parts(conv)
            d = d_ref[rows, cs]
            dc = d * g_ref[rows, cs] * dsilu
            dx = w_ref[taps - 1:taps, cs] * dc
            for j in range(taps - 1):
                dx += w_ref[j:j + 1, cs] * _shift_up(dc, nxt, taps - 1 - j)
            return dx, d * y, dc, [dc * s for s in shifted]

        def pair(r0, above_of):
            top, bot = _pair_rows(r0)
            both = pl.ds(r0, PAIR)
            for c0 in range(0, c_w, LANES):
                cs = slice(c0, c0 + LANES)
                cur_t, cur_b = u_ref[top, cs], u_ref[bot, cs]
                dx_b, dg_b, dc_b, dw_b = strip(cur_b, cur_t, bot, cs, below[:, cs])
                dx_t, dg_t, dc_t, dw_t = strip(cur_t, above_of(cs), top, cs, dc_b)
                below[:, cs] = dc_t
                dup_ref[0, both, cs] = jnp.concatenate([dx_t, dx_b], axis=0).astype(dup_ref.dtype)
                dup_ref[1, both, cs] = jnp.concatenate([dg_t, dg_b], axis=0).astype(dup_ref.dtype)
                for j in range(taps):
                    dw_ref[j, :, cs] += dw_t[j] + dw_b[j]

        def loop_body(it, carry):
            r0 = pl.multiple_of((n_pairs - 1 - it) * PAIR, PAIR)
            pair(r0, lambda cs: u_ref[pl.ds(pl.multiple_of(r0 - SUB, SUB), SUB), cs])
            return carry

        lax.fori_loop(0, n_pairs - 1, loop_body, 0, unroll=STRIP_UNROLL)
        pair(0, lambda cs: jnp.where(first_tile, 0.0, halo_ref[:, cs]))

    return pl.pallas_call(
        body,
        name=name,
        grid=(2, lp // tm),
        in_specs=[
            pl.BlockSpec((None, tm, c_w), lambda s, i: (s, last - i, 0)),
            pl.BlockSpec((None, HALO, c_w), lambda s, i: (s, _halo_index(last - i, tm), 0)),
            pl.BlockSpec((None, tm, c_w), lambda s, i: (2 + s, last - i, 0)),
            pl.BlockSpec((None, tm, c_w), lambda s, i: (s, last - i, 0)),
            pl.BlockSpec((None, taps, c_w), lambda s, i: (s, 0, 0)),
        ],
        out_specs=[
            pl.BlockSpec((2, None, tm, c_w), lambda s, i: (0, s, last - i, 0)),
            pl.BlockSpec((None, taps, SUB, c_w), lambda s, i: (s, 0, 0, 0)),
        ],
        out_shape=[jax.ShapeDtypeStruct((2, 2, lp, c_w), BF16), jax.ShapeDtypeStruct((2, taps, SUB, c_w), F32)],
        scratch_shapes=[pltpu.VMEM((SUB, c_w), F32)],
        compiler_params=_params(("arbitrary", "arbitrary")),
    )(up, up, up, dact, conv_w)


def sc_fwd(pb, conv_w, *, tm, cb):
    _, lp, width = pb.shape
    taps = conv_w.shape[0]

    def body(x_ref, halo_ref, w_ref, o_ref):
        first_tile = pl.program_id(1) == 0

        def strip(cur, prev, rows, cs):
            conv = w_ref[taps - 1:taps, cs] * cur
            for j in range(taps - 1):
                conv += w_ref[j:j + 1, cs] * _shift_down(cur, prev, taps - 1 - j)
            return x_ref[0, rows, cs] * conv

        def pair(r0, above_of):
            top, bot = _pair_rows(r0)
            for c0 in range(0, cb, LANES):
                cs = slice(c0, c0 + LANES)
                cur_t = x_ref[1, top, cs] * x_ref[2, top, cs]
                cur_b = x_ref[1, bot, cs] * x_ref[2, bot, cs]
                out = [strip(cur_t, above_of(cs), top, cs), strip(cur_b, cur_t, bot, cs)]
                o_ref[pl.ds(r0, PAIR), cs] = jnp.concatenate(out, axis=0).astype(o_ref.dtype)

        pair(0, lambda cs: jnp.where(first_tile, 0.0, halo_ref[1, :, cs] * halo_ref[2, :, cs]))

        def loop_body(k, carry):
            r0 = pl.multiple_of(k * PAIR, PAIR)
            before = pl.ds(pl.multiple_of(r0 - SUB, SUB), SUB)
            pair(r0, lambda cs: x_ref[1, before, cs] * x_ref[2, before, cs])
            return carry

        lax.fori_loop(1, tm // PAIR, loop_body, 0, unroll=STRIP_UNROLL)

    return pl.pallas_call(
        body,
        name="sc_fwd",
        grid=(width // cb, lp // tm),
        in_specs=[
            pl.BlockSpec((3, tm, cb), lambda j, i: (0, i, j)),
            pl.BlockSpec((3, HALO, cb), lambda j, i: (0, _halo_index(i, tm), j)),
            pl.BlockSpec((taps, cb), lambda j, i: (0, j)),
        ],
        out_specs=pl.BlockSpec((None, tm, cb), lambda j, i: (0, i, j)),
        out_shape=jax.ShapeDtypeStruct((1, lp, width), BF16),
        compiler_params=_params(("arbitrary", "arbitrary")),
    )(pb, pb, conv_w)


def sc_bwd(pb, ds, conv_w, *, tm, cb):
    _, lp, width = pb.shape
    taps = conv_w.shape[0]
    last = lp // tm - 1
    n_pairs = tm // PAIR

    def body(x_ref, halo_ref, d_ref, w_ref, dx_ref, dw_ref, below):
        step = pl.program_id(1)
        first_tile = step == last

        @pl.when(step == 0)
        def _():
            below[...] = jnp.zeros_like(below)
            dw_ref[...] = jnp.zeros_like(dw_ref)

        def strip(cur, prev, rows, cs, nxt):
            gate, left, right = x_ref[0, rows, cs], x_ref[1, rows, cs], x_ref[2, rows, cs]
            shifted = [_shift_down(cur, prev, taps - 1 - j) for j in range(taps)]
            conv = w_ref[0:1, cs] * shifted[0]
            for j in range(1, taps):
                conv += w_ref[j:j + 1, cs] * shifted[j]
            d = d_ref[rows, cs]
            dc = d * gate
            dp = w_ref[taps - 1:taps, cs] * dc
            for j in range(taps - 1):
                dp += w_ref[j:j + 1, cs] * _shift_up(dc, nxt, taps - 1 - j)
            return d * conv, dp * right, dp * left, dc, [dc * s for s in shifted]

        def pair(r0, above_of):
            top, bot = _pair_rows(r0)
            both = pl.ds(r0, PAIR)
            for c0 in range(0, cb, LANES):
                cs = slice(c0, c0 + LANES)
                cur_t = x_ref[1, top, cs] * x_ref[2, top, cs]
                cur_b = x_ref[1, bot, cs] * x_ref[2, bot, cs]
                *dx_b, dc_b, dw_b = strip(cur_b, cur_t, bot, cs, below[:, cs])
                *dx_t, dc_t, dw_t = strip(cur_t, above_of(cs), top, cs, dc_b)
                below[:, cs] = dc_t
                for s in range(3):
                    dx_ref[s, both, cs] = jnp.concatenate([dx_t[s], dx_b[s]], axis=0).astype(dx_ref.dtype)
                for j in range(taps):
                    dw_ref[j, :, cs] += dw_t[j] + dw_b[j]

        def loop_body(it, carry):
            r0 = pl.multiple_of((n_pairs - 1 - it) * PAIR, PAIR)
            before = pl.ds(pl.multiple_of(r0 - SUB, SUB), SUB)
            pair(r0, lambda cs: x_ref[1, before, cs] * x_ref[2, before, cs])
            return carry

        lax.fori_loop(0, n_pairs - 1, loop_body, 0, unroll=STRIP_UNROLL)
        pair(0, lambda cs: jnp.where(first_tile, 0.0, halo_ref[1, :, cs] * halo_ref[2, :, cs]))

    tile_spec = pl.BlockSpec((3, tm, cb), lambda j, i: (0, last - i, j))
    return pl.pallas_call(
        body,
        name="sc_bwd",
        grid=(width // cb, lp // tm),
        in_specs=[
            tile_spec,
            pl.BlockSpec((3, HALO, cb), lambda j, i: (0, _halo_index(last - i, tm), j)),
            pl.BlockSpec((None, tm, cb), lambda j, i: (0, last - i, j)),
            pl.BlockSpec((taps, cb), lambda j, i: (0, j)),
        ],
        out_specs=[tile_spec, pl.BlockSpec((taps, SUB, cb), lambda j, i: (0, 0, j))],
        out_shape=[jax.ShapeDtypeStruct((3, lp, width), BF16), jax.ShapeDtypeStruct((taps, SUB, width), F32)],
        scratch_shapes=[pltpu.VMEM((SUB, cb), F32)],
        compiler_params=_params(("arbitrary", "arbitrary")),
    )(pb, pb, ds, conv_w)


TILE_BYTES = 1536 * 1024


def _rows_tile(rows, cols, multiple=8):
    if rows * cols * 4 <= TILE_BYTES or rows % multiple:
        return rows
    best = multiple
    for t in range(multiple, rows + 1, multiple):
        if rows % t == 0 and t * cols * 4 <= TILE_BYTES:
            best = t
    return best


def pair_sum(g, landed, core, out_dtype, name):
    _, rows, cols = g.shape
    half = rows // 2
    tr = _rows_tile(half, cols, 16)
    nb = half // tr

    def body(c_ref, g_ref, l_ref, o_ref):
        o_ref[...] = (g_ref[...] + l_ref[...]).astype(out_dtype)

    return pl.pallas_call(
        body,
        name=name,
        grid_spec=pltpu.PrefetchScalarGridSpec(
            num_scalar_prefetch=1,
            grid=(4, nb),
            in_specs=[
                pl.BlockSpec((None, tr, cols), lambda s, i, c: (s, c[0] * nb + i, 0)),
                pl.BlockSpec((None, tr, cols), lambda s, i, c: (s, i, 0)),
            ],
            out_specs=pl.BlockSpec((None, tr, cols), lambda s, i, c: (s, i, 0)),
        ),
        out_shape=jax.ShapeDtypeStruct((4, half, cols), out_dtype),
        compiler_params=_params(("arbitrary", "arbitrary")),
    )(core, g, landed)


def chip_sum(x, name):
    _, rows, cols = x.shape
    tr = _rows_tile(rows, cols, 16)

    def body(x0, x1, x2, x3, o_ref):
        acc = x0[...].astype(F32) + x1[...].astype(F32)
        o_ref[...] = (acc + x2[...].astype(F32)) + x3[...].astype(F32)

    return pl.pallas_call(
        body,
        name=name,
        grid=(rows // tr,),
        in_specs=[pl.BlockSpec((None, tr, cols), lambda i, k=k: (k, i, 0)) for k in range(4)],
        out_specs=pl.BlockSpec((tr, cols), lambda i: (i, 0)),
        out_shape=jax.ShapeDtypeStruct((rows, cols), F32),
        compiler_params=_params(("arbitrary",)),
    )(x, x, x, x)


def adamw(w, g, m, v, name):
    shape = w.shape
    cols = shape[-1]
    rows = w.size // cols
    tr = _rows_tile(rows, cols)

    def body(w_ref, g_ref, m_ref, v_ref, d_ref, m2_ref, v2_ref):
        gv = g_ref[...]
        m2 = ADAM_B1 * m_ref[...] + (1.0 - ADAM_B1) * gv
        v2 = ADAM_B2 * v_ref[...] + (1.0 - ADAM_B2) * (gv * gv)
        m_hat = m2 / (1.0 - ADAM_B1 ** ADAM_STEP)
        v_hat = v2 / (1.0 - ADAM_B2 ** ADAM_STEP)
        d_ref[...] = -ADAM_LR * (m_hat / (jnp.sqrt(v_hat) + ADAM_EPS) + ADAM_WD * w_ref[...])
        m2_ref[...] = m2
        v2_ref[...] = v2

    spec = pl.BlockSpec((tr, cols), lambda i: (i, 0))
    outs = pl.pallas_call(
        body,
        name=name,
        grid=(rows // tr,),
        in_specs=[spec] * 4,
        out_specs=[spec] * 3,
        out_shape=[jax.ShapeDtypeStruct((rows, cols), F32)] * 3,
        compiler_params=_params(("arbitrary",)),
    )(*[t.reshape(rows, cols) for t in (w, g, m, v)])
    return tuple(o.reshape(shape) for o in outs)


MESH_ID = pl.DeviceIdType.MESH
ANY = pl.BlockSpec(memory_space=pl.ANY)


def _place():
    x, y, c = lax.axis_index("x"), lax.axis_index("y"), lax.axis_index("c")
    other_chips = [(1 - x, y), (x, 1 - y), (1 - x, 1 - y)]
    return x, y, c, other_chips


def all_gather_shards(bufs, name):
    n = len(bufs)

    def body(*refs):
        x_refs, o_refs = refs[:n], refs[n:2 * n]
        copies = _gather_copies(x_refs, o_refs, *refs[2 * n:])
        _gather_start(copies)
        _gather_finish(copies)

    outs = pl.pallas_call(
        body,
        name=name,
        in_specs=[ANY] * n,
        out_specs=[ANY] * n,
        out_shape=_gather_out_shapes(bufs),
        scratch_shapes=_gather_sems(n),
    )(*bufs)
    return _set_own_slots(outs, bufs)


def _gather_out_shapes(bufs):
    return [jax.ShapeDtypeStruct((4,) + b.shape, b.dtype) for b in bufs]


def _gather_sems(n):
    return [pltpu.SemaphoreType.DMA((6 * n,)), pltpu.SemaphoreType.DMA((6 * n,))]


def _set_own_slots(outs, bufs):
    if not outs:
        return []
    me = 2 * lax.axis_index("x") + lax.axis_index("y")
    return [lax.dynamic_update_index_in_dim(o, b, me, 0) for o, b in zip(outs, bufs)]


def _gather_copies(x_refs, o_refs, send_sems, recv_sems):
    x, y, c, chips = _place()
    me = 2 * x + y
    sibling = (x, y, 1 - c)

    def part(a, slot, hf):
        half = x_refs[a].shape[0] // 2
        return o_refs[a].at[slot, pl.ds(hf * half, half), :]

    def mine(a):
        half = x_refs[a].shape[0] // 2
        return x_refs[a].at[pl.ds(c * half, half), :]

    def copy(k, src, dst, to):
        return pltpu.make_async_remote_copy(src_ref=src, dst_ref=dst, send_sem=send_sems.at[k],
                                            recv_sem=recv_sems.at[k], device_id=to, device_id_type=MESH_ID)

    sends, arrivals, passes, passed = [], [], [], []
    for a in range(len(x_refs)):
        for j, (px, py) in enumerate(chips):
            landed, theirs = part(a, 2 * px + py, c), part(a, 2 * px + py, 1 - c)
            sends.append(copy(6 * a + j, mine(a), part(a, me, c), (px, py, c)))
            arrivals.append(copy(6 * a + j, mine(a), landed, (px, py, c)))
            passes.append(copy(6 * a + 3 + j, landed, landed, sibling))
            passed.append(copy(6 * a + 3 + j, theirs, theirs, sibling))
    return sends, arrivals, passes, passed


def _gather_start(copies):
    for cp in copies[0]:
        cp.start()


def _gather_finish(copies):
    sends, arrivals, passes, passed = copies
    for arrival, cp in zip(arrivals, passes):
        arrival.wait_recv()
        cp.start()
    for cp in passed:
        cp.wait_recv()
    for cp in sends + passes:
        cp.wait_send()


def swap_halves(bufs, name):
    n = len(bufs)

    def body(*refs):
        x_refs, o_refs = refs[:n], refs[n:2 * n]
        send_sems, recv_sems = refs[2 * n:]
        x, y, c, _ = _place()
        copies = []
        for a in range(n):
            half = bufs[a].shape[1] // 2
            cp = pltpu.make_async_remote_copy(src_ref=x_refs[a].at[:, pl.ds((1 - c) * half, half), :], dst_ref=o_refs[a],
                                              send_sem=send_sems.at[a], recv_sem=recv_sems.at[a],
                                              device_id=(x, y, 1 - c), device_id_type=MESH_ID)
            cp.start()
            copies.append(cp)
        for cp in copies:
            cp.wait()

    return pl.pallas_call(
        body,
        name=name,
        in_specs=[ANY] * n,
        out_specs=[ANY] * n,
        out_shape=[jax.ShapeDtypeStruct((4, b.shape[1] // 2, b.shape[2]), b.dtype) for b in bufs],
        scratch_shapes=[pltpu.SemaphoreType.DMA((n,)), pltpu.SemaphoreType.DMA((n,))],
    )(*bufs)


def scatter_to_chips(bufs, name):
    n = len(bufs)

    def body(*refs):
        x_refs, o_refs = refs[:n], refs[n:2 * n]
        copies = _scatter_copies(x_refs, o_refs, *refs[2 * n:])
        _scatter_start(copies)
        _scatter_finish(copies)

    outs = pl.pallas_call(
        body,
        name=name,
        in_specs=[ANY] * n,
        out_specs=[ANY] * n,
        out_shape=[jax.ShapeDtypeStruct(b.shape, b.dtype) for b in bufs],
        scratch_shapes=_scatter_sems(n),
    )(*bufs)
    return _keep_own_slots(outs, bufs)


def _scatter_sems(n):
    return [pltpu.SemaphoreType.DMA((3 * n,)), pltpu.SemaphoreType.DMA((3 * n,))]


def _keep_own_slots(outs, bufs):
    if not outs:
        return []
    me = 2 * lax.axis_index("x") + lax.axis_index("y")
    return [lax.dynamic_update_index_in_dim(o, lax.dynamic_index_in_dim(b, me, 0, keepdims=False), me, 0)
            for o, b in zip(outs, bufs)]


def _scatter_copies(x_refs, o_refs, send_sems, recv_sems):
    x, y, c, chips = _place()
    me = 2 * x + y

    def copy(a, j, src_slot, dst_slot, px, py):
        return pltpu.make_async_remote_copy(src_ref=x_refs[a].at[src_slot], dst_ref=o_refs[a].at[dst_slot],
                                            send_sem=send_sems.at[3 * a + j], recv_sem=recv_sems.at[3 * a + j],
                                            device_id=(px, py, c), device_id_type=MESH_ID)

    sends = [copy(a, j, 2 * px + py, me, px, py) for a in range(len(x_refs)) for j, (px, py) in enumerate(chips)]
    arrivals = [copy(a, j, me, 2 * px + py, px, py) for a in range(len(x_refs)) for j, (px, py) in enumerate(chips)]
    return sends, arrivals


def _scatter_start(copies):
    for cp in copies[0]:
        cp.start()


def _scatter_finish(copies):
    for cp in copies[1]:
        cp.wait_recv()
    for cp in copies[0]:
        cp.wait_send()


def share_halves(groups, name):
    bufs = [b for grp in groups for b in grp]
    where = [(gi, li) for gi, grp in enumerate(groups) for li in range(len(grp))]
    n = len(bufs)

    def body(*refs):
        x_refs, o_refs = refs[:n], refs[n:n + len(groups)]
        send_sems, recv_sems = refs[n + len(groups):]
        x, y, c, _ = _place()
        sent, arrive = [], []
        for a, (gi, li) in enumerate(where):

            def copy(hf, a=a, gi=gi, li=li):
                return pltpu.make_async_remote_copy(src_ref=x_refs[a], dst_ref=o_refs[gi].at[li, hf],
                                                    send_sem=send_sems.at[a], recv_sem=recv_sems.at[a],
                                                    device_id=(x, y, 1 - c), device_id_type=MESH_ID)

            sent.append(copy(c))
            arrive.append(copy(1 - c))
        for cp in sent:
            cp.start()
        for cp in arrive:
            cp.wait_recv()
        for cp in sent:
            cp.wait_send()

    outs = pl.pallas_call(
        body,
        name=name,
        in_specs=[ANY] * n,
        out_specs=[ANY] * len(groups),
        out_shape=[jax.ShapeDtypeStruct((len(grp), 2) + grp[0].shape, grp[0].dtype) for grp in groups],
        scratch_shapes=[pltpu.SemaphoreType.DMA((n,)), pltpu.SemaphoreType.DMA((n,))],
    )(*bufs)
    c = lax.axis_index("c")
    full = [lax.dynamic_update_index_in_dim(o, jnp.stack(grp), c, 1) for o, grp in zip(outs, groups)]
    return [t.reshape(t.shape[0], 2 * t.shape[2], t.shape[3]) for t in full]


def pair_sums(bufs, dtypes, tag):
    core = lax.axis_index("c").astype(jnp.int32).reshape(1)
    landed = swap_halves(bufs, "rs_pair_" + tag)
    return [pair_sum(b, l, core, dt, "rs_pair_sum_%s%d" % (tag, i)) for i, (b, l, dt) in enumerate(zip(bufs, landed, dtypes))]


def _row_tiles(length):
    return (640, 320) if length > 2048 else (128, 64)


def _divisor_tile(rows, target):
    return max(t for t in range(8, min(rows, target) + 1, 8) if rows % t == 0)


def _local_step(x, target, wt, late_shards, layout_late, reduce_early):
    seq, d = x.shape
    length = N_META + seq
    tm, tm_ffn = _row_tiles(length)
    lp = -(-length // tm) * tm
    tail = jnp.zeros((lp - length, d), F32)
    h0 = jnp.concatenate([wt["meta"], x, tail], axis=0)[None]
    tgt = jnp.concatenate([jnp.zeros((N_META, d), F32), target, tail], axis=0)
    nn = functools.partial(mm_nn, tm=_divisor_tile(lp, 1664))
    nt = functools.partial(mm_nt, tm=_divisor_tile(lp, 1040))
    tn = functools.partial(mm_tn, tm=_divisor_tile(lp, 1664), rb=256)
    nn_ln = functools.partial(mm_nn_ln, tm=_divisor_tile(lp, 832))
    nt_ln_bwd = functools.partial(mm_nt_ln_bwd, tm=_divisor_tile(lp, 832))
    ln_g = [wt["ln_mix_g"][0:1], wt["ln_ffn_g"][0:1], wt["ln_mix_g"][1:2], wt["ln_ffn_g"][1:2]]
    ln_b = [wt["ln_mix_b"][0:1], wt["ln_ffn_b"][0:1], wt["ln_mix_b"][1:2], wt["ln_ffn_b"][1:2]]

    p5 = nn(h0, wt["a5"], name="a_in5")
    pz = nn(h0, wt["az"], name="a_inz")
    qkvbg = gdn_pre_fwd(p5, wt["a_conv3"], wt["alog_b"], wt["dtb_b"], tm=tm, cb=2 * HEAD_DIM)
    o, states, tinv, late_stacks = gdn_chunk_fwd(qkvbg, late_shards)
    wt = {**wt, **layout_late(late_stacks)}
    onz = gdn_post_fwd(o[None], pz, wt["anorm_b"], tm=tm)
    r1, h1 = nn_ln(onz, wt["a_out"], h0, ln_g[0], ln_b[0], name="a_out_ln1")
    up0 = nn(h1, wt["up"][0], name="up0")
    act0 = ffn_act_fwd(up0, wt["fconv"][0], tm=tm_ffn, name="ffn_act0")
    r2, h2 = nn_ln(act0, wt["down"][0], h1, ln_g[1], ln_b[1], name="down0_ln2")
    pb = nn(h2, wt["b_in"], name="b_in")
    sc = sc_fwd(pb, wt["b_conv"], tm=tm_ffn, cb=d)
    r3, h3 = nn_ln(sc, wt["b_out"], h2, ln_g[2], ln_b[2], name="b_out_ln3")
    up1 = nn(h3, wt["up"][1], name="up1")
    act1 = ffn_act_fwd(up1, wt["fconv"][1], tm=tm_ffn, name="ffn_act1")
    r4, h4 = nn_ln(act1, wt["down"][1], h3, ln_g[3], ln_b[3], name="down1_ln4")

    dh4, loss_part = loss_grad(h4, tgt, first=N_META, count=seq, tm=tm)

    grads = {}
    dr4, dgb4 = ln_bwd(r4, dh4, ln_g[3], tm=tm, name="ln4_bwd")
    d_down1 = tn(act1, dr4, name="d_down1")
    dact1 = nt(dr4, wt["down"][1], name="d_act1")
    dup1, dfconv1 = ffn_act_bwd(up1, dact1, wt["fconv"][1], tm=tm_ffn, name="ffn_act1_bwd")
    dup1 = dup1.reshape(up1.shape)
    d_up1 = tn(h3, dup1, name="d_up1")

    dr3, dgb3 = nt_ln_bwd(dup1, wt["up"][1], dr4, r3, ln_g[2], name="d_h3_ln3")
    d_bout = tn(sc, dr3, name="d_b_out")
    dsc = nt(dr3, wt["b_out"], name="d_sc")
    dpb, dbconv = sc_bwd(pb, dsc, wt["b_conv"], tm=tm_ffn, cb=d)
    d_bin = tn(h2, dpb, name="d_b_in")

    dr2, dgb2 = nt_ln_bwd(dpb, wt["b_in"], dr3, r2, ln_g[1], name="d_h2_ln2")
    d_down0 = tn(act0, dr2, name="d_down0")
    dact0 = nt(dr2, wt["down"][0], name="d_act0")
    dup0, dfconv0 = ffn_act_bwd(up0, dact0, wt["fconv"][0], tm=tm_ffn, name="ffn_act0_bwd")
    dup0 = dup0.reshape(up0.shape)
    d_up0 = tn(h1, dup0, name="d_up0")
    grads["b_w_in"] = [d_bin[0].transpose(1, 0, 2).reshape(d, 4, 3 * d // 4).transpose(1, 0, 2)]
    grads["b_w_out"] = [d_bout.reshape(4, d // 4, d)]
    grads["ffn_w_up"] = [d_up0[0], d_up1[0]]
    grads["ffn_w_down"] = [t.reshape(4, -1, d) for t in (d_down0, d_down1)]
    leaving = reduce_early(grads)

    dr1, dgb1 = nt_ln_bwd(dup0, wt["up"][0], dr2, r1, ln_g[0], name="d_h1_ln1")
    d_aout = tn(onz, dr1, name="d_a_out")
    donz = nt(dr1, wt["a_out"], name="d_onz")
    d_o, dz, dnw = gdn_post_bwd(o[None], pz, donz, wt["anorm_b"], tm=tm)
    dqkvbg, landed = gdn_chunk_bwd(qkvbg, states, tinv, d_o[0], leaving)
    dp5, daconv, dscal = gdn_pre_bwd(p5, dqkvbg, wt["a_conv3"], wt["alog_b"], wt["dtb_b"], tm=tm, cb=2 * HEAD_DIM)
    d_a5 = tn(h0, dp5, name="d_a_in5")
    d_az = tn(h0, dz, name="d_a_inz")
    dh0 = nt(dp5, wt["a5"], res=dr1, res_scale=ALPHA, name="d_h0a")
    dh0 = nt(dz, wt["az"], res=dh0, res_scale=1.0, name="d_h0")

    width = HEADS * HEAD_DIM
    d_ba = head_lane_sum(d_a5[0, 3:5])[:, :, :HEADS]
    d_a_in = jnp.concatenate([d_a5[0, 0], d_a5[0, 1], d_a5[0, 2], d_az[0, 0], d_ba[0], d_ba[1]], axis=1)
    n_in = d_a_in.shape[1] // 4
    grads["a_w_in"] = [d_a_in.reshape(d, 4, n_in).transpose(1, 0, 2)]
    grads["a_w_out"] = [d_aout.reshape(4, width // 4, d)]
    grads["a_conv"] = daconv.sum(axis=2).transpose(1, 0, 2).reshape(1, GDN_CONV, 3 * width)
    per_head = dscal.reshape(2, 8, HEADS, HEAD_DIM).sum(axis=(1, 3))
    grads["a_log"] = per_head[0][None]
    grads["a_dt_bias"] = per_head[1][None]
    grads["a_norm"] = dnw.reshape(8, HEADS, HEAD_DIM).sum(axis=(0, 1))[None]
    grads["b_conv"] = dbconv.sum(axis=1)[None]
    lns = [dgb1, dgb2, dgb3, dgb4]
    grads["ln_mix_g"] = jnp.stack([lns[0][0].sum(0), lns[2][0].sum(0)])
    grads["ln_mix_b"] = jnp.stack([lns[0][1].sum(0), lns[2][1].sum(0)])
    grads["ln_ffn_g"] = jnp.stack([lns[1][0].sum(0), lns[3][0].sum(0)])
    grads["ln_ffn_b"] = jnp.stack([lns[1][1].sum(0), lns[3][1].sum(0)])
    grads["ffn_conv"] = jnp.stack([t.sum(axis=2).transpose(1, 0, 2).reshape(FFN_CONV, -1) for t in (dfconv0, dfconv1)])
    grads["meta"] = dh0[0, :N_META]
    return loss_part, dh0, grads, landed


WEIGHTS = ["meta", "a_w_in", "a_conv", "a_log", "a_dt_bias", "a_norm", "a_w_out", "b_w_in", "b_conv", "b_w_out",
           "ln_mix_g", "ln_mix_b", "ffn_w_up", "ffn_conv", "ffn_w_down", "ln_ffn_g", "ln_ffn_b"]
EARLY_WEIGHTS = ["a_w_in", "a_w_out"]
LATE_WEIGHTS = ["b_w_in", "b_w_out", "ffn_w_up", "ffn_w_down"]
MATMUL_WEIGHTS = EARLY_WEIGHTS + LATE_WEIGHTS
SMALL_SHARDED = ["a_conv", "b_conv", "ffn_conv", "meta"]
REPLICATED = ["a_log", "a_dt_bias", "a_norm", "ln_mix_g", "ln_mix_b", "ln_ffn_g", "ln_ffn_b"]
SHARD_AXIS = {"meta": 1, "a_w_in": 2, "a_conv": 2, "a_w_out": 1, "b_w_in": 2, "b_conv": 2, "b_w_out": 1,
              "ffn_w_up": 2, "ffn_conv": 2, "ffn_w_down": 1}
PACK_COLS = 1024
PACK_ROWS_MULTIPLE = 32


def _pack(pieces, lead=()):
    flat = jnp.concatenate([p.reshape(lead + (-1,)) for p in pieces], axis=-1)
    n = flat.shape[-1]
    rows = -(-n // (PACK_COLS * PACK_ROWS_MULTIPLE)) * PACK_ROWS_MULTIPLE
    flat = jnp.pad(flat, [(0, 0)] * len(lead) + [(0, rows * PACK_COLS - n)])
    return flat.reshape(lead + (rows, PACK_COLS))


def _unpack(buf, shapes, lead=()):
    flat = buf.reshape(lead + (-1,))
    out, off = [], 0
    for shp in shapes:
        n = 1
        for s in shp:
            n *= s
        out.append(flat[..., off:off + n].reshape(lead + tuple(shp)))
        off += n
    return out


def _join_shards(stacked, axis):
    return jnp.concatenate([stacked[k] for k in range(4)], axis=axis)


def _split_shards(full, axis):
    return jnp.stack(jnp.split(full, 4, axis=axis))


def _weight_layers(w, names):
    return [w[n][l].astype(BF16) for n in names for l in range(w[n].shape[0])]


def _per_weight(arrays, w, names):
    it = iter(arrays)
    return {n: [next(it) for _ in range(w[n].shape[0])] for n in names}


def _layout_early(full, w):
    width = HEADS * HEAD_DIM
    wt = {n: w[n] for n in ("ln_mix_g", "ln_mix_b", "ln_ffn_g", "ln_ffn_b")}
    w_in = _join_shards(full["a_w_in"][0], 1)
    d = w_in.shape[0]
    n_ff = full["ffn_conv"].shape[2] // 2
    blocks = [w_in[:, s * width:(s + 1) * width] for s in range(4)]
    b_exp = jnp.repeat(w_in[:, 4 * width:4 * width + HEADS], HEAD_DIM, axis=1)
    a_exp = jnp.repeat(w_in[:, 4 * width + HEADS:], HEAD_DIM, axis=1)
    wt["a5"] = jnp.stack([blocks[0], blocks[1], blocks[2], b_exp, a_exp])[None]
    wt["az"] = blocks[3][None, None]
    wt["a_out"] = full["a_w_out"][0].reshape(1, 1, width, d)
    wt["a_conv3"] = full["a_conv"][0].reshape(GDN_CONV, 3, width).transpose(1, 0, 2)
    wt["b_conv"] = full["b_conv"][0]
    wt["fconv"] = [full["ffn_conv"][l].reshape(FFN_CONV, 2, n_ff).transpose(1, 0, 2) for l in range(2)]
    wt["meta"] = full["meta"]
    wt["alog_b"] = jnp.repeat(w["a_log"][0], HEAD_DIM)[None]
    wt["dtb_b"] = jnp.repeat(w["a_dt_bias"][0], HEAD_DIM)[None]
    wt["anorm_b"] = jnp.tile(w["a_norm"][0], HEADS)[None]
    return wt


def _layout_late(full):
    d = full["b_w_in"][0].shape[1]
    n_ff = full["ffn_w_up"][0].shape[2]
    return {
        "b_in": _join_shards(full["b_w_in"][0], 1).reshape(d, 3, d).transpose(1, 0, 2)[None],
        "b_out": full["b_w_out"][0].reshape(1, 1, d, d),
        "up": [t[None] for t in full["ffn_w_up"]],
        "down": [t.reshape(2, 1, n_ff, d) for t in full["ffn_w_down"]],
    }


def kernel(x, meta, a_w_in, a_conv, a_log, a_dt_bias, a_norm, a_w_out, b_w_in, b_conv, b_w_out, ln_mix_g, ln_mix_b, ffn_w_up, ffn_conv, ffn_w_down, ln_ffn_g, ln_ffn_b, loss_target, m_meta, m_a_w_in, m_a_conv, m_a_log, m_a_dt_bias, m_a_norm, m_a_w_out, m_b_w_in, m_b_conv, m_b_w_out, m_ln_mix_g, m_ln_mix_b, m_ffn_w_up, m_ffn_conv, m_ffn_w_down, m_ln_ffn_g, m_ln_ffn_b, v_meta, v_a_w_in, v_a_conv, v_a_log, v_a_dt_bias, v_a_norm, v_a_w_out, v_b_w_in, v_b_conv, v_b_w_out, v_ln_mix_g, v_ln_mix_b, v_ffn_w_up, v_ffn_conv, v_ffn_w_down, v_ln_ffn_g, v_ln_ffn_b):
    w = dict(meta=meta, a_w_in=a_w_in, a_conv=a_conv, a_log=a_log, a_dt_bias=a_dt_bias, a_norm=a_norm, a_w_out=a_w_out,
             b_w_in=b_w_in, b_conv=b_conv, b_w_out=b_w_out, ln_mix_g=ln_mix_g, ln_mix_b=ln_mix_b, ffn_w_up=ffn_w_up,
             ffn_conv=ffn_conv, ffn_w_down=ffn_w_down, ln_ffn_g=ln_ffn_g, ln_ffn_b=ln_ffn_b)
    m = dict(meta=m_meta, a_w_in=m_a_w_in, a_conv=m_a_conv, a_log=m_a_log, a_dt_bias=m_a_dt_bias, a_norm=m_a_norm,
             a_w_out=m_a_w_out, b_w_in=m_b_w_in, b_conv=m_b_conv, b_w_out=m_b_w_out, ln_mix_g=m_ln_mix_g,
             ln_mix_b=m_ln_mix_b, ffn_w_up=m_ffn_w_up, ffn_conv=m_ffn_conv, ffn_w_down=m_ffn_w_down,
             ln_ffn_g=m_ln_ffn_g, ln_ffn_b=m_ln_ffn_b)
    v = dict(meta=v_meta, a_w_in=v_a_w_in, a_conv=v_a_conv, a_log=v_a_log, a_dt_bias=v_a_dt_bias, a_norm=v_a_norm,
             a_w_out=v_a_w_out, b_w_in=v_b_w_in, b_conv=v_b_conv, b_w_out=v_b_w_out, ln_mix_g=v_ln_mix_g,
             ln_mix_b=v_ln_mix_b, ffn_w_up=v_ffn_w_up, ffn_conv=v_ffn_conv, ffn_w_down=v_ffn_w_down,
             ln_ffn_g=v_ln_ffn_g, ln_ffn_b=v_ln_ffn_b)
    seq = x.shape[1]
    *stacks, small = all_gather_shards(_weight_layers(w, EARLY_WEIGHTS) + [_pack([w[n] for n in SMALL_SHARDED])],
                                       "gather_early")
    full = _per_weight(stacks, w, EARLY_WEIGHTS)
    for n, t in zip(SMALL_SHARDED, _unpack(small, [w[n].shape for n in SMALL_SHARDED], lead=(4,))):
        full[n] = _join_shards(t, SHARD_AXIS[n])

    def layout_late(late_stacks):
        return _layout_late(_per_weight(late_stacks, w, LATE_WEIGHTS))

    def reduce_early(grads):
        bufs = [g for n in LATE_WEIGHTS for g in grads[n]]
        return pair_sums(bufs, [BF16] * len(bufs), "late")

    loss_part, dh0, grads, landed_late = _local_step(x[0], loss_target[0], _layout_early(full, w),
                                                     _weight_layers(w, LATE_WEIGHTS), layout_late, reduce_early)
    pieces = [_split_shards(grads[n], SHARD_AXIS[n]) for n in SMALL_SHARDED]
    same = jnp.concatenate([grads[n].reshape(-1) for n in REPLICATED] + [jnp.sum(loss_part).reshape(1)])
    pieces.append(jnp.broadcast_to(same, (4,) + same.shape))
    bufs = [g for n in EARLY_WEIGHTS for g in grads[n]] + [_pack(pieces, lead=(4,))]
    landed = scatter_to_chips(pair_sums(bufs, [BF16] * (len(bufs) - 1) + [F32], "early"), "rs_chips_early")
    totals = [chip_sum(t, "rs_chip_sum%d" % i) for i, t in enumerate(landed + landed_late)]
    by_weight = _per_weight(totals[:len(bufs) - 1] + totals[len(bufs):], w, MATMUL_WEIGHTS)
    *shared, small_total = share_halves([by_weight[n] for n in MATMUL_WEIGHTS] + [[totals[len(bufs) - 1]]], "rs_share")
    grad_w = {n: t.reshape(w[n].shape) for n, t in zip(MATMUL_WEIGHTS, shared)}
    rest = SMALL_SHARDED + REPLICATED
    unpacked = _unpack(small_total[0], [w[n].shape for n in rest] + [()])
    grad_w.update(zip(rest, unpacked[:-1]))
    loss = unpacked[-1]
    grad_x = dh0[:, N_META:N_META + seq]
    steps = [adamw(w[n], grad_w[n], m[n], v[n], "adamw_" + n) for n in WEIGHTS]
    return (loss, grad_x, *[grad_w[n] for n in WEIGHTS], *[s[0] for s in steps], *[s[1] for s in steps],
            *[s[2] for s in steps])
```

```python
import functools

import jax
import jax.numpy as jnp
from jax import lax
from jax.experimental import pallas as pl
from jax.experimental.pallas import tpu as pltpu

F32 = jnp.float32
BF16 = jnp.bfloat16
HI = lax.Precision.HIGHEST

N_META = 16
HEADS = 8
HEAD_DIM = 128
CHUNK = 64
GDN_CONV = 4
SC_CONV = 3
FFN_CONV = 3
ALPHA = 4.0 ** 0.25
LN_EPS = 1e-5
RMS_EPS = 1e-6
L2_EPS = 1e-6
Q_SCALE = HEAD_DIM ** -0.5

ADAM_LR = 0.001
ADAM_B1 = 0.9
ADAM_B2 = 0.999
ADAM_EPS = 1e-08
ADAM_WD = 0.01
ADAM_STEP = 10

HALO = 8
VMEM_LIMIT = 48 * 1024 * 1024


def _params(sem=None):
    return pltpu.CompilerParams(dimension_semantics=sem, vmem_limit_bytes=VMEM_LIMIT)


def _dot(a, b, prec=None):
    return jnp.dot(a, b, preferred_element_type=F32, precision=prec)


def _dot_nt(a, b, prec=None):
    return lax.dot_general(a, b, (((1,), (1,)), ((), ())), preferred_element_type=F32, precision=prec)


def _dot_tn(a, b, prec=None):
    return lax.dot_general(a, b, (((0,), (0,)), ((), ())), preferred_element_type=F32, precision=prec)


def _sigmoid(x):
    return 1.0 / (1.0 + jnp.exp(-x))


def _tri_masks():
    r = lax.broadcasted_iota(jnp.int32, (CHUNK, CHUNK), 0)
    c = lax.broadcasted_iota(jnp.int32, (CHUNK, CHUNK), 1)
    return r >= c, r > c, r == c


def _split_hi_lo(x):
    hi = x.astype(BF16)
    return hi, (x - hi.astype(F32)).astype(BF16)


def _mask_dot(mask, x):
    hi, lo = _split_hi_lo(x)
    return _dot(mask, hi) + _dot(mask, lo)


@jax.custom_vjp
def _cumsum_rows(g):
    causal, _, _ = _tri_masks()
    return _mask_dot(causal.astype(BF16), g)


def _cumsum_rows_fwd(g):
    return _cumsum_rows(g), None


def _cumsum_rows_bwd(_, dy):
    _, strict, _ = _tri_masks()
    return (_mask_dot((~strict).astype(BF16), dy),)


_cumsum_rows.defvjp(_cumsum_rows_fwd, _cumsum_rows_bwd)


def _dot_split3(a, b):
    a_hi, a_lo = _split_hi_lo(a)
    b_hi, b_lo = _split_hi_lo(b)
    return _dot(a_hi, b_hi) + (_dot(a_hi, b_lo) + _dot(a_lo, b_hi))


@jax.custom_vjp
def _dot_precise(a, b):
    return _dot_split3(a, b)


def _dot_precise_fwd(a, b):
    return _dot_split3(a, b), (a, b)


def _dot_precise_bwd(operands, ct):
    a, b = operands
    return _dot_split3(ct, b.T), _dot_split3(a.T, ct)


_dot_precise.defvjp(_dot_precise_fwd, _dot_precise_bwd)


def _gdn_m(ks, g64s, bbs):
    causal, strict, _ = _tri_masks()
    a = [_cumsum_rows(g) for g in g64s]
    decay = [jnp.exp(jnp.where(causal, x - x.T, -1e30)) for x in a]
    kk = [_dot_nt(k * b, k) for k, b in zip(ks, bbs)]
    return [jnp.where(strict, x * d, 0.0) for x, d in zip(kk, decay)]


def _gdn_inverse_stages(ks, g64s, bbs):
    ms = _gdn_m(ks, g64s, bbs)
    yield
    r = lax.broadcasted_iota(jnp.int32, (CHUNK, CHUNK), 0)
    c = lax.broadcasted_iota(jnp.int32, (CHUNK, CHUNK), 1)
    eye = (r == c).astype(F32)
    same = [jnp.right_shift(r, s) == jnp.right_shift(c, s) for s in (3, 4, 5)]
    d = [jnp.where(same[0], m, 0.0) for m in ms]
    p = [_dot(x, x) for x in d]
    yield
    t = [eye - x for x in d]
    t = [x + _dot(x, y) for x, y in zip(t, p)]
    p = [_dot(x, x) for x in p]
    yield
    t = [x + _dot(x, y) for x, y in zip(t, p)]
    yield
    for inner, outer in ((same[0], same[1]), (same[1], same[2]), (same[2], None)):
        joins = ~inner if outer is None else (outer & ~inner)
        o = [_dot(x, jnp.where(joins, m, 0.0)) for x, m in zip(t, ms)]
        yield
        t = [x - _dot(y, x) for x, y in zip(t, o)]
        yield
    res = [eye - x - _dot_split3(m, x) for m, x in zip(ms, t)]
    yield
    return [x + _dot(x, y) for x, y in zip(t, res)]


def _gdn_apply_stages(qs, ks, vs, gbs, g64s, bbs, ss, ts):
    causal, _, _ = _tri_masks()
    n = range(len(qs))
    gc = [_cumsum_rows(g) for g in gbs]
    a = [_cumsum_rows(g) for g in g64s]
    qk = [_dot_nt(qs[h], ks[h]) for h in n]
    yield
    decay = [jnp.exp(jnp.where(causal, x - x.T, -1e30)) for x in a]
    eg = [jnp.exp(x) for x in gc]
    u = [_dot_precise(ts[h], vs[h] * bbs[h]) for h in n]
    w = [_dot_precise(ts[h], ks[h] * bbs[h] * eg[h]) for h in n]
    qk = [qk[h] * decay[h] for h in n]
    gl = [jnp.sum(g, axis=0, keepdims=True) for g in gbs]
    kd = [ks[h] * jnp.exp(gl[h] - gc[h]) for h in n]
    yield
    v_new = [u[h] - _dot(w[h], ss[h]) for h in n]
    q_s = [_dot(qs[h] * eg[h], ss[h]) for h in n]
    yield
    o = [q_s[h] + _dot(qk[h], v_new[h]) for h in n]
    s2 = [ss[h] * jnp.exp(gl[h]) + _dot_tn(kd[h], v_new[h]) for h in n]
    return o, s2


def _run_stages(*generators):
    results = [None] * len(generators)
    live = dict(enumerate(generators))
    while live:
        for i, gen in list(live.items()):
            try:
                next(gen)
            except StopIteration as stop:
                results[i] = stop.value
                del live[i]
    return results


def _head_slices(h):
    return slice(h * HEAD_DIM, (h + 1) * HEAD_DIM), slice(h * HEAD_DIM, h * HEAD_DIM + CHUNK)


def _gdn_head_values(x_ref, gate_ref):
    qs, ks, vs = ([x_ref[s, :, _head_slices(h)[0]] for h in range(HEADS)] for s in range(3))
    gcols = [gate_ref[:, HEADS + h:HEADS + h + 1] for h in range(HEADS)]
    bcols = [gate_ref[:, h:h + 1] for h in range(HEADS)]
    return qs, ks, vs, gcols, bcols


def _expand_cols(gcols, bcols):
    return ([jnp.broadcast_to(g, (CHUNK, HEAD_DIM)) for g in gcols], [jnp.broadcast_to(g, (CHUNK, CHUNK)) for g in gcols],
            [jnp.broadcast_to(b, (CHUNK, HEAD_DIM)) for b in bcols])


def _gdn_inverse_cols(ks, gcols, bcols):
    _, g64s, bbs = _expand_cols(gcols, bcols)
    return _gdn_inverse_stages(ks, g64s, bbs)


def _gdn_apply_cols_stages(qs, ks, vs, gcols, bcols, ss, ts):
    gbs, g64s, bbs = _expand_cols(gcols, bcols)
    return _gdn_apply_stages(qs, ks, vs, gbs, g64s, bbs, ss, ts)


def _gdn_apply_cols(qs, ks, vs, gcols, bcols, ss, ts):
    return _run_stages(_gdn_apply_cols_stages(qs, ks, vs, gcols, bcols, ss, ts))[0]


def _gdn_m_cols(ks, gcols, bcols):
    _, g64s, bbs = _expand_cols(gcols, bcols)
    return _gdn_m(ks, g64s, bbs)


def _gate_lanes(bcols, gcols):
    lane = lax.broadcasted_iota(jnp.int32, (CHUNK, HEAD_DIM), 1)
    out = jnp.zeros((CHUNK, HEAD_DIM), F32)
    for h in range(HEADS):
        out = jnp.where(lane == h, jnp.broadcast_to(bcols[h], out.shape), out)
        out = jnp.where(lane == HEADS + h, jnp.broadcast_to(gcols[h], out.shape), out)
    return out


def gdn_chunk_fwd(qkv, gates, gather=()):
    _, lp, width = qkv.shape
    n_chunks = lp // CHUNK
    n = len(gather)

    def body(x_ref, gate_ref, next_ref, next_gate_ref, *refs):
        shard_refs, (o_ref, s_ref, t_ref), refs = refs[:n], refs[n:n + 3], refs[n + 3:]
        stack_refs, state, t_next, sems = refs[:n], refs[n], refs[n + 1], refs[n + 2:]
        copies = _gather_copies(shard_refs, stack_refs, *sems) if n else None

        def inverse_stages(ref, g_ref):
            _, ks, _, gcols, bcols = _gdn_head_values(ref, g_ref)
            return _gdn_inverse_cols(ks, gcols, bcols)

        @pl.when(pl.program_id(0) == 0)
        def _():
            state[...] = jnp.zeros_like(state)
            for h, t in enumerate(_run_stages(inverse_stages(x_ref, gate_ref))[0]):
                t_next[h] = t
            if n:
                _gather_start(copies)

        qs, ks, vs, gcols, bcols = _gdn_head_values(x_ref, gate_ref)
        ss = [state[h] for h in range(HEADS)]
        ts = [t_next[h] for h in range(HEADS)]
        ts_next, (os_, s2) = _run_stages(inverse_stages(next_ref, next_gate_ref),
                                         _gdn_apply_cols_stages(qs, ks, vs, gcols, bcols, ss, ts))
        for h in range(HEADS):
            s_ref[0, h] = ss[h]
            t_ref[0, h] = ts[h]
            t_next[h] = ts_next[h]
            o_ref[:, _head_slices(h)[0]] = os_[h]
            state[h] = s2[h]

        if n:
            @pl.when(pl.program_id(0) == n_chunks - 1)
            def _():
                _gather_finish(copies)

    o, states, tinv, *stacks = pl.pallas_call(
        body,
        name="gdn_chunk_fwd",
        grid=(n_chunks,),
        in_specs=[pl.BlockSpec((3, CHUNK, width), lambda c: (0, c, 0)),
                  pl.BlockSpec((CHUNK, HEAD_DIM), lambda c: (c, 0)),
                  pl.BlockSpec((3, CHUNK, width), lambda c: (0, jnp.minimum(c + 1, n_chunks - 1), 0)),
                  pl.BlockSpec((CHUNK, HEAD_DIM), lambda c: (jnp.minimum(c + 1, n_chunks - 1), 0))] + [ANY] * n,
        out_specs=[
            pl.BlockSpec((CHUNK, width), lambda c: (c, 0)),
            pl.BlockSpec((1, HEADS, HEAD_DIM, HEAD_DIM), lambda c: (c, 0, 0, 0)),
            pl.BlockSpec((1, HEADS, CHUNK, CHUNK), lambda c: (c, 0, 0, 0)),
        ] + [ANY] * n,
        out_shape=[
            jax.ShapeDtypeStruct((lp, width), F32),
            jax.ShapeDtypeStruct((n_chunks, HEADS, HEAD_DIM, HEAD_DIM), F32),
            jax.ShapeDtypeStruct((n_chunks, HEADS, CHUNK, CHUNK), F32),
        ] + _gather_out_shapes(gather),
        scratch_shapes=[pltpu.VMEM((HEADS, HEAD_DIM, HEAD_DIM), F32), pltpu.VMEM((HEADS, CHUNK, CHUNK), F32)]
        + (_gather_sems(n) if n else []),
        compiler_params=_params(("arbitrary",)),
    )(qkv, gates, qkv, gates, *gather)
    return o, states, tinv, _set_own_slots(stacks, gather)


def gdn_chunk_bwd(qkv, gates, states, tinv, d_o, scatter=()):
    _, lp, width = qkv.shape
    n_chunks = lp // CHUNK
    last = n_chunks - 1
    n = len(scatter)

    def body(x_ref, gate_ref, s_ref, t_ref, do_ref, *refs):
        leaving_refs, dx_ref, dgate_ref, refs = refs[:n], refs[n], refs[n + 1], refs[n + 2:]
        landing_refs, dstate, sems = refs[:n], refs[n], refs[n + 1:]
        copies = _scatter_copies(leaving_refs, landing_refs, *sems) if n else None

        @pl.when(pl.program_id(0) == 0)
        def _():
            dstate[...] = jnp.zeros_like(dstate)
            if n:
                _scatter_start(copies)

        heads = range(HEADS)
        qs, ks, vs, gcols, bcols = _gdn_head_values(x_ref, gate_ref)
        ss = [s_ref[0, h] for h in heads]
        ts = [t_ref[0, h] for h in heads]
        d_out = ([do_ref[:, _head_slices(h)[0]] for h in heads], [dstate[h] for h in heads])
        _, vjp_apply = jax.vjp(_gdn_apply_cols, qs, ks, vs, gcols, bcols, ss, ts)
        dq, dk, dv, dg, db, ds, dt = vjp_apply(d_out)
        tts = [t.T for t in ts]
        dm = [_dot(tts[h], dt[h]) for h in heads]
        dm = [-_dot(dm[h], tts[h]) for h in heads]
        _, vjp_m = jax.vjp(_gdn_m_cols, ks, gcols, bcols)
        dk2, dg2, db2 = vjp_m(dm)
        for h in heads:
            sl = _head_slices(h)[0]
            dx_ref[0, :, sl] = dq[h]
            dx_ref[1, :, sl] = dk[h] + dk2[h]
            dx_ref[2, :, sl] = dv[h]
            dstate[h] = ds[h]
        dgate_ref[...] = _gate_lanes([db[h] + db2[h] for h in heads], [dg[h] + dg2[h] for h in heads])

        if n:
            @pl.when(pl.program_id(0) == n_chunks - 1)
            def _():
                _scatter_finish(copies)

    dqkv, dgates, *landed = pl.pallas_call(
        body,
        name="gdn_chunk_bwd",
        grid=(n_chunks,),
        in_specs=[
            pl.BlockSpec((3, CHUNK, width), lambda c: (0, last - c, 0)),
            pl.BlockSpec((CHUNK, HEAD_DIM), lambda c: (last - c, 0)),
            pl.BlockSpec((1, HEADS, HEAD_DIM, HEAD_DIM), lambda c: (last - c, 0, 0, 0)),
            pl.BlockSpec((1, HEADS, CHUNK, CHUNK), lambda c: (last - c, 0, 0, 0)),
            pl.BlockSpec((CHUNK, width), lambda c: (last - c, 0)),
        ] + [ANY] * n,
        out_specs=[pl.BlockSpec((3, CHUNK, width), lambda c: (0, last - c, 0)),
                   pl.BlockSpec((CHUNK, HEAD_DIM), lambda c: (last - c, 0))] + [ANY] * n,
        out_shape=[jax.ShapeDtypeStruct(qkv.shape, F32), jax.ShapeDtypeStruct(gates.shape, F32)]
        + [jax.ShapeDtypeStruct(b.shape, b.dtype) for b in scatter],
        scratch_shapes=[pltpu.VMEM((HEADS, HEAD_DIM, HEAD_DIM), F32)] + (_scatter_sems(n) if n else []),
        compiler_params=_params(("arbitrary",)),
    )(qkv, gates, states, tinv, d_o, *scatter)
    return dqkv, dgates, _keep_own_slots(landed, scatter)


def mm_nn(a, b, *, tm, name):
    ks, m, tk = a.shape
    _, ns, _, tn = b.shape

    def body(a_ref, b_ref, o_ref):
        p = _dot(a_ref[...].astype(BF16), b_ref[...])

        @pl.when(pl.program_id(2) == 0)
        def _():
            o_ref[...] = p

        @pl.when(pl.program_id(2) > 0)
        def _():
            o_ref[...] += p

    return pl.pallas_call(
        body,
        name=name,
        grid=(ns, m // tm, ks),
        in_specs=[
            pl.BlockSpec((None, tm, tk), lambda n, i, k: (k, i, 0)),
            pl.BlockSpec((None, None, tk, tn), lambda n, i, k: (k, n, 0, 0)),
        ],
        out_specs=pl.BlockSpec((None, tm, tn), lambda n, i, k: (n, i, 0)),
        out_shape=jax.ShapeDtypeStruct((ns, m, tn), F32),
        compiler_params=_params(("arbitrary", "arbitrary", "arbitrary")),
    )(a, b)


def mm_nt(dy, w, *, tm, name, res=None, res_scale=1.0):
    ns, m, tn = dy.shape
    ks, _, tk, _ = w.shape

    def body(*refs):
        if res is None:
            dy_ref, w_ref, o_ref = refs
        else:
            dy_ref, w_ref, r_ref, o_ref = refs
        p = _dot_nt(dy_ref[...].astype(BF16), w_ref[...])

        @pl.when(pl.program_id(2) == 0)
        def _():
            o_ref[...] = p if res is None else p + res_scale * r_ref[...]

        @pl.when(pl.program_id(2) > 0)
        def _():
            o_ref[...] += p

    in_specs = [
        pl.BlockSpec((None, tm, tn), lambda k, i, n: (n, i, 0)),
        pl.BlockSpec((None, None, tk, tn), lambda k, i, n: (k, n, 0, 0)),
    ]
    args = [dy, w]
    if res is not None:
        in_specs.append(pl.BlockSpec((None, tm, tk), lambda k, i, n: (k, i, 0)))
        args.append(res)
    return pl.pallas_call(
        body,
        name=name,
        grid=(ks, m // tm, ns),
        in_specs=in_specs,
        out_specs=pl.BlockSpec((None, tm, tk), lambda k, i, n: (k, i, 0)),
        out_shape=jax.ShapeDtypeStruct((ks, m, tk), F32),
        compiler_params=_params(("arbitrary", "arbitrary", "arbitrary")),
    )(*args)


def mm_tn(x, dy, *, tm, name, rb=None):
    ks, m, tk = x.shape
    ns, _, tn = dy.shape
    rb = tk if rb is None else rb

    def body(x_ref, dy_ref, o_ref):
        @pl.when(pl.program_id(2) == 0)
        def _():
            o_ref[...] = jnp.zeros_like(o_ref)

        dyb = dy_ref[...].astype(BF16)
        for r in range(0, tk, rb):
            o_ref[r:r + rb, :] += _dot_tn(x_ref[:, r:r + rb].astype(BF16), dyb)

    return pl.pallas_call(
        body,
        name=name,
        grid=(ks, ns, m // tm),
        in_specs=[
            pl.BlockSpec((None, tm, tk), lambda k, n, i: (k, i, 0)),
            pl.BlockSpec((None, tm, tn), lambda k, n, i: (n, i, 0)),
        ],
        out_specs=pl.BlockSpec((None, None, tk, tn), lambda k, n, i: (k, n, 0, 0)),
        out_shape=jax.ShapeDtypeStruct((ks, ns, tk, tn), F32),
        compiler_params=_params(("arbitrary", "arbitrary", "arbitrary")),
    )(x, dy)


def _row_partial(x):
    rows, c = x.shape
    return jnp.sum(x.reshape(rows // 8, 8, c), axis=0)


def _layer_norm(r, g, b):
    mu = jnp.mean(r, axis=-1, keepdims=True)
    xc = r - mu
    var = jnp.mean(xc * xc, axis=-1, keepdims=True)
    return xc * lax.rsqrt(var + LN_EPS) * g + b


def _layer_norm_bwd(x, dh, g):
    mu = jnp.mean(x, axis=-1, keepdims=True)
    xc = x - mu
    rstd = lax.rsqrt(jnp.mean(xc * xc, axis=-1, keepdims=True) + LN_EPS)
    xh = xc * rstd
    dxh = dh * g
    m1 = jnp.mean(dxh, axis=-1, keepdims=True)
    m2 = jnp.mean(dxh * xh, axis=-1, keepdims=True)
    return rstd * (dxh - m1 - xh * m2), _row_partial(dh * xh), _row_partial(dh)


def mm_nn_ln(a, b, h_prev, g, beta, *, tm, name):
    ks, m, tk = a.shape
    d = b.shape[3]

    def body(a_ref, b_ref, hp_ref, g_ref, be_ref, r_ref, h_ref):
        p = _dot(a_ref[...].astype(BF16), b_ref[...])

        @pl.when(pl.program_id(1) == 0)
        def _():
            r_ref[...] = p

        @pl.when(pl.program_id(1) > 0)
        def _():
            r_ref[...] += p

        @pl.when(pl.program_id(1) == ks - 1)
        def _():
            r = ALPHA * hp_ref[...] + r_ref[...]
            r_ref[...] = r
            h_ref[...] = _layer_norm(r, g_ref[...], be_ref[...])

    row = pl.BlockSpec((None, tm, d), lambda i, k: (0, i, 0))
    vec = pl.BlockSpec((1, d), lambda i, k: (0, 0))
    return pl.pallas_call(
        body,
        name=name,
        grid=(m // tm, ks),
        in_specs=[
            pl.BlockSpec((None, tm, tk), lambda i, k: (k, i, 0)),
            pl.BlockSpec((None, None, tk, d), lambda i, k: (k, 0, 0, 0)),
            row, vec, vec,
        ],
        out_specs=[row, row],
        out_shape=[jax.ShapeDtypeStruct((1, m, d), F32)] * 2,
        compiler_params=_params(("arbitrary", "arbitrary")),
    )(a, b, h_prev, g, beta)


def mm_nt_ln_bwd(dy, w, res, r, g, *, tm, name):
    ns, m, tn = dy.shape
    d = w.shape[2]

    def body(dy_ref, w_ref, res_ref, r_ref, g_ref, dr_ref, dgb_ref):
        p = _dot_nt(dy_ref[...].astype(BF16), w_ref[...])

        @pl.when((pl.program_id(0) == 0) & (pl.program_id(1) == 0))
        def _():
            dgb_ref[...] = jnp.zeros_like(dgb_ref)

        @pl.when(pl.program_id(1) == 0)
        def _():
            dr_ref[...] = p + ALPHA * res_ref[...]

        @pl.when(pl.program_id(1) > 0)
        def _():
            dr_ref[...] += p

        @pl.when(pl.program_id(1) == ns - 1)
        def _():
            dr, dgamma, dbeta = _layer_norm_bwd(r_ref[...], dr_ref[...], g_ref[...])
            dr_ref[...] = dr
            dgb_ref[0] += dgamma
            dgb_ref[1] += dbeta

    row = pl.BlockSpec((None, tm, d), lambda i, n: (0, i, 0))
    return pl.pallas_call(
        body,
        name=name,
        grid=(m // tm, ns),
        in_specs=[
            pl.BlockSpec((None, tm, tn), lambda i, n: (n, i, 0)),
            pl.BlockSpec((None, None, d, tn), lambda i, n: (0, n, 0, 0)),
            row, row,
            pl.BlockSpec((1, d), lambda i, n: (0, 0)),
        ],
        out_specs=[row, pl.BlockSpec((2, 8, d), lambda i, n: (0, 0, 0))],
        out_shape=[jax.ShapeDtypeStruct((1, m, d), F32), jax.ShapeDtypeStruct((2, 8, d), F32)],
        compiler_params=_params(("arbitrary", "arbitrary")),
    )(dy, w, res, r, g)


def ln_bwd(r, dh, g, *, tm, name):
    _, lp, d = r.shape

    def body(r_ref, dh_ref, g_ref, dr_ref, dgb_ref):
        dr, dgamma, dbeta = _layer_norm_bwd(r_ref[...], dh_ref[...], g_ref[...])
        dr_ref[...] = dr

        @pl.when(pl.program_id(0) == 0)
        def _():
            dgb_ref[...] = jnp.zeros_like(dgb_ref)

        dgb_ref[0] += dgamma
        dgb_ref[1] += dbeta

    row = pl.BlockSpec((None, tm, d), lambda i: (0, i, 0))
    return pl.pallas_call(
        body,
        name=name,
        grid=(lp // tm,),
        in_specs=[row, row, pl.BlockSpec((1, d), lambda i: (0, 0))],
        out_specs=[row, pl.BlockSpec((2, 8, d), lambda i: (0, 0, 0))],
        out_shape=[jax.ShapeDtypeStruct((1, lp, d), F32), jax.ShapeDtypeStruct((2, 8, d), F32)],
        compiler_params=_params(("arbitrary",)),
    )(r, dh, g)


def loss_grad(h, target, *, first, count, tm):
    _, lp, d = h.shape

    def body(h_ref, t_ref, dh_ref, l_ref):
        row = pl.program_id(0) * tm + lax.broadcasted_iota(jnp.int32, (tm, d), 0)
        valid = (row >= first) & (row < first + count)
        err = jnp.where(valid, h_ref[...] - t_ref[...], 0.0)
        dh_ref[...] = err * (1.0 / d)

        @pl.when(pl.program_id(0) == 0)
        def _():
            l_ref[...] = jnp.zeros_like(l_ref)

        l_ref[...] += _row_partial(err * err) * (0.5 / d)

    return pl.pallas_call(
        body,
        name="loss_grad",
        grid=(lp // tm,),
        in_specs=[pl.BlockSpec((None, tm, d), lambda i: (0, i, 0)), pl.BlockSpec((tm, d), lambda i: (i, 0))],
        out_specs=[pl.BlockSpec((None, tm, d), lambda i: (0, i, 0)), pl.BlockSpec((8, d), lambda i: (0, 0))],
        out_shape=[jax.ShapeDtypeStruct((1, lp, d), F32), jax.ShapeDtypeStruct((8, d), F32)],
        compiler_params=_params(("arbitrary",)),
    )(h, target)


def _halo_index(tile, tm):
    return jnp.maximum(tile * (tm // HALO) - 1, 0)


def _conv_fwd(xs_ref, w, taps, tm):
    acc = w(0) * xs_ref[pl.ds(HALO - taps + 1, tm), :]
    for j in range(1, taps):
        acc += w(j) * xs_ref[pl.ds(HALO - taps + 1 + j, tm), :]
    return acc


def _conv_bwd_x(dcs_ref, w, taps, tm):
    acc = w(0) * dcs_ref[pl.ds(taps - 1, tm), :]
    for j in range(1, taps):
        acc += w(j) * dcs_ref[pl.ds(taps - 1 - j, tm), :]
    return acc


SUB = 8
LANES = 128
PAIR = 2 * SUB
STRIP_UNROLL = 2


def _pair_rows(r0):
    return pl.ds(r0, SUB), pl.ds(r0 + SUB if isinstance(r0, int) else pl.multiple_of(r0 + SUB, SUB), SUB)


def _shift_down(cur, prev, s):
    if s == 0:
        return cur
    row = lax.broadcasted_iota(jnp.int32, cur.shape, 0)
    return jnp.where(row < s, pltpu.roll(prev, s, axis=0), pltpu.roll(cur, s, axis=0))


def _shift_up(cur, nxt, s):
    if s == 0:
        return cur
    row = lax.broadcasted_iota(jnp.int32, cur.shape, 0)
    return jnp.where(row < SUB - s, pltpu.roll(cur, SUB - s, axis=0), pltpu.roll(nxt, SUB - s, axis=0))


def _silu_parts(c):
    sg = _sigmoid(c)
    return c * sg, sg * (1.0 + c * (1.0 - sg))


def _head_sum(x):
    rows, c = x.shape
    parts = []
    for h in range(c // HEAD_DIM):
        s = jnp.sum(x[:, h * HEAD_DIM:(h + 1) * HEAD_DIM], axis=-1, keepdims=True)
        parts.append(jnp.broadcast_to(s, (rows, HEAD_DIM)))
    return parts[0] if len(parts) == 1 else jnp.concatenate(parts, axis=-1)


def _log1p(y):
    u = 1.0 + y
    d = u - 1.0
    return jnp.where(d == 0.0, y, jnp.log(u) * (y / jnp.where(d == 0.0, 1.0, d)))


def _softplus(x):
    return jnp.maximum(x, 0.0) + _log1p(jnp.exp(-jnp.abs(x)))


def _gate_values(x, al, dt):
    lane = lax.broadcasted_iota(jnp.int32, x.shape, 1)
    is_beta, is_g = lane < HEADS, (lane >= HEADS) & (lane < 2 * HEADS)
    return _sigmoid(x), -jnp.exp(al) * _softplus(x + dt), is_beta, is_g


def gdn_gates_fwd(pba, al, dt, *, tm):
    _, lp, width = pba.shape

    def body(x_ref, al_ref, dt_ref, o_ref):
        beta, g, is_beta, is_g = _gate_values(x_ref[...], al_ref[...], dt_ref[...])
        o_ref[...] = jnp.where(is_beta, beta, jnp.where(is_g, g, 0.0))

    vec = pl.BlockSpec((1, width), lambda i: (0, 0))
    return pl.pallas_call(
        body,
        name="gdn_gates_fwd",
        grid=(lp // tm,),
        in_specs=[pl.BlockSpec((None, tm, width), lambda i: (0, i, 0)), vec, vec],
        out_specs=pl.BlockSpec((tm, width), lambda i: (i, 0)),
        out_shape=jax.ShapeDtypeStruct((lp, width), F32),
        compiler_params=_params(("arbitrary",)),
    )(pba, al, dt)


def gdn_gates_bwd(pba, dgates, al, dt, *, tm):
    _, lp, width = pba.shape

    def body(x_ref, d_ref, al_ref, dt_ref, dx_ref, dsc_ref):
        x = x_ref[...]
        beta, g, is_beta, is_g = _gate_values(x, al_ref[...], dt_ref[...])
        d = d_ref[...]
        dg = jnp.where(is_g, d, 0.0)
        da = dg * -jnp.exp(al_ref[...]) * _sigmoid(x + dt_ref[...])
        dx_ref[...] = jnp.where(is_beta, d * beta * (1.0 - beta), da).astype(dx_ref.dtype)

        @pl.when(pl.program_id(0) == 0)
        def _():
            dsc_ref[...] = jnp.zeros_like(dsc_ref)

        dsc_ref[0] += _row_partial(dg * g)
        dsc_ref[1] += _row_partial(da)

    vec = pl.BlockSpec((1, width), lambda i: (0, 0))
    return pl.pallas_call(
        body,
        name="gdn_gates_bwd",
        grid=(lp // tm,),
        in_specs=[pl.BlockSpec((None, tm, width), lambda i: (0, i, 0)), pl.BlockSpec((tm, width), lambda i: (i, 0)), vec, vec],
        out_specs=[pl.BlockSpec((None, tm, width), lambda i: (0, i, 0)), pl.BlockSpec((2, SUB, width), lambda i: (0, 0, 0))],
        out_shape=[jax.ShapeDtypeStruct((1, lp, width), BF16), jax.ShapeDtypeStruct((2, SUB, width), F32)],
        compiler_params=_params(("arbitrary",)),
    )(pba, dgates, al, dt)


def gdn_pre_fwd(p3, conv_w, *, tm, cb):
    _, lp, width = p3.shape
    taps = conv_w.shape[1]

    def body(x_ref, halo_ref, w_ref, o_ref, xs):
        i = pl.program_id(1)
        for s in range(3):
            xs[s, 0:HALO, :] = jnp.where(i > 0, halo_ref[s], 0.0)
            xs[s, HALO:, :] = x_ref[s]
            c = _conv_fwd(xs.at[s], lambda j, s=s: w_ref[s, j:j + 1, :], taps, tm)
            y, _ = _silu_parts(c)
            if s < 2:
                y = y * lax.rsqrt(_head_sum(y * y) + L2_EPS)
                if s == 0:
                    y = y * Q_SCALE
            o_ref[s] = y

    return pl.pallas_call(
        body,
        name="gdn_pre_fwd",
        grid=(width // cb, lp // tm),
        in_specs=[
            pl.BlockSpec((3, tm, cb), lambda j, i: (0, i, j)),
            pl.BlockSpec((3, HALO, cb), lambda j, i: (0, _halo_index(i, tm), j)),
            pl.BlockSpec((3, taps, cb), lambda j, i: (0, 0, j)),
        ],
        out_specs=pl.BlockSpec((3, tm, cb), lambda j, i: (0, i, j)),
        out_shape=jax.ShapeDtypeStruct((3, lp, width), F32),
        scratch_shapes=[pltpu.VMEM((3, tm + HALO, cb), F32)],
        compiler_params=_params(("arbitrary", "arbitrary")),
    )(p3, p3, conv_w)


def gdn_pre_bwd(p3, dqkv, conv_w, *, tm, cb):
    _, lp, width = p3.shape
    taps = conv_w.shape[1]
    last = lp // tm - 1

    def body(x_ref, halo_ref, d_ref, w_ref, dx_ref, dw_ref, xs, dcs, carry):
        step = pl.program_id(1)
        tile = last - step

        @pl.when(step == 0)
        def _():
            carry[...] = jnp.zeros_like(carry)
            dw_ref[...] = jnp.zeros_like(dw_ref)

        for s in range(3):
            w = lambda j, s=s: w_ref[s, j:j + 1, :]
            xs[s, 0:HALO, :] = jnp.where(tile > 0, halo_ref[s], 0.0)
            xs[s, HALO:, :] = x_ref[s]
            c = _conv_fwd(xs.at[s], w, taps, tm)
            y, dsilu = _silu_parts(c)
            dy = d_ref[s]
            if s < 2:
                rn = lax.rsqrt(_head_sum(y * y) + L2_EPS)
                yn = y * rn
                if s == 0:
                    dy = dy * Q_SCALE
                dy = rn * (dy - yn * _head_sum(dy * yn))
            dc = dy * dsilu
            dcs[s, 0:tm, :] = dc
            dcs[s, tm:, :] = carry[s]
            dx_ref[s] = _conv_bwd_x(dcs.at[s], w, taps, tm).astype(dx_ref.dtype)
            carry[s] = dc[0:HALO, :]
            for j in range(taps):
                dw_ref[s, j] += _row_partial(dc * xs[s, pl.ds(HALO - taps + 1 + j, tm), :])

    tile_spec = pl.BlockSpec((3, tm, cb), lambda j, i: (0, last - i, j))
    return pl.pallas_call(
        body,
        name="gdn_pre_bwd",
        grid=(width // cb, lp // tm),
        in_specs=[
            tile_spec,
            pl.BlockSpec((3, HALO, cb), lambda j, i: (0, _halo_index(last - i, tm), j)),
            tile_spec,
            pl.BlockSpec((3, taps, cb), lambda j, i: (0, 0, j)),
        ],
        out_specs=[tile_spec, pl.BlockSpec((3, taps, SUB, cb), lambda j, i: (0, 0, 0, j))],
        out_shape=[jax.ShapeDtypeStruct((3, lp, width), BF16), jax.ShapeDtypeStruct((3, taps, SUB, width), F32)],
        scratch_shapes=[
            pltpu.VMEM((3, tm + HALO, cb), F32),
            pltpu.VMEM((3, tm + HALO, cb), F32),
            pltpu.VMEM((3, HALO, cb), F32),
        ],
        compiler_params=_params(("arbitrary", "arbitrary")),
    )(p3, p3, dqkv, conv_w)


def gdn_post_fwd(o, z, nw_b, *, tm):
    _, lp, width = o.shape

    def body(o_ref, z_ref, nw_ref, y_ref):
        ov = o_ref[...]
        rn = lax.rsqrt(_head_sum(ov * ov) * (1.0 / HEAD_DIM) + RMS_EPS)
        gate, _ = _silu_parts(z_ref[...])
        y_ref[...] = (ov * rn * nw_ref[...] * gate).astype(y_ref.dtype)

    row = pl.BlockSpec((None, tm, width), lambda i: (0, i, 0))
    return pl.pallas_call(
        body,
        name="gdn_post_fwd",
        grid=(lp // tm,),
        in_specs=[row, row, pl.BlockSpec((1, width), lambda i: (0, 0))],
        out_specs=row,
        out_shape=jax.ShapeDtypeStruct((1, lp, width), BF16),
        compiler_params=_params(("arbitrary",)),
    )(o, z, nw_b)


def gdn_post_bwd(o, z, dy, nw_b, *, tm):
    _, lp, width = o.shape

    def body(o_ref, z_ref, dy_ref, nw_ref, do_ref, dz_ref, dnw_ref):
        ov = o_ref[...]
        rn = lax.rsqrt(_head_sum(ov * ov) * (1.0 / HEAD_DIM) + RMS_EPS)
        yn = ov * rn
        gate, dgate = _silu_parts(z_ref[...])
        d_on = dy_ref[...] * gate
        dz_ref[...] = (dy_ref[...] * yn * nw_ref[...] * dgate).astype(dz_ref.dtype)
        a = d_on * nw_ref[...]
        do_ref[...] = rn * (a - yn * (_head_sum(a * yn) * (1.0 / HEAD_DIM)))

        @pl.when(pl.program_id(0) == 0)
        def _():
            dnw_ref[...] = jnp.zeros_like(dnw_ref)

        dnw_ref[...] += _row_partial(d_on * yn)

    row = pl.BlockSpec((None, tm, width), lambda i: (0, i, 0))
    return pl.pallas_call(
        body,
        name="gdn_post_bwd",
        grid=(lp // tm,),
        in_specs=[row, row, row, pl.BlockSpec((1, width), lambda i: (0, 0))],
        out_specs=[row, row, pl.BlockSpec((8, width), lambda i: (0, 0))],
        out_shape=[jax.ShapeDtypeStruct((1, lp, width), F32), jax.ShapeDtypeStruct((1, lp, width), BF16),
                   jax.ShapeDtypeStruct((8, width), F32)],
        compiler_params=_params(("arbitrary",)),
    )(o, z, dy, nw_b)


def ffn_act_fwd(up, conv_w, *, tm, name):
    _, lp, c_w = up.shape
    taps = conv_w.shape[1]

    def body(u_ref, halo_ref, g_ref, w_ref, o_ref):
        first_tile = pl.program_id(1) == 0

        def strip(cur, prev, rows, cs):
            conv = w_ref[taps - 1:taps, cs] * cur
            for j in range(taps - 1):
                conv += w_ref[j:j + 1, cs] * _shift_down(cur, prev, taps - 1 - j)
            y, _ = _silu_parts(conv)
            return y * g_ref[rows, cs]

        def pair(r0, above_of):
            top, bot = _pair_rows(r0)
            for c0 in range(0, c_w, LANES):
                cs = slice(c0, c0 + LANES)
                cur_t, cur_b = u_ref[top, cs], u_ref[bot, cs]
                out = [strip(cur_t, above_of(cs), top, cs), strip(cur_b, cur_t, bot, cs)]
                o_ref[pl.ds(r0, PAIR), cs] = jnp.concatenate(out, axis=0).astype(o_ref.dtype)

        pair(0, lambda cs: jnp.where(first_tile, 0.0, halo_ref[:, cs]))

        def loop_body(s, carry):
            r0 = pl.multiple_of(s * PAIR, PAIR)
            pair(r0, lambda cs: u_ref[pl.ds(pl.multiple_of(r0 - SUB, SUB), SUB), cs])
            return carry

        lax.fori_loop(1, tm // PAIR, loop_body, 0, unroll=STRIP_UNROLL)

    return pl.pallas_call(
        body,
        name=name,
        grid=(2, lp // tm),
        in_specs=[
            pl.BlockSpec((None, tm, c_w), lambda s, i: (s, i, 0)),
            pl.BlockSpec((None, HALO, c_w), lambda s, i: (s, _halo_index(i, tm), 0)),
            pl.BlockSpec((None, tm, c_w), lambda s, i: (2 + s, i, 0)),
            pl.BlockSpec((None, taps, c_w), lambda s, i: (s, 0, 0)),
        ],
        out_specs=pl.BlockSpec((None, tm, c_w), lambda s, i: (s, i, 0)),
        out_shape=jax.ShapeDtypeStruct((2, lp, c_w), BF16),
        compiler_params=_params(("arbitrary", "arbitrary")),
    )(up, up, up, conv_w)


def ffn_act_bwd(up, dact, conv_w, *, tm, name):
    _, lp, c_w = up.shape
    taps = conv_w.shape[1]
    last = lp // tm - 1
    n_pairs = tm // PAIR

    def body(u_ref, halo_ref, g_ref, d_ref, w_ref, dup_ref, dw_ref, below):
        step = pl.program_id(1)
        first_tile = step == last

        @pl.when(step == 0)
        def _():
            below[...] = jnp.zeros_like(below)
            dw_ref[...] = jnp.zeros_like(dw_ref)

        def strip(cur, prev, rows, cs, nxt):
            shifted = [_shift_down(cur, prev, taps - 1 - j) for j in range(taps)]
            conv = w_ref[0:1, cs] * shifted[0]
            for j in range(1, taps):
                conv += w_ref[j:j + 1, cs] * shifted[j]
            y, dsilu = _silu_parts(conv)
            d = d_ref[rows, cs]
            dc = d * g_ref[rows, cs] * dsilu
            dx = w_ref[taps - 1:taps, cs] * dc
            for j in range(taps - 1):
                dx += w_ref[j:j + 1, cs] * _shift_up(dc, nxt, taps - 1 - j)
            return dx, d * y, dc, [dc * s for s in shifted]

        def pair(r0, above_of):
            top, bot = _pair_rows(r0)
            both = pl.ds(r0, PAIR)
            for c0 in range(0, c_w, LANES):
                cs = slice(c0, c0 + LANES)
                cur_t, cur_b = u_ref[top, cs], u_ref[bot, cs]
                dx_b, dg_b, dc_b, dw_b = strip(cur_b, cur_t, bot, cs, below[:, cs])
                dx_t, dg_t, dc_t, dw_t = strip(cur_t, above_of(cs), top, cs, dc_b)
                below[:, cs] = dc_t
                dup_ref[0, both, cs] = jnp.concatenate([dx_t, dx_b], axis=0).astype(dup_ref.dtype)
                dup_ref[1, both, cs] = jnp.concatenate([dg_t, dg_b], axis=0).astype(dup_ref.dtype)
                for j in range(taps):
                    dw_ref[j, :, cs] += dw_t[j] + dw_b[j]

        def loop_body(it, carry):
            r0 = pl.multiple_of((n_pairs - 1 - it) * PAIR, PAIR)
            pair(r0, lambda cs: u_ref[pl.ds(pl.multiple_of(r0 - SUB, SUB), SUB), cs])
            return carry

        lax.fori_loop(0, n_pairs - 1, loop_body, 0, unroll=STRIP_UNROLL)
        pair(0, lambda cs: jnp.where(first_tile, 0.0, halo_ref[:, cs]))

    return pl.pallas_call(
        body,
        name=name,
        grid=(2, lp // tm),
        in_specs=[
            pl.BlockSpec((None, tm, c_w), lambda s, i: (s, last - i, 0)),
            pl.BlockSpec((None, HALO, c_w), lambda s, i: (s, _halo_index(last - i, tm), 0)),
            pl.BlockSpec((None, tm, c_w), lambda s, i: (2 + s, last - i, 0)),
            pl.BlockSpec((None, tm, c_w), lambda s, i: (s, last - i, 0)),
            pl.BlockSpec((None, taps, c_w), lambda s, i: (s, 0, 0)),
        ],
        out_specs=[
            pl.BlockSpec((2, None, tm, c_w), lambda s, i: (0, s, last - i, 0)),
            pl.BlockSpec((None, taps, SUB, c_w), lambda s, i: (s, 0, 0, 0)),
        ],
        out_shape=[jax.ShapeDtypeStruct((2, 2, lp, c_w), BF16), jax.ShapeDtypeStruct((2, taps, SUB, c_w), F32)],
        scratch_shapes=[pltpu.VMEM((SUB, c_w), F32)],
        compiler_params=_params(("arbitrary", "arbitrary")),
    )(up, up, up, dact, conv_w)


def sc_fwd(pb, conv_w, *, tm, cb):
    _, lp, width = pb.shape
    taps = conv_w.shape[0]

    def body(x_ref, halo_ref, w_ref, o_ref):
        first_tile = pl.program_id(1) == 0

        def strip(cur, prev, rows, cs):
            conv = w_ref[taps - 1:taps, cs] * cur
            for j in range(taps - 1):
                conv += w_ref[j:j + 1, cs] * _shift_down(cur, prev, taps - 1 - j)
            return x_ref[0, rows, cs] * conv

        def pair(r0, above_of):
            top, bot = _pair_rows(r0)
            for c0 in range(0, cb, LANES):
                cs = slice(c0, c0 + LANES)
                cur_t = x_ref[1, top, cs] * x_ref[2, top, cs]
                cur_b = x_ref[1, bot, cs] * x_ref[2, bot, cs]
                out = [strip(cur_t, above_of(cs), top, cs), strip(cur_b, cur_t, bot, cs)]
                o_ref[pl.ds(r0, PAIR), cs] = jnp.concatenate(out, axis=0).astype(o_ref.dtype)

        pair(0, lambda cs: jnp.where(first_tile, 0.0, halo_ref[1, :, cs] * halo_ref[2, :, cs]))

        def loop_body(k, carry):
            r0 = pl.multiple_of(k * PAIR, PAIR)
            before = pl.ds(pl.multiple_of(r0 - SUB, SUB), SUB)
            pair(r0, lambda cs: x_ref[1, before, cs] * x_ref[2, before, cs])
            return carry

        lax.fori_loop(1, tm // PAIR, loop_body, 0, unroll=STRIP_UNROLL)

    return pl.pallas_call(
        body,
        name="sc_fwd",
        grid=(width // cb, lp // tm),
        in_specs=[
            pl.BlockSpec((3, tm, cb), lambda j, i: (0, i, j)),
            pl.BlockSpec((3, HALO, cb), lambda j, i: (0, _halo_index(i, tm), j)),
            pl.BlockSpec((taps, cb), lambda j, i: (0, j)),
        ],
        out_specs=pl.BlockSpec((None, tm, cb), lambda j, i: (0, i, j)),
        out_shape=jax.ShapeDtypeStruct((1, lp, width), BF16),
        compiler_params=_params(("arbitrary", "arbitrary")),
    )(pb, pb, conv_w)


def sc_bwd(pb, ds, conv_w, *, tm, cb):
    _, lp, width = pb.shape
    taps = conv_w.shape[0]
    last = lp // tm - 1
    n_pairs = tm // PAIR

    def body(x_ref, halo_ref, d_ref, w_ref, dx_ref, dw_ref, below):
        step = pl.program_id(1)
        first_tile = step == last

        @pl.when(step == 0)
        def _():
            below[...] = jnp.zeros_like(below)
            dw_ref[...] = jnp.zeros_like(dw_ref)

        def strip(cur, prev, rows, cs, nxt):
            gate, left, right = x_ref[0, rows, cs], x_ref[1, rows, cs], x_ref[2, rows, cs]
            shifted = [_shift_down(cur, prev, taps - 1 - j) for j in range(taps)]
            conv = w_ref[0:1, cs] * shifted[0]
            for j in range(1, taps):
                conv += w_ref[j:j + 1, cs] * shifted[j]
            d = d_ref[rows, cs]
            dc = d * gate
            dp = w_ref[taps - 1:taps, cs] * dc
            for j in range(taps - 1):
                dp += w_ref[j:j + 1, cs] * _shift_up(dc, nxt, taps - 1 - j)
            return d * conv, dp * right, dp * left, dc, [dc * s for s in shifted]

        def pair(r0, above_of):
            top, bot = _pair_rows(r0)
            both = pl.ds(r0, PAIR)
            for c0 in range(0, cb, LANES):
                cs = slice(c0, c0 + LANES)
                cur_t = x_ref[1, top, cs] * x_ref[2, top, cs]
                cur_b = x_ref[1, bot, cs] * x_ref[2, bot, cs]
                *dx_b, dc_b, dw_b = strip(cur_b, cur_t, bot, cs, below[:, cs])
                *dx_t, dc_t, dw_t = strip(cur_t, above_of(cs), top, cs, dc_b)
                below[:, cs] = dc_t
                for s in range(3):
                    dx_ref[s, both, cs] = jnp.concatenate([dx_t[s], dx_b[s]], axis=0).astype(dx_ref.dtype)
                for j in range(taps):
                    dw_ref[j, :, cs] += dw_t[j] + dw_b[j]

        def loop_body(it, carry):
            r0 = pl.multiple_of((n_pairs - 1 - it) * PAIR, PAIR)
            before = pl.ds(pl.multiple_of(r0 - SUB, SUB), SUB)
            pair(r0, lambda cs: x_ref[1, before, cs] * x_ref[2, before, cs])
            return carry

        lax.fori_loop(0, n_pairs - 1, loop_body, 0, unroll=STRIP_UNROLL)
        pair(0, lambda cs: jnp.where(first_tile, 0.0, halo_ref[1, :, cs] * halo_ref[2, :, cs]))

    tile_spec = pl.BlockSpec((3, tm, cb), lambda j, i: (0, last - i, j))
    return pl.pallas_call(
        body,
        name="sc_bwd",
        grid=(width // cb, lp // tm),
        in_specs=[
            tile_spec,
            pl.BlockSpec((3, HALO, cb), lambda j, i: (0, _halo_index(last - i, tm), j)),
            pl.BlockSpec((None, tm, cb), lambda j, i: (0, last - i, j)),
            pl.BlockSpec((taps, cb), lambda j, i: (0, j)),
        ],
        out_specs=[tile_spec, pl.BlockSpec((taps, SUB, cb), lambda j, i: (0, 0, j))],
        out_shape=[jax.ShapeDtypeStruct((3, lp, width), BF16), jax.ShapeDtypeStruct((taps, SUB, width), F32)],
        scratch_shapes=[pltpu.VMEM((SUB, cb), F32)],
        compiler_params=_params(("arbitrary", "arbitrary")),
    )(pb, pb, ds, conv_w)


TILE_BYTES = 1536 * 1024


def _rows_tile(rows, cols, multiple=8):
    if rows * cols * 4 <= TILE_BYTES or rows % multiple:
        return rows
    best = multiple
    for t in range(multiple, rows + 1, multiple):
        if rows % t == 0 and t * cols * 4 <= TILE_BYTES:
            best = t
    return best


def pair_sum(g, landed, core, out_dtype, name):
    _, rows, cols = g.shape
    half = rows // 2
    tr = _rows_tile(half, cols, 16)
    nb = half // tr

    def body(c_ref, g_ref, l_ref, o_ref):
        o_ref[...] = (g_ref[...] + l_ref[...]).astype(out_dtype)

    return pl.pallas_call(
        body,
        name=name,
        grid_spec=pltpu.PrefetchScalarGridSpec(
            num_scalar_prefetch=1,
            grid=(4, nb),
            in_specs=[
                pl.BlockSpec((None, tr, cols), lambda s, i, c: (s, c[0] * nb + i, 0)),
                pl.BlockSpec((None, tr, cols), lambda s, i, c: (s, i, 0)),
            ],
            out_specs=pl.BlockSpec((None, tr, cols), lambda s, i, c: (s, i, 0)),
        ),
        out_shape=jax.ShapeDtypeStruct((4, half, cols), out_dtype),
        compiler_params=_params(("arbitrary", "arbitrary")),
    )(core, g, landed)


def chip_sum(x, name):
    _, rows, cols = x.shape
    tr = _rows_tile(rows, cols, 16)

    def body(x0, x1, x2, x3, o_ref):
        acc = x0[...].astype(F32) + x1[...].astype(F32)
        o_ref[...] = (acc + x2[...].astype(F32)) + x3[...].astype(F32)

    return pl.pallas_call(
        body,
        name=name,
        grid=(rows // tr,),
        in_specs=[pl.BlockSpec((None, tr, cols), lambda i, k=k: (k, i, 0)) for k in range(4)],
        out_specs=pl.BlockSpec((tr, cols), lambda i: (i, 0)),
        out_shape=jax.ShapeDtypeStruct((rows, cols), F32),
        compiler_params=_params(("arbitrary",)),
    )(x, x, x, x)


def adamw(w, g, m, v, name):
    shape = w.shape
    cols = shape[-1]
    rows = w.size // cols
    tr = _rows_tile(rows, cols)

    def body(w_ref, g_ref, m_ref, v_ref, d_ref, m2_ref, v2_ref):
        gv = g_ref[...]
        m2 = ADAM_B1 * m_ref[...] + (1.0 - ADAM_B1) * gv
        v2 = ADAM_B2 * v_ref[...] + (1.0 - ADAM_B2) * (gv * gv)
        m_hat = m2 / (1.0 - ADAM_B1 ** ADAM_STEP)
        v_hat = v2 / (1.0 - ADAM_B2 ** ADAM_STEP)
        d_ref[...] = -ADAM_LR * (m_hat / (jnp.sqrt(v_hat) + ADAM_EPS) + ADAM_WD * w_ref[...])
        m2_ref[...] = m2
        v2_ref[...] = v2

    spec = pl.BlockSpec((tr, cols), lambda i: (i, 0))
    outs = pl.pallas_call(
        body,
        name=name,
        grid=(rows // tr,),
        in_specs=[spec] * 4,
        out_specs=[spec] * 3,
        out_shape=[jax.ShapeDtypeStruct((rows, cols), F32)] * 3,
        compiler_params=_params(("arbitrary",)),
    )(*[t.reshape(rows, cols) for t in (w, g, m, v)])
    return tuple(o.reshape(shape) for o in outs)


MESH_ID = pl.DeviceIdType.MESH
ANY = pl.BlockSpec(memory_space=pl.ANY)


def _place():
    x, y, c = lax.axis_index("x"), lax.axis_index("y"), lax.axis_index("c")
    other_chips = [(1 - x, y), (x, 1 - y), (1 - x, 1 - y)]
    return x, y, c, other_chips


def all_gather_shards(bufs, name):
    n = len(bufs)

    def body(*refs):
        x_refs, o_refs = refs[:n], refs[n:2 * n]
        copies = _gather_copies(x_refs, o_refs, *refs[2 * n:])
        _gather_start(copies)
        _gather_finish(copies)

    outs = pl.pallas_call(
        body,
        name=name,
        in_specs=[ANY] * n,
        out_specs=[ANY] * n,
        out_shape=_gather_out_shapes(bufs),
        scratch_shapes=_gather_sems(n),
    )(*bufs)
    return _set_own_slots(outs, bufs)


def _gather_out_shapes(bufs):
    return [jax.ShapeDtypeStruct((4,) + b.shape, b.dtype) for b in bufs]


def _gather_sems(n):
    return [pltpu.SemaphoreType.DMA((6 * n,)), pltpu.SemaphoreType.DMA((6 * n,))]


def _set_own_slots(outs, bufs):
    if not outs:
        return []
    me = 2 * lax.axis_index("x") + lax.axis_index("y")
    return [lax.dynamic_update_index_in_dim(o, b, me, 0) for o, b in zip(outs, bufs)]


def _gather_copies(x_refs, o_refs, send_sems, recv_sems):
    x, y, c, chips = _place()
    me = 2 * x + y
    sibling = (x, y, 1 - c)

    def part(a, slot, hf):
        half = x_refs[a].shape[0] // 2
        return o_refs[a].at[slot, pl.ds(hf * half, half), :]

    def mine(a):
        half = x_refs[a].shape[0] // 2
        return x_refs[a].at[pl.ds(c * half, half), :]

    def copy(k, src, dst, to):
        return pltpu.make_async_remote_copy(src_ref=src, dst_ref=dst, send_sem=send_sems.at[k],
                                            recv_sem=recv_sems.at[k], device_id=to, device_id_type=MESH_ID)

    sends, arrivals, passes, passed = [], [], [], []
    for a in range(len(x_refs)):
        for j, (px, py) in enumerate(chips):
            landed, theirs = part(a, 2 * px + py, c), part(a, 2 * px + py, 1 - c)
            sends.append(copy(6 * a + j, mine(a), part(a, me, c), (px, py, c)))
            arrivals.append(copy(6 * a + j, mine(a), landed, (px, py, c)))
            passes.append(copy(6 * a + 3 + j, landed, landed, sibling))
            passed.append(copy(6 * a + 3 + j, theirs, theirs, sibling))
    return sends, arrivals, passes, passed


def _gather_start(copies):
    for cp in copies[0]:
        cp.start()


def _gather_finish(copies):
    sends, arrivals, passes, passed = copies
    for arrival, cp in zip(arrivals, passes):
        arrival.wait_recv()
        cp.start()
    for cp in passed:
        cp.wait_recv()
    for cp in sends + passes:
        cp.wait_send()


def swap_halves(bufs, name):
    n = len(bufs)

    def body(*refs):
        x_refs, o_refs = refs[:n], refs[n:2 * n]
        send_sems, recv_sems = refs[2 * n:]
        x, y, c, _ = _place()
        copies = []
        for a in range(n):
            half = bufs[a].shape[1] // 2
            cp = pltpu.make_async_remote_copy(src_ref=x_refs[a].at[:, pl.ds((1 - c) * half, half), :], dst_ref=o_refs[a],
                                              send_sem=send_sems.at[a], recv_sem=recv_sems.at[a],
                                              device_id=(x, y, 1 - c), device_id_type=MESH_ID)
            cp.start()
            copies.append(cp)
        for cp in copies:
            cp.wait()

    return pl.pallas_call(
        body,
        name=name,
        in_specs=[ANY] * n,
        out_specs=[ANY] * n,
        out_shape=[jax.ShapeDtypeStruct((4, b.shape[1] // 2, b.shape[2]), b.dtype) for b in bufs],
        scratch_shapes=[pltpu.SemaphoreType.DMA((n,)), pltpu.SemaphoreType.DMA((n,))],
    )(*bufs)


def scatter_to_chips(bufs, name):
    n = len(bufs)

    def body(*refs):
        x_refs, o_refs = refs[:n], refs[n:2 * n]
        copies = _scatter_copies(x_refs, o_refs, *refs[2 * n:])
        _scatter_start(copies)
        _scatter_finish(copies)

    outs = pl.pallas_call(
        body,
        name=name,
        in_specs=[ANY] * n,
        out_specs=[ANY] * n,
        out_shape=[jax.ShapeDtypeStruct(b.shape, b.dtype) for b in bufs],
        scratch_shapes=_scatter_sems(n),
    )(*bufs)
    return _keep_own_slots(outs, bufs)


def _scatter_sems(n):
    return [pltpu.SemaphoreType.DMA((3 * n,)), pltpu.SemaphoreType.DMA((3 * n,))]


def _keep_own_slots(outs, bufs):
    if not outs:
        return []
    me = 2 * lax.axis_index("x") + lax.axis_index("y")
    return [lax.dynamic_update_index_in_dim(o, lax.dynamic_index_in_dim(b, me, 0, keepdims=False), me, 0)
            for o, b in zip(outs, bufs)]


def _scatter_copies(x_refs, o_refs, send_sems, recv_sems):
    x, y, c, chips = _place()
    me = 2 * x + y

    def copy(a, j, src_slot, dst_slot, px, py):
        return pltpu.make_async_remote_copy(src_ref=x_refs[a].at[src_slot], dst_ref=o_refs[a].at[dst_slot],
                                            send_sem=send_sems.at[3 * a + j], recv_sem=recv_sems.at[3 * a + j],
                                            device_id=(px, py, c), device_id_type=MESH_ID)

    sends = [copy(a, j, 2 * px + py, me, px, py) for a in range(len(x_refs)) for j, (px, py) in enumerate(chips)]
    arrivals = [copy(a, j, me, 2 * px + py, px, py) for a in range(len(x_refs)) for j, (px, py) in enumerate(chips)]
    return sends, arrivals


def _scatter_start(copies):
    for cp in copies[0]:
        cp.start()


def _scatter_finish(copies):
    for cp in copies[1]:
        cp.wait_recv()
    for cp in copies[0]:
        cp.wait_send()


def share_halves(groups, name):
    bufs = [b for grp in groups for b in grp]
    where = [(gi, li) for gi, grp in enumerate(groups) for li in range(len(grp))]
    n = len(bufs)

    def body(*refs):
        x_refs, o_refs = refs[:n], refs[n:n + len(groups)]
        send_sems, recv_sems = refs[n + len(groups):]
        x, y, c, _ = _place()
        sent, arrive = [], []
        for a, (gi, li) in enumerate(where):

            def copy(hf, a=a, gi=gi, li=li):
                return pltpu.make_async_remote_copy(src_ref=x_refs[a], dst_ref=o_refs[gi].at[li, hf],
                                                    send_sem=send_sems.at[a], recv_sem=recv_sems.at[a],
                                                    device_id=(x, y, 1 - c), device_id_type=MESH_ID)

            sent.append(copy(c))
            arrive.append(copy(1 - c))
        for cp in sent:
            cp.start()
        for cp in arrive:
            cp.wait_recv()
        for cp in sent:
            cp.wait_send()

    outs = pl.pallas_call(
        body,
        name=name,
        in_specs=[ANY] * n,
        out_specs=[ANY] * len(groups),
        out_shape=[jax.ShapeDtypeStruct((len(grp), 2) + grp[0].shape, grp[0].dtype) for grp in groups],
        scratch_shapes=[pltpu.SemaphoreType.DMA((n,)), pltpu.SemaphoreType.DMA((n,))],
    )(*bufs)
    c = lax.axis_index("c")
    full = [lax.dynamic_update_index_in_dim(o, jnp.stack(grp), c, 1) for o, grp in zip(outs, groups)]
    return [t.reshape(t.shape[0], 2 * t.shape[2], t.shape[3]) for t in full]


def pair_sums(bufs, dtypes, tag):
    core = lax.axis_index("c").astype(jnp.int32).reshape(1)
    landed = swap_halves(bufs, "rs_pair_" + tag)
    return [pair_sum(b, l, core, dt, "rs_pair_sum_%s%d" % (tag, i)) for i, (b, l, dt) in enumerate(zip(bufs, landed, dtypes))]


def _row_tiles(length):
    return (640, 320) if length > 2048 else (128, 64)


def _divisor_tile(rows, target):
    return max(t for t in range(8, min(rows, target) + 1, 8) if rows % t == 0)


def _local_step(x, target, wt, late_shards, layout_late, reduce_early):
    seq, d = x.shape
    length = N_META + seq
    tm, tm_ffn = _row_tiles(length)
    lp = -(-length // tm) * tm
    tail = jnp.zeros((lp - length, d), F32)
    h0 = jnp.concatenate([wt["meta"], x, tail], axis=0)[None]
    tgt = jnp.concatenate([jnp.zeros((N_META, d), F32), target, tail], axis=0)
    nn = functools.partial(mm_nn, tm=_divisor_tile(lp, 1664))
    nt = functools.partial(mm_nt, tm=_divisor_tile(lp, 1040))
    tn = functools.partial(mm_tn, tm=_divisor_tile(lp, 1664), rb=256)
    nn_ln = functools.partial(mm_nn_ln, tm=_divisor_tile(lp, 832))
    nt_ln_bwd = functools.partial(mm_nt_ln_bwd, tm=_divisor_tile(lp, 832))
    ln_g = [wt["ln_mix_g"][0:1], wt["ln_ffn_g"][0:1], wt["ln_mix_g"][1:2], wt["ln_ffn_g"][1:2]]
    ln_b = [wt["ln_mix_b"][0:1], wt["ln_ffn_b"][0:1], wt["ln_mix_b"][1:2], wt["ln_ffn_b"][1:2]]

    p3 = nn(h0, wt["a3"], name="a_in3")
    pz = nn(h0, wt["az"], name="a_inz")
    pba = nn(h0, wt["a_ba"], name="a_inba")
    qkv = gdn_pre_fwd(p3, wt["a_conv3"], tm=tm, cb=2 * HEAD_DIM)
    gates = gdn_gates_fwd(pba, wt["alog_lanes"], wt["dtb_lanes"], tm=tm)
    o, states, tinv, late_stacks = gdn_chunk_fwd(qkv, gates, late_shards)
    wt = {**wt, **layout_late(late_stacks)}
    onz = gdn_post_fwd(o[None], pz, wt["anorm_b"], tm=tm)
    r1, h1 = nn_ln(onz, wt["a_out"], h0, ln_g[0], ln_b[0], name="a_out_ln1")
    up0 = nn(h1, wt["up"][0], name="up0")
    act0 = ffn_act_fwd(up0, wt["fconv"][0], tm=tm_ffn, name="ffn_act0")
    r2, h2 = nn_ln(act0, wt["down"][0], h1, ln_g[1], ln_b[1], name="down0_ln2")
    pb = nn(h2, wt["b_in"], name="b_in")
    sc = sc_fwd(pb, wt["b_conv"], tm=tm_ffn, cb=d)
    r3, h3 = nn_ln(sc, wt["b_out"], h2, ln_g[2], ln_b[2], name="b_out_ln3")
    up1 = nn(h3, wt["up"][1], name="up1")
    act1 = ffn_act_fwd(up1, wt["fconv"][1], tm=tm_ffn, name="ffn_act1")
    r4, h4 = nn_ln(act1, wt["down"][1], h3, ln_g[3], ln_b[3], name="down1_ln4")

    dh4, loss_part = loss_grad(h4, tgt, first=N_META, count=seq, tm=tm)

    grads = {}
    dr4, dgb4 = ln_bwd(r4, dh4, ln_g[3], tm=tm, name="ln4_bwd")
    d_down1 = tn(act1, dr4, name="d_down1")
    dact1 = nt(dr4, wt["down"][1], name="d_act1")
    dup1, dfconv1 = ffn_act_bwd(up1, dact1, wt["fconv"][1], tm=tm_ffn, name="ffn_act1_bwd")
    dup1 = dup1.reshape(up1.shape)
    d_up1 = tn(h3, dup1, name="d_up1")

    dr3, dgb3 = nt_ln_bwd(dup1, wt["up"][1], dr4, r3, ln_g[2], name="d_h3_ln3")
    d_bout = tn(sc, dr3, name="d_b_out")
    dsc = nt(dr3, wt["b_out"], name="d_sc")
    dpb, dbconv = sc_bwd(pb, dsc, wt["b_conv"], tm=tm_ffn, cb=d)
    d_bin = tn(h2, dpb, name="d_b_in")

    dr2, dgb2 = nt_ln_bwd(dpb, wt["b_in"], dr3, r2, ln_g[1], name="d_h2_ln2")
    d_down0 = tn(act0, dr2, name="d_down0")
    dact0 = nt(dr2, wt["down"][0], name="d_act0")
    dup0, dfconv0 = ffn_act_bwd(up0, dact0, wt["fconv"][0], tm=tm_ffn, name="ffn_act0_bwd")
    dup0 = dup0.reshape(up0.shape)
    d_up0 = tn(h1, dup0, name="d_up0")
    grads["b_w_in"] = [d_bin[0].transpose(1, 0, 2).reshape(d, 4, 3 * d // 4).transpose(1, 0, 2)]
    grads["b_w_out"] = [d_bout.reshape(4, d // 4, d)]
    grads["ffn_w_up"] = [d_up0[0], d_up1[0]]
    grads["ffn_w_down"] = [t.reshape(4, -1, d) for t in (d_down0, d_down1)]
    leaving = reduce_early(grads)

    dr1, dgb1 = nt_ln_bwd(dup0, wt["up"][0], dr2, r1, ln_g[0], name="d_h1_ln1")
    d_aout = tn(onz, dr1, name="d_a_out")
    donz = nt(dr1, wt["a_out"], name="d_onz")
    d_o, dz, dnw = gdn_post_bwd(o[None], pz, donz, wt["anorm_b"], tm=tm)
    dqkv, dgates, landed = gdn_chunk_bwd(qkv, gates, states, tinv, d_o[0], leaving)
    dp3, daconv = gdn_pre_bwd(p3, dqkv, wt["a_conv3"], tm=tm, cb=2 * HEAD_DIM)
    dpba, dscal = gdn_gates_bwd(pba, dgates, wt["alog_lanes"], wt["dtb_lanes"], tm=tm)
    d_a3 = tn(h0, dp3, name="d_a_in3")
    d_az = tn(h0, dz, name="d_a_inz")
    d_aba = tn(h0, dpba, name="d_a_inba")
    dh0 = nt(dp3, wt["a3"], res=dr1, res_scale=ALPHA, name="d_h0a")
    dh0 = nt(dz, wt["az"], res=dh0, res_scale=1.0, name="d_h0z")
    dh0 = nt(dpba, wt["a_ba"], res=dh0, res_scale=1.0, name="d_h0")

    width = HEADS * HEAD_DIM
    d_a_in = jnp.concatenate([d_a3[0, 0], d_a3[0, 1], d_a3[0, 2], d_az[0, 0], d_aba[0, 0][:, :2 * HEADS]], axis=1)
    n_in = d_a_in.shape[1] // 4
    grads["a_w_in"] = [d_a_in.reshape(d, 4, n_in).transpose(1, 0, 2)]
    grads["a_w_out"] = [d_aout.reshape(4, width // 4, d)]
    grads["a_conv"] = daconv.sum(axis=2).transpose(1, 0, 2).reshape(1, GDN_CONV, 3 * width)
    per_head = dscal.sum(axis=1)[:, HEADS:2 * HEADS]
    grads["a_log"] = per_head[0][None]
    grads["a_dt_bias"] = per_head[1][None]
    grads["a_norm"] = dnw.reshape(8, HEADS, HEAD_DIM).sum(axis=(0, 1))[None]
    grads["b_conv"] = dbconv.sum(axis=1)[None]
    lns = [dgb1, dgb2, dgb3, dgb4]
    grads["ln_mix_g"] = jnp.stack([lns[0][0].sum(0), lns[2][0].sum(0)])
    grads["ln_mix_b"] = jnp.stack([lns[0][1].sum(0), lns[2][1].sum(0)])
    grads["ln_ffn_g"] = jnp.stack([lns[1][0].sum(0), lns[3][0].sum(0)])
    grads["ln_ffn_b"] = jnp.stack([lns[1][1].sum(0), lns[3][1].sum(0)])
    grads["ffn_conv"] = jnp.stack([t.sum(axis=2).transpose(1, 0, 2).reshape(FFN_CONV, -1) for t in (dfconv0, dfconv1)])
    grads["meta"] = dh0[0, :N_META]
    return loss_part, dh0, grads, landed


WEIGHTS = ["meta", "a_w_in", "a_conv", "a_log", "a_dt_bias", "a_norm", "a_w_out", "b_w_in", "b_conv", "b_w_out",
           "ln_mix_g", "ln_mix_b", "ffn_w_up", "ffn_conv", "ffn_w_down", "ln_ffn_g", "ln_ffn_b"]
EARLY_WEIGHTS = ["a_w_in", "a_w_out"]
LATE_WEIGHTS = ["b_w_in", "b_w_out", "ffn_w_up", "ffn_w_down"]
MATMUL_WEIGHTS = EARLY_WEIGHTS + LATE_WEIGHTS
SMALL_SHARDED = ["a_conv", "b_conv", "ffn_conv", "meta"]
REPLICATED = ["a_log", "a_dt_bias", "a_norm", "ln_mix_g", "ln_mix_b", "ln_ffn_g", "ln_ffn_b"]
SHARD_AXIS = {"meta": 1, "a_w_in": 2, "a_conv": 2, "a_w_out": 1, "b_w_in": 2, "b_conv": 2, "b_w_out": 1,
              "ffn_w_up": 2, "ffn_conv": 2, "ffn_w_down": 1}
PACK_COLS = 1024
PACK_ROWS_MULTIPLE = 32


def _pack(pieces, lead=()):
    flat = jnp.concatenate([p.reshape(lead + (-1,)) for p in pieces], axis=-1)
    n = flat.shape[-1]
    rows = -(-n // (PACK_COLS * PACK_ROWS_MULTIPLE)) * PACK_ROWS_MULTIPLE
    flat = jnp.pad(flat, [(0, 0)] * len(lead) + [(0, rows * PACK_COLS - n)])
    return flat.reshape(lead + (rows, PACK_COLS))


def _unpack(buf, shapes, lead=()):
    flat = buf.reshape(lead + (-1,))
    out, off = [], 0
    for shp in shapes:
        n = 1
        for s in shp:
            n *= s
        out.append(flat[..., off:off + n].reshape(lead + tuple(shp)))
        off += n
    return out


def _join_shards(stacked, axis):
    return jnp.concatenate([stacked[k] for k in range(4)], axis=axis)


def _split_shards(full, axis):
    return jnp.stack(jnp.split(full, 4, axis=axis))


def _weight_layers(w, names):
    return [w[n][l].astype(BF16) for n in names for l in range(w[n].shape[0])]


def _per_weight(arrays, w, names):
    it = iter(arrays)
    return {n: [next(it) for _ in range(w[n].shape[0])] for n in names}


def _layout_early(full, w):
    width = HEADS * HEAD_DIM
    wt = {n: w[n] for n in ("ln_mix_g", "ln_mix_b", "ln_ffn_g", "ln_ffn_b")}
    w_in = _join_shards(full["a_w_in"][0], 1)
    d = w_in.shape[0]
    n_ff = full["ffn_conv"].shape[2] // 2
    blocks = [w_in[:, s * width:(s + 1) * width] for s in range(4)]
    wt["a3"] = jnp.stack(blocks[:3])[None]
    wt["az"] = blocks[3][None, None]
    wt["a_ba"] = jnp.pad(w_in[:, 4 * width:], ((0, 0), (0, HEAD_DIM - 2 * HEADS)))[None, None]
    wt["a_out"] = full["a_w_out"][0].reshape(1, 1, width, d)
    wt["a_conv3"] = full["a_conv"][0].reshape(GDN_CONV, 3, width).transpose(1, 0, 2)
    wt["b_conv"] = full["b_conv"][0]
    wt["fconv"] = [full["ffn_conv"][l].reshape(FFN_CONV, 2, n_ff).transpose(1, 0, 2) for l in range(2)]
    wt["meta"] = full["meta"]
    in_g_lanes = (HEADS, HEAD_DIM - 2 * HEADS)
    wt["alog_lanes"] = jnp.pad(w["a_log"][0], in_g_lanes)[None]
    wt["dtb_lanes"] = jnp.pad(w["a_dt_bias"][0], in_g_lanes)[None]
    wt["anorm_b"] = jnp.tile(w["a_norm"][0], HEADS)[None]
    return wt


def _layout_late(full):
    d = full["b_w_in"][0].shape[1]
    n_ff = full["ffn_w_up"][0].shape[2]
    return {
        "b_in": _join_shards(full["b_w_in"][0], 1).reshape(d, 3, d).transpose(1, 0, 2)[None],
        "b_out": full["b_w_out"][0].reshape(1, 1, d, d),
        "up": [t[None] for t in full["ffn_w_up"]],
        "down": [t.reshape(2, 1, n_ff, d) for t in full["ffn_w_down"]],
    }


def kernel(x, meta, a_w_in, a_conv, a_log, a_dt_bias, a_norm, a_w_out, b_w_in, b_conv, b_w_out, ln_mix_g, ln_mix_b, ffn_w_up, ffn_conv, ffn_w_down, ln_ffn_g, ln_ffn_b, loss_target, m_meta, m_a_w_in, m_a_conv, m_a_log, m_a_dt_bias, m_a_norm, m_a_w_out, m_b_w_in, m_b_conv, m_b_w_out, m_ln_mix_g, m_ln_mix_b, m_ffn_w_up, m_ffn_conv, m_ffn_w_down, m_ln_ffn_g, m_ln_ffn_b, v_meta, v_a_w_in, v_a_conv, v_a_log, v_a_dt_bias, v_a_norm, v_a_w_out, v_b_w_in, v_b_conv, v_b_w_out, v_ln_mix_g, v_ln_mix_b, v_ffn_w_up, v_ffn_conv, v_ffn_w_down, v_ln_ffn_g, v_ln_ffn_b):
    w = dict(meta=meta, a_w_in=a_w_in, a_conv=a_conv, a_log=a_log, a_dt_bias=a_dt_bias, a_norm=a_norm, a_w_out=a_w_out,
             b_w_in=b_w_in, b_conv=b_conv, b_w_out=b_w_out, ln_mix_g=ln_mix_g, ln_mix_b=ln_mix_b, ffn_w_up=ffn_w_up,
             ffn_conv=ffn_conv, ffn_w_down=ffn_w_down, ln_ffn_g=ln_ffn_g, ln_ffn_b=ln_ffn_b)
    m = dict(meta=m_meta, a_w_in=m_a_w_in, a_conv=m_a_conv, a_log=m_a_log, a_dt_bias=m_a_dt_bias, a_norm=m_a_norm,
             a_w_out=m_a_w_out, b_w_in=m_b_w_in, b_conv=m_b_conv, b_w_out=m_b_w_out, ln_mix_g=m_ln_mix_g,
             ln_mix_b=m_ln_mix_b, ffn_w_up=m_ffn_w_up, ffn_conv=m_ffn_conv, ffn_w_down=m_ffn_w_down,
             ln_ffn_g=m_ln_ffn_g, ln_ffn_b=m_ln_ffn_b)
    v = dict(meta=v_meta, a_w_in=v_a_w_in, a_conv=v_a_conv, a_log=v_a_log, a_dt_bias=v_a_dt_bias, a_norm=v_a_norm,
             a_w_out=v_a_w_out, b_w_in=v_b_w_in, b_conv=v_b_conv, b_w_out=v_b_w_out, ln_mix_g=v_ln_mix_g,
             ln_mix_b=v_ln_mix_b, ffn_w_up=v_ffn_w_up, ffn_conv=v_ffn_conv, ffn_w_down=v_ffn_w_down,
             ln_ffn_g=v_ln_ffn_g, ln_ffn_b=v_ln_ffn_b)
    seq = x.shape[1]
    *stacks, small = all_gather_shards(_weight_layers(w, EARLY_WEIGHTS) + [_pack([w[n] for n in SMALL_SHARDED])],
                                       "gather_early")
    full = _per_weight(stacks, w, EARLY_WEIGHTS)
    for n, t in zip(SMALL_SHARDED, _unpack(small, [w[n].shape for n in SMALL_SHARDED], lead=(4,))):
        full[n] = _join_shards(t, SHARD_AXIS[n])

    def layout_late(late_stacks):
        return _layout_late(_per_weight(late_stacks, w, LATE_WEIGHTS))

    def reduce_early(grads):
        bufs = [g for n in LATE_WEIGHTS for g in grads[n]]
        return pair_sums(bufs, [BF16] * len(bufs), "late")

    loss_part, dh0, grads, landed_late = _local_step(x[0], loss_target[0], _layout_early(full, w),
                                                     _weight_layers(w, LATE_WEIGHTS), layout_late, reduce_early)
    pieces = [_split_shards(grads[n], SHARD_AXIS[n]) for n in SMALL_SHARDED]
    same = jnp.concatenate([grads[n].reshape(-1) for n in REPLICATED] + [jnp.sum(loss_part).reshape(1)])
    pieces.append(jnp.broadcast_to(same, (4,) + same.shape))
    bufs = [g for n in EARLY_WEIGHTS for g in grads[n]] + [_pack(pieces, lead=(4,))]
    landed = scatter_to_chips(pair_sums(bufs, [BF16] * (len(bufs) - 1) + [F32], "early"), "rs_chips_early")
    totals = [chip_sum(t, "rs_chip_sum%d" % i) for i, t in enumerate(landed + landed_late)]
    by_weight = _per_weight(totals[:len(bufs) - 1] + totals[len(bufs):], w, MATMUL_WEIGHTS)
    *shared, small_total = share_halves([by_weight[n] for n in MATMUL_WEIGHTS] + [[totals[len(bufs) - 1]]], "rs_share")
    grad_w = {n: t.reshape(w[n].shape) for n, t in zip(MATMUL_WEIGHTS, shared)}
    rest = SMALL_SHARDED + REPLICATED
    unpacked = _unpack(small_total[0], [w[n].shape for n in rest] + [()])
    grad_w.update(zip(rest, unpacked[:-1]))
    loss = unpacked[-1]
    grad_x = dh0[:, N_META:N_META + seq]
    steps = [adamw(w[n], grad_w[n], m[n], v[n], "adamw_" + n) for n in WEIGHTS]
    return (loss, grad_x, *[grad_w[n] for n in WEIGHTS], *[s[0] for s in steps], *[s[1] for s in steps],
            *[s[2] for s in steps])
```

```python
import functools

import jax
import jax.numpy as jnp
from jax import lax
from jax.experimental import pallas as pl
from jax.experimental.pallas import tpu as pltpu

F32 = jnp.float32
BF16 = jnp.bfloat16
HI = lax.Precision.HIGHEST

N_META = 16
HEADS = 8
HEAD_DIM = 128
CHUNK = 64
GDN_CONV = 4
SC_CONV = 3
FFN_CONV = 3
ALPHA = 4.0 ** 0.25
LN_EPS = 1e-5
RMS_EPS = 1e-6
L2_EPS = 1e-6
Q_SCALE = HEAD_DIM ** -0.5

ADAM_LR = 0.001
ADAM_B1 = 0.9
ADAM_B2 = 0.999
ADAM_EPS = 1e-08
ADAM_WD = 0.01
ADAM_STEP = 10

HALO = 8
VMEM_LIMIT = 48 * 1024 * 1024


def _params(sem=None):
    return pltpu.CompilerParams(dimension_semantics=sem, vmem_limit_bytes=VMEM_LIMIT)


def _dot(a, b, prec=None):
    return jnp.dot(a, b, preferred_element_type=F32, precision=prec)


def _dot_nt(a, b, prec=None):
    return lax.dot_general(a, b, (((1,), (1,)), ((), ())), preferred_element_type=F32, precision=prec)


def _dot_tn(a, b, prec=None):
    return lax.dot_general(a, b, (((0,), (0,)), ((), ())), preferred_element_type=F32, precision=prec)


def _sigmoid(x):
    return 1.0 / (1.0 + jnp.exp(-x))


def _tri_masks():
    r = lax.broadcasted_iota(jnp.int32, (CHUNK, CHUNK), 0)
    c = lax.broadcasted_iota(jnp.int32, (CHUNK, CHUNK), 1)
    return r >= c, r > c, r == c


def _split_hi_lo(x):
    hi = x.astype(BF16)
    return hi, (x - hi.astype(F32)).astype(BF16)


def _mask_dot(mask, x):
    hi, lo = _split_hi_lo(x)
    return _dot(mask, hi) + _dot(mask, lo)


@jax.custom_vjp
def _cumsum_rows(g):
    causal, _, _ = _tri_masks()
    return _mask_dot(causal.astype(BF16), g)


def _cumsum_rows_fwd(g):
    return _cumsum_rows(g), None


def _cumsum_rows_bwd(_, dy):
    _, strict, _ = _tri_masks()
    return (_mask_dot((~strict).astype(BF16), dy),)


_cumsum_rows.defvjp(_cumsum_rows_fwd, _cumsum_rows_bwd)


def _dot_split3(a, b):
    a_hi, a_lo = _split_hi_lo(a)
    b_hi, b_lo = _split_hi_lo(b)
    return _dot(a_hi, b_hi) + (_dot(a_hi, b_lo) + _dot(a_lo, b_hi))


@jax.custom_vjp
def _dot_precise(a, b):
    return _dot_split3(a, b)


def _dot_precise_fwd(a, b):
    return _dot_split3(a, b), (a, b)


def _dot_precise_bwd(operands, ct):
    a, b = operands
    return _dot_split3(ct, b.T), _dot_split3(a.T, ct)


_dot_precise.defvjp(_dot_precise_fwd, _dot_precise_bwd)


def _gdn_m(ks, g64s, bbs):
    causal, strict, _ = _tri_masks()
    a = [_cumsum_rows(g) for g in g64s]
    decay = [jnp.exp(jnp.where(causal, x - x.T, -1e30)) for x in a]
    kk = [_dot_nt(k * b, k) for k, b in zip(ks, bbs)]
    return [jnp.where(strict, x * d, 0.0) for x, d in zip(kk, decay)]


def _gdn_inverse_stages(ks, g64s, bbs):
    ms = _gdn_m(ks, g64s, bbs)
    yield
    r = lax.broadcasted_iota(jnp.int32, (CHUNK, CHUNK), 0)
    c = lax.broadcasted_iota(jnp.int32, (CHUNK, CHUNK), 1)
    eye = (r == c).astype(F32)
    same = [jnp.right_shift(r, s) == jnp.right_shift(c, s) for s in (3, 4, 5)]
    d = [jnp.where(same[0], m, 0.0) for m in ms]
    p = [_dot(x, x) for x in d]
    yield
    t = [eye - x for x in d]
    t = [x + _dot(x, y) for x, y in zip(t, p)]
    p = [_dot(x, x) for x in p]
    yield
    t = [x + _dot(x, y) for x, y in zip(t, p)]
    yield
    for inner, outer in ((same[0], same[1]), (same[1], same[2]), (same[2], None)):
        joins = ~inner if outer is None else (outer & ~inner)
        o = [_dot(x, jnp.where(joins, m, 0.0)) for x, m in zip(t, ms)]
        yield
        t = [x - _dot(y, x) for x, y in zip(t, o)]
        yield
    res = [eye - x - _dot_split3(m, x) for m, x in zip(ms, t)]
    yield
    return [x + _dot(x, y) for x, y in zip(t, res)]


def _gdn_apply_stages(qs, ks, vs, gbs, g64s, bbs, ss, ts):
    causal, _, _ = _tri_masks()
    n = range(len(qs))
    gc = [_cumsum_rows(g) for g in gbs]
    a = [_cumsum_rows(g) for g in g64s]
    qk = [_dot_nt(qs[h], ks[h]) for h in n]
    yield
    decay = [jnp.exp(jnp.where(causal, x - x.T, -1e30)) for x in a]
    eg = [jnp.exp(x) for x in gc]
    u = [_dot_precise(ts[h], vs[h] * bbs[h]) for h in n]
    w = [_dot_precise(ts[h], ks[h] * bbs[h] * eg[h]) for h in n]
    qk = [qk[h] * decay[h] for h in n]
    gl = [jnp.sum(g, axis=0, keepdims=True) for g in gbs]
    kd = [ks[h] * jnp.exp(gl[h] - gc[h]) for h in n]
    yield
    v_new = [u[h] - _dot(w[h], ss[h]) for h in n]
    q_s = [_dot(qs[h] * eg[h], ss[h]) for h in n]
    yield
    o = [q_s[h] + _dot(qk[h], v_new[h]) for h in n]
    s2 = [ss[h] * jnp.exp(gl[h]) + _dot_tn(kd[h], v_new[h]) for h in n]
    return o, s2


def _run_stages(*generators):
    results = [None] * len(generators)
    live = dict(enumerate(generators))
    while live:
        for i, gen in list(live.items()):
            try:
                next(gen)
            except StopIteration as stop:
                results[i] = stop.value
                del live[i]
    return results


def _head_slices(h):
    return slice(h * HEAD_DIM, (h + 1) * HEAD_DIM), slice(h * HEAD_DIM, h * HEAD_DIM + CHUNK)


def _gdn_head_values(x_ref, gate_ref):
    qs, ks, vs = ([x_ref[s, :, _head_slices(h)[0]] for h in range(HEADS)] for s in range(3))
    gcols = [gate_ref[:, HEADS + h:HEADS + h + 1] for h in range(HEADS)]
    bcols = [gate_ref[:, h:h + 1] for h in range(HEADS)]
    return qs, ks, vs, gcols, bcols


def _expand_cols(gcols, bcols):
    return ([jnp.broadcast_to(g, (CHUNK, HEAD_DIM)) for g in gcols], [jnp.broadcast_to(g, (CHUNK, CHUNK)) for g in gcols],
            [jnp.broadcast_to(b, (CHUNK, HEAD_DIM)) for b in bcols])


def _gdn_inverse_cols(ks, gcols, bcols):
    _, g64s, bbs = _expand_cols(gcols, bcols)
    return _gdn_inverse_stages(ks, g64s, bbs)


def _gdn_apply_cols_stages(qs, ks, vs, gcols, bcols, ss, ts):
    gbs, g64s, bbs = _expand_cols(gcols, bcols)
    return _gdn_apply_stages(qs, ks, vs, gbs, g64s, bbs, ss, ts)


def _gdn_apply_cols(qs, ks, vs, gcols, bcols, ss, ts):
    return _run_stages(_gdn_apply_cols_stages(qs, ks, vs, gcols, bcols, ss, ts))[0]


def _gdn_m_cols(ks, gcols, bcols):
    _, g64s, bbs = _expand_cols(gcols, bcols)
    return _gdn_m(ks, g64s, bbs)


def _gate_lanes(bcols, gcols):
    lane = lax.broadcasted_iota(jnp.int32, (CHUNK, HEAD_DIM), 1)
    out = jnp.zeros((CHUNK, HEAD_DIM), F32)
    for h in range(HEADS):
        out = jnp.where(lane == h, jnp.broadcast_to(bcols[h], out.shape), out)
        out = jnp.where(lane == HEADS + h, jnp.broadcast_to(gcols[h], out.shape), out)
    return out


def gdn_chunk_fwd(qkv, gates, gather=()):
    _, lp, width = qkv.shape
    n_chunks = lp // CHUNK
    n = len(gather)

    def body(x_ref, gate_ref, next_ref, next_gate_ref, *refs):
        shard_refs, (o_ref, s_ref, t_ref), refs = refs[:n], refs[n:n + 3], refs[n + 3:]
        stack_refs, state, t_next, sems = refs[:n], refs[n], refs[n + 1], refs[n + 2:]
        copies = _gather_copies(shard_refs, stack_refs, *sems) if n else None

        def inverse_stages(ref, g_ref):
            _, ks, _, gcols, bcols = _gdn_head_values(ref, g_ref)
            return _gdn_inverse_cols(ks, gcols, bcols)

        @pl.when(pl.program_id(0) == 0)
        def _():
            state[...] = jnp.zeros_like(state)
            for h, t in enumerate(_run_stages(inverse_stages(x_ref, gate_ref))[0]):
                t_next[h] = t
            if n:
                _gather_start(copies)

        qs, ks, vs, gcols, bcols = _gdn_head_values(x_ref, gate_ref)
        ss = [state[h] for h in range(HEADS)]
        ts = [t_next[h] for h in range(HEADS)]
        ts_next, (os_, s2) = _run_stages(inverse_stages(next_ref, next_gate_ref),
                                         _gdn_apply_cols_stages(qs, ks, vs, gcols, bcols, ss, ts))
        for h in range(HEADS):
            s_ref[0, h] = ss[h]
            t_ref[0, h] = ts[h]
            t_next[h] = ts_next[h]
            o_ref[:, _head_slices(h)[0]] = os_[h]
            state[h] = s2[h]

        if n:
            @pl.when(pl.program_id(0) == n_chunks - 1)
            def _():
                _gather_finish(copies)

    o, states, tinv, *stacks = pl.pallas_call(
        body,
        name="gdn_chunk_fwd",
        grid=(n_chunks,),
        in_specs=[pl.BlockSpec((3, CHUNK, width), lambda c: (0, c, 0)),
                  pl.BlockSpec((CHUNK, HEAD_DIM), lambda c: (c, 0)),
                  pl.BlockSpec((3, CHUNK, width), lambda c: (0, jnp.minimum(c + 1, n_chunks - 1), 0)),
                  pl.BlockSpec((CHUNK, HEAD_DIM), lambda c: (jnp.minimum(c + 1, n_chunks - 1), 0))] + [ANY] * n,
        out_specs=[
            pl.BlockSpec((CHUNK, width), lambda c: (c, 0)),
            pl.BlockSpec((1, HEADS, HEAD_DIM, HEAD_DIM), lambda c: (c, 0, 0, 0)),
            pl.BlockSpec((1, HEADS, CHUNK, CHUNK), lambda c: (c, 0, 0, 0)),
        ] + [ANY] * n,
        out_shape=[
            jax.ShapeDtypeStruct((lp, width), F32),
            jax.ShapeDtypeStruct((n_chunks, HEADS, HEAD_DIM, HEAD_DIM), F32),
            jax.ShapeDtypeStruct((n_chunks, HEADS, CHUNK, CHUNK), F32),
        ] + _gather_out_shapes(gather),
        scratch_shapes=[pltpu.VMEM((HEADS, HEAD_DIM, HEAD_DIM), F32), pltpu.VMEM((HEADS, CHUNK, CHUNK), F32)]
        + (_gather_sems(n) if n else []),
        compiler_params=_params(("arbitrary",)),
    )(qkv, gates, qkv, gates, *gather)
    return o, states, tinv, _set_own_slots(stacks, gather)


def gdn_chunk_bwd(qkv, gates, states, tinv, d_o, scatter=()):
    _, lp, width = qkv.shape
    n_chunks = lp // CHUNK
    last = n_chunks - 1
    n = len(scatter)

    def body(x_ref, gate_ref, s_ref, t_ref, do_ref, *refs):
        leaving_refs, dx_ref, dgate_ref, refs = refs[:n], refs[n], refs[n + 1], refs[n + 2:]
        landing_refs, dstate, sems = refs[:n], refs[n], refs[n + 1:]
        copies = _scatter_copies(leaving_refs, landing_refs, *sems) if n else None

        @pl.when(pl.program_id(0) == 0)
        def _():
            dstate[...] = jnp.zeros_like(dstate)
            if n:
                _scatter_start(copies)

        heads = range(HEADS)
        qs, ks, vs, gcols, bcols = _gdn_head_values(x_ref, gate_ref)
        ss = [s_ref[0, h] for h in heads]
        ts = [t_ref[0, h] for h in heads]
        d_out = ([do_ref[:, _head_slices(h)[0]] for h in heads], [dstate[h] for h in heads])
        _, vjp_apply = jax.vjp(_gdn_apply_cols, qs, ks, vs, gcols, bcols, ss, ts)
        dq, dk, dv, dg, db, ds, dt = vjp_apply(d_out)
        tts = [t.T for t in ts]
        dm = [_dot(tts[h], dt[h]) for h in heads]
        dm = [-_dot(dm[h], tts[h]) for h in heads]
        _, vjp_m = jax.vjp(_gdn_m_cols, ks, gcols, bcols)
        dk2, dg2, db2 = vjp_m(dm)
        for h in heads:
            sl = _head_slices(h)[0]
            dx_ref[0, :, sl] = dq[h]
            dx_ref[1, :, sl] = dk[h] + dk2[h]
            dx_ref[2, :, sl] = dv[h]
            dstate[h] = ds[h]
        dgate_ref[...] = _gate_lanes([db[h] + db2[h] for h in heads], [dg[h] + dg2[h] for h in heads])

        if n:
            @pl.when(pl.program_id(0) == n_chunks - 1)
            def _():
                _scatter_finish(copies)

    dqkv, dgates, *landed = pl.pallas_call(
        body,
        name="gdn_chunk_bwd",
        grid=(n_chunks,),
        in_specs=[
            pl.BlockSpec((3, CHUNK, width), lambda c: (0, last - c, 0)),
            pl.BlockSpec((CHUNK, HEAD_DIM), lambda c: (last - c, 0)),
            pl.BlockSpec((1, HEADS, HEAD_DIM, HEAD_DIM), lambda c: (last - c, 0, 0, 0)),
            pl.BlockSpec((1, HEADS, CHUNK, CHUNK), lambda c: (last - c, 0, 0, 0)),
            pl.BlockSpec((CHUNK, width), lambda c: (last - c, 0)),
        ] + [ANY] * n,
        out_specs=[pl.BlockSpec((3, CHUNK, width), lambda c: (0, last - c, 0)),
                   pl.BlockSpec((CHUNK, HEAD_DIM), lambda c: (last - c, 0))] + [ANY] * n,
        out_shape=[jax.ShapeDtypeStruct(qkv.shape, F32), jax.ShapeDtypeStruct(gates.shape, F32)]
        + [jax.ShapeDtypeStruct(b.shape, b.dtype) for b in scatter],
        scratch_shapes=[pltpu.VMEM((HEADS, HEAD_DIM, HEAD_DIM), F32)] + (_scatter_sems(n) if n else []),
        compiler_params=_params(("arbitrary",)),
    )(qkv, gates, states, tinv, d_o, *scatter)
    return dqkv, dgates, _keep_own_slots(landed, scatter)


def mm_nn(a, b, *, tm, name):
    ks, m, tk = a.shape
    _, ns, _, tn = b.shape

    def body(a_ref, b_ref, o_ref):
        p = _dot(a_ref[...].astype(BF16), b_ref[...])

        @pl.when(pl.program_id(2) == 0)
        def _():
            o_ref[...] = p

        @pl.when(pl.program_id(2) > 0)
        def _():
            o_ref[...] += p

    return pl.pallas_call(
        body,
        name=name,
        grid=(ns, m // tm, ks),
        in_specs=[
            pl.BlockSpec((None, tm, tk), lambda n, i, k: (k, i, 0)),
            pl.BlockSpec((None, None, tk, tn), lambda n, i, k: (k, n, 0, 0)),
        ],
        out_specs=pl.BlockSpec((None, tm, tn), lambda n, i, k: (n, i, 0)),
        out_shape=jax.ShapeDtypeStruct((ns, m, tn), F32),
        compiler_params=_params(("arbitrary", "arbitrary", "arbitrary")),
    )(a, b)


def mm_nt(dy, w, *, tm, name, res=None, res_scale=1.0):
    ns, m, tn = dy.shape
    ks, _, tk, _ = w.shape

    def body(*refs):
        if res is None:
            dy_ref, w_ref, o_ref = refs
        else:
            dy_ref, w_ref, r_ref, o_ref = refs
        p = _dot_nt(dy_ref[...].astype(BF16), w_ref[...])

        @pl.when(pl.program_id(2) == 0)
        def _():
            o_ref[...] = p if res is None else p + res_scale * r_ref[...]

        @pl.when(pl.program_id(2) > 0)
        def _():
            o_ref[...] += p

    in_specs = [
        pl.BlockSpec((None, tm, tn), lambda k, i, n: (n, i, 0)),
        pl.BlockSpec((None, None, tk, tn), lambda k, i, n: (k, n, 0, 0)),
    ]
    args = [dy, w]
    if res is not None:
        in_specs.append(pl.BlockSpec((None, tm, tk), lambda k, i, n: (k, i, 0)))
        args.append(res)
    return pl.pallas_call(
        body,
        name=name,
        grid=(ks, m // tm, ns),
        in_specs=in_specs,
        out_specs=pl.BlockSpec((None, tm, tk), lambda k, i, n: (k, i, 0)),
        out_shape=jax.ShapeDtypeStruct((ks, m, tk), F32),
        compiler_params=_params(("arbitrary", "arbitrary", "arbitrary")),
    )(*args)


def mm_tn(x, dy, *, tm, name, rb=None):
    ks, m, tk = x.shape
    ns, _, tn = dy.shape
    rb = tk if rb is None else rb

    def body(x_ref, dy_ref, o_ref):
        @pl.when(pl.program_id(2) == 0)
        def _():
            o_ref[...] = jnp.zeros_like(o_ref)

        dyb = dy_ref[...].astype(BF16)
        for r in range(0, tk, rb):
            o_ref[r:r + rb, :] += _dot_tn(x_ref[:, r:r + rb].astype(BF16), dyb)

    return pl.pallas_call(
        body,
        name=name,
        grid=(ks, ns, m // tm),
        in_specs=[
            pl.BlockSpec((None, tm, tk), lambda k, n, i: (k, i, 0)),
            pl.BlockSpec((None, tm, tn), lambda k, n, i: (n, i, 0)),
        ],
        out_specs=pl.BlockSpec((None, None, tk, tn), lambda k, n, i: (k, n, 0, 0)),
        out_shape=jax.ShapeDtypeStruct((ks, ns, tk, tn), F32),
        compiler_params=_params(("arbitrary", "arbitrary", "arbitrary")),
    )(x, dy)


def _row_partial(x):
    rows, c = x.shape
    return jnp.sum(x.reshape(rows // 8, 8, c), axis=0)


def _layer_norm(r, g, b):
    mu = jnp.mean(r, axis=-1, keepdims=True)
    xc = r - mu
    var = jnp.mean(xc * xc, axis=-1, keepdims=True)
    return xc * lax.rsqrt(var + LN_EPS) * g + b


def _layer_norm_bwd(x, dh, g):
    mu = jnp.mean(x, axis=-1, keepdims=True)
    xc = x - mu
    rstd = lax.rsqrt(jnp.mean(xc * xc, axis=-1, keepdims=True) + LN_EPS)
    xh = xc * rstd
    dxh = dh * g
    m1 = jnp.mean(dxh, axis=-1, keepdims=True)
    m2 = jnp.mean(dxh * xh, axis=-1, keepdims=True)
    return rstd * (dxh - m1 - xh * m2), _row_partial(dh * xh), _row_partial(dh)


def mm_nn_ln(a, b, h_prev, g, beta, *, tm, name):
    ks, m, tk = a.shape
    d = b.shape[3]

    def body(a_ref, b_ref, hp_ref, g_ref, be_ref, r_ref, h_ref):
        p = _dot(a_ref[...].astype(BF16), b_ref[...])

        @pl.when(pl.program_id(1) == 0)
        def _():
            r_ref[...] = p

        @pl.when(pl.program_id(1) > 0)
        def _():
            r_ref[...] += p

        @pl.when(pl.program_id(1) == ks - 1)
        def _():
            r = ALPHA * hp_ref[...] + r_ref[...]
            r_ref[...] = r
            h_ref[...] = _layer_norm(r, g_ref[...], be_ref[...])

    row = pl.BlockSpec((None, tm, d), lambda i, k: (0, i, 0))
    vec = pl.BlockSpec((1, d), lambda i, k: (0, 0))
    return pl.pallas_call(
        body,
        name=name,
        grid=(m // tm, ks),
        in_specs=[
            pl.BlockSpec((None, tm, tk), lambda i, k: (k, i, 0)),
            pl.BlockSpec((None, None, tk, d), lambda i, k: (k, 0, 0, 0)),
            row, vec, vec,
        ],
        out_specs=[row, row],
        out_shape=[jax.ShapeDtypeStruct((1, m, d), F32)] * 2,
        compiler_params=_params(("arbitrary", "arbitrary")),
    )(a, b, h_prev, g, beta)


def mm_nt_ln_bwd(dy, w, res, r, g, *, tm, name, swap=()):
    ns, m, tn = dy.shape
    d = w.shape[2]
    n_swap = len(swap)
    last_tile = m // tm - 1

    def body(dy_ref, w_ref, res_ref, r_ref, g_ref, *refs):
        leaving_refs, (dr_ref, dgb_ref), refs = refs[:n_swap], refs[n_swap:n_swap + 2], refs[n_swap + 2:]
        copies = _swap_copies(leaving_refs, refs[:n_swap], *refs[n_swap:]) if n_swap else None
        p = _dot_nt(dy_ref[...].astype(BF16), w_ref[...])

        @pl.when((pl.program_id(0) == 0) & (pl.program_id(1) == 0))
        def _():
            dgb_ref[...] = jnp.zeros_like(dgb_ref)
            if n_swap:
                _swap_start(copies)

        @pl.when(pl.program_id(1) == 0)
        def _():
            dr_ref[...] = p + ALPHA * res_ref[...]

        @pl.when(pl.program_id(1) > 0)
        def _():
            dr_ref[...] += p

        @pl.when(pl.program_id(1) == ns - 1)
        def _():
            dr, dgamma, dbeta = _layer_norm_bwd(r_ref[...], dr_ref[...], g_ref[...])
            dr_ref[...] = dr
            dgb_ref[0] += dgamma
            dgb_ref[1] += dbeta

        if n_swap:
            @pl.when((pl.program_id(0) == last_tile) & (pl.program_id(1) == ns - 1))
            def _():
                _swap_finish(copies)

    row = pl.BlockSpec((None, tm, d), lambda i, n: (0, i, 0))
    dr, dgb, *landed = pl.pallas_call(
        body,
        name=name,
        grid=(m // tm, ns),
        in_specs=[
            pl.BlockSpec((None, tm, tn), lambda i, n: (n, i, 0)),
            pl.BlockSpec((None, None, d, tn), lambda i, n: (0, n, 0, 0)),
            row, row,
            pl.BlockSpec((1, d), lambda i, n: (0, 0)),
        ] + [ANY] * n_swap,
        out_specs=[row, pl.BlockSpec((2, 8, d), lambda i, n: (0, 0, 0))] + [ANY] * n_swap,
        out_shape=[jax.ShapeDtypeStruct((1, m, d), F32), jax.ShapeDtypeStruct((2, 8, d), F32)] + _swap_out_shapes(swap),
        scratch_shapes=_swap_sems(n_swap) if n_swap else [],
        compiler_params=_params(("arbitrary", "arbitrary")),
    )(dy, w, res, r, g, *swap)
    return dr, dgb, landed


def ln_bwd(r, dh, g, *, tm, name):
    _, lp, d = r.shape

    def body(r_ref, dh_ref, g_ref, dr_ref, dgb_ref):
        dr, dgamma, dbeta = _layer_norm_bwd(r_ref[...], dh_ref[...], g_ref[...])
        dr_ref[...] = dr

        @pl.when(pl.program_id(0) == 0)
        def _():
            dgb_ref[...] = jnp.zeros_like(dgb_ref)

        dgb_ref[0] += dgamma
        dgb_ref[1] += dbeta

    row = pl.BlockSpec((None, tm, d), lambda i: (0, i, 0))
    return pl.pallas_call(
        body,
        name=name,
        grid=(lp // tm,),
        in_specs=[row, row, pl.BlockSpec((1, d), lambda i: (0, 0))],
        out_specs=[row, pl.BlockSpec((2, 8, d), lambda i: (0, 0, 0))],
        out_shape=[jax.ShapeDtypeStruct((1, lp, d), F32), jax.ShapeDtypeStruct((2, 8, d), F32)],
        compiler_params=_params(("arbitrary",)),
    )(r, dh, g)


def loss_grad(h, target, *, first, count, tm):
    _, lp, d = h.shape

    def body(h_ref, t_ref, dh_ref, l_ref):
        row = pl.program_id(0) * tm + lax.broadcasted_iota(jnp.int32, (tm, d), 0)
        valid = (row >= first) & (row < first + count)
        err = jnp.where(valid, h_ref[...] - t_ref[...], 0.0)
        dh_ref[...] = err * (1.0 / d)

        @pl.when(pl.program_id(0) == 0)
        def _():
            l_ref[...] = jnp.zeros_like(l_ref)

        l_ref[...] += _row_partial(err * err) * (0.5 / d)

    return pl.pallas_call(
        body,
        name="loss_grad",
        grid=(lp // tm,),
        in_specs=[pl.BlockSpec((None, tm, d), lambda i: (0, i, 0)), pl.BlockSpec((tm, d), lambda i: (i, 0))],
        out_specs=[pl.BlockSpec((None, tm, d), lambda i: (0, i, 0)), pl.BlockSpec((8, d), lambda i: (0, 0))],
        out_shape=[jax.ShapeDtypeStruct((1, lp, d), F32), jax.ShapeDtypeStruct((8, d), F32)],
        compiler_params=_params(("arbitrary",)),
    )(h, target)


def _halo_index(tile, tm):
    return jnp.maximum(tile * (tm // HALO) - 1, 0)


def _conv_fwd(xs_ref, w, taps, tm):
    acc = w(0) * xs_ref[pl.ds(HALO - taps + 1, tm), :]
    for j in range(1, taps):
        acc += w(j) * xs_ref[pl.ds(HALO - taps + 1 + j, tm), :]
    return acc


def _conv_bwd_x(dcs_ref, w, taps, tm):
    acc = w(0) * dcs_ref[pl.ds(taps - 1, tm), :]
    for j in range(1, taps):
        acc += w(j) * dcs_ref[pl.ds(taps - 1 - j, tm), :]
    return acc


SUB = 8
LANES = 128
PAIR = 2 * SUB
STRIP_UNROLL = 2


def _pair_rows(r0):
    return pl.ds(r0, SUB), pl.ds(r0 + SUB if isinstance(r0, int) else pl.multiple_of(r0 + SUB, SUB), SUB)


def _shift_down(cur, prev, s):
    if s == 0:
        return cur
    row = lax.broadcasted_iota(jnp.int32, cur.shape, 0)
    return jnp.where(row < s, pltpu.roll(prev, s, axis=0), pltpu.roll(cur, s, axis=0))


def _shift_up(cur, nxt, s):
    if s == 0:
        return cur
    row = lax.broadcasted_iota(jnp.int32, cur.shape, 0)
    return jnp.where(row < SUB - s, pltpu.roll(cur, SUB - s, axis=0), pltpu.roll(nxt, SUB - s, axis=0))


def _silu_parts(c):
    sg = _sigmoid(c)
    return c * sg, sg * (1.0 + c * (1.0 - sg))


def _head_sum(x):
    rows, c = x.shape
    parts = []
    for h in range(c // HEAD_DIM):
        s = jnp.sum(x[:, h * HEAD_DIM:(h + 1) * HEAD_DIM], axis=-1, keepdims=True)
        parts.append(jnp.broadcast_to(s, (rows, HEAD_DIM)))
    return parts[0] if len(parts) == 1 else jnp.concatenate(parts, axis=-1)


def _log1p(y):
    u = 1.0 + y
    d = u - 1.0
    return jnp.where(d == 0.0, y, jnp.log(u) * (y / jnp.where(d == 0.0, 1.0, d)))


def _softplus(x):
    return jnp.maximum(x, 0.0) + _log1p(jnp.exp(-jnp.abs(x)))


def _gate_values(x, al, dt):
    lane = lax.broadcasted_iota(jnp.int32, x.shape, 1)
    is_beta, is_g = lane < HEADS, (lane >= HEADS) & (lane < 2 * HEADS)
    return _sigmoid(x), -jnp.exp(al) * _softplus(x + dt), is_beta, is_g


def gdn_gates_fwd(pba, al, dt, *, tm):
    _, lp, width = pba.shape

    def body(x_ref, al_ref, dt_ref, o_ref):
        beta, g, is_beta, is_g = _gate_values(x_ref[...], al_ref[...], dt_ref[...])
        o_ref[...] = jnp.where(is_beta, beta, jnp.where(is_g, g, 0.0))

    vec = pl.BlockSpec((1, width), lambda i: (0, 0))
    return pl.pallas_call(
        body,
        name="gdn_gates_fwd",
        grid=(lp // tm,),
        in_specs=[pl.BlockSpec((None, tm, width), lambda i: (0, i, 0)), vec, vec],
        out_specs=pl.BlockSpec((tm, width), lambda i: (i, 0)),
        out_shape=jax.ShapeDtypeStruct((lp, width), F32),
        compiler_params=_params(("arbitrary",)),
    )(pba, al, dt)


def gdn_gates_bwd(pba, dgates, al, dt, *, tm):
    _, lp, width = pba.shape

    def body(x_ref, d_ref, al_ref, dt_ref, dx_ref, dsc_ref):
        x = x_ref[...]
        beta, g, is_beta, is_g = _gate_values(x, al_ref[...], dt_ref[...])
        d = d_ref[...]
        dg = jnp.where(is_g, d, 0.0)
        da = dg * -jnp.exp(al_ref[...]) * _sigmoid(x + dt_ref[...])
        dx_ref[...] = jnp.where(is_beta, d * beta * (1.0 - beta), da).astype(dx_ref.dtype)

        @pl.when(pl.program_id(0) == 0)
        def _():
            dsc_ref[...] = jnp.zeros_like(dsc_ref)

        dsc_ref[0] += _row_partial(dg * g)
        dsc_ref[1] += _row_partial(da)

    vec = pl.BlockSpec((1, width), lambda i: (0, 0))
    return pl.pallas_call(
        body,
        name="gdn_gates_bwd",
        grid=(lp // tm,),
        in_specs=[pl.BlockSpec((None, tm, width), lambda i: (0, i, 0)), pl.BlockSpec((tm, width), lambda i: (i, 0)), vec, vec],
        out_specs=[pl.BlockSpec((None, tm, width), lambda i: (0, i, 0)), pl.BlockSpec((2, SUB, width), lambda i: (0, 0, 0))],
        out_shape=[jax.ShapeDtypeStruct((1, lp, width), BF16), jax.ShapeDtypeStruct((2, SUB, width), F32)],
        compiler_params=_params(("arbitrary",)),
    )(pba, dgates, al, dt)


def gdn_pre_fwd(p3, conv_w, *, tm, cb):
    _, lp, width = p3.shape
    taps = conv_w.shape[1]

    def body(x_ref, halo_ref, w_ref, o_ref, xs):
        i = pl.program_id(1)
        for s in range(3):
            xs[s, 0:HALO, :] = jnp.where(i > 0, halo_ref[s], 0.0)
            xs[s, HALO:, :] = x_ref[s]
            c = _conv_fwd(xs.at[s], lambda j, s=s: w_ref[s, j:j + 1, :], taps, tm)
            y, _ = _silu_parts(c)
            if s < 2:
                y = y * lax.rsqrt(_head_sum(y * y) + L2_EPS)
                if s == 0:
                    y = y * Q_SCALE
            o_ref[s] = y

    return pl.pallas_call(
        body,
        name="gdn_pre_fwd",
        grid=(width // cb, lp // tm),
        in_specs=[
            pl.BlockSpec((3, tm, cb), lambda j, i: (0, i, j)),
            pl.BlockSpec((3, HALO, cb), lambda j, i: (0, _halo_index(i, tm), j)),
            pl.BlockSpec((3, taps, cb), lambda j, i: (0, 0, j)),
        ],
        out_specs=pl.BlockSpec((3, tm, cb), lambda j, i: (0, i, j)),
        out_shape=jax.ShapeDtypeStruct((3, lp, width), F32),
        scratch_shapes=[pltpu.VMEM((3, tm + HALO, cb), F32)],
        compiler_params=_params(("arbitrary", "arbitrary")),
    )(p3, p3, conv_w)


def gdn_pre_bwd(p3, dqkv, conv_w, *, tm, cb):
    _, lp, width = p3.shape
    taps = conv_w.shape[1]
    last = lp // tm - 1

    def body(x_ref, halo_ref, d_ref, w_ref, dx_ref, dw_ref, xs, dcs, carry):
        step = pl.program_id(1)
        tile = last - step

        @pl.when(step == 0)
        def _():
            carry[...] = jnp.zeros_like(carry)
            dw_ref[...] = jnp.zeros_like(dw_ref)

        for s in range(3):
            w = lambda j, s=s: w_ref[s, j:j + 1, :]
            xs[s, 0:HALO, :] = jnp.where(tile > 0, halo_ref[s], 0.0)
            xs[s, HALO:, :] = x_ref[s]
            c = _conv_fwd(xs.at[s], w, taps, tm)
            y, dsilu = _silu_parts(c)
            dy = d_ref[s]
            if s < 2:
                rn = lax.rsqrt(_head_sum(y * y) + L2_EPS)
                yn = y * rn
                if s == 0:
                    dy = dy * Q_SCALE
                dy = rn * (dy - yn * _head_sum(dy * yn))
            dc = dy * dsilu
            dcs[s, 0:tm, :] = dc
            dcs[s, tm:, :] = carry[s]
            dx_ref[s] = _conv_bwd_x(dcs.at[s], w, taps, tm).astype(dx_ref.dtype)
            carry[s] = dc[0:HALO, :]
            for j in range(taps):
                dw_ref[s, j] += _row_partial(dc * xs[s, pl.ds(HALO - taps + 1 + j, tm), :])

    tile_spec = pl.BlockSpec((3, tm, cb), lambda j, i: (0, last - i, j))
    return pl.pallas_call(
        body,
        name="gdn_pre_bwd",
        grid=(width // cb, lp // tm),
        in_specs=[
            tile_spec,
            pl.BlockSpec((3, HALO, cb), lambda j, i: (0, _halo_index(last - i, tm), j)),
            tile_spec,
            pl.BlockSpec((3, taps, cb), lambda j, i: (0, 0, j)),
        ],
        out_specs=[tile_spec, pl.BlockSpec((3, taps, SUB, cb), lambda j, i: (0, 0, 0, j))],
        out_shape=[jax.ShapeDtypeStruct((3, lp, width), BF16), jax.ShapeDtypeStruct((3, taps, SUB, width), F32)],
        scratch_shapes=[
            pltpu.VMEM((3, tm + HALO, cb), F32),
            pltpu.VMEM((3, tm + HALO, cb), F32),
            pltpu.VMEM((3, HALO, cb), F32),
        ],
        compiler_params=_params(("arbitrary", "arbitrary")),
    )(p3, p3, dqkv, conv_w)


def gdn_post_fwd(o, z, nw_b, *, tm):
    _, lp, width = o.shape

    def body(o_ref, z_ref, nw_ref, y_ref):
        ov = o_ref[...]
        rn = lax.rsqrt(_head_sum(ov * ov) * (1.0 / HEAD_DIM) + RMS_EPS)
        gate, _ = _silu_parts(z_ref[...])
        y_ref[...] = (ov * rn * nw_ref[...] * gate).astype(y_ref.dtype)

    row = pl.BlockSpec((None, tm, width), lambda i: (0, i, 0))
    return pl.pallas_call(
        body,
        name="gdn_post_fwd",
        grid=(lp // tm,),
        in_specs=[row, row, pl.BlockSpec((1, width), lambda i: (0, 0))],
        out_specs=row,
        out_shape=jax.ShapeDtypeStruct((1, lp, width), BF16),
        compiler_params=_params(("arbitrary",)),
    )(o, z, nw_b)


def gdn_post_bwd(o, z, dy, nw_b, *, tm):
    _, lp, width = o.shape

    def body(o_ref, z_ref, dy_ref, nw_ref, do_ref, dz_ref, dnw_ref):
        ov = o_ref[...]
        rn = lax.rsqrt(_head_sum(ov * ov) * (1.0 / HEAD_DIM) + RMS_EPS)
        yn = ov * rn
        gate, dgate = _silu_parts(z_ref[...])
        d_on = dy_ref[...] * gate
        dz_ref[...] = (dy_ref[...] * yn * nw_ref[...] * dgate).astype(dz_ref.dtype)
        a = d_on * nw_ref[...]
        do_ref[...] = rn * (a - yn * (_head_sum(a * yn) * (1.0 / HEAD_DIM)))

        @pl.when(pl.program_id(0) == 0)
        def _():
            dnw_ref[...] = jnp.zeros_like(dnw_ref)

        dnw_ref[...] += _row_partial(d_on * yn)

    row = pl.BlockSpec((None, tm, width), lambda i: (0, i, 0))
    return pl.pallas_call(
        body,
        name="gdn_post_bwd",
        grid=(lp // tm,),
        in_specs=[row, row, row, pl.BlockSpec((1, width), lambda i: (0, 0))],
        out_specs=[row, row, pl.BlockSpec((8, width), lambda i: (0, 0))],
        out_shape=[jax.ShapeDtypeStruct((1, lp, width), F32), jax.ShapeDtypeStruct((1, lp, width), BF16),
                   jax.ShapeDtypeStruct((8, width), F32)],
        compiler_params=_params(("arbitrary",)),
    )(o, z, dy, nw_b)


def ffn_act_fwd(up, conv_w, *, tm, name):
    _, lp, c_w = up.shape
    taps = conv_w.shape[1]

    def body(u_ref, halo_ref, g_ref, w_ref, o_ref):
        first_tile = pl.program_id(1) == 0

        def strip(cur, prev, rows, cs):
            conv = w_ref[taps - 1:taps, cs] * cur
            for j in range(taps - 1):
                conv += w_ref[j:j + 1, cs] * _shift_down(cur, prev, taps - 1 - j)
            y, _ = _silu_parts(conv)
            return y * g_ref[rows, cs]

        def pair(r0, above_of):
            top, bot = _pair_rows(r0)
            for c0 in range(0, c_w, LANES):
                cs = slice(c0, c0 + LANES)
                cur_t, cur_b = u_ref[top, cs], u_ref[bot, cs]
                out = [strip(cur_t, above_of(cs), top, cs), strip(cur_b, cur_t, bot, cs)]
                o_ref[pl.ds(r0, PAIR), cs] = jnp.concatenate(out, axis=0).astype(o_ref.dtype)

        pair(0, lambda cs: jnp.where(first_tile, 0.0, halo_ref[:, cs]))

        def loop_body(s, carry):
            r0 = pl.multiple_of(s * PAIR, PAIR)
            pair(r0, lambda cs: u_ref[pl.ds(pl.multiple_of(r0 - SUB, SUB), SUB), cs])
            return carry

        lax.fori_loop(1, tm // PAIR, loop_body, 0, unroll=STRIP_UNROLL)

    return pl.pallas_call(
        body,
        name=name,
        grid=(2, lp // tm),
        in_specs=[
            pl.BlockSpec((None, tm, c_w), lambda s, i: (s, i, 0)),
            pl.BlockSpec((None, HALO, c_w), lambda s, i: (s, _halo_index(i, tm), 0)),
            pl.BlockSpec((None, tm, c_w), lambda s, i: (2 + s, i, 0)),
            pl.BlockSpec((None, taps, c_w), lambda s, i: (s, 0, 0)),
        ],
        out_specs=pl.BlockSpec((None, tm, c_w), lambda s, i: (s, i, 0)),
        out_shape=jax.ShapeDtypeStruct((2, lp, c_w), BF16),
        compiler_params=_params(("arbitrary", "arbitrary")),
    )(up, up, up, conv_w)


def ffn_act_bwd(up, dact, conv_w, *, tm, name):
    _, lp, c_w = up.shape
    taps = conv_w.shape[1]
    last = lp // tm - 1
    n_pairs = tm // PAIR

    def body(u_ref, halo_ref, g_ref, d_ref, w_ref, dup_ref, dw_ref, below):
        step = pl.program_id(1)
        first_tile = step == last

        @pl.when(step == 0)
        def _():
            below[...] = jnp.zeros_like(below)
            dw_ref[...] = jnp.zeros_like(dw_ref)

        def strip(cur, prev, rows, cs, nxt):
            shifted = [_shift_down(cur, prev, taps - 1 - j) for j in range(taps)]
            conv = w_ref[0:1, cs] * shifted[0]
            for j in range(1, taps):
                conv += w_ref[j:j + 1, cs] * shifted[j]
            y, dsilu = _silu_parts(conv)
            d = d_ref[rows, cs]
            dc = d * g_ref[rows, cs] * dsilu
            dx = w_ref[taps - 1:taps, cs] * dc
            for j in range(taps - 1):
                dx += w_ref[j:j + 1, cs] * _shift_up(dc, nxt, taps - 1 - j)
            return dx, d * y, dc, [dc * s for s in shifted]

        def pair(r0, above_of):
            top, bot = _pair_rows(r0)
            both = pl.ds(r0, PAIR)
            for c0 in range(0, c_w, LANES):
                cs = slice(c0, c0 + LANES)
                cur_t, cur_b = u_ref[top, cs], u_ref[bot, cs]
                dx_b, dg_b, dc_b, dw_b = strip(cur_b, cur_t, bot, cs, below[:, cs])
                dx_t, dg_t, dc_t, dw_t = strip(cur_t, above_of(cs), top, cs, dc_b)
                below[:, cs] = dc_t
                dup_ref[0, both, cs] = jnp.concatenate([dx_t, dx_b], axis=0).astype(dup_ref.dtype)
                dup_ref[1, both, cs] = jnp.concatenate([dg_t, dg_b], axis=0).astype(dup_ref.dtype)
                for j in range(taps):
                    dw_ref[j, :, cs] += dw_t[j] + dw_b[j]

        def loop_body(it, carry):
            r0 = pl.multiple_of((n_pairs - 1 - it) * PAIR, PAIR)
            pair(r0, lambda cs: u_ref[pl.ds(pl.multiple_of(r0 - SUB, SUB), SUB), cs])
            return carry

        lax.fori_loop(0, n_pairs - 1, loop_body, 0, unroll=STRIP_UNROLL)
        pair(0, lambda cs: jnp.where(first_tile, 0.0, halo_ref[:, cs]))

    return pl.pallas_call(
        body,
        name=name,
        grid=(2, lp // tm),
        in_specs=[
            pl.BlockSpec((None, tm, c_w), lambda s, i: (s, last - i, 0)),
            pl.BlockSpec((None, HALO, c_w), lambda s, i: (s, _halo_index(last - i, tm), 0)),
            pl.BlockSpec((None, tm, c_w), lambda s, i: (2 + s, last - i, 0)),
            pl.BlockSpec((None, tm, c_w), lambda s, i: (s, last - i, 0)),
            pl.BlockSpec((None, taps, c_w), lambda s, i: (s, 0, 0)),
        ],
        out_specs=[
            pl.BlockSpec((2, None, tm, c_w), lambda s, i: (0, s, last - i, 0)),
            pl.BlockSpec((None, taps, SUB, c_w), lambda s, i: (s, 0, 0, 0)),
        ],
        out_shape=[jax.ShapeDtypeStruct((2, 2, lp, c_w), BF16), jax.ShapeDtypeStruct((2, taps, SUB, c_w), F32)],
        scratch_shapes=[pltpu.VMEM((SUB, c_w), F32)],
        compiler_params=_params(("arbitrary", "arbitrary")),
    )(up, up, up, dact, conv_w)


def sc_fwd(pb, conv_w, *, tm, cb):
    _, lp, width = pb.shape
    taps = conv_w.shape[0]

    def body(x_ref, halo_ref, w_ref, o_ref):
        first_tile = pl.program_id(1) == 0

        def strip(cur, prev, rows, cs):
            conv = w_ref[taps - 1:taps, cs] * cur
            for j in range(taps - 1):
                conv += w_ref[j:j + 1, cs] * _shift_down(cur, prev, taps - 1 - j)
            return x_ref[0, rows, cs] * conv

        def pair(r0, above_of):
            top, bot = _pair_rows(r0)
            for c0 in range(0, cb, LANES):
                cs = slice(c0, c0 + LANES)
                cur_t = x_ref[1, top, cs] * x_ref[2, top, cs]
                cur_b = x_ref[1, bot, cs] * x_ref[2, bot, cs]
                out = [strip(cur_t, above_of(cs), top, cs), strip(cur_b, cur_t, bot, cs)]
                o_ref[pl.ds(r0, PAIR), cs] = jnp.concatenate(out, axis=0).astype(o_ref.dtype)

        pair(0, lambda cs: jnp.where(first_tile, 0.0, halo_ref[1, :, cs] * halo_ref[2, :, cs]))

        def loop_body(k, carry):
            r0 = pl.multiple_of(k * PAIR, PAIR)
            before = pl.ds(pl.multiple_of(r0 - SUB, SUB), SUB)
            pair(r0, lambda cs: x_ref[1, before, cs] * x_ref[2, before, cs])
            return carry

        lax.fori_loop(1, tm // PAIR, loop_body, 0, unroll=STRIP_UNROLL)

    return pl.pallas_call(
        body,
        name="sc_fwd",
        grid=(width // cb, lp // tm),
        in_specs=[
            pl.BlockSpec((3, tm, cb), lambda j, i: (0, i, j)),
            pl.BlockSpec((3, HALO, cb), lambda j, i: (0, _halo_index(i, tm), j)),
            pl.BlockSpec((taps, cb), lambda j, i: (0, j)),
        ],
        out_specs=pl.BlockSpec((None, tm, cb), lambda j, i: (0, i, j)),
        out_shape=jax.ShapeDtypeStruct((1, lp, width), BF16),
        compiler_params=_params(("arbitrary", "arbitrary")),
    )(pb, pb, conv_w)


def sc_bwd(pb, ds, conv_w, *, tm, cb):
    _, lp, width = pb.shape
    taps = conv_w.shape[0]
    last = lp // tm - 1
    n_pairs = tm // PAIR

    def body(x_ref, halo_ref, d_ref, w_ref, dx_ref, dw_ref, below):
        step = pl.program_id(1)
        first_tile = step == last

        @pl.when(step == 0)
        def _():
            below[...] = jnp.zeros_like(below)
            dw_ref[...] = jnp.zeros_like(dw_ref)

        def strip(cur, prev, rows, cs, nxt):
            gate, left, right = x_ref[0, rows, cs], x_ref[1, rows, cs], x_ref[2, rows, cs]
            shifted = [_shift_down(cur, prev, taps - 1 - j) for j in range(taps)]
            conv = w_ref[0:1, cs] * shifted[0]
            for j in range(1, taps):
                conv += w_ref[j:j + 1, cs] * shifted[j]
            d = d_ref[rows, cs]
            dc = d * gate
            dp = w_ref[taps - 1:taps, cs] * dc
            for j in range(taps - 1):
                dp += w_ref[j:j + 1, cs] * _shift_up(dc, nxt, taps - 1 - j)
            return d * conv, dp * right, dp * left, dc, [dc * s for s in shifted]

        def pair(r0, above_of):
            top, bot = _pair_rows(r0)
            both = pl.ds(r0, PAIR)
            for c0 in range(0, cb, LANES):
                cs = slice(c0, c0 + LANES)
                cur_t = x_ref[1, top, cs] * x_ref[2, top, cs]
                cur_b = x_ref[1, bot, cs] * x_ref[2, bot, cs]
                *dx_b, dc_b, dw_b = strip(cur_b, cur_t, bot, cs, below[:, cs])
                *dx_t, dc_t, dw_t = strip(cur_t, above_of(cs), top, cs, dc_b)
                below[:, cs] = dc_t
                for s in range(3):
                    dx_ref[s, both, cs] = jnp.concatenate([dx_t[s], dx_b[s]], axis=0).astype(dx_ref.dtype)
                for j in range(taps):
                    dw_ref[j, :, cs] += dw_t[j] + dw_b[j]

        def loop_body(it, carry):
            r0 = pl.multiple_of((n_pairs - 1 - it) * PAIR, PAIR)
            before = pl.ds(pl.multiple_of(r0 - SUB, SUB), SUB)
            pair(r0, lambda cs: x_ref[1, before, cs] * x_ref[2, before, cs])
            return carry

        lax.fori_loop(0, n_pairs - 1, loop_body, 0, unroll=STRIP_UNROLL)
        pair(0, lambda cs: jnp.where(first_tile, 0.0, halo_ref[1, :, cs] * halo_ref[2, :, cs]))

    tile_spec = pl.BlockSpec((3, tm, cb), lambda j, i: (0, last - i, j))
    return pl.pallas_call(
        body,
        name="sc_bwd",
        grid=(width // cb, lp // tm),
        in_specs=[
            tile_spec,
            pl.BlockSpec((3, HALO, cb), lambda j, i: (0, _halo_index(last - i, tm), j)),
            pl.BlockSpec((None, tm, cb), lambda j, i: (0, last - i, j)),
            pl.BlockSpec((taps, cb), lambda j, i: (0, j)),
        ],
        out_specs=[tile_spec, pl.BlockSpec((taps, SUB, cb), lambda j, i: (0, 0, j))],
        out_shape=[jax.ShapeDtypeStruct((3, lp, width), BF16), jax.ShapeDtypeStruct((taps, SUB, width), F32)],
        scratch_shapes=[pltpu.VMEM((SUB, cb), F32)],
        compiler_params=_params(("arbitrary", "arbitrary")),
    )(pb, pb, ds, conv_w)


TILE_BYTES = 1536 * 1024


def _rows_tile(rows, cols, multiple=8):
    if rows * cols * 4 <= TILE_BYTES or rows % multiple:
        return rows
    best = multiple
    for t in range(multiple, rows + 1, multiple):
        if rows % t == 0 and t * cols * 4 <= TILE_BYTES:
            best = t
    return best


def pair_sum(g, landed, core, out_dtype, name):
    _, rows, cols = g.shape
    half = rows // 2
    tr = _rows_tile(half, cols, 16)
    nb = half // tr

    def body(c_ref, g_ref, l_ref, o_ref):
        o_ref[...] = (g_ref[...] + l_ref[...]).astype(out_dtype)

    return pl.pallas_call(
        body,
        name=name,
        grid_spec=pltpu.PrefetchScalarGridSpec(
            num_scalar_prefetch=1,
            grid=(4, nb),
            in_specs=[
                pl.BlockSpec((None, tr, cols), lambda s, i, c: (s, c[0] * nb + i, 0)),
                pl.BlockSpec((None, tr, cols), lambda s, i, c: (s, i, 0)),
            ],
            out_specs=pl.BlockSpec((None, tr, cols), lambda s, i, c: (s, i, 0)),
        ),
        out_shape=jax.ShapeDtypeStruct((4, half, cols), out_dtype),
        compiler_params=_params(("arbitrary", "arbitrary")),
    )(core, g, landed)


def chip_sum(x, name):
    _, rows, cols = x.shape
    tr = _rows_tile(rows, cols, 16)

    def body(x0, x1, x2, x3, o_ref):
        acc = x0[...].astype(F32) + x1[...].astype(F32)
        o_ref[...] = (acc + x2[...].astype(F32)) + x3[...].astype(F32)

    return pl.pallas_call(
        body,
        name=name,
        grid=(rows // tr,),
        in_specs=[pl.BlockSpec((None, tr, cols), lambda i, k=k: (k, i, 0)) for k in range(4)],
        out_specs=pl.BlockSpec((tr, cols), lambda i: (i, 0)),
        out_shape=jax.ShapeDtypeStruct((rows, cols), F32),
        compiler_params=_params(("arbitrary",)),
    )(x, x, x, x)


def adamw(w, g, m, v, name):
    shape = w.shape
    cols = shape[-1]
    rows = w.size // cols
    tr = _rows_tile(rows, cols)

    def body(w_ref, g_ref, m_ref, v_ref, d_ref, m2_ref, v2_ref):
        gv = g_ref[...]
        m2 = ADAM_B1 * m_ref[...] + (1.0 - ADAM_B1) * gv
        v2 = ADAM_B2 * v_ref[...] + (1.0 - ADAM_B2) * (gv * gv)
        m_hat = m2 / (1.0 - ADAM_B1 ** ADAM_STEP)
        v_hat = v2 / (1.0 - ADAM_B2 ** ADAM_STEP)
        d_ref[...] = -ADAM_LR * (m_hat / (jnp.sqrt(v_hat) + ADAM_EPS) + ADAM_WD * w_ref[...])
        m2_ref[...] = m2
        v2_ref[...] = v2

    spec = pl.BlockSpec((tr, cols), lambda i: (i, 0))
    outs = pl.pallas_call(
        body,
        name=name,
        grid=(rows // tr,),
        in_specs=[spec] * 4,
        out_specs=[spec] * 3,
        out_shape=[jax.ShapeDtypeStruct((rows, cols), F32)] * 3,
        compiler_params=_params(("arbitrary",)),
    )(*[t.reshape(rows, cols) for t in (w, g, m, v)])
    return tuple(o.reshape(shape) for o in outs)


MESH_ID = pl.DeviceIdType.MESH
ANY = pl.BlockSpec(memory_space=pl.ANY)


def _place():
    x, y, c = lax.axis_index("x"), lax.axis_index("y"), lax.axis_index("c")
    other_chips = [(1 - x, y), (x, 1 - y), (1 - x, 1 - y)]
    return x, y, c, other_chips


def all_gather_shards(bufs, name):
    n = len(bufs)

    def body(*refs):
        x_refs, o_refs = refs[:n], refs[n:2 * n]
        copies = _gather_copies(x_refs, o_refs, *refs[2 * n:])
        _gather_start(copies)
        _gather_finish(copies)

    outs = pl.pallas_call(
        body,
        name=name,
        in_specs=[ANY] * n,
        out_specs=[ANY] * n,
        out_shape=_gather_out_shapes(bufs),
        scratch_shapes=_gather_sems(n),
    )(*bufs)
    return _set_own_slots(outs, bufs)


def _gather_out_shapes(bufs):
    return [jax.ShapeDtypeStruct((4,) + b.shape, b.dtype) for b in bufs]


def _gather_sems(n):
    return [pltpu.SemaphoreType.DMA((6 * n,)), pltpu.SemaphoreType.DMA((6 * n,))]


def _set_own_slots(outs, bufs):
    if not outs:
        return []
    me = 2 * lax.axis_index("x") + lax.axis_index("y")
    return [lax.dynamic_update_index_in_dim(o, b, me, 0) for o, b in zip(outs, bufs)]


def _gather_copies(x_refs, o_refs, send_sems, recv_sems):
    x, y, c, chips = _place()
    me = 2 * x + y
    sibling = (x, y, 1 - c)

    def part(a, slot, hf):
        half = x_refs[a].shape[0] // 2
        return o_refs[a].at[slot, pl.ds(hf * half, half), :]

    def mine(a):
        half = x_refs[a].shape[0] // 2
        return x_refs[a].at[pl.ds(c * half, half), :]

    def copy(k, src, dst, to):
        return pltpu.make_async_remote_copy(src_ref=src, dst_ref=dst, send_sem=send_sems.at[k],
                                            recv_sem=recv_sems.at[k], device_id=to, device_id_type=MESH_ID)

    sends, arrivals, passes, passed = [], [], [], []
    for a in range(len(x_refs)):
        for j, (px, py) in enumerate(chips):
            landed, theirs = part(a, 2 * px + py, c), part(a, 2 * px + py, 1 - c)
            sends.append(copy(6 * a + j, mine(a), part(a, me, c), (px, py, c)))
            arrivals.append(copy(6 * a + j, mine(a), landed, (px, py, c)))
            passes.append(copy(6 * a + 3 + j, landed, landed, sibling))
            passed.append(copy(6 * a + 3 + j, theirs, theirs, sibling))
    return sends, arrivals, passes, passed


def _gather_start(copies):
    for cp in copies[0]:
        cp.start()


def _gather_finish(copies):
    sends, arrivals, passes, passed = copies
    for arrival, cp in zip(arrivals, passes):
        arrival.wait_recv()
        cp.start()
    for cp in passed:
        cp.wait_recv()
    for cp in sends + passes:
        cp.wait_send()


def swap_halves(bufs, name):
    n = len(bufs)

    def body(*refs):
        copies = _swap_copies(refs[:n], refs[n:2 * n], *refs[2 * n:])
        _swap_start(copies)
        _swap_finish(copies)

    return pl.pallas_call(
        body,
        name=name,
        in_specs=[ANY] * n,
        out_specs=[ANY] * n,
        out_shape=_swap_out_shapes(bufs),
        scratch_shapes=_swap_sems(n),
    )(*bufs)


def _swap_out_shapes(bufs):
    return [jax.ShapeDtypeStruct((4, b.shape[1] // 2, b.shape[2]), b.dtype) for b in bufs]


def _swap_sems(n):
    return [pltpu.SemaphoreType.DMA((n,)), pltpu.SemaphoreType.DMA((n,))]


def _swap_copies(x_refs, o_refs, send_sems, recv_sems):
    x, y, c, _ = _place()
    copies = []
    for a, (x_ref, o_ref) in enumerate(zip(x_refs, o_refs)):
        half = x_ref.shape[1] // 2
        copies.append(pltpu.make_async_remote_copy(src_ref=x_ref.at[:, pl.ds((1 - c) * half, half), :], dst_ref=o_ref,
                                                   send_sem=send_sems.at[a], recv_sem=recv_sems.at[a],
                                                   device_id=(x, y, 1 - c), device_id_type=MESH_ID))
    return copies


def _swap_start(copies):
    for cp in copies:
        cp.start()


def _swap_finish(copies):
    for cp in copies:
        cp.wait()


def scatter_to_chips(bufs, name):
    n = len(bufs)

    def body(*refs):
        x_refs, o_refs = refs[:n], refs[n:2 * n]
        copies = _scatter_copies(x_refs, o_refs, *refs[2 * n:])
        _scatter_start(copies)
        _scatter_finish(copies)

    outs = pl.pallas_call(
        body,
        name=name,
        in_specs=[ANY] * n,
        out_specs=[ANY] * n,
        out_shape=[jax.ShapeDtypeStruct(b.shape, b.dtype) for b in bufs],
        scratch_shapes=_scatter_sems(n),
    )(*bufs)
    return _keep_own_slots(outs, bufs)


def _scatter_sems(n):
    return [pltpu.SemaphoreType.DMA((3 * n,)), pltpu.SemaphoreType.DMA((3 * n,))]


def _keep_own_slots(outs, bufs):
    if not outs:
        return []
    me = 2 * lax.axis_index("x") + lax.axis_index("y")
    return [lax.dynamic_update_index_in_dim(o, lax.dynamic_index_in_dim(b, me, 0, keepdims=False), me, 0)
            for o, b in zip(outs, bufs)]


def _scatter_copies(x_refs, o_refs, send_sems, recv_sems):
    x, y, c, chips = _place()
    me = 2 * x + y

    def copy(a, j, src_slot, dst_slot, px, py):
        return pltpu.make_async_remote_copy(src_ref=x_refs[a].at[src_slot], dst_ref=o_refs[a].at[dst_slot],
                                            send_sem=send_sems.at[3 * a + j], recv_sem=recv_sems.at[3 * a + j],
                                            device_id=(px, py, c), device_id_type=MESH_ID)

    sends = [copy(a, j, 2 * px + py, me, px, py) for a in range(len(x_refs)) for j, (px, py) in enumerate(chips)]
    arrivals = [copy(a, j, me, 2 * px + py, px, py) for a in range(len(x_refs)) for j, (px, py) in enumerate(chips)]
    return sends, arrivals


def _scatter_start(copies):
    for cp in copies[0]:
        cp.start()


def _scatter_finish(copies):
    for cp in copies[1]:
        cp.wait_recv()
    for cp in copies[0]:
        cp.wait_send()


def share_halves(groups, name):
    bufs = [b for grp in groups for b in grp]
    where = [(gi, li) for gi, grp in enumerate(groups) for li in range(len(grp))]
    n = len(bufs)

    def body(*refs):
        x_refs, o_refs = refs[:n], refs[n:n + len(groups)]
        send_sems, recv_sems = refs[n + len(groups):]
        x, y, c, _ = _place()
        sent, arrive = [], []
        for a, (gi, li) in enumerate(where):

            def copy(hf, a=a, gi=gi, li=li):
                return pltpu.make_async_remote_copy(src_ref=x_refs[a], dst_ref=o_refs[gi].at[li, hf],
                                                    send_sem=send_sems.at[a], recv_sem=recv_sems.at[a],
                                                    device_id=(x, y, 1 - c), device_id_type=MESH_ID)

            sent.append(copy(c))
            arrive.append(copy(1 - c))
        for cp in sent:
            cp.start()
        for cp in arrive:
            cp.wait_recv()
        for cp in sent:
            cp.wait_send()

    outs = pl.pallas_call(
        body,
        name=name,
        in_specs=[ANY] * n,
        out_specs=[ANY] * len(groups),
        out_shape=[jax.ShapeDtypeStruct((len(grp), 2) + grp[0].shape, grp[0].dtype) for grp in groups],
        scratch_shapes=[pltpu.SemaphoreType.DMA((n,)), pltpu.SemaphoreType.DMA((n,))],
    )(*bufs)
    c = lax.axis_index("c")
    full = [lax.dynamic_update_index_in_dim(o, jnp.stack(grp), c, 1) for o, grp in zip(outs, groups)]
    return [t.reshape(t.shape[0], 2 * t.shape[2], t.shape[3]) for t in full]


def pair_sums(bufs, landed, dtypes, tag):
    core = lax.axis_index("c").astype(jnp.int32).reshape(1)
    return [pair_sum(b, l, core, dt, "rs_pair_sum_%s%d" % (tag, i)) for i, (b, l, dt) in enumerate(zip(bufs, landed, dtypes))]


def _row_tiles(length):
    return (640, 320) if length > 2048 else (128, 64)


def _divisor_tile(rows, target):
    return max(t for t in range(8, min(rows, target) + 1, 8) if rows % t == 0)


def _local_step(x, target, wt, late_shards, layout_late, complete_grads, sum_pairs):
    seq, d = x.shape
    length = N_META + seq
    tm, tm_ffn = _row_tiles(length)
    lp = -(-length // tm) * tm
    tail = jnp.zeros((lp - length, d), F32)
    h0 = jnp.concatenate([wt["meta"], x, tail], axis=0)[None]
    tgt = jnp.concatenate([jnp.zeros((N_META, d), F32), target, tail], axis=0)
    nn = functools.partial(mm_nn, tm=_divisor_tile(lp, 1664))
    nt = functools.partial(mm_nt, tm=_divisor_tile(lp, 1040))
    tn = functools.partial(mm_tn, tm=_divisor_tile(lp, 1664), rb=256)
    nn_ln = functools.partial(mm_nn_ln, tm=_divisor_tile(lp, 832))
    nt_ln_bwd = functools.partial(mm_nt_ln_bwd, tm=_divisor_tile(lp, 832))
    ln_g = [wt["ln_mix_g"][0:1], wt["ln_ffn_g"][0:1], wt["ln_mix_g"][1:2], wt["ln_ffn_g"][1:2]]
    ln_b = [wt["ln_mix_b"][0:1], wt["ln_ffn_b"][0:1], wt["ln_mix_b"][1:2], wt["ln_ffn_b"][1:2]]

    p3 = nn(h0, wt["a3"], name="a_in3")
    pz = nn(h0, wt["az"], name="a_inz")
    pba = nn(h0, wt["a_ba"], name="a_inba")
    qkv = gdn_pre_fwd(p3, wt["a_conv3"], tm=tm, cb=2 * HEAD_DIM)
    gates = gdn_gates_fwd(pba, wt["alog_lanes"], wt["dtb_lanes"], tm=tm)
    o, states, tinv, late_stacks = gdn_chunk_fwd(qkv, gates, late_shards)
    wt = {**wt, **layout_late(late_stacks)}
    onz = gdn_post_fwd(o[None], pz, wt["anorm_b"], tm=tm)
    r1, h1 = nn_ln(onz, wt["a_out"], h0, ln_g[0], ln_b[0], name="a_out_ln1")
    up0 = nn(h1, wt["up"][0], name="up0")
    act0 = ffn_act_fwd(up0, wt["fconv"][0], tm=tm_ffn, name="ffn_act0")
    r2, h2 = nn_ln(act0, wt["down"][0], h1, ln_g[1], ln_b[1], name="down0_ln2")
    pb = nn(h2, wt["b_in"], name="b_in")
    sc = sc_fwd(pb, wt["b_conv"], tm=tm_ffn, cb=d)
    r3, h3 = nn_ln(sc, wt["b_out"], h2, ln_g[2], ln_b[2], name="b_out_ln3")
    up1 = nn(h3, wt["up"][1], name="up1")
    act1 = ffn_act_fwd(up1, wt["fconv"][1], tm=tm_ffn, name="ffn_act1")
    r4, h4 = nn_ln(act1, wt["down"][1], h3, ln_g[3], ln_b[3], name="down1_ln4")

    dh4, loss_part = loss_grad(h4, tgt, first=N_META, count=seq, tm=tm)

    grads = {}
    dr4, dgb4 = ln_bwd(r4, dh4, ln_g[3], tm=tm, name="ln4_bwd")
    d_down1 = tn(act1, dr4, name="d_down1")
    dact1 = nt(dr4, wt["down"][1], name="d_act1")
    dup1, dfconv1 = ffn_act_bwd(up1, dact1, wt["fconv"][1], tm=tm_ffn, name="ffn_act1_bwd")
    dup1 = dup1.reshape(up1.shape)
    d_up1 = tn(h3, dup1, name="d_up1")

    dr3, dgb3, _ = nt_ln_bwd(dup1, wt["up"][1], dr4, r3, ln_g[2], name="d_h3_ln3")
    d_bout = tn(sc, dr3, name="d_b_out")
    dsc = nt(dr3, wt["b_out"], name="d_sc")
    dpb, dbconv = sc_bwd(pb, dsc, wt["b_conv"], tm=tm_ffn, cb=d)
    d_bin = tn(h2, dpb, name="d_b_in")

    dr2, dgb2, _ = nt_ln_bwd(dpb, wt["b_in"], dr3, r2, ln_g[1], name="d_h2_ln2")
    d_down0 = tn(act0, dr2, name="d_down0")
    dact0 = nt(dr2, wt["down"][0], name="d_act0")
    dup0, dfconv0 = ffn_act_bwd(up0, dact0, wt["fconv"][0], tm=tm_ffn, name="ffn_act0_bwd")
    dup0 = dup0.reshape(up0.shape)
    d_up0 = tn(h1, dup0, name="d_up0")
    grads["b_w_in"] = [d_bin[0].transpose(1, 0, 2).reshape(d, 4, 3 * d // 4).transpose(1, 0, 2)]
    grads["b_w_out"] = [d_bout.reshape(4, d // 4, d)]
    grads["ffn_w_up"] = [d_up0[0], d_up1[0]]
    grads["ffn_w_down"] = [t.reshape(4, -1, d) for t in (d_down0, d_down1)]
    complete = complete_grads(grads)

    dr1, dgb1, from_sibling = nt_ln_bwd(dup0, wt["up"][0], dr2, r1, ln_g[0], name="d_h1_ln1", swap=complete)
    leaving = sum_pairs(complete, from_sibling)
    d_aout = tn(onz, dr1, name="d_a_out")
    donz = nt(dr1, wt["a_out"], name="d_onz")
    d_o, dz, dnw = gdn_post_bwd(o[None], pz, donz, wt["anorm_b"], tm=tm)
    dqkv, dgates, landed = gdn_chunk_bwd(qkv, gates, states, tinv, d_o[0], leaving)
    dp3, daconv = gdn_pre_bwd(p3, dqkv, wt["a_conv3"], tm=tm, cb=2 * HEAD_DIM)
    dpba, dscal = gdn_gates_bwd(pba, dgates, wt["alog_lanes"], wt["dtb_lanes"], tm=tm)
    d_a3 = tn(h0, dp3, name="d_a_in3")
    d_az = tn(h0, dz, name="d_a_inz")
    d_aba = tn(h0, dpba, name="d_a_inba")
    dh0 = nt(dp3, wt["a3"], res=dr1, res_scale=ALPHA, name="d_h0a")
    dh0 = nt(dz, wt["az"], res=dh0, res_scale=1.0, name="d_h0z")
    dh0 = nt(dpba, wt["a_ba"], res=dh0, res_scale=1.0, name="d_h0")

    width = HEADS * HEAD_DIM
    d_a_in = jnp.concatenate([d_a3[0, 0], d_a3[0, 1], d_a3[0, 2], d_az[0, 0], d_aba[0, 0][:, :2 * HEADS]], axis=1)
    n_in = d_a_in.shape[1] // 4
    grads["a_w_in"] = [d_a_in.reshape(d, 4, n_in).transpose(1, 0, 2)]
    grads["a_w_out"] = [d_aout.reshape(4, width // 4, d)]
    grads["a_conv"] = daconv.sum(axis=2).transpose(1, 0, 2).reshape(1, GDN_CONV, 3 * width)
    per_head = dscal.sum(axis=1)[:, HEADS:2 * HEADS]
    grads["a_log"] = per_head[0][None]
    grads["a_dt_bias"] = per_head[1][None]
    grads["a_norm"] = dnw.reshape(8, HEADS, HEAD_DIM).sum(axis=(0, 1))[None]
    grads["b_conv"] = dbconv.sum(axis=1)[None]
    lns = [dgb1, dgb2, dgb3, dgb4]
    grads["ln_mix_g"] = jnp.stack([lns[0][0].sum(0), lns[2][0].sum(0)])
    grads["ln_mix_b"] = jnp.stack([lns[0][1].sum(0), lns[2][1].sum(0)])
    grads["ln_ffn_g"] = jnp.stack([lns[1][0].sum(0), lns[3][0].sum(0)])
    grads["ln_ffn_b"] = jnp.stack([lns[1][1].sum(0), lns[3][1].sum(0)])
    grads["ffn_conv"] = jnp.stack([t.sum(axis=2).transpose(1, 0, 2).reshape(FFN_CONV, -1) for t in (dfconv0, dfconv1)])
    grads["meta"] = dh0[0, :N_META]
    return loss_part, dh0, grads, landed


WEIGHTS = ["meta", "a_w_in", "a_conv", "a_log", "a_dt_bias", "a_norm", "a_w_out", "b_w_in", "b_conv", "b_w_out",
           "ln_mix_g", "ln_mix_b", "ffn_w_up", "ffn_conv", "ffn_w_down", "ln_ffn_g", "ln_ffn_b"]
EARLY_WEIGHTS = ["a_w_in", "a_w_out"]
LATE_WEIGHTS = ["b_w_in", "b_w_out", "ffn_w_up", "ffn_w_down"]
MATMUL_WEIGHTS = EARLY_WEIGHTS + LATE_WEIGHTS
SMALL_SHARDED = ["a_conv", "b_conv", "ffn_conv", "meta"]
REPLICATED = ["a_log", "a_dt_bias", "a_norm", "ln_mix_g", "ln_mix_b", "ln_ffn_g", "ln_ffn_b"]
SHARD_AXIS = {"meta": 1, "a_w_in": 2, "a_conv": 2, "a_w_out": 1, "b_w_in": 2, "b_conv": 2, "b_w_out": 1,
              "ffn_w_up": 2, "ffn_conv": 2, "ffn_w_down": 1}
PACK_COLS = 1024
PACK_ROWS_MULTIPLE = 32


def _pack(pieces, lead=()):
    flat = jnp.concatenate([p.reshape(lead + (-1,)) for p in pieces], axis=-1)
    n = flat.shape[-1]
    rows = -(-n // (PACK_COLS * PACK_ROWS_MULTIPLE)) * PACK_ROWS_MULTIPLE
    flat = jnp.pad(flat, [(0, 0)] * len(lead) + [(0, rows * PACK_COLS - n)])
    return flat.reshape(lead + (rows, PACK_COLS))


def _unpack(buf, shapes, lead=()):
    flat = buf.reshape(lead + (-1,))
    out, off = [], 0
    for shp in shapes:
        n = 1
        for s in shp:
            n *= s
        out.append(flat[..., off:off + n].reshape(lead + tuple(shp)))
        off += n
    return out


def _join_shards(stacked, axis):
    return jnp.concatenate([stacked[k] for k in range(4)], axis=axis)


def _split_shards(full, axis):
    return jnp.stack(jnp.split(full, 4, axis=axis))


def _weight_layers(w, names):
    return [w[n][l].astype(BF16) for n in names for l in range(w[n].shape[0])]


def _per_weight(arrays, w, names):
    it = iter(arrays)
    return {n: [next(it) for _ in range(w[n].shape[0])] for n in names}


def _layout_early(full, w):
    width = HEADS * HEAD_DIM
    wt = {n: w[n] for n in ("ln_mix_g", "ln_mix_b", "ln_ffn_g", "ln_ffn_b")}
    w_in = _join_shards(full["a_w_in"][0], 1)
    d = w_in.shape[0]
    n_ff = full["ffn_conv"].shape[2] // 2
    blocks = [w_in[:, s * width:(s + 1) * width] for s in range(4)]
    wt["a3"] = jnp.stack(blocks[:3])[None]
    wt["az"] = blocks[3][None, None]
    wt["a_ba"] = jnp.pad(w_in[:, 4 * width:], ((0, 0), (0, HEAD_DIM - 2 * HEADS)))[None, None]
    wt["a_out"] = full["a_w_out"][0].reshape(1, 1, width, d)
    wt["a_conv3"] = full["a_conv"][0].reshape(GDN_CONV, 3, width).transpose(1, 0, 2)
    wt["b_conv"] = full["b_conv"][0]
    wt["fconv"] = [full["ffn_conv"][l].reshape(FFN_CONV, 2, n_ff).transpose(1, 0, 2) for l in range(2)]
    wt["meta"] = full["meta"]
    in_g_lanes = (HEADS, HEAD_DIM - 2 * HEADS)
    wt["alog_lanes"] = jnp.pad(w["a_log"][0], in_g_lanes)[None]
    wt["dtb_lanes"] = jnp.pad(w["a_dt_bias"][0], in_g_lanes)[None]
    wt["anorm_b"] = jnp.tile(w["a_norm"][0], HEADS)[None]
    return wt


def _layout_late(full):
    d = full["b_w_in"][0].shape[1]
    n_ff = full["ffn_w_up"][0].shape[2]
    return {
        "b_in": _join_shards(full["b_w_in"][0], 1).reshape(d, 3, d).transpose(1, 0, 2)[None],
        "b_out": full["b_w_out"][0].reshape(1, 1, d, d),
        "up": [t[None] for t in full["ffn_w_up"]],
        "down": [t.reshape(2, 1, n_ff, d) for t in full["ffn_w_down"]],
    }


def kernel(x, meta, a_w_in, a_conv, a_log, a_dt_bias, a_norm, a_w_out, b_w_in, b_conv, b_w_out, ln_mix_g, ln_mix_b, ffn_w_up, ffn_conv, ffn_w_down, ln_ffn_g, ln_ffn_b, loss_target, m_meta, m_a_w_in, m_a_conv, m_a_log, m_a_dt_bias, m_a_norm, m_a_w_out, m_b_w_in, m_b_conv, m_b_w_out, m_ln_mix_g, m_ln_mix_b, m_ffn_w_up, m_ffn_conv, m_ffn_w_down, m_ln_ffn_g, m_ln_ffn_b, v_meta, v_a_w_in, v_a_conv, v_a_log, v_a_dt_bias, v_a_norm, v_a_w_out, v_b_w_in, v_b_conv, v_b_w_out, v_ln_mix_g, v_ln_mix_b, v_ffn_w_up, v_ffn_conv, v_ffn_w_down, v_ln_ffn_g, v_ln_ffn_b):
    w = dict(meta=meta, a_w_in=a_w_in, a_conv=a_conv, a_log=a_log, a_dt_bias=a_dt_bias, a_norm=a_norm, a_w_out=a_w_out,
             b_w_in=b_w_in, b_conv=b_conv, b_w_out=b_w_out, ln_mix_g=ln_mix_g, ln_mix_b=ln_mix_b, ffn_w_up=ffn_w_up,
             ffn_conv=ffn_conv, ffn_w_down=ffn_w_down, ln_ffn_g=ln_ffn_g, ln_ffn_b=ln_ffn_b)
    m = dict(meta=m_meta, a_w_in=m_a_w_in, a_conv=m_a_conv, a_log=m_a_log, a_dt_bias=m_a_dt_bias, a_norm=m_a_norm,
             a_w_out=m_a_w_out, b_w_in=m_b_w_in, b_conv=m_b_conv, b_w_out=m_b_w_out, ln_mix_g=m_ln_mix_g,
             ln_mix_b=m_ln_mix_b, ffn_w_up=m_ffn_w_up, ffn_conv=m_ffn_conv, ffn_w_down=m_ffn_w_down,
             ln_ffn_g=m_ln_ffn_g, ln_ffn_b=m_ln_ffn_b)
    v = dict(meta=v_meta, a_w_in=v_a_w_in, a_conv=v_a_conv, a_log=v_a_log, a_dt_bias=v_a_dt_bias, a_norm=v_a_norm,
             a_w_out=v_a_w_out, b_w_in=v_b_w_in, b_conv=v_b_conv, b_w_out=v_b_w_out, ln_mix_g=v_ln_mix_g,
             ln_mix_b=v_ln_mix_b, ffn_w_up=v_ffn_w_up, ffn_conv=v_ffn_conv, ffn_w_down=v_ffn_w_down,
             ln_ffn_g=v_ln_ffn_g, ln_ffn_b=v_ln_ffn_b)
    seq = x.shape[1]
    *stacks, small = all_gather_shards(_weight_layers(w, EARLY_WEIGHTS) + [_pack([w[n] for n in SMALL_SHARDED])],
                                       "gather_early")
    full = _per_weight(stacks, w, EARLY_WEIGHTS)
    for n, t in zip(SMALL_SHARDED, _unpack(small, [w[n].shape for n in SMALL_SHARDED], lead=(4,))):
        full[n] = _join_shards(t, SHARD_AXIS[n])

    def layout_late(late_stacks):
        return _layout_late(_per_weight(late_stacks, w, LATE_WEIGHTS))

    def complete_grads(grads):
        return [g for n in LATE_WEIGHTS for g in grads[n]]

    def sum_pairs(bufs, from_sibling):
        return pair_sums(bufs, from_sibling, [BF16] * len(bufs), "late")

    loss_part, dh0, grads, landed_late = _local_step(x[0], loss_target[0], _layout_early(full, w),
                                                     _weight_layers(w, LATE_WEIGHTS), layout_late, complete_grads, sum_pairs)
    pieces = [_split_shards(grads[n], SHARD_AXIS[n]) for n in SMALL_SHARDED]
    same = jnp.concatenate([grads[n].reshape(-1) for n in REPLICATED] + [jnp.sum(loss_part).reshape(1)])
    pieces.append(jnp.broadcast_to(same, (4,) + same.shape))
    bufs = [g for n in EARLY_WEIGHTS for g in grads[n]] + [_pack(pieces, lead=(4,))]
    from_sibling = swap_halves(bufs, "rs_pair_early")
    landed = scatter_to_chips(pair_sums(bufs, from_sibling, [BF16] * (len(bufs) - 1) + [F32], "early"), "rs_chips_early")
    totals = [chip_sum(t, "rs_chip_sum%d" % i) for i, t in enumerate(landed + landed_late)]
    by_weight = _per_weight(totals[:len(bufs) - 1] + totals[len(bufs):], w, MATMUL_WEIGHTS)
    *shared, small_total = share_halves([by_weight[n] for n in MATMUL_WEIGHTS] + [[totals[len(bufs) - 1]]], "rs_share")
    grad_w = {n: t.reshape(w[n].shape) for n, t in zip(MATMUL_WEIGHTS, shared)}
    rest = SMALL_SHARDED + REPLICATED
    unpacked = _unpack(small_total[0], [w[n].shape for n in rest] + [()])
    grad_w.update(zip(rest, unpacked[:-1]))
    loss = unpacked[-1]
    grad_x = dh0[:, N_META:N_META + seq]
    steps = [adamw(w[n], grad_w[n], m[n], v[n], "adamw_" + n) for n in WEIGHTS]
    return (loss, grad_x, *[grad_w[n] for n in WEIGHTS], *[s[0] for s in steps], *[s[1] for s in steps],
            *[s[2] for s in steps])
```

```python
import functools

import jax
import jax.numpy as jnp
from jax import lax
from jax.experimental import pallas as pl
from jax.experimental.pallas import tpu as pltpu

F32 = jnp.float32
BF16 = jnp.bfloat16
HI = lax.Precision.HIGHEST

N_META = 16
HEADS = 8
HEAD_DIM = 128
CHUNK = 64
GDN_CONV = 4
SC_CONV = 3
FFN_CONV = 3
ALPHA = 4.0 ** 0.25
LN_EPS = 1e-5
RMS_EPS = 1e-6
L2_EPS = 1e-6
Q_SCALE = HEAD_DIM ** -0.5

ADAM_LR = 0.001
ADAM_B1 = 0.9
ADAM_B2 = 0.999
ADAM_EPS = 1e-08
ADAM_WD = 0.01
ADAM_STEP = 10

HALO = 8
VMEM_LIMIT = 48 * 1024 * 1024


def _params(sem=None):
    return pltpu.CompilerParams(dimension_semantics=sem, vmem_limit_bytes=VMEM_LIMIT)


def _dot(a, b, prec=None):
    return jnp.dot(a, b, preferred_element_type=F32, precision=prec)


def _dot_nt(a, b, prec=None):
    return lax.dot_general(a, b, (((1,), (1,)), ((), ())), preferred_element_type=F32, precision=prec)


def _dot_tn(a, b, prec=None):
    return lax.dot_general(a, b, (((0,), (0,)), ((), ())), preferred_element_type=F32, precision=prec)


def _sigmoid(x):
    return 1.0 / (1.0 + jnp.exp(-x))


def _tri_masks():
    r = lax.broadcasted_iota(jnp.int32, (CHUNK, CHUNK), 0)
    c = lax.broadcasted_iota(jnp.int32, (CHUNK, CHUNK), 1)
    return r >= c, r > c, r == c


def _split_hi_lo(x):
    hi = x.astype(BF16)
    return hi, (x - hi.astype(F32)).astype(BF16)


def _mask_dot(mask, x):
    hi, lo = _split_hi_lo(x)
    return _dot(mask, hi) + _dot(mask, lo)


def _cumsum_rows(g):
    causal, _, _ = _tri_masks()
    return _mask_dot(causal.astype(BF16), g)


def _cumsum_rows_transposed(dy):
    _, strict, _ = _tri_masks()
    return _mask_dot((~strict).astype(BF16), dy)


def _dot_split3(a, b):
    a_hi, a_lo = _split_hi_lo(a)
    b_hi, b_lo = _split_hi_lo(b)
    return _dot(a_hi, b_hi) + (_dot(a_hi, b_lo) + _dot(a_lo, b_hi))


@jax.custom_vjp
def _dot_precise(a, b):
    return _dot_split3(a, b)


def _dot_precise_fwd(a, b):
    return _dot_split3(a, b), (a, b)


def _dot_precise_bwd(operands, ct):
    a, b = operands
    return _dot_split3(ct, b.T), _dot_split3(a.T, ct)


_dot_precise.defvjp(_dot_precise_fwd, _dot_precise_bwd)


def _gdn_m(ks, a64s, bbs):
    causal, strict, _ = _tri_masks()
    decay = [jnp.exp(jnp.where(causal, x - x.T, -1e30)) for x in a64s]
    kk = [_dot_nt(k * b, k) for k, b in zip(ks, bbs)]
    return [jnp.where(strict, x * d, 0.0) for x, d in zip(kk, decay)]


def _gdn_inverse_stages(ks, a64s, bbs):
    ms = _gdn_m(ks, a64s, bbs)
    yield
    r = lax.broadcasted_iota(jnp.int32, (CHUNK, CHUNK), 0)
    c = lax.broadcasted_iota(jnp.int32, (CHUNK, CHUNK), 1)
    eye = (r == c).astype(F32)
    same = [jnp.right_shift(r, s) == jnp.right_shift(c, s) for s in (3, 4, 5)]
    d = [jnp.where(same[0], m, 0.0) for m in ms]
    p = [_dot(x, x) for x in d]
    yield
    t = [eye - x for x in d]
    t = [x + _dot(x, y) for x, y in zip(t, p)]
    p = [_dot(x, x) for x in p]
    yield
    t = [x + _dot(x, y) for x, y in zip(t, p)]
    yield
    for inner, outer in ((same[0], same[1]), (same[1], same[2]), (same[2], None)):
        joins = ~inner if outer is None else (outer & ~inner)
        o = [_dot(x, jnp.where(joins, m, 0.0)) for x, m in zip(t, ms)]
        yield
        t = [x - _dot(y, x) for x, y in zip(t, o)]
        yield
    res = [eye - x - _dot_split3(m, x) for m, x in zip(ms, t)]
    yield
    return [x + _dot(x, y) for x, y in zip(t, res)]


def _gdn_apply_stages(qs, ks, vs, gc, a64s, gl, bbs, ss, ts):
    causal, _, _ = _tri_masks()
    n = range(len(qs))
    qk = [_dot_nt(qs[h], ks[h]) for h in n]
    yield
    decay = [jnp.exp(jnp.where(causal, x - x.T, -1e30)) for x in a64s]
    eg = [jnp.exp(x) for x in gc]
    u = [_dot_precise(ts[h], vs[h] * bbs[h]) for h in n]
    w = [_dot_precise(ts[h], ks[h] * bbs[h] * eg[h]) for h in n]
    qk = [qk[h] * decay[h] for h in n]
    kd = [ks[h] * jnp.exp(gl[h] - gc[h]) for h in n]
    yield
    v_new = [u[h] - _dot(w[h], ss[h]) for h in n]
    q_s = [_dot(qs[h] * eg[h], ss[h]) for h in n]
    yield
    o = [q_s[h] + _dot(qk[h], v_new[h]) for h in n]
    s2 = [ss[h] * jnp.exp(gl[h]) + _dot_tn(kd[h], v_new[h]) for h in n]
    return o, s2


def _run_stages(*generators):
    results = [None] * len(generators)
    live = dict(enumerate(generators))
    while live:
        for i, gen in list(live.items()):
            try:
                next(gen)
            except StopIteration as stop:
                results[i] = stop.value
                del live[i]
    return results


def _head_slices(h):
    return slice(h * HEAD_DIM, (h + 1) * HEAD_DIM), slice(h * HEAD_DIM, h * HEAD_DIM + CHUNK)


def _gdn_head_values(x_ref, gate_ref):
    heads = range(HEADS)
    qs, ks, vs = ([x_ref[s, :, _head_slices(h)[0]] for h in heads] for s in range(3))
    gate = gate_ref[...]
    cumulative = _cumsum_rows(gate)
    total = jnp.sum(gate, axis=0, keepdims=True)
    gcums = [cumulative[:, HEADS + h:HEADS + h + 1] for h in heads]
    gtots = [total[:, HEADS + h:HEADS + h + 1] for h in heads]
    bcols = [gate[:, h:h + 1] for h in heads]
    return qs, ks, vs, gcums, gtots, bcols


def _over_lanes(cols, lanes):
    return [jnp.broadcast_to(c, (c.shape[0], lanes)) for c in cols]


def _gdn_inverse_cols(ks, gcums, bcols):
    return _gdn_inverse_stages(ks, _over_lanes(gcums, CHUNK), _over_lanes(bcols, HEAD_DIM))


def _gdn_apply_cols_stages(qs, ks, vs, gcums, gtots, bcols, ss, ts):
    return _gdn_apply_stages(qs, ks, vs, _over_lanes(gcums, HEAD_DIM), _over_lanes(gcums, CHUNK),
                             _over_lanes(gtots, HEAD_DIM), _over_lanes(bcols, HEAD_DIM), ss, ts)


def _gdn_apply_cols(qs, ks, vs, gcums, gtots, bcols, ss, ts):
    return _run_stages(_gdn_apply_cols_stages(qs, ks, vs, gcums, gtots, bcols, ss, ts))[0]


def _gdn_m_cols(ks, gcums, bcols):
    return _gdn_m(ks, _over_lanes(gcums, CHUNK), _over_lanes(bcols, HEAD_DIM))


def _gate_lanes(bcols, gcols):
    rows = gcols[0].shape[0]
    lane = lax.broadcasted_iota(jnp.int32, (rows, HEAD_DIM), 1)
    out = jnp.zeros((rows, HEAD_DIM), F32)
    for h in range(HEADS):
        if bcols is not None:
            out = jnp.where(lane == h, jnp.broadcast_to(bcols[h], out.shape), out)
        out = jnp.where(lane == HEADS + h, jnp.broadcast_to(gcols[h], out.shape), out)
    return out


def _gate_gradient(dbcols, dgcums, dgtots):
    block = _gate_lanes(dbcols, dgcums)
    lane = lax.broadcasted_iota(jnp.int32, block.shape, 1)
    return jnp.where(lane < HEADS, block, _cumsum_rows_transposed(block) + _gate_lanes(None, dgtots))


def gdn_chunk_fwd(qkv, gates, gather=()):
    _, lp, width = qkv.shape
    n_chunks = lp // CHUNK
    n = len(gather)

    def body(x_ref, gate_ref, next_ref, next_gate_ref, *refs):
        shard_refs, (o_ref, s_ref, t_ref), refs = refs[:n], refs[n:n + 3], refs[n + 3:]
        stack_refs, state, t_next, sems = refs[:n], refs[n], refs[n + 1], refs[n + 2:]
        copies = _gather_copies(shard_refs, stack_refs, *sems) if n else None

        def inverse_stages(ref, g_ref):
            _, ks, _, gcums, _, bcols = _gdn_head_values(ref, g_ref)
            return _gdn_inverse_cols(ks, gcums, bcols)

        @pl.when(pl.program_id(0) == 0)
        def _():
            state[...] = jnp.zeros_like(state)
            for h, t in enumerate(_run_stages(inverse_stages(x_ref, gate_ref))[0]):
                t_next[h] = t
            if n:
                _gather_start(copies)

        qs, ks, vs, gcums, gtots, bcols = _gdn_head_values(x_ref, gate_ref)
        ss = [state[h] for h in range(HEADS)]
        ts = [t_next[h] for h in range(HEADS)]
        ts_next, (os_, s2) = _run_stages(inverse_stages(next_ref, next_gate_ref),
                                         _gdn_apply_cols_stages(qs, ks, vs, gcums, gtots, bcols, ss, ts))
        for h in range(HEADS):
            s_ref[0, h] = ss[h]
            t_ref[0, h] = ts[h]
            t_next[h] = ts_next[h]
            o_ref[:, _head_slices(h)[0]] = os_[h]
            state[h] = s2[h]

        if n:
            @pl.when(pl.program_id(0) == n_chunks - 1)
            def _():
                _gather_finish(copies)

    o, states, tinv, *stacks = pl.pallas_call(
        body,
        name="gdn_chunk_fwd",
        grid=(n_chunks,),
        in_specs=[pl.BlockSpec((3, CHUNK, width), lambda c: (0, c, 0)),
                  pl.BlockSpec((CHUNK, HEAD_DIM), lambda c: (c, 0)),
                  pl.BlockSpec((3, CHUNK, width), lambda c: (0, jnp.minimum(c + 1, n_chunks - 1), 0)),
                  pl.BlockSpec((CHUNK, HEAD_DIM), lambda c: (jnp.minimum(c + 1, n_chunks - 1), 0))] + [ANY] * n,
        out_specs=[
            pl.BlockSpec((CHUNK, width), lambda c: (c, 0)),
            pl.BlockSpec((1, HEADS, HEAD_DIM, HEAD_DIM), lambda c: (c, 0, 0, 0)),
            pl.BlockSpec((1, HEADS, CHUNK, CHUNK), lambda c: (c, 0, 0, 0)),
        ] + [ANY] * n,
        out_shape=[
            jax.ShapeDtypeStruct((lp, width), F32),
            jax.ShapeDtypeStruct((n_chunks, HEADS, HEAD_DIM, HEAD_DIM), F32),
            jax.ShapeDtypeStruct((n_chunks, HEADS, CHUNK, CHUNK), F32),
        ] + _gather_out_shapes(gather),
        scratch_shapes=[pltpu.VMEM((HEADS, HEAD_DIM, HEAD_DIM), F32), pltpu.VMEM((HEADS, CHUNK, CHUNK), F32)]
        + (_gather_sems(n) if n else []),
        compiler_params=_params(("arbitrary",)),
    )(qkv, gates, qkv, gates, *gather)
    return o, states, tinv, _set_own_slots(stacks, gather)


def gdn_chunk_bwd(qkv, gates, states, tinv, d_o, scatter=()):
    _, lp, width = qkv.shape
    n_chunks = lp // CHUNK
    last = n_chunks - 1
    n = len(scatter)

    def body(x_ref, gate_ref, s_ref, t_ref, do_ref, *refs):
        leaving_refs, dx_ref, dgate_ref, refs = refs[:n], refs[n], refs[n + 1], refs[n + 2:]
        landing_refs, dstate, sems = refs[:n], refs[n], refs[n + 1:]
        copies = _scatter_copies(leaving_refs, landing_refs, *sems) if n else None

        @pl.when(pl.program_id(0) == 0)
        def _():
            dstate[...] = jnp.zeros_like(dstate)
            if n:
                _scatter_start(copies)

        heads = range(HEADS)
        qs, ks, vs, gcums, gtots, bcols = _gdn_head_values(x_ref, gate_ref)
        ss = [s_ref[0, h] for h in heads]
        ts = [t_ref[0, h] for h in heads]
        d_out = ([do_ref[:, _head_slices(h)[0]] for h in heads], [dstate[h] for h in heads])
        _, vjp_apply = jax.vjp(_gdn_apply_cols, qs, ks, vs, gcums, gtots, bcols, ss, ts)
        dq, dk, dv, dgc, dgt, db, ds, dt = vjp_apply(d_out)
        tts = [t.T for t in ts]
        dm = [_dot(tts[h], dt[h]) for h in heads]
        dm = [-_dot(dm[h], tts[h]) for h in heads]
        _, vjp_m = jax.vjp(_gdn_m_cols, ks, gcums, bcols)
        dk2, dgc2, db2 = vjp_m(dm)
        for h in heads:
            sl = _head_slices(h)[0]
            dx_ref[0, :, sl] = dq[h]
            dx_ref[1, :, sl] = dk[h] + dk2[h]
            dx_ref[2, :, sl] = dv[h]
            dstate[h] = ds[h]
        dgate_ref[...] = _gate_gradient([db[h] + db2[h] for h in heads], [dgc[h] + dgc2[h] for h in heads], dgt)

        if n:
            @pl.when(pl.program_id(0) == n_chunks - 1)
            def _():
                _scatter_finish(copies)

    dqkv, dgates, *landed = pl.pallas_call(
        body,
        name="gdn_chunk_bwd",
        grid=(n_chunks,),
        in_specs=[
            pl.BlockSpec((3, CHUNK, width), lambda c: (0, last - c, 0)),
            pl.BlockSpec((CHUNK, HEAD_DIM), lambda c: (last - c, 0)),
            pl.BlockSpec((1, HEADS, HEAD_DIM, HEAD_DIM), lambda c: (last - c, 0, 0, 0)),
            pl.BlockSpec((1, HEADS, CHUNK, CHUNK), lambda c: (last - c, 0, 0, 0)),
            pl.BlockSpec((CHUNK, width), lambda c: (last - c, 0)),
        ] + [ANY] * n,
        out_specs=[pl.BlockSpec((3, CHUNK, width), lambda c: (0, last - c, 0)),
                   pl.BlockSpec((CHUNK, HEAD_DIM), lambda c: (last - c, 0))] + [ANY] * n,
        out_shape=[jax.ShapeDtypeStruct(qkv.shape, F32), jax.ShapeDtypeStruct(gates.shape, F32)]
        + [jax.ShapeDtypeStruct(b.shape, b.dtype) for b in scatter],
        scratch_shapes=[pltpu.VMEM((HEADS, HEAD_DIM, HEAD_DIM), F32)] + (_scatter_sems(n) if n else []),
        compiler_params=_params(("arbitrary",)),
    )(qkv, gates, states, tinv, d_o, *scatter)
    return dqkv, dgates, _keep_own_slots(landed, scatter)


def mm_nn(a, b, *, tm, name):
    ks, m, tk = a.shape
    _, ns, _, tn = b.shape

    def body(a_ref, b_ref, o_ref):
        p = _dot(a_ref[...].astype(BF16), b_ref[...])

        @pl.when(pl.program_id(2) == 0)
        def _():
            o_ref[...] = p

        @pl.when(pl.program_id(2) > 0)
        def _():
            o_ref[...] += p

    return pl.pallas_call(
        body,
        name=name,
        grid=(ns, m // tm, ks),
        in_specs=[
            pl.BlockSpec((None, tm, tk), lambda n, i, k: (k, i, 0)),
            pl.BlockSpec((None, None, tk, tn), lambda n, i, k: (k, n, 0, 0)),
        ],
        out_specs=pl.BlockSpec((None, tm, tn), lambda n, i, k: (n, i, 0)),
        out_shape=jax.ShapeDtypeStruct((ns, m, tn), F32),
        compiler_params=_params(("arbitrary", "arbitrary", "arbitrary")),
    )(a, b)


def mm_nt(dy, w, *, tm, name, res=None, res_scale=1.0):
    ns, m, tn = dy.shape
    ks, _, tk, _ = w.shape

    def body(*refs):
        if res is None:
            dy_ref, w_ref, o_ref = refs
        else:
            dy_ref, w_ref, r_ref, o_ref = refs
        p = _dot_nt(dy_ref[...].astype(BF16), w_ref[...])

        @pl.when(pl.program_id(2) == 0)
        def _():
            o_ref[...] = p if res is None else p + res_scale * r_ref[...]

        @pl.when(pl.program_id(2) > 0)
        def _():
            o_ref[...] += p

    in_specs = [
        pl.BlockSpec((None, tm, tn), lambda k, i, n: (n, i, 0)),
        pl.BlockSpec((None, None, tk, tn), lambda k, i, n: (k, n, 0, 0)),
    ]
    args = [dy, w]
    if res is not None:
        in_specs.append(pl.BlockSpec((None, tm, tk), lambda k, i, n: (k, i, 0)))
        args.append(res)
    return pl.pallas_call(
        body,
        name=name,
        grid=(ks, m // tm, ns),
        in_specs=in_specs,
        out_specs=pl.BlockSpec((None, tm, tk), lambda k, i, n: (k, i, 0)),
        out_shape=jax.ShapeDtypeStruct((ks, m, tk), F32),
        compiler_params=_params(("arbitrary", "arbitrary", "arbitrary")),
    )(*args)


def mm_tn(x, dy, *, tm, name, rb=None):
    ks, m, tk = x.shape
    ns, _, tn = dy.shape
    rb = tk if rb is None else rb

    def body(x_ref, dy_ref, o_ref):
        @pl.when(pl.program_id(2) == 0)
        def _():
            o_ref[...] = jnp.zeros_like(o_ref)

        dyb = dy_ref[...].astype(BF16)
        for r in range(0, tk, rb):
            o_ref[r:r + rb, :] += _dot_tn(x_ref[:, r:r + rb].astype(BF16), dyb)

    return pl.pallas_call(
        body,
        name=name,
        grid=(ks, ns, m // tm),
        in_specs=[
            pl.BlockSpec((None, tm, tk), lambda k, n, i: (k, i, 0)),
            pl.BlockSpec((None, tm, tn), lambda k, n, i: (n, i, 0)),
        ],
        out_specs=pl.BlockSpec((None, None, tk, tn), lambda k, n, i: (k, n, 0, 0)),
        out_shape=jax.ShapeDtypeStruct((ks, ns, tk, tn), F32),
        compiler_params=_params(("arbitrary", "arbitrary", "arbitrary")),
    )(x, dy)


def _row_partial(x):
    rows, c = x.shape
    return jnp.sum(x.reshape(rows // 8, 8, c), axis=0)


def _layer_norm(r, g, b):
    mu = jnp.mean(r, axis=-1, keepdims=True)
    xc = r - mu
    var = jnp.mean(xc * xc, axis=-1, keepdims=True)
    return xc * lax.rsqrt(var + LN_EPS) * g + b


def _layer_norm_bwd(x, dh, g):
    mu = jnp.mean(x, axis=-1, keepdims=True)
    xc = x - mu
    rstd = lax.rsqrt(jnp.mean(xc * xc, axis=-1, keepdims=True) + LN_EPS)
    xh = xc * rstd
    dxh = dh * g
    m1 = jnp.mean(dxh, axis=-1, keepdims=True)
    m2 = jnp.mean(dxh * xh, axis=-1, keepdims=True)
    return rstd * (dxh - m1 - xh * m2), _row_partial(dh * xh), _row_partial(dh)


def mm_nn_ln(a, b, h_prev, g, beta, *, tm, name):
    ks, m, tk = a.shape
    d = b.shape[3]

    def body(a_ref, b_ref, hp_ref, g_ref, be_ref, r_ref, h_ref):
        p = _dot(a_ref[...].astype(BF16), b_ref[...])

        @pl.when(pl.program_id(1) == 0)
        def _():
            r_ref[...] = p

        @pl.when(pl.program_id(1) > 0)
        def _():
            r_ref[...] += p

        @pl.when(pl.program_id(1) == ks - 1)
        def _():
            r = ALPHA * hp_ref[...] + r_ref[...]
            r_ref[...] = r
            h_ref[...] = _layer_norm(r, g_ref[...], be_ref[...])

    row = pl.BlockSpec((None, tm, d), lambda i, k: (0, i, 0))
    vec = pl.BlockSpec((1, d), lambda i, k: (0, 0))
    return pl.pallas_call(
        body,
        name=name,
        grid=(m // tm, ks),
        in_specs=[
            pl.BlockSpec((None, tm, tk), lambda i, k: (k, i, 0)),
            pl.BlockSpec((None, None, tk, d), lambda i, k: (k, 0, 0, 0)),
            row, vec, vec,
        ],
        out_specs=[row, row],
        out_shape=[jax.ShapeDtypeStruct((1, m, d), F32)] * 2,
        compiler_params=_params(("arbitrary", "arbitrary")),
    )(a, b, h_prev, g, beta)


def mm_nt_ln_bwd(dy, w, res, r, g, *, tm, name, swap=()):
    ns, m, tn = dy.shape
    d = w.shape[2]
    n_swap = len(swap)
    last_tile = m // tm - 1

    def body(dy_ref, w_ref, res_ref, r_ref, g_ref, *refs):
        leaving_refs, (dr_ref, dgb_ref), refs = refs[:n_swap], refs[n_swap:n_swap + 2], refs[n_swap + 2:]
        copies = _swap_copies(leaving_refs, refs[:n_swap], *refs[n_swap:]) if n_swap else None
        p = _dot_nt(dy_ref[...].astype(BF16), w_ref[...])

        @pl.when((pl.program_id(0) == 0) & (pl.program_id(1) == 0))
        def _():
            dgb_ref[...] = jnp.zeros_like(dgb_ref)
            if n_swap:
                _swap_start(copies)

        @pl.when(pl.program_id(1) == 0)
        def _():
            dr_ref[...] = p + ALPHA * res_ref[...]

        @pl.when(pl.program_id(1) > 0)
        def _():
            dr_ref[...] += p

        @pl.when(pl.program_id(1) == ns - 1)
        def _():
            dr, dgamma, dbeta = _layer_norm_bwd(r_ref[...], dr_ref[...], g_ref[...])
            dr_ref[...] = dr
            dgb_ref[0] += dgamma
            dgb_ref[1] += dbeta

        if n_swap:
            @pl.when((pl.program_id(0) == last_tile) & (pl.program_id(1) == ns - 1))
            def _():
                _swap_finish(copies)

    row = pl.BlockSpec((None, tm, d), lambda i, n: (0, i, 0))
    dr, dgb, *landed = pl.pallas_call(
        body,
        name=name,
        grid=(m // tm, ns),
        in_specs=[
            pl.BlockSpec((None, tm, tn), lambda i, n: (n, i, 0)),
            pl.BlockSpec((None, None, d, tn), lambda i, n: (0, n, 0, 0)),
            row, row,
            pl.BlockSpec((1, d), lambda i, n: (0, 0)),
        ] + [ANY] * n_swap,
        out_specs=[row, pl.BlockSpec((2, 8, d), lambda i, n: (0, 0, 0))] + [ANY] * n_swap,
        out_shape=[jax.ShapeDtypeStruct((1, m, d), F32), jax.ShapeDtypeStruct((2, 8, d), F32)] + _swap_out_shapes(swap),
        scratch_shapes=_swap_sems(n_swap) if n_swap else [],
        compiler_params=_params(("arbitrary", "arbitrary")),
    )(dy, w, res, r, g, *swap)
    return dr, dgb, landed


def ln_bwd(r, dh, g, *, tm, name):
    _, lp, d = r.shape

    def body(r_ref, dh_ref, g_ref, dr_ref, dgb_ref):
        dr, dgamma, dbeta = _layer_norm_bwd(r_ref[...], dh_ref[...], g_ref[...])
        dr_ref[...] = dr

        @pl.when(pl.program_id(0) == 0)
        def _():
            dgb_ref[...] = jnp.zeros_like(dgb_ref)

        dgb_ref[0] += dgamma
        dgb_ref[1] += dbeta

    row = pl.BlockSpec((None, tm, d), lambda i: (0, i, 0))
    return pl.pallas_call(
        body,
        name=name,
        grid=(lp // tm,),
        in_specs=[row, row, pl.BlockSpec((1, d), lambda i: (0, 0))],
        out_specs=[row, pl.BlockSpec((2, 8, d), lambda i: (0, 0, 0))],
        out_shape=[jax.ShapeDtypeStruct((1, lp, d), F32), jax.ShapeDtypeStruct((2, 8, d), F32)],
        compiler_params=_params(("arbitrary",)),
    )(r, dh, g)


def loss_grad(h, target, *, first, count, tm):
    _, lp, d = h.shape

    def body(h_ref, t_ref, dh_ref, l_ref):
        row = pl.program_id(0) * tm + lax.broadcasted_iota(jnp.int32, (tm, d), 0)
        valid = (row >= first) & (row < first + count)
        err = jnp.where(valid, h_ref[...] - t_ref[...], 0.0)
        dh_ref[...] = err * (1.0 / d)

        @pl.when(pl.program_id(0) == 0)
        def _():
            l_ref[...] = jnp.zeros_like(l_ref)

        l_ref[...] += _row_partial(err * err) * (0.5 / d)

    return pl.pallas_call(
        body,
        name="loss_grad",
        grid=(lp // tm,),
        in_specs=[pl.BlockSpec((None, tm, d), lambda i: (0, i, 0)), pl.BlockSpec((tm, d), lambda i: (i, 0))],
        out_specs=[pl.BlockSpec((None, tm, d), lambda i: (0, i, 0)), pl.BlockSpec((8, d), lambda i: (0, 0))],
        out_shape=[jax.ShapeDtypeStruct((1, lp, d), F32), jax.ShapeDtypeStruct((8, d), F32)],
        compiler_params=_params(("arbitrary",)),
    )(h, target)


def _halo_index(tile, tm):
    return jnp.maximum(tile * (tm // HALO) - 1, 0)


def _conv_fwd(xs_ref, w, taps, tm):
    acc = w(0) * xs_ref[pl.ds(HALO - taps + 1, tm), :]
    for j in range(1, taps):
        acc += w(j) * xs_ref[pl.ds(HALO - taps + 1 + j, tm), :]
    return acc


def _conv_bwd_x(dcs_ref, w, taps, tm):
    acc = w(0) * dcs_ref[pl.ds(taps - 1, tm), :]
    for j in range(1, taps):
        acc += w(j) * dcs_ref[pl.ds(taps - 1 - j, tm), :]
    return acc


SUB = 8
LANES = 128
PAIR = 2 * SUB
STRIP_UNROLL = 2


def _pair_rows(r0):
    return pl.ds(r0, SUB), pl.ds(r0 + SUB if isinstance(r0, int) else pl.multiple_of(r0 + SUB, SUB), SUB)


def _shift_down(cur, prev, s):
    if s == 0:
        return cur
    row = lax.broadcasted_iota(jnp.int32, cur.shape, 0)
    return jnp.where(row < s, pltpu.roll(prev, s, axis=0), pltpu.roll(cur, s, axis=0))


def _shift_up(cur, nxt, s):
    if s == 0:
        return cur
    row = lax.broadcasted_iota(jnp.int32, cur.shape, 0)
    return jnp.where(row < SUB - s, pltpu.roll(cur, SUB - s, axis=0), pltpu.roll(nxt, SUB - s, axis=0))


def _silu_parts(c):
    sg = _sigmoid(c)
    return c * sg, sg * (1.0 + c * (1.0 - sg))


def _head_sum(x):
    rows, c = x.shape
    parts = []
    for h in range(c // HEAD_DIM):
        s = jnp.sum(x[:, h * HEAD_DIM:(h + 1) * HEAD_DIM], axis=-1, keepdims=True)
        parts.append(jnp.broadcast_to(s, (rows, HEAD_DIM)))
    return parts[0] if len(parts) == 1 else jnp.concatenate(parts, axis=-1)


def _log1p(y):
    u = 1.0 + y
    d = u - 1.0
    return jnp.where(d == 0.0, y, jnp.log(u) * (y / jnp.where(d == 0.0, 1.0, d)))


def _softplus(x):
    return jnp.maximum(x, 0.0) + _log1p(jnp.exp(-jnp.abs(x)))


def _gate_values(x, al, dt):
    lane = lax.broadcasted_iota(jnp.int32, x.shape, 1)
    is_beta, is_g = lane < HEADS, (lane >= HEADS) & (lane < 2 * HEADS)
    return _sigmoid(x), -jnp.exp(al) * _softplus(x + dt), is_beta, is_g


def gdn_gates_fwd(pba, al, dt, *, tm):
    _, lp, width = pba.shape

    def body(x_ref, al_ref, dt_ref, o_ref):
        beta, g, is_beta, is_g = _gate_values(x_ref[...], al_ref[...], dt_ref[...])
        o_ref[...] = jnp.where(is_beta, beta, jnp.where(is_g, g, 0.0))

    vec = pl.BlockSpec((1, width), lambda i: (0, 0))
    return pl.pallas_call(
        body,
        name="gdn_gates_fwd",
        grid=(lp // tm,),
        in_specs=[pl.BlockSpec((None, tm, width), lambda i: (0, i, 0)), vec, vec],
        out_specs=pl.BlockSpec((tm, width), lambda i: (i, 0)),
        out_shape=jax.ShapeDtypeStruct((lp, width), F32),
        compiler_params=_params(("arbitrary",)),
    )(pba, al, dt)


def gdn_gates_bwd(pba, dgates, al, dt, *, tm):
    _, lp, width = pba.shape

    def body(x_ref, d_ref, al_ref, dt_ref, dx_ref, dsc_ref):
        x = x_ref[...]
        beta, g, is_beta, is_g = _gate_values(x, al_ref[...], dt_ref[...])
        d = d_ref[...]
        dg = jnp.where(is_g, d, 0.0)
        da = dg * -jnp.exp(al_ref[...]) * _sigmoid(x + dt_ref[...])
        dx_ref[...] = jnp.where(is_beta, d * beta * (1.0 - beta), da).astype(dx_ref.dtype)

        @pl.when(pl.program_id(0) == 0)
        def _():
            dsc_ref[...] = jnp.zeros_like(dsc_ref)

        dsc_ref[0] += _row_partial(dg * g)
        dsc_ref[1] += _row_partial(da)

    vec = pl.BlockSpec((1, width), lambda i: (0, 0))
    return pl.pallas_call(
        body,
        name="gdn_gates_bwd",
        grid=(lp // tm,),
        in_specs=[pl.BlockSpec((None, tm, width), lambda i: (0, i, 0)), pl.BlockSpec((tm, width), lambda i: (i, 0)), vec, vec],
        out_specs=[pl.BlockSpec((None, tm, width), lambda i: (0, i, 0)), pl.BlockSpec((2, SUB, width), lambda i: (0, 0, 0))],
        out_shape=[jax.ShapeDtypeStruct((1, lp, width), BF16), jax.ShapeDtypeStruct((2, SUB, width), F32)],
        compiler_params=_params(("arbitrary",)),
    )(pba, dgates, al, dt)


def gdn_pre_fwd(p3, conv_w, *, tm, cb):
    _, lp, width = p3.shape
    taps = conv_w.shape[1]

    def body(x_ref, halo_ref, w_ref, o_ref, xs):
        i = pl.program_id(1)
        for s in range(3):
            xs[s, 0:HALO, :] = jnp.where(i > 0, halo_ref[s], 0.0)
            xs[s, HALO:, :] = x_ref[s]
            c = _conv_fwd(xs.at[s], lambda j, s=s: w_ref[s, j:j + 1, :], taps, tm)
            y, _ = _silu_parts(c)
            if s < 2:
                y = y * lax.rsqrt(_head_sum(y * y) + L2_EPS)
                if s == 0:
                    y = y * Q_SCALE
            o_ref[s] = y

    return pl.pallas_call(
        body,
        name="gdn_pre_fwd",
        grid=(width // cb, lp // tm),
        in_specs=[
            pl.BlockSpec((3, tm, cb), lambda j, i: (0, i, j)),
            pl.BlockSpec((3, HALO, cb), lambda j, i: (0, _halo_index(i, tm), j)),
            pl.BlockSpec((3, taps, cb), lambda j, i: (0, 0, j)),
        ],
        out_specs=pl.BlockSpec((3, tm, cb), lambda j, i: (0, i, j)),
        out_shape=jax.ShapeDtypeStruct((3, lp, width), F32),
        scratch_shapes=[pltpu.VMEM((3, tm + HALO, cb), F32)],
        compiler_params=_params(("arbitrary", "arbitrary")),
    )(p3, p3, conv_w)


def gdn_pre_bwd(p3, dqkv, conv_w, *, tm, cb):
    _, lp, width = p3.shape
    taps = conv_w.shape[1]
    last = lp // tm - 1

    def body(x_ref, halo_ref, d_ref, w_ref, dx_ref, dw_ref, xs, dcs, carry):
        step = pl.program_id(1)
        tile = last - step

        @pl.when(step == 0)
        def _():
            carry[...] = jnp.zeros_like(carry)
            dw_ref[...] = jnp.zeros_like(dw_ref)

        for s in range(3):
            w = lambda j, s=s: w_ref[s, j:j + 1, :]
            xs[s, 0:HALO, :] = jnp.where(tile > 0, halo_ref[s], 0.0)
            xs[s, HALO:, :] = x_ref[s]
            c = _conv_fwd(xs.at[s], w, taps, tm)
            y, dsilu = _silu_parts(c)
            dy = d_ref[s]
            if s < 2:
                rn = lax.rsqrt(_head_sum(y * y) + L2_EPS)
                yn = y * rn
                if s == 0:
                    dy = dy * Q_SCALE
                dy = rn * (dy - yn * _head_sum(dy * yn))
            dc = dy * dsilu
            dcs[s, 0:tm, :] = dc
            dcs[s, tm:, :] = carry[s]
            dx_ref[s] = _conv_bwd_x(dcs.at[s], w, taps, tm).astype(dx_ref.dtype)
            carry[s] = dc[0:HALO, :]
            for j in range(taps):
                dw_ref[s, j] += _row_partial(dc * xs[s, pl.ds(HALO - taps + 1 + j, tm), :])

    tile_spec = pl.BlockSpec((3, tm, cb), lambda j, i: (0, last - i, j))
    return pl.pallas_call(
        body,
        name="gdn_pre_bwd",
        grid=(width // cb, lp // tm),
        in_specs=[
            tile_spec,
            pl.BlockSpec((3, HALO, cb), lambda j, i: (0, _halo_index(last - i, tm), j)),
            tile_spec,
            pl.BlockSpec((3, taps, cb), lambda j, i: (0, 0, j)),
        ],
        out_specs=[tile_spec, pl.BlockSpec((3, taps, SUB, cb), lambda j, i: (0, 0, 0, j))],
        out_shape=[jax.ShapeDtypeStruct((3, lp, width), BF16), jax.ShapeDtypeStruct((3, taps, SUB, width), F32)],
        scratch_shapes=[
            pltpu.VMEM((3, tm + HALO, cb), F32),
            pltpu.VMEM((3, tm + HALO, cb), F32),
            pltpu.VMEM((3, HALO, cb), F32),
        ],
        compiler_params=_params(("arbitrary", "arbitrary")),
    )(p3, p3, dqkv, conv_w)


def gdn_post_fwd(o, z, nw_b, *, tm):
    _, lp, width = o.shape

    def body(o_ref, z_ref, nw_ref, y_ref):
        ov = o_ref[...]
        rn = lax.rsqrt(_head_sum(ov * ov) * (1.0 / HEAD_DIM) + RMS_EPS)
        gate, _ = _silu_parts(z_ref[...])
        y_ref[...] = (ov * rn * nw_ref[...] * gate).astype(y_ref.dtype)

    row = pl.BlockSpec((None, tm, width), lambda i: (0, i, 0))
    return pl.pallas_call(
        body,
        name="gdn_post_fwd",
        grid=(lp // tm,),
        in_specs=[row, row, pl.BlockSpec((1, width), lambda i: (0, 0))],
        out_specs=row,
        out_shape=jax.ShapeDtypeStruct((1, lp, width), BF16),
        compiler_params=_params(("arbitrary",)),
    )(o, z, nw_b)


def gdn_post_bwd(o, z, dy, nw_b, *, tm):
    _, lp, width = o.shape

    def body(o_ref, z_ref, dy_ref, nw_ref, do_ref, dz_ref, dnw_ref):
        ov = o_ref[...]
        rn = lax.rsqrt(_head_sum(ov * ov) * (1.0 / HEAD_DIM) + RMS_EPS)
        yn = ov * rn
        gate, dgate = _silu_parts(z_ref[...])
        d_on = dy_ref[...] * gate
        dz_ref[...] = (dy_ref[...] * yn * nw_ref[...] * dgate).astype(dz_ref.dtype)
        a = d_on * nw_ref[...]
        do_ref[...] = rn * (a - yn * (_head_sum(a * yn) * (1.0 / HEAD_DIM)))

        @pl.when(pl.program_id(0) == 0)
        def _():
            dnw_ref[...] = jnp.zeros_like(dnw_ref)

        dnw_ref[...] += _row_partial(d_on * yn)

    row = pl.BlockSpec((None, tm, width), lambda i: (0, i, 0))
    return pl.pallas_call(
        body,
        name="gdn_post_bwd",
        grid=(lp // tm,),
        in_specs=[row, row, row, pl.BlockSpec((1, width), lambda i: (0, 0))],
        out_specs=[row, row, pl.BlockSpec((8, width), lambda i: (0, 0))],
        out_shape=[jax.ShapeDtypeStruct((1, lp, width), F32), jax.ShapeDtypeStruct((1, lp, width), BF16),
                   jax.ShapeDtypeStruct((8, width), F32)],
        compiler_params=_params(("arbitrary",)),
    )(o, z, dy, nw_b)


def ffn_act_fwd(up, conv_w, *, tm, name):
    _, lp, c_w = up.shape
    taps = conv_w.shape[1]

    def body(u_ref, halo_ref, g_ref, w_ref, o_ref):
        first_tile = pl.program_id(1) == 0

        def strip(cur, prev, rows, cs):
            conv = w_ref[taps - 1:taps, cs] * cur
            for j in range(taps - 1):
                conv += w_ref[j:j + 1, cs] * _shift_down(cur, prev, taps - 1 - j)
            y, _ = _silu_parts(conv)
            return y * g_ref[rows, cs]

        def pair(r0, above_of):
            top, bot = _pair_rows(r0)
            for c0 in range(0, c_w, LANES):
                cs = slice(c0, c0 + LANES)
                cur_t, cur_b = u_ref[top, cs], u_ref[bot, cs]
                out = [strip(cur_t, above_of(cs), top, cs), strip(cur_b, cur_t, bot, cs)]
                o_ref[pl.ds(r0, PAIR), cs] = jnp.concatenate(out, axis=0).astype(o_ref.dtype)

        pair(0, lambda cs: jnp.where(first_tile, 0.0, halo_ref[:, cs]))

        def loop_body(s, carry):
            r0 = pl.multiple_of(s * PAIR, PAIR)
            pair(r0, lambda cs: u_ref[pl.ds(pl.multiple_of(r0 - SUB, SUB), SUB), cs])
            return carry

        lax.fori_loop(1, tm // PAIR, loop_body, 0, unroll=STRIP_UNROLL)

    return pl.pallas_call(
        body,
        name=name,
        grid=(2, lp // tm),
        in_specs=[
            pl.BlockSpec((None, tm, c_w), lambda s, i: (s, i, 0)),
            pl.BlockSpec((None, HALO, c_w), lambda s, i: (s, _halo_index(i, tm), 0)),
            pl.BlockSpec((None, tm, c_w), lambda s, i: (2 + s, i, 0)),
            pl.BlockSpec((None, taps, c_w), lambda s, i: (s, 0, 0)),
        ],
        out_specs=pl.BlockSpec((None, tm, c_w), lambda s, i: (s, i, 0)),
        out_shape=jax.ShapeDtypeStruct((2, lp, c_w), BF16),
        compiler_params=_params(("arbitrary", "arbitrary")),
    )(up, up, up, conv_w)


def ffn_act_bwd(up, dact, conv_w, *, tm, name):
    _, lp, c_w = up.shape
    taps = conv_w.shape[1]
    last = lp // tm - 1
    n_pairs = tm // PAIR

    def body(u_ref, halo_ref, g_ref, d_ref, w_ref, dup_ref, dw_ref, below):
        step = pl.program_id(1)
        first_tile = step == last

        @pl.when(step == 0)
        def _():
            below[...] = jnp.zeros_like(below)
            dw_ref[...] = jnp.zeros_like(dw_ref)

        def strip(cur, prev, rows, cs, nxt):
            shifted = [_shift_down(cur, prev, taps - 1 - j) for j in range(taps)]
            conv = w_ref[0:1, cs] * shifted[0]
            for j in range(1, taps):
                conv += w_ref[j:j + 1, cs] * shifted[j]
            y, dsilu = _silu_parts(conv)
            d = d_ref[rows, cs]
            dc = d * g_ref[rows, cs] * dsilu
            dx = w_ref[taps - 1:taps, cs] * dc
            for j in range(taps - 1):
                dx += w_ref[j:j + 1, cs] * _shift_up(dc, nxt, taps - 1 - j)
            return dx, d * y, dc, [dc * s for s in shifted]

        def pair(r0, above_of):
            top, bot = _pair_rows(r0)
            both = pl.ds(r0, PAIR)
            for c0 in range(0, c_w, LANES):
                cs = slice(c0, c0 + LANES)
                cur_t, cur_b = u_ref[top, cs], u_ref[bot, cs]
                dx_b, dg_b, dc_b, dw_b = strip(cur_b, cur_t, bot, cs, below[:, cs])
                dx_t, dg_t, dc_t, dw_t = strip(cur_t, above_of(cs), top, cs, dc_b)
                below[:, cs] = dc_t
                dup_ref[0, both, cs] = jnp.concatenate([dx_t, dx_b], axis=0).astype(dup_ref.dtype)
                dup_ref[1, both, cs] = jnp.concatenate([dg_t, dg_b], axis=0).astype(dup_ref.dtype)
                for j in range(taps):
                    dw_ref[j, :, cs] += dw_t[j] + dw_b[j]

        def loop_body(it, carry):
            r0 = pl.multiple_of((n_pairs - 1 - it) * PAIR, PAIR)
            pair(r0, lambda cs: u_ref[pl.ds(pl.multiple_of(r0 - SUB, SUB), SUB), cs])
            return carry

        lax.fori_loop(0, n_pairs - 1, loop_body, 0, unroll=STRIP_UNROLL)
        pair(0, lambda cs: jnp.where(first_tile, 0.0, halo_ref[:, cs]))

    return pl.pallas_call(
        body,
        name=name,
        grid=(2, lp // tm),
        in_specs=[
            pl.BlockSpec((None, tm, c_w), lambda s, i: (s, last - i, 0)),
            pl.BlockSpec((None, HALO, c_w), lambda s, i: (s, _halo_index(last - i, tm), 0)),
            pl.BlockSpec((None, tm, c_w), lambda s, i: (2 + s, last - i, 0)),
            pl.BlockSpec((None, tm, c_w), lambda s, i: (s, last - i, 0)),
            pl.BlockSpec((None, taps, c_w), lambda s, i: (s, 0, 0)),
        ],
        out_specs=[
            pl.BlockSpec((2, None, tm, c_w), lambda s, i: (0, s, last - i, 0)),
            pl.BlockSpec((None, taps, SUB, c_w), lambda s, i: (s, 0, 0, 0)),
        ],
        out_shape=[jax.ShapeDtypeStruct((2, 2, lp, c_w), BF16), jax.ShapeDtypeStruct((2, taps, SUB, c_w), F32)],
        scratch_shapes=[pltpu.VMEM((SUB, c_w), F32)],
        compiler_params=_params(("arbitrary", "arbitrary")),
    )(up, up, up, dact, conv_w)


def sc_fwd(pb, conv_w, *, tm, cb):
    _, lp, width = pb.shape
    taps = conv_w.shape[0]

    def body(x_ref, halo_ref, w_ref, o_ref):
        first_tile = pl.program_id(1) == 0

        def strip(cur, prev, rows, cs):
            conv = w_ref[taps - 1:taps, cs] * cur
            for j in range(taps - 1):
                conv += w_ref[j:j + 1, cs] * _shift_down(cur, prev, taps - 1 - j)
            return x_ref[0, rows, cs] * conv

        def pair(r0, above_of):
            top, bot = _pair_rows(r0)
            for c0 in range(0, cb, LANES):
                cs = slice(c0, c0 + LANES)
                cur_t = x_ref[1, top, cs] * x_ref[2, top, cs]
                cur_b = x_ref[1, bot, cs] * x_ref[2, bot, cs]
                out = [strip(cur_t, above_of(cs), top, cs), strip(cur_b, cur_t, bot, cs)]
                o_ref[pl.ds(r0, PAIR), cs] = jnp.concatenate(out, axis=0).astype(o_ref.dtype)

        pair(0, lambda cs: jnp.where(first_tile, 0.0, halo_ref[1, :, cs] * halo_ref[2, :, cs]))

        def loop_body(k, carry):
            r0 = pl.multiple_of(k * PAIR, PAIR)
            before = pl.ds(pl.multiple_of(r0 - SUB, SUB), SUB)
            pair(r0, lambda cs: x_ref[1, before, cs] * x_ref[2, before, cs])
            return carry

        lax.fori_loop(1, tm // PAIR, loop_body, 0, unroll=STRIP_UNROLL)

    return pl.pallas_call(
        body,
        name="sc_fwd",
        grid=(width // cb, lp // tm),
        in_specs=[
            pl.BlockSpec((3, tm, cb), lambda j, i: (0, i, j)),
            pl.BlockSpec((3, HALO, cb), lambda j, i: (0, _halo_index(i, tm), j)),
            pl.BlockSpec((taps, cb), lambda j, i: (0, j)),
        ],
        out_specs=pl.BlockSpec((None, tm, cb), lambda j, i: (0, i, j)),
        out_shape=jax.ShapeDtypeStruct((1, lp, width), BF16),
        compiler_params=_params(("arbitrary", "arbitrary")),
    )(pb, pb, conv_w)


def sc_bwd(pb, ds, conv_w, *, tm, cb):
    _, lp, width = pb.shape
    taps = conv_w.shape[0]
    last = lp // tm - 1
    n_pairs = tm // PAIR

    def body(x_ref, halo_ref, d_ref, w_ref, dx_ref, dw_ref, below):
        step = pl.program_id(1)
        first_tile = step == last

        @pl.when(step == 0)
        def _():
            below[...] = jnp.zeros_like(below)
            dw_ref[...] = jnp.zeros_like(dw_ref)

        def strip(cur, prev, rows, cs, nxt):
            gate, left, right = x_ref[0, rows, cs], x_ref[1, rows, cs], x_ref[2, rows, cs]
            shifted = [_shift_down(cur, prev, taps - 1 - j) for j in range(taps)]
            conv = w_ref[0:1, cs] * shifted[0]
            for j in range(1, taps):
                conv += w_ref[j:j + 1, cs] * shifted[j]
            d = d_ref[rows, cs]
            dc = d * gate
            dp = w_ref[taps - 1:taps, cs] * dc
            for j in range(taps - 1):
                dp += w_ref[j:j + 1, cs] * _shift_up(dc, nxt, taps - 1 - j)
            return d * conv, dp * right, dp * left, dc, [dc * s for s in shifted]

        def pair(r0, above_of):
            top, bot = _pair_rows(r0)
            both = pl.ds(r0, PAIR)
            for c0 in range(0, cb, LANES):
                cs = slice(c0, c0 + LANES)
                cur_t = x_ref[1, top, cs] * x_ref[2, top, cs]
                cur_b = x_ref[1, bot, cs] * x_ref[2, bot, cs]
                *dx_b, dc_b, dw_b = strip(cur_b, cur_t, bot, cs, below[:, cs])
                *dx_t, dc_t, dw_t = strip(cur_t, above_of(cs), top, cs, dc_b)
                below[:, cs] = dc_t
                for s in range(3):
                    dx_ref[s, both, cs] = jnp.concatenate([dx_t[s], dx_b[s]], axis=0).astype(dx_ref.dtype)
                for j in range(taps):
                    dw_ref[j, :, cs] += dw_t[j] + dw_b[j]

        def loop_body(it, carry):
            r0 = pl.multiple_of((n_pairs - 1 - it) * PAIR, PAIR)
            before = pl.ds(pl.multiple_of(r0 - SUB, SUB), SUB)
            pair(r0, lambda cs: x_ref[1, before, cs] * x_ref[2, before, cs])
            return carry

        lax.fori_loop(0, n_pairs - 1, loop_body, 0, unroll=STRIP_UNROLL)
        pair(0, lambda cs: jnp.where(first_tile, 0.0, halo_ref[1, :, cs] * halo_ref[2, :, cs]))

    tile_spec = pl.BlockSpec((3, tm, cb), lambda j, i: (0, last - i, j))
    return pl.pallas_call(
        body,
        name="sc_bwd",
        grid=(width // cb, lp // tm),
        in_specs=[
            tile_spec,
            pl.BlockSpec((3, HALO, cb), lambda j, i: (0, _halo_index(last - i, tm), j)),
            pl.BlockSpec((None, tm, cb), lambda j, i: (0, last - i, j)),
            pl.BlockSpec((taps, cb), lambda j, i: (0, j)),
        ],
        out_specs=[tile_spec, pl.BlockSpec((taps, SUB, cb), lambda j, i: (0, 0, j))],
        out_shape=[jax.ShapeDtypeStruct((3, lp, width), BF16), jax.ShapeDtypeStruct((taps, SUB, width), F32)],
        scratch_shapes=[pltpu.VMEM((SUB, cb), F32)],
        compiler_params=_params(("arbitrary", "arbitrary")),
    )(pb, pb, ds, conv_w)


TILE_BYTES = 1536 * 1024


def _rows_tile(rows, cols, multiple=8):
    if rows * cols * 4 <= TILE_BYTES or rows % multiple:
        return rows
    best = multiple
    for t in range(multiple, rows + 1, multiple):
        if rows % t == 0 and t * cols * 4 <= TILE_BYTES:
            best = t
    return best


def pair_sum(g, landed, core, out_dtype, name):
    _, rows, cols = g.shape
    half = rows // 2
    tr = _rows_tile(half, cols, 16)
    nb = half // tr

    def body(c_ref, g_ref, l_ref, o_ref):
        o_ref[...] = (g_ref[...] + l_ref[...]).astype(out_dtype)

    return pl.pallas_call(
        body,
        name=name,
        grid_spec=pltpu.PrefetchScalarGridSpec(
            num_scalar_prefetch=1,
            grid=(4, nb),
            in_specs=[
                pl.BlockSpec((None, tr, cols), lambda s, i, c: (s, c[0] * nb + i, 0)),
                pl.BlockSpec((None, tr, cols), lambda s, i, c: (s, i, 0)),
            ],
            out_specs=pl.BlockSpec((None, tr, cols), lambda s, i, c: (s, i, 0)),
        ),
        out_shape=jax.ShapeDtypeStruct((4, half, cols), out_dtype),
        compiler_params=_params(("arbitrary", "arbitrary")),
    )(core, g, landed)


def chip_sum(x, name):
    _, rows, cols = x.shape
    tr = _rows_tile(rows, cols, 16)

    def body(x0, x1, x2, x3, o_ref):
        acc = x0[...].astype(F32) + x1[...].astype(F32)
        o_ref[...] = (acc + x2[...].astype(F32)) + x3[...].astype(F32)

    return pl.pallas_call(
        body,
        name=name,
        grid=(rows // tr,),
        in_specs=[pl.BlockSpec((None, tr, cols), lambda i, k=k: (k, i, 0)) for k in range(4)],
        out_specs=pl.BlockSpec((tr, cols), lambda i: (i, 0)),
        out_shape=jax.ShapeDtypeStruct((rows, cols), F32),
        compiler_params=_params(("arbitrary",)),
    )(x, x, x, x)


def adamw(w, g, m, v, name):
    shape = w.shape
    cols = shape[-1]
    rows = w.size // cols
    tr = _rows_tile(rows, cols)

    def body(w_ref, g_ref, m_ref, v_ref, d_ref, m2_ref, v2_ref):
        gv = g_ref[...]
        m2 = ADAM_B1 * m_ref[...] + (1.0 - ADAM_B1) * gv
        v2 = ADAM_B2 * v_ref[...] + (1.0 - ADAM_B2) * (gv * gv)
        m_hat = m2 / (1.0 - ADAM_B1 ** ADAM_STEP)
        v_hat = v2 / (1.0 - ADAM_B2 ** ADAM_STEP)
        d_ref[...] = -ADAM_LR * (m_hat / (jnp.sqrt(v_hat) + ADAM_EPS) + ADAM_WD * w_ref[...])
        m2_ref[...] = m2
        v2_ref[...] = v2

    spec = pl.BlockSpec((tr, cols), lambda i: (i, 0))
    outs = pl.pallas_call(
        body,
        name=name,
        grid=(rows // tr,),
        in_specs=[spec] * 4,
        out_specs=[spec] * 3,
        out_shape=[jax.ShapeDtypeStruct((rows, cols), F32)] * 3,
        compiler_params=_params(("arbitrary",)),
    )(*[t.reshape(rows, cols) for t in (w, g, m, v)])
    return tuple(o.reshape(shape) for o in outs)


MESH_ID = pl.DeviceIdType.MESH
ANY = pl.BlockSpec(memory_space=pl.ANY)


def _place():
    x, y, c = lax.axis_index("x"), lax.axis_index("y"), lax.axis_index("c")
    other_chips = [(1 - x, y), (x, 1 - y), (1 - x, 1 - y)]
    return x, y, c, other_chips


def all_gather_shards(bufs, name):
    n = len(bufs)

    def body(*refs):
        x_refs, o_refs = refs[:n], refs[n:2 * n]
        copies = _gather_copies(x_refs, o_refs, *refs[2 * n:])
        _gather_start(copies)
        _gather_finish(copies)

    outs = pl.pallas_call(
        body,
        name=name,
        in_specs=[ANY] * n,
        out_specs=[ANY] * n,
        out_shape=_gather_out_shapes(bufs),
        scratch_shapes=_gather_sems(n),
    )(*bufs)
    return _set_own_slots(outs, bufs)


def _gather_out_shapes(bufs):
    return [jax.ShapeDtypeStruct((4,) + b.shape, b.dtype) for b in bufs]


def _gather_sems(n):
    return [pltpu.SemaphoreType.DMA((6 * n,)), pltpu.SemaphoreType.DMA((6 * n,))]


def _set_own_slots(outs, bufs):
    if not outs:
        return []
    me = 2 * lax.axis_index("x") + lax.axis_index("y")
    return [lax.dynamic_update_index_in_dim(o, b, me, 0) for o, b in zip(outs, bufs)]


def _gather_copies(x_refs, o_refs, send_sems, recv_sems):
    x, y, c, chips = _place()
    me = 2 * x + y
    sibling = (x, y, 1 - c)

    def part(a, slot, hf):
        half = x_refs[a].shape[0] // 2
        return o_refs[a].at[slot, pl.ds(hf * half, half), :]

    def mine(a):
        half = x_refs[a].shape[0] // 2
        return x_refs[a].at[pl.ds(c * half, half), :]

    def copy(k, src, dst, to):
        return pltpu.make_async_remote_copy(src_ref=src, dst_ref=dst, send_sem=send_sems.at[k],
                                            recv_sem=recv_sems.at[k], device_id=to, device_id_type=MESH_ID)

    sends, arrivals, passes, passed = [], [], [], []
    for a in range(len(x_refs)):
        for j, (px, py) in enumerate(chips):
            landed, theirs = part(a, 2 * px + py, c), part(a, 2 * px + py, 1 - c)
            sends.append(copy(6 * a + j, mine(a), part(a, me, c), (px, py, c)))
            arrivals.append(copy(6 * a + j, mine(a), landed, (px, py, c)))
            passes.append(copy(6 * a + 3 + j, landed, landed, sibling))
            passed.append(copy(6 * a + 3 + j, theirs, theirs, sibling))
    return sends, arrivals, passes, passed


def _gather_start(copies):
    for cp in copies[0]:
        cp.start()


def _gather_finish(copies):
    sends, arrivals, passes, passed = copies
    for arrival, cp in zip(arrivals, passes):
        arrival.wait_recv()
        cp.start()
    for cp in passed:
        cp.wait_recv()
    for cp in sends + passes:
        cp.wait_send()


def swap_halves(bufs, name):
    n = len(bufs)

    def body(*refs):
        copies = _swap_copies(refs[:n], refs[n:2 * n], *refs[2 * n:])
        _swap_start(copies)
        _swap_finish(copies)

    return pl.pallas_call(
        body,
        name=name,
        in_specs=[ANY] * n,
        out_specs=[ANY] * n,
        out_shape=_swap_out_shapes(bufs),
        scratch_shapes=_swap_sems(n),
    )(*bufs)


def _swap_out_shapes(bufs):
    return [jax.ShapeDtypeStruct((4, b.shape[1] // 2, b.shape[2]), b.dtype) for b in bufs]


def _swap_sems(n):
    return [pltpu.SemaphoreType.DMA((n,)), pltpu.SemaphoreType.DMA((n,))]


def _swap_copies(x_refs, o_refs, send_sems, recv_sems):
    x, y, c, _ = _place()
    copies = []
    for a, (x_ref, o_ref) in enumerate(zip(x_refs, o_refs)):
        half = x_ref.shape[1] // 2
        copies.append(pltpu.make_async_remote_copy(src_ref=x_ref.at[:, pl.ds((1 - c) * half, half), :], dst_ref=o_ref,
                                                   send_sem=send_sems.at[a], recv_sem=recv_sems.at[a],
                                                   device_id=(x, y, 1 - c), device_id_type=MESH_ID))
    return copies


def _swap_start(copies):
    for cp in copies:
        cp.start()


def _swap_finish(copies):
    for cp in copies:
        cp.wait()


def scatter_to_chips(bufs, name):
    n = len(bufs)

    def body(*refs):
        x_refs, o_refs = refs[:n], refs[n:2 * n]
        copies = _scatter_copies(x_refs, o_refs, *refs[2 * n:])
        _scatter_start(copies)
        _scatter_finish(copies)

    outs = pl.pallas_call(
        body,
        name=name,
        in_specs=[ANY] * n,
        out_specs=[ANY] * n,
        out_shape=[jax.ShapeDtypeStruct(b.shape, b.dtype) for b in bufs],
        scratch_shapes=_scatter_sems(n),
    )(*bufs)
    return _keep_own_slots(outs, bufs)


def _scatter_sems(n):
    return [pltpu.SemaphoreType.DMA((3 * n,)), pltpu.SemaphoreType.DMA((3 * n,))]


def _keep_own_slots(outs, bufs):
    if not outs:
        return []
    me = 2 * lax.axis_index("x") + lax.axis_index("y")
    return [lax.dynamic_update_index_in_dim(o, lax.dynamic_index_in_dim(b, me, 0, keepdims=False), me, 0)
            for o, b in zip(outs, bufs)]


def _scatter_copies(x_refs, o_refs, send_sems, recv_sems):
    x, y, c, chips = _place()
    me = 2 * x + y

    def copy(a, j, src_slot, dst_slot, px, py):
        return pltpu.make_async_remote_copy(src_ref=x_refs[a].at[src_slot], dst_ref=o_refs[a].at[dst_slot],
                                            send_sem=send_sems.at[3 * a + j], recv_sem=recv_sems.at[3 * a + j],
                                            device_id=(px, py, c), device_id_type=MESH_ID)

    sends = [copy(a, j, 2 * px + py, me, px, py) for a in range(len(x_refs)) for j, (px, py) in enumerate(chips)]
    arrivals = [copy(a, j, me, 2 * px + py, px, py) for a in range(len(x_refs)) for j, (px, py) in enumerate(chips)]
    return sends, arrivals


def _scatter_start(copies):
    for cp in copies[0]:
        cp.start()


def _scatter_finish(copies):
    for cp in copies[1]:
        cp.wait_recv()
    for cp in copies[0]:
        cp.wait_send()


def share_halves(groups, name):
    bufs = [b for grp in groups for b in grp]
    where = [(gi, li) for gi, grp in enumerate(groups) for li in range(len(grp))]
    n = len(bufs)

    def body(*refs):
        x_refs, o_refs = refs[:n], refs[n:n + len(groups)]
        send_sems, recv_sems = refs[n + len(groups):]
        x, y, c, _ = _place()
        sent, arrive = [], []
        for a, (gi, li) in enumerate(where):

            def copy(hf, a=a, gi=gi, li=li):
                return pltpu.make_async_remote_copy(src_ref=x_refs[a], dst_ref=o_refs[gi].at[li, hf],
                                                    send_sem=send_sems.at[a], recv_sem=recv_sems.at[a],
                                                    device_id=(x, y, 1 - c), device_id_type=MESH_ID)

            sent.append(copy(c))
            arrive.append(copy(1 - c))
        for cp in sent:
            cp.start()
        for cp in arrive:
            cp.wait_recv()
        for cp in sent:
            cp.wait_send()

    outs = pl.pallas_call(
        body,
        name=name,
        in_specs=[ANY] * n,
        out_specs=[ANY] * len(groups),
        out_shape=[jax.ShapeDtypeStruct((len(grp), 2) + grp[0].shape, grp[0].dtype) for grp in groups],
        scratch_shapes=[pltpu.SemaphoreType.DMA((n,)), pltpu.SemaphoreType.DMA((n,))],
    )(*bufs)
    c = lax.axis_index("c")
    full = [lax.dynamic_update_index_in_dim(o, jnp.stack(grp), c, 1) for o, grp in zip(outs, groups)]
    return [t.reshape(t.shape[0], 2 * t.shape[2], t.shape[3]) for t in full]


def pair_sums(bufs, landed, dtypes, tag):
    core = lax.axis_index("c").astype(jnp.int32).reshape(1)
    return [pair_sum(b, l, core, dt, "rs_pair_sum_%s%d" % (tag, i)) for i, (b, l, dt) in enumerate(zip(bufs, landed, dtypes))]


def _row_tiles(length):
    return (640, 320) if length > 2048 else (128, 64)


def _divisor_tile(rows, target):
    return max(t for t in range(8, min(rows, target) + 1, 8) if rows % t == 0)


def _local_step(x, target, wt, late_shards, layout_late, complete_grads, sum_pairs):
    seq, d = x.shape
    length = N_META + seq
    tm, tm_ffn = _row_tiles(length)
    lp = -(-length // tm) * tm
    tail = jnp.zeros((lp - length, d), F32)
    h0 = jnp.concatenate([wt["meta"], x, tail], axis=0)[None]
    tgt = jnp.concatenate([jnp.zeros((N_META, d), F32), target, tail], axis=0)
    nn = functools.partial(mm_nn, tm=_divisor_tile(lp, 1664))
    nt = functools.partial(mm_nt, tm=_divisor_tile(lp, 1040))
    tn = functools.partial(mm_tn, tm=_divisor_tile(lp, 1664), rb=256)
    nn_ln = functools.partial(mm_nn_ln, tm=_divisor_tile(lp, 832))
    nt_ln_bwd = functools.partial(mm_nt_ln_bwd, tm=_divisor_tile(lp, 832))
    ln_g = [wt["ln_mix_g"][0:1], wt["ln_ffn_g"][0:1], wt["ln_mix_g"][1:2], wt["ln_ffn_g"][1:2]]
    ln_b = [wt["ln_mix_b"][0:1], wt["ln_ffn_b"][0:1], wt["ln_mix_b"][1:2], wt["ln_ffn_b"][1:2]]

    p3 = nn(h0, wt["a3"], name="a_in3")
    pz = nn(h0, wt["az"], name="a_inz")
    pba = nn(h0, wt["a_ba"], name="a_inba")
    qkv = gdn_pre_fwd(p3, wt["a_conv3"], tm=tm, cb=2 * HEAD_DIM)
    gates = gdn_gates_fwd(pba, wt["alog_lanes"], wt["dtb_lanes"], tm=tm)
    o, states, tinv, late_stacks = gdn_chunk_fwd(qkv, gates, late_shards)
    wt = {**wt, **layout_late(late_stacks)}
    onz = gdn_post_fwd(o[None], pz, wt["anorm_b"], tm=tm)
    r1, h1 = nn_ln(onz, wt["a_out"], h0, ln_g[0], ln_b[0], name="a_out_ln1")
    up0 = nn(h1, wt["up"][0], name="up0")
    act0 = ffn_act_fwd(up0, wt["fconv"][0], tm=tm_ffn, name="ffn_act0")
    r2, h2 = nn_ln(act0, wt["down"][0], h1, ln_g[1], ln_b[1], name="down0_ln2")
    pb = nn(h2, wt["b_in"], name="b_in")
    sc = sc_fwd(pb, wt["b_conv"], tm=tm_ffn, cb=d)
    r3, h3 = nn_ln(sc, wt["b_out"], h2, ln_g[2], ln_b[2], name="b_out_ln3")
    up1 = nn(h3, wt["up"][1], name="up1")
    act1 = ffn_act_fwd(up1, wt["fconv"][1], tm=tm_ffn, name="ffn_act1")
    r4, h4 = nn_ln(act1, wt["down"][1], h3, ln_g[3], ln_b[3], name="down1_ln4")

    dh4, loss_part = loss_grad(h4, tgt, first=N_META, count=seq, tm=tm)

    grads = {}
    dr4, dgb4 = ln_bwd(r4, dh4, ln_g[3], tm=tm, name="ln4_bwd")
    d_down1 = tn(act1, dr4, name="d_down1")
    dact1 = nt(dr4, wt["down"][1], name="d_act1")
    dup1, dfconv1 = ffn_act_bwd(up1, dact1, wt["fconv"][1], tm=tm_ffn, name="ffn_act1_bwd")
    dup1 = dup1.reshape(up1.shape)
    d_up1 = tn(h3, dup1, name="d_up1")

    dr3, dgb3, _ = nt_ln_bwd(dup1, wt["up"][1], dr4, r3, ln_g[2], name="d_h3_ln3")
    d_bout = tn(sc, dr3, name="d_b_out")
    dsc = nt(dr3, wt["b_out"], name="d_sc")
    dpb, dbconv = sc_bwd(pb, dsc, wt["b_conv"], tm=tm_ffn, cb=d)
    d_bin = tn(h2, dpb, name="d_b_in")

    dr2, dgb2, _ = nt_ln_bwd(dpb, wt["b_in"], dr3, r2, ln_g[1], name="d_h2_ln2")
    d_down0 = tn(act0, dr2, name="d_down0")
    dact0 = nt(dr2, wt["down"][0], name="d_act0")
    dup0, dfconv0 = ffn_act_bwd(up0, dact0, wt["fconv"][0], tm=tm_ffn, name="ffn_act0_bwd")
    dup0 = dup0.reshape(up0.shape)
    d_up0 = tn(h1, dup0, name="d_up0")
    grads["b_w_in"] = [d_bin[0].transpose(1, 0, 2).reshape(d, 4, 3 * d // 4).transpose(1, 0, 2)]
    grads["b_w_out"] = [d_bout.reshape(4, d // 4, d)]
    grads["ffn_w_up"] = [d_up0[0], d_up1[0]]
    grads["ffn_w_down"] = [t.reshape(4, -1, d) for t in (d_down0, d_down1)]
    complete = complete_grads(grads)

    dr1, dgb1, from_sibling = nt_ln_bwd(dup0, wt["up"][0], dr2, r1, ln_g[0], name="d_h1_ln1", swap=complete)
    leaving = sum_pairs(complete, from_sibling)
    d_aout = tn(onz, dr1, name="d_a_out")
    donz = nt(dr1, wt["a_out"], name="d_onz")
    d_o, dz, dnw = gdn_post_bwd(o[None], pz, donz, wt["anorm_b"], tm=tm)
    dqkv, dgates, landed = gdn_chunk_bwd(qkv, gates, states, tinv, d_o[0], leaving)
    dp3, daconv = gdn_pre_bwd(p3, dqkv, wt["a_conv3"], tm=tm, cb=2 * HEAD_DIM)
    dpba, dscal = gdn_gates_bwd(pba, dgates, wt["alog_lanes"], wt["dtb_lanes"], tm=tm)
    d_a3 = tn(h0, dp3, name="d_a_in3")
    d_az = tn(h0, dz, name="d_a_inz")
    d_aba = tn(h0, dpba, name="d_a_inba")
    dh0 = nt(dp3, wt["a3"], res=dr1, res_scale=ALPHA, name="d_h0a")
    dh0 = nt(dz, wt["az"], res=dh0, res_scale=1.0, name="d_h0z")
    dh0 = nt(dpba, wt["a_ba"], res=dh0, res_scale=1.0, name="d_h0")

    width = HEADS * HEAD_DIM
    d_a_in = jnp.concatenate([d_a3[0, 0], d_a3[0, 1], d_a3[0, 2], d_az[0, 0], d_aba[0, 0][:, :2 * HEADS]], axis=1)
    n_in = d_a_in.shape[1] // 4
    grads["a_w_in"] = [d_a_in.reshape(d, 4, n_in).transpose(1, 0, 2)]
    grads["a_w_out"] = [d_aout.reshape(4, width // 4, d)]
    grads["a_conv"] = daconv.sum(axis=2).transpose(1, 0, 2).reshape(1, GDN_CONV, 3 * width)
    per_head = dscal.sum(axis=1)[:, HEADS:2 * HEADS]
    grads["a_log"] = per_head[0][None]
    grads["a_dt_bias"] = per_head[1][None]
    grads["a_norm"] = dnw.reshape(8, HEADS, HEAD_DIM).sum(axis=(0, 1))[None]
    grads["b_conv"] = dbconv.sum(axis=1)[None]
    lns = [dgb1, dgb2, dgb3, dgb4]
    grads["ln_mix_g"] = jnp.stack([lns[0][0].sum(0), lns[2][0].sum(0)])
    grads["ln_mix_b"] = jnp.stack([lns[0][1].sum(0), lns[2][1].sum(0)])
    grads["ln_ffn_g"] = jnp.stack([lns[1][0].sum(0), lns[3][0].sum(0)])
    grads["ln_ffn_b"] = jnp.stack([lns[1][1].sum(0), lns[3][1].sum(0)])
    grads["ffn_conv"] = jnp.stack([t.sum(axis=2).transpose(1, 0, 2).reshape(FFN_CONV, -1) for t in (dfconv0, dfconv1)])
    grads["meta"] = dh0[0, :N_META]
    return loss_part, dh0, grads, landed


WEIGHTS = ["meta", "a_w_in", "a_conv", "a_log", "a_dt_bias", "a_norm", "a_w_out", "b_w_in", "b_conv", "b_w_out",
           "ln_mix_g", "ln_mix_b", "ffn_w_up", "ffn_conv", "ffn_w_down", "ln_ffn_g", "ln_ffn_b"]
EARLY_WEIGHTS = ["a_w_in", "a_w_out"]
LATE_WEIGHTS = ["b_w_in", "b_w_out", "ffn_w_up", "ffn_w_down"]
MATMUL_WEIGHTS = EARLY_WEIGHTS + LATE_WEIGHTS
SMALL_SHARDED = ["a_conv", "b_conv", "ffn_conv", "meta"]
REPLICATED = ["a_log", "a_dt_bias", "a_norm", "ln_mix_g", "ln_mix_b", "ln_ffn_g", "ln_ffn_b"]
SHARD_AXIS = {"meta": 1, "a_w_in": 2, "a_conv": 2, "a_w_out": 1, "b_w_in": 2, "b_conv": 2, "b_w_out": 1,
              "ffn_w_up": 2, "ffn_conv": 2, "ffn_w_down": 1}
PACK_COLS = 1024
PACK_ROWS_MULTIPLE = 32


def _pack(pieces, lead=()):
    flat = jnp.concatenate([p.reshape(lead + (-1,)) for p in pieces], axis=-1)
    n = flat.shape[-1]
    rows = -(-n // (PACK_COLS * PACK_ROWS_MULTIPLE)) * PACK_ROWS_MULTIPLE
    flat = jnp.pad(flat, [(0, 0)] * len(lead) + [(0, rows * PACK_COLS - n)])
    return flat.reshape(lead + (rows, PACK_COLS))


def _unpack(buf, shapes, lead=()):
    flat = buf.reshape(lead + (-1,))
    out, off = [], 0
    for shp in shapes:
        n = 1
        for s in shp:
            n *= s
        out.append(flat[..., off:off + n].reshape(lead + tuple(shp)))
        off += n
    return out


def _join_shards(stacked, axis):
    return jnp.concatenate([stacked[k] for k in range(4)], axis=axis)


def _split_shards(full, axis):
    return jnp.stack(jnp.split(full, 4, axis=axis))


def _weight_layers(w, names):
    return [w[n][l].astype(BF16) for n in names for l in range(w[n].shape[0])]


def _per_weight(arrays, w, names):
    it = iter(arrays)
    return {n: [next(it) for _ in range(w[n].shape[0])] for n in names}


def _layout_early(full, w):
    width = HEADS * HEAD_DIM
    wt = {n: w[n] for n in ("ln_mix_g", "ln_mix_b", "ln_ffn_g", "ln_ffn_b")}
    w_in = _join_shards(full["a_w_in"][0], 1)
    d = w_in.shape[0]
    n_ff = full["ffn_conv"].shape[2] // 2
    blocks = [w_in[:, s * width:(s + 1) * width] for s in range(4)]
    wt["a3"] = jnp.stack(blocks[:3])[None]
    wt["az"] = blocks[3][None, None]
    wt["a_ba"] = jnp.pad(w_in[:, 4 * width:], ((0, 0), (0, HEAD_DIM - 2 * HEADS)))[None, None]
    wt["a_out"] = full["a_w_out"][0].reshape(1, 1, width, d)
    wt["a_conv3"] = full["a_conv"][0].reshape(GDN_CONV, 3, width).transpose(1, 0, 2)
    wt["b_conv"] = full["b_conv"][0]
    wt["fconv"] = [full["ffn_conv"][l].reshape(FFN_CONV, 2, n_ff).transpose(1, 0, 2) for l in range(2)]
    wt["meta"] = full["meta"]
    in_g_lanes = (HEADS, HEAD_DIM - 2 * HEADS)
    wt["alog_lanes"] = jnp.pad(w["a_log"][0], in_g_lanes)[None]
    wt["dtb_lanes"] = jnp.pad(w["a_dt_bias"][0], in_g_lanes)[None]
    wt["anorm_b"] = jnp.tile(w["a_norm"][0], HEADS)[None]
    return wt


def _layout_late(full):
    d = full["b_w_in"][0].shape[1]
    n_ff = full["ffn_w_up"][0].shape[2]
    return {
        "b_in": _join_shards(full["b_w_in"][0], 1).reshape(d, 3, d).transpose(1, 0, 2)[None],
        "b_out": full["b_w_out"][0].reshape(1, 1, d, d),
        "up": [t[None] for t in full["ffn_w_up"]],
        "down": [t.reshape(2, 1, n_ff, d) for t in full["ffn_w_down"]],
    }


def kernel(x, meta, a_w_in, a_conv, a_log, a_dt_bias, a_norm, a_w_out, b_w_in, b_conv, b_w_out, ln_mix_g, ln_mix_b, ffn_w_up, ffn_conv, ffn_w_down, ln_ffn_g, ln_ffn_b, loss_target, m_meta, m_a_w_in, m_a_conv, m_a_log, m_a_dt_bias, m_a_norm, m_a_w_out, m_b_w_in, m_b_conv, m_b_w_out, m_ln_mix_g, m_ln_mix_b, m_ffn_w_up, m_ffn_conv, m_ffn_w_down, m_ln_ffn_g, m_ln_ffn_b, v_meta, v_a_w_in, v_a_conv, v_a_log, v_a_dt_bias, v_a_norm, v_a_w_out, v_b_w_in, v_b_conv, v_b_w_out, v_ln_mix_g, v_ln_mix_b, v_ffn_w_up, v_ffn_conv, v_ffn_w_down, v_ln_ffn_g, v_ln_ffn_b):
    w = dict(meta=meta, a_w_in=a_w_in, a_conv=a_conv, a_log=a_log, a_dt_bias=a_dt_bias, a_norm=a_norm, a_w_out=a_w_out,
             b_w_in=b_w_in, b_conv=b_conv, b_w_out=b_w_out, ln_mix_g=ln_mix_g, ln_mix_b=ln_mix_b, ffn_w_up=ffn_w_up,
             ffn_conv=ffn_conv, ffn_w_down=ffn_w_down, ln_ffn_g=ln_ffn_g, ln_ffn_b=ln_ffn_b)
    m = dict(meta=m_meta, a_w_in=m_a_w_in, a_conv=m_a_conv, a_log=m_a_log, a_dt_bias=m_a_dt_bias, a_norm=m_a_norm,
             a_w_out=m_a_w_out, b_w_in=m_b_w_in, b_conv=m_b_conv, b_w_out=m_b_w_out, ln_mix_g=m_ln_mix_g,
             ln_mix_b=m_ln_mix_b, ffn_w_up=m_ffn_w_up, ffn_conv=m_ffn_conv, ffn_w_down=m_ffn_w_down,
             ln_ffn_g=m_ln_ffn_g, ln_ffn_b=m_ln_ffn_b)
    v = dict(meta=v_meta, a_w_in=v_a_w_in, a_conv=v_a_conv, a_log=v_a_log, a_dt_bias=v_a_dt_bias, a_norm=v_a_norm,
             a_w_out=v_a_w_out, b_w_in=v_b_w_in, b_conv=v_b_conv, b_w_out=v_b_w_out, ln_mix_g=v_ln_mix_g,
             ln_mix_b=v_ln_mix_b, ffn_w_up=v_ffn_w_up, ffn_conv=v_ffn_conv, ffn_w_down=v_ffn_w_down,
             ln_ffn_g=v_ln_ffn_g, ln_ffn_b=v_ln_ffn_b)
    seq = x.shape[1]
    *stacks, small = all_gather_shards(_weight_layers(w, EARLY_WEIGHTS) + [_pack([w[n] for n in SMALL_SHARDED])],
                                       "gather_early")
    full = _per_weight(stacks, w, EARLY_WEIGHTS)
    for n, t in zip(SMALL_SHARDED, _unpack(small, [w[n].shape for n in SMALL_SHARDED], lead=(4,))):
        full[n] = _join_shards(t, SHARD_AXIS[n])

    def layout_late(late_stacks):
        return _layout_late(_per_weight(late_stacks, w, LATE_WEIGHTS))

    def complete_grads(grads):
        return [g for n in LATE_WEIGHTS for g in grads[n]]

    def sum_pairs(bufs, from_sibling):
        return pair_sums(bufs, from_sibling, [BF16] * len(bufs), "late")

    loss_part, dh0, grads, landed_late = _local_step(x[0], loss_target[0], _layout_early(full, w),
                                                     _weight_layers(w, LATE_WEIGHTS), layout_late, complete_grads, sum_pairs)
    pieces = [_split_shards(grads[n], SHARD_AXIS[n]) for n in SMALL_SHARDED]
    same = jnp.concatenate([grads[n].reshape(-1) for n in REPLICATED] + [jnp.sum(loss_part).reshape(1)])
    pieces.append(jnp.broadcast_to(same, (4,) + same.shape))
    bufs = [g for n in EARLY_WEIGHTS for g in grads[n]] + [_pack(pieces, lead=(4,))]
    from_sibling = swap_halves(bufs, "rs_pair_early")
    landed = scatter_to_chips(pair_sums(bufs, from_sibling, [BF16] * (len(bufs) - 1) + [F32], "early"), "rs_chips_early")
    totals = [chip_sum(t, "rs_chip_sum%d" % i) for i, t in enumerate(landed + landed_late)]
    by_weight = _per_weight(totals[:len(bufs) - 1] + totals[len(bufs):], w, MATMUL_WEIGHTS)
    *shared, small_total = share_halves([by_weight[n] for n in MATMUL_WEIGHTS] + [[totals[len(bufs) - 1]]], "rs_share")
    grad_w = {n: t.reshape(w[n].shape) for n, t in zip(MATMUL_WEIGHTS, shared)}
    rest = SMALL_SHARDED + REPLICATED
    unpacked = _unpack(small_total[0], [w[n].shape for n in rest] + [()])
    grad_w.update(zip(rest, unpacked[:-1]))
    loss = unpacked[-1]
    grad_x = dh0[:, N_META:N_META + seq]
    steps = [adamw(w[n], grad_w[n], m[n], v[n], "adamw_" + n) for n in WEIGHTS]
    return (loss, grad_x, *[grad_w[n] for n in WEIGHTS], *[s[0] for s in steps], *[s[1] for s in steps],
            *[s[2] for s in steps])
```

```python
import functools

import jax
import jax.numpy as jnp
from jax import lax
from jax.experimental import pallas as pl
from jax.experimental.pallas import tpu as pltpu

F32 = jnp.float32
BF16 = jnp.bfloat16
HI = lax.Precision.HIGHEST

N_META = 16
HEADS = 8
HEAD_DIM = 128
CHUNK = 64
GDN_CONV = 4
SC_CONV = 3
FFN_CONV = 3
ALPHA = 4.0 ** 0.25
LN_EPS = 1e-5
RMS_EPS = 1e-6
L2_EPS = 1e-6
Q_SCALE = HEAD_DIM ** -0.5

ADAM_LR = 0.001
ADAM_B1 = 0.9
ADAM_B2 = 0.999
ADAM_EPS = 1e-08
ADAM_WD = 0.01
ADAM_STEP = 10

HALO = 8
VMEM_LIMIT = 48 * 1024 * 1024


def _params(sem=None):
    return pltpu.CompilerParams(dimension_semantics=sem, vmem_limit_bytes=VMEM_LIMIT)


def _dot(a, b, prec=None):
    return jnp.dot(a, b, preferred_element_type=F32, precision=prec)


def _dot_nt(a, b, prec=None):
    return lax.dot_general(a, b, (((1,), (1,)), ((), ())), preferred_element_type=F32, precision=prec)


def _dot_tn(a, b, prec=None):
    return lax.dot_general(a, b, (((0,), (0,)), ((), ())), preferred_element_type=F32, precision=prec)


def _sigmoid(x):
    return 1.0 / (1.0 + jnp.exp(-x))


def _tri_masks():
    r = lax.broadcasted_iota(jnp.int32, (CHUNK, CHUNK), 0)
    c = lax.broadcasted_iota(jnp.int32, (CHUNK, CHUNK), 1)
    return r >= c, r > c, r == c


def _split_hi_lo(x):
    hi = x.astype(BF16)
    return hi, (x - hi.astype(F32)).astype(BF16)


def _mask_dot(mask, x):
    hi, lo = _split_hi_lo(x)
    return _dot(mask, hi) + _dot(mask, lo)


def _cumsum_rows(g):
    causal, _, _ = _tri_masks()
    return _mask_dot(causal.astype(BF16), g)


def _cumsum_rows_transposed(dy):
    _, strict, _ = _tri_masks()
    return _mask_dot((~strict).astype(BF16), dy)


def _dot_split3(a, b):
    a_hi, a_lo = _split_hi_lo(a)
    b_hi, b_lo = _split_hi_lo(b)
    return _dot(a_hi, b_hi) + (_dot(a_hi, b_lo) + _dot(a_lo, b_hi))


@jax.custom_vjp
def _dot_precise(a, b):
    return _dot_split3(a, b)


def _dot_precise_fwd(a, b):
    return _dot_split3(a, b), (a, b)


def _dot_precise_bwd(operands, ct):
    a, b = operands
    return _dot_split3(ct, b.T), _dot_split3(a.T, ct)


_dot_precise.defvjp(_dot_precise_fwd, _dot_precise_bwd)


def _gdn_m(ks, a64s, bbs):
    causal, strict, _ = _tri_masks()
    decay = [jnp.exp(jnp.where(causal, x - x.T, -1e30)) for x in a64s]
    kk = [_dot_nt(k * b, k) for k, b in zip(ks, bbs)]
    return [jnp.where(strict, x * d, 0.0) for x, d in zip(kk, decay)]


def _gdn_inverse_stages(ks, a64s, bbs):
    ms = _gdn_m(ks, a64s, bbs)
    yield
    r = lax.broadcasted_iota(jnp.int32, (CHUNK, CHUNK), 0)
    c = lax.broadcasted_iota(jnp.int32, (CHUNK, CHUNK), 1)
    eye = (r == c).astype(F32)
    same = [jnp.right_shift(r, s) == jnp.right_shift(c, s) for s in (3, 4, 5)]
    d = [jnp.where(same[0], m, 0.0) for m in ms]
    p = [_dot(x, x) for x in d]
    yield
    t = [eye - x for x in d]
    t = [x + _dot(x, y) for x, y in zip(t, p)]
    p = [_dot(x, x) for x in p]
    yield
    t = [x + _dot(x, y) for x, y in zip(t, p)]
    yield
    for inner, outer in ((same[0], same[1]), (same[1], same[2]), (same[2], None)):
        joins = ~inner if outer is None else (outer & ~inner)
        o = [_dot(x, jnp.where(joins, m, 0.0)) for x, m in zip(t, ms)]
        yield
        t = [x - _dot(y, x) for x, y in zip(t, o)]
        yield
    res = [eye - x - _dot_split3(m, x) for m, x in zip(ms, t)]
    yield
    return [x + _dot(x, y) for x, y in zip(t, res)]


def _gdn_apply_stages(qs, ks, vs, gc, a64s, gl, bbs, ss, ts):
    causal, _, _ = _tri_masks()
    n = range(len(qs))
    qk = [_dot_nt(qs[h], ks[h]) for h in n]
    yield
    decay = [jnp.exp(jnp.where(causal, x - x.T, -1e30)) for x in a64s]
    eg = [jnp.exp(x) for x in gc]
    u = [_dot_precise(ts[h], vs[h] * bbs[h]) for h in n]
    w = [_dot_precise(ts[h], ks[h] * bbs[h] * eg[h]) for h in n]
    qk = [qk[h] * decay[h] for h in n]
    kd = [ks[h] * jnp.exp(gl[h] - gc[h]) for h in n]
    yield
    v_new = [u[h] - _dot(w[h], ss[h]) for h in n]
    q_s = [_dot(qs[h] * eg[h], ss[h]) for h in n]
    yield
    o = [q_s[h] + _dot(qk[h], v_new[h]) for h in n]
    s2 = [ss[h] * jnp.exp(gl[h]) + _dot_tn(kd[h], v_new[h]) for h in n]
    return o, s2


def _run_stages(*generators):
    results = [None] * len(generators)
    live = dict(enumerate(generators))
    while live:
        for i, gen in list(live.items()):
            try:
                next(gen)
            except StopIteration as stop:
                results[i] = stop.value
                del live[i]
    return results


def _head_slices(h):
    return slice(h * HEAD_DIM, (h + 1) * HEAD_DIM), slice(h * HEAD_DIM, h * HEAD_DIM + CHUNK)


def _gdn_head_values(x_ref, gate_ref):
    heads = range(HEADS)
    qs, ks, vs = ([x_ref[s, :, _head_slices(h)[0]] for h in heads] for s in range(3))
    gate = gate_ref[...]
    cumulative = _cumsum_rows(gate)
    total = jnp.sum(gate, axis=0, keepdims=True)
    gcums = [cumulative[:, HEADS + h:HEADS + h + 1] for h in heads]
    gtots = [total[:, HEADS + h:HEADS + h + 1] for h in heads]
    bcols = [gate[:, h:h + 1] for h in heads]
    return qs, ks, vs, gcums, gtots, bcols


def _over_lanes(cols, lanes):
    return [jnp.broadcast_to(c, (c.shape[0], lanes)) for c in cols]


def _gdn_inverse_cols(ks, gcums, bcols):
    return _gdn_inverse_stages(ks, _over_lanes(gcums, CHUNK), _over_lanes(bcols, HEAD_DIM))


def _gdn_apply_cols_stages(qs, ks, vs, gcums, gtots, bcols, ss, ts):
    return _gdn_apply_stages(qs, ks, vs, _over_lanes(gcums, HEAD_DIM), _over_lanes(gcums, CHUNK),
                             _over_lanes(gtots, HEAD_DIM), _over_lanes(bcols, HEAD_DIM), ss, ts)


def _gdn_apply_cols(qs, ks, vs, gcums, gtots, bcols, ss, ts):
    return _run_stages(_gdn_apply_cols_stages(qs, ks, vs, gcums, gtots, bcols, ss, ts))[0]


def _gdn_m_cols(ks, gcums, bcols):
    return _gdn_m(ks, _over_lanes(gcums, CHUNK), _over_lanes(bcols, HEAD_DIM))


def _gate_lanes(bcols, gcols):
    rows = gcols[0].shape[0]
    lane = lax.broadcasted_iota(jnp.int32, (rows, HEAD_DIM), 1)
    out = jnp.zeros((rows, HEAD_DIM), F32)
    for h in range(HEADS):
        if bcols is not None:
            out = jnp.where(lane == h, jnp.broadcast_to(bcols[h], out.shape), out)
        out = jnp.where(lane == HEADS + h, jnp.broadcast_to(gcols[h], out.shape), out)
    return out


def _gate_gradient(dbcols, dgcums, dgtots):
    block = _gate_lanes(dbcols, dgcums)
    lane = lax.broadcasted_iota(jnp.int32, block.shape, 1)
    return jnp.where(lane < HEADS, block, _cumsum_rows_transposed(block) + _gate_lanes(None, dgtots))


def gdn_chunk_fwd(qkv, gates, gather=()):
    _, lp, width = qkv.shape
    n_chunks = lp // CHUNK
    n = len(gather)

    def body(x_ref, gate_ref, next_ref, next_gate_ref, *refs):
        shard_refs, (o_ref, s_ref, t_ref), refs = refs[:n], refs[n:n + 3], refs[n + 3:]
        stack_refs, state, t_next, sems = refs[:n], refs[n], refs[n + 1], refs[n + 2:]
        copies = _gather_copies(shard_refs, stack_refs, *sems) if n else None

        def inverse_stages(ref, g_ref):
            _, ks, _, gcums, _, bcols = _gdn_head_values(ref, g_ref)
            return _gdn_inverse_cols(ks, gcums, bcols)

        @pl.when(pl.program_id(0) == 0)
        def _():
            state[...] = jnp.zeros_like(state)
            for h, t in enumerate(_run_stages(inverse_stages(x_ref, gate_ref))[0]):
                t_next[h] = t
            if n:
                _gather_start(copies)

        qs, ks, vs, gcums, gtots, bcols = _gdn_head_values(x_ref, gate_ref)
        ss = [state[h] for h in range(HEADS)]
        ts = [t_next[h] for h in range(HEADS)]
        ts_next, (os_, s2) = _run_stages(inverse_stages(next_ref, next_gate_ref),
                                         _gdn_apply_cols_stages(qs, ks, vs, gcums, gtots, bcols, ss, ts))
        for h in range(HEADS):
            s_ref[0, h] = ss[h]
            t_ref[0, h] = ts[h]
            t_next[h] = ts_next[h]
            o_ref[:, _head_slices(h)[0]] = os_[h]
            state[h] = s2[h]

        if n:
            @pl.when(pl.program_id(0) == n_chunks - 1)
            def _():
                _gather_finish(copies)

    o, states, tinv, *stacks = pl.pallas_call(
        body,
        name="gdn_chunk_fwd",
        grid=(n_chunks,),
        in_specs=[pl.BlockSpec((3, CHUNK, width), lambda c: (0, c, 0)),
                  pl.BlockSpec((CHUNK, HEAD_DIM), lambda c: (c, 0)),
                  pl.BlockSpec((3, CHUNK, width), lambda c: (0, jnp.minimum(c + 1, n_chunks - 1), 0)),
                  pl.BlockSpec((CHUNK, HEAD_DIM), lambda c: (jnp.minimum(c + 1, n_chunks - 1), 0))] + [ANY] * n,
        out_specs=[
            pl.BlockSpec((CHUNK, width), lambda c: (c, 0)),
            pl.BlockSpec((1, HEADS, HEAD_DIM, HEAD_DIM), lambda c: (c, 0, 0, 0)),
            pl.BlockSpec((1, HEADS, CHUNK, CHUNK), lambda c: (c, 0, 0, 0)),
        ] + [ANY] * n,
        out_shape=[
            jax.ShapeDtypeStruct((lp, width), F32),
            jax.ShapeDtypeStruct((n_chunks, HEADS, HEAD_DIM, HEAD_DIM), F32),
            jax.ShapeDtypeStruct((n_chunks, HEADS, CHUNK, CHUNK), F32),
        ] + _gather_out_shapes(gather),
        scratch_shapes=[pltpu.VMEM((HEADS, HEAD_DIM, HEAD_DIM), F32), pltpu.VMEM((HEADS, CHUNK, CHUNK), F32)]
        + (_gather_sems(n) if n else []),
        compiler_params=_params(("arbitrary",)),
    )(qkv, gates, qkv, gates, *gather)
    return o, states, tinv, _set_own_slots(stacks, gather)


def gdn_chunk_bwd(qkv, gates, states, tinv, d_o, scatter=()):
    _, lp, width = qkv.shape
    n_chunks = lp // CHUNK
    last = n_chunks - 1
    n = len(scatter)

    def body(x_ref, gate_ref, s_ref, t_ref, do_ref, *refs):
        leaving_refs, dx_ref, dgate_ref, refs = refs[:n], refs[n], refs[n + 1], refs[n + 2:]
        landing_refs, dstate, sems = refs[:n], refs[n], refs[n + 1:]
        copies = _scatter_copies(leaving_refs, landing_refs, *sems) if n else None

        @pl.when(pl.program_id(0) == 0)
        def _():
            dstate[...] = jnp.zeros_like(dstate)
            if n:
                _scatter_start(copies)

        heads = range(HEADS)
        qs, ks, vs, gcums, gtots, bcols = _gdn_head_values(x_ref, gate_ref)
        ss = [s_ref[0, h] for h in heads]
        ts = [t_ref[0, h] for h in heads]
        d_out = ([do_ref[:, _head_slices(h)[0]] for h in heads], [dstate[h] for h in heads])
        _, vjp_apply = jax.vjp(_gdn_apply_cols, qs, ks, vs, gcums, gtots, bcols, ss, ts)
        dq, dk, dv, dgc, dgt, db, ds, dt = vjp_apply(d_out)
        tts = [t.T for t in ts]
        dm = [_dot(tts[h], dt[h]) for h in heads]
        dm = [-_dot(dm[h], tts[h]) for h in heads]
        _, vjp_m = jax.vjp(_gdn_m_cols, ks, gcums, bcols)
        dk2, dgc2, db2 = vjp_m(dm)
        for h in heads:
            sl = _head_slices(h)[0]
            dx_ref[0, :, sl] = dq[h]
            dx_ref[1, :, sl] = dk[h] + dk2[h]
            dx_ref[2, :, sl] = dv[h]
            dstate[h] = ds[h]
        dgate_ref[...] = _gate_gradient([db[h] + db2[h] for h in heads], [dgc[h] + dgc2[h] for h in heads], dgt)

        if n:
            @pl.when(pl.program_id(0) == n_chunks - 1)
            def _():
                _scatter_finish(copies)

    dqkv, dgates, *landed = pl.pallas_call(
        body,
        name="gdn_chunk_bwd",
        grid=(n_chunks,),
        in_specs=[
            pl.BlockSpec((3, CHUNK, width), lambda c: (0, last - c, 0)),
            pl.BlockSpec((CHUNK, HEAD_DIM), lambda c: (last - c, 0)),
            pl.BlockSpec((1, HEADS, HEAD_DIM, HEAD_DIM), lambda c: (last - c, 0, 0, 0)),
            pl.BlockSpec((1, HEADS, CHUNK, CHUNK), lambda c: (last - c, 0, 0, 0)),
            pl.BlockSpec((CHUNK, width), lambda c: (last - c, 0)),
        ] + [ANY] * n,
        out_specs=[pl.BlockSpec((3, CHUNK, width), lambda c: (0, last - c, 0)),
                   pl.BlockSpec((CHUNK, HEAD_DIM), lambda c: (last - c, 0))] + [ANY] * n,
        out_shape=[jax.ShapeDtypeStruct(qkv.shape, F32), jax.ShapeDtypeStruct(gates.shape, F32)]
        + [jax.ShapeDtypeStruct(b.shape, b.dtype) for b in scatter],
        scratch_shapes=[pltpu.VMEM((HEADS, HEAD_DIM, HEAD_DIM), F32)] + (_scatter_sems(n) if n else []),
        compiler_params=_params(("arbitrary",)),
    )(qkv, gates, states, tinv, d_o, *scatter)
    return dqkv, dgates, _keep_own_slots(landed, scatter)


def mm_nn(a, b, *, tm, name):
    ks, m, tk = a.shape
    _, ns, _, tn = b.shape

    def body(a_ref, b_ref, o_ref):
        p = _dot(a_ref[...].astype(BF16), b_ref[...])

        @pl.when(pl.program_id(2) == 0)
        def _():
            o_ref[...] = p

        @pl.when(pl.program_id(2) > 0)
        def _():
            o_ref[...] += p

    return pl.pallas_call(
        body,
        name=name,
        grid=(ns, m // tm, ks),
        in_specs=[
            pl.BlockSpec((None, tm, tk), lambda n, i, k: (k, i, 0)),
            pl.BlockSpec((None, None, tk, tn), lambda n, i, k: (k, n, 0, 0)),
        ],
        out_specs=pl.BlockSpec((None, tm, tn), lambda n, i, k: (n, i, 0)),
        out_shape=jax.ShapeDtypeStruct((ns, m, tn), F32),
        compiler_params=_params(("arbitrary", "arbitrary", "arbitrary")),
    )(a, b)


def mm_nt(dy, w, *, tm, name, res=None, res_scale=1.0):
    ns, m, tn = dy.shape
    ks, _, tk, _ = w.shape

    def body(*refs):
        if res is None:
            dy_ref, w_ref, o_ref = refs
        else:
            dy_ref, w_ref, r_ref, o_ref = refs
        p = _dot_nt(dy_ref[...].astype(BF16), w_ref[...])

        @pl.when(pl.program_id(2) == 0)
        def _():
            o_ref[...] = p if res is None else p + res_scale * r_ref[...]

        @pl.when(pl.program_id(2) > 0)
        def _():
            o_ref[...] += p

    in_specs = [
        pl.BlockSpec((None, tm, tn), lambda k, i, n: (n, i, 0)),
        pl.BlockSpec((None, None, tk, tn), lambda k, i, n: (k, n, 0, 0)),
    ]
    args = [dy, w]
    if res is not None:
        in_specs.append(pl.BlockSpec((None, tm, tk), lambda k, i, n: (k, i, 0)))
        args.append(res)
    return pl.pallas_call(
        body,
        name=name,
        grid=(ks, m // tm, ns),
        in_specs=in_specs,
        out_specs=pl.BlockSpec((None, tm, tk), lambda k, i, n: (k, i, 0)),
        out_shape=jax.ShapeDtypeStruct((ks, m, tk), F32),
        compiler_params=_params(("arbitrary", "arbitrary", "arbitrary")),
    )(*args)


def mm_tn(x, dy, *, tm, name, rb=None):
    ks, m, tk = x.shape
    ns, _, tn = dy.shape
    rb = tk if rb is None else rb

    def body(x_ref, dy_ref, o_ref):
        @pl.when(pl.program_id(2) == 0)
        def _():
            o_ref[...] = jnp.zeros_like(o_ref)

        dyb = dy_ref[...].astype(BF16)
        for r in range(0, tk, rb):
            o_ref[r:r + rb, :] += _dot_tn(x_ref[:, r:r + rb].astype(BF16), dyb)

    return pl.pallas_call(
        body,
        name=name,
        grid=(ks, ns, m // tm),
        in_specs=[
            pl.BlockSpec((None, tm, tk), lambda k, n, i: (k, i, 0)),
            pl.BlockSpec((None, tm, tn), lambda k, n, i: (n, i, 0)),
        ],
        out_specs=pl.BlockSpec((None, None, tk, tn), lambda k, n, i: (k, n, 0, 0)),
        out_shape=jax.ShapeDtypeStruct((ks, ns, tk, tn), F32),
        compiler_params=_params(("arbitrary", "arbitrary", "arbitrary")),
    )(x, dy)


def _row_partial(x):
    rows, c = x.shape
    return jnp.sum(x.reshape(rows // 8, 8, c), axis=0)


def _layer_norm(r, g, b):
    mu = jnp.mean(r, axis=-1, keepdims=True)
    xc = r - mu
    var = jnp.mean(xc * xc, axis=-1, keepdims=True)
    return xc * lax.rsqrt(var + LN_EPS) * g + b


def _layer_norm_bwd(x, dh, g):
    mu = jnp.mean(x, axis=-1, keepdims=True)
    xc = x - mu
    rstd = lax.rsqrt(jnp.mean(xc * xc, axis=-1, keepdims=True) + LN_EPS)
    xh = xc * rstd
    dxh = dh * g
    m1 = jnp.mean(dxh, axis=-1, keepdims=True)
    m2 = jnp.mean(dxh * xh, axis=-1, keepdims=True)
    return rstd * (dxh - m1 - xh * m2), _row_partial(dh * xh), _row_partial(dh)


def mm_nn_ln(a, b, h_prev, g, beta, *, tm, name):
    ks, m, tk = a.shape
    d = b.shape[3]

    def body(a_ref, b_ref, hp_ref, g_ref, be_ref, r_ref, h_ref):
        p = _dot(a_ref[...].astype(BF16), b_ref[...])

        @pl.when(pl.program_id(1) == 0)
        def _():
            r_ref[...] = p

        @pl.when(pl.program_id(1) > 0)
        def _():
            r_ref[...] += p

        @pl.when(pl.program_id(1) == ks - 1)
        def _():
            r = ALPHA * hp_ref[...] + r_ref[...]
            r_ref[...] = r
            h_ref[...] = _layer_norm(r, g_ref[...], be_ref[...])

    row = pl.BlockSpec((None, tm, d), lambda i, k: (0, i, 0))
    vec = pl.BlockSpec((1, d), lambda i, k: (0, 0))
    return pl.pallas_call(
        body,
        name=name,
        grid=(m // tm, ks),
        in_specs=[
            pl.BlockSpec((None, tm, tk), lambda i, k: (k, i, 0)),
            pl.BlockSpec((None, None, tk, d), lambda i, k: (k, 0, 0, 0)),
            row, vec, vec,
        ],
        out_specs=[row, row],
        out_shape=[jax.ShapeDtypeStruct((1, m, d), F32)] * 2,
        compiler_params=_params(("arbitrary", "arbitrary")),
    )(a, b, h_prev, g, beta)


def mm_nt_ln_bwd(dy, w, res, r, g, *, tm, name, swap=()):
    ns, m, tn = dy.shape
    d = w.shape[2]
    n_swap = len(swap)
    last_tile = m // tm - 1

    def body(dy_ref, w_ref, res_ref, r_ref, g_ref, *refs):
        leaving_refs, (dr_ref, dgb_ref), refs = refs[:n_swap], refs[n_swap:n_swap + 2], refs[n_swap + 2:]
        copies = _swap_copies(leaving_refs, refs[:n_swap], *refs[n_swap:]) if n_swap else None
        p = _dot_nt(dy_ref[...].astype(BF16), w_ref[...])

        @pl.when((pl.program_id(0) == 0) & (pl.program_id(1) == 0))
        def _():
            dgb_ref[...] = jnp.zeros_like(dgb_ref)
            if n_swap:
                _swap_start(copies)

        @pl.when(pl.program_id(1) == 0)
        def _():
            dr_ref[...] = p + ALPHA * res_ref[...]

        @pl.when(pl.program_id(1) > 0)
        def _():
            dr_ref[...] += p

        @pl.when(pl.program_id(1) == ns - 1)
        def _():
            dr, dgamma, dbeta = _layer_norm_bwd(r_ref[...], dr_ref[...], g_ref[...])
            dr_ref[...] = dr
            dgb_ref[0] += dgamma
            dgb_ref[1] += dbeta

        if n_swap:
            @pl.when((pl.program_id(0) == last_tile) & (pl.program_id(1) == ns - 1))
            def _():
                _swap_finish(copies)

    row = pl.BlockSpec((None, tm, d), lambda i, n: (0, i, 0))
    dr, dgb, *landed = pl.pallas_call(
        body,
        name=name,
        grid=(m // tm, ns),
        in_specs=[
            pl.BlockSpec((None, tm, tn), lambda i, n: (n, i, 0)),
            pl.BlockSpec((None, None, d, tn), lambda i, n: (0, n, 0, 0)),
            row, row,
            pl.BlockSpec((1, d), lambda i, n: (0, 0)),
        ] + [ANY] * n_swap,
        out_specs=[row, pl.BlockSpec((2, 8, d), lambda i, n: (0, 0, 0))] + [ANY] * n_swap,
        out_shape=[jax.ShapeDtypeStruct((1, m, d), F32), jax.ShapeDtypeStruct((2, 8, d), F32)] + _swap_out_shapes(swap),
        scratch_shapes=_swap_sems(n_swap) if n_swap else [],
        compiler_params=_params(("arbitrary", "arbitrary")),
    )(dy, w, res, r, g, *swap)
    return dr, dgb, landed


def loss_ln_bwd(h, target, r, g, *, first, count, tm):
    _, lp, d = h.shape

    def body(h_ref, t_ref, r_ref, g_ref, dr_ref, dgb_ref, l_ref):
        row = pl.program_id(0) * tm + lax.broadcasted_iota(jnp.int32, (tm, d), 0)
        valid = (row >= first) & (row < first + count)
        err = jnp.where(valid, h_ref[...] - t_ref[...], 0.0)
        dr, dgamma, dbeta = _layer_norm_bwd(r_ref[...], err * (1.0 / d), g_ref[...])
        dr_ref[...] = dr

        @pl.when(pl.program_id(0) == 0)
        def _():
            dgb_ref[...] = jnp.zeros_like(dgb_ref)
            l_ref[...] = jnp.zeros_like(l_ref)

        dgb_ref[0] += dgamma
        dgb_ref[1] += dbeta
        l_ref[...] += _row_partial(err * err) * (0.5 / d)

    row3 = pl.BlockSpec((None, tm, d), lambda i: (0, i, 0))
    return pl.pallas_call(
        body,
        name="loss_ln4_bwd",
        grid=(lp // tm,),
        in_specs=[row3, pl.BlockSpec((tm, d), lambda i: (i, 0)), row3, pl.BlockSpec((1, d), lambda i: (0, 0))],
        out_specs=[row3, pl.BlockSpec((2, 8, d), lambda i: (0, 0, 0)), pl.BlockSpec((8, d), lambda i: (0, 0))],
        out_shape=[jax.ShapeDtypeStruct((1, lp, d), F32), jax.ShapeDtypeStruct((2, 8, d), F32),
                   jax.ShapeDtypeStruct((8, d), F32)],
        compiler_params=_params(("arbitrary",)),
    )(h, target, r, g)


def _halo_index(tile, tm):
    return jnp.maximum(tile * (tm // HALO) - 1, 0)


def _conv_fwd(xs_ref, w, taps, tm):
    acc = w(0) * xs_ref[pl.ds(HALO - taps + 1, tm), :]
    for j in range(1, taps):
        acc += w(j) * xs_ref[pl.ds(HALO - taps + 1 + j, tm), :]
    return acc


def _conv_bwd_x(dcs_ref, w, taps, tm):
    acc = w(0) * dcs_ref[pl.ds(taps - 1, tm), :]
    for j in range(1, taps):
        acc += w(j) * dcs_ref[pl.ds(taps - 1 - j, tm), :]
    return acc


SUB = 8
LANES = 128
PAIR = 2 * SUB
STRIP_UNROLL = 2


def _pair_rows(r0):
    return pl.ds(r0, SUB), pl.ds(r0 + SUB if isinstance(r0, int) else pl.multiple_of(r0 + SUB, SUB), SUB)


def _shift_down(cur, prev, s):
    if s == 0:
        return cur
    row = lax.broadcasted_iota(jnp.int32, cur.shape, 0)
    return jnp.where(row < s, pltpu.roll(prev, s, axis=0), pltpu.roll(cur, s, axis=0))


def _shift_up(cur, nxt, s):
    if s == 0:
        return cur
    row = lax.broadcasted_iota(jnp.int32, cur.shape, 0)
    return jnp.where(row < SUB - s, pltpu.roll(cur, SUB - s, axis=0), pltpu.roll(nxt, SUB - s, axis=0))


def _silu_parts(c):
    sg = _sigmoid(c)
    return c * sg, sg * (1.0 + c * (1.0 - sg))


def _head_sum(x):
    rows, c = x.shape
    parts = []
    for h in range(c // HEAD_DIM):
        s = jnp.sum(x[:, h * HEAD_DIM:(h + 1) * HEAD_DIM], axis=-1, keepdims=True)
        parts.append(jnp.broadcast_to(s, (rows, HEAD_DIM)))
    return parts[0] if len(parts) == 1 else jnp.concatenate(parts, axis=-1)


def _log1p(y):
    u = 1.0 + y
    d = u - 1.0
    return jnp.where(d == 0.0, y, jnp.log(u) * (y / jnp.where(d == 0.0, 1.0, d)))


def _softplus(x):
    return jnp.maximum(x, 0.0) + _log1p(jnp.exp(-jnp.abs(x)))


def _gate_values(x, al, dt):
    lane = lax.broadcasted_iota(jnp.int32, x.shape, 1)
    is_beta, is_g = lane < HEADS, (lane >= HEADS) & (lane < 2 * HEADS)
    return _sigmoid(x), -jnp.exp(al) * _softplus(x + dt), is_beta, is_g


def gdn_gates_fwd(pba, al, dt, *, tm):
    _, lp, width = pba.shape

    def body(x_ref, al_ref, dt_ref, o_ref):
        beta, g, is_beta, is_g = _gate_values(x_ref[...], al_ref[...], dt_ref[...])
        o_ref[...] = jnp.where(is_beta, beta, jnp.where(is_g, g, 0.0))

    vec = pl.BlockSpec((1, width), lambda i: (0, 0))
    return pl.pallas_call(
        body,
        name="gdn_gates_fwd",
        grid=(lp // tm,),
        in_specs=[pl.BlockSpec((None, tm, width), lambda i: (0, i, 0)), vec, vec],
        out_specs=pl.BlockSpec((tm, width), lambda i: (i, 0)),
        out_shape=jax.ShapeDtypeStruct((lp, width), F32),
        compiler_params=_params(("arbitrary",)),
    )(pba, al, dt)


def gdn_gates_bwd(pba, dgates, al, dt, *, tm):
    _, lp, width = pba.shape

    def body(x_ref, d_ref, al_ref, dt_ref, dx_ref, dsc_ref):
        x = x_ref[...]
        beta, g, is_beta, is_g = _gate_values(x, al_ref[...], dt_ref[...])
        d = d_ref[...]
        dg = jnp.where(is_g, d, 0.0)
        da = dg * -jnp.exp(al_ref[...]) * _sigmoid(x + dt_ref[...])
        dx_ref[...] = jnp.where(is_beta, d * beta * (1.0 - beta), da).astype(dx_ref.dtype)

        @pl.when(pl.program_id(0) == 0)
        def _():
            dsc_ref[...] = jnp.zeros_like(dsc_ref)

        dsc_ref[0] += _row_partial(dg * g)
        dsc_ref[1] += _row_partial(da)

    vec = pl.BlockSpec((1, width), lambda i: (0, 0))
    return pl.pallas_call(
        body,
        name="gdn_gates_bwd",
        grid=(lp // tm,),
        in_specs=[pl.BlockSpec((None, tm, width), lambda i: (0, i, 0)), pl.BlockSpec((tm, width), lambda i: (i, 0)), vec, vec],
        out_specs=[pl.BlockSpec((None, tm, width), lambda i: (0, i, 0)), pl.BlockSpec((2, SUB, width), lambda i: (0, 0, 0))],
        out_shape=[jax.ShapeDtypeStruct((1, lp, width), BF16), jax.ShapeDtypeStruct((2, SUB, width), F32)],
        compiler_params=_params(("arbitrary",)),
    )(pba, dgates, al, dt)


def gdn_pre_fwd(p3, conv_w, *, tm, cb):
    _, lp, width = p3.shape
    taps = conv_w.shape[1]

    def body(x_ref, halo_ref, w_ref, o_ref, xs):
        i = pl.program_id(1)
        for s in range(3):
            xs[s, 0:HALO, :] = jnp.where(i > 0, halo_ref[s], 0.0)
            xs[s, HALO:, :] = x_ref[s]
            c = _conv_fwd(xs.at[s], lambda j, s=s: w_ref[s, j:j + 1, :], taps, tm)
            y, _ = _silu_parts(c)
            if s < 2:
                y = y * lax.rsqrt(_head_sum(y * y) + L2_EPS)
                if s == 0:
                    y = y * Q_SCALE
            o_ref[s] = y

    return pl.pallas_call(
        body,
        name="gdn_pre_fwd",
        grid=(width // cb, lp // tm),
        in_specs=[
            pl.BlockSpec((3, tm, cb), lambda j, i: (0, i, j)),
            pl.BlockSpec((3, HALO, cb), lambda j, i: (0, _halo_index(i, tm), j)),
            pl.BlockSpec((3, taps, cb), lambda j, i: (0, 0, j)),
        ],
        out_specs=pl.BlockSpec((3, tm, cb), lambda j, i: (0, i, j)),
        out_shape=jax.ShapeDtypeStruct((3, lp, width), F32),
        scratch_shapes=[pltpu.VMEM((3, tm + HALO, cb), F32)],
        compiler_params=_params(("arbitrary", "arbitrary")),
    )(p3, p3, conv_w)


def gdn_pre_bwd(p3, dqkv, conv_w, *, tm, cb):
    _, lp, width = p3.shape
    taps = conv_w.shape[1]
    last = lp // tm - 1

    def body(x_ref, halo_ref, d_ref, w_ref, dx_ref, dw_ref, xs, dcs, carry):
        step = pl.program_id(1)
        tile = last - step

        @pl.when(step == 0)
        def _():
            carry[...] = jnp.zeros_like(carry)
            dw_ref[...] = jnp.zeros_like(dw_ref)

        for s in range(3):
            w = lambda j, s=s: w_ref[s, j:j + 1, :]
            xs[s, 0:HALO, :] = jnp.where(tile > 0, halo_ref[s], 0.0)
            xs[s, HALO:, :] = x_ref[s]
            c = _conv_fwd(xs.at[s], w, taps, tm)
            y, dsilu = _silu_parts(c)
            dy = d_ref[s]
            if s < 2:
                rn = lax.rsqrt(_head_sum(y * y) + L2_EPS)
                yn = y * rn
                if s == 0:
                    dy = dy * Q_SCALE
                dy = rn * (dy - yn * _head_sum(dy * yn))
            dc = dy * dsilu
            dcs[s, 0:tm, :] = dc
            dcs[s, tm:, :] = carry[s]
            dx_ref[s] = _conv_bwd_x(dcs.at[s], w, taps, tm).astype(dx_ref.dtype)
            carry[s] = dc[0:HALO, :]
            for j in range(taps):
                dw_ref[s, j] += _row_partial(dc * xs[s, pl.ds(HALO - taps + 1 + j, tm), :])

    tile_spec = pl.BlockSpec((3, tm, cb), lambda j, i: (0, last - i, j))
    return pl.pallas_call(
        body,
        name="gdn_pre_bwd",
        grid=(width // cb, lp // tm),
        in_specs=[
            tile_spec,
            pl.BlockSpec((3, HALO, cb), lambda j, i: (0, _halo_index(last - i, tm), j)),
            tile_spec,
            pl.BlockSpec((3, taps, cb), lambda j, i: (0, 0, j)),
        ],
        out_specs=[tile_spec, pl.BlockSpec((3, taps, SUB, cb), lambda j, i: (0, 0, 0, j))],
        out_shape=[jax.ShapeDtypeStruct((3, lp, width), BF16), jax.ShapeDtypeStruct((3, taps, SUB, width), F32)],
        scratch_shapes=[
            pltpu.VMEM((3, tm + HALO, cb), F32),
            pltpu.VMEM((3, tm + HALO, cb), F32),
            pltpu.VMEM((3, HALO, cb), F32),
        ],
        compiler_params=_params(("arbitrary", "arbitrary")),
    )(p3, p3, dqkv, conv_w)


def gdn_post_fwd(o, z, nw_b, *, tm):
    _, lp, width = o.shape

    def body(o_ref, z_ref, nw_ref, y_ref):
        ov = o_ref[...]
        rn = lax.rsqrt(_head_sum(ov * ov) * (1.0 / HEAD_DIM) + RMS_EPS)
        gate, _ = _silu_parts(z_ref[...])
        y_ref[...] = (ov * rn * nw_ref[...] * gate).astype(y_ref.dtype)

    row = pl.BlockSpec((None, tm, width), lambda i: (0, i, 0))
    return pl.pallas_call(
        body,
        name="gdn_post_fwd",
        grid=(lp // tm,),
        in_specs=[row, row, pl.BlockSpec((1, width), lambda i: (0, 0))],
        out_specs=row,
        out_shape=jax.ShapeDtypeStruct((1, lp, width), BF16),
        compiler_params=_params(("arbitrary",)),
    )(o, z, nw_b)


def gdn_post_bwd(o, z, dy, nw_b, *, tm):
    _, lp, width = o.shape

    def body(o_ref, z_ref, dy_ref, nw_ref, do_ref, dz_ref, dnw_ref):
        ov = o_ref[...]
        rn = lax.rsqrt(_head_sum(ov * ov) * (1.0 / HEAD_DIM) + RMS_EPS)
        yn = ov * rn
        gate, dgate = _silu_parts(z_ref[...])
        d_on = dy_ref[...] * gate
        dz_ref[...] = (dy_ref[...] * yn * nw_ref[...] * dgate).astype(dz_ref.dtype)
        a = d_on * nw_ref[...]
        do_ref[...] = rn * (a - yn * (_head_sum(a * yn) * (1.0 / HEAD_DIM)))

        @pl.when(pl.program_id(0) == 0)
        def _():
            dnw_ref[...] = jnp.zeros_like(dnw_ref)

        dnw_ref[...] += _row_partial(d_on * yn)

    row = pl.BlockSpec((None, tm, width), lambda i: (0, i, 0))
    return pl.pallas_call(
        body,
        name="gdn_post_bwd",
        grid=(lp // tm,),
        in_specs=[row, row, row, pl.BlockSpec((1, width), lambda i: (0, 0))],
        out_specs=[row, row, pl.BlockSpec((8, width), lambda i: (0, 0))],
        out_shape=[jax.ShapeDtypeStruct((1, lp, width), F32), jax.ShapeDtypeStruct((1, lp, width), BF16),
                   jax.ShapeDtypeStruct((8, width), F32)],
        compiler_params=_params(("arbitrary",)),
    )(o, z, dy, nw_b)


def ffn_act_fwd(up, conv_w, *, tm, name):
    _, lp, c_w = up.shape
    taps = conv_w.shape[1]

    def body(u_ref, halo_ref, g_ref, w_ref, o_ref):
        first_tile = pl.program_id(1) == 0

        def strip(cur, prev, rows, cs):
            conv = w_ref[taps - 1:taps, cs] * cur
            for j in range(taps - 1):
                conv += w_ref[j:j + 1, cs] * _shift_down(cur, prev, taps - 1 - j)
            y, _ = _silu_parts(conv)
            return y * g_ref[rows, cs]

        def pair(r0, above_of):
            top, bot = _pair_rows(r0)
            for c0 in range(0, c_w, LANES):
                cs = slice(c0, c0 + LANES)
                cur_t, cur_b = u_ref[top, cs], u_ref[bot, cs]
                out = [strip(cur_t, above_of(cs), top, cs), strip(cur_b, cur_t, bot, cs)]
                o_ref[pl.ds(r0, PAIR), cs] = jnp.concatenate(out, axis=0).astype(o_ref.dtype)

        pair(0, lambda cs: jnp.where(first_tile, 0.0, halo_ref[:, cs]))

        def loop_body(s, carry):
            r0 = pl.multiple_of(s * PAIR, PAIR)
            pair(r0, lambda cs: u_ref[pl.ds(pl.multiple_of(r0 - SUB, SUB), SUB), cs])
            return carry

        lax.fori_loop(1, tm // PAIR, loop_body, 0, unroll=STRIP_UNROLL)

    return pl.pallas_call(
        body,
        name=name,
        grid=(2, lp // tm),
        in_specs=[
            pl.BlockSpec((None, tm, c_w), lambda s, i: (s, i, 0)),
            pl.BlockSpec((None, HALO, c_w), lambda s, i: (s, _halo_index(i, tm), 0)),
            pl.BlockSpec((None, tm, c_w), lambda s, i: (2 + s, i, 0)),
            pl.BlockSpec((None, taps, c_w), lambda s, i: (s, 0, 0)),
        ],
        out_specs=pl.BlockSpec((None, tm, c_w), lambda s, i: (s, i, 0)),
        out_shape=jax.ShapeDtypeStruct((2, lp, c_w), BF16),
        compiler_params=_params(("arbitrary", "arbitrary")),
    )(up, up, up, conv_w)


def ffn_act_bwd(up, dact, conv_w, *, tm, name):
    _, lp, c_w = up.shape
    taps = conv_w.shape[1]
    last = lp // tm - 1
    n_pairs = tm // PAIR

    def body(u_ref, halo_ref, g_ref, d_ref, w_ref, dup_ref, dw_ref, below):
        step = pl.program_id(1)
        first_tile = step == last

        @pl.when(step == 0)
        def _():
            below[...] = jnp.zeros_like(below)
            dw_ref[...] = jnp.zeros_like(dw_ref)

        def strip(cur, prev, rows, cs, nxt):
            shifted = [_shift_down(cur, prev, taps - 1 - j) for j in range(taps)]
            conv = w_ref[0:1, cs] * shifted[0]
            for j in range(1, taps):
                conv += w_ref[j:j + 1, cs] * shifted[j]
            y, dsilu = _silu_parts(conv)
            d = d_ref[rows, cs]
            dc = d * g_ref[rows, cs] * dsilu
            dx = w_ref[taps - 1:taps, cs] * dc
            for j in range(taps - 1):
                dx += w_ref[j:j + 1, cs] * _shift_up(dc, nxt, taps - 1 - j)
            return dx, d * y, dc, [dc * s for s in shifted]

        def pair(r0, above_of):
            top, bot = _pair_rows(r0)
            both = pl.ds(r0, PAIR)
            for c0 in range(0, c_w, LANES):
                cs = slice(c0, c0 + LANES)
                cur_t, cur_b = u_ref[top, cs], u_ref[bot, cs]
                dx_b, dg_b, dc_b, dw_b = strip(cur_b, cur_t, bot, cs, below[:, cs])
                dx_t, dg_t, dc_t, dw_t = strip(cur_t, above_of(cs), top, cs, dc_b)
                below[:, cs] = dc_t
                dup_ref[0, both, cs] = jnp.concatenate([dx_t, dx_b], axis=0).astype(dup_ref.dtype)
                dup_ref[1, both, cs] = jnp.concatenate([dg_t, dg_b], axis=0).astype(dup_ref.dtype)
                for j in range(taps):
                    dw_ref[j, :, cs] += dw_t[j] + dw_b[j]

        def loop_body(it, carry):
            r0 = pl.multiple_of((n_pairs - 1 - it) * PAIR, PAIR)
            pair(r0, lambda cs: u_ref[pl.ds(pl.multiple_of(r0 - SUB, SUB), SUB), cs])
            return carry

        lax.fori_loop(0, n_pairs - 1, loop_body, 0, unroll=STRIP_UNROLL)
        pair(0, lambda cs: jnp.where(first_tile, 0.0, halo_ref[:, cs]))

    return pl.pallas_call(
        body,
        name=name,
        grid=(2, lp // tm),
        in_specs=[
            pl.BlockSpec((None, tm, c_w), lambda s, i: (s, last - i, 0)),
            pl.BlockSpec((None, HALO, c_w), lambda s, i: (s, _halo_index(last - i, tm), 0)),
            pl.BlockSpec((None, tm, c_w), lambda s, i: (2 + s, last - i, 0)),
            pl.BlockSpec((None, tm, c_w), lambda s, i: (s, last - i, 0)),
            pl.BlockSpec((None, taps, c_w), lambda s, i: (s, 0, 0)),
        ],
        out_specs=[
            pl.BlockSpec((2, None, tm, c_w), lambda s, i: (0, s, last - i, 0)),
            pl.BlockSpec((None, taps, SUB, c_w), lambda s, i: (s, 0, 0, 0)),
        ],
        out_shape=[jax.ShapeDtypeStruct((2, 2, lp, c_w), BF16), jax.ShapeDtypeStruct((2, taps, SUB, c_w), F32)],
        scratch_shapes=[pltpu.VMEM((SUB, c_w), F32)],
        compiler_params=_params(("arbitrary", "arbitrary")),
    )(up, up, up, dact, conv_w)


def sc_fwd(pb, conv_w, *, tm, cb):
    _, lp, width = pb.shape
    taps = conv_w.shape[0]

    def body(x_ref, halo_ref, w_ref, o_ref):
        first_tile = pl.program_id(1) == 0

        def strip(cur, prev, rows, cs):
            conv = w_ref[taps - 1:taps, cs] * cur
            for j in range(taps - 1):
                conv += w_ref[j:j + 1, cs] * _shift_down(cur, prev, taps - 1 - j)
            return x_ref[0, rows, cs] * conv

        def pair(r0, above_of):
            top, bot = _pair_rows(r0)
            for c0 in range(0, cb, LANES):
                cs = slice(c0, c0 + LANES)
                cur_t = x_ref[1, top, cs] * x_ref[2, top, cs]
                cur_b = x_ref[1, bot, cs] * x_ref[2, bot, cs]
                out = [strip(cur_t, above_of(cs), top, cs), strip(cur_b, cur_t, bot, cs)]
                o_ref[pl.ds(r0, PAIR), cs] = jnp.concatenate(out, axis=0).astype(o_ref.dtype)

        pair(0, lambda cs: jnp.where(first_tile, 0.0, halo_ref[1, :, cs] * halo_ref[2, :, cs]))

        def loop_body(k, carry):
            r0 = pl.multiple_of(k * PAIR, PAIR)
            before = pl.ds(pl.multiple_of(r0 - SUB, SUB), SUB)
            pair(r0, lambda cs: x_ref[1, before, cs] * x_ref[2, before, cs])
            return carry

        lax.fori_loop(1, tm // PAIR, loop_body, 0, unroll=STRIP_UNROLL)

    return pl.pallas_call(
        body,
        name="sc_fwd",
        grid=(width // cb, lp // tm),
        in_specs=[
            pl.BlockSpec((3, tm, cb), lambda j, i: (0, i, j)),
            pl.BlockSpec((3, HALO, cb), lambda j, i: (0, _halo_index(i, tm), j)),
            pl.BlockSpec((taps, cb), lambda j, i: (0, j)),
        ],
        out_specs=pl.BlockSpec((None, tm, cb), lambda j, i: (0, i, j)),
        out_shape=jax.ShapeDtypeStruct((1, lp, width), BF16),
        compiler_params=_params(("arbitrary", "arbitrary")),
    )(pb, pb, conv_w)


def sc_bwd(pb, ds, conv_w, *, tm, cb):
    _, lp, width = pb.shape
    taps = conv_w.shape[0]
    last = lp // tm - 1
    n_pairs = tm // PAIR

    def body(x_ref, halo_ref, d_ref, w_ref, dx_ref, dw_ref, below):
        step = pl.program_id(1)
        first_tile = step == last

        @pl.when(step == 0)
        def _():
            below[...] = jnp.zeros_like(below)
            dw_ref[...] = jnp.zeros_like(dw_ref)

        def strip(cur, prev, rows, cs, nxt):
            gate, left, right = x_ref[0, rows, cs], x_ref[1, rows, cs], x_ref[2, rows, cs]
            shifted = [_shift_down(cur, prev, taps - 1 - j) for j in range(taps)]
            conv = w_ref[0:1, cs] * shifted[0]
            for j in range(1, taps):
                conv += w_ref[j:j + 1, cs] * shifted[j]
            d = d_ref[rows, cs]
            dc = d * gate
            dp = w_ref[taps - 1:taps, cs] * dc
            for j in range(taps - 1):
                dp += w_ref[j:j + 1, cs] * _shift_up(dc, nxt, taps - 1 - j)
            return d * conv, dp * right, dp * left, dc, [dc * s for s in shifted]

        def pair(r0, above_of):
            top, bot = _pair_rows(r0)
            both = pl.ds(r0, PAIR)
            for c0 in range(0, cb, LANES):
                cs = slice(c0, c0 + LANES)
                cur_t = x_ref[1, top, cs] * x_ref[2, top, cs]
                cur_b = x_ref[1, bot, cs] * x_ref[2, bot, cs]
                *dx_b, dc_b, dw_b = strip(cur_b, cur_t, bot, cs, below[:, cs])
                *dx_t, dc_t, dw_t = strip(cur_t, above_of(cs), top, cs, dc_b)
                below[:, cs] = dc_t
                for s in range(3):
                    dx_ref[s, both, cs] = jnp.concatenate([dx_t[s], dx_b[s]], axis=0).astype(dx_ref.dtype)
                for j in range(taps):
                    dw_ref[j, :, cs] += dw_t[j] + dw_b[j]

        def loop_body(it, carry):
            r0 = pl.multiple_of((n_pairs - 1 - it) * PAIR, PAIR)
            before = pl.ds(pl.multiple_of(r0 - SUB, SUB), SUB)
            pair(r0, lambda cs: x_ref[1, before, cs] * x_ref[2, before, cs])
            return carry

        lax.fori_loop(0, n_pairs - 1, loop_body, 0, unroll=STRIP_UNROLL)
        pair(0, lambda cs: jnp.where(first_tile, 0.0, halo_ref[1, :, cs] * halo_ref[2, :, cs]))

    tile_spec = pl.BlockSpec((3, tm, cb), lambda j, i: (0, last - i, j))
    return pl.pallas_call(
        body,
        name="sc_bwd",
        grid=(width // cb, lp // tm),
        in_specs=[
            tile_spec,
            pl.BlockSpec((3, HALO, cb), lambda j, i: (0, _halo_index(last - i, tm), j)),
            pl.BlockSpec((None, tm, cb), lambda j, i: (0, last - i, j)),
            pl.BlockSpec((taps, cb), lambda j, i: (0, j)),
        ],
        out_specs=[tile_spec, pl.BlockSpec((taps, SUB, cb), lambda j, i: (0, 0, j))],
        out_shape=[jax.ShapeDtypeStruct((3, lp, width), BF16), jax.ShapeDtypeStruct((taps, SUB, width), F32)],
        scratch_shapes=[pltpu.VMEM((SUB, cb), F32)],
        compiler_params=_params(("arbitrary", "arbitrary")),
    )(pb, pb, ds, conv_w)


TILE_BYTES = 1536 * 1024


def _rows_tile(rows, cols, multiple=8):
    if rows * cols * 4 <= TILE_BYTES or rows % multiple:
        return rows
    best = multiple
    for t in range(multiple, rows + 1, multiple):
        if rows % t == 0 and t * cols * 4 <= TILE_BYTES:
            best = t
    return best


def pair_sum(g, landed, core, out_dtype, name):
    _, rows, cols = g.shape
    half = rows // 2
    tr = _rows_tile(half, cols, 16)
    nb = half // tr

    def body(c_ref, g_ref, l_ref, o_ref):
        o_ref[...] = (g_ref[...] + l_ref[...]).astype(out_dtype)

    return pl.pallas_call(
        body,
        name=name,
        grid_spec=pltpu.PrefetchScalarGridSpec(
            num_scalar_prefetch=1,
            grid=(4, nb),
            in_specs=[
                pl.BlockSpec((None, tr, cols), lambda s, i, c: (s, c[0] * nb + i, 0)),
                pl.BlockSpec((None, tr, cols), lambda s, i, c: (s, i, 0)),
            ],
            out_specs=pl.BlockSpec((None, tr, cols), lambda s, i, c: (s, i, 0)),
        ),
        out_shape=jax.ShapeDtypeStruct((4, half, cols), out_dtype),
        compiler_params=_params(("arbitrary", "arbitrary")),
    )(core, g, landed)


def chip_sum(x, name):
    _, rows, cols = x.shape
    tr = _rows_tile(rows, cols, 16)

    def body(x0, x1, x2, x3, o_ref):
        acc = x0[...].astype(F32) + x1[...].astype(F32)
        o_ref[...] = (acc + x2[...].astype(F32)) + x3[...].astype(F32)

    return pl.pallas_call(
        body,
        name=name,
        grid=(rows // tr,),
        in_specs=[pl.BlockSpec((None, tr, cols), lambda i, k=k: (k, i, 0)) for k in range(4)],
        out_specs=pl.BlockSpec((tr, cols), lambda i: (i, 0)),
        out_shape=jax.ShapeDtypeStruct((rows, cols), F32),
        compiler_params=_params(("arbitrary",)),
    )(x, x, x, x)


def adamw(w, g, m, v, name):
    shape = w.shape
    cols = shape[-1]
    rows = w.size // cols
    tr = _rows_tile(rows, cols)

    def body(w_ref, g_ref, m_ref, v_ref, d_ref, m2_ref, v2_ref):
        gv = g_ref[...]
        m2 = ADAM_B1 * m_ref[...] + (1.0 - ADAM_B1) * gv
        v2 = ADAM_B2 * v_ref[...] + (1.0 - ADAM_B2) * (gv * gv)
        m_hat = m2 / (1.0 - ADAM_B1 ** ADAM_STEP)
        v_hat = v2 / (1.0 - ADAM_B2 ** ADAM_STEP)
        d_ref[...] = -ADAM_LR * (m_hat / (jnp.sqrt(v_hat) + ADAM_EPS) + ADAM_WD * w_ref[...])
        m2_ref[...] = m2
        v2_ref[...] = v2

    spec = pl.BlockSpec((tr, cols), lambda i: (i, 0))
    outs = pl.pallas_call(
        body,
        name=name,
        grid=(rows // tr,),
        in_specs=[spec] * 4,
        out_specs=[spec] * 3,
        out_shape=[jax.ShapeDtypeStruct((rows, cols), F32)] * 3,
        compiler_params=_params(("arbitrary",)),
    )(*[t.reshape(rows, cols) for t in (w, g, m, v)])
    return tuple(o.reshape(shape) for o in outs)


MESH_ID = pl.DeviceIdType.MESH
ANY = pl.BlockSpec(memory_space=pl.ANY)


def _place():
    x, y, c = lax.axis_index("x"), lax.axis_index("y"), lax.axis_index("c")
    other_chips = [(1 - x, y), (x, 1 - y), (1 - x, 1 - y)]
    return x, y, c, other_chips


def all_gather_shards(bufs, name):
    n = len(bufs)

    def body(*refs):
        x_refs, o_refs = refs[:n], refs[n:2 * n]
        copies = _gather_copies(x_refs, o_refs, *refs[2 * n:])
        _gather_start(copies)
        _gather_finish(copies)

    outs = pl.pallas_call(
        body,
        name=name,
        in_specs=[ANY] * n,
        out_specs=[ANY] * n,
        out_shape=_gather_out_shapes(bufs),
        scratch_shapes=_gather_sems(n),
    )(*bufs)
    return _set_own_slots(outs, bufs)


def _gather_out_shapes(bufs):
    return [jax.ShapeDtypeStruct((4,) + b.shape, b.dtype) for b in bufs]


def _gather_sems(n):
    return [pltpu.SemaphoreType.DMA((6 * n,)), pltpu.SemaphoreType.DMA((6 * n,))]


def _set_own_slots(outs, bufs):
    if not outs:
        return []
    me = 2 * lax.axis_index("x") + lax.axis_index("y")
    return [lax.dynamic_update_index_in_dim(o, b, me, 0) for o, b in zip(outs, bufs)]


def _gather_copies(x_refs, o_refs, send_sems, recv_sems):
    x, y, c, chips = _place()
    me = 2 * x + y
    sibling = (x, y, 1 - c)

    def part(a, slot, hf):
        half = x_refs[a].shape[0] // 2
        return o_refs[a].at[slot, pl.ds(hf * half, half), :]

    def mine(a):
        half = x_refs[a].shape[0] // 2
        return x_refs[a].at[pl.ds(c * half, half), :]

    def copy(k, src, dst, to):
        return pltpu.make_async_remote_copy(src_ref=src, dst_ref=dst, send_sem=send_sems.at[k],
                                            recv_sem=recv_sems.at[k], device_id=to, device_id_type=MESH_ID)

    sends, arrivals, passes, passed = [], [], [], []
    for a in range(len(x_refs)):
        for j, (px, py) in enumerate(chips):
            landed, theirs = part(a, 2 * px + py, c), part(a, 2 * px + py, 1 - c)
            sends.append(copy(6 * a + j, mine(a), part(a, me, c), (px, py, c)))
            arrivals.append(copy(6 * a + j, mine(a), landed, (px, py, c)))
            passes.append(copy(6 * a + 3 + j, landed, landed, sibling))
            passed.append(copy(6 * a + 3 + j, theirs, theirs, sibling))
    return sends, arrivals, passes, passed


def _gather_start(copies):
    for cp in copies[0]:
        cp.start()


def _gather_finish(copies):
    sends, arrivals, passes, passed = copies
    for arrival, cp in zip(arrivals, passes):
        arrival.wait_recv()
        cp.start()
    for cp in passed:
        cp.wait_recv()
    for cp in sends + passes:
        cp.wait_send()


def swap_halves(bufs, name):
    n = len(bufs)

    def body(*refs):
        copies = _swap_copies(refs[:n], refs[n:2 * n], *refs[2 * n:])
        _swap_start(copies)
        _swap_finish(copies)

    return pl.pallas_call(
        body,
        name=name,
        in_specs=[ANY] * n,
        out_specs=[ANY] * n,
        out_shape=_swap_out_shapes(bufs),
        scratch_shapes=_swap_sems(n),
    )(*bufs)


def _swap_out_shapes(bufs):
    return [jax.ShapeDtypeStruct((4, b.shape[1] // 2, b.shape[2]), b.dtype) for b in bufs]


def _swap_sems(n):
    return [pltpu.SemaphoreType.DMA((n,)), pltpu.SemaphoreType.DMA((n,))]


def _swap_copies(x_refs, o_refs, send_sems, recv_sems):
    x, y, c, _ = _place()
    copies = []
    for a, (x_ref, o_ref) in enumerate(zip(x_refs, o_refs)):
        half = x_ref.shape[1] // 2
        copies.append(pltpu.make_async_remote_copy(src_ref=x_ref.at[:, pl.ds((1 - c) * half, half), :], dst_ref=o_ref,
                                                   send_sem=send_sems.at[a], recv_sem=recv_sems.at[a],
                                                   device_id=(x, y, 1 - c), device_id_type=MESH_ID))
    return copies


def _swap_start(copies):
    for cp in copies:
        cp.start()


def _swap_finish(copies):
    for cp in copies:
        cp.wait()


def scatter_to_chips(bufs, name):
    n = len(bufs)

    def body(*refs):
        x_refs, o_refs = refs[:n], refs[n:2 * n]
        copies = _scatter_copies(x_refs, o_refs, *refs[2 * n:])
        _scatter_start(copies)
        _scatter_finish(copies)

    outs = pl.pallas_call(
        body,
        name=name,
        in_specs=[ANY] * n,
        out_specs=[ANY] * n,
        out_shape=[jax.ShapeDtypeStruct(b.shape, b.dtype) for b in bufs],
        scratch_shapes=_scatter_sems(n),
    )(*bufs)
    return _keep_own_slots(outs, bufs)


def _scatter_sems(n):
    return [pltpu.SemaphoreType.DMA((3 * n,)), pltpu.SemaphoreType.DMA((3 * n,))]


def _keep_own_slots(outs, bufs):
    if not outs:
        return []
    me = 2 * lax.axis_index("x") + lax.axis_index("y")
    return [lax.dynamic_update_index_in_dim(o, lax.dynamic_index_in_dim(b, me, 0, keepdims=False), me, 0)
            for o, b in zip(outs, bufs)]


def _scatter_copies(x_refs, o_refs, send_sems, recv_sems):
    x, y, c, chips = _place()
    me = 2 * x + y

    def copy(a, j, src_slot, dst_slot, px, py):
        return pltpu.make_async_remote_copy(src_ref=x_refs[a].at[src_slot], dst_ref=o_refs[a].at[dst_slot],
                                            send_sem=send_sems.at[3 * a + j], recv_sem=recv_sems.at[3 * a + j],
                                            device_id=(px, py, c), device_id_type=MESH_ID)

    sends = [copy(a, j, 2 * px + py, me, px, py) for a in range(len(x_refs)) for j, (px, py) in enumerate(chips)]
    arrivals = [copy(a, j, me, 2 * px + py, px, py) for a in range(len(x_refs)) for j, (px, py) in enumerate(chips)]
    return sends, arrivals


def _scatter_start(copies):
    for cp in copies[0]:
        cp.start()


def _scatter_finish(copies):
    for cp in copies[1]:
        cp.wait_recv()
    for cp in copies[0]:
        cp.wait_send()


def share_halves(groups, name):
    bufs = [b for grp in groups for b in grp]
    where = [(gi, li) for gi, grp in enumerate(groups) for li in range(len(grp))]
    n = len(bufs)

    def body(*refs):
        x_refs, o_refs = refs[:n], refs[n:n + len(groups)]
        send_sems, recv_sems = refs[n + len(groups):]
        x, y, c, _ = _place()
        sent, arrive = [], []
        for a, (gi, li) in enumerate(where):

            def copy(hf, a=a, gi=gi, li=li):
                return pltpu.make_async_remote_copy(src_ref=x_refs[a], dst_ref=o_refs[gi].at[li, hf],
                                                    send_sem=send_sems.at[a], recv_sem=recv_sems.at[a],
                                                    device_id=(x, y, 1 - c), device_id_type=MESH_ID)

            sent.append(copy(c))
            arrive.append(copy(1 - c))
        for cp in sent:
            cp.start()
        for cp in arrive:
            cp.wait_recv()
        for cp in sent:
            cp.wait_send()

    outs = pl.pallas_call(
        body,
        name=name,
        in_specs=[ANY] * n,
        out_specs=[ANY] * len(groups),
        out_shape=[jax.ShapeDtypeStruct((len(grp), 2) + grp[0].shape, grp[0].dtype) for grp in groups],
        scratch_shapes=[pltpu.SemaphoreType.DMA((n,)), pltpu.SemaphoreType.DMA((n,))],
    )(*bufs)
    c = lax.axis_index("c")
    full = [lax.dynamic_update_index_in_dim(o, jnp.stack(grp), c, 1) for o, grp in zip(outs, groups)]
    return [t.reshape(t.shape[0], 2 * t.shape[2], t.shape[3]) for t in full]


def pair_sums(bufs, landed, dtypes, tag):
    core = lax.axis_index("c").astype(jnp.int32).reshape(1)
    return [pair_sum(b, l, core, dt, "rs_pair_sum_%s%d" % (tag, i)) for i, (b, l, dt) in enumerate(zip(bufs, landed, dtypes))]


def _row_tiles(length):
    return (640, 320) if length > 2048 else (128, 64)


def _divisor_tile(rows, target):
    return max(t for t in range(8, min(rows, target) + 1, 8) if rows % t == 0)


def _local_step(x, target, wt, late_shards, layout_late, complete_grads, sum_pairs):
    seq, d = x.shape
    length = N_META + seq
    tm, tm_ffn = _row_tiles(length)
    lp = -(-length // tm) * tm
    tail = jnp.zeros((lp - length, d), F32)
    h0 = jnp.concatenate([wt["meta"], x, tail], axis=0)[None]
    tgt = jnp.concatenate([jnp.zeros((N_META, d), F32), target, tail], axis=0)
    nn = functools.partial(mm_nn, tm=_divisor_tile(lp, 1664))
    nt = functools.partial(mm_nt, tm=_divisor_tile(lp, 1040))
    tn = functools.partial(mm_tn, tm=_divisor_tile(lp, 1664), rb=256)
    nn_ln = functools.partial(mm_nn_ln, tm=_divisor_tile(lp, 832))
    nt_ln_bwd = functools.partial(mm_nt_ln_bwd, tm=_divisor_tile(lp, 832))
    ln_g = [wt["ln_mix_g"][0:1], wt["ln_ffn_g"][0:1], wt["ln_mix_g"][1:2], wt["ln_ffn_g"][1:2]]
    ln_b = [wt["ln_mix_b"][0:1], wt["ln_ffn_b"][0:1], wt["ln_mix_b"][1:2], wt["ln_ffn_b"][1:2]]

    p3 = nn(h0, wt["a3"], name="a_in3")
    pz = nn(h0, wt["az"], name="a_inz")
    pba = nn(h0, wt["a_ba"], name="a_inba")
    qkv = gdn_pre_fwd(p3, wt["a_conv3"], tm=tm, cb=2 * HEAD_DIM)
    gates = gdn_gates_fwd(pba, wt["alog_lanes"], wt["dtb_lanes"], tm=tm)
    o, states, tinv, late_stacks = gdn_chunk_fwd(qkv, gates, late_shards)
    wt = {**wt, **layout_late(late_stacks)}
    onz = gdn_post_fwd(o[None], pz, wt["anorm_b"], tm=tm)
    r1, h1 = nn_ln(onz, wt["a_out"], h0, ln_g[0], ln_b[0], name="a_out_ln1")
    up0 = nn(h1, wt["up"][0], name="up0")
    act0 = ffn_act_fwd(up0, wt["fconv"][0], tm=tm_ffn, name="ffn_act0")
    r2, h2 = nn_ln(act0, wt["down"][0], h1, ln_g[1], ln_b[1], name="down0_ln2")
    pb = nn(h2, wt["b_in"], name="b_in")
    sc = sc_fwd(pb, wt["b_conv"], tm=tm_ffn, cb=d)
    r3, h3 = nn_ln(sc, wt["b_out"], h2, ln_g[2], ln_b[2], name="b_out_ln3")
    up1 = nn(h3, wt["up"][1], name="up1")
    act1 = ffn_act_fwd(up1, wt["fconv"][1], tm=tm_ffn, name="ffn_act1")
    r4, h4 = nn_ln(act1, wt["down"][1], h3, ln_g[3], ln_b[3], name="down1_ln4")

    grads = {}
    dr4, dgb4, loss_part = loss_ln_bwd(h4, tgt, r4, ln_g[3], first=N_META, count=seq, tm=tm)
    d_down1 = tn(act1, dr4, name="d_down1")
    dact1 = nt(dr4, wt["down"][1], name="d_act1")
    dup1, dfconv1 = ffn_act_bwd(up1, dact1, wt["fconv"][1], tm=tm_ffn, name="ffn_act1_bwd")
    dup1 = dup1.reshape(up1.shape)
    d_up1 = tn(h3, dup1, name="d_up1")

    dr3, dgb3, _ = nt_ln_bwd(dup1, wt["up"][1], dr4, r3, ln_g[2], name="d_h3_ln3")
    d_bout = tn(sc, dr3, name="d_b_out")
    dsc = nt(dr3, wt["b_out"], name="d_sc")
    dpb, dbconv = sc_bwd(pb, dsc, wt["b_conv"], tm=tm_ffn, cb=d)
    d_bin = tn(h2, dpb, name="d_b_in")

    dr2, dgb2, _ = nt_ln_bwd(dpb, wt["b_in"], dr3, r2, ln_g[1], name="d_h2_ln2")
    d_down0 = tn(act0, dr2, name="d_down0")
    dact0 = nt(dr2, wt["down"][0], name="d_act0")
    dup0, dfconv0 = ffn_act_bwd(up0, dact0, wt["fconv"][0], tm=tm_ffn, name="ffn_act0_bwd")
    dup0 = dup0.reshape(up0.shape)
    d_up0 = tn(h1, dup0, name="d_up0")
    grads["b_w_in"] = [d_bin[0].transpose(1, 0, 2).reshape(d, 4, 3 * d // 4).transpose(1, 0, 2)]
    grads["b_w_out"] = [d_bout.reshape(4, d // 4, d)]
    grads["ffn_w_up"] = [d_up0[0], d_up1[0]]
    grads["ffn_w_down"] = [t.reshape(4, -1, d) for t in (d_down0, d_down1)]
    complete = complete_grads(grads)

    dr1, dgb1, from_sibling = nt_ln_bwd(dup0, wt["up"][0], dr2, r1, ln_g[0], name="d_h1_ln1", swap=complete)
    leaving = sum_pairs(complete, from_sibling)
    d_aout = tn(onz, dr1, name="d_a_out")
    donz = nt(dr1, wt["a_out"], name="d_onz")
    d_o, dz, dnw = gdn_post_bwd(o[None], pz, donz, wt["anorm_b"], tm=tm)
    dqkv, dgates, landed = gdn_chunk_bwd(qkv, gates, states, tinv, d_o[0], leaving)
    dp3, daconv = gdn_pre_bwd(p3, dqkv, wt["a_conv3"], tm=tm, cb=2 * HEAD_DIM)
    dpba, dscal = gdn_gates_bwd(pba, dgates, wt["alog_lanes"], wt["dtb_lanes"], tm=tm)
    d_a3 = tn(h0, dp3, name="d_a_in3")
    d_az = tn(h0, dz, name="d_a_inz")
    d_aba = tn(h0, dpba, name="d_a_inba")
    dh0 = nt(dp3, wt["a3"], res=dr1, res_scale=ALPHA, name="d_h0a")
    dh0 = nt(dz, wt["az"], res=dh0, res_scale=1.0, name="d_h0z")
    dh0 = nt(dpba, wt["a_ba"], res=dh0, res_scale=1.0, name="d_h0")

    width = HEADS * HEAD_DIM
    d_a_in = jnp.concatenate([d_a3[0, 0], d_a3[0, 1], d_a3[0, 2], d_az[0, 0], d_aba[0, 0][:, :2 * HEADS]], axis=1)
    n_in = d_a_in.shape[1] // 4
    grads["a_w_in"] = [d_a_in.reshape(d, 4, n_in).transpose(1, 0, 2)]
    grads["a_w_out"] = [d_aout.reshape(4, width // 4, d)]
    grads["a_conv"] = daconv.sum(axis=2).transpose(1, 0, 2).reshape(1, GDN_CONV, 3 * width)
    per_head = dscal.sum(axis=1)[:, HEADS:2 * HEADS]
    grads["a_log"] = per_head[0][None]
    grads["a_dt_bias"] = per_head[1][None]
    grads["a_norm"] = dnw.reshape(8, HEADS, HEAD_DIM).sum(axis=(0, 1))[None]
    grads["b_conv"] = dbconv.sum(axis=1)[None]
    lns = [dgb1, dgb2, dgb3, dgb4]
    grads["ln_mix_g"] = jnp.stack([lns[0][0].sum(0), lns[2][0].sum(0)])
    grads["ln_mix_b"] = jnp.stack([lns[0][1].sum(0), lns[2][1].sum(0)])
    grads["ln_ffn_g"] = jnp.stack([lns[1][0].sum(0), lns[3][0].sum(0)])
    grads["ln_ffn_b"] = jnp.stack([lns[1][1].sum(0), lns[3][1].sum(0)])
    grads["ffn_conv"] = jnp.stack([t.sum(axis=2).transpose(1, 0, 2).reshape(FFN_CONV, -1) for t in (dfconv0, dfconv1)])
    grads["meta"] = dh0[0, :N_META]
    return loss_part, dh0, grads, landed


WEIGHTS = ["meta", "a_w_in", "a_conv", "a_log", "a_dt_bias", "a_norm", "a_w_out", "b_w_in", "b_conv", "b_w_out",
           "ln_mix_g", "ln_mix_b", "ffn_w_up", "ffn_conv", "ffn_w_down", "ln_ffn_g", "ln_ffn_b"]
EARLY_WEIGHTS = ["a_w_in", "a_w_out"]
LATE_WEIGHTS = ["b_w_in", "b_w_out", "ffn_w_up", "ffn_w_down"]
MATMUL_WEIGHTS = EARLY_WEIGHTS + LATE_WEIGHTS
SMALL_SHARDED = ["a_conv", "b_conv", "ffn_conv", "meta"]
REPLICATED = ["a_log", "a_dt_bias", "a_norm", "ln_mix_g", "ln_mix_b", "ln_ffn_g", "ln_ffn_b"]
SHARD_AXIS = {"meta": 1, "a_w_in": 2, "a_conv": 2, "a_w_out": 1, "b_w_in": 2, "b_conv": 2, "b_w_out": 1,
              "ffn_w_up": 2, "ffn_conv": 2, "ffn_w_down": 1}
PACK_COLS = 1024
PACK_ROWS_MULTIPLE = 32


def _pack(pieces, lead=()):
    flat = jnp.concatenate([p.reshape(lead + (-1,)) for p in pieces], axis=-1)
    n = flat.shape[-1]
    rows = -(-n // (PACK_COLS * PACK_ROWS_MULTIPLE)) * PACK_ROWS_MULTIPLE
    flat = jnp.pad(flat, [(0, 0)] * len(lead) + [(0, rows * PACK_COLS - n)])
    return flat.reshape(lead + (rows, PACK_COLS))


def _unpack(buf, shapes, lead=()):
    flat = buf.reshape(lead + (-1,))
    out, off = [], 0
    for shp in shapes:
        n = 1
        for s in shp:
            n *= s
        out.append(flat[..., off:off + n].reshape(lead + tuple(shp)))
        off += n
    return out


def _join_shards(stacked, axis):
    return jnp.concatenate([stacked[k] for k in range(4)], axis=axis)


def _split_shards(full, axis):
    return jnp.stack(jnp.split(full, 4, axis=axis))


def _weight_layers(w, names):
    return [w[n][l].astype(BF16) for n in names for l in range(w[n].shape[0])]


def _per_weight(arrays, w, names):
    it = iter(arrays)
    return {n: [next(it) for _ in range(w[n].shape[0])] for n in names}


def _layout_early(full, w):
    width = HEADS * HEAD_DIM
    wt = {n: w[n] for n in ("ln_mix_g", "ln_mix_b", "ln_ffn_g", "ln_ffn_b")}
    w_in = _join_shards(full["a_w_in"][0], 1)
    d = w_in.shape[0]
    n_ff = full["ffn_conv"].shape[2] // 2
    blocks = [w_in[:, s * width:(s + 1) * width] for s in range(4)]
    wt["a3"] = jnp.stack(blocks[:3])[None]
    wt["az"] = blocks[3][None, None]
    wt["a_ba"] = jnp.pad(w_in[:, 4 * width:], ((0, 0), (0, HEAD_DIM - 2 * HEADS)))[None, None]
    wt["a_out"] = full["a_w_out"][0].reshape(1, 1, width, d)
    wt["a_conv3"] = full["a_conv"][0].reshape(GDN_CONV, 3, width).transpose(1, 0, 2)
    wt["b_conv"] = full["b_conv"][0]
    wt["fconv"] = [full["ffn_conv"][l].reshape(FFN_CONV, 2, n_ff).transpose(1, 0, 2) for l in range(2)]
    wt["meta"] = full["meta"]
    in_g_lanes = (HEADS, HEAD_DIM - 2 * HEADS)
    wt["alog_lanes"] = jnp.pad(w["a_log"][0], in_g_lanes)[None]
    wt["dtb_lanes"] = jnp.pad(w["a_dt_bias"][0], in_g_lanes)[None]
    wt["anorm_b"] = jnp.tile(w["a_norm"][0], HEADS)[None]
    return wt


def _layout_late(full):
    d = full["b_w_in"][0].shape[1]
    n_ff = full["ffn_w_up"][0].shape[2]
    return {
        "b_in": _join_shards(full["b_w_in"][0], 1).reshape(d, 3, d).transpose(1, 0, 2)[None],
        "b_out": full["b_w_out"][0].reshape(1, 1, d, d),
        "up": [t[None] for t in full["ffn_w_up"]],
        "down": [t.reshape(2, 1, n_ff, d) for t in full["ffn_w_down"]],
    }


def kernel(x, meta, a_w_in, a_conv, a_log, a_dt_bias, a_norm, a_w_out, b_w_in, b_conv, b_w_out, ln_mix_g, ln_mix_b, ffn_w_up, ffn_conv, ffn_w_down, ln_ffn_g, ln_ffn_b, loss_target, m_meta, m_a_w_in, m_a_conv, m_a_log, m_a_dt_bias, m_a_norm, m_a_w_out, m_b_w_in, m_b_conv, m_b_w_out, m_ln_mix_g, m_ln_mix_b, m_ffn_w_up, m_ffn_conv, m_ffn_w_down, m_ln_ffn_g, m_ln_ffn_b, v_meta, v_a_w_in, v_a_conv, v_a_log, v_a_dt_bias, v_a_norm, v_a_w_out, v_b_w_in, v_b_conv, v_b_w_out, v_ln_mix_g, v_ln_mix_b, v_ffn_w_up, v_ffn_conv, v_ffn_w_down, v_ln_ffn_g, v_ln_ffn_b):
    w = dict(meta=meta, a_w_in=a_w_in, a_conv=a_conv, a_log=a_log, a_dt_bias=a_dt_bias, a_norm=a_norm, a_w_out=a_w_out,
             b_w_in=b_w_in, b_conv=b_conv, b_w_out=b_w_out, ln_mix_g=ln_mix_g, ln_mix_b=ln_mix_b, ffn_w_up=ffn_w_up,
             ffn_conv=ffn_conv, ffn_w_down=ffn_w_down, ln_ffn_g=ln_ffn_g, ln_ffn_b=ln_ffn_b)
    m = dict(meta=m_meta, a_w_in=m_a_w_in, a_conv=m_a_conv, a_log=m_a_log, a_dt_bias=m_a_dt_bias, a_norm=m_a_norm,
             a_w_out=m_a_w_out, b_w_in=m_b_w_in, b_conv=m_b_conv, b_w_out=m_b_w_out, ln_mix_g=m_ln_mix_g,
             ln_mix_b=m_ln_mix_b, ffn_w_up=m_ffn_w_up, ffn_conv=m_ffn_conv, ffn_w_down=m_ffn_w_down,
             ln_ffn_g=m_ln_ffn_g, ln_ffn_b=m_ln_ffn_b)
    v = dict(meta=v_meta, a_w_in=v_a_w_in, a_conv=v_a_conv, a_log=v_a_log, a_dt_bias=v_a_dt_bias, a_norm=v_a_norm,
             a_w_out=v_a_w_out, b_w_in=v_b_w_in, b_conv=v_b_conv, b_w_out=v_b_w_out, ln_mix_g=v_ln_mix_g,
             ln_mix_b=v_ln_mix_b, ffn_w_up=v_ffn_w_up, ffn_conv=v_ffn_conv, ffn_w_down=v_ffn_w_down,
             ln_ffn_g=v_ln_ffn_g, ln_ffn_b=v_ln_ffn_b)
    seq = x.shape[1]
    *stacks, small = all_gather_shards(_weight_layers(w, EARLY_WEIGHTS) + [_pack([w[n] for n in SMALL_SHARDED])],
                                       "gather_early")
    full = _per_weight(stacks, w, EARLY_WEIGHTS)
    for n, t in zip(SMALL_SHARDED, _unpack(small, [w[n].shape for n in SMALL_SHARDED], lead=(4,))):
        full[n] = _join_shards(t, SHARD_AXIS[n])

    def layout_late(late_stacks):
        return _layout_late(_per_weight(late_stacks, w, LATE_WEIGHTS))

    def complete_grads(grads):
        return [g for n in LATE_WEIGHTS for g in grads[n]]

    def sum_pairs(bufs, from_sibling):
        return pair_sums(bufs, from_sibling, [BF16] * len(bufs), "late")

    loss_part, dh0, grads, landed_late = _local_step(x[0], loss_target[0], _layout_early(full, w),
                                                     _weight_layers(w, LATE_WEIGHTS), layout_late, complete_grads, sum_pairs)
    pieces = [_split_shards(grads[n], SHARD_AXIS[n]) for n in SMALL_SHARDED]
    same = jnp.concatenate([grads[n].reshape(-1) for n in REPLICATED] + [jnp.sum(loss_part).reshape(1)])
    pieces.append(jnp.broadcast_to(same, (4,) + same.shape))
    bufs = [g for n in EARLY_WEIGHTS for g in grads[n]] + [_pack(pieces, lead=(4,))]
    from_sibling = swap_halves(bufs, "rs_pair_early")
    landed = scatter_to_chips(pair_sums(bufs, from_sibling, [BF16] * (len(bufs) - 1) + [F32], "early"), "rs_chips_early")
    totals = [chip_sum(t, "rs_chip_sum%d" % i) for i, t in enumerate(landed + landed_late)]
    by_weight = _per_weight(totals[:len(bufs) - 1] + totals[len(bufs):], w, MATMUL_WEIGHTS)
    *shared, small_total = share_halves([by_weight[n] for n in MATMUL_WEIGHTS] + [[totals[len(bufs) - 1]]], "rs_share")
    grad_w = {n: t.reshape(w[n].shape) for n, t in zip(MATMUL_WEIGHTS, shared)}
    rest = SMALL_SHARDED + REPLICATED
    unpacked = _unpack(small_total[0], [w[n].shape for n in rest] + [()])
    grad_w.update(zip(rest, unpacked[:-1]))
    loss = unpacked[-1]
    grad_x = dh0[:, N_META:N_META + seq]
    steps = [adamw(w[n], grad_w[n], m[n], v[n], "adamw_" + n) for n in WEIGHTS]
    return (loss, grad_x, *[grad_w[n] for n in WEIGHTS], *[s[0] for s in steps], *[s[1] for s in steps],
            *[s[2] for s in steps])
```

```python
import functools

import jax
import jax.numpy as jnp
from jax import lax
from jax.experimental import pallas as pl
from jax.experimental.pallas import tpu as pltpu

F32 = jnp.float32
BF16 = jnp.bfloat16
HI = lax.Precision.HIGHEST

N_META = 16
HEADS = 8
HEAD_DIM = 128
CHUNK = 64
GDN_CONV = 4
SC_CONV = 3
FFN_CONV = 3
ALPHA = 4.0 ** 0.25
LN_EPS = 1e-5
RMS_EPS = 1e-6
L2_EPS = 1e-6
Q_SCALE = HEAD_DIM ** -0.5

ADAM_LR = 0.001
ADAM_B1 = 0.9
ADAM_B2 = 0.999
ADAM_EPS = 1e-08
ADAM_WD = 0.01
ADAM_STEP = 10

HALO = 8
VMEM_LIMIT = 48 * 1024 * 1024


def _params(sem=None):
    return pltpu.CompilerParams(dimension_semantics=sem, vmem_limit_bytes=VMEM_LIMIT)


def _dot(a, b, prec=None):
    return jnp.dot(a, b, preferred_element_type=F32, precision=prec)


def _dot_nt(a, b, prec=None):
    return lax.dot_general(a, b, (((1,), (1,)), ((), ())), preferred_element_type=F32, precision=prec)


def _dot_tn(a, b, prec=None):
    return lax.dot_general(a, b, (((0,), (0,)), ((), ())), preferred_element_type=F32, precision=prec)


def _sigmoid(x):
    return 0.5 * jnp.tanh(0.5 * x) + 0.5


def _tri_masks():
    r = lax.broadcasted_iota(jnp.int32, (CHUNK, CHUNK), 0)
    c = lax.broadcasted_iota(jnp.int32, (CHUNK, CHUNK), 1)
    return r >= c, r > c, r == c


def _split_hi_lo(x):
    hi = x.astype(BF16)
    return hi, (x - hi.astype(F32)).astype(BF16)


def _mask_dot(mask, x):
    hi, lo = _split_hi_lo(x)
    return _dot(mask, hi) + _dot(mask, lo)


def _cumsum_rows(g):
    causal, _, _ = _tri_masks()
    return _mask_dot(causal.astype(BF16), g)


def _cumsum_rows_transposed(dy):
    _, strict, _ = _tri_masks()
    return _mask_dot((~strict).astype(BF16), dy)


def _dot_split3(a, b):
    a_hi, a_lo = _split_hi_lo(a)
    b_hi, b_lo = _split_hi_lo(b)
    return _dot(a_hi, b_hi) + (_dot(a_hi, b_lo) + _dot(a_lo, b_hi))


@jax.custom_vjp
def _dot_precise(a, b):
    return _dot_split3(a, b)


def _dot_precise_fwd(a, b):
    return _dot_split3(a, b), (a, b)


def _dot_precise_bwd(operands, ct):
    a, b = operands
    return _dot_split3(ct, b.T), _dot_split3(a.T, ct)


_dot_precise.defvjp(_dot_precise_fwd, _dot_precise_bwd)


def _gdn_m(ks, a64s, bbs):
    causal, strict, _ = _tri_masks()
    decay = [jnp.exp(jnp.where(causal, x - x.T, -1e30)) for x in a64s]
    kk = [_dot_nt(k * b, k) for k, b in zip(ks, bbs)]
    return [jnp.where(strict, x * d, 0.0) for x, d in zip(kk, decay)]


def _gdn_inverse_stages(ks, a64s, bbs):
    ms = _gdn_m(ks, a64s, bbs)
    yield
    r = lax.broadcasted_iota(jnp.int32, (CHUNK, CHUNK), 0)
    c = lax.broadcasted_iota(jnp.int32, (CHUNK, CHUNK), 1)
    eye = (r == c).astype(F32)
    same = [jnp.right_shift(r, s) == jnp.right_shift(c, s) for s in (3, 4, 5)]
    d = [jnp.where(same[0], m, 0.0) for m in ms]
    p = [_dot(x, x) for x in d]
    yield
    t = [eye - x for x in d]
    t = [x + _dot(x, y) for x, y in zip(t, p)]
    p = [_dot(x, x) for x in p]
    yield
    t = [x + _dot(x, y) for x, y in zip(t, p)]
    yield
    for inner, outer in ((same[0], same[1]), (same[1], same[2]), (same[2], None)):
        joins = ~inner if outer is None else (outer & ~inner)
        o = [_dot(x, jnp.where(joins, m, 0.0)) for x, m in zip(t, ms)]
        yield
        t = [x - _dot(y, x) for x, y in zip(t, o)]
        yield
    res = [eye - x - _dot_split3(m, x) for m, x in zip(ms, t)]
    yield
    return [x + _dot(x, y) for x, y in zip(t, res)]


def _gdn_apply_stages(qs, ks, vs, gc, a64s, gl, bbs, ss, ts):
    causal, _, _ = _tri_masks()
    n = range(len(qs))
    qk = [_dot_nt(qs[h], ks[h]) for h in n]
    yield
    decay = [jnp.exp(jnp.where(causal, x - x.T, -1e30)) for x in a64s]
    eg = [jnp.exp(x) for x in gc]
    u = [_dot_precise(ts[h], vs[h] * bbs[h]) for h in n]
    w = [_dot_precise(ts[h], ks[h] * bbs[h] * eg[h]) for h in n]
    qk = [qk[h] * decay[h] for h in n]
    kd = [ks[h] * jnp.exp(gl[h] - gc[h]) for h in n]
    yield
    v_new = [u[h] - _dot(w[h], ss[h]) for h in n]
    q_s = [_dot(qs[h] * eg[h], ss[h]) for h in n]
    yield
    o = [q_s[h] + _dot(qk[h], v_new[h]) for h in n]
    s2 = [ss[h] * jnp.exp(gl[h]) + _dot_tn(kd[h], v_new[h]) for h in n]
    return o, s2


def _run_stages(*generators):
    results = [None] * len(generators)
    live = dict(enumerate(generators))
    while live:
        for i, gen in list(live.items()):
            try:
                next(gen)
            except StopIteration as stop:
                results[i] = stop.value
                del live[i]
    return results


def _head_slices(h):
    return slice(h * HEAD_DIM, (h + 1) * HEAD_DIM), slice(h * HEAD_DIM, h * HEAD_DIM + CHUNK)


def _gdn_head_values(x_ref, gate_ref):
    heads = range(HEADS)
    qs, ks, vs = ([x_ref[s, :, _head_slices(h)[0]] for h in heads] for s in range(3))
    gate = gate_ref[...]
    cumulative = _cumsum_rows(gate)
    total = jnp.sum(gate, axis=0, keepdims=True)
    gcums = [cumulative[:, HEADS + h:HEADS + h + 1] for h in heads]
    gtots = [total[:, HEADS + h:HEADS + h + 1] for h in heads]
    bcols = [gate[:, h:h + 1] for h in heads]
    return qs, ks, vs, gcums, gtots, bcols


def _over_lanes(cols, lanes):
    return [jnp.broadcast_to(c, (c.shape[0], lanes)) for c in cols]


def _gdn_inverse_cols(ks, gcums, bcols):
    return _gdn_inverse_stages(ks, _over_lanes(gcums, CHUNK), _over_lanes(bcols, HEAD_DIM))


def _gdn_apply_cols_stages(qs, ks, vs, gcums, gtots, bcols, ss, ts):
    return _gdn_apply_stages(qs, ks, vs, _over_lanes(gcums, HEAD_DIM), _over_lanes(gcums, CHUNK),
                             _over_lanes(gtots, HEAD_DIM), _over_lanes(bcols, HEAD_DIM), ss, ts)


def _gdn_apply_cols(qs, ks, vs, gcums, gtots, bcols, ss, ts):
    return _run_stages(_gdn_apply_cols_stages(qs, ks, vs, gcums, gtots, bcols, ss, ts))[0]


def _gdn_m_cols(ks, gcums, bcols):
    return _gdn_m(ks, _over_lanes(gcums, CHUNK), _over_lanes(bcols, HEAD_DIM))


def _gate_lanes(bcols, gcols):
    rows = gcols[0].shape[0]
    lane = lax.broadcasted_iota(jnp.int32, (rows, HEAD_DIM), 1)
    out = jnp.zeros((rows, HEAD_DIM), F32)
    for h in range(HEADS):
        if bcols is not None:
            out = jnp.where(lane == h, jnp.broadcast_to(bcols[h], out.shape), out)
        out = jnp.where(lane == HEADS + h, jnp.broadcast_to(gcols[h], out.shape), out)
    return out


def _gate_gradient(dbcols, dgcums, dgtots):
    block = _gate_lanes(dbcols, dgcums)
    lane = lax.broadcasted_iota(jnp.int32, block.shape, 1)
    return jnp.where(lane < HEADS, block, _cumsum_rows_transposed(block) + _gate_lanes(None, dgtots))


def gdn_chunk_fwd(qkv, gates, gather=()):
    _, lp, width = qkv.shape
    n_chunks = lp // CHUNK
    n = len(gather)

    def body(x_ref, gate_ref, next_ref, next_gate_ref, *refs):
        shard_refs, (o_ref, s_ref, t_ref), refs = refs[:n], refs[n:n + 3], refs[n + 3:]
        stack_refs, state, t_next, sems = refs[:n], refs[n], refs[n + 1], refs[n + 2:]
        copies = _gather_copies(shard_refs, stack_refs, *sems) if n else None

        def inverse_stages(ref, g_ref):
            _, ks, _, gcums, _, bcols = _gdn_head_values(ref, g_ref)
            return _gdn_inverse_cols(ks, gcums, bcols)

        @pl.when(pl.program_id(0) == 0)
        def _():
            state[...] = jnp.zeros_like(state)
            for h, t in enumerate(_run_stages(inverse_stages(x_ref, gate_ref))[0]):
                t_next[h] = t
            if n:
                _gather_start(copies)

        qs, ks, vs, gcums, gtots, bcols = _gdn_head_values(x_ref, gate_ref)
        ss = [state[h] for h in range(HEADS)]
        ts = [t_next[h] for h in range(HEADS)]
        ts_next, (os_, s2) = _run_stages(inverse_stages(next_ref, next_gate_ref),
                                         _gdn_apply_cols_stages(qs, ks, vs, gcums, gtots, bcols, ss, ts))
        for h in range(HEADS):
            s_ref[0, h] = ss[h]
            t_ref[0, h] = ts[h]
            t_next[h] = ts_next[h]
            o_ref[:, _head_slices(h)[0]] = os_[h]
            state[h] = s2[h]

        if n:
            @pl.when(pl.program_id(0) == n_chunks - 1)
            def _():
                _gather_finish(copies)

    o, states, tinv, *stacks = pl.pallas_call(
        body,
        name="gdn_chunk_fwd",
        grid=(n_chunks,),
        in_specs=[pl.BlockSpec((3, CHUNK, width), lambda c: (0, c, 0)),
                  pl.BlockSpec((CHUNK, HEAD_DIM), lambda c: (c, 0)),
                  pl.BlockSpec((3, CHUNK, width), lambda c: (0, jnp.minimum(c + 1, n_chunks - 1), 0)),
                  pl.BlockSpec((CHUNK, HEAD_DIM), lambda c: (jnp.minimum(c + 1, n_chunks - 1), 0))] + [ANY] * n,
        out_specs=[
            pl.BlockSpec((CHUNK, width), lambda c: (c, 0)),
            pl.BlockSpec((1, HEADS, HEAD_DIM, HEAD_DIM), lambda c: (c, 0, 0, 0)),
            pl.BlockSpec((1, HEADS, CHUNK, CHUNK), lambda c: (c, 0, 0, 0)),
        ] + [ANY] * n,
        out_shape=[
            jax.ShapeDtypeStruct((lp, width), F32),
            jax.ShapeDtypeStruct((n_chunks, HEADS, HEAD_DIM, HEAD_DIM), F32),
            jax.ShapeDtypeStruct((n_chunks, HEADS, CHUNK, CHUNK), F32),
        ] + _gather_out_shapes(gather),
        scratch_shapes=[pltpu.VMEM((HEADS, HEAD_DIM, HEAD_DIM), F32), pltpu.VMEM((HEADS, CHUNK, CHUNK), F32)]
        + (_gather_sems(n) if n else []),
        compiler_params=_params(("arbitrary",)),
    )(qkv, gates, qkv, gates, *gather)
    return o, states, tinv, _set_own_slots(stacks, gather)


def gdn_chunk_bwd(qkv, gates, states, tinv, d_o, scatter=()):
    _, lp, width = qkv.shape
    n_chunks = lp // CHUNK
    last = n_chunks - 1
    n = len(scatter)

    def body(x_ref, gate_ref, s_ref, t_ref, do_ref, *refs):
        leaving_refs, dx_ref, dgate_ref, refs = refs[:n], refs[n], refs[n + 1], refs[n + 2:]
        landing_refs, dstate, sems = refs[:n], refs[n], refs[n + 1:]
        copies = _scatter_copies(leaving_refs, landing_refs, *sems) if n else None

        @pl.when(pl.program_id(0) == 0)
        def _():
            dstate[...] = jnp.zeros_like(dstate)
            if n:
                _scatter_start(copies)

        heads = range(HEADS)
        qs, ks, vs, gcums, gtots, bcols = _gdn_head_values(x_ref, gate_ref)
        ss = [s_ref[0, h] for h in heads]
        ts = [t_ref[0, h] for h in heads]
        d_out = ([do_ref[:, _head_slices(h)[0]] for h in heads], [dstate[h] for h in heads])
        _, vjp_apply = jax.vjp(_gdn_apply_cols, qs, ks, vs, gcums, gtots, bcols, ss, ts)
        dq, dk, dv, dgc, dgt, db, ds, dt = vjp_apply(d_out)
        tts = [t.T for t in ts]
        dm = [_dot(tts[h], dt[h]) for h in heads]
        dm = [-_dot(dm[h], tts[h]) for h in heads]
        _, vjp_m = jax.vjp(_gdn_m_cols, ks, gcums, bcols)
        dk2, dgc2, db2 = vjp_m(dm)
        for h in heads:
            sl = _head_slices(h)[0]
            dx_ref[0, :, sl] = dq[h]
            dx_ref[1, :, sl] = dk[h] + dk2[h]
            dx_ref[2, :, sl] = dv[h]
            dstate[h] = ds[h]
        dgate_ref[...] = _gate_gradient([db[h] + db2[h] for h in heads], [dgc[h] + dgc2[h] for h in heads], dgt)

        if n:
            @pl.when(pl.program_id(0) == n_chunks - 1)
            def _():
                _scatter_finish(copies)

    dqkv, dgates, *landed = pl.pallas_call(
        body,
        name="gdn_chunk_bwd",
        grid=(n_chunks,),
        in_specs=[
            pl.BlockSpec((3, CHUNK, width), lambda c: (0, last - c, 0)),
            pl.BlockSpec((CHUNK, HEAD_DIM), lambda c: (last - c, 0)),
            pl.BlockSpec((1, HEADS, HEAD_DIM, HEAD_DIM), lambda c: (last - c, 0, 0, 0)),
            pl.BlockSpec((1, HEADS, CHUNK, CHUNK), lambda c: (last - c, 0, 0, 0)),
            pl.BlockSpec((CHUNK, width), lambda c: (last - c, 0)),
        ] + [ANY] * n,
        out_specs=[pl.BlockSpec((3, CHUNK, width), lambda c: (0, last - c, 0)),
                   pl.BlockSpec((CHUNK, HEAD_DIM), lambda c: (last - c, 0))] + [ANY] * n,
        out_shape=[jax.ShapeDtypeStruct(qkv.shape, F32), jax.ShapeDtypeStruct(gates.shape, F32)]
        + [jax.ShapeDtypeStruct(b.shape, b.dtype) for b in scatter],
        scratch_shapes=[pltpu.VMEM((HEADS, HEAD_DIM, HEAD_DIM), F32)] + (_scatter_sems(n) if n else []),
        compiler_params=_params(("arbitrary",)),
    )(qkv, gates, states, tinv, d_o, *scatter)
    return dqkv, dgates, _keep_own_slots(landed, scatter)


def mm_nn(a, b, *, tm, name):
    ks, m, tk = a.shape
    _, ns, _, tn = b.shape

    def body(a_ref, b_ref, o_ref):
        p = _dot(a_ref[...].astype(BF16), b_ref[...])

        @pl.when(pl.program_id(2) == 0)
        def _():
            o_ref[...] = p

        @pl.when(pl.program_id(2) > 0)
        def _():
            o_ref[...] += p

    return pl.pallas_call(
        body,
        name=name,
        grid=(ns, m // tm, ks),
        in_specs=[
            pl.BlockSpec((None, tm, tk), lambda n, i, k: (k, i, 0)),
            pl.BlockSpec((None, None, tk, tn), lambda n, i, k: (k, n, 0, 0)),
        ],
        out_specs=pl.BlockSpec((None, tm, tn), lambda n, i, k: (n, i, 0)),
        out_shape=jax.ShapeDtypeStruct((ns, m, tn), F32),
        compiler_params=_params(("arbitrary", "arbitrary", "arbitrary")),
    )(a, b)


def mm_nt(dy, w, *, tm, name, res=None, res_scale=1.0):
    ns, m, tn = dy.shape
    ks, _, tk, _ = w.shape

    def body(*refs):
        if res is None:
            dy_ref, w_ref, o_ref = refs
        else:
            dy_ref, w_ref, r_ref, o_ref = refs
        p = _dot_nt(dy_ref[...].astype(BF16), w_ref[...])

        @pl.when(pl.program_id(2) == 0)
        def _():
            o_ref[...] = p if res is None else p + res_scale * r_ref[...]

        @pl.when(pl.program_id(2) > 0)
        def _():
            o_ref[...] += p

    in_specs = [
        pl.BlockSpec((None, tm, tn), lambda k, i, n: (n, i, 0)),
        pl.BlockSpec((None, None, tk, tn), lambda k, i, n: (k, n, 0, 0)),
    ]
    args = [dy, w]
    if res is not None:
        in_specs.append(pl.BlockSpec((None, tm, tk), lambda k, i, n: (k, i, 0)))
        args.append(res)
    return pl.pallas_call(
        body,
        name=name,
        grid=(ks, m // tm, ns),
        in_specs=in_specs,
        out_specs=pl.BlockSpec((None, tm, tk), lambda k, i, n: (k, i, 0)),
        out_shape=jax.ShapeDtypeStruct((ks, m, tk), F32),
        compiler_params=_params(("arbitrary", "arbitrary", "arbitrary")),
    )(*args)


def mm_tn(x, dy, *, tm, name, rb=None):
    ks, m, tk = x.shape
    ns, _, tn = dy.shape
    rb = tk if rb is None else rb

    def body(x_ref, dy_ref, o_ref):
        @pl.when(pl.program_id(2) == 0)
        def _():
            o_ref[...] = jnp.zeros_like(o_ref)

        dyb = dy_ref[...].astype(BF16)
        for r in range(0, tk, rb):
            o_ref[r:r + rb, :] += _dot_tn(x_ref[:, r:r + rb].astype(BF16), dyb)

    return pl.pallas_call(
        body,
        name=name,
        grid=(ks, ns, m // tm),
        in_specs=[
            pl.BlockSpec((None, tm, tk), lambda k, n, i: (k, i, 0)),
            pl.BlockSpec((None, tm, tn), lambda k, n, i: (n, i, 0)),
        ],
        out_specs=pl.BlockSpec((None, None, tk, tn), lambda k, n, i: (k, n, 0, 0)),
        out_shape=jax.ShapeDtypeStruct((ks, ns, tk, tn), F32),
        compiler_params=_params(("arbitrary", "arbitrary", "arbitrary")),
    )(x, dy)


def _row_partial(x):
    rows, c = x.shape
    return jnp.sum(x.reshape(rows // 8, 8, c), axis=0)


def _layer_norm(r, g, b):
    mu = jnp.mean(r, axis=-1, keepdims=True)
    xc = r - mu
    var = jnp.mean(xc * xc, axis=-1, keepdims=True)
    return xc * lax.rsqrt(var + LN_EPS) * g + b


def _layer_norm_bwd(x, dh, g):
    mu = jnp.mean(x, axis=-1, keepdims=True)
    xc = x - mu
    rstd = lax.rsqrt(jnp.mean(xc * xc, axis=-1, keepdims=True) + LN_EPS)
    xh = xc * rstd
    dxh = dh * g
    m1 = jnp.mean(dxh, axis=-1, keepdims=True)
    m2 = jnp.mean(dxh * xh, axis=-1, keepdims=True)
    return rstd * (dxh - m1 - xh * m2), _row_partial(dh * xh), _row_partial(dh)


def mm_nn_ln(a, b, h_prev, g, beta, *, tm, name):
    ks, m, tk = a.shape
    d = b.shape[3]

    def body(a_ref, b_ref, hp_ref, g_ref, be_ref, r_ref, h_ref):
        p = _dot(a_ref[...].astype(BF16), b_ref[...])

        @pl.when(pl.program_id(1) == 0)
        def _():
            r_ref[...] = p

        @pl.when(pl.program_id(1) > 0)
        def _():
            r_ref[...] += p

        @pl.when(pl.program_id(1) == ks - 1)
        def _():
            r = ALPHA * hp_ref[...] + r_ref[...]
            r_ref[...] = r
            h_ref[...] = _layer_norm(r, g_ref[...], be_ref[...])

    row = pl.BlockSpec((None, tm, d), lambda i, k: (0, i, 0))
    vec = pl.BlockSpec((1, d), lambda i, k: (0, 0))
    return pl.pallas_call(
        body,
        name=name,
        grid=(m // tm, ks),
        in_specs=[
            pl.BlockSpec((None, tm, tk), lambda i, k: (k, i, 0)),
            pl.BlockSpec((None, None, tk, d), lambda i, k: (k, 0, 0, 0)),
            row, vec, vec,
        ],
        out_specs=[row, row],
        out_shape=[jax.ShapeDtypeStruct((1, m, d), F32)] * 2,
        compiler_params=_params(("arbitrary", "arbitrary")),
    )(a, b, h_prev, g, beta)


def mm_nt_ln_bwd(dy, w, res, r, g, *, tm, name, swap=()):
    ns, m, tn = dy.shape
    d = w.shape[2]
    n_swap = len(swap)
    last_tile = m // tm - 1

    def body(dy_ref, w_ref, res_ref, r_ref, g_ref, *refs):
        leaving_refs, (dr_ref, dgb_ref), refs = refs[:n_swap], refs[n_swap:n_swap + 2], refs[n_swap + 2:]
        copies = _swap_copies(leaving_refs, refs[:n_swap], *refs[n_swap:]) if n_swap else None
        p = _dot_nt(dy_ref[...].astype(BF16), w_ref[...])

        @pl.when((pl.program_id(0) == 0) & (pl.program_id(1) == 0))
        def _():
            dgb_ref[...] = jnp.zeros_like(dgb_ref)
            if n_swap:
                _swap_start(copies)

        @pl.when(pl.program_id(1) == 0)
        def _():
            dr_ref[...] = p + ALPHA * res_ref[...]

        @pl.when(pl.program_id(1) > 0)
        def _():
            dr_ref[...] += p

        @pl.when(pl.program_id(1) == ns - 1)
        def _():
            for rows in (pl.ds(0, tm // 2), pl.ds(tm // 2, tm // 2)):
                dr, dgamma, dbeta = _layer_norm_bwd(r_ref[rows, :], dr_ref[rows, :], g_ref[...])
                dr_ref[rows, :] = dr
                dgb_ref[0] += dgamma
                dgb_ref[1] += dbeta

        if n_swap:
            @pl.when((pl.program_id(0) == last_tile) & (pl.program_id(1) == ns - 1))
            def _():
                _swap_finish(copies)

    row = pl.BlockSpec((None, tm, d), lambda i, n: (0, i, 0))
    dr, dgb, *landed = pl.pallas_call(
        body,
        name=name,
        grid=(m // tm, ns),
        in_specs=[
            pl.BlockSpec((None, tm, tn), lambda i, n: (n, i, 0)),
            pl.BlockSpec((None, None, d, tn), lambda i, n: (0, n, 0, 0)),
            row, row,
            pl.BlockSpec((1, d), lambda i, n: (0, 0)),
        ] + [ANY] * n_swap,
        out_specs=[row, pl.BlockSpec((2, 8, d), lambda i, n: (0, 0, 0))] + [ANY] * n_swap,
        out_shape=[jax.ShapeDtypeStruct((1, m, d), F32), jax.ShapeDtypeStruct((2, 8, d), F32)] + _swap_out_shapes(swap),
        scratch_shapes=_swap_sems(n_swap) if n_swap else [],
        compiler_params=_params(("arbitrary", "arbitrary")),
    )(dy, w, res, r, g, *swap)
    return dr, dgb, landed


def loss_ln_bwd(h, target, r, g, *, first, count, tm):
    _, lp, d = h.shape

    def body(h_ref, t_ref, r_ref, g_ref, dr_ref, dgb_ref, l_ref):
        row = pl.program_id(0) * tm + lax.broadcasted_iota(jnp.int32, (tm, d), 0)
        valid = (row >= first) & (row < first + count)
        err = jnp.where(valid, h_ref[...] - t_ref[...], 0.0)
        dr, dgamma, dbeta = _layer_norm_bwd(r_ref[...], err * (1.0 / d), g_ref[...])
        dr_ref[...] = dr

        @pl.when(pl.program_id(0) == 0)
        def _():
            dgb_ref[...] = jnp.zeros_like(dgb_ref)
            l_ref[...] = jnp.zeros_like(l_ref)

        dgb_ref[0] += dgamma
        dgb_ref[1] += dbeta
        l_ref[...] += _row_partial(err * err) * (0.5 / d)

    row3 = pl.BlockSpec((None, tm, d), lambda i: (0, i, 0))
    return pl.pallas_call(
        body,
        name="loss_ln4_bwd",
        grid=(lp // tm,),
        in_specs=[row3, pl.BlockSpec((tm, d), lambda i: (i, 0)), row3, pl.BlockSpec((1, d), lambda i: (0, 0))],
        out_specs=[row3, pl.BlockSpec((2, 8, d), lambda i: (0, 0, 0)), pl.BlockSpec((8, d), lambda i: (0, 0))],
        out_shape=[jax.ShapeDtypeStruct((1, lp, d), F32), jax.ShapeDtypeStruct((2, 8, d), F32),
                   jax.ShapeDtypeStruct((8, d), F32)],
        compiler_params=_params(("arbitrary",)),
    )(h, target, r, g)


def _halo_index(tile, tm):
    return jnp.maximum(tile * (tm // HALO) - 1, 0)


def _conv_fwd(xs_ref, w, taps, tm):
    acc = w(0) * xs_ref[pl.ds(HALO - taps + 1, tm), :]
    for j in range(1, taps):
        acc += w(j) * xs_ref[pl.ds(HALO - taps + 1 + j, tm), :]
    return acc


def _conv_bwd_x(dcs_ref, w, taps, tm):
    acc = w(0) * dcs_ref[pl.ds(taps - 1, tm), :]
    for j in range(1, taps):
        acc += w(j) * dcs_ref[pl.ds(taps - 1 - j, tm), :]
    return acc


SUB = 8
LANES = 128
PAIR = 2 * SUB
STRIP_UNROLL = 2


def _pair_rows(r0):
    return pl.ds(r0, SUB), pl.ds(r0 + SUB if isinstance(r0, int) else pl.multiple_of(r0 + SUB, SUB), SUB)


def _shift_down(cur, prev, s):
    if s == 0:
        return cur
    row = lax.broadcasted_iota(jnp.int32, cur.shape, 0)
    return jnp.where(row < s, pltpu.roll(prev, s, axis=0), pltpu.roll(cur, s, axis=0))


def _shift_up(cur, nxt, s):
    if s == 0:
        return cur
    row = lax.broadcasted_iota(jnp.int32, cur.shape, 0)
    return jnp.where(row < SUB - s, pltpu.roll(cur, SUB - s, axis=0), pltpu.roll(nxt, SUB - s, axis=0))


def _silu_parts(c):
    sg = _sigmoid(c)
    return c * sg, sg * (1.0 + c * (1.0 - sg))


def _head_sum(x):
    rows, c = x.shape
    parts = []
    for h in range(c // HEAD_DIM):
        s = jnp.sum(x[:, h * HEAD_DIM:(h + 1) * HEAD_DIM], axis=-1, keepdims=True)
        parts.append(jnp.broadcast_to(s, (rows, HEAD_DIM)))
    return parts[0] if len(parts) == 1 else jnp.concatenate(parts, axis=-1)


def _log1p(y):
    u = 1.0 + y
    d = u - 1.0
    return jnp.where(d == 0.0, y, jnp.log(u) * (y / jnp.where(d == 0.0, 1.0, d)))


def _softplus(x):
    return jnp.maximum(x, 0.0) + _log1p(jnp.exp(-jnp.abs(x)))


def _gate_values(x, al, dt):
    lane = lax.broadcasted_iota(jnp.int32, x.shape, 1)
    is_beta, is_g = lane < HEADS, (lane >= HEADS) & (lane < 2 * HEADS)
    return _sigmoid(x), -jnp.exp(al) * _softplus(x + dt), is_beta, is_g


def gdn_gates_fwd(pba, al, dt, *, tm):
    _, lp, width = pba.shape

    def body(x_ref, al_ref, dt_ref, o_ref):
        beta, g, is_beta, is_g = _gate_values(x_ref[...], al_ref[...], dt_ref[...])
        o_ref[...] = jnp.where(is_beta, beta, jnp.where(is_g, g, 0.0))

    vec = pl.BlockSpec((1, width), lambda i: (0, 0))
    return pl.pallas_call(
        body,
        name="gdn_gates_fwd",
        grid=(lp // tm,),
        in_specs=[pl.BlockSpec((None, tm, width), lambda i: (0, i, 0)), vec, vec],
        out_specs=pl.BlockSpec((tm, width), lambda i: (i, 0)),
        out_shape=jax.ShapeDtypeStruct((lp, width), F32),
        compiler_params=_params(("arbitrary",)),
    )(pba, al, dt)


def gdn_gates_bwd(pba, dgates, al, dt, *, tm):
    _, lp, width = pba.shape

    def body(x_ref, d_ref, al_ref, dt_ref, dx_ref, dsc_ref):
        x = x_ref[...]
        beta, g, is_beta, is_g = _gate_values(x, al_ref[...], dt_ref[...])
        d = d_ref[...]
        dg = jnp.where(is_g, d, 0.0)
        da = dg * -jnp.exp(al_ref[...]) * _sigmoid(x + dt_ref[...])
        dx_ref[...] = jnp.where(is_beta, d * beta * (1.0 - beta), da).astype(dx_ref.dtype)

        @pl.when(pl.program_id(0) == 0)
        def _():
            dsc_ref[...] = jnp.zeros_like(dsc_ref)

        dsc_ref[0] += _row_partial(dg * g)
        dsc_ref[1] += _row_partial(da)

    vec = pl.BlockSpec((1, width), lambda i: (0, 0))
    return pl.pallas_call(
        body,
        name="gdn_gates_bwd",
        grid=(lp // tm,),
        in_specs=[pl.BlockSpec((None, tm, width), lambda i: (0, i, 0)), pl.BlockSpec((tm, width), lambda i: (i, 0)), vec, vec],
        out_specs=[pl.BlockSpec((None, tm, width), lambda i: (0, i, 0)), pl.BlockSpec((2, SUB, width), lambda i: (0, 0, 0))],
        out_shape=[jax.ShapeDtypeStruct((1, lp, width), BF16), jax.ShapeDtypeStruct((2, SUB, width), F32)],
        compiler_params=_params(("arbitrary",)),
    )(pba, dgates, al, dt)


def gdn_pre_fwd(p3, conv_w, *, tm, cb):
    _, lp, width = p3.shape
    taps = conv_w.shape[1]

    def body(x_ref, halo_ref, w_ref, o_ref, xs):
        i = pl.program_id(1)
        for s in range(3):
            xs[s, 0:HALO, :] = jnp.where(i > 0, halo_ref[s], 0.0)
            xs[s, HALO:, :] = x_ref[s]
            c = _conv_fwd(xs.at[s], lambda j, s=s: w_ref[s, j:j + 1, :], taps, tm)
            y, _ = _silu_parts(c)
            if s < 2:
                y = y * lax.rsqrt(_head_sum(y * y) + L2_EPS)
                if s == 0:
                    y = y * Q_SCALE
            o_ref[s] = y

    return pl.pallas_call(
        body,
        name="gdn_pre_fwd",
        grid=(width // cb, lp // tm),
        in_specs=[
            pl.BlockSpec((3, tm, cb), lambda j, i: (0, i, j)),
            pl.BlockSpec((3, HALO, cb), lambda j, i: (0, _halo_index(i, tm), j)),
            pl.BlockSpec((3, taps, cb), lambda j, i: (0, 0, j)),
        ],
        out_specs=pl.BlockSpec((3, tm, cb), lambda j, i: (0, i, j)),
        out_shape=jax.ShapeDtypeStruct((3, lp, width), F32),
        scratch_shapes=[pltpu.VMEM((3, tm + HALO, cb), F32)],
        compiler_params=_params(("arbitrary", "arbitrary")),
    )(p3, p3, conv_w)


def gdn_pre_bwd(p3, dqkv, conv_w, *, tm, cb):
    _, lp, width = p3.shape
    taps = conv_w.shape[1]
    last = lp // tm - 1

    def body(x_ref, halo_ref, d_ref, w_ref, dx_ref, dw_ref, xs, dcs, carry):
        step = pl.program_id(1)
        tile = last - step

        @pl.when(step == 0)
        def _():
            carry[...] = jnp.zeros_like(carry)
            dw_ref[...] = jnp.zeros_like(dw_ref)

        for s in range(3):
            w = lambda j, s=s: w_ref[s, j:j + 1, :]
            xs[s, 0:HALO, :] = jnp.where(tile > 0, halo_ref[s], 0.0)
            xs[s, HALO:, :] = x_ref[s]
            c = _conv_fwd(xs.at[s], w, taps, tm)
            y, dsilu = _silu_parts(c)
            dy = d_ref[s]
            if s < 2:
                rn = lax.rsqrt(_head_sum(y * y) + L2_EPS)
                yn = y * rn
                if s == 0:
                    dy = dy * Q_SCALE
                dy = rn * (dy - yn * _head_sum(dy * yn))
            dc = dy * dsilu
            dcs[s, 0:tm, :] = dc
            dcs[s, tm:, :] = carry[s]
            dx_ref[s] = _conv_bwd_x(dcs.at[s], w, taps, tm).astype(dx_ref.dtype)
            carry[s] = dc[0:HALO, :]
            for j in range(taps):
                dw_ref[s, j] += _row_partial(dc * xs[s, pl.ds(HALO - taps + 1 + j, tm), :])

    tile_spec = pl.BlockSpec((3, tm, cb), lambda j, i: (0, last - i, j))
    return pl.pallas_call(
        body,
        name="gdn_pre_bwd",
        grid=(width // cb, lp // tm),
        in_specs=[
            tile_spec,
            pl.BlockSpec((3, HALO, cb), lambda j, i: (0, _halo_index(last - i, tm), j)),
            tile_spec,
            pl.BlockSpec((3, taps, cb), lambda j, i: (0, 0, j)),
        ],
        out_specs=[tile_spec, pl.BlockSpec((3, taps, SUB, cb), lambda j, i: (0, 0, 0, j))],
        out_shape=[jax.ShapeDtypeStruct((3, lp, width), BF16), jax.ShapeDtypeStruct((3, taps, SUB, width), F32)],
        scratch_shapes=[
            pltpu.VMEM((3, tm + HALO, cb), F32),
            pltpu.VMEM((3, tm + HALO, cb), F32),
            pltpu.VMEM((3, HALO, cb), F32),
        ],
        compiler_params=_params(("arbitrary", "arbitrary")),
    )(p3, p3, dqkv, conv_w)


def gdn_post_fwd(o, z, nw_b, *, tm):
    _, lp, width = o.shape

    def body(o_ref, z_ref, nw_ref, y_ref):
        ov = o_ref[...]
        rn = lax.rsqrt(_head_sum(ov * ov) * (1.0 / HEAD_DIM) + RMS_EPS)
        gate, _ = _silu_parts(z_ref[...])
        y_ref[...] = (ov * rn * nw_ref[...] * gate).astype(y_ref.dtype)

    row = pl.BlockSpec((None, tm, width), lambda i: (0, i, 0))
    return pl.pallas_call(
        body,
        name="gdn_post_fwd",
        grid=(lp // tm,),
        in_specs=[row, row, pl.BlockSpec((1, width), lambda i: (0, 0))],
        out_specs=row,
        out_shape=jax.ShapeDtypeStruct((1, lp, width), BF16),
        compiler_params=_params(("arbitrary",)),
    )(o, z, nw_b)


def gdn_post_bwd(o, z, dy, nw_b, *, tm):
    _, lp, width = o.shape

    def body(o_ref, z_ref, dy_ref, nw_ref, do_ref, dz_ref, dnw_ref):
        ov = o_ref[...]
        rn = lax.rsqrt(_head_sum(ov * ov) * (1.0 / HEAD_DIM) + RMS_EPS)
        yn = ov * rn
        gate, dgate = _silu_parts(z_ref[...])
        d_on = dy_ref[...] * gate
        dz_ref[...] = (dy_ref[...] * yn * nw_ref[...] * dgate).astype(dz_ref.dtype)
        a = d_on * nw_ref[...]
        do_ref[...] = rn * (a - yn * (_head_sum(a * yn) * (1.0 / HEAD_DIM)))

        @pl.when(pl.program_id(0) == 0)
        def _():
            dnw_ref[...] = jnp.zeros_like(dnw_ref)

        dnw_ref[...] += _row_partial(d_on * yn)

    row = pl.BlockSpec((None, tm, width), lambda i: (0, i, 0))
    return pl.pallas_call(
        body,
        name="gdn_post_bwd",
        grid=(lp // tm,),
        in_specs=[row, row, row, pl.BlockSpec((1, width), lambda i: (0, 0))],
        out_specs=[row, row, pl.BlockSpec((8, width), lambda i: (0, 0))],
        out_shape=[jax.ShapeDtypeStruct((1, lp, width), F32), jax.ShapeDtypeStruct((1, lp, width), BF16),
                   jax.ShapeDtypeStruct((8, width), F32)],
        compiler_params=_params(("arbitrary",)),
    )(o, z, dy, nw_b)


def ffn_act_fwd(up, conv_w, *, tm, name):
    _, lp, c_w = up.shape
    taps = conv_w.shape[1]

    def body(u_ref, halo_ref, g_ref, w_ref, o_ref):
        first_tile = pl.program_id(1) == 0

        def strip(cur, prev, rows, cs):
            conv = w_ref[taps - 1:taps, cs] * cur
            for j in range(taps - 1):
                conv += w_ref[j:j + 1, cs] * _shift_down(cur, prev, taps - 1 - j)
            y, _ = _silu_parts(conv)
            return y * g_ref[rows, cs]

        def pair(r0, above_of):
            top, bot = _pair_rows(r0)
            for c0 in range(0, c_w, LANES):
                cs = slice(c0, c0 + LANES)
                cur_t, cur_b = u_ref[top, cs], u_ref[bot, cs]
                out = [strip(cur_t, above_of(cs), top, cs), strip(cur_b, cur_t, bot, cs)]
                o_ref[pl.ds(r0, PAIR), cs] = jnp.concatenate(out, axis=0).astype(o_ref.dtype)

        pair(0, lambda cs: jnp.where(first_tile, 0.0, halo_ref[:, cs]))

        def loop_body(s, carry):
            r0 = pl.multiple_of(s * PAIR, PAIR)
            pair(r0, lambda cs: u_ref[pl.ds(pl.multiple_of(r0 - SUB, SUB), SUB), cs])
            return carry

        lax.fori_loop(1, tm // PAIR, loop_body, 0, unroll=STRIP_UNROLL)

    return pl.pallas_call(
        body,
        name=name,
        grid=(2, lp // tm),
        in_specs=[
            pl.BlockSpec((None, tm, c_w), lambda s, i: (s, i, 0)),
            pl.BlockSpec((None, HALO, c_w), lambda s, i: (s, _halo_index(i, tm), 0)),
            pl.BlockSpec((None, tm, c_w), lambda s, i: (2 + s, i, 0)),
            pl.BlockSpec((None, taps, c_w), lambda s, i: (s, 0, 0)),
        ],
        out_specs=pl.BlockSpec((None, tm, c_w), lambda s, i: (s, i, 0)),
        out_shape=jax.ShapeDtypeStruct((2, lp, c_w), BF16),
        compiler_params=_params(("arbitrary", "arbitrary")),
    )(up, up, up, conv_w)


def ffn_act_bwd(up, dact, conv_w, *, tm, name):
    _, lp, c_w = up.shape
    taps = conv_w.shape[1]
    last = lp // tm - 1
    n_pairs = tm // PAIR

    def body(u_ref, halo_ref, g_ref, d_ref, w_ref, dup_ref, dw_ref, below):
        step = pl.program_id(1)
        first_tile = step == last

        @pl.when(step == 0)
        def _():
            below[...] = jnp.zeros_like(below)
            dw_ref[...] = jnp.zeros_like(dw_ref)

        def strip(cur, prev, rows, cs, nxt):
            shifted = [_shift_down(cur, prev, taps - 1 - j) for j in range(taps)]
            conv = w_ref[0:1, cs] * shifted[0]
            for j in range(1, taps):
                conv += w_ref[j:j + 1, cs] * shifted[j]
            y, dsilu = _silu_parts(conv)
            d = d_ref[rows, cs]
            dc = d * g_ref[rows, cs] * dsilu
            dx = w_ref[taps - 1:taps, cs] * dc
            for j in range(taps - 1):
                dx += w_ref[j:j + 1, cs] * _shift_up(dc, nxt, taps - 1 - j)
            return dx, d * y, dc, [dc * s for s in shifted]

        def pair(r0, above_of):
            top, bot = _pair_rows(r0)
            both = pl.ds(r0, PAIR)
            for c0 in range(0, c_w, LANES):
                cs = slice(c0, c0 + LANES)
                cur_t, cur_b = u_ref[top, cs], u_ref[bot, cs]
                dx_b, dg_b, dc_b, dw_b = strip(cur_b, cur_t, bot, cs, below[:, cs])
                dx_t, dg_t, dc_t, dw_t = strip(cur_t, above_of(cs), top, cs, dc_b)
                below[:, cs] = dc_t
                dup_ref[0, both, cs] = jnp.concatenate([dx_t, dx_b], axis=0).astype(dup_ref.dtype)
                dup_ref[1, both, cs] = jnp.concatenate([dg_t, dg_b], axis=0).astype(dup_ref.dtype)
                for j in range(taps):
                    dw_ref[j, :, cs] += dw_t[j] + dw_b[j]

        def loop_body(it, carry):
            r0 = pl.multiple_of((n_pairs - 1 - it) * PAIR, PAIR)
            pair(r0, lambda cs: u_ref[pl.ds(pl.multiple_of(r0 - SUB, SUB), SUB), cs])
            return carry

        lax.fori_loop(0, n_pairs - 1, loop_body, 0, unroll=STRIP_UNROLL)
        pair(0, lambda cs: jnp.where(first_tile, 0.0, halo_ref[:, cs]))

    return pl.pallas_call(
        body,
        name=name,
        grid=(2, lp // tm),
        in_specs=[
            pl.BlockSpec((None, tm, c_w), lambda s, i: (s, last - i, 0)),
            pl.BlockSpec((None, HALO, c_w), lambda s, i: (s, _halo_index(last - i, tm), 0)),
            pl.BlockSpec((None, tm, c_w), lambda s, i: (2 + s, last - i, 0)),
            pl.BlockSpec((None, tm, c_w), lambda s, i: (s, last - i, 0)),
            pl.BlockSpec((None, taps, c_w), lambda s, i: (s, 0, 0)),
        ],
        out_specs=[
            pl.BlockSpec((2, None, tm, c_w), lambda s, i: (0, s, last - i, 0)),
            pl.BlockSpec((None, taps, SUB, c_w), lambda s, i: (s, 0, 0, 0)),
        ],
        out_shape=[jax.ShapeDtypeStruct((2, 2, lp, c_w), BF16), jax.ShapeDtypeStruct((2, taps, SUB, c_w), F32)],
        scratch_shapes=[pltpu.VMEM((SUB, c_w), F32)],
        compiler_params=_params(("arbitrary", "arbitrary")),
    )(up, up, up, dact, conv_w)


def sc_fwd(pb, conv_w, *, tm, cb):
    _, lp, width = pb.shape
    taps = conv_w.shape[0]

    def body(x_ref, halo_ref, w_ref, o_ref):
        first_tile = pl.program_id(1) == 0

        def strip(cur, prev, rows, cs):
            conv = w_ref[taps - 1:taps, cs] * cur
            for j in range(taps - 1):
                conv += w_ref[j:j + 1, cs] * _shift_down(cur, prev, taps - 1 - j)
            return x_ref[0, rows, cs] * conv

        def pair(r0, above_of):
            top, bot = _pair_rows(r0)
            for c0 in range(0, cb, LANES):
                cs = slice(c0, c0 + LANES)
                cur_t = x_ref[1, top, cs] * x_ref[2, top, cs]
                cur_b = x_ref[1, bot, cs] * x_ref[2, bot, cs]
                out = [strip(cur_t, above_of(cs), top, cs), strip(cur_b, cur_t, bot, cs)]
                o_ref[pl.ds(r0, PAIR), cs] = jnp.concatenate(out, axis=0).astype(o_ref.dtype)

        pair(0, lambda cs: jnp.where(first_tile, 0.0, halo_ref[1, :, cs] * halo_ref[2, :, cs]))

        def loop_body(k, carry):
            r0 = pl.multiple_of(k * PAIR, PAIR)
            before = pl.ds(pl.multiple_of(r0 - SUB, SUB), SUB)
            pair(r0, lambda cs: x_ref[1, before, cs] * x_ref[2, before, cs])
            return carry

        lax.fori_loop(1, tm // PAIR, loop_body, 0, unroll=STRIP_UNROLL)

    return pl.pallas_call(
        body,
        name="sc_fwd",
        grid=(width // cb, lp // tm),
        in_specs=[
            pl.BlockSpec((3, tm, cb), lambda j, i: (0, i, j)),
            pl.BlockSpec((3, HALO, cb), lambda j, i: (0, _halo_index(i, tm), j)),
            pl.BlockSpec((taps, cb), lambda j, i: (0, j)),
        ],
        out_specs=pl.BlockSpec((None, tm, cb), lambda j, i: (0, i, j)),
        out_shape=jax.ShapeDtypeStruct((1, lp, width), BF16),
        compiler_params=_params(("arbitrary", "arbitrary")),
    )(pb, pb, conv_w)


def sc_bwd(pb, ds, conv_w, *, tm, cb):
    _, lp, width = pb.shape
    taps = conv_w.shape[0]
    last = lp // tm - 1
    n_pairs = tm // PAIR

    def body(x_ref, halo_ref, d_ref, w_ref, dx_ref, dw_ref, below):
        step = pl.program_id(1)
        first_tile = step == last

        @pl.when(step == 0)
        def _():
            below[...] = jnp.zeros_like(below)
            dw_ref[...] = jnp.zeros_like(dw_ref)

        def strip(cur, prev, rows, cs, nxt):
            gate, left, right = x_ref[0, rows, cs], x_ref[1, rows, cs], x_ref[2, rows, cs]
            shifted = [_shift_down(cur, prev, taps - 1 - j) for j in range(taps)]
            conv = w_ref[0:1, cs] * shifted[0]
            for j in range(1, taps):
                conv += w_ref[j:j + 1, cs] * shifted[j]
            d = d_ref[rows, cs]
            dc = d * gate
            dp = w_ref[taps - 1:taps, cs] * dc
            for j in range(taps - 1):
                dp += w_ref[j:j + 1, cs] * _shift_up(dc, nxt, taps - 1 - j)
            return d * conv, dp * right, dp * left, dc, [dc * s for s in shifted]

        def pair(r0, above_of):
            top, bot = _pair_rows(r0)
            both = pl.ds(r0, PAIR)
            for c0 in range(0, cb, LANES):
                cs = slice(c0, c0 + LANES)
                cur_t = x_ref[1, top, cs] * x_ref[2, top, cs]
                cur_b = x_ref[1, bot, cs] * x_ref[2, bot, cs]
                *dx_b, dc_b, dw_b = strip(cur_b, cur_t, bot, cs, below[:, cs])
                *dx_t, dc_t, dw_t = strip(cur_t, above_of(cs), top, cs, dc_b)
                below[:, cs] = dc_t
                for s in range(3):
                    dx_ref[s, both, cs] = jnp.concatenate([dx_t[s], dx_b[s]], axis=0).astype(dx_ref.dtype)
                for j in range(taps):
                    dw_ref[j, :, cs] += dw_t[j] + dw_b[j]

        def loop_body(it, carry):
            r0 = pl.multiple_of((n_pairs - 1 - it) * PAIR, PAIR)
            before = pl.ds(pl.multiple_of(r0 - SUB, SUB), SUB)
            pair(r0, lambda cs: x_ref[1, before, cs] * x_ref[2, before, cs])
            return carry

        lax.fori_loop(0, n_pairs - 1, loop_body, 0, unroll=STRIP_UNROLL)
        pair(0, lambda cs: jnp.where(first_tile, 0.0, halo_ref[1, :, cs] * halo_ref[2, :, cs]))

    tile_spec = pl.BlockSpec((3, tm, cb), lambda j, i: (0, last - i, j))
    return pl.pallas_call(
        body,
        name="sc_bwd",
        grid=(width // cb, lp // tm),
        in_specs=[
            tile_spec,
            pl.BlockSpec((3, HALO, cb), lambda j, i: (0, _halo_index(last - i, tm), j)),
            pl.BlockSpec((None, tm, cb), lambda j, i: (0, last - i, j)),
            pl.BlockSpec((taps, cb), lambda j, i: (0, j)),
        ],
        out_specs=[tile_spec, pl.BlockSpec((taps, SUB, cb), lambda j, i: (0, 0, j))],
        out_shape=[jax.ShapeDtypeStruct((3, lp, width), BF16), jax.ShapeDtypeStruct((taps, SUB, width), F32)],
        scratch_shapes=[pltpu.VMEM((SUB, cb), F32)],
        compiler_params=_params(("arbitrary", "arbitrary")),
    )(pb, pb, ds, conv_w)


TILE_BYTES = 1536 * 1024


def _rows_tile(rows, cols, multiple=8):
    if rows * cols * 4 <= TILE_BYTES or rows % multiple:
        return rows
    best = multiple
    for t in range(multiple, rows + 1, multiple):
        if rows % t == 0 and t * cols * 4 <= TILE_BYTES:
            best = t
    return best


def pair_sum(g, landed, core, out_dtype, name):
    _, rows, cols = g.shape
    half = rows // 2
    tr = _rows_tile(half, cols, 16)
    nb = half // tr

    def body(c_ref, g_ref, l_ref, o_ref):
        o_ref[...] = (g_ref[...] + l_ref[...]).astype(out_dtype)

    return pl.pallas_call(
        body,
        name=name,
        grid_spec=pltpu.PrefetchScalarGridSpec(
            num_scalar_prefetch=1,
            grid=(4, nb),
            in_specs=[
                pl.BlockSpec((None, tr, cols), lambda s, i, c: (s, c[0] * nb + i, 0)),
                pl.BlockSpec((None, tr, cols), lambda s, i, c: (s, i, 0)),
            ],
            out_specs=pl.BlockSpec((None, tr, cols), lambda s, i, c: (s, i, 0)),
        ),
        out_shape=jax.ShapeDtypeStruct((4, half, cols), out_dtype),
        compiler_params=_params(("arbitrary", "arbitrary")),
    )(core, g, landed)


def chip_sum(x, name):
    _, rows, cols = x.shape
    tr = _rows_tile(rows, cols, 16)

    def body(x0, x1, x2, x3, o_ref):
        acc = x0[...].astype(F32) + x1[...].astype(F32)
        o_ref[...] = (acc + x2[...].astype(F32)) + x3[...].astype(F32)

    return pl.pallas_call(
        body,
        name=name,
        grid=(rows // tr,),
        in_specs=[pl.BlockSpec((None, tr, cols), lambda i, k=k: (k, i, 0)) for k in range(4)],
        out_specs=pl.BlockSpec((tr, cols), lambda i: (i, 0)),
        out_shape=jax.ShapeDtypeStruct((rows, cols), F32),
        compiler_params=_params(("arbitrary",)),
    )(x, x, x, x)


def adamw(w, g, m, v, name):
    shape = w.shape
    cols = shape[-1]
    rows = w.size // cols
    tr = _rows_tile(rows, cols)

    def body(w_ref, g_ref, m_ref, v_ref, d_ref, m2_ref, v2_ref):
        gv = g_ref[...]
        m2 = ADAM_B1 * m_ref[...] + (1.0 - ADAM_B1) * gv
        v2 = ADAM_B2 * v_ref[...] + (1.0 - ADAM_B2) * (gv * gv)
        m_hat = m2 / (1.0 - ADAM_B1 ** ADAM_STEP)
        v_hat = v2 / (1.0 - ADAM_B2 ** ADAM_STEP)
        d_ref[...] = -ADAM_LR * (m_hat / (jnp.sqrt(v_hat) + ADAM_EPS) + ADAM_WD * w_ref[...])
        m2_ref[...] = m2
        v2_ref[...] = v2

    spec = pl.BlockSpec((tr, cols), lambda i: (i, 0))
    outs = pl.pallas_call(
        body,
        name=name,
        grid=(rows // tr,),
        in_specs=[spec] * 4,
        out_specs=[spec] * 3,
        out_shape=[jax.ShapeDtypeStruct((rows, cols), F32)] * 3,
        compiler_params=_params(("arbitrary",)),
    )(*[t.reshape(rows, cols) for t in (w, g, m, v)])
    return tuple(o.reshape(shape) for o in outs)


MESH_ID = pl.DeviceIdType.MESH
ANY = pl.BlockSpec(memory_space=pl.ANY)


def _place():
    x, y, c = lax.axis_index("x"), lax.axis_index("y"), lax.axis_index("c")
    other_chips = [(1 - x, y), (x, 1 - y), (1 - x, 1 - y)]
    return x, y, c, other_chips


def all_gather_shards(bufs, name):
    n = len(bufs)

    def body(*refs):
        x_refs, o_refs = refs[:n], refs[n:2 * n]
        copies = _gather_copies(x_refs, o_refs, *refs[2 * n:])
        _gather_start(copies)
        _gather_finish(copies)

    outs = pl.pallas_call(
        body,
        name=name,
        in_specs=[ANY] * n,
        out_specs=[ANY] * n,
        out_shape=_gather_out_shapes(bufs),
        scratch_shapes=_gather_sems(n),
    )(*bufs)
    return _set_own_slots(outs, bufs)


def _gather_out_shapes(bufs):
    return [jax.ShapeDtypeStruct((4,) + b.shape, b.dtype) for b in bufs]


def _gather_sems(n):
    return [pltpu.SemaphoreType.DMA((6 * n,)), pltpu.SemaphoreType.DMA((6 * n,))]


def _set_own_slots(outs, bufs):
    if not outs:
        return []
    me = 2 * lax.axis_index("x") + lax.axis_index("y")
    return [lax.dynamic_update_index_in_dim(o, b, me, 0) for o, b in zip(outs, bufs)]


def _gather_copies(x_refs, o_refs, send_sems, recv_sems):
    x, y, c, chips = _place()
    me = 2 * x + y
    sibling = (x, y, 1 - c)

    def part(a, slot, hf):
        half = x_refs[a].shape[0] // 2
        return o_refs[a].at[slot, pl.ds(hf * half, half), :]

    def mine(a):
        half = x_refs[a].shape[0] // 2
        return x_refs[a].at[pl.ds(c * half, half), :]

    def copy(k, src, dst, to):
        return pltpu.make_async_remote_copy(src_ref=src, dst_ref=dst, send_sem=send_sems.at[k],
                                            recv_sem=recv_sems.at[k], device_id=to, device_id_type=MESH_ID)

    sends, arrivals, passes, passed = [], [], [], []
    for a in range(len(x_refs)):
        for j, (px, py) in enumerate(chips):
            landed, theirs = part(a, 2 * px + py, c), part(a, 2 * px + py, 1 - c)
            sends.append(copy(6 * a + j, mine(a), part(a, me, c), (px, py, c)))
            arrivals.append(copy(6 * a + j, mine(a), landed, (px, py, c)))
            passes.append(copy(6 * a + 3 + j, landed, landed, sibling))
            passed.append(copy(6 * a + 3 + j, theirs, theirs, sibling))
    return sends, arrivals, passes, passed


def _gather_start(copies):
    for cp in copies[0]:
        cp.start()


def _gather_finish(copies):
    sends, arrivals, passes, passed = copies
    for arrival, cp in zip(arrivals, passes):
        arrival.wait_recv()
        cp.start()
    for cp in passed:
        cp.wait_recv()
    for cp in sends + passes:
        cp.wait_send()


def swap_halves(bufs, name):
    n = len(bufs)

    def body(*refs):
        copies = _swap_copies(refs[:n], refs[n:2 * n], *refs[2 * n:])
        _swap_start(copies)
        _swap_finish(copies)

    return pl.pallas_call(
        body,
        name=name,
        in_specs=[ANY] * n,
        out_specs=[ANY] * n,
        out_shape=_swap_out_shapes(bufs),
        scratch_shapes=_swap_sems(n),
    )(*bufs)


def _swap_out_shapes(bufs):
    return [jax.ShapeDtypeStruct((4, b.shape[1] // 2, b.shape[2]), b.dtype) for b in bufs]


def _swap_sems(n):
    return [pltpu.SemaphoreType.DMA((n,)), pltpu.SemaphoreType.DMA((n,))]


def _swap_copies(x_refs, o_refs, send_sems, recv_sems):
    x, y, c, _ = _place()
    copies = []
    for a, (x_ref, o_ref) in enumerate(zip(x_refs, o_refs)):
        half = x_ref.shape[1] // 2
        copies.append(pltpu.make_async_remote_copy(src_ref=x_ref.at[:, pl.ds((1 - c) * half, half), :], dst_ref=o_ref,
                                                   send_sem=send_sems.at[a], recv_sem=recv_sems.at[a],
                                                   device_id=(x, y, 1 - c), device_id_type=MESH_ID))
    return copies


def _swap_start(copies):
    for cp in copies:
        cp.start()


def _swap_finish(copies):
    for cp in copies:
        cp.wait()


def scatter_to_chips(bufs, name):
    n = len(bufs)

    def body(*refs):
        x_refs, o_refs = refs[:n], refs[n:2 * n]
        copies = _scatter_copies(x_refs, o_refs, *refs[2 * n:])
        _scatter_start(copies)
        _scatter_finish(copies)

    outs = pl.pallas_call(
        body,
        name=name,
        in_specs=[ANY] * n,
        out_specs=[ANY] * n,
        out_shape=[jax.ShapeDtypeStruct(b.shape, b.dtype) for b in bufs],
        scratch_shapes=_scatter_sems(n),
    )(*bufs)
    return _keep_own_slots(outs, bufs)


def _scatter_sems(n):
    return [pltpu.SemaphoreType.DMA((3 * n,)), pltpu.SemaphoreType.DMA((3 * n,))]


def _keep_own_slots(outs, bufs):
    if not outs:
        return []
    me = 2 * lax.axis_index("x") + lax.axis_index("y")
    return [lax.dynamic_update_index_in_dim(o, lax.dynamic_index_in_dim(b, me, 0, keepdims=False), me, 0)
            for o, b in zip(outs, bufs)]


def _scatter_copies(x_refs, o_refs, send_sems, recv_sems):
    x, y, c, chips = _place()
    me = 2 * x + y

    def copy(a, j, src_slot, dst_slot, px, py):
        return pltpu.make_async_remote_copy(src_ref=x_refs[a].at[src_slot], dst_ref=o_refs[a].at[dst_slot],
                                            send_sem=send_sems.at[3 * a + j], recv_sem=recv_sems.at[3 * a + j],
                                            device_id=(px, py, c), device_id_type=MESH_ID)

    sends = [copy(a, j, 2 * px + py, me, px, py) for a in range(len(x_refs)) for j, (px, py) in enumerate(chips)]
    arrivals = [copy(a, j, me, 2 * px + py, px, py) for a in range(len(x_refs)) for j, (px, py) in enumerate(chips)]
    return sends, arrivals


def _scatter_start(copies):
    for cp in copies[0]:
        cp.start()


def _scatter_finish(copies):
    for cp in copies[1]:
        cp.wait_recv()
    for cp in copies[0]:
        cp.wait_send()


def share_halves(groups, name):
    bufs = [b for grp in groups for b in grp]
    where = [(gi, li) for gi, grp in enumerate(groups) for li in range(len(grp))]
    n = len(bufs)

    def body(*refs):
        x_refs, o_refs = refs[:n], refs[n:n + len(groups)]
        send_sems, recv_sems = refs[n + len(groups):]
        x, y, c, _ = _place()
        sent, arrive = [], []
        for a, (gi, li) in enumerate(where):

            def copy(hf, a=a, gi=gi, li=li):
                return pltpu.make_async_remote_copy(src_ref=x_refs[a], dst_ref=o_refs[gi].at[li, hf],
                                                    send_sem=send_sems.at[a], recv_sem=recv_sems.at[a],
                                                    device_id=(x, y, 1 - c), device_id_type=MESH_ID)

            sent.append(copy(c))
            arrive.append(copy(1 - c))
        for cp in sent:
            cp.start()
        for cp in arrive:
            cp.wait_recv()
        for cp in sent:
            cp.wait_send()

    outs = pl.pallas_call(
        body,
        name=name,
        in_specs=[ANY] * n,
        out_specs=[ANY] * len(groups),
        out_shape=[jax.ShapeDtypeStruct((len(grp), 2) + grp[0].shape, grp[0].dtype) for grp in groups],
        scratch_shapes=[pltpu.SemaphoreType.DMA((n,)), pltpu.SemaphoreType.DMA((n,))],
    )(*bufs)
    c = lax.axis_index("c")
    full = [lax.dynamic_update_index_in_dim(o, jnp.stack(grp), c, 1) for o, grp in zip(outs, groups)]
    return [t.reshape(t.shape[0], 2 * t.shape[2], t.shape[3]) for t in full]


def pair_sums(bufs, landed, dtypes, tag):
    core = lax.axis_index("c").astype(jnp.int32).reshape(1)
    return [pair_sum(b, l, core, dt, "rs_pair_sum_%s%d" % (tag, i)) for i, (b, l, dt) in enumerate(zip(bufs, landed, dtypes))]


def _row_tiles(length):
    return (640, 320) if length > 2048 else (128, 64)


def _divisor_tile(rows, target):
    return max(t for t in range(8, min(rows, target) + 1, 8) if rows % t == 0)


def _local_step(x, target, wt, late_shards, layout_late, complete_grads, sum_pairs):
    seq, d = x.shape
    length = N_META + seq
    tm, tm_ffn = _row_tiles(length)
    lp = -(-length // tm) * tm
    tail = jnp.zeros((lp - length, d), F32)
    h0 = jnp.concatenate([wt["meta"], x, tail], axis=0)[None]
    tgt = jnp.concatenate([jnp.zeros((N_META, d), F32), target, tail], axis=0)
    nn = functools.partial(mm_nn, tm=_divisor_tile(lp, 1664))
    nt = functools.partial(mm_nt, tm=_divisor_tile(lp, 1040))
    tn = functools.partial(mm_tn, tm=_divisor_tile(lp, 1664), rb=256)
    nn_ln = functools.partial(mm_nn_ln, tm=_divisor_tile(lp, 832))
    nt_ln_bwd = functools.partial(mm_nt_ln_bwd, tm=_divisor_tile(lp, 1040))
    ln_g = [wt["ln_mix_g"][0:1], wt["ln_ffn_g"][0:1], wt["ln_mix_g"][1:2], wt["ln_ffn_g"][1:2]]
    ln_b = [wt["ln_mix_b"][0:1], wt["ln_ffn_b"][0:1], wt["ln_mix_b"][1:2], wt["ln_ffn_b"][1:2]]

    p3 = nn(h0, wt["a3"], name="a_in3")
    pz = nn(h0, wt["az"], name="a_inz")
    pba = nn(h0, wt["a_ba"], name="a_inba")
    qkv = gdn_pre_fwd(p3, wt["a_conv3"], tm=tm, cb=2 * HEAD_DIM)
    gates = gdn_gates_fwd(pba, wt["alog_lanes"], wt["dtb_lanes"], tm=tm)
    o, states, tinv, late_stacks = gdn_chunk_fwd(qkv, gates, late_shards)
    wt = {**wt, **layout_late(late_stacks)}
    onz = gdn_post_fwd(o[None], pz, wt["anorm_b"], tm=tm)
    r1, h1 = nn_ln(onz, wt["a_out"], h0, ln_g[0], ln_b[0], name="a_out_ln1")
    up0 = nn(h1, wt["up"][0], name="up0")
    act0 = ffn_act_fwd(up0, wt["fconv"][0], tm=tm_ffn, name="ffn_act0")
    r2, h2 = nn_ln(act0, wt["down"][0], h1, ln_g[1], ln_b[1], name="down0_ln2")
    pb = nn(h2, wt["b_in"], name="b_in")
    sc = sc_fwd(pb, wt["b_conv"], tm=tm_ffn, cb=d)
    r3, h3 = nn_ln(sc, wt["b_out"], h2, ln_g[2], ln_b[2], name="b_out_ln3")
    up1 = nn(h3, wt["up"][1], name="up1")
    act1 = ffn_act_fwd(up1, wt["fconv"][1], tm=tm_ffn, name="ffn_act1")
    r4, h4 = nn_ln(act1, wt["down"][1], h3, ln_g[3], ln_b[3], name="down1_ln4")

    grads = {}
    dr4, dgb4, loss_part = loss_ln_bwd(h4, tgt, r4, ln_g[3], first=N_META, count=seq, tm=tm)
    d_down1 = tn(act1, dr4, name="d_down1")
    dact1 = nt(dr4, wt["down"][1], name="d_act1")
    dup1, dfconv1 = ffn_act_bwd(up1, dact1, wt["fconv"][1], tm=tm_ffn, name="ffn_act1_bwd")
    dup1 = dup1.reshape(up1.shape)
    d_up1 = tn(h3, dup1, name="d_up1")

    dr3, dgb3, _ = nt_ln_bwd(dup1, wt["up"][1], dr4, r3, ln_g[2], name="d_h3_ln3")
    d_bout = tn(sc, dr3, name="d_b_out")
    dsc = nt(dr3, wt["b_out"], name="d_sc")
    dpb, dbconv = sc_bwd(pb, dsc, wt["b_conv"], tm=tm_ffn, cb=d)
    d_bin = tn(h2, dpb, name="d_b_in")

    dr2, dgb2, _ = nt_ln_bwd(dpb, wt["b_in"], dr3, r2, ln_g[1], name="d_h2_ln2")
    d_down0 = tn(act0, dr2, name="d_down0")
    dact0 = nt(dr2, wt["down"][0], name="d_act0")
    dup0, dfconv0 = ffn_act_bwd(up0, dact0, wt["fconv"][0], tm=tm_ffn, name="ffn_act0_bwd")
    dup0 = dup0.reshape(up0.shape)
    d_up0 = tn(h1, dup0, name="d_up0")
    grads["b_w_in"] = [d_bin[0].transpose(1, 0, 2).reshape(d, 4, 3 * d // 4).transpose(1, 0, 2)]
    grads["b_w_out"] = [d_bout.reshape(4, d // 4, d)]
    grads["ffn_w_up"] = [d_up0[0], d_up1[0]]
    grads["ffn_w_down"] = [t.reshape(4, -1, d) for t in (d_down0, d_down1)]
    complete = complete_grads(grads)

    dr1, dgb1, from_sibling = nt_ln_bwd(dup0, wt["up"][0], dr2, r1, ln_g[0], name="d_h1_ln1", swap=complete)
    leaving = sum_pairs(complete, from_sibling)
    d_aout = tn(onz, dr1, name="d_a_out")
    donz = nt(dr1, wt["a_out"], name="d_onz")
    d_o, dz, dnw = gdn_post_bwd(o[None], pz, donz, wt["anorm_b"], tm=tm)
    dqkv, dgates, landed = gdn_chunk_bwd(qkv, gates, states, tinv, d_o[0], leaving)
    dp3, daconv = gdn_pre_bwd(p3, dqkv, wt["a_conv3"], tm=tm, cb=2 * HEAD_DIM)
    dpba, dscal = gdn_gates_bwd(pba, dgates, wt["alog_lanes"], wt["dtb_lanes"], tm=tm)
    d_a3 = tn(h0, dp3, name="d_a_in3")
    d_az = tn(h0, dz, name="d_a_inz")
    d_aba = tn(h0, dpba, name="d_a_inba")
    dh0 = nt(dp3, wt["a3"], res=dr1, res_scale=ALPHA, name="d_h0a")
    dh0 = nt(dz, wt["az"], res=dh0, res_scale=1.0, name="d_h0z")
    dh0 = nt(dpba, wt["a_ba"], res=dh0, res_scale=1.0, name="d_h0")

    width = HEADS * HEAD_DIM
    d_a_in = jnp.concatenate([d_a3[0, 0], d_a3[0, 1], d_a3[0, 2], d_az[0, 0], d_aba[0, 0][:, :2 * HEADS]], axis=1)
    n_in = d_a_in.shape[1] // 4
    grads["a_w_in"] = [d_a_in.reshape(d, 4, n_in).transpose(1, 0, 2)]
    grads["a_w_out"] = [d_aout.reshape(4, width // 4, d)]
    grads["a_conv"] = daconv.sum(axis=2).transpose(1, 0, 2).reshape(1, GDN_CONV, 3 * width)
    per_head = dscal.sum(axis=1)[:, HEADS:2 * HEADS]
    grads["a_log"] = per_head[0][None]
    grads["a_dt_bias"] = per_head[1][None]
    grads["a_norm"] = dnw.reshape(8, HEADS, HEAD_DIM).sum(axis=(0, 1))[None]
    grads["b_conv"] = dbconv.sum(axis=1)[None]
    lns = [dgb1, dgb2, dgb3, dgb4]
    grads["ln_mix_g"] = jnp.stack([lns[0][0].sum(0), lns[2][0].sum(0)])
    grads["ln_mix_b"] = jnp.stack([lns[0][1].sum(0), lns[2][1].sum(0)])
    grads["ln_ffn_g"] = jnp.stack([lns[1][0].sum(0), lns[3][0].sum(0)])
    grads["ln_ffn_b"] = jnp.stack([lns[1][1].sum(0), lns[3][1].sum(0)])
    grads["ffn_conv"] = jnp.stack([t.sum(axis=2).transpose(1, 0, 2).reshape(FFN_CONV, -1) for t in (dfconv0, dfconv1)])
    grads["meta"] = dh0[0, :N_META]
    return loss_part, dh0, grads, landed


WEIGHTS = ["meta", "a_w_in", "a_conv", "a_log", "a_dt_bias", "a_norm", "a_w_out", "b_w_in", "b_conv", "b_w_out",
           "ln_mix_g", "ln_mix_b", "ffn_w_up", "ffn_conv", "ffn_w_down", "ln_ffn_g", "ln_ffn_b"]
EARLY_WEIGHTS = ["a_w_in", "a_w_out"]
LATE_WEIGHTS = ["b_w_in", "b_w_out", "ffn_w_up", "ffn_w_down"]
MATMUL_WEIGHTS = EARLY_WEIGHTS + LATE_WEIGHTS
SMALL_SHARDED = ["a_conv", "b_conv", "ffn_conv", "meta"]
REPLICATED = ["a_log", "a_dt_bias", "a_norm", "ln_mix_g", "ln_mix_b", "ln_ffn_g", "ln_ffn_b"]
SHARD_AXIS = {"meta": 1, "a_w_in": 2, "a_conv": 2, "a_w_out": 1, "b_w_in": 2, "b_conv": 2, "b_w_out": 1,
              "ffn_w_up": 2, "ffn_conv": 2, "ffn_w_down": 1}
PACK_COLS = 1024
PACK_ROWS_MULTIPLE = 32


def _pack(pieces, lead=()):
    flat = jnp.concatenate([p.reshape(lead + (-1,)) for p in pieces], axis=-1)
    n = flat.shape[-1]
    rows = -(-n // (PACK_COLS * PACK_ROWS_MULTIPLE)) * PACK_ROWS_MULTIPLE
    flat = jnp.pad(flat, [(0, 0)] * len(lead) + [(0, rows * PACK_COLS - n)])
    return flat.reshape(lead + (rows, PACK_COLS))


def _unpack(buf, shapes, lead=()):
    flat = buf.reshape(lead + (-1,))
    out, off = [], 0
    for shp in shapes:
        n = 1
        for s in shp:
            n *= s
        out.append(flat[..., off:off + n].reshape(lead + tuple(shp)))
        off += n
    return out


def _join_shards(stacked, axis):
    return jnp.concatenate([stacked[k] for k in range(4)], axis=axis)


def _split_shards(full, axis):
    return jnp.stack(jnp.split(full, 4, axis=axis))


def _weight_layers(w, names):
    return [w[n][l].astype(BF16) for n in names for l in range(w[n].shape[0])]


def _per_weight(arrays, w, names):
    it = iter(arrays)
    return {n: [next(it) for _ in range(w[n].shape[0])] for n in names}


def _layout_early(full, w):
    width = HEADS * HEAD_DIM
    wt = {n: w[n] for n in ("ln_mix_g", "ln_mix_b", "ln_ffn_g", "ln_ffn_b")}
    w_in = _join_shards(full["a_w_in"][0], 1)
    d = w_in.shape[0]
    n_ff = full["ffn_conv"].shape[2] // 2
    blocks = [w_in[:, s * width:(s + 1) * width] for s in range(4)]
    wt["a3"] = jnp.stack(blocks[:3])[None]
    wt["az"] = blocks[3][None, None]
    wt["a_ba"] = jnp.pad(w_in[:, 4 * width:], ((0, 0), (0, HEAD_DIM - 2 * HEADS)))[None, None]
    wt["a_out"] = full["a_w_out"][0].reshape(1, 1, width, d)
    wt["a_conv3"] = full["a_conv"][0].reshape(GDN_CONV, 3, width).transpose(1, 0, 2)
    wt["b_conv"] = full["b_conv"][0]
    wt["fconv"] = [full["ffn_conv"][l].reshape(FFN_CONV, 2, n_ff).transpose(1, 0, 2) for l in range(2)]
    wt["meta"] = full["meta"]
    in_g_lanes = (HEADS, HEAD_DIM - 2 * HEADS)
    wt["alog_lanes"] = jnp.pad(w["a_log"][0], in_g_lanes)[None]
    wt["dtb_lanes"] = jnp.pad(w["a_dt_bias"][0], in_g_lanes)[None]
    wt["anorm_b"] = jnp.tile(w["a_norm"][0], HEADS)[None]
    return wt


def _layout_late(full):
    d = full["b_w_in"][0].shape[1]
    n_ff = full["ffn_w_up"][0].shape[2]
    return {
        "b_in": _join_shards(full["b_w_in"][0], 1).reshape(d, 3, d).transpose(1, 0, 2)[None],
        "b_out": full["b_w_out"][0].reshape(1, 1, d, d),
        "up": [t[None] for t in full["ffn_w_up"]],
        "down": [t.reshape(2, 1, n_ff, d) for t in full["ffn_w_down"]],
    }


def kernel(x, meta, a_w_in, a_conv, a_log, a_dt_bias, a_norm, a_w_out, b_w_in, b_conv, b_w_out, ln_mix_g, ln_mix_b, ffn_w_up, ffn_conv, ffn_w_down, ln_ffn_g, ln_ffn_b, loss_target, m_meta, m_a_w_in, m_a_conv, m_a_log, m_a_dt_bias, m_a_norm, m_a_w_out, m_b_w_in, m_b_conv, m_b_w_out, m_ln_mix_g, m_ln_mix_b, m_ffn_w_up, m_ffn_conv, m_ffn_w_down, m_ln_ffn_g, m_ln_ffn_b, v_meta, v_a_w_in, v_a_conv, v_a_log, v_a_dt_bias, v_a_norm, v_a_w_out, v_b_w_in, v_b_conv, v_b_w_out, v_ln_mix_g, v_ln_mix_b, v_ffn_w_up, v_ffn_conv, v_ffn_w_down, v_ln_ffn_g, v_ln_ffn_b):
    w = dict(meta=meta, a_w_in=a_w_in, a_conv=a_conv, a_log=a_log, a_dt_bias=a_dt_bias, a_norm=a_norm, a_w_out=a_w_out,
             b_w_in=b_w_in, b_conv=b_conv, b_w_out=b_w_out, ln_mix_g=ln_mix_g, ln_mix_b=ln_mix_b, ffn_w_up=ffn_w_up,
             ffn_conv=ffn_conv, ffn_w_down=ffn_w_down, ln_ffn_g=ln_ffn_g, ln_ffn_b=ln_ffn_b)
    m = dict(meta=m_meta, a_w_in=m_a_w_in, a_conv=m_a_conv, a_log=m_a_log, a_dt_bias=m_a_dt_bias, a_norm=m_a_norm,
             a_w_out=m_a_w_out, b_w_in=m_b_w_in, b_conv=m_b_conv, b_w_out=m_b_w_out, ln_mix_g=m_ln_mix_g,
             ln_mix_b=m_ln_mix_b, ffn_w_up=m_ffn_w_up, ffn_conv=m_ffn_conv, ffn_w_down=m_ffn_w_down,
             ln_ffn_g=m_ln_ffn_g, ln_ffn_b=m_ln_ffn_b)
    v = dict(meta=v_meta, a_w_in=v_a_w_in, a_conv=v_a_conv, a_log=v_a_log, a_dt_bias=v_a_dt_bias, a_norm=v_a_norm,
             a_w_out=v_a_w_out, b_w_in=v_b_w_in, b_conv=v_b_conv, b_w_out=v_b_w_out, ln_mix_g=v_ln_mix_g,
             ln_mix_b=v_ln_mix_b, ffn_w_up=v_ffn_w_up, ffn_conv=v_ffn_conv, ffn_w_down=v_ffn_w_down,
             ln_ffn_g=v_ln_ffn_g, ln_ffn_b=v_ln_ffn_b)
    seq = x.shape[1]
    *stacks, small = all_gather_shards(_weight_layers(w, EARLY_WEIGHTS) + [_pack([w[n] for n in SMALL_SHARDED])],
                                       "gather_early")
    full = _per_weight(stacks, w, EARLY_WEIGHTS)
    for n, t in zip(SMALL_SHARDED, _unpack(small, [w[n].shape for n in SMALL_SHARDED], lead=(4,))):
        full[n] = _join_shards(t, SHARD_AXIS[n])

    def layout_late(late_stacks):
        return _layout_late(_per_weight(late_stacks, w, LATE_WEIGHTS))

    def complete_grads(grads):
        return [g for n in LATE_WEIGHTS for g in grads[n]]

    def sum_pairs(bufs, from_sibling):
        return pair_sums(bufs, from_sibling, [BF16] * len(bufs), "late")

    loss_part, dh0, grads, landed_late = _local_step(x[0], loss_target[0], _layout_early(full, w),
                                                     _weight_layers(w, LATE_WEIGHTS), layout_late, complete_grads, sum_pairs)
    pieces = [_split_shards(grads[n], SHARD_AXIS[n]) for n in SMALL_SHARDED]
    same = jnp.concatenate([grads[n].reshape(-1) for n in REPLICATED] + [jnp.sum(loss_part).reshape(1)])
    pieces.append(jnp.broadcast_to(same, (4,) + same.shape))
    bufs = [g for n in EARLY_WEIGHTS for g in grads[n]] + [_pack(pieces, lead=(4,))]
    from_sibling = swap_halves(bufs, "rs_pair_early")
    landed = scatter_to_chips(pair_sums(bufs, from_sibling, [BF16] * (len(bufs) - 1) + [F32], "early"), "rs_chips_early")
    totals = [chip_sum(t, "rs_chip_sum%d" % i) for i, t in enumerate(landed + landed_late)]
    by_weight = _per_weight(totals[:len(bufs) - 1] + totals[len(bufs):], w, MATMUL_WEIGHTS)
    *shared, small_total = share_halves([by_weight[n] for n in MATMUL_WEIGHTS] + [[totals[len(bufs) - 1]]], "rs_share")
    grad_w = {n: t.reshape(w[n].shape) for n, t in zip(MATMUL_WEIGHTS, shared)}
    rest = SMALL_SHARDED + REPLICATED
    unpacked = _unpack(small_total[0], [w[n].shape for n in rest] + [()])
    grad_w.update(zip(rest, unpacked[:-1]))
    loss = unpacked[-1]
    grad_x = dh0[:, N_META:N_META + seq]
    steps = [adamw(w[n], grad_w[n], m[n], v[n], "adamw_" + n) for n in WEIGHTS]
    return (loss, grad_x, *[grad_w[n] for n in WEIGHTS], *[s[0] for s in steps], *[s[1] for s in steps],
            *[s[2] for s in steps])
```

```python
import functools

import jax
import jax.numpy as jnp
from jax import lax
from jax.experimental import pallas as pl
from jax.experimental.pallas import tpu as pltpu

F32 = jnp.float32
BF16 = jnp.bfloat16

N_META = 16
HEADS = 8
HEAD_DIM = 128
CHUNK = 64
GDN_CONV = 4
FFN_CONV = 3
ALPHA = 4.0 ** 0.25
LN_EPS = 1e-5
RMS_EPS = 1e-6
L2_EPS = 1e-6
Q_SCALE = HEAD_DIM ** -0.5

ADAM_LR = 0.001
ADAM_B1 = 0.9
ADAM_B2 = 0.999
ADAM_EPS = 1e-08
ADAM_WD = 0.01
ADAM_STEP = 10

HALO = 8
VMEM_LIMIT = 48 * 1024 * 1024


def _params(sem=None):
    return pltpu.CompilerParams(dimension_semantics=sem, vmem_limit_bytes=VMEM_LIMIT)


def _dot(a, b, prec=None):
    return jnp.dot(a, b, preferred_element_type=F32, precision=prec)


def _dot_nt(a, b, prec=None):
    return lax.dot_general(a, b, (((1,), (1,)), ((), ())), preferred_element_type=F32, precision=prec)


def _dot_tn(a, b, prec=None):
    return lax.dot_general(a, b, (((0,), (0,)), ((), ())), preferred_element_type=F32, precision=prec)


def _sigmoid(x):
    return 0.5 * jnp.tanh(0.5 * x) + 0.5


def _tri_masks():
    r = lax.broadcasted_iota(jnp.int32, (CHUNK, CHUNK), 0)
    c = lax.broadcasted_iota(jnp.int32, (CHUNK, CHUNK), 1)
    return r >= c, r > c, r == c


def _split_hi_lo(x):
    hi = x.astype(BF16)
    return hi, (x - hi.astype(F32)).astype(BF16)


def _mask_dot(mask, x):
    hi, lo = _split_hi_lo(x)
    return _dot(mask, hi) + _dot(mask, lo)


def _cumsum_rows(g):
    causal, _, _ = _tri_masks()
    return _mask_dot(causal.astype(BF16), g)


def _cumsum_rows_transposed(dy):
    _, strict, _ = _tri_masks()
    return _mask_dot((~strict).astype(BF16), dy)


def _dot_split3(a, b):
    a_hi, a_lo = _split_hi_lo(a)
    b_hi, b_lo = _split_hi_lo(b)
    return _dot(a_hi, b_hi) + (_dot(a_hi, b_lo) + _dot(a_lo, b_hi))


@jax.custom_vjp
def _dot_precise(a, b):
    return _dot_split3(a, b)


def _dot_precise_fwd(a, b):
    return _dot_split3(a, b), (a, b)


def _dot_precise_bwd(operands, ct):
    a, b = operands
    return _dot_split3(ct, b.T), _dot_split3(a.T, ct)


_dot_precise.defvjp(_dot_precise_fwd, _dot_precise_bwd)


def _gdn_m(ks, a64s, bbs):
    causal, strict, _ = _tri_masks()
    decay = [jnp.exp(jnp.where(causal, x - x.T, -1e30)) for x in a64s]
    kk = [_dot_nt(k * b, k) for k, b in zip(ks, bbs)]
    return [jnp.where(strict, x * d, 0.0) for x, d in zip(kk, decay)]


def _gdn_inverse_stages(ks, a64s, bbs):
    ms = _gdn_m(ks, a64s, bbs)
    yield
    r = lax.broadcasted_iota(jnp.int32, (CHUNK, CHUNK), 0)
    c = lax.broadcasted_iota(jnp.int32, (CHUNK, CHUNK), 1)
    eye = (r == c).astype(F32)
    same = [jnp.right_shift(r, s) == jnp.right_shift(c, s) for s in (3, 4, 5)]
    d = [jnp.where(same[0], m, 0.0) for m in ms]
    p = [_dot(x, x) for x in d]
    yield
    t = [eye - x for x in d]
    t = [x + _dot(x, y) for x, y in zip(t, p)]
    p = [_dot(x, x) for x in p]
    yield
    t = [x + _dot(x, y) for x, y in zip(t, p)]
    yield
    for inner, outer in ((same[0], same[1]), (same[1], same[2]), (same[2], None)):
        joins = ~inner if outer is None else (outer & ~inner)
        o = [_dot(x, jnp.where(joins, m, 0.0)) for x, m in zip(t, ms)]
        yield
        t = [x - _dot(y, x) for x, y in zip(t, o)]
        yield
    res = [eye - x - _dot_split3(m, x) for m, x in zip(ms, t)]
    yield
    return [x + _dot(x, y) for x, y in zip(t, res)]


def _gdn_apply_stages(qs, ks, vs, gc, a64s, gl, bbs, ss, ts):
    causal, _, _ = _tri_masks()
    n = range(len(qs))
    qk = [_dot_nt(qs[h], ks[h]) for h in n]
    yield
    decay = [jnp.exp(jnp.where(causal, x - x.T, -1e30)) for x in a64s]
    eg = [jnp.exp(x) for x in gc]
    u = [_dot_precise(ts[h], vs[h] * bbs[h]) for h in n]
    w = [_dot_precise(ts[h], ks[h] * bbs[h] * eg[h]) for h in n]
    qk = [qk[h] * decay[h] for h in n]
    kd = [ks[h] * jnp.exp(gl[h] - gc[h]) for h in n]
    yield
    v_new = [u[h] - _dot(w[h], ss[h]) for h in n]
    q_s = [_dot(qs[h] * eg[h], ss[h]) for h in n]
    yield
    o = [q_s[h] + _dot(qk[h], v_new[h]) for h in n]
    s2 = [ss[h] * jnp.exp(gl[h]) + _dot_tn(kd[h], v_new[h]) for h in n]
    return o, s2


def _run_stages(*generators):
    results = [None] * len(generators)
    live = dict(enumerate(generators))
    while live:
        for i, gen in list(live.items()):
            try:
                next(gen)
            except StopIteration as stop:
                results[i] = stop.value
                del live[i]
    return results


def _head_slices(h):
    return slice(h * HEAD_DIM, (h + 1) * HEAD_DIM), slice(h * HEAD_DIM, h * HEAD_DIM + CHUNK)


def _gdn_head_values(x_ref, gate_ref):
    heads = range(HEADS)
    qs, ks, vs = ([x_ref[s, :, _head_slices(h)[0]] for h in heads] for s in range(3))
    gate = gate_ref[...]
    cumulative = _cumsum_rows(gate)
    total = jnp.sum(gate, axis=0, keepdims=True)
    gcums = [cumulative[:, HEADS + h:HEADS + h + 1] for h in heads]
    gtots = [total[:, HEADS + h:HEADS + h + 1] for h in heads]
    bcols = [gate[:, h:h + 1] for h in heads]
    return qs, ks, vs, gcums, gtots, bcols


def _over_lanes(cols, lanes):
    return [jnp.broadcast_to(c, (c.shape[0], lanes)) for c in cols]


def _gdn_inverse_cols(ks, gcums, bcols):
    return _gdn_inverse_stages(ks, _over_lanes(gcums, CHUNK), _over_lanes(bcols, HEAD_DIM))


def _gdn_apply_cols_stages(qs, ks, vs, gcums, gtots, bcols, ss, ts):
    return _gdn_apply_stages(qs, ks, vs, _over_lanes(gcums, HEAD_DIM), _over_lanes(gcums, CHUNK),
                             _over_lanes(gtots, HEAD_DIM), _over_lanes(bcols, HEAD_DIM), ss, ts)


def _gdn_apply_cols(qs, ks, vs, gcums, gtots, bcols, ss, ts):
    return _run_stages(_gdn_apply_cols_stages(qs, ks, vs, gcums, gtots, bcols, ss, ts))[0]


def _gdn_m_cols(ks, gcums, bcols):
    return _gdn_m(ks, _over_lanes(gcums, CHUNK), _over_lanes(bcols, HEAD_DIM))


def _gate_lanes(bcols, gcols):
    rows = gcols[0].shape[0]
    lane = lax.broadcasted_iota(jnp.int32, (rows, HEAD_DIM), 1)
    out = jnp.zeros((rows, HEAD_DIM), F32)
    for h in range(HEADS):
        if bcols is not None:
            out = jnp.where(lane == h, jnp.broadcast_to(bcols[h], out.shape), out)
        out = jnp.where(lane == HEADS + h, jnp.broadcast_to(gcols[h], out.shape), out)
    return out


def _gate_gradient(dbcols, dgcums, dgtots):
    block = _gate_lanes(dbcols, dgcums)
    lane = lax.broadcasted_iota(jnp.int32, block.shape, 1)
    return jnp.where(lane < HEADS, block, _cumsum_rows_transposed(block) + _gate_lanes(None, dgtots))


def gdn_chunk_fwd(qkv, gates, gather=()):
    _, lp, width = qkv.shape
    n_chunks = lp // CHUNK
    n = len(gather)

    def body(x_ref, gate_ref, next_ref, next_gate_ref, *refs):
        shard_refs, (o_ref, s_ref, t_ref), refs = refs[:n], refs[n:n + 3], refs[n + 3:]
        stack_refs, state, t_next, sems = refs[:n], refs[n], refs[n + 1], refs[n + 2:]
        copies = _gather_copies(shard_refs, stack_refs, *sems) if n else None

        def inverse_stages(ref, g_ref):
            _, ks, _, gcums, _, bcols = _gdn_head_values(ref, g_ref)
            return _gdn_inverse_cols(ks, gcums, bcols)

        @pl.when(pl.program_id(0) == 0)
        def _():
            state[...] = jnp.zeros_like(state)
            for h, t in enumerate(_run_stages(inverse_stages(x_ref, gate_ref))[0]):
                t_next[h] = t
            if n:
                _gather_start(copies)

        qs, ks, vs, gcums, gtots, bcols = _gdn_head_values(x_ref, gate_ref)
        ss = [state[h] for h in range(HEADS)]
        ts = [t_next[h] for h in range(HEADS)]
        ts_next, (os_, s2) = _run_stages(inverse_stages(next_ref, next_gate_ref),
                                         _gdn_apply_cols_stages(qs, ks, vs, gcums, gtots, bcols, ss, ts))
        for h in range(HEADS):
            s_ref[0, h] = ss[h]
            t_ref[0, h] = ts[h]
            t_next[h] = ts_next[h]
            o_ref[:, _head_slices(h)[0]] = os_[h]
            state[h] = s2[h]

        if n:
            @pl.when(pl.program_id(0) == n_chunks - 1)
            def _():
                _gather_finish(copies)

    o, states, tinv, *stacks = pl.pallas_call(
        body,
        name="gdn_chunk_fwd",
        grid=(n_chunks,),
        in_specs=[pl.BlockSpec((3, CHUNK, width), lambda c: (0, c, 0)),
                  pl.BlockSpec((CHUNK, HEAD_DIM), lambda c: (c, 0)),
                  pl.BlockSpec((3, CHUNK, width), lambda c: (0, jnp.minimum(c + 1, n_chunks - 1), 0)),
                  pl.BlockSpec((CHUNK, HEAD_DIM), lambda c: (jnp.minimum(c + 1, n_chunks - 1), 0))] + [ANY] * n,
        out_specs=[
            pl.BlockSpec((CHUNK, width), lambda c: (c, 0)),
            pl.BlockSpec((1, HEADS, HEAD_DIM, HEAD_DIM), lambda c: (c, 0, 0, 0)),
            pl.BlockSpec((1, HEADS, CHUNK, CHUNK), lambda c: (c, 0, 0, 0)),
        ] + [ANY] * n,
        out_shape=[
            jax.ShapeDtypeStruct((lp, width), F32),
            jax.ShapeDtypeStruct((n_chunks, HEADS, HEAD_DIM, HEAD_DIM), F32),
            jax.ShapeDtypeStruct((n_chunks, HEADS, CHUNK, CHUNK), F32),
        ] + _gather_out_shapes(gather),
        scratch_shapes=[pltpu.VMEM((HEADS, HEAD_DIM, HEAD_DIM), F32), pltpu.VMEM((HEADS, CHUNK, CHUNK), F32)]
        + (_gather_sems(n) if n else []),
        compiler_params=_params(("arbitrary",)),
    )(qkv, gates, qkv, gates, *gather)
    return o, states, tinv, _set_own_slots(stacks, gather)


def gdn_chunk_bwd(qkv, gates, states, tinv, d_o, scatter=()):
    _, lp, width = qkv.shape
    n_chunks = lp // CHUNK
    last = n_chunks - 1
    n = len(scatter)

    def body(x_ref, gate_ref, s_ref, t_ref, do_ref, *refs):
        leaving_refs, dx_ref, dgate_ref, refs = refs[:n], refs[n], refs[n + 1], refs[n + 2:]
        landing_refs, dstate, sems = refs[:n], refs[n], refs[n + 1:]
        copies = _scatter_copies(leaving_refs, landing_refs, *sems) if n else None

        @pl.when(pl.program_id(0) == 0)
        def _():
            dstate[...] = jnp.zeros_like(dstate)
            if n:
                _scatter_start(copies)

        heads = range(HEADS)
        qs, ks, vs, gcums, gtots, bcols = _gdn_head_values(x_ref, gate_ref)
        ss = [s_ref[0, h] for h in heads]
        ts = [t_ref[0, h] for h in heads]
        d_out = ([do_ref[:, _head_slices(h)[0]] for h in heads], [dstate[h] for h in heads])
        _, vjp_apply = jax.vjp(_gdn_apply_cols, qs, ks, vs, gcums, gtots, bcols, ss, ts)
        dq, dk, dv, dgc, dgt, db, ds, dt = vjp_apply(d_out)
        tts = [t.T for t in ts]
        dm = [_dot(tts[h], dt[h]) for h in heads]
        dm = [-_dot(dm[h], tts[h]) for h in heads]
        _, vjp_m = jax.vjp(_gdn_m_cols, ks, gcums, bcols)
        dk2, dgc2, db2 = vjp_m(dm)
        for h in heads:
            sl = _head_slices(h)[0]
            dx_ref[0, :, sl] = dq[h]
            dx_ref[1, :, sl] = dk[h] + dk2[h]
            dx_ref[2, :, sl] = dv[h]
            dstate[h] = ds[h]
        dgate_ref[...] = _gate_gradient([db[h] + db2[h] for h in heads], [dgc[h] + dgc2[h] for h in heads], dgt)

        if n:
            @pl.when(pl.program_id(0) == n_chunks - 1)
            def _():
                _scatter_finish(copies)

    dqkv, dgates, *landed = pl.pallas_call(
        body,
        name="gdn_chunk_bwd",
        grid=(n_chunks,),
        in_specs=[
            pl.BlockSpec((3, CHUNK, width), lambda c: (0, last - c, 0)),
            pl.BlockSpec((CHUNK, HEAD_DIM), lambda c: (last - c, 0)),
            pl.BlockSpec((1, HEADS, HEAD_DIM, HEAD_DIM), lambda c: (last - c, 0, 0, 0)),
            pl.BlockSpec((1, HEADS, CHUNK, CHUNK), lambda c: (last - c, 0, 0, 0)),
            pl.BlockSpec((CHUNK, width), lambda c: (last - c, 0)),
        ] + [ANY] * n,
        out_specs=[pl.BlockSpec((3, CHUNK, width), lambda c: (0, last - c, 0)),
                   pl.BlockSpec((CHUNK, HEAD_DIM), lambda c: (last - c, 0))] + [ANY] * n,
        out_shape=[jax.ShapeDtypeStruct(qkv.shape, F32), jax.ShapeDtypeStruct(gates.shape, F32)]
        + [jax.ShapeDtypeStruct(b.shape, b.dtype) for b in scatter],
        scratch_shapes=[pltpu.VMEM((HEADS, HEAD_DIM, HEAD_DIM), F32)] + (_scatter_sems(n) if n else []),
        compiler_params=_params(("arbitrary",)),
    )(qkv, gates, states, tinv, d_o, *scatter)
    return dqkv, dgates, _keep_own_slots(landed, scatter)


def mm_nn(a, b, *, tm, name):
    ks, m, tk = a.shape
    _, ns, _, tn = b.shape

    def body(a_ref, b_ref, o_ref):
        p = _dot(a_ref[...].astype(BF16), b_ref[...])

        @pl.when(pl.program_id(2) == 0)
        def _():
            o_ref[...] = p

        @pl.when(pl.program_id(2) > 0)
        def _():
            o_ref[...] += p

    return pl.pallas_call(
        body,
        name=name,
        grid=(ns, m // tm, ks),
        in_specs=[
            pl.BlockSpec((None, tm, tk), lambda n, i, k: (k, i, 0)),
            pl.BlockSpec((None, None, tk, tn), lambda n, i, k: (k, n, 0, 0)),
        ],
        out_specs=pl.BlockSpec((None, tm, tn), lambda n, i, k: (n, i, 0)),
        out_shape=jax.ShapeDtypeStruct((ns, m, tn), F32),
        compiler_params=_params(("arbitrary", "arbitrary", "arbitrary")),
    )(a, b)


def mm_nt(dy, w, *, tm, name, res=None, res_scale=1.0):
    ns, m, tn = dy.shape
    ks, _, tk, _ = w.shape

    def body(*refs):
        if res is None:
            dy_ref, w_ref, o_ref = refs
        else:
            dy_ref, w_ref, r_ref, o_ref = refs
        p = _dot_nt(dy_ref[...].astype(BF16), w_ref[...])

        @pl.when(pl.program_id(2) == 0)
        def _():
            o_ref[...] = p if res is None else p + res_scale * r_ref[...]

        @pl.when(pl.program_id(2) > 0)
        def _():
            o_ref[...] += p

    in_specs = [
        pl.BlockSpec((None, tm, tn), lambda k, i, n: (n, i, 0)),
        pl.BlockSpec((None, None, tk, tn), lambda k, i, n: (k, n, 0, 0)),
    ]
    args = [dy, w]
    if res is not None:
        in_specs.append(pl.BlockSpec((None, tm, tk), lambda k, i, n: (k, i, 0)))
        args.append(res)
    return pl.pallas_call(
        body,
        name=name,
        grid=(ks, m // tm, ns),
        in_specs=in_specs,
        out_specs=pl.BlockSpec((None, tm, tk), lambda k, i, n: (k, i, 0)),
        out_shape=jax.ShapeDtypeStruct((ks, m, tk), F32),
        compiler_params=_params(("arbitrary", "arbitrary", "arbitrary")),
    )(*args)


def mm_tn(x, dy, *, tm, name, rb=None):
    ks, m, tk = x.shape
    ns, _, tn = dy.shape
    rb = tk if rb is None else rb

    def body(x_ref, dy_ref, o_ref):
        @pl.when(pl.program_id(2) == 0)
        def _():
            o_ref[...] = jnp.zeros_like(o_ref)

        dyb = dy_ref[...].astype(BF16)
        for r in range(0, tk, rb):
            o_ref[r:r + rb, :] += _dot_tn(x_ref[:, r:r + rb].astype(BF16), dyb)

    return pl.pallas_call(
        body,
        name=name,
        grid=(ks, ns, m // tm),
        in_specs=[
            pl.BlockSpec((None, tm, tk), lambda k, n, i: (k, i, 0)),
            pl.BlockSpec((None, tm, tn), lambda k, n, i: (n, i, 0)),
        ],
        out_specs=pl.BlockSpec((None, None, tk, tn), lambda k, n, i: (k, n, 0, 0)),
        out_shape=jax.ShapeDtypeStruct((ks, ns, tk, tn), F32),
        compiler_params=_params(("arbitrary", "arbitrary", "arbitrary")),
    )(x, dy)


def _row_partial(x):
    rows, c = x.shape
    return jnp.sum(x.reshape(rows // 8, 8, c), axis=0)


def _layer_norm(r, g, b):
    mu = jnp.mean(r, axis=-1, keepdims=True)
    xc = r - mu
    var = jnp.mean(xc * xc, axis=-1, keepdims=True)
    return xc * lax.rsqrt(var + LN_EPS) * g + b


def _layer_norm_bwd(x, dh, g):
    mu = jnp.mean(x, axis=-1, keepdims=True)
    xc = x - mu
    rstd = lax.rsqrt(jnp.mean(xc * xc, axis=-1, keepdims=True) + LN_EPS)
    xh = xc * rstd
    dxh = dh * g
    m1 = jnp.mean(dxh, axis=-1, keepdims=True)
    m2 = jnp.mean(dxh * xh, axis=-1, keepdims=True)
    return rstd * (dxh - m1 - xh * m2), _row_partial(dh * xh), _row_partial(dh)


def mm_nn_ln(a, b, h_prev, g, beta, *, tm, name):
    ks, m, tk = a.shape
    d = b.shape[3]

    def body(a_ref, b_ref, hp_ref, g_ref, be_ref, r_ref, h_ref, hb_ref):
        p = _dot(a_ref[...].astype(BF16), b_ref[...])

        @pl.when(pl.program_id(1) == 0)
        def _():
            r_ref[...] = p

        @pl.when(pl.program_id(1) > 0)
        def _():
            r_ref[...] += p

        @pl.when(pl.program_id(1) == ks - 1)
        def _():
            r = ALPHA * hp_ref[...] + r_ref[...]
            r_ref[...] = r
            h = _layer_norm(r, g_ref[...], be_ref[...])
            h_ref[...] = h
            hb_ref[...] = h.astype(BF16)

    row = pl.BlockSpec((None, tm, d), lambda i, k: (0, i, 0))
    vec = pl.BlockSpec((1, d), lambda i, k: (0, 0))
    return pl.pallas_call(
        body,
        name=name,
        grid=(m // tm, ks),
        in_specs=[
            pl.BlockSpec((None, tm, tk), lambda i, k: (k, i, 0)),
            pl.BlockSpec((None, None, tk, d), lambda i, k: (k, 0, 0, 0)),
            row, vec, vec,
        ],
        out_specs=[row, row, row],
        out_shape=[jax.ShapeDtypeStruct((1, m, d), F32)] * 2 + [jax.ShapeDtypeStruct((1, m, d), BF16)],
        compiler_params=_params(("arbitrary", "arbitrary")),
    )(a, b, h_prev, g, beta)


def mm_nt_ln_bwd(dy, w, res, r, g, *, tm, name, swap=()):
    ns, m, tn = dy.shape
    d = w.shape[2]
    n_swap = len(swap)
    last_tile = m // tm - 1

    def body(dy_ref, w_ref, res_ref, r_ref, g_ref, *refs):
        leaving_refs, (dr_ref, dgb_ref), refs = refs[:n_swap], refs[n_swap:n_swap + 2], refs[n_swap + 2:]
        copies = _swap_copies(leaving_refs, refs[:n_swap], *refs[n_swap:]) if n_swap else None
        p = _dot_nt(dy_ref[...].astype(BF16), w_ref[...])

        @pl.when((pl.program_id(0) == 0) & (pl.program_id(1) == 0))
        def _():
            dgb_ref[...] = jnp.zeros_like(dgb_ref)
            if n_swap:
                _swap_start(copies)

        @pl.when(pl.program_id(1) == 0)
        def _():
            dr_ref[...] = p + ALPHA * res_ref[...]

        @pl.when(pl.program_id(1) > 0)
        def _():
            dr_ref[...] += p

        @pl.when(pl.program_id(1) == ns - 1)
        def _():
            for rows in (pl.ds(0, tm // 2), pl.ds(tm // 2, tm // 2)):
                dr, dgamma, dbeta = _layer_norm_bwd(r_ref[rows, :], dr_ref[rows, :], g_ref[...])
                dr_ref[rows, :] = dr
                dgb_ref[0] += dgamma
                dgb_ref[1] += dbeta

        if n_swap:
            @pl.when((pl.program_id(0) == last_tile) & (pl.program_id(1) == ns - 1))
            def _():
                _swap_finish(copies)

    row = pl.BlockSpec((None, tm, d), lambda i, n: (0, i, 0))
    dr, dgb, *landed = pl.pallas_call(
        body,
        name=name,
        grid=(m // tm, ns),
        in_specs=[
            pl.BlockSpec((None, tm, tn), lambda i, n: (n, i, 0)),
            pl.BlockSpec((None, None, d, tn), lambda i, n: (0, n, 0, 0)),
            row, row,
            pl.BlockSpec((1, d), lambda i, n: (0, 0)),
        ] + [ANY] * n_swap,
        out_specs=[row, pl.BlockSpec((2, 8, d), lambda i, n: (0, 0, 0))] + [ANY] * n_swap,
        out_shape=[jax.ShapeDtypeStruct((1, m, d), F32), jax.ShapeDtypeStruct((2, 8, d), F32)] + _swap_out_shapes(swap),
        scratch_shapes=_swap_sems(n_swap) if n_swap else [],
        compiler_params=_params(("arbitrary", "arbitrary")),
    )(dy, w, res, r, g, *swap)
    return dr, dgb, landed


def loss_ln_bwd(h, target, r, g, *, first, count, tm):
    _, lp, d = h.shape

    def body(h_ref, t_ref, r_ref, g_ref, dr_ref, dgb_ref, l_ref):
        row = pl.program_id(0) * tm + lax.broadcasted_iota(jnp.int32, (tm, d), 0)
        valid = (row >= first) & (row < first + count)
        err = jnp.where(valid, h_ref[...] - t_ref[...], 0.0)
        dr, dgamma, dbeta = _layer_norm_bwd(r_ref[...], err * (1.0 / d), g_ref[...])
        dr_ref[...] = dr

        @pl.when(pl.program_id(0) == 0)
        def _():
            dgb_ref[...] = jnp.zeros_like(dgb_ref)
            l_ref[...] = jnp.zeros_like(l_ref)

        dgb_ref[0] += dgamma
        dgb_ref[1] += dbeta
        l_ref[...] += _row_partial(err * err) * (0.5 / d)

    row3 = pl.BlockSpec((None, tm, d), lambda i: (0, i, 0))
    return pl.pallas_call(
        body,
        name="loss_ln4_bwd",
        grid=(lp // tm,),
        in_specs=[row3, pl.BlockSpec((tm, d), lambda i: (i, 0)), row3, pl.BlockSpec((1, d), lambda i: (0, 0))],
        out_specs=[row3, pl.BlockSpec((2, 8, d), lambda i: (0, 0, 0)), pl.BlockSpec((8, d), lambda i: (0, 0))],
        out_shape=[jax.ShapeDtypeStruct((1, lp, d), F32), jax.ShapeDtypeStruct((2, 8, d), F32),
                   jax.ShapeDtypeStruct((8, d), F32)],
        compiler_params=_params(("arbitrary",)),
    )(h, target, r, g)


def _halo_index(tile, tm):
    return jnp.maximum(tile * (tm // HALO) - 1, 0)


def _conv_fwd(xs_ref, w, taps, tm):
    acc = w(0) * xs_ref[pl.ds(HALO - taps + 1, tm), :]
    for j in range(1, taps):
        acc += w(j) * xs_ref[pl.ds(HALO - taps + 1 + j, tm), :]
    return acc


def _conv_bwd_x(dcs_ref, w, taps, tm):
    acc = w(0) * dcs_ref[pl.ds(taps - 1, tm), :]
    for j in range(1, taps):
        acc += w(j) * dcs_ref[pl.ds(taps - 1 - j, tm), :]
    return acc


SUB = 8
LANES = 128
PAIR = 2 * SUB
STRIP_UNROLL = 2


def _pair_rows(r0):
    return pl.ds(r0, SUB), pl.ds(r0 + SUB if isinstance(r0, int) else pl.multiple_of(r0 + SUB, SUB), SUB)


def _shift_down(cur, prev, s):
    if s == 0:
        return cur
    row = lax.broadcasted_iota(jnp.int32, cur.shape, 0)
    return jnp.where(row < s, pltpu.roll(prev, s, axis=0), pltpu.roll(cur, s, axis=0))


def _shift_up(cur, nxt, s):
    if s == 0:
        return cur
    row = lax.broadcasted_iota(jnp.int32, cur.shape, 0)
    return jnp.where(row < SUB - s, pltpu.roll(cur, SUB - s, axis=0), pltpu.roll(nxt, SUB - s, axis=0))


def _silu_parts(c):
    sg = _sigmoid(c)
    return c * sg, sg * (1.0 + c * (1.0 - sg))


def _head_sum(x):
    rows, c = x.shape
    parts = []
    for h in range(c // HEAD_DIM):
        s = jnp.sum(x[:, h * HEAD_DIM:(h + 1) * HEAD_DIM], axis=-1, keepdims=True)
        parts.append(jnp.broadcast_to(s, (rows, HEAD_DIM)))
    return parts[0] if len(parts) == 1 else jnp.concatenate(parts, axis=-1)


def _log1p(y):
    u = 1.0 + y
    d = u - 1.0
    return jnp.where(d == 0.0, y, jnp.log(u) * (y / jnp.where(d == 0.0, 1.0, d)))


def _softplus(x):
    return jnp.maximum(x, 0.0) + _log1p(jnp.exp(-jnp.abs(x)))


def _gate_values(x, al, dt):
    lane = lax.broadcasted_iota(jnp.int32, x.shape, 1)
    is_beta, is_g = lane < HEADS, (lane >= HEADS) & (lane < 2 * HEADS)
    return _sigmoid(x), -jnp.exp(al) * _softplus(x + dt), is_beta, is_g


def gdn_gates_fwd(pba, al, dt, *, tm):
    _, lp, width = pba.shape

    def body(x_ref, al_ref, dt_ref, o_ref):
        beta, g, is_beta, is_g = _gate_values(x_ref[...], al_ref[...], dt_ref[...])
        o_ref[...] = jnp.where(is_beta, beta, jnp.where(is_g, g, 0.0))

    vec = pl.BlockSpec((1, width), lambda i: (0, 0))
    return pl.pallas_call(
        body,
        name="gdn_gates_fwd",
        grid=(lp // tm,),
        in_specs=[pl.BlockSpec((None, tm, width), lambda i: (0, i, 0)), vec, vec],
        out_specs=pl.BlockSpec((tm, width), lambda i: (i, 0)),
        out_shape=jax.ShapeDtypeStruct((lp, width), F32),
        compiler_params=_params(("arbitrary",)),
    )(pba, al, dt)


def gdn_gates_bwd(pba, dgates, al, dt, *, tm):
    _, lp, width = pba.shape

    def body(x_ref, d_ref, al_ref, dt_ref, dx_ref, dsc_ref):
        x = x_ref[...]
        beta, g, is_beta, is_g = _gate_values(x, al_ref[...], dt_ref[...])
        d = d_ref[...]
        dg = jnp.where(is_g, d, 0.0)
        da = dg * -jnp.exp(al_ref[...]) * _sigmoid(x + dt_ref[...])
        dx_ref[...] = jnp.where(is_beta, d * beta * (1.0 - beta), da).astype(dx_ref.dtype)

        @pl.when(pl.program_id(0) == 0)
        def _():
            dsc_ref[...] = jnp.zeros_like(dsc_ref)

        dsc_ref[0] += _row_partial(dg * g)
        dsc_ref[1] += _row_partial(da)

    vec = pl.BlockSpec((1, width), lambda i: (0, 0))
    return pl.pallas_call(
        body,
        name="gdn_gates_bwd",
        grid=(lp // tm,),
        in_specs=[pl.BlockSpec((None, tm, width), lambda i: (0, i, 0)), pl.BlockSpec((tm, width), lambda i: (i, 0)), vec, vec],
        out_specs=[pl.BlockSpec((None, tm, width), lambda i: (0, i, 0)), pl.BlockSpec((2, SUB, width), lambda i: (0, 0, 0))],
        out_shape=[jax.ShapeDtypeStruct((1, lp, width), BF16), jax.ShapeDtypeStruct((2, SUB, width), F32)],
        compiler_params=_params(("arbitrary",)),
    )(pba, dgates, al, dt)


def gdn_pre_fwd(p3, conv_w, *, tm, cb):
    _, lp, width = p3.shape
    taps = conv_w.shape[1]

    def body(x_ref, halo_ref, w_ref, o_ref, xs):
        i = pl.program_id(1)
        for s in range(3):
            xs[s, 0:HALO, :] = jnp.where(i > 0, halo_ref[s], 0.0)
            xs[s, HALO:, :] = x_ref[s]
            c = _conv_fwd(xs.at[s], lambda j, s=s: w_ref[s, j:j + 1, :], taps, tm)
            y, _ = _silu_parts(c)
            if s < 2:
                y = y * lax.rsqrt(_head_sum(y * y) + L2_EPS)
                if s == 0:
                    y = y * Q_SCALE
            o_ref[s] = y

    return pl.pallas_call(
        body,
        name="gdn_pre_fwd",
        grid=(width // cb, lp // tm),
        in_specs=[
            pl.BlockSpec((3, tm, cb), lambda j, i: (0, i, j)),
            pl.BlockSpec((3, HALO, cb), lambda j, i: (0, _halo_index(i, tm), j)),
            pl.BlockSpec((3, taps, cb), lambda j, i: (0, 0, j)),
        ],
        out_specs=pl.BlockSpec((3, tm, cb), lambda j, i: (0, i, j)),
        out_shape=jax.ShapeDtypeStruct((3, lp, width), F32),
        scratch_shapes=[pltpu.VMEM((3, tm + HALO, cb), F32)],
        compiler_params=_params(("arbitrary", "arbitrary")),
    )(p3, p3, conv_w)


def gdn_pre_bwd(p3, dqkv, conv_w, *, tm, cb):
    _, lp, width = p3.shape
    taps = conv_w.shape[1]
    last = lp // tm - 1

    def body(x_ref, halo_ref, d_ref, w_ref, dx_ref, dw_ref, xs, dcs, carry):
        step = pl.program_id(1)
        tile = last - step

        @pl.when(step == 0)
        def _():
            carry[...] = jnp.zeros_like(carry)
            dw_ref[...] = jnp.zeros_like(dw_ref)

        for s in range(3):
            w = lambda j, s=s: w_ref[s, j:j + 1, :]
            xs[s, 0:HALO, :] = jnp.where(tile > 0, halo_ref[s], 0.0)
            xs[s, HALO:, :] = x_ref[s]
            c = _conv_fwd(xs.at[s], w, taps, tm)
            y, dsilu = _silu_parts(c)
            dy = d_ref[s]
            if s < 2:
                rn = lax.rsqrt(_head_sum(y * y) + L2_EPS)
                yn = y * rn
                if s == 0:
                    dy = dy * Q_SCALE
                dy = rn * (dy - yn * _head_sum(dy * yn))
            dc = dy * dsilu
            dcs[s, 0:tm, :] = dc
            dcs[s, tm:, :] = carry[s]
            dx_ref[s] = _conv_bwd_x(dcs.at[s], w, taps, tm).astype(dx_ref.dtype)
            carry[s] = dc[0:HALO, :]
            for j in range(taps):
                dw_ref[s, j] += _row_partial(dc * xs[s, pl.ds(HALO - taps + 1 + j, tm), :])

    tile_spec = pl.BlockSpec((3, tm, cb), lambda j, i: (0, last - i, j))
    return pl.pallas_call(
        body,
        name="gdn_pre_bwd",
        grid=(width // cb, lp // tm),
        in_specs=[
            tile_spec,
            pl.BlockSpec((3, HALO, cb), lambda j, i: (0, _halo_index(last - i, tm), j)),
            tile_spec,
            pl.BlockSpec((3, taps, cb), lambda j, i: (0, 0, j)),
        ],
        out_specs=[tile_spec, pl.BlockSpec((3, taps, SUB, cb), lambda j, i: (0, 0, 0, j))],
        out_shape=[jax.ShapeDtypeStruct((3, lp, width), BF16), jax.ShapeDtypeStruct((3, taps, SUB, width), F32)],
        scratch_shapes=[
            pltpu.VMEM((3, tm + HALO, cb), F32),
            pltpu.VMEM((3, tm + HALO, cb), F32),
            pltpu.VMEM((3, HALO, cb), F32),
        ],
        compiler_params=_params(("arbitrary", "arbitrary")),
    )(p3, p3, dqkv, conv_w)


def gdn_post_fwd(o, z, nw_b, *, tm):
    _, lp, width = o.shape

    def body(o_ref, z_ref, nw_ref, y_ref):
        ov = o_ref[...]
        rn = lax.rsqrt(_head_sum(ov * ov) * (1.0 / HEAD_DIM) + RMS_EPS)
        gate, _ = _silu_parts(z_ref[...])
        y_ref[...] = (ov * rn * nw_ref[...] * gate).astype(y_ref.dtype)

    row = pl.BlockSpec((None, tm, width), lambda i: (0, i, 0))
    return pl.pallas_call(
        body,
        name="gdn_post_fwd",
        grid=(lp // tm,),
        in_specs=[row, row, pl.BlockSpec((1, width), lambda i: (0, 0))],
        out_specs=row,
        out_shape=jax.ShapeDtypeStruct((1, lp, width), BF16),
        compiler_params=_params(("arbitrary",)),
    )(o, z, nw_b)


def gdn_post_bwd(o, z, dy, nw_b, *, tm):
    _, lp, width = o.shape

    def body(o_ref, z_ref, dy_ref, nw_ref, do_ref, dz_ref, dnw_ref):
        ov = o_ref[...]
        rn = lax.rsqrt(_head_sum(ov * ov) * (1.0 / HEAD_DIM) + RMS_EPS)
        yn = ov * rn
        gate, dgate = _silu_parts(z_ref[...])
        d_on = dy_ref[...] * gate
        dz_ref[...] = (dy_ref[...] * yn * nw_ref[...] * dgate).astype(dz_ref.dtype)
        a = d_on * nw_ref[...]
        do_ref[...] = rn * (a - yn * (_head_sum(a * yn) * (1.0 / HEAD_DIM)))

        @pl.when(pl.program_id(0) == 0)
        def _():
            dnw_ref[...] = jnp.zeros_like(dnw_ref)

        dnw_ref[...] += _row_partial(d_on * yn)

    row = pl.BlockSpec((None, tm, width), lambda i: (0, i, 0))
    return pl.pallas_call(
        body,
        name="gdn_post_bwd",
        grid=(lp // tm,),
        in_specs=[row, row, row, pl.BlockSpec((1, width), lambda i: (0, 0))],
        out_specs=[row, row, pl.BlockSpec((8, width), lambda i: (0, 0))],
        out_shape=[jax.ShapeDtypeStruct((1, lp, width), F32), jax.ShapeDtypeStruct((1, lp, width), BF16),
                   jax.ShapeDtypeStruct((8, width), F32)],
        compiler_params=_params(("arbitrary",)),
    )(o, z, dy, nw_b)


def ffn_act_fwd(up, conv_w, *, tm, name):
    _, lp, c_w = up.shape
    taps = conv_w.shape[1]

    def body(u_ref, halo_ref, g_ref, w_ref, o_ref):
        first_tile = pl.program_id(1) == 0

        def strip(cur, prev, rows, cs):
            conv = w_ref[taps - 1:taps, cs] * cur
            for j in range(taps - 1):
                conv += w_ref[j:j + 1, cs] * _shift_down(cur, prev, taps - 1 - j)
            y, _ = _silu_parts(conv)
            return y * g_ref[rows, cs]

        def pair(r0, above_of):
            top, bot = _pair_rows(r0)
            for c0 in range(0, c_w, LANES):
                cs = slice(c0, c0 + LANES)
                cur_t, cur_b = u_ref[top, cs], u_ref[bot, cs]
                out = [strip(cur_t, above_of(cs), top, cs), strip(cur_b, cur_t, bot, cs)]
                o_ref[pl.ds(r0, PAIR), cs] = jnp.concatenate(out, axis=0).astype(o_ref.dtype)

        pair(0, lambda cs: jnp.where(first_tile, 0.0, halo_ref[:, cs]))

        def loop_body(s, carry):
            r0 = pl.multiple_of(s * PAIR, PAIR)
            pair(r0, lambda cs: u_ref[pl.ds(pl.multiple_of(r0 - SUB, SUB), SUB), cs])
            return carry

        lax.fori_loop(1, tm // PAIR, loop_body, 0, unroll=STRIP_UNROLL)

    return pl.pallas_call(
        body,
        name=name,
        grid=(2, lp // tm),
        in_specs=[
            pl.BlockSpec((None, tm, c_w), lambda s, i: (s, i, 0)),
            pl.BlockSpec((None, HALO, c_w), lambda s, i: (s, _halo_index(i, tm), 0)),
            pl.BlockSpec((None, tm, c_w), lambda s, i: (2 + s, i, 0)),
            pl.BlockSpec((None, taps, c_w), lambda s, i: (s, 0, 0)),
        ],
        out_specs=pl.BlockSpec((None, tm, c_w), lambda s, i: (s, i, 0)),
        out_shape=jax.ShapeDtypeStruct((2, lp, c_w), BF16),
        compiler_params=_params(("arbitrary", "arbitrary")),
    )(up, up, up, conv_w)


def ffn_act_bwd(up, dact, conv_w, *, tm, name):
    _, lp, c_w = up.shape
    taps = conv_w.shape[1]
    last = lp // tm - 1
    n_pairs = tm // PAIR

    def body(u_ref, halo_ref, g_ref, d_ref, w_ref, dup_ref, dw_ref, below):
        step = pl.program_id(1)
        first_tile = step == last

        @pl.when(step == 0)
        def _():
            below[...] = jnp.zeros_like(below)
            dw_ref[...] = jnp.zeros_like(dw_ref)

        def strip(cur, prev, rows, cs, nxt):
            shifted = [_shift_down(cur, prev, taps - 1 - j) for j in range(taps)]
            conv = w_ref[0:1, cs] * shifted[0]
            for j in range(1, taps):
                conv += w_ref[j:j + 1, cs] * shifted[j]
            y, dsilu = _silu_parts(conv)
            d = d_ref[rows, cs]
            dc = d * g_ref[rows, cs] * dsilu
            dx = w_ref[taps - 1:taps, cs] * dc
            for j in range(taps - 1):
                dx += w_ref[j:j + 1, cs] * _shift_up(dc, nxt, taps - 1 - j)
            return dx, d * y, dc, [dc * s for s in shifted]

        def pair(r0, above_of):
            top, bot = _pair_rows(r0)
            both = pl.ds(r0, PAIR)
            for c0 in range(0, c_w, LANES):
                cs = slice(c0, c0 + LANES)
                cur_t, cur_b = u_ref[top, cs], u_ref[bot, cs]
                dx_b, dg_b, dc_b, dw_b = strip(cur_b, cur_t, bot, cs, below[:, cs])
                dx_t, dg_t, dc_t, dw_t = strip(cur_t, above_of(cs), top, cs, dc_b)
                below[:, cs] = dc_t
                dup_ref[0, both, cs] = jnp.concatenate([dx_t, dx_b], axis=0).astype(dup_ref.dtype)
                dup_ref[1, both, cs] = jnp.concatenate([dg_t, dg_b], axis=0).astype(dup_ref.dtype)
                for j in range(taps):
                    dw_ref[j, :, cs] += dw_t[j] + dw_b[j]

        def loop_body(it, carry):
            r0 = pl.multiple_of((n_pairs - 1 - it) * PAIR, PAIR)
            pair(r0, lambda cs: u_ref[pl.ds(pl.multiple_of(r0 - SUB, SUB), SUB), cs])
            return carry

        lax.fori_loop(0, n_pairs - 1, loop_body, 0, unroll=STRIP_UNROLL)
        pair(0, lambda cs: jnp.where(first_tile, 0.0, halo_ref[:, cs]))

    return pl.pallas_call(
        body,
        name=name,
        grid=(2, lp // tm),
        in_specs=[
            pl.BlockSpec((None, tm, c_w), lambda s, i: (s, last - i, 0)),
            pl.BlockSpec((None, HALO, c_w), lambda s, i: (s, _halo_index(last - i, tm), 0)),
            pl.BlockSpec((None, tm, c_w), lambda s, i: (2 + s, last - i, 0)),
            pl.BlockSpec((None, tm, c_w), lambda s, i: (s, last - i, 0)),
            pl.BlockSpec((None, taps, c_w), lambda s, i: (s, 0, 0)),
        ],
        out_specs=[
            pl.BlockSpec((2, None, tm, c_w), lambda s, i: (0, s, last - i, 0)),
            pl.BlockSpec((None, taps, SUB, c_w), lambda s, i: (s, 0, 0, 0)),
        ],
        out_shape=[jax.ShapeDtypeStruct((2, 2, lp, c_w), BF16), jax.ShapeDtypeStruct((2, taps, SUB, c_w), F32)],
        scratch_shapes=[pltpu.VMEM((SUB, c_w), F32)],
        compiler_params=_params(("arbitrary", "arbitrary")),
    )(up, up, up, dact, conv_w)


def sc_fwd(pb, conv_w, *, tm, cb):
    _, lp, width = pb.shape
    taps = conv_w.shape[0]

    def body(x_ref, halo_ref, w_ref, o_ref):
        first_tile = pl.program_id(1) == 0

        def strip(cur, prev, rows, cs):
            conv = w_ref[taps - 1:taps, cs] * cur
            for j in range(taps - 1):
                conv += w_ref[j:j + 1, cs] * _shift_down(cur, prev, taps - 1 - j)
            return x_ref[0, rows, cs] * conv

        def pair(r0, above_of):
            top, bot = _pair_rows(r0)
            for c0 in range(0, cb, LANES):
                cs = slice(c0, c0 + LANES)
                cur_t = x_ref[1, top, cs] * x_ref[2, top, cs]
                cur_b = x_ref[1, bot, cs] * x_ref[2, bot, cs]
                out = [strip(cur_t, above_of(cs), top, cs), strip(cur_b, cur_t, bot, cs)]
                o_ref[pl.ds(r0, PAIR), cs] = jnp.concatenate(out, axis=0).astype(o_ref.dtype)

        pair(0, lambda cs: jnp.where(first_tile, 0.0, halo_ref[1, :, cs] * halo_ref[2, :, cs]))

        def loop_body(k, carry):
            r0 = pl.multiple_of(k * PAIR, PAIR)
            before = pl.ds(pl.multiple_of(r0 - SUB, SUB), SUB)
            pair(r0, lambda cs: x_ref[1, before, cs] * x_ref[2, before, cs])
            return carry

        lax.fori_loop(1, tm // PAIR, loop_body, 0, unroll=STRIP_UNROLL)

    return pl.pallas_call(
        body,
        name="sc_fwd",
        grid=(width // cb, lp // tm),
        in_specs=[
            pl.BlockSpec((3, tm, cb), lambda j, i: (0, i, j)),
            pl.BlockSpec((3, HALO, cb), lambda j, i: (0, _halo_index(i, tm), j)),
            pl.BlockSpec((taps, cb), lambda j, i: (0, j)),
        ],
        out_specs=pl.BlockSpec((None, tm, cb), lambda j, i: (0, i, j)),
        out_shape=jax.ShapeDtypeStruct((1, lp, width), BF16),
        compiler_params=_params(("arbitrary", "arbitrary")),
    )(pb, pb, conv_w)


def sc_bwd(pb, ds, conv_w, *, tm, cb):
    _, lp, width = pb.shape
    taps = conv_w.shape[0]
    last = lp // tm - 1
    n_pairs = tm // PAIR

    def body(x_ref, halo_ref, d_ref, w_ref, dx_ref, dw_ref, below):
        step = pl.program_id(1)
        first_tile = step == last

        @pl.when(step == 0)
        def _():
            below[...] = jnp.zeros_like(below)
            dw_ref[...] = jnp.zeros_like(dw_ref)

        def strip(cur, prev, rows, cs, nxt):
            gate, left, right = x_ref[0, rows, cs], x_ref[1, rows, cs], x_ref[2, rows, cs]
            shifted = [_shift_down(cur, prev, taps - 1 - j) for j in range(taps)]
            conv = w_ref[0:1, cs] * shifted[0]
            for j in range(1, taps):
                conv += w_ref[j:j + 1, cs] * shifted[j]
            d = d_ref[rows, cs]
            dc = d * gate
            dp = w_ref[taps - 1:taps, cs] * dc
            for j in range(taps - 1):
                dp += w_ref[j:j + 1, cs] * _shift_up(dc, nxt, taps - 1 - j)
            return d * conv, dp * right, dp * left, dc, [dc * s for s in shifted]

        def pair(r0, above_of):
            top, bot = _pair_rows(r0)
            both = pl.ds(r0, PAIR)
            for c0 in range(0, cb, LANES):
                cs = slice(c0, c0 + LANES)
                cur_t = x_ref[1, top, cs] * x_ref[2, top, cs]
                cur_b = x_ref[1, bot, cs] * x_ref[2, bot, cs]
                *dx_b, dc_b, dw_b = strip(cur_b, cur_t, bot, cs, below[:, cs])
                *dx_t, dc_t, dw_t = strip(cur_t, above_of(cs), top, cs, dc_b)
                below[:, cs] = dc_t
                for s in range(3):
                    dx_ref[s, both, cs] = jnp.concatenate([dx_t[s], dx_b[s]], axis=0).astype(dx_ref.dtype)
                for j in range(taps):
                    dw_ref[j, :, cs] += dw_t[j] + dw_b[j]

        def loop_body(it, carry):
            r0 = pl.multiple_of((n_pairs - 1 - it) * PAIR, PAIR)
            before = pl.ds(pl.multiple_of(r0 - SUB, SUB), SUB)
            pair(r0, lambda cs: x_ref[1, before, cs] * x_ref[2, before, cs])
            return carry

        lax.fori_loop(0, n_pairs - 1, loop_body, 0, unroll=STRIP_UNROLL)
        pair(0, lambda cs: jnp.where(first_tile, 0.0, halo_ref[1, :, cs] * halo_ref[2, :, cs]))

    tile_spec = pl.BlockSpec((3, tm, cb), lambda j, i: (0, last - i, j))
    return pl.pallas_call(
        body,
        name="sc_bwd",
        grid=(width // cb, lp // tm),
        in_specs=[
            tile_spec,
            pl.BlockSpec((3, HALO, cb), lambda j, i: (0, _halo_index(last - i, tm), j)),
            pl.BlockSpec((None, tm, cb), lambda j, i: (0, last - i, j)),
            pl.BlockSpec((taps, cb), lambda j, i: (0, j)),
        ],
        out_specs=[tile_spec, pl.BlockSpec((taps, SUB, cb), lambda j, i: (0, 0, j))],
        out_shape=[jax.ShapeDtypeStruct((3, lp, width), BF16), jax.ShapeDtypeStruct((taps, SUB, width), F32)],
        scratch_shapes=[pltpu.VMEM((SUB, cb), F32)],
        compiler_params=_params(("arbitrary", "arbitrary")),
    )(pb, pb, ds, conv_w)


TILE_BYTES = 1536 * 1024


def _rows_tile(rows, cols, multiple=8):
    if rows * cols * 4 <= TILE_BYTES or rows % multiple:
        return rows
    best = multiple
    for t in range(multiple, rows + 1, multiple):
        if rows % t == 0 and t * cols * 4 <= TILE_BYTES:
            best = t
    return best


def pair_sum(g, landed, core, out_dtype, name):
    _, rows, cols = g.shape
    half = rows // 2
    tr = _rows_tile(half, cols, 16)
    nb = half // tr

    def body(c_ref, g_ref, l_ref, o_ref):
        o_ref[...] = (g_ref[...] + l_ref[...]).astype(out_dtype)

    return pl.pallas_call(
        body,
        name=name,
        grid_spec=pltpu.PrefetchScalarGridSpec(
            num_scalar_prefetch=1,
            grid=(4, nb),
            in_specs=[
                pl.BlockSpec((None, tr, cols), lambda s, i, c: (s, c[0] * nb + i, 0)),
                pl.BlockSpec((None, tr, cols), lambda s, i, c: (s, i, 0)),
            ],
            out_specs=pl.BlockSpec((None, tr, cols), lambda s, i, c: (s, i, 0)),
        ),
        out_shape=jax.ShapeDtypeStruct((4, half, cols), out_dtype),
        compiler_params=_params(("arbitrary", "arbitrary")),
    )(core, g, landed)


def chip_sum(x, name):
    _, rows, cols = x.shape
    tr = _rows_tile(rows, cols, 16)

    def body(x0, x1, x2, x3, o_ref):
        acc = x0[...].astype(F32) + x1[...].astype(F32)
        o_ref[...] = (acc + x2[...].astype(F32)) + x3[...].astype(F32)

    return pl.pallas_call(
        body,
        name=name,
        grid=(rows // tr,),
        in_specs=[pl.BlockSpec((None, tr, cols), lambda i, k=k: (k, i, 0)) for k in range(4)],
        out_specs=pl.BlockSpec((tr, cols), lambda i: (i, 0)),
        out_shape=jax.ShapeDtypeStruct((rows, cols), F32),
        compiler_params=_params(("arbitrary",)),
    )(x, x, x, x)


def adamw(w, g, m, v, name):
    shape = w.shape
    cols = shape[-1]
    rows = w.size // cols
    tr = _rows_tile(rows, cols)

    def body(w_ref, g_ref, m_ref, v_ref, d_ref, m2_ref, v2_ref):
        gv = g_ref[...]
        m2 = ADAM_B1 * m_ref[...] + (1.0 - ADAM_B1) * gv
        v2 = ADAM_B2 * v_ref[...] + (1.0 - ADAM_B2) * (gv * gv)
        m_hat = m2 / (1.0 - ADAM_B1 ** ADAM_STEP)
        v_hat = v2 / (1.0 - ADAM_B2 ** ADAM_STEP)
        d_ref[...] = -ADAM_LR * (m_hat / (jnp.sqrt(v_hat) + ADAM_EPS) + ADAM_WD * w_ref[...])
        m2_ref[...] = m2
        v2_ref[...] = v2

    spec = pl.BlockSpec((tr, cols), lambda i: (i, 0))
    outs = pl.pallas_call(
        body,
        name=name,
        grid=(rows // tr,),
        in_specs=[spec] * 4,
        out_specs=[spec] * 3,
        out_shape=[jax.ShapeDtypeStruct((rows, cols), F32)] * 3,
        compiler_params=_params(("arbitrary",)),
    )(*[t.reshape(rows, cols) for t in (w, g, m, v)])
    return tuple(o.reshape(shape) for o in outs)


MESH_ID = pl.DeviceIdType.MESH
ANY = pl.BlockSpec(memory_space=pl.ANY)


def _place():
    x, y, c = lax.axis_index("x"), lax.axis_index("y"), lax.axis_index("c")
    other_chips = [(1 - x, y), (x, 1 - y), (1 - x, 1 - y)]
    return x, y, c, other_chips


def all_gather_shards(bufs, name):
    n = len(bufs)

    def body(*refs):
        x_refs, o_refs = refs[:n], refs[n:2 * n]
        copies = _gather_copies(x_refs, o_refs, *refs[2 * n:])
        _gather_start(copies)
        _gather_finish(copies)

    outs = pl.pallas_call(
        body,
        name=name,
        in_specs=[ANY] * n,
        out_specs=[ANY] * n,
        out_shape=_gather_out_shapes(bufs),
        scratch_shapes=_gather_sems(n),
    )(*bufs)
    return _set_own_slots(outs, bufs)


def _gather_out_shapes(bufs):
    return [jax.ShapeDtypeStruct((4,) + b.shape, b.dtype) for b in bufs]


def _gather_sems(n):
    return [pltpu.SemaphoreType.DMA((6 * n,)), pltpu.SemaphoreType.DMA((6 * n,))]


def _set_own_slots(outs, bufs):
    if not outs:
        return []
    me = 2 * lax.axis_index("x") + lax.axis_index("y")
    return [lax.dynamic_update_index_in_dim(o, b, me, 0) for o, b in zip(outs, bufs)]


def _gather_copies(x_refs, o_refs, send_sems, recv_sems):
    x, y, c, chips = _place()
    me = 2 * x + y
    sibling = (x, y, 1 - c)

    def part(a, slot, hf):
        half = x_refs[a].shape[0] // 2
        return o_refs[a].at[slot, pl.ds(hf * half, half), :]

    def mine(a):
        half = x_refs[a].shape[0] // 2
        return x_refs[a].at[pl.ds(c * half, half), :]

    def copy(k, src, dst, to):
        return pltpu.make_async_remote_copy(src_ref=src, dst_ref=dst, send_sem=send_sems.at[k],
                                            recv_sem=recv_sems.at[k], device_id=to, device_id_type=MESH_ID)

    sends, arrivals, passes, passed = [], [], [], []
    for a in range(len(x_refs)):
        for j, (px, py) in enumerate(chips):
            landed, theirs = part(a, 2 * px + py, c), part(a, 2 * px + py, 1 - c)
            sends.append(copy(6 * a + j, mine(a), part(a, me, c), (px, py, c)))
            arrivals.append(copy(6 * a + j, mine(a), landed, (px, py, c)))
            passes.append(copy(6 * a + 3 + j, landed, landed, sibling))
            passed.append(copy(6 * a + 3 + j, theirs, theirs, sibling))
    return sends, arrivals, passes, passed


def _gather_start(copies):
    for cp in copies[0]:
        cp.start()


def _gather_finish(copies):
    sends, arrivals, passes, passed = copies
    for arrival, cp in zip(arrivals, passes):
        arrival.wait_recv()
        cp.start()
    for cp in passed:
        cp.wait_recv()
    for cp in sends + passes:
        cp.wait_send()


def swap_halves(bufs, name):
    n = len(bufs)

    def body(*refs):
        copies = _swap_copies(refs[:n], refs[n:2 * n], *refs[2 * n:])
        _swap_start(copies)
        _swap_finish(copies)

    return pl.pallas_call(
        body,
        name=name,
        in_specs=[ANY] * n,
        out_specs=[ANY] * n,
        out_shape=_swap_out_shapes(bufs),
        scratch_shapes=_swap_sems(n),
    )(*bufs)


def _swap_out_shapes(bufs):
    return [jax.ShapeDtypeStruct((4, b.shape[1] // 2, b.shape[2]), b.dtype) for b in bufs]


def _swap_sems(n):
    return [pltpu.SemaphoreType.DMA((n,)), pltpu.SemaphoreType.DMA((n,))]


def _swap_copies(x_refs, o_refs, send_sems, recv_sems):
    x, y, c, _ = _place()
    copies = []
    for a, (x_ref, o_ref) in enumerate(zip(x_refs, o_refs)):
        half = x_ref.shape[1] // 2
        copies.append(pltpu.make_async_remote_copy(src_ref=x_ref.at[:, pl.ds((1 - c) * half, half), :], dst_ref=o_ref,
                                                   send_sem=send_sems.at[a], recv_sem=recv_sems.at[a],
                                                   device_id=(x, y, 1 - c), device_id_type=MESH_ID))
    return copies


def _swap_start(copies):
    for cp in copies:
        cp.start()


def _swap_finish(copies):
    for cp in copies:
        cp.wait()


def scatter_to_chips(bufs, name):
    n = len(bufs)

    def body(*refs):
        x_refs, o_refs = refs[:n], refs[n:2 * n]
        copies = _scatter_copies(x_refs, o_refs, *refs[2 * n:])
        _scatter_start(copies)
        _scatter_finish(copies)

    outs = pl.pallas_call(
        body,
        name=name,
        in_specs=[ANY] * n,
        out_specs=[ANY] * n,
        out_shape=[jax.ShapeDtypeStruct(b.shape, b.dtype) for b in bufs],
        scratch_shapes=_scatter_sems(n),
    )(*bufs)
    return _keep_own_slots(outs, bufs)


def _scatter_sems(n):
    return [pltpu.SemaphoreType.DMA((3 * n,)), pltpu.SemaphoreType.DMA((3 * n,))]


def _keep_own_slots(outs, bufs):
    if not outs:
        return []
    me = 2 * lax.axis_index("x") + lax.axis_index("y")
    return [lax.dynamic_update_index_in_dim(o, lax.dynamic_index_in_dim(b, me, 0, keepdims=False), me, 0)
            for o, b in zip(outs, bufs)]


def _scatter_copies(x_refs, o_refs, send_sems, recv_sems):
    x, y, c, chips = _place()
    me = 2 * x + y

    def copy(a, j, src_slot, dst_slot, px, py):
        return pltpu.make_async_remote_copy(src_ref=x_refs[a].at[src_slot], dst_ref=o_refs[a].at[dst_slot],
                                            send_sem=send_sems.at[3 * a + j], recv_sem=recv_sems.at[3 * a + j],
                                            device_id=(px, py, c), device_id_type=MESH_ID)

    sends = [copy(a, j, 2 * px + py, me, px, py) for a in range(len(x_refs)) for j, (px, py) in enumerate(chips)]
    arrivals = [copy(a, j, me, 2 * px + py, px, py) for a in range(len(x_refs)) for j, (px, py) in enumerate(chips)]
    return sends, arrivals


def _scatter_start(copies):
    for cp in copies[0]:
        cp.start()


def _scatter_finish(copies):
    for cp in copies[1]:
        cp.wait_recv()
    for cp in copies[0]:
        cp.wait_send()


def share_halves(groups, name):
    bufs = [b for grp in groups for b in grp]
    where = [(gi, li) for gi, grp in enumerate(groups) for li in range(len(grp))]
    n = len(bufs)

    def body(*refs):
        x_refs, o_refs = refs[:n], refs[n:n + len(groups)]
        send_sems, recv_sems = refs[n + len(groups):]
        x, y, c, _ = _place()
        sent, arrive = [], []
        for a, (gi, li) in enumerate(where):

            def copy(hf, a=a, gi=gi, li=li):
                return pltpu.make_async_remote_copy(src_ref=x_refs[a], dst_ref=o_refs[gi].at[li, hf],
                                                    send_sem=send_sems.at[a], recv_sem=recv_sems.at[a],
                                                    device_id=(x, y, 1 - c), device_id_type=MESH_ID)

            sent.append(copy(c))
            arrive.append(copy(1 - c))
        for cp in sent:
            cp.start()
        for cp in arrive:
            cp.wait_recv()
        for cp in sent:
            cp.wait_send()

    outs = pl.pallas_call(
        body,
        name=name,
        in_specs=[ANY] * n,
        out_specs=[ANY] * len(groups),
        out_shape=[jax.ShapeDtypeStruct((len(grp), 2) + grp[0].shape, grp[0].dtype) for grp in groups],
        scratch_shapes=[pltpu.SemaphoreType.DMA((n,)), pltpu.SemaphoreType.DMA((n,))],
    )(*bufs)
    c = lax.axis_index("c")
    full = [lax.dynamic_update_index_in_dim(o, jnp.stack(grp), c, 1) for o, grp in zip(outs, groups)]
    return [t.reshape(t.shape[0], 2 * t.shape[2], t.shape[3]) for t in full]


def pair_sums(bufs, landed, dtypes, tag):
    core = lax.axis_index("c").astype(jnp.int32).reshape(1)
    return [pair_sum(b, l, core, dt, "rs_pair_sum_%s%d" % (tag, i)) for i, (b, l, dt) in enumerate(zip(bufs, landed, dtypes))]


def _row_tiles(length):
    return (640, 320) if length > 2048 else (128, 64)


def _divisor_tile(rows, target):
    return max(t for t in range(8, min(rows, target) + 1, 8) if rows % t == 0)


def _local_step(x, target, wt, late_shards, layout_late, complete_grads, sum_pairs):
    seq, d = x.shape
    length = N_META + seq
    tm, tm_ffn = _row_tiles(length)
    lp = -(-length // tm) * tm
    tail = jnp.zeros((lp - length, d), F32)
    h0 = jnp.concatenate([wt["meta"], x, tail], axis=0)[None]
    tgt = jnp.concatenate([jnp.zeros((N_META, d), F32), target, tail], axis=0)
    nn = functools.partial(mm_nn, tm=_divisor_tile(lp, 1664))
    nt = functools.partial(mm_nt, tm=_divisor_tile(lp, 1040))
    tn = functools.partial(mm_tn, tm=_divisor_tile(lp, 1664), rb=256)
    nn_ln = functools.partial(mm_nn_ln, tm=_divisor_tile(lp, 832))
    nt_ln_bwd = functools.partial(mm_nt_ln_bwd, tm=_divisor_tile(lp, 1040))
    ln_g = [wt["ln_mix_g"][0:1], wt["ln_ffn_g"][0:1], wt["ln_mix_g"][1:2], wt["ln_ffn_g"][1:2]]
    ln_b = [wt["ln_mix_b"][0:1], wt["ln_ffn_b"][0:1], wt["ln_mix_b"][1:2], wt["ln_ffn_b"][1:2]]

    h0b = h0.astype(BF16)
    p3 = nn(h0b, wt["a3"], name="a_in3")
    pz = nn(h0b, wt["az"], name="a_inz")
    pba = nn(h0b, wt["a_ba"], name="a_inba")
    qkv = gdn_pre_fwd(p3, wt["a_conv3"], tm=tm, cb=2 * HEAD_DIM)
    gates = gdn_gates_fwd(pba, wt["alog_lanes"], wt["dtb_lanes"], tm=tm)
    o, states, tinv, late_stacks = gdn_chunk_fwd(qkv, gates, late_shards)
    wt = {**wt, **layout_late(late_stacks)}
    onz = gdn_post_fwd(o[None], pz, wt["anorm_b"], tm=tm)
    r1, h1, h1b = nn_ln(onz, wt["a_out"], h0, ln_g[0], ln_b[0], name="a_out_ln1")
    up0 = nn(h1b, wt["up"][0], name="up0")
    act0 = ffn_act_fwd(up0, wt["fconv"][0], tm=tm_ffn, name="ffn_act0")
    r2, h2, h2b = nn_ln(act0, wt["down"][0], h1, ln_g[1], ln_b[1], name="down0_ln2")
    pb = nn(h2b, wt["b_in"], name="b_in")
    sc = sc_fwd(pb, wt["b_conv"], tm=tm_ffn, cb=d)
    r3, h3, h3b = nn_ln(sc, wt["b_out"], h2, ln_g[2], ln_b[2], name="b_out_ln3")
    up1 = nn(h3b, wt["up"][1], name="up1")
    act1 = ffn_act_fwd(up1, wt["fconv"][1], tm=tm_ffn, name="ffn_act1")
    r4, h4, _ = nn_ln(act1, wt["down"][1], h3, ln_g[3], ln_b[3], name="down1_ln4")

    grads = {}
    dr4, dgb4, loss_part = loss_ln_bwd(h4, tgt, r4, ln_g[3], first=N_META, count=seq, tm=tm)
    d_down1 = tn(act1, dr4, name="d_down1")
    dact1 = nt(dr4, wt["down"][1], name="d_act1")
    dup1, dfconv1 = ffn_act_bwd(up1, dact1, wt["fconv"][1], tm=tm_ffn, name="ffn_act1_bwd")
    dup1 = dup1.reshape(up1.shape)
    d_up1 = tn(h3b, dup1, name="d_up1")

    dr3, dgb3, _ = nt_ln_bwd(dup1, wt["up"][1], dr4, r3, ln_g[2], name="d_h3_ln3")
    d_bout = tn(sc, dr3, name="d_b_out")
    dsc = nt(dr3, wt["b_out"], name="d_sc")
    dpb, dbconv = sc_bwd(pb, dsc, wt["b_conv"], tm=tm_ffn, cb=d)
    d_bin = tn(h2b, dpb, name="d_b_in")

    dr2, dgb2, _ = nt_ln_bwd(dpb, wt["b_in"], dr3, r2, ln_g[1], name="d_h2_ln2")
    d_down0 = tn(act0, dr2, name="d_down0")
    dact0 = nt(dr2, wt["down"][0], name="d_act0")
    dup0, dfconv0 = ffn_act_bwd(up0, dact0, wt["fconv"][0], tm=tm_ffn, name="ffn_act0_bwd")
    dup0 = dup0.reshape(up0.shape)
    d_up0 = tn(h1b, dup0, name="d_up0")
    grads["b_w_in"] = [d_bin[0].transpose(1, 0, 2).reshape(d, 4, 3 * d // 4).transpose(1, 0, 2)]
    grads["b_w_out"] = [d_bout.reshape(4, d // 4, d)]
    grads["ffn_w_up"] = [d_up0[0], d_up1[0]]
    grads["ffn_w_down"] = [t.reshape(4, -1, d) for t in (d_down0, d_down1)]
    complete = complete_grads(grads)

    dr1, dgb1, from_sibling = nt_ln_bwd(dup0, wt["up"][0], dr2, r1, ln_g[0], name="d_h1_ln1", swap=complete)
    leaving = sum_pairs(complete, from_sibling)
    d_aout = tn(onz, dr1, name="d_a_out")
    donz = nt(dr1, wt["a_out"], name="d_onz")
    d_o, dz, dnw = gdn_post_bwd(o[None], pz, donz, wt["anorm_b"], tm=tm)
    dqkv, dgates, landed = gdn_chunk_bwd(qkv, gates, states, tinv, d_o[0], leaving)
    dp3, daconv = gdn_pre_bwd(p3, dqkv, wt["a_conv3"], tm=tm, cb=2 * HEAD_DIM)
    dpba, dscal = gdn_gates_bwd(pba, dgates, wt["alog_lanes"], wt["dtb_lanes"], tm=tm)
    d_a3 = tn(h0b, dp3, name="d_a_in3")
    d_az = tn(h0b, dz, name="d_a_inz")
    d_aba = tn(h0b, dpba, name="d_a_inba")
    dh0 = nt(dp3, wt["a3"], res=dr1, res_scale=ALPHA, name="d_h0a")
    dh0 = nt(dz, wt["az"], res=dh0, res_scale=1.0, name="d_h0z")
    dh0 = nt(dpba, wt["a_ba"], res=dh0, res_scale=1.0, name="d_h0")

    width = HEADS * HEAD_DIM
    d_a_in = jnp.concatenate([d_a3[0, 0], d_a3[0, 1], d_a3[0, 2], d_az[0, 0], d_aba[0, 0][:, :2 * HEADS]], axis=1)
    n_in = d_a_in.shape[1] // 4
    grads["a_w_in"] = [d_a_in.reshape(d, 4, n_in).transpose(1, 0, 2)]
    grads["a_w_out"] = [d_aout.reshape(4, width // 4, d)]
    grads["a_conv"] = daconv.sum(axis=2).transpose(1, 0, 2).reshape(1, GDN_CONV, 3 * width)
    per_head = dscal.sum(axis=1)[:, HEADS:2 * HEADS]
    grads["a_log"] = per_head[0][None]
    grads["a_dt_bias"] = per_head[1][None]
    grads["a_norm"] = dnw.reshape(8, HEADS, HEAD_DIM).sum(axis=(0, 1))[None]
    grads["b_conv"] = dbconv.sum(axis=1)[None]
    lns = [dgb1, dgb2, dgb3, dgb4]
    grads["ln_mix_g"] = jnp.stack([lns[0][0].sum(0), lns[2][0].sum(0)])
    grads["ln_mix_b"] = jnp.stack([lns[0][1].sum(0), lns[2][1].sum(0)])
    grads["ln_ffn_g"] = jnp.stack([lns[1][0].sum(0), lns[3][0].sum(0)])
    grads["ln_ffn_b"] = jnp.stack([lns[1][1].sum(0), lns[3][1].sum(0)])
    grads["ffn_conv"] = jnp.stack([t.sum(axis=2).transpose(1, 0, 2).reshape(FFN_CONV, -1) for t in (dfconv0, dfconv1)])
    grads["meta"] = dh0[0, :N_META]
    return loss_part, dh0, grads, landed


WEIGHTS = ["meta", "a_w_in", "a_conv", "a_log", "a_dt_bias", "a_norm", "a_w_out", "b_w_in", "b_conv", "b_w_out",
           "ln_mix_g", "ln_mix_b", "ffn_w_up", "ffn_conv", "ffn_w_down", "ln_ffn_g", "ln_ffn_b"]
EARLY_WEIGHTS = ["a_w_in", "a_w_out"]
LATE_WEIGHTS = ["b_w_in", "b_w_out", "ffn_w_up", "ffn_w_down"]
MATMUL_WEIGHTS = EARLY_WEIGHTS + LATE_WEIGHTS
SMALL_SHARDED = ["a_conv", "b_conv", "ffn_conv", "meta"]
REPLICATED = ["a_log", "a_dt_bias", "a_norm", "ln_mix_g", "ln_mix_b", "ln_ffn_g", "ln_ffn_b"]
SHARD_AXIS = {"meta": 1, "a_w_in": 2, "a_conv": 2, "a_w_out": 1, "b_w_in": 2, "b_conv": 2, "b_w_out": 1,
              "ffn_w_up": 2, "ffn_conv": 2, "ffn_w_down": 1}
PACK_COLS = 1024
PACK_ROWS_MULTIPLE = 32


def _pack(pieces, lead=()):
    flat = jnp.concatenate([p.reshape(lead + (-1,)) for p in pieces], axis=-1)
    n = flat.shape[-1]
    rows = -(-n // (PACK_COLS * PACK_ROWS_MULTIPLE)) * PACK_ROWS_MULTIPLE
    flat = jnp.pad(flat, [(0, 0)] * len(lead) + [(0, rows * PACK_COLS - n)])
    return flat.reshape(lead + (rows, PACK_COLS))


def _unpack(buf, shapes, lead=()):
    flat = buf.reshape(lead + (-1,))
    out, off = [], 0
    for shp in shapes:
        n = 1
        for s in shp:
            n *= s
        out.append(flat[..., off:off + n].reshape(lead + tuple(shp)))
        off += n
    return out


def _join_shards(stacked, axis):
    return jnp.concatenate([stacked[k] for k in range(4)], axis=axis)


def _split_shards(full, axis):
    return jnp.stack(jnp.split(full, 4, axis=axis))


def _weight_layers(w, names):
    return [w[n][l].astype(BF16) for n in names for l in range(w[n].shape[0])]


def _per_weight(arrays, w, names):
    it = iter(arrays)
    return {n: [next(it) for _ in range(w[n].shape[0])] for n in names}


def _layout_early(full, w):
    width = HEADS * HEAD_DIM
    wt = {n: w[n] for n in ("ln_mix_g", "ln_mix_b", "ln_ffn_g", "ln_ffn_b")}
    w_in = _join_shards(full["a_w_in"][0], 1)
    d = w_in.shape[0]
    n_ff = full["ffn_conv"].shape[2] // 2
    blocks = [w_in[:, s * width:(s + 1) * width] for s in range(4)]
    wt["a3"] = jnp.stack(blocks[:3])[None]
    wt["az"] = blocks[3][None, None]
    wt["a_ba"] = jnp.pad(w_in[:, 4 * width:], ((0, 0), (0, HEAD_DIM - 2 * HEADS)))[None, None]
    wt["a_out"] = full["a_w_out"][0].reshape(1, 1, width, d)
    wt["a_conv3"] = full["a_conv"][0].reshape(GDN_CONV, 3, width).transpose(1, 0, 2)
    wt["b_conv"] = full["b_conv"][0]
    wt["fconv"] = [full["ffn_conv"][l].reshape(FFN_CONV, 2, n_ff).transpose(1, 0, 2) for l in range(2)]
    wt["meta"] = full["meta"]
    in_g_lanes = (HEADS, HEAD_DIM - 2 * HEADS)
    wt["alog_lanes"] = jnp.pad(w["a_log"][0], in_g_lanes)[None]
    wt["dtb_lanes"] = jnp.pad(w["a_dt_bias"][0], in_g_lanes)[None]
    wt["anorm_b"] = jnp.tile(w["a_norm"][0], HEADS)[None]
    return wt


def _layout_late(full):
    d = full["b_w_in"][0].shape[1]
    n_ff = full["ffn_w_up"][0].shape[2]
    return {
        "b_in": _join_shards(full["b_w_in"][0], 1).reshape(d, 3, d).transpose(1, 0, 2)[None],
        "b_out": full["b_w_out"][0].reshape(1, 1, d, d),
        "up": [t[None] for t in full["ffn_w_up"]],
        "down": [t.reshape(2, 1, n_ff, d) for t in full["ffn_w_down"]],
    }


def kernel(x, meta, a_w_in, a_conv, a_log, a_dt_bias, a_norm, a_w_out, b_w_in, b_conv, b_w_out, ln_mix_g, ln_mix_b, ffn_w_up, ffn_conv, ffn_w_down, ln_ffn_g, ln_ffn_b, loss_target, m_meta, m_a_w_in, m_a_conv, m_a_log, m_a_dt_bias, m_a_norm, m_a_w_out, m_b_w_in, m_b_conv, m_b_w_out, m_ln_mix_g, m_ln_mix_b, m_ffn_w_up, m_ffn_conv, m_ffn_w_down, m_ln_ffn_g, m_ln_ffn_b, v_meta, v_a_w_in, v_a_conv, v_a_log, v_a_dt_bias, v_a_norm, v_a_w_out, v_b_w_in, v_b_conv, v_b_w_out, v_ln_mix_g, v_ln_mix_b, v_ffn_w_up, v_ffn_conv, v_ffn_w_down, v_ln_ffn_g, v_ln_ffn_b):
    w = dict(meta=meta, a_w_in=a_w_in, a_conv=a_conv, a_log=a_log, a_dt_bias=a_dt_bias, a_norm=a_norm, a_w_out=a_w_out,
             b_w_in=b_w_in, b_conv=b_conv, b_w_out=b_w_out, ln_mix_g=ln_mix_g, ln_mix_b=ln_mix_b, ffn_w_up=ffn_w_up,
             ffn_conv=ffn_conv, ffn_w_down=ffn_w_down, ln_ffn_g=ln_ffn_g, ln_ffn_b=ln_ffn_b)
    m = dict(meta=m_meta, a_w_in=m_a_w_in, a_conv=m_a_conv, a_log=m_a_log, a_dt_bias=m_a_dt_bias, a_norm=m_a_norm,
             a_w_out=m_a_w_out, b_w_in=m_b_w_in, b_conv=m_b_conv, b_w_out=m_b_w_out, ln_mix_g=m_ln_mix_g,
             ln_mix_b=m_ln_mix_b, ffn_w_up=m_ffn_w_up, ffn_conv=m_ffn_conv, ffn_w_down=m_ffn_w_down,
             ln_ffn_g=m_ln_ffn_g, ln_ffn_b=m_ln_ffn_b)
    v = dict(meta=v_meta, a_w_in=v_a_w_in, a_conv=v_a_conv, a_log=v_a_log, a_dt_bias=v_a_dt_bias, a_norm=v_a_norm,
             a_w_out=v_a_w_out, b_w_in=v_b_w_in, b_conv=v_b_conv, b_w_out=v_b_w_out, ln_mix_g=v_ln_mix_g,
             ln_mix_b=v_ln_mix_b, ffn_w_up=v_ffn_w_up, ffn_conv=v_ffn_conv, ffn_w_down=v_ffn_w_down,
             ln_ffn_g=v_ln_ffn_g, ln_ffn_b=v_ln_ffn_b)
    seq = x.shape[1]
    *stacks, small = all_gather_shards(_weight_layers(w, EARLY_WEIGHTS) + [_pack([w[n] for n in SMALL_SHARDED])],
                                       "gather_early")
    full = _per_weight(stacks, w, EARLY_WEIGHTS)
    for n, t in zip(SMALL_SHARDED, _unpack(small, [w[n].shape for n in SMALL_SHARDED], lead=(4,))):
        full[n] = _join_shards(t, SHARD_AXIS[n])

    def layout_late(late_stacks):
        return _layout_late(_per_weight(late_stacks, w, LATE_WEIGHTS))

    def complete_grads(grads):
        return [g for n in LATE_WEIGHTS for g in grads[n]]

    def sum_pairs(bufs, from_sibling):
        return pair_sums(bufs, from_sibling, [BF16] * len(bufs), "late")

    loss_part, dh0, grads, landed_late = _local_step(x[0], loss_target[0], _layout_early(full, w),
                                                     _weight_layers(w, LATE_WEIGHTS), layout_late, complete_grads, sum_pairs)
    pieces = [_split_shards(grads[n], SHARD_AXIS[n]) for n in SMALL_SHARDED]
    same = jnp.concatenate([grads[n].reshape(-1) for n in REPLICATED] + [jnp.sum(loss_part).reshape(1)])
    pieces.append(jnp.broadcast_to(same, (4,) + same.shape))
    bufs = [g for n in EARLY_WEIGHTS for g in grads[n]] + [_pack(pieces, lead=(4,))]
    from_sibling = swap_halves(bufs, "rs_pair_early")
    landed = scatter_to_chips(pair_sums(bufs, from_sibling, [BF16] * (len(bufs) - 1) + [F32], "early"), "rs_chips_early")
    totals = [chip_sum(t, "rs_chip_sum%d" % i) for i, t in enumerate(landed + landed_late)]
    by_weight = _per_weight(totals[:len(bufs) - 1] + totals[len(bufs):], w, MATMUL_WEIGHTS)
    *shared, small_total = share_halves([by_weight[n] for n in MATMUL_WEIGHTS] + [[totals[len(bufs) - 1]]], "rs_share")
    grad_w = {n: t.reshape(w[n].shape) for n, t in zip(MATMUL_WEIGHTS, shared)}
    rest = SMALL_SHARDED + REPLICATED
    unpacked = _unpack(small_total[0], [w[n].shape for n in rest] + [()])
    grad_w.update(zip(rest, unpacked[:-1]))
    loss = unpacked[-1]
    grad_x = dh0[:, N_META:N_META + seq]
    steps = [adamw(w[n], grad_w[n], m[n], v[n], "adamw_" + n) for n in WEIGHTS]
    return (loss, grad_x, *[grad_w[n] for n in WEIGHTS], *[s[0] for s in steps], *[s[1] for s in steps],
            *[s[2] for s in steps])
```

```python
import functools

import jax
import jax.numpy as jnp
from jax import lax
from jax.experimental import pallas as pl
from jax.experimental.pallas import tpu as pltpu

F32 = jnp.float32
BF16 = jnp.bfloat16

N_META = 16
HEADS = 8
HEAD_DIM = 128
CHUNK = 64
GDN_CONV = 4
FFN_CONV = 3
ALPHA = 4.0 ** 0.25
LN_EPS = 1e-5
RMS_EPS = 1e-6
L2_EPS = 1e-6
Q_SCALE = HEAD_DIM ** -0.5

ADAM_LR = 0.001
ADAM_B1 = 0.9
ADAM_B2 = 0.999
ADAM_EPS = 1e-08
ADAM_WD = 0.01
ADAM_STEP = 10

HALO = 8
VMEM_LIMIT = 48 * 1024 * 1024


def _params(sem=None):
    return pltpu.CompilerParams(dimension_semantics=sem, vmem_limit_bytes=VMEM_LIMIT)


def _dot(a, b, prec=None):
    return jnp.dot(a, b, preferred_element_type=F32, precision=prec)


def _dot_nt(a, b, prec=None):
    return lax.dot_general(a, b, (((1,), (1,)), ((), ())), preferred_element_type=F32, precision=prec)


def _dot_tn(a, b, prec=None):
    return lax.dot_general(a, b, (((0,), (0,)), ((), ())), preferred_element_type=F32, precision=prec)


def _sigmoid(x):
    return 0.5 * jnp.tanh(0.5 * x) + 0.5


def _tri_masks():
    r = lax.broadcasted_iota(jnp.int32, (CHUNK, CHUNK), 0)
    c = lax.broadcasted_iota(jnp.int32, (CHUNK, CHUNK), 1)
    return r >= c, r > c, r == c


def _split_hi_lo(x):
    hi = x.astype(BF16)
    return hi, (x - hi.astype(F32)).astype(BF16)


def _mask_dot(mask, x):
    hi, lo = _split_hi_lo(x)
    return _dot(mask, hi) + _dot(mask, lo)


def _cumsum_rows(g):
    causal, _, _ = _tri_masks()
    return _mask_dot(causal.astype(BF16), g)


def _cumsum_rows_transposed(dy):
    _, strict, _ = _tri_masks()
    return _mask_dot((~strict).astype(BF16), dy)


def _dot_split3(a, b):
    a_hi, a_lo = _split_hi_lo(a)
    b_hi, b_lo = _split_hi_lo(b)
    return _dot(a_hi, b_hi) + (_dot(a_hi, b_lo) + _dot(a_lo, b_hi))


@jax.custom_vjp
def _dot_precise(a, b):
    return _dot_split3(a, b)


def _dot_precise_fwd(a, b):
    return _dot_split3(a, b), (a, b)


def _dot_precise_bwd(operands, ct):
    a, b = operands
    return _dot_split3(ct, b.T), _dot_split3(a.T, ct)


_dot_precise.defvjp(_dot_precise_fwd, _dot_precise_bwd)


def _gdn_m(ks, a64s, bbs):
    causal, strict, _ = _tri_masks()
    decay = [jnp.exp(jnp.where(causal, x - x.T, -1e30)) for x in a64s]
    kk = [_dot_nt(k * b, k) for k, b in zip(ks, bbs)]
    return [jnp.where(strict, x * d, 0.0) for x, d in zip(kk, decay)]


def _gdn_inverse_stages(ks, a64s, bbs):
    ms = _gdn_m(ks, a64s, bbs)
    yield
    r = lax.broadcasted_iota(jnp.int32, (CHUNK, CHUNK), 0)
    c = lax.broadcasted_iota(jnp.int32, (CHUNK, CHUNK), 1)
    eye = (r == c).astype(F32)
    same = [jnp.right_shift(r, s) == jnp.right_shift(c, s) for s in (3, 4, 5)]
    d = [jnp.where(same[0], m, 0.0) for m in ms]
    p = [_dot(x, x) for x in d]
    yield
    t = [eye - x for x in d]
    t = [x + _dot(x, y) for x, y in zip(t, p)]
    p = [_dot(x, x) for x in p]
    yield
    t = [x + _dot(x, y) for x, y in zip(t, p)]
    yield
    for inner, outer in ((same[0], same[1]), (same[1], same[2]), (same[2], None)):
        joins = ~inner if outer is None else (outer & ~inner)
        o = [_dot(x, jnp.where(joins, m, 0.0)) for x, m in zip(t, ms)]
        yield
        t = [x - _dot(y, x) for x, y in zip(t, o)]
        yield
    res = [eye - x - _dot_split3(m, x) for m, x in zip(ms, t)]
    yield
    return [x + _dot(x, y) for x, y in zip(t, res)]


def _gdn_apply_stages(qs, ks, vs, gc, a64s, gl, bbs, ss, ts):
    causal, _, _ = _tri_masks()
    n = range(len(qs))
    qk = [_dot_nt(qs[h], ks[h]) for h in n]
    yield
    decay = [jnp.exp(jnp.where(causal, x - x.T, -1e30)) for x in a64s]
    eg = [jnp.exp(x) for x in gc]
    u = [_dot_precise(ts[h], vs[h] * bbs[h]) for h in n]
    w = [_dot_precise(ts[h], ks[h] * bbs[h] * eg[h]) for h in n]
    qk = [qk[h] * decay[h] for h in n]
    kd = [ks[h] * jnp.exp(gl[h] - gc[h]) for h in n]
    yield
    v_new = [u[h] - _dot(w[h], ss[h]) for h in n]
    q_s = [_dot(qs[h] * eg[h], ss[h]) for h in n]
    yield
    o = [q_s[h] + _dot(qk[h], v_new[h]) for h in n]
    s2 = [ss[h] * jnp.exp(gl[h]) + _dot_tn(kd[h], v_new[h]) for h in n]
    return o, s2


def _run_stages(*generators):
    results = [None] * len(generators)
    live = dict(enumerate(generators))
    while live:
        for i, gen in list(live.items()):
            try:
                next(gen)
            except StopIteration as stop:
                results[i] = stop.value
                del live[i]
    return results


def _head_slices(h):
    return slice(h * HEAD_DIM, (h + 1) * HEAD_DIM), slice(h * HEAD_DIM, h * HEAD_DIM + CHUNK)


def _gdn_head_values(x_ref, gate_ref):
    heads = range(HEADS)
    qs, ks, vs = ([x_ref[s, :, _head_slices(h)[0]] for h in heads] for s in range(3))
    gate = gate_ref[...]
    cumulative = _cumsum_rows(gate)
    total = jnp.sum(gate, axis=0, keepdims=True)
    gcums = [cumulative[:, HEADS + h:HEADS + h + 1] for h in heads]
    gtots = [total[:, HEADS + h:HEADS + h + 1] for h in heads]
    bcols = [gate[:, h:h + 1] for h in heads]
    return qs, ks, vs, gcums, gtots, bcols


def _over_lanes(cols, lanes):
    return [jnp.broadcast_to(c, (c.shape[0], lanes)) for c in cols]


def _gdn_inverse_cols(ks, gcums, bcols):
    return _gdn_inverse_stages(ks, _over_lanes(gcums, CHUNK), _over_lanes(bcols, HEAD_DIM))


def _gdn_apply_cols_stages(qs, ks, vs, gcums, gtots, bcols, ss, ts):
    return _gdn_apply_stages(qs, ks, vs, _over_lanes(gcums, HEAD_DIM), _over_lanes(gcums, CHUNK),
                             _over_lanes(gtots, HEAD_DIM), _over_lanes(bcols, HEAD_DIM), ss, ts)


def _gdn_apply_cols(qs, ks, vs, gcums, gtots, bcols, ss, ts):
    return _run_stages(_gdn_apply_cols_stages(qs, ks, vs, gcums, gtots, bcols, ss, ts))[0]


def _gdn_m_cols(ks, gcums, bcols):
    return _gdn_m(ks, _over_lanes(gcums, CHUNK), _over_lanes(bcols, HEAD_DIM))


def _gate_lanes(bcols, gcols):
    rows = gcols[0].shape[0]
    lane = lax.broadcasted_iota(jnp.int32, (rows, HEAD_DIM), 1)
    out = jnp.zeros((rows, HEAD_DIM), F32)
    for h in range(HEADS):
        if bcols is not None:
            out = jnp.where(lane == h, jnp.broadcast_to(bcols[h], out.shape), out)
        out = jnp.where(lane == HEADS + h, jnp.broadcast_to(gcols[h], out.shape), out)
    return out


def _gate_gradient(dbcols, dgcums, dgtots):
    block = _gate_lanes(dbcols, dgcums)
    lane = lax.broadcasted_iota(jnp.int32, block.shape, 1)
    return jnp.where(lane < HEADS, block, _cumsum_rows_transposed(block) + _gate_lanes(None, dgtots))


def gdn_chunk_fwd(qkv, gates, gather=()):
    _, lp, width = qkv.shape
    n_chunks = lp // CHUNK
    n = len(gather)

    def body(x_ref, gate_ref, next_ref, next_gate_ref, *refs):
        shard_refs, (o_ref, s_ref, t_ref), refs = refs[:n], refs[n:n + 3], refs[n + 3:]
        stack_refs, state, t_next, sems = refs[:n], refs[n], refs[n + 1], refs[n + 2:]
        copies = _gather_copies(shard_refs, stack_refs, *sems) if n else None

        def inverse_stages(ref, g_ref):
            _, ks, _, gcums, _, bcols = _gdn_head_values(ref, g_ref)
            return _gdn_inverse_cols(ks, gcums, bcols)

        @pl.when(pl.program_id(0) == 0)
        def _():
            state[...] = jnp.zeros_like(state)
            for h, t in enumerate(_run_stages(inverse_stages(x_ref, gate_ref))[0]):
                t_next[h] = t
            if n:
                _gather_start(copies)

        qs, ks, vs, gcums, gtots, bcols = _gdn_head_values(x_ref, gate_ref)
        ss = [state[h] for h in range(HEADS)]
        ts = [t_next[h] for h in range(HEADS)]
        ts_next, (os_, s2) = _run_stages(inverse_stages(next_ref, next_gate_ref),
                                         _gdn_apply_cols_stages(qs, ks, vs, gcums, gtots, bcols, ss, ts))
        for h in range(HEADS):
            s_ref[0, h] = ss[h]
            t_ref[0, h] = ts[h]
            t_next[h] = ts_next[h]
            o_ref[:, _head_slices(h)[0]] = os_[h]
            state[h] = s2[h]

        if n:
            @pl.when(pl.program_id(0) == n_chunks - 1)
            def _():
                _gather_finish(copies)

    o, states, tinv, *stacks = pl.pallas_call(
        body,
        name="gdn_chunk_fwd",
        grid=(n_chunks,),
        in_specs=[pl.BlockSpec((3, CHUNK, width), lambda c: (0, c, 0)),
                  pl.BlockSpec((CHUNK, HEAD_DIM), lambda c: (c, 0)),
                  pl.BlockSpec((3, CHUNK, width), lambda c: (0, jnp.minimum(c + 1, n_chunks - 1), 0)),
                  pl.BlockSpec((CHUNK, HEAD_DIM), lambda c: (jnp.minimum(c + 1, n_chunks - 1), 0))] + [ANY] * n,
        out_specs=[
            pl.BlockSpec((CHUNK, width), lambda c: (c, 0)),
            pl.BlockSpec((1, HEADS, HEAD_DIM, HEAD_DIM), lambda c: (c, 0, 0, 0)),
            pl.BlockSpec((1, HEADS, CHUNK, CHUNK), lambda c: (c, 0, 0, 0)),
        ] + [ANY] * n,
        out_shape=[
            jax.ShapeDtypeStruct((lp, width), F32),
            jax.ShapeDtypeStruct((n_chunks, HEADS, HEAD_DIM, HEAD_DIM), F32),
            jax.ShapeDtypeStruct((n_chunks, HEADS, CHUNK, CHUNK), F32),
        ] + _gather_out_shapes(gather),
        scratch_shapes=[pltpu.VMEM((HEADS, HEAD_DIM, HEAD_DIM), F32), pltpu.VMEM((HEADS, CHUNK, CHUNK), F32)]
        + (_gather_sems(n) if n else []),
        compiler_params=_params(("arbitrary",)),
    )(qkv, gates, qkv, gates, *gather)
    return o, states, tinv, _set_own_slots(stacks, gather)


def gdn_chunk_bwd(qkv, gates, states, tinv, d_o, scatter=()):
    _, lp, width = qkv.shape
    n_chunks = lp // CHUNK
    last = n_chunks - 1
    n = len(scatter)

    def body(x_ref, gate_ref, s_ref, t_ref, do_ref, *refs):
        leaving_refs, dx_ref, dgate_ref, refs = refs[:n], refs[n], refs[n + 1], refs[n + 2:]
        landing_refs, dstate, sems = refs[:n], refs[n], refs[n + 1:]
        copies = _scatter_copies(leaving_refs, landing_refs, *sems) if n else None

        @pl.when(pl.program_id(0) == 0)
        def _():
            dstate[...] = jnp.zeros_like(dstate)
            if n:
                _scatter_start(copies)

        heads = range(HEADS)
        qs, ks, vs, gcums, gtots, bcols = _gdn_head_values(x_ref, gate_ref)
        ss = [s_ref[0, h] for h in heads]
        ts = [t_ref[0, h] for h in heads]
        d_out = ([do_ref[:, _head_slices(h)[0]] for h in heads], [dstate[h] for h in heads])
        _, vjp_apply = jax.vjp(_gdn_apply_cols, qs, ks, vs, gcums, gtots, bcols, ss, ts)
        dq, dk, dv, dgc, dgt, db, ds, dt = vjp_apply(d_out)
        tts = [t.T for t in ts]
        dm = [_dot(tts[h], dt[h]) for h in heads]
        dm = [-_dot(dm[h], tts[h]) for h in heads]
        _, vjp_m = jax.vjp(_gdn_m_cols, ks, gcums, bcols)
        dk2, dgc2, db2 = vjp_m(dm)
        for h in heads:
            sl = _head_slices(h)[0]
            dx_ref[0, :, sl] = dq[h]
            dx_ref[1, :, sl] = dk[h] + dk2[h]
            dx_ref[2, :, sl] = dv[h]
            dstate[h] = ds[h]
        dgate_ref[...] = _gate_gradient([db[h] + db2[h] for h in heads], [dgc[h] + dgc2[h] for h in heads], dgt)

        if n:
            @pl.when(pl.program_id(0) == n_chunks - 1)
            def _():
                _scatter_finish(copies)

    dqkv, dgates, *landed = pl.pallas_call(
        body,
        name="gdn_chunk_bwd",
        grid=(n_chunks,),
        in_specs=[
            pl.BlockSpec((3, CHUNK, width), lambda c: (0, last - c, 0)),
            pl.BlockSpec((CHUNK, HEAD_DIM), lambda c: (last - c, 0)),
            pl.BlockSpec((1, HEADS, HEAD_DIM, HEAD_DIM), lambda c: (last - c, 0, 0, 0)),
            pl.BlockSpec((1, HEADS, CHUNK, CHUNK), lambda c: (last - c, 0, 0, 0)),
            pl.BlockSpec((CHUNK, width), lambda c: (last - c, 0)),
        ] + [ANY] * n,
        out_specs=[pl.BlockSpec((3, CHUNK, width), lambda c: (0, last - c, 0)),
                   pl.BlockSpec((CHUNK, HEAD_DIM), lambda c: (last - c, 0))] + [ANY] * n,
        out_shape=[jax.ShapeDtypeStruct(qkv.shape, F32), jax.ShapeDtypeStruct(gates.shape, F32)]
        + [jax.ShapeDtypeStruct(b.shape, b.dtype) for b in scatter],
        scratch_shapes=[pltpu.VMEM((HEADS, HEAD_DIM, HEAD_DIM), F32)] + (_scatter_sems(n) if n else []),
        compiler_params=_params(("arbitrary",)),
    )(qkv, gates, states, tinv, d_o, *scatter)
    return dqkv, dgates, _keep_own_slots(landed, scatter)


def mm_nn(a, b, *, tm, name):
    ks, m, tk = a.shape
    _, ns, _, tn = b.shape

    def body(a_ref, b_ref, o_ref):
        p = _dot(a_ref[...].astype(BF16), b_ref[...])

        @pl.when(pl.program_id(2) == 0)
        def _():
            o_ref[...] = p

        @pl.when(pl.program_id(2) > 0)
        def _():
            o_ref[...] += p

    return pl.pallas_call(
        body,
        name=name,
        grid=(ns, m // tm, ks),
        in_specs=[
            pl.BlockSpec((None, tm, tk), lambda n, i, k: (k, i, 0)),
            pl.BlockSpec((None, None, tk, tn), lambda n, i, k: (k, n, 0, 0)),
        ],
        out_specs=pl.BlockSpec((None, tm, tn), lambda n, i, k: (n, i, 0)),
        out_shape=jax.ShapeDtypeStruct((ns, m, tn), F32),
        compiler_params=_params(("arbitrary", "arbitrary", "arbitrary")),
    )(a, b)


def mm_nt(dy, w, *, tm, name, res=None, res_scale=1.0):
    ns, m, tn = dy.shape
    ks, _, tk, _ = w.shape

    def body(*refs):
        if res is None:
            dy_ref, w_ref, o_ref = refs
        else:
            dy_ref, w_ref, r_ref, o_ref = refs
        p = _dot_nt(dy_ref[...].astype(BF16), w_ref[...])

        @pl.when(pl.program_id(2) == 0)
        def _():
            o_ref[...] = p if res is None else p + res_scale * r_ref[...]

        @pl.when(pl.program_id(2) > 0)
        def _():
            o_ref[...] += p

    in_specs = [
        pl.BlockSpec((None, tm, tn), lambda k, i, n: (n, i, 0)),
        pl.BlockSpec((None, None, tk, tn), lambda k, i, n: (k, n, 0, 0)),
    ]
    args = [dy, w]
    if res is not None:
        in_specs.append(pl.BlockSpec((None, tm, tk), lambda k, i, n: (k, i, 0)))
        args.append(res)
    return pl.pallas_call(
        body,
        name=name,
        grid=(ks, m // tm, ns),
        in_specs=in_specs,
        out_specs=pl.BlockSpec((None, tm, tk), lambda k, i, n: (k, i, 0)),
        out_shape=jax.ShapeDtypeStruct((ks, m, tk), F32),
        compiler_params=_params(("arbitrary", "arbitrary", "arbitrary")),
    )(*args)


def mm_tn(x, dy, *, tm, name, rb=None):
    ks, m, tk = x.shape
    ns, _, tn = dy.shape
    rb = tk if rb is None else rb

    def body(x_ref, dy_ref, o_ref):
        @pl.when(pl.program_id(2) == 0)
        def _():
            o_ref[...] = jnp.zeros_like(o_ref)

        dyb = dy_ref[...].astype(BF16)
        for r in range(0, tk, rb):
            o_ref[r:r + rb, :] += _dot_tn(x_ref[:, r:r + rb].astype(BF16), dyb)

    return pl.pallas_call(
        body,
        name=name,
        grid=(ks, ns, m // tm),
        in_specs=[
            pl.BlockSpec((None, tm, tk), lambda k, n, i: (k, i, 0)),
            pl.BlockSpec((None, tm, tn), lambda k, n, i: (n, i, 0)),
        ],
        out_specs=pl.BlockSpec((None, None, tk, tn), lambda k, n, i: (k, n, 0, 0)),
        out_shape=jax.ShapeDtypeStruct((ks, ns, tk, tn), F32),
        compiler_params=_params(("arbitrary", "arbitrary", "arbitrary")),
    )(x, dy)


def _row_partial(x):
    rows, c = x.shape
    return jnp.sum(x.reshape(rows // 8, 8, c), axis=0)


def _layer_norm(r, g, b):
    mu = jnp.mean(r, axis=-1, keepdims=True)
    xc = r - mu
    var = jnp.mean(xc * xc, axis=-1, keepdims=True)
    return xc * lax.rsqrt(var + LN_EPS) * g + b


def _layer_norm_bwd(x, dh, g):
    mu = jnp.mean(x, axis=-1, keepdims=True)
    xc = x - mu
    rstd = lax.rsqrt(jnp.mean(xc * xc, axis=-1, keepdims=True) + LN_EPS)
    xh = xc * rstd
    dxh = dh * g
    m1 = jnp.mean(dxh, axis=-1, keepdims=True)
    m2 = jnp.mean(dxh * xh, axis=-1, keepdims=True)
    return rstd * (dxh - m1 - xh * m2), _row_partial(dh * xh), _row_partial(dh)


def mm_nn_ln(a, b, h_prev, g, beta, *, tm, name):
    ks, m, tk = a.shape
    d = b.shape[3]

    def body(a_ref, b_ref, hp_ref, g_ref, be_ref, r_ref, h_ref, hb_ref):
        p = _dot(a_ref[...].astype(BF16), b_ref[...])

        @pl.when(pl.program_id(1) == 0)
        def _():
            r_ref[...] = p

        @pl.when(pl.program_id(1) > 0)
        def _():
            r_ref[...] += p

        @pl.when(pl.program_id(1) == ks - 1)
        def _():
            r = ALPHA * hp_ref[...] + r_ref[...]
            r_ref[...] = r
            h = _layer_norm(r, g_ref[...], be_ref[...])
            h_ref[...] = h
            hb_ref[...] = h.astype(BF16)

    row = pl.BlockSpec((None, tm, d), lambda i, k: (0, i, 0))
    vec = pl.BlockSpec((1, d), lambda i, k: (0, 0))
    return pl.pallas_call(
        body,
        name=name,
        grid=(m // tm, ks),
        in_specs=[
            pl.BlockSpec((None, tm, tk), lambda i, k: (k, i, 0)),
            pl.BlockSpec((None, None, tk, d), lambda i, k: (k, 0, 0, 0)),
            row, vec, vec,
        ],
        out_specs=[row, row, row],
        out_shape=[jax.ShapeDtypeStruct((1, m, d), F32)] * 2 + [jax.ShapeDtypeStruct((1, m, d), BF16)],
        compiler_params=_params(("arbitrary", "arbitrary")),
    )(a, b, h_prev, g, beta)


def mm_nt_ln_bwd(dy, w, res, r, g, *, tm, name, swap=()):
    ns, m, tn = dy.shape
    d = w.shape[2]
    n_swap = len(swap)
    last_tile = m // tm - 1

    def body(dy_ref, w_ref, res_ref, r_ref, g_ref, *refs):
        leaving_refs, (dr_ref, dgb_ref), refs = refs[:n_swap], refs[n_swap:n_swap + 2], refs[n_swap + 2:]
        copies = _swap_copies(leaving_refs, refs[:n_swap], *refs[n_swap:]) if n_swap else None
        p = _dot_nt(dy_ref[...].astype(BF16), w_ref[...])

        @pl.when((pl.program_id(0) == 0) & (pl.program_id(1) == 0))
        def _():
            dgb_ref[...] = jnp.zeros_like(dgb_ref)
            if n_swap:
                _swap_start(copies)

        @pl.when(pl.program_id(1) == 0)
        def _():
            dr_ref[...] = p + ALPHA * res_ref[...]

        @pl.when(pl.program_id(1) > 0)
        def _():
            dr_ref[...] += p

        @pl.when(pl.program_id(1) == ns - 1)
        def _():
            for rows in (pl.ds(0, tm // 2), pl.ds(tm // 2, tm // 2)):
                dr, dgamma, dbeta = _layer_norm_bwd(r_ref[rows, :], dr_ref[rows, :], g_ref[...])
                dr_ref[rows, :] = dr
                dgb_ref[0] += dgamma
                dgb_ref[1] += dbeta

        if n_swap:
            @pl.when((pl.program_id(0) == last_tile) & (pl.program_id(1) == ns - 1))
            def _():
                _swap_finish(copies)

    row = pl.BlockSpec((None, tm, d), lambda i, n: (0, i, 0))
    dr, dgb, *landed = pl.pallas_call(
        body,
        name=name,
        grid=(m // tm, ns),
        in_specs=[
            pl.BlockSpec((None, tm, tn), lambda i, n: (n, i, 0)),
            pl.BlockSpec((None, None, d, tn), lambda i, n: (0, n, 0, 0)),
            row, row,
            pl.BlockSpec((1, d), lambda i, n: (0, 0)),
        ] + [ANY] * n_swap,
        out_specs=[row, pl.BlockSpec((2, 8, d), lambda i, n: (0, 0, 0))] + [ANY] * n_swap,
        out_shape=[jax.ShapeDtypeStruct((1, m, d), F32), jax.ShapeDtypeStruct((2, 8, d), F32)] + _swap_out_shapes(swap),
        scratch_shapes=_swap_sems(n_swap) if n_swap else [],
        compiler_params=_params(("arbitrary", "arbitrary")),
    )(dy, w, res, r, g, *swap)
    return dr, dgb, landed


def loss_ln_bwd(h, target, r, g, *, first, count, tm):
    _, lp, d = h.shape

    def body(h_ref, t_ref, r_ref, g_ref, dr_ref, dgb_ref, l_ref):
        row = pl.program_id(0) * tm + lax.broadcasted_iota(jnp.int32, (tm, d), 0)
        valid = (row >= first) & (row < first + count)
        err = jnp.where(valid, h_ref[...] - t_ref[...], 0.0)
        dr, dgamma, dbeta = _layer_norm_bwd(r_ref[...], err * (1.0 / d), g_ref[...])
        dr_ref[...] = dr

        @pl.when(pl.program_id(0) == 0)
        def _():
            dgb_ref[...] = jnp.zeros_like(dgb_ref)
            l_ref[...] = jnp.zeros_like(l_ref)

        dgb_ref[0] += dgamma
        dgb_ref[1] += dbeta
        l_ref[...] += _row_partial(err * err) * (0.5 / d)

    row3 = pl.BlockSpec((None, tm, d), lambda i: (0, i, 0))
    return pl.pallas_call(
        body,
        name="loss_ln4_bwd",
        grid=(lp // tm,),
        in_specs=[row3, pl.BlockSpec((tm, d), lambda i: (i, 0)), row3, pl.BlockSpec((1, d), lambda i: (0, 0))],
        out_specs=[row3, pl.BlockSpec((2, 8, d), lambda i: (0, 0, 0)), pl.BlockSpec((8, d), lambda i: (0, 0))],
        out_shape=[jax.ShapeDtypeStruct((1, lp, d), F32), jax.ShapeDtypeStruct((2, 8, d), F32),
                   jax.ShapeDtypeStruct((8, d), F32)],
        compiler_params=_params(("arbitrary",)),
    )(h, target, r, g)


def _halo_index(tile, tm):
    return jnp.maximum(tile * (tm // HALO) - 1, 0)


def _conv_fwd(xs_ref, w, taps, tm):
    acc = w(0) * xs_ref[pl.ds(HALO - taps + 1, tm), :]
    for j in range(1, taps):
        acc += w(j) * xs_ref[pl.ds(HALO - taps + 1 + j, tm), :]
    return acc


def _conv_bwd_x(dcs_ref, w, taps, tm):
    acc = w(0) * dcs_ref[pl.ds(taps - 1, tm), :]
    for j in range(1, taps):
        acc += w(j) * dcs_ref[pl.ds(taps - 1 - j, tm), :]
    return acc


SUB = 8
LANES = 128
PAIR = 2 * SUB
STRIP_UNROLL = 2


def _pair_rows(r0):
    return pl.ds(r0, SUB), pl.ds(r0 + SUB if isinstance(r0, int) else pl.multiple_of(r0 + SUB, SUB), SUB)


def _shift_down(cur, prev, s):
    if s == 0:
        return cur
    row = lax.broadcasted_iota(jnp.int32, cur.shape, 0)
    return jnp.where(row < s, pltpu.roll(prev, s, axis=0), pltpu.roll(cur, s, axis=0))


def _shift_up(cur, nxt, s):
    if s == 0:
        return cur
    row = lax.broadcasted_iota(jnp.int32, cur.shape, 0)
    return jnp.where(row < SUB - s, pltpu.roll(cur, SUB - s, axis=0), pltpu.roll(nxt, SUB - s, axis=0))


def _silu_parts(c):
    sg = _sigmoid(c)
    return c * sg, sg * (1.0 + c * (1.0 - sg))


def _head_sum(x):
    rows, c = x.shape
    parts = []
    for h in range(c // HEAD_DIM):
        s = jnp.sum(x[:, h * HEAD_DIM:(h + 1) * HEAD_DIM], axis=-1, keepdims=True)
        parts.append(jnp.broadcast_to(s, (rows, HEAD_DIM)))
    return parts[0] if len(parts) == 1 else jnp.concatenate(parts, axis=-1)


def _log1p(y):
    u = 1.0 + y
    d = u - 1.0
    return jnp.where(d == 0.0, y, jnp.log(u) * (y / jnp.where(d == 0.0, 1.0, d)))


def _softplus(x):
    return jnp.maximum(x, 0.0) + _log1p(jnp.exp(-jnp.abs(x)))


def _gate_values(x, al, dt):
    lane = lax.broadcasted_iota(jnp.int32, x.shape, 1)
    is_beta, is_g = lane < HEADS, (lane >= HEADS) & (lane < 2 * HEADS)
    return _sigmoid(x), -jnp.exp(al) * _softplus(x + dt), is_beta, is_g


def gdn_gates_fwd(pba, al, dt, *, tm):
    _, lp, width = pba.shape

    def body(x_ref, al_ref, dt_ref, o_ref):
        beta, g, is_beta, is_g = _gate_values(x_ref[...], al_ref[...], dt_ref[...])
        o_ref[...] = jnp.where(is_beta, beta, jnp.where(is_g, g, 0.0))

    vec = pl.BlockSpec((1, width), lambda i: (0, 0))
    return pl.pallas_call(
        body,
        name="gdn_gates_fwd",
        grid=(lp // tm,),
        in_specs=[pl.BlockSpec((None, tm, width), lambda i: (0, i, 0)), vec, vec],
        out_specs=pl.BlockSpec((tm, width), lambda i: (i, 0)),
        out_shape=jax.ShapeDtypeStruct((lp, width), F32),
        compiler_params=_params(("arbitrary",)),
    )(pba, al, dt)


def gdn_gates_bwd(pba, dgates, al, dt, *, tm):
    _, lp, width = pba.shape

    def body(x_ref, d_ref, al_ref, dt_ref, dx_ref, dsc_ref):
        x = x_ref[...]
        beta, g, is_beta, is_g = _gate_values(x, al_ref[...], dt_ref[...])
        d = d_ref[...]
        dg = jnp.where(is_g, d, 0.0)
        da = dg * -jnp.exp(al_ref[...]) * _sigmoid(x + dt_ref[...])
        dx_ref[...] = jnp.where(is_beta, d * beta * (1.0 - beta), da).astype(dx_ref.dtype)

        @pl.when(pl.program_id(0) == 0)
        def _():
            dsc_ref[...] = jnp.zeros_like(dsc_ref)

        dsc_ref[0] += _row_partial(dg * g)
        dsc_ref[1] += _row_partial(da)

    vec = pl.BlockSpec((1, width), lambda i: (0, 0))
    return pl.pallas_call(
        body,
        name="gdn_gates_bwd",
        grid=(lp // tm,),
        in_specs=[pl.BlockSpec((None, tm, width), lambda i: (0, i, 0)), pl.BlockSpec((tm, width), lambda i: (i, 0)), vec, vec],
        out_specs=[pl.BlockSpec((None, tm, width), lambda i: (0, i, 0)), pl.BlockSpec((2, SUB, width), lambda i: (0, 0, 0))],
        out_shape=[jax.ShapeDtypeStruct((1, lp, width), BF16), jax.ShapeDtypeStruct((2, SUB, width), F32)],
        compiler_params=_params(("arbitrary",)),
    )(pba, dgates, al, dt)


def gdn_pre_fwd(p3, conv_w, *, tm, cb):
    _, lp, width = p3.shape
    taps = conv_w.shape[1]

    def body(x_ref, halo_ref, w_ref, o_ref, xs):
        i = pl.program_id(1)
        for s in range(3):
            xs[s, 0:HALO, :] = jnp.where(i > 0, halo_ref[s], 0.0)
            xs[s, HALO:, :] = x_ref[s]
            c = _conv_fwd(xs.at[s], lambda j, s=s: w_ref[s, j:j + 1, :], taps, tm)
            y, _ = _silu_parts(c)
            if s < 2:
                y = y * lax.rsqrt(_head_sum(y * y) + L2_EPS)
                if s == 0:
                    y = y * Q_SCALE
            o_ref[s] = y

    return pl.pallas_call(
        body,
        name="gdn_pre_fwd",
        grid=(width // cb, lp // tm),
        in_specs=[
            pl.BlockSpec((3, tm, cb), lambda j, i: (0, i, j)),
            pl.BlockSpec((3, HALO, cb), lambda j, i: (0, _halo_index(i, tm), j)),
            pl.BlockSpec((3, taps, cb), lambda j, i: (0, 0, j)),
        ],
        out_specs=pl.BlockSpec((3, tm, cb), lambda j, i: (0, i, j)),
        out_shape=jax.ShapeDtypeStruct((3, lp, width), F32),
        scratch_shapes=[pltpu.VMEM((3, tm + HALO, cb), F32)],
        compiler_params=_params(("arbitrary", "arbitrary")),
    )(p3, p3, conv_w)


def gdn_pre_bwd(p3, dqkv, conv_w, *, tm, cb):
    _, lp, width = p3.shape
    taps = conv_w.shape[1]
    last = lp // tm - 1

    def body(x_ref, halo_ref, d_ref, w_ref, dx_ref, dw_ref, xs, dcs, carry):
        step = pl.program_id(1)
        tile = last - step

        @pl.when(step == 0)
        def _():
            carry[...] = jnp.zeros_like(carry)
            dw_ref[...] = jnp.zeros_like(dw_ref)

        for s in range(3):
            w = lambda j, s=s: w_ref[s, j:j + 1, :]
            xs[s, 0:HALO, :] = jnp.where(tile > 0, halo_ref[s], 0.0)
            xs[s, HALO:, :] = x_ref[s]
            c = _conv_fwd(xs.at[s], w, taps, tm)
            y, dsilu = _silu_parts(c)
            dy = d_ref[s]
            if s < 2:
                rn = lax.rsqrt(_head_sum(y * y) + L2_EPS)
                yn = y * rn
                if s == 0:
                    dy = dy * Q_SCALE
                dy = rn * (dy - yn * _head_sum(dy * yn))
            dc = dy * dsilu
            dcs[s, 0:tm, :] = dc
            dcs[s, tm:, :] = carry[s]
            dx_ref[s] = _conv_bwd_x(dcs.at[s], w, taps, tm).astype(dx_ref.dtype)
            carry[s] = dc[0:HALO, :]
            for j in range(taps):
                dw_ref[s, j] += _row_partial(dc * xs[s, pl.ds(HALO - taps + 1 + j, tm), :])

    tile_spec = pl.BlockSpec((3, tm, cb), lambda j, i: (0, last - i, j))
    return pl.pallas_call(
        body,
        name="gdn_pre_bwd",
        grid=(width // cb, lp // tm),
        in_specs=[
            tile_spec,
            pl.BlockSpec((3, HALO, cb), lambda j, i: (0, _halo_index(last - i, tm), j)),
            tile_spec,
            pl.BlockSpec((3, taps, cb), lambda j, i: (0, 0, j)),
        ],
        out_specs=[tile_spec, pl.BlockSpec((3, taps, SUB, cb), lambda j, i: (0, 0, 0, j))],
        out_shape=[jax.ShapeDtypeStruct((3, lp, width), BF16), jax.ShapeDtypeStruct((3, taps, SUB, width), F32)],
        scratch_shapes=[
            pltpu.VMEM((3, tm + HALO, cb), F32),
            pltpu.VMEM((3, tm + HALO, cb), F32),
            pltpu.VMEM((3, HALO, cb), F32),
        ],
        compiler_params=_params(("arbitrary", "arbitrary")),
    )(p3, p3, dqkv, conv_w)


def gdn_post_fwd(o, z, nw_b, *, tm):
    _, lp, width = o.shape

    def body(o_ref, z_ref, nw_ref, y_ref):
        ov = o_ref[...]
        rn = lax.rsqrt(_head_sum(ov * ov) * (1.0 / HEAD_DIM) + RMS_EPS)
        gate, _ = _silu_parts(z_ref[...])
        y_ref[...] = (ov * rn * nw_ref[...] * gate).astype(y_ref.dtype)

    row = pl.BlockSpec((None, tm, width), lambda i: (0, i, 0))
    return pl.pallas_call(
        body,
        name="gdn_post_fwd",
        grid=(lp // tm,),
        in_specs=[row, row, pl.BlockSpec((1, width), lambda i: (0, 0))],
        out_specs=row,
        out_shape=jax.ShapeDtypeStruct((1, lp, width), BF16),
        compiler_params=_params(("arbitrary",)),
    )(o, z, nw_b)


def gdn_post_bwd(o, z, dy, nw_b, *, tm):
    _, lp, width = o.shape

    def body(o_ref, z_ref, dy_ref, nw_ref, do_ref, dz_ref, dnw_ref):
        ov = o_ref[...]
        rn = lax.rsqrt(_head_sum(ov * ov) * (1.0 / HEAD_DIM) + RMS_EPS)
        yn = ov * rn
        gate, dgate = _silu_parts(z_ref[...])
        d_on = dy_ref[...] * gate
        dz_ref[...] = (dy_ref[...] * yn * nw_ref[...] * dgate).astype(dz_ref.dtype)
        a = d_on * nw_ref[...]
        do_ref[...] = rn * (a - yn * (_head_sum(a * yn) * (1.0 / HEAD_DIM)))

        @pl.when(pl.program_id(0) == 0)
        def _():
            dnw_ref[...] = jnp.zeros_like(dnw_ref)

        dnw_ref[...] += _row_partial(d_on * yn)

    row = pl.BlockSpec((None, tm, width), lambda i: (0, i, 0))
    return pl.pallas_call(
        body,
        name="gdn_post_bwd",
        grid=(lp // tm,),
        in_specs=[row, row, row, pl.BlockSpec((1, width), lambda i: (0, 0))],
        out_specs=[row, row, pl.BlockSpec((8, width), lambda i: (0, 0))],
        out_shape=[jax.ShapeDtypeStruct((1, lp, width), F32), jax.ShapeDtypeStruct((1, lp, width), BF16),
                   jax.ShapeDtypeStruct((8, width), F32)],
        compiler_params=_params(("arbitrary",)),
    )(o, z, dy, nw_b)


def ffn_act_fwd(up, conv_w, *, tm, name):
    _, lp, c_w = up.shape
    taps = conv_w.shape[1]

    def body(u_ref, halo_ref, g_ref, w_ref, o_ref):
        first_tile = pl.program_id(1) == 0

        def strip(cur, prev, rows, cs):
            conv = w_ref[taps - 1:taps, cs] * cur
            for j in range(taps - 1):
                conv += w_ref[j:j + 1, cs] * _shift_down(cur, prev, taps - 1 - j)
            y, _ = _silu_parts(conv)
            return y * g_ref[rows, cs]

        def pair(r0, above_of):
            top, bot = _pair_rows(r0)
            for c0 in range(0, c_w, LANES):
                cs = slice(c0, c0 + LANES)
                cur_t, cur_b = u_ref[top, cs], u_ref[bot, cs]
                out = [strip(cur_t, above_of(cs), top, cs), strip(cur_b, cur_t, bot, cs)]
                o_ref[pl.ds(r0, PAIR), cs] = jnp.concatenate(out, axis=0).astype(o_ref.dtype)

        pair(0, lambda cs: jnp.where(first_tile, 0.0, halo_ref[:, cs]))

        def loop_body(s, carry):
            r0 = pl.multiple_of(s * PAIR, PAIR)
            pair(r0, lambda cs: u_ref[pl.ds(pl.multiple_of(r0 - SUB, SUB), SUB), cs])
            return carry

        lax.fori_loop(1, tm // PAIR, loop_body, 0, unroll=STRIP_UNROLL)

    return pl.pallas_call(
        body,
        name=name,
        grid=(2, lp // tm),
        in_specs=[
            pl.BlockSpec((None, tm, c_w), lambda s, i: (s, i, 0)),
            pl.BlockSpec((None, HALO, c_w), lambda s, i: (s, _halo_index(i, tm), 0)),
            pl.BlockSpec((None, tm, c_w), lambda s, i: (2 + s, i, 0)),
            pl.BlockSpec((None, taps, c_w), lambda s, i: (s, 0, 0)),
        ],
        out_specs=pl.BlockSpec((None, tm, c_w), lambda s, i: (s, i, 0)),
        out_shape=jax.ShapeDtypeStruct((2, lp, c_w), BF16),
        compiler_params=_params(("arbitrary", "arbitrary")),
    )(up, up, up, conv_w)


def ffn_act_bwd(up, dact, conv_w, *, tm, name):
    _, lp, c_w = up.shape
    taps = conv_w.shape[1]
    last = lp // tm - 1
    n_pairs = tm // PAIR

    def body(u_ref, halo_ref, g_ref, d_ref, w_ref, dup_ref, dw_ref, below):
        step = pl.program_id(1)
        first_tile = step == last

        @pl.when(step == 0)
        def _():
            below[...] = jnp.zeros_like(below)
            dw_ref[...] = jnp.zeros_like(dw_ref)

        def strip(cur, prev, rows, cs, nxt):
            shifted = [_shift_down(cur, prev, taps - 1 - j) for j in range(taps)]
            conv = w_ref[0:1, cs] * shifted[0]
            for j in range(1, taps):
                conv += w_ref[j:j + 1, cs] * shifted[j]
            y, dsilu = _silu_parts(conv)
            d = d_ref[rows, cs]
            dc = d * g_ref[rows, cs] * dsilu
            dx = w_ref[taps - 1:taps, cs] * dc
            for j in range(taps - 1):
                dx += w_ref[j:j + 1, cs] * _shift_up(dc, nxt, taps - 1 - j)
            return dx, d * y, dc, [dc * s for s in shifted]

        def pair(r0, above_of):
            top, bot = _pair_rows(r0)
            both = pl.ds(r0, PAIR)
            for c0 in range(0, c_w, LANES):
                cs = slice(c0, c0 + LANES)
                cur_t, cur_b = u_ref[top, cs], u_ref[bot, cs]
                dx_b, dg_b, dc_b, dw_b = strip(cur_b, cur_t, bot, cs, below[:, cs])
                dx_t, dg_t, dc_t, dw_t = strip(cur_t, above_of(cs), top, cs, dc_b)
                below[:, cs] = dc_t
                dup_ref[0, both, cs] = jnp.concatenate([dx_t, dx_b], axis=0).astype(dup_ref.dtype)
                dup_ref[1, both, cs] = jnp.concatenate([dg_t, dg_b], axis=0).astype(dup_ref.dtype)
                for j in range(taps):
                    dw_ref[j, :, cs] += dw_t[j] + dw_b[j]

        def loop_body(it, carry):
            r0 = pl.multiple_of((n_pairs - 1 - it) * PAIR, PAIR)
            pair(r0, lambda cs: u_ref[pl.ds(pl.multiple_of(r0 - SUB, SUB), SUB), cs])
            return carry

        lax.fori_loop(0, n_pairs - 1, loop_body, 0, unroll=STRIP_UNROLL)
        pair(0, lambda cs: jnp.where(first_tile, 0.0, halo_ref[:, cs]))

    return pl.pallas_call(
        body,
        name=name,
        grid=(2, lp // tm),
        in_specs=[
            pl.BlockSpec((None, tm, c_w), lambda s, i: (s, last - i, 0)),
            pl.BlockSpec((None, HALO, c_w), lambda s, i: (s, _halo_index(last - i, tm), 0)),
            pl.BlockSpec((None, tm, c_w), lambda s, i: (2 + s, last - i, 0)),
            pl.BlockSpec((None, tm, c_w), lambda s, i: (s, last - i, 0)),
            pl.BlockSpec((None, taps, c_w), lambda s, i: (s, 0, 0)),
        ],
        out_specs=[
            pl.BlockSpec((2, None, tm, c_w), lambda s, i: (0, s, last - i, 0)),
            pl.BlockSpec((None, taps, SUB, c_w), lambda s, i: (s, 0, 0, 0)),
        ],
        out_shape=[jax.ShapeDtypeStruct((2, 2, lp, c_w), BF16), jax.ShapeDtypeStruct((2, taps, SUB, c_w), F32)],
        scratch_shapes=[pltpu.VMEM((SUB, c_w), F32)],
        compiler_params=_params(("arbitrary", "arbitrary")),
    )(up, up, up, dact, conv_w)


def sc_fwd(pb, conv_w, *, tm, cb):
    _, lp, width = pb.shape
    taps = conv_w.shape[0]

    def body(x_ref, halo_ref, w_ref, o_ref):
        first_tile = pl.program_id(1) == 0

        def strip(cur, prev, rows, cs):
            conv = w_ref[taps - 1:taps, cs] * cur
            for j in range(taps - 1):
                conv += w_ref[j:j + 1, cs] * _shift_down(cur, prev, taps - 1 - j)
            return x_ref[0, rows, cs] * conv

        def pair(r0, above_of):
            top, bot = _pair_rows(r0)
            for c0 in range(0, cb, LANES):
                cs = slice(c0, c0 + LANES)
                cur_t = x_ref[1, top, cs] * x_ref[2, top, cs]
                cur_b = x_ref[1, bot, cs] * x_ref[2, bot, cs]
                out = [strip(cur_t, above_of(cs), top, cs), strip(cur_b, cur_t, bot, cs)]
                o_ref[pl.ds(r0, PAIR), cs] = jnp.concatenate(out, axis=0).astype(o_ref.dtype)

        pair(0, lambda cs: jnp.where(first_tile, 0.0, halo_ref[1, :, cs] * halo_ref[2, :, cs]))

        def loop_body(k, carry):
            r0 = pl.multiple_of(k * PAIR, PAIR)
            before = pl.ds(pl.multiple_of(r0 - SUB, SUB), SUB)
            pair(r0, lambda cs: x_ref[1, before, cs] * x_ref[2, before, cs])
            return carry

        lax.fori_loop(1, tm // PAIR, loop_body, 0, unroll=STRIP_UNROLL)

    return pl.pallas_call(
        body,
        name="sc_fwd",
        grid=(width // cb, lp // tm),
        in_specs=[
            pl.BlockSpec((3, tm, cb), lambda j, i: (0, i, j)),
            pl.BlockSpec((3, HALO, cb), lambda j, i: (0, _halo_index(i, tm), j)),
            pl.BlockSpec((taps, cb), lambda j, i: (0, j)),
        ],
        out_specs=pl.BlockSpec((None, tm, cb), lambda j, i: (0, i, j)),
        out_shape=jax.ShapeDtypeStruct((1, lp, width), BF16),
        compiler_params=_params(("arbitrary", "arbitrary")),
    )(pb, pb, conv_w)


def sc_bwd(pb, ds, conv_w, *, tm, cb):
    _, lp, width = pb.shape
    taps = conv_w.shape[0]
    last = lp // tm - 1
    n_pairs = tm // PAIR

    def body(x_ref, halo_ref, d_ref, w_ref, dx_ref, dw_ref, below):
        step = pl.program_id(1)
        first_tile = step == last

        @pl.when(step == 0)
        def _():
            below[...] = jnp.zeros_like(below)
            dw_ref[...] = jnp.zeros_like(dw_ref)

        def strip(cur, prev, rows, cs, nxt):
            gate, left, right = x_ref[0, rows, cs], x_ref[1, rows, cs], x_ref[2, rows, cs]
            shifted = [_shift_down(cur, prev, taps - 1 - j) for j in range(taps)]
            conv = w_ref[0:1, cs] * shifted[0]
            for j in range(1, taps):
                conv += w_ref[j:j + 1, cs] * shifted[j]
            d = d_ref[rows, cs]
            dc = d * gate
            dp = w_ref[taps - 1:taps, cs] * dc
            for j in range(taps - 1):
                dp += w_ref[j:j + 1, cs] * _shift_up(dc, nxt, taps - 1 - j)
            return d * conv, dp * right, dp * left, dc, [dc * s for s in shifted]

        def pair(r0, above_of):
            top, bot = _pair_rows(r0)
            both = pl.ds(r0, PAIR)
            for c0 in range(0, cb, LANES):
                cs = slice(c0, c0 + LANES)
                cur_t = x_ref[1, top, cs] * x_ref[2, top, cs]
                cur_b = x_ref[1, bot, cs] * x_ref[2, bot, cs]
                *dx_b, dc_b, dw_b = strip(cur_b, cur_t, bot, cs, below[:, cs])
                *dx_t, dc_t, dw_t = strip(cur_t, above_of(cs), top, cs, dc_b)
                below[:, cs] = dc_t
                for s in range(3):
                    dx_ref[s, both, cs] = jnp.concatenate([dx_t[s], dx_b[s]], axis=0).astype(dx_ref.dtype)
                for j in range(taps):
                    dw_ref[j, :, cs] += dw_t[j] + dw_b[j]

        def loop_body(it, carry):
            r0 = pl.multiple_of((n_pairs - 1 - it) * PAIR, PAIR)
            before = pl.ds(pl.multiple_of(r0 - SUB, SUB), SUB)
            pair(r0, lambda cs: x_ref[1, before, cs] * x_ref[2, before, cs])
            return carry

        lax.fori_loop(0, n_pairs - 1, loop_body, 0, unroll=STRIP_UNROLL)
        pair(0, lambda cs: jnp.where(first_tile, 0.0, halo_ref[1, :, cs] * halo_ref[2, :, cs]))

    tile_spec = pl.BlockSpec((3, tm, cb), lambda j, i: (0, last - i, j))
    return pl.pallas_call(
        body,
        name="sc_bwd",
        grid=(width // cb, lp // tm),
        in_specs=[
            tile_spec,
            pl.BlockSpec((3, HALO, cb), lambda j, i: (0, _halo_index(last - i, tm), j)),
            pl.BlockSpec((None, tm, cb), lambda j, i: (0, last - i, j)),
            pl.BlockSpec((taps, cb), lambda j, i: (0, j)),
        ],
        out_specs=[tile_spec, pl.BlockSpec((taps, SUB, cb), lambda j, i: (0, 0, j))],
        out_shape=[jax.ShapeDtypeStruct((3, lp, width), BF16), jax.ShapeDtypeStruct((taps, SUB, width), F32)],
        scratch_shapes=[pltpu.VMEM((SUB, cb), F32)],
        compiler_params=_params(("arbitrary", "arbitrary")),
    )(pb, pb, ds, conv_w)


TILE_BYTES = 1536 * 1024


def _rows_tile(rows, cols, multiple=8):
    if rows * cols * 4 <= TILE_BYTES or rows % multiple:
        return rows
    best = multiple
    for t in range(multiple, rows + 1, multiple):
        if rows % t == 0 and t * cols * 4 <= TILE_BYTES:
            best = t
    return best


def pair_sum(g, landed, core, out_dtype, name):
    _, rows, cols = g.shape
    half = rows // 2
    tr = _rows_tile(half, cols, 16)
    nb = half // tr

    def body(c_ref, g_ref, l_ref, o_ref):
        o_ref[...] = (g_ref[...] + l_ref[...]).astype(out_dtype)

    return pl.pallas_call(
        body,
        name=name,
        grid_spec=pltpu.PrefetchScalarGridSpec(
            num_scalar_prefetch=1,
            grid=(4, nb),
            in_specs=[
                pl.BlockSpec((None, tr, cols), lambda s, i, c: (s, c[0] * nb + i, 0)),
                pl.BlockSpec((None, tr, cols), lambda s, i, c: (s, i, 0)),
            ],
            out_specs=pl.BlockSpec((None, tr, cols), lambda s, i, c: (s, i, 0)),
        ),
        out_shape=jax.ShapeDtypeStruct((4, half, cols), out_dtype),
        compiler_params=_params(("arbitrary", "arbitrary")),
    )(core, g, landed)


def chip_sum(x, name):
    _, rows, cols = x.shape
    tr = _rows_tile(rows, cols, 16)

    def body(x0, x1, x2, x3, o_ref):
        acc = x0[...].astype(F32) + x1[...].astype(F32)
        o_ref[...] = (acc + x2[...].astype(F32)) + x3[...].astype(F32)

    return pl.pallas_call(
        body,
        name=name,
        grid=(rows // tr,),
        in_specs=[pl.BlockSpec((None, tr, cols), lambda i, k=k: (k, i, 0)) for k in range(4)],
        out_specs=pl.BlockSpec((tr, cols), lambda i: (i, 0)),
        out_shape=jax.ShapeDtypeStruct((rows, cols), F32),
        compiler_params=_params(("arbitrary",)),
    )(x, x, x, x)


def adamw(w, g, m, v, name):
    shape = w.shape
    cols = shape[-1]
    rows = w.size // cols
    tr = _rows_tile(rows, cols)

    def body(w_ref, g_ref, m_ref, v_ref, d_ref, m2_ref, v2_ref):
        gv = g_ref[...]
        m2 = ADAM_B1 * m_ref[...] + (1.0 - ADAM_B1) * gv
        v2 = ADAM_B2 * v_ref[...] + (1.0 - ADAM_B2) * (gv * gv)
        m_hat = m2 / (1.0 - ADAM_B1 ** ADAM_STEP)
        v_hat = v2 / (1.0 - ADAM_B2 ** ADAM_STEP)
        d_ref[...] = -ADAM_LR * (m_hat / (jnp.sqrt(v_hat) + ADAM_EPS) + ADAM_WD * w_ref[...])
        m2_ref[...] = m2
        v2_ref[...] = v2

    spec = pl.BlockSpec((tr, cols), lambda i: (i, 0))
    outs = pl.pallas_call(
        body,
        name=name,
        grid=(rows // tr,),
        in_specs=[spec] * 4,
        out_specs=[spec] * 3,
        out_shape=[jax.ShapeDtypeStruct((rows, cols), F32)] * 3,
        compiler_params=_params(("arbitrary",)),
    )(*[t.reshape(rows, cols) for t in (w, g, m, v)])
    return tuple(o.reshape(shape) for o in outs)


MESH_ID = pl.DeviceIdType.MESH
ANY = pl.BlockSpec(memory_space=pl.ANY)


def _place():
    x, y, c = lax.axis_index("x"), lax.axis_index("y"), lax.axis_index("c")
    other_chips = [(1 - x, y), (x, 1 - y), (1 - x, 1 - y)]
    return x, y, c, other_chips


def all_gather_shards(bufs, name):
    n = len(bufs)

    def body(*refs):
        x_refs, o_refs = refs[:n], refs[n:2 * n]
        copies = _gather_copies(x_refs, o_refs, *refs[2 * n:])
        _gather_start(copies)
        _gather_finish(copies)

    outs = pl.pallas_call(
        body,
        name=name,
        in_specs=[ANY] * n,
        out_specs=[ANY] * n,
        out_shape=_gather_out_shapes(bufs),
        scratch_shapes=_gather_sems(n),
    )(*bufs)
    return _set_own_slots(outs, bufs)


def _gather_out_shapes(bufs):
    return [jax.ShapeDtypeStruct((4,) + b.shape, b.dtype) for b in bufs]


def _gather_sems(n):
    return [pltpu.SemaphoreType.DMA((6 * n,)), pltpu.SemaphoreType.DMA((6 * n,))]


def _set_own_slots(outs, bufs):
    if not outs:
        return []
    me = 2 * lax.axis_index("x") + lax.axis_index("y")
    return [lax.dynamic_update_index_in_dim(o, b, me, 0) for o, b in zip(outs, bufs)]


def _gather_copies(x_refs, o_refs, send_sems, recv_sems):
    x, y, c, chips = _place()
    me = 2 * x + y
    sibling = (x, y, 1 - c)

    def part(a, slot, hf):
        half = x_refs[a].shape[0] // 2
        return o_refs[a].at[slot, pl.ds(hf * half, half), :]

    def mine(a):
        half = x_refs[a].shape[0] // 2
        return x_refs[a].at[pl.ds(c * half, half), :]

    def copy(k, src, dst, to):
        return pltpu.make_async_remote_copy(src_ref=src, dst_ref=dst, send_sem=send_sems.at[k],
                                            recv_sem=recv_sems.at[k], device_id=to, device_id_type=MESH_ID)

    sends, arrivals, passes, passed = [], [], [], []
    for a in range(len(x_refs)):
        for j, (px, py) in enumerate(chips):
            landed, theirs = part(a, 2 * px + py, c), part(a, 2 * px + py, 1 - c)
            sends.append(copy(6 * a + j, mine(a), part(a, me, c), (px, py, c)))
            arrivals.append(copy(6 * a + j, mine(a), landed, (px, py, c)))
            passes.append(copy(6 * a + 3 + j, landed, landed, sibling))
            passed.append(copy(6 * a + 3 + j, theirs, theirs, sibling))
    return sends, arrivals, passes, passed


def _gather_start(copies):
    for cp in copies[0]:
        cp.start()


def _gather_finish(copies):
    sends, arrivals, passes, passed = copies
    for arrival, cp in zip(arrivals, passes):
        arrival.wait_recv()
        cp.start()
    for cp in passed:
        cp.wait_recv()
    for cp in sends + passes:
        cp.wait_send()


def swap_halves(bufs, name):
    n = len(bufs)

    def body(*refs):
        copies = _swap_copies(refs[:n], refs[n:2 * n], *refs[2 * n:])
        _swap_start(copies)
        _swap_finish(copies)

    return pl.pallas_call(
        body,
        name=name,
        in_specs=[ANY] * n,
        out_specs=[ANY] * n,
        out_shape=_swap_out_shapes(bufs),
        scratch_shapes=_swap_sems(n),
    )(*bufs)


def _swap_out_shapes(bufs):
    return [jax.ShapeDtypeStruct((4, b.shape[1] // 2, b.shape[2]), b.dtype) for b in bufs]


def _swap_sems(n):
    return [pltpu.SemaphoreType.DMA((n,)), pltpu.SemaphoreType.DMA((n,))]


def _swap_copies(x_refs, o_refs, send_sems, recv_sems):
    x, y, c, _ = _place()
    copies = []
    for a, (x_ref, o_ref) in enumerate(zip(x_refs, o_refs)):
        half = x_ref.shape[1] // 2
        copies.append(pltpu.make_async_remote_copy(src_ref=x_ref.at[:, pl.ds((1 - c) * half, half), :], dst_ref=o_ref,
                                                   send_sem=send_sems.at[a], recv_sem=recv_sems.at[a],
                                                   device_id=(x, y, 1 - c), device_id_type=MESH_ID))
    return copies


def _swap_start(copies):
    for cp in copies:
        cp.start()


def _swap_finish(copies):
    for cp in copies:
        cp.wait()


def scatter_to_chips(bufs, name):
    n = len(bufs)

    def body(*refs):
        x_refs, o_refs = refs[:n], refs[n:2 * n]
        copies = _scatter_copies(x_refs, o_refs, *refs[2 * n:])
        _scatter_start(copies)
        _scatter_finish(copies)

    outs = pl.pallas_call(
        body,
        name=name,
        in_specs=[ANY] * n,
        out_specs=[ANY] * n,
        out_shape=[jax.ShapeDtypeStruct(b.shape, b.dtype) for b in bufs],
        scratch_shapes=_scatter_sems(n),
    )(*bufs)
    return _keep_own_slots(outs, bufs)


def _scatter_sems(n):
    return [pltpu.SemaphoreType.DMA((3 * n,)), pltpu.SemaphoreType.DMA((3 * n,))]


def _keep_own_slots(outs, bufs):
    if not outs:
        return []
    me = 2 * lax.axis_index("x") + lax.axis_index("y")
    return [lax.dynamic_update_index_in_dim(o, lax.dynamic_index_in_dim(b, me, 0, keepdims=False), me, 0)
            for o, b in zip(outs, bufs)]


def _scatter_copies(x_refs, o_refs, send_sems, recv_sems):
    x, y, c, chips = _place()
    me = 2 * x + y

    def copy(a, j, src_slot, dst_slot, px, py):
        return pltpu.make_async_remote_copy(src_ref=x_refs[a].at[src_slot], dst_ref=o_refs[a].at[dst_slot],
                                            send_sem=send_sems.at[3 * a + j], recv_sem=recv_sems.at[3 * a + j],
                                            device_id=(px, py, c), device_id_type=MESH_ID)

    sends = [copy(a, j, 2 * px + py, me, px, py) for a in range(len(x_refs)) for j, (px, py) in enumerate(chips)]
    arrivals = [copy(a, j, me, 2 * px + py, px, py) for a in range(len(x_refs)) for j, (px, py) in enumerate(chips)]
    return sends, arrivals


def _scatter_start(copies):
    for cp in copies[0]:
        cp.start()


def _scatter_finish(copies):
    for cp in copies[1]:
        cp.wait_recv()
    for cp in copies[0]:
        cp.wait_send()


def share_halves(groups, name):
    bufs = [b for grp in groups for b in grp]
    where = [(gi, li) for gi, grp in enumerate(groups) for li in range(len(grp))]
    n = len(bufs)

    def body(*refs):
        x_refs, o_refs = refs[:n], refs[n:n + len(groups)]
        send_sems, recv_sems = refs[n + len(groups):]
        x, y, c, _ = _place()
        sent, arrive = [], []
        for a, (gi, li) in enumerate(where):

            def copy(hf, a=a, gi=gi, li=li):
                return pltpu.make_async_remote_copy(src_ref=x_refs[a], dst_ref=o_refs[gi].at[li, hf],
                                                    send_sem=send_sems.at[a], recv_sem=recv_sems.at[a],
                                                    device_id=(x, y, 1 - c), device_id_type=MESH_ID)

            sent.append(copy(c))
            arrive.append(copy(1 - c))
        for cp in sent:
            cp.start()
        for cp in arrive:
            cp.wait_recv()
        for cp in sent:
            cp.wait_send()

    outs = pl.pallas_call(
        body,
        name=name,
        in_specs=[ANY] * n,
        out_specs=[ANY] * len(groups),
        out_shape=[jax.ShapeDtypeStruct((len(grp), 2) + grp[0].shape, grp[0].dtype) for grp in groups],
        scratch_shapes=[pltpu.SemaphoreType.DMA((n,)), pltpu.SemaphoreType.DMA((n,))],
    )(*bufs)
    c = lax.axis_index("c")
    full = [lax.dynamic_update_index_in_dim(o, jnp.stack(grp), c, 1) for o, grp in zip(outs, groups)]
    return [t.reshape(t.shape[0], 2 * t.shape[2], t.shape[3]) for t in full]


def pair_sums(bufs, landed, dtypes, tag):
    core = lax.axis_index("c").astype(jnp.int32).reshape(1)
    return [pair_sum(b, l, core, dt, "rs_pair_sum_%s%d" % (tag, i)) for i, (b, l, dt) in enumerate(zip(bufs, landed, dtypes))]


def _row_tiles(length):
    return (640, 640) if length > 2048 else (128, 64)


def _divisor_tile(rows, target):
    return max(t for t in range(8, min(rows, target) + 1, 8) if rows % t == 0)


def _local_step(x, target, wt, late_shards, layout_late, complete_grads, sum_pairs):
    seq, d = x.shape
    length = N_META + seq
    tm, tm_ffn = _row_tiles(length)
    lp = -(-length // tm) * tm
    tail = jnp.zeros((lp - length, d), F32)
    h0 = jnp.concatenate([wt["meta"], x, tail], axis=0)[None]
    tgt = jnp.concatenate([jnp.zeros((N_META, d), F32), target, tail], axis=0)
    nn = functools.partial(mm_nn, tm=_divisor_tile(lp, 1664))
    nt = functools.partial(mm_nt, tm=_divisor_tile(lp, 1040))
    tn = functools.partial(mm_tn, tm=_divisor_tile(lp, 1664), rb=256)
    nn_ln = functools.partial(mm_nn_ln, tm=_divisor_tile(lp, 832))
    nt_ln_bwd = functools.partial(mm_nt_ln_bwd, tm=_divisor_tile(lp, 1040))
    ln_g = [wt["ln_mix_g"][0:1], wt["ln_ffn_g"][0:1], wt["ln_mix_g"][1:2], wt["ln_ffn_g"][1:2]]
    ln_b = [wt["ln_mix_b"][0:1], wt["ln_ffn_b"][0:1], wt["ln_mix_b"][1:2], wt["ln_ffn_b"][1:2]]

    h0b = h0.astype(BF16)
    p3 = nn(h0b, wt["a3"], name="a_in3")
    pz = nn(h0b, wt["az"], name="a_inz")
    pba = nn(h0b, wt["a_ba"], name="a_inba")
    qkv = gdn_pre_fwd(p3, wt["a_conv3"], tm=tm, cb=2 * HEAD_DIM)
    gates = gdn_gates_fwd(pba, wt["alog_lanes"], wt["dtb_lanes"], tm=tm)
    o, states, tinv, late_stacks = gdn_chunk_fwd(qkv, gates, late_shards)
    wt = {**wt, **layout_late(late_stacks)}
    onz = gdn_post_fwd(o[None], pz, wt["anorm_b"], tm=tm)
    r1, h1, h1b = nn_ln(onz, wt["a_out"], h0, ln_g[0], ln_b[0], name="a_out_ln1")
    up0 = nn(h1b, wt["up"][0], name="up0")
    act0 = ffn_act_fwd(up0, wt["fconv"][0], tm=tm_ffn, name="ffn_act0")
    r2, h2, h2b = nn_ln(act0, wt["down"][0], h1, ln_g[1], ln_b[1], name="down0_ln2")
    pb = nn(h2b, wt["b_in"], name="b_in")
    sc = sc_fwd(pb, wt["b_conv"], tm=tm_ffn, cb=d)
    r3, h3, h3b = nn_ln(sc, wt["b_out"], h2, ln_g[2], ln_b[2], name="b_out_ln3")
    up1 = nn(h3b, wt["up"][1], name="up1")
    act1 = ffn_act_fwd(up1, wt["fconv"][1], tm=tm_ffn, name="ffn_act1")
    r4, h4, _ = nn_ln(act1, wt["down"][1], h3, ln_g[3], ln_b[3], name="down1_ln4")

    grads = {}
    dr4, dgb4, loss_part = loss_ln_bwd(h4, tgt, r4, ln_g[3], first=N_META, count=seq, tm=tm)
    d_down1 = tn(act1, dr4, name="d_down1")
    dact1 = nt(dr4, wt["down"][1], name="d_act1")
    dup1, dfconv1 = ffn_act_bwd(up1, dact1, wt["fconv"][1], tm=tm_ffn, name="ffn_act1_bwd")
    dup1 = dup1.reshape(up1.shape)
    d_up1 = tn(h3b, dup1, name="d_up1")

    dr3, dgb3, _ = nt_ln_bwd(dup1, wt["up"][1], dr4, r3, ln_g[2], name="d_h3_ln3")
    d_bout = tn(sc, dr3, name="d_b_out")
    dsc = nt(dr3, wt["b_out"], name="d_sc")
    dpb, dbconv = sc_bwd(pb, dsc, wt["b_conv"], tm=tm_ffn, cb=d)
    d_bin = tn(h2b, dpb, name="d_b_in")

    dr2, dgb2, _ = nt_ln_bwd(dpb, wt["b_in"], dr3, r2, ln_g[1], name="d_h2_ln2")
    d_down0 = tn(act0, dr2, name="d_down0")
    dact0 = nt(dr2, wt["down"][0], name="d_act0")
    dup0, dfconv0 = ffn_act_bwd(up0, dact0, wt["fconv"][0], tm=tm_ffn, name="ffn_act0_bwd")
    dup0 = dup0.reshape(up0.shape)
    d_up0 = tn(h1b, dup0, name="d_up0")
    grads["b_w_in"] = [d_bin[0].transpose(1, 0, 2).reshape(d, 4, 3 * d // 4).transpose(1, 0, 2)]
    grads["b_w_out"] = [d_bout.reshape(4, d // 4, d)]
    grads["ffn_w_up"] = [d_up0[0], d_up1[0]]
    grads["ffn_w_down"] = [t.reshape(4, -1, d) for t in (d_down0, d_down1)]
    complete = complete_grads(grads)

    dr1, dgb1, from_sibling = nt_ln_bwd(dup0, wt["up"][0], dr2, r1, ln_g[0], name="d_h1_ln1", swap=complete)
    leaving = sum_pairs(complete, from_sibling)
    d_aout = tn(onz, dr1, name="d_a_out")
    donz = nt(dr1, wt["a_out"], name="d_onz")
    d_o, dz, dnw = gdn_post_bwd(o[None], pz, donz, wt["anorm_b"], tm=tm)
    dqkv, dgates, landed = gdn_chunk_bwd(qkv, gates, states, tinv, d_o[0], leaving)
    dp3, daconv = gdn_pre_bwd(p3, dqkv, wt["a_conv3"], tm=tm, cb=2 * HEAD_DIM)
    dpba, dscal = gdn_gates_bwd(pba, dgates, wt["alog_lanes"], wt["dtb_lanes"], tm=tm)
    d_a3 = tn(h0b, dp3, name="d_a_in3")
    d_az = tn(h0b, dz, name="d_a_inz")
    d_aba = tn(h0b, dpba, name="d_a_inba")
    dh0 = nt(dp3, wt["a3"], res=dr1, res_scale=ALPHA, name="d_h0a")
    dh0 = nt(dz, wt["az"], res=dh0, res_scale=1.0, name="d_h0z")
    dh0 = nt(dpba, wt["a_ba"], res=dh0, res_scale=1.0, name="d_h0")

    width = HEADS * HEAD_DIM
    d_a_in = jnp.concatenate([d_a3[0, 0], d_a3[0, 1], d_a3[0, 2], d_az[0, 0], d_aba[0, 0][:, :2 * HEADS]], axis=1)
    n_in = d_a_in.shape[1] // 4
    grads["a_w_in"] = [d_a_in.reshape(d, 4, n_in).transpose(1, 0, 2)]
    grads["a_w_out"] = [d_aout.reshape(4, width // 4, d)]
    grads["a_conv"] = daconv.sum(axis=2).transpose(1, 0, 2).reshape(1, GDN_CONV, 3 * width)
    per_head = dscal.sum(axis=1)[:, HEADS:2 * HEADS]
    grads["a_log"] = per_head[0][None]
    grads["a_dt_bias"] = per_head[1][None]
    grads["a_norm"] = dnw.reshape(8, HEADS, HEAD_DIM).sum(axis=(0, 1))[None]
    grads["b_conv"] = dbconv.sum(axis=1)[None]
    lns = [dgb1, dgb2, dgb3, dgb4]
    grads["ln_mix_g"] = jnp.stack([lns[0][0].sum(0), lns[2][0].sum(0)])
    grads["ln_mix_b"] = jnp.stack([lns[0][1].sum(0), lns[2][1].sum(0)])
    grads["ln_ffn_g"] = jnp.stack([lns[1][0].sum(0), lns[3][0].sum(0)])
    grads["ln_ffn_b"] = jnp.stack([lns[1][1].sum(0), lns[3][1].sum(0)])
    grads["ffn_conv"] = jnp.stack([t.sum(axis=2).transpose(1, 0, 2).reshape(FFN_CONV, -1) for t in (dfconv0, dfconv1)])
    grads["meta"] = dh0[0, :N_META]
    return loss_part, dh0, grads, landed


WEIGHTS = ["meta", "a_w_in", "a_conv", "a_log", "a_dt_bias", "a_norm", "a_w_out", "b_w_in", "b_conv", "b_w_out",
           "ln_mix_g", "ln_mix_b", "ffn_w_up", "ffn_conv", "ffn_w_down", "ln_ffn_g", "ln_ffn_b"]
EARLY_WEIGHTS = ["a_w_in", "a_w_out"]
LATE_WEIGHTS = ["b_w_in", "b_w_out", "ffn_w_up", "ffn_w_down"]
MATMUL_WEIGHTS = EARLY_WEIGHTS + LATE_WEIGHTS
SMALL_SHARDED = ["a_conv", "b_conv", "ffn_conv", "meta"]
REPLICATED = ["a_log", "a_dt_bias", "a_norm", "ln_mix_g", "ln_mix_b", "ln_ffn_g", "ln_ffn_b"]
SHARD_AXIS = {"meta": 1, "a_w_in": 2, "a_conv": 2, "a_w_out": 1, "b_w_in": 2, "b_conv": 2, "b_w_out": 1,
              "ffn_w_up": 2, "ffn_conv": 2, "ffn_w_down": 1}
PACK_COLS = 1024
PACK_ROWS_MULTIPLE = 32


def _pack(pieces, lead=()):
    flat = jnp.concatenate([p.reshape(lead + (-1,)) for p in pieces], axis=-1)
    n = flat.shape[-1]
    rows = -(-n // (PACK_COLS * PACK_ROWS_MULTIPLE)) * PACK_ROWS_MULTIPLE
    flat = jnp.pad(flat, [(0, 0)] * len(lead) + [(0, rows * PACK_COLS - n)])
    return flat.reshape(lead + (rows, PACK_COLS))


def _unpack(buf, shapes, lead=()):
    flat = buf.reshape(lead + (-1,))
    out, off = [], 0
    for shp in shapes:
        n = 1
        for s in shp:
            n *= s
        out.append(flat[..., off:off + n].reshape(lead + tuple(shp)))
        off += n
    return out


def _join_shards(stacked, axis):
    return jnp.concatenate([stacked[k] for k in range(4)], axis=axis)


def _split_shards(full, axis):
    return jnp.stack(jnp.split(full, 4, axis=axis))


def _weight_layers(w, names):
    return [w[n][l].astype(BF16) for n in names for l in range(w[n].shape[0])]


def _per_weight(arrays, w, names):
    it = iter(arrays)
    return {n: [next(it) for _ in range(w[n].shape[0])] for n in names}


def _layout_early(full, w):
    width = HEADS * HEAD_DIM
    wt = {n: w[n] for n in ("ln_mix_g", "ln_mix_b", "ln_ffn_g", "ln_ffn_b")}
    w_in = _join_shards(full["a_w_in"][0], 1)
    d = w_in.shape[0]
    n_ff = full["ffn_conv"].shape[2] // 2
    blocks = [w_in[:, s * width:(s + 1) * width] for s in range(4)]
    wt["a3"] = jnp.stack(blocks[:3])[None]
    wt["az"] = blocks[3][None, None]
    wt["a_ba"] = jnp.pad(w_in[:, 4 * width:], ((0, 0), (0, HEAD_DIM - 2 * HEADS)))[None, None]
    wt["a_out"] = full["a_w_out"][0].reshape(1, 1, width, d)
    wt["a_conv3"] = full["a_conv"][0].reshape(GDN_CONV, 3, width).transpose(1, 0, 2)
    wt["b_conv"] = full["b_conv"][0]
    wt["fconv"] = [full["ffn_conv"][l].reshape(FFN_CONV, 2, n_ff).transpose(1, 0, 2) for l in range(2)]
    wt["meta"] = full["meta"]
    in_g_lanes = (HEADS, HEAD_DIM - 2 * HEADS)
    wt["alog_lanes"] = jnp.pad(w["a_log"][0], in_g_lanes)[None]
    wt["dtb_lanes"] = jnp.pad(w["a_dt_bias"][0], in_g_lanes)[None]
    wt["anorm_b"] = jnp.tile(w["a_norm"][0], HEADS)[None]
    return wt


def _layout_late(full):
    d = full["b_w_in"][0].shape[1]
    n_ff = full["ffn_w_up"][0].shape[2]
    return {
        "b_in": _join_shards(full["b_w_in"][0], 1).reshape(d, 3, d).transpose(1, 0, 2)[None],
        "b_out": full["b_w_out"][0].reshape(1, 1, d, d),
        "up": [t[None] for t in full["ffn_w_up"]],
        "down": [t.reshape(2, 1, n_ff, d) for t in full["ffn_w_down"]],
    }


def kernel(x, meta, a_w_in, a_conv, a_log, a_dt_bias, a_norm, a_w_out, b_w_in, b_conv, b_w_out, ln_mix_g, ln_mix_b, ffn_w_up, ffn_conv, ffn_w_down, ln_ffn_g, ln_ffn_b, loss_target, m_meta, m_a_w_in, m_a_conv, m_a_log, m_a_dt_bias, m_a_norm, m_a_w_out, m_b_w_in, m_b_conv, m_b_w_out, m_ln_mix_g, m_ln_mix_b, m_ffn_w_up, m_ffn_conv, m_ffn_w_down, m_ln_ffn_g, m_ln_ffn_b, v_meta, v_a_w_in, v_a_conv, v_a_log, v_a_dt_bias, v_a_norm, v_a_w_out, v_b_w_in, v_b_conv, v_b_w_out, v_ln_mix_g, v_ln_mix_b, v_ffn_w_up, v_ffn_conv, v_ffn_w_down, v_ln_ffn_g, v_ln_ffn_b):
    w = dict(meta=meta, a_w_in=a_w_in, a_conv=a_conv, a_log=a_log, a_dt_bias=a_dt_bias, a_norm=a_norm, a_w_out=a_w_out,
             b_w_in=b_w_in, b_conv=b_conv, b_w_out=b_w_out, ln_mix_g=ln_mix_g, ln_mix_b=ln_mix_b, ffn_w_up=ffn_w_up,
             ffn_conv=ffn_conv, ffn_w_down=ffn_w_down, ln_ffn_g=ln_ffn_g, ln_ffn_b=ln_ffn_b)
    m = dict(meta=m_meta, a_w_in=m_a_w_in, a_conv=m_a_conv, a_log=m_a_log, a_dt_bias=m_a_dt_bias, a_norm=m_a_norm,
             a_w_out=m_a_w_out, b_w_in=m_b_w_in, b_conv=m_b_conv, b_w_out=m_b_w_out, ln_mix_g=m_ln_mix_g,
             ln_mix_b=m_ln_mix_b, ffn_w_up=m_ffn_w_up, ffn_conv=m_ffn_conv, ffn_w_down=m_ffn_w_down,
             ln_ffn_g=m_ln_ffn_g, ln_ffn_b=m_ln_ffn_b)
    v = dict(meta=v_meta, a_w_in=v_a_w_in, a_conv=v_a_conv, a_log=v_a_log, a_dt_bias=v_a_dt_bias, a_norm=v_a_norm,
             a_w_out=v_a_w_out, b_w_in=v_b_w_in, b_conv=v_b_conv, b_w_out=v_b_w_out, ln_mix_g=v_ln_mix_g,
             ln_mix_b=v_ln_mix_b, ffn_w_up=v_ffn_w_up, ffn_conv=v_ffn_conv, ffn_w_down=v_ffn_w_down,
             ln_ffn_g=v_ln_ffn_g, ln_ffn_b=v_ln_ffn_b)
    seq = x.shape[1]
    *stacks, small = all_gather_shards(_weight_layers(w, EARLY_WEIGHTS) + [_pack([w[n] for n in SMALL_SHARDED])],
                                       "gather_early")
    full = _per_weight(stacks, w, EARLY_WEIGHTS)
    for n, t in zip(SMALL_SHARDED, _unpack(small, [w[n].shape for n in SMALL_SHARDED], lead=(4,))):
        full[n] = _join_shards(t, SHARD_AXIS[n])

    def layout_late(late_stacks):
        return _layout_late(_per_weight(late_stacks, w, LATE_WEIGHTS))

    def complete_grads(grads):
        return [g for n in LATE_WEIGHTS for g in grads[n]]

    def sum_pairs(bufs, from_sibling):
        return pair_sums(bufs, from_sibling, [BF16] * len(bufs), "late")

    loss_part, dh0, grads, landed_late = _local_step(x[0], loss_target[0], _layout_early(full, w),
                                                     _weight_layers(w, LATE_WEIGHTS), layout_late, complete_grads, sum_pairs)
    pieces = [_split_shards(grads[n], SHARD_AXIS[n]) for n in SMALL_SHARDED]
    same = jnp.concatenate([grads[n].reshape(-1) for n in REPLICATED] + [jnp.sum(loss_part).reshape(1)])
    pieces.append(jnp.broadcast_to(same, (4,) + same.shape))
    bufs = [g for n in EARLY_WEIGHTS for g in grads[n]] + [_pack(pieces, lead=(4,))]
    from_sibling = swap_halves(bufs, "rs_pair_early")
    landed = scatter_to_chips(pair_sums(bufs, from_sibling, [BF16] * (len(bufs) - 1) + [F32], "early"), "rs_chips_early")
    totals = [chip_sum(t, "rs_chip_sum%d" % i) for i, t in enumerate(landed + landed_late)]
    by_weight = _per_weight(totals[:len(bufs) - 1] + totals[len(bufs):], w, MATMUL_WEIGHTS)
    *shared, small_total = share_halves([by_weight[n] for n in MATMUL_WEIGHTS] + [[totals[len(bufs) - 1]]], "rs_share")
    grad_w = {n: t.reshape(w[n].shape) for n, t in zip(MATMUL_WEIGHTS, shared)}
    rest = SMALL_SHARDED + REPLICATED
    unpacked = _unpack(small_total[0], [w[n].shape for n in rest] + [()])
    grad_w.update(zip(rest, unpacked[:-1]))
    loss = unpacked[-1]
    grad_x = dh0[:, N_META:N_META + seq]
    steps = [adamw(w[n], grad_w[n], m[n], v[n], "adamw_" + n) for n in WEIGHTS]
    return (loss, grad_x, *[grad_w[n] for n in WEIGHTS], *[s[0] for s in steps], *[s[1] for s in steps],
            *[s[2] for s in steps])
```

```python
import functools

import jax
import jax.numpy as jnp
from jax import lax
from jax.experimental import pallas as pl
from jax.experimental.pallas import tpu as pltpu

F32 = jnp.float32
BF16 = jnp.bfloat16

N_META = 16
HEADS = 8
HEAD_DIM = 128
CHUNK = 64
GDN_CONV = 4
FFN_CONV = 3
ALPHA = 4.0 ** 0.25
LN_EPS = 1e-5
RMS_EPS = 1e-6
L2_EPS = 1e-6
Q_SCALE = HEAD_DIM ** -0.5

ADAM_LR = 0.001
ADAM_B1 = 0.9
ADAM_B2 = 0.999
ADAM_EPS = 1e-08
ADAM_WD = 0.01
ADAM_STEP = 10

HALO = 8
VMEM_LIMIT = 48 * 1024 * 1024


def _params(sem=None):
    return pltpu.CompilerParams(dimension_semantics=sem, vmem_limit_bytes=VMEM_LIMIT)


def _dot(a, b, prec=None):
    return jnp.dot(a, b, preferred_element_type=F32, precision=prec)


def _dot_nt(a, b, prec=None):
    return lax.dot_general(a, b, (((1,), (1,)), ((), ())), preferred_element_type=F32, precision=prec)


def _dot_tn(a, b, prec=None):
    return lax.dot_general(a, b, (((0,), (0,)), ((), ())), preferred_element_type=F32, precision=prec)


def _sigmoid(x):
    return 0.5 * jnp.tanh(0.5 * x) + 0.5


def _tri_masks():
    r = lax.broadcasted_iota(jnp.int32, (CHUNK, CHUNK), 0)
    c = lax.broadcasted_iota(jnp.int32, (CHUNK, CHUNK), 1)
    return r >= c, r > c, r == c


def _split_hi_lo(x):
    hi = x.astype(BF16)
    return hi, (x - hi.astype(F32)).astype(BF16)


def _mask_dot(mask, x):
    hi, lo = _split_hi_lo(x)
    return _dot(mask, hi) + _dot(mask, lo)


def _cumsum_rows(g):
    causal, _, _ = _tri_masks()
    return _mask_dot(causal.astype(BF16), g)


def _cumsum_rows_transposed(dy):
    _, strict, _ = _tri_masks()
    return _mask_dot((~strict).astype(BF16), dy)


def _dot_split3(a, b):
    a_hi, a_lo = _split_hi_lo(a)
    b_hi, b_lo = _split_hi_lo(b)
    return _dot(a_hi, b_hi) + (_dot(a_hi, b_lo) + _dot(a_lo, b_hi))


@jax.custom_vjp
def _dot_precise(a, b):
    return _dot_split3(a, b)


def _dot_precise_fwd(a, b):
    return _dot_split3(a, b), (a, b)


def _dot_precise_bwd(operands, ct):
    a, b = operands
    return _dot_split3(ct, b.T), _dot_split3(a.T, ct)


_dot_precise.defvjp(_dot_precise_fwd, _dot_precise_bwd)


def _gdn_m(ks, a64s, bbs):
    causal, strict, _ = _tri_masks()
    decay = [jnp.exp(jnp.where(causal, x - x.T, -1e30)) for x in a64s]
    kk = [_dot_nt(k * b, k) for k, b in zip(ks, bbs)]
    return [jnp.where(strict, x * d, 0.0) for x, d in zip(kk, decay)]


def _gdn_inverse_stages(ks, a64s, bbs):
    ms = _gdn_m(ks, a64s, bbs)
    yield
    r = lax.broadcasted_iota(jnp.int32, (CHUNK, CHUNK), 0)
    c = lax.broadcasted_iota(jnp.int32, (CHUNK, CHUNK), 1)
    eye = (r == c).astype(F32)
    same = [jnp.right_shift(r, s) == jnp.right_shift(c, s) for s in (3, 4, 5)]
    d = [jnp.where(same[0], m, 0.0) for m in ms]
    p = [_dot(x, x) for x in d]
    yield
    t = [eye - x for x in d]
    t = [x + _dot(x, y) for x, y in zip(t, p)]
    p = [_dot(x, x) for x in p]
    yield
    t = [x + _dot(x, y) for x, y in zip(t, p)]
    yield
    for inner, outer in ((same[0], same[1]), (same[1], same[2]), (same[2], None)):
        joins = ~inner if outer is None else (outer & ~inner)
        o = [_dot(x, jnp.where(joins, m, 0.0)) for x, m in zip(t, ms)]
        yield
        t = [x - _dot(y, x) for x, y in zip(t, o)]
        yield
    res = [eye - x - _dot_split3(m, x) for m, x in zip(ms, t)]
    yield
    return [x + _dot(x, y) for x, y in zip(t, res)]


def _gdn_apply_stages(qs, ks, vs, gc, a64s, gl, bbs, ss, ts):
    causal, _, _ = _tri_masks()
    n = range(len(qs))
    qk = [_dot_nt(qs[h], ks[h]) for h in n]
    yield
    decay = [jnp.exp(jnp.where(causal, x - x.T, -1e30)) for x in a64s]
    eg = [jnp.exp(x) for x in gc]
    u = [_dot_precise(ts[h], vs[h] * bbs[h]) for h in n]
    w = [_dot_precise(ts[h], ks[h] * bbs[h] * eg[h]) for h in n]
    qk = [qk[h] * decay[h] for h in n]
    kd = [ks[h] * jnp.exp(gl[h] - gc[h]) for h in n]
    yield
    v_new = [u[h] - _dot(w[h], ss[h]) for h in n]
    q_s = [_dot(qs[h] * eg[h], ss[h]) for h in n]
    yield
    o = [q_s[h] + _dot(qk[h], v_new[h]) for h in n]
    s2 = [ss[h] * jnp.exp(gl[h]) + _dot_tn(kd[h], v_new[h]) for h in n]
    return o, s2


def _run_stages(*generators):
    results = [None] * len(generators)
    live = dict(enumerate(generators))
    while live:
        for i, gen in list(live.items()):
            try:
                next(gen)
            except StopIteration as stop:
                results[i] = stop.value
                del live[i]
    return results


def _head_slices(h):
    return slice(h * HEAD_DIM, (h + 1) * HEAD_DIM), slice(h * HEAD_DIM, h * HEAD_DIM + CHUNK)


def _gdn_head_values(x_ref, gate_ref):
    heads = range(HEADS)
    qs, ks, vs = ([x_ref[s, :, _head_slices(h)[0]] for h in heads] for s in range(3))
    gate = gate_ref[...]
    cumulative = _cumsum_rows(gate)
    total = jnp.sum(gate, axis=0, keepdims=True)
    gcums = [cumulative[:, HEADS + h:HEADS + h + 1] for h in heads]
    gtots = [total[:, HEADS + h:HEADS + h + 1] for h in heads]
    bcols = [gate[:, h:h + 1] for h in heads]
    return qs, ks, vs, gcums, gtots, bcols


def _over_lanes(cols, lanes):
    return [jnp.broadcast_to(c, (c.shape[0], lanes)) for c in cols]


def _gdn_inverse_cols(ks, gcums, bcols):
    return _gdn_inverse_stages(ks, _over_lanes(gcums, CHUNK), _over_lanes(bcols, HEAD_DIM))


def _gdn_apply_cols_stages(qs, ks, vs, gcums, gtots, bcols, ss, ts):
    return _gdn_apply_stages(qs, ks, vs, _over_lanes(gcums, HEAD_DIM), _over_lanes(gcums, CHUNK),
                             _over_lanes(gtots, HEAD_DIM), _over_lanes(bcols, HEAD_DIM), ss, ts)


def _gdn_apply_cols(qs, ks, vs, gcums, gtots, bcols, ss, ts):
    return _run_stages(_gdn_apply_cols_stages(qs, ks, vs, gcums, gtots, bcols, ss, ts))[0]


def _gdn_m_cols(ks, gcums, bcols):
    return _gdn_m(ks, _over_lanes(gcums, CHUNK), _over_lanes(bcols, HEAD_DIM))


def _gate_lanes(bcols, gcols):
    rows = gcols[0].shape[0]
    lane = lax.broadcasted_iota(jnp.int32, (rows, HEAD_DIM), 1)
    out = jnp.zeros((rows, HEAD_DIM), F32)
    for h in range(HEADS):
        if bcols is not None:
            out = jnp.where(lane == h, jnp.broadcast_to(bcols[h], out.shape), out)
        out = jnp.where(lane == HEADS + h, jnp.broadcast_to(gcols[h], out.shape), out)
    return out


def _gate_gradient(dbcols, dgcums, dgtots):
    block = _gate_lanes(dbcols, dgcums)
    lane = lax.broadcasted_iota(jnp.int32, block.shape, 1)
    return jnp.where(lane < HEADS, block, _cumsum_rows_transposed(block) + _gate_lanes(None, dgtots))


def gdn_chunk_fwd(qkv, gates, gather=()):
    _, lp, width = qkv.shape
    n_chunks = lp // CHUNK
    n = len(gather)

    def body(x_ref, gate_ref, next_ref, next_gate_ref, *refs):
        shard_refs, (o_ref, s_ref, t_ref), refs = refs[:n], refs[n:n + 3], refs[n + 3:]
        stack_refs, state, t_next, sems = refs[:n], refs[n], refs[n + 1], refs[n + 2:]
        copies = _gather_copies(shard_refs, stack_refs, *sems) if n else None

        def inverse_stages(ref, g_ref):
            _, ks, _, gcums, _, bcols = _gdn_head_values(ref, g_ref)
            return _gdn_inverse_cols(ks, gcums, bcols)

        @pl.when(pl.program_id(0) == 0)
        def _():
            state[...] = jnp.zeros_like(state)
            for h, t in enumerate(_run_stages(inverse_stages(x_ref, gate_ref))[0]):
                t_next[h] = t
            if n:
                _gather_start(copies)

        qs, ks, vs, gcums, gtots, bcols = _gdn_head_values(x_ref, gate_ref)
        ss = [state[h] for h in range(HEADS)]
        ts = [t_next[h] for h in range(HEADS)]
        ts_next, (os_, s2) = _run_stages(inverse_stages(next_ref, next_gate_ref),
                                         _gdn_apply_cols_stages(qs, ks, vs, gcums, gtots, bcols, ss, ts))
        for h in range(HEADS):
            s_ref[0, h] = ss[h]
            t_ref[0, h] = ts[h]
            t_next[h] = ts_next[h]
            o_ref[:, _head_slices(h)[0]] = os_[h]
            state[h] = s2[h]

        if n:
            @pl.when(pl.program_id(0) == n_chunks - 1)
            def _():
                _gather_finish(copies)

    o, states, tinv, *stacks = pl.pallas_call(
        body,
        name="gdn_chunk_fwd",
        grid=(n_chunks,),
        in_specs=[pl.BlockSpec((3, CHUNK, width), lambda c: (0, c, 0)),
                  pl.BlockSpec((CHUNK, HEAD_DIM), lambda c: (c, 0)),
                  pl.BlockSpec((3, CHUNK, width), lambda c: (0, jnp.minimum(c + 1, n_chunks - 1), 0)),
                  pl.BlockSpec((CHUNK, HEAD_DIM), lambda c: (jnp.minimum(c + 1, n_chunks - 1), 0))] + [ANY] * n,
        out_specs=[
            pl.BlockSpec((CHUNK, width), lambda c: (c, 0)),
            pl.BlockSpec((1, HEADS, HEAD_DIM, HEAD_DIM), lambda c: (c, 0, 0, 0)),
            pl.BlockSpec((1, HEADS, CHUNK, CHUNK), lambda c: (c, 0, 0, 0)),
        ] + [ANY] * n,
        out_shape=[
            jax.ShapeDtypeStruct((lp, width), F32),
            jax.ShapeDtypeStruct((n_chunks, HEADS, HEAD_DIM, HEAD_DIM), F32),
            jax.ShapeDtypeStruct((n_chunks, HEADS, CHUNK, CHUNK), F32),
        ] + _gather_out_shapes(gather),
        scratch_shapes=[pltpu.VMEM((HEADS, HEAD_DIM, HEAD_DIM), F32), pltpu.VMEM((HEADS, CHUNK, CHUNK), F32)]
        + (_gather_sems(n) if n else []),
        compiler_params=_params(("arbitrary",)),
    )(qkv, gates, qkv, gates, *gather)
    return o, states, tinv, _set_own_slots(stacks, gather)


def gdn_chunk_bwd(qkv, gates, states, tinv, d_o, scatter=()):
    _, lp, width = qkv.shape
    n_chunks = lp // CHUNK
    last = n_chunks - 1
    n = len(scatter)

    def body(x_ref, gate_ref, s_ref, t_ref, do_ref, *refs):
        leaving_refs, dx_ref, dgate_ref, refs = refs[:n], refs[n], refs[n + 1], refs[n + 2:]
        landing_refs, dstate, sems = refs[:n], refs[n], refs[n + 1:]
        copies = _scatter_copies(leaving_refs, landing_refs, *sems) if n else None

        @pl.when(pl.program_id(0) == 0)
        def _():
            dstate[...] = jnp.zeros_like(dstate)
            if n:
                _scatter_start(copies)

        heads = range(HEADS)
        qs, ks, vs, gcums, gtots, bcols = _gdn_head_values(x_ref, gate_ref)
        ss = [s_ref[0, h] for h in heads]
        ts = [t_ref[0, h] for h in heads]
        d_out = ([do_ref[:, _head_slices(h)[0]] for h in heads], [dstate[h] for h in heads])
        _, vjp_apply = jax.vjp(_gdn_apply_cols, qs, ks, vs, gcums, gtots, bcols, ss, ts)
        dq, dk, dv, dgc, dgt, db, ds, dt = vjp_apply(d_out)
        tts = [t.T for t in ts]
        dm = [_dot(tts[h], dt[h]) for h in heads]
        dm = [-_dot(dm[h], tts[h]) for h in heads]
        _, vjp_m = jax.vjp(_gdn_m_cols, ks, gcums, bcols)
        dk2, dgc2, db2 = vjp_m(dm)
        for h in heads:
            sl = _head_slices(h)[0]
            dx_ref[0, :, sl] = dq[h]
            dx_ref[1, :, sl] = dk[h] + dk2[h]
            dx_ref[2, :, sl] = dv[h]
            dstate[h] = ds[h]
        dgate_ref[...] = _gate_gradient([db[h] + db2[h] for h in heads], [dgc[h] + dgc2[h] for h in heads], dgt)

        if n:
            @pl.when(pl.program_id(0) == n_chunks - 1)
            def _():
                _scatter_finish(copies)

    dqkv, dgates, *landed = pl.pallas_call(
        body,
        name="gdn_chunk_bwd",
        grid=(n_chunks,),
        in_specs=[
            pl.BlockSpec((3, CHUNK, width), lambda c: (0, last - c, 0)),
            pl.BlockSpec((CHUNK, HEAD_DIM), lambda c: (last - c, 0)),
            pl.BlockSpec((1, HEADS, HEAD_DIM, HEAD_DIM), lambda c: (last - c, 0, 0, 0)),
            pl.BlockSpec((1, HEADS, CHUNK, CHUNK), lambda c: (last - c, 0, 0, 0)),
            pl.BlockSpec((CHUNK, width), lambda c: (last - c, 0)),
        ] + [ANY] * n,
        out_specs=[pl.BlockSpec((3, CHUNK, width), lambda c: (0, last - c, 0)),
                   pl.BlockSpec((CHUNK, HEAD_DIM), lambda c: (last - c, 0))] + [ANY] * n,
        out_shape=[jax.ShapeDtypeStruct(qkv.shape, F32), jax.ShapeDtypeStruct(gates.shape, F32)]
        + [jax.ShapeDtypeStruct(b.shape, b.dtype) for b in scatter],
        scratch_shapes=[pltpu.VMEM((HEADS, HEAD_DIM, HEAD_DIM), F32)] + (_scatter_sems(n) if n else []),
        compiler_params=_params(("arbitrary",)),
    )(qkv, gates, states, tinv, d_o, *scatter)
    return dqkv, dgates, _keep_own_slots(landed, scatter)


def mm_nn(a, b, *, tm, name):
    ks, m, tk = a.shape
    _, ns, _, tn = b.shape

    def body(a_ref, b_ref, o_ref):
        p = _dot(a_ref[...].astype(BF16), b_ref[...])

        @pl.when(pl.program_id(2) == 0)
        def _():
            o_ref[...] = p

        @pl.when(pl.program_id(2) > 0)
        def _():
            o_ref[...] += p

    return pl.pallas_call(
        body,
        name=name,
        grid=(ns, m // tm, ks),
        in_specs=[
            pl.BlockSpec((None, tm, tk), lambda n, i, k: (k, i, 0)),
            pl.BlockSpec((None, None, tk, tn), lambda n, i, k: (k, n, 0, 0)),
        ],
        out_specs=pl.BlockSpec((None, tm, tn), lambda n, i, k: (n, i, 0)),
        out_shape=jax.ShapeDtypeStruct((ns, m, tn), F32),
        compiler_params=_params(("arbitrary", "arbitrary", "arbitrary")),
    )(a, b)


def mm_nt(dy, w, *, tm, name, res=None, res_scale=1.0):
    ns, m, tn = dy.shape
    ks, _, tk, _ = w.shape

    def body(*refs):
        if res is None:
            dy_ref, w_ref, o_ref = refs
        else:
            dy_ref, w_ref, r_ref, o_ref = refs
        p = _dot_nt(dy_ref[...].astype(BF16), w_ref[...])

        @pl.when(pl.program_id(2) == 0)
        def _():
            o_ref[...] = p if res is None else p + res_scale * r_ref[...]

        @pl.when(pl.program_id(2) > 0)
        def _():
            o_ref[...] += p

    in_specs = [
        pl.BlockSpec((None, tm, tn), lambda k, i, n: (n, i, 0)),
        pl.BlockSpec((None, None, tk, tn), lambda k, i, n: (k, n, 0, 0)),
    ]
    args = [dy, w]
    if res is not None:
        in_specs.append(pl.BlockSpec((None, tm, tk), lambda k, i, n: (k, i, 0)))
        args.append(res)
    return pl.pallas_call(
        body,
        name=name,
        grid=(ks, m // tm, ns),
        in_specs=in_specs,
        out_specs=pl.BlockSpec((None, tm, tk), lambda k, i, n: (k, i, 0)),
        out_shape=jax.ShapeDtypeStruct((ks, m, tk), F32),
        compiler_params=_params(("arbitrary", "arbitrary", "arbitrary")),
    )(*args)


def mm_tn(x, dy, *, tm, name, rb=None):
    ks, m, tk = x.shape
    ns, _, tn = dy.shape
    rb = tk if rb is None else rb

    def body(x_ref, dy_ref, o_ref):
        @pl.when(pl.program_id(2) == 0)
        def _():
            o_ref[...] = jnp.zeros_like(o_ref)

        dyb = dy_ref[...].astype(BF16)
        for r in range(0, tk, rb):
            o_ref[r:r + rb, :] += _dot_tn(x_ref[:, r:r + rb].astype(BF16), dyb)

    return pl.pallas_call(
        body,
        name=name,
        grid=(ks, ns, m // tm),
        in_specs=[
            pl.BlockSpec((None, tm, tk), lambda k, n, i: (k, i, 0)),
            pl.BlockSpec((None, tm, tn), lambda k, n, i: (n, i, 0)),
        ],
        out_specs=pl.BlockSpec((None, None, tk, tn), lambda k, n, i: (k, n, 0, 0)),
        out_shape=jax.ShapeDtypeStruct((ks, ns, tk, tn), F32),
        compiler_params=_params(("arbitrary", "arbitrary", "arbitrary")),
    )(x, dy)


def _row_partial(x):
    rows, c = x.shape
    return jnp.sum(x.reshape(rows // 8, 8, c), axis=0)


def _layer_norm(r, g, b):
    mu = jnp.mean(r, axis=-1, keepdims=True)
    xc = r - mu
    var = jnp.mean(xc * xc, axis=-1, keepdims=True)
    return xc * lax.rsqrt(var + LN_EPS) * g + b


def _layer_norm_bwd(x, dh, g):
    mu = jnp.mean(x, axis=-1, keepdims=True)
    xc = x - mu
    rstd = lax.rsqrt(jnp.mean(xc * xc, axis=-1, keepdims=True) + LN_EPS)
    xh = xc * rstd
    dxh = dh * g
    m1 = jnp.mean(dxh, axis=-1, keepdims=True)
    m2 = jnp.mean(dxh * xh, axis=-1, keepdims=True)
    return rstd * (dxh - m1 - xh * m2), _row_partial(dh * xh), _row_partial(dh)


def mm_nn_ln(a, b, h_prev, g, beta, *, tm, name):
    ks, m, tk = a.shape
    d = b.shape[3]

    def body(a_ref, b_ref, hp_ref, g_ref, be_ref, r_ref, h_ref, hb_ref):
        p = _dot(a_ref[...].astype(BF16), b_ref[...])

        @pl.when(pl.program_id(1) == 0)
        def _():
            r_ref[...] = p

        @pl.when(pl.program_id(1) > 0)
        def _():
            r_ref[...] += p

        @pl.when(pl.program_id(1) == ks - 1)
        def _():
            r = ALPHA * hp_ref[...] + r_ref[...]
            r_ref[...] = r
            h = _layer_norm(r, g_ref[...], be_ref[...])
            h_ref[...] = h
            hb_ref[...] = h.astype(BF16)

    row = pl.BlockSpec((None, tm, d), lambda i, k: (0, i, 0))
    vec = pl.BlockSpec((1, d), lambda i, k: (0, 0))
    return pl.pallas_call(
        body,
        name=name,
        grid=(m // tm, ks),
        in_specs=[
            pl.BlockSpec((None, tm, tk), lambda i, k: (k, i, 0)),
            pl.BlockSpec((None, None, tk, d), lambda i, k: (k, 0, 0, 0)),
            row, vec, vec,
        ],
        out_specs=[row, row, row],
        out_shape=[jax.ShapeDtypeStruct((1, m, d), F32)] * 2 + [jax.ShapeDtypeStruct((1, m, d), BF16)],
        compiler_params=_params(("arbitrary", "arbitrary")),
    )(a, b, h_prev, g, beta)


def mm_nt_ln_bwd(dy, w, res, r, g, *, tm, name, swap=()):
    ns, m, tn = dy.shape
    d = w.shape[2]
    n_swap = len(swap)
    last_tile = m // tm - 1

    def body(dy_ref, w_ref, res_ref, r_ref, g_ref, *refs):
        leaving_refs, (dr_ref, dgb_ref), refs = refs[:n_swap], refs[n_swap:n_swap + 2], refs[n_swap + 2:]
        copies = _swap_copies(leaving_refs, refs[:n_swap], *refs[n_swap:]) if n_swap else None
        p = _dot_nt(dy_ref[...].astype(BF16), w_ref[...])

        @pl.when((pl.program_id(0) == 0) & (pl.program_id(1) == 0))
        def _():
            dgb_ref[...] = jnp.zeros_like(dgb_ref)
            if n_swap:
                _swap_start(copies)

        @pl.when(pl.program_id(1) == 0)
        def _():
            dr_ref[...] = p + ALPHA * res_ref[...]

        @pl.when(pl.program_id(1) > 0)
        def _():
            dr_ref[...] += p

        @pl.when(pl.program_id(1) == ns - 1)
        def _():
            for rows in (pl.ds(0, tm // 2), pl.ds(tm // 2, tm // 2)):
                dr, dgamma, dbeta = _layer_norm_bwd(r_ref[rows, :], dr_ref[rows, :], g_ref[...])
                dr_ref[rows, :] = dr
                dgb_ref[0] += dgamma
                dgb_ref[1] += dbeta

        if n_swap:
            @pl.when((pl.program_id(0) == last_tile) & (pl.program_id(1) == ns - 1))
            def _():
                _swap_finish(copies)

    row = pl.BlockSpec((None, tm, d), lambda i, n: (0, i, 0))
    dr, dgb, *landed = pl.pallas_call(
        body,
        name=name,
        grid=(m // tm, ns),
        in_specs=[
            pl.BlockSpec((None, tm, tn), lambda i, n: (n, i, 0)),
            pl.BlockSpec((None, None, d, tn), lambda i, n: (0, n, 0, 0)),
            row, row,
            pl.BlockSpec((1, d), lambda i, n: (0, 0)),
        ] + [ANY] * n_swap,
        out_specs=[row, pl.BlockSpec((2, 8, d), lambda i, n: (0, 0, 0))] + [ANY] * n_swap,
        out_shape=[jax.ShapeDtypeStruct((1, m, d), F32), jax.ShapeDtypeStruct((2, 8, d), F32)] + _swap_out_shapes(swap),
        scratch_shapes=_swap_sems(n_swap) if n_swap else [],
        compiler_params=_params(("arbitrary", "arbitrary")),
    )(dy, w, res, r, g, *swap)
    return dr, dgb, landed


def loss_ln_bwd(h, target, r, g, *, first, count, tm):
    _, lp, d = h.shape

    def body(h_ref, t_ref, r_ref, g_ref, dr_ref, dgb_ref, l_ref):
        row = pl.program_id(0) * tm + lax.broadcasted_iota(jnp.int32, (tm, d), 0)
        valid = (row >= first) & (row < first + count)
        err = jnp.where(valid, h_ref[...] - t_ref[...], 0.0)
        dr, dgamma, dbeta = _layer_norm_bwd(r_ref[...], err * (1.0 / d), g_ref[...])
        dr_ref[...] = dr

        @pl.when(pl.program_id(0) == 0)
        def _():
            dgb_ref[...] = jnp.zeros_like(dgb_ref)
            l_ref[...] = jnp.zeros_like(l_ref)

        dgb_ref[0] += dgamma
        dgb_ref[1] += dbeta
        l_ref[...] += _row_partial(err * err) * (0.5 / d)

    row3 = pl.BlockSpec((None, tm, d), lambda i: (0, i, 0))
    return pl.pallas_call(
        body,
        name="loss_ln4_bwd",
        grid=(lp // tm,),
        in_specs=[row3, pl.BlockSpec((tm, d), lambda i: (i, 0)), row3, pl.BlockSpec((1, d), lambda i: (0, 0))],
        out_specs=[row3, pl.BlockSpec((2, 8, d), lambda i: (0, 0, 0)), pl.BlockSpec((8, d), lambda i: (0, 0))],
        out_shape=[jax.ShapeDtypeStruct((1, lp, d), F32), jax.ShapeDtypeStruct((2, 8, d), F32),
                   jax.ShapeDtypeStruct((8, d), F32)],
        compiler_params=_params(("arbitrary",)),
    )(h, target, r, g)


def _halo_index(tile, tm):
    return jnp.maximum(tile * (tm // HALO) - 1, 0)


def _conv_fwd(xs_ref, w, taps, tm):
    acc = w(0) * xs_ref[pl.ds(HALO - taps + 1, tm), :]
    for j in range(1, taps):
        acc += w(j) * xs_ref[pl.ds(HALO - taps + 1 + j, tm), :]
    return acc


def _conv_bwd_x(dcs_ref, w, taps, tm):
    acc = w(0) * dcs_ref[pl.ds(taps - 1, tm), :]
    for j in range(1, taps):
        acc += w(j) * dcs_ref[pl.ds(taps - 1 - j, tm), :]
    return acc


SUB = 8
LANES = 128
PAIR = 2 * SUB
STRIP_UNROLL = 2


def _pair_rows(r0):
    return pl.ds(r0, SUB), pl.ds(r0 + SUB if isinstance(r0, int) else pl.multiple_of(r0 + SUB, SUB), SUB)


def _shift_down(cur, prev, s):
    if s == 0:
        return cur
    row = lax.broadcasted_iota(jnp.int32, cur.shape, 0)
    return jnp.where(row < s, pltpu.roll(prev, s, axis=0), pltpu.roll(cur, s, axis=0))


def _shift_up(cur, nxt, s):
    if s == 0:
        return cur
    row = lax.broadcasted_iota(jnp.int32, cur.shape, 0)
    return jnp.where(row < SUB - s, pltpu.roll(cur, SUB - s, axis=0), pltpu.roll(nxt, SUB - s, axis=0))


def _silu_parts(c):
    sg = _sigmoid(c)
    return c * sg, sg * (1.0 + c * (1.0 - sg))


def _head_sum(x):
    rows, c = x.shape
    parts = []
    for h in range(c // HEAD_DIM):
        s = jnp.sum(x[:, h * HEAD_DIM:(h + 1) * HEAD_DIM], axis=-1, keepdims=True)
        parts.append(jnp.broadcast_to(s, (rows, HEAD_DIM)))
    return parts[0] if len(parts) == 1 else jnp.concatenate(parts, axis=-1)


def _log1p(y):
    u = 1.0 + y
    d = u - 1.0
    return jnp.where(d == 0.0, y, jnp.log(u) * (y / jnp.where(d == 0.0, 1.0, d)))


def _softplus(x):
    return jnp.maximum(x, 0.0) + _log1p(jnp.exp(-jnp.abs(x)))


def _gate_values(x, al, dt):
    lane = lax.broadcasted_iota(jnp.int32, x.shape, 1)
    is_beta, is_g = lane < HEADS, (lane >= HEADS) & (lane < 2 * HEADS)
    return _sigmoid(x), -jnp.exp(al) * _softplus(x + dt), is_beta, is_g


def gdn_gates_fwd(pba, al, dt, *, tm):
    _, lp, width = pba.shape

    def body(x_ref, al_ref, dt_ref, o_ref):
        beta, g, is_beta, is_g = _gate_values(x_ref[...], al_ref[...], dt_ref[...])
        o_ref[...] = jnp.where(is_beta, beta, jnp.where(is_g, g, 0.0))

    vec = pl.BlockSpec((1, width), lambda i: (0, 0))
    return pl.pallas_call(
        body,
        name="gdn_gates_fwd",
        grid=(lp // tm,),
        in_specs=[pl.BlockSpec((None, tm, width), lambda i: (0, i, 0)), vec, vec],
        out_specs=pl.BlockSpec((tm, width), lambda i: (i, 0)),
        out_shape=jax.ShapeDtypeStruct((lp, width), F32),
        compiler_params=_params(("arbitrary",)),
    )(pba, al, dt)


def gdn_gates_bwd(pba, dgates, al, dt, *, tm):
    _, lp, width = pba.shape

    def body(x_ref, d_ref, al_ref, dt_ref, dx_ref, dsc_ref):
        x = x_ref[...]
        beta, g, is_beta, is_g = _gate_values(x, al_ref[...], dt_ref[...])
        d = d_ref[...]
        dg = jnp.where(is_g, d, 0.0)
        da = dg * -jnp.exp(al_ref[...]) * _sigmoid(x + dt_ref[...])
        dx_ref[...] = jnp.where(is_beta, d * beta * (1.0 - beta), da).astype(dx_ref.dtype)

        @pl.when(pl.program_id(0) == 0)
        def _():
            dsc_ref[...] = jnp.zeros_like(dsc_ref)

        dsc_ref[0] += _row_partial(dg * g)
        dsc_ref[1] += _row_partial(da)

    vec = pl.BlockSpec((1, width), lambda i: (0, 0))
    return pl.pallas_call(
        body,
        name="gdn_gates_bwd",
        grid=(lp // tm,),
        in_specs=[pl.BlockSpec((None, tm, width), lambda i: (0, i, 0)), pl.BlockSpec((tm, width), lambda i: (i, 0)), vec, vec],
        out_specs=[pl.BlockSpec((None, tm, width), lambda i: (0, i, 0)), pl.BlockSpec((2, SUB, width), lambda i: (0, 0, 0))],
        out_shape=[jax.ShapeDtypeStruct((1, lp, width), BF16), jax.ShapeDtypeStruct((2, SUB, width), F32)],
        compiler_params=_params(("arbitrary",)),
    )(pba, dgates, al, dt)


def gdn_pre_fwd(p3, conv_w, *, tm, cb):
    _, lp, width = p3.shape
    taps = conv_w.shape[1]

    def body(x_ref, halo_ref, w_ref, o_ref, xs):
        i = pl.program_id(1)
        for s in range(3):
            xs[s, 0:HALO, :] = jnp.where(i > 0, halo_ref[s], 0.0)
            xs[s, HALO:, :] = x_ref[s]
            c = _conv_fwd(xs.at[s], lambda j, s=s: w_ref[s, j:j + 1, :], taps, tm)
            y, _ = _silu_parts(c)
            if s < 2:
                y = y * lax.rsqrt(_head_sum(y * y) + L2_EPS)
                if s == 0:
                    y = y * Q_SCALE
            o_ref[s] = y

    return pl.pallas_call(
        body,
        name="gdn_pre_fwd",
        grid=(width // cb, lp // tm),
        in_specs=[
            pl.BlockSpec((3, tm, cb), lambda j, i: (0, i, j)),
            pl.BlockSpec((3, HALO, cb), lambda j, i: (0, _halo_index(i, tm), j)),
            pl.BlockSpec((3, taps, cb), lambda j, i: (0, 0, j)),
        ],
        out_specs=pl.BlockSpec((3, tm, cb), lambda j, i: (0, i, j)),
        out_shape=jax.ShapeDtypeStruct((3, lp, width), F32),
        scratch_shapes=[pltpu.VMEM((3, tm + HALO, cb), F32)],
        compiler_params=_params(("arbitrary", "arbitrary")),
    )(p3, p3, conv_w)


def gdn_pre_bwd(p3, dqkv, conv_w, *, tm, cb):
    _, lp, width = p3.shape
    taps = conv_w.shape[1]
    last = lp // tm - 1

    def body(x_ref, halo_ref, d_ref, w_ref, dx_ref, dw_ref, xs, dcs, carry):
        step = pl.program_id(1)
        tile = last - step

        @pl.when(step == 0)
        def _():
            carry[...] = jnp.zeros_like(carry)
            dw_ref[...] = jnp.zeros_like(dw_ref)

        for s in range(3):
            w = lambda j, s=s: w_ref[s, j:j + 1, :]
            xs[s, 0:HALO, :] = jnp.where(tile > 0, halo_ref[s], 0.0)
            xs[s, HALO:, :] = x_ref[s]
            c = _conv_fwd(xs.at[s], w, taps, tm)
            y, dsilu = _silu_parts(c)
            dy = d_ref[s]
            if s < 2:
                rn = lax.rsqrt(_head_sum(y * y) + L2_EPS)
                yn = y * rn
                if s == 0:
                    dy = dy * Q_SCALE
                dy = rn * (dy - yn * _head_sum(dy * yn))
            dc = dy * dsilu
            dcs[s, 0:tm, :] = dc
            dcs[s, tm:, :] = carry[s]
            dx_ref[s] = _conv_bwd_x(dcs.at[s], w, taps, tm).astype(dx_ref.dtype)
            carry[s] = dc[0:HALO, :]
            for j in range(taps):
                dw_ref[s, j] += _row_partial(dc * xs[s, pl.ds(HALO - taps + 1 + j, tm), :])

    tile_spec = pl.BlockSpec((3, tm, cb), lambda j, i: (0, last - i, j))
    return pl.pallas_call(
        body,
        name="gdn_pre_bwd",
        grid=(width // cb, lp // tm),
        in_specs=[
            tile_spec,
            pl.BlockSpec((3, HALO, cb), lambda j, i: (0, _halo_index(last - i, tm), j)),
            tile_spec,
            pl.BlockSpec((3, taps, cb), lambda j, i: (0, 0, j)),
        ],
        out_specs=[tile_spec, pl.BlockSpec((3, taps, SUB, cb), lambda j, i: (0, 0, 0, j))],
        out_shape=[jax.ShapeDtypeStruct((3, lp, width), BF16), jax.ShapeDtypeStruct((3, taps, SUB, width), F32)],
        scratch_shapes=[
            pltpu.VMEM((3, tm + HALO, cb), F32),
            pltpu.VMEM((3, tm + HALO, cb), F32),
            pltpu.VMEM((3, HALO, cb), F32),
        ],
        compiler_params=_params(("arbitrary", "arbitrary")),
    )(p3, p3, dqkv, conv_w)


def gdn_post_fwd(o, z, nw_b, *, tm):
    _, lp, width = o.shape

    def body(o_ref, z_ref, nw_ref, y_ref):
        ov = o_ref[...]
        rn = lax.rsqrt(_head_sum(ov * ov) * (1.0 / HEAD_DIM) + RMS_EPS)
        gate, _ = _silu_parts(z_ref[...])
        y_ref[...] = (ov * rn * nw_ref[...] * gate).astype(y_ref.dtype)

    row = pl.BlockSpec((None, tm, width), lambda i: (0, i, 0))
    return pl.pallas_call(
        body,
        name="gdn_post_fwd",
        grid=(lp // tm,),
        in_specs=[row, row, pl.BlockSpec((1, width), lambda i: (0, 0))],
        out_specs=row,
        out_shape=jax.ShapeDtypeStruct((1, lp, width), BF16),
        compiler_params=_params(("arbitrary",)),
    )(o, z, nw_b)


def gdn_post_bwd(o, z, dy, nw_b, *, tm):
    _, lp, width = o.shape

    def body(o_ref, z_ref, dy_ref, nw_ref, do_ref, dz_ref, dnw_ref):
        ov = o_ref[...]
        rn = lax.rsqrt(_head_sum(ov * ov) * (1.0 / HEAD_DIM) + RMS_EPS)
        yn = ov * rn
        gate, dgate = _silu_parts(z_ref[...])
        d_on = dy_ref[...] * gate
        dz_ref[...] = (dy_ref[...] * yn * nw_ref[...] * dgate).astype(dz_ref.dtype)
        a = d_on * nw_ref[...]
        do_ref[...] = rn * (a - yn * (_head_sum(a * yn) * (1.0 / HEAD_DIM)))

        @pl.when(pl.program_id(0) == 0)
        def _():
            dnw_ref[...] = jnp.zeros_like(dnw_ref)

        dnw_ref[...] += _row_partial(d_on * yn)

    row = pl.BlockSpec((None, tm, width), lambda i: (0, i, 0))
    return pl.pallas_call(
        body,
        name="gdn_post_bwd",
        grid=(lp // tm,),
        in_specs=[row, row, row, pl.BlockSpec((1, width), lambda i: (0, 0))],
        out_specs=[row, row, pl.BlockSpec((8, width), lambda i: (0, 0))],
        out_shape=[jax.ShapeDtypeStruct((1, lp, width), F32), jax.ShapeDtypeStruct((1, lp, width), BF16),
                   jax.ShapeDtypeStruct((8, width), F32)],
        compiler_params=_params(("arbitrary",)),
    )(o, z, dy, nw_b)


def ffn_act_fwd(up, conv_w, *, tm, name):
    _, lp, c_w = up.shape
    taps = conv_w.shape[1]

    def body(u_ref, halo_ref, g_ref, w_ref, o_ref):
        first_tile = pl.program_id(1) == 0

        def strip(cur, prev, rows, cs):
            conv = w_ref[taps - 1:taps, cs] * cur
            for j in range(taps - 1):
                conv += w_ref[j:j + 1, cs] * _shift_down(cur, prev, taps - 1 - j)
            y, _ = _silu_parts(conv)
            return y * g_ref[rows, cs]

        def pair(r0, above_of):
            top, bot = _pair_rows(r0)
            for c0 in range(0, c_w, LANES):
                cs = slice(c0, c0 + LANES)
                cur_t, cur_b = u_ref[top, cs], u_ref[bot, cs]
                out = [strip(cur_t, above_of(cs), top, cs), strip(cur_b, cur_t, bot, cs)]
                o_ref[pl.ds(r0, PAIR), cs] = jnp.concatenate(out, axis=0).astype(o_ref.dtype)

        pair(0, lambda cs: jnp.where(first_tile, 0.0, halo_ref[:, cs]))

        def loop_body(s, carry):
            r0 = pl.multiple_of(s * PAIR, PAIR)
            pair(r0, lambda cs: u_ref[pl.ds(pl.multiple_of(r0 - SUB, SUB), SUB), cs])
            return carry

        lax.fori_loop(1, tm // PAIR, loop_body, 0, unroll=STRIP_UNROLL)

    return pl.pallas_call(
        body,
        name=name,
        grid=(2, lp // tm),
        in_specs=[
            pl.BlockSpec((None, tm, c_w), lambda s, i: (s, i, 0)),
            pl.BlockSpec((None, HALO, c_w), lambda s, i: (s, _halo_index(i, tm), 0)),
            pl.BlockSpec((None, tm, c_w), lambda s, i: (2 + s, i, 0)),
            pl.BlockSpec((None, taps, c_w), lambda s, i: (s, 0, 0)),
        ],
        out_specs=pl.BlockSpec((None, tm, c_w), lambda s, i: (s, i, 0)),
        out_shape=jax.ShapeDtypeStruct((2, lp, c_w), BF16),
        compiler_params=_params(("arbitrary", "arbitrary")),
    )(up, up, up, conv_w)


def ffn_act_bwd(up, dact, conv_w, *, tm, name):
    _, lp, c_w = up.shape
    taps = conv_w.shape[1]
    last = lp // tm - 1
    n_pairs = tm // PAIR

    def body(u_ref, halo_ref, g_ref, d_ref, w_ref, dup_ref, dw_ref, below):
        step = pl.program_id(1)
        first_tile = step == last

        @pl.when(step == 0)
        def _():
            below[...] = jnp.zeros_like(below)
            dw_ref[...] = jnp.zeros_like(dw_ref)

        def strip(cur, prev, rows, cs, nxt):
            shifted = [_shift_down(cur, prev, taps - 1 - j) for j in range(taps)]
            conv = w_ref[0:1, cs] * shifted[0]
            for j in range(1, taps):
                conv += w_ref[j:j + 1, cs] * shifted[j]
            y, dsilu = _silu_parts(conv)
            d = d_ref[rows, cs]
            dc = d * g_ref[rows, cs] * dsilu
            dx = w_ref[taps - 1:taps, cs] * dc
            for j in range(taps - 1):
                dx += w_ref[j:j + 1, cs] * _shift_up(dc, nxt, taps - 1 - j)
            return dx, d * y, dc, [dc * s for s in shifted]

        def pair(r0, above_of):
            top, bot = _pair_rows(r0)
            both = pl.ds(r0, PAIR)
            for c0 in range(0, c_w, LANES):
                cs = slice(c0, c0 + LANES)
                cur_t, cur_b = u_ref[top, cs], u_ref[bot, cs]
                dx_b, dg_b, dc_b, dw_b = strip(cur_b, cur_t, bot, cs, below[:, cs])
                dx_t, dg_t, dc_t, dw_t = strip(cur_t, above_of(cs), top, cs, dc_b)
                below[:, cs] = dc_t
                dup_ref[0, both, cs] = jnp.concatenate([dx_t, dx_b], axis=0).astype(dup_ref.dtype)
                dup_ref[1, both, cs] = jnp.concatenate([dg_t, dg_b], axis=0).astype(dup_ref.dtype)
                for j in range(taps):
                    dw_ref[j, :, cs] += dw_t[j] + dw_b[j]

        def loop_body(it, carry):
            r0 = pl.multiple_of((n_pairs - 1 - it) * PAIR, PAIR)
            pair(r0, lambda cs: u_ref[pl.ds(pl.multiple_of(r0 - SUB, SUB), SUB), cs])
            return carry

        lax.fori_loop(0, n_pairs - 1, loop_body, 0, unroll=STRIP_UNROLL)
        pair(0, lambda cs: jnp.where(first_tile, 0.0, halo_ref[:, cs]))

    return pl.pallas_call(
        body,
        name=name,
        grid=(2, lp // tm),
        in_specs=[
            pl.BlockSpec((None, tm, c_w), lambda s, i: (s, last - i, 0)),
            pl.BlockSpec((None, HALO, c_w), lambda s, i: (s, _halo_index(last - i, tm), 0)),
            pl.BlockSpec((None, tm, c_w), lambda s, i: (2 + s, last - i, 0)),
            pl.BlockSpec((None, tm, c_w), lambda s, i: (s, last - i, 0)),
            pl.BlockSpec((None, taps, c_w), lambda s, i: (s, 0, 0)),
        ],
        out_specs=[
            pl.BlockSpec((2, None, tm, c_w), lambda s, i: (0, s, last - i, 0)),
            pl.BlockSpec((None, taps, SUB, c_w), lambda s, i: (s, 0, 0, 0)),
        ],
        out_shape=[jax.ShapeDtypeStruct((2, 2, lp, c_w), BF16), jax.ShapeDtypeStruct((2, taps, SUB, c_w), F32)],
        scratch_shapes=[pltpu.VMEM((SUB, c_w), F32)],
        compiler_params=_params(("arbitrary", "arbitrary")),
    )(up, up, up, dact, conv_w)


def sc_fwd(pb, conv_w, *, tm, cb):
    _, lp, width = pb.shape
    taps = conv_w.shape[0]

    def body(x_ref, halo_ref, w_ref, o_ref):
        first_tile = pl.program_id(1) == 0

        def strip(cur, prev, rows, cs):
            conv = w_ref[taps - 1:taps, cs] * cur
            for j in range(taps - 1):
                conv += w_ref[j:j + 1, cs] * _shift_down(cur, prev, taps - 1 - j)
            return x_ref[0, rows, cs] * conv

        def pair(r0, above_of):
            top, bot = _pair_rows(r0)
            for c0 in range(0, cb, LANES):
                cs = slice(c0, c0 + LANES)
                cur_t = x_ref[1, top, cs] * x_ref[2, top, cs]
                cur_b = x_ref[1, bot, cs] * x_ref[2, bot, cs]
                out = [strip(cur_t, above_of(cs), top, cs), strip(cur_b, cur_t, bot, cs)]
                o_ref[pl.ds(r0, PAIR), cs] = jnp.concatenate(out, axis=0).astype(o_ref.dtype)

        pair(0, lambda cs: jnp.where(first_tile, 0.0, halo_ref[1, :, cs] * halo_ref[2, :, cs]))

        def loop_body(k, carry):
            r0 = pl.multiple_of(k * PAIR, PAIR)
            before = pl.ds(pl.multiple_of(r0 - SUB, SUB), SUB)
            pair(r0, lambda cs: x_ref[1, before, cs] * x_ref[2, before, cs])
            return carry

        lax.fori_loop(1, tm // PAIR, loop_body, 0, unroll=STRIP_UNROLL)

    return pl.pallas_call(
        body,
        name="sc_fwd",
        grid=(width // cb, lp // tm),
        in_specs=[
            pl.BlockSpec((3, tm, cb), lambda j, i: (0, i, j)),
            pl.BlockSpec((3, HALO, cb), lambda j, i: (0, _halo_index(i, tm), j)),
            pl.BlockSpec((taps, cb), lambda j, i: (0, j)),
        ],
        out_specs=pl.BlockSpec((None, tm, cb), lambda j, i: (0, i, j)),
        out_shape=jax.ShapeDtypeStruct((1, lp, width), BF16),
        compiler_params=_params(("arbitrary", "arbitrary")),
    )(pb, pb, conv_w)


def sc_bwd(pb, ds, conv_w, *, tm, cb):
    _, lp, width = pb.shape
    taps = conv_w.shape[0]
    last = lp // tm - 1
    n_pairs = tm // PAIR

    def body(x_ref, halo_ref, d_ref, w_ref, dx_ref, dw_ref, below):
        step = pl.program_id(1)
        first_tile = step == last

        @pl.when(step == 0)
        def _():
            below[...] = jnp.zeros_like(below)
            dw_ref[...] = jnp.zeros_like(dw_ref)

        def strip(cur, prev, rows, cs, nxt):
            gate, left, right = x_ref[0, rows, cs], x_ref[1, rows, cs], x_ref[2, rows, cs]
            shifted = [_shift_down(cur, prev, taps - 1 - j) for j in range(taps)]
            conv = w_ref[0:1, cs] * shifted[0]
            for j in range(1, taps):
                conv += w_ref[j:j + 1, cs] * shifted[j]
            d = d_ref[rows, cs]
            dc = d * gate
            dp = w_ref[taps - 1:taps, cs] * dc
            for j in range(taps - 1):
                dp += w_ref[j:j + 1, cs] * _shift_up(dc, nxt, taps - 1 - j)
            return d * conv, dp * right, dp * left, dc, [dc * s for s in shifted]

        def pair(r0, above_of):
            top, bot = _pair_rows(r0)
            both = pl.ds(r0, PAIR)
            for c0 in range(0, cb, LANES):
                cs = slice(c0, c0 + LANES)
                cur_t = x_ref[1, top, cs] * x_ref[2, top, cs]
                cur_b = x_ref[1, bot, cs] * x_ref[2, bot, cs]
                *dx_b, dc_b, dw_b = strip(cur_b, cur_t, bot, cs, below[:, cs])
                *dx_t, dc_t, dw_t = strip(cur_t, above_of(cs), top, cs, dc_b)
                below[:, cs] = dc_t
                for s in range(3):
                    dx_ref[s, both, cs] = jnp.concatenate([dx_t[s], dx_b[s]], axis=0).astype(dx_ref.dtype)
                for j in range(taps):
                    dw_ref[j, :, cs] += dw_t[j] + dw_b[j]

        def loop_body(it, carry):
            r0 = pl.multiple_of((n_pairs - 1 - it) * PAIR, PAIR)
            before = pl.ds(pl.multiple_of(r0 - SUB, SUB), SUB)
            pair(r0, lambda cs: x_ref[1, before, cs] * x_ref[2, before, cs])
            return carry

        lax.fori_loop(0, n_pairs - 1, loop_body, 0, unroll=STRIP_UNROLL)
        pair(0, lambda cs: jnp.where(first_tile, 0.0, halo_ref[1, :, cs] * halo_ref[2, :, cs]))

    tile_spec = pl.BlockSpec((3, tm, cb), lambda j, i: (0, last - i, j))
    return pl.pallas_call(
        body,
        name="sc_bwd",
        grid=(width // cb, lp // tm),
        in_specs=[
            tile_spec,
            pl.BlockSpec((3, HALO, cb), lambda j, i: (0, _halo_index(last - i, tm), j)),
            pl.BlockSpec((None, tm, cb), lambda j, i: (0, last - i, j)),
            pl.BlockSpec((taps, cb), lambda j, i: (0, j)),
        ],
        out_specs=[tile_spec, pl.BlockSpec((taps, SUB, cb), lambda j, i: (0, 0, j))],
        out_shape=[jax.ShapeDtypeStruct((3, lp, width), BF16), jax.ShapeDtypeStruct((taps, SUB, width), F32)],
        scratch_shapes=[pltpu.VMEM((SUB, cb), F32)],
        compiler_params=_params(("arbitrary", "arbitrary")),
    )(pb, pb, ds, conv_w)


TILE_BYTES = 1536 * 1024


def _rows_tile(rows, cols, multiple=8):
    if rows * cols * 4 <= TILE_BYTES or rows % multiple:
        return rows
    best = multiple
    for t in range(multiple, rows + 1, multiple):
        if rows % t == 0 and t * cols * 4 <= TILE_BYTES:
            best = t
    return best


def pair_sum(g, landed, core, out_dtype, name):
    _, rows, cols = g.shape
    half = rows // 2
    tr = _rows_tile(half, cols, 16)
    nb = half // tr

    def body(c_ref, g_ref, l_ref, o_ref):
        o_ref[...] = (g_ref[...] + l_ref[...]).astype(out_dtype)

    return pl.pallas_call(
        body,
        name=name,
        grid_spec=pltpu.PrefetchScalarGridSpec(
            num_scalar_prefetch=1,
            grid=(4, nb),
            in_specs=[
                pl.BlockSpec((None, tr, cols), lambda s, i, c: (s, c[0] * nb + i, 0)),
                pl.BlockSpec((None, tr, cols), lambda s, i, c: (s, i, 0)),
            ],
            out_specs=pl.BlockSpec((None, tr, cols), lambda s, i, c: (s, i, 0)),
        ),
        out_shape=jax.ShapeDtypeStruct((4, half, cols), out_dtype),
        compiler_params=_params(("arbitrary", "arbitrary")),
    )(core, g, landed)


def chip_sum(x, name):
    _, rows, cols = x.shape
    tr = _rows_tile(rows, cols, 16)

    def body(x0, x1, x2, x3, o_ref):
        acc = x0[...].astype(F32) + x1[...].astype(F32)
        o_ref[...] = (acc + x2[...].astype(F32)) + x3[...].astype(F32)

    return pl.pallas_call(
        body,
        name=name,
        grid=(rows // tr,),
        in_specs=[pl.BlockSpec((None, tr, cols), lambda i, k=k: (k, i, 0)) for k in range(4)],
        out_specs=pl.BlockSpec((tr, cols), lambda i: (i, 0)),
        out_shape=jax.ShapeDtypeStruct((rows, cols), F32),
        compiler_params=_params(("arbitrary",)),
    )(x, x, x, x)


def adamw(w, g, m, v, name):
    shape = w.shape
    cols = shape[-1]
    rows = w.size // cols
    tr = _rows_tile(rows, cols)

    def body(w_ref, g_ref, m_ref, v_ref, d_ref, m2_ref, v2_ref):
        gv = g_ref[...]
        m2 = ADAM_B1 * m_ref[...] + (1.0 - ADAM_B1) * gv
        v2 = ADAM_B2 * v_ref[...] + (1.0 - ADAM_B2) * (gv * gv)
        m_hat = m2 / (1.0 - ADAM_B1 ** ADAM_STEP)
        v_hat = v2 / (1.0 - ADAM_B2 ** ADAM_STEP)
        d_ref[...] = -ADAM_LR * (m_hat / (jnp.sqrt(v_hat) + ADAM_EPS) + ADAM_WD * w_ref[...])
        m2_ref[...] = m2
        v2_ref[...] = v2

    spec = pl.BlockSpec((tr, cols), lambda i: (i, 0))
    outs = pl.pallas_call(
        body,
        name=name,
        grid=(rows // tr,),
        in_specs=[spec] * 4,
        out_specs=[spec] * 3,
        out_shape=[jax.ShapeDtypeStruct((rows, cols), F32)] * 3,
        compiler_params=_params(("arbitrary",)),
    )(*[t.reshape(rows, cols) for t in (w, g, m, v)])
    return tuple(o.reshape(shape) for o in outs)


MESH_ID = pl.DeviceIdType.MESH
ANY = pl.BlockSpec(memory_space=pl.ANY)


def _place():
    x, y, c = lax.axis_index("x"), lax.axis_index("y"), lax.axis_index("c")
    other_chips = [(1 - x, y), (x, 1 - y), (1 - x, 1 - y)]
    return x, y, c, other_chips


def all_gather_shards(bufs, name):
    n = len(bufs)

    def body(*refs):
        x_refs, o_refs = refs[:n], refs[n:2 * n]
        copies = _gather_copies(x_refs, o_refs, *refs[2 * n:])
        _gather_start(copies)
        _gather_finish(copies)

    outs = pl.pallas_call(
        body,
        name=name,
        in_specs=[ANY] * n,
        out_specs=[ANY] * n,
        out_shape=_gather_out_shapes(bufs),
        scratch_shapes=_gather_sems(n),
    )(*bufs)
    return _set_own_slots(outs, bufs)


def _gather_out_shapes(bufs):
    return [jax.ShapeDtypeStruct((4,) + b.shape, b.dtype) for b in bufs]


def _gather_sems(n):
    return [pltpu.SemaphoreType.DMA((6 * n,)), pltpu.SemaphoreType.DMA((6 * n,))]


def _set_own_slots(outs, bufs):
    if not outs:
        return []
    me = 2 * lax.axis_index("x") + lax.axis_index("y")
    return [lax.dynamic_update_index_in_dim(o, b, me, 0) for o, b in zip(outs, bufs)]


def _gather_copies(x_refs, o_refs, send_sems, recv_sems):
    x, y, c, chips = _place()
    me = 2 * x + y
    sibling = (x, y, 1 - c)

    def part(a, slot, hf):
        half = x_refs[a].shape[0] // 2
        return o_refs[a].at[slot, pl.ds(hf * half, half), :]

    def mine(a):
        half = x_refs[a].shape[0] // 2
        return x_refs[a].at[pl.ds(c * half, half), :]

    def copy(k, src, dst, to):
        return pltpu.make_async_remote_copy(src_ref=src, dst_ref=dst, send_sem=send_sems.at[k],
                                            recv_sem=recv_sems.at[k], device_id=to, device_id_type=MESH_ID)

    sends, arrivals, passes, passed = [], [], [], []
    for a in range(len(x_refs)):
        for j, (px, py) in enumerate(chips):
            landed, theirs = part(a, 2 * px + py, c), part(a, 2 * px + py, 1 - c)
            sends.append(copy(6 * a + j, mine(a), part(a, me, c), (px, py, c)))
            arrivals.append(copy(6 * a + j, mine(a), landed, (px, py, c)))
            passes.append(copy(6 * a + 3 + j, landed, landed, sibling))
            passed.append(copy(6 * a + 3 + j, theirs, theirs, sibling))
    return sends, arrivals, passes, passed


def _gather_start(copies):
    for cp in copies[0]:
        cp.start()


def _gather_finish(copies):
    sends, arrivals, passes, passed = copies
    for arrival, cp in zip(arrivals, passes):
        arrival.wait_recv()
        cp.start()
    for cp in passed:
        cp.wait_recv()
    for cp in sends + passes:
        cp.wait_send()


def swap_halves(bufs, name):
    n = len(bufs)

    def body(*refs):
        copies = _swap_copies(refs[:n], refs[n:2 * n], *refs[2 * n:])
        _swap_start(copies)
        _swap_finish(copies)

    return pl.pallas_call(
        body,
        name=name,
        in_specs=[ANY] * n,
        out_specs=[ANY] * n,
        out_shape=_swap_out_shapes(bufs),
        scratch_shapes=_swap_sems(n),
    )(*bufs)


def _swap_out_shapes(bufs):
    return [jax.ShapeDtypeStruct((4, b.shape[1] // 2, b.shape[2]), b.dtype) for b in bufs]


def _swap_sems(n):
    return [pltpu.SemaphoreType.DMA((n,)), pltpu.SemaphoreType.DMA((n,))]


def _swap_copies(x_refs, o_refs, send_sems, recv_sems):
    x, y, c, _ = _place()
    copies = []
    for a, (x_ref, o_ref) in enumerate(zip(x_refs, o_refs)):
        half = x_ref.shape[1] // 2
        copies.append(pltpu.make_async_remote_copy(src_ref=x_ref.at[:, pl.ds((1 - c) * half, half), :], dst_ref=o_ref,
                                                   send_sem=send_sems.at[a], recv_sem=recv_sems.at[a],
                                                   device_id=(x, y, 1 - c), device_id_type=MESH_ID))
    return copies


def _swap_start(copies):
    for cp in copies:
        cp.start()


def _swap_finish(copies):
    for cp in copies:
        cp.wait()


def scatter_to_chips(bufs, name):
    n = len(bufs)

    def body(*refs):
        x_refs, o_refs = refs[:n], refs[n:2 * n]
        copies = _scatter_copies(x_refs, o_refs, *refs[2 * n:])
        _scatter_start(copies)
        _scatter_finish(copies)

    outs = pl.pallas_call(
        body,
        name=name,
        in_specs=[ANY] * n,
        out_specs=[ANY] * n,
        out_shape=[jax.ShapeDtypeStruct(b.shape, b.dtype) for b in bufs],
        scratch_shapes=_scatter_sems(n),
    )(*bufs)
    return _keep_own_slots(outs, bufs)


def _scatter_sems(n):
    return [pltpu.SemaphoreType.DMA((3 * n,)), pltpu.SemaphoreType.DMA((3 * n,))]


def _keep_own_slots(outs, bufs):
    if not outs:
        return []
    me = 2 * lax.axis_index("x") + lax.axis_index("y")
    return [lax.dynamic_update_index_in_dim(o, lax.dynamic_index_in_dim(b, me, 0, keepdims=False), me, 0)
            for o, b in zip(outs, bufs)]


def _scatter_copies(x_refs, o_refs, send_sems, recv_sems):
    x, y, c, chips = _place()
    me = 2 * x + y

    def copy(a, j, src_slot, dst_slot, px, py):
        return pltpu.make_async_remote_copy(src_ref=x_refs[a].at[src_slot], dst_ref=o_refs[a].at[dst_slot],
                                            send_sem=send_sems.at[3 * a + j], recv_sem=recv_sems.at[3 * a + j],
                                            device_id=(px, py, c), device_id_type=MESH_ID)

    sends = [copy(a, j, 2 * px + py, me, px, py) for a in range(len(x_refs)) for j, (px, py) in enumerate(chips)]
    arrivals = [copy(a, j, me, 2 * px + py, px, py) for a in range(len(x_refs)) for j, (px, py) in enumerate(chips)]
    return sends, arrivals


def _scatter_start(copies):
    for cp in copies[0]:
        cp.start()


def _scatter_finish(copies):
    for cp in copies[1]:
        cp.wait_recv()
    for cp in copies[0]:
        cp.wait_send()


def share_halves(groups, name):
    bufs = [b for grp in groups for b in grp]
    where = [(gi, li) for gi, grp in enumerate(groups) for li in range(len(grp))]
    n = len(bufs)

    def body(*refs):
        x_refs, o_refs = refs[:n], refs[n:n + len(groups)]
        send_sems, recv_sems = refs[n + len(groups):]
        x, y, c, _ = _place()
        sent, arrive = [], []
        for a, (gi, li) in enumerate(where):

            def copy(hf, a=a, gi=gi, li=li):
                return pltpu.make_async_remote_copy(src_ref=x_refs[a], dst_ref=o_refs[gi].at[li, hf],
                                                    send_sem=send_sems.at[a], recv_sem=recv_sems.at[a],
                                                    device_id=(x, y, 1 - c), device_id_type=MESH_ID)

            sent.append(copy(c))
            arrive.append(copy(1 - c))
        for cp in sent:
            cp.start()
        for cp in arrive:
            cp.wait_recv()
        for cp in sent:
            cp.wait_send()

    outs = pl.pallas_call(
        body,
        name=name,
        in_specs=[ANY] * n,
        out_specs=[ANY] * len(groups),
        out_shape=[jax.ShapeDtypeStruct((len(grp), 2) + grp[0].shape, grp[0].dtype) for grp in groups],
        scratch_shapes=[pltpu.SemaphoreType.DMA((n,)), pltpu.SemaphoreType.DMA((n,))],
    )(*bufs)
    c = lax.axis_index("c")
    full = [lax.dynamic_update_index_in_dim(o, jnp.stack(grp), c, 1) for o, grp in zip(outs, groups)]
    return [t.reshape(t.shape[0], 2 * t.shape[2], t.shape[3]) for t in full]


def pair_sums(bufs, landed, dtypes, tag):
    core = lax.axis_index("c").astype(jnp.int32).reshape(1)
    return [pair_sum(b, l, core, dt, "rs_pair_sum_%s%d" % (tag, i)) for i, (b, l, dt) in enumerate(zip(bufs, landed, dtypes))]


def _row_tiles(length):
    return (640, 832, 160) if length > 2048 else (128, 64, 64)


def _divisor_tile(rows, target):
    return max(t for t in range(8, min(rows, target) + 1, 8) if rows % t == 0)


def _local_step(x, target, wt, late_shards, layout_late, complete_grads, sum_pairs):
    seq, d = x.shape
    length = N_META + seq
    tm, tm_ffn, tm_pre = _row_tiles(length)
    lp = -(-length // tm) * tm
    tail = jnp.zeros((lp - length, d), F32)
    h0 = jnp.concatenate([wt["meta"], x, tail], axis=0)[None]
    tgt = jnp.concatenate([jnp.zeros((N_META, d), F32), target, tail], axis=0)
    nn = functools.partial(mm_nn, tm=_divisor_tile(lp, 1664))
    nt = functools.partial(mm_nt, tm=_divisor_tile(lp, 1040))
    tn = functools.partial(mm_tn, tm=_divisor_tile(lp, 1664), rb=256)
    nn_ln = functools.partial(mm_nn_ln, tm=_divisor_tile(lp, 832))
    nt_ln_bwd = functools.partial(mm_nt_ln_bwd, tm=_divisor_tile(lp, 1040))
    ln_g = [wt["ln_mix_g"][0:1], wt["ln_ffn_g"][0:1], wt["ln_mix_g"][1:2], wt["ln_ffn_g"][1:2]]
    ln_b = [wt["ln_mix_b"][0:1], wt["ln_ffn_b"][0:1], wt["ln_mix_b"][1:2], wt["ln_ffn_b"][1:2]]

    h0b = h0.astype(BF16)
    p3 = nn(h0b, wt["a3"], name="a_in3")
    pz = nn(h0b, wt["az"], name="a_inz")
    pba = nn(h0b, wt["a_ba"], name="a_inba")
    qkv = gdn_pre_fwd(p3, wt["a_conv3"], tm=tm_pre, cb=HEADS * HEAD_DIM)
    gates = gdn_gates_fwd(pba, wt["alog_lanes"], wt["dtb_lanes"], tm=tm)
    o, states, tinv, late_stacks = gdn_chunk_fwd(qkv, gates, late_shards)
    wt = {**wt, **layout_late(late_stacks)}
    onz = gdn_post_fwd(o[None], pz, wt["anorm_b"], tm=tm)
    r1, h1, h1b = nn_ln(onz, wt["a_out"], h0, ln_g[0], ln_b[0], name="a_out_ln1")
    up0 = nn(h1b, wt["up"][0], name="up0")
    act0 = ffn_act_fwd(up0, wt["fconv"][0], tm=tm_ffn, name="ffn_act0")
    r2, h2, h2b = nn_ln(act0, wt["down"][0], h1, ln_g[1], ln_b[1], name="down0_ln2")
    pb = nn(h2b, wt["b_in"], name="b_in")
    sc = sc_fwd(pb, wt["b_conv"], tm=tm_ffn, cb=d)
    r3, h3, h3b = nn_ln(sc, wt["b_out"], h2, ln_g[2], ln_b[2], name="b_out_ln3")
    up1 = nn(h3b, wt["up"][1], name="up1")
    act1 = ffn_act_fwd(up1, wt["fconv"][1], tm=tm_ffn, name="ffn_act1")
    r4, h4, _ = nn_ln(act1, wt["down"][1], h3, ln_g[3], ln_b[3], name="down1_ln4")

    grads = {}
    dr4, dgb4, loss_part = loss_ln_bwd(h4, tgt, r4, ln_g[3], first=N_META, count=seq, tm=tm)
    d_down1 = tn(act1, dr4, name="d_down1")
    dact1 = nt(dr4, wt["down"][1], name="d_act1")
    dup1, dfconv1 = ffn_act_bwd(up1, dact1, wt["fconv"][1], tm=tm_ffn, name="ffn_act1_bwd")
    dup1 = dup1.reshape(up1.shape)
    d_up1 = tn(h3b, dup1, name="d_up1")

    dr3, dgb3, _ = nt_ln_bwd(dup1, wt["up"][1], dr4, r3, ln_g[2], name="d_h3_ln3")
    d_bout = tn(sc, dr3, name="d_b_out")
    dsc = nt(dr3, wt["b_out"], name="d_sc")
    dpb, dbconv = sc_bwd(pb, dsc, wt["b_conv"], tm=tm_ffn, cb=d)
    d_bin = tn(h2b, dpb, name="d_b_in")

    dr2, dgb2, _ = nt_ln_bwd(dpb, wt["b_in"], dr3, r2, ln_g[1], name="d_h2_ln2")
    d_down0 = tn(act0, dr2, name="d_down0")
    dact0 = nt(dr2, wt["down"][0], name="d_act0")
    dup0, dfconv0 = ffn_act_bwd(up0, dact0, wt["fconv"][0], tm=tm_ffn, name="ffn_act0_bwd")
    dup0 = dup0.reshape(up0.shape)
    d_up0 = tn(h1b, dup0, name="d_up0")
    grads["b_w_in"] = [d_bin[0].transpose(1, 0, 2).reshape(d, 4, 3 * d // 4).transpose(1, 0, 2)]
    grads["b_w_out"] = [d_bout.reshape(4, d // 4, d)]
    grads["ffn_w_up"] = [d_up0[0], d_up1[0]]
    grads["ffn_w_down"] = [t.reshape(4, -1, d) for t in (d_down0, d_down1)]
    complete = complete_grads(grads)

    dr1, dgb1, from_sibling = nt_ln_bwd(dup0, wt["up"][0], dr2, r1, ln_g[0], name="d_h1_ln1", swap=complete)
    leaving = sum_pairs(complete, from_sibling)
    d_aout = tn(onz, dr1, name="d_a_out")
    donz = nt(dr1, wt["a_out"], name="d_onz")
    d_o, dz, dnw = gdn_post_bwd(o[None], pz, donz, wt["anorm_b"], tm=tm)
    dqkv, dgates, landed = gdn_chunk_bwd(qkv, gates, states, tinv, d_o[0], leaving)
    dp3, daconv = gdn_pre_bwd(p3, dqkv, wt["a_conv3"], tm=tm_pre, cb=HEADS * HEAD_DIM)
    dpba, dscal = gdn_gates_bwd(pba, dgates, wt["alog_lanes"], wt["dtb_lanes"], tm=tm)
    d_a3 = tn(h0b, dp3, name="d_a_in3")
    d_az = tn(h0b, dz, name="d_a_inz")
    d_aba = tn(h0b, dpba, name="d_a_inba")
    dh0 = nt(dp3, wt["a3"], res=dr1, res_scale=ALPHA, name="d_h0a")
    dh0 = nt(dz, wt["az"], res=dh0, res_scale=1.0, name="d_h0z")
    dh0 = nt(dpba, wt["a_ba"], res=dh0, res_scale=1.0, name="d_h0")

    width = HEADS * HEAD_DIM
    d_a_in = jnp.concatenate([d_a3[0, 0], d_a3[0, 1], d_a3[0, 2], d_az[0, 0], d_aba[0, 0][:, :2 * HEADS]], axis=1)
    n_in = d_a_in.shape[1] // 4
    grads["a_w_in"] = [d_a_in.reshape(d, 4, n_in).transpose(1, 0, 2)]
    grads["a_w_out"] = [d_aout.reshape(4, width // 4, d)]
    grads["a_conv"] = daconv.sum(axis=2).transpose(1, 0, 2).reshape(1, GDN_CONV, 3 * width)
    per_head = dscal.sum(axis=1)[:, HEADS:2 * HEADS]
    grads["a_log"] = per_head[0][None]
    grads["a_dt_bias"] = per_head[1][None]
    grads["a_norm"] = dnw.reshape(8, HEADS, HEAD_DIM).sum(axis=(0, 1))[None]
    grads["b_conv"] = dbconv.sum(axis=1)[None]
    lns = [dgb1, dgb2, dgb3, dgb4]
    grads["ln_mix_g"] = jnp.stack([lns[0][0].sum(0), lns[2][0].sum(0)])
    grads["ln_mix_b"] = jnp.stack([lns[0][1].sum(0), lns[2][1].sum(0)])
    grads["ln_ffn_g"] = jnp.stack([lns[1][0].sum(0), lns[3][0].sum(0)])
    grads["ln_ffn_b"] = jnp.stack([lns[1][1].sum(0), lns[3][1].sum(0)])
    grads["ffn_conv"] = jnp.stack([t.sum(axis=2).transpose(1, 0, 2).reshape(FFN_CONV, -1) for t in (dfconv0, dfconv1)])
    grads["meta"] = dh0[0, :N_META]
    return loss_part, dh0, grads, landed


WEIGHTS = ["meta", "a_w_in", "a_conv", "a_log", "a_dt_bias", "a_norm", "a_w_out", "b_w_in", "b_conv", "b_w_out",
           "ln_mix_g", "ln_mix_b", "ffn_w_up", "ffn_conv", "ffn_w_down", "ln_ffn_g", "ln_ffn_b"]
EARLY_WEIGHTS = ["a_w_in", "a_w_out"]
LATE_WEIGHTS = ["b_w_in", "b_w_out", "ffn_w_up", "ffn_w_down"]
MATMUL_WEIGHTS = EARLY_WEIGHTS + LATE_WEIGHTS
SMALL_SHARDED = ["a_conv", "b_conv", "ffn_conv", "meta"]
REPLICATED = ["a_log", "a_dt_bias", "a_norm", "ln_mix_g", "ln_mix_b", "ln_ffn_g", "ln_ffn_b"]
SHARD_AXIS = {"meta": 1, "a_w_in": 2, "a_conv": 2, "a_w_out": 1, "b_w_in": 2, "b_conv": 2, "b_w_out": 1,
              "ffn_w_up": 2, "ffn_conv": 2, "ffn_w_down": 1}
PACK_COLS = 1024
PACK_ROWS_MULTIPLE = 32


def _pack(pieces, lead=()):
    flat = jnp.concatenate([p.reshape(lead + (-1,)) for p in pieces], axis=-1)
    n = flat.shape[-1]
    rows = -(-n // (PACK_COLS * PACK_ROWS_MULTIPLE)) * PACK_ROWS_MULTIPLE
    flat = jnp.pad(flat, [(0, 0)] * len(lead) + [(0, rows * PACK_COLS - n)])
    return flat.reshape(lead + (rows, PACK_COLS))


def _unpack(buf, shapes, lead=()):
    flat = buf.reshape(lead + (-1,))
    out, off = [], 0
    for shp in shapes:
        n = 1
        for s in shp:
            n *= s
        out.append(flat[..., off:off + n].reshape(lead + tuple(shp)))
        off += n
    return out


def _join_shards(stacked, axis):
    return jnp.concatenate([stacked[k] for k in range(4)], axis=axis)


def _split_shards(full, axis):
    return jnp.stack(jnp.split(full, 4, axis=axis))


def _weight_layers(w, names):
    return [w[n][l].astype(BF16) for n in names for l in range(w[n].shape[0])]


def _per_weight(arrays, w, names):
    it = iter(arrays)
    return {n: [next(it) for _ in range(w[n].shape[0])] for n in names}


def _layout_early(full, w):
    width = HEADS * HEAD_DIM
    wt = {n: w[n] for n in ("ln_mix_g", "ln_mix_b", "ln_ffn_g", "ln_ffn_b")}
    w_in = _join_shards(full["a_w_in"][0], 1)
    d = w_in.shape[0]
    n_ff = full["ffn_conv"].shape[2] // 2
    blocks = [w_in[:, s * width:(s + 1) * width] for s in range(4)]
    wt["a3"] = jnp.stack(blocks[:3])[None]
    wt["az"] = blocks[3][None, None]
    wt["a_ba"] = jnp.pad(w_in[:, 4 * width:], ((0, 0), (0, HEAD_DIM - 2 * HEADS)))[None, None]
    wt["a_out"] = full["a_w_out"][0].reshape(1, 1, width, d)
    wt["a_conv3"] = full["a_conv"][0].reshape(GDN_CONV, 3, width).transpose(1, 0, 2)
    wt["b_conv"] = full["b_conv"][0]
    wt["fconv"] = [full["ffn_conv"][l].reshape(FFN_CONV, 2, n_ff).transpose(1, 0, 2) for l in range(2)]
    wt["meta"] = full["meta"]
    in_g_lanes = (HEADS, HEAD_DIM - 2 * HEADS)
    wt["alog_lanes"] = jnp.pad(w["a_log"][0], in_g_lanes)[None]
    wt["dtb_lanes"] = jnp.pad(w["a_dt_bias"][0], in_g_lanes)[None]
    wt["anorm_b"] = jnp.tile(w["a_norm"][0], HEADS)[None]
    return wt


def _layout_late(full):
    d = full["b_w_in"][0].shape[1]
    n_ff = full["ffn_w_up"][0].shape[2]
    return {
        "b_in": _join_shards(full["b_w_in"][0], 1).reshape(d, 3, d).transpose(1, 0, 2)[None],
        "b_out": full["b_w_out"][0].reshape(1, 1, d, d),
        "up": [t[None] for t in full["ffn_w_up"]],
        "down": [t.reshape(2, 1, n_ff, d) for t in full["ffn_w_down"]],
    }


def kernel(x, meta, a_w_in, a_conv, a_log, a_dt_bias, a_norm, a_w_out, b_w_in, b_conv, b_w_out, ln_mix_g, ln_mix_b, ffn_w_up, ffn_conv, ffn_w_down, ln_ffn_g, ln_ffn_b, loss_target, m_meta, m_a_w_in, m_a_conv, m_a_log, m_a_dt_bias, m_a_norm, m_a_w_out, m_b_w_in, m_b_conv, m_b_w_out, m_ln_mix_g, m_ln_mix_b, m_ffn_w_up, m_ffn_conv, m_ffn_w_down, m_ln_ffn_g, m_ln_ffn_b, v_meta, v_a_w_in, v_a_conv, v_a_log, v_a_dt_bias, v_a_norm, v_a_w_out, v_b_w_in, v_b_conv, v_b_w_out, v_ln_mix_g, v_ln_mix_b, v_ffn_w_up, v_ffn_conv, v_ffn_w_down, v_ln_ffn_g, v_ln_ffn_b):
    w = dict(meta=meta, a_w_in=a_w_in, a_conv=a_conv, a_log=a_log, a_dt_bias=a_dt_bias, a_norm=a_norm, a_w_out=a_w_out,
             b_w_in=b_w_in, b_conv=b_conv, b_w_out=b_w_out, ln_mix_g=ln_mix_g, ln_mix_b=ln_mix_b, ffn_w_up=ffn_w_up,
             ffn_conv=ffn_conv, ffn_w_down=ffn_w_down, ln_ffn_g=ln_ffn_g, ln_ffn_b=ln_ffn_b)
    m = dict(meta=m_meta, a_w_in=m_a_w_in, a_conv=m_a_conv, a_log=m_a_log, a_dt_bias=m_a_dt_bias, a_norm=m_a_norm,
             a_w_out=m_a_w_out, b_w_in=m_b_w_in, b_conv=m_b_conv, b_w_out=m_b_w_out, ln_mix_g=m_ln_mix_g,
             ln_mix_b=m_ln_mix_b, ffn_w_up=m_ffn_w_up, ffn_conv=m_ffn_conv, ffn_w_down=m_ffn_w_down,
             ln_ffn_g=m_ln_ffn_g, ln_ffn_b=m_ln_ffn_b)
    v = dict(meta=v_meta, a_w_in=v_a_w_in, a_conv=v_a_conv, a_log=v_a_log, a_dt_bias=v_a_dt_bias, a_norm=v_a_norm,
             a_w_out=v_a_w_out, b_w_in=v_b_w_in, b_conv=v_b_conv, b_w_out=v_b_w_out, ln_mix_g=v_ln_mix_g,
             ln_mix_b=v_ln_mix_b, ffn_w_up=v_ffn_w_up, ffn_conv=v_ffn_conv, ffn_w_down=v_ffn_w_down,
             ln_ffn_g=v_ln_ffn_g, ln_ffn_b=v_ln_ffn_b)
    seq = x.shape[1]
    *stacks, small = all_gather_shards(_weight_layers(w, EARLY_WEIGHTS) + [_pack([w[n] for n in SMALL_SHARDED])],
                                       "gather_early")
    full = _per_weight(stacks, w, EARLY_WEIGHTS)
    for n, t in zip(SMALL_SHARDED, _unpack(small, [w[n].shape for n in SMALL_SHARDED], lead=(4,))):
        full[n] = _join_shards(t, SHARD_AXIS[n])

    def layout_late(late_stacks):
        return _layout_late(_per_weight(late_stacks, w, LATE_WEIGHTS))

    def complete_grads(grads):
        return [g for n in LATE_WEIGHTS for g in grads[n]]

    def sum_pairs(bufs, from_sibling):
        return pair_sums(bufs, from_sibling, [BF16] * len(bufs), "late")

    loss_part, dh0, grads, landed_late = _local_step(x[0], loss_target[0], _layout_early(full, w),
                                                     _weight_layers(w, LATE_WEIGHTS), layout_late, complete_grads, sum_pairs)
    pieces = [_split_shards(grads[n], SHARD_AXIS[n]) for n in SMALL_SHARDED]
    same = jnp.concatenate([grads[n].reshape(-1) for n in REPLICATED] + [jnp.sum(loss_part).reshape(1)])
    pieces.append(jnp.broadcast_to(same, (4,) + same.shape))
    bufs = [g for n in EARLY_WEIGHTS for g in grads[n]] + [_pack(pieces, lead=(4,))]
    from_sibling = swap_halves(bufs, "rs_pair_early")
    landed = scatter_to_chips(pair_sums(bufs, from_sibling, [BF16] * (len(bufs) - 1) + [F32], "early"), "rs_chips_early")
    totals = [chip_sum(t, "rs_chip_sum%d" % i) for i, t in enumerate(landed + landed_late)]
    by_weight = _per_weight(totals[:len(bufs) - 1] + totals[len(bufs):], w, MATMUL_WEIGHTS)
    *shared, small_total = share_halves([by_weight[n] for n in MATMUL_WEIGHTS] + [[totals[len(bufs) - 1]]], "rs_share")
    grad_w = {n: t.reshape(w[n].shape) for n, t in zip(MATMUL_WEIGHTS, shared)}
    rest = SMALL_SHARDED + REPLICATED
    unpacked = _unpack(small_total[0], [w[n].shape for n in rest] + [()])
    grad_w.update(zip(rest, unpacked[:-1]))
    loss = unpacked[-1]
    grad_x = dh0[:, N_META:N_META + seq]
    steps = [adamw(w[n], grad_w[n], m[n], v[n], "adamw_" + n) for n in WEIGHTS]
    return (loss, grad_x, *[grad_w[n] for n in WEIGHTS], *[s[0] for s in steps], *[s[1] for s in steps],
            *[s[2] for s in steps])
```

```python
import functools

import jax
import jax.numpy as jnp
from jax import lax
from jax.experimental import pallas as pl
from jax.experimental.pallas import tpu as pltpu

F32 = jnp.float32
BF16 = jnp.bfloat16

N_META = 16
HEADS = 8
HEAD_DIM = 128
CHUNK = 64
GDN_CONV = 4
FFN_CONV = 3
ALPHA = 4.0 ** 0.25
LN_EPS = 1e-5
RMS_EPS = 1e-6
L2_EPS = 1e-6
Q_SCALE = HEAD_DIM ** -0.5

ADAM_LR = 0.001
ADAM_B1 = 0.9
ADAM_B2 = 0.999
ADAM_EPS = 1e-08
ADAM_WD = 0.01
ADAM_STEP = 10

HALO = 8
VMEM_LIMIT = 48 * 1024 * 1024


def _params(sem=None):
    return pltpu.CompilerParams(dimension_semantics=sem, vmem_limit_bytes=VMEM_LIMIT)


def _dot(a, b, prec=None):
    return jnp.dot(a, b, preferred_element_type=F32, precision=prec)


def _dot_nt(a, b, prec=None):
    return lax.dot_general(a, b, (((1,), (1,)), ((), ())), preferred_element_type=F32, precision=prec)


def _dot_tn(a, b, prec=None):
    return lax.dot_general(a, b, (((0,), (0,)), ((), ())), preferred_element_type=F32, precision=prec)


def _sigmoid(x):
    return 0.5 * jnp.tanh(0.5 * x) + 0.5


def _tri_masks():
    r = lax.broadcasted_iota(jnp.int32, (CHUNK, CHUNK), 0)
    c = lax.broadcasted_iota(jnp.int32, (CHUNK, CHUNK), 1)
    return r >= c, r > c, r == c


def _split_hi_lo(x):
    hi = x.astype(BF16)
    return hi, (x - hi.astype(F32)).astype(BF16)


def _mask_dot(mask, x):
    hi, lo = _split_hi_lo(x)
    return _dot(mask, hi) + _dot(mask, lo)


def _cumsum_rows(g):
    causal, _, _ = _tri_masks()
    return _mask_dot(causal.astype(BF16), g)


def _cumsum_rows_transposed(dy):
    _, strict, _ = _tri_masks()
    return _mask_dot((~strict).astype(BF16), dy)


def _dot_split3(a, b):
    a_hi, a_lo = _split_hi_lo(a)
    b_hi, b_lo = _split_hi_lo(b)
    return _dot(a_hi, b_hi) + (_dot(a_hi, b_lo) + _dot(a_lo, b_hi))


@jax.custom_vjp
def _dot_precise(a, b):
    return _dot_split3(a, b)


def _dot_precise_fwd(a, b):
    return _dot_split3(a, b), (a, b)


def _dot_precise_bwd(operands, ct):
    a, b = operands
    return _dot_split3(ct, b.T), _dot_split3(a.T, ct)


_dot_precise.defvjp(_dot_precise_fwd, _dot_precise_bwd)


def _gdn_m(ks, a64s, bbs):
    causal, strict, _ = _tri_masks()
    decay = [jnp.exp(jnp.where(causal, x - x.T, -1e30)) for x in a64s]
    kk = [_dot_nt(k * b, k) for k, b in zip(ks, bbs)]
    return [jnp.where(strict, x * d, 0.0) for x, d in zip(kk, decay)]


def _gdn_inverse_stages(ks, a64s, bbs):
    ms = _gdn_m(ks, a64s, bbs)
    yield
    r = lax.broadcasted_iota(jnp.int32, (CHUNK, CHUNK), 0)
    c = lax.broadcasted_iota(jnp.int32, (CHUNK, CHUNK), 1)
    eye = (r == c).astype(F32)
    same = [jnp.right_shift(r, s) == jnp.right_shift(c, s) for s in (3, 4, 5)]
    d = [jnp.where(same[0], m, 0.0) for m in ms]
    p = [_dot(x, x) for x in d]
    yield
    t = [eye - x for x in d]
    t = [x + _dot(x, y) for x, y in zip(t, p)]
    p = [_dot(x, x) for x in p]
    yield
    t = [x + _dot(x, y) for x, y in zip(t, p)]
    yield
    for inner, outer in ((same[0], same[1]), (same[1], same[2]), (same[2], None)):
        joins = ~inner if outer is None else (outer & ~inner)
        o = [_dot(x, jnp.where(joins, m, 0.0)) for x, m in zip(t, ms)]
        yield
        t = [x - _dot(y, x) for x, y in zip(t, o)]
        yield
    res = [eye - x - _dot_split3(m, x) for m, x in zip(ms, t)]
    yield
    return [x + _dot(x, y) for x, y in zip(t, res)]


def _gdn_apply_stages(qs, ks, vs, gc, a64s, gl, bbs, ss, ts):
    causal, _, _ = _tri_masks()
    n = range(len(qs))
    qk = [_dot_nt(qs[h], ks[h]) for h in n]
    yield
    decay = [jnp.exp(jnp.where(causal, x - x.T, -1e30)) for x in a64s]
    eg = [jnp.exp(x) for x in gc]
    u = [_dot_precise(ts[h], vs[h] * bbs[h]) for h in n]
    w = [_dot_precise(ts[h], ks[h] * bbs[h] * eg[h]) for h in n]
    qk = [qk[h] * decay[h] for h in n]
    kd = [ks[h] * jnp.exp(gl[h] - gc[h]) for h in n]
    yield
    v_new = [u[h] - _dot(w[h], ss[h]) for h in n]
    q_s = [_dot(qs[h] * eg[h], ss[h]) for h in n]
    yield
    o = [q_s[h] + _dot(qk[h], v_new[h]) for h in n]
    s2 = [ss[h] * jnp.exp(gl[h]) + _dot_tn(kd[h], v_new[h]) for h in n]
    return o, s2


def _run_stages(*generators):
    results = [None] * len(generators)
    live = dict(enumerate(generators))
    while live:
        for i, gen in list(live.items()):
            try:
                next(gen)
            except StopIteration as stop:
                results[i] = stop.value
                del live[i]
    return results


def _head_slices(h):
    return slice(h * HEAD_DIM, (h + 1) * HEAD_DIM), slice(h * HEAD_DIM, h * HEAD_DIM + CHUNK)


def _gdn_head_values(x_ref, gate_ref):
    heads = range(HEADS)
    qs, ks, vs = ([x_ref[s, :, _head_slices(h)[0]] for h in heads] for s in range(3))
    gate = gate_ref[...]
    cumulative = _cumsum_rows(gate)
    total = jnp.sum(gate, axis=0, keepdims=True)
    gcums = [cumulative[:, HEADS + h:HEADS + h + 1] for h in heads]
    gtots = [total[:, HEADS + h:HEADS + h + 1] for h in heads]
    bcols = [gate[:, h:h + 1] for h in heads]
    return qs, ks, vs, gcums, gtots, bcols


def _over_lanes(cols, lanes):
    return [jnp.broadcast_to(c, (c.shape[0], lanes)) for c in cols]


def _gdn_inverse_cols(ks, gcums, bcols):
    return _gdn_inverse_stages(ks, _over_lanes(gcums, CHUNK), _over_lanes(bcols, HEAD_DIM))


def _gdn_apply_cols_stages(qs, ks, vs, gcums, gtots, bcols, ss, ts):
    return _gdn_apply_stages(qs, ks, vs, _over_lanes(gcums, HEAD_DIM), _over_lanes(gcums, CHUNK),
                             _over_lanes(gtots, HEAD_DIM), _over_lanes(bcols, HEAD_DIM), ss, ts)


def _gdn_apply_cols(qs, ks, vs, gcums, gtots, bcols, ss, ts):
    return _run_stages(_gdn_apply_cols_stages(qs, ks, vs, gcums, gtots, bcols, ss, ts))[0]


def _gdn_m_cols(ks, gcums, bcols):
    return _gdn_m(ks, _over_lanes(gcums, CHUNK), _over_lanes(bcols, HEAD_DIM))


def _gate_lanes(bcols, gcols):
    rows = gcols[0].shape[0]
    lane = lax.broadcasted_iota(jnp.int32, (rows, HEAD_DIM), 1)
    out = jnp.zeros((rows, HEAD_DIM), F32)
    for h in range(HEADS):
        if bcols is not None:
            out = jnp.where(lane == h, jnp.broadcast_to(bcols[h], out.shape), out)
        out = jnp.where(lane == HEADS + h, jnp.broadcast_to(gcols[h], out.shape), out)
    return out


def _gate_gradient(dbcols, dgcums, dgtots):
    block = _gate_lanes(dbcols, dgcums)
    lane = lax.broadcasted_iota(jnp.int32, block.shape, 1)
    return jnp.where(lane < HEADS, block, _cumsum_rows_transposed(block) + _gate_lanes(None, dgtots))


def gdn_chunk_fwd(qkv, gates, gather=()):
    _, lp, width = qkv.shape
    n_chunks = lp // CHUNK
    n = len(gather)

    def body(x_ref, gate_ref, next_ref, next_gate_ref, *refs):
        shard_refs, (o_ref, s_ref, t_ref), refs = refs[:n], refs[n:n + 3], refs[n + 3:]
        stack_refs, state, t_next, sems = refs[:n], refs[n], refs[n + 1], refs[n + 2:]
        copies = _gather_copies(shard_refs, stack_refs, *sems) if n else None

        def inverse_stages(ref, g_ref):
            _, ks, _, gcums, _, bcols = _gdn_head_values(ref, g_ref)
            return _gdn_inverse_cols(ks, gcums, bcols)

        @pl.when(pl.program_id(0) == 0)
        def _():
            state[...] = jnp.zeros_like(state)
            for h, t in enumerate(_run_stages(inverse_stages(x_ref, gate_ref))[0]):
                t_next[h] = t
            if n:
                _gather_start(copies)

        qs, ks, vs, gcums, gtots, bcols = _gdn_head_values(x_ref, gate_ref)
        ss = [state[h] for h in range(HEADS)]
        ts = [t_next[h] for h in range(HEADS)]
        ts_next, (os_, s2) = _run_stages(inverse_stages(next_ref, next_gate_ref),
                                         _gdn_apply_cols_stages(qs, ks, vs, gcums, gtots, bcols, ss, ts))
        for h in range(HEADS):
            s_ref[0, h] = ss[h]
            t_ref[0, h] = ts[h]
            t_next[h] = ts_next[h]
            o_ref[:, _head_slices(h)[0]] = os_[h]
            state[h] = s2[h]

        if n:
            @pl.when(pl.program_id(0) == n_chunks - 1)
            def _():
                _gather_finish(copies)

    o, states, tinv, *stacks = pl.pallas_call(
        body,
        name="gdn_chunk_fwd",
        grid=(n_chunks,),
        in_specs=[pl.BlockSpec((3, CHUNK, width), lambda c: (0, c, 0)),
                  pl.BlockSpec((CHUNK, HEAD_DIM), lambda c: (c, 0)),
                  pl.BlockSpec((3, CHUNK, width), lambda c: (0, jnp.minimum(c + 1, n_chunks - 1), 0)),
                  pl.BlockSpec((CHUNK, HEAD_DIM), lambda c: (jnp.minimum(c + 1, n_chunks - 1), 0))] + [ANY] * n,
        out_specs=[
            pl.BlockSpec((CHUNK, width), lambda c: (c, 0)),
            pl.BlockSpec((1, HEADS, HEAD_DIM, HEAD_DIM), lambda c: (c, 0, 0, 0)),
            pl.BlockSpec((1, HEADS, CHUNK, CHUNK), lambda c: (c, 0, 0, 0)),
        ] + [ANY] * n,
        out_shape=[
            jax.ShapeDtypeStruct((lp, width), F32),
            jax.ShapeDtypeStruct((n_chunks, HEADS, HEAD_DIM, HEAD_DIM), F32),
            jax.ShapeDtypeStruct((n_chunks, HEADS, CHUNK, CHUNK), F32),
        ] + _gather_out_shapes(gather),
        scratch_shapes=[pltpu.VMEM((HEADS, HEAD_DIM, HEAD_DIM), F32), pltpu.VMEM((HEADS, CHUNK, CHUNK), F32)]
        + (_gather_sems(n) if n else []),
        compiler_params=_params(("arbitrary",)),
    )(qkv, gates, qkv, gates, *gather)
    return o, states, tinv, _set_own_slots(stacks, gather)


def gdn_chunk_bwd(qkv, gates, states, tinv, d_o, scatter=()):
    _, lp, width = qkv.shape
    n_chunks = lp // CHUNK
    last = n_chunks - 1
    n = len(scatter)

    def body(x_ref, gate_ref, s_ref, t_ref, do_ref, *refs):
        leaving_refs, dx_ref, dgate_ref, refs = refs[:n], refs[n], refs[n + 1], refs[n + 2:]
        landing_refs, dstate, sems = refs[:n], refs[n], refs[n + 1:]
        copies = _scatter_copies(leaving_refs, landing_refs, *sems) if n else None

        @pl.when(pl.program_id(0) == 0)
        def _():
            dstate[...] = jnp.zeros_like(dstate)
            if n:
                _scatter_start(copies)

        heads = range(HEADS)
        qs, ks, vs, gcums, gtots, bcols = _gdn_head_values(x_ref, gate_ref)
        ss = [s_ref[0, h] for h in heads]
        ts = [t_ref[0, h] for h in heads]
        d_out = ([do_ref[:, _head_slices(h)[0]] for h in heads], [dstate[h] for h in heads])
        _, vjp_apply = jax.vjp(_gdn_apply_cols, qs, ks, vs, gcums, gtots, bcols, ss, ts)
        dq, dk, dv, dgc, dgt, db, ds, dt = vjp_apply(d_out)
        tts = [t.T for t in ts]
        dm = [_dot(tts[h], dt[h]) for h in heads]
        dm = [-_dot(dm[h], tts[h]) for h in heads]
        _, vjp_m = jax.vjp(_gdn_m_cols, ks, gcums, bcols)
        dk2, dgc2, db2 = vjp_m(dm)
        for h in heads:
            sl = _head_slices(h)[0]
            dx_ref[0, :, sl] = dq[h]
            dx_ref[1, :, sl] = dk[h] + dk2[h]
            dx_ref[2, :, sl] = dv[h]
            dstate[h] = ds[h]
        dgate_ref[...] = _gate_gradient([db[h] + db2[h] for h in heads], [dgc[h] + dgc2[h] for h in heads], dgt)

        if n:
            @pl.when(pl.program_id(0) == n_chunks - 1)
            def _():
                _scatter_finish(copies)

    dqkv, dgates, *landed = pl.pallas_call(
        body,
        name="gdn_chunk_bwd",
        grid=(n_chunks,),
        in_specs=[
            pl.BlockSpec((3, CHUNK, width), lambda c: (0, last - c, 0)),
            pl.BlockSpec((CHUNK, HEAD_DIM), lambda c: (last - c, 0)),
            pl.BlockSpec((1, HEADS, HEAD_DIM, HEAD_DIM), lambda c: (last - c, 0, 0, 0)),
            pl.BlockSpec((1, HEADS, CHUNK, CHUNK), lambda c: (last - c, 0, 0, 0)),
            pl.BlockSpec((CHUNK, width), lambda c: (last - c, 0)),
        ] + [ANY] * n,
        out_specs=[pl.BlockSpec((3, CHUNK, width), lambda c: (0, last - c, 0)),
                   pl.BlockSpec((CHUNK, HEAD_DIM), lambda c: (last - c, 0))] + [ANY] * n,
        out_shape=[jax.ShapeDtypeStruct(qkv.shape, F32), jax.ShapeDtypeStruct(gates.shape, F32)]
        + [jax.ShapeDtypeStruct(b.shape, b.dtype) for b in scatter],
        scratch_shapes=[pltpu.VMEM((HEADS, HEAD_DIM, HEAD_DIM), F32)] + (_scatter_sems(n) if n else []),
        compiler_params=_params(("arbitrary",)),
    )(qkv, gates, states, tinv, d_o, *scatter)
    return dqkv, dgates, _keep_own_slots(landed, scatter)


def mm_nn(a, b, *, tm, name):
    ks, m, tk = a.shape
    _, ns, _, tn = b.shape

    def body(a_ref, b_ref, o_ref):
        p = _dot(a_ref[...].astype(BF16), b_ref[...])

        @pl.when(pl.program_id(2) == 0)
        def _():
            o_ref[...] = p

        @pl.when(pl.program_id(2) > 0)
        def _():
            o_ref[...] += p

    return pl.pallas_call(
        body,
        name=name,
        grid=(ns, m // tm, ks),
        in_specs=[
            pl.BlockSpec((None, tm, tk), lambda n, i, k: (k, i, 0)),
            pl.BlockSpec((None, None, tk, tn), lambda n, i, k: (k, n, 0, 0)),
        ],
        out_specs=pl.BlockSpec((None, tm, tn), lambda n, i, k: (n, i, 0)),
        out_shape=jax.ShapeDtypeStruct((ns, m, tn), F32),
        compiler_params=_params(("arbitrary", "arbitrary", "arbitrary")),
    )(a, b)


def mm_nt(dy, w, *, tm, name, res=None, res_scale=1.0):
    ns, m, tn = dy.shape
    ks, _, tk, _ = w.shape

    def body(*refs):
        if res is None:
            dy_ref, w_ref, o_ref = refs
        else:
            dy_ref, w_ref, r_ref, o_ref = refs
        p = _dot_nt(dy_ref[...].astype(BF16), w_ref[...])

        @pl.when(pl.program_id(2) == 0)
        def _():
            o_ref[...] = p if res is None else p + res_scale * r_ref[...]

        @pl.when(pl.program_id(2) > 0)
        def _():
            o_ref[...] += p

    in_specs = [
        pl.BlockSpec((None, tm, tn), lambda k, i, n: (n, i, 0)),
        pl.BlockSpec((None, None, tk, tn), lambda k, i, n: (k, n, 0, 0)),
    ]
    args = [dy, w]
    if res is not None:
        in_specs.append(pl.BlockSpec((None, tm, tk), lambda k, i, n: (k, i, 0)))
        args.append(res)
    return pl.pallas_call(
        body,
        name=name,
        grid=(ks, m // tm, ns),
        in_specs=in_specs,
        out_specs=pl.BlockSpec((None, tm, tk), lambda k, i, n: (k, i, 0)),
        out_shape=jax.ShapeDtypeStruct((ks, m, tk), F32),
        compiler_params=_params(("arbitrary", "arbitrary", "arbitrary")),
    )(*args)


def mm_tn(x, dy, *, tm, name, rb=None):
    ks, m, tk = x.shape
    ns, _, tn = dy.shape
    rb = tk if rb is None else rb

    def body(x_ref, dy_ref, o_ref):
        @pl.when(pl.program_id(2) == 0)
        def _():
            o_ref[...] = jnp.zeros_like(o_ref)

        dyb = dy_ref[...].astype(BF16)
        for r in range(0, tk, rb):
            o_ref[r:r + rb, :] += _dot_tn(x_ref[:, r:r + rb].astype(BF16), dyb)

    return pl.pallas_call(
        body,
        name=name,
        grid=(ks, ns, m // tm),
        in_specs=[
            pl.BlockSpec((None, tm, tk), lambda k, n, i: (k, i, 0)),
            pl.BlockSpec((None, tm, tn), lambda k, n, i: (n, i, 0)),
        ],
        out_specs=pl.BlockSpec((None, None, tk, tn), lambda k, n, i: (k, n, 0, 0)),
        out_shape=jax.ShapeDtypeStruct((ks, ns, tk, tn), F32),
        compiler_params=_params(("arbitrary", "arbitrary", "arbitrary")),
    )(x, dy)


def _row_partial(x):
    rows, c = x.shape
    return jnp.sum(x.reshape(rows // 8, 8, c), axis=0)


def _layer_norm(r, g, b):
    mu = jnp.mean(r, axis=-1, keepdims=True)
    xc = r - mu
    var = jnp.mean(xc * xc, axis=-1, keepdims=True)
    return xc * lax.rsqrt(var + LN_EPS) * g + b


def _layer_norm_bwd(x, dh, g):
    mu = jnp.mean(x, axis=-1, keepdims=True)
    xc = x - mu
    rstd = lax.rsqrt(jnp.mean(xc * xc, axis=-1, keepdims=True) + LN_EPS)
    xh = xc * rstd
    dxh = dh * g
    m1 = jnp.mean(dxh, axis=-1, keepdims=True)
    m2 = jnp.mean(dxh * xh, axis=-1, keepdims=True)
    return rstd * (dxh - m1 - xh * m2), _row_partial(dh * xh), _row_partial(dh)


def mm_nn_ln(a, b, h_prev, g, beta, *, tm, name):
    ks, m, tk = a.shape
    d = b.shape[3]

    def body(a_ref, b_ref, hp_ref, g_ref, be_ref, r_ref, h_ref, hb_ref):
        p = _dot(a_ref[...].astype(BF16), b_ref[...])

        @pl.when(pl.program_id(1) == 0)
        def _():
            r_ref[...] = p

        @pl.when(pl.program_id(1) > 0)
        def _():
            r_ref[...] += p

        @pl.when(pl.program_id(1) == ks - 1)
        def _():
            r = ALPHA * hp_ref[...] + r_ref[...]
            r_ref[...] = r
            h = _layer_norm(r, g_ref[...], be_ref[...])
            h_ref[...] = h
            hb_ref[...] = h.astype(BF16)

    row = pl.BlockSpec((None, tm, d), lambda i, k: (0, i, 0))
    vec = pl.BlockSpec((1, d), lambda i, k: (0, 0))
    return pl.pallas_call(
        body,
        name=name,
        grid=(m // tm, ks),
        in_specs=[
            pl.BlockSpec((None, tm, tk), lambda i, k: (k, i, 0)),
            pl.BlockSpec((None, None, tk, d), lambda i, k: (k, 0, 0, 0)),
            row, vec, vec,
        ],
        out_specs=[row, row, row],
        out_shape=[jax.ShapeDtypeStruct((1, m, d), F32)] * 2 + [jax.ShapeDtypeStruct((1, m, d), BF16)],
        compiler_params=_params(("arbitrary", "arbitrary")),
    )(a, b, h_prev, g, beta)


def mm_nt_ln_bwd(dy, w, res, r, g, *, tm, name, swap=()):
    ns, m, tn = dy.shape
    d = w.shape[2]
    n_swap = len(swap)
    last_tile = m // tm - 1

    def body(dy_ref, w_ref, res_ref, r_ref, g_ref, *refs):
        leaving_refs, (dr_ref, dgb_ref), refs = refs[:n_swap], refs[n_swap:n_swap + 2], refs[n_swap + 2:]
        copies = _swap_copies(leaving_refs, refs[:n_swap], *refs[n_swap:]) if n_swap else None
        p = _dot_nt(dy_ref[...].astype(BF16), w_ref[...])

        @pl.when((pl.program_id(0) == 0) & (pl.program_id(1) == 0))
        def _():
            dgb_ref[...] = jnp.zeros_like(dgb_ref)
            if n_swap:
                _swap_start(copies)

        @pl.when(pl.program_id(1) == 0)
        def _():
            dr_ref[...] = p + ALPHA * res_ref[...]

        @pl.when(pl.program_id(1) > 0)
        def _():
            dr_ref[...] += p

        @pl.when(pl.program_id(1) == ns - 1)
        def _():
            for rows in (pl.ds(0, tm // 2), pl.ds(tm // 2, tm // 2)):
                dr, dgamma, dbeta = _layer_norm_bwd(r_ref[rows, :], dr_ref[rows, :], g_ref[...])
                dr_ref[rows, :] = dr
                dgb_ref[0] += dgamma
                dgb_ref[1] += dbeta

        if n_swap:
            @pl.when((pl.program_id(0) == last_tile) & (pl.program_id(1) == ns - 1))
            def _():
                _swap_finish(copies)

    row = pl.BlockSpec((None, tm, d), lambda i, n: (0, i, 0))
    dr, dgb, *landed = pl.pallas_call(
        body,
        name=name,
        grid=(m // tm, ns),
        in_specs=[
            pl.BlockSpec((None, tm, tn), lambda i, n: (n, i, 0)),
            pl.BlockSpec((None, None, d, tn), lambda i, n: (0, n, 0, 0)),
            row, row,
            pl.BlockSpec((1, d), lambda i, n: (0, 0)),
        ] + [ANY] * n_swap,
        out_specs=[row, pl.BlockSpec((2, 8, d), lambda i, n: (0, 0, 0))] + [ANY] * n_swap,
        out_shape=[jax.ShapeDtypeStruct((1, m, d), F32), jax.ShapeDtypeStruct((2, 8, d), F32)] + _swap_out_shapes(swap),
        scratch_shapes=_swap_sems(n_swap) if n_swap else [],
        compiler_params=_params(("arbitrary", "arbitrary")),
    )(dy, w, res, r, g, *swap)
    return dr, dgb, landed


def loss_ln_bwd(h, target, r, g, *, first, count, tm):
    _, lp, d = h.shape

    def body(h_ref, t_ref, r_ref, g_ref, dr_ref, dgb_ref, l_ref):
        row = pl.program_id(0) * tm + lax.broadcasted_iota(jnp.int32, (tm, d), 0)
        valid = (row >= first) & (row < first + count)
        err = jnp.where(valid, h_ref[...] - t_ref[...], 0.0)
        dr, dgamma, dbeta = _layer_norm_bwd(r_ref[...], err * (1.0 / d), g_ref[...])
        dr_ref[...] = dr

        @pl.when(pl.program_id(0) == 0)
        def _():
            dgb_ref[...] = jnp.zeros_like(dgb_ref)
            l_ref[...] = jnp.zeros_like(l_ref)

        dgb_ref[0] += dgamma
        dgb_ref[1] += dbeta
        l_ref[...] += _row_partial(err * err) * (0.5 / d)

    row3 = pl.BlockSpec((None, tm, d), lambda i: (0, i, 0))
    return pl.pallas_call(
        body,
        name="loss_ln4_bwd",
        grid=(lp // tm,),
        in_specs=[row3, pl.BlockSpec((tm, d), lambda i: (i, 0)), row3, pl.BlockSpec((1, d), lambda i: (0, 0))],
        out_specs=[row3, pl.BlockSpec((2, 8, d), lambda i: (0, 0, 0)), pl.BlockSpec((8, d), lambda i: (0, 0))],
        out_shape=[jax.ShapeDtypeStruct((1, lp, d), F32), jax.ShapeDtypeStruct((2, 8, d), F32),
                   jax.ShapeDtypeStruct((8, d), F32)],
        compiler_params=_params(("arbitrary",)),
    )(h, target, r, g)


def _halo_index(tile, tm):
    return jnp.maximum(tile * (tm // HALO) - 1, 0)


def _conv_fwd(xs_ref, w, taps, tm):
    acc = w(0) * xs_ref[pl.ds(HALO - taps + 1, tm), :]
    for j in range(1, taps):
        acc += w(j) * xs_ref[pl.ds(HALO - taps + 1 + j, tm), :]
    return acc


def _conv_bwd_x(dcs_ref, w, taps, tm):
    acc = w(0) * dcs_ref[pl.ds(taps - 1, tm), :]
    for j in range(1, taps):
        acc += w(j) * dcs_ref[pl.ds(taps - 1 - j, tm), :]
    return acc


SUB = 8
LANES = 128
PAIR = 2 * SUB
STRIP_UNROLL = 2


def _pair_rows(r0):
    return pl.ds(r0, SUB), pl.ds(r0 + SUB if isinstance(r0, int) else pl.multiple_of(r0 + SUB, SUB), SUB)


def _shift_down(cur, prev, s):
    if s == 0:
        return cur
    row = lax.broadcasted_iota(jnp.int32, cur.shape, 0)
    return jnp.where(row < s, pltpu.roll(prev, s, axis=0), pltpu.roll(cur, s, axis=0))


def _shift_up(cur, nxt, s):
    if s == 0:
        return cur
    row = lax.broadcasted_iota(jnp.int32, cur.shape, 0)
    return jnp.where(row < SUB - s, pltpu.roll(cur, SUB - s, axis=0), pltpu.roll(nxt, SUB - s, axis=0))


def _silu_parts(c):
    sg = _sigmoid(c)
    return c * sg, sg * (1.0 + c * (1.0 - sg))


def _head_sum(x):
    rows, c = x.shape
    parts = []
    for h in range(c // HEAD_DIM):
        s = jnp.sum(x[:, h * HEAD_DIM:(h + 1) * HEAD_DIM], axis=-1, keepdims=True)
        parts.append(jnp.broadcast_to(s, (rows, HEAD_DIM)))
    return parts[0] if len(parts) == 1 else jnp.concatenate(parts, axis=-1)


def _log1p(y):
    u = 1.0 + y
    d = u - 1.0
    return jnp.where(d == 0.0, y, jnp.log(u) * (y / jnp.where(d == 0.0, 1.0, d)))


def _softplus(x):
    return jnp.maximum(x, 0.0) + _log1p(jnp.exp(-jnp.abs(x)))


def _gate_values(x, al, dt):
    lane = lax.broadcasted_iota(jnp.int32, x.shape, 1)
    is_beta, is_g = lane < HEADS, (lane >= HEADS) & (lane < 2 * HEADS)
    return _sigmoid(x), -jnp.exp(al) * _softplus(x + dt), is_beta, is_g


def gdn_gates_fwd(pba, al, dt, *, tm):
    _, lp, width = pba.shape

    def body(x_ref, al_ref, dt_ref, o_ref):
        beta, g, is_beta, is_g = _gate_values(x_ref[...], al_ref[...], dt_ref[...])
        o_ref[...] = jnp.where(is_beta, beta, jnp.where(is_g, g, 0.0))

    vec = pl.BlockSpec((1, width), lambda i: (0, 0))
    return pl.pallas_call(
        body,
        name="gdn_gates_fwd",
        grid=(lp // tm,),
        in_specs=[pl.BlockSpec((None, tm, width), lambda i: (0, i, 0)), vec, vec],
        out_specs=pl.BlockSpec((tm, width), lambda i: (i, 0)),
        out_shape=jax.ShapeDtypeStruct((lp, width), F32),
        compiler_params=_params(("arbitrary",)),
    )(pba, al, dt)


def gdn_gates_bwd(pba, dgates, al, dt, *, tm):
    _, lp, width = pba.shape

    def body(x_ref, d_ref, al_ref, dt_ref, dx_ref, dsc_ref):
        x = x_ref[...]
        beta, g, is_beta, is_g = _gate_values(x, al_ref[...], dt_ref[...])
        d = d_ref[...]
        dg = jnp.where(is_g, d, 0.0)
        da = dg * -jnp.exp(al_ref[...]) * _sigmoid(x + dt_ref[...])
        dx_ref[...] = jnp.where(is_beta, d * beta * (1.0 - beta), da).astype(dx_ref.dtype)

        @pl.when(pl.program_id(0) == 0)
        def _():
            dsc_ref[...] = jnp.zeros_like(dsc_ref)

        dsc_ref[0] += _row_partial(dg * g)
        dsc_ref[1] += _row_partial(da)

    vec = pl.BlockSpec((1, width), lambda i: (0, 0))
    return pl.pallas_call(
        body,
        name="gdn_gates_bwd",
        grid=(lp // tm,),
        in_specs=[pl.BlockSpec((None, tm, width), lambda i: (0, i, 0)), pl.BlockSpec((tm, width), lambda i: (i, 0)), vec, vec],
        out_specs=[pl.BlockSpec((None, tm, width), lambda i: (0, i, 0)), pl.BlockSpec((2, SUB, width), lambda i: (0, 0, 0))],
        out_shape=[jax.ShapeDtypeStruct((1, lp, width), BF16), jax.ShapeDtypeStruct((2, SUB, width), F32)],
        compiler_params=_params(("arbitrary",)),
    )(pba, dgates, al, dt)


def gdn_pre_fwd(p3, conv_w, *, tm, cb):
    _, lp, width = p3.shape
    taps = conv_w.shape[1]

    def body(x_ref, halo_ref, w_ref, o_ref, xs):
        i = pl.program_id(1)
        for s in range(3):
            xs[s, 0:HALO, :] = jnp.where(i > 0, halo_ref[s], 0.0)
            xs[s, HALO:, :] = x_ref[s]
            c = _conv_fwd(xs.at[s], lambda j, s=s: w_ref[s, j:j + 1, :], taps, tm)
            y, _ = _silu_parts(c)
            if s < 2:
                y = y * lax.rsqrt(_head_sum(y * y) + L2_EPS)
                if s == 0:
                    y = y * Q_SCALE
            o_ref[s] = y

    return pl.pallas_call(
        body,
        name="gdn_pre_fwd",
        grid=(width // cb, lp // tm),
        in_specs=[
            pl.BlockSpec((3, tm, cb), lambda j, i: (0, i, j)),
            pl.BlockSpec((3, HALO, cb), lambda j, i: (0, _halo_index(i, tm), j)),
            pl.BlockSpec((3, taps, cb), lambda j, i: (0, 0, j)),
        ],
        out_specs=pl.BlockSpec((3, tm, cb), lambda j, i: (0, i, j)),
        out_shape=jax.ShapeDtypeStruct((3, lp, width), F32),
        scratch_shapes=[pltpu.VMEM((3, tm + HALO, cb), F32)],
        compiler_params=_params(("arbitrary", "arbitrary")),
    )(p3, p3, conv_w)


def gdn_pre_bwd(p3, dqkv, conv_w, *, tm, cb):
    _, lp, width = p3.shape
    taps = conv_w.shape[1]
    last = lp // tm - 1

    def body(x_ref, halo_ref, d_ref, w_ref, dx_ref, dw_ref, xs, dcs, carry):
        step = pl.program_id(1)
        tile = last - step

        @pl.when(step == 0)
        def _():
            carry[...] = jnp.zeros_like(carry)
            dw_ref[...] = jnp.zeros_like(dw_ref)

        for s in range(3):
            w = lambda j, s=s: w_ref[s, j:j + 1, :]
            xs[s, 0:HALO, :] = jnp.where(tile > 0, halo_ref[s], 0.0)
            xs[s, HALO:, :] = x_ref[s]
            c = _conv_fwd(xs.at[s], w, taps, tm)
            y, dsilu = _silu_parts(c)
            dy = d_ref[s]
            if s < 2:
                rn = lax.rsqrt(_head_sum(y * y) + L2_EPS)
                yn = y * rn
                if s == 0:
                    dy = dy * Q_SCALE
                dy = rn * (dy - yn * _head_sum(dy * yn))
            dc = dy * dsilu
            dcs[s, 0:tm, :] = dc
            dcs[s, tm:, :] = carry[s]
            dx_ref[s] = _conv_bwd_x(dcs.at[s], w, taps, tm).astype(dx_ref.dtype)
            carry[s] = dc[0:HALO, :]
            for j in range(taps):
                dw_ref[s, j] += _row_partial(dc * xs[s, pl.ds(HALO - taps + 1 + j, tm), :])

    tile_spec = pl.BlockSpec((3, tm, cb), lambda j, i: (0, last - i, j))
    return pl.pallas_call(
        body,
        name="gdn_pre_bwd",
        grid=(width // cb, lp // tm),
        in_specs=[
            tile_spec,
            pl.BlockSpec((3, HALO, cb), lambda j, i: (0, _halo_index(last - i, tm), j)),
            tile_spec,
            pl.BlockSpec((3, taps, cb), lambda j, i: (0, 0, j)),
        ],
        out_specs=[tile_spec, pl.BlockSpec((3, taps, SUB, cb), lambda j, i: (0, 0, 0, j))],
        out_shape=[jax.ShapeDtypeStruct((3, lp, width), BF16), jax.ShapeDtypeStruct((3, taps, SUB, width), F32)],
        scratch_shapes=[
            pltpu.VMEM((3, tm + HALO, cb), F32),
            pltpu.VMEM((3, tm + HALO, cb), F32),
            pltpu.VMEM((3, HALO, cb), F32),
        ],
        compiler_params=_params(("arbitrary", "arbitrary")),
    )(p3, p3, dqkv, conv_w)


def gdn_post_fwd(o, z, nw_b, *, tm):
    _, lp, width = o.shape

    def body(o_ref, z_ref, nw_ref, y_ref):
        ov = o_ref[...]
        rn = lax.rsqrt(_head_sum(ov * ov) * (1.0 / HEAD_DIM) + RMS_EPS)
        gate, _ = _silu_parts(z_ref[...])
        y_ref[...] = (ov * rn * nw_ref[...] * gate).astype(y_ref.dtype)

    row = pl.BlockSpec((None, tm, width), lambda i: (0, i, 0))
    return pl.pallas_call(
        body,
        name="gdn_post_fwd",
        grid=(lp // tm,),
        in_specs=[row, row, pl.BlockSpec((1, width), lambda i: (0, 0))],
        out_specs=row,
        out_shape=jax.ShapeDtypeStruct((1, lp, width), BF16),
        compiler_params=_params(("arbitrary",)),
    )(o, z, nw_b)


def gdn_post_bwd(o, z, dy, nw_b, *, tm):
    _, lp, width = o.shape

    def body(o_ref, z_ref, dy_ref, nw_ref, do_ref, dz_ref, dnw_ref):
        ov = o_ref[...]
        rn = lax.rsqrt(_head_sum(ov * ov) * (1.0 / HEAD_DIM) + RMS_EPS)
        yn = ov * rn
        gate, dgate = _silu_parts(z_ref[...])
        d_on = dy_ref[...] * gate
        dz_ref[...] = (dy_ref[...] * yn * nw_ref[...] * dgate).astype(dz_ref.dtype)
        a = d_on * nw_ref[...]
        do_ref[...] = rn * (a - yn * (_head_sum(a * yn) * (1.0 / HEAD_DIM)))

        @pl.when(pl.program_id(0) == 0)
        def _():
            dnw_ref[...] = jnp.zeros_like(dnw_ref)

        dnw_ref[...] += _row_partial(d_on * yn)

    row = pl.BlockSpec((None, tm, width), lambda i: (0, i, 0))
    return pl.pallas_call(
        body,
        name="gdn_post_bwd",
        grid=(lp // tm,),
        in_specs=[row, row, row, pl.BlockSpec((1, width), lambda i: (0, 0))],
        out_specs=[row, row, pl.BlockSpec((8, width), lambda i: (0, 0))],
        out_shape=[jax.ShapeDtypeStruct((1, lp, width), F32), jax.ShapeDtypeStruct((1, lp, width), BF16),
                   jax.ShapeDtypeStruct((8, width), F32)],
        compiler_params=_params(("arbitrary",)),
    )(o, z, dy, nw_b)


def ffn_act_fwd(up, conv_w, *, tm, name):
    _, lp, c_w = up.shape
    taps = conv_w.shape[1]

    def body(u_ref, halo_ref, g_ref, w_ref, o_ref):
        first_tile = pl.program_id(1) == 0

        def strip(cur, prev, rows, cs):
            conv = w_ref[taps - 1:taps, cs] * cur
            for j in range(taps - 1):
                conv += w_ref[j:j + 1, cs] * _shift_down(cur, prev, taps - 1 - j)
            y, _ = _silu_parts(conv)
            return y * g_ref[rows, cs]

        def pair(r0, above_of):
            top, bot = _pair_rows(r0)
            for c0 in range(0, c_w, LANES):
                cs = slice(c0, c0 + LANES)
                cur_t, cur_b = u_ref[top, cs], u_ref[bot, cs]
                out = [strip(cur_t, above_of(cs), top, cs), strip(cur_b, cur_t, bot, cs)]
                o_ref[pl.ds(r0, PAIR), cs] = jnp.concatenate(out, axis=0).astype(o_ref.dtype)

        pair(0, lambda cs: jnp.where(first_tile, 0.0, halo_ref[:, cs]))

        def loop_body(s, carry):
            r0 = pl.multiple_of(s * PAIR, PAIR)
            pair(r0, lambda cs: u_ref[pl.ds(pl.multiple_of(r0 - SUB, SUB), SUB), cs])
            return carry

        lax.fori_loop(1, tm // PAIR, loop_body, 0, unroll=STRIP_UNROLL)

    return pl.pallas_call(
        body,
        name=name,
        grid=(2, lp // tm),
        in_specs=[
            pl.BlockSpec((None, tm, c_w), lambda s, i: (s, i, 0)),
            pl.BlockSpec((None, HALO, c_w), lambda s, i: (s, _halo_index(i, tm), 0)),
            pl.BlockSpec((None, tm, c_w), lambda s, i: (2 + s, i, 0)),
            pl.BlockSpec((None, taps, c_w), lambda s, i: (s, 0, 0)),
        ],
        out_specs=pl.BlockSpec((None, tm, c_w), lambda s, i: (s, i, 0)),
        out_shape=jax.ShapeDtypeStruct((2, lp, c_w), BF16),
        compiler_params=_params(("arbitrary", "arbitrary")),
    )(up, up, up, conv_w)


def ffn_act_bwd(up, dact, conv_w, *, tm, name):
    _, lp, c_w = up.shape
    taps = conv_w.shape[1]
    last = lp // tm - 1
    n_pairs = tm // PAIR

    def body(u_ref, halo_ref, g_ref, d_ref, w_ref, dup_ref, dw_ref, below):
        step = pl.program_id(1)
        first_tile = step == last

        @pl.when(step == 0)
        def _():
            below[...] = jnp.zeros_like(below)
            dw_ref[...] = jnp.zeros_like(dw_ref)

        def strip(cur, prev, rows, cs, nxt):
            shifted = [_shift_down(cur, prev, taps - 1 - j) for j in range(taps)]
            conv = w_ref[0:1, cs] * shifted[0]
            for j in range(1, taps):
                conv += w_ref[j:j + 1, cs] * shifted[j]
            y, dsilu = _silu_parts(conv)
            d = d_ref[rows, cs]
            dc = d * g_ref[rows, cs] * dsilu
            dx = w_ref[taps - 1:taps, cs] * dc
            for j in range(taps - 1):
                dx += w_ref[j:j + 1, cs] * _shift_up(dc, nxt, taps - 1 - j)
            return dx, d * y, dc, [dc * s for s in shifted]

        def pair(r0, above_of):
            top, bot = _pair_rows(r0)
            both = pl.ds(r0, PAIR)
            for c0 in range(0, c_w, LANES):
                cs = slice(c0, c0 + LANES)
                cur_t, cur_b = u_ref[top, cs], u_ref[bot, cs]
                dx_b, dg_b, dc_b, dw_b = strip(cur_b, cur_t, bot, cs, below[:, cs])
                dx_t, dg_t, dc_t, dw_t = strip(cur_t, above_of(cs), top, cs, dc_b)
                below[:, cs] = dc_t
                dup_ref[0, both, cs] = jnp.concatenate([dx_t, dx_b], axis=0).astype(dup_ref.dtype)
                dup_ref[1, both, cs] = jnp.concatenate([dg_t, dg_b], axis=0).astype(dup_ref.dtype)
                for j in range(taps):
                    dw_ref[j, :, cs] += dw_t[j] + dw_b[j]

        def loop_body(it, carry):
            r0 = pl.multiple_of((n_pairs - 1 - it) * PAIR, PAIR)
            pair(r0, lambda cs: u_ref[pl.ds(pl.multiple_of(r0 - SUB, SUB), SUB), cs])
            return carry

        lax.fori_loop(0, n_pairs - 1, loop_body, 0, unroll=STRIP_UNROLL)
        pair(0, lambda cs: jnp.where(first_tile, 0.0, halo_ref[:, cs]))

    return pl.pallas_call(
        body,
        name=name,
        grid=(2, lp // tm),
        in_specs=[
            pl.BlockSpec((None, tm, c_w), lambda s, i: (s, last - i, 0)),
            pl.BlockSpec((None, HALO, c_w), lambda s, i: (s, _halo_index(last - i, tm), 0)),
            pl.BlockSpec((None, tm, c_w), lambda s, i: (2 + s, last - i, 0)),
            pl.BlockSpec((None, tm, c_w), lambda s, i: (s, last - i, 0)),
            pl.BlockSpec((None, taps, c_w), lambda s, i: (s, 0, 0)),
        ],
        out_specs=[
            pl.BlockSpec((2, None, tm, c_w), lambda s, i: (0, s, last - i, 0)),
            pl.BlockSpec((None, taps, SUB, c_w), lambda s, i: (s, 0, 0, 0)),
        ],
        out_shape=[jax.ShapeDtypeStruct((2, 2, lp, c_w), BF16), jax.ShapeDtypeStruct((2, taps, SUB, c_w), F32)],
        scratch_shapes=[pltpu.VMEM((SUB, c_w), F32)],
        compiler_params=_params(("arbitrary", "arbitrary")),
    )(up, up, up, dact, conv_w)


def sc_fwd(pb, conv_w, *, tm, cb):
    _, lp, width = pb.shape
    taps = conv_w.shape[0]

    def body(x_ref, halo_ref, w_ref, o_ref):
        first_tile = pl.program_id(1) == 0

        def strip(cur, prev, rows, cs):
            conv = w_ref[taps - 1:taps, cs] * cur
            for j in range(taps - 1):
                conv += w_ref[j:j + 1, cs] * _shift_down(cur, prev, taps - 1 - j)
            return x_ref[0, rows, cs] * conv

        def pair(r0, above_of):
            top, bot = _pair_rows(r0)
            for c0 in range(0, cb, LANES):
                cs = slice(c0, c0 + LANES)
                cur_t = x_ref[1, top, cs] * x_ref[2, top, cs]
                cur_b = x_ref[1, bot, cs] * x_ref[2, bot, cs]
                out = [strip(cur_t, above_of(cs), top, cs), strip(cur_b, cur_t, bot, cs)]
                o_ref[pl.ds(r0, PAIR), cs] = jnp.concatenate(out, axis=0).astype(o_ref.dtype)

        pair(0, lambda cs: jnp.where(first_tile, 0.0, halo_ref[1, :, cs] * halo_ref[2, :, cs]))

        def loop_body(k, carry):
            r0 = pl.multiple_of(k * PAIR, PAIR)
            before = pl.ds(pl.multiple_of(r0 - SUB, SUB), SUB)
            pair(r0, lambda cs: x_ref[1, before, cs] * x_ref[2, before, cs])
            return carry

        lax.fori_loop(1, tm // PAIR, loop_body, 0, unroll=STRIP_UNROLL)

    return pl.pallas_call(
        body,
        name="sc_fwd",
        grid=(width // cb, lp // tm),
        in_specs=[
            pl.BlockSpec((3, tm, cb), lambda j, i: (0, i, j)),
            pl.BlockSpec((3, HALO, cb), lambda j, i: (0, _halo_index(i, tm), j)),
            pl.BlockSpec((taps, cb), lambda j, i: (0, j)),
        ],
        out_specs=pl.BlockSpec((None, tm, cb), lambda j, i: (0, i, j)),
        out_shape=jax.ShapeDtypeStruct((1, lp, width), BF16),
        compiler_params=_params(("arbitrary", "arbitrary")),
    )(pb, pb, conv_w)


def sc_bwd(pb, ds, conv_w, *, tm, cb):
    _, lp, width = pb.shape
    taps = conv_w.shape[0]
    last = lp // tm - 1
    n_pairs = tm // PAIR

    def body(x_ref, halo_ref, d_ref, w_ref, dx_ref, dw_ref, below):
        step = pl.program_id(1)
        first_tile = step == last

        @pl.when(step == 0)
        def _():
            below[...] = jnp.zeros_like(below)
            dw_ref[...] = jnp.zeros_like(dw_ref)

        def strip(cur, prev, rows, cs, nxt):
            gate, left, right = x_ref[0, rows, cs], x_ref[1, rows, cs], x_ref[2, rows, cs]
            shifted = [_shift_down(cur, prev, taps - 1 - j) for j in range(taps)]
            conv = w_ref[0:1, cs] * shifted[0]
            for j in range(1, taps):
                conv += w_ref[j:j + 1, cs] * shifted[j]
            d = d_ref[rows, cs]
            dc = d * gate
            dp = w_ref[taps - 1:taps, cs] * dc
            for j in range(taps - 1):
                dp += w_ref[j:j + 1, cs] * _shift_up(dc, nxt, taps - 1 - j)
            return d * conv, dp * right, dp * left, dc, [dc * s for s in shifted]

        def pair(r0, above_of):
            top, bot = _pair_rows(r0)
            both = pl.ds(r0, PAIR)
            for c0 in range(0, cb, LANES):
                cs = slice(c0, c0 + LANES)
                cur_t = x_ref[1, top, cs] * x_ref[2, top, cs]
                cur_b = x_ref[1, bot, cs] * x_ref[2, bot, cs]
                *dx_b, dc_b, dw_b = strip(cur_b, cur_t, bot, cs, below[:, cs])
                *dx_t, dc_t, dw_t = strip(cur_t, above_of(cs), top, cs, dc_b)
                below[:, cs] = dc_t
                for s in range(3):
                    dx_ref[s, both, cs] = jnp.concatenate([dx_t[s], dx_b[s]], axis=0).astype(dx_ref.dtype)
                for j in range(taps):
                    dw_ref[j, :, cs] += dw_t[j] + dw_b[j]

        def loop_body(it, carry):
            r0 = pl.multiple_of((n_pairs - 1 - it) * PAIR, PAIR)
            before = pl.ds(pl.multiple_of(r0 - SUB, SUB), SUB)
            pair(r0, lambda cs: x_ref[1, before, cs] * x_ref[2, before, cs])
            return carry

        lax.fori_loop(0, n_pairs - 1, loop_body, 0, unroll=STRIP_UNROLL)
        pair(0, lambda cs: jnp.where(first_tile, 0.0, halo_ref[1, :, cs] * halo_ref[2, :, cs]))

    tile_spec = pl.BlockSpec((3, tm, cb), lambda j, i: (0, last - i, j))
    return pl.pallas_call(
        body,
        name="sc_bwd",
        grid=(width // cb, lp // tm),
        in_specs=[
            tile_spec,
            pl.BlockSpec((3, HALO, cb), lambda j, i: (0, _halo_index(last - i, tm), j)),
            pl.BlockSpec((None, tm, cb), lambda j, i: (0, last - i, j)),
            pl.BlockSpec((taps, cb), lambda j, i: (0, j)),
        ],
        out_specs=[tile_spec, pl.BlockSpec((taps, SUB, cb), lambda j, i: (0, 0, j))],
        out_shape=[jax.ShapeDtypeStruct((3, lp, width), BF16), jax.ShapeDtypeStruct((taps, SUB, width), F32)],
        scratch_shapes=[pltpu.VMEM((SUB, cb), F32)],
        compiler_params=_params(("arbitrary", "arbitrary")),
    )(pb, pb, ds, conv_w)


TILE_BYTES = 1536 * 1024


def _rows_tile(rows, cols, multiple=8):
    if rows * cols * 4 <= TILE_BYTES or rows % multiple:
        return rows
    best = multiple
    for t in range(multiple, rows + 1, multiple):
        if rows % t == 0 and t * cols * 4 <= TILE_BYTES:
            best = t
    return best


def pair_sum(g, landed, core, out_dtype, name):
    _, rows, cols = g.shape
    half = rows // 2
    tr = _rows_tile(half, cols, 16)
    nb = half // tr

    def body(c_ref, g_ref, l_ref, o_ref):
        o_ref[...] = (g_ref[...] + l_ref[...]).astype(out_dtype)

    return pl.pallas_call(
        body,
        name=name,
        grid_spec=pltpu.PrefetchScalarGridSpec(
            num_scalar_prefetch=1,
            grid=(4, nb),
            in_specs=[
                pl.BlockSpec((None, tr, cols), lambda s, i, c: (s, c[0] * nb + i, 0)),
                pl.BlockSpec((None, tr, cols), lambda s, i, c: (s, i, 0)),
            ],
            out_specs=pl.BlockSpec((None, tr, cols), lambda s, i, c: (s, i, 0)),
        ),
        out_shape=jax.ShapeDtypeStruct((4, half, cols), out_dtype),
        compiler_params=_params(("arbitrary", "arbitrary")),
    )(core, g, landed)


def chip_sum(x, name):
    _, rows, cols = x.shape
    tr = _rows_tile(rows, cols, 16)

    def body(x0, x1, x2, x3, o_ref):
        acc = x0[...].astype(F32) + x1[...].astype(F32)
        o_ref[...] = (acc + x2[...].astype(F32)) + x3[...].astype(F32)

    return pl.pallas_call(
        body,
        name=name,
        grid=(rows // tr,),
        in_specs=[pl.BlockSpec((None, tr, cols), lambda i, k=k: (k, i, 0)) for k in range(4)],
        out_specs=pl.BlockSpec((tr, cols), lambda i: (i, 0)),
        out_shape=jax.ShapeDtypeStruct((rows, cols), F32),
        compiler_params=_params(("arbitrary",)),
    )(x, x, x, x)


def adamw(w, g, m, v, name):
    shape = w.shape
    cols = shape[-1]
    rows = w.size // cols
    tr = _rows_tile(rows, cols)

    def body(w_ref, g_ref, m_ref, v_ref, d_ref, m2_ref, v2_ref):
        gv = g_ref[...]
        m2 = ADAM_B1 * m_ref[...] + (1.0 - ADAM_B1) * gv
        v2 = ADAM_B2 * v_ref[...] + (1.0 - ADAM_B2) * (gv * gv)
        m_hat = m2 / (1.0 - ADAM_B1 ** ADAM_STEP)
        v_hat = v2 / (1.0 - ADAM_B2 ** ADAM_STEP)
        d_ref[...] = -ADAM_LR * (m_hat / (jnp.sqrt(v_hat) + ADAM_EPS) + ADAM_WD * w_ref[...])
        m2_ref[...] = m2
        v2_ref[...] = v2

    spec = pl.BlockSpec((tr, cols), lambda i: (i, 0))
    outs = pl.pallas_call(
        body,
        name=name,
        grid=(rows // tr,),
        in_specs=[spec] * 4,
        out_specs=[spec] * 3,
        out_shape=[jax.ShapeDtypeStruct((rows, cols), F32)] * 3,
        compiler_params=_params(("arbitrary",)),
    )(*[t.reshape(rows, cols) for t in (w, g, m, v)])
    return tuple(o.reshape(shape) for o in outs)


MESH_ID = pl.DeviceIdType.MESH
ANY = pl.BlockSpec(memory_space=pl.ANY)


def _place():
    x, y, c = lax.axis_index("x"), lax.axis_index("y"), lax.axis_index("c")
    other_chips = [(1 - x, y), (x, 1 - y), (1 - x, 1 - y)]
    return x, y, c, other_chips


def all_gather_shards(bufs, name):
    n = len(bufs)

    def body(*refs):
        x_refs, o_refs = refs[:n], refs[n:2 * n]
        copies = _gather_copies(x_refs, o_refs, *refs[2 * n:])
        _gather_start(copies)
        _gather_finish(copies)

    outs = pl.pallas_call(
        body,
        name=name,
        in_specs=[ANY] * n,
        out_specs=[ANY] * n,
        out_shape=_gather_out_shapes(bufs),
        scratch_shapes=_gather_sems(n),
    )(*bufs)
    return _set_own_slots(outs, bufs)


def _gather_out_shapes(bufs):
    return [jax.ShapeDtypeStruct((4,) + b.shape, b.dtype) for b in bufs]


def _gather_sems(n):
    return [pltpu.SemaphoreType.DMA((6 * n,)), pltpu.SemaphoreType.DMA((6 * n,))]


def _set_own_slots(outs, bufs):
    if not outs:
        return []
    me = 2 * lax.axis_index("x") + lax.axis_index("y")
    return [lax.dynamic_update_index_in_dim(o, b, me, 0) for o, b in zip(outs, bufs)]


def _gather_copies(x_refs, o_refs, send_sems, recv_sems):
    x, y, c, chips = _place()
    me = 2 * x + y
    sibling = (x, y, 1 - c)

    def part(a, slot, hf):
        half = x_refs[a].shape[0] // 2
        return o_refs[a].at[slot, pl.ds(hf * half, half), :]

    def mine(a):
        half = x_refs[a].shape[0] // 2
        return x_refs[a].at[pl.ds(c * half, half), :]

    def copy(k, src, dst, to):
        return pltpu.make_async_remote_copy(src_ref=src, dst_ref=dst, send_sem=send_sems.at[k],
                                            recv_sem=recv_sems.at[k], device_id=to, device_id_type=MESH_ID)

    sends, arrivals, passes, passed = [], [], [], []
    for a in range(len(x_refs)):
        for j, (px, py) in enumerate(chips):
            landed, theirs = part(a, 2 * px + py, c), part(a, 2 * px + py, 1 - c)
            sends.append(copy(6 * a + j, mine(a), part(a, me, c), (px, py, c)))
            arrivals.append(copy(6 * a + j, mine(a), landed, (px, py, c)))
            passes.append(copy(6 * a + 3 + j, landed, landed, sibling))
            passed.append(copy(6 * a + 3 + j, theirs, theirs, sibling))
    return sends, arrivals, passes, passed


def _gather_start(copies):
    for cp in copies[0]:
        cp.start()


def _gather_finish(copies):
    sends, arrivals, passes, passed = copies
    for arrival, cp in zip(arrivals, passes):
        arrival.wait_recv()
        cp.start()
    for cp in passed:
        cp.wait_recv()
    for cp in sends + passes:
        cp.wait_send()


def swap_halves(bufs, name):
    n = len(bufs)

    def body(*refs):
        copies = _swap_copies(refs[:n], refs[n:2 * n], *refs[2 * n:])
        _swap_start(copies)
        _swap_finish(copies)

    return pl.pallas_call(
        body,
        name=name,
        in_specs=[ANY] * n,
        out_specs=[ANY] * n,
        out_shape=_swap_out_shapes(bufs),
        scratch_shapes=_swap_sems(n),
    )(*bufs)


def _swap_out_shapes(bufs):
    return [jax.ShapeDtypeStruct((4, b.shape[1] // 2, b.shape[2]), b.dtype) for b in bufs]


def _swap_sems(n):
    return [pltpu.SemaphoreType.DMA((n,)), pltpu.SemaphoreType.DMA((n,))]


def _swap_copies(x_refs, o_refs, send_sems, recv_sems):
    x, y, c, _ = _place()
    copies = []
    for a, (x_ref, o_ref) in enumerate(zip(x_refs, o_refs)):
        half = x_ref.shape[1] // 2
        copies.append(pltpu.make_async_remote_copy(src_ref=x_ref.at[:, pl.ds((1 - c) * half, half), :], dst_ref=o_ref,
                                                   send_sem=send_sems.at[a], recv_sem=recv_sems.at[a],
                                                   device_id=(x, y, 1 - c), device_id_type=MESH_ID))
    return copies


def _swap_start(copies):
    for cp in copies:
        cp.start()


def _swap_finish(copies):
    for cp in copies:
        cp.wait()


def scatter_to_chips(bufs, name):
    n = len(bufs)

    def body(*refs):
        x_refs, o_refs = refs[:n], refs[n:2 * n]
        copies = _scatter_copies(x_refs, o_refs, *refs[2 * n:])
        _scatter_start(copies)
        _scatter_finish(copies)

    outs = pl.pallas_call(
        body,
        name=name,
        in_specs=[ANY] * n,
        out_specs=[ANY] * n,
        out_shape=[jax.ShapeDtypeStruct(b.shape, b.dtype) for b in bufs],
        scratch_shapes=_scatter_sems(n),
    )(*bufs)
    return _keep_own_slots(outs, bufs)


def _scatter_sems(n):
    return [pltpu.SemaphoreType.DMA((3 * n,)), pltpu.SemaphoreType.DMA((3 * n,))]


def _keep_own_slots(outs, bufs):
    if not outs:
        return []
    me = 2 * lax.axis_index("x") + lax.axis_index("y")
    return [lax.dynamic_update_index_in_dim(o, lax.dynamic_index_in_dim(b, me, 0, keepdims=False), me, 0)
            for o, b in zip(outs, bufs)]


def _scatter_copies(x_refs, o_refs, send_sems, recv_sems):
    x, y, c, chips = _place()
    me = 2 * x + y

    def copy(a, j, src_slot, dst_slot, px, py):
        return pltpu.make_async_remote_copy(src_ref=x_refs[a].at[src_slot], dst_ref=o_refs[a].at[dst_slot],
                                            send_sem=send_sems.at[3 * a + j], recv_sem=recv_sems.at[3 * a + j],
                                            device_id=(px, py, c), device_id_type=MESH_ID)

    sends = [copy(a, j, 2 * px + py, me, px, py) for a in range(len(x_refs)) for j, (px, py) in enumerate(chips)]
    arrivals = [copy(a, j, me, 2 * px + py, px, py) for a in range(len(x_refs)) for j, (px, py) in enumerate(chips)]
    return sends, arrivals


def _scatter_start(copies):
    for cp in copies[0]:
        cp.start()


def _scatter_finish(copies):
    for cp in copies[1]:
        cp.wait_recv()
    for cp in copies[0]:
        cp.wait_send()


def share_halves(groups, name):
    bufs = [b for grp in groups for b in grp]
    where = [(gi, li) for gi, grp in enumerate(groups) for li in range(len(grp))]
    n = len(bufs)

    def body(*refs):
        x_refs, o_refs = refs[:n], refs[n:n + len(groups)]
        send_sems, recv_sems = refs[n + len(groups):]
        x, y, c, _ = _place()
        sent, arrive = [], []
        for a, (gi, li) in enumerate(where):

            def copy(hf, a=a, gi=gi, li=li):
                return pltpu.make_async_remote_copy(src_ref=x_refs[a], dst_ref=o_refs[gi].at[li, hf],
                                                    send_sem=send_sems.at[a], recv_sem=recv_sems.at[a],
                                                    device_id=(x, y, 1 - c), device_id_type=MESH_ID)

            sent.append(copy(c))
            arrive.append(copy(1 - c))
        for cp in sent:
            cp.start()
        for cp in arrive:
            cp.wait_recv()
        for cp in sent:
            cp.wait_send()

    outs = pl.pallas_call(
        body,
        name=name,
        in_specs=[ANY] * n,
        out_specs=[ANY] * len(groups),
        out_shape=[jax.ShapeDtypeStruct((len(grp), 2) + grp[0].shape, grp[0].dtype) for grp in groups],
        scratch_shapes=[pltpu.SemaphoreType.DMA((n,)), pltpu.SemaphoreType.DMA((n,))],
    )(*bufs)
    c = lax.axis_index("c")
    full = [lax.dynamic_update_index_in_dim(o, jnp.stack(grp), c, 1) for o, grp in zip(outs, groups)]
    return [t.reshape(t.shape[0], 2 * t.shape[2], t.shape[3]) for t in full]


def pair_sums(bufs, landed, dtypes, tag):
    core = lax.axis_index("c").astype(jnp.int32).reshape(1)
    return [pair_sum(b, l, core, dt, "rs_pair_sum_%s%d" % (tag, i)) for i, (b, l, dt) in enumerate(zip(bufs, landed, dtypes))]


def _row_tiles(length):
    return (640, 640) if length > 2048 else (128, 64)


def _divisor_tile(rows, target):
    return max(t for t in range(8, min(rows, target) + 1, 8) if rows % t == 0)


def _local_step(x, target, wt, late_shards, layout_late, complete_grads, sum_pairs):
    seq, d = x.shape
    length = N_META + seq
    tm, tm_ffn = _row_tiles(length)
    lp = -(-length // tm) * tm
    tail = jnp.zeros((lp - length, d), F32)
    h0 = jnp.concatenate([wt["meta"], x, tail], axis=0)[None]
    tgt = jnp.concatenate([jnp.zeros((N_META, d), F32), target, tail], axis=0)
    nn = functools.partial(mm_nn, tm=_divisor_tile(lp, 2080))
    nt = functools.partial(mm_nt, tm=_divisor_tile(lp, 1664))
    tn = functools.partial(mm_tn, tm=_divisor_tile(lp, 2080), rb=256)
    nn_ln = functools.partial(mm_nn_ln, tm=_divisor_tile(lp, 832))
    nt_ln_bwd = functools.partial(mm_nt_ln_bwd, tm=_divisor_tile(lp, 1040))
    ln_g = [wt["ln_mix_g"][0:1], wt["ln_ffn_g"][0:1], wt["ln_mix_g"][1:2], wt["ln_ffn_g"][1:2]]
    ln_b = [wt["ln_mix_b"][0:1], wt["ln_ffn_b"][0:1], wt["ln_mix_b"][1:2], wt["ln_ffn_b"][1:2]]

    h0b = h0.astype(BF16)
    p3 = nn(h0b, wt["a3"], name="a_in3")
    pz = nn(h0b, wt["az"], name="a_inz")
    pba = nn(h0b, wt["a_ba"], name="a_inba")
    qkv = gdn_pre_fwd(p3, wt["a_conv3"], tm=tm, cb=2 * HEAD_DIM)
    gates = gdn_gates_fwd(pba, wt["alog_lanes"], wt["dtb_lanes"], tm=tm)
    o, states, tinv, late_stacks = gdn_chunk_fwd(qkv, gates, late_shards)
    wt = {**wt, **layout_late(late_stacks)}
    onz = gdn_post_fwd(o[None], pz, wt["anorm_b"], tm=tm)
    r1, h1, h1b = nn_ln(onz, wt["a_out"], h0, ln_g[0], ln_b[0], name="a_out_ln1")
    up0 = nn(h1b, wt["up"][0], name="up0")
    act0 = ffn_act_fwd(up0, wt["fconv"][0], tm=tm_ffn, name="ffn_act0")
    r2, h2, h2b = nn_ln(act0, wt["down"][0], h1, ln_g[1], ln_b[1], name="down0_ln2")
    pb = nn(h2b, wt["b_in"], name="b_in")
    sc = sc_fwd(pb, wt["b_conv"], tm=tm_ffn, cb=d)
    r3, h3, h3b = nn_ln(sc, wt["b_out"], h2, ln_g[2], ln_b[2], name="b_out_ln3")
    up1 = nn(h3b, wt["up"][1], name="up1")
    act1 = ffn_act_fwd(up1, wt["fconv"][1], tm=tm_ffn, name="ffn_act1")
    r4, h4, _ = nn_ln(act1, wt["down"][1], h3, ln_g[3], ln_b[3], name="down1_ln4")

    grads = {}
    dr4, dgb4, loss_part = loss_ln_bwd(h4, tgt, r4, ln_g[3], first=N_META, count=seq, tm=tm)
    d_down1 = tn(act1, dr4, name="d_down1")
    dact1 = nt(dr4, wt["down"][1], name="d_act1")
    dup1, dfconv1 = ffn_act_bwd(up1, dact1, wt["fconv"][1], tm=tm_ffn, name="ffn_act1_bwd")
    dup1 = dup1.reshape(up1.shape)
    d_up1 = tn(h3b, dup1, name="d_up1")

    dr3, dgb3, _ = nt_ln_bwd(dup1, wt["up"][1], dr4, r3, ln_g[2], name="d_h3_ln3")
    d_bout = tn(sc, dr3, name="d_b_out")
    dsc = nt(dr3, wt["b_out"], name="d_sc")
    dpb, dbconv = sc_bwd(pb, dsc, wt["b_conv"], tm=tm_ffn, cb=d)
    d_bin = tn(h2b, dpb, name="d_b_in")

    dr2, dgb2, _ = nt_ln_bwd(dpb, wt["b_in"], dr3, r2, ln_g[1], name="d_h2_ln2")
    d_down0 = tn(act0, dr2, name="d_down0")
    dact0 = nt(dr2, wt["down"][0], name="d_act0")
    dup0, dfconv0 = ffn_act_bwd(up0, dact0, wt["fconv"][0], tm=tm_ffn, name="ffn_act0_bwd")
    dup0 = dup0.reshape(up0.shape)
    d_up0 = tn(h1b, dup0, name="d_up0")
    grads["b_w_in"] = [d_bin[0].transpose(1, 0, 2).reshape(d, 4, 3 * d // 4).transpose(1, 0, 2)]
    grads["b_w_out"] = [d_bout.reshape(4, d // 4, d)]
    grads["ffn_w_up"] = [d_up0[0], d_up1[0]]
    grads["ffn_w_down"] = [t.reshape(4, -1, d) for t in (d_down0, d_down1)]
    complete = complete_grads(grads)

    dr1, dgb1, from_sibling = nt_ln_bwd(dup0, wt["up"][0], dr2, r1, ln_g[0], name="d_h1_ln1", swap=complete)
    leaving = sum_pairs(complete, from_sibling)
    d_aout = tn(onz, dr1, name="d_a_out")
    donz = nt(dr1, wt["a_out"], name="d_onz")
    d_o, dz, dnw = gdn_post_bwd(o[None], pz, donz, wt["anorm_b"], tm=tm)
    dqkv, dgates, landed = gdn_chunk_bwd(qkv, gates, states, tinv, d_o[0], leaving)
    dp3, daconv = gdn_pre_bwd(p3, dqkv, wt["a_conv3"], tm=tm, cb=2 * HEAD_DIM)
    dpba, dscal = gdn_gates_bwd(pba, dgates, wt["alog_lanes"], wt["dtb_lanes"], tm=tm)
    d_a3 = tn(h0b, dp3, name="d_a_in3")
    d_az = tn(h0b, dz, name="d_a_inz")
    d_aba = tn(h0b, dpba, name="d_a_inba")
    dh0 = nt(dp3, wt["a3"], res=dr1, res_scale=ALPHA, name="d_h0a")
    dh0 = nt(dz, wt["az"], res=dh0, res_scale=1.0, name="d_h0z")
    dh0 = nt(dpba, wt["a_ba"], res=dh0, res_scale=1.0, name="d_h0")

    width = HEADS * HEAD_DIM
    d_a_in = jnp.concatenate([d_a3[0, 0], d_a3[0, 1], d_a3[0, 2], d_az[0, 0], d_aba[0, 0][:, :2 * HEADS]], axis=1)
    n_in = d_a_in.shape[1] // 4
    grads["a_w_in"] = [d_a_in.reshape(d, 4, n_in).transpose(1, 0, 2)]
    grads["a_w_out"] = [d_aout.reshape(4, width // 4, d)]
    grads["a_conv"] = daconv.sum(axis=2).transpose(1, 0, 2).reshape(1, GDN_CONV, 3 * width)
    per_head = dscal.sum(axis=1)[:, HEADS:2 * HEADS]
    grads["a_log"] = per_head[0][None]
    grads["a_dt_bias"] = per_head[1][None]
    grads["a_norm"] = dnw.reshape(8, HEADS, HEAD_DIM).sum(axis=(0, 1))[None]
    grads["b_conv"] = dbconv.sum(axis=1)[None]
    lns = [dgb1, dgb2, dgb3, dgb4]
    grads["ln_mix_g"] = jnp.stack([lns[0][0].sum(0), lns[2][0].sum(0)])
    grads["ln_mix_b"] = jnp.stack([lns[0][1].sum(0), lns[2][1].sum(0)])
    grads["ln_ffn_g"] = jnp.stack([lns[1][0].sum(0), lns[3][0].sum(0)])
    grads["ln_ffn_b"] = jnp.stack([lns[1][1].sum(0), lns[3][1].sum(0)])
    grads["ffn_conv"] = jnp.stack([t.sum(axis=2).transpose(1, 0, 2).reshape(FFN_CONV, -1) for t in (dfconv0, dfconv1)])
    grads["meta"] = dh0[0, :N_META]
    return loss_part, dh0, grads, landed


WEIGHTS = ["meta", "a_w_in", "a_conv", "a_log", "a_dt_bias", "a_norm", "a_w_out", "b_w_in", "b_conv", "b_w_out",
           "ln_mix_g", "ln_mix_b", "ffn_w_up", "ffn_conv", "ffn_w_down", "ln_ffn_g", "ln_ffn_b"]
EARLY_WEIGHTS = ["a_w_in", "a_w_out"]
LATE_WEIGHTS = ["b_w_in", "b_w_out", "ffn_w_up", "ffn_w_down"]
MATMUL_WEIGHTS = EARLY_WEIGHTS + LATE_WEIGHTS
SMALL_SHARDED = ["a_conv", "b_conv", "ffn_conv", "meta"]
REPLICATED = ["a_log", "a_dt_bias", "a_norm", "ln_mix_g", "ln_mix_b", "ln_ffn_g", "ln_ffn_b"]
SHARD_AXIS = {"meta": 1, "a_w_in": 2, "a_conv": 2, "a_w_out": 1, "b_w_in": 2, "b_conv": 2, "b_w_out": 1,
              "ffn_w_up": 2, "ffn_conv": 2, "ffn_w_down": 1}
PACK_COLS = 1024
PACK_ROWS_MULTIPLE = 32


def _pack(pieces, lead=()):
    flat = jnp.concatenate([p.reshape(lead + (-1,)) for p in pieces], axis=-1)
    n = flat.shape[-1]
    rows = -(-n // (PACK_COLS * PACK_ROWS_MULTIPLE)) * PACK_ROWS_MULTIPLE
    flat = jnp.pad(flat, [(0, 0)] * len(lead) + [(0, rows * PACK_COLS - n)])
    return flat.reshape(lead + (rows, PACK_COLS))


def _unpack(buf, shapes, lead=()):
    flat = buf.reshape(lead + (-1,))
    out, off = [], 0
    for shp in shapes:
        n = 1
        for s in shp:
            n *= s
        out.append(flat[..., off:off + n].reshape(lead + tuple(shp)))
        off += n
    return out


def _join_shards(stacked, axis):
    return jnp.concatenate([stacked[k] for k in range(4)], axis=axis)


def _split_shards(full, axis):
    return jnp.stack(jnp.split(full, 4, axis=axis))


def _weight_layers(w, names):
    return [w[n][l].astype(BF16) for n in names for l in range(w[n].shape[0])]


def _per_weight(arrays, w, names):
    it = iter(arrays)
    return {n: [next(it) for _ in range(w[n].shape[0])] for n in names}


def _layout_early(full, w):
    width = HEADS * HEAD_DIM
    wt = {n: w[n] for n in ("ln_mix_g", "ln_mix_b", "ln_ffn_g", "ln_ffn_b")}
    w_in = _join_shards(full["a_w_in"][0], 1)
    d = w_in.shape[0]
    n_ff = full["ffn_conv"].shape[2] // 2
    blocks = [w_in[:, s * width:(s + 1) * width] for s in range(4)]
    wt["a3"] = jnp.stack(blocks[:3])[None]
    wt["az"] = blocks[3][None, None]
    wt["a_ba"] = jnp.pad(w_in[:, 4 * width:], ((0, 0), (0, HEAD_DIM - 2 * HEADS)))[None, None]
    wt["a_out"] = full["a_w_out"][0].reshape(1, 1, width, d)
    wt["a_conv3"] = full["a_conv"][0].reshape(GDN_CONV, 3, width).transpose(1, 0, 2)
    wt["b_conv"] = full["b_conv"][0]
    wt["fconv"] = [full["ffn_conv"][l].reshape(FFN_CONV, 2, n_ff).transpose(1, 0, 2) for l in range(2)]
    wt["meta"] = full["meta"]
    in_g_lanes = (HEADS, HEAD_DIM - 2 * HEADS)
    wt["alog_lanes"] = jnp.pad(w["a_log"][0], in_g_lanes)[None]
    wt["dtb_lanes"] = jnp.pad(w["a_dt_bias"][0], in_g_lanes)[None]
    wt["anorm_b"] = jnp.tile(w["a_norm"][0], HEADS)[None]
    return wt


def _layout_late(full):
    d = full["b_w_in"][0].shape[1]
    n_ff = full["ffn_w_up"][0].shape[2]
    return {
        "b_in": _join_shards(full["b_w_in"][0], 1).reshape(d, 3, d).transpose(1, 0, 2)[None],
        "b_out": full["b_w_out"][0].reshape(1, 1, d, d),
        "up": [t[None] for t in full["ffn_w_up"]],
        "down": [t.reshape(2, 1, n_ff, d) for t in full["ffn_w_down"]],
    }


def kernel(x, meta, a_w_in, a_conv, a_log, a_dt_bias, a_norm, a_w_out, b_w_in, b_conv, b_w_out, ln_mix_g, ln_mix_b, ffn_w_up, ffn_conv, ffn_w_down, ln_ffn_g, ln_ffn_b, loss_target, m_meta, m_a_w_in, m_a_conv, m_a_log, m_a_dt_bias, m_a_norm, m_a_w_out, m_b_w_in, m_b_conv, m_b_w_out, m_ln_mix_g, m_ln_mix_b, m_ffn_w_up, m_ffn_conv, m_ffn_w_down, m_ln_ffn_g, m_ln_ffn_b, v_meta, v_a_w_in, v_a_conv, v_a_log, v_a_dt_bias, v_a_norm, v_a_w_out, v_b_w_in, v_b_conv, v_b_w_out, v_ln_mix_g, v_ln_mix_b, v_ffn_w_up, v_ffn_conv, v_ffn_w_down, v_ln_ffn_g, v_ln_ffn_b):
    w = dict(meta=meta, a_w_in=a_w_in, a_conv=a_conv, a_log=a_log, a_dt_bias=a_dt_bias, a_norm=a_norm, a_w_out=a_w_out,
             b_w_in=b_w_in, b_conv=b_conv, b_w_out=b_w_out, ln_mix_g=ln_mix_g, ln_mix_b=ln_mix_b, ffn_w_up=ffn_w_up,
             ffn_conv=ffn_conv, ffn_w_down=ffn_w_down, ln_ffn_g=ln_ffn_g, ln_ffn_b=ln_ffn_b)
    m = dict(meta=m_meta, a_w_in=m_a_w_in, a_conv=m_a_conv, a_log=m_a_log, a_dt_bias=m_a_dt_bias, a_norm=m_a_norm,
             a_w_out=m_a_w_out, b_w_in=m_b_w_in, b_conv=m_b_conv, b_w_out=m_b_w_out, ln_mix_g=m_ln_mix_g,
             ln_mix_b=m_ln_mix_b, ffn_w_up=m_ffn_w_up, ffn_conv=m_ffn_conv, ffn_w_down=m_ffn_w_down,
             ln_ffn_g=m_ln_ffn_g, ln_ffn_b=m_ln_ffn_b)
    v = dict(meta=v_meta, a_w_in=v_a_w_in, a_conv=v_a_conv, a_log=v_a_log, a_dt_bias=v_a_dt_bias, a_norm=v_a_norm,
             a_w_out=v_a_w_out, b_w_in=v_b_w_in, b_conv=v_b_conv, b_w_out=v_b_w_out, ln_mix_g=v_ln_mix_g,
             ln_mix_b=v_ln_mix_b, ffn_w_up=v_ffn_w_up, ffn_conv=v_ffn_conv, ffn_w_down=v_ffn_w_down,
             ln_ffn_g=v_ln_ffn_g, ln_ffn_b=v_ln_ffn_b)
    seq = x.shape[1]
    *stacks, small = all_gather_shards(_weight_layers(w, EARLY_WEIGHTS) + [_pack([w[n] for n in SMALL_SHARDED])],
                                       "gather_early")
    full = _per_weight(stacks, w, EARLY_WEIGHTS)
    for n, t in zip(SMALL_SHARDED, _unpack(small, [w[n].shape for n in SMALL_SHARDED], lead=(4,))):
        full[n] = _join_shards(t, SHARD_AXIS[n])

    def layout_late(late_stacks):
        return _layout_late(_per_weight(late_stacks, w, LATE_WEIGHTS))

    def complete_grads(grads):
        return [g for n in LATE_WEIGHTS for g in grads[n]]

    def sum_pairs(bufs, from_sibling):
        return pair_sums(bufs, from_sibling, [BF16] * len(bufs), "late")

    loss_part, dh0, grads, landed_late = _local_step(x[0], loss_target[0], _layout_early(full, w),
                                                     _weight_layers(w, LATE_WEIGHTS), layout_late, complete_grads, sum_pairs)
    pieces = [_split_shards(grads[n], SHARD_AXIS[n]) for n in SMALL_SHARDED]
    same = jnp.concatenate([grads[n].reshape(-1) for n in REPLICATED] + [jnp.sum(loss_part).reshape(1)])
    pieces.append(jnp.broadcast_to(same, (4,) + same.shape))
    bufs = [g for n in EARLY_WEIGHTS for g in grads[n]] + [_pack(pieces, lead=(4,))]
    from_sibling = swap_halves(bufs, "rs_pair_early")
    landed = scatter_to_chips(pair_sums(bufs, from_sibling, [BF16] * (len(bufs) - 1) + [F32], "early"), "rs_chips_early")
    totals = [chip_sum(t, "rs_chip_sum%d" % i) for i, t in enumerate(landed + landed_late)]
    by_weight = _per_weight(totals[:len(bufs) - 1] + totals[len(bufs):], w, MATMUL_WEIGHTS)
    *shared, small_total = share_halves([by_weight[n] for n in MATMUL_WEIGHTS] + [[totals[len(bufs) - 1]]], "rs_share")
    grad_w = {n: t.reshape(w[n].shape) for n, t in zip(MATMUL_WEIGHTS, shared)}
    rest = SMALL_SHARDED + REPLICATED
    unpacked = _unpack(small_total[0], [w[n].shape for n in rest] + [()])
    grad_w.update(zip(rest, unpacked[:-1]))
    loss = unpacked[-1]
    grad_x = dh0[:, N_META:N_META + seq]
    steps = [adamw(w[n], grad_w[n], m[n], v[n], "adamw_" + n) for n in WEIGHTS]
    return (loss, grad_x, *[grad_w[n] for n in WEIGHTS], *[s[0] for s in steps], *[s[1] for s in steps],
            *[s[2] for s in steps])
```

```python
import functools

import jax
import jax.numpy as jnp
from jax import lax
from jax.experimental import pallas as pl
from jax.experimental.pallas import tpu as pltpu

F32 = jnp.float32
BF16 = jnp.bfloat16

N_META = 16
HEADS = 8
HEAD_DIM = 128
CHUNK = 64
GDN_CONV = 4
FFN_CONV = 3
ALPHA = 4.0 ** 0.25
LN_EPS = 1e-5
RMS_EPS = 1e-6
L2_EPS = 1e-6
Q_SCALE = HEAD_DIM ** -0.5

ADAM_LR = 0.001
ADAM_B1 = 0.9
ADAM_B2 = 0.999
ADAM_EPS = 1e-08
ADAM_WD = 0.01
ADAM_STEP = 10

HALO = 8
VMEM_LIMIT = 48 * 1024 * 1024


def _params(sem=None):
    return pltpu.CompilerParams(dimension_semantics=sem, vmem_limit_bytes=VMEM_LIMIT)


def _dot(a, b, prec=None):
    return jnp.dot(a, b, preferred_element_type=F32, precision=prec)


def _dot_nt(a, b, prec=None):
    return lax.dot_general(a, b, (((1,), (1,)), ((), ())), preferred_element_type=F32, precision=prec)


def _dot_tn(a, b, prec=None):
    return lax.dot_general(a, b, (((0,), (0,)), ((), ())), preferred_element_type=F32, precision=prec)


def _sigmoid(x):
    return 0.5 * jnp.tanh(0.5 * x) + 0.5


def _tri_masks():
    r = lax.broadcasted_iota(jnp.int32, (CHUNK, CHUNK), 0)
    c = lax.broadcasted_iota(jnp.int32, (CHUNK, CHUNK), 1)
    return r >= c, r > c, r == c


def _split_hi_lo(x):
    hi = x.astype(BF16)
    return hi, (x - hi.astype(F32)).astype(BF16)


def _mask_dot(mask, x):
    hi, lo = _split_hi_lo(x)
    return _dot(mask, hi) + _dot(mask, lo)


def _cumsum_rows(g):
    causal, _, _ = _tri_masks()
    return _mask_dot(causal.astype(BF16), g)


def _cumsum_rows_transposed(dy):
    _, strict, _ = _tri_masks()
    return _mask_dot((~strict).astype(BF16), dy)


def _dot_split3(a, b):
    a_hi, a_lo = _split_hi_lo(a)
    b_hi, b_lo = _split_hi_lo(b)
    return _dot(a_hi, b_hi) + (_dot(a_hi, b_lo) + _dot(a_lo, b_hi))


@jax.custom_vjp
def _dot_precise(a, b):
    return _dot_split3(a, b)


def _dot_precise_fwd(a, b):
    return _dot_split3(a, b), (a, b)


def _dot_precise_bwd(operands, ct):
    a, b = operands
    return _dot_split3(ct, b.T), _dot_split3(a.T, ct)


_dot_precise.defvjp(_dot_precise_fwd, _dot_precise_bwd)


def _gdn_m(ks, a64s, bbs):
    causal, strict, _ = _tri_masks()
    decay = [jnp.exp(jnp.where(causal, x - x.T, -1e30)) for x in a64s]
    kk = [_dot_nt(k * b, k) for k, b in zip(ks, bbs)]
    return [jnp.where(strict, x * d, 0.0) for x, d in zip(kk, decay)]


def _gdn_inverse_stages(ks, a64s, bbs):
    ms = _gdn_m(ks, a64s, bbs)
    yield
    r = lax.broadcasted_iota(jnp.int32, (CHUNK, CHUNK), 0)
    c = lax.broadcasted_iota(jnp.int32, (CHUNK, CHUNK), 1)
    eye = (r == c).astype(F32)
    same = [jnp.right_shift(r, s) == jnp.right_shift(c, s) for s in (3, 4, 5)]
    d = [jnp.where(same[0], m, 0.0) for m in ms]
    p = [_dot(x, x) for x in d]
    yield
    t = [eye - x for x in d]
    t = [x + _dot(x, y) for x, y in zip(t, p)]
    p = [_dot(x, x) for x in p]
    yield
    t = [x + _dot(x, y) for x, y in zip(t, p)]
    yield
    for inner, outer in ((same[0], same[1]), (same[1], same[2]), (same[2], None)):
        joins = ~inner if outer is None else (outer & ~inner)
        o = [_dot(x, jnp.where(joins, m, 0.0)) for x, m in zip(t, ms)]
        yield
        t = [x - _dot(y, x) for x, y in zip(t, o)]
        yield
    res = [eye - x - _dot_split3(m, x) for m, x in zip(ms, t)]
    yield
    return [x + _dot(x, y) for x, y in zip(t, res)]


def _gdn_apply_stages(qs, ks, vs, gc, a64s, gl, bbs, ss, ts):
    causal, _, _ = _tri_masks()
    n = range(len(qs))
    qk = [_dot_nt(qs[h], ks[h]) for h in n]
    yield
    decay = [jnp.exp(jnp.where(causal, x - x.T, -1e30)) for x in a64s]
    eg = [jnp.exp(x) for x in gc]
    u = [_dot_precise(ts[h], vs[h] * bbs[h]) for h in n]
    w = [_dot_precise(ts[h], ks[h] * bbs[h] * eg[h]) for h in n]
    qk = [qk[h] * decay[h] for h in n]
    kd = [ks[h] * jnp.exp(gl[h] - gc[h]) for h in n]
    yield
    v_new = [u[h] - _dot(w[h], ss[h]) for h in n]
    q_s = [_dot(qs[h] * eg[h], ss[h]) for h in n]
    yield
    o = [q_s[h] + _dot(qk[h], v_new[h]) for h in n]
    s2 = [ss[h] * jnp.exp(gl[h]) + _dot_tn(kd[h], v_new[h]) for h in n]
    return o, s2


def _run_stages(*generators):
    results = [None] * len(generators)
    live = dict(enumerate(generators))
    while live:
        for i, gen in list(live.items()):
            try:
                next(gen)
            except StopIteration as stop:
                results[i] = stop.value
                del live[i]
    return results


def _head_slices(h):
    return slice(h * HEAD_DIM, (h + 1) * HEAD_DIM), slice(h * HEAD_DIM, h * HEAD_DIM + CHUNK)


def _gdn_head_values(x_ref, gate_ref):
    heads = range(HEADS)
    qs, ks, vs = ([x_ref[s, :, _head_slices(h)[0]] for h in heads] for s in range(3))
    gate = gate_ref[...]
    cumulative = _cumsum_rows(gate)
    total = jnp.sum(gate, axis=0, keepdims=True)
    gcums = [cumulative[:, HEADS + h:HEADS + h + 1] for h in heads]
    gtots = [total[:, HEADS + h:HEADS + h + 1] for h in heads]
    bcols = [gate[:, h:h + 1] for h in heads]
    return qs, ks, vs, gcums, gtots, bcols


def _over_lanes(cols, lanes):
    return [jnp.broadcast_to(c, (c.shape[0], lanes)) for c in cols]


def _gdn_inverse_cols(ks, gcums, bcols):
    return _gdn_inverse_stages(ks, _over_lanes(gcums, CHUNK), _over_lanes(bcols, HEAD_DIM))


def _gdn_apply_cols_stages(qs, ks, vs, gcums, gtots, bcols, ss, ts):
    return _gdn_apply_stages(qs, ks, vs, _over_lanes(gcums, HEAD_DIM), _over_lanes(gcums, CHUNK),
                             _over_lanes(gtots, HEAD_DIM), _over_lanes(bcols, HEAD_DIM), ss, ts)


def _gdn_apply_cols(qs, ks, vs, gcums, gtots, bcols, ss, ts):
    return _run_stages(_gdn_apply_cols_stages(qs, ks, vs, gcums, gtots, bcols, ss, ts))[0]


def _gdn_m_cols(ks, gcums, bcols):
    return _gdn_m(ks, _over_lanes(gcums, CHUNK), _over_lanes(bcols, HEAD_DIM))


def _gate_lanes(bcols, gcols):
    rows = gcols[0].shape[0]
    lane = lax.broadcasted_iota(jnp.int32, (rows, HEAD_DIM), 1)
    out = jnp.zeros((rows, HEAD_DIM), F32)
    for h in range(HEADS):
        if bcols is not None:
            out = jnp.where(lane == h, jnp.broadcast_to(bcols[h], out.shape), out)
        out = jnp.where(lane == HEADS + h, jnp.broadcast_to(gcols[h], out.shape), out)
    return out


def _gate_gradient(dbcols, dgcums, dgtots):
    block = _gate_lanes(dbcols, dgcums)
    lane = lax.broadcasted_iota(jnp.int32, block.shape, 1)
    return jnp.where(lane < HEADS, block, _cumsum_rows_transposed(block) + _gate_lanes(None, dgtots))


def gdn_chunk_fwd(qkv, gates, gather=()):
    _, lp, width = qkv.shape
    n_chunks = lp // CHUNK
    n = len(gather)

    def body(x_ref, gate_ref, next_ref, next_gate_ref, *refs):
        shard_refs, (o_ref, s_ref, t_ref), refs = refs[:n], refs[n:n + 3], refs[n + 3:]
        stack_refs, state, t_next, sems = refs[:n], refs[n], refs[n + 1], refs[n + 2:]
        copies = _gather_copies(shard_refs, stack_refs, *sems) if n else None

        def inverse_stages(ref, g_ref):
            _, ks, _, gcums, _, bcols = _gdn_head_values(ref, g_ref)
            return _gdn_inverse_cols(ks, gcums, bcols)

        @pl.when(pl.program_id(0) == 0)
        def _():
            state[...] = jnp.zeros_like(state)
            for h, t in enumerate(_run_stages(inverse_stages(x_ref, gate_ref))[0]):
                t_next[h] = t
            if n:
                _gather_start(copies)

        qs, ks, vs, gcums, gtots, bcols = _gdn_head_values(x_ref, gate_ref)
        ss = [state[h] for h in range(HEADS)]
        ts = [t_next[h] for h in range(HEADS)]
        ts_next, (os_, s2) = _run_stages(inverse_stages(next_ref, next_gate_ref),
                                         _gdn_apply_cols_stages(qs, ks, vs, gcums, gtots, bcols, ss, ts))
        for h in range(HEADS):
            s_ref[0, h] = ss[h]
            t_ref[0, h] = ts[h]
            t_next[h] = ts_next[h]
            o_ref[:, _head_slices(h)[0]] = os_[h]
            state[h] = s2[h]

        if n:
            @pl.when(pl.program_id(0) == n_chunks - 1)
            def _():
                _gather_finish(copies)

    o, states, tinv, *stacks = pl.pallas_call(
        body,
        name="gdn_chunk_fwd",
        grid=(n_chunks,),
        in_specs=[pl.BlockSpec((3, CHUNK, width), lambda c: (0, c, 0)),
                  pl.BlockSpec((CHUNK, HEAD_DIM), lambda c: (c, 0)),
                  pl.BlockSpec((3, CHUNK, width), lambda c: (0, jnp.minimum(c + 1, n_chunks - 1), 0)),
                  pl.BlockSpec((CHUNK, HEAD_DIM), lambda c: (jnp.minimum(c + 1, n_chunks - 1), 0))] + [ANY] * n,
        out_specs=[
            pl.BlockSpec((CHUNK, width), lambda c: (c, 0)),
            pl.BlockSpec((1, HEADS, HEAD_DIM, HEAD_DIM), lambda c: (c, 0, 0, 0)),
            pl.BlockSpec((1, HEADS, CHUNK, CHUNK), lambda c: (c, 0, 0, 0)),
        ] + [ANY] * n,
        out_shape=[
            jax.ShapeDtypeStruct((lp, width), F32),
            jax.ShapeDtypeStruct((n_chunks, HEADS, HEAD_DIM, HEAD_DIM), F32),
            jax.ShapeDtypeStruct((n_chunks, HEADS, CHUNK, CHUNK), F32),
        ] + _gather_out_shapes(gather),
        scratch_shapes=[pltpu.VMEM((HEADS, HEAD_DIM, HEAD_DIM), F32), pltpu.VMEM((HEADS, CHUNK, CHUNK), F32)]
        + (_gather_sems(n) if n else []),
        compiler_params=_params(("arbitrary",)),
    )(qkv, gates, qkv, gates, *gather)
    return o, states, tinv, _set_own_slots(stacks, gather)


def gdn_chunk_bwd(qkv, gates, states, tinv, d_o, scatter=()):
    _, lp, width = qkv.shape
    n_chunks = lp // CHUNK
    last = n_chunks - 1
    n = len(scatter)

    def body(x_ref, gate_ref, s_ref, t_ref, do_ref, *refs):
        leaving_refs, dx_ref, dgate_ref, refs = refs[:n], refs[n], refs[n + 1], refs[n + 2:]
        landing_refs, dstate, sems = refs[:n], refs[n], refs[n + 1:]
        copies = _scatter_copies(leaving_refs, landing_refs, *sems) if n else None

        @pl.when(pl.program_id(0) == 0)
        def _():
            dstate[...] = jnp.zeros_like(dstate)
            if n:
                _scatter_start(copies)

        heads = range(HEADS)
        qs, ks, vs, gcums, gtots, bcols = _gdn_head_values(x_ref, gate_ref)
        ss = [s_ref[0, h] for h in heads]
        ts = [t_ref[0, h] for h in heads]
        d_out = ([do_ref[:, _head_slices(h)[0]] for h in heads], [dstate[h] for h in heads])
        _, vjp_apply = jax.vjp(_gdn_apply_cols, qs, ks, vs, gcums, gtots, bcols, ss, ts)
        dq, dk, dv, dgc, dgt, db, ds, dt = vjp_apply(d_out)
        tts = [t.T for t in ts]
        dm = [_dot(tts[h], dt[h]) for h in heads]
        dm = [-_dot(dm[h], tts[h]) for h in heads]
        _, vjp_m = jax.vjp(_gdn_m_cols, ks, gcums, bcols)
        dk2, dgc2, db2 = vjp_m(dm)
        for h in heads:
            sl = _head_slices(h)[0]
            dx_ref[0, :, sl] = dq[h]
            dx_ref[1, :, sl] = dk[h] + dk2[h]
            dx_ref[2, :, sl] = dv[h]
            dstate[h] = ds[h]
        dgate_ref[...] = _gate_gradient([db[h] + db2[h] for h in heads], [dgc[h] + dgc2[h] for h in heads], dgt)

        if n:
            @pl.when(pl.program_id(0) == n_chunks - 1)
            def _():
                _scatter_finish(copies)

    dqkv, dgates, *landed = pl.pallas_call(
        body,
        name="gdn_chunk_bwd",
        grid=(n_chunks,),
        in_specs=[
            pl.BlockSpec((3, CHUNK, width), lambda c: (0, last - c, 0)),
            pl.BlockSpec((CHUNK, HEAD_DIM), lambda c: (last - c, 0)),
            pl.BlockSpec((1, HEADS, HEAD_DIM, HEAD_DIM), lambda c: (last - c, 0, 0, 0)),
            pl.BlockSpec((1, HEADS, CHUNK, CHUNK), lambda c: (last - c, 0, 0, 0)),
            pl.BlockSpec((CHUNK, width), lambda c: (last - c, 0)),
        ] + [ANY] * n,
        out_specs=[pl.BlockSpec((3, CHUNK, width), lambda c: (0, last - c, 0)),
                   pl.BlockSpec((CHUNK, HEAD_DIM), lambda c: (last - c, 0))] + [ANY] * n,
        out_shape=[jax.ShapeDtypeStruct(qkv.shape, F32), jax.ShapeDtypeStruct(gates.shape, F32)]
        + [jax.ShapeDtypeStruct(b.shape, b.dtype) for b in scatter],
        scratch_shapes=[pltpu.VMEM((HEADS, HEAD_DIM, HEAD_DIM), F32)] + (_scatter_sems(n) if n else []),
        compiler_params=_params(("arbitrary",)),
    )(qkv, gates, states, tinv, d_o, *scatter)
    return dqkv, dgates, _keep_own_slots(landed, scatter)


def mm_nn(a, b, *, tm, name):
    ks, m, tk = a.shape
    _, ns, _, tn = b.shape

    def body(a_ref, b_ref, o_ref):
        p = _dot(a_ref[...].astype(BF16), b_ref[...])

        @pl.when(pl.program_id(2) == 0)
        def _():
            o_ref[...] = p

        @pl.when(pl.program_id(2) > 0)
        def _():
            o_ref[...] += p

    return pl.pallas_call(
        body,
        name=name,
        grid=(ns, m // tm, ks),
        in_specs=[
            pl.BlockSpec((None, tm, tk), lambda n, i, k: (k, i, 0)),
            pl.BlockSpec((None, None, tk, tn), lambda n, i, k: (k, n, 0, 0)),
        ],
        out_specs=pl.BlockSpec((None, tm, tn), lambda n, i, k: (n, i, 0)),
        out_shape=jax.ShapeDtypeStruct((ns, m, tn), F32),
        compiler_params=_params(("arbitrary", "arbitrary", "arbitrary")),
    )(a, b)


def mm_nt(dy, w, *, tm, name, res=None, res_scale=1.0):
    ns, m, tn = dy.shape
    ks, _, tk, _ = w.shape

    def body(*refs):
        if res is None:
            dy_ref, w_ref, o_ref = refs
        else:
            dy_ref, w_ref, r_ref, o_ref = refs
        p = _dot_nt(dy_ref[...].astype(BF16), w_ref[...])

        @pl.when(pl.program_id(2) == 0)
        def _():
            o_ref[...] = p if res is None else p + res_scale * r_ref[...]

        @pl.when(pl.program_id(2) > 0)
        def _():
            o_ref[...] += p

    in_specs = [
        pl.BlockSpec((None, tm, tn), lambda k, i, n: (n, i, 0)),
        pl.BlockSpec((None, None, tk, tn), lambda k, i, n: (k, n, 0, 0)),
    ]
    args = [dy, w]
    if res is not None:
        in_specs.append(pl.BlockSpec((None, tm, tk), lambda k, i, n: (k, i, 0)))
        args.append(res)
    return pl.pallas_call(
        body,
        name=name,
        grid=(ks, m // tm, ns),
        in_specs=in_specs,
        out_specs=pl.BlockSpec((None, tm, tk), lambda k, i, n: (k, i, 0)),
        out_shape=jax.ShapeDtypeStruct((ks, m, tk), F32),
        compiler_params=_params(("arbitrary", "arbitrary", "arbitrary")),
    )(*args)


def mm_tn(x, dy, *, tm, name, rb=None):
    ks, m, tk = x.shape
    ns, _, tn = dy.shape
    rb = tk if rb is None else rb

    def body(x_ref, dy_ref, o_ref):
        @pl.when(pl.program_id(2) == 0)
        def _():
            o_ref[...] = jnp.zeros_like(o_ref)

        dyb = dy_ref[...].astype(BF16)
        for r in range(0, tk, rb):
            o_ref[r:r + rb, :] += _dot_tn(x_ref[:, r:r + rb].astype(BF16), dyb)

    return pl.pallas_call(
        body,
        name=name,
        grid=(ks, ns, m // tm),
        in_specs=[
            pl.BlockSpec((None, tm, tk), lambda k, n, i: (k, i, 0)),
            pl.BlockSpec((None, tm, tn), lambda k, n, i: (n, i, 0)),
        ],
        out_specs=pl.BlockSpec((None, None, tk, tn), lambda k, n, i: (k, n, 0, 0)),
        out_shape=jax.ShapeDtypeStruct((ks, ns, tk, tn), F32),
        compiler_params=_params(("arbitrary", "arbitrary", "arbitrary")),
    )(x, dy)


def _row_partial(x):
    rows, c = x.shape
    return jnp.sum(x.reshape(rows // 8, 8, c), axis=0)


def _layer_norm(r, g, b):
    mu = jnp.mean(r, axis=-1, keepdims=True)
    xc = r - mu
    var = jnp.mean(xc * xc, axis=-1, keepdims=True)
    return xc * lax.rsqrt(var + LN_EPS) * g + b


def _layer_norm_bwd(x, dh, g):
    mu = jnp.mean(x, axis=-1, keepdims=True)
    xc = x - mu
    rstd = lax.rsqrt(jnp.mean(xc * xc, axis=-1, keepdims=True) + LN_EPS)
    xh = xc * rstd
    dxh = dh * g
    m1 = jnp.mean(dxh, axis=-1, keepdims=True)
    m2 = jnp.mean(dxh * xh, axis=-1, keepdims=True)
    return rstd * (dxh - m1 - xh * m2), _row_partial(dh * xh), _row_partial(dh)


def mm_nn_ln(a, b, h_prev, g, beta, *, tm, name):
    ks, m, tk = a.shape
    d = b.shape[3]

    def body(a_ref, b_ref, hp_ref, g_ref, be_ref, r_ref, h_ref, hb_ref):
        p = _dot(a_ref[...].astype(BF16), b_ref[...])

        @pl.when(pl.program_id(1) == 0)
        def _():
            r_ref[...] = p

        @pl.when(pl.program_id(1) > 0)
        def _():
            r_ref[...] += p

        @pl.when(pl.program_id(1) == ks - 1)
        def _():
            r = ALPHA * hp_ref[...] + r_ref[...]
            r_ref[...] = r
            h = _layer_norm(r, g_ref[...], be_ref[...])
            h_ref[...] = h
            hb_ref[...] = h.astype(BF16)

    row = pl.BlockSpec((None, tm, d), lambda i, k: (0, i, 0))
    vec = pl.BlockSpec((1, d), lambda i, k: (0, 0))
    return pl.pallas_call(
        body,
        name=name,
        grid=(m // tm, ks),
        in_specs=[
            pl.BlockSpec((None, tm, tk), lambda i, k: (k, i, 0)),
            pl.BlockSpec((None, None, tk, d), lambda i, k: (k, 0, 0, 0)),
            row, vec, vec,
        ],
        out_specs=[row, row, row],
        out_shape=[jax.ShapeDtypeStruct((1, m, d), F32)] * 2 + [jax.ShapeDtypeStruct((1, m, d), BF16)],
        compiler_params=_params(("arbitrary", "arbitrary")),
    )(a, b, h_prev, g, beta)


def mm_nt_ln_bwd(dy, w, res, r, g, *, tm, name, swap=()):
    ns, m, tn = dy.shape
    d = w.shape[2]
    n_swap = len(swap)
    last_tile = m // tm - 1

    def body(dy_ref, w_ref, res_ref, r_ref, g_ref, *refs):
        leaving_refs, (dr_ref, dgb_ref), refs = refs[:n_swap], refs[n_swap:n_swap + 2], refs[n_swap + 2:]
        copies = _swap_copies(leaving_refs, refs[:n_swap], *refs[n_swap:]) if n_swap else None
        p = _dot_nt(dy_ref[...].astype(BF16), w_ref[...])

        @pl.when((pl.program_id(0) == 0) & (pl.program_id(1) == 0))
        def _():
            dgb_ref[...] = jnp.zeros_like(dgb_ref)
            if n_swap:
                _swap_start(copies)

        @pl.when(pl.program_id(1) == 0)
        def _():
            dr_ref[...] = p + ALPHA * res_ref[...]

        @pl.when(pl.program_id(1) > 0)
        def _():
            dr_ref[...] += p

        @pl.when(pl.program_id(1) == ns - 1)
        def _():
            for rows in (pl.ds(0, tm // 2), pl.ds(tm // 2, tm // 2)):
                dr, dgamma, dbeta = _layer_norm_bwd(r_ref[rows, :], dr_ref[rows, :], g_ref[...])
                dr_ref[rows, :] = dr
                dgb_ref[0] += dgamma
                dgb_ref[1] += dbeta

        if n_swap:
            @pl.when((pl.program_id(0) == last_tile) & (pl.program_id(1) == ns - 1))
            def _():
                _swap_finish(copies)

    row = pl.BlockSpec((None, tm, d), lambda i, n: (0, i, 0))
    dr, dgb, *landed = pl.pallas_call(
        body,
        name=name,
        grid=(m // tm, ns),
        in_specs=[
            pl.BlockSpec((None, tm, tn), lambda i, n: (n, i, 0)),
            pl.BlockSpec((None, None, d, tn), lambda i, n: (0, n, 0, 0)),
            row, row,
            pl.BlockSpec((1, d), lambda i, n: (0, 0)),
        ] + [ANY] * n_swap,
        out_specs=[row, pl.BlockSpec((2, 8, d), lambda i, n: (0, 0, 0))] + [ANY] * n_swap,
        out_shape=[jax.ShapeDtypeStruct((1, m, d), F32), jax.ShapeDtypeStruct((2, 8, d), F32)] + _swap_out_shapes(swap),
        scratch_shapes=_swap_sems(n_swap) if n_swap else [],
        compiler_params=_params(("arbitrary", "arbitrary")),
    )(dy, w, res, r, g, *swap)
    return dr, dgb, landed


def loss_ln_bwd(h, target, r, g, *, first, count, tm):
    _, lp, d = h.shape

    def body(h_ref, t_ref, r_ref, g_ref, dr_ref, dgb_ref, l_ref):
        row = pl.program_id(0) * tm + lax.broadcasted_iota(jnp.int32, (tm, d), 0)
        valid = (row >= first) & (row < first + count)
        err = jnp.where(valid, h_ref[...] - t_ref[...], 0.0)
        dr, dgamma, dbeta = _layer_norm_bwd(r_ref[...], err * (1.0 / d), g_ref[...])
        dr_ref[...] = dr

        @pl.when(pl.program_id(0) == 0)
        def _():
            dgb_ref[...] = jnp.zeros_like(dgb_ref)
            l_ref[...] = jnp.zeros_like(l_ref)

        dgb_ref[0] += dgamma
        dgb_ref[1] += dbeta
        l_ref[...] += _row_partial(err * err) * (0.5 / d)

    row3 = pl.BlockSpec((None, tm, d), lambda i: (0, i, 0))
    return pl.pallas_call(
        body,
        name="loss_ln4_bwd",
        grid=(lp // tm,),
        in_specs=[row3, pl.BlockSpec((tm, d), lambda i: (i, 0)), row3, pl.BlockSpec((1, d), lambda i: (0, 0))],
        out_specs=[row3, pl.BlockSpec((2, 8, d), lambda i: (0, 0, 0)), pl.BlockSpec((8, d), lambda i: (0, 0))],
        out_shape=[jax.ShapeDtypeStruct((1, lp, d), F32), jax.ShapeDtypeStruct((2, 8, d), F32),
                   jax.ShapeDtypeStruct((8, d), F32)],
        compiler_params=_params(("arbitrary",)),
    )(h, target, r, g)


def _halo_index(tile, tm):
    return jnp.maximum(tile * (tm // HALO) - 1, 0)


def _conv_fwd(xs_ref, w, taps, tm):
    acc = w(0) * xs_ref[pl.ds(HALO - taps + 1, tm), :]
    for j in range(1, taps):
        acc += w(j) * xs_ref[pl.ds(HALO - taps + 1 + j, tm), :]
    return acc


def _conv_bwd_x(dcs_ref, w, taps, tm):
    acc = w(0) * dcs_ref[pl.ds(taps - 1, tm), :]
    for j in range(1, taps):
        acc += w(j) * dcs_ref[pl.ds(taps - 1 - j, tm), :]
    return acc


SUB = 8
LANES = 128
PAIR = 2 * SUB
STRIP_UNROLL = 2


def _pair_rows(r0):
    return pl.ds(r0, SUB), pl.ds(r0 + SUB if isinstance(r0, int) else pl.multiple_of(r0 + SUB, SUB), SUB)


def _shift_down(cur, prev, s):
    if s == 0:
        return cur
    row = lax.broadcasted_iota(jnp.int32, cur.shape, 0)
    return jnp.where(row < s, pltpu.roll(prev, s, axis=0), pltpu.roll(cur, s, axis=0))


def _shift_up(cur, nxt, s):
    if s == 0:
        return cur
    row = lax.broadcasted_iota(jnp.int32, cur.shape, 0)
    return jnp.where(row < SUB - s, pltpu.roll(cur, SUB - s, axis=0), pltpu.roll(nxt, SUB - s, axis=0))


def _silu_parts(c):
    sg = _sigmoid(c)
    return c * sg, sg * (1.0 + c * (1.0 - sg))


def _head_sum(x):
    rows, c = x.shape
    parts = []
    for h in range(c // HEAD_DIM):
        s = jnp.sum(x[:, h * HEAD_DIM:(h + 1) * HEAD_DIM], axis=-1, keepdims=True)
        parts.append(jnp.broadcast_to(s, (rows, HEAD_DIM)))
    return parts[0] if len(parts) == 1 else jnp.concatenate(parts, axis=-1)


def _log1p(y):
    u = 1.0 + y
    d = u - 1.0
    return jnp.where(d == 0.0, y, jnp.log(u) * (y / jnp.where(d == 0.0, 1.0, d)))


def _softplus(x):
    return jnp.maximum(x, 0.0) + _log1p(jnp.exp(-jnp.abs(x)))


def _gate_values(x, al, dt):
    lane = lax.broadcasted_iota(jnp.int32, x.shape, 1)
    is_beta, is_g = lane < HEADS, (lane >= HEADS) & (lane < 2 * HEADS)
    return _sigmoid(x), -jnp.exp(al) * _softplus(x + dt), is_beta, is_g


def gdn_gates_fwd(pba, al, dt, *, tm):
    _, lp, width = pba.shape

    def body(x_ref, al_ref, dt_ref, o_ref):
        beta, g, is_beta, is_g = _gate_values(x_ref[...], al_ref[...], dt_ref[...])
        o_ref[...] = jnp.where(is_beta, beta, jnp.where(is_g, g, 0.0))

    vec = pl.BlockSpec((1, width), lambda i: (0, 0))
    return pl.pallas_call(
        body,
        name="gdn_gates_fwd",
        grid=(lp // tm,),
        in_specs=[pl.BlockSpec((None, tm, width), lambda i: (0, i, 0)), vec, vec],
        out_specs=pl.BlockSpec((tm, width), lambda i: (i, 0)),
        out_shape=jax.ShapeDtypeStruct((lp, width), F32),
        compiler_params=_params(("arbitrary",)),
    )(pba, al, dt)


def gdn_gates_bwd(pba, dgates, al, dt, *, tm):
    _, lp, width = pba.shape

    def body(x_ref, d_ref, al_ref, dt_ref, dx_ref, dsc_ref):
        x = x_ref[...]
        beta, g, is_beta, is_g = _gate_values(x, al_ref[...], dt_ref[...])
        d = d_ref[...]
        dg = jnp.where(is_g, d, 0.0)
        da = dg * -jnp.exp(al_ref[...]) * _sigmoid(x + dt_ref[...])
        dx_ref[...] = jnp.where(is_beta, d * beta * (1.0 - beta), da).astype(dx_ref.dtype)

        @pl.when(pl.program_id(0) == 0)
        def _():
            dsc_ref[...] = jnp.zeros_like(dsc_ref)

        dsc_ref[0] += _row_partial(dg * g)
        dsc_ref[1] += _row_partial(da)

    vec = pl.BlockSpec((1, width), lambda i: (0, 0))
    return pl.pallas_call(
        body,
        name="gdn_gates_bwd",
        grid=(lp // tm,),
        in_specs=[pl.BlockSpec((None, tm, width), lambda i: (0, i, 0)), pl.BlockSpec((tm, width), lambda i: (i, 0)), vec, vec],
        out_specs=[pl.BlockSpec((None, tm, width), lambda i: (0, i, 0)), pl.BlockSpec((2, SUB, width), lambda i: (0, 0, 0))],
        out_shape=[jax.ShapeDtypeStruct((1, lp, width), BF16), jax.ShapeDtypeStruct((2, SUB, width), F32)],
        compiler_params=_params(("arbitrary",)),
    )(pba, dgates, al, dt)


def gdn_pre_fwd(p3, conv_w, *, tm, cb):
    _, lp, width = p3.shape
    taps = conv_w.shape[1]

    def body(x_ref, halo_ref, w_ref, o_ref, xs):
        i = pl.program_id(1)
        for s in range(3):
            xs[s, 0:HALO, :] = jnp.where(i > 0, halo_ref[s], 0.0)
            xs[s, HALO:, :] = x_ref[s]
            c = _conv_fwd(xs.at[s], lambda j, s=s: w_ref[s, j:j + 1, :], taps, tm)
            y, _ = _silu_parts(c)
            if s < 2:
                y = y * lax.rsqrt(_head_sum(y * y) + L2_EPS)
                if s == 0:
                    y = y * Q_SCALE
            o_ref[s] = y

    return pl.pallas_call(
        body,
        name="gdn_pre_fwd",
        grid=(width // cb, lp // tm),
        in_specs=[
            pl.BlockSpec((3, tm, cb), lambda j, i: (0, i, j)),
            pl.BlockSpec((3, HALO, cb), lambda j, i: (0, _halo_index(i, tm), j)),
            pl.BlockSpec((3, taps, cb), lambda j, i: (0, 0, j)),
        ],
        out_specs=pl.BlockSpec((3, tm, cb), lambda j, i: (0, i, j)),
        out_shape=jax.ShapeDtypeStruct((3, lp, width), F32),
        scratch_shapes=[pltpu.VMEM((3, tm + HALO, cb), F32)],
        compiler_params=_params(("arbitrary", "arbitrary")),
    )(p3, p3, conv_w)


def gdn_pre_bwd(p3, dqkv, conv_w, *, tm, cb):
    _, lp, width = p3.shape
    taps = conv_w.shape[1]
    last = lp // tm - 1

    def body(x_ref, halo_ref, d_ref, w_ref, dx_ref, dw_ref, xs, dcs, carry):
        step = pl.program_id(1)
        tile = last - step

        @pl.when(step == 0)
        def _():
            carry[...] = jnp.zeros_like(carry)
            dw_ref[...] = jnp.zeros_like(dw_ref)

        for s in range(3):
            w = lambda j, s=s: w_ref[s, j:j + 1, :]
            xs[s, 0:HALO, :] = jnp.where(tile > 0, halo_ref[s], 0.0)
            xs[s, HALO:, :] = x_ref[s]
            c = _conv_fwd(xs.at[s], w, taps, tm)
            y, dsilu = _silu_parts(c)
            dy = d_ref[s]
            if s < 2:
                rn = lax.rsqrt(_head_sum(y * y) + L2_EPS)
                yn = y * rn
                if s == 0:
                    dy = dy * Q_SCALE
                dy = rn * (dy - yn * _head_sum(dy * yn))
            dc = dy * dsilu
            dcs[s, 0:tm, :] = dc
            dcs[s, tm:, :] = carry[s]
            dx_ref[s] = _conv_bwd_x(dcs.at[s], w, taps, tm).astype(dx_ref.dtype)
            carry[s] = dc[0:HALO, :]
            for j in range(taps):
                dw_ref[s, j] += _row_partial(dc * xs[s, pl.ds(HALO - taps + 1 + j, tm), :])

    tile_spec = pl.BlockSpec((3, tm, cb), lambda j, i: (0, last - i, j))
    return pl.pallas_call(
        body,
        name="gdn_pre_bwd",
        grid=(width // cb, lp // tm),
        in_specs=[
            tile_spec,
            pl.BlockSpec((3, HALO, cb), lambda j, i: (0, _halo_index(last - i, tm), j)),
            tile_spec,
            pl.BlockSpec((3, taps, cb), lambda j, i: (0, 0, j)),
        ],
        out_specs=[tile_spec, pl.BlockSpec((3, taps, SUB, cb), lambda j, i: (0, 0, 0, j))],
        out_shape=[jax.ShapeDtypeStruct((3, lp, width), BF16), jax.ShapeDtypeStruct((3, taps, SUB, width), F32)],
        scratch_shapes=[
            pltpu.VMEM((3, tm + HALO, cb), F32),
            pltpu.VMEM((3, tm + HALO, cb), F32),
            pltpu.VMEM((3, HALO, cb), F32),
        ],
        compiler_params=_params(("arbitrary", "arbitrary")),
    )(p3, p3, dqkv, conv_w)


def gdn_post_fwd(o, z, nw_b, *, tm):
    _, lp, width = o.shape

    def body(o_ref, z_ref, nw_ref, y_ref):
        ov = o_ref[...]
        rn = lax.rsqrt(_head_sum(ov * ov) * (1.0 / HEAD_DIM) + RMS_EPS)
        gate, _ = _silu_parts(z_ref[...])
        y_ref[...] = (ov * rn * nw_ref[...] * gate).astype(y_ref.dtype)

    row = pl.BlockSpec((None, tm, width), lambda i: (0, i, 0))
    return pl.pallas_call(
        body,
        name="gdn_post_fwd",
        grid=(lp // tm,),
        in_specs=[row, row, pl.BlockSpec((1, width), lambda i: (0, 0))],
        out_specs=row,
        out_shape=jax.ShapeDtypeStruct((1, lp, width), BF16),
        compiler_params=_params(("arbitrary",)),
    )(o, z, nw_b)


def gdn_post_bwd(o, z, dy, nw_b, *, tm):
    _, lp, width = o.shape

    def body(o_ref, z_ref, dy_ref, nw_ref, do_ref, dz_ref, dnw_ref):
        ov = o_ref[...]
        rn = lax.rsqrt(_head_sum(ov * ov) * (1.0 / HEAD_DIM) + RMS_EPS)
        yn = ov * rn
        gate, dgate = _silu_parts(z_ref[...])
        d_on = dy_ref[...] * gate
        dz_ref[...] = (dy_ref[...] * yn * nw_ref[...] * dgate).astype(dz_ref.dtype)
        a = d_on * nw_ref[...]
        do_ref[...] = rn * (a - yn * (_head_sum(a * yn) * (1.0 / HEAD_DIM)))

        @pl.when(pl.program_id(0) == 0)
        def _():
            dnw_ref[...] = jnp.zeros_like(dnw_ref)

        dnw_ref[...] += _row_partial(d_on * yn)

    row = pl.BlockSpec((None, tm, width), lambda i: (0, i, 0))
    return pl.pallas_call(
        body,
        name="gdn_post_bwd",
        grid=(lp // tm,),
        in_specs=[row, row, row, pl.BlockSpec((1, width), lambda i: (0, 0))],
        out_specs=[row, row, pl.BlockSpec((8, width), lambda i: (0, 0))],
        out_shape=[jax.ShapeDtypeStruct((1, lp, width), F32), jax.ShapeDtypeStruct((1, lp, width), BF16),
                   jax.ShapeDtypeStruct((8, width), F32)],
        compiler_params=_params(("arbitrary",)),
    )(o, z, dy, nw_b)


def ffn_act_fwd(up, conv_w, *, tm, name):
    _, lp, c_w = up.shape
    taps = conv_w.shape[1]

    def body(u_ref, halo_ref, g_ref, w_ref, o_ref):
        first_tile = pl.program_id(1) == 0

        def strip(cur, prev, rows, cs):
            conv = w_ref[taps - 1:taps, cs] * cur
            for j in range(taps - 1):
                conv += w_ref[j:j + 1, cs] * _shift_down(cur, prev, taps - 1 - j)
            y, _ = _silu_parts(conv)
            return y * g_ref[rows, cs]

        def pair(r0, above_of):
            top, bot = _pair_rows(r0)
            for c0 in range(0, c_w, LANES):
                cs = slice(c0, c0 + LANES)
                cur_t, cur_b = u_ref[top, cs], u_ref[bot, cs]
                out = [strip(cur_t, above_of(cs), top, cs), strip(cur_b, cur_t, bot, cs)]
                o_ref[pl.ds(r0, PAIR), cs] = jnp.concatenate(out, axis=0).astype(o_ref.dtype)

        pair(0, lambda cs: jnp.where(first_tile, 0.0, halo_ref[:, cs]))

        def loop_body(s, carry):
            r0 = pl.multiple_of(s * PAIR, PAIR)
            pair(r0, lambda cs: u_ref[pl.ds(pl.multiple_of(r0 - SUB, SUB), SUB), cs])
            return carry

        lax.fori_loop(1, tm // PAIR, loop_body, 0, unroll=STRIP_UNROLL)

    return pl.pallas_call(
        body,
        name=name,
        grid=(2, lp // tm),
        in_specs=[
            pl.BlockSpec((None, tm, c_w), lambda s, i: (s, i, 0)),
            pl.BlockSpec((None, HALO, c_w), lambda s, i: (s, _halo_index(i, tm), 0)),
            pl.BlockSpec((None, tm, c_w), lambda s, i: (2 + s, i, 0)),
            pl.BlockSpec((None, taps, c_w), lambda s, i: (s, 0, 0)),
        ],
        out_specs=pl.BlockSpec((None, tm, c_w), lambda s, i: (s, i, 0)),
        out_shape=jax.ShapeDtypeStruct((2, lp, c_w), BF16),
        compiler_params=_params(("arbitrary", "arbitrary")),
    )(up, up, up, conv_w)


def ffn_act_bwd(up, dact, conv_w, *, tm, name):
    _, lp, c_w = up.shape
    taps = conv_w.shape[1]
    last = lp // tm - 1
    n_pairs = tm // PAIR

    def body(u_ref, halo_ref, g_ref, d_ref, w_ref, dup_ref, dw_ref, below):
        step = pl.program_id(1)
        first_tile = step == last

        @pl.when(step == 0)
        def _():
            below[...] = jnp.zeros_like(below)
            dw_ref[...] = jnp.zeros_like(dw_ref)

        def strip(cur, prev, rows, cs, nxt):
            shifted = [_shift_down(cur, prev, taps - 1 - j) for j in range(taps)]
            conv = w_ref[0:1, cs] * shifted[0]
            for j in range(1, taps):
                conv += w_ref[j:j + 1, cs] * shifted[j]
            y, dsilu = _silu_parts(conv)
            d = d_ref[rows, cs]
            dc = d * g_ref[rows, cs] * dsilu
            dx = w_ref[taps - 1:taps, cs] * dc
            for j in range(taps - 1):
                dx += w_ref[j:j + 1, cs] * _shift_up(dc, nxt, taps - 1 - j)
            return dx, d * y, dc, [dc * s for s in shifted]

        def pair(r0, above_of):
            top, bot = _pair_rows(r0)
            both = pl.ds(r0, PAIR)
            for c0 in range(0, c_w, LANES):
                cs = slice(c0, c0 + LANES)
                cur_t, cur_b = u_ref[top, cs], u_ref[bot, cs]
                dx_b, dg_b, dc_b, dw_b = strip(cur_b, cur_t, bot, cs, below[:, cs])
                dx_t, dg_t, dc_t, dw_t = strip(cur_t, above_of(cs), top, cs, dc_b)
                below[:, cs] = dc_t
                dup_ref[0, both, cs] = jnp.concatenate([dx_t, dx_b], axis=0).astype(dup_ref.dtype)
                dup_ref[1, both, cs] = jnp.concatenate([dg_t, dg_b], axis=0).astype(dup_ref.dtype)
                for j in range(taps):
                    dw_ref[j, :, cs] += dw_t[j] + dw_b[j]

        def loop_body(it, carry):
            r0 = pl.multiple_of((n_pairs - 1 - it) * PAIR, PAIR)
            pair(r0, lambda cs: u_ref[pl.ds(pl.multiple_of(r0 - SUB, SUB), SUB), cs])
            return carry

        lax.fori_loop(0, n_pairs - 1, loop_body, 0, unroll=STRIP_UNROLL)
        pair(0, lambda cs: jnp.where(first_tile, 0.0, halo_ref[:, cs]))

    return pl.pallas_call(
        body,
        name=name,
        grid=(2, lp // tm),
        in_specs=[
            pl.BlockSpec((None, tm, c_w), lambda s, i: (s, last - i, 0)),
            pl.BlockSpec((None, HALO, c_w), lambda s, i: (s, _halo_index(last - i, tm), 0)),
            pl.BlockSpec((None, tm, c_w), lambda s, i: (2 + s, last - i, 0)),
            pl.BlockSpec((None, tm, c_w), lambda s, i: (s, last - i, 0)),
            pl.BlockSpec((None, taps, c_w), lambda s, i: (s, 0, 0)),
        ],
        out_specs=[
            pl.BlockSpec((2, None, tm, c_w), lambda s, i: (0, s, last - i, 0)),
            pl.BlockSpec((None, taps, SUB, c_w), lambda s, i: (s, 0, 0, 0)),
        ],
        out_shape=[jax.ShapeDtypeStruct((2, 2, lp, c_w), BF16), jax.ShapeDtypeStruct((2, taps, SUB, c_w), F32)],
        scratch_shapes=[pltpu.VMEM((SUB, c_w), F32)],
        compiler_params=_params(("arbitrary", "arbitrary")),
    )(up, up, up, dact, conv_w)


def sc_fwd(pb, conv_w, *, tm, cb):
    _, lp, width = pb.shape
    taps = conv_w.shape[0]

    def body(x_ref, halo_ref, w_ref, o_ref):
        first_tile = pl.program_id(1) == 0

        def strip(cur, prev, rows, cs):
            conv = w_ref[taps - 1:taps, cs] * cur
            for j in range(taps - 1):
                conv += w_ref[j:j + 1, cs] * _shift_down(cur, prev, taps - 1 - j)
            return x_ref[0, rows, cs] * conv

        def pair(r0, above_of):
            top, bot = _pair_rows(r0)
            for c0 in range(0, cb, LANES):
                cs = slice(c0, c0 + LANES)
                cur_t = x_ref[1, top, cs] * x_ref[2, top, cs]
                cur_b = x_ref[1, bot, cs] * x_ref[2, bot, cs]
                out = [strip(cur_t, above_of(cs), top, cs), strip(cur_b, cur_t, bot, cs)]
                o_ref[pl.ds(r0, PAIR), cs] = jnp.concatenate(out, axis=0).astype(o_ref.dtype)

        pair(0, lambda cs: jnp.where(first_tile, 0.0, halo_ref[1, :, cs] * halo_ref[2, :, cs]))

        def loop_body(k, carry):
            r0 = pl.multiple_of(k * PAIR, PAIR)
            before = pl.ds(pl.multiple_of(r0 - SUB, SUB), SUB)
            pair(r0, lambda cs: x_ref[1, before, cs] * x_ref[2, before, cs])
            return carry

        lax.fori_loop(1, tm // PAIR, loop_body, 0, unroll=STRIP_UNROLL)

    return pl.pallas_call(
        body,
        name="sc_fwd",
        grid=(width // cb, lp // tm),
        in_specs=[
            pl.BlockSpec((3, tm, cb), lambda j, i: (0, i, j)),
            pl.BlockSpec((3, HALO, cb), lambda j, i: (0, _halo_index(i, tm), j)),
            pl.BlockSpec((taps, cb), lambda j, i: (0, j)),
        ],
        out_specs=pl.BlockSpec((None, tm, cb), lambda j, i: (0, i, j)),
        out_shape=jax.ShapeDtypeStruct((1, lp, width), BF16),
        compiler_params=_params(("arbitrary", "arbitrary")),
    )(pb, pb, conv_w)


def sc_bwd(pb, ds, conv_w, *, tm, cb):
    _, lp, width = pb.shape
    taps = conv_w.shape[0]
    last = lp // tm - 1
    n_pairs = tm // PAIR

    def body(x_ref, halo_ref, d_ref, w_ref, dx_ref, dw_ref, below):
        step = pl.program_id(1)
        first_tile = step == last

        @pl.when(step == 0)
        def _():
            below[...] = jnp.zeros_like(below)
            dw_ref[...] = jnp.zeros_like(dw_ref)

        def strip(cur, prev, rows, cs, nxt):
            gate, left, right = x_ref[0, rows, cs], x_ref[1, rows, cs], x_ref[2, rows, cs]
            shifted = [_shift_down(cur, prev, taps - 1 - j) for j in range(taps)]
            conv = w_ref[0:1, cs] * shifted[0]
            for j in range(1, taps):
                conv += w_ref[j:j + 1, cs] * shifted[j]
            d = d_ref[rows, cs]
            dc = d * gate
            dp = w_ref[taps - 1:taps, cs] * dc
            for j in range(taps - 1):
                dp += w_ref[j:j + 1, cs] * _shift_up(dc, nxt, taps - 1 - j)
            return d * conv, dp * right, dp * left, dc, [dc * s for s in shifted]

        def pair(r0, above_of):
            top, bot = _pair_rows(r0)
            both = pl.ds(r0, PAIR)
            for c0 in range(0, cb, LANES):
                cs = slice(c0, c0 + LANES)
                cur_t = x_ref[1, top, cs] * x_ref[2, top, cs]
                cur_b = x_ref[1, bot, cs] * x_ref[2, bot, cs]
                *dx_b, dc_b, dw_b = strip(cur_b, cur_t, bot, cs, below[:, cs])
                *dx_t, dc_t, dw_t = strip(cur_t, above_of(cs), top, cs, dc_b)
                below[:, cs] = dc_t
                for s in range(3):
                    dx_ref[s, both, cs] = jnp.concatenate([dx_t[s], dx_b[s]], axis=0).astype(dx_ref.dtype)
                for j in range(taps):
                    dw_ref[j, :, cs] += dw_t[j] + dw_b[j]

        def loop_body(it, carry):
            r0 = pl.multiple_of((n_pairs - 1 - it) * PAIR, PAIR)
            before = pl.ds(pl.multiple_of(r0 - SUB, SUB), SUB)
            pair(r0, lambda cs: x_ref[1, before, cs] * x_ref[2, before, cs])
            return carry

        lax.fori_loop(0, n_pairs - 1, loop_body, 0, unroll=STRIP_UNROLL)
        pair(0, lambda cs: jnp.where(first_tile, 0.0, halo_ref[1, :, cs] * halo_ref[2, :, cs]))

    tile_spec = pl.BlockSpec((3, tm, cb), lambda j, i: (0, last - i, j))
    return pl.pallas_call(
        body,
        name="sc_bwd",
        grid=(width // cb, lp // tm),
        in_specs=[
            tile_spec,
            pl.BlockSpec((3, HALO, cb), lambda j, i: (0, _halo_index(last - i, tm), j)),
            pl.BlockSpec((None, tm, cb), lambda j, i: (0, last - i, j)),
            pl.BlockSpec((taps, cb), lambda j, i: (0, j)),
        ],
        out_specs=[tile_spec, pl.BlockSpec((taps, SUB, cb), lambda j, i: (0, 0, j))],
        out_shape=[jax.ShapeDtypeStruct((3, lp, width), BF16), jax.ShapeDtypeStruct((taps, SUB, width), F32)],
        scratch_shapes=[pltpu.VMEM((SUB, cb), F32)],
        compiler_params=_params(("arbitrary", "arbitrary")),
    )(pb, pb, ds, conv_w)


TILE_BYTES = 1536 * 1024


def _rows_tile(rows, cols, multiple=8):
    if rows * cols * 4 <= TILE_BYTES or rows % multiple:
        return rows
    best = multiple
    for t in range(multiple, rows + 1, multiple):
        if rows % t == 0 and t * cols * 4 <= TILE_BYTES:
            best = t
    return best


def pair_sum(g, landed, core, out_dtype, name):
    _, rows, cols = g.shape
    half = rows // 2
    tr = _rows_tile(half, cols, 16)
    nb = half // tr

    def body(c_ref, g_ref, l_ref, o_ref):
        o_ref[...] = (g_ref[...] + l_ref[...]).astype(out_dtype)

    return pl.pallas_call(
        body,
        name=name,
        grid_spec=pltpu.PrefetchScalarGridSpec(
            num_scalar_prefetch=1,
            grid=(4, nb),
            in_specs=[
                pl.BlockSpec((None, tr, cols), lambda s, i, c: (s, c[0] * nb + i, 0)),
                pl.BlockSpec((None, tr, cols), lambda s, i, c: (s, i, 0)),
            ],
            out_specs=pl.BlockSpec((None, tr, cols), lambda s, i, c: (s, i, 0)),
        ),
        out_shape=jax.ShapeDtypeStruct((4, half, cols), out_dtype),
        compiler_params=_params(("arbitrary", "arbitrary")),
    )(core, g, landed)


def chip_sum(x, name):
    _, rows, cols = x.shape
    tr = _rows_tile(rows, cols, 16)

    def body(x0, x1, x2, x3, o_ref):
        acc = x0[...].astype(F32) + x1[...].astype(F32)
        o_ref[...] = (acc + x2[...].astype(F32)) + x3[...].astype(F32)

    return pl.pallas_call(
        body,
        name=name,
        grid=(rows // tr,),
        in_specs=[pl.BlockSpec((None, tr, cols), lambda i, k=k: (k, i, 0)) for k in range(4)],
        out_specs=pl.BlockSpec((tr, cols), lambda i: (i, 0)),
        out_shape=jax.ShapeDtypeStruct((rows, cols), F32),
        compiler_params=_params(("arbitrary",)),
    )(x, x, x, x)


def adamw(w, g, m, v, name):
    shape = w.shape
    cols = shape[-1]
    rows = w.size // cols
    tr = _rows_tile(rows, cols)

    def body(w_ref, g_ref, m_ref, v_ref, d_ref, m2_ref, v2_ref):
        gv = g_ref[...]
        m2 = ADAM_B1 * m_ref[...] + (1.0 - ADAM_B1) * gv
        v2 = ADAM_B2 * v_ref[...] + (1.0 - ADAM_B2) * (gv * gv)
        m_hat = m2 / (1.0 - ADAM_B1 ** ADAM_STEP)
        v_hat = v2 / (1.0 - ADAM_B2 ** ADAM_STEP)
        d_ref[...] = -ADAM_LR * (m_hat / (jnp.sqrt(v_hat) + ADAM_EPS) + ADAM_WD * w_ref[...])
        m2_ref[...] = m2
        v2_ref[...] = v2

    spec = pl.BlockSpec((tr, cols), lambda i: (i, 0))
    outs = pl.pallas_call(
        body,
        name=name,
        grid=(rows // tr,),
        in_specs=[spec] * 4,
        out_specs=[spec] * 3,
        out_shape=[jax.ShapeDtypeStruct((rows, cols), F32)] * 3,
        compiler_params=_params(("arbitrary",)),
    )(*[t.reshape(rows, cols) for t in (w, g, m, v)])
    return tuple(o.reshape(shape) for o in outs)


MESH_ID = pl.DeviceIdType.MESH
ANY = pl.BlockSpec(memory_space=pl.ANY)


def _place():
    x, y, c = lax.axis_index("x"), lax.axis_index("y"), lax.axis_index("c")
    other_chips = [(1 - x, y), (x, 1 - y), (1 - x, 1 - y)]
    return x, y, c, other_chips


def all_gather_shards(bufs, name):
    n = len(bufs)

    def body(*refs):
        x_refs, o_refs = refs[:n], refs[n:2 * n]
        copies = _gather_copies(x_refs, o_refs, *refs[2 * n:])
        _gather_start(copies)
        _gather_finish(copies)

    outs = pl.pallas_call(
        body,
        name=name,
        in_specs=[ANY] * n,
        out_specs=[ANY] * n,
        out_shape=_gather_out_shapes(bufs),
        scratch_shapes=_gather_sems(n),
    )(*bufs)
    return _set_own_slots(outs, bufs)


def _gather_out_shapes(bufs):
    return [jax.ShapeDtypeStruct((4,) + b.shape, b.dtype) for b in bufs]


def _gather_sems(n):
    return [pltpu.SemaphoreType.DMA((6 * n,)), pltpu.SemaphoreType.DMA((6 * n,))]


def _set_own_slots(outs, bufs):
    if not outs:
        return []
    me = 2 * lax.axis_index("x") + lax.axis_index("y")
    return [lax.dynamic_update_index_in_dim(o, b, me, 0) for o, b in zip(outs, bufs)]


def _gather_copies(x_refs, o_refs, send_sems, recv_sems):
    x, y, c, chips = _place()
    me = 2 * x + y
    sibling = (x, y, 1 - c)

    def part(a, slot, hf):
        half = x_refs[a].shape[0] // 2
        return o_refs[a].at[slot, pl.ds(hf * half, half), :]

    def mine(a):
        half = x_refs[a].shape[0] // 2
        return x_refs[a].at[pl.ds(c * half, half), :]

    def copy(k, src, dst, to):
        return pltpu.make_async_remote_copy(src_ref=src, dst_ref=dst, send_sem=send_sems.at[k],
                                            recv_sem=recv_sems.at[k], device_id=to, device_id_type=MESH_ID)

    sends, arrivals, passes, passed = [], [], [], []
    for a in range(len(x_refs)):
        for j, (px, py) in enumerate(chips):
            landed, theirs = part(a, 2 * px + py, c), part(a, 2 * px + py, 1 - c)
            sends.append(copy(6 * a + j, mine(a), part(a, me, c), (px, py, c)))
            arrivals.append(copy(6 * a + j, mine(a), landed, (px, py, c)))
            passes.append(copy(6 * a + 3 + j, landed, landed, sibling))
            passed.append(copy(6 * a + 3 + j, theirs, theirs, sibling))
    return sends, arrivals, passes, passed


def _gather_start(copies):
    for cp in copies[0]:
        cp.start()


def _gather_finish(copies):
    sends, arrivals, passes, passed = copies
    for arrival, cp in zip(arrivals, passes):
        arrival.wait_recv()
        cp.start()
    for cp in passed:
        cp.wait_recv()
    for cp in sends + passes:
        cp.wait_send()


def swap_halves(bufs, name):
    n = len(bufs)

    def body(*refs):
        copies = _swap_copies(refs[:n], refs[n:2 * n], *refs[2 * n:])
        _swap_start(copies)
        _swap_finish(copies)

    return pl.pallas_call(
        body,
        name=name,
        in_specs=[ANY] * n,
        out_specs=[ANY] * n,
        out_shape=_swap_out_shapes(bufs),
        scratch_shapes=_swap_sems(n),
    )(*bufs)


def _swap_out_shapes(bufs):
    return [jax.ShapeDtypeStruct((4, b.shape[1] // 2, b.shape[2]), b.dtype) for b in bufs]


def _swap_sems(n):
    return [pltpu.SemaphoreType.DMA((n,)), pltpu.SemaphoreType.DMA((n,))]


def _swap_copies(x_refs, o_refs, send_sems, recv_sems):
    x, y, c, _ = _place()
    copies = []
    for a, (x_ref, o_ref) in enumerate(zip(x_refs, o_refs)):
        half = x_ref.shape[1] // 2
        copies.append(pltpu.make_async_remote_copy(src_ref=x_ref.at[:, pl.ds((1 - c) * half, half), :], dst_ref=o_ref,
                                                   send_sem=send_sems.at[a], recv_sem=recv_sems.at[a],
                                                   device_id=(x, y, 1 - c), device_id_type=MESH_ID))
    return copies


def _swap_start(copies):
    for cp in copies:
        cp.start()


def _swap_finish(copies):
    for cp in copies:
        cp.wait()


def scatter_to_chips(bufs, name):
    n = len(bufs)

    def body(*refs):
        x_refs, o_refs = refs[:n], refs[n:2 * n]
        copies = _scatter_copies(x_refs, o_refs, *refs[2 * n:])
        _scatter_start(copies)
        _scatter_finish(copies)

    outs = pl.pallas_call(
        body,
        name=name,
        in_specs=[ANY] * n,
        out_specs=[ANY] * n,
        out_shape=[jax.ShapeDtypeStruct(b.shape, b.dtype) for b in bufs],
        scratch_shapes=_scatter_sems(n),
    )(*bufs)
    return _keep_own_slots(outs, bufs)


def _scatter_sems(n):
    return [pltpu.SemaphoreType.DMA((3 * n,)), pltpu.SemaphoreType.DMA((3 * n,))]


def _keep_own_slots(outs, bufs):
    if not outs:
        return []
    me = 2 * lax.axis_index("x") + lax.axis_index("y")
    return [lax.dynamic_update_index_in_dim(o, lax.dynamic_index_in_dim(b, me, 0, keepdims=False), me, 0)
            for o, b in zip(outs, bufs)]


def _scatter_copies(x_refs, o_refs, send_sems, recv_sems):
    x, y, c, chips = _place()
    me = 2 * x + y

    def copy(a, j, src_slot, dst_slot, px, py):
        return pltpu.make_async_remote_copy(src_ref=x_refs[a].at[src_slot], dst_ref=o_refs[a].at[dst_slot],
                                            send_sem=send_sems.at[3 * a + j], recv_sem=recv_sems.at[3 * a + j],
                                            device_id=(px, py, c), device_id_type=MESH_ID)

    sends = [copy(a, j, 2 * px + py, me, px, py) for a in range(len(x_refs)) for j, (px, py) in enumerate(chips)]
    arrivals = [copy(a, j, me, 2 * px + py, px, py) for a in range(len(x_refs)) for j, (px, py) in enumerate(chips)]
    return sends, arrivals


def _scatter_start(copies):
    for cp in copies[0]:
        cp.start()


def _scatter_finish(copies):
    for cp in copies[1]:
        cp.wait_recv()
    for cp in copies[0]:
        cp.wait_send()


def share_halves(groups, name):
    bufs = [b for grp in groups for b in grp]
    where = [(gi, li) for gi, grp in enumerate(groups) for li in range(len(grp))]
    n = len(bufs)

    def body(*refs):
        x_refs, o_refs = refs[:n], refs[n:n + len(groups)]
        send_sems, recv_sems = refs[n + len(groups):]
        x, y, c, _ = _place()
        sent, arrive = [], []
        for a, (gi, li) in enumerate(where):

            def copy(hf, a=a, gi=gi, li=li):
                return pltpu.make_async_remote_copy(src_ref=x_refs[a], dst_ref=o_refs[gi].at[li, hf],
                                                    send_sem=send_sems.at[a], recv_sem=recv_sems.at[a],
                                                    device_id=(x, y, 1 - c), device_id_type=MESH_ID)

            sent.append(copy(c))
            arrive.append(copy(1 - c))
        for cp in sent:
            cp.start()
        for cp in arrive:
            cp.wait_recv()
        for cp in sent:
            cp.wait_send()

    outs = pl.pallas_call(
        body,
        name=name,
        in_specs=[ANY] * n,
        out_specs=[ANY] * len(groups),
        out_shape=[jax.ShapeDtypeStruct((len(grp), 2) + grp[0].shape, grp[0].dtype) for grp in groups],
        scratch_shapes=[pltpu.SemaphoreType.DMA((n,)), pltpu.SemaphoreType.DMA((n,))],
    )(*bufs)
    c = lax.axis_index("c")
    full = [lax.dynamic_update_index_in_dim(o, jnp.stack(grp), c, 1) for o, grp in zip(outs, groups)]
    return [t.reshape(t.shape[0], 2 * t.shape[2], t.shape[3]) for t in full]


def pair_sums(bufs, landed, dtypes, tag):
    core = lax.axis_index("c").astype(jnp.int32).reshape(1)
    return [pair_sum(b, l, core, dt, "rs_pair_sum_%s%d" % (tag, i)) for i, (b, l, dt) in enumerate(zip(bufs, landed, dtypes))]


def _row_tiles(length):
    return (640, 640) if length > 2048 else (128, 64)


def _divisor_tile(rows, target):
    return max(t for t in range(8, min(rows, target) + 1, 8) if rows % t == 0)


def _local_step(x, target, wt, late_shards, layout_late, complete_grads, sum_pairs):
    seq, d = x.shape
    length = N_META + seq
    tm, tm_ffn = _row_tiles(length)
    lp = -(-length // tm) * tm
    tail = jnp.zeros((lp - length, d), F32)
    h0 = jnp.concatenate([wt["meta"], x, tail], axis=0)[None]
    tgt = jnp.concatenate([jnp.zeros((N_META, d), F32), target, tail], axis=0)
    nn = functools.partial(mm_nn, tm=_divisor_tile(lp, 1664))
    nt = functools.partial(mm_nt, tm=_divisor_tile(lp, 1664))
    tn = functools.partial(mm_tn, tm=_divisor_tile(lp, 1664), rb=256)
    nn_ln = functools.partial(mm_nn_ln, tm=_divisor_tile(lp, 832))
    nt_ln_bwd = functools.partial(mm_nt_ln_bwd, tm=_divisor_tile(lp, 1040))
    ln_g = [wt["ln_mix_g"][0:1], wt["ln_ffn_g"][0:1], wt["ln_mix_g"][1:2], wt["ln_ffn_g"][1:2]]
    ln_b = [wt["ln_mix_b"][0:1], wt["ln_ffn_b"][0:1], wt["ln_mix_b"][1:2], wt["ln_ffn_b"][1:2]]

    h0b = h0.astype(BF16)
    p3 = nn(h0b, wt["a3"], name="a_in3")
    pz = nn(h0b, wt["az"], name="a_inz")
    pba = nn(h0b, wt["a_ba"], name="a_inba")
    qkv = gdn_pre_fwd(p3, wt["a_conv3"], tm=tm, cb=2 * HEAD_DIM)
    gates = gdn_gates_fwd(pba, wt["alog_lanes"], wt["dtb_lanes"], tm=tm)
    o, states, tinv, late_stacks = gdn_chunk_fwd(qkv, gates, late_shards)
    wt = {**wt, **layout_late(late_stacks)}
    onz = gdn_post_fwd(o[None], pz, wt["anorm_b"], tm=tm)
    r1, h1, h1b = nn_ln(onz, wt["a_out"], h0, ln_g[0], ln_b[0], name="a_out_ln1")
    up0 = nn(h1b, wt["up"][0], name="up0")
    act0 = ffn_act_fwd(up0, wt["fconv"][0], tm=tm_ffn, name="ffn_act0")
    r2, h2, h2b = nn_ln(act0, wt["down"][0], h1, ln_g[1], ln_b[1], name="down0_ln2")
    pb = nn(h2b, wt["b_in"], name="b_in")
    sc = sc_fwd(pb, wt["b_conv"], tm=tm_ffn, cb=d)
    r3, h3, h3b = nn_ln(sc, wt["b_out"], h2, ln_g[2], ln_b[2], name="b_out_ln3")
    up1 = nn(h3b, wt["up"][1], name="up1")
    act1 = ffn_act_fwd(up1, wt["fconv"][1], tm=tm_ffn, name="ffn_act1")
    r4, h4, _ = nn_ln(act1, wt["down"][1], h3, ln_g[3], ln_b[3], name="down1_ln4")

    grads = {}
    dr4, dgb4, loss_part = loss_ln_bwd(h4, tgt, r4, ln_g[3], first=N_META, count=seq, tm=tm)
    d_down1 = tn(act1, dr4, name="d_down1")
    dact1 = nt(dr4, wt["down"][1], name="d_act1")
    dup1, dfconv1 = ffn_act_bwd(up1, dact1, wt["fconv"][1], tm=tm_ffn, name="ffn_act1_bwd")
    dup1 = dup1.reshape(up1.shape)
    d_up1 = tn(h3b, dup1, name="d_up1")

    dr3, dgb3, _ = nt_ln_bwd(dup1, wt["up"][1], dr4, r3, ln_g[2], name="d_h3_ln3")
    d_bout = tn(sc, dr3, name="d_b_out")
    dsc = nt(dr3, wt["b_out"], name="d_sc")
    dpb, dbconv = sc_bwd(pb, dsc, wt["b_conv"], tm=tm_ffn, cb=d)
    d_bin = tn(h2b, dpb, name="d_b_in")

    dr2, dgb2, _ = nt_ln_bwd(dpb, wt["b_in"], dr3, r2, ln_g[1], name="d_h2_ln2")
    d_down0 = tn(act0, dr2, name="d_down0")
    dact0 = nt(dr2, wt["down"][0], name="d_act0")
    dup0, dfconv0 = ffn_act_bwd(up0, dact0, wt["fconv"][0], tm=tm_ffn, name="ffn_act0_bwd")
    dup0 = dup0.reshape(up0.shape)
    d_up0 = tn(h1b, dup0, name="d_up0")
    grads["b_w_in"] = [d_bin[0].transpose(1, 0, 2).reshape(d, 4, 3 * d // 4).transpose(1, 0, 2)]
    grads["b_w_out"] = [d_bout.reshape(4, d // 4, d)]
    grads["ffn_w_up"] = [d_up0[0], d_up1[0]]
    grads["ffn_w_down"] = [t.reshape(4, -1, d) for t in (d_down0, d_down1)]
    complete = complete_grads(grads)

    dr1, dgb1, from_sibling = nt_ln_bwd(dup0, wt["up"][0], dr2, r1, ln_g[0], name="d_h1_ln1", swap=complete)
    leaving = sum_pairs(complete, from_sibling)
    d_aout = tn(onz, dr1, name="d_a_out")
    donz = nt(dr1, wt["a_out"], name="d_onz")
    d_o, dz, dnw = gdn_post_bwd(o[None], pz, donz, wt["anorm_b"], tm=tm)
    dqkv, dgates, landed = gdn_chunk_bwd(qkv, gates, states, tinv, d_o[0], leaving)
    dp3, daconv = gdn_pre_bwd(p3, dqkv, wt["a_conv3"], tm=tm, cb=2 * HEAD_DIM)
    dpba, dscal = gdn_gates_bwd(pba, dgates, wt["alog_lanes"], wt["dtb_lanes"], tm=tm)
    d_a3 = tn(h0b, dp3, name="d_a_in3")
    d_az = tn(h0b, dz, name="d_a_inz")
    d_aba = tn(h0b, dpba, name="d_a_inba")
    dh0 = nt(dp3, wt["a3"], res=dr1, res_scale=ALPHA, name="d_h0a")
    dh0 = nt(dz, wt["az"], res=dh0, res_scale=1.0, name="d_h0z")
    dh0 = nt(dpba, wt["a_ba"], res=dh0, res_scale=1.0, name="d_h0")

    width = HEADS * HEAD_DIM
    d_a_in = jnp.concatenate([d_a3[0, 0], d_a3[0, 1], d_a3[0, 2], d_az[0, 0], d_aba[0, 0][:, :2 * HEADS]], axis=1)
    n_in = d_a_in.shape[1] // 4
    grads["a_w_in"] = [d_a_in.reshape(d, 4, n_in).transpose(1, 0, 2)]
    grads["a_w_out"] = [d_aout.reshape(4, width // 4, d)]
    grads["a_conv"] = daconv.sum(axis=2).transpose(1, 0, 2).reshape(1, GDN_CONV, 3 * width)
    per_head = dscal.sum(axis=1)[:, HEADS:2 * HEADS]
    grads["a_log"] = per_head[0][None]
    grads["a_dt_bias"] = per_head[1][None]
    grads["a_norm"] = dnw.reshape(8, HEADS, HEAD_DIM).sum(axis=(0, 1))[None]
    grads["b_conv"] = dbconv.sum(axis=1)[None]
    lns = [dgb1, dgb2, dgb3, dgb4]
    grads["ln_mix_g"] = jnp.stack([lns[0][0].sum(0), lns[2][0].sum(0)])
    grads["ln_mix_b"] = jnp.stack([lns[0][1].sum(0), lns[2][1].sum(0)])
    grads["ln_ffn_g"] = jnp.stack([lns[1][0].sum(0), lns[3][0].sum(0)])
    grads["ln_ffn_b"] = jnp.stack([lns[1][1].sum(0), lns[3][1].sum(0)])
    grads["ffn_conv"] = jnp.stack([t.sum(axis=2).transpose(1, 0, 2).reshape(FFN_CONV, -1) for t in (dfconv0, dfconv1)])
    grads["meta"] = dh0[0, :N_META]
    return loss_part, dh0, grads, landed


WEIGHTS = ["meta", "a_w_in", "a_conv", "a_log", "a_dt_bias", "a_norm", "a_w_out", "b_w_in", "b_conv", "b_w_out",
           "ln_mix_g", "ln_mix_b", "ffn_w_up", "ffn_conv", "ffn_w_down", "ln_ffn_g", "ln_ffn_b"]
EARLY_WEIGHTS = ["a_w_in", "a_w_out"]
LATE_WEIGHTS = ["b_w_in", "b_w_out", "ffn_w_up", "ffn_w_down"]
MATMUL_WEIGHTS = EARLY_WEIGHTS + LATE_WEIGHTS
SMALL_SHARDED = ["a_conv", "b_conv", "ffn_conv", "meta"]
REPLICATED = ["a_log", "a_dt_bias", "a_norm", "ln_mix_g", "ln_mix_b", "ln_ffn_g", "ln_ffn_b"]
SHARD_AXIS = {"meta": 1, "a_w_in": 2, "a_conv": 2, "a_w_out": 1, "b_w_in": 2, "b_conv": 2, "b_w_out": 1,
              "ffn_w_up": 2, "ffn_conv": 2, "ffn_w_down": 1}
PACK_COLS = 1024
PACK_ROWS_MULTIPLE = 32


def _pack(pieces, lead=()):
    flat = jnp.concatenate([p.reshape(lead + (-1,)) for p in pieces], axis=-1)
    n = flat.shape[-1]
    rows = -(-n // (PACK_COLS * PACK_ROWS_MULTIPLE)) * PACK_ROWS_MULTIPLE
    flat = jnp.pad(flat, [(0, 0)] * len(lead) + [(0, rows * PACK_COLS - n)])
    return flat.reshape(lead + (rows, PACK_COLS))


def _unpack(buf, shapes, lead=()):
    flat = buf.reshape(lead + (-1,))
    out, off = [], 0
    for shp in shapes:
        n = 1
        for s in shp:
            n *= s
        out.append(flat[..., off:off + n].reshape(lead + tuple(shp)))
        off += n
    return out


def _join_shards(stacked, axis):
    return jnp.concatenate([stacked[k] for k in range(4)], axis=axis)


def _split_shards(full, axis):
    return jnp.stack(jnp.split(full, 4, axis=axis))


def _weight_layers(w, names):
    return [w[n][l].astype(BF16) for n in names for l in range(w[n].shape[0])]


def _per_weight(arrays, w, names):
    it = iter(arrays)
    return {n: [next(it) for _ in range(w[n].shape[0])] for n in names}


def _layout_early(full, w):
    width = HEADS * HEAD_DIM
    wt = {n: w[n] for n in ("ln_mix_g", "ln_mix_b", "ln_ffn_g", "ln_ffn_b")}
    w_in = _join_shards(full["a_w_in"][0], 1)
    d = w_in.shape[0]
    n_ff = full["ffn_conv"].shape[2] // 2
    blocks = [w_in[:, s * width:(s + 1) * width] for s in range(4)]
    wt["a3"] = jnp.stack(blocks[:3])[None]
    wt["az"] = blocks[3][None, None]
    wt["a_ba"] = jnp.pad(w_in[:, 4 * width:], ((0, 0), (0, HEAD_DIM - 2 * HEADS)))[None, None]
    wt["a_out"] = full["a_w_out"][0].reshape(1, 1, width, d)
    wt["a_conv3"] = full["a_conv"][0].reshape(GDN_CONV, 3, width).transpose(1, 0, 2)
    wt["b_conv"] = full["b_conv"][0]
    wt["fconv"] = [full["ffn_conv"][l].reshape(FFN_CONV, 2, n_ff).transpose(1, 0, 2) for l in range(2)]
    wt["meta"] = full["meta"]
    in_g_lanes = (HEADS, HEAD_DIM - 2 * HEADS)
    wt["alog_lanes"] = jnp.pad(w["a_log"][0], in_g_lanes)[None]
    wt["dtb_lanes"] = jnp.pad(w["a_dt_bias"][0], in_g_lanes)[None]
    wt["anorm_b"] = jnp.tile(w["a_norm"][0], HEADS)[None]
    return wt


def _layout_late(full):
    d = full["b_w_in"][0].shape[1]
    n_ff = full["ffn_w_up"][0].shape[2]
    return {
        "b_in": _join_shards(full["b_w_in"][0], 1).reshape(d, 3, d).transpose(1, 0, 2)[None],
        "b_out": full["b_w_out"][0].reshape(1, 1, d, d),
        "up": [t[None] for t in full["ffn_w_up"]],
        "down": [t.reshape(2, 1, n_ff, d) for t in full["ffn_w_down"]],
    }


def kernel(x, meta, a_w_in, a_conv, a_log, a_dt_bias, a_norm, a_w_out, b_w_in, b_conv, b_w_out, ln_mix_g, ln_mix_b, ffn_w_up, ffn_conv, ffn_w_down, ln_ffn_g, ln_ffn_b, loss_target, m_meta, m_a_w_in, m_a_conv, m_a_log, m_a_dt_bias, m_a_norm, m_a_w_out, m_b_w_in, m_b_conv, m_b_w_out, m_ln_mix_g, m_ln_mix_b, m_ffn_w_up, m_ffn_conv, m_ffn_w_down, m_ln_ffn_g, m_ln_ffn_b, v_meta, v_a_w_in, v_a_conv, v_a_log, v_a_dt_bias, v_a_norm, v_a_w_out, v_b_w_in, v_b_conv, v_b_w_out, v_ln_mix_g, v_ln_mix_b, v_ffn_w_up, v_ffn_conv, v_ffn_w_down, v_ln_ffn_g, v_ln_ffn_b):
    w = dict(meta=meta, a_w_in=a_w_in, a_conv=a_conv, a_log=a_log, a_dt_bias=a_dt_bias, a_norm=a_norm, a_w_out=a_w_out,
             b_w_in=b_w_in, b_conv=b_conv, b_w_out=b_w_out, ln_mix_g=ln_mix_g, ln_mix_b=ln_mix_b, ffn_w_up=ffn_w_up,
             ffn_conv=ffn_conv, ffn_w_down=ffn_w_down, ln_ffn_g=ln_ffn_g, ln_ffn_b=ln_ffn_b)
    m = dict(meta=m_meta, a_w_in=m_a_w_in, a_conv=m_a_conv, a_log=m_a_log, a_dt_bias=m_a_dt_bias, a_norm=m_a_norm,
             a_w_out=m_a_w_out, b_w_in=m_b_w_in, b_conv=m_b_conv, b_w_out=m_b_w_out, ln_mix_g=m_ln_mix_g,
             ln_mix_b=m_ln_mix_b, ffn_w_up=m_ffn_w_up, ffn_conv=m_ffn_conv, ffn_w_down=m_ffn_w_down,
             ln_ffn_g=m_ln_ffn_g, ln_ffn_b=m_ln_ffn_b)
    v = dict(meta=v_meta, a_w_in=v_a_w_in, a_conv=v_a_conv, a_log=v_a_log, a_dt_bias=v_a_dt_bias, a_norm=v_a_norm,
             a_w_out=v_a_w_out, b_w_in=v_b_w_in, b_conv=v_b_conv, b_w_out=v_b_w_out, ln_mix_g=v_ln_mix_g,
             ln_mix_b=v_ln_mix_b, ffn_w_up=v_ffn_w_up, ffn_conv=v_ffn_conv, ffn_w_down=v_ffn_w_down,
             ln_ffn_g=v_ln_ffn_g, ln_ffn_b=v_ln_ffn_b)
    seq = x.shape[1]
    *stacks, small = all_gather_shards(_weight_layers(w, EARLY_WEIGHTS) + [_pack([w[n] for n in SMALL_SHARDED])],
                                       "gather_early")
    full = _per_weight(stacks, w, EARLY_WEIGHTS)
    for n, t in zip(SMALL_SHARDED, _unpack(small, [w[n].shape for n in SMALL_SHARDED], lead=(4,))):
        full[n] = _join_shards(t, SHARD_AXIS[n])

    def layout_late(late_stacks):
        return _layout_late(_per_weight(late_stacks, w, LATE_WEIGHTS))

    def complete_grads(grads):
        return [g for n in LATE_WEIGHTS for g in grads[n]]

    def sum_pairs(bufs, from_sibling):
        return pair_sums(bufs, from_sibling, [BF16] * len(bufs), "late")

    loss_part, dh0, grads, landed_late = _local_step(x[0], loss_target[0], _layout_early(full, w),
                                                     _weight_layers(w, LATE_WEIGHTS), layout_late, complete_grads, sum_pairs)
    pieces = [_split_shards(grads[n], SHARD_AXIS[n]) for n in SMALL_SHARDED]
    same = jnp.concatenate([grads[n].reshape(-1) for n in REPLICATED] + [jnp.sum(loss_part).reshape(1)])
    pieces.append(jnp.broadcast_to(same, (4,) + same.shape))
    bufs = [g for n in EARLY_WEIGHTS for g in grads[n]] + [_pack(pieces, lead=(4,))]
    from_sibling = swap_halves(bufs, "rs_pair_early")
    landed = scatter_to_chips(pair_sums(bufs, from_sibling, [BF16] * (len(bufs) - 1) + [F32], "early"), "rs_chips_early")
    totals = [chip_sum(t, "rs_chip_sum%d" % i) for i, t in enumerate(landed + landed_late)]
    by_weight = _per_weight(totals[:len(bufs) - 1] + totals[len(bufs):], w, MATMUL_WEIGHTS)
    *shared, small_total = share_halves([by_weight[n] for n in MATMUL_WEIGHTS] + [[totals[len(bufs) - 1]]], "rs_share")
    grad_w = {n: t.reshape(w[n].shape) for n, t in zip(MATMUL_WEIGHTS, shared)}
    rest = SMALL_SHARDED + REPLICATED
    unpacked = _unpack(small_total[0], [w[n].shape for n in rest] + [()])
    grad_w.update(zip(rest, unpacked[:-1]))
    loss = unpacked[-1]
    grad_x = dh0[:, N_META:N_META + seq]
    steps = [adamw(w[n], grad_w[n], m[n], v[n], "adamw_" + n) for n in WEIGHTS]
    return (loss, grad_x, *[grad_w[n] for n in WEIGHTS], *[s[0] for s in steps], *[s[1] for s in steps],
            *[s[2] for s in steps])
```

```python
import functools

import jax
import jax.numpy as jnp
from jax import lax
from jax.experimental import pallas as pl
from jax.experimental.pallas import tpu as pltpu

F32 = jnp.float32
BF16 = jnp.bfloat16

N_META = 16
HEADS = 8
HEAD_DIM = 128
CHUNK = 64
GDN_CONV = 4
FFN_CONV = 3
ALPHA = 4.0 ** 0.25
LN_EPS = 1e-5
RMS_EPS = 1e-6
L2_EPS = 1e-6
Q_SCALE = HEAD_DIM ** -0.5

ADAM_LR = 0.001
ADAM_B1 = 0.9
ADAM_B2 = 0.999
ADAM_EPS = 1e-08
ADAM_WD = 0.01
ADAM_STEP = 10

HALO = 8
VMEM_LIMIT = 48 * 1024 * 1024


def _params(sem=None):
    return pltpu.CompilerParams(dimension_semantics=sem, vmem_limit_bytes=VMEM_LIMIT)


def _dot(a, b, prec=None):
    return jnp.dot(a, b, preferred_element_type=F32, precision=prec)


def _dot_nt(a, b, prec=None):
    return lax.dot_general(a, b, (((1,), (1,)), ((), ())), preferred_element_type=F32, precision=prec)


def _dot_tn(a, b, prec=None):
    return lax.dot_general(a, b, (((0,), (0,)), ((), ())), preferred_element_type=F32, precision=prec)


def _sigmoid(x):
    return 0.5 * jnp.tanh(0.5 * x) + 0.5


def _tri_masks():
    r = lax.broadcasted_iota(jnp.int32, (CHUNK, CHUNK), 0)
    c = lax.broadcasted_iota(jnp.int32, (CHUNK, CHUNK), 1)
    return r >= c, r > c, r == c


def _split_hi_lo(x):
    hi = x.astype(BF16)
    return hi, (x - hi.astype(F32)).astype(BF16)


def _mask_dot(mask, x):
    hi, lo = _split_hi_lo(x)
    return _dot(mask, hi) + _dot(mask, lo)


def _cumsum_rows(g):
    causal, _, _ = _tri_masks()
    return _mask_dot(causal.astype(BF16), g)


def _cumsum_rows_transposed(dy):
    _, strict, _ = _tri_masks()
    return _mask_dot((~strict).astype(BF16), dy)


def _dot_split3(a, b):
    a_hi, a_lo = _split_hi_lo(a)
    b_hi, b_lo = _split_hi_lo(b)
    return _dot(a_hi, b_hi) + (_dot(a_hi, b_lo) + _dot(a_lo, b_hi))


@jax.custom_vjp
def _dot_precise(a, b):
    return _dot_split3(a, b)


def _dot_precise_fwd(a, b):
    return _dot_split3(a, b), (a, b)


def _dot_precise_bwd(operands, ct):
    a, b = operands
    return _dot_split3(ct, b.T), _dot_split3(a.T, ct)


_dot_precise.defvjp(_dot_precise_fwd, _dot_precise_bwd)


def _gdn_m(ks, a64s, bbs):
    causal, strict, _ = _tri_masks()
    decay = [jnp.exp(jnp.where(causal, x - x.T, -1e30)) for x in a64s]
    kk = [_dot_nt(k * b, k) for k, b in zip(ks, bbs)]
    return [jnp.where(strict, x * d, 0.0) for x, d in zip(kk, decay)]


def _gdn_inverse_stages(ks, a64s, bbs):
    ms = _gdn_m(ks, a64s, bbs)
    yield
    r = lax.broadcasted_iota(jnp.int32, (CHUNK, CHUNK), 0)
    c = lax.broadcasted_iota(jnp.int32, (CHUNK, CHUNK), 1)
    eye = (r == c).astype(F32)
    same = [jnp.right_shift(r, s) == jnp.right_shift(c, s) for s in (3, 4, 5)]
    d = [jnp.where(same[0], m, 0.0) for m in ms]
    p = [_dot(x, x) for x in d]
    yield
    t = [eye - x for x in d]
    t = [x + _dot(x, y) for x, y in zip(t, p)]
    p = [_dot(x, x) for x in p]
    yield
    t = [x + _dot(x, y) for x, y in zip(t, p)]
    yield
    for inner, outer in ((same[0], same[1]), (same[1], same[2]), (same[2], None)):
        joins = ~inner if outer is None else (outer & ~inner)
        o = [_dot(x, jnp.where(joins, m, 0.0)) for x, m in zip(t, ms)]
        yield
        t = [x - _dot(y, x) for x, y in zip(t, o)]
        yield
    res = [eye - x - _dot_split3(m, x) for m, x in zip(ms, t)]
    yield
    return [x + _dot(x, y) for x, y in zip(t, res)]


def _gdn_apply_stages(qs, ks, vs, gc, a64s, gl, bbs, ss, ts):
    causal, _, _ = _tri_masks()
    n = range(len(qs))
    qk = [_dot_nt(qs[h], ks[h]) for h in n]
    yield
    decay = [jnp.exp(jnp.where(causal, x - x.T, -1e30)) for x in a64s]
    eg = [jnp.exp(x) for x in gc]
    u = [_dot_precise(ts[h], vs[h] * bbs[h]) for h in n]
    w = [_dot_precise(ts[h], ks[h] * bbs[h] * eg[h]) for h in n]
    qk = [qk[h] * decay[h] for h in n]
    kd = [ks[h] * jnp.exp(gl[h] - gc[h]) for h in n]
    yield
    v_new = [u[h] - _dot(w[h], ss[h]) for h in n]
    q_s = [_dot(qs[h] * eg[h], ss[h]) for h in n]
    yield
    o = [q_s[h] + _dot(qk[h], v_new[h]) for h in n]
    s2 = [ss[h] * jnp.exp(gl[h]) + _dot_tn(kd[h], v_new[h]) for h in n]
    return o, s2


def _run_stages(*generators):
    results = [None] * len(generators)
    live = dict(enumerate(generators))
    while live:
        for i, gen in list(live.items()):
            try:
                next(gen)
            except StopIteration as stop:
                results[i] = stop.value
                del live[i]
    return results


def _head_slices(h):
    return slice(h * HEAD_DIM, (h + 1) * HEAD_DIM), slice(h * HEAD_DIM, h * HEAD_DIM + CHUNK)


def _gdn_head_values(x_ref, gate_ref):
    heads = range(HEADS)
    qs, ks, vs = ([x_ref[s, :, _head_slices(h)[0]] for h in heads] for s in range(3))
    gate = gate_ref[...]
    cumulative = _cumsum_rows(gate)
    total = jnp.sum(gate, axis=0, keepdims=True)
    gcums = [cumulative[:, HEADS + h:HEADS + h + 1] for h in heads]
    gtots = [total[:, HEADS + h:HEADS + h + 1] for h in heads]
    bcols = [gate[:, h:h + 1] for h in heads]
    return qs, ks, vs, gcums, gtots, bcols


def _over_lanes(cols, lanes):
    return [jnp.broadcast_to(c, (c.shape[0], lanes)) for c in cols]


def _gdn_inverse_cols(ks, gcums, bcols):
    return _gdn_inverse_stages(ks, _over_lanes(gcums, CHUNK), _over_lanes(bcols, HEAD_DIM))


def _gdn_apply_cols_stages(qs, ks, vs, gcums, gtots, bcols, ss, ts):
    return _gdn_apply_stages(qs, ks, vs, _over_lanes(gcums, HEAD_DIM), _over_lanes(gcums, CHUNK),
                             _over_lanes(gtots, HEAD_DIM), _over_lanes(bcols, HEAD_DIM), ss, ts)


def _gdn_apply_cols(qs, ks, vs, gcums, gtots, bcols, ss, ts):
    return _run_stages(_gdn_apply_cols_stages(qs, ks, vs, gcums, gtots, bcols, ss, ts))[0]


def _gdn_m_cols(ks, gcums, bcols):
    return _gdn_m(ks, _over_lanes(gcums, CHUNK), _over_lanes(bcols, HEAD_DIM))


def _gate_lanes(bcols, gcols):
    rows = gcols[0].shape[0]
    lane = lax.broadcasted_iota(jnp.int32, (rows, HEAD_DIM), 1)
    out = jnp.zeros((rows, HEAD_DIM), F32)
    for h in range(HEADS):
        if bcols is not None:
            out = jnp.where(lane == h, jnp.broadcast_to(bcols[h], out.shape), out)
        out = jnp.where(lane == HEADS + h, jnp.broadcast_to(gcols[h], out.shape), out)
    return out


def _gate_gradient(dbcols, dgcums, dgtots):
    block = _gate_lanes(dbcols, dgcums)
    lane = lax.broadcasted_iota(jnp.int32, block.shape, 1)
    return jnp.where(lane < HEADS, block, _cumsum_rows_transposed(block) + _gate_lanes(None, dgtots))


GATHER_PASS_ON_CHUNKS = 16


def gdn_chunk_fwd(qkv, gates, gather=()):
    _, lp, width = qkv.shape
    n_chunks = lp // CHUNK
    n = len(gather)

    def body(x_ref, gate_ref, next_ref, next_gate_ref, *refs):
        shard_refs, (o_ref, s_ref, t_ref), refs = refs[:n], refs[n:n + 3], refs[n + 3:]
        stack_refs, state, t_next, sems = refs[:n], refs[n], refs[n + 1], refs[n + 2:]
        copies = _gather_copies(shard_refs, stack_refs, *sems) if n else None

        def inverse_stages(ref, g_ref):
            _, ks, _, gcums, _, bcols = _gdn_head_values(ref, g_ref)
            return _gdn_inverse_cols(ks, gcums, bcols)

        @pl.when(pl.program_id(0) == 0)
        def _():
            state[...] = jnp.zeros_like(state)
            for h, t in enumerate(_run_stages(inverse_stages(x_ref, gate_ref))[0]):
                t_next[h] = t
            if n:
                _gather_start(copies)

        qs, ks, vs, gcums, gtots, bcols = _gdn_head_values(x_ref, gate_ref)
        ss = [state[h] for h in range(HEADS)]
        ts = [t_next[h] for h in range(HEADS)]
        ts_next, (os_, s2) = _run_stages(inverse_stages(next_ref, next_gate_ref),
                                         _gdn_apply_cols_stages(qs, ks, vs, gcums, gtots, bcols, ss, ts))
        for h in range(HEADS):
            s_ref[0, h] = ss[h]
            t_ref[0, h] = ts[h]
            t_next[h] = ts_next[h]
            o_ref[:, _head_slices(h)[0]] = os_[h]
            state[h] = s2[h]

        if n:
            @pl.when(pl.program_id(0) == max(n_chunks - GATHER_PASS_ON_CHUNKS, 0))
            def _():
                _gather_pass_on(copies)

            @pl.when(pl.program_id(0) == n_chunks - 1)
            def _():
                _gather_wait_rest(copies)

    o, states, tinv, *stacks = pl.pallas_call(
        body,
        name="gdn_chunk_fwd",
        grid=(n_chunks,),
        in_specs=[pl.BlockSpec((3, CHUNK, width), lambda c: (0, c, 0)),
                  pl.BlockSpec((CHUNK, HEAD_DIM), lambda c: (c, 0)),
                  pl.BlockSpec((3, CHUNK, width), lambda c: (0, jnp.minimum(c + 1, n_chunks - 1), 0)),
                  pl.BlockSpec((CHUNK, HEAD_DIM), lambda c: (jnp.minimum(c + 1, n_chunks - 1), 0))] + [ANY] * n,
        out_specs=[
            pl.BlockSpec((CHUNK, width), lambda c: (c, 0)),
            pl.BlockSpec((1, HEADS, HEAD_DIM, HEAD_DIM), lambda c: (c, 0, 0, 0)),
            pl.BlockSpec((1, HEADS, CHUNK, CHUNK), lambda c: (c, 0, 0, 0)),
        ] + [ANY] * n,
        out_shape=[
            jax.ShapeDtypeStruct((lp, width), F32),
            jax.ShapeDtypeStruct((n_chunks, HEADS, HEAD_DIM, HEAD_DIM), F32),
            jax.ShapeDtypeStruct((n_chunks, HEADS, CHUNK, CHUNK), F32),
        ] + _gather_out_shapes(gather),
        scratch_shapes=[pltpu.VMEM((HEADS, HEAD_DIM, HEAD_DIM), F32), pltpu.VMEM((HEADS, CHUNK, CHUNK), F32)]
        + (_gather_sems(n) if n else []),
        compiler_params=_params(("arbitrary",)),
    )(qkv, gates, qkv, gates, *gather)
    return o, states, tinv, _set_own_slots(stacks, gather)


def gdn_chunk_bwd(qkv, gates, states, tinv, d_o, scatter=()):
    _, lp, width = qkv.shape
    n_chunks = lp // CHUNK
    last = n_chunks - 1
    n = len(scatter)

    def body(x_ref, gate_ref, s_ref, t_ref, do_ref, *refs):
        leaving_refs, dx_ref, dgate_ref, refs = refs[:n], refs[n], refs[n + 1], refs[n + 2:]
        landing_refs, dstate, sems = refs[:n], refs[n], refs[n + 1:]
        copies = _scatter_copies(leaving_refs, landing_refs, *sems) if n else None

        @pl.when(pl.program_id(0) == 0)
        def _():
            dstate[...] = jnp.zeros_like(dstate)
            if n:
                _scatter_start(copies)

        heads = range(HEADS)
        qs, ks, vs, gcums, gtots, bcols = _gdn_head_values(x_ref, gate_ref)
        ss = [s_ref[0, h] for h in heads]
        ts = [t_ref[0, h] for h in heads]
        d_out = ([do_ref[:, _head_slices(h)[0]] for h in heads], [dstate[h] for h in heads])
        _, vjp_apply = jax.vjp(_gdn_apply_cols, qs, ks, vs, gcums, gtots, bcols, ss, ts)
        dq, dk, dv, dgc, dgt, db, ds, dt = vjp_apply(d_out)
        tts = [t.T for t in ts]
        dm = [_dot(tts[h], dt[h]) for h in heads]
        dm = [-_dot(dm[h], tts[h]) for h in heads]
        _, vjp_m = jax.vjp(_gdn_m_cols, ks, gcums, bcols)
        dk2, dgc2, db2 = vjp_m(dm)
        for h in heads:
            sl = _head_slices(h)[0]
            dx_ref[0, :, sl] = dq[h]
            dx_ref[1, :, sl] = dk[h] + dk2[h]
            dx_ref[2, :, sl] = dv[h]
            dstate[h] = ds[h]
        dgate_ref[...] = _gate_gradient([db[h] + db2[h] for h in heads], [dgc[h] + dgc2[h] for h in heads], dgt)

        if n:
            @pl.when(pl.program_id(0) == n_chunks - 1)
            def _():
                _scatter_finish(copies)

    dqkv, dgates, *landed = pl.pallas_call(
        body,
        name="gdn_chunk_bwd",
        grid=(n_chunks,),
        in_specs=[
            pl.BlockSpec((3, CHUNK, width), lambda c: (0, last - c, 0)),
            pl.BlockSpec((CHUNK, HEAD_DIM), lambda c: (last - c, 0)),
            pl.BlockSpec((1, HEADS, HEAD_DIM, HEAD_DIM), lambda c: (last - c, 0, 0, 0)),
            pl.BlockSpec((1, HEADS, CHUNK, CHUNK), lambda c: (last - c, 0, 0, 0)),
            pl.BlockSpec((CHUNK, width), lambda c: (last - c, 0)),
        ] + [ANY] * n,
        out_specs=[pl.BlockSpec((3, CHUNK, width), lambda c: (0, last - c, 0)),
                   pl.BlockSpec((CHUNK, HEAD_DIM), lambda c: (last - c, 0))] + [ANY] * n,
        out_shape=[jax.ShapeDtypeStruct(qkv.shape, F32), jax.ShapeDtypeStruct(gates.shape, F32)]
        + [jax.ShapeDtypeStruct(b.shape, b.dtype) for b in scatter],
        scratch_shapes=[pltpu.VMEM((HEADS, HEAD_DIM, HEAD_DIM), F32)] + (_scatter_sems(n) if n else []),
        compiler_params=_params(("arbitrary",)),
    )(qkv, gates, states, tinv, d_o, *scatter)
    return dqkv, dgates, _keep_own_slots(landed, scatter)


def mm_nn(a, b, *, tm, name):
    ks, m, tk = a.shape
    _, ns, _, tn = b.shape

    def body(a_ref, b_ref, o_ref):
        p = _dot(a_ref[...].astype(BF16), b_ref[...])

        @pl.when(pl.program_id(2) == 0)
        def _():
            o_ref[...] = p

        @pl.when(pl.program_id(2) > 0)
        def _():
            o_ref[...] += p

    return pl.pallas_call(
        body,
        name=name,
        grid=(ns, m // tm, ks),
        in_specs=[
            pl.BlockSpec((None, tm, tk), lambda n, i, k: (k, i, 0)),
            pl.BlockSpec((None, None, tk, tn), lambda n, i, k: (k, n, 0, 0)),
        ],
        out_specs=pl.BlockSpec((None, tm, tn), lambda n, i, k: (n, i, 0)),
        out_shape=jax.ShapeDtypeStruct((ns, m, tn), F32),
        compiler_params=_params(("arbitrary", "arbitrary", "arbitrary")),
    )(a, b)


def mm_nt(dy, w, *, tm, name, res=None, res_scale=1.0):
    ns, m, tn = dy.shape
    ks, _, tk, _ = w.shape

    def body(*refs):
        if res is None:
            dy_ref, w_ref, o_ref = refs
        else:
            dy_ref, w_ref, r_ref, o_ref = refs
        p = _dot_nt(dy_ref[...].astype(BF16), w_ref[...])

        @pl.when(pl.program_id(2) == 0)
        def _():
            o_ref[...] = p if res is None else p + res_scale * r_ref[...]

        @pl.when(pl.program_id(2) > 0)
        def _():
            o_ref[...] += p

    in_specs = [
        pl.BlockSpec((None, tm, tn), lambda k, i, n: (n, i, 0)),
        pl.BlockSpec((None, None, tk, tn), lambda k, i, n: (k, n, 0, 0)),
    ]
    args = [dy, w]
    if res is not None:
        in_specs.append(pl.BlockSpec((None, tm, tk), lambda k, i, n: (k, i, 0)))
        args.append(res)
    return pl.pallas_call(
        body,
        name=name,
        grid=(ks, m // tm, ns),
        in_specs=in_specs,
        out_specs=pl.BlockSpec((None, tm, tk), lambda k, i, n: (k, i, 0)),
        out_shape=jax.ShapeDtypeStruct((ks, m, tk), F32),
        compiler_params=_params(("arbitrary", "arbitrary", "arbitrary")),
    )(*args)


def mm_tn(x, dy, *, tm, name, rb=None):
    ks, m, tk = x.shape
    ns, _, tn = dy.shape
    rb = tk if rb is None else rb

    def body(x_ref, dy_ref, o_ref):
        @pl.when(pl.program_id(2) == 0)
        def _():
            o_ref[...] = jnp.zeros_like(o_ref)

        dyb = dy_ref[...].astype(BF16)
        for r in range(0, tk, rb):
            o_ref[r:r + rb, :] += _dot_tn(x_ref[:, r:r + rb].astype(BF16), dyb)

    return pl.pallas_call(
        body,
        name=name,
        grid=(ks, ns, m // tm),
        in_specs=[
            pl.BlockSpec((None, tm, tk), lambda k, n, i: (k, i, 0)),
            pl.BlockSpec((None, tm, tn), lambda k, n, i: (n, i, 0)),
        ],
        out_specs=pl.BlockSpec((None, None, tk, tn), lambda k, n, i: (k, n, 0, 0)),
        out_shape=jax.ShapeDtypeStruct((ks, ns, tk, tn), F32),
        compiler_params=_params(("arbitrary", "arbitrary", "arbitrary")),
    )(x, dy)


def _row_partial(x):
    rows, c = x.shape
    return jnp.sum(x.reshape(rows // 8, 8, c), axis=0)


def _layer_norm(r, g, b):
    mu = jnp.mean(r, axis=-1, keepdims=True)
    xc = r - mu
    var = jnp.mean(xc * xc, axis=-1, keepdims=True)
    return xc * lax.rsqrt(var + LN_EPS) * g + b


def _layer_norm_bwd(x, dh, g):
    mu = jnp.mean(x, axis=-1, keepdims=True)
    xc = x - mu
    rstd = lax.rsqrt(jnp.mean(xc * xc, axis=-1, keepdims=True) + LN_EPS)
    xh = xc * rstd
    dxh = dh * g
    m1 = jnp.mean(dxh, axis=-1, keepdims=True)
    m2 = jnp.mean(dxh * xh, axis=-1, keepdims=True)
    return rstd * (dxh - m1 - xh * m2), _row_partial(dh * xh), _row_partial(dh)


def mm_nn_ln(a, b, h_prev, g, beta, *, tm, name):
    ks, m, tk = a.shape
    d = b.shape[3]

    def body(a_ref, b_ref, hp_ref, g_ref, be_ref, r_ref, h_ref, hb_ref):
        p = _dot(a_ref[...].astype(BF16), b_ref[...])

        @pl.when(pl.program_id(1) == 0)
        def _():
            r_ref[...] = p

        @pl.when(pl.program_id(1) > 0)
        def _():
            r_ref[...] += p

        @pl.when(pl.program_id(1) == ks - 1)
        def _():
            r = ALPHA * hp_ref[...] + r_ref[...]
            r_ref[...] = r
            h = _layer_norm(r, g_ref[...], be_ref[...])
            h_ref[...] = h
            hb_ref[...] = h.astype(BF16)

    row = pl.BlockSpec((None, tm, d), lambda i, k: (0, i, 0))
    vec = pl.BlockSpec((1, d), lambda i, k: (0, 0))
    return pl.pallas_call(
        body,
        name=name,
        grid=(m // tm, ks),
        in_specs=[
            pl.BlockSpec((None, tm, tk), lambda i, k: (k, i, 0)),
            pl.BlockSpec((None, None, tk, d), lambda i, k: (k, 0, 0, 0)),
            row, vec, vec,
        ],
        out_specs=[row, row, row],
        out_shape=[jax.ShapeDtypeStruct((1, m, d), F32)] * 2 + [jax.ShapeDtypeStruct((1, m, d), BF16)],
        compiler_params=_params(("arbitrary", "arbitrary")),
    )(a, b, h_prev, g, beta)


def mm_nt_ln_bwd(dy, w, res, r, g, *, tm, name, swap=()):
    ns, m, tn = dy.shape
    d = w.shape[2]
    n_swap = len(swap)
    last_tile = m // tm - 1

    def body(dy_ref, w_ref, res_ref, r_ref, g_ref, *refs):
        leaving_refs, (dr_ref, dgb_ref), refs = refs[:n_swap], refs[n_swap:n_swap + 2], refs[n_swap + 2:]
        copies = _swap_copies(leaving_refs, refs[:n_swap], *refs[n_swap:]) if n_swap else None
        p = _dot_nt(dy_ref[...].astype(BF16), w_ref[...])

        @pl.when((pl.program_id(0) == 0) & (pl.program_id(1) == 0))
        def _():
            dgb_ref[...] = jnp.zeros_like(dgb_ref)
            if n_swap:
                _swap_start(copies)

        @pl.when(pl.program_id(1) == 0)
        def _():
            dr_ref[...] = p + ALPHA * res_ref[...]

        @pl.when(pl.program_id(1) > 0)
        def _():
            dr_ref[...] += p

        @pl.when(pl.program_id(1) == ns - 1)
        def _():
            for rows in (pl.ds(0, tm // 2), pl.ds(tm // 2, tm // 2)):
                dr, dgamma, dbeta = _layer_norm_bwd(r_ref[rows, :], dr_ref[rows, :], g_ref[...])
                dr_ref[rows, :] = dr
                dgb_ref[0] += dgamma
                dgb_ref[1] += dbeta

        if n_swap:
            @pl.when((pl.program_id(0) == last_tile) & (pl.program_id(1) == ns - 1))
            def _():
                _swap_finish(copies)

    row = pl.BlockSpec((None, tm, d), lambda i, n: (0, i, 0))
    dr, dgb, *landed = pl.pallas_call(
        body,
        name=name,
        grid=(m // tm, ns),
        in_specs=[
            pl.BlockSpec((None, tm, tn), lambda i, n: (n, i, 0)),
            pl.BlockSpec((None, None, d, tn), lambda i, n: (0, n, 0, 0)),
            row, row,
            pl.BlockSpec((1, d), lambda i, n: (0, 0)),
        ] + [ANY] * n_swap,
        out_specs=[row, pl.BlockSpec((2, 8, d), lambda i, n: (0, 0, 0))] + [ANY] * n_swap,
        out_shape=[jax.ShapeDtypeStruct((1, m, d), F32), jax.ShapeDtypeStruct((2, 8, d), F32)] + _swap_out_shapes(swap),
        scratch_shapes=_swap_sems(n_swap) if n_swap else [],
        compiler_params=_params(("arbitrary", "arbitrary")),
    )(dy, w, res, r, g, *swap)
    return dr, dgb, landed


def loss_ln_bwd(h, target, r, g, *, first, count, tm):
    _, lp, d = h.shape

    def body(h_ref, t_ref, r_ref, g_ref, dr_ref, dgb_ref, l_ref):
        row = pl.program_id(0) * tm + lax.broadcasted_iota(jnp.int32, (tm, d), 0)
        valid = (row >= first) & (row < first + count)
        err = jnp.where(valid, h_ref[...] - t_ref[...], 0.0)
        dr, dgamma, dbeta = _layer_norm_bwd(r_ref[...], err * (1.0 / d), g_ref[...])
        dr_ref[...] = dr

        @pl.when(pl.program_id(0) == 0)
        def _():
            dgb_ref[...] = jnp.zeros_like(dgb_ref)
            l_ref[...] = jnp.zeros_like(l_ref)

        dgb_ref[0] += dgamma
        dgb_ref[1] += dbeta
        l_ref[...] += _row_partial(err * err) * (0.5 / d)

    row3 = pl.BlockSpec((None, tm, d), lambda i: (0, i, 0))
    return pl.pallas_call(
        body,
        name="loss_ln4_bwd",
        grid=(lp // tm,),
        in_specs=[row3, pl.BlockSpec((tm, d), lambda i: (i, 0)), row3, pl.BlockSpec((1, d), lambda i: (0, 0))],
        out_specs=[row3, pl.BlockSpec((2, 8, d), lambda i: (0, 0, 0)), pl.BlockSpec((8, d), lambda i: (0, 0))],
        out_shape=[jax.ShapeDtypeStruct((1, lp, d), F32), jax.ShapeDtypeStruct((2, 8, d), F32),
                   jax.ShapeDtypeStruct((8, d), F32)],
        compiler_params=_params(("arbitrary",)),
    )(h, target, r, g)


def _halo_index(tile, tm):
    return jnp.maximum(tile * (tm // HALO) - 1, 0)


def _conv_fwd(xs_ref, w, taps, tm):
    acc = w(0) * xs_ref[pl.ds(HALO - taps + 1, tm), :]
    for j in range(1, taps):
        acc += w(j) * xs_ref[pl.ds(HALO - taps + 1 + j, tm), :]
    return acc


def _conv_bwd_x(dcs_ref, w, taps, tm):
    acc = w(0) * dcs_ref[pl.ds(taps - 1, tm), :]
    for j in range(1, taps):
        acc += w(j) * dcs_ref[pl.ds(taps - 1 - j, tm), :]
    return acc


SUB = 8
LANES = 128
PAIR = 2 * SUB
STRIP_UNROLL = 2


def _pair_rows(r0):
    return pl.ds(r0, SUB), pl.ds(r0 + SUB if isinstance(r0, int) else pl.multiple_of(r0 + SUB, SUB), SUB)


def _shift_down(cur, prev, s):
    if s == 0:
        return cur
    row = lax.broadcasted_iota(jnp.int32, cur.shape, 0)
    return jnp.where(row < s, pltpu.roll(prev, s, axis=0), pltpu.roll(cur, s, axis=0))


def _shift_up(cur, nxt, s):
    if s == 0:
        return cur
    row = lax.broadcasted_iota(jnp.int32, cur.shape, 0)
    return jnp.where(row < SUB - s, pltpu.roll(cur, SUB - s, axis=0), pltpu.roll(nxt, SUB - s, axis=0))


def _silu_parts(c):
    sg = _sigmoid(c)
    return c * sg, sg * (1.0 + c * (1.0 - sg))


def _head_sum(x):
    rows, c = x.shape
    parts = []
    for h in range(c // HEAD_DIM):
        s = jnp.sum(x[:, h * HEAD_DIM:(h + 1) * HEAD_DIM], axis=-1, keepdims=True)
        parts.append(jnp.broadcast_to(s, (rows, HEAD_DIM)))
    return parts[0] if len(parts) == 1 else jnp.concatenate(parts, axis=-1)


def _log1p(y):
    u = 1.0 + y
    d = u - 1.0
    return jnp.where(d == 0.0, y, jnp.log(u) * (y / jnp.where(d == 0.0, 1.0, d)))


def _softplus(x):
    return jnp.maximum(x, 0.0) + _log1p(jnp.exp(-jnp.abs(x)))


def _gate_values(x, al, dt):
    lane = lax.broadcasted_iota(jnp.int32, x.shape, 1)
    is_beta, is_g = lane < HEADS, (lane >= HEADS) & (lane < 2 * HEADS)
    return _sigmoid(x), -jnp.exp(al) * _softplus(x + dt), is_beta, is_g


def gdn_gates_fwd(pba, al, dt, *, tm):
    _, lp, width = pba.shape

    def body(x_ref, al_ref, dt_ref, o_ref):
        beta, g, is_beta, is_g = _gate_values(x_ref[...], al_ref[...], dt_ref[...])
        o_ref[...] = jnp.where(is_beta, beta, jnp.where(is_g, g, 0.0))

    vec = pl.BlockSpec((1, width), lambda i: (0, 0))
    return pl.pallas_call(
        body,
        name="gdn_gates_fwd",
        grid=(lp // tm,),
        in_specs=[pl.BlockSpec((None, tm, width), lambda i: (0, i, 0)), vec, vec],
        out_specs=pl.BlockSpec((tm, width), lambda i: (i, 0)),
        out_shape=jax.ShapeDtypeStruct((lp, width), F32),
        compiler_params=_params(("arbitrary",)),
    )(pba, al, dt)


def gdn_gates_bwd(pba, dgates, al, dt, *, tm):
    _, lp, width = pba.shape

    def body(x_ref, d_ref, al_ref, dt_ref, dx_ref, dsc_ref):
        x = x_ref[...]
        beta, g, is_beta, is_g = _gate_values(x, al_ref[...], dt_ref[...])
        d = d_ref[...]
        dg = jnp.where(is_g, d, 0.0)
        da = dg * -jnp.exp(al_ref[...]) * _sigmoid(x + dt_ref[...])
        dx_ref[...] = jnp.where(is_beta, d * beta * (1.0 - beta), da).astype(dx_ref.dtype)

        @pl.when(pl.program_id(0) == 0)
        def _():
            dsc_ref[...] = jnp.zeros_like(dsc_ref)

        dsc_ref[0] += _row_partial(dg * g)
        dsc_ref[1] += _row_partial(da)

    vec = pl.BlockSpec((1, width), lambda i: (0, 0))
    return pl.pallas_call(
        body,
        name="gdn_gates_bwd",
        grid=(lp // tm,),
        in_specs=[pl.BlockSpec((None, tm, width), lambda i: (0, i, 0)), pl.BlockSpec((tm, width), lambda i: (i, 0)), vec, vec],
        out_specs=[pl.BlockSpec((None, tm, width), lambda i: (0, i, 0)), pl.BlockSpec((2, SUB, width), lambda i: (0, 0, 0))],
        out_shape=[jax.ShapeDtypeStruct((1, lp, width), BF16), jax.ShapeDtypeStruct((2, SUB, width), F32)],
        compiler_params=_params(("arbitrary",)),
    )(pba, dgates, al, dt)


def gdn_pre_fwd(p3, conv_w, *, tm, cb):
    _, lp, width = p3.shape
    taps = conv_w.shape[1]

    def body(x_ref, halo_ref, w_ref, o_ref, xs):
        i = pl.program_id(1)
        for s in range(3):
            xs[s, 0:HALO, :] = jnp.where(i > 0, halo_ref[s], 0.0)
            xs[s, HALO:, :] = x_ref[s]
            c = _conv_fwd(xs.at[s], lambda j, s=s: w_ref[s, j:j + 1, :], taps, tm)
            y, _ = _silu_parts(c)
            if s < 2:
                y = y * lax.rsqrt(_head_sum(y * y) + L2_EPS)
                if s == 0:
                    y = y * Q_SCALE
            o_ref[s] = y

    return pl.pallas_call(
        body,
        name="gdn_pre_fwd",
        grid=(width // cb, lp // tm),
        in_specs=[
            pl.BlockSpec((3, tm, cb), lambda j, i: (0, i, j)),
            pl.BlockSpec((3, HALO, cb), lambda j, i: (0, _halo_index(i, tm), j)),
            pl.BlockSpec((3, taps, cb), lambda j, i: (0, 0, j)),
        ],
        out_specs=pl.BlockSpec((3, tm, cb), lambda j, i: (0, i, j)),
        out_shape=jax.ShapeDtypeStruct((3, lp, width), F32),
        scratch_shapes=[pltpu.VMEM((3, tm + HALO, cb), F32)],
        compiler_params=_params(("arbitrary", "arbitrary")),
    )(p3, p3, conv_w)


def gdn_pre_bwd(p3, dqkv, conv_w, *, tm, cb):
    _, lp, width = p3.shape
    taps = conv_w.shape[1]
    last = lp // tm - 1

    def body(x_ref, halo_ref, d_ref, w_ref, dx_ref, dw_ref, xs, dcs, carry):
        step = pl.program_id(1)
        tile = last - step

        @pl.when(step == 0)
        def _():
            carry[...] = jnp.zeros_like(carry)
            dw_ref[...] = jnp.zeros_like(dw_ref)

        for s in range(3):
            w = lambda j, s=s: w_ref[s, j:j + 1, :]
            xs[s, 0:HALO, :] = jnp.where(tile > 0, halo_ref[s], 0.0)
            xs[s, HALO:, :] = x_ref[s]
            c = _conv_fwd(xs.at[s], w, taps, tm)
            y, dsilu = _silu_parts(c)
            dy = d_ref[s]
            if s < 2:
                rn = lax.rsqrt(_head_sum(y * y) + L2_EPS)
                yn = y * rn
                if s == 0:
                    dy = dy * Q_SCALE
                dy = rn * (dy - yn * _head_sum(dy * yn))
            dc = dy * dsilu
            dcs[s, 0:tm, :] = dc
            dcs[s, tm:, :] = carry[s]
            dx_ref[s] = _conv_bwd_x(dcs.at[s], w, taps, tm).astype(dx_ref.dtype)
            carry[s] = dc[0:HALO, :]
            for j in range(taps):
                dw_ref[s, j] += _row_partial(dc * xs[s, pl.ds(HALO - taps + 1 + j, tm), :])

    tile_spec = pl.BlockSpec((3, tm, cb), lambda j, i: (0, last - i, j))
    return pl.pallas_call(
        body,
        name="gdn_pre_bwd",
        grid=(width // cb, lp // tm),
        in_specs=[
            tile_spec,
            pl.BlockSpec((3, HALO, cb), lambda j, i: (0, _halo_index(last - i, tm), j)),
            tile_spec,
            pl.BlockSpec((3, taps, cb), lambda j, i: (0, 0, j)),
        ],
        out_specs=[tile_spec, pl.BlockSpec((3, taps, SUB, cb), lambda j, i: (0, 0, 0, j))],
        out_shape=[jax.ShapeDtypeStruct((3, lp, width), BF16), jax.ShapeDtypeStruct((3, taps, SUB, width), F32)],
        scratch_shapes=[
            pltpu.VMEM((3, tm + HALO, cb), F32),
            pltpu.VMEM((3, tm + HALO, cb), F32),
            pltpu.VMEM((3, HALO, cb), F32),
        ],
        compiler_params=_params(("arbitrary", "arbitrary")),
    )(p3, p3, dqkv, conv_w)


def gdn_post_fwd(o, z, nw_b, *, tm):
    _, lp, width = o.shape

    def body(o_ref, z_ref, nw_ref, y_ref):
        ov = o_ref[...]
        rn = lax.rsqrt(_head_sum(ov * ov) * (1.0 / HEAD_DIM) + RMS_EPS)
        gate, _ = _silu_parts(z_ref[...])
        y_ref[...] = (ov * rn * nw_ref[...] * gate).astype(y_ref.dtype)

    row = pl.BlockSpec((None, tm, width), lambda i: (0, i, 0))
    return pl.pallas_call(
        body,
        name="gdn_post_fwd",
        grid=(lp // tm,),
        in_specs=[row, row, pl.BlockSpec((1, width), lambda i: (0, 0))],
        out_specs=row,
        out_shape=jax.ShapeDtypeStruct((1, lp, width), BF16),
        compiler_params=_params(("arbitrary",)),
    )(o, z, nw_b)


def gdn_post_bwd(o, z, dy, nw_b, *, tm):
    _, lp, width = o.shape

    def body(o_ref, z_ref, dy_ref, nw_ref, do_ref, dz_ref, dnw_ref):
        ov = o_ref[...]
        rn = lax.rsqrt(_head_sum(ov * ov) * (1.0 / HEAD_DIM) + RMS_EPS)
        yn = ov * rn
        gate, dgate = _silu_parts(z_ref[...])
        d_on = dy_ref[...] * gate
        dz_ref[...] = (dy_ref[...] * yn * nw_ref[...] * dgate).astype(dz_ref.dtype)
        a = d_on * nw_ref[...]
        do_ref[...] = rn * (a - yn * (_head_sum(a * yn) * (1.0 / HEAD_DIM)))

        @pl.when(pl.program_id(0) == 0)
        def _():
            dnw_ref[...] = jnp.zeros_like(dnw_ref)

        dnw_ref[...] += _row_partial(d_on * yn)

    row = pl.BlockSpec((None, tm, width), lambda i: (0, i, 0))
    return pl.pallas_call(
        body,
        name="gdn_post_bwd",
        grid=(lp // tm,),
        in_specs=[row, row, row, pl.BlockSpec((1, width), lambda i: (0, 0))],
        out_specs=[row, row, pl.BlockSpec((8, width), lambda i: (0, 0))],
        out_shape=[jax.ShapeDtypeStruct((1, lp, width), F32), jax.ShapeDtypeStruct((1, lp, width), BF16),
                   jax.ShapeDtypeStruct((8, width), F32)],
        compiler_params=_params(("arbitrary",)),
    )(o, z, dy, nw_b)


def ffn_act_fwd(up, conv_w, *, tm, name):
    _, lp, c_w = up.shape
    taps = conv_w.shape[1]

    def body(u_ref, halo_ref, g_ref, w_ref, o_ref):
        first_tile = pl.program_id(1) == 0

        def strip(cur, prev, rows, cs):
            conv = w_ref[taps - 1:taps, cs] * cur
            for j in range(taps - 1):
                conv += w_ref[j:j + 1, cs] * _shift_down(cur, prev, taps - 1 - j)
            y, _ = _silu_parts(conv)
            return y * g_ref[rows, cs]

        def pair(r0, above_of):
            top, bot = _pair_rows(r0)
            for c0 in range(0, c_w, LANES):
                cs = slice(c0, c0 + LANES)
                cur_t, cur_b = u_ref[top, cs], u_ref[bot, cs]
                out = [strip(cur_t, above_of(cs), top, cs), strip(cur_b, cur_t, bot, cs)]
                o_ref[pl.ds(r0, PAIR), cs] = jnp.concatenate(out, axis=0).astype(o_ref.dtype)

        pair(0, lambda cs: jnp.where(first_tile, 0.0, halo_ref[:, cs]))

        def loop_body(s, carry):
            r0 = pl.multiple_of(s * PAIR, PAIR)
            pair(r0, lambda cs: u_ref[pl.ds(pl.multiple_of(r0 - SUB, SUB), SUB), cs])
            return carry

        lax.fori_loop(1, tm // PAIR, loop_body, 0, unroll=STRIP_UNROLL)

    return pl.pallas_call(
        body,
        name=name,
        grid=(2, lp // tm),
        in_specs=[
            pl.BlockSpec((None, tm, c_w), lambda s, i: (s, i, 0)),
            pl.BlockSpec((None, HALO, c_w), lambda s, i: (s, _halo_index(i, tm), 0)),
            pl.BlockSpec((None, tm, c_w), lambda s, i: (2 + s, i, 0)),
            pl.BlockSpec((None, taps, c_w), lambda s, i: (s, 0, 0)),
        ],
        out_specs=pl.BlockSpec((None, tm, c_w), lambda s, i: (s, i, 0)),
        out_shape=jax.ShapeDtypeStruct((2, lp, c_w), BF16),
        compiler_params=_params(("arbitrary", "arbitrary")),
    )(up, up, up, conv_w)


def ffn_act_bwd(up, dact, conv_w, *, tm, name):
    _, lp, c_w = up.shape
    taps = conv_w.shape[1]
    last = lp // tm - 1
    n_pairs = tm // PAIR

    def body(u_ref, halo_ref, g_ref, d_ref, w_ref, dup_ref, dw_ref, below):
        step = pl.program_id(1)
        first_tile = step == last

        @pl.when(step == 0)
        def _():
            below[...] = jnp.zeros_like(below)
            dw_ref[...] = jnp.zeros_like(dw_ref)

        def strip(cur, prev, rows, cs, nxt):
            shifted = [_shift_down(cur, prev, taps - 1 - j) for j in range(taps)]
            conv = w_ref[0:1, cs] * shifted[0]
            for j in range(1, taps):
                conv += w_ref[j:j + 1, cs] * shifted[j]
            y, dsilu = _silu_parts(conv)
            d = d_ref[rows, cs]
            dc = d * g_ref[rows, cs] * dsilu
            dx = w_ref[taps - 1:taps, cs] * dc
            for j in range(taps - 1):
                dx += w_ref[j:j + 1, cs] * _shift_up(dc, nxt, taps - 1 - j)
            return dx, d * y, dc, [dc * s for s in shifted]

        def pair(r0, above_of):
            top, bot = _pair_rows(r0)
            both = pl.ds(r0, PAIR)
            for c0 in range(0, c_w, LANES):
                cs = slice(c0, c0 + LANES)
                cur_t, cur_b = u_ref[top, cs], u_ref[bot, cs]
                dx_b, dg_b, dc_b, dw_b = strip(cur_b, cur_t, bot, cs, below[:, cs])
                dx_t, dg_t, dc_t, dw_t = strip(cur_t, above_of(cs), top, cs, dc_b)
                below[:, cs] = dc_t
                dup_ref[0, both, cs] = jnp.concatenate([dx_t, dx_b], axis=0).astype(dup_ref.dtype)
                dup_ref[1, both, cs] = jnp.concatenate([dg_t, dg_b], axis=0).astype(dup_ref.dtype)
                for j in range(taps):
                    dw_ref[j, :, cs] += dw_t[j] + dw_b[j]

        def loop_body(it, carry):
            r0 = pl.multiple_of((n_pairs - 1 - it) * PAIR, PAIR)
            pair(r0, lambda cs: u_ref[pl.ds(pl.multiple_of(r0 - SUB, SUB), SUB), cs])
            return carry

        lax.fori_loop(0, n_pairs - 1, loop_body, 0, unroll=STRIP_UNROLL)
        pair(0, lambda cs: jnp.where(first_tile, 0.0, halo_ref[:, cs]))

    return pl.pallas_call(
        body,
        name=name,
        grid=(2, lp // tm),
        in_specs=[
            pl.BlockSpec((None, tm, c_w), lambda s, i: (s, last - i, 0)),
            pl.BlockSpec((None, HALO, c_w), lambda s, i: (s, _halo_index(last - i, tm), 0)),
            pl.BlockSpec((None, tm, c_w), lambda s, i: (2 + s, last - i, 0)),
            pl.BlockSpec((None, tm, c_w), lambda s, i: (s, last - i, 0)),
            pl.BlockSpec((None, taps, c_w), lambda s, i: (s, 0, 0)),
        ],
        out_specs=[
            pl.BlockSpec((2, None, tm, c_w), lambda s, i: (0, s, last - i, 0)),
            pl.BlockSpec((None, taps, SUB, c_w), lambda s, i: (s, 0, 0, 0)),
        ],
        out_shape=[jax.ShapeDtypeStruct((2, 2, lp, c_w), BF16), jax.ShapeDtypeStruct((2, taps, SUB, c_w), F32)],
        scratch_shapes=[pltpu.VMEM((SUB, c_w), F32)],
        compiler_params=_params(("arbitrary", "arbitrary")),
    )(up, up, up, dact, conv_w)


def sc_fwd(pb, conv_w, *, tm, cb):
    _, lp, width = pb.shape
    taps = conv_w.shape[0]

    def body(x_ref, halo_ref, w_ref, o_ref):
        first_tile = pl.program_id(1) == 0

        def strip(cur, prev, rows, cs):
            conv = w_ref[taps - 1:taps, cs] * cur
            for j in range(taps - 1):
                conv += w_ref[j:j + 1, cs] * _shift_down(cur, prev, taps - 1 - j)
            return x_ref[0, rows, cs] * conv

        def pair(r0, above_of):
            top, bot = _pair_rows(r0)
            for c0 in range(0, cb, LANES):
                cs = slice(c0, c0 + LANES)
                cur_t = x_ref[1, top, cs] * x_ref[2, top, cs]
                cur_b = x_ref[1, bot, cs] * x_ref[2, bot, cs]
                out = [strip(cur_t, above_of(cs), top, cs), strip(cur_b, cur_t, bot, cs)]
                o_ref[pl.ds(r0, PAIR), cs] = jnp.concatenate(out, axis=0).astype(o_ref.dtype)

        pair(0, lambda cs: jnp.where(first_tile, 0.0, halo_ref[1, :, cs] * halo_ref[2, :, cs]))

        def loop_body(k, carry):
            r0 = pl.multiple_of(k * PAIR, PAIR)
            before = pl.ds(pl.multiple_of(r0 - SUB, SUB), SUB)
            pair(r0, lambda cs: x_ref[1, before, cs] * x_ref[2, before, cs])
            return carry

        lax.fori_loop(1, tm // PAIR, loop_body, 0, unroll=STRIP_UNROLL)

    return pl.pallas_call(
        body,
        name="sc_fwd",
        grid=(width // cb, lp // tm),
        in_specs=[
            pl.BlockSpec((3, tm, cb), lambda j, i: (0, i, j)),
            pl.BlockSpec((3, HALO, cb), lambda j, i: (0, _halo_index(i, tm), j)),
            pl.BlockSpec((taps, cb), lambda j, i: (0, j)),
        ],
        out_specs=pl.BlockSpec((None, tm, cb), lambda j, i: (0, i, j)),
        out_shape=jax.ShapeDtypeStruct((1, lp, width), BF16),
        compiler_params=_params(("arbitrary", "arbitrary")),
    )(pb, pb, conv_w)


def sc_bwd(pb, ds, conv_w, *, tm, cb):
    _, lp, width = pb.shape
    taps = conv_w.shape[0]
    last = lp // tm - 1
    n_pairs = tm // PAIR

    def body(x_ref, halo_ref, d_ref, w_ref, dx_ref, dw_ref, below):
        step = pl.program_id(1)
        first_tile = step == last

        @pl.when(step == 0)
        def _():
            below[...] = jnp.zeros_like(below)
            dw_ref[...] = jnp.zeros_like(dw_ref)

        def strip(cur, prev, rows, cs, nxt):
            gate, left, right = x_ref[0, rows, cs], x_ref[1, rows, cs], x_ref[2, rows, cs]
            shifted = [_shift_down(cur, prev, taps - 1 - j) for j in range(taps)]
            conv = w_ref[0:1, cs] * shifted[0]
            for j in range(1, taps):
                conv += w_ref[j:j + 1, cs] * shifted[j]
            d = d_ref[rows, cs]
            dc = d * gate
            dp = w_ref[taps - 1:taps, cs] * dc
            for j in range(taps - 1):
                dp += w_ref[j:j + 1, cs] * _shift_up(dc, nxt, taps - 1 - j)
            return d * conv, dp * right, dp * left, dc, [dc * s for s in shifted]

        def pair(r0, above_of):
            top, bot = _pair_rows(r0)
            both = pl.ds(r0, PAIR)
            for c0 in range(0, cb, LANES):
                cs = slice(c0, c0 + LANES)
                cur_t = x_ref[1, top, cs] * x_ref[2, top, cs]
                cur_b = x_ref[1, bot, cs] * x_ref[2, bot, cs]
                *dx_b, dc_b, dw_b = strip(cur_b, cur_t, bot, cs, below[:, cs])
                *dx_t, dc_t, dw_t = strip(cur_t, above_of(cs), top, cs, dc_b)
                below[:, cs] = dc_t
                for s in range(3):
                    dx_ref[s, both, cs] = jnp.concatenate([dx_t[s], dx_b[s]], axis=0).astype(dx_ref.dtype)
                for j in range(taps):
                    dw_ref[j, :, cs] += dw_t[j] + dw_b[j]

        def loop_body(it, carry):
            r0 = pl.multiple_of((n_pairs - 1 - it) * PAIR, PAIR)
            before = pl.ds(pl.multiple_of(r0 - SUB, SUB), SUB)
            pair(r0, lambda cs: x_ref[1, before, cs] * x_ref[2, before, cs])
            return carry

        lax.fori_loop(0, n_pairs - 1, loop_body, 0, unroll=STRIP_UNROLL)
        pair(0, lambda cs: jnp.where(first_tile, 0.0, halo_ref[1, :, cs] * halo_ref[2, :, cs]))

    tile_spec = pl.BlockSpec((3, tm, cb), lambda j, i: (0, last - i, j))
    return pl.pallas_call(
        body,
        name="sc_bwd",
        grid=(width // cb, lp // tm),
        in_specs=[
            tile_spec,
            pl.BlockSpec((3, HALO, cb), lambda j, i: (0, _halo_index(last - i, tm), j)),
            pl.BlockSpec((None, tm, cb), lambda j, i: (0, last - i, j)),
            pl.BlockSpec((taps, cb), lambda j, i: (0, j)),
        ],
        out_specs=[tile_spec, pl.BlockSpec((taps, SUB, cb), lambda j, i: (0, 0, j))],
        out_shape=[jax.ShapeDtypeStruct((3, lp, width), BF16), jax.ShapeDtypeStruct((taps, SUB, width), F32)],
        scratch_shapes=[pltpu.VMEM((SUB, cb), F32)],
        compiler_params=_params(("arbitrary", "arbitrary")),
    )(pb, pb, ds, conv_w)


TILE_BYTES = 1536 * 1024


def _rows_tile(rows, cols, multiple=8):
    if rows * cols * 4 <= TILE_BYTES or rows % multiple:
        return rows
    best = multiple
    for t in range(multiple, rows + 1, multiple):
        if rows % t == 0 and t * cols * 4 <= TILE_BYTES:
            best = t
    return best


def pair_sum(g, landed, core, out_dtype, name):
    _, rows, cols = g.shape
    half = rows // 2
    tr = _rows_tile(half, cols, 16)
    nb = half // tr

    def body(c_ref, g_ref, l_ref, o_ref):
        o_ref[...] = (g_ref[...] + l_ref[...]).astype(out_dtype)

    return pl.pallas_call(
        body,
        name=name,
        grid_spec=pltpu.PrefetchScalarGridSpec(
            num_scalar_prefetch=1,
            grid=(4, nb),
            in_specs=[
                pl.BlockSpec((None, tr, cols), lambda s, i, c: (s, c[0] * nb + i, 0)),
                pl.BlockSpec((None, tr, cols), lambda s, i, c: (s, i, 0)),
            ],
            out_specs=pl.BlockSpec((None, tr, cols), lambda s, i, c: (s, i, 0)),
        ),
        out_shape=jax.ShapeDtypeStruct((4, half, cols), out_dtype),
        compiler_params=_params(("arbitrary", "arbitrary")),
    )(core, g, landed)


def chip_sum(x, name):
    _, rows, cols = x.shape
    tr = _rows_tile(rows, cols, 16)

    def body(x0, x1, x2, x3, o_ref):
        acc = x0[...].astype(F32) + x1[...].astype(F32)
        o_ref[...] = (acc + x2[...].astype(F32)) + x3[...].astype(F32)

    return pl.pallas_call(
        body,
        name=name,
        grid=(rows // tr,),
        in_specs=[pl.BlockSpec((None, tr, cols), lambda i, k=k: (k, i, 0)) for k in range(4)],
        out_specs=pl.BlockSpec((tr, cols), lambda i: (i, 0)),
        out_shape=jax.ShapeDtypeStruct((rows, cols), F32),
        compiler_params=_params(("arbitrary",)),
    )(x, x, x, x)


def adamw(w, g, m, v, name):
    shape = w.shape
    cols = shape[-1]
    rows = w.size // cols
    tr = _rows_tile(rows, cols)

    def body(w_ref, g_ref, m_ref, v_ref, d_ref, m2_ref, v2_ref):
        gv = g_ref[...]
        m2 = ADAM_B1 * m_ref[...] + (1.0 - ADAM_B1) * gv
        v2 = ADAM_B2 * v_ref[...] + (1.0 - ADAM_B2) * (gv * gv)
        m_hat = m2 / (1.0 - ADAM_B1 ** ADAM_STEP)
        v_hat = v2 / (1.0 - ADAM_B2 ** ADAM_STEP)
        d_ref[...] = -ADAM_LR * (m_hat / (jnp.sqrt(v_hat) + ADAM_EPS) + ADAM_WD * w_ref[...])
        m2_ref[...] = m2
        v2_ref[...] = v2

    spec = pl.BlockSpec((tr, cols), lambda i: (i, 0))
    outs = pl.pallas_call(
        body,
        name=name,
        grid=(rows // tr,),
        in_specs=[spec] * 4,
        out_specs=[spec] * 3,
        out_shape=[jax.ShapeDtypeStruct((rows, cols), F32)] * 3,
        compiler_params=_params(("arbitrary",)),
    )(*[t.reshape(rows, cols) for t in (w, g, m, v)])
    return tuple(o.reshape(shape) for o in outs)


MESH_ID = pl.DeviceIdType.MESH
ANY = pl.BlockSpec(memory_space=pl.ANY)


def _place():
    x, y, c = lax.axis_index("x"), lax.axis_index("y"), lax.axis_index("c")
    other_chips = [(1 - x, y), (x, 1 - y), (1 - x, 1 - y)]
    return x, y, c, other_chips


def all_gather_shards(bufs, name):
    n = len(bufs)

    def body(*refs):
        x_refs, o_refs = refs[:n], refs[n:2 * n]
        copies = _gather_copies(x_refs, o_refs, *refs[2 * n:])
        _gather_start(copies)
        _gather_finish(copies)

    outs = pl.pallas_call(
        body,
        name=name,
        in_specs=[ANY] * n,
        out_specs=[ANY] * n,
        out_shape=_gather_out_shapes(bufs),
        scratch_shapes=_gather_sems(n),
    )(*bufs)
    return _set_own_slots(outs, bufs)


def _gather_out_shapes(bufs):
    return [jax.ShapeDtypeStruct((4,) + b.shape, b.dtype) for b in bufs]


def _gather_sems(n):
    return [pltpu.SemaphoreType.DMA((6 * n,)), pltpu.SemaphoreType.DMA((6 * n,))]


def _set_own_slots(outs, bufs):
    if not outs:
        return []
    me = 2 * lax.axis_index("x") + lax.axis_index("y")
    return [lax.dynamic_update_index_in_dim(o, b, me, 0) for o, b in zip(outs, bufs)]


def _gather_copies(x_refs, o_refs, send_sems, recv_sems):
    x, y, c, chips = _place()
    me = 2 * x + y
    sibling = (x, y, 1 - c)

    def part(a, slot, hf):
        half = x_refs[a].shape[0] // 2
        return o_refs[a].at[slot, pl.ds(hf * half, half), :]

    def mine(a):
        half = x_refs[a].shape[0] // 2
        return x_refs[a].at[pl.ds(c * half, half), :]

    def copy(k, src, dst, to):
        return pltpu.make_async_remote_copy(src_ref=src, dst_ref=dst, send_sem=send_sems.at[k],
                                            recv_sem=recv_sems.at[k], device_id=to, device_id_type=MESH_ID)

    sends, arrivals, passes, passed = [], [], [], []
    for a in range(len(x_refs)):
        for j, (px, py) in enumerate(chips):
            landed, theirs = part(a, 2 * px + py, c), part(a, 2 * px + py, 1 - c)
            sends.append(copy(6 * a + j, mine(a), part(a, me, c), (px, py, c)))
            arrivals.append(copy(6 * a + j, mine(a), landed, (px, py, c)))
            passes.append(copy(6 * a + 3 + j, landed, landed, sibling))
            passed.append(copy(6 * a + 3 + j, theirs, theirs, sibling))
    return sends, arrivals, passes, passed


def _gather_start(copies):
    for cp in copies[0]:
        cp.start()


def _gather_pass_on(copies):
    _, arrivals, passes, _ = copies
    for arrival, cp in zip(arrivals, passes):
        arrival.wait_recv()
        cp.start()


def _gather_wait_rest(copies):
    sends, _, passes, passed = copies
    for cp in passed:
        cp.wait_recv()
    for cp in sends + passes:
        cp.wait_send()


def _gather_finish(copies):
    _gather_pass_on(copies)
    _gather_wait_rest(copies)


def swap_halves(bufs, name):
    n = len(bufs)

    def body(*refs):
        copies = _swap_copies(refs[:n], refs[n:2 * n], *refs[2 * n:])
        _swap_start(copies)
        _swap_finish(copies)

    return pl.pallas_call(
        body,
        name=name,
        in_specs=[ANY] * n,
        out_specs=[ANY] * n,
        out_shape=_swap_out_shapes(bufs),
        scratch_shapes=_swap_sems(n),
    )(*bufs)


def _swap_out_shapes(bufs):
    return [jax.ShapeDtypeStruct((4, b.shape[1] // 2, b.shape[2]), b.dtype) for b in bufs]


def _swap_sems(n):
    return [pltpu.SemaphoreType.DMA((n,)), pltpu.SemaphoreType.DMA((n,))]


def _swap_copies(x_refs, o_refs, send_sems, recv_sems):
    x, y, c, _ = _place()
    copies = []
    for a, (x_ref, o_ref) in enumerate(zip(x_refs, o_refs)):
        half = x_ref.shape[1] // 2
        copies.append(pltpu.make_async_remote_copy(src_ref=x_ref.at[:, pl.ds((1 - c) * half, half), :], dst_ref=o_ref,
                                                   send_sem=send_sems.at[a], recv_sem=recv_sems.at[a],
                                                   device_id=(x, y, 1 - c), device_id_type=MESH_ID))
    return copies


def _swap_start(copies):
    for cp in copies:
        cp.start()


def _swap_finish(copies):
    for cp in copies:
        cp.wait()


def scatter_to_chips(bufs, name):
    n = len(bufs)

    def body(*refs):
        x_refs, o_refs = refs[:n], refs[n:2 * n]
        copies = _scatter_copies(x_refs, o_refs, *refs[2 * n:])
        _scatter_start(copies)
        _scatter_finish(copies)

    outs = pl.pallas_call(
        body,
        name=name,
        in_specs=[ANY] * n,
        out_specs=[ANY] * n,
        out_shape=[jax.ShapeDtypeStruct(b.shape, b.dtype) for b in bufs],
        scratch_shapes=_scatter_sems(n),
    )(*bufs)
    return _keep_own_slots(outs, bufs)


def _scatter_sems(n):
    return [pltpu.SemaphoreType.DMA((3 * n,)), pltpu.SemaphoreType.DMA((3 * n,))]


def _keep_own_slots(outs, bufs):
    if not outs:
        return []
    me = 2 * lax.axis_index("x") + lax.axis_index("y")
    return [lax.dynamic_update_index_in_dim(o, lax.dynamic_index_in_dim(b, me, 0, keepdims=False), me, 0)
            for o, b in zip(outs, bufs)]


def _scatter_copies(x_refs, o_refs, send_sems, recv_sems):
    x, y, c, chips = _place()
    me = 2 * x + y

    def copy(a, j, src_slot, dst_slot, px, py):
        return pltpu.make_async_remote_copy(src_ref=x_refs[a].at[src_slot], dst_ref=o_refs[a].at[dst_slot],
                                            send_sem=send_sems.at[3 * a + j], recv_sem=recv_sems.at[3 * a + j],
                                            device_id=(px, py, c), device_id_type=MESH_ID)

    sends = [copy(a, j, 2 * px + py, me, px, py) for a in range(len(x_refs)) for j, (px, py) in enumerate(chips)]
    arrivals = [copy(a, j, me, 2 * px + py, px, py) for a in range(len(x_refs)) for j, (px, py) in enumerate(chips)]
    return sends, arrivals


def _scatter_start(copies):
    for cp in copies[0]:
        cp.start()


def _scatter_finish(copies):
    for cp in copies[1]:
        cp.wait_recv()
    for cp in copies[0]:
        cp.wait_send()


def share_halves(groups, name):
    bufs = [b for grp in groups for b in grp]
    where = [(gi, li) for gi, grp in enumerate(groups) for li in range(len(grp))]
    n = len(bufs)

    def body(*refs):
        x_refs, o_refs = refs[:n], refs[n:n + len(groups)]
        send_sems, recv_sems = refs[n + len(groups):]
        x, y, c, _ = _place()
        sent, arrive = [], []
        for a, (gi, li) in enumerate(where):

            def copy(hf, a=a, gi=gi, li=li):
                return pltpu.make_async_remote_copy(src_ref=x_refs[a], dst_ref=o_refs[gi].at[li, hf],
                                                    send_sem=send_sems.at[a], recv_sem=recv_sems.at[a],
                                                    device_id=(x, y, 1 - c), device_id_type=MESH_ID)

            sent.append(copy(c))
            arrive.append(copy(1 - c))
        for cp in sent:
            cp.start()
        for cp in arrive:
            cp.wait_recv()
        for cp in sent:
            cp.wait_send()

    outs = pl.pallas_call(
        body,
        name=name,
        in_specs=[ANY] * n,
        out_specs=[ANY] * len(groups),
        out_shape=[jax.ShapeDtypeStruct((len(grp), 2) + grp[0].shape, grp[0].dtype) for grp in groups],
        scratch_shapes=[pltpu.SemaphoreType.DMA((n,)), pltpu.SemaphoreType.DMA((n,))],
    )(*bufs)
    c = lax.axis_index("c")
    full = [lax.dynamic_update_index_in_dim(o, jnp.stack(grp), c, 1) for o, grp in zip(outs, groups)]
    return [t.reshape(t.shape[0], 2 * t.shape[2], t.shape[3]) for t in full]


def pair_sums(bufs, landed, dtypes, tag):
    core = lax.axis_index("c").astype(jnp.int32).reshape(1)
    return [pair_sum(b, l, core, dt, "rs_pair_sum_%s%d" % (tag, i)) for i, (b, l, dt) in enumerate(zip(bufs, landed, dtypes))]


def _row_tiles(length):
    return (640, 640) if length > 2048 else (128, 64)


def _divisor_tile(rows, target):
    return max(t for t in range(8, min(rows, target) + 1, 8) if rows % t == 0)


def _local_step(x, target, wt, late_shards, layout_late, complete_grads, sum_pairs):
    seq, d = x.shape
    length = N_META + seq
    tm, tm_ffn = _row_tiles(length)
    lp = -(-length // tm) * tm
    tail = jnp.zeros((lp - length, d), F32)
    h0 = jnp.concatenate([wt["meta"], x, tail], axis=0)[None]
    tgt = jnp.concatenate([jnp.zeros((N_META, d), F32), target, tail], axis=0)
    nn = functools.partial(mm_nn, tm=_divisor_tile(lp, 1664))
    nt = functools.partial(mm_nt, tm=_divisor_tile(lp, 1664))
    tn = functools.partial(mm_tn, tm=_divisor_tile(lp, 1664), rb=256)
    nn_ln = functools.partial(mm_nn_ln, tm=_divisor_tile(lp, 832))
    nt_ln_bwd = functools.partial(mm_nt_ln_bwd, tm=_divisor_tile(lp, 1040))
    ln_g = [wt["ln_mix_g"][0:1], wt["ln_ffn_g"][0:1], wt["ln_mix_g"][1:2], wt["ln_ffn_g"][1:2]]
    ln_b = [wt["ln_mix_b"][0:1], wt["ln_ffn_b"][0:1], wt["ln_mix_b"][1:2], wt["ln_ffn_b"][1:2]]

    h0b = h0.astype(BF16)
    p3 = nn(h0b, wt["a3"], name="a_in3")
    pz = nn(h0b, wt["az"], name="a_inz")
    pba = nn(h0b, wt["a_ba"], name="a_inba")
    qkv = gdn_pre_fwd(p3, wt["a_conv3"], tm=tm, cb=2 * HEAD_DIM)
    gates = gdn_gates_fwd(pba, wt["alog_lanes"], wt["dtb_lanes"], tm=tm)
    o, states, tinv, late_stacks = gdn_chunk_fwd(qkv, gates, late_shards)
    wt = {**wt, **layout_late(late_stacks)}
    onz = gdn_post_fwd(o[None], pz, wt["anorm_b"], tm=tm)
    r1, h1, h1b = nn_ln(onz, wt["a_out"], h0, ln_g[0], ln_b[0], name="a_out_ln1")
    up0 = nn(h1b, wt["up"][0], name="up0")
    act0 = ffn_act_fwd(up0, wt["fconv"][0], tm=tm_ffn, name="ffn_act0")
    r2, h2, h2b = nn_ln(act0, wt["down"][0], h1, ln_g[1], ln_b[1], name="down0_ln2")
    pb = nn(h2b, wt["b_in"], name="b_in")
    sc = sc_fwd(pb, wt["b_conv"], tm=tm_ffn, cb=d)
    r3, h3, h3b = nn_ln(sc, wt["b_out"], h2, ln_g[2], ln_b[2], name="b_out_ln3")
    up1 = nn(h3b, wt["up"][1], name="up1")
    act1 = ffn_act_fwd(up1, wt["fconv"][1], tm=tm_ffn, name="ffn_act1")
    r4, h4, _ = nn_ln(act1, wt["down"][1], h3, ln_g[3], ln_b[3], name="down1_ln4")

    grads = {}
    dr4, dgb4, loss_part = loss_ln_bwd(h4, tgt, r4, ln_g[3], first=N_META, count=seq, tm=tm)
    d_down1 = tn(act1, dr4, name="d_down1")
    dact1 = nt(dr4, wt["down"][1], name="d_act1")
    dup1, dfconv1 = ffn_act_bwd(up1, dact1, wt["fconv"][1], tm=tm_ffn, name="ffn_act1_bwd")
    dup1 = dup1.reshape(up1.shape)
    d_up1 = tn(h3b, dup1, name="d_up1")

    dr3, dgb3, _ = nt_ln_bwd(dup1, wt["up"][1], dr4, r3, ln_g[2], name="d_h3_ln3")
    d_bout = tn(sc, dr3, name="d_b_out")
    dsc = nt(dr3, wt["b_out"], name="d_sc")
    dpb, dbconv = sc_bwd(pb, dsc, wt["b_conv"], tm=tm_ffn, cb=d)
    d_bin = tn(h2b, dpb, name="d_b_in")

    dr2, dgb2, _ = nt_ln_bwd(dpb, wt["b_in"], dr3, r2, ln_g[1], name="d_h2_ln2")
    d_down0 = tn(act0, dr2, name="d_down0")
    dact0 = nt(dr2, wt["down"][0], name="d_act0")
    dup0, dfconv0 = ffn_act_bwd(up0, dact0, wt["fconv"][0], tm=tm_ffn, name="ffn_act0_bwd")
    dup0 = dup0.reshape(up0.shape)
    d_up0 = tn(h1b, dup0, name="d_up0")
    grads["b_w_in"] = [d_bin[0].transpose(1, 0, 2).reshape(d, 4, 3 * d // 4).transpose(1, 0, 2)]
    grads["b_w_out"] = [d_bout.reshape(4, d // 4, d)]
    grads["ffn_w_up"] = [d_up0[0], d_up1[0]]
    grads["ffn_w_down"] = [t.reshape(4, -1, d) for t in (d_down0, d_down1)]
    complete = complete_grads(grads)

    dr1, dgb1, from_sibling = nt_ln_bwd(dup0, wt["up"][0], dr2, r1, ln_g[0], name="d_h1_ln1", swap=complete)
    leaving = sum_pairs(complete, from_sibling)
    d_aout = tn(onz, dr1, name="d_a_out")
    donz = nt(dr1, wt["a_out"], name="d_onz")
    d_o, dz, dnw = gdn_post_bwd(o[None], pz, donz, wt["anorm_b"], tm=tm)
    dqkv, dgates, landed = gdn_chunk_bwd(qkv, gates, states, tinv, d_o[0], leaving)
    dp3, daconv = gdn_pre_bwd(p3, dqkv, wt["a_conv3"], tm=tm, cb=2 * HEAD_DIM)
    dpba, dscal = gdn_gates_bwd(pba, dgates, wt["alog_lanes"], wt["dtb_lanes"], tm=tm)
    d_a3 = tn(h0b, dp3, name="d_a_in3")
    d_az = tn(h0b, dz, name="d_a_inz")
    d_aba = tn(h0b, dpba, name="d_a_inba")
    dh0 = nt(dp3, wt["a3"], res=dr1, res_scale=ALPHA, name="d_h0a")
    dh0 = nt(dz, wt["az"], res=dh0, res_scale=1.0, name="d_h0z")
    dh0 = nt(dpba, wt["a_ba"], res=dh0, res_scale=1.0, name="d_h0")

    width = HEADS * HEAD_DIM
    d_a_in = jnp.concatenate([d_a3[0, 0], d_a3[0, 1], d_a3[0, 2], d_az[0, 0], d_aba[0, 0][:, :2 * HEADS]], axis=1)
    n_in = d_a_in.shape[1] // 4
    grads["a_w_in"] = [d_a_in.reshape(d, 4, n_in).transpose(1, 0, 2)]
    grads["a_w_out"] = [d_aout.reshape(4, width // 4, d)]
    grads["a_conv"] = daconv.sum(axis=2).transpose(1, 0, 2).reshape(1, GDN_CONV, 3 * width)
    per_head = dscal.sum(axis=1)[:, HEADS:2 * HEADS]
    grads["a_log"] = per_head[0][None]
    grads["a_dt_bias"] = per_head[1][None]
    grads["a_norm"] = dnw.reshape(8, HEADS, HEAD_DIM).sum(axis=(0, 1))[None]
    grads["b_conv"] = dbconv.sum(axis=1)[None]
    lns = [dgb1, dgb2, dgb3, dgb4]
    grads["ln_mix_g"] = jnp.stack([lns[0][0].sum(0), lns[2][0].sum(0)])
    grads["ln_mix_b"] = jnp.stack([lns[0][1].sum(0), lns[2][1].sum(0)])
    grads["ln_ffn_g"] = jnp.stack([lns[1][0].sum(0), lns[3][0].sum(0)])
    grads["ln_ffn_b"] = jnp.stack([lns[1][1].sum(0), lns[3][1].sum(0)])
    grads["ffn_conv"] = jnp.stack([t.sum(axis=2).transpose(1, 0, 2).reshape(FFN_CONV, -1) for t in (dfconv0, dfconv1)])
    grads["meta"] = dh0[0, :N_META]
    return loss_part, dh0, grads, landed


WEIGHTS = ["meta", "a_w_in", "a_conv", "a_log", "a_dt_bias", "a_norm", "a_w_out", "b_w_in", "b_conv", "b_w_out",
           "ln_mix_g", "ln_mix_b", "ffn_w_up", "ffn_conv", "ffn_w_down", "ln_ffn_g", "ln_ffn_b"]
EARLY_WEIGHTS = ["a_w_in", "a_w_out"]
LATE_WEIGHTS = ["b_w_in", "b_w_out", "ffn_w_up", "ffn_w_down"]
MATMUL_WEIGHTS = EARLY_WEIGHTS + LATE_WEIGHTS
SMALL_SHARDED = ["a_conv", "b_conv", "ffn_conv", "meta"]
REPLICATED = ["a_log", "a_dt_bias", "a_norm", "ln_mix_g", "ln_mix_b", "ln_ffn_g", "ln_ffn_b"]
SHARD_AXIS = {"meta": 1, "a_w_in": 2, "a_conv": 2, "a_w_out": 1, "b_w_in": 2, "b_conv": 2, "b_w_out": 1,
              "ffn_w_up": 2, "ffn_conv": 2, "ffn_w_down": 1}
PACK_COLS = 1024
PACK_ROWS_MULTIPLE = 32


def _pack(pieces, lead=()):
    flat = jnp.concatenate([p.reshape(lead + (-1,)) for p in pieces], axis=-1)
    n = flat.shape[-1]
    rows = -(-n // (PACK_COLS * PACK_ROWS_MULTIPLE)) * PACK_ROWS_MULTIPLE
    flat = jnp.pad(flat, [(0, 0)] * len(lead) + [(0, rows * PACK_COLS - n)])
    return flat.reshape(lead + (rows, PACK_COLS))


def _unpack(buf, shapes, lead=()):
    flat = buf.reshape(lead + (-1,))
    out, off = [], 0
    for shp in shapes:
        n = 1
        for s in shp:
            n *= s
        out.append(flat[..., off:off + n].reshape(lead + tuple(shp)))
        off += n
    return out


def _join_shards(stacked, axis):
    return jnp.concatenate([stacked[k] for k in range(4)], axis=axis)


def _split_shards(full, axis):
    return jnp.stack(jnp.split(full, 4, axis=axis))


def _weight_layers(w, names):
    return [w[n][l].astype(BF16) for n in names for l in range(w[n].shape[0])]


def _per_weight(arrays, w, names):
    it = iter(arrays)
    return {n: [next(it) for _ in range(w[n].shape[0])] for n in names}


def _layout_early(full, w):
    width = HEADS * HEAD_DIM
    wt = {n: w[n] for n in ("ln_mix_g", "ln_mix_b", "ln_ffn_g", "ln_ffn_b")}
    w_in = _join_shards(full["a_w_in"][0], 1)
    d = w_in.shape[0]
    n_ff = full["ffn_conv"].shape[2] // 2
    blocks = [w_in[:, s * width:(s + 1) * width] for s in range(4)]
    wt["a3"] = jnp.stack(blocks[:3])[None]
    wt["az"] = blocks[3][None, None]
    wt["a_ba"] = jnp.pad(w_in[:, 4 * width:], ((0, 0), (0, HEAD_DIM - 2 * HEADS)))[None, None]
    wt["a_out"] = full["a_w_out"][0].reshape(1, 1, width, d)
    wt["a_conv3"] = full["a_conv"][0].reshape(GDN_CONV, 3, width).transpose(1, 0, 2)
    wt["b_conv"] = full["b_conv"][0]
    wt["fconv"] = [full["ffn_conv"][l].reshape(FFN_CONV, 2, n_ff).transpose(1, 0, 2) for l in range(2)]
    wt["meta"] = full["meta"]
    in_g_lanes = (HEADS, HEAD_DIM - 2 * HEADS)
    wt["alog_lanes"] = jnp.pad(w["a_log"][0], in_g_lanes)[None]
    wt["dtb_lanes"] = jnp.pad(w["a_dt_bias"][0], in_g_lanes)[None]
    wt["anorm_b"] = jnp.tile(w["a_norm"][0], HEADS)[None]
    return wt


def _layout_late(full):
    d = full["b_w_in"][0].shape[1]
    n_ff = full["ffn_w_up"][0].shape[2]
    return {
        "b_in": _join_shards(full["b_w_in"][0], 1).reshape(d, 3, d).transpose(1, 0, 2)[None],
        "b_out": full["b_w_out"][0].reshape(1, 1, d, d),
        "up": [t[None] for t in full["ffn_w_up"]],
        "down": [t.reshape(2, 1, n_ff, d) for t in full["ffn_w_down"]],
    }


def kernel(x, meta, a_w_in, a_conv, a_log, a_dt_bias, a_norm, a_w_out, b_w_in, b_conv, b_w_out, ln_mix_g, ln_mix_b, ffn_w_up, ffn_conv, ffn_w_down, ln_ffn_g, ln_ffn_b, loss_target, m_meta, m_a_w_in, m_a_conv, m_a_log, m_a_dt_bias, m_a_norm, m_a_w_out, m_b_w_in, m_b_conv, m_b_w_out, m_ln_mix_g, m_ln_mix_b, m_ffn_w_up, m_ffn_conv, m_ffn_w_down, m_ln_ffn_g, m_ln_ffn_b, v_meta, v_a_w_in, v_a_conv, v_a_log, v_a_dt_bias, v_a_norm, v_a_w_out, v_b_w_in, v_b_conv, v_b_w_out, v_ln_mix_g, v_ln_mix_b, v_ffn_w_up, v_ffn_conv, v_ffn_w_down, v_ln_ffn_g, v_ln_ffn_b):
    w = dict(meta=meta, a_w_in=a_w_in, a_conv=a_conv, a_log=a_log, a_dt_bias=a_dt_bias, a_norm=a_norm, a_w_out=a_w_out,
             b_w_in=b_w_in, b_conv=b_conv, b_w_out=b_w_out, ln_mix_g=ln_mix_g, ln_mix_b=ln_mix_b, ffn_w_up=ffn_w_up,
             ffn_conv=ffn_conv, ffn_w_down=ffn_w_down, ln_ffn_g=ln_ffn_g, ln_ffn_b=ln_ffn_b)
    m = dict(meta=m_meta, a_w_in=m_a_w_in, a_conv=m_a_conv, a_log=m_a_log, a_dt_bias=m_a_dt_bias, a_norm=m_a_norm,
             a_w_out=m_a_w_out, b_w_in=m_b_w_in, b_conv=m_b_conv, b_w_out=m_b_w_out, ln_mix_g=m_ln_mix_g,
             ln_mix_b=m_ln_mix_b, ffn_w_up=m_ffn_w_up, ffn_conv=m_ffn_conv, ffn_w_down=m_ffn_w_down,
             ln_ffn_g=m_ln_ffn_g, ln_ffn_b=m_ln_ffn_b)
    v = dict(meta=v_meta, a_w_in=v_a_w_in, a_conv=v_a_conv, a_log=v_a_log, a_dt_bias=v_a_dt_bias, a_norm=v_a_norm,
             a_w_out=v_a_w_out, b_w_in=v_b_w_in, b_conv=v_b_conv, b_w_out=v_b_w_out, ln_mix_g=v_ln_mix_g,
             ln_mix_b=v_ln_mix_b, ffn_w_up=v_ffn_w_up, ffn_conv=v_ffn_conv, ffn_w_down=v_ffn_w_down,
             ln_ffn_g=v_ln_ffn_g, ln_ffn_b=v_ln_ffn_b)
    seq = x.shape[1]
    *stacks, small = all_gather_shards(_weight_layers(w, EARLY_WEIGHTS) + [_pack([w[n] for n in SMALL_SHARDED])],
                                       "gather_early")
    full = _per_weight(stacks, w, EARLY_WEIGHTS)
    for n, t in zip(SMALL_SHARDED, _unpack(small, [w[n].shape for n in SMALL_SHARDED], lead=(4,))):
        full[n] = _join_shards(t, SHARD_AXIS[n])

    def layout_late(late_stacks):
        return _layout_late(_per_weight(late_stacks, w, LATE_WEIGHTS))

    def complete_grads(grads):
        return [g for n in LATE_WEIGHTS for g in grads[n]]

    def sum_pairs(bufs, from_sibling):
        return pair_sums(bufs, from_sibling, [BF16] * len(bufs), "late")

    loss_part, dh0, grads, landed_late = _local_step(x[0], loss_target[0], _layout_early(full, w),
                                                     _weight_layers(w, LATE_WEIGHTS), layout_late, complete_grads, sum_pairs)
    pieces = [_split_shards(grads[n], SHARD_AXIS[n]) for n in SMALL_SHARDED]
    same = jnp.concatenate([grads[n].reshape(-1) for n in REPLICATED] + [jnp.sum(loss_part).reshape(1)])
    pieces.append(jnp.broadcast_to(same, (4,) + same.shape))
    bufs = [g for n in EARLY_WEIGHTS for g in grads[n]] + [_pack(pieces, lead=(4,))]
    from_sibling = swap_halves(bufs, "rs_pair_early")
    landed = scatter_to_chips(pair_sums(bufs, from_sibling, [BF16] * (len(bufs) - 1) + [F32], "early"), "rs_chips_early")
    totals = [chip_sum(t, "rs_chip_sum%d" % i) for i, t in enumerate(landed + landed_late)]
    by_weight = _per_weight(totals[:len(bufs) - 1] + totals[len(bufs):], w, MATMUL_WEIGHTS)
    *shared, small_total = share_halves([by_weight[n] for n in MATMUL_WEIGHTS] + [[totals[len(bufs) - 1]]], "rs_share")
    grad_w = {n: t.reshape(w[n].shape) for n, t in zip(MATMUL_WEIGHTS, shared)}
    rest = SMALL_SHARDED + REPLICATED
    unpacked = _unpack(small_total[0], [w[n].shape for n in rest] + [()])
    grad_w.update(zip(rest, unpacked[:-1]))
    loss = unpacked[-1]
    grad_x = dh0[:, N_META:N_META + seq]
    steps = [adamw(w[n], grad_w[n], m[n], v[n], "adamw_" + n) for n in WEIGHTS]
    return (loss, grad_x, *[grad_w[n] for n in WEIGHTS], *[s[0] for s in steps], *[s[1] for s in steps],
            *[s[2] for s in steps])
```

```python
import functools

import jax
import jax.numpy as jnp
from jax import lax
from jax.experimental import pallas as pl
from jax.experimental.pallas import tpu as pltpu

F32 = jnp.float32
BF16 = jnp.bfloat16

N_META = 16
HEADS = 8
HEAD_DIM = 128
CHUNK = 64
GDN_CONV = 4
FFN_CONV = 3
ALPHA = 4.0 ** 0.25
LN_EPS = 1e-5
RMS_EPS = 1e-6
L2_EPS = 1e-6
Q_SCALE = HEAD_DIM ** -0.5

ADAM_LR = 0.001
ADAM_B1 = 0.9
ADAM_B2 = 0.999
ADAM_EPS = 1e-08
ADAM_WD = 0.01
ADAM_STEP = 10

HALO = 8
VMEM_LIMIT = 48 * 1024 * 1024


def _params(sem=None):
    return pltpu.CompilerParams(dimension_semantics=sem, vmem_limit_bytes=VMEM_LIMIT)


def _mxu(a, b, contract):
    a, b = (t if t.dtype == BF16 else t.astype(BF16) for t in (a, b))
    return lax.dot_general(a, b, ((contract[:1], contract[1:]), ((), ())), preferred_element_type=F32)


def _with_grads(contract, grad_a, grad_b):
    @jax.custom_vjp
    def dot(a, b):
        return _mxu(a, b, contract)

    def fwd(a, b):
        return _mxu(a, b, contract), (a, b)

    def bwd(operands, ct):
        a, b = operands
        return grad_a(ct, a, b).astype(a.dtype), grad_b(ct, a, b).astype(b.dtype)

    dot.defvjp(fwd, bwd)
    return dot


_dot = _with_grads((1, 0), lambda ct, a, b: _mxu(ct, b, (1, 1)), lambda ct, a, b: _mxu(a, ct, (0, 0)))
_dot_nt = _with_grads((1, 1), lambda ct, a, b: _mxu(ct, b, (1, 0)), lambda ct, a, b: _mxu(ct, a, (0, 0)))
_dot_tn = _with_grads((0, 0), lambda ct, a, b: _mxu(b, ct, (1, 1)), lambda ct, a, b: _mxu(a, ct, (1, 0)))


def _sigmoid(x):
    return 0.5 * jnp.tanh(0.5 * x) + 0.5


def _tri_masks():
    r = lax.broadcasted_iota(jnp.int32, (CHUNK, CHUNK), 0)
    c = lax.broadcasted_iota(jnp.int32, (CHUNK, CHUNK), 1)
    return r >= c, r > c, r == c


def _split_hi_lo(x):
    hi = x.astype(BF16)
    return hi, (x - hi.astype(F32)).astype(BF16)


def _mask_dot(mask, x):
    hi, lo = _split_hi_lo(x)
    return _dot(mask, hi) + _dot(mask, lo)


def _cumsum_rows(g):
    causal, _, _ = _tri_masks()
    return _mask_dot(causal.astype(BF16), g)


def _cumsum_rows_transposed(dy):
    _, strict, _ = _tri_masks()
    return _mask_dot((~strict).astype(BF16), dy)


def _dot_split3(a, b):
    a_hi, a_lo = _split_hi_lo(a)
    b_hi, b_lo = _split_hi_lo(b)
    return _dot(a_hi, b_hi) + (_dot(a_hi, b_lo) + _dot(a_lo, b_hi))


@jax.custom_vjp
def _dot_precise(a, b):
    return _dot_split3(a, b)


def _dot_precise_fwd(a, b):
    return _dot_split3(a, b), (a, b)


def _dot_precise_bwd(operands, ct):
    a, b = operands
    return _dot_split3(ct, b.T), _dot_split3(a.T, ct)


_dot_precise.defvjp(_dot_precise_fwd, _dot_precise_bwd)


def _gdn_m(ks, a64s, bbs):
    causal, strict, _ = _tri_masks()
    decay = [jnp.exp(jnp.where(causal, x - x.T, -1e30)) for x in a64s]
    kk = [_dot_nt(k * b, k) for k, b in zip(ks, bbs)]
    return [jnp.where(strict, x * d, 0.0) for x, d in zip(kk, decay)]


def _gdn_inverse_stages(ks, a64s, bbs):
    ms = _gdn_m(ks, a64s, bbs)
    yield
    r = lax.broadcasted_iota(jnp.int32, (CHUNK, CHUNK), 0)
    c = lax.broadcasted_iota(jnp.int32, (CHUNK, CHUNK), 1)
    eye = (r == c).astype(F32)
    same = [jnp.right_shift(r, s) == jnp.right_shift(c, s) for s in (3, 4, 5)]
    d = [jnp.where(same[0], m, 0.0) for m in ms]
    p = [_dot(x, x) for x in d]
    yield
    t = [eye - x for x in d]
    t = [x + _dot(x, y) for x, y in zip(t, p)]
    p = [_dot(x, x) for x in p]
    yield
    t = [x + _dot(x, y) for x, y in zip(t, p)]
    yield
    for inner, outer in ((same[0], same[1]), (same[1], same[2]), (same[2], None)):
        joins = ~inner if outer is None else (outer & ~inner)
        o = [_dot(x, jnp.where(joins, m, 0.0)) for x, m in zip(t, ms)]
        yield
        t = [x - _dot(y, x) for x, y in zip(t, o)]
        yield
    res = [eye - x - _dot_split3(m, x) for m, x in zip(ms, t)]
    yield
    return [x + _dot(x, y) for x, y in zip(t, res)]


def _gdn_apply_stages(qs, ks, vs, gc, a64s, gl, bbs, ss, ts):
    causal, _, _ = _tri_masks()
    n = range(len(qs))
    qk = [_dot_nt(qs[h], ks[h]) for h in n]
    yield
    decay = [jnp.exp(jnp.where(causal, x - x.T, -1e30)) for x in a64s]
    eg = [jnp.exp(x) for x in gc]
    u = [_dot_precise(ts[h], vs[h] * bbs[h]) for h in n]
    w = [_dot_precise(ts[h], ks[h] * bbs[h] * eg[h]) for h in n]
    qk = [qk[h] * decay[h] for h in n]
    kd = [ks[h] * jnp.exp(gl[h] - gc[h]) for h in n]
    yield
    v_new = [u[h] - _dot(w[h], ss[h]) for h in n]
    q_s = [_dot(qs[h] * eg[h], ss[h]) for h in n]
    yield
    o = [q_s[h] + _dot(qk[h], v_new[h]) for h in n]
    s2 = [ss[h] * jnp.exp(gl[h]) + _dot_tn(kd[h], v_new[h]) for h in n]
    return o, s2


def _run_stages(*generators):
    results = [None] * len(generators)
    live = dict(enumerate(generators))
    while live:
        for i, gen in list(live.items()):
            try:
                next(gen)
            except StopIteration as stop:
                results[i] = stop.value
                del live[i]
    return results


def _head_slices(h):
    return slice(h * HEAD_DIM, (h + 1) * HEAD_DIM), slice(h * HEAD_DIM, h * HEAD_DIM + CHUNK)


def _gdn_head_values(x_ref, gate_ref):
    heads = range(HEADS)
    qs, ks, vs = ([x_ref[s, :, _head_slices(h)[0]] for h in heads] for s in range(3))
    gate = gate_ref[...]
    cumulative = _cumsum_rows(gate)
    total = jnp.sum(gate, axis=0, keepdims=True)
    gcums = [cumulative[:, HEADS + h:HEADS + h + 1] for h in heads]
    gtots = [total[:, HEADS + h:HEADS + h + 1] for h in heads]
    bcols = [gate[:, h:h + 1] for h in heads]
    return qs, ks, vs, gcums, gtots, bcols


def _over_lanes(cols, lanes):
    return [jnp.broadcast_to(c, (c.shape[0], lanes)) for c in cols]


def _gdn_inverse_cols(ks, gcums, bcols):
    return _gdn_inverse_stages(ks, _over_lanes(gcums, CHUNK), _over_lanes(bcols, HEAD_DIM))


def _gdn_apply_cols_stages(qs, ks, vs, gcums, gtots, bcols, ss, ts):
    return _gdn_apply_stages(qs, ks, vs, _over_lanes(gcums, HEAD_DIM), _over_lanes(gcums, CHUNK),
                             _over_lanes(gtots, HEAD_DIM), _over_lanes(bcols, HEAD_DIM), ss, ts)


def _gdn_apply_cols(qs, ks, vs, gcums, gtots, bcols, ss, ts):
    return _run_stages(_gdn_apply_cols_stages(qs, ks, vs, gcums, gtots, bcols, ss, ts))[0]


def _gdn_m_cols(ks, gcums, bcols):
    return _gdn_m(ks, _over_lanes(gcums, CHUNK), _over_lanes(bcols, HEAD_DIM))


def _gate_lanes(bcols, gcols):
    rows = gcols[0].shape[0]
    lane = lax.broadcasted_iota(jnp.int32, (rows, HEAD_DIM), 1)
    out = jnp.zeros((rows, HEAD_DIM), F32)
    for h in range(HEADS):
        if bcols is not None:
            out = jnp.where(lane == h, jnp.broadcast_to(bcols[h], out.shape), out)
        out = jnp.where(lane == HEADS + h, jnp.broadcast_to(gcols[h], out.shape), out)
    return out


def _gate_gradient(dbcols, dgcums, dgtots):
    block = _gate_lanes(dbcols, dgcums)
    lane = lax.broadcasted_iota(jnp.int32, block.shape, 1)
    return jnp.where(lane < HEADS, block, _cumsum_rows_transposed(block) + _gate_lanes(None, dgtots))


def gdn_chunk_fwd(qkv, gates, gather=()):
    _, lp, width = qkv.shape
    n_chunks = lp // CHUNK
    n = len(gather)

    def body(x_ref, gate_ref, next_ref, next_gate_ref, *refs):
        shard_refs, (o_ref, s_ref, t_ref), refs = refs[:n], refs[n:n + 3], refs[n + 3:]
        stack_refs, state, t_next, sems = refs[:n], refs[n], refs[n + 1], refs[n + 2:]
        copies = _gather_copies(shard_refs, stack_refs, *sems) if n else None

        def inverse_stages(ref, g_ref):
            _, ks, _, gcums, _, bcols = _gdn_head_values(ref, g_ref)
            return _gdn_inverse_cols(ks, gcums, bcols)

        @pl.when(pl.program_id(0) == 0)
        def _():
            state[...] = jnp.zeros_like(state)
            for h, t in enumerate(_run_stages(inverse_stages(x_ref, gate_ref))[0]):
                t_next[h] = t
            if n:
                _gather_start(copies)

        qs, ks, vs, gcums, gtots, bcols = _gdn_head_values(x_ref, gate_ref)
        ss = [state[h] for h in range(HEADS)]
        ts = [t_next[h] for h in range(HEADS)]
        ts_next, (os_, s2) = _run_stages(inverse_stages(next_ref, next_gate_ref),
                                         _gdn_apply_cols_stages(qs, ks, vs, gcums, gtots, bcols, ss, ts))
        for h in range(HEADS):
            s_ref[0, h] = ss[h]
            t_ref[0, h] = ts[h]
            t_next[h] = ts_next[h]
            o_ref[:, _head_slices(h)[0]] = os_[h]
            state[h] = s2[h]

        if n:
            @pl.when(pl.program_id(0) == n_chunks - 1)
            def _():
                _gather_finish(copies)

    o, states, tinv, *stacks = pl.pallas_call(
        body,
        name="gdn_chunk_fwd",
        grid=(n_chunks,),
        in_specs=[pl.BlockSpec((3, CHUNK, width), lambda c: (0, c, 0)),
                  pl.BlockSpec((CHUNK, HEAD_DIM), lambda c: (c, 0)),
                  pl.BlockSpec((3, CHUNK, width), lambda c: (0, jnp.minimum(c + 1, n_chunks - 1), 0)),
                  pl.BlockSpec((CHUNK, HEAD_DIM), lambda c: (jnp.minimum(c + 1, n_chunks - 1), 0))] + [ANY] * n,
        out_specs=[
            pl.BlockSpec((CHUNK, width), lambda c: (c, 0)),
            pl.BlockSpec((1, HEADS, HEAD_DIM, HEAD_DIM), lambda c: (c, 0, 0, 0)),
            pl.BlockSpec((1, HEADS, CHUNK, CHUNK), lambda c: (c, 0, 0, 0)),
        ] + [ANY] * n,
        out_shape=[
            jax.ShapeDtypeStruct((lp, width), F32),
            jax.ShapeDtypeStruct((n_chunks, HEADS, HEAD_DIM, HEAD_DIM), F32),
            jax.ShapeDtypeStruct((n_chunks, HEADS, CHUNK, CHUNK), F32),
        ] + _gather_out_shapes(gather),
        scratch_shapes=[pltpu.VMEM((HEADS, HEAD_DIM, HEAD_DIM), F32), pltpu.VMEM((HEADS, CHUNK, CHUNK), F32)]
        + (_gather_sems(n) if n else []),
        compiler_params=_params(("arbitrary",)),
    )(qkv, gates, qkv, gates, *gather)
    return o, states, tinv, _set_own_slots(stacks, gather)


def gdn_chunk_bwd(qkv, gates, states, tinv, d_o, scatter=()):
    _, lp, width = qkv.shape
    n_chunks = lp // CHUNK
    last = n_chunks - 1
    n = len(scatter)

    def body(x_ref, gate_ref, s_ref, t_ref, do_ref, *refs):
        leaving_refs, dx_ref, dgate_ref, refs = refs[:n], refs[n], refs[n + 1], refs[n + 2:]
        landing_refs, dstate, sems = refs[:n], refs[n], refs[n + 1:]
        copies = _scatter_copies(leaving_refs, landing_refs, *sems) if n else None

        @pl.when(pl.program_id(0) == 0)
        def _():
            dstate[...] = jnp.zeros_like(dstate)
            if n:
                _scatter_start(copies)

        heads = range(HEADS)
        qs, ks, vs, gcums, gtots, bcols = _gdn_head_values(x_ref, gate_ref)
        ss = [s_ref[0, h] for h in heads]
        ts = [t_ref[0, h] for h in heads]
        d_out = ([do_ref[:, _head_slices(h)[0]] for h in heads], [dstate[h] for h in heads])
        _, vjp_apply = jax.vjp(_gdn_apply_cols, qs, ks, vs, gcums, gtots, bcols, ss, ts)
        dq, dk, dv, dgc, dgt, db, ds, dt = vjp_apply(d_out)
        tts = [t.T for t in ts]
        dm = [_dot(tts[h], dt[h]) for h in heads]
        dm = [-_dot(dm[h], tts[h]) for h in heads]
        _, vjp_m = jax.vjp(_gdn_m_cols, ks, gcums, bcols)
        dk2, dgc2, db2 = vjp_m(dm)
        for h in heads:
            sl = _head_slices(h)[0]
            dx_ref[0, :, sl] = dq[h]
            dx_ref[1, :, sl] = dk[h] + dk2[h]
            dx_ref[2, :, sl] = dv[h]
            dstate[h] = ds[h]
        dgate_ref[...] = _gate_gradient([db[h] + db2[h] for h in heads], [dgc[h] + dgc2[h] for h in heads], dgt)

        if n:
            @pl.when(pl.program_id(0) == n_chunks - 1)
            def _():
                _scatter_finish(copies)

    dqkv, dgates, *landed = pl.pallas_call(
        body,
        name="gdn_chunk_bwd",
        grid=(n_chunks,),
        in_specs=[
            pl.BlockSpec((3, CHUNK, width), lambda c: (0, last - c, 0)),
            pl.BlockSpec((CHUNK, HEAD_DIM), lambda c: (last - c, 0)),
            pl.BlockSpec((1, HEADS, HEAD_DIM, HEAD_DIM), lambda c: (last - c, 0, 0, 0)),
            pl.BlockSpec((1, HEADS, CHUNK, CHUNK), lambda c: (last - c, 0, 0, 0)),
            pl.BlockSpec((CHUNK, width), lambda c: (last - c, 0)),
        ] + [ANY] * n,
        out_specs=[pl.BlockSpec((3, CHUNK, width), lambda c: (0, last - c, 0)),
                   pl.BlockSpec((CHUNK, HEAD_DIM), lambda c: (last - c, 0))] + [ANY] * n,
        out_shape=[jax.ShapeDtypeStruct(qkv.shape, F32), jax.ShapeDtypeStruct(gates.shape, F32)]
        + [jax.ShapeDtypeStruct(b.shape, b.dtype) for b in scatter],
        scratch_shapes=[pltpu.VMEM((HEADS, HEAD_DIM, HEAD_DIM), F32)] + (_scatter_sems(n) if n else []),
        compiler_params=_params(("arbitrary",)),
    )(qkv, gates, states, tinv, d_o, *scatter)
    return dqkv, dgates, _keep_own_slots(landed, scatter)


def mm_nn(a, b, *, tm, name):
    ks, m, tk = a.shape
    _, ns, _, tn = b.shape

    def body(a_ref, b_ref, o_ref):
        p = _dot(a_ref[...].astype(BF16), b_ref[...])

        @pl.when(pl.program_id(2) == 0)
        def _():
            o_ref[...] = p

        @pl.when(pl.program_id(2) > 0)
        def _():
            o_ref[...] += p

    return pl.pallas_call(
        body,
        name=name,
        grid=(ns, m // tm, ks),
        in_specs=[
            pl.BlockSpec((None, tm, tk), lambda n, i, k: (k, i, 0)),
            pl.BlockSpec((None, None, tk, tn), lambda n, i, k: (k, n, 0, 0)),
        ],
        out_specs=pl.BlockSpec((None, tm, tn), lambda n, i, k: (n, i, 0)),
        out_shape=jax.ShapeDtypeStruct((ns, m, tn), F32),
        compiler_params=_params(("arbitrary", "arbitrary", "arbitrary")),
    )(a, b)


def mm_nt(dy, w, *, tm, name, res=None, res_scale=1.0):
    ns, m, tn = dy.shape
    ks, _, tk, _ = w.shape

    def body(*refs):
        if res is None:
            dy_ref, w_ref, o_ref = refs
        else:
            dy_ref, w_ref, r_ref, o_ref = refs
        p = _dot_nt(dy_ref[...].astype(BF16), w_ref[...])

        @pl.when(pl.program_id(2) == 0)
        def _():
            o_ref[...] = p if res is None else p + res_scale * r_ref[...]

        @pl.when(pl.program_id(2) > 0)
        def _():
            o_ref[...] += p

    in_specs = [
        pl.BlockSpec((None, tm, tn), lambda k, i, n: (n, i, 0)),
        pl.BlockSpec((None, None, tk, tn), lambda k, i, n: (k, n, 0, 0)),
    ]
    args = [dy, w]
    if res is not None:
        in_specs.append(pl.BlockSpec((None, tm, tk), lambda k, i, n: (k, i, 0)))
        args.append(res)
    return pl.pallas_call(
        body,
        name=name,
        grid=(ks, m // tm, ns),
        in_specs=in_specs,
        out_specs=pl.BlockSpec((None, tm, tk), lambda k, i, n: (k, i, 0)),
        out_shape=jax.ShapeDtypeStruct((ks, m, tk), F32),
        compiler_params=_params(("arbitrary", "arbitrary", "arbitrary")),
    )(*args)


def mm_tn(x, dy, *, tm, name, rb=None):
    ks, m, tk = x.shape
    ns, _, tn = dy.shape
    rb = tk if rb is None else rb

    def body(x_ref, dy_ref, o_ref):
        @pl.when(pl.program_id(2) == 0)
        def _():
            o_ref[...] = jnp.zeros_like(o_ref)

        dyb = dy_ref[...].astype(BF16)
        for r in range(0, tk, rb):
            o_ref[r:r + rb, :] += _dot_tn(x_ref[:, r:r + rb].astype(BF16), dyb)

    return pl.pallas_call(
        body,
        name=name,
        grid=(ks, ns, m // tm),
        in_specs=[
            pl.BlockSpec((None, tm, tk), lambda k, n, i: (k, i, 0)),
            pl.BlockSpec((None, tm, tn), lambda k, n, i: (n, i, 0)),
        ],
        out_specs=pl.BlockSpec((None, None, tk, tn), lambda k, n, i: (k, n, 0, 0)),
        out_shape=jax.ShapeDtypeStruct((ks, ns, tk, tn), F32),
        compiler_params=_params(("arbitrary", "arbitrary", "arbitrary")),
    )(x, dy)


def _row_partial(x):
    rows, c = x.shape
    return jnp.sum(x.reshape(rows // 8, 8, c), axis=0)


def _layer_norm(r, g, b):
    mu = jnp.mean(r, axis=-1, keepdims=True)
    xc = r - mu
    var = jnp.mean(xc * xc, axis=-1, keepdims=True)
    return xc * lax.rsqrt(var + LN_EPS) * g + b


def _layer_norm_bwd(x, dh, g):
    mu = jnp.mean(x, axis=-1, keepdims=True)
    xc = x - mu
    rstd = lax.rsqrt(jnp.mean(xc * xc, axis=-1, keepdims=True) + LN_EPS)
    xh = xc * rstd
    dxh = dh * g
    m1 = jnp.mean(dxh, axis=-1, keepdims=True)
    m2 = jnp.mean(dxh * xh, axis=-1, keepdims=True)
    return rstd * (dxh - m1 - xh * m2), _row_partial(dh * xh), _row_partial(dh)


def mm_nn_ln(a, b, h_prev, g, beta, *, tm, name):
    ks, m, tk = a.shape
    d = b.shape[3]

    def body(a_ref, b_ref, hp_ref, g_ref, be_ref, r_ref, h_ref, hb_ref):
        p = _dot(a_ref[...].astype(BF16), b_ref[...])

        @pl.when(pl.program_id(1) == 0)
        def _():
            r_ref[...] = p

        @pl.when(pl.program_id(1) > 0)
        def _():
            r_ref[...] += p

        @pl.when(pl.program_id(1) == ks - 1)
        def _():
            r = ALPHA * hp_ref[...] + r_ref[...]
            r_ref[...] = r
            h = _layer_norm(r, g_ref[...], be_ref[...])
            h_ref[...] = h
            hb_ref[...] = h.astype(BF16)

    row = pl.BlockSpec((None, tm, d), lambda i, k: (0, i, 0))
    vec = pl.BlockSpec((1, d), lambda i, k: (0, 0))
    return pl.pallas_call(
        body,
        name=name,
        grid=(m // tm, ks),
        in_specs=[
            pl.BlockSpec((None, tm, tk), lambda i, k: (k, i, 0)),
            pl.BlockSpec((None, None, tk, d), lambda i, k: (k, 0, 0, 0)),
            row, vec, vec,
        ],
        out_specs=[row, row, row],
        out_shape=[jax.ShapeDtypeStruct((1, m, d), F32)] * 2 + [jax.ShapeDtypeStruct((1, m, d), BF16)],
        compiler_params=_params(("arbitrary", "arbitrary")),
    )(a, b, h_prev, g, beta)


def mm_nt_ln_bwd(dy, w, res, r, g, *, tm, name, swap=()):
    ns, m, tn = dy.shape
    d = w.shape[2]
    n_swap = len(swap)
    last_tile = m // tm - 1

    def body(dy_ref, w_ref, res_ref, r_ref, g_ref, *refs):
        leaving_refs, (dr_ref, dgb_ref), refs = refs[:n_swap], refs[n_swap:n_swap + 2], refs[n_swap + 2:]
        copies = _swap_copies(leaving_refs, refs[:n_swap], *refs[n_swap:]) if n_swap else None
        p = _dot_nt(dy_ref[...].astype(BF16), w_ref[...])

        @pl.when((pl.program_id(0) == 0) & (pl.program_id(1) == 0))
        def _():
            dgb_ref[...] = jnp.zeros_like(dgb_ref)
            if n_swap:
                _swap_start(copies)

        @pl.when(pl.program_id(1) == 0)
        def _():
            dr_ref[...] = p + ALPHA * res_ref[...]

        @pl.when(pl.program_id(1) > 0)
        def _():
            dr_ref[...] += p

        @pl.when(pl.program_id(1) == ns - 1)
        def _():
            for rows in (pl.ds(0, tm // 2), pl.ds(tm // 2, tm // 2)):
                dr, dgamma, dbeta = _layer_norm_bwd(r_ref[rows, :], dr_ref[rows, :], g_ref[...])
                dr_ref[rows, :] = dr
                dgb_ref[0] += dgamma
                dgb_ref[1] += dbeta

        if n_swap:
            @pl.when((pl.program_id(0) == last_tile) & (pl.program_id(1) == ns - 1))
            def _():
                _swap_finish(copies)

    row = pl.BlockSpec((None, tm, d), lambda i, n: (0, i, 0))
    dr, dgb, *landed = pl.pallas_call(
        body,
        name=name,
        grid=(m // tm, ns),
        in_specs=[
            pl.BlockSpec((None, tm, tn), lambda i, n: (n, i, 0)),
            pl.BlockSpec((None, None, d, tn), lambda i, n: (0, n, 0, 0)),
            row, row,
            pl.BlockSpec((1, d), lambda i, n: (0, 0)),
        ] + [ANY] * n_swap,
        out_specs=[row, pl.BlockSpec((2, 8, d), lambda i, n: (0, 0, 0))] + [ANY] * n_swap,
        out_shape=[jax.ShapeDtypeStruct((1, m, d), F32), jax.ShapeDtypeStruct((2, 8, d), F32)] + _swap_out_shapes(swap),
        scratch_shapes=_swap_sems(n_swap) if n_swap else [],
        compiler_params=_params(("arbitrary", "arbitrary")),
    )(dy, w, res, r, g, *swap)
    return dr, dgb, landed


def loss_ln_bwd(h, target, r, g, *, first, count, tm):
    _, lp, d = h.shape

    def body(h_ref, t_ref, r_ref, g_ref, dr_ref, dgb_ref, l_ref):
        row = pl.program_id(0) * tm + lax.broadcasted_iota(jnp.int32, (tm, d), 0)
        valid = (row >= first) & (row < first + count)
        err = jnp.where(valid, h_ref[...] - t_ref[...], 0.0)
        dr, dgamma, dbeta = _layer_norm_bwd(r_ref[...], err * (1.0 / d), g_ref[...])
        dr_ref[...] = dr

        @pl.when(pl.program_id(0) == 0)
        def _():
            dgb_ref[...] = jnp.zeros_like(dgb_ref)
            l_ref[...] = jnp.zeros_like(l_ref)

        dgb_ref[0] += dgamma
        dgb_ref[1] += dbeta
        l_ref[...] += _row_partial(err * err) * (0.5 / d)

    row3 = pl.BlockSpec((None, tm, d), lambda i: (0, i, 0))
    return pl.pallas_call(
        body,
        name="loss_ln4_bwd",
        grid=(lp // tm,),
        in_specs=[row3, pl.BlockSpec((tm, d), lambda i: (i, 0)), row3, pl.BlockSpec((1, d), lambda i: (0, 0))],
        out_specs=[row3, pl.BlockSpec((2, 8, d), lambda i: (0, 0, 0)), pl.BlockSpec((8, d), lambda i: (0, 0))],
        out_shape=[jax.ShapeDtypeStruct((1, lp, d), F32), jax.ShapeDtypeStruct((2, 8, d), F32),
                   jax.ShapeDtypeStruct((8, d), F32)],
        compiler_params=_params(("arbitrary",)),
    )(h, target, r, g)


def _halo_index(tile, tm):
    return jnp.maximum(tile * (tm // HALO) - 1, 0)


def _conv_fwd(xs_ref, w, taps, tm):
    acc = w(0) * xs_ref[pl.ds(HALO - taps + 1, tm), :]
    for j in range(1, taps):
        acc += w(j) * xs_ref[pl.ds(HALO - taps + 1 + j, tm), :]
    return acc


def _conv_bwd_x(dcs_ref, w, taps, tm):
    acc = w(0) * dcs_ref[pl.ds(taps - 1, tm), :]
    for j in range(1, taps):
        acc += w(j) * dcs_ref[pl.ds(taps - 1 - j, tm), :]
    return acc


SUB = 8
LANES = 128
PAIR = 2 * SUB
STRIP_UNROLL = 2


def _pair_rows(r0):
    return pl.ds(r0, SUB), pl.ds(r0 + SUB if isinstance(r0, int) else pl.multiple_of(r0 + SUB, SUB), SUB)


def _shift_down(cur, prev, s):
    if s == 0:
        return cur
    row = lax.broadcasted_iota(jnp.int32, cur.shape, 0)
    return jnp.where(row < s, pltpu.roll(prev, s, axis=0), pltpu.roll(cur, s, axis=0))


def _shift_up(cur, nxt, s):
    if s == 0:
        return cur
    row = lax.broadcasted_iota(jnp.int32, cur.shape, 0)
    return jnp.where(row < SUB - s, pltpu.roll(cur, SUB - s, axis=0), pltpu.roll(nxt, SUB - s, axis=0))


def _silu_parts(c):
    sg = _sigmoid(c)
    return c * sg, sg * (1.0 + c * (1.0 - sg))


def _head_sum(x):
    rows, c = x.shape
    parts = []
    for h in range(c // HEAD_DIM):
        s = jnp.sum(x[:, h * HEAD_DIM:(h + 1) * HEAD_DIM], axis=-1, keepdims=True)
        parts.append(jnp.broadcast_to(s, (rows, HEAD_DIM)))
    return parts[0] if len(parts) == 1 else jnp.concatenate(parts, axis=-1)


def _log1p(y):
    u = 1.0 + y
    d = u - 1.0
    return jnp.where(d == 0.0, y, jnp.log(u) * (y / jnp.where(d == 0.0, 1.0, d)))


def _softplus(x):
    return jnp.maximum(x, 0.0) + _log1p(jnp.exp(-jnp.abs(x)))


def _gate_values(x, al, dt):
    lane = lax.broadcasted_iota(jnp.int32, x.shape, 1)
    is_beta, is_g = lane < HEADS, (lane >= HEADS) & (lane < 2 * HEADS)
    return _sigmoid(x), -jnp.exp(al) * _softplus(x + dt), is_beta, is_g


def gdn_gates_fwd(pba, al, dt, *, tm):
    _, lp, width = pba.shape

    def body(x_ref, al_ref, dt_ref, o_ref):
        beta, g, is_beta, is_g = _gate_values(x_ref[...], al_ref[...], dt_ref[...])
        o_ref[...] = jnp.where(is_beta, beta, jnp.where(is_g, g, 0.0))

    vec = pl.BlockSpec((1, width), lambda i: (0, 0))
    return pl.pallas_call(
        body,
        name="gdn_gates_fwd",
        grid=(lp // tm,),
        in_specs=[pl.BlockSpec((None, tm, width), lambda i: (0, i, 0)), vec, vec],
        out_specs=pl.BlockSpec((tm, width), lambda i: (i, 0)),
        out_shape=jax.ShapeDtypeStruct((lp, width), F32),
        compiler_params=_params(("arbitrary",)),
    )(pba, al, dt)


def gdn_gates_bwd(pba, dgates, al, dt, *, tm):
    _, lp, width = pba.shape

    def body(x_ref, d_ref, al_ref, dt_ref, dx_ref, dsc_ref):
        x = x_ref[...]
        beta, g, is_beta, is_g = _gate_values(x, al_ref[...], dt_ref[...])
        d = d_ref[...]
        dg = jnp.where(is_g, d, 0.0)
        da = dg * -jnp.exp(al_ref[...]) * _sigmoid(x + dt_ref[...])
        dx_ref[...] = jnp.where(is_beta, d * beta * (1.0 - beta), da).astype(dx_ref.dtype)

        @pl.when(pl.program_id(0) == 0)
        def _():
            dsc_ref[...] = jnp.zeros_like(dsc_ref)

        dsc_ref[0] += _row_partial(dg * g)
        dsc_ref[1] += _row_partial(da)

    vec = pl.BlockSpec((1, width), lambda i: (0, 0))
    return pl.pallas_call(
        body,
        name="gdn_gates_bwd",
        grid=(lp // tm,),
        in_specs=[pl.BlockSpec((None, tm, width), lambda i: (0, i, 0)), pl.BlockSpec((tm, width), lambda i: (i, 0)), vec, vec],
        out_specs=[pl.BlockSpec((None, tm, width), lambda i: (0, i, 0)), pl.BlockSpec((2, SUB, width), lambda i: (0, 0, 0))],
        out_shape=[jax.ShapeDtypeStruct((1, lp, width), BF16), jax.ShapeDtypeStruct((2, SUB, width), F32)],
        compiler_params=_params(("arbitrary",)),
    )(pba, dgates, al, dt)


def gdn_pre_fwd(p3, conv_w, *, tm, cb):
    _, lp, width = p3.shape
    taps = conv_w.shape[1]

    def body(x_ref, halo_ref, w_ref, o_ref, xs):
        i = pl.program_id(1)
        for s in range(3):
            xs[s, 0:HALO, :] = jnp.where(i > 0, halo_ref[s], 0.0)
            xs[s, HALO:, :] = x_ref[s]
            c = _conv_fwd(xs.at[s], lambda j, s=s: w_ref[s, j:j + 1, :], taps, tm)
            y, _ = _silu_parts(c)
            if s < 2:
                y = y * lax.rsqrt(_head_sum(y * y) + L2_EPS)
                if s == 0:
                    y = y * Q_SCALE
            o_ref[s] = y

    return pl.pallas_call(
        body,
        name="gdn_pre_fwd",
        grid=(width // cb, lp // tm),
        in_specs=[
            pl.BlockSpec((3, tm, cb), lambda j, i: (0, i, j)),
            pl.BlockSpec((3, HALO, cb), lambda j, i: (0, _halo_index(i, tm), j)),
            pl.BlockSpec((3, taps, cb), lambda j, i: (0, 0, j)),
        ],
        out_specs=pl.BlockSpec((3, tm, cb), lambda j, i: (0, i, j)),
        out_shape=jax.ShapeDtypeStruct((3, lp, width), F32),
        scratch_shapes=[pltpu.VMEM((3, tm + HALO, cb), F32)],
        compiler_params=_params(("arbitrary", "arbitrary")),
    )(p3, p3, conv_w)


def gdn_pre_bwd(p3, dqkv, conv_w, *, tm, cb):
    _, lp, width = p3.shape
    taps = conv_w.shape[1]
    last = lp // tm - 1

    def body(x_ref, halo_ref, d_ref, w_ref, dx_ref, dw_ref, xs, dcs, carry):
        step = pl.program_id(1)
        tile = last - step

        @pl.when(step == 0)
        def _():
            carry[...] = jnp.zeros_like(carry)
            dw_ref[...] = jnp.zeros_like(dw_ref)

        for s in range(3):
            w = lambda j, s=s: w_ref[s, j:j + 1, :]
            xs[s, 0:HALO, :] = jnp.where(tile > 0, halo_ref[s], 0.0)
            xs[s, HALO:, :] = x_ref[s]
            c = _conv_fwd(xs.at[s], w, taps, tm)
            y, dsilu = _silu_parts(c)
            dy = d_ref[s]
            if s < 2:
                rn = lax.rsqrt(_head_sum(y * y) + L2_EPS)
                yn = y * rn
                if s == 0:
                    dy = dy * Q_SCALE
                dy = rn * (dy - yn * _head_sum(dy * yn))
            dc = dy * dsilu
            dcs[s, 0:tm, :] = dc
            dcs[s, tm:, :] = carry[s]
            dx_ref[s] = _conv_bwd_x(dcs.at[s], w, taps, tm).astype(dx_ref.dtype)
            carry[s] = dc[0:HALO, :]
            for j in range(taps):
                dw_ref[s, j] += _row_partial(dc * xs[s, pl.ds(HALO - taps + 1 + j, tm), :])

    tile_spec = pl.BlockSpec((3, tm, cb), lambda j, i: (0, last - i, j))
    return pl.pallas_call(
        body,
        name="gdn_pre_bwd",
        grid=(width // cb, lp // tm),
        in_specs=[
            tile_spec,
            pl.BlockSpec((3, HALO, cb), lambda j, i: (0, _halo_index(last - i, tm), j)),
            tile_spec,
            pl.BlockSpec((3, taps, cb), lambda j, i: (0, 0, j)),
        ],
        out_specs=[tile_spec, pl.BlockSpec((3, taps, SUB, cb), lambda j, i: (0, 0, 0, j))],
        out_shape=[jax.ShapeDtypeStruct((3, lp, width), BF16), jax.ShapeDtypeStruct((3, taps, SUB, width), F32)],
        scratch_shapes=[
            pltpu.VMEM((3, tm + HALO, cb), F32),
            pltpu.VMEM((3, tm + HALO, cb), F32),
            pltpu.VMEM((3, HALO, cb), F32),
        ],
        compiler_params=_params(("arbitrary", "arbitrary")),
    )(p3, p3, dqkv, conv_w)


def gdn_post_fwd(o, z, nw_b, *, tm):
    _, lp, width = o.shape

    def body(o_ref, z_ref, nw_ref, y_ref):
        ov = o_ref[...]
        rn = lax.rsqrt(_head_sum(ov * ov) * (1.0 / HEAD_DIM) + RMS_EPS)
        gate, _ = _silu_parts(z_ref[...])
        y_ref[...] = (ov * rn * nw_ref[...] * gate).astype(y_ref.dtype)

    row = pl.BlockSpec((None, tm, width), lambda i: (0, i, 0))
    return pl.pallas_call(
        body,
        name="gdn_post_fwd",
        grid=(lp // tm,),
        in_specs=[row, row, pl.BlockSpec((1, width), lambda i: (0, 0))],
        out_specs=row,
        out_shape=jax.ShapeDtypeStruct((1, lp, width), BF16),
        compiler_params=_params(("arbitrary",)),
    )(o, z, nw_b)


def gdn_post_bwd(o, z, dy, nw_b, *, tm):
    _, lp, width = o.shape

    def body(o_ref, z_ref, dy_ref, nw_ref, do_ref, dz_ref, dnw_ref):
        ov = o_ref[...]
        rn = lax.rsqrt(_head_sum(ov * ov) * (1.0 / HEAD_DIM) + RMS_EPS)
        yn = ov * rn
        gate, dgate = _silu_parts(z_ref[...])
        d_on = dy_ref[...] * gate
        dz_ref[...] = (dy_ref[...] * yn * nw_ref[...] * dgate).astype(dz_ref.dtype)
        a = d_on * nw_ref[...]
        do_ref[...] = rn * (a - yn * (_head_sum(a * yn) * (1.0 / HEAD_DIM)))

        @pl.when(pl.program_id(0) == 0)
        def _():
            dnw_ref[...] = jnp.zeros_like(dnw_ref)

        dnw_ref[...] += _row_partial(d_on * yn)

    row = pl.BlockSpec((None, tm, width), lambda i: (0, i, 0))
    return pl.pallas_call(
        body,
        name="gdn_post_bwd",
        grid=(lp // tm,),
        in_specs=[row, row, row, pl.BlockSpec((1, width), lambda i: (0, 0))],
        out_specs=[row, row, pl.BlockSpec((8, width), lambda i: (0, 0))],
        out_shape=[jax.ShapeDtypeStruct((1, lp, width), F32), jax.ShapeDtypeStruct((1, lp, width), BF16),
                   jax.ShapeDtypeStruct((8, width), F32)],
        compiler_params=_params(("arbitrary",)),
    )(o, z, dy, nw_b)


def ffn_act_fwd(up, conv_w, *, tm, name):
    _, lp, c_w = up.shape
    taps = conv_w.shape[1]

    def body(u_ref, halo_ref, g_ref, w_ref, o_ref):
        first_tile = pl.program_id(1) == 0

        def strip(cur, prev, rows, cs):
            conv = w_ref[taps - 1:taps, cs] * cur
            for j in range(taps - 1):
                conv += w_ref[j:j + 1, cs] * _shift_down(cur, prev, taps - 1 - j)
            y, _ = _silu_parts(conv)
            return y * g_ref[rows, cs]

        def pair(r0, above_of):
            top, bot = _pair_rows(r0)
            for c0 in range(0, c_w, LANES):
                cs = slice(c0, c0 + LANES)
                cur_t, cur_b = u_ref[top, cs], u_ref[bot, cs]
                out = [strip(cur_t, above_of(cs), top, cs), strip(cur_b, cur_t, bot, cs)]
                o_ref[pl.ds(r0, PAIR), cs] = jnp.concatenate(out, axis=0).astype(o_ref.dtype)

        pair(0, lambda cs: jnp.where(first_tile, 0.0, halo_ref[:, cs]))

        def loop_body(s, carry):
            r0 = pl.multiple_of(s * PAIR, PAIR)
            pair(r0, lambda cs: u_ref[pl.ds(pl.multiple_of(r0 - SUB, SUB), SUB), cs])
            return carry

        lax.fori_loop(1, tm // PAIR, loop_body, 0, unroll=STRIP_UNROLL)

    return pl.pallas_call(
        body,
        name=name,
        grid=(2, lp // tm),
        in_specs=[
            pl.BlockSpec((None, tm, c_w), lambda s, i: (s, i, 0)),
            pl.BlockSpec((None, HALO, c_w), lambda s, i: (s, _halo_index(i, tm), 0)),
            pl.BlockSpec((None, tm, c_w), lambda s, i: (2 + s, i, 0)),
            pl.BlockSpec((None, taps, c_w), lambda s, i: (s, 0, 0)),
        ],
        out_specs=pl.BlockSpec((None, tm, c_w), lambda s, i: (s, i, 0)),
        out_shape=jax.ShapeDtypeStruct((2, lp, c_w), BF16),
        compiler_params=_params(("arbitrary", "arbitrary")),
    )(up, up, up, conv_w)


def ffn_act_bwd(up, dact, conv_w, *, tm, name):
    _, lp, c_w = up.shape
    taps = conv_w.shape[1]
    last = lp // tm - 1
    n_pairs = tm // PAIR

    def body(u_ref, halo_ref, g_ref, d_ref, w_ref, dup_ref, dw_ref, below):
        step = pl.program_id(1)
        first_tile = step == last

        @pl.when(step == 0)
        def _():
            below[...] = jnp.zeros_like(below)
            dw_ref[...] = jnp.zeros_like(dw_ref)

        def strip(cur, prev, rows, cs, nxt):
            shifted = [_shift_down(cur, prev, taps - 1 - j) for j in range(taps)]
            conv = w_ref[0:1, cs] * shifted[0]
            for j in range(1, taps):
                conv += w_ref[j:j + 1, cs] * shifted[j]
            y, dsilu = _silu_parts(conv)
            d = d_ref[rows, cs]
            dc = d * g_ref[rows, cs] * dsilu
            dx = w_ref[taps - 1:taps, cs] * dc
            for j in range(taps - 1):
                dx += w_ref[j:j + 1, cs] * _shift_up(dc, nxt, taps - 1 - j)
            return dx, d * y, dc, [dc * s for s in shifted]

        def pair(r0, above_of):
            top, bot = _pair_rows(r0)
            both = pl.ds(r0, PAIR)
            for c0 in range(0, c_w, LANES):
                cs = slice(c0, c0 + LANES)
                cur_t, cur_b = u_ref[top, cs], u_ref[bot, cs]
                dx_b, dg_b, dc_b, dw_b = strip(cur_b, cur_t, bot, cs, below[:, cs])
                dx_t, dg_t, dc_t, dw_t = strip(cur_t, above_of(cs), top, cs, dc_b)
                below[:, cs] = dc_t
                dup_ref[0, both, cs] = jnp.concatenate([dx_t, dx_b], axis=0).astype(dup_ref.dtype)
                dup_ref[1, both, cs] = jnp.concatenate([dg_t, dg_b], axis=0).astype(dup_ref.dtype)
                for j in range(taps):
                    dw_ref[j, :, cs] += dw_t[j] + dw_b[j]

        def loop_body(it, carry):
            r0 = pl.multiple_of((n_pairs - 1 - it) * PAIR, PAIR)
            pair(r0, lambda cs: u_ref[pl.ds(pl.multiple_of(r0 - SUB, SUB), SUB), cs])
            return carry

        lax.fori_loop(0, n_pairs - 1, loop_body, 0, unroll=STRIP_UNROLL)
        pair(0, lambda cs: jnp.where(first_tile, 0.0, halo_ref[:, cs]))

    return pl.pallas_call(
        body,
        name=name,
        grid=(2, lp // tm),
        in_specs=[
            pl.BlockSpec((None, tm, c_w), lambda s, i: (s, last - i, 0)),
            pl.BlockSpec((None, HALO, c_w), lambda s, i: (s, _halo_index(last - i, tm), 0)),
            pl.BlockSpec((None, tm, c_w), lambda s, i: (2 + s, last - i, 0)),
            pl.BlockSpec((None, tm, c_w), lambda s, i: (s, last - i, 0)),
            pl.BlockSpec((None, taps, c_w), lambda s, i: (s, 0, 0)),
        ],
        out_specs=[
            pl.BlockSpec((2, None, tm, c_w), lambda s, i: (0, s, last - i, 0)),
            pl.BlockSpec((None, taps, SUB, c_w), lambda s, i: (s, 0, 0, 0)),
        ],
        out_shape=[jax.ShapeDtypeStruct((2, 2, lp, c_w), BF16), jax.ShapeDtypeStruct((2, taps, SUB, c_w), F32)],
        scratch_shapes=[pltpu.VMEM((SUB, c_w), F32)],
        compiler_params=_params(("arbitrary", "arbitrary")),
    )(up, up, up, dact, conv_w)


def sc_fwd(pb, conv_w, *, tm, cb):
    _, lp, width = pb.shape
    taps = conv_w.shape[0]

    def body(x_ref, halo_ref, w_ref, o_ref):
        first_tile = pl.program_id(1) == 0

        def strip(cur, prev, rows, cs):
            conv = w_ref[taps - 1:taps, cs] * cur
            for j in range(taps - 1):
                conv += w_ref[j:j + 1, cs] * _shift_down(cur, prev, taps - 1 - j)
            return x_ref[0, rows, cs] * conv

        def pair(r0, above_of):
            top, bot = _pair_rows(r0)
            for c0 in range(0, cb, LANES):
                cs = slice(c0, c0 + LANES)
                cur_t = x_ref[1, top, cs] * x_ref[2, top, cs]
                cur_b = x_ref[1, bot, cs] * x_ref[2, bot, cs]
                out = [strip(cur_t, above_of(cs), top, cs), strip(cur_b, cur_t, bot, cs)]
                o_ref[pl.ds(r0, PAIR), cs] = jnp.concatenate(out, axis=0).astype(o_ref.dtype)

        pair(0, lambda cs: jnp.where(first_tile, 0.0, halo_ref[1, :, cs] * halo_ref[2, :, cs]))

        def loop_body(k, carry):
            r0 = pl.multiple_of(k * PAIR, PAIR)
            before = pl.ds(pl.multiple_of(r0 - SUB, SUB), SUB)
            pair(r0, lambda cs: x_ref[1, before, cs] * x_ref[2, before, cs])
            return carry

        lax.fori_loop(1, tm // PAIR, loop_body, 0, unroll=STRIP_UNROLL)

    return pl.pallas_call(
        body,
        name="sc_fwd",
        grid=(width // cb, lp // tm),
        in_specs=[
            pl.BlockSpec((3, tm, cb), lambda j, i: (0, i, j)),
            pl.BlockSpec((3, HALO, cb), lambda j, i: (0, _halo_index(i, tm), j)),
            pl.BlockSpec((taps, cb), lambda j, i: (0, j)),
        ],
        out_specs=pl.BlockSpec((None, tm, cb), lambda j, i: (0, i, j)),
        out_shape=jax.ShapeDtypeStruct((1, lp, width), BF16),
        compiler_params=_params(("arbitrary", "arbitrary")),
    )(pb, pb, conv_w)


def sc_bwd(pb, ds, conv_w, *, tm, cb):
    _, lp, width = pb.shape
    taps = conv_w.shape[0]
    last = lp // tm - 1
    n_pairs = tm // PAIR

    def body(x_ref, halo_ref, d_ref, w_ref, dx_ref, dw_ref, below):
        step = pl.program_id(1)
        first_tile = step == last

        @pl.when(step == 0)
        def _():
            below[...] = jnp.zeros_like(below)
            dw_ref[...] = jnp.zeros_like(dw_ref)

        def strip(cur, prev, rows, cs, nxt):
            gate, left, right = x_ref[0, rows, cs], x_ref[1, rows, cs], x_ref[2, rows, cs]
            shifted = [_shift_down(cur, prev, taps - 1 - j) for j in range(taps)]
            conv = w_ref[0:1, cs] * shifted[0]
            for j in range(1, taps):
                conv += w_ref[j:j + 1, cs] * shifted[j]
            d = d_ref[rows, cs]
            dc = d * gate
            dp = w_ref[taps - 1:taps, cs] * dc
            for j in range(taps - 1):
                dp += w_ref[j:j + 1, cs] * _shift_up(dc, nxt, taps - 1 - j)
            return d * conv, dp * right, dp * left, dc, [dc * s for s in shifted]

        def pair(r0, above_of):
            top, bot = _pair_rows(r0)
            both = pl.ds(r0, PAIR)
            for c0 in range(0, cb, LANES):
                cs = slice(c0, c0 + LANES)
                cur_t = x_ref[1, top, cs] * x_ref[2, top, cs]
                cur_b = x_ref[1, bot, cs] * x_ref[2, bot, cs]
                *dx_b, dc_b, dw_b = strip(cur_b, cur_t, bot, cs, below[:, cs])
                *dx_t, dc_t, dw_t = strip(cur_t, above_of(cs), top, cs, dc_b)
                below[:, cs] = dc_t
                for s in range(3):
                    dx_ref[s, both, cs] = jnp.concatenate([dx_t[s], dx_b[s]], axis=0).astype(dx_ref.dtype)
                for j in range(taps):
                    dw_ref[j, :, cs] += dw_t[j] + dw_b[j]

        def loop_body(it, carry):
            r0 = pl.multiple_of((n_pairs - 1 - it) * PAIR, PAIR)
            before = pl.ds(pl.multiple_of(r0 - SUB, SUB), SUB)
            pair(r0, lambda cs: x_ref[1, before, cs] * x_ref[2, before, cs])
            return carry

        lax.fori_loop(0, n_pairs - 1, loop_body, 0, unroll=STRIP_UNROLL)
        pair(0, lambda cs: jnp.where(first_tile, 0.0, halo_ref[1, :, cs] * halo_ref[2, :, cs]))

    tile_spec = pl.BlockSpec((3, tm, cb), lambda j, i: (0, last - i, j))
    return pl.pallas_call(
        body,
        name="sc_bwd",
        grid=(width // cb, lp // tm),
        in_specs=[
            tile_spec,
            pl.BlockSpec((3, HALO, cb), lambda j, i: (0, _halo_index(last - i, tm), j)),
            pl.BlockSpec((None, tm, cb), lambda j, i: (0, last - i, j)),
            pl.BlockSpec((taps, cb), lambda j, i: (0, j)),
        ],
        out_specs=[tile_spec, pl.BlockSpec((taps, SUB, cb), lambda j, i: (0, 0, j))],
        out_shape=[jax.ShapeDtypeStruct((3, lp, width), BF16), jax.ShapeDtypeStruct((taps, SUB, width), F32)],
        scratch_shapes=[pltpu.VMEM((SUB, cb), F32)],
        compiler_params=_params(("arbitrary", "arbitrary")),
    )(pb, pb, ds, conv_w)


TILE_BYTES = 1536 * 1024


def _rows_tile(rows, cols, multiple=8):
    if rows * cols * 4 <= TILE_BYTES or rows % multiple:
        return rows
    best = multiple
    for t in range(multiple, rows + 1, multiple):
        if rows % t == 0 and t * cols * 4 <= TILE_BYTES:
            best = t
    return best


def pair_sum(g, landed, core, out_dtype, name):
    _, rows, cols = g.shape
    half = rows // 2
    tr = _rows_tile(half, cols, 16)
    nb = half // tr

    def body(c_ref, g_ref, l_ref, o_ref):
        o_ref[...] = (g_ref[...] + l_ref[...]).astype(out_dtype)

    return pl.pallas_call(
        body,
        name=name,
        grid_spec=pltpu.PrefetchScalarGridSpec(
            num_scalar_prefetch=1,
            grid=(4, nb),
            in_specs=[
                pl.BlockSpec((None, tr, cols), lambda s, i, c: (s, c[0] * nb + i, 0)),
                pl.BlockSpec((None, tr, cols), lambda s, i, c: (s, i, 0)),
            ],
            out_specs=pl.BlockSpec((None, tr, cols), lambda s, i, c: (s, i, 0)),
        ),
        out_shape=jax.ShapeDtypeStruct((4, half, cols), out_dtype),
        compiler_params=_params(("arbitrary", "arbitrary")),
    )(core, g, landed)


def chip_sum(x, name):
    _, rows, cols = x.shape
    tr = _rows_tile(rows, cols, 16)

    def body(x0, x1, x2, x3, o_ref):
        acc = x0[...].astype(F32) + x1[...].astype(F32)
        o_ref[...] = (acc + x2[...].astype(F32)) + x3[...].astype(F32)

    return pl.pallas_call(
        body,
        name=name,
        grid=(rows // tr,),
        in_specs=[pl.BlockSpec((None, tr, cols), lambda i, k=k: (k, i, 0)) for k in range(4)],
        out_specs=pl.BlockSpec((tr, cols), lambda i: (i, 0)),
        out_shape=jax.ShapeDtypeStruct((rows, cols), F32),
        compiler_params=_params(("arbitrary",)),
    )(x, x, x, x)


def adamw(w, g, m, v, name):
    shape = w.shape
    cols = shape[-1]
    rows = w.size // cols
    tr = _rows_tile(rows, cols)

    def body(w_ref, g_ref, m_ref, v_ref, d_ref, m2_ref, v2_ref):
        gv = g_ref[...]
        m2 = ADAM_B1 * m_ref[...] + (1.0 - ADAM_B1) * gv
        v2 = ADAM_B2 * v_ref[...] + (1.0 - ADAM_B2) * (gv * gv)
        m_hat = m2 / (1.0 - ADAM_B1 ** ADAM_STEP)
        v_hat = v2 / (1.0 - ADAM_B2 ** ADAM_STEP)
        d_ref[...] = -ADAM_LR * (m_hat / (jnp.sqrt(v_hat) + ADAM_EPS) + ADAM_WD * w_ref[...])
        m2_ref[...] = m2
        v2_ref[...] = v2

    spec = pl.BlockSpec((tr, cols), lambda i: (i, 0))
    outs = pl.pallas_call(
        body,
        name=name,
        grid=(rows // tr,),
        in_specs=[spec] * 4,
        out_specs=[spec] * 3,
        out_shape=[jax.ShapeDtypeStruct((rows, cols), F32)] * 3,
        compiler_params=_params(("arbitrary",)),
    )(*[t.reshape(rows, cols) for t in (w, g, m, v)])
    return tuple(o.reshape(shape) for o in outs)


MESH_ID = pl.DeviceIdType.MESH
ANY = pl.BlockSpec(memory_space=pl.ANY)


def _place():
    x, y, c = lax.axis_index("x"), lax.axis_index("y"), lax.axis_index("c")
    other_chips = [(1 - x, y), (x, 1 - y), (1 - x, 1 - y)]
    return x, y, c, other_chips


def all_gather_shards(bufs, name):
    n = len(bufs)

    def body(*refs):
        x_refs, o_refs = refs[:n], refs[n:2 * n]
        copies = _gather_copies(x_refs, o_refs, *refs[2 * n:])
        _gather_start(copies)
        _gather_finish(copies)

    outs = pl.pallas_call(
        body,
        name=name,
        in_specs=[ANY] * n,
        out_specs=[ANY] * n,
        out_shape=_gather_out_shapes(bufs),
        scratch_shapes=_gather_sems(n),
    )(*bufs)
    return _set_own_slots(outs, bufs)


def _gather_out_shapes(bufs):
    return [jax.ShapeDtypeStruct((4,) + b.shape, b.dtype) for b in bufs]


def _gather_sems(n):
    return [pltpu.SemaphoreType.DMA((6 * n,)), pltpu.SemaphoreType.DMA((6 * n,))]


def _set_own_slots(outs, bufs):
    if not outs:
        return []
    me = 2 * lax.axis_index("x") + lax.axis_index("y")
    return [lax.dynamic_update_index_in_dim(o, b, me, 0) for o, b in zip(outs, bufs)]


def _gather_copies(x_refs, o_refs, send_sems, recv_sems):
    x, y, c, chips = _place()
    me = 2 * x + y
    sibling = (x, y, 1 - c)

    def part(a, slot, hf):
        half = x_refs[a].shape[0] // 2
        return o_refs[a].at[slot, pl.ds(hf * half, half), :]

    def mine(a):
        half = x_refs[a].shape[0] // 2
        return x_refs[a].at[pl.ds(c * half, half), :]

    def copy(k, src, dst, to):
        return pltpu.make_async_remote_copy(src_ref=src, dst_ref=dst, send_sem=send_sems.at[k],
                                            recv_sem=recv_sems.at[k], device_id=to, device_id_type=MESH_ID)

    sends, arrivals, passes, passed = [], [], [], []
    for a in range(len(x_refs)):
        for j, (px, py) in enumerate(chips):
            landed, theirs = part(a, 2 * px + py, c), part(a, 2 * px + py, 1 - c)
            sends.append(copy(6 * a + j, mine(a), part(a, me, c), (px, py, c)))
            arrivals.append(copy(6 * a + j, mine(a), landed, (px, py, c)))
            passes.append(copy(6 * a + 3 + j, landed, landed, sibling))
            passed.append(copy(6 * a + 3 + j, theirs, theirs, sibling))
    return sends, arrivals, passes, passed


def _gather_start(copies):
    for cp in copies[0]:
        cp.start()


def _gather_finish(copies):
    sends, arrivals, passes, passed = copies
    for arrival, cp in zip(arrivals, passes):
        arrival.wait_recv()
        cp.start()
    for cp in passed:
        cp.wait_recv()
    for cp in sends + passes:
        cp.wait_send()


def swap_halves(bufs, name):
    n = len(bufs)

    def body(*refs):
        copies = _swap_copies(refs[:n], refs[n:2 * n], *refs[2 * n:])
        _swap_start(copies)
        _swap_finish(copies)

    return pl.pallas_call(
        body,
        name=name,
        in_specs=[ANY] * n,
        out_specs=[ANY] * n,
        out_shape=_swap_out_shapes(bufs),
        scratch_shapes=_swap_sems(n),
    )(*bufs)


def _swap_out_shapes(bufs):
    return [jax.ShapeDtypeStruct((4, b.shape[1] // 2, b.shape[2]), b.dtype) for b in bufs]


def _swap_sems(n):
    return [pltpu.SemaphoreType.DMA((n,)), pltpu.SemaphoreType.DMA((n,))]


def _swap_copies(x_refs, o_refs, send_sems, recv_sems):
    x, y, c, _ = _place()
    copies = []
    for a, (x_ref, o_ref) in enumerate(zip(x_refs, o_refs)):
        half = x_ref.shape[1] // 2
        copies.append(pltpu.make_async_remote_copy(src_ref=x_ref.at[:, pl.ds((1 - c) * half, half), :], dst_ref=o_ref,
                                                   send_sem=send_sems.at[a], recv_sem=recv_sems.at[a],
                                                   device_id=(x, y, 1 - c), device_id_type=MESH_ID))
    return copies


def _swap_start(copies):
    for cp in copies:
        cp.start()


def _swap_finish(copies):
    for cp in copies:
        cp.wait()


def scatter_to_chips(bufs, name):
    n = len(bufs)

    def body(*refs):
        x_refs, o_refs = refs[:n], refs[n:2 * n]
        copies = _scatter_copies(x_refs, o_refs, *refs[2 * n:])
        _scatter_start(copies)
        _scatter_finish(copies)

    outs = pl.pallas_call(
        body,
        name=name,
        in_specs=[ANY] * n,
        out_specs=[ANY] * n,
        out_shape=[jax.ShapeDtypeStruct(b.shape, b.dtype) for b in bufs],
        scratch_shapes=_scatter_sems(n),
    )(*bufs)
    return _keep_own_slots(outs, bufs)


def _scatter_sems(n):
    return [pltpu.SemaphoreType.DMA((3 * n,)), pltpu.SemaphoreType.DMA((3 * n,))]


def _keep_own_slots(outs, bufs):
    if not outs:
        return []
    me = 2 * lax.axis_index("x") + lax.axis_index("y")
    return [lax.dynamic_update_index_in_dim(o, lax.dynamic_index_in_dim(b, me, 0, keepdims=False), me, 0)
            for o, b in zip(outs, bufs)]


def _scatter_copies(x_refs, o_refs, send_sems, recv_sems):
    x, y, c, chips = _place()
    me = 2 * x + y

    def copy(a, j, src_slot, dst_slot, px, py):
        return pltpu.make_async_remote_copy(src_ref=x_refs[a].at[src_slot], dst_ref=o_refs[a].at[dst_slot],
                                            send_sem=send_sems.at[3 * a + j], recv_sem=recv_sems.at[3 * a + j],
                                            device_id=(px, py, c), device_id_type=MESH_ID)

    sends = [copy(a, j, 2 * px + py, me, px, py) for a in range(len(x_refs)) for j, (px, py) in enumerate(chips)]
    arrivals = [copy(a, j, me, 2 * px + py, px, py) for a in range(len(x_refs)) for j, (px, py) in enumerate(chips)]
    return sends, arrivals


def _scatter_start(copies):
    for cp in copies[0]:
        cp.start()


def _scatter_finish(copies):
    for cp in copies[1]:
        cp.wait_recv()
    for cp in copies[0]:
        cp.wait_send()


def share_halves(groups, name):
    bufs = [b for grp in groups for b in grp]
    where = [(gi, li) for gi, grp in enumerate(groups) for li in range(len(grp))]
    n = len(bufs)

    def body(*refs):
        x_refs, o_refs = refs[:n], refs[n:n + len(groups)]
        send_sems, recv_sems = refs[n + len(groups):]
        x, y, c, _ = _place()
        sent, arrive = [], []
        for a, (gi, li) in enumerate(where):

            def copy(hf, a=a, gi=gi, li=li):
                return pltpu.make_async_remote_copy(src_ref=x_refs[a], dst_ref=o_refs[gi].at[li, hf],
                                                    send_sem=send_sems.at[a], recv_sem=recv_sems.at[a],
                                                    device_id=(x, y, 1 - c), device_id_type=MESH_ID)

            sent.append(copy(c))
            arrive.append(copy(1 - c))
        for cp in sent:
            cp.start()
        for cp in arrive:
            cp.wait_recv()
        for cp in sent:
            cp.wait_send()

    outs = pl.pallas_call(
        body,
        name=name,
        in_specs=[ANY] * n,
        out_specs=[ANY] * len(groups),
        out_shape=[jax.ShapeDtypeStruct((len(grp), 2) + grp[0].shape, grp[0].dtype) for grp in groups],
        scratch_shapes=[pltpu.SemaphoreType.DMA((n,)), pltpu.SemaphoreType.DMA((n,))],
    )(*bufs)
    c = lax.axis_index("c")
    full = [lax.dynamic_update_index_in_dim(o, jnp.stack(grp), c, 1) for o, grp in zip(outs, groups)]
    return [t.reshape(t.shape[0], 2 * t.shape[2], t.shape[3]) for t in full]


def pair_sums(bufs, landed, dtypes, tag):
    core = lax.axis_index("c").astype(jnp.int32).reshape(1)
    return [pair_sum(b, l, core, dt, "rs_pair_sum_%s%d" % (tag, i)) for i, (b, l, dt) in enumerate(zip(bufs, landed, dtypes))]


def _row_tiles(length):
    return (640, 640) if length > 2048 else (128, 64)


def _divisor_tile(rows, target):
    return max(t for t in range(8, min(rows, target) + 1, 8) if rows % t == 0)


def _local_step(x, target, wt, late_shards, layout_late, complete_grads, sum_pairs):
    seq, d = x.shape
    length = N_META + seq
    tm, tm_ffn = _row_tiles(length)
    lp = -(-length // tm) * tm
    tail = jnp.zeros((lp - length, d), F32)
    h0 = jnp.concatenate([wt["meta"], x, tail], axis=0)[None]
    tgt = jnp.concatenate([jnp.zeros((N_META, d), F32), target, tail], axis=0)
    nn = functools.partial(mm_nn, tm=_divisor_tile(lp, 1664))
    nt = functools.partial(mm_nt, tm=_divisor_tile(lp, 1664))
    tn = functools.partial(mm_tn, tm=_divisor_tile(lp, 1664), rb=256)
    nn_ln = functools.partial(mm_nn_ln, tm=_divisor_tile(lp, 832))
    nt_ln_bwd = functools.partial(mm_nt_ln_bwd, tm=_divisor_tile(lp, 1040))
    ln_g = [wt["ln_mix_g"][0:1], wt["ln_ffn_g"][0:1], wt["ln_mix_g"][1:2], wt["ln_ffn_g"][1:2]]
    ln_b = [wt["ln_mix_b"][0:1], wt["ln_ffn_b"][0:1], wt["ln_mix_b"][1:2], wt["ln_ffn_b"][1:2]]

    h0b = h0.astype(BF16)
    p3 = nn(h0b, wt["a3"], name="a_in3")
    pz = nn(h0b, wt["az"], name="a_inz")
    pba = nn(h0b, wt["a_ba"], name="a_inba")
    qkv = gdn_pre_fwd(p3, wt["a_conv3"], tm=tm, cb=2 * HEAD_DIM)
    gates = gdn_gates_fwd(pba, wt["alog_lanes"], wt["dtb_lanes"], tm=tm)
    o, states, tinv, late_stacks = gdn_chunk_fwd(qkv, gates, late_shards)
    wt = {**wt, **layout_late(late_stacks)}
    onz = gdn_post_fwd(o[None], pz, wt["anorm_b"], tm=tm)
    r1, h1, h1b = nn_ln(onz, wt["a_out"], h0, ln_g[0], ln_b[0], name="a_out_ln1")
    up0 = nn(h1b, wt["up"][0], name="up0")
    act0 = ffn_act_fwd(up0, wt["fconv"][0], tm=tm_ffn, name="ffn_act0")
    r2, h2, h2b = nn_ln(act0, wt["down"][0], h1, ln_g[1], ln_b[1], name="down0_ln2")
    pb = nn(h2b, wt["b_in"], name="b_in")
    sc = sc_fwd(pb, wt["b_conv"], tm=tm_ffn, cb=d)
    r3, h3, h3b = nn_ln(sc, wt["b_out"], h2, ln_g[2], ln_b[2], name="b_out_ln3")
    up1 = nn(h3b, wt["up"][1], name="up1")
    act1 = ffn_act_fwd(up1, wt["fconv"][1], tm=tm_ffn, name="ffn_act1")
    r4, h4, _ = nn_ln(act1, wt["down"][1], h3, ln_g[3], ln_b[3], name="down1_ln4")

    grads = {}
    dr4, dgb4, loss_part = loss_ln_bwd(h4, tgt, r4, ln_g[3], first=N_META, count=seq, tm=tm)
    d_down1 = tn(act1, dr4, name="d_down1")
    dact1 = nt(dr4, wt["down"][1], name="d_act1")
    dup1, dfconv1 = ffn_act_bwd(up1, dact1, wt["fconv"][1], tm=tm_ffn, name="ffn_act1_bwd")
    dup1 = dup1.reshape(up1.shape)
    d_up1 = tn(h3b, dup1, name="d_up1")

    dr3, dgb3, _ = nt_ln_bwd(dup1, wt["up"][1], dr4, r3, ln_g[2], name="d_h3_ln3")
    d_bout = tn(sc, dr3, name="d_b_out")
    dsc = nt(dr3, wt["b_out"], name="d_sc")
    dpb, dbconv = sc_bwd(pb, dsc, wt["b_conv"], tm=tm_ffn, cb=d)
    d_bin = tn(h2b, dpb, name="d_b_in")

    dr2, dgb2, _ = nt_ln_bwd(dpb, wt["b_in"], dr3, r2, ln_g[1], name="d_h2_ln2")
    d_down0 = tn(act0, dr2, name="d_down0")
    dact0 = nt(dr2, wt["down"][0], name="d_act0")
    dup0, dfconv0 = ffn_act_bwd(up0, dact0, wt["fconv"][0], tm=tm_ffn, name="ffn_act0_bwd")
    dup0 = dup0.reshape(up0.shape)
    d_up0 = tn(h1b, dup0, name="d_up0")
    grads["b_w_in"] = [d_bin[0].transpose(1, 0, 2).reshape(d, 4, 3 * d // 4).transpose(1, 0, 2)]
    grads["b_w_out"] = [d_bout.reshape(4, d // 4, d)]
    grads["ffn_w_up"] = [d_up0[0], d_up1[0]]
    grads["ffn_w_down"] = [t.reshape(4, -1, d) for t in (d_down0, d_down1)]
    complete = complete_grads(grads)

    dr1, dgb1, from_sibling = nt_ln_bwd(dup0, wt["up"][0], dr2, r1, ln_g[0], name="d_h1_ln1", swap=complete)
    leaving = sum_pairs(complete, from_sibling)
    d_aout = tn(onz, dr1, name="d_a_out")
    donz = nt(dr1, wt["a_out"], name="d_onz")
    d_o, dz, dnw = gdn_post_bwd(o[None], pz, donz, wt["anorm_b"], tm=tm)
    dqkv, dgates, landed = gdn_chunk_bwd(qkv, gates, states, tinv, d_o[0], leaving)
    dp3, daconv = gdn_pre_bwd(p3, dqkv, wt["a_conv3"], tm=tm, cb=2 * HEAD_DIM)
    dpba, dscal = gdn_gates_bwd(pba, dgates, wt["alog_lanes"], wt["dtb_lanes"], tm=tm)
    d_a3 = tn(h0b, dp3, name="d_a_in3")
    d_az = tn(h0b, dz, name="d_a_inz")
    d_aba = tn(h0b, dpba, name="d_a_inba")
    dh0 = nt(dp3, wt["a3"], res=dr1, res_scale=ALPHA, name="d_h0a")
    dh0 = nt(dz, wt["az"], res=dh0, res_scale=1.0, name="d_h0z")
    dh0 = nt(dpba, wt["a_ba"], res=dh0, res_scale=1.0, name="d_h0")

    width = HEADS * HEAD_DIM
    d_a_in = jnp.concatenate([d_a3[0, 0], d_a3[0, 1], d_a3[0, 2], d_az[0, 0], d_aba[0, 0][:, :2 * HEADS]], axis=1)
    n_in = d_a_in.shape[1] // 4
    grads["a_w_in"] = [d_a_in.reshape(d, 4, n_in).transpose(1, 0, 2)]
    grads["a_w_out"] = [d_aout.reshape(4, width // 4, d)]
    grads["a_conv"] = daconv.sum(axis=2).transpose(1, 0, 2).reshape(1, GDN_CONV, 3 * width)
    per_head = dscal.sum(axis=1)[:, HEADS:2 * HEADS]
    grads["a_log"] = per_head[0][None]
    grads["a_dt_bias"] = per_head[1][None]
    grads["a_norm"] = dnw.reshape(8, HEADS, HEAD_DIM).sum(axis=(0, 1))[None]
    grads["b_conv"] = dbconv.sum(axis=1)[None]
    lns = [dgb1, dgb2, dgb3, dgb4]
    grads["ln_mix_g"] = jnp.stack([lns[0][0].sum(0), lns[2][0].sum(0)])
    grads["ln_mix_b"] = jnp.stack([lns[0][1].sum(0), lns[2][1].sum(0)])
    grads["ln_ffn_g"] = jnp.stack([lns[1][0].sum(0), lns[3][0].sum(0)])
    grads["ln_ffn_b"] = jnp.stack([lns[1][1].sum(0), lns[3][1].sum(0)])
    grads["ffn_conv"] = jnp.stack([t.sum(axis=2).transpose(1, 0, 2).reshape(FFN_CONV, -1) for t in (dfconv0, dfconv1)])
    grads["meta"] = dh0[0, :N_META]
    return loss_part, dh0, grads, landed


WEIGHTS = ["meta", "a_w_in", "a_conv", "a_log", "a_dt_bias", "a_norm", "a_w_out", "b_w_in", "b_conv", "b_w_out",
           "ln_mix_g", "ln_mix_b", "ffn_w_up", "ffn_conv", "ffn_w_down", "ln_ffn_g", "ln_ffn_b"]
EARLY_WEIGHTS = ["a_w_in", "a_w_out"]
LATE_WEIGHTS = ["b_w_in", "b_w_out", "ffn_w_up", "ffn_w_down"]
MATMUL_WEIGHTS = EARLY_WEIGHTS + LATE_WEIGHTS
SMALL_SHARDED = ["a_conv", "b_conv", "ffn_conv", "meta"]
REPLICATED = ["a_log", "a_dt_bias", "a_norm", "ln_mix_g", "ln_mix_b", "ln_ffn_g", "ln_ffn_b"]
SHARD_AXIS = {"meta": 1, "a_w_in": 2, "a_conv": 2, "a_w_out": 1, "b_w_in": 2, "b_conv": 2, "b_w_out": 1,
              "ffn_w_up": 2, "ffn_conv": 2, "ffn_w_down": 1}
PACK_COLS = 1024
PACK_ROWS_MULTIPLE = 32


def _pack(pieces, lead=()):
    flat = jnp.concatenate([p.reshape(lead + (-1,)) for p in pieces], axis=-1)
    n = flat.shape[-1]
    rows = -(-n // (PACK_COLS * PACK_ROWS_MULTIPLE)) * PACK_ROWS_MULTIPLE
    flat = jnp.pad(flat, [(0, 0)] * len(lead) + [(0, rows * PACK_COLS - n)])
    return flat.reshape(lead + (rows, PACK_COLS))


def _unpack(buf, shapes, lead=()):
    flat = buf.reshape(lead + (-1,))
    out, off = [], 0
    for shp in shapes:
        n = 1
        for s in shp:
            n *= s
        out.append(flat[..., off:off + n].reshape(lead + tuple(shp)))
        off += n
    return out


def _join_shards(stacked, axis):
    return jnp.concatenate([stacked[k] for k in range(4)], axis=axis)


def _split_shards(full, axis):
    return jnp.stack(jnp.split(full, 4, axis=axis))


def _weight_layers(w, names):
    return [w[n][l].astype(BF16) for n in names for l in range(w[n].shape[0])]


def _per_weight(arrays, w, names):
    it = iter(arrays)
    return {n: [next(it) for _ in range(w[n].shape[0])] for n in names}


def _layout_early(full, w):
    width = HEADS * HEAD_DIM
    wt = {n: w[n] for n in ("ln_mix_g", "ln_mix_b", "ln_ffn_g", "ln_ffn_b")}
    w_in = _join_shards(full["a_w_in"][0], 1)
    d = w_in.shape[0]
    n_ff = full["ffn_conv"].shape[2] // 2
    blocks = [w_in[:, s * width:(s + 1) * width] for s in range(4)]
    wt["a3"] = jnp.stack(blocks[:3])[None]
    wt["az"] = blocks[3][None, None]
    wt["a_ba"] = jnp.pad(w_in[:, 4 * width:], ((0, 0), (0, HEAD_DIM - 2 * HEADS)))[None, None]
    wt["a_out"] = full["a_w_out"][0].reshape(1, 1, width, d)
    wt["a_conv3"] = full["a_conv"][0].reshape(GDN_CONV, 3, width).transpose(1, 0, 2)
    wt["b_conv"] = full["b_conv"][0]
    wt["fconv"] = [full["ffn_conv"][l].reshape(FFN_CONV, 2, n_ff).transpose(1, 0, 2) for l in range(2)]
    wt["meta"] = full["meta"]
    in_g_lanes = (HEADS, HEAD_DIM - 2 * HEADS)
    wt["alog_lanes"] = jnp.pad(w["a_log"][0], in_g_lanes)[None]
    wt["dtb_lanes"] = jnp.pad(w["a_dt_bias"][0], in_g_lanes)[None]
    wt["anorm_b"] = jnp.tile(w["a_norm"][0], HEADS)[None]
    return wt


def _layout_late(full):
    d = full["b_w_in"][0].shape[1]
    n_ff = full["ffn_w_up"][0].shape[2]
    return {
        "b_in": _join_shards(full["b_w_in"][0], 1).reshape(d, 3, d).transpose(1, 0, 2)[None],
        "b_out": full["b_w_out"][0].reshape(1, 1, d, d),
        "up": [t[None] for t in full["ffn_w_up"]],
        "down": [t.reshape(2, 1, n_ff, d) for t in full["ffn_w_down"]],
    }


def kernel(x, meta, a_w_in, a_conv, a_log, a_dt_bias, a_norm, a_w_out, b_w_in, b_conv, b_w_out, ln_mix_g, ln_mix_b, ffn_w_up, ffn_conv, ffn_w_down, ln_ffn_g, ln_ffn_b, loss_target, m_meta, m_a_w_in, m_a_conv, m_a_log, m_a_dt_bias, m_a_norm, m_a_w_out, m_b_w_in, m_b_conv, m_b_w_out, m_ln_mix_g, m_ln_mix_b, m_ffn_w_up, m_ffn_conv, m_ffn_w_down, m_ln_ffn_g, m_ln_ffn_b, v_meta, v_a_w_in, v_a_conv, v_a_log, v_a_dt_bias, v_a_norm, v_a_w_out, v_b_w_in, v_b_conv, v_b_w_out, v_ln_mix_g, v_ln_mix_b, v_ffn_w_up, v_ffn_conv, v_ffn_w_down, v_ln_ffn_g, v_ln_ffn_b):
    w = dict(meta=meta, a_w_in=a_w_in, a_conv=a_conv, a_log=a_log, a_dt_bias=a_dt_bias, a_norm=a_norm, a_w_out=a_w_out,
             b_w_in=b_w_in, b_conv=b_conv, b_w_out=b_w_out, ln_mix_g=ln_mix_g, ln_mix_b=ln_mix_b, ffn_w_up=ffn_w_up,
             ffn_conv=ffn_conv, ffn_w_down=ffn_w_down, ln_ffn_g=ln_ffn_g, ln_ffn_b=ln_ffn_b)
    m = dict(meta=m_meta, a_w_in=m_a_w_in, a_conv=m_a_conv, a_log=m_a_log, a_dt_bias=m_a_dt_bias, a_norm=m_a_norm,
             a_w_out=m_a_w_out, b_w_in=m_b_w_in, b_conv=m_b_conv, b_w_out=m_b_w_out, ln_mix_g=m_ln_mix_g,
             ln_mix_b=m_ln_mix_b, ffn_w_up=m_ffn_w_up, ffn_conv=m_ffn_conv, ffn_w_down=m_ffn_w_down,
             ln_ffn_g=m_ln_ffn_g, ln_ffn_b=m_ln_ffn_b)
    v = dict(meta=v_meta, a_w_in=v_a_w_in, a_conv=v_a_conv, a_log=v_a_log, a_dt_bias=v_a_dt_bias, a_norm=v_a_norm,
             a_w_out=v_a_w_out, b_w_in=v_b_w_in, b_conv=v_b_conv, b_w_out=v_b_w_out, ln_mix_g=v_ln_mix_g,
             ln_mix_b=v_ln_mix_b, ffn_w_up=v_ffn_w_up, ffn_conv=v_ffn_conv, ffn_w_down=v_ffn_w_down,
             ln_ffn_g=v_ln_ffn_g, ln_ffn_b=v_ln_ffn_b)
    seq = x.shape[1]
    *stacks, small = all_gather_shards(_weight_layers(w, EARLY_WEIGHTS) + [_pack([w[n] for n in SMALL_SHARDED])],
                                       "gather_early")
    full = _per_weight(stacks, w, EARLY_WEIGHTS)
    for n, t in zip(SMALL_SHARDED, _unpack(small, [w[n].shape for n in SMALL_SHARDED], lead=(4,))):
        full[n] = _join_shards(t, SHARD_AXIS[n])

    def layout_late(late_stacks):
        return _layout_late(_per_weight(late_stacks, w, LATE_WEIGHTS))

    def complete_grads(grads):
        return [g for n in LATE_WEIGHTS for g in grads[n]]

    def sum_pairs(bufs, from_sibling):
        return pair_sums(bufs, from_sibling, [BF16] * len(bufs), "late")

    loss_part, dh0, grads, landed_late = _local_step(x[0], loss_target[0], _layout_early(full, w),
                                                     _weight_layers(w, LATE_WEIGHTS), layout_late, complete_grads, sum_pairs)
    pieces = [_split_shards(grads[n], SHARD_AXIS[n]) for n in SMALL_SHARDED]
    same = jnp.concatenate([grads[n].reshape(-1) for n in REPLICATED] + [jnp.sum(loss_part).reshape(1)])
    pieces.append(jnp.broadcast_to(same, (4,) + same.shape))
    bufs = [g for n in EARLY_WEIGHTS for g in grads[n]] + [_pack(pieces, lead=(4,))]
    from_sibling = swap_halves(bufs, "rs_pair_early")
    landed = scatter_to_chips(pair_sums(bufs, from_sibling, [BF16] * (len(bufs) - 1) + [F32], "early"), "rs_chips_early")
    totals = [chip_sum(t, "rs_chip_sum%d" % i) for i, t in enumerate(landed + landed_late)]
    by_weight = _per_weight(totals[:len(bufs) - 1] + totals[len(bufs):], w, MATMUL_WEIGHTS)
    *shared, small_total = share_halves([by_weight[n] for n in MATMUL_WEIGHTS] + [[totals[len(bufs) - 1]]], "rs_share")
    grad_w = {n: t.reshape(w[n].shape) for n, t in zip(MATMUL_WEIGHTS, shared)}
    rest = SMALL_SHARDED + REPLICATED
    unpacked = _unpack(small_total[0], [w[n].shape for n in rest] + [()])
    grad_w.update(zip(rest, unpacked[:-1]))
    loss = unpacked[-1]
    grad_x = dh0[:, N_META:N_META + seq]
    steps = [adamw(w[n], grad_w[n], m[n], v[n], "adamw_" + n) for n in WEIGHTS]
    return (loss, grad_x, *[grad_w[n] for n in WEIGHTS], *[s[0] for s in steps], *[s[1] for s in steps],
            *[s[2] for s in steps])
```

```python
import functools

import jax
import jax.numpy as jnp
from jax import lax
from jax.experimental import pallas as pl
from jax.experimental.pallas import tpu as pltpu

F32 = jnp.float32
BF16 = jnp.bfloat16

N_META = 16
HEADS = 8
HEAD_DIM = 128
CHUNK = 64
GDN_CONV = 4
FFN_CONV = 3
ALPHA = 4.0 ** 0.25
LN_EPS = 1e-5
RMS_EPS = 1e-6
L2_EPS = 1e-6
Q_SCALE = HEAD_DIM ** -0.5

ADAM_LR = 0.001
ADAM_B1 = 0.9
ADAM_B2 = 0.999
ADAM_EPS = 1e-08
ADAM_WD = 0.01
ADAM_STEP = 10

HALO = 8
VMEM_LIMIT = 48 * 1024 * 1024


def _params(sem=None):
    return pltpu.CompilerParams(dimension_semantics=sem, vmem_limit_bytes=VMEM_LIMIT)


def _dot(a, b, prec=None):
    return jnp.dot(a, b, preferred_element_type=F32, precision=prec)


def _dot_nt(a, b, prec=None):
    return lax.dot_general(a, b, (((1,), (1,)), ((), ())), preferred_element_type=F32, precision=prec)


def _dot_tn(a, b, prec=None):
    return lax.dot_general(a, b, (((0,), (0,)), ((), ())), preferred_element_type=F32, precision=prec)


def _sigmoid(x):
    return 0.5 * jnp.tanh(0.5 * x) + 0.5


def _tri_masks():
    r = lax.broadcasted_iota(jnp.int32, (CHUNK, CHUNK), 0)
    c = lax.broadcasted_iota(jnp.int32, (CHUNK, CHUNK), 1)
    return r >= c, r > c, r == c


def _split_hi_lo(x):
    hi = x.astype(BF16)
    return hi, (x - hi.astype(F32)).astype(BF16)


def _mask_dot(mask, x):
    hi, lo = _split_hi_lo(x)
    return _dot(mask, hi) + _dot(mask, lo)


def _cumsum_rows(g):
    causal, _, _ = _tri_masks()
    return _mask_dot(causal.astype(BF16), g)


def _cumsum_rows_transposed(dy):
    _, strict, _ = _tri_masks()
    return _mask_dot((~strict).astype(BF16), dy)


def _dot_split3(a, b):
    a_hi, a_lo = _split_hi_lo(a)
    b_hi, b_lo = _split_hi_lo(b)
    return _dot(a_hi, b_hi) + (_dot(a_hi, b_lo) + _dot(a_lo, b_hi))


@jax.custom_vjp
def _dot_precise(a, b):
    return _dot_split3(a, b)


def _dot_precise_fwd(a, b):
    return _dot_split3(a, b), (a, b)


def _dot_precise_bwd(operands, ct):
    a, b = operands
    return _dot_split3(ct, b.T), _dot_split3(a.T, ct)


_dot_precise.defvjp(_dot_precise_fwd, _dot_precise_bwd)


def _gdn_m(ks, a64s, bbs):
    causal, strict, _ = _tri_masks()
    decay = [jnp.exp(jnp.where(causal, x - x.T, -1e30)) for x in a64s]
    kk = [_dot_nt(k * b, k) for k, b in zip(ks, bbs)]
    return [jnp.where(strict, x * d, 0.0) for x, d in zip(kk, decay)]


def _gdn_inverse_stages(ks, a64s, bbs):
    ms = _gdn_m(ks, a64s, bbs)
    yield
    r = lax.broadcasted_iota(jnp.int32, (CHUNK, CHUNK), 0)
    c = lax.broadcasted_iota(jnp.int32, (CHUNK, CHUNK), 1)
    eye = (r == c).astype(F32)
    same = [jnp.right_shift(r, s) == jnp.right_shift(c, s) for s in (3, 4, 5)]
    d = [jnp.where(same[0], m, 0.0) for m in ms]
    p = [_dot(x, x) for x in d]
    yield
    t = [eye - x for x in d]
    t = [x + _dot(x, y) for x, y in zip(t, p)]
    p = [_dot(x, x) for x in p]
    yield
    t = [x + _dot(x, y) for x, y in zip(t, p)]
    yield
    for inner, outer in ((same[0], same[1]), (same[1], same[2]), (same[2], None)):
        joins = ~inner if outer is None else (outer & ~inner)
        o = [_dot(x, jnp.where(joins, m, 0.0)) for x, m in zip(t, ms)]
        yield
        t = [x - _dot(y, x) for x, y in zip(t, o)]
        yield
    res = [eye - x - _dot_split3(m, x) for m, x in zip(ms, t)]
    yield
    return [x + _dot(x, y) for x, y in zip(t, res)]


def _gdn_apply_stages(qs, ks, vs, gc, a64s, gl, bbs, ss, ts):
    causal, _, _ = _tri_masks()
    n = range(len(qs))
    qk = [_dot_nt(qs[h], ks[h]) for h in n]
    yield
    decay = [jnp.exp(jnp.where(causal, x - x.T, -1e30)) for x in a64s]
    eg = [jnp.exp(x) for x in gc]
    u = [_dot_precise(ts[h], vs[h] * bbs[h]) for h in n]
    w = [_dot_precise(ts[h], ks[h] * bbs[h] * eg[h]) for h in n]
    qk = [qk[h] * decay[h] for h in n]
    kd = [ks[h] * jnp.exp(gl[h] - gc[h]) for h in n]
    yield
    v_new = [u[h] - _dot(w[h], ss[h]) for h in n]
    q_s = [_dot(qs[h] * eg[h], ss[h]) for h in n]
    yield
    o = [q_s[h] + _dot(qk[h], v_new[h]) for h in n]
    s2 = [ss[h] * jnp.exp(gl[h]) + _dot_tn(kd[h], v_new[h]) for h in n]
    return o, s2


def _run_stages(*generators):
    results = [None] * len(generators)
    live = dict(enumerate(generators))
    while live:
        for i, gen in list(live.items()):
            try:
                next(gen)
            except StopIteration as stop:
                results[i] = stop.value
                del live[i]
    return results


def _head_slices(h):
    return slice(h * HEAD_DIM, (h + 1) * HEAD_DIM), slice(h * HEAD_DIM, h * HEAD_DIM + CHUNK)


def _gdn_head_values(x_ref, gate_ref):
    heads = range(HEADS)
    qs, ks, vs = ([x_ref[s, :, _head_slices(h)[0]] for h in heads] for s in range(3))
    gate = gate_ref[...]
    cumulative = _cumsum_rows(gate)
    total = jnp.sum(gate, axis=0, keepdims=True)
    gcums = [cumulative[:, HEADS + h:HEADS + h + 1] for h in heads]
    gtots = [total[:, HEADS + h:HEADS + h + 1] for h in heads]
    bcols = [gate[:, h:h + 1] for h in heads]
    return qs, ks, vs, gcums, gtots, bcols


def _over_lanes(cols, lanes):
    return [jnp.broadcast_to(c, (c.shape[0], lanes)) for c in cols]


def _gdn_inverse_cols(ks, gcums, bcols):
    return _gdn_inverse_stages(ks, _over_lanes(gcums, CHUNK), _over_lanes(bcols, HEAD_DIM))


def _gdn_apply_cols_stages(qs, ks, vs, gcums, gtots, bcols, ss, ts):
    return _gdn_apply_stages(qs, ks, vs, _over_lanes(gcums, HEAD_DIM), _over_lanes(gcums, CHUNK),
                             _over_lanes(gtots, HEAD_DIM), _over_lanes(bcols, HEAD_DIM), ss, ts)


def _gdn_apply_cols(qs, ks, vs, gcums, gtots, bcols, ss, ts):
    return _run_stages(_gdn_apply_cols_stages(qs, ks, vs, gcums, gtots, bcols, ss, ts))[0]


def _gdn_m_cols(ks, gcums, bcols):
    return _gdn_m(ks, _over_lanes(gcums, CHUNK), _over_lanes(bcols, HEAD_DIM))


def _gate_lanes(bcols, gcols):
    rows = gcols[0].shape[0]
    lane = lax.broadcasted_iota(jnp.int32, (rows, HEAD_DIM), 1)
    out = jnp.zeros((rows, HEAD_DIM), F32)
    for h in range(HEADS):
        if bcols is not None:
            out = jnp.where(lane == h, jnp.broadcast_to(bcols[h], out.shape), out)
        out = jnp.where(lane == HEADS + h, jnp.broadcast_to(gcols[h], out.shape), out)
    return out


def _gate_gradient(dbcols, dgcums, dgtots):
    block = _gate_lanes(dbcols, dgcums)
    lane = lax.broadcasted_iota(jnp.int32, block.shape, 1)
    return jnp.where(lane < HEADS, block, _cumsum_rows_transposed(block) + _gate_lanes(None, dgtots))


def gdn_chunk_fwd(qkv, gates, gather=()):
    _, lp, width = qkv.shape
    n_chunks = lp // CHUNK
    n = len(gather)

    def body(x_ref, gate_ref, next_ref, next_gate_ref, *refs):
        shard_refs, (o_ref, s_ref, t_ref), refs = refs[:n], refs[n:n + 3], refs[n + 3:]
        stack_refs, state, t_next, sems = refs[:n], refs[n], refs[n + 1], refs[n + 2:]
        copies = _gather_copies(shard_refs, stack_refs, *sems) if n else None

        def inverse_stages(ref, g_ref):
            _, ks, _, gcums, _, bcols = _gdn_head_values(ref, g_ref)
            return _gdn_inverse_cols(ks, gcums, bcols)

        @pl.when(pl.program_id(0) == 0)
        def _():
            state[...] = jnp.zeros_like(state)
            for h, t in enumerate(_run_stages(inverse_stages(x_ref, gate_ref))[0]):
                t_next[h] = t
            if n:
                _gather_start(copies)

        qs, ks, vs, gcums, gtots, bcols = _gdn_head_values(x_ref, gate_ref)
        ss = [state[h] for h in range(HEADS)]
        ts = [t_next[h] for h in range(HEADS)]
        ts_next, (os_, s2) = _run_stages(inverse_stages(next_ref, next_gate_ref),
                                         _gdn_apply_cols_stages(qs, ks, vs, gcums, gtots, bcols, ss, ts))
        for h in range(HEADS):
            s_ref[0, h] = ss[h]
            t_ref[0, h] = ts[h]
            t_next[h] = ts_next[h]
            o_ref[:, _head_slices(h)[0]] = os_[h]
            state[h] = s2[h]

        if n:
            @pl.when(pl.program_id(0) == n_chunks - 1)
            def _():
                _gather_finish(copies)

    o, states, tinv, *stacks = pl.pallas_call(
        body,
        name="gdn_chunk_fwd",
        grid=(n_chunks,),
        in_specs=[pl.BlockSpec((3, CHUNK, width), lambda c: (0, c, 0)),
                  pl.BlockSpec((CHUNK, HEAD_DIM), lambda c: (c, 0)),
                  pl.BlockSpec((3, CHUNK, width), lambda c: (0, jnp.minimum(c + 1, n_chunks - 1), 0)),
                  pl.BlockSpec((CHUNK, HEAD_DIM), lambda c: (jnp.minimum(c + 1, n_chunks - 1), 0))] + [ANY] * n,
        out_specs=[
            pl.BlockSpec((CHUNK, width), lambda c: (c, 0)),
            pl.BlockSpec((1, HEADS, HEAD_DIM, HEAD_DIM), lambda c: (c, 0, 0, 0)),
            pl.BlockSpec((1, HEADS, CHUNK, CHUNK), lambda c: (c, 0, 0, 0)),
        ] + [ANY] * n,
        out_shape=[
            jax.ShapeDtypeStruct((lp, width), F32),
            jax.ShapeDtypeStruct((n_chunks, HEADS, HEAD_DIM, HEAD_DIM), F32),
            jax.ShapeDtypeStruct((n_chunks, HEADS, CHUNK, CHUNK), F32),
        ] + _gather_out_shapes(gather),
        scratch_shapes=[pltpu.VMEM((HEADS, HEAD_DIM, HEAD_DIM), F32), pltpu.VMEM((HEADS, CHUNK, CHUNK), F32)]
        + (_gather_sems(n) if n else []),
        compiler_params=_params(("arbitrary",)),
    )(qkv, gates, qkv, gates, *gather)
    return o, states, tinv, _set_own_slots(stacks, gather)


def gdn_chunk_bwd(qkv, gates, states, tinv, d_o, scatter=()):
    _, lp, width = qkv.shape
    n_chunks = lp // CHUNK
    last = n_chunks - 1
    n = len(scatter)

    def body(x_ref, gate_ref, s_ref, t_ref, do_ref, *refs):
        leaving_refs, dx_ref, dgate_ref, refs = refs[:n], refs[n], refs[n + 1], refs[n + 2:]
        landing_refs, dstate, sems = refs[:n], refs[n], refs[n + 1:]
        copies = _scatter_copies(leaving_refs, landing_refs, *sems) if n else None

        @pl.when(pl.program_id(0) == 0)
        def _():
            dstate[...] = jnp.zeros_like(dstate)
            if n:
                _scatter_start(copies)

        heads = range(HEADS)
        qs, ks, vs, gcums, gtots, bcols = _gdn_head_values(x_ref, gate_ref)
        ss = [s_ref[0, h] for h in heads]
        ts = [t_ref[0, h] for h in heads]
        d_out = ([do_ref[:, _head_slices(h)[0]] for h in heads], [dstate[h] for h in heads])
        _, vjp_apply = jax.vjp(_gdn_apply_cols, qs, ks, vs, gcums, gtots, bcols, ss, ts)
        dq, dk, dv, dgc, dgt, db, ds, dt = vjp_apply(d_out)
        tts = [t.T for t in ts]
        dm = [_dot(tts[h], dt[h]) for h in heads]
        dm = [-_dot(dm[h], tts[h]) for h in heads]
        _, vjp_m = jax.vjp(_gdn_m_cols, ks, gcums, bcols)
        dk2, dgc2, db2 = vjp_m(dm)
        for h in heads:
            sl = _head_slices(h)[0]
            dx_ref[0, :, sl] = dq[h]
            dx_ref[1, :, sl] = dk[h] + dk2[h]
            dx_ref[2, :, sl] = dv[h]
            dstate[h] = ds[h]
        dgate_ref[...] = _gate_gradient([db[h] + db2[h] for h in heads], [dgc[h] + dgc2[h] for h in heads], dgt)

        if n:
            @pl.when(pl.program_id(0) == n_chunks - 1)
            def _():
                _scatter_finish(copies)

    dqkv, dgates, *landed = pl.pallas_call(
        body,
        name="gdn_chunk_bwd",
        grid=(n_chunks,),
        in_specs=[
            pl.BlockSpec((3, CHUNK, width), lambda c: (0, last - c, 0)),
            pl.BlockSpec((CHUNK, HEAD_DIM), lambda c: (last - c, 0)),
            pl.BlockSpec((1, HEADS, HEAD_DIM, HEAD_DIM), lambda c: (last - c, 0, 0, 0)),
            pl.BlockSpec((1, HEADS, CHUNK, CHUNK), lambda c: (last - c, 0, 0, 0)),
            pl.BlockSpec((CHUNK, width), lambda c: (last - c, 0)),
        ] + [ANY] * n,
        out_specs=[pl.BlockSpec((3, CHUNK, width), lambda c: (0, last - c, 0)),
                   pl.BlockSpec((CHUNK, HEAD_DIM), lambda c: (last - c, 0))] + [ANY] * n,
        out_shape=[jax.ShapeDtypeStruct(qkv.shape, F32), jax.ShapeDtypeStruct(gates.shape, F32)]
        + [jax.ShapeDtypeStruct(b.shape, b.dtype) for b in scatter],
        scratch_shapes=[pltpu.VMEM((HEADS, HEAD_DIM, HEAD_DIM), F32)] + (_scatter_sems(n) if n else []),
        compiler_params=_params(("arbitrary",)),
    )(qkv, gates, states, tinv, d_o, *scatter)
    return dqkv, dgates, _keep_own_slots(landed, scatter)


def mm_nn(a, b, *, tm, name):
    ks, m, tk = a.shape
    _, ns, _, tn = b.shape

    def body(a_ref, b_ref, o_ref):
        p = _dot(a_ref[...].astype(BF16), b_ref[...])

        @pl.when(pl.program_id(2) == 0)
        def _():
            o_ref[...] = p

        @pl.when(pl.program_id(2) > 0)
        def _():
            o_ref[...] += p

    return pl.pallas_call(
        body,
        name=name,
        grid=(ns, m // tm, ks),
        in_specs=[
            pl.BlockSpec((None, tm, tk), lambda n, i, k: (k, i, 0)),
            pl.BlockSpec((None, None, tk, tn), lambda n, i, k: (k, n, 0, 0)),
        ],
        out_specs=pl.BlockSpec((None, tm, tn), lambda n, i, k: (n, i, 0)),
        out_shape=jax.ShapeDtypeStruct((ns, m, tn), F32),
        compiler_params=_params(("arbitrary", "arbitrary", "arbitrary")),
    )(a, b)


def mm_nt(dy, w, *, tm, name, res=None, res_scale=1.0):
    ns, m, tn = dy.shape
    ks, _, tk, _ = w.shape

    def body(*refs):
        if res is None:
            dy_ref, w_ref, o_ref = refs
        else:
            dy_ref, w_ref, r_ref, o_ref = refs
        p = _dot_nt(dy_ref[...].astype(BF16), w_ref[...])

        @pl.when(pl.program_id(2) == 0)
        def _():
            o_ref[...] = p if res is None else p + res_scale * r_ref[...]

        @pl.when(pl.program_id(2) > 0)
        def _():
            o_ref[...] += p

    in_specs = [
        pl.BlockSpec((None, tm, tn), lambda k, i, n: (n, i, 0)),
        pl.BlockSpec((None, None, tk, tn), lambda k, i, n: (k, n, 0, 0)),
    ]
    args = [dy, w]
    if res is not None:
        in_specs.append(pl.BlockSpec((None, tm, tk), lambda k, i, n: (k, i, 0)))
        args.append(res)
    return pl.pallas_call(
        body,
        name=name,
        grid=(ks, m // tm, ns),
        in_specs=in_specs,
        out_specs=pl.BlockSpec((None, tm, tk), lambda k, i, n: (k, i, 0)),
        out_shape=jax.ShapeDtypeStruct((ks, m, tk), F32),
        compiler_params=_params(("arbitrary", "arbitrary", "arbitrary")),
    )(*args)


def mm_tn(x, dy, *, tm, name, rb=None):
    ks, m, tk = x.shape
    ns, _, tn = dy.shape
    rb = tk if rb is None else rb

    def body(x_ref, dy_ref, o_ref):
        @pl.when(pl.program_id(2) == 0)
        def _():
            o_ref[...] = jnp.zeros_like(o_ref)

        dyb = dy_ref[...].astype(BF16)
        for r in range(0, tk, rb):
            o_ref[r:r + rb, :] += _dot_tn(x_ref[:, r:r + rb].astype(BF16), dyb)

    return pl.pallas_call(
        body,
        name=name,
        grid=(ks, ns, m // tm),
        in_specs=[
            pl.BlockSpec((None, tm, tk), lambda k, n, i: (k, i, 0)),
            pl.BlockSpec((None, tm, tn), lambda k, n, i: (n, i, 0)),
        ],
        out_specs=pl.BlockSpec((None, None, tk, tn), lambda k, n, i: (k, n, 0, 0)),
        out_shape=jax.ShapeDtypeStruct((ks, ns, tk, tn), F32),
        compiler_params=_params(("arbitrary", "arbitrary", "arbitrary")),
    )(x, dy)


def _row_partial(x):
    rows, c = x.shape
    return jnp.sum(x.reshape(rows // 8, 8, c), axis=0)


def _layer_norm(r, g, b):
    mu = jnp.mean(r, axis=-1, keepdims=True)
    xc = r - mu
    var = jnp.mean(xc * xc, axis=-1, keepdims=True)
    return xc * lax.rsqrt(var + LN_EPS) * g + b


def _layer_norm_bwd(x, dh, g):
    mu = jnp.mean(x, axis=-1, keepdims=True)
    xc = x - mu
    rstd = lax.rsqrt(jnp.mean(xc * xc, axis=-1, keepdims=True) + LN_EPS)
    xh = xc * rstd
    dxh = dh * g
    m1 = jnp.mean(dxh, axis=-1, keepdims=True)
    m2 = jnp.mean(dxh * xh, axis=-1, keepdims=True)
    return rstd * (dxh - m1 - xh * m2), _row_partial(dh * xh), _row_partial(dh)


def mm_nn_ln(a, b, h_prev, g, beta, *, tm, name):
    ks, m, tk = a.shape
    d = b.shape[3]

    def body(a_ref, b_ref, hp_ref, g_ref, be_ref, r_ref, h_ref, hb_ref):
        p = _dot(a_ref[...].astype(BF16), b_ref[...])

        @pl.when(pl.program_id(1) == 0)
        def _():
            r_ref[...] = p

        @pl.when(pl.program_id(1) > 0)
        def _():
            r_ref[...] += p

        @pl.when(pl.program_id(1) == ks - 1)
        def _():
            r = ALPHA * hp_ref[...] + r_ref[...]
            r_ref[...] = r
            h = _layer_norm(r, g_ref[...], be_ref[...])
            h_ref[...] = h
            hb_ref[...] = h.astype(BF16)

    row = pl.BlockSpec((None, tm, d), lambda i, k: (0, i, 0))
    vec = pl.BlockSpec((1, d), lambda i, k: (0, 0))
    return pl.pallas_call(
        body,
        name=name,
        grid=(m // tm, ks),
        in_specs=[
            pl.BlockSpec((None, tm, tk), lambda i, k: (k, i, 0)),
            pl.BlockSpec((None, None, tk, d), lambda i, k: (k, 0, 0, 0)),
            row, vec, vec,
        ],
        out_specs=[row, row, row],
        out_shape=[jax.ShapeDtypeStruct((1, m, d), F32)] * 2 + [jax.ShapeDtypeStruct((1, m, d), BF16)],
        compiler_params=_params(("arbitrary", "arbitrary")),
    )(a, b, h_prev, g, beta)


def mm_nt_ln_bwd(dy, w, res, r, g, *, tm, name, swap=()):
    ns, m, tn = dy.shape
    d = w.shape[2]
    n_swap = len(swap)
    last_tile = m // tm - 1

    def body(dy_ref, w_ref, res_ref, r_ref, g_ref, *refs):
        leaving_refs, (dr_ref, dgb_ref), refs = refs[:n_swap], refs[n_swap:n_swap + 2], refs[n_swap + 2:]
        copies = _swap_copies(leaving_refs, refs[:n_swap], *refs[n_swap:]) if n_swap else None
        p = _dot_nt(dy_ref[...].astype(BF16), w_ref[...])

        @pl.when((pl.program_id(0) == 0) & (pl.program_id(1) == 0))
        def _():
            dgb_ref[...] = jnp.zeros_like(dgb_ref)
            if n_swap:
                _swap_start(copies)

        @pl.when(pl.program_id(1) == 0)
        def _():
            dr_ref[...] = p + ALPHA * res_ref[...]

        @pl.when(pl.program_id(1) > 0)
        def _():
            dr_ref[...] += p

        @pl.when(pl.program_id(1) == ns - 1)
        def _():
            for rows in (pl.ds(0, tm // 2), pl.ds(tm // 2, tm // 2)):
                dr, dgamma, dbeta = _layer_norm_bwd(r_ref[rows, :], dr_ref[rows, :], g_ref[...])
                dr_ref[rows, :] = dr
                dgb_ref[0] += dgamma
                dgb_ref[1] += dbeta

        if n_swap:
            @pl.when((pl.program_id(0) == last_tile) & (pl.program_id(1) == ns - 1))
            def _():
                _swap_finish(copies)

    row = pl.BlockSpec((None, tm, d), lambda i, n: (0, i, 0))
    dr, dgb, *landed = pl.pallas_call(
        body,
        name=name,
        grid=(m // tm, ns),
        in_specs=[
            pl.BlockSpec((None, tm, tn), lambda i, n: (n, i, 0)),
            pl.BlockSpec((None, None, d, tn), lambda i, n: (0, n, 0, 0)),
            row, row,
            pl.BlockSpec((1, d), lambda i, n: (0, 0)),
        ] + [ANY] * n_swap,
        out_specs=[row, pl.BlockSpec((2, 8, d), lambda i, n: (0, 0, 0))] + [ANY] * n_swap,
        out_shape=[jax.ShapeDtypeStruct((1, m, d), F32), jax.ShapeDtypeStruct((2, 8, d), F32)] + _swap_out_shapes(swap),
        scratch_shapes=_swap_sems(n_swap) if n_swap else [],
        compiler_params=_params(("arbitrary", "arbitrary")),
    )(dy, w, res, r, g, *swap)
    return dr, dgb, landed


def loss_ln_bwd(h, target, r, g, *, first, count, tm):
    _, lp, d = h.shape

    def body(h_ref, t_ref, r_ref, g_ref, dr_ref, dgb_ref, l_ref):
        row = pl.program_id(0) * tm + lax.broadcasted_iota(jnp.int32, (tm, d), 0)
        valid = (row >= first) & (row < first + count)
        err = jnp.where(valid, h_ref[...] - t_ref[...], 0.0)
        dr, dgamma, dbeta = _layer_norm_bwd(r_ref[...], err * (1.0 / d), g_ref[...])
        dr_ref[...] = dr

        @pl.when(pl.program_id(0) == 0)
        def _():
            dgb_ref[...] = jnp.zeros_like(dgb_ref)
            l_ref[...] = jnp.zeros_like(l_ref)

        dgb_ref[0] += dgamma
        dgb_ref[1] += dbeta
        l_ref[...] += _row_partial(err * err) * (0.5 / d)

    row3 = pl.BlockSpec((None, tm, d), lambda i: (0, i, 0))
    return pl.pallas_call(
        body,
        name="loss_ln4_bwd",
        grid=(lp // tm,),
        in_specs=[row3, pl.BlockSpec((tm, d), lambda i: (i, 0)), row3, pl.BlockSpec((1, d), lambda i: (0, 0))],
        out_specs=[row3, pl.BlockSpec((2, 8, d), lambda i: (0, 0, 0)), pl.BlockSpec((8, d), lambda i: (0, 0))],
        out_shape=[jax.ShapeDtypeStruct((1, lp, d), F32), jax.ShapeDtypeStruct((2, 8, d), F32),
                   jax.ShapeDtypeStruct((8, d), F32)],
        compiler_params=_params(("arbitrary",)),
    )(h, target, r, g)


def _halo_index(tile, tm):
    return jnp.maximum(tile * (tm // HALO) - 1, 0)


def _conv_fwd(xs_ref, w, taps, tm):
    acc = w(0) * xs_ref[pl.ds(HALO - taps + 1, tm), :]
    for j in range(1, taps):
        acc += w(j) * xs_ref[pl.ds(HALO - taps + 1 + j, tm), :]
    return acc


def _conv_bwd_x(dcs_ref, w, taps, tm):
    acc = w(0) * dcs_ref[pl.ds(taps - 1, tm), :]
    for j in range(1, taps):
        acc += w(j) * dcs_ref[pl.ds(taps - 1 - j, tm), :]
    return acc


SUB = 8
LANES = 128
PAIR = 2 * SUB
STRIP_UNROLL = 2


def _pair_rows(r0):
    return pl.ds(r0, SUB), pl.ds(r0 + SUB if isinstance(r0, int) else pl.multiple_of(r0 + SUB, SUB), SUB)


def _shift_down(cur, prev, s):
    if s == 0:
        return cur
    row = lax.broadcasted_iota(jnp.int32, cur.shape, 0)
    return jnp.where(row < s, pltpu.roll(prev, s, axis=0), pltpu.roll(cur, s, axis=0))


def _shift_up(cur, nxt, s):
    if s == 0:
        return cur
    row = lax.broadcasted_iota(jnp.int32, cur.shape, 0)
    return jnp.where(row < SUB - s, pltpu.roll(cur, SUB - s, axis=0), pltpu.roll(nxt, SUB - s, axis=0))


def _silu_parts(c):
    sg = _sigmoid(c)
    return c * sg, sg * (1.0 + c * (1.0 - sg))


def _head_sum(x):
    rows, c = x.shape
    parts = []
    for h in range(c // HEAD_DIM):
        s = jnp.sum(x[:, h * HEAD_DIM:(h + 1) * HEAD_DIM], axis=-1, keepdims=True)
        parts.append(jnp.broadcast_to(s, (rows, HEAD_DIM)))
    return parts[0] if len(parts) == 1 else jnp.concatenate(parts, axis=-1)


def _log1p(y):
    u = 1.0 + y
    d = u - 1.0
    return jnp.where(d == 0.0, y, jnp.log(u) * (y / jnp.where(d == 0.0, 1.0, d)))


def _softplus(x):
    return jnp.maximum(x, 0.0) + _log1p(jnp.exp(-jnp.abs(x)))


def _gate_values(x, al, dt):
    lane = lax.broadcasted_iota(jnp.int32, x.shape, 1)
    is_beta, is_g = lane < HEADS, (lane >= HEADS) & (lane < 2 * HEADS)
    return _sigmoid(x), -jnp.exp(al) * _softplus(x + dt), is_beta, is_g


def gdn_gates_fwd(pba, al, dt, *, tm):
    _, lp, width = pba.shape

    def body(x_ref, al_ref, dt_ref, o_ref):
        beta, g, is_beta, is_g = _gate_values(x_ref[...], al_ref[...], dt_ref[...])
        o_ref[...] = jnp.where(is_beta, beta, jnp.where(is_g, g, 0.0))

    vec = pl.BlockSpec((1, width), lambda i: (0, 0))
    return pl.pallas_call(
        body,
        name="gdn_gates_fwd",
        grid=(lp // tm,),
        in_specs=[pl.BlockSpec((None, tm, width), lambda i: (0, i, 0)), vec, vec],
        out_specs=pl.BlockSpec((tm, width), lambda i: (i, 0)),
        out_shape=jax.ShapeDtypeStruct((lp, width), F32),
        compiler_params=_params(("arbitrary",)),
    )(pba, al, dt)


def gdn_gates_bwd(pba, dgates, al, dt, *, tm):
    _, lp, width = pba.shape

    def body(x_ref, d_ref, al_ref, dt_ref, dx_ref, dsc_ref):
        x = x_ref[...]
        beta, g, is_beta, is_g = _gate_values(x, al_ref[...], dt_ref[...])
        d = d_ref[...]
        dg = jnp.where(is_g, d, 0.0)
        da = dg * -jnp.exp(al_ref[...]) * _sigmoid(x + dt_ref[...])
        dx_ref[...] = jnp.where(is_beta, d * beta * (1.0 - beta), da).astype(dx_ref.dtype)

        @pl.when(pl.program_id(0) == 0)
        def _():
            dsc_ref[...] = jnp.zeros_like(dsc_ref)

        dsc_ref[0] += _row_partial(dg * g)
        dsc_ref[1] += _row_partial(da)

    vec = pl.BlockSpec((1, width), lambda i: (0, 0))
    return pl.pallas_call(
        body,
        name="gdn_gates_bwd",
        grid=(lp // tm,),
        in_specs=[pl.BlockSpec((None, tm, width), lambda i: (0, i, 0)), pl.BlockSpec((tm, width), lambda i: (i, 0)), vec, vec],
        out_specs=[pl.BlockSpec((None, tm, width), lambda i: (0, i, 0)), pl.BlockSpec((2, SUB, width), lambda i: (0, 0, 0))],
        out_shape=[jax.ShapeDtypeStruct((1, lp, width), BF16), jax.ShapeDtypeStruct((2, SUB, width), F32)],
        compiler_params=_params(("arbitrary",)),
    )(pba, dgates, al, dt)


def gdn_pre_fwd(p3, conv_w, *, tm, cb):
    _, lp, width = p3.shape
    taps = conv_w.shape[1]

    def body(x_ref, halo_ref, w_ref, o_ref, xs):
        i = pl.program_id(1)
        for s in range(3):
            xs[s, 0:HALO, :] = jnp.where(i > 0, halo_ref[s], 0.0)
            xs[s, HALO:, :] = x_ref[s]
            c = _conv_fwd(xs.at[s], lambda j, s=s: w_ref[s, j:j + 1, :], taps, tm)
            y, _ = _silu_parts(c)
            if s < 2:
                y = y * lax.rsqrt(_head_sum(y * y) + L2_EPS)
                if s == 0:
                    y = y * Q_SCALE
            o_ref[s] = y

    return pl.pallas_call(
        body,
        name="gdn_pre_fwd",
        grid=(width // cb, lp // tm),
        in_specs=[
            pl.BlockSpec((3, tm, cb), lambda j, i: (0, i, j)),
            pl.BlockSpec((3, HALO, cb), lambda j, i: (0, _halo_index(i, tm), j)),
            pl.BlockSpec((3, taps, cb), lambda j, i: (0, 0, j)),
        ],
        out_specs=pl.BlockSpec((3, tm, cb), lambda j, i: (0, i, j)),
        out_shape=jax.ShapeDtypeStruct((3, lp, width), F32),
        scratch_shapes=[pltpu.VMEM((3, tm + HALO, cb), F32)],
        compiler_params=_params(("arbitrary", "arbitrary")),
    )(p3, p3, conv_w)


def gdn_pre_bwd(p3, dqkv, conv_w, into, *, tm, cb):
    _, lp, width = p3.shape
    taps = conv_w.shape[1]
    last = lp // tm - 1

    def body(x_ref, halo_ref, d_ref, w_ref, into_ref, dx_ref, dw_ref, xs, dcs, carry):
        step = pl.program_id(1)
        tile = last - step

        @pl.when(step == 0)
        def _():
            carry[...] = jnp.zeros_like(carry)
            dw_ref[...] = jnp.zeros_like(dw_ref)

        for s in range(3):
            w = lambda j, s=s: w_ref[s, j:j + 1, :]
            xs[s, 0:HALO, :] = jnp.where(tile > 0, halo_ref[s], 0.0)
            xs[s, HALO:, :] = x_ref[s]
            c = _conv_fwd(xs.at[s], w, taps, tm)
            y, dsilu = _silu_parts(c)
            dy = d_ref[s]
            if s < 2:
                rn = lax.rsqrt(_head_sum(y * y) + L2_EPS)
                yn = y * rn
                if s == 0:
                    dy = dy * Q_SCALE
                dy = rn * (dy - yn * _head_sum(dy * yn))
            dc = dy * dsilu
            dcs[s, 0:tm, :] = dc
            dcs[s, tm:, :] = carry[s]
            dx_ref[s] = _conv_bwd_x(dcs.at[s], w, taps, tm).astype(dx_ref.dtype)
            carry[s] = dc[0:HALO, :]
            for j in range(taps):
                dw_ref[s, j] += _row_partial(dc * xs[s, pl.ds(HALO - taps + 1 + j, tm), :])

    tile_spec = pl.BlockSpec((3, tm, cb), lambda j, i: (0, last - i, j))
    return pl.pallas_call(
        body,
        name="gdn_pre_bwd",
        grid=(width // cb, lp // tm),
        in_specs=[
            tile_spec,
            pl.BlockSpec((3, HALO, cb), lambda j, i: (0, _halo_index(last - i, tm), j)),
            tile_spec,
            pl.BlockSpec((3, taps, cb), lambda j, i: (0, 0, j)),
            pl.BlockSpec(memory_space=pl.ANY),
        ],
        out_specs=[tile_spec, pl.BlockSpec((3, taps, SUB, cb), lambda j, i: (0, 0, 0, j))],
        out_shape=[jax.ShapeDtypeStruct(into.shape, into.dtype), jax.ShapeDtypeStruct((3, taps, SUB, width), F32)],
        input_output_aliases={4: 0},
        scratch_shapes=[
            pltpu.VMEM((3, tm + HALO, cb), F32),
            pltpu.VMEM((3, tm + HALO, cb), F32),
            pltpu.VMEM((3, HALO, cb), F32),
        ],
        compiler_params=_params(("arbitrary", "arbitrary")),
    )(p3, p3, dqkv, conv_w, into)


def gdn_post_fwd(o, z, nw_b, *, tm):
    _, lp, width = o.shape

    def body(o_ref, z_ref, nw_ref, y_ref):
        ov = o_ref[...]
        rn = lax.rsqrt(_head_sum(ov * ov) * (1.0 / HEAD_DIM) + RMS_EPS)
        gate, _ = _silu_parts(z_ref[...])
        y_ref[...] = (ov * rn * nw_ref[...] * gate).astype(y_ref.dtype)

    row = pl.BlockSpec((None, tm, width), lambda i: (0, i, 0))
    z_row = pl.BlockSpec((None, tm, width), lambda i: (z.shape[0] - 1, i, 0))
    return pl.pallas_call(
        body,
        name="gdn_post_fwd",
        grid=(lp // tm,),
        in_specs=[row, z_row, pl.BlockSpec((1, width), lambda i: (0, 0))],
        out_specs=row,
        out_shape=jax.ShapeDtypeStruct((1, lp, width), BF16),
        compiler_params=_params(("arbitrary",)),
    )(o, z, nw_b)


def gdn_post_bwd(o, z, dy, nw_b, *, tm):
    _, lp, width = o.shape

    def body(o_ref, z_ref, dy_ref, nw_ref, do_ref, dz_ref, dnw_ref):
        ov = o_ref[...]
        rn = lax.rsqrt(_head_sum(ov * ov) * (1.0 / HEAD_DIM) + RMS_EPS)
        yn = ov * rn
        gate, dgate = _silu_parts(z_ref[...])
        d_on = dy_ref[...] * gate
        dz_ref[...] = (dy_ref[...] * yn * nw_ref[...] * dgate).astype(dz_ref.dtype)
        a = d_on * nw_ref[...]
        do_ref[...] = rn * (a - yn * (_head_sum(a * yn) * (1.0 / HEAD_DIM)))

        @pl.when(pl.program_id(0) == 0)
        def _():
            dnw_ref[...] = jnp.zeros_like(dnw_ref)

        dnw_ref[...] += _row_partial(d_on * yn)

    row = pl.BlockSpec((None, tm, width), lambda i: (0, i, 0))
    sections = z.shape[0]
    z_row = pl.BlockSpec((None, tm, width), lambda i: (sections - 1, i, 0))
    return pl.pallas_call(
        body,
        name="gdn_post_bwd",
        grid=(lp // tm,),
        in_specs=[row, z_row, row, pl.BlockSpec((1, width), lambda i: (0, 0))],
        out_specs=[row, z_row, pl.BlockSpec((8, width), lambda i: (0, 0))],
        out_shape=[jax.ShapeDtypeStruct((1, lp, width), F32), jax.ShapeDtypeStruct((sections, lp, width), BF16),
                   jax.ShapeDtypeStruct((8, width), F32)],
        compiler_params=_params(("arbitrary",)),
    )(o, z, dy, nw_b)


def ffn_act_fwd(up, conv_w, *, tm, name):
    _, lp, c_w = up.shape
    taps = conv_w.shape[1]

    def body(u_ref, halo_ref, g_ref, w_ref, o_ref):
        first_tile = pl.program_id(1) == 0

        def strip(cur, prev, rows, cs):
            conv = w_ref[taps - 1:taps, cs] * cur
            for j in range(taps - 1):
                conv += w_ref[j:j + 1, cs] * _shift_down(cur, prev, taps - 1 - j)
            y, _ = _silu_parts(conv)
            return y * g_ref[rows, cs]

        def pair(r0, above_of):
            top, bot = _pair_rows(r0)
            for c0 in range(0, c_w, LANES):
                cs = slice(c0, c0 + LANES)
                cur_t, cur_b = u_ref[top, cs], u_ref[bot, cs]
                out = [strip(cur_t, above_of(cs), top, cs), strip(cur_b, cur_t, bot, cs)]
                o_ref[pl.ds(r0, PAIR), cs] = jnp.concatenate(out, axis=0).astype(o_ref.dtype)

        pair(0, lambda cs: jnp.where(first_tile, 0.0, halo_ref[:, cs]))

        def loop_body(s, carry):
            r0 = pl.multiple_of(s * PAIR, PAIR)
            pair(r0, lambda cs: u_ref[pl.ds(pl.multiple_of(r0 - SUB, SUB), SUB), cs])
            return carry

        lax.fori_loop(1, tm // PAIR, loop_body, 0, unroll=STRIP_UNROLL)

    return pl.pallas_call(
        body,
        name=name,
        grid=(2, lp // tm),
        in_specs=[
            pl.BlockSpec((None, tm, c_w), lambda s, i: (s, i, 0)),
            pl.BlockSpec((None, HALO, c_w), lambda s, i: (s, _halo_index(i, tm), 0)),
            pl.BlockSpec((None, tm, c_w), lambda s, i: (2 + s, i, 0)),
            pl.BlockSpec((None, taps, c_w), lambda s, i: (s, 0, 0)),
        ],
        out_specs=pl.BlockSpec((None, tm, c_w), lambda s, i: (s, i, 0)),
        out_shape=jax.ShapeDtypeStruct((2, lp, c_w), BF16),
        compiler_params=_params(("arbitrary", "arbitrary")),
    )(up, up, up, conv_w)


def ffn_act_bwd(up, dact, conv_w, *, tm, name):
    _, lp, c_w = up.shape
    taps = conv_w.shape[1]
    last = lp // tm - 1
    n_pairs = tm // PAIR

    def body(u_ref, halo_ref, g_ref, d_ref, w_ref, dup_ref, dw_ref, below):
        step = pl.program_id(1)
        first_tile = step == last

        @pl.when(step == 0)
        def _():
            below[...] = jnp.zeros_like(below)
            dw_ref[...] = jnp.zeros_like(dw_ref)

        def strip(cur, prev, rows, cs, nxt):
            shifted = [_shift_down(cur, prev, taps - 1 - j) for j in range(taps)]
            conv = w_ref[0:1, cs] * shifted[0]
            for j in range(1, taps):
                conv += w_ref[j:j + 1, cs] * shifted[j]
            y, dsilu = _silu_parts(conv)
            d = d_ref[rows, cs]
            dc = d * g_ref[rows, cs] * dsilu
            dx = w_ref[taps - 1:taps, cs] * dc
            for j in range(taps - 1):
                dx += w_ref[j:j + 1, cs] * _shift_up(dc, nxt, taps - 1 - j)
            return dx, d * y, dc, [dc * s for s in shifted]

        def pair(r0, above_of):
            top, bot = _pair_rows(r0)
            both = pl.ds(r0, PAIR)
            for c0 in range(0, c_w, LANES):
                cs = slice(c0, c0 + LANES)
                cur_t, cur_b = u_ref[top, cs], u_ref[bot, cs]
                dx_b, dg_b, dc_b, dw_b = strip(cur_b, cur_t, bot, cs, below[:, cs])
                dx_t, dg_t, dc_t, dw_t = strip(cur_t, above_of(cs), top, cs, dc_b)
                below[:, cs] = dc_t
                dup_ref[0, both, cs] = jnp.concatenate([dx_t, dx_b], axis=0).astype(dup_ref.dtype)
                dup_ref[1, both, cs] = jnp.concatenate([dg_t, dg_b], axis=0).astype(dup_ref.dtype)
                for j in range(taps):
                    dw_ref[j, :, cs] += dw_t[j] + dw_b[j]

        def loop_body(it, carry):
            r0 = pl.multiple_of((n_pairs - 1 - it) * PAIR, PAIR)
            pair(r0, lambda cs: u_ref[pl.ds(pl.multiple_of(r0 - SUB, SUB), SUB), cs])
            return carry

        lax.fori_loop(0, n_pairs - 1, loop_body, 0, unroll=STRIP_UNROLL)
        pair(0, lambda cs: jnp.where(first_tile, 0.0, halo_ref[:, cs]))

    return pl.pallas_call(
        body,
        name=name,
        grid=(2, lp // tm),
        in_specs=[
            pl.BlockSpec((None, tm, c_w), lambda s, i: (s, last - i, 0)),
            pl.BlockSpec((None, HALO, c_w), lambda s, i: (s, _halo_index(last - i, tm), 0)),
            pl.BlockSpec((None, tm, c_w), lambda s, i: (2 + s, last - i, 0)),
            pl.BlockSpec((None, tm, c_w), lambda s, i: (s, last - i, 0)),
            pl.BlockSpec((None, taps, c_w), lambda s, i: (s, 0, 0)),
        ],
        out_specs=[
            pl.BlockSpec((2, None, tm, c_w), lambda s, i: (0, s, last - i, 0)),
            pl.BlockSpec((None, taps, SUB, c_w), lambda s, i: (s, 0, 0, 0)),
        ],
        out_shape=[jax.ShapeDtypeStruct((2, 2, lp, c_w), BF16), jax.ShapeDtypeStruct((2, taps, SUB, c_w), F32)],
        scratch_shapes=[pltpu.VMEM((SUB, c_w), F32)],
        compiler_params=_params(("arbitrary", "arbitrary")),
    )(up, up, up, dact, conv_w)


def sc_fwd(pb, conv_w, *, tm, cb):
    _, lp, width = pb.shape
    taps = conv_w.shape[0]

    def body(x_ref, halo_ref, w_ref, o_ref):
        first_tile = pl.program_id(1) == 0

        def strip(cur, prev, rows, cs):
            conv = w_ref[taps - 1:taps, cs] * cur
            for j in range(taps - 1):
                conv += w_ref[j:j + 1, cs] * _shift_down(cur, prev, taps - 1 - j)
            return x_ref[0, rows, cs] * conv

        def pair(r0, above_of):
            top, bot = _pair_rows(r0)
            for c0 in range(0, cb, LANES):
                cs = slice(c0, c0 + LANES)
                cur_t = x_ref[1, top, cs] * x_ref[2, top, cs]
                cur_b = x_ref[1, bot, cs] * x_ref[2, bot, cs]
                out = [strip(cur_t, above_of(cs), top, cs), strip(cur_b, cur_t, bot, cs)]
                o_ref[pl.ds(r0, PAIR), cs] = jnp.concatenate(out, axis=0).astype(o_ref.dtype)

        pair(0, lambda cs: jnp.where(first_tile, 0.0, halo_ref[1, :, cs] * halo_ref[2, :, cs]))

        def loop_body(k, carry):
            r0 = pl.multiple_of(k * PAIR, PAIR)
            before = pl.ds(pl.multiple_of(r0 - SUB, SUB), SUB)
            pair(r0, lambda cs: x_ref[1, before, cs] * x_ref[2, before, cs])
            return carry

        lax.fori_loop(1, tm // PAIR, loop_body, 0, unroll=STRIP_UNROLL)

    return pl.pallas_call(
        body,
        name="sc_fwd",
        grid=(width // cb, lp // tm),
        in_specs=[
            pl.BlockSpec((3, tm, cb), lambda j, i: (0, i, j)),
            pl.BlockSpec((3, HALO, cb), lambda j, i: (0, _halo_index(i, tm), j)),
            pl.BlockSpec((taps, cb), lambda j, i: (0, j)),
        ],
        out_specs=pl.BlockSpec((None, tm, cb), lambda j, i: (0, i, j)),
        out_shape=jax.ShapeDtypeStruct((1, lp, width), BF16),
        compiler_params=_params(("arbitrary", "arbitrary")),
    )(pb, pb, conv_w)


def sc_bwd(pb, ds, conv_w, *, tm, cb):
    _, lp, width = pb.shape
    taps = conv_w.shape[0]
    last = lp // tm - 1
    n_pairs = tm // PAIR

    def body(x_ref, halo_ref, d_ref, w_ref, dx_ref, dw_ref, below):
        step = pl.program_id(1)
        first_tile = step == last

        @pl.when(step == 0)
        def _():
            below[...] = jnp.zeros_like(below)
            dw_ref[...] = jnp.zeros_like(dw_ref)

        def strip(cur, prev, rows, cs, nxt):
            gate, left, right = x_ref[0, rows, cs], x_ref[1, rows, cs], x_ref[2, rows, cs]
            shifted = [_shift_down(cur, prev, taps - 1 - j) for j in range(taps)]
            conv = w_ref[0:1, cs] * shifted[0]
            for j in range(1, taps):
                conv += w_ref[j:j + 1, cs] * shifted[j]
            d = d_ref[rows, cs]
            dc = d * gate
            dp = w_ref[taps - 1:taps, cs] * dc
            for j in range(taps - 1):
                dp += w_ref[j:j + 1, cs] * _shift_up(dc, nxt, taps - 1 - j)
            return d * conv, dp * right, dp * left, dc, [dc * s for s in shifted]

        def pair(r0, above_of):
            top, bot = _pair_rows(r0)
            both = pl.ds(r0, PAIR)
            for c0 in range(0, cb, LANES):
                cs = slice(c0, c0 + LANES)
                cur_t = x_ref[1, top, cs] * x_ref[2, top, cs]
                cur_b = x_ref[1, bot, cs] * x_ref[2, bot, cs]
                *dx_b, dc_b, dw_b = strip(cur_b, cur_t, bot, cs, below[:, cs])
                *dx_t, dc_t, dw_t = strip(cur_t, above_of(cs), top, cs, dc_b)
                below[:, cs] = dc_t
                for s in range(3):
                    dx_ref[s, both, cs] = jnp.concatenate([dx_t[s], dx_b[s]], axis=0).astype(dx_ref.dtype)
                for j in range(taps):
                    dw_ref[j, :, cs] += dw_t[j] + dw_b[j]

        def loop_body(it, carry):
            r0 = pl.multiple_of((n_pairs - 1 - it) * PAIR, PAIR)
            before = pl.ds(pl.multiple_of(r0 - SUB, SUB), SUB)
            pair(r0, lambda cs: x_ref[1, before, cs] * x_ref[2, before, cs])
            return carry

        lax.fori_loop(0, n_pairs - 1, loop_body, 0, unroll=STRIP_UNROLL)
        pair(0, lambda cs: jnp.where(first_tile, 0.0, halo_ref[1, :, cs] * halo_ref[2, :, cs]))

    tile_spec = pl.BlockSpec((3, tm, cb), lambda j, i: (0, last - i, j))
    return pl.pallas_call(
        body,
        name="sc_bwd",
        grid=(width // cb, lp // tm),
        in_specs=[
            tile_spec,
            pl.BlockSpec((3, HALO, cb), lambda j, i: (0, _halo_index(last - i, tm), j)),
            pl.BlockSpec((None, tm, cb), lambda j, i: (0, last - i, j)),
            pl.BlockSpec((taps, cb), lambda j, i: (0, j)),
        ],
        out_specs=[tile_spec, pl.BlockSpec((taps, SUB, cb), lambda j, i: (0, 0, j))],
        out_shape=[jax.ShapeDtypeStruct((3, lp, width), BF16), jax.ShapeDtypeStruct((taps, SUB, width), F32)],
        scratch_shapes=[pltpu.VMEM((SUB, cb), F32)],
        compiler_params=_params(("arbitrary", "arbitrary")),
    )(pb, pb, ds, conv_w)


TILE_BYTES = 1536 * 1024


def _rows_tile(rows, cols, multiple=8):
    if rows * cols * 4 <= TILE_BYTES or rows % multiple:
        return rows
    best = multiple
    for t in range(multiple, rows + 1, multiple):
        if rows % t == 0 and t * cols * 4 <= TILE_BYTES:
            best = t
    return best


def pair_sum(g, landed, core, out_dtype, name):
    _, rows, cols = g.shape
    half = rows // 2
    tr = _rows_tile(half, cols, 16)
    nb = half // tr

    def body(c_ref, g_ref, l_ref, o_ref):
        o_ref[...] = (g_ref[...] + l_ref[...]).astype(out_dtype)

    return pl.pallas_call(
        body,
        name=name,
        grid_spec=pltpu.PrefetchScalarGridSpec(
            num_scalar_prefetch=1,
            grid=(4, nb),
            in_specs=[
                pl.BlockSpec((None, tr, cols), lambda s, i, c: (s, c[0] * nb + i, 0)),
                pl.BlockSpec((None, tr, cols), lambda s, i, c: (s, i, 0)),
            ],
            out_specs=pl.BlockSpec((None, tr, cols), lambda s, i, c: (s, i, 0)),
        ),
        out_shape=jax.ShapeDtypeStruct((4, half, cols), out_dtype),
        compiler_params=_params(("arbitrary", "arbitrary")),
    )(core, g, landed)


def chip_sum(x, name):
    _, rows, cols = x.shape
    tr = _rows_tile(rows, cols, 16)

    def body(x0, x1, x2, x3, o_ref):
        acc = x0[...].astype(F32) + x1[...].astype(F32)
        o_ref[...] = (acc + x2[...].astype(F32)) + x3[...].astype(F32)

    return pl.pallas_call(
        body,
        name=name,
        grid=(rows // tr,),
        in_specs=[pl.BlockSpec((None, tr, cols), lambda i, k=k: (k, i, 0)) for k in range(4)],
        out_specs=pl.BlockSpec((tr, cols), lambda i: (i, 0)),
        out_shape=jax.ShapeDtypeStruct((rows, cols), F32),
        compiler_params=_params(("arbitrary",)),
    )(x, x, x, x)


def adamw(w, g, m, v, name):
    shape = w.shape
    cols = shape[-1]
    rows = w.size // cols
    tr = _rows_tile(rows, cols)

    def body(w_ref, g_ref, m_ref, v_ref, d_ref, m2_ref, v2_ref):
        gv = g_ref[...]
        m2 = ADAM_B1 * m_ref[...] + (1.0 - ADAM_B1) * gv
        v2 = ADAM_B2 * v_ref[...] + (1.0 - ADAM_B2) * (gv * gv)
        m_hat = m2 / (1.0 - ADAM_B1 ** ADAM_STEP)
        v_hat = v2 / (1.0 - ADAM_B2 ** ADAM_STEP)
        d_ref[...] = -ADAM_LR * (m_hat / (jnp.sqrt(v_hat) + ADAM_EPS) + ADAM_WD * w_ref[...])
        m2_ref[...] = m2
        v2_ref[...] = v2

    spec = pl.BlockSpec((tr, cols), lambda i: (i, 0))
    outs = pl.pallas_call(
        body,
        name=name,
        grid=(rows // tr,),
        in_specs=[spec] * 4,
        out_specs=[spec] * 3,
        out_shape=[jax.ShapeDtypeStruct((rows, cols), F32)] * 3,
        compiler_params=_params(("arbitrary",)),
    )(*[t.reshape(rows, cols) for t in (w, g, m, v)])
    return tuple(o.reshape(shape) for o in outs)


MESH_ID = pl.DeviceIdType.MESH
ANY = pl.BlockSpec(memory_space=pl.ANY)


def _place():
    x, y, c = lax.axis_index("x"), lax.axis_index("y"), lax.axis_index("c")
    other_chips = [(1 - x, y), (x, 1 - y), (1 - x, 1 - y)]
    return x, y, c, other_chips


def all_gather_shards(bufs, name):
    n = len(bufs)

    def body(*refs):
        x_refs, o_refs = refs[:n], refs[n:2 * n]
        copies = _gather_copies(x_refs, o_refs, *refs[2 * n:])
        _gather_start(copies)
        _gather_finish(copies)

    outs = pl.pallas_call(
        body,
        name=name,
        in_specs=[ANY] * n,
        out_specs=[ANY] * n,
        out_shape=_gather_out_shapes(bufs),
        scratch_shapes=_gather_sems(n),
    )(*bufs)
    return _set_own_slots(outs, bufs)


def _gather_out_shapes(bufs):
    return [jax.ShapeDtypeStruct((4,) + b.shape, b.dtype) for b in bufs]


def _gather_sems(n):
    return [pltpu.SemaphoreType.DMA((6 * n,)), pltpu.SemaphoreType.DMA((6 * n,))]


def _set_own_slots(outs, bufs):
    if not outs:
        return []
    me = 2 * lax.axis_index("x") + lax.axis_index("y")
    return [lax.dynamic_update_index_in_dim(o, b, me, 0) for o, b in zip(outs, bufs)]


def _gather_copies(x_refs, o_refs, send_sems, recv_sems):
    x, y, c, chips = _place()
    me = 2 * x + y
    sibling = (x, y, 1 - c)

    def part(a, slot, hf):
        half = x_refs[a].shape[0] // 2
        return o_refs[a].at[slot, pl.ds(hf * half, half), :]

    def mine(a):
        half = x_refs[a].shape[0] // 2
        return x_refs[a].at[pl.ds(c * half, half), :]

    def copy(k, src, dst, to):
        return pltpu.make_async_remote_copy(src_ref=src, dst_ref=dst, send_sem=send_sems.at[k],
                                            recv_sem=recv_sems.at[k], device_id=to, device_id_type=MESH_ID)

    sends, arrivals, passes, passed = [], [], [], []
    for a in range(len(x_refs)):
        for j, (px, py) in enumerate(chips):
            landed, theirs = part(a, 2 * px + py, c), part(a, 2 * px + py, 1 - c)
            sends.append(copy(6 * a + j, mine(a), part(a, me, c), (px, py, c)))
            arrivals.append(copy(6 * a + j, mine(a), landed, (px, py, c)))
            passes.append(copy(6 * a + 3 + j, landed, landed, sibling))
            passed.append(copy(6 * a + 3 + j, theirs, theirs, sibling))
    return sends, arrivals, passes, passed


def _gather_start(copies):
    for cp in copies[0]:
        cp.start()


def _gather_finish(copies):
    sends, arrivals, passes, passed = copies
    for arrival, cp in zip(arrivals, passes):
        arrival.wait_recv()
        cp.start()
    for cp in passed:
        cp.wait_recv()
    for cp in sends + passes:
        cp.wait_send()


def swap_halves(bufs, name):
    n = len(bufs)

    def body(*refs):
        copies = _swap_copies(refs[:n], refs[n:2 * n], *refs[2 * n:])
        _swap_start(copies)
        _swap_finish(copies)

    return pl.pallas_call(
        body,
        name=name,
        in_specs=[ANY] * n,
        out_specs=[ANY] * n,
        out_shape=_swap_out_shapes(bufs),
        scratch_shapes=_swap_sems(n),
    )(*bufs)


def _swap_out_shapes(bufs):
    return [jax.ShapeDtypeStruct((4, b.shape[1] // 2, b.shape[2]), b.dtype) for b in bufs]


def _swap_sems(n):
    return [pltpu.SemaphoreType.DMA((n,)), pltpu.SemaphoreType.DMA((n,))]


def _swap_copies(x_refs, o_refs, send_sems, recv_sems):
    x, y, c, _ = _place()
    copies = []
    for a, (x_ref, o_ref) in enumerate(zip(x_refs, o_refs)):
        half = x_ref.shape[1] // 2
        copies.append(pltpu.make_async_remote_copy(src_ref=x_ref.at[:, pl.ds((1 - c) * half, half), :], dst_ref=o_ref,
                                                   send_sem=send_sems.at[a], recv_sem=recv_sems.at[a],
                                                   device_id=(x, y, 1 - c), device_id_type=MESH_ID))
    return copies


def _swap_start(copies):
    for cp in copies:
        cp.start()


def _swap_finish(copies):
    for cp in copies:
        cp.wait()


def scatter_to_chips(bufs, name):
    n = len(bufs)

    def body(*refs):
        x_refs, o_refs = refs[:n], refs[n:2 * n]
        copies = _scatter_copies(x_refs, o_refs, *refs[2 * n:])
        _scatter_start(copies)
        _scatter_finish(copies)

    outs = pl.pallas_call(
        body,
        name=name,
        in_specs=[ANY] * n,
        out_specs=[ANY] * n,
        out_shape=[jax.ShapeDtypeStruct(b.shape, b.dtype) for b in bufs],
        scratch_shapes=_scatter_sems(n),
    )(*bufs)
    return _keep_own_slots(outs, bufs)


def _scatter_sems(n):
    return [pltpu.SemaphoreType.DMA((3 * n,)), pltpu.SemaphoreType.DMA((3 * n,))]


def _keep_own_slots(outs, bufs):
    if not outs:
        return []
    me = 2 * lax.axis_index("x") + lax.axis_index("y")
    return [lax.dynamic_update_index_in_dim(o, lax.dynamic_index_in_dim(b, me, 0, keepdims=False), me, 0)
            for o, b in zip(outs, bufs)]


def _scatter_copies(x_refs, o_refs, send_sems, recv_sems):
    x, y, c, chips = _place()
    me = 2 * x + y

    def copy(a, j, src_slot, dst_slot, px, py):
        return pltpu.make_async_remote_copy(src_ref=x_refs[a].at[src_slot], dst_ref=o_refs[a].at[dst_slot],
                                            send_sem=send_sems.at[3 * a + j], recv_sem=recv_sems.at[3 * a + j],
                                            device_id=(px, py, c), device_id_type=MESH_ID)

    sends = [copy(a, j, 2 * px + py, me, px, py) for a in range(len(x_refs)) for j, (px, py) in enumerate(chips)]
    arrivals = [copy(a, j, me, 2 * px + py, px, py) for a in range(len(x_refs)) for j, (px, py) in enumerate(chips)]
    return sends, arrivals


def _scatter_start(copies):
    for cp in copies[0]:
        cp.start()


def _scatter_finish(copies):
    for cp in copies[1]:
        cp.wait_recv()
    for cp in copies[0]:
        cp.wait_send()


def share_halves(groups, name):
    bufs = [b for grp in groups for b in grp]
    where = [(gi, li) for gi, grp in enumerate(groups) for li in range(len(grp))]
    n = len(bufs)

    def body(*refs):
        x_refs, o_refs = refs[:n], refs[n:n + len(groups)]
        send_sems, recv_sems = refs[n + len(groups):]
        x, y, c, _ = _place()
        sent, arrive = [], []
        for a, (gi, li) in enumerate(where):

            def copy(hf, a=a, gi=gi, li=li):
                return pltpu.make_async_remote_copy(src_ref=x_refs[a], dst_ref=o_refs[gi].at[li, hf],
                                                    send_sem=send_sems.at[a], recv_sem=recv_sems.at[a],
                                                    device_id=(x, y, 1 - c), device_id_type=MESH_ID)

            sent.append(copy(c))
            arrive.append(copy(1 - c))
        for cp in sent:
            cp.start()
        for cp in arrive:
            cp.wait_recv()
        for cp in sent:
            cp.wait_send()

    outs = pl.pallas_call(
        body,
        name=name,
        in_specs=[ANY] * n,
        out_specs=[ANY] * len(groups),
        out_shape=[jax.ShapeDtypeStruct((len(grp), 2) + grp[0].shape, grp[0].dtype) for grp in groups],
        scratch_shapes=[pltpu.SemaphoreType.DMA((n,)), pltpu.SemaphoreType.DMA((n,))],
    )(*bufs)
    c = lax.axis_index("c")
    full = [lax.dynamic_update_index_in_dim(o, jnp.stack(grp), c, 1) for o, grp in zip(outs, groups)]
    return [t.reshape(t.shape[0], 2 * t.shape[2], t.shape[3]) for t in full]


def pair_sums(bufs, landed, dtypes, tag):
    core = lax.axis_index("c").astype(jnp.int32).reshape(1)
    return [pair_sum(b, l, core, dt, "rs_pair_sum_%s%d" % (tag, i)) for i, (b, l, dt) in enumerate(zip(bufs, landed, dtypes))]


def _row_tiles(length):
    return (640, 640) if length > 2048 else (128, 64)


def _divisor_tile(rows, target):
    return max(t for t in range(8, min(rows, target) + 1, 8) if rows % t == 0)


def _local_step(x, target, wt, late_shards, layout_late, complete_grads, sum_pairs):
    seq, d = x.shape
    length = N_META + seq
    tm, tm_ffn = _row_tiles(length)
    lp = -(-length // tm) * tm
    tail = jnp.zeros((lp - length, d), F32)
    h0 = jnp.concatenate([wt["meta"], x, tail], axis=0)[None]
    tgt = jnp.concatenate([jnp.zeros((N_META, d), F32), target, tail], axis=0)
    nn = functools.partial(mm_nn, tm=_divisor_tile(lp, 1664))
    nt = functools.partial(mm_nt, tm=_divisor_tile(lp, 1664))
    tn = functools.partial(mm_tn, tm=_divisor_tile(lp, 1664), rb=256)
    nn_ln = functools.partial(mm_nn_ln, tm=_divisor_tile(lp, 832))
    nt_ln_bwd = functools.partial(mm_nt_ln_bwd, tm=_divisor_tile(lp, 1040))
    ln_g = [wt["ln_mix_g"][0:1], wt["ln_ffn_g"][0:1], wt["ln_mix_g"][1:2], wt["ln_ffn_g"][1:2]]
    ln_b = [wt["ln_mix_b"][0:1], wt["ln_ffn_b"][0:1], wt["ln_mix_b"][1:2], wt["ln_ffn_b"][1:2]]

    h0b = h0.astype(BF16)
    p4 = nn(h0b, wt["a4"], name="a_in4")
    pba = nn(h0b, wt["a_ba"], name="a_inba")
    qkv = gdn_pre_fwd(p4, wt["a_conv3"], tm=tm, cb=2 * HEAD_DIM)
    gates = gdn_gates_fwd(pba, wt["alog_lanes"], wt["dtb_lanes"], tm=tm)
    o, states, tinv, late_stacks = gdn_chunk_fwd(qkv, gates, late_shards)
    wt = {**wt, **layout_late(late_stacks)}
    onz = gdn_post_fwd(o[None], p4, wt["anorm_b"], tm=tm)
    r1, h1, h1b = nn_ln(onz, wt["a_out"], h0, ln_g[0], ln_b[0], name="a_out_ln1")
    up0 = nn(h1b, wt["up"][0], name="up0")
    act0 = ffn_act_fwd(up0, wt["fconv"][0], tm=tm_ffn, name="ffn_act0")
    r2, h2, h2b = nn_ln(act0, wt["down"][0], h1, ln_g[1], ln_b[1], name="down0_ln2")
    pb = nn(h2b, wt["b_in"], name="b_in")
    sc = sc_fwd(pb, wt["b_conv"], tm=tm_ffn, cb=d)
    r3, h3, h3b = nn_ln(sc, wt["b_out"], h2, ln_g[2], ln_b[2], name="b_out_ln3")
    up1 = nn(h3b, wt["up"][1], name="up1")
    act1 = ffn_act_fwd(up1, wt["fconv"][1], tm=tm_ffn, name="ffn_act1")
    r4, h4, _ = nn_ln(act1, wt["down"][1], h3, ln_g[3], ln_b[3], name="down1_ln4")

    grads = {}
    dr4, dgb4, loss_part = loss_ln_bwd(h4, tgt, r4, ln_g[3], first=N_META, count=seq, tm=tm)
    d_down1 = tn(act1, dr4, name="d_down1")
    dact1 = nt(dr4, wt["down"][1], name="d_act1")
    dup1, dfconv1 = ffn_act_bwd(up1, dact1, wt["fconv"][1], tm=tm_ffn, name="ffn_act1_bwd")
    dup1 = dup1.reshape(up1.shape)
    d_up1 = tn(h3b, dup1, name="d_up1")

    dr3, dgb3, _ = nt_ln_bwd(dup1, wt["up"][1], dr4, r3, ln_g[2], name="d_h3_ln3")
    d_bout = tn(sc, dr3, name="d_b_out")
    dsc = nt(dr3, wt["b_out"], name="d_sc")
    dpb, dbconv = sc_bwd(pb, dsc, wt["b_conv"], tm=tm_ffn, cb=d)
    d_bin = tn(h2b, dpb, name="d_b_in")

    dr2, dgb2, _ = nt_ln_bwd(dpb, wt["b_in"], dr3, r2, ln_g[1], name="d_h2_ln2")
    d_down0 = tn(act0, dr2, name="d_down0")
    dact0 = nt(dr2, wt["down"][0], name="d_act0")
    dup0, dfconv0 = ffn_act_bwd(up0, dact0, wt["fconv"][0], tm=tm_ffn, name="ffn_act0_bwd")
    dup0 = dup0.reshape(up0.shape)
    d_up0 = tn(h1b, dup0, name="d_up0")
    grads["b_w_in"] = [d_bin[0].transpose(1, 0, 2).reshape(d, 4, 3 * d // 4).transpose(1, 0, 2)]
    grads["b_w_out"] = [d_bout.reshape(4, d // 4, d)]
    grads["ffn_w_up"] = [d_up0[0], d_up1[0]]
    grads["ffn_w_down"] = [t.reshape(4, -1, d) for t in (d_down0, d_down1)]
    complete = complete_grads(grads)

    dr1, dgb1, from_sibling = nt_ln_bwd(dup0, wt["up"][0], dr2, r1, ln_g[0], name="d_h1_ln1", swap=complete)
    leaving = sum_pairs(complete, from_sibling)
    d_aout = tn(onz, dr1, name="d_a_out")
    donz = nt(dr1, wt["a_out"], name="d_onz")
    d_o, dp4, dnw = gdn_post_bwd(o[None], p4, donz, wt["anorm_b"], tm=tm)
    dqkv, dgates, landed = gdn_chunk_bwd(qkv, gates, states, tinv, d_o[0], leaving)
    dp4, daconv = gdn_pre_bwd(p4, dqkv, wt["a_conv3"], dp4, tm=tm, cb=2 * HEAD_DIM)
    dpba, dscal = gdn_gates_bwd(pba, dgates, wt["alog_lanes"], wt["dtb_lanes"], tm=tm)
    d_a4 = tn(h0b, dp4, name="d_a_in4")
    d_aba = tn(h0b, dpba, name="d_a_inba")
    dh0 = nt(dp4, wt["a4"], res=dr1, res_scale=ALPHA, name="d_h0a")
    dh0 = nt(dpba, wt["a_ba"], res=dh0, res_scale=1.0, name="d_h0")

    width = HEADS * HEAD_DIM
    d_a_in = jnp.concatenate([d_a4[0, 0], d_a4[0, 1], d_a4[0, 2], d_a4[0, 3], d_aba[0, 0][:, :2 * HEADS]], axis=1)
    n_in = d_a_in.shape[1] // 4
    grads["a_w_in"] = [d_a_in.reshape(d, 4, n_in).transpose(1, 0, 2)]
    grads["a_w_out"] = [d_aout.reshape(4, width // 4, d)]
    grads["a_conv"] = daconv.sum(axis=2).transpose(1, 0, 2).reshape(1, GDN_CONV, 3 * width)
    per_head = dscal.sum(axis=1)[:, HEADS:2 * HEADS]
    grads["a_log"] = per_head[0][None]
    grads["a_dt_bias"] = per_head[1][None]
    grads["a_norm"] = dnw.reshape(8, HEADS, HEAD_DIM).sum(axis=(0, 1))[None]
    grads["b_conv"] = dbconv.sum(axis=1)[None]
    lns = [dgb1, dgb2, dgb3, dgb4]
    grads["ln_mix_g"] = jnp.stack([lns[0][0].sum(0), lns[2][0].sum(0)])
    grads["ln_mix_b"] = jnp.stack([lns[0][1].sum(0), lns[2][1].sum(0)])
    grads["ln_ffn_g"] = jnp.stack([lns[1][0].sum(0), lns[3][0].sum(0)])
    grads["ln_ffn_b"] = jnp.stack([lns[1][1].sum(0), lns[3][1].sum(0)])
    grads["ffn_conv"] = jnp.stack([t.sum(axis=2).transpose(1, 0, 2).reshape(FFN_CONV, -1) for t in (dfconv0, dfconv1)])
    grads["meta"] = dh0[0, :N_META]
    return loss_part, dh0, grads, landed


WEIGHTS = ["meta", "a_w_in", "a_conv", "a_log", "a_dt_bias", "a_norm", "a_w_out", "b_w_in", "b_conv", "b_w_out",
           "ln_mix_g", "ln_mix_b", "ffn_w_up", "ffn_conv", "ffn_w_down", "ln_ffn_g", "ln_ffn_b"]
EARLY_WEIGHTS = ["a_w_in", "a_w_out"]
LATE_WEIGHTS = ["b_w_in", "b_w_out", "ffn_w_up", "ffn_w_down"]
MATMUL_WEIGHTS = EARLY_WEIGHTS + LATE_WEIGHTS
SMALL_SHARDED = ["a_conv", "b_conv", "ffn_conv", "meta"]
REPLICATED = ["a_log", "a_dt_bias", "a_norm", "ln_mix_g", "ln_mix_b", "ln_ffn_g", "ln_ffn_b"]
SHARD_AXIS = {"meta": 1, "a_w_in": 2, "a_conv": 2, "a_w_out": 1, "b_w_in": 2, "b_conv": 2, "b_w_out": 1,
              "ffn_w_up": 2, "ffn_conv": 2, "ffn_w_down": 1}
PACK_COLS = 1024
PACK_ROWS_MULTIPLE = 32


def _pack(pieces, lead=()):
    flat = jnp.concatenate([p.reshape(lead + (-1,)) for p in pieces], axis=-1)
    n = flat.shape[-1]
    rows = -(-n // (PACK_COLS * PACK_ROWS_MULTIPLE)) * PACK_ROWS_MULTIPLE
    flat = jnp.pad(flat, [(0, 0)] * len(lead) + [(0, rows * PACK_COLS - n)])
    return flat.reshape(lead + (rows, PACK_COLS))


def _unpack(buf, shapes, lead=()):
    flat = buf.reshape(lead + (-1,))
    out, off = [], 0
    for shp in shapes:
        n = 1
        for s in shp:
            n *= s
        out.append(flat[..., off:off + n].reshape(lead + tuple(shp)))
        off += n
    return out


def _join_shards(stacked, axis):
    return jnp.concatenate([stacked[k] for k in range(4)], axis=axis)


def _split_shards(full, axis):
    return jnp.stack(jnp.split(full, 4, axis=axis))


def _weight_layers(w, names):
    return [w[n][l].astype(BF16) for n in names for l in range(w[n].shape[0])]


def _per_weight(arrays, w, names):
    it = iter(arrays)
    return {n: [next(it) for _ in range(w[n].shape[0])] for n in names}


def _layout_early(full, w):
    width = HEADS * HEAD_DIM
    wt = {n: w[n] for n in ("ln_mix_g", "ln_mix_b", "ln_ffn_g", "ln_ffn_b")}
    w_in = _join_shards(full["a_w_in"][0], 1)
    d = w_in.shape[0]
    n_ff = full["ffn_conv"].shape[2] // 2
    blocks = [w_in[:, s * width:(s + 1) * width] for s in range(4)]
    wt["a4"] = jnp.stack(blocks)[None]
    wt["a_ba"] = jnp.pad(w_in[:, 4 * width:], ((0, 0), (0, HEAD_DIM - 2 * HEADS)))[None, None]
    wt["a_out"] = full["a_w_out"][0].reshape(1, 1, width, d)
    wt["a_conv3"] = full["a_conv"][0].reshape(GDN_CONV, 3, width).transpose(1, 0, 2)
    wt["b_conv"] = full["b_conv"][0]
    wt["fconv"] = [full["ffn_conv"][l].reshape(FFN_CONV, 2, n_ff).transpose(1, 0, 2) for l in range(2)]
    wt["meta"] = full["meta"]
    in_g_lanes = (HEADS, HEAD_DIM - 2 * HEADS)
    wt["alog_lanes"] = jnp.pad(w["a_log"][0], in_g_lanes)[None]
    wt["dtb_lanes"] = jnp.pad(w["a_dt_bias"][0], in_g_lanes)[None]
    wt["anorm_b"] = jnp.tile(w["a_norm"][0], HEADS)[None]
    return wt


def _layout_late(full):
    d = full["b_w_in"][0].shape[1]
    n_ff = full["ffn_w_up"][0].shape[2]
    return {
        "b_in": _join_shards(full["b_w_in"][0], 1).reshape(d, 3, d).transpose(1, 0, 2)[None],
        "b_out": full["b_w_out"][0].reshape(1, 1, d, d),
        "up": [t[None] for t in full["ffn_w_up"]],
        "down": [t.reshape(2, 1, n_ff, d) for t in full["ffn_w_down"]],
    }


def kernel(x, meta, a_w_in, a_conv, a_log, a_dt_bias, a_norm, a_w_out, b_w_in, b_conv, b_w_out, ln_mix_g, ln_mix_b, ffn_w_up, ffn_conv, ffn_w_down, ln_ffn_g, ln_ffn_b, loss_target, m_meta, m_a_w_in, m_a_conv, m_a_log, m_a_dt_bias, m_a_norm, m_a_w_out, m_b_w_in, m_b_conv, m_b_w_out, m_ln_mix_g, m_ln_mix_b, m_ffn_w_up, m_ffn_conv, m_ffn_w_down, m_ln_ffn_g, m_ln_ffn_b, v_meta, v_a_w_in, v_a_conv, v_a_log, v_a_dt_bias, v_a_norm, v_a_w_out, v_b_w_in, v_b_conv, v_b_w_out, v_ln_mix_g, v_ln_mix_b, v_ffn_w_up, v_ffn_conv, v_ffn_w_down, v_ln_ffn_g, v_ln_ffn_b):
    w = dict(meta=meta, a_w_in=a_w_in, a_conv=a_conv, a_log=a_log, a_dt_bias=a_dt_bias, a_norm=a_norm, a_w_out=a_w_out,
             b_w_in=b_w_in, b_conv=b_conv, b_w_out=b_w_out, ln_mix_g=ln_mix_g, ln_mix_b=ln_mix_b, ffn_w_up=ffn_w_up,
             ffn_conv=ffn_conv, ffn_w_down=ffn_w_down, ln_ffn_g=ln_ffn_g, ln_ffn_b=ln_ffn_b)
    m = dict(meta=m_meta, a_w_in=m_a_w_in, a_conv=m_a_conv, a_log=m_a_log, a_dt_bias=m_a_dt_bias, a_norm=m_a_norm,
             a_w_out=m_a_w_out, b_w_in=m_b_w_in, b_conv=m_b_conv, b_w_out=m_b_w_out, ln_mix_g=m_ln_mix_g,
             ln_mix_b=m_ln_mix_b, ffn_w_up=m_ffn_w_up, ffn_conv=m_ffn_conv, ffn_w_down=m_ffn_w_down,
             ln_ffn_g=m_ln_ffn_g, ln_ffn_b=m_ln_ffn_b)
    v = dict(meta=v_meta, a_w_in=v_a_w_in, a_conv=v_a_conv, a_log=v_a_log, a_dt_bias=v_a_dt_bias, a_norm=v_a_norm,
             a_w_out=v_a_w_out, b_w_in=v_b_w_in, b_conv=v_b_conv, b_w_out=v_b_w_out, ln_mix_g=v_ln_mix_g,
             ln_mix_b=v_ln_mix_b, ffn_w_up=v_ffn_w_up, ffn_conv=v_ffn_conv, ffn_w_down=v_ffn_w_down,
             ln_ffn_g=v_ln_ffn_g, ln_ffn_b=v_ln_ffn_b)
    seq = x.shape[1]
    *stacks, small = all_gather_shards(_weight_layers(w, EARLY_WEIGHTS) + [_pack([w[n] for n in SMALL_SHARDED])],
                                       "gather_early")
    full = _per_weight(stacks, w, EARLY_WEIGHTS)
    for n, t in zip(SMALL_SHARDED, _unpack(small, [w[n].shape for n in SMALL_SHARDED], lead=(4,))):
        full[n] = _join_shards(t, SHARD_AXIS[n])

    def layout_late(late_stacks):
        return _layout_late(_per_weight(late_stacks, w, LATE_WEIGHTS))

    def complete_grads(grads):
        return [g for n in LATE_WEIGHTS for g in grads[n]]

    def sum_pairs(bufs, from_sibling):
        return pair_sums(bufs, from_sibling, [BF16] * len(bufs), "late")

    loss_part, dh0, grads, landed_late = _local_step(x[0], loss_target[0], _layout_early(full, w),
                                                     _weight_layers(w, LATE_WEIGHTS), layout_late, complete_grads, sum_pairs)
    pieces = [_split_shards(grads[n], SHARD_AXIS[n]) for n in SMALL_SHARDED]
    same = jnp.concatenate([grads[n].reshape(-1) for n in REPLICATED] + [jnp.sum(loss_part).reshape(1)])
    pieces.append(jnp.broadcast_to(same, (4,) + same.shape))
    bufs = [g for n in EARLY_WEIGHTS for g in grads[n]] + [_pack(pieces, lead=(4,))]
    from_sibling = swap_halves(bufs, "rs_pair_early")
    landed = scatter_to_chips(pair_sums(bufs, from_sibling, [BF16] * (len(bufs) - 1) + [F32], "early"), "rs_chips_early")
    totals = [chip_sum(t, "rs_chip_sum%d" % i) for i, t in enumerate(landed + landed_late)]
    by_weight = _per_weight(totals[:len(bufs) - 1] + totals[len(bufs):], w, MATMUL_WEIGHTS)
    *shared, small_total = share_halves([by_weight[n] for n in MATMUL_WEIGHTS] + [[totals[len(bufs) - 1]]], "rs_share")
    grad_w = {n: t.reshape(w[n].shape) for n, t in zip(MATMUL_WEIGHTS, shared)}
    rest = SMALL_SHARDED + REPLICATED
    unpacked = _unpack(small_total[0], [w[n].shape for n in rest] + [()])
    grad_w.update(zip(rest, unpacked[:-1]))
    loss = unpacked[-1]
    grad_x = dh0[:, N_META:N_META + seq]
    steps = [adamw(w[n], grad_w[n], m[n], v[n], "adamw_" + n) for n in WEIGHTS]
    return (loss, grad_x, *[grad_w[n] for n in WEIGHTS], *[s[0] for s in steps], *[s[1] for s in steps],
            *[s[2] for s in steps])
```
